```python
import jax, jax.numpy as jnp
from jax import lax
import numpy as np

D_MODEL = 1024
BATCH = 8
SEQ = 4096
DEPTH = 1

ATTN_WIDTH = D_MODEL // 2
HGRN_WIDTH = D_MODEL - ATTN_WIDTH
MIX_WIDTH = ATTN_WIDTH + HGRN_WIDTH
ATTN_HEAD_DIM = 64
ATTN_HEADS = ATTN_WIDTH // ATTN_HEAD_DIM
HGRN_EXPAND = 128
HGRN_HEADS = HGRN_WIDTH // HGRN_EXPAND
DILATED_PATTERNS = ((128, 1), (512, 4), (2048, 16))
ATTN_BLOCK = 128
ROPE_THETA = 500000.0
ROPE_DIMS = ATTN_HEAD_DIM // 4
HGRN_CHUNK = 64
NORM_EPS = 1e-6
IN_COLS = 4 * ATTN_WIDTH + 4 * HGRN_WIDTH

kernel_name = "hymba_dilated_attn_hgrn2_hybrid"


def _rmsnorm(x, w):
    xf = x.astype(jnp.float32)
    y = xf * lax.rsqrt(jnp.mean(xf * xf, axis=-1, keepdims=True) + NORM_EPS)
    return (y * w.astype(jnp.float32)).astype(x.dtype)


def _head_rmsnorm(o, w, n_heads):
    b, s, width = o.shape
    oh = o.reshape(b, s, n_heads, width // n_heads)
    oh = oh * lax.rsqrt(jnp.mean(oh * oh, axis=-1, keepdims=True) + NORM_EPS)
    return oh.reshape(b, s, width) * w.astype(jnp.float32)


def _partial_rotary(t, positions):
    half = ROPE_DIMS // 2
    inv_freq = ROPE_THETA ** (-jnp.arange(half, dtype=jnp.float32) * (2.0 / ROPE_DIMS))
    ang = positions.astype(jnp.float32)[..., None] * inv_freq
    cos = jnp.cos(ang)[:, :, None, :]
    sin = jnp.sin(ang)[:, :, None, :]
    t = t.astype(jnp.float32)
    t1, t2, rest = t[..., :half], t[..., half:ROPE_DIMS], t[..., ROPE_DIMS:]
    return jnp.concatenate([t1 * cos - t2 * sin, t2 * cos + t1 * sin, rest], axis=-1)


def _dilated_window_attn(q, k, v, window, dilation):
    b, h, s, e = q.shape
    span = window // dilation
    sub_len = s // dilation
    n_blk = -(-sub_len // ATTN_BLOCK)
    pad = n_blk * ATTN_BLOCK - sub_len

    def to_blocks(t):
        t = t.reshape(b, h, sub_len, dilation, e).transpose(0, 1, 3, 2, 4)
        t = jnp.pad(t, ((0, 0), (0, 0), (0, 0), (0, pad), (0, 0)))
        return t.reshape(b, h, dilation, n_blk, ATTN_BLOCK, e)

    def with_prev(t):
        prev = jnp.concatenate([jnp.zeros_like(t[:, :, :, :1]), t[:, :, :, :-1]], axis=3)
        return jnp.concatenate([prev, t], axis=4)

    qb, kb, vb = to_blocks(q), to_blocks(k), to_blocks(v)
    kw, vw = with_prev(kb), with_prev(vb)
    scores = jnp.einsum('bhrnqe,bhrnke->bhrnqk', qb, kw) * (e ** -0.5)
    qi = jnp.arange(ATTN_BLOCK)[:, None]
    kj = jnp.arange(2 * ATTN_BLOCK)[None, :]
    dist = ATTN_BLOCK + qi - kj
    band = (dist >= 0) & (dist <= span)
    first = (jnp.arange(n_blk) == 0)[:, None, None]
    mask = band[None] & ~(first & (kj < ATTN_BLOCK)[None])
    scores = jnp.where(mask, scores, -jnp.inf)
    m = jnp.max(scores, axis=-1)
    p = jnp.exp(scores - m[..., None])
    l = jnp.sum(p, axis=-1)
    o = jnp.einsum('bhrnqk,bhrnke->bhrnqe', p, vw)

    def from_blocks(t):
        tail = t.shape[5:]
        t = t.reshape(b, h, dilation, n_blk * ATTN_BLOCK, *tail)[:, :, :, :sub_len]
        t = jnp.moveaxis(t, 2, 3)
        return t.reshape(b, h, s, *tail)

    return from_blocks(o), from_blocks(m), from_blocks(l)


def _longnet_mixture(q, k, v):
    outs = [_dilated_window_attn(q, k, v, w, d) for (w, d) in DILATED_PATTERNS]
    m_all = jnp.stack([o[1] for o in outs], axis=0)
    m_top = jnp.max(m_all, axis=0)
    wts = jnp.exp(m_all - m_top)
    num = sum(wts[i][..., None] * outs[i][0] for i in range(len(outs)))
    den = sum(wts[i] * outs[i][2] for i in range(len(outs)))
    return num / den[..., None]


def _hgrn2_chunked(q, k, v, log_f):
    b, h, s, e = q.shape
    ev = v.shape[-1]
    nc = s // HGRN_CHUNK
    rs = lambda t: t.reshape(b, h, nc, HGRN_CHUNK, t.shape[-1])
    q, k, v, log_f = rs(q), rs(k), rs(v), rs(log_f)
    cum = jnp.cumsum(log_f, axis=3)
    last = cum[:, :, :, -1:]
    q_dec = q * jnp.exp(cum)
    k_inv = k * jnp.exp(-cum)
    k_end = k * jnp.exp(last - cum)
    causal = jnp.tril(jnp.ones((HGRN_CHUNK, HGRN_CHUNK), dtype=bool))
    att = jnp.where(causal, jnp.einsum('bhnte,bhnse->bhnts', q_dec, k_inv), 0.0)
    o_intra = jnp.einsum('bhnts,bhnsv->bhntv', att, v)
    chunk_decay = jnp.exp(last[:, :, :, 0])

    def step(state, xs):
        qd, ke, vc, dec = xs
        o = jnp.einsum('bhte,bhev->bhtv', qd, state)
        state = dec[..., None] * state + jnp.einsum('bhte,bhtv->bhev', ke, vc)
        return state, o

    xs = (jnp.moveaxis(q_dec, 2, 0), jnp.moveaxis(k_end, 2, 0),
          jnp.moveaxis(v, 2, 0), jnp.moveaxis(chunk_decay, 2, 0))
    state0 = jnp.zeros((b, h, e, ev), jnp.float32)
    _, o_inter = lax.scan(step, state0, xs)
    o = o_intra + jnp.moveaxis(o_inter, 0, 2)
    return o.reshape(b, h, s, ev)


def _fwd_setup_inputs(seed: int = 0) -> dict:
    key = jax.random.key(seed)
    ks = jax.random.split(key, 10)
    x = jax.random.normal(ks[0], (BATCH, SEQ, D_MODEL), jnp.float32)
    offset = jax.random.randint(ks[1], (BATCH, 1), 0, 4096, dtype=jnp.int32)
    positions = (offset + jnp.arange(SEQ, dtype=jnp.int32)[None, :]).astype(jnp.int32)
    w_in = jax.random.normal(ks[2], (DEPTH, D_MODEL, IN_COLS), jnp.float32) * D_MODEL ** -0.5
    w_out = jax.random.normal(ks[3], (DEPTH, MIX_WIDTH, D_MODEL), jnp.float32) * MIX_WIDTH ** -0.5
    mix_norm_w = 1.0 + 0.02 * jax.random.normal(ks[4], (DEPTH, D_MODEL), jnp.float32)
    attn_out_norm_w = 1.0 + 0.02 * jax.random.normal(ks[5], (DEPTH, ATTN_WIDTH), jnp.float32)
    hgrn_out_norm_w = 1.0 + 0.02 * jax.random.normal(ks[6], (DEPTH, HGRN_WIDTH), jnp.float32)
    hgrn_lb_raw = 0.1 * jax.random.normal(ks[7], (DEPTH + 1, HGRN_WIDTH), jnp.float32)
    final_norm_w = 1.0 + 0.02 * jax.random.normal(ks[8], (D_MODEL,), jnp.float32)
    return {"x": x, "positions": positions, "w_in": w_in, "w_out": w_out,
            "mix_norm_w": mix_norm_w, "attn_out_norm_w": attn_out_norm_w,
            "hgrn_out_norm_w": hgrn_out_norm_w, "hgrn_lb_raw": hgrn_lb_raw,
            "final_norm_w": final_norm_w}


def _fwd_reference(x, positions, w_in, w_out, mix_norm_w, attn_out_norm_w,
              hgrn_out_norm_w, hgrn_lb_raw, final_norm_w):
    b, s, _ = x.shape
    f32 = jnp.float32
    lower_bounds = jnp.cumsum(jax.nn.softmax(hgrn_lb_raw.astype(f32), axis=0), axis=0)
    split_at = [ATTN_WIDTH * i for i in range(1, 5)] + \
               [4 * ATTN_WIDTH + HGRN_WIDTH * i for i in range(1, 4)]
    for layer in range(DEPTH):
        hn = _rmsnorm(x, mix_norm_w[layer])
        proj = hn @ w_in[layer]
        aq, ak, av, ag, hq, hf, hi, hg = jnp.split(proj, split_at, axis=-1)

        aq = _partial_rotary(aq.reshape(b, s, ATTN_HEADS, ATTN_HEAD_DIM), positions)
        ak = _partial_rotary(ak.reshape(b, s, ATTN_HEADS, ATTN_HEAD_DIM), positions)
        av = av.reshape(b, s, ATTN_HEADS, ATTN_HEAD_DIM).astype(f32)
        bhse = lambda t: t.transpose(0, 2, 1, 3)
        attn = _longnet_mixture(bhse(aq), bhse(ak), bhse(av))
        attn = attn.transpose(0, 2, 1, 3).reshape(b, s, ATTN_WIDTH)

        lb = lower_bounds[layer]
        f = lb + (1.0 - lb) * jax.nn.sigmoid(hf.astype(f32))
        hkey = 1.0 - f
        hquery = jax.nn.silu(hq.astype(f32))
        hh = lambda t: t.reshape(b, s, HGRN_HEADS, HGRN_EXPAND).transpose(0, 2, 1, 3)
        rec = _hgrn2_chunked(hh(hquery), hh(hkey), hh(hi.astype(f32)), hh(jnp.log(f)))
        rec = rec.transpose(0, 2, 1, 3).reshape(b, s, HGRN_WIDTH)

        y_attn = _head_rmsnorm(attn, attn_out_norm_w[layer], ATTN_HEADS) * jax.nn.silu(ag.astype(f32))
        y_hgrn = _head_rmsnorm(rec, hgrn_out_norm_w[layer], HGRN_HEADS) * jax.nn.silu(hg.astype(f32))
        mixed = jnp.concatenate([y_attn, y_hgrn], axis=-1).astype(x.dtype)
        x = x + mixed @ w_out[layer]
    return _rmsnorm(x, final_norm_w)


import jax as _jax
import jax.numpy as _jnp

TWIN_FORMAT = 'train_step'
FWD_PARAMS = ['x', 'positions', 'w_in', 'w_out', 'mix_norm_w', 'attn_out_norm_w', 'hgrn_out_norm_w', 'hgrn_lb_raw', 'final_norm_w']
TWIN_WEIGHTS = ['w_in', 'w_out', 'mix_norm_w', 'attn_out_norm_w', 'hgrn_out_norm_w', 'hgrn_lb_raw', 'final_norm_w']
TWIN_DIFF_INPUT = 'x'
TWIN_INPUTS = ['x', 'positions', 'w_in', 'w_out', 'mix_norm_w', 'attn_out_norm_w', 'hgrn_out_norm_w', 'hgrn_lb_raw', 'final_norm_w', 'loss_target', 'm_w_in', 'm_w_out', 'm_mix_norm_w', 'm_attn_out_norm_w', 'm_hgrn_out_norm_w', 'm_hgrn_lb_raw', 'm_final_norm_w', 'v_w_in', 'v_w_out', 'v_mix_norm_w', 'v_attn_out_norm_w', 'v_hgrn_out_norm_w', 'v_hgrn_lb_raw', 'v_final_norm_w']
TWIN_OUTPUTS = ['loss', 'grad_x', 'grad_w_in', 'grad_w_out', 'grad_mix_norm_w', 'grad_attn_out_norm_w', 'grad_hgrn_out_norm_w', 'grad_hgrn_lb_raw', 'grad_final_norm_w', 'delta_w_in', 'delta_w_out', 'delta_mix_norm_w', 'delta_attn_out_norm_w', 'delta_hgrn_out_norm_w', 'delta_hgrn_lb_raw', 'delta_final_norm_w', 'new_m_w_in', 'new_m_w_out', 'new_m_mix_norm_w', 'new_m_attn_out_norm_w', 'new_m_hgrn_out_norm_w', 'new_m_hgrn_lb_raw', 'new_m_final_norm_w', 'new_v_w_in', 'new_v_w_out', 'new_v_mix_norm_w', 'new_v_attn_out_norm_w', 'new_v_hgrn_out_norm_w', 'new_v_hgrn_lb_raw', 'new_v_final_norm_w']
TWIN_LEAF_KINDS = {'loss': 'loss', 'grad_x': 'grad_x', 'grad_w_in': 'grad_w', 'grad_w_out': 'grad_w', 'grad_mix_norm_w': 'grad_w', 'grad_attn_out_norm_w': 'grad_w', 'grad_hgrn_out_norm_w': 'grad_w', 'grad_hgrn_lb_raw': 'grad_w', 'grad_final_norm_w': 'grad_w', 'delta_w_in': 'delta_w', 'delta_w_out': 'delta_w', 'delta_mix_norm_w': 'delta_w', 'delta_attn_out_norm_w': 'delta_w', 'delta_hgrn_out_norm_w': 'delta_w', 'delta_hgrn_lb_raw': 'delta_w', 'delta_final_norm_w': 'delta_w', 'new_m_w_in': 'new_m', 'new_m_w_out': 'new_m', 'new_m_mix_norm_w': 'new_m', 'new_m_attn_out_norm_w': 'new_m', 'new_m_hgrn_out_norm_w': 'new_m', 'new_m_hgrn_lb_raw': 'new_m', 'new_m_final_norm_w': 'new_m', 'new_v_w_in': 'new_v', 'new_v_w_out': 'new_v', 'new_v_mix_norm_w': 'new_v', 'new_v_attn_out_norm_w': 'new_v', 'new_v_hgrn_out_norm_w': 'new_v', 'new_v_hgrn_lb_raw': 'new_v', 'new_v_final_norm_w': 'new_v'}


def _forward(args):
    return _fwd_reference(*[args[k] for k in FWD_PARAMS])


def _output_shape():
    out = _jax.eval_shape(lambda: _forward(_fwd_setup_inputs(0)))
    return out.shape, out.dtype

N_MICROBATCH = 1
ADAM_LR = 0.001
ADAM_B1 = 0.9
ADAM_B2 = 0.999
ADAM_EPS = 1e-08
ADAM_WD = 0.01
ADAM_STEP = 10
PER_EXAMPLE_BATCH_AXIS = {'x': 0, 'positions': 0, 'loss_target': 0}
SHARED_INPUTS = []
_WEIGHT_DTYPES = {'w_in': _jnp.float32, 'w_out': _jnp.float32, 'mix_norm_w': _jnp.float32, 'attn_out_norm_w': _jnp.float32, 'hgrn_out_norm_w': _jnp.float32, 'hgrn_lb_raw': _jnp.float32, 'final_norm_w': _jnp.float32}
MOMENT_SCALE = {'w_in': 7.981602e-02, 'w_out': 9.041171e-02, 'mix_norm_w': 1.745442e-01, 'attn_out_norm_w': 9.327626e-02, 'hgrn_out_norm_w': 8.868516e-02, 'hgrn_lb_raw': 8.722494e-03, 'final_norm_w': 3.201403e+01}


def _to_microbatches(a, axis):
    t = _jnp.moveaxis(a, axis, 0)
    t = t.reshape((N_MICROBATCH, t.shape[0] // N_MICROBATCH) + t.shape[1:])
    return _jnp.moveaxis(t, 1, axis + 1)


def setup_inputs(seed: int = 0) -> dict:
    inp = _fwd_setup_inputs(seed)
    key = _jax.random.fold_in(_jax.random.key(seed), 7919)
    shape, _ = _output_shape()
    out = dict(inp)
    out["loss_target"] = _jax.random.normal(_jax.random.fold_in(key, 0), shape, _jnp.float32)
    for i, name in enumerate(TWIN_WEIGHTS):
        w = inp[name].astype(_jnp.float32)
        if MOMENT_SCALE is None:
            s = _jnp.sqrt(_jnp.mean(_jnp.square(w)) + 1e-30)
        else:
            s = MOMENT_SCALE[name]
        km, kv = _jax.random.split(_jax.random.fold_in(key, i + 1))
        out[name] = w
        out["m_" + name] = s * _jax.random.normal(km, w.shape, _jnp.float32)
        out["v_" + name] = (s * s) * _jax.random.uniform(kv, w.shape, _jnp.float32, 0.5, 1.5)
    if N_MICROBATCH > 1:
        for name, axis in PER_EXAMPLE_BATCH_AXIS.items():
            out[name] = _to_microbatches(out[name], axis)
    return {'x': out['x'], 'positions': out['positions'], 'w_in': out['w_in'], 'w_out': out['w_out'], 'mix_norm_w': out['mix_norm_w'], 'attn_out_norm_w': out['attn_out_norm_w'], 'hgrn_out_norm_w': out['hgrn_out_norm_w'], 'hgrn_lb_raw': out['hgrn_lb_raw'], 'final_norm_w': out['final_norm_w'], 'loss_target': out['loss_target'], 'm_w_in': out['m_w_in'], 'm_w_out': out['m_w_out'], 'm_mix_norm_w': out['m_mix_norm_w'], 'm_attn_out_norm_w': out['m_attn_out_norm_w'], 'm_hgrn_out_norm_w': out['m_hgrn_out_norm_w'], 'm_hgrn_lb_raw': out['m_hgrn_lb_raw'], 'm_final_norm_w': out['m_final_norm_w'], 'v_w_in': out['v_w_in'], 'v_w_out': out['v_w_out'], 'v_mix_norm_w': out['v_mix_norm_w'], 'v_attn_out_norm_w': out['v_attn_out_norm_w'], 'v_hgrn_out_norm_w': out['v_hgrn_out_norm_w'], 'v_hgrn_lb_raw': out['v_hgrn_lb_raw'], 'v_final_norm_w': out['v_final_norm_w']}


def _loss(weights, diff, rest, loss_target):
    with _jax.named_scope("forward"):
        args = {**rest, TWIN_DIFF_INPUT: diff, **{k: w.astype(_WEIGHT_DTYPES[k]) for k, w in weights.items()}}
        y = _forward(args)
    with _jax.named_scope("loss_head"):
        err = _jnp.square(y.astype(_jnp.float32) - loss_target)
        return 0.5 * _jnp.sum(_jnp.mean(err, axis=-1)) if err.ndim else 0.5 * err


def _adamw(w, g, m, v):
    m = ADAM_B1 * m + (1.0 - ADAM_B1) * g
    v = ADAM_B2 * v + (1.0 - ADAM_B2) * _jnp.square(g)
    m_hat = m / (1.0 - ADAM_B1 ** ADAM_STEP)
    v_hat = v / (1.0 - ADAM_B2 ** ADAM_STEP)
    delta = -ADAM_LR * (m_hat / (_jnp.sqrt(v_hat) + ADAM_EPS) + ADAM_WD * w)
    return delta, m, v


def reference(x, positions, w_in, w_out, mix_norm_w, attn_out_norm_w, hgrn_out_norm_w, hgrn_lb_raw, final_norm_w, loss_target, m_w_in, m_w_out, m_mix_norm_w, m_attn_out_norm_w, m_hgrn_out_norm_w, m_hgrn_lb_raw, m_final_norm_w, v_w_in, v_w_out, v_mix_norm_w, v_attn_out_norm_w, v_hgrn_out_norm_w, v_hgrn_lb_raw, v_final_norm_w):
    given = dict(x=x, positions=positions, w_in=w_in, w_out=w_out, mix_norm_w=mix_norm_w, attn_out_norm_w=attn_out_norm_w, hgrn_out_norm_w=hgrn_out_norm_w, hgrn_lb_raw=hgrn_lb_raw, final_norm_w=final_norm_w, loss_target=loss_target, m_w_in=m_w_in, m_w_out=m_w_out, m_mix_norm_w=m_mix_norm_w, m_attn_out_norm_w=m_attn_out_norm_w, m_hgrn_out_norm_w=m_hgrn_out_norm_w, m_hgrn_lb_raw=m_hgrn_lb_raw, m_final_norm_w=m_final_norm_w, v_w_in=v_w_in, v_w_out=v_w_out, v_mix_norm_w=v_mix_norm_w, v_attn_out_norm_w=v_attn_out_norm_w, v_hgrn_out_norm_w=v_hgrn_out_norm_w, v_hgrn_lb_raw=v_hgrn_lb_raw, v_final_norm_w=v_final_norm_w)
    weights = {n: given[n] for n in TWIN_WEIGHTS}
    shared = {n: given[n] for n in SHARED_INPUTS}
    per_example = {n: given[n] for n in ['x', 'positions']}
    grad_fn = _jax.value_and_grad(_loss, argnums=(0, 1))

    def one_microbatch(ex, loss_target):
        ex = dict(ex)
        diff = ex.pop(TWIN_DIFF_INPUT)
        return grad_fn(weights, diff, {**shared, **ex}, loss_target)

    if N_MICROBATCH == 1:
        loss, (grad_w, grad_x) = one_microbatch(per_example, given["loss_target"])
    else:
        def body(carry, xs):
            loss_sum, grad_sum = carry
            l_k, (gw_k, gx_k) = one_microbatch(xs[0], xs[1])
            with _jax.named_scope("update"):
                return (loss_sum + l_k, _jax.tree.map(_jnp.add, grad_sum, gw_k)), gx_k

        init = (_jnp.zeros((), _jnp.float32), _jax.tree.map(_jnp.zeros_like, weights))
        (loss, grad_w), grad_x = _jax.lax.scan(body, init, (per_example, given["loss_target"]))
    with _jax.named_scope("update"):
        delta_w, new_m, new_v = {}, {}, {}
        for n in TWIN_WEIGHTS:
            delta_w[n], new_m[n], new_v[n] = _adamw(weights[n], grad_w[n], given["m_" + n], given["v_" + n])
    return (loss, grad_x, *[grad_w[n] for n in TWIN_WEIGHTS], *[delta_w[n] for n in TWIN_WEIGHTS],
            *[new_m[n] for n in TWIN_WEIGHTS], *[new_v[n] for n in TWIN_WEIGHTS])
```

```python
import functools

import jax
import jax.numpy as jnp
from jax import lax
from jax.experimental import pallas as pl
from jax.experimental.pallas import tpu as pltpu

F32 = jnp.float32
BF16 = jnp.bfloat16

SEQ = 4096
D_MODEL = 1024
ATTN_WIDTH = 512
HGRN_WIDTH = 512
HEAD_DIM = 64
HGRN_HEADS = 4
HGRN_DIM = 128
HGRN_CHUNK = 64
N_CHUNKS = SEQ // HGRN_CHUNK
IN_COLS = 4096
COL_BLOCK = 512
N_DEV = 8
WOUT_ROWS = D_MODEL // N_DEV
ATTN_BLOCK = 128
DILATIONS = (1, 4, 16)
ROPE_THETA = 500000.0
ROPE_DIMS = 16
ROPE_HALF = 8
NORM_EPS = 1e-6
NEG_BIG = -1e30
LANES = 128

ADAM_LR = 0.001
ADAM_B1 = 0.9
ADAM_B2 = 0.999
ADAM_EPS = 1e-08
ADAM_WD = 0.01
ADAM_STEP = 10

SMALL_ROWS = 48
ROW_MIX, ROW_ATTN, ROW_HGRN, ROW_LB, ROW_FINAL, ROW_LOSS = 0, 8, 16, 24, 32, 40

VMEM_LIMIT = 56 * 1024 * 1024
MESH = pl.DeviceIdType.MESH


def _mm(a, b):
    return lax.dot_general(a, b, (((1,), (0,)), ((), ())), preferred_element_type=F32)


def _mm_nt(a, b):
    return lax.dot_general(a, b, (((1,), (1,)), ((), ())), preferred_element_type=F32)


def _mm_tn(a, b):
    return lax.dot_general(a, b, (((0,), (0,)), ((), ())), preferred_element_type=F32)


def _mm_exact(a, b):
    return lax.dot_general(a, b, (((1,), (0,)), ((), ())), preferred_element_type=F32,
                           precision=lax.Precision.HIGHEST)


def _sigmoid(v):
    return 1.0 / (1.0 + jnp.exp(-v))


def _params(sem=None, **kw):
    return pltpu.CompilerParams(dimension_semantics=sem, vmem_limit_bytes=VMEM_LIMIT, **kw)


def _my_place():
    return lax.axis_index("x"), lax.axis_index("y"), lax.axis_index("c")


def _peer(place, rel):
    x, y, c = place
    return (x ^ ((rel >> 2) & 1), y ^ ((rel >> 1) & 1), c ^ (rel & 1))


def _flat(place):
    x, y, c = place
    return 4 * x + 2 * y + c


def _gather_weights(w_in, w_out):
    def body(win_ref, wout_ref, gin_ref, gout_ref, send_sems, recv_sems):
        me = _my_place()
        x, y, c = me
        sibling = (x, y, 1 - c)
        chips = [(1 - x, y), (x, 1 - y), (1 - x, 1 - y)]

        def slab(which, place):
            idx = _flat(place)
            if which == 0:
                return gin_ref.at[idx]
            return gout_ref.at[pl.ds(pl.multiple_of(idx * WOUT_ROWS, WOUT_ROWS), WOUT_ROWS), :]

        def copy(which, k, block, to):
            ref = slab(which, block)
            return pltpu.make_async_remote_copy(
                src_ref=ref, dst_ref=ref, send_sem=send_sems.at[7 * which + k],
                recv_sem=recv_sems.at[7 * which + k], device_id=to, device_id_type=MESH)

        gin_ref[_flat(me)] = win_ref[...].astype(BF16)
        gout_ref[pl.ds(pl.multiple_of(_flat(me) * WOUT_ROWS, WOUT_ROWS), WOUT_ROWS), :] = (
            wout_ref[...].astype(BF16))

        started = []
        for which in (0, 1):
            first = [copy(which, 0, me, sibling)]
            first += [copy(which, 1 + j, me, (*chip, c)) for j, chip in enumerate(chips)]
            for cp in first:
                cp.start()
            started += first
        for which in (0, 1):
            for j, chip in enumerate(chips):
                copy(which, 1 + j, (*chip, c), me).wait_recv()
                fwd = copy(which, 4 + j, (*chip, c), sibling)
                fwd.start()
                started.append(fwd)
        for which in (0, 1):
            copy(which, 0, sibling, me).wait_recv()
            for j, chip in enumerate(chips):
                copy(which, 4 + j, (*chip, 1 - c), me).wait_recv()
        for cp in started:
            cp.wait_send()

    return pl.pallas_call(
        body, name="gather_weights",
        out_shape=(jax.ShapeDtypeStruct((N_DEV, D_MODEL, COL_BLOCK), BF16),
                   jax.ShapeDtypeStruct((D_MODEL, D_MODEL), BF16)),
        in_specs=[pl.BlockSpec(memory_space=pltpu.VMEM), pl.BlockSpec(memory_space=pltpu.VMEM)],
        out_specs=(pl.BlockSpec(memory_space=pltpu.VMEM), pl.BlockSpec(memory_space=pltpu.VMEM)),
        scratch_shapes=[pltpu.SemaphoreType.DMA((14,)), pltpu.SemaphoreType.DMA((14,))],
        compiler_params=_params(),
    )(w_in, w_out)


def _rope_tables(pos_col, inv_freq_lanes):
    tm = 512

    def body(pos_ref, invf_ref, c_ref, sa_ref, sb_ref):
        ang = pos_ref[...].astype(F32) * invf_ref[...]
        e = lax.broadcasted_iota(jnp.int32, (tm, LANES), 1) & (HEAD_DIM - 1)
        cos, sin = jnp.cos(ang), jnp.sin(ang)
        c_ref[...] = jnp.where(e < ROPE_DIMS, cos, 1.0)
        sa_ref[...] = jnp.where((e >= ROPE_HALF) & (e < ROPE_DIMS), sin, 0.0)
        sb_ref[...] = jnp.where(e < ROPE_HALF, -sin, 0.0)

    tab = jax.ShapeDtypeStruct((SEQ, LANES), F32)
    spec = pl.BlockSpec((tm, LANES), lambda i: (i, 0))
    return pl.pallas_call(
        body, name="rope_tables", grid=(SEQ // tm,), out_shape=(tab, tab, tab),
        in_specs=[pl.BlockSpec((tm, 1), lambda i: (i, 0)), pl.BlockSpec((1, LANES), lambda i: (0, 0))],
        out_specs=(spec, spec, spec), compiler_params=_params(("parallel",)),
    )(pos_col, inv_freq_lanes)


def _rot(t, c, sa, sb):
    n = t.shape[1]
    return t * c + pltpu.roll(t, ROPE_HALF, 1) * sa + pltpu.roll(t, n - ROPE_HALF, 1) * sb


def _rot_transposed(g, c, sa, sb):
    n = g.shape[1]
    return g * c + pltpu.roll(g * sa, n - ROPE_HALF, 1) + pltpu.roll(g * sb, ROPE_HALF, 1)


def _in_proj_fwd(x, mix_w, w_g, rc, rsa, rsb):
    tm = 256

    def body(x_ref, w_ref, wg_ref, c_ref, sa_ref, sb_ref, proj_ref, hnt_ref):
        xf = x_ref[...]
        ms = jnp.mean(xf * xf, axis=-1, keepdims=True)
        hn = xf * lax.rsqrt(ms + NORM_EPS) * w_ref[...]
        hnt_ref[...] = hn.T.astype(BF16)
        hb = hn.astype(BF16)
        c = jnp.tile(c_ref[...], (1, 4))
        sa = jnp.tile(sa_ref[...], (1, 4))
        sb = jnp.tile(sb_ref[...], (1, 4))
        for j in range(N_DEV):
            acc = _mm(hb, wg_ref[j])
            if j < 2:
                acc = _rot(acc, c, sa, sb)
            proj_ref[:, COL_BLOCK * j:COL_BLOCK * (j + 1)] = acc

    tab = pl.BlockSpec((tm, LANES), lambda i: (i, 0))
    return pl.pallas_call(
        body, name="in_proj_fwd", grid=(SEQ // tm,),
        out_shape=(jax.ShapeDtypeStruct((SEQ, IN_COLS), F32),
                   jax.ShapeDtypeStruct((D_MODEL, SEQ), BF16)),
        in_specs=[pl.BlockSpec((tm, D_MODEL), lambda i: (i, 0)),
                  pl.BlockSpec((1, D_MODEL), lambda i: (0, 0)),
                  pl.BlockSpec((N_DEV, D_MODEL, COL_BLOCK), lambda i: (0, 0, 0)),
                  tab, tab, tab],
        out_specs=(pl.BlockSpec((tm, IN_COLS), lambda i: (i, 0)),
                   pl.BlockSpec((D_MODEL, tm), lambda i: (0, i))),
        compiler_params=_params(("parallel",)),
    )(x, mix_w, w_g, rc, rsa, rsb)


def _band_masks():
    qi = lax.broadcasted_iota(jnp.int32, (ATTN_BLOCK, 2 * ATTN_BLOCK), 0)
    kj = lax.broadcasted_iota(jnp.int32, (ATTN_BLOCK, 2 * ATTN_BLOCK), 1)
    both = (kj >= qi) & (kj <= qi + ATTN_BLOCK)
    qi1 = lax.broadcasted_iota(jnp.int32, (ATTN_BLOCK, ATTN_BLOCK), 0)
    kj1 = lax.broadcasted_iota(jnp.int32, (ATTN_BLOCK, ATTN_BLOCK), 1)
    return kj1 <= qi1, both


def _head0_lanes():
    return lax.broadcasted_iota(jnp.int32, (ATTN_BLOCK, LANES), 1) < HEAD_DIM


def _attn_specs(d, length):
    qkv = [pl.BlockSpec((length, LANES), functools.partial(lambda hp, r, g: (0, r * 32 + 4 * g + hp), g=g))
           for g in range(3)]
    wide = pl.BlockSpec((length, LANES), lambda hp, r: (0, r * 4 + hp))
    return qkv, wide


def _attn_fwd(proj, prev, d, last):
    length = SEQ // d
    nblk = length // ATTN_BLOCK
    first = prev is None

    def body(*refs):
        q_ref, k_ref, v_ref = refs[:3]
        if first:
            o_in = m_in = l_in = None
            outs = refs[3:]
        else:
            o_in, m_in, l_in = refs[3:6]
            outs = refs[6:]
        mask_first, mask_both = _band_masks()
        h0 = _head0_lanes()

        def block(row0, key0, nk, mask):
            rows = pl.ds(row0, ATTN_BLOCK)
            q = q_ref[rows, :]
            kb = k_ref[pl.ds(key0, nk), :].astype(BF16)
            vb = v_ref[pl.ds(key0, nk), :].astype(BF16)
            if not first:
                o_old, m_old, l_old = o_in[rows, :], m_in[rows, :], l_in[rows, :]
            res = []
            for h in range(2):
                hm = h0 if h == 0 else jnp.logical_not(h0)
                qh = jnp.where(hm, q, 0.0).astype(BF16)
                s = jnp.where(mask, _mm_nt(qh, kb) * 0.125, NEG_BIG)
                mb = jnp.max(s, axis=-1, keepdims=True)
                if first:
                    mn = mb
                else:
                    mo = m_old[:, HEAD_DIM * h:HEAD_DIM * h + 1]
                    lo = l_old[:, HEAD_DIM * h:HEAD_DIM * h + 1]
                    mn = jnp.maximum(mo, mb)
                    alpha = jnp.exp(mo - mn)
                p = jnp.exp(s - mn)
                ls = jnp.sum(p, axis=-1, keepdims=True)
                pv = _mm(p.astype(BF16), vb)
                if first:
                    res.append((pv, mn, ls))
                else:
                    res.append((alpha * o_old + pv, mn, alpha * lo + ls))
            o_t = jnp.where(h0, res[0][0], res[1][0])
            m_t = jnp.where(h0, res[0][1], res[1][1])
            l_t = jnp.where(h0, res[0][2], res[1][2])
            if last:
                outs[0][rows, :] = o_t / l_t
                outs[1][rows, :] = m_t + jnp.log(l_t)
            else:
                outs[0][rows, :] = o_t
                outs[1][rows, :] = m_t
                outs[2][rows, :] = l_t

        block(0, 0, ATTN_BLOCK, mask_first)

        def step(n, carry):
            row0 = pl.multiple_of(n * ATTN_BLOCK, ATTN_BLOCK)
            block(row0, row0 - ATTN_BLOCK, 2 * ATTN_BLOCK, mask_both)
            return carry

        lax.fori_loop(1, nblk, step, 0)

    qkv, wide = _attn_specs(d, length)
    proj_v = proj.reshape(length, d * IN_COLS)
    wide_shape = jax.ShapeDtypeStruct((length, d * ATTN_WIDTH), F32)
    n_out = 2 if last else 3
    operands = [proj_v, proj_v, proj_v]
    in_specs = list(qkv)
    if not first:
        operands += [a.reshape(length, d * ATTN_WIDTH) for a in prev]
        in_specs += [wide, wide, wide]
    outs = pl.pallas_call(
        body, name=f"attn_fwd_d{d}", grid=(4, d),
        out_shape=tuple([wide_shape] * n_out), in_specs=in_specs, out_specs=tuple([wide] * n_out),
        compiler_params=_params(("parallel", "parallel")),
    )(*operands)
    return tuple(a.reshape(SEQ, ATTN_WIDTH) for a in outs)


def _attn_bwd(proj, d_out, lse, delta, prev, d):
    length = SEQ // d
    nblk = length // ATTN_BLOCK
    first = prev is None

    def body(*refs):
        q_ref, k_ref, v_ref, do_ref, lse_ref, del_ref = refs[:6]
        if first:
            dq_out, dk_out, dv_out = refs[6:]
            dk_out[...] = jnp.zeros_like(dk_out)
            dv_out[...] = jnp.zeros_like(dv_out)
        else:
            dq_in, dk_in, dv_in, dq_out, dk_out, dv_out = refs[6:]
            dk_out[...] = dk_in[...]
            dv_out[...] = dv_in[...]
        mask_first, mask_both = _band_masks()
        h0 = _head0_lanes()

        def block(row0, key0, nk, mask):
            rows = pl.ds(row0, ATTN_BLOCK)
            keys = pl.ds(key0, nk)
            q, g = q_ref[rows, :], do_ref[rows, :]
            lse_t, del_t = lse_ref[rows, :], del_ref[rows, :]
            kb = k_ref[keys, :].astype(BF16)
            vb = v_ref[keys, :].astype(BF16)
            dq_h = []
            dk_c = jnp.zeros((nk, LANES), F32)
            dv_c = jnp.zeros((nk, LANES), F32)
            for h in range(2):
                hm = h0 if h == 0 else jnp.logical_not(h0)
                qh = jnp.where(hm, q, 0.0).astype(BF16)
                gh = jnp.where(hm, g, 0.0).astype(BF16)
                s = _mm_nt(qh, kb) * 0.125
                p = jnp.where(mask, jnp.exp(s - lse_t[:, HEAD_DIM * h:HEAD_DIM * h + 1]), 0.0)
                dp = _mm_nt(gh, vb)
                ds = (p * (dp - del_t[:, HEAD_DIM * h:HEAD_DIM * h + 1]) * 0.125).astype(BF16)
                dq_h.append(_mm(ds, kb))
                dk_c = dk_c + _mm_tn(ds, qh)
                dv_c = dv_c + _mm_tn(p.astype(BF16), gh)
            dq_t = jnp.where(h0, dq_h[0], dq_h[1])
            if not first:
                dq_t = dq_t + dq_in[rows, :]
            dq_out[rows, :] = dq_t
            dk_out[keys, :] += dk_c
            dv_out[keys, :] += dv_c

        block(0, 0, ATTN_BLOCK, mask_first)

        def step(n, carry):
            row0 = pl.multiple_of(n * ATTN_BLOCK, ATTN_BLOCK)
            block(row0, row0 - ATTN_BLOCK, 2 * ATTN_BLOCK, mask_both)
            return carry

        lax.fori_loop(1, nblk, step, 0)

    qkv, wide = _attn_specs(d, length)
    proj_v = proj.reshape(length, d * IN_COLS)
    view = lambda a: a.reshape(length, d * ATTN_WIDTH)
    wide_shape = jax.ShapeDtypeStruct((length, d * ATTN_WIDTH), F32)
    operands = [proj_v, proj_v, proj_v, view(d_out), view(lse), view(delta)]
    in_specs = list(qkv) + [wide, wide, wide]
    if not first:
        operands += [view(a) for a in prev]
        in_specs += [wide, wide, wide]
    outs = pl.pallas_call(
        body, name=f"attn_bwd_d{d}", grid=(4, d),
        out_shape=(wide_shape, wide_shape, wide_shape), in_specs=in_specs, out_specs=(wide, wide, wide),
        compiler_params=_params(("parallel", "parallel")),
    )(*operands)
    return tuple(a.reshape(SEQ, ATTN_WIDTH) for a in outs)


def _hgrn_lower_bound(lb_ref):
    r0, r1 = lb_ref[0:1, :], lb_ref[1:2, :]
    mx = jnp.maximum(r0, r1)
    e0, e1 = jnp.exp(r0 - mx), jnp.exp(r1 - mx)
    return e0 / (e0 + e1)


def _hgrn_gates(hq, hf, lb):
    sq = _sigmoid(hq)
    sg = _sigmoid(hf)
    f = lb + (1.0 - lb) * sg
    return hq * sq, sq, sg, f, 1.0 - f, jnp.log(f)


def _hgrn_specs():
    cols = lambda g: pl.BlockSpec((SEQ, HGRN_DIM), functools.partial(lambda h, g: (0, 4 * g + h), g=g))
    head = pl.BlockSpec((SEQ, HGRN_DIM), lambda h: (0, h))
    lb = pl.BlockSpec((2, HGRN_DIM), lambda h: (0, h))
    states = pl.BlockSpec((1, N_CHUNKS, HGRN_DIM, HGRN_DIM), lambda h: (h, 0, 0, 0))
    return cols, head, lb, states


def _hgrn_fwd(proj, lb_raw):
    t = HGRN_CHUNK

    def body(hq_ref, hf_ref, hi_ref, lb_ref, rec_ref, st_ref, state):
        lb = _hgrn_lower_bound(lb_ref)
        ri = lax.broadcasted_iota(jnp.int32, (t, t), 0)
        ci = lax.broadcasted_iota(jnp.int32, (t, t), 1)
        causal = ri >= ci
        tril = causal.astype(F32)
        state[...] = jnp.zeros_like(state)

        def chunk(n, carry):
            rows = pl.ds(pl.multiple_of(n * t, t), t)
            q, _, _, _, k, lf = _hgrn_gates(hq_ref[rows, :], hf_ref[rows, :], lb)
            cum = _mm_exact(tril, lf)
            last = cum[t - 1:t, :]
            qd = (q * jnp.exp(cum)).astype(BF16)
            ki = (k * jnp.exp(-cum)).astype(BF16)
            ke = (k * jnp.exp(last - cum)).astype(BF16)
            vb = hi_ref[rows, :].astype(BF16)
            att = jnp.where(causal, _mm_nt(qd, ki), 0.0)
            st = state[...]
            st_ref[0, n] = st
            rec_ref[rows, :] = _mm(att.astype(BF16), vb) + _mm_nt(qd, st.astype(BF16))
            state[...] = st * jnp.exp(last) + _mm_tn(vb, ke)
            return carry

        lax.fori_loop(0, N_CHUNKS, chunk, 0)

    cols, head, lb, states = _hgrn_specs()
    return pl.pallas_call(
        body, name="hgrn_fwd", grid=(HGRN_HEADS,),
        out_shape=(jax.ShapeDtypeStruct((SEQ, HGRN_WIDTH), F32),
                   jax.ShapeDtypeStruct((HGRN_HEADS, N_CHUNKS, HGRN_DIM, HGRN_DIM), F32)),
        in_specs=[cols(4), cols(5), cols(6), lb], out_specs=(head, states),
        scratch_shapes=[pltpu.VMEM((HGRN_DIM, HGRN_DIM), F32)],
        compiler_params=_params(("parallel",)),
    )(proj, proj, proj, lb_raw)


def _hgrn_bwd(proj, lb_raw, d_rec, states):
    t = HGRN_CHUNK

    def body(hq_ref, hf_ref, hi_ref, lb_ref, do_ref, st_ref, dhq_ref, dhf_ref, dhi_ref, dlb_ref,
             dstate, dlb_acc):
        lb = _hgrn_lower_bound(lb_ref)
        ri = lax.broadcasted_iota(jnp.int32, (t, t), 0)
        ci = lax.broadcasted_iota(jnp.int32, (t, t), 1)
        causal = ri >= ci
        tril = causal.astype(F32)
        triu = (ri <= ci).astype(F32)
        last_row = lax.broadcasted_iota(jnp.int32, (t, HGRN_DIM), 0) == t - 1
        dstate[...] = jnp.zeros_like(dstate)
        dlb_acc[...] = jnp.zeros_like(dlb_acc)

        def chunk(i, carry):
            n = N_CHUNKS - 1 - i
            rows = pl.ds(pl.multiple_of(n * t, t), t)
            hq = hq_ref[rows, :]
            q, sq, sg, f, k, lf = _hgrn_gates(hq, hf_ref[rows, :], lb)
            cum = _mm_exact(tril, lf)
            last = cum[t - 1:t, :]
            e_cum, e_inv, e_end, dec = jnp.exp(cum), jnp.exp(-cum), jnp.exp(last - cum), jnp.exp(last)
            qd, ki, ke = q * e_cum, k * e_inv, k * e_end
            qdb, kib, keb = qd.astype(BF16), ki.astype(BF16), ke.astype(BF16)
            vb = hi_ref[rows, :].astype(BF16)
            gb = do_ref[rows, :].astype(BF16)
            st_prev = st_ref[0, n]
            dst = dstate[...]
            dstb = dst.astype(BF16)

            att = jnp.where(causal, _mm_nt(qdb, kib), 0.0).astype(BF16)
            datt = jnp.where(causal, _mm_nt(gb, vb), 0.0).astype(BF16)
            dv = _mm_tn(att, gb) + _mm_nt(keb, dstb)
            dqd = _mm(datt, kib) + _mm(gb, st_prev.astype(BF16))
            dki = _mm_tn(datt, qdb)
            dke = _mm(vb, dstb)
            ddec = jnp.sum(dst * st_prev, axis=0, keepdims=True)
            dstate[...] = dst * dec + _mm_tn(gb, qdb)

            dq = dqd * e_cum
            dk = dki * e_inv + dke * e_end
            dlast = jnp.sum(dke * ke, axis=0, keepdims=True) + ddec * dec
            dcum = dqd * qd - dki * ki - dke * ke + jnp.where(last_row, dlast, 0.0)
            dlf = _mm_exact(triu, dcum)
            df = dlf / f - dk
            dhq_ref[rows, :] = dq * (sq * (1.0 + hq * (1.0 - sq)))
            dhf_ref[rows, :] = df * (1.0 - lb) * (sg * (1.0 - sg))
            dhi_ref[rows, :] = dv
            dlb_acc[...] += jnp.sum(df * (1.0 - sg), axis=0, keepdims=True)
            return carry

        lax.fori_loop(0, N_CHUNKS, chunk, 0)
        g0 = dlb_acc[...] * lb * (1.0 - lb)
        dlb_ref[...] = jnp.concatenate([g0, -g0], axis=0)

    cols, head, lb_spec, st_spec = _hgrn_specs()
    wide = jax.ShapeDtypeStruct((SEQ, HGRN_WIDTH), F32)
    return pl.pallas_call(
        body, name="hgrn_bwd", grid=(HGRN_HEADS,),
        out_shape=(wide, wide, wide, jax.ShapeDtypeStruct((2, HGRN_WIDTH), F32)),
        in_specs=[cols(4), cols(5), cols(6), lb_spec, head, st_spec],
        out_specs=(head, head, head, lb_spec),
        scratch_shapes=[pltpu.VMEM((HGRN_DIM, HGRN_DIM), F32), pltpu.VMEM((1, HGRN_DIM), F32)],
        compiler_params=_params(("parallel",)),
    )(proj, proj, proj, lb_raw, d_rec, states)


def _group_sum(v, group):
    parts = []
    for s in range(v.shape[1] // LANES):
        slab = v[:, LANES * s:LANES * (s + 1)]
        if group == LANES:
            parts.append(jnp.broadcast_to(jnp.sum(slab, axis=-1, keepdims=True), slab.shape))
        else:
            h0 = lax.broadcasted_iota(jnp.int32, slab.shape, 1) < HEAD_DIM
            s0 = jnp.sum(jnp.where(h0, slab, 0.0), axis=-1, keepdims=True)
            s1 = jnp.sum(jnp.where(h0, 0.0, slab), axis=-1, keepdims=True)
            parts.append(jnp.where(h0, s0, s1))
    return jnp.concatenate(parts, axis=1)


def _mid(attn_o, rec, proj, x, target, w_out_g, attn_w, hgrn_w, final_w):
    tm = 256

    def branch_fwd(o, gate, w, group):
        r = lax.rsqrt(_group_sum(o * o, group) * (1.0 / group) + NORM_EPS)
        nrm = o * r
        sg = _sigmoid(gate)
        return r, nrm, sg, nrm * w * (gate * sg)

    def branch_bwd(dy, r, nrm, sg, gate, w, group):
        silu = gate * sg
        d_gate = dy * nrm * w * (sg * (1.0 + gate * (1.0 - sg)))
        d_w = jnp.sum(dy * nrm * silu, axis=0, keepdims=True)
        dn = dy * w * silu
        d_o = r * (dn - nrm * (_group_sum(dn * nrm, group) * (1.0 / group)))
        return d_o, d_gate, d_w

    def body(o_ref, rec_ref, ag_ref, hg_ref, x_ref, tgt_ref, wout_ref, aw_ref, hw_ref, fw_ref,
             dx2_ref, do_ref, delta_ref, dag_ref, drec_ref, dhg_ref, dwout_ref, dfw_ref, daw_ref, dhw_ref,
             loss_ref):
        i = pl.program_id(0)

        @pl.when(i == 0)
        def _():
            dwout_ref[...] = jnp.zeros_like(dwout_ref)
            dfw_ref[...] = jnp.zeros_like(dfw_ref)
            daw_ref[...] = jnp.zeros_like(daw_ref)
            dhw_ref[...] = jnp.zeros_like(dhw_ref)
            loss_ref[...] = jnp.zeros_like(loss_ref)

        o, rc, ag, hg = o_ref[...], rec_ref[...], ag_ref[...], hg_ref[...]
        aw, hw, fw = aw_ref[...], hw_ref[...], fw_ref[...]
        ra, na, sga, ya = branch_fwd(o, ag, aw, HEAD_DIM)
        rh, nh, sgh, yh = branch_fwd(rc, hg, hw, HGRN_DIM)
        mixed = jnp.concatenate([ya, yh], axis=1).astype(BF16)
        wout = wout_ref[...]
        x2 = x_ref[...] + _mm(mixed, wout)
        rstd = lax.rsqrt(jnp.mean(x2 * x2, axis=-1, keepdims=True) + NORM_EPS)
        xn = x2 * rstd
        err = xn * fw - tgt_ref[...]
        row_loss = jnp.mean(err * err, axis=-1, keepdims=True)
        loss_ref[...] += 0.5 * jnp.sum(row_loss, axis=0, keepdims=True)
        dy = err * (1.0 / D_MODEL)
        dfw_ref[...] += jnp.sum(dy * xn, axis=0, keepdims=True)
        dxn = dy * fw
        dx2 = rstd * (dxn - xn * jnp.mean(dxn * xn, axis=-1, keepdims=True))
        dx2_ref[...] = dx2
        dx2b = dx2.astype(BF16)
        dwout_ref[...] += _mm_tn(mixed, dx2b)
        dmixed = _mm_nt(dx2b, wout)

        d_o, d_ag, d_aw = branch_bwd(dmixed[:, :ATTN_WIDTH], ra, na, sga, ag, aw, HEAD_DIM)
        d_rec, d_hg, d_hw = branch_bwd(dmixed[:, ATTN_WIDTH:], rh, nh, sgh, hg, hw, HGRN_DIM)
        do_ref[...] = d_o
        delta_ref[...] = _group_sum(d_o * o, HEAD_DIM)
        dag_ref[...] = d_ag
        drec_ref[...] = d_rec
        dhg_ref[...] = d_hg
        daw_ref[...] += d_aw
        dhw_ref[...] += d_hw

    half = lambda: pl.BlockSpec((tm, COL_BLOCK), lambda i: (i, 0))
    full = lambda: pl.BlockSpec((tm, D_MODEL), lambda i: (i, 0))
    fixed = lambda r, c: pl.BlockSpec((r, c), lambda i: (0, 0))
    wide = jax.ShapeDtypeStruct((SEQ, COL_BLOCK), F32)
    return pl.pallas_call(
        body, name="mid", grid=(SEQ // tm,),
        out_shape=(jax.ShapeDtypeStruct((SEQ, D_MODEL), F32), wide, wide, wide, wide, wide,
                   jax.ShapeDtypeStruct((D_MODEL, D_MODEL), F32),
                   jax.ShapeDtypeStruct((1, D_MODEL), F32), jax.ShapeDtypeStruct((1, COL_BLOCK), F32),
                   jax.ShapeDtypeStruct((1, COL_BLOCK), F32), jax.ShapeDtypeStruct((1, 1), F32)),
        in_specs=[half(), half(),
                  pl.BlockSpec((tm, COL_BLOCK), lambda i: (i, 3)), pl.BlockSpec((tm, COL_BLOCK), lambda i: (i, 7)),
                  full(), full(), fixed(D_MODEL, D_MODEL), fixed(1, COL_BLOCK), fixed(1, COL_BLOCK),
                  fixed(1, D_MODEL)],
        out_specs=(full(), half(), half(), half(), half(), half(), fixed(D_MODEL, D_MODEL),
                   fixed(1, D_MODEL), fixed(1, COL_BLOCK), fixed(1, COL_BLOCK), fixed(1, 1)),
        compiler_params=_params(("arbitrary",)),
    )(attn_o, rec, proj, proj, x, target, w_out_g, attn_w, hgrn_w, final_w)


def _in_proj_bwd(d_groups, hn_t, w_g, x, dx2, mix_w, rc, rsa, rsb):
    tm = 256
    n_tiles = SEQ // tm
    last_j = N_DEV - 1

    def body(*refs):
        dg_refs = refs[:N_DEV]
        hnt_ref, wg_ref, x_ref, dx2_ref, w_ref, c_ref, sa_ref, sb_ref = refs[N_DEV:N_DEV + 8]
        gx_ref, dwin_ref, dmw_ref, dhn = refs[N_DEV + 8:]
        j, i = pl.program_id(0), pl.program_id(1)
        rows = pl.ds(pl.multiple_of(i * tm, tm), tm)

        for jj in range(N_DEV):
            @pl.when(j == jj)
            def _(jj=jj):
                dp = dg_refs[jj][...]
                if jj < 2:
                    dp = _rot_transposed(dp, jnp.tile(c_ref[...], (1, 4)), jnp.tile(sa_ref[...], (1, 4)),
                                         jnp.tile(sb_ref[...], (1, 4)))
                dpb = dp.astype(BF16)
                contrib = _mm_nt(dpb, wg_ref[0])
                if jj == 0:
                    dhn[rows, :] = contrib
                else:
                    dhn[rows, :] += contrib
                dw = _mm(hnt_ref[...], dpb)

                @pl.when(i == 0)
                def _():
                    dwin_ref[0] = dw

                @pl.when(i > 0)
                def _():
                    dwin_ref[0] += dw

        @pl.when((j == last_j) & (i == 0))
        def _():
            dmw_ref[...] = jnp.zeros_like(dmw_ref)

        @pl.when(j == last_j)
        def _():
            xf = x_ref[...]
            w = w_ref[...]
            rstd = lax.rsqrt(jnp.mean(xf * xf, axis=-1, keepdims=True) + NORM_EPS)
            xn = xf * rstd
            g = dhn[rows, :]
            dmw_ref[...] += jnp.sum(g * xn, axis=0, keepdims=True)
            gw = g * w
            gx_ref[...] = dx2_ref[...] + rstd * (gw - xn * jnp.mean(gw * xn, axis=-1, keepdims=True))

    def group_spec(jj):
        return pl.BlockSpec((tm, COL_BLOCK), functools.partial(lambda j, i, jj: (jnp.where(j == jj, i, 0), 0), jj=jj))

    at_end = lambda cols: pl.BlockSpec((tm, cols), lambda j, i: (jnp.where(j == last_j, i, 0), 0))
    tab = lambda: pl.BlockSpec((tm, LANES), lambda j, i: (jnp.where(j < 2, i, 0), 0))
    return pl.pallas_call(
        body, name="in_proj_bwd", grid=(N_DEV, n_tiles),
        out_shape=(jax.ShapeDtypeStruct((SEQ, D_MODEL), F32),
                   jax.ShapeDtypeStruct((N_DEV, D_MODEL, COL_BLOCK), F32),
                   jax.ShapeDtypeStruct((1, D_MODEL), F32)),
        in_specs=[group_spec(jj) for jj in range(N_DEV)] + [
            pl.BlockSpec((D_MODEL, tm), lambda j, i: (0, i)),
            pl.BlockSpec((1, D_MODEL, COL_BLOCK), lambda j, i: (j, 0, 0)),
            at_end(D_MODEL), at_end(D_MODEL), pl.BlockSpec((1, D_MODEL), lambda j, i: (0, 0)),
            tab(), tab(), tab()],
        out_specs=(at_end(D_MODEL), pl.BlockSpec((1, D_MODEL, COL_BLOCK), lambda j, i: (j, 0, 0)),
                   pl.BlockSpec((1, D_MODEL), lambda j, i: (0, 0))),
        scratch_shapes=[pltpu.VMEM((SEQ, D_MODEL), F32)],
        compiler_params=_params(("arbitrary", "arbitrary")),
    )(*d_groups, hn_t, w_g, x, dx2, mix_w, rc, rsa, rsb)


def _adamw(w, g, m, v):
    m = ADAM_B1 * m + (1.0 - ADAM_B1) * g
    v = ADAM_B2 * v + (1.0 - ADAM_B2) * (g * g)
    m_hat = m / (1.0 - ADAM_B1 ** ADAM_STEP)
    v_hat = v / (1.0 - ADAM_B2 ** ADAM_STEP)
    delta = -ADAM_LR * (m_hat / (jnp.sqrt(v_hat) + ADAM_EPS) + ADAM_WD * w)
    return delta, m, v


def _exchange_update(dwin_p, dwout_p, small_p, w_in, m_in, v_in, w_out, m_out, v_out, w_s, m_s, v_s):
    rb = 128

    def body(dwin_hbm, dwout_hbm, small_ref, win_ref, min_ref, vin_ref, wout_ref, mout_ref, vout_ref,
             ws_ref, ms_ref, vs_ref,
             gin_ref, din_ref, nmin_ref, nvin_ref, gout_ref, dout_ref, nmout_ref, nvout_ref,
             gs_ref, ds_ref, nms_ref, nvs_ref,
             land_in, land_out, land_s, send_sems, recv_sems, local_sems):
        me = _my_place()
        my_flat = _flat(me)

        def out_rows(idx):
            return pl.ds(pl.multiple_of(idx * WOUT_ROWS, WOUT_ROWS), WOUT_ROWS)

        own_in = pltpu.make_async_copy(dwin_hbm.at[my_flat], land_in.at[0], local_sems.at[0])
        own_out = pltpu.make_async_copy(dwout_hbm.at[out_rows(my_flat), :], land_out.at[0], local_sems.at[1])
        own_in.start()
        own_out.start()
        land_s[0] = small_ref[...]

        def copies(rel):
            peer = _peer(me, rel)
            pf = _flat(peer)
            mk = lambda which, src, dst: pltpu.make_async_remote_copy(
                src_ref=src, dst_ref=dst, send_sem=send_sems.at[7 * which + rel - 1],
                recv_sem=recv_sems.at[7 * which + rel - 1], device_id=peer, device_id_type=MESH)
            return [mk(0, dwin_hbm.at[pf], land_in.at[rel]),
                    mk(1, dwout_hbm.at[out_rows(pf), :], land_out.at[rel]),
                    mk(2, small_ref, land_s.at[rel])]

        sent = []
        for rel in range(1, N_DEV):
            for cp in copies(rel):
                cp.start()
                sent.append(cp)
        for cp in sent:
            cp.wait_recv()
        for cp in sent:
            cp.wait_send()
        own_in.wait()
        own_out.wait()

        def update(land, w_ref, m_ref, v_ref, g_ref, d_ref, nm_ref, nv_ref, n_rows):
            def step(b, carry):
                rows = pl.ds(pl.multiple_of(b * rb, rb), rb)
                g = land[0, rows, :]
                for rel in range(1, N_DEV):
                    g = g + land[rel, rows, :]
                delta, nm, nv = _adamw(w_ref[rows, :], g, m_ref[rows, :], v_ref[rows, :])
                g_ref[rows, :] = g
                d_ref[rows, :] = delta
                nm_ref[rows, :] = nm
                nv_ref[rows, :] = nv
                return carry
            lax.fori_loop(0, n_rows // rb, step, 0)

        update(land_in, win_ref, min_ref, vin_ref, gin_ref, din_ref, nmin_ref, nvin_ref, D_MODEL)
        update(land_out, wout_ref, mout_ref, vout_ref, gout_ref, dout_ref, nmout_ref, nvout_ref, WOUT_ROWS)

        g = land_s[my_flat ^ 0]
        for dev in range(1, N_DEV):
            g = g + land_s[my_flat ^ dev]
        delta, nm, nv = _adamw(ws_ref[...], g, ms_ref[...], vs_ref[...])
        gs_ref[...] = g
        ds_ref[...] = delta
        nms_ref[...] = nm
        nvs_ref[...] = nv

    vm = lambda: pl.BlockSpec(memory_space=pltpu.VMEM)
    anyspace = lambda: pl.BlockSpec(memory_space=pl.ANY)
    big = jax.ShapeDtypeStruct((D_MODEL, COL_BLOCK), F32)
    flat = jax.ShapeDtypeStruct((WOUT_ROWS, D_MODEL), F32)
    small = jax.ShapeDtypeStruct((SMALL_ROWS, LANES), F32)
    return pl.pallas_call(
        body, name="exchange_update",
        out_shape=tuple([big] * 4 + [flat] * 4 + [small] * 4),
        in_specs=[anyspace(), anyspace()] + [vm() for _ in range(10)],
        out_specs=tuple(vm() for _ in range(12)),
        scratch_shapes=[pltpu.VMEM((N_DEV, D_MODEL, COL_BLOCK), F32),
                        pltpu.VMEM((N_DEV, WOUT_ROWS, D_MODEL), F32),
                        pltpu.VMEM((N_DEV, SMALL_ROWS, LANES), F32),
                        pltpu.SemaphoreType.DMA((21,)), pltpu.SemaphoreType.DMA((21,)),
                        pltpu.SemaphoreType.DMA((2,))],
        compiler_params=_params(),
    )(dwin_p, dwout_p, small_p, w_in, m_in, v_in, w_out, m_out, v_out, w_s, m_s, v_s)


def _pack_small(mix, attn, hgrn, lb, final, loss=None):
    def rows8(a):
        a = a.reshape(-1, LANES)
        return jnp.pad(a, ((0, 8 - a.shape[0]), (0, 0)))
    last = jnp.zeros((8, LANES), F32) if loss is None else jnp.pad(loss.reshape(1, 1), ((0, 7), (0, LANES - 1)))
    return jnp.concatenate([rows8(mix), rows8(attn), rows8(hgrn), rows8(lb), rows8(final), last], axis=0)


def _unpack_small(slab):
    return (slab[ROW_MIX:ROW_MIX + 8].reshape(1, D_MODEL), slab[ROW_ATTN:ROW_ATTN + 4].reshape(1, ATTN_WIDTH),
            slab[ROW_HGRN:ROW_HGRN + 4].reshape(1, HGRN_WIDTH), slab[ROW_LB:ROW_LB + 8].reshape(2, HGRN_WIDTH),
            slab[ROW_FINAL:ROW_FINAL + 8].reshape(D_MODEL))


def _local_step(x, pos_col, w_in_g, w_out_g, mix_w, attn_w, hgrn_w, lb_raw, final_w, target):
    inv = ROPE_THETA ** (-jnp.arange(ROPE_HALF, dtype=F32) * (2.0 / ROPE_DIMS))
    lane_e = jnp.arange(LANES) % HEAD_DIM
    inv_lanes = jnp.where(lane_e < ROPE_DIMS, inv[lane_e % ROPE_HALF], 0.0).reshape(1, LANES)
    rc, rsa, rsb = _rope_tables(pos_col, inv_lanes)

    proj, hn_t = _in_proj_fwd(x, mix_w, w_in_g, rc, rsa, rsb)
    acc = None
    for n, d in enumerate(DILATIONS):
        acc = _attn_fwd(proj, acc, d, last=(n == len(DILATIONS) - 1))
    attn_o, lse = acc
    rec, states = _hgrn_fwd(proj, lb_raw)

    (dx2, d_o, delta, d_ag, d_rec, d_hg, dwout_p, d_final, d_attn_w, d_hgrn_w, loss) = _mid(
        attn_o, rec, proj, x, target, w_out_g, attn_w, hgrn_w, final_w.reshape(1, D_MODEL))

    dqkv = None
    for d in DILATIONS:
        dqkv = _attn_bwd(proj, d_o, lse, delta, dqkv, d)
    d_hq, d_hf, d_hi, d_lb = _hgrn_bwd(proj, lb_raw, d_rec, states)

    grad_x, dwin_p, d_mix = _in_proj_bwd(
        (dqkv[0], dqkv[1], dqkv[2], d_ag, d_hq, d_hf, d_hi, d_hg), hn_t, w_in_g, x, dx2, mix_w, rc, rsa, rsb)
    small_p = _pack_small(d_mix, d_attn_w, d_hgrn_w, d_lb, d_final, loss)
    return grad_x, dwin_p, dwout_p, small_p


def kernel(x, positions, w_in, w_out, mix_norm_w, attn_out_norm_w, hgrn_out_norm_w, hgrn_lb_raw, final_norm_w, loss_target, m_w_in, m_w_out, m_mix_norm_w, m_attn_out_norm_w, m_hgrn_out_norm_w, m_hgrn_lb_raw, m_final_norm_w, v_w_in, v_w_out, v_mix_norm_w, v_attn_out_norm_w, v_hgrn_out_norm_w, v_hgrn_lb_raw, v_final_norm_w):
    w_in_g, w_out_g = _gather_weights(w_in[0], w_out[0])
    grad_x, dwin_p, dwout_p, small_p = _local_step(
        x[0], positions.reshape(SEQ, 1), w_in_g, w_out_g, mix_norm_w, attn_out_norm_w, hgrn_out_norm_w,
        hgrn_lb_raw, final_norm_w, loss_target[0])

    w_s = _pack_small(mix_norm_w, attn_out_norm_w, hgrn_out_norm_w, hgrn_lb_raw, final_norm_w)
    m_s = _pack_small(m_mix_norm_w, m_attn_out_norm_w, m_hgrn_out_norm_w, m_hgrn_lb_raw, m_final_norm_w)
    v_s = _pack_small(v_mix_norm_w, v_attn_out_norm_w, v_hgrn_out_norm_w, v_hgrn_lb_raw, v_final_norm_w)
    (g_in, d_in, nm_in, nv_in, g_out, d_out, nm_out, nv_out, g_s, d_s, nm_s, nv_s) = _exchange_update(
        dwin_p, dwout_p, small_p, w_in[0], m_w_in[0], v_w_in[0], w_out[0], m_w_out[0], v_w_out[0], w_s, m_s, v_s)

    loss = g_s[ROW_LOSS, 0]
    return (loss, grad_x[None], g_in[None], g_out[None], *_unpack_small(g_s),
            d_in[None], d_out[None], *_unpack_small(d_s),
            nm_in[None], nm_out[None], *_unpack_small(nm_s),
            nv_in[None], nv_out[None], *_unpack_small(nv_s))
```

```python
import functools

import jax
import jax.numpy as jnp
from jax import lax
from jax.experimental import pallas as pl
from jax.experimental.pallas import tpu as pltpu

F32 = jnp.float32
BF16 = jnp.bfloat16

SEQ = 4096
D_MODEL = 1024
ATTN_WIDTH = 512
HGRN_WIDTH = 512
HEAD_DIM = 64
HGRN_HEADS = 4
HGRN_DIM = 128
HGRN_CHUNK = 64
N_CHUNKS = SEQ // HGRN_CHUNK
IN_COLS = 4096
COL_BLOCK = 512
N_DEV = 8
WOUT_ROWS = D_MODEL // N_DEV
ATTN_BLOCK = 128
DILATIONS = (1, 4, 16)
ROPE_THETA = 500000.0
ROPE_DIMS = 16
ROPE_HALF = 8
NORM_EPS = 1e-6
NEG_BIG = -1e30
LANES = 128

ADAM_LR = 0.001
ADAM_B1 = 0.9
ADAM_B2 = 0.999
ADAM_EPS = 1e-08
ADAM_WD = 0.01
ADAM_STEP = 10

SMALL_ROWS = 48
ROW_MIX, ROW_ATTN, ROW_HGRN, ROW_LB, ROW_FINAL, ROW_LOSS = 0, 8, 16, 24, 32, 40

VMEM_LIMIT = 56 * 1024 * 1024
MESH = pl.DeviceIdType.MESH


def _mm(a, b):
    return lax.dot_general(a, b, (((1,), (0,)), ((), ())), preferred_element_type=F32)


def _mm_nt(a, b):
    return lax.dot_general(a, b, (((1,), (1,)), ((), ())), preferred_element_type=F32)


def _mm_tn(a, b):
    return lax.dot_general(a, b, (((0,), (0,)), ((), ())), preferred_element_type=F32)


def _mm_exact(a, b):
    return lax.dot_general(a, b, (((1,), (0,)), ((), ())), preferred_element_type=F32,
                           precision=lax.Precision.HIGHEST)


def _sigmoid(v):
    return 1.0 / (1.0 + jnp.exp(-v))


def _params(sem=None, **kw):
    return pltpu.CompilerParams(dimension_semantics=sem, vmem_limit_bytes=VMEM_LIMIT, **kw)


def _my_place():
    return lax.axis_index("x"), lax.axis_index("y"), lax.axis_index("c")


def _peer(place, rel):
    x, y, c = place
    return (x ^ ((rel >> 2) & 1), y ^ ((rel >> 1) & 1), c ^ (rel & 1))


def _flat(place):
    x, y, c = place
    return 4 * x + 2 * y + c


def _gather_weights(w_in, w_out):
    def body(win_ref, wout_ref, gin_ref, gout_ref, send_sems, recv_sems):
        me = _my_place()
        x, y, c = me
        sibling = (x, y, 1 - c)
        chips = [(1 - x, y), (x, 1 - y), (1 - x, 1 - y)]

        def slab(which, place):
            idx = _flat(place)
            if which == 0:
                return gin_ref.at[idx]
            return gout_ref.at[pl.ds(pl.multiple_of(idx * WOUT_ROWS, WOUT_ROWS), WOUT_ROWS), :]

        def copy(which, k, block, to):
            ref = slab(which, block)
            return pltpu.make_async_remote_copy(
                src_ref=ref, dst_ref=ref, send_sem=send_sems.at[7 * which + k],
                recv_sem=recv_sems.at[7 * which + k], device_id=to, device_id_type=MESH)

        gin_ref[_flat(me)] = win_ref[...].astype(BF16)
        gout_ref[pl.ds(pl.multiple_of(_flat(me) * WOUT_ROWS, WOUT_ROWS), WOUT_ROWS), :] = (
            wout_ref[...].astype(BF16))

        started = []
        for which in (0, 1):
            first = [copy(which, 0, me, sibling)]
            first += [copy(which, 1 + j, me, (*chip, c)) for j, chip in enumerate(chips)]
            for cp in first:
                cp.start()
            started += first
        for which in (0, 1):
            for j, chip in enumerate(chips):
                copy(which, 1 + j, (*chip, c), me).wait_recv()
                fwd = copy(which, 4 + j, (*chip, c), sibling)
                fwd.start()
                started.append(fwd)
        for which in (0, 1):
            copy(which, 0, sibling, me).wait_recv()
            for j, chip in enumerate(chips):
                copy(which, 4 + j, (*chip, 1 - c), me).wait_recv()
        for cp in started:
            cp.wait_send()

    return pl.pallas_call(
        body, name="gather_weights",
        out_shape=(jax.ShapeDtypeStruct((N_DEV, D_MODEL, COL_BLOCK), BF16),
                   jax.ShapeDtypeStruct((D_MODEL, D_MODEL), BF16)),
        in_specs=[pl.BlockSpec(memory_space=pltpu.VMEM), pl.BlockSpec(memory_space=pltpu.VMEM)],
        out_specs=(pl.BlockSpec(memory_space=pltpu.VMEM), pl.BlockSpec(memory_space=pltpu.VMEM)),
        scratch_shapes=[pltpu.SemaphoreType.DMA((14,)), pltpu.SemaphoreType.DMA((14,))],
        compiler_params=_params(),
    )(w_in, w_out)


def _rope_tables(pos_col, inv_freq_lanes):
    tm = 512

    def body(pos_ref, invf_ref, c_ref, sa_ref, sb_ref):
        ang = pos_ref[...].astype(F32) * invf_ref[...]
        e = lax.broadcasted_iota(jnp.int32, (tm, LANES), 1) & (HEAD_DIM - 1)
        cos, sin = jnp.cos(ang), jnp.sin(ang)
        c_ref[...] = jnp.where(e < ROPE_DIMS, cos, 1.0)
        sa_ref[...] = jnp.where((e >= ROPE_HALF) & (e < ROPE_DIMS), sin, 0.0)
        sb_ref[...] = jnp.where(e < ROPE_HALF, -sin, 0.0)

    tab = jax.ShapeDtypeStruct((SEQ, LANES), F32)
    spec = pl.BlockSpec((tm, LANES), lambda i: (i, 0))
    return pl.pallas_call(
        body, name="rope_tables", grid=(SEQ // tm,), out_shape=(tab, tab, tab),
        in_specs=[pl.BlockSpec((tm, 1), lambda i: (i, 0)), pl.BlockSpec((1, LANES), lambda i: (0, 0))],
        out_specs=(spec, spec, spec), compiler_params=_params(("parallel",)),
    )(pos_col, inv_freq_lanes)


def _rot(t, c, sa, sb):
    n = t.shape[1]
    return t * c + pltpu.roll(t, ROPE_HALF, 1) * sa + pltpu.roll(t, n - ROPE_HALF, 1) * sb


def _rot_transposed(g, c, sa, sb):
    n = g.shape[1]
    return g * c + pltpu.roll(g * sa, n - ROPE_HALF, 1) + pltpu.roll(g * sb, ROPE_HALF, 1)


def _in_proj_fwd(x, mix_w, w_g, rc, rsa, rsb):
    tm = 256

    def body(x_ref, w_ref, wg_ref, c_ref, sa_ref, sb_ref, proj_ref, hnt_ref):
        xf = x_ref[...]
        ms = jnp.mean(xf * xf, axis=-1, keepdims=True)
        hn = xf * lax.rsqrt(ms + NORM_EPS) * w_ref[...]
        hnt_ref[...] = hn.T.astype(BF16)
        hb = hn.astype(BF16)
        c = jnp.tile(c_ref[...], (1, 4))
        sa = jnp.tile(sa_ref[...], (1, 4))
        sb = jnp.tile(sb_ref[...], (1, 4))
        for j in range(N_DEV):
            acc = _mm(hb, wg_ref[j])
            if j < 2:
                acc = _rot(acc, c, sa, sb)
            proj_ref[:, COL_BLOCK * j:COL_BLOCK * (j + 1)] = acc

    tab = pl.BlockSpec((tm, LANES), lambda i: (i, 0))
    return pl.pallas_call(
        body, name="in_proj_fwd", grid=(SEQ // tm,),
        out_shape=(jax.ShapeDtypeStruct((SEQ, IN_COLS), F32),
                   jax.ShapeDtypeStruct((D_MODEL, SEQ), BF16)),
        in_specs=[pl.BlockSpec((tm, D_MODEL), lambda i: (i, 0)),
                  pl.BlockSpec((1, D_MODEL), lambda i: (0, 0)),
                  pl.BlockSpec((N_DEV, D_MODEL, COL_BLOCK), lambda i: (0, 0, 0)),
                  tab, tab, tab],
        out_specs=(pl.BlockSpec((tm, IN_COLS), lambda i: (i, 0)),
                   pl.BlockSpec((D_MODEL, tm), lambda i: (0, i))),
        compiler_params=_params(("parallel",)),
    )(x, mix_w, w_g, rc, rsa, rsb)


def _band_masks():
    qi = lax.broadcasted_iota(jnp.int32, (ATTN_BLOCK, 2 * ATTN_BLOCK), 0)
    kj = lax.broadcasted_iota(jnp.int32, (ATTN_BLOCK, 2 * ATTN_BLOCK), 1)
    both = (kj >= qi) & (kj <= qi + ATTN_BLOCK)
    qi1 = lax.broadcasted_iota(jnp.int32, (ATTN_BLOCK, ATTN_BLOCK), 0)
    kj1 = lax.broadcasted_iota(jnp.int32, (ATTN_BLOCK, ATTN_BLOCK), 1)
    return kj1 <= qi1, both


def _head0_lanes():
    return lax.broadcasted_iota(jnp.int32, (ATTN_BLOCK, LANES), 1) < HEAD_DIM


def _strided(start, size, d):
    return pl.ds(start, size) if d == 1 else pl.ds(start, size, stride=d)


def _for_each_block(d, block, mask_first, mask_both):
    span = d * ATTN_BLOCK

    def residue(r, carry):
        block(r, r, ATTN_BLOCK, mask_first)

        def step(n, c):
            row0 = n * span + r
            if d == 1:
                row0 = pl.multiple_of(row0, ATTN_BLOCK)
            block(row0, row0 - span, 2 * ATTN_BLOCK, mask_both)
            return c

        lax.fori_loop(1, SEQ // span, step, 0)
        return carry

    if d == 1:
        residue(0, 0)
    else:
        lax.fori_loop(0, d, residue, 0)


def _attn_fwd_fused(proj):
    n_pat = len(DILATIONS)

    def body(q_ref, k_ref, v_ref, o_ref, lse_ref, l_acc):
        mask_first, mask_both = _band_masks()
        h0 = _head0_lanes()
        for pi, d in enumerate(DILATIONS):
            first, last = pi == 0, pi == n_pat - 1

            def block(row0, key0, nk, mask, d=d, first=first, last=last):
                rows = _strided(row0, ATTN_BLOCK, d)
                keys = _strided(key0, nk, d)
                q = q_ref[rows, :]
                kb = k_ref[keys, :].astype(BF16)
                vb = v_ref[keys, :].astype(BF16)
                if not first:
                    o_old, m_old, l_old = o_ref[rows, :], lse_ref[rows, :], l_acc[rows, :]
                res = []
                for h in range(2):
                    hm = h0 if h == 0 else jnp.logical_not(h0)
                    qh = jnp.where(hm, q, 0.0).astype(BF16)
                    s = jnp.where(mask, _mm_nt(qh, kb) * 0.125, NEG_BIG)
                    mb = jnp.max(s, axis=-1, keepdims=True)
                    if first:
                        mn = mb
                    else:
                        mo = m_old[:, HEAD_DIM * h:HEAD_DIM * h + 1]
                        lo = l_old[:, HEAD_DIM * h:HEAD_DIM * h + 1]
                        mn = jnp.maximum(mo, mb)
                        alpha = jnp.exp(mo - mn)
                    p = jnp.exp(s - mn)
                    ls = jnp.sum(p, axis=-1, keepdims=True)
                    pv = _mm(p.astype(BF16), vb)
                    if first:
                        res.append((pv, mn, ls))
                    else:
                        res.append((alpha * o_old + pv, mn, alpha * lo + ls))
                o_t = jnp.where(h0, res[0][0], res[1][0])
                m_t = jnp.where(h0, res[0][1], res[1][1])
                l_t = jnp.where(h0, res[0][2], res[1][2])
                if last:
                    o_ref[rows, :] = o_t / l_t
                    lse_ref[rows, :] = m_t + jnp.log(l_t)
                else:
                    o_ref[rows, :] = o_t
                    lse_ref[rows, :] = m_t
                    l_acc[rows, :] = l_t

            _for_each_block(d, block, mask_first, mask_both)

    slab = lambda g: pl.BlockSpec((SEQ, LANES), functools.partial(lambda hp, g: (0, 4 * g + hp), g=g))
    wide = jax.ShapeDtypeStruct((SEQ, ATTN_WIDTH), F32)
    return pl.pallas_call(
        body, name="attn_fwd", grid=(4,), out_shape=(wide, wide),
        in_specs=[slab(0), slab(1), slab(2)], out_specs=(slab(0), slab(0)),
        scratch_shapes=[pltpu.VMEM((SEQ, LANES), F32)],
        compiler_params=_params(("parallel",)),
    )(proj, proj, proj)


def _attn_bwd_fused(proj, d_out, lse, delta):
    def body(q_ref, k_ref, v_ref, do_ref, lse_ref, del_ref, dq_ref, dk_ref, dv_ref):
        dq_ref[...] = jnp.zeros_like(dq_ref)
        dk_ref[...] = jnp.zeros_like(dk_ref)
        dv_ref[...] = jnp.zeros_like(dv_ref)
        mask_first, mask_both = _band_masks()
        h0 = _head0_lanes()
        for d in DILATIONS:
            def block(row0, key0, nk, mask, d=d):
                rows = _strided(row0, ATTN_BLOCK, d)
                keys = _strided(key0, nk, d)
                q, g = q_ref[rows, :], do_ref[rows, :]
                lse_t, del_t = lse_ref[rows, :], del_ref[rows, :]
                kb = k_ref[keys, :].astype(BF16)
                vb = v_ref[keys, :].astype(BF16)
                dq_h = []
                dk_c = jnp.zeros((nk, LANES), F32)
                dv_c = jnp.zeros((nk, LANES), F32)
                for h in range(2):
                    hm = h0 if h == 0 else jnp.logical_not(h0)
                    qh = jnp.where(hm, q, 0.0).astype(BF16)
                    gh = jnp.where(hm, g, 0.0).astype(BF16)
                    s = _mm_nt(qh, kb) * 0.125
                    p = jnp.where(mask, jnp.exp(s - lse_t[:, HEAD_DIM * h:HEAD_DIM * h + 1]), 0.0)
                    dp = _mm_nt(gh, vb)
                    ds = (p * (dp - del_t[:, HEAD_DIM * h:HEAD_DIM * h + 1]) * 0.125).astype(BF16)
                    dq_h.append(_mm(ds, kb))
                    dk_c = dk_c + _mm_tn(ds, qh)
                    dv_c = dv_c + _mm_tn(p.astype(BF16), gh)
                dq_ref[rows, :] += jnp.where(h0, dq_h[0], dq_h[1])
                dk_ref[keys, :] += dk_c
                dv_ref[keys, :] += dv_c

            _for_each_block(d, block, mask_first, mask_both)

    slab = lambda g: pl.BlockSpec((SEQ, LANES), functools.partial(lambda hp, g: (0, 4 * g + hp), g=g))
    wide = jax.ShapeDtypeStruct((SEQ, ATTN_WIDTH), F32)
    return pl.pallas_call(
        body, name="attn_bwd", grid=(4,), out_shape=(wide, wide, wide),
        in_specs=[slab(0), slab(1), slab(2), slab(0), slab(0), slab(0)], out_specs=(slab(0), slab(0), slab(0)),
        compiler_params=_params(("parallel",)),
    )(proj, proj, proj, d_out, lse, delta)


def _hgrn_lower_bound(lb_ref):
    r0, r1 = lb_ref[0:1, :], lb_ref[1:2, :]
    mx = jnp.maximum(r0, r1)
    e0, e1 = jnp.exp(r0 - mx), jnp.exp(r1 - mx)
    return e0 / (e0 + e1)


def _hgrn_gates(hq, hf, lb):
    sq = _sigmoid(hq)
    sg = _sigmoid(hf)
    f = lb + (1.0 - lb) * sg
    return hq * sq, sq, sg, f, 1.0 - f, jnp.log(f)


def _hgrn_specs():
    cols = lambda g: pl.BlockSpec((SEQ, HGRN_DIM), functools.partial(lambda h, g: (0, 4 * g + h), g=g))
    head = pl.BlockSpec((SEQ, HGRN_DIM), lambda h: (0, h))
    lb = pl.BlockSpec((2, HGRN_DIM), lambda h: (0, h))
    states = pl.BlockSpec((1, N_CHUNKS, HGRN_DIM, HGRN_DIM), lambda h: (h, 0, 0, 0))
    return cols, head, lb, states


def _hgrn_fwd(proj, lb_raw):
    t = HGRN_CHUNK

    def body(hq_ref, hf_ref, hi_ref, lb_ref, rec_ref, st_ref, state):
        lb = _hgrn_lower_bound(lb_ref)
        ri = lax.broadcasted_iota(jnp.int32, (t, t), 0)
        ci = lax.broadcasted_iota(jnp.int32, (t, t), 1)
        causal = ri >= ci
        tril = causal.astype(F32)
        state[...] = jnp.zeros_like(state)

        def chunk(n, carry):
            rows = pl.ds(pl.multiple_of(n * t, t), t)
            q, _, _, _, k, lf = _hgrn_gates(hq_ref[rows, :], hf_ref[rows, :], lb)
            cum = _mm_exact(tril, lf)
            last = cum[t - 1:t, :]
            qd = (q * jnp.exp(cum)).astype(BF16)
            ki = (k * jnp.exp(-cum)).astype(BF16)
            ke = (k * jnp.exp(last - cum)).astype(BF16)
            vb = hi_ref[rows, :].astype(BF16)
            att = jnp.where(causal, _mm_nt(qd, ki), 0.0)
            st = state[...]
            st_ref[0, n] = st
            rec_ref[rows, :] = _mm(att.astype(BF16), vb) + _mm_nt(qd, st.astype(BF16))
            state[...] = st * jnp.exp(last) + _mm_tn(vb, ke)
            return carry

        lax.fori_loop(0, N_CHUNKS, chunk, 0)

    cols, head, lb, states = _hgrn_specs()
    return pl.pallas_call(
        body, name="hgrn_fwd", grid=(HGRN_HEADS,),
        out_shape=(jax.ShapeDtypeStruct((SEQ, HGRN_WIDTH), F32),
                   jax.ShapeDtypeStruct((HGRN_HEADS, N_CHUNKS, HGRN_DIM, HGRN_DIM), F32)),
        in_specs=[cols(4), cols(5), cols(6), lb], out_specs=(head, states),
        scratch_shapes=[pltpu.VMEM((HGRN_DIM, HGRN_DIM), F32)],
        compiler_params=_params(("parallel",)),
    )(proj, proj, proj, lb_raw)


def _hgrn_bwd(proj, lb_raw, d_rec, states):
    t = HGRN_CHUNK

    def body(hq_ref, hf_ref, hi_ref, lb_ref, do_ref, st_ref, dhq_ref, dhf_ref, dhi_ref, dlb_ref,
             dstate, dlb_acc):
        lb = _hgrn_lower_bound(lb_ref)
        ri = lax.broadcasted_iota(jnp.int32, (t, t), 0)
        ci = lax.broadcasted_iota(jnp.int32, (t, t), 1)
        causal = ri >= ci
        tril = causal.astype(F32)
        triu = (ri <= ci).astype(F32)
        last_row = lax.broadcasted_iota(jnp.int32, (t, HGRN_DIM), 0) == t - 1
        dstate[...] = jnp.zeros_like(dstate)
        dlb_acc[...] = jnp.zeros_like(dlb_acc)

        def chunk(i, carry):
            n = N_CHUNKS - 1 - i
            rows = pl.ds(pl.multiple_of(n * t, t), t)
            hq = hq_ref[rows, :]
            q, sq, sg, f, k, lf = _hgrn_gates(hq, hf_ref[rows, :], lb)
            cum = _mm_exact(tril, lf)
            last = cum[t - 1:t, :]
            e_cum, e_inv, e_end, dec = jnp.exp(cum), jnp.exp(-cum), jnp.exp(last - cum), jnp.exp(last)
            qd, ki, ke = q * e_cum, k * e_inv, k * e_end
            qdb, kib, keb = qd.astype(BF16), ki.astype(BF16), ke.astype(BF16)
            vb = hi_ref[rows, :].astype(BF16)
            gb = do_ref[rows, :].astype(BF16)
            st_prev = st_ref[0, n]
            dst = dstate[...]
            dstb = dst.astype(BF16)

            att = jnp.where(causal, _mm_nt(qdb, kib), 0.0).astype(BF16)
            datt = jnp.where(causal, _mm_nt(gb, vb), 0.0).astype(BF16)
            dv = _mm_tn(att, gb) + _mm_nt(keb, dstb)
            dqd = _mm(datt, kib) + _mm(gb, st_prev.astype(BF16))
            dki = _mm_tn(datt, qdb)
            dke = _mm(vb, dstb)
            ddec = jnp.sum(dst * st_prev, axis=0, keepdims=True)
            dstate[...] = dst * dec + _mm_tn(gb, qdb)

            dq = dqd * e_cum
            dk = dki * e_inv + dke * e_end
            dlast = jnp.sum(dke * ke, axis=0, keepdims=True) + ddec * dec
            dcum = dqd * qd - dki * ki - dke * ke + jnp.where(last_row, dlast, 0.0)
            dlf = _mm_exact(triu, dcum)
            df = dlf / f - dk
            dhq_ref[rows, :] = dq * (sq * (1.0 + hq * (1.0 - sq)))
            dhf_ref[rows, :] = df * (1.0 - lb) * (sg * (1.0 - sg))
            dhi_ref[rows, :] = dv
            dlb_acc[...] += jnp.sum(df * (1.0 - sg), axis=0, keepdims=True)
            return carry

        lax.fori_loop(0, N_CHUNKS, chunk, 0)
        g0 = dlb_acc[...] * lb * (1.0 - lb)
        dlb_ref[...] = jnp.concatenate([g0, -g0], axis=0)

    cols, head, lb_spec, st_spec = _hgrn_specs()
    wide = jax.ShapeDtypeStruct((SEQ, HGRN_WIDTH), F32)
    return pl.pallas_call(
        body, name="hgrn_bwd", grid=(HGRN_HEADS,),
        out_shape=(wide, wide, wide, jax.ShapeDtypeStruct((2, HGRN_WIDTH), F32)),
        in_specs=[cols(4), cols(5), cols(6), lb_spec, head, st_spec],
        out_specs=(head, head, head, lb_spec),
        scratch_shapes=[pltpu.VMEM((HGRN_DIM, HGRN_DIM), F32), pltpu.VMEM((1, HGRN_DIM), F32)],
        compiler_params=_params(("parallel",)),
    )(proj, proj, proj, lb_raw, d_rec, states)


def _group_sum(v, group):
    parts = []
    for s in range(v.shape[1] // LANES):
        slab = v[:, LANES * s:LANES * (s + 1)]
        if group == LANES:
            parts.append(jnp.broadcast_to(jnp.sum(slab, axis=-1, keepdims=True), slab.shape))
        else:
            h0 = lax.broadcasted_iota(jnp.int32, slab.shape, 1) < HEAD_DIM
            s0 = jnp.sum(jnp.where(h0, slab, 0.0), axis=-1, keepdims=True)
            s1 = jnp.sum(jnp.where(h0, 0.0, slab), axis=-1, keepdims=True)
            parts.append(jnp.where(h0, s0, s1))
    return jnp.concatenate(parts, axis=1)


def _mid(attn_o, rec, proj, x, target, w_out_g, attn_w, hgrn_w, final_w):
    tm = 256

    def branch_fwd(o, gate, w, group):
        r = lax.rsqrt(_group_sum(o * o, group) * (1.0 / group) + NORM_EPS)
        nrm = o * r
        sg = _sigmoid(gate)
        return r, nrm, sg, nrm * w * (gate * sg)

    def branch_bwd(dy, r, nrm, sg, gate, w, group):
        silu = gate * sg
        d_gate = dy * nrm * w * (sg * (1.0 + gate * (1.0 - sg)))
        d_w = jnp.sum(dy * nrm * silu, axis=0, keepdims=True)
        dn = dy * w * silu
        d_o = r * (dn - nrm * (_group_sum(dn * nrm, group) * (1.0 / group)))
        return d_o, d_gate, d_w

    def body(o_ref, rec_ref, ag_ref, hg_ref, x_ref, tgt_ref, wout_ref, aw_ref, hw_ref, fw_ref,
             dx2_ref, do_ref, delta_ref, dag_ref, drec_ref, dhg_ref, dwout_ref, dfw_ref, daw_ref, dhw_ref,
             loss_ref):
        i = pl.program_id(0)

        @pl.when(i == 0)
        def _():
            dwout_ref[...] = jnp.zeros_like(dwout_ref)
            dfw_ref[...] = jnp.zeros_like(dfw_ref)
            daw_ref[...] = jnp.zeros_like(daw_ref)
            dhw_ref[...] = jnp.zeros_like(dhw_ref)
            loss_ref[...] = jnp.zeros_like(loss_ref)

        o, rc, ag, hg = o_ref[...], rec_ref[...], ag_ref[...], hg_ref[...]
        aw, hw, fw = aw_ref[...], hw_ref[...], fw_ref[...]
        ra, na, sga, ya = branch_fwd(o, ag, aw, HEAD_DIM)
        rh, nh, sgh, yh = branch_fwd(rc, hg, hw, HGRN_DIM)
        mixed = jnp.concatenate([ya, yh], axis=1).astype(BF16)
        wout = wout_ref[...]
        x2 = x_ref[...] + _mm(mixed, wout)
        rstd = lax.rsqrt(jnp.mean(x2 * x2, axis=-1, keepdims=True) + NORM_EPS)
        xn = x2 * rstd
        err = xn * fw - tgt_ref[...]
        row_loss = jnp.mean(err * err, axis=-1, keepdims=True)
        loss_ref[...] += 0.5 * jnp.sum(row_loss, axis=0, keepdims=True)
        dy = err * (1.0 / D_MODEL)
        dfw_ref[...] += jnp.sum(dy * xn, axis=0, keepdims=True)
        dxn = dy * fw
        dx2 = rstd * (dxn - xn * jnp.mean(dxn * xn, axis=-1, keepdims=True))
        dx2_ref[...] = dx2
        dx2b = dx2.astype(BF16)
        dwout_ref[...] += _mm_tn(mixed, dx2b)
        dmixed = _mm_nt(dx2b, wout)

        d_o, d_ag, d_aw = branch_bwd(dmixed[:, :ATTN_WIDTH], ra, na, sga, ag, aw, HEAD_DIM)
        d_rec, d_hg, d_hw = branch_bwd(dmixed[:, ATTN_WIDTH:], rh, nh, sgh, hg, hw, HGRN_DIM)
        do_ref[...] = d_o
        delta_ref[...] = _group_sum(d_o * o, HEAD_DIM)
        dag_ref[...] = d_ag
        drec_ref[...] = d_rec
        dhg_ref[...] = d_hg
        daw_ref[...] += d_aw
        dhw_ref[...] += d_hw

    half = lambda: pl.BlockSpec((tm, COL_BLOCK), lambda i: (i, 0))
    full = lambda: pl.BlockSpec((tm, D_MODEL), lambda i: (i, 0))
    fixed = lambda r, c: pl.BlockSpec((r, c), lambda i: (0, 0))
    wide = jax.ShapeDtypeStruct((SEQ, COL_BLOCK), F32)
    return pl.pallas_call(
        body, name="mid", grid=(SEQ // tm,),
        out_shape=(jax.ShapeDtypeStruct((SEQ, D_MODEL), F32), wide, wide, wide, wide, wide,
                   jax.ShapeDtypeStruct((D_MODEL, D_MODEL), F32),
                   jax.ShapeDtypeStruct((1, D_MODEL), F32), jax.ShapeDtypeStruct((1, COL_BLOCK), F32),
                   jax.ShapeDtypeStruct((1, COL_BLOCK), F32), jax.ShapeDtypeStruct((1, 1), F32)),
        in_specs=[half(), half(),
                  pl.BlockSpec((tm, COL_BLOCK), lambda i: (i, 3)), pl.BlockSpec((tm, COL_BLOCK), lambda i: (i, 7)),
                  full(), full(), fixed(D_MODEL, D_MODEL), fixed(1, COL_BLOCK), fixed(1, COL_BLOCK),
                  fixed(1, D_MODEL)],
        out_specs=(full(), half(), half(), half(), half(), half(), fixed(D_MODEL, D_MODEL),
                   fixed(1, D_MODEL), fixed(1, COL_BLOCK), fixed(1, COL_BLOCK), fixed(1, 1)),
        compiler_params=_params(("arbitrary",)),
    )(attn_o, rec, proj, proj, x, target, w_out_g, attn_w, hgrn_w, final_w)


def _in_proj_bwd(d_groups, hn_t, w_g, x, dx2, mix_w, rc, rsa, rsb):
    tm = 256
    n_tiles = SEQ // tm
    last_j = N_DEV - 1

    def body(*refs):
        dg_refs = refs[:N_DEV]
        hnt_ref, wg_ref, x_ref, dx2_ref, w_ref, c_ref, sa_ref, sb_ref = refs[N_DEV:N_DEV + 8]
        gx_ref, dwin_ref, dmw_ref, dhn = refs[N_DEV + 8:]
        j, i = pl.program_id(0), pl.program_id(1)
        rows = pl.ds(pl.multiple_of(i * tm, tm), tm)

        for jj in range(N_DEV):
            @pl.when(j == jj)
            def _(jj=jj):
                dp = dg_refs[jj][...]
                if jj < 2:
                    dp = _rot_transposed(dp, jnp.tile(c_ref[...], (1, 4)), jnp.tile(sa_ref[...], (1, 4)),
                                         jnp.tile(sb_ref[...], (1, 4)))
                dpb = dp.astype(BF16)
                contrib = _mm_nt(dpb, wg_ref[0])
                if jj == 0:
                    dhn[rows, :] = contrib
                else:
                    dhn[rows, :] += contrib
                dw = _mm(hnt_ref[...], dpb)

                @pl.when(i == 0)
                def _():
                    dwin_ref[0] = dw

                @pl.when(i > 0)
                def _():
                    dwin_ref[0] += dw

        @pl.when((j == last_j) & (i == 0))
        def _():
            dmw_ref[...] = jnp.zeros_like(dmw_ref)

        @pl.when(j == last_j)
        def _():
            xf = x_ref[...]
            w = w_ref[...]
            rstd = lax.rsqrt(jnp.mean(xf * xf, axis=-1, keepdims=True) + NORM_EPS)
            xn = xf * rstd
            g = dhn[rows, :]
            dmw_ref[...] += jnp.sum(g * xn, axis=0, keepdims=True)
            gw = g * w
            gx_ref[...] = dx2_ref[...] + rstd * (gw - xn * jnp.mean(gw * xn, axis=-1, keepdims=True))

    def group_spec(jj):
        return pl.BlockSpec((tm, COL_BLOCK), functools.partial(lambda j, i, jj: (jnp.where(j == jj, i, 0), 0), jj=jj))

    at_end = lambda cols: pl.BlockSpec((tm, cols), lambda j, i: (jnp.where(j == last_j, i, 0), 0))
    tab = lambda: pl.BlockSpec((tm, LANES), lambda j, i: (jnp.where(j < 2, i, 0), 0))
    return pl.pallas_call(
        body, name="in_proj_bwd", grid=(N_DEV, n_tiles),
        out_shape=(jax.ShapeDtypeStruct((SEQ, D_MODEL), F32),
                   jax.ShapeDtypeStruct((N_DEV, D_MODEL, COL_BLOCK), F32),
                   jax.ShapeDtypeStruct((1, D_MODEL), F32)),
        in_specs=[group_spec(jj) for jj in range(N_DEV)] + [
            pl.BlockSpec((D_MODEL, tm), lambda j, i: (0, i)),
            pl.BlockSpec((1, D_MODEL, COL_BLOCK), lambda j, i: (j, 0, 0)),
            at_end(D_MODEL), at_end(D_MODEL), pl.BlockSpec((1, D_MODEL), lambda j, i: (0, 0)),
            tab(), tab(), tab()],
        out_specs=(at_end(D_MODEL), pl.BlockSpec((1, D_MODEL, COL_BLOCK), lambda j, i: (j, 0, 0)),
                   pl.BlockSpec((1, D_MODEL), lambda j, i: (0, 0))),
        scratch_shapes=[pltpu.VMEM((SEQ, D_MODEL), F32)],
        compiler_params=_params(("arbitrary", "arbitrary")),
    )(*d_groups, hn_t, w_g, x, dx2, mix_w, rc, rsa, rsb)


def _adamw(w, g, m, v):
    m = ADAM_B1 * m + (1.0 - ADAM_B1) * g
    v = ADAM_B2 * v + (1.0 - ADAM_B2) * (g * g)
    m_hat = m / (1.0 - ADAM_B1 ** ADAM_STEP)
    v_hat = v / (1.0 - ADAM_B2 ** ADAM_STEP)
    delta = -ADAM_LR * (m_hat / (jnp.sqrt(v_hat) + ADAM_EPS) + ADAM_WD * w)
    return delta, m, v


def _exchange_update(dwin_p, dwout_p, small_p, w_in, m_in, v_in, w_out, m_out, v_out, w_s, m_s, v_s):
    rb = 128

    def body(dwin_hbm, dwout_hbm, small_ref, win_ref, min_ref, vin_ref, wout_ref, mout_ref, vout_ref,
             ws_ref, ms_ref, vs_ref,
             gin_ref, din_ref, nmin_ref, nvin_ref, gout_ref, dout_ref, nmout_ref, nvout_ref,
             gs_ref, ds_ref, nms_ref, nvs_ref,
             land_in, land_out, land_s, send_sems, recv_sems, local_sems):
        me = _my_place()
        my_flat = _flat(me)

        def out_rows(idx):
            return pl.ds(pl.multiple_of(idx * WOUT_ROWS, WOUT_ROWS), WOUT_ROWS)

        own_in = pltpu.make_async_copy(dwin_hbm.at[my_flat], land_in.at[0], local_sems.at[0])
        own_out = pltpu.make_async_copy(dwout_hbm.at[out_rows(my_flat), :], land_out.at[0], local_sems.at[1])
        own_in.start()
        own_out.start()
        land_s[0] = small_ref[...]

        def copies(rel):
            peer = _peer(me, rel)
            pf = _flat(peer)
            mk = lambda which, src, dst: pltpu.make_async_remote_copy(
                src_ref=src, dst_ref=dst, send_sem=send_sems.at[7 * which + rel - 1],
                recv_sem=recv_sems.at[7 * which + rel - 1], device_id=peer, device_id_type=MESH)
            return [mk(0, dwin_hbm.at[pf], land_in.at[rel]),
                    mk(1, dwout_hbm.at[out_rows(pf), :], land_out.at[rel]),
                    mk(2, small_ref, land_s.at[rel])]

        sent = []
        for rel in range(1, N_DEV):
            for cp in copies(rel):
                cp.start()
                sent.append(cp)
        for cp in sent:
            cp.wait_recv()
        for cp in sent:
            cp.wait_send()
        own_in.wait()
        own_out.wait()

        def update(land, w_ref, m_ref, v_ref, g_ref, d_ref, nm_ref, nv_ref, n_rows):
            def step(b, carry):
                rows = pl.ds(pl.multiple_of(b * rb, rb), rb)
                g = land[0, rows, :]
                for rel in range(1, N_DEV):
                    g = g + land[rel, rows, :]
                delta, nm, nv = _adamw(w_ref[rows, :], g, m_ref[rows, :], v_ref[rows, :])
                g_ref[rows, :] = g
                d_ref[rows, :] = delta
                nm_ref[rows, :] = nm
                nv_ref[rows, :] = nv
                return carry
            lax.fori_loop(0, n_rows // rb, step, 0)

        update(land_in, win_ref, min_ref, vin_ref, gin_ref, din_ref, nmin_ref, nvin_ref, D_MODEL)
        update(land_out, wout_ref, mout_ref, vout_ref, gout_ref, dout_ref, nmout_ref, nvout_ref, WOUT_ROWS)

        g = land_s[my_flat ^ 0]
        for dev in range(1, N_DEV):
            g = g + land_s[my_flat ^ dev]
        delta, nm, nv = _adamw(ws_ref[...], g, ms_ref[...], vs_ref[...])
        gs_ref[...] = g
        ds_ref[...] = delta
        nms_ref[...] = nm
        nvs_ref[...] = nv

    vm = lambda: pl.BlockSpec(memory_space=pltpu.VMEM)
    anyspace = lambda: pl.BlockSpec(memory_space=pl.ANY)
    big = jax.ShapeDtypeStruct((D_MODEL, COL_BLOCK), F32)
    flat = jax.ShapeDtypeStruct((WOUT_ROWS, D_MODEL), F32)
    small = jax.ShapeDtypeStruct((SMALL_ROWS, LANES), F32)
    return pl.pallas_call(
        body, name="exchange_update",
        out_shape=tuple([big] * 4 + [flat] * 4 + [small] * 4),
        in_specs=[anyspace(), anyspace()] + [vm() for _ in range(10)],
        out_specs=tuple(vm() for _ in range(12)),
        scratch_shapes=[pltpu.VMEM((N_DEV, D_MODEL, COL_BLOCK), F32),
                        pltpu.VMEM((N_DEV, WOUT_ROWS, D_MODEL), F32),
                        pltpu.VMEM((N_DEV, SMALL_ROWS, LANES), F32),
                        pltpu.SemaphoreType.DMA((21,)), pltpu.SemaphoreType.DMA((21,)),
                        pltpu.SemaphoreType.DMA((2,))],
        compiler_params=_params(),
    )(dwin_p, dwout_p, small_p, w_in, m_in, v_in, w_out, m_out, v_out, w_s, m_s, v_s)


def _pack_small(mix, attn, hgrn, lb, final, loss=None):
    def rows8(a):
        a = a.reshape(-1, LANES)
        return jnp.pad(a, ((0, 8 - a.shape[0]), (0, 0)))
    last = jnp.zeros((8, LANES), F32) if loss is None else jnp.pad(loss.reshape(1, 1), ((0, 7), (0, LANES - 1)))
    return jnp.concatenate([rows8(mix), rows8(attn), rows8(hgrn), rows8(lb), rows8(final), last], axis=0)


def _unpack_small(slab):
    return (slab[ROW_MIX:ROW_MIX + 8].reshape(1, D_MODEL), slab[ROW_ATTN:ROW_ATTN + 4].reshape(1, ATTN_WIDTH),
            slab[ROW_HGRN:ROW_HGRN + 4].reshape(1, HGRN_WIDTH), slab[ROW_LB:ROW_LB + 8].reshape(2, HGRN_WIDTH),
            slab[ROW_FINAL:ROW_FINAL + 8].reshape(D_MODEL))


def _local_step(x, pos_col, w_in_g, w_out_g, mix_w, attn_w, hgrn_w, lb_raw, final_w, target):
    inv = ROPE_THETA ** (-jnp.arange(ROPE_HALF, dtype=F32) * (2.0 / ROPE_DIMS))
    lane_e = jnp.arange(LANES) % HEAD_DIM
    inv_lanes = jnp.where(lane_e < ROPE_DIMS, inv[lane_e % ROPE_HALF], 0.0).reshape(1, LANES)
    rc, rsa, rsb = _rope_tables(pos_col, inv_lanes)

    proj, hn_t = _in_proj_fwd(x, mix_w, w_in_g, rc, rsa, rsb)
    attn_o, lse = _attn_fwd_fused(proj)
    rec, states = _hgrn_fwd(proj, lb_raw)

    (dx2, d_o, delta, d_ag, d_rec, d_hg, dwout_p, d_final, d_attn_w, d_hgrn_w, loss) = _mid(
        attn_o, rec, proj, x, target, w_out_g, attn_w, hgrn_w, final_w.reshape(1, D_MODEL))

    dqkv = _attn_bwd_fused(proj, d_o, lse, delta)
    d_hq, d_hf, d_hi, d_lb = _hgrn_bwd(proj, lb_raw, d_rec, states)

    grad_x, dwin_p, d_mix = _in_proj_bwd(
        (dqkv[0], dqkv[1], dqkv[2], d_ag, d_hq, d_hf, d_hi, d_hg), hn_t, w_in_g, x, dx2, mix_w, rc, rsa, rsb)
    small_p = _pack_small(d_mix, d_attn_w, d_hgrn_w, d_lb, d_final, loss)
    return grad_x, dwin_p, dwout_p, small_p


def kernel(x, positions, w_in, w_out, mix_norm_w, attn_out_norm_w, hgrn_out_norm_w, hgrn_lb_raw, final_norm_w, loss_target, m_w_in, m_w_out, m_mix_norm_w, m_attn_out_norm_w, m_hgrn_out_norm_w, m_hgrn_lb_raw, m_final_norm_w, v_w_in, v_w_out, v_mix_norm_w, v_attn_out_norm_w, v_hgrn_out_norm_w, v_hgrn_lb_raw, v_final_norm_w):
    w_in_g, w_out_g = _gather_weights(w_in[0], w_out[0])
    grad_x, dwin_p, dwout_p, small_p = _local_step(
        x[0], positions.reshape(SEQ, 1), w_in_g, w_out_g, mix_norm_w, attn_out_norm_w, hgrn_out_norm_w,
        hgrn_lb_raw, final_norm_w, loss_target[0])

    w_s = _pack_small(mix_norm_w, attn_out_norm_w, hgrn_out_norm_w, hgrn_lb_raw, final_norm_w)
    m_s = _pack_small(m_mix_norm_w, m_attn_out_norm_w, m_hgrn_out_norm_w, m_hgrn_lb_raw, m_final_norm_w)
    v_s = _pack_small(v_mix_norm_w, v_attn_out_norm_w, v_hgrn_out_norm_w, v_hgrn_lb_raw, v_final_norm_w)
    (g_in, d_in, nm_in, nv_in, g_out, d_out, nm_out, nv_out, g_s, d_s, nm_s, nv_s) = _exchange_update(
        dwin_p, dwout_p, small_p, w_in[0], m_w_in[0], v_w_in[0], w_out[0], m_w_out[0], v_w_out[0], w_s, m_s, v_s)

    loss = g_s[ROW_LOSS, 0]
    return (loss, grad_x[None], g_in[None], g_out[None], *_unpack_small(g_s),
            d_in[None], d_out[None], *_unpack_small(d_s),
            nm_in[None], nm_out[None], *_unpack_small(nm_s),
            nv_in[None], nv_out[None], *_unpack_small(nv_s))
```

```python
import functools

import jax
import jax.numpy as jnp
from jax import lax
from jax.experimental import pallas as pl
from jax.experimental.pallas import tpu as pltpu

F32 = jnp.float32
BF16 = jnp.bfloat16

SEQ = 4096
D_MODEL = 1024
ATTN_WIDTH = 512
HGRN_WIDTH = 512
HEAD_DIM = 64
HGRN_HEADS = 4
HGRN_DIM = 128
HGRN_CHUNK = 64
N_CHUNKS = SEQ // HGRN_CHUNK
IN_COLS = 4096
COL_BLOCK = 512
N_DEV = 8
WOUT_ROWS = D_MODEL // N_DEV
ATTN_BLOCK = 128
DILATIONS = (1, 4, 16)
ROPE_THETA = 500000.0
ROPE_DIMS = 16
ROPE_HALF = 8
NORM_EPS = 1e-6
NEG_BIG = -1e30
LANES = 128

ADAM_LR = 0.001
ADAM_B1 = 0.9
ADAM_B2 = 0.999
ADAM_EPS = 1e-08
ADAM_WD = 0.01
ADAM_STEP = 10

SMALL_ROWS = 48
ROW_MIX, ROW_ATTN, ROW_HGRN, ROW_LB, ROW_FINAL, ROW_LOSS = 0, 8, 16, 24, 32, 40

VMEM_LIMIT = 56 * 1024 * 1024
MESH = pl.DeviceIdType.MESH


def _mm(a, b):
    return lax.dot_general(a, b, (((1,), (0,)), ((), ())), preferred_element_type=F32)


def _mm_nt(a, b):
    return lax.dot_general(a, b, (((1,), (1,)), ((), ())), preferred_element_type=F32)


def _mm_tn(a, b):
    return lax.dot_general(a, b, (((0,), (0,)), ((), ())), preferred_element_type=F32)


def _mm_exact(a, b):
    return lax.dot_general(a, b, (((1,), (0,)), ((), ())), preferred_element_type=F32,
                           precision=lax.Precision.HIGHEST)


def _sigmoid(v):
    return 1.0 / (1.0 + jnp.exp(-v))


def _params(sem=None, **kw):
    return pltpu.CompilerParams(dimension_semantics=sem, vmem_limit_bytes=VMEM_LIMIT, **kw)


def _my_place():
    return lax.axis_index("x"), lax.axis_index("y"), lax.axis_index("c")


def _peer(place, rel):
    x, y, c = place
    return (x ^ ((rel >> 2) & 1), y ^ ((rel >> 1) & 1), c ^ (rel & 1))


def _flat(place):
    x, y, c = place
    return 4 * x + 2 * y + c


def _gather_weights(w_in, w_out):
    def body(win_ref, wout_ref, gin_ref, gout_ref, send_sems, recv_sems):
        me = _my_place()
        x, y, c = me
        sibling = (x, y, 1 - c)
        chips = [(1 - x, y), (x, 1 - y), (1 - x, 1 - y)]

        def slab(which, place):
            idx = _flat(place)
            if which == 0:
                return gin_ref.at[idx]
            return gout_ref.at[pl.ds(pl.multiple_of(idx * WOUT_ROWS, WOUT_ROWS), WOUT_ROWS), :]

        def copy(which, k, block, to):
            ref = slab(which, block)
            return pltpu.make_async_remote_copy(
                src_ref=ref, dst_ref=ref, send_sem=send_sems.at[7 * which + k],
                recv_sem=recv_sems.at[7 * which + k], device_id=to, device_id_type=MESH)

        gin_ref[_flat(me)] = win_ref[...].astype(BF16)
        gout_ref[pl.ds(pl.multiple_of(_flat(me) * WOUT_ROWS, WOUT_ROWS), WOUT_ROWS), :] = (
            wout_ref[...].astype(BF16))

        started = []
        for which in (0, 1):
            first = [copy(which, 0, me, sibling)]
            first += [copy(which, 1 + j, me, (*chip, c)) for j, chip in enumerate(chips)]
            for cp in first:
                cp.start()
            started += first
        for which in (0, 1):
            for j, chip in enumerate(chips):
                copy(which, 1 + j, (*chip, c), me).wait_recv()
                fwd = copy(which, 4 + j, (*chip, c), sibling)
                fwd.start()
                started.append(fwd)
        for which in (0, 1):
            copy(which, 0, sibling, me).wait_recv()
            for j, chip in enumerate(chips):
                copy(which, 4 + j, (*chip, 1 - c), me).wait_recv()
        for cp in started:
            cp.wait_send()

    return pl.pallas_call(
        body, name="gather_weights",
        out_shape=(jax.ShapeDtypeStruct((N_DEV, D_MODEL, COL_BLOCK), BF16),
                   jax.ShapeDtypeStruct((D_MODEL, D_MODEL), BF16)),
        in_specs=[pl.BlockSpec(memory_space=pltpu.VMEM), pl.BlockSpec(memory_space=pltpu.VMEM)],
        out_specs=(pl.BlockSpec(memory_space=pltpu.VMEM), pl.BlockSpec(memory_space=pltpu.VMEM)),
        scratch_shapes=[pltpu.SemaphoreType.DMA((14,)), pltpu.SemaphoreType.DMA((14,))],
        compiler_params=_params(),
    )(w_in, w_out)


def _rope_tables(pos_col, inv_freq_lanes):
    tm = 512

    def body(pos_ref, invf_ref, c_ref, sa_ref, sb_ref):
        ang = pos_ref[...].astype(F32) * invf_ref[...]
        e = lax.broadcasted_iota(jnp.int32, (tm, LANES), 1) & (HEAD_DIM - 1)
        cos, sin = jnp.cos(ang), jnp.sin(ang)
        c_ref[...] = jnp.where(e < ROPE_DIMS, cos, 1.0)
        sa_ref[...] = jnp.where((e >= ROPE_HALF) & (e < ROPE_DIMS), sin, 0.0)
        sb_ref[...] = jnp.where(e < ROPE_HALF, -sin, 0.0)

    tab = jax.ShapeDtypeStruct((SEQ, LANES), F32)
    spec = pl.BlockSpec((tm, LANES), lambda i: (i, 0))
    return pl.pallas_call(
        body, name="rope_tables", grid=(SEQ // tm,), out_shape=(tab, tab, tab),
        in_specs=[pl.BlockSpec((tm, 1), lambda i: (i, 0)), pl.BlockSpec((1, LANES), lambda i: (0, 0))],
        out_specs=(spec, spec, spec), compiler_params=_params(("parallel",)),
    )(pos_col, inv_freq_lanes)


def _rot(t, c, sa, sb):
    n = t.shape[1]
    return t * c + pltpu.roll(t, ROPE_HALF, 1) * sa + pltpu.roll(t, n - ROPE_HALF, 1) * sb


def _rot_transposed(g, c, sa, sb):
    n = g.shape[1]
    return g * c + pltpu.roll(g * sa, n - ROPE_HALF, 1) + pltpu.roll(g * sb, ROPE_HALF, 1)


def _in_proj_fwd(x, mix_w, w_g, rc, rsa, rsb):
    tm = 256

    def body(x_ref, w_ref, wg_ref, c_ref, sa_ref, sb_ref, proj_ref, hnt_ref):
        xf = x_ref[...]
        ms = jnp.mean(xf * xf, axis=-1, keepdims=True)
        hn = xf * lax.rsqrt(ms + NORM_EPS) * w_ref[...]
        hnt_ref[...] = hn.T.astype(BF16)
        hb = hn.astype(BF16)
        c = jnp.tile(c_ref[...], (1, 4))
        sa = jnp.tile(sa_ref[...], (1, 4))
        sb = jnp.tile(sb_ref[...], (1, 4))
        for j in range(N_DEV):
            acc = _mm(hb, wg_ref[j])
            if j < 2:
                acc = _rot(acc, c, sa, sb)
            proj_ref[:, COL_BLOCK * j:COL_BLOCK * (j + 1)] = acc

    tab = pl.BlockSpec((tm, LANES), lambda i: (i, 0))
    return pl.pallas_call(
        body, name="in_proj_fwd", grid=(SEQ // tm,),
        out_shape=(jax.ShapeDtypeStruct((SEQ, IN_COLS), F32),
                   jax.ShapeDtypeStruct((D_MODEL, SEQ), BF16)),
        in_specs=[pl.BlockSpec((tm, D_MODEL), lambda i: (i, 0)),
                  pl.BlockSpec((1, D_MODEL), lambda i: (0, 0)),
                  pl.BlockSpec((N_DEV, D_MODEL, COL_BLOCK), lambda i: (0, 0, 0)),
                  tab, tab, tab],
        out_specs=(pl.BlockSpec((tm, IN_COLS), lambda i: (i, 0)),
                   pl.BlockSpec((D_MODEL, tm), lambda i: (0, i))),
        compiler_params=_params(("parallel",)),
    )(x, mix_w, w_g, rc, rsa, rsb)


ATTN_GROUP = 4
BLOCKS_PER_PATTERN = SEQ // ATTN_BLOCK


def _write_band_bias(bias_ref):
    qi = lax.broadcasted_iota(jnp.int32, (ATTN_BLOCK, 2 * ATTN_BLOCK), 0)
    kj = lax.broadcasted_iota(jnp.int32, (ATTN_BLOCK, 2 * ATTN_BLOCK), 1)
    bias_ref[0] = jnp.where((kj >= qi) & (kj <= qi + ATTN_BLOCK), 0.0, NEG_BIG)
    bias_ref[1] = jnp.where(kj <= qi, 0.0, NEG_BIG)


def _head0_lanes():
    return lax.broadcasted_iota(jnp.int32, (ATTN_BLOCK, LANES), 1) < HEAD_DIM


def _strided(start, size, d):
    return pl.ds(start, size) if d == 1 else pl.ds(start, size, stride=d)


def _block_place(i, d):
    nblk = BLOCKS_PER_PATTERN // d
    r, n = i // nblk, i % nblk
    kn = jnp.maximum(n - 1, 0)
    row0, key0 = n * (d * ATTN_BLOCK) + r, kn * (d * ATTN_BLOCK) + r
    if d == 1:
        row0, key0 = pl.multiple_of(row0, ATTN_BLOCK), pl.multiple_of(key0, ATTN_BLOCK)
    return _strided(row0, ATTN_BLOCK, d), _strided(key0, 2 * ATTN_BLOCK, d), (n == 0).astype(jnp.int32)


def _for_each_group(d, load, compute, store):
    def group(g, carry):
        items = [load(*_block_place(g * ATTN_GROUP + u, d)) for u in range(ATTN_GROUP)]
        results = [compute(item) for item in items]
        for item, res in zip(items, results):
            store(item, res)
        return carry

    lax.fori_loop(0, BLOCKS_PER_PATTERN // ATTN_GROUP, group, 0)


def _attn_fwd_fused(proj):
    n_pat = len(DILATIONS)
    tile = (ATTN_BLOCK, LANES)

    def body(q_ref, k_ref, v_ref, o_ref, lse_ref, m_acc, l_acc, bias_ref):
        _write_band_bias(bias_ref)
        h0 = _head0_lanes()
        for pi, d in enumerate(DILATIONS):
            first, last = pi == 0, pi == n_pat - 1

            def load(rows, keys, which, first=first):
                item = dict(rows=rows, keys=keys, which=which)
                if not first:
                    item.update(o=o_ref[rows, :], m=[m_acc.at[h][rows, :] for h in range(2)],
                                l=[l_acc.at[h][rows, :] for h in range(2)])
                return item

            def compute(item, first=first):
                res = []
                q = q_ref[item["rows"], :]
                kb = k_ref[item["keys"], :].astype(BF16)
                vb = v_ref[item["keys"], :].astype(BF16)
                for h in range(2):
                    hm = h0 if h == 0 else jnp.logical_not(h0)
                    qh = jnp.where(hm, q, 0.0).astype(BF16)
                    s = _mm_nt(qh, kb) * 0.125 + bias_ref[item["which"]]
                    mb = jnp.max(s, axis=-1, keepdims=True)
                    if first:
                        p = jnp.exp(s - mb)
                        mn = jnp.broadcast_to(mb, tile)
                    else:
                        mn = jnp.maximum(item["m"][h], mb)
                        alpha = jnp.exp(item["m"][h] - mn)
                        p = jnp.exp(s - jnp.concatenate([mn, mn], axis=1))
                    ls = jnp.sum(p, axis=-1, keepdims=True)
                    pv = _mm(p.astype(BF16), vb)
                    if first:
                        res.append((pv, mn, jnp.broadcast_to(ls, tile)))
                    else:
                        res.append((alpha * item["o"] + pv, mn, alpha * item["l"][h] + ls))
                return res

            def store(item, res, last=last):
                rows = item["rows"]
                (o0, m0, l0), (o1, m1, l1) = res
                if last:
                    o_ref[rows, :] = jnp.where(h0, o0 / l0, o1 / l1)
                    lse_ref[rows, :] = jnp.where(h0, m0 + jnp.log(l0), m1 + jnp.log(l1))
                else:
                    o_ref[rows, :] = jnp.where(h0, o0, o1)
                    m_acc.at[0][rows, :], m_acc.at[1][rows, :] = m0, m1
                    l_acc.at[0][rows, :], l_acc.at[1][rows, :] = l0, l1

            _for_each_group(d, load, compute, store)

    slab = lambda g: pl.BlockSpec((SEQ, LANES), functools.partial(lambda hp, g: (0, 4 * g + hp), g=g))
    wide = jax.ShapeDtypeStruct((SEQ, ATTN_WIDTH), F32)
    return pl.pallas_call(
        body, name="attn_fwd", grid=(4,), out_shape=(wide, wide),
        in_specs=[slab(0), slab(1), slab(2)], out_specs=(slab(0), slab(0)),
        scratch_shapes=[pltpu.VMEM((2, SEQ, LANES), F32), pltpu.VMEM((2, SEQ, LANES), F32),
                        pltpu.VMEM((2, ATTN_BLOCK, 2 * ATTN_BLOCK), F32)],
        compiler_params=_params(("parallel",)),
    )(proj, proj, proj)


def _attn_bwd_fused(proj, d_out, lse, delta):
    def body(q_ref, k_ref, v_ref, do_ref, lse_ref, del_ref, dq_ref, dk_ref, dv_ref, bias_ref):
        _write_band_bias(bias_ref)
        dk_ref[...] = jnp.zeros_like(dk_ref)
        dv_ref[...] = jnp.zeros_like(dv_ref)
        h0 = _head0_lanes()
        for pi, d in enumerate(DILATIONS):
            first = pi == 0

            def load(rows, keys, which):
                return dict(rows=rows, keys=keys, q=q_ref[rows, :], g=do_ref[rows, :], lse=lse_ref[rows, :],
                            delta=del_ref[rows, :], k=k_ref[keys, :].astype(BF16),
                            v=v_ref[keys, :].astype(BF16), bias=bias_ref[which])

            def per_head(t):
                swapped = pltpu.roll(t, HEAD_DIM, 1)
                a, b = jnp.where(h0, t, swapped), jnp.where(h0, swapped, t)
                return jnp.concatenate([a, a], axis=1), jnp.concatenate([b, b], axis=1)

            def compute(item):
                dq_h = []
                dk_c = jnp.zeros((2 * ATTN_BLOCK, LANES), F32)
                dv_c = jnp.zeros((2 * ATTN_BLOCK, LANES), F32)
                lse_h, delta_h = per_head(item["lse"]), per_head(item["delta"])
                for h in range(2):
                    hm = h0 if h == 0 else jnp.logical_not(h0)
                    qh = jnp.where(hm, item["q"], 0.0).astype(BF16)
                    gh = jnp.where(hm, item["g"], 0.0).astype(BF16)
                    s = _mm_nt(qh, item["k"]) * 0.125 + item["bias"]
                    p = jnp.exp(s - lse_h[h])
                    dp = _mm_nt(gh, item["v"])
                    ds = (p * (dp - delta_h[h]) * 0.125).astype(BF16)
                    dq_h.append(_mm(ds, item["k"]))
                    dk_c = dk_c + _mm_tn(ds, qh)
                    dv_c = dv_c + _mm_tn(p.astype(BF16), gh)
                return jnp.where(h0, dq_h[0], dq_h[1]), dk_c, dv_c

            def store(item, res, first=first):
                rows, keys = item["rows"], item["keys"]
                if first:
                    dq_ref[rows, :] = res[0]
                else:
                    dq_ref[rows, :] += res[0]
                dk_ref[keys, :] += res[1]
                dv_ref[keys, :] += res[2]

            _for_each_group(d, load, compute, store)

    slab = lambda g: pl.BlockSpec((SEQ, LANES), functools.partial(lambda hp, g: (0, 4 * g + hp), g=g))
    wide = jax.ShapeDtypeStruct((SEQ, ATTN_WIDTH), F32)
    return pl.pallas_call(
        body, name="attn_bwd", grid=(4,), out_shape=(wide, wide, wide),
        scratch_shapes=[pltpu.VMEM((2, ATTN_BLOCK, 2 * ATTN_BLOCK), F32)],
        in_specs=[slab(0), slab(1), slab(2), slab(0), slab(0), slab(0)], out_specs=(slab(0), slab(0), slab(0)),
        compiler_params=_params(("parallel",)),
    )(proj, proj, proj, d_out, lse, delta)


def _hgrn_lower_bound(lb_ref):
    r0, r1 = lb_ref[0:1, :], lb_ref[1:2, :]
    mx = jnp.maximum(r0, r1)
    e0, e1 = jnp.exp(r0 - mx), jnp.exp(r1 - mx)
    return e0 / (e0 + e1)


def _hgrn_gates(hq, hf, lb):
    sq = _sigmoid(hq)
    sg = _sigmoid(hf)
    f = lb + (1.0 - lb) * sg
    return hq * sq, sq, sg, f, 1.0 - f, jnp.log(f)


def _hgrn_specs():
    cols = lambda g: pl.BlockSpec((SEQ, HGRN_DIM), functools.partial(lambda h, g: (0, 4 * g + h), g=g))
    head = pl.BlockSpec((SEQ, HGRN_DIM), lambda h: (0, h))
    lb = pl.BlockSpec((2, HGRN_DIM), lambda h: (0, h))
    states = pl.BlockSpec((1, N_CHUNKS, HGRN_DIM, HGRN_DIM), lambda h: (h, 0, 0, 0))
    return cols, head, lb, states


def _hgrn_fwd(proj, lb_raw):
    t = HGRN_CHUNK

    def body(hq_ref, hf_ref, hi_ref, lb_ref, rec_ref, st_ref, state):
        lb = _hgrn_lower_bound(lb_ref)
        ri = lax.broadcasted_iota(jnp.int32, (t, t), 0)
        ci = lax.broadcasted_iota(jnp.int32, (t, t), 1)
        causal = ri >= ci
        tril = causal.astype(F32)
        state[...] = jnp.zeros_like(state)

        def chunk(n, carry):
            rows = pl.ds(pl.multiple_of(n * t, t), t)
            q, _, _, _, k, lf = _hgrn_gates(hq_ref[rows, :], hf_ref[rows, :], lb)
            cum = _mm_exact(tril, lf)
            last = cum[t - 1:t, :]
            qd = (q * jnp.exp(cum)).astype(BF16)
            ki = (k * jnp.exp(-cum)).astype(BF16)
            ke = (k * jnp.exp(last - cum)).astype(BF16)
            vb = hi_ref[rows, :].astype(BF16)
            att = jnp.where(causal, _mm_nt(qd, ki), 0.0)
            st = state[...]
            st_ref[0, n] = st
            rec_ref[rows, :] = _mm(att.astype(BF16), vb) + _mm_nt(qd, st.astype(BF16))
            state[...] = st * jnp.exp(last) + _mm_tn(vb, ke)
            return carry

        lax.fori_loop(0, N_CHUNKS, chunk, 0)

    cols, head, lb, states = _hgrn_specs()
    return pl.pallas_call(
        body, name="hgrn_fwd", grid=(HGRN_HEADS,),
        out_shape=(jax.ShapeDtypeStruct((SEQ, HGRN_WIDTH), F32),
                   jax.ShapeDtypeStruct((HGRN_HEADS, N_CHUNKS, HGRN_DIM, HGRN_DIM), F32)),
        in_specs=[cols(4), cols(5), cols(6), lb], out_specs=(head, states),
        scratch_shapes=[pltpu.VMEM((HGRN_DIM, HGRN_DIM), F32)],
        compiler_params=_params(("parallel",)),
    )(proj, proj, proj, lb_raw)


def _hgrn_bwd(proj, lb_raw, d_rec, states):
    t = HGRN_CHUNK

    def body(hq_ref, hf_ref, hi_ref, lb_ref, do_ref, st_ref, dhq_ref, dhf_ref, dhi_ref, dlb_ref,
             dstate, dlb_acc):
        lb = _hgrn_lower_bound(lb_ref)
        ri = lax.broadcasted_iota(jnp.int32, (t, t), 0)
        ci = lax.broadcasted_iota(jnp.int32, (t, t), 1)
        causal = ri >= ci
        tril = causal.astype(F32)
        triu = (ri <= ci).astype(F32)
        last_row = lax.broadcasted_iota(jnp.int32, (t, HGRN_DIM), 0) == t - 1
        dstate[...] = jnp.zeros_like(dstate)
        dlb_acc[...] = jnp.zeros_like(dlb_acc)

        def chunk(i, carry):
            n = N_CHUNKS - 1 - i
            rows = pl.ds(pl.multiple_of(n * t, t), t)
            hq = hq_ref[rows, :]
            q, sq, sg, f, k, lf = _hgrn_gates(hq, hf_ref[rows, :], lb)
            cum = _mm_exact(tril, lf)
            last = cum[t - 1:t, :]
            e_cum, e_inv, e_end, dec = jnp.exp(cum), jnp.exp(-cum), jnp.exp(last - cum), jnp.exp(last)
            qd, ki, ke = q * e_cum, k * e_inv, k * e_end
            qdb, kib, keb = qd.astype(BF16), ki.astype(BF16), ke.astype(BF16)
            vb = hi_ref[rows, :].astype(BF16)
            gb = do_ref[rows, :].astype(BF16)
            st_prev = st_ref[0, n]
            dst = dstate[...]
            dstb = dst.astype(BF16)

            att = jnp.where(causal, _mm_nt(qdb, kib), 0.0).astype(BF16)
            datt = jnp.where(causal, _mm_nt(gb, vb), 0.0).astype(BF16)
            dv = _mm_tn(att, gb) + _mm_nt(keb, dstb)
            dqd = _mm(datt, kib) + _mm(gb, st_prev.astype(BF16))
            dki = _mm_tn(datt, qdb)
            dke = _mm(vb, dstb)
            ddec = jnp.sum(dst * st_prev, axis=0, keepdims=True)
            dstate[...] = dst * dec + _mm_tn(gb, qdb)

            dq = dqd * e_cum
            dk = dki * e_inv + dke * e_end
            dlast = jnp.sum(dke * ke, axis=0, keepdims=True) + ddec * dec
            dcum = dqd * qd - dki * ki - dke * ke + jnp.where(last_row, dlast, 0.0)
            dlf = _mm_exact(triu, dcum)
            df = dlf / f - dk
            dhq_ref[rows, :] = dq * (sq * (1.0 + hq * (1.0 - sq)))
            dhf_ref[rows, :] = df * (1.0 - lb) * (sg * (1.0 - sg))
            dhi_ref[rows, :] = dv
            dlb_acc[...] += jnp.sum(df * (1.0 - sg), axis=0, keepdims=True)
            return carry

        lax.fori_loop(0, N_CHUNKS, chunk, 0)
        g0 = dlb_acc[...] * lb * (1.0 - lb)
        dlb_ref[...] = jnp.concatenate([g0, -g0], axis=0)

    cols, head, lb_spec, st_spec = _hgrn_specs()
    wide = jax.ShapeDtypeStruct((SEQ, HGRN_WIDTH), F32)
    return pl.pallas_call(
        body, name="hgrn_bwd", grid=(HGRN_HEADS,),
        out_shape=(wide, wide, wide, jax.ShapeDtypeStruct((2, HGRN_WIDTH), F32)),
        in_specs=[cols(4), cols(5), cols(6), lb_spec, head, st_spec],
        out_specs=(head, head, head, lb_spec),
        scratch_shapes=[pltpu.VMEM((HGRN_DIM, HGRN_DIM), F32), pltpu.VMEM((1, HGRN_DIM), F32)],
        compiler_params=_params(("parallel",)),
    )(proj, proj, proj, lb_raw, d_rec, states)


def _group_sum(v, group):
    parts = []
    for s in range(v.shape[1] // LANES):
        slab = v[:, LANES * s:LANES * (s + 1)]
        if group == LANES:
            parts.append(jnp.broadcast_to(jnp.sum(slab, axis=-1, keepdims=True), slab.shape))
        else:
            h0 = lax.broadcasted_iota(jnp.int32, slab.shape, 1) < HEAD_DIM
            s0 = jnp.sum(jnp.where(h0, slab, 0.0), axis=-1, keepdims=True)
            s1 = jnp.sum(jnp.where(h0, 0.0, slab), axis=-1, keepdims=True)
            parts.append(jnp.where(h0, s0, s1))
    return jnp.concatenate(parts, axis=1)


def _mid(attn_o, rec, proj, x, target, w_out_g, attn_w, hgrn_w, final_w):
    tm = 256

    def branch_fwd(o, gate, w, group):
        r = lax.rsqrt(_group_sum(o * o, group) * (1.0 / group) + NORM_EPS)
        nrm = o * r
        sg = _sigmoid(gate)
        return r, nrm, sg, nrm * w * (gate * sg)

    def branch_bwd(dy, r, nrm, sg, gate, w, group):
        silu = gate * sg
        d_gate = dy * nrm * w * (sg * (1.0 + gate * (1.0 - sg)))
        d_w = jnp.sum(dy * nrm * silu, axis=0, keepdims=True)
        dn = dy * w * silu
        d_o = r * (dn - nrm * (_group_sum(dn * nrm, group) * (1.0 / group)))
        return d_o, d_gate, d_w

    def body(o_ref, rec_ref, ag_ref, hg_ref, x_ref, tgt_ref, wout_ref, aw_ref, hw_ref, fw_ref,
             dx2_ref, do_ref, delta_ref, dag_ref, drec_ref, dhg_ref, dwout_ref, dfw_ref, daw_ref, dhw_ref,
             loss_ref):
        i = pl.program_id(0)

        @pl.when(i == 0)
        def _():
            dwout_ref[...] = jnp.zeros_like(dwout_ref)
            dfw_ref[...] = jnp.zeros_like(dfw_ref)
            daw_ref[...] = jnp.zeros_like(daw_ref)
            dhw_ref[...] = jnp.zeros_like(dhw_ref)
            loss_ref[...] = jnp.zeros_like(loss_ref)

        o, rc, ag, hg = o_ref[...], rec_ref[...], ag_ref[...], hg_ref[...]
        aw, hw, fw = aw_ref[...], hw_ref[...], fw_ref[...]
        ra, na, sga, ya = branch_fwd(o, ag, aw, HEAD_DIM)
        rh, nh, sgh, yh = branch_fwd(rc, hg, hw, HGRN_DIM)
        mixed = jnp.concatenate([ya, yh], axis=1).astype(BF16)
        wout = wout_ref[...]
        x2 = x_ref[...] + _mm(mixed, wout)
        rstd = lax.rsqrt(jnp.mean(x2 * x2, axis=-1, keepdims=True) + NORM_EPS)
        xn = x2 * rstd
        err = xn * fw - tgt_ref[...]
        row_loss = jnp.mean(err * err, axis=-1, keepdims=True)
        loss_ref[...] += 0.5 * jnp.sum(row_loss, axis=0, keepdims=True)
        dy = err * (1.0 / D_MODEL)
        dfw_ref[...] += jnp.sum(dy * xn, axis=0, keepdims=True)
        dxn = dy * fw
        dx2 = rstd * (dxn - xn * jnp.mean(dxn * xn, axis=-1, keepdims=True))
        dx2_ref[...] = dx2
        dx2b = dx2.astype(BF16)
        dwout_ref[...] += _mm_tn(mixed, dx2b)
        dmixed = _mm_nt(dx2b, wout)

        d_o, d_ag, d_aw = branch_bwd(dmixed[:, :ATTN_WIDTH], ra, na, sga, ag, aw, HEAD_DIM)
        d_rec, d_hg, d_hw = branch_bwd(dmixed[:, ATTN_WIDTH:], rh, nh, sgh, hg, hw, HGRN_DIM)
        do_ref[...] = d_o
        delta_ref[...] = _group_sum(d_o * o, HEAD_DIM)
        dag_ref[...] = d_ag
        drec_ref[...] = d_rec
        dhg_ref[...] = d_hg
        daw_ref[...] += d_aw
        dhw_ref[...] += d_hw

    half = lambda: pl.BlockSpec((tm, COL_BLOCK), lambda i: (i, 0))
    full = lambda: pl.BlockSpec((tm, D_MODEL), lambda i: (i, 0))
    fixed = lambda r, c: pl.BlockSpec((r, c), lambda i: (0, 0))
    wide = jax.ShapeDtypeStruct((SEQ, COL_BLOCK), F32)
    return pl.pallas_call(
        body, name="mid", grid=(SEQ // tm,),
        out_shape=(jax.ShapeDtypeStruct((SEQ, D_MODEL), F32), wide, wide, wide, wide, wide,
                   jax.ShapeDtypeStruct((D_MODEL, D_MODEL), F32),
                   jax.ShapeDtypeStruct((1, D_MODEL), F32), jax.ShapeDtypeStruct((1, COL_BLOCK), F32),
                   jax.ShapeDtypeStruct((1, COL_BLOCK), F32), jax.ShapeDtypeStruct((1, 1), F32)),
        in_specs=[half(), half(),
                  pl.BlockSpec((tm, COL_BLOCK), lambda i: (i, 3)), pl.BlockSpec((tm, COL_BLOCK), lambda i: (i, 7)),
                  full(), full(), fixed(D_MODEL, D_MODEL), fixed(1, COL_BLOCK), fixed(1, COL_BLOCK),
                  fixed(1, D_MODEL)],
        out_specs=(full(), half(), half(), half(), half(), half(), fixed(D_MODEL, D_MODEL),
                   fixed(1, D_MODEL), fixed(1, COL_BLOCK), fixed(1, COL_BLOCK), fixed(1, 1)),
        compiler_params=_params(("arbitrary",)),
    )(attn_o, rec, proj, proj, x, target, w_out_g, attn_w, hgrn_w, final_w)


def _in_proj_bwd(d_groups, hn_t, w_g, x, dx2, mix_w, rc, rsa, rsb):
    tm = 256
    n_tiles = SEQ // tm
    last_j = N_DEV - 1

    def body(*refs):
        dg_refs = refs[:N_DEV]
        hnt_ref, wg_ref, x_ref, dx2_ref, w_ref, c_ref, sa_ref, sb_ref = refs[N_DEV:N_DEV + 8]
        gx_ref, dwin_ref, dmw_ref, dhn = refs[N_DEV + 8:]
        j, i = pl.program_id(0), pl.program_id(1)
        rows = pl.ds(pl.multiple_of(i * tm, tm), tm)

        for jj in range(N_DEV):
            @pl.when(j == jj)
            def _(jj=jj):
                dp = dg_refs[jj][...]
                if jj < 2:
                    dp = _rot_transposed(dp, jnp.tile(c_ref[...], (1, 4)), jnp.tile(sa_ref[...], (1, 4)),
                                         jnp.tile(sb_ref[...], (1, 4)))
                dpb = dp.astype(BF16)
                contrib = _mm_nt(dpb, wg_ref[0])
                if jj == 0:
                    dhn[rows, :] = contrib
                else:
                    dhn[rows, :] += contrib
                dw = _mm(hnt_ref[...], dpb)

                @pl.when(i == 0)
                def _():
                    dwin_ref[0] = dw

                @pl.when(i > 0)
                def _():
                    dwin_ref[0] += dw

        @pl.when((j == last_j) & (i == 0))
        def _():
            dmw_ref[...] = jnp.zeros_like(dmw_ref)

        @pl.when(j == last_j)
        def _():
            xf = x_ref[...]
            w = w_ref[...]
            rstd = lax.rsqrt(jnp.mean(xf * xf, axis=-1, keepdims=True) + NORM_EPS)
            xn = xf * rstd
            g = dhn[rows, :]
            dmw_ref[...] += jnp.sum(g * xn, axis=0, keepdims=True)
            gw = g * w
            gx_ref[...] = dx2_ref[...] + rstd * (gw - xn * jnp.mean(gw * xn, axis=-1, keepdims=True))

    def group_spec(jj):
        return pl.BlockSpec((tm, COL_BLOCK), functools.partial(lambda j, i, jj: (jnp.where(j == jj, i, 0), 0), jj=jj))

    at_end = lambda cols: pl.BlockSpec((tm, cols), lambda j, i: (jnp.where(j == last_j, i, 0), 0))
    tab = lambda: pl.BlockSpec((tm, LANES), lambda j, i: (jnp.where(j < 2, i, 0), 0))
    return pl.pallas_call(
        body, name="in_proj_bwd", grid=(N_DEV, n_tiles),
        out_shape=(jax.ShapeDtypeStruct((SEQ, D_MODEL), F32),
                   jax.ShapeDtypeStruct((N_DEV, D_MODEL, COL_BLOCK), F32),
                   jax.ShapeDtypeStruct((1, D_MODEL), F32)),
        in_specs=[group_spec(jj) for jj in range(N_DEV)] + [
            pl.BlockSpec((D_MODEL, tm), lambda j, i: (0, i)),
            pl.BlockSpec((1, D_MODEL, COL_BLOCK), lambda j, i: (j, 0, 0)),
            at_end(D_MODEL), at_end(D_MODEL), pl.BlockSpec((1, D_MODEL), lambda j, i: (0, 0)),
            tab(), tab(), tab()],
        out_specs=(at_end(D_MODEL), pl.BlockSpec((1, D_MODEL, COL_BLOCK), lambda j, i: (j, 0, 0)),
                   pl.BlockSpec((1, D_MODEL), lambda j, i: (0, 0))),
        scratch_shapes=[pltpu.VMEM((SEQ, D_MODEL), F32)],
        compiler_params=_params(("arbitrary", "arbitrary")),
    )(*d_groups, hn_t, w_g, x, dx2, mix_w, rc, rsa, rsb)


def _adamw(w, g, m, v):
    m = ADAM_B1 * m + (1.0 - ADAM_B1) * g
    v = ADAM_B2 * v + (1.0 - ADAM_B2) * (g * g)
    m_hat = m / (1.0 - ADAM_B1 ** ADAM_STEP)
    v_hat = v / (1.0 - ADAM_B2 ** ADAM_STEP)
    delta = -ADAM_LR * (m_hat / (jnp.sqrt(v_hat) + ADAM_EPS) + ADAM_WD * w)
    return delta, m, v


def _exchange_update(dwin_p, dwout_p, small_p, w_in, m_in, v_in, w_out, m_out, v_out, w_s, m_s, v_s):
    rb = 128

    def body(dwin_hbm, dwout_hbm, small_ref, win_ref, min_ref, vin_ref, wout_ref, mout_ref, vout_ref,
             ws_ref, ms_ref, vs_ref,
             gin_ref, din_ref, nmin_ref, nvin_ref, gout_ref, dout_ref, nmout_ref, nvout_ref,
             gs_ref, ds_ref, nms_ref, nvs_ref,
             land_in, land_out, land_s, send_sems, recv_sems, local_sems):
        me = _my_place()
        my_flat = _flat(me)

        def out_rows(idx):
            return pl.ds(pl.multiple_of(idx * WOUT_ROWS, WOUT_ROWS), WOUT_ROWS)

        own_in = pltpu.make_async_copy(dwin_hbm.at[my_flat], land_in.at[0], local_sems.at[0])
        own_out = pltpu.make_async_copy(dwout_hbm.at[out_rows(my_flat), :], land_out.at[0], local_sems.at[1])
        own_in.start()
        own_out.start()
        land_s[0] = small_ref[...]

        def copies(rel):
            peer = _peer(me, rel)
            pf = _flat(peer)
            mk = lambda which, src, dst: pltpu.make_async_remote_copy(
                src_ref=src, dst_ref=dst, send_sem=send_sems.at[7 * which + rel - 1],
                recv_sem=recv_sems.at[7 * which + rel - 1], device_id=peer, device_id_type=MESH)
            return [mk(0, dwin_hbm.at[pf], land_in.at[rel]),
                    mk(1, dwout_hbm.at[out_rows(pf), :], land_out.at[rel]),
                    mk(2, small_ref, land_s.at[rel])]

        sent = []
        for rel in range(1, N_DEV):
            for cp in copies(rel):
                cp.start()
                sent.append(cp)
        for cp in sent:
            cp.wait_recv()
        for cp in sent:
            cp.wait_send()
        own_in.wait()
        own_out.wait()

        def update(land, w_ref, m_ref, v_ref, g_ref, d_ref, nm_ref, nv_ref, n_rows):
            def step(b, carry):
                rows = pl.ds(pl.multiple_of(b * rb, rb), rb)
                g = land[0, rows, :]
                for rel in range(1, N_DEV):
                    g = g + land[rel, rows, :]
                delta, nm, nv = _adamw(w_ref[rows, :], g, m_ref[rows, :], v_ref[rows, :])
                g_ref[rows, :] = g
                d_ref[rows, :] = delta
                nm_ref[rows, :] = nm
                nv_ref[rows, :] = nv
                return carry
            lax.fori_loop(0, n_rows // rb, step, 0)

        update(land_in, win_ref, min_ref, vin_ref, gin_ref, din_ref, nmin_ref, nvin_ref, D_MODEL)
        update(land_out, wout_ref, mout_ref, vout_ref, gout_ref, dout_ref, nmout_ref, nvout_ref, WOUT_ROWS)

        g = land_s[my_flat ^ 0]
        for dev in range(1, N_DEV):
            g = g + land_s[my_flat ^ dev]
        delta, nm, nv = _adamw(ws_ref[...], g, ms_ref[...], vs_ref[...])
        gs_ref[...] = g
        ds_ref[...] = delta
        nms_ref[...] = nm
        nvs_ref[...] = nv

    vm = lambda: pl.BlockSpec(memory_space=pltpu.VMEM)
    anyspace = lambda: pl.BlockSpec(memory_space=pl.ANY)
    big = jax.ShapeDtypeStruct((D_MODEL, COL_BLOCK), F32)
    flat = jax.ShapeDtypeStruct((WOUT_ROWS, D_MODEL), F32)
    small = jax.ShapeDtypeStruct((SMALL_ROWS, LANES), F32)
    return pl.pallas_call(
        body, name="exchange_update",
        out_shape=tuple([big] * 4 + [flat] * 4 + [small] * 4),
        in_specs=[anyspace(), anyspace()] + [vm() for _ in range(10)],
        out_specs=tuple(vm() for _ in range(12)),
        scratch_shapes=[pltpu.VMEM((N_DEV, D_MODEL, COL_BLOCK), F32),
                        pltpu.VMEM((N_DEV, WOUT_ROWS, D_MODEL), F32),
                        pltpu.VMEM((N_DEV, SMALL_ROWS, LANES), F32),
                        pltpu.SemaphoreType.DMA((21,)), pltpu.SemaphoreType.DMA((21,)),
                        pltpu.SemaphoreType.DMA((2,))],
        compiler_params=_params(),
    )(dwin_p, dwout_p, small_p, w_in, m_in, v_in, w_out, m_out, v_out, w_s, m_s, v_s)


def _pack_small(mix, attn, hgrn, lb, final, loss=None):
    def rows8(a):
        a = a.reshape(-1, LANES)
        return jnp.pad(a, ((0, 8 - a.shape[0]), (0, 0)))
    last = jnp.zeros((8, LANES), F32) if loss is None else jnp.pad(loss.reshape(1, 1), ((0, 7), (0, LANES - 1)))
    return jnp.concatenate([rows8(mix), rows8(attn), rows8(hgrn), rows8(lb), rows8(final), last], axis=0)


def _unpack_small(slab):
    return (slab[ROW_MIX:ROW_MIX + 8].reshape(1, D_MODEL), slab[ROW_ATTN:ROW_ATTN + 4].reshape(1, ATTN_WIDTH),
            slab[ROW_HGRN:ROW_HGRN + 4].reshape(1, HGRN_WIDTH), slab[ROW_LB:ROW_LB + 8].reshape(2, HGRN_WIDTH),
            slab[ROW_FINAL:ROW_FINAL + 8].reshape(D_MODEL))


def _local_step(x, pos_col, w_in_g, w_out_g, mix_w, attn_w, hgrn_w, lb_raw, final_w, target):
    inv = ROPE_THETA ** (-jnp.arange(ROPE_HALF, dtype=F32) * (2.0 / ROPE_DIMS))
    lane_e = jnp.arange(LANES) % HEAD_DIM
    inv_lanes = jnp.where(lane_e < ROPE_DIMS, inv[lane_e % ROPE_HALF], 0.0).reshape(1, LANES)
    rc, rsa, rsb = _rope_tables(pos_col, inv_lanes)

    proj, hn_t = _in_proj_fwd(x, mix_w, w_in_g, rc, rsa, rsb)
    attn_o, lse = _attn_fwd_fused(proj)
    rec, states = _hgrn_fwd(proj, lb_raw)

    (dx2, d_o, delta, d_ag, d_rec, d_hg, dwout_p, d_final, d_attn_w, d_hgrn_w, loss) = _mid(
        attn_o, rec, proj, x, target, w_out_g, attn_w, hgrn_w, final_w.reshape(1, D_MODEL))

    dqkv = _attn_bwd_fused(proj, d_o, lse, delta)
    d_hq, d_hf, d_hi, d_lb = _hgrn_bwd(proj, lb_raw, d_rec, states)

    grad_x, dwin_p, d_mix = _in_proj_bwd(
        (dqkv[0], dqkv[1], dqkv[2], d_ag, d_hq, d_hf, d_hi, d_hg), hn_t, w_in_g, x, dx2, mix_w, rc, rsa, rsb)
    small_p = _pack_small(d_mix, d_attn_w, d_hgrn_w, d_lb, d_final, loss)
    return grad_x, dwin_p, dwout_p, small_p


def kernel(x, positions, w_in, w_out, mix_norm_w, attn_out_norm_w, hgrn_out_norm_w, hgrn_lb_raw, final_norm_w, loss_target, m_w_in, m_w_out, m_mix_norm_w, m_attn_out_norm_w, m_hgrn_out_norm_w, m_hgrn_lb_raw, m_final_norm_w, v_w_in, v_w_out, v_mix_norm_w, v_attn_out_norm_w, v_hgrn_out_norm_w, v_hgrn_lb_raw, v_final_norm_w):
    w_in_g, w_out_g = _gather_weights(w_in[0], w_out[0])
    grad_x, dwin_p, dwout_p, small_p = _local_step(
        x[0], positions.reshape(SEQ, 1), w_in_g, w_out_g, mix_norm_w, attn_out_norm_w, hgrn_out_norm_w,
        hgrn_lb_raw, final_norm_w, loss_target[0])

    w_s = _pack_small(mix_norm_w, attn_out_norm_w, hgrn_out_norm_w, hgrn_lb_raw, final_norm_w)
    m_s = _pack_small(m_mix_norm_w, m_attn_out_norm_w, m_hgrn_out_norm_w, m_hgrn_lb_raw, m_final_norm_w)
    v_s = _pack_small(v_mix_norm_w, v_attn_out_norm_w, v_hgrn_out_norm_w, v_hgrn_lb_raw, v_final_norm_w)
    (g_in, d_in, nm_in, nv_in, g_out, d_out, nm_out, nv_out, g_s, d_s, nm_s, nv_s) = _exchange_update(
        dwin_p, dwout_p, small_p, w_in[0], m_w_in[0], v_w_in[0], w_out[0], m_w_out[0], v_w_out[0], w_s, m_s, v_s)

    loss = g_s[ROW_LOSS, 0]
    return (loss, grad_x[None], g_in[None], g_out[None], *_unpack_small(g_s),
            d_in[None], d_out[None], *_unpack_small(d_s),
            nm_in[None], nm_out[None], *_unpack_small(nm_s),
            nv_in[None], nv_out[None], *_unpack_small(nv_s))
```

```python
import functools

import jax
import jax.numpy as jnp
from jax import lax
from jax.experimental import pallas as pl
from jax.experimental.pallas import tpu as pltpu

F32 = jnp.float32
BF16 = jnp.bfloat16

SEQ = 4096
D_MODEL = 1024
ATTN_WIDTH = 512
HGRN_WIDTH = 512
HEAD_DIM = 64
HGRN_HEADS = 4
HGRN_DIM = 128
HGRN_CHUNK = 64
N_CHUNKS = SEQ // HGRN_CHUNK
IN_COLS = 4096
COL_BLOCK = 512
N_DEV = 8
WOUT_ROWS = D_MODEL // N_DEV
ATTN_BLOCK = 128
DILATIONS = (1, 4, 16)
ROPE_THETA = 500000.0
ROPE_DIMS = 16
ROPE_HALF = 8
NORM_EPS = 1e-6
NEG_BIG = -1e30
LANES = 128

ADAM_LR = 0.001
ADAM_B1 = 0.9
ADAM_B2 = 0.999
ADAM_EPS = 1e-08
ADAM_WD = 0.01
ADAM_STEP = 10

SMALL_ROWS = 48
ROW_MIX, ROW_ATTN, ROW_HGRN, ROW_LB, ROW_FINAL, ROW_LOSS = 0, 8, 16, 24, 32, 40

VMEM_LIMIT = 56 * 1024 * 1024
MESH = pl.DeviceIdType.MESH


def _mm(a, b):
    return lax.dot_general(a, b, (((1,), (0,)), ((), ())), preferred_element_type=F32)


def _mm_nt(a, b):
    return lax.dot_general(a, b, (((1,), (1,)), ((), ())), preferred_element_type=F32)


def _mm_tn(a, b):
    return lax.dot_general(a, b, (((0,), (0,)), ((), ())), preferred_element_type=F32)


def _mm_exact(a, b):
    return lax.dot_general(a, b, (((1,), (0,)), ((), ())), preferred_element_type=F32,
                           precision=lax.Precision.HIGHEST)


def _sigmoid(v):
    return 1.0 / (1.0 + jnp.exp(-v))


def _params(sem=None, **kw):
    return pltpu.CompilerParams(dimension_semantics=sem, vmem_limit_bytes=VMEM_LIMIT, **kw)


def _my_place():
    return lax.axis_index("x"), lax.axis_index("y"), lax.axis_index("c")


def _peer(place, rel):
    x, y, c = place
    return (x ^ ((rel >> 2) & 1), y ^ ((rel >> 1) & 1), c ^ (rel & 1))


def _flat(place):
    x, y, c = place
    return 4 * x + 2 * y + c


def _gather_weights(w_in, w_out):
    def body(win_ref, wout_ref, gin_ref, gout_ref, send_sems, recv_sems):
        me = _my_place()
        x, y, c = me
        sibling = (x, y, 1 - c)
        chips = [(1 - x, y), (x, 1 - y), (1 - x, 1 - y)]

        def slab(which, place):
            idx = _flat(place)
            if which == 0:
                return gin_ref.at[idx]
            return gout_ref.at[pl.ds(pl.multiple_of(idx * WOUT_ROWS, WOUT_ROWS), WOUT_ROWS), :]

        def copy(which, k, block, to):
            ref = slab(which, block)
            return pltpu.make_async_remote_copy(
                src_ref=ref, dst_ref=ref, send_sem=send_sems.at[7 * which + k],
                recv_sem=recv_sems.at[7 * which + k], device_id=to, device_id_type=MESH)

        gin_ref[_flat(me)] = win_ref[...].astype(BF16)
        gout_ref[pl.ds(pl.multiple_of(_flat(me) * WOUT_ROWS, WOUT_ROWS), WOUT_ROWS), :] = (
            wout_ref[...].astype(BF16))

        started = []
        for which in (0, 1):
            first = [copy(which, 0, me, sibling)]
            first += [copy(which, 1 + j, me, (*chip, c)) for j, chip in enumerate(chips)]
            for cp in first:
                cp.start()
            started += first
        for which in (0, 1):
            for j, chip in enumerate(chips):
                copy(which, 1 + j, (*chip, c), me).wait_recv()
                fwd = copy(which, 4 + j, (*chip, c), sibling)
                fwd.start()
                started.append(fwd)
        for which in (0, 1):
            copy(which, 0, sibling, me).wait_recv()
            for j, chip in enumerate(chips):
                copy(which, 4 + j, (*chip, 1 - c), me).wait_recv()
        for cp in started:
            cp.wait_send()

    return pl.pallas_call(
        body, name="gather_weights",
        out_shape=(jax.ShapeDtypeStruct((N_DEV, D_MODEL, COL_BLOCK), BF16),
                   jax.ShapeDtypeStruct((D_MODEL, D_MODEL), BF16)),
        in_specs=[pl.BlockSpec(memory_space=pltpu.VMEM), pl.BlockSpec(memory_space=pltpu.VMEM)],
        out_specs=(pl.BlockSpec(memory_space=pltpu.VMEM), pl.BlockSpec(memory_space=pltpu.VMEM)),
        scratch_shapes=[pltpu.SemaphoreType.DMA((14,)), pltpu.SemaphoreType.DMA((14,))],
        compiler_params=_params(),
    )(w_in, w_out)


def _rope_tables(pos_col, inv_freq_lanes):
    tm = 512

    def body(pos_ref, invf_ref, c_ref, sa_ref, sb_ref):
        ang = pos_ref[...].astype(F32) * invf_ref[...]
        e = lax.broadcasted_iota(jnp.int32, (tm, LANES), 1) & (HEAD_DIM - 1)
        cos, sin = jnp.cos(ang), jnp.sin(ang)
        c_ref[...] = jnp.where(e < ROPE_DIMS, cos, 1.0)
        sa_ref[...] = jnp.where((e >= ROPE_HALF) & (e < ROPE_DIMS), sin, 0.0)
        sb_ref[...] = jnp.where(e < ROPE_HALF, -sin, 0.0)

    tab = jax.ShapeDtypeStruct((SEQ, LANES), F32)
    spec = pl.BlockSpec((tm, LANES), lambda i: (i, 0))
    return pl.pallas_call(
        body, name="rope_tables", grid=(SEQ // tm,), out_shape=(tab, tab, tab),
        in_specs=[pl.BlockSpec((tm, 1), lambda i: (i, 0)), pl.BlockSpec((1, LANES), lambda i: (0, 0))],
        out_specs=(spec, spec, spec), compiler_params=_params(("parallel",)),
    )(pos_col, inv_freq_lanes)


def _rot(t, c, sa, sb):
    n = t.shape[1]
    return t * c + pltpu.roll(t, ROPE_HALF, 1) * sa + pltpu.roll(t, n - ROPE_HALF, 1) * sb


def _rot_transposed(g, c, sa, sb):
    n = g.shape[1]
    return g * c + pltpu.roll(g * sa, n - ROPE_HALF, 1) + pltpu.roll(g * sb, ROPE_HALF, 1)


def _in_proj_fwd(x, mix_w, w_g, rc, rsa, rsb):
    tm = 256

    def body(x_ref, w_ref, wg_ref, c_ref, sa_ref, sb_ref, proj_ref, hnt_ref):
        xf = x_ref[...]
        ms = jnp.mean(xf * xf, axis=-1, keepdims=True)
        hn = xf * lax.rsqrt(ms + NORM_EPS) * w_ref[...]
        hnt_ref[...] = hn.T.astype(BF16)
        hb = hn.astype(BF16)
        c = jnp.tile(c_ref[...], (1, 4))
        sa = jnp.tile(sa_ref[...], (1, 4))
        sb = jnp.tile(sb_ref[...], (1, 4))
        for j in range(N_DEV):
            acc = _mm(hb, wg_ref[j])
            if j < 2:
                acc = _rot(acc, c, sa, sb)
            proj_ref[:, COL_BLOCK * j:COL_BLOCK * (j + 1)] = acc

    tab = pl.BlockSpec((tm, LANES), lambda i: (i, 0))
    return pl.pallas_call(
        body, name="in_proj_fwd", grid=(SEQ // tm,),
        out_shape=(jax.ShapeDtypeStruct((SEQ, IN_COLS), F32),
                   jax.ShapeDtypeStruct((D_MODEL, SEQ), BF16)),
        in_specs=[pl.BlockSpec((tm, D_MODEL), lambda i: (i, 0)),
                  pl.BlockSpec((1, D_MODEL), lambda i: (0, 0)),
                  pl.BlockSpec((N_DEV, D_MODEL, COL_BLOCK), lambda i: (0, 0, 0)),
                  tab, tab, tab],
        out_specs=(pl.BlockSpec((tm, IN_COLS), lambda i: (i, 0)),
                   pl.BlockSpec((D_MODEL, tm), lambda i: (0, i))),
        compiler_params=_params(("parallel",)),
    )(x, mix_w, w_g, rc, rsa, rsb)


ATTN_GROUP = 4
BLOCKS_PER_PATTERN = SEQ // ATTN_BLOCK


def _write_band_bias(bias_ref):
    qi = lax.broadcasted_iota(jnp.int32, (ATTN_BLOCK, 2 * ATTN_BLOCK), 0)
    kj = lax.broadcasted_iota(jnp.int32, (ATTN_BLOCK, 2 * ATTN_BLOCK), 1)
    bias_ref[0] = jnp.where((kj >= qi) & (kj <= qi + ATTN_BLOCK), 0.0, NEG_BIG)
    bias_ref[1] = jnp.where(kj <= qi, 0.0, NEG_BIG)


def _head0_lanes():
    return lax.broadcasted_iota(jnp.int32, (ATTN_BLOCK, LANES), 1) < HEAD_DIM


def _strided(start, size, d):
    return pl.ds(start, size) if d == 1 else pl.ds(start, size, stride=d)


def _block_place(i, d):
    nblk = BLOCKS_PER_PATTERN // d
    r, n = i // nblk, i % nblk
    kn = jnp.maximum(n - 1, 0)
    row0, key0 = n * (d * ATTN_BLOCK) + r, kn * (d * ATTN_BLOCK) + r
    if d == 1:
        row0, key0 = pl.multiple_of(row0, ATTN_BLOCK), pl.multiple_of(key0, ATTN_BLOCK)
    return _strided(row0, ATTN_BLOCK, d), _strided(key0, 2 * ATTN_BLOCK, d), (n == 0).astype(jnp.int32)


def _for_each_group(d, load, compute, store):
    def group(g, carry):
        items = [load(*_block_place(g * ATTN_GROUP + u, d)) for u in range(ATTN_GROUP)]
        results = [compute(item) for item in items]
        for item, res in zip(items, results):
            store(item, res)
        return carry

    lax.fori_loop(0, BLOCKS_PER_PATTERN // ATTN_GROUP, group, 0)


def _attn_fwd_fused(proj):
    n_pat = len(DILATIONS)
    tile = (ATTN_BLOCK, LANES)

    def body(q_ref, k_ref, v_ref, o_ref, lse_ref, m_acc, l_acc, bias_ref):
        _write_band_bias(bias_ref)
        h0 = _head0_lanes()
        for pi, d in enumerate(DILATIONS):
            first, last = pi == 0, pi == n_pat - 1

            def load(rows, keys, which, first=first):
                item = dict(rows=rows, keys=keys, which=which)
                if not first:
                    item.update(o=o_ref[rows, :], m=[m_acc.at[h][rows, :] for h in range(2)],
                                l=[l_acc.at[h][rows, :] for h in range(2)])
                return item

            def compute(item, first=first):
                res = []
                q = q_ref[item["rows"], :]
                kb = k_ref[item["keys"], :].astype(BF16)
                vb = v_ref[item["keys"], :].astype(BF16)
                for h in range(2):
                    hm = h0 if h == 0 else jnp.logical_not(h0)
                    qh = jnp.where(hm, q, 0.0).astype(BF16)
                    s = _mm_nt(qh, kb) * 0.125 + bias_ref[item["which"]]
                    mb = jnp.max(s, axis=-1, keepdims=True)
                    if first:
                        p = jnp.exp(s - mb)
                        mn = jnp.broadcast_to(mb, tile)
                    else:
                        mn = jnp.maximum(item["m"][h], mb)
                        alpha = jnp.exp(item["m"][h] - mn)
                        p = jnp.exp(s - jnp.concatenate([mn, mn], axis=1))
                    ls = jnp.sum(p, axis=-1, keepdims=True)
                    pv = _mm(p.astype(BF16), vb)
                    if first:
                        res.append((pv, mn, jnp.broadcast_to(ls, tile)))
                    else:
                        res.append((alpha * item["o"] + pv, mn, alpha * item["l"][h] + ls))
                return res

            def store(item, res, last=last):
                rows = item["rows"]
                (o0, m0, l0), (o1, m1, l1) = res
                if last:
                    o_ref[rows, :] = jnp.where(h0, o0 / l0, o1 / l1)
                    lse_ref[rows, :] = jnp.where(h0, m0 + jnp.log(l0), m1 + jnp.log(l1))
                else:
                    o_ref[rows, :] = jnp.where(h0, o0, o1)
                    m_acc.at[0][rows, :], m_acc.at[1][rows, :] = m0, m1
                    l_acc.at[0][rows, :], l_acc.at[1][rows, :] = l0, l1

            _for_each_group(d, load, compute, store)

    slab = lambda g: pl.BlockSpec((SEQ, LANES), functools.partial(lambda hp, g: (0, 4 * g + hp), g=g))
    wide = jax.ShapeDtypeStruct((SEQ, ATTN_WIDTH), F32)
    return pl.pallas_call(
        body, name="attn_fwd", grid=(4,), out_shape=(wide, wide),
        in_specs=[slab(0), slab(1), slab(2)], out_specs=(slab(0), slab(0)),
        scratch_shapes=[pltpu.VMEM((2, SEQ, LANES), F32), pltpu.VMEM((2, SEQ, LANES), F32),
                        pltpu.VMEM((2, ATTN_BLOCK, 2 * ATTN_BLOCK), F32)],
        compiler_params=_params(("parallel",)),
    )(proj, proj, proj)


def _attn_bwd_fused(proj, d_out, lse, delta):
    def body(q_ref, k_ref, v_ref, do_ref, lse_ref, del_ref, dq_ref, dk_ref, dv_ref, bias_ref):
        _write_band_bias(bias_ref)
        dk_ref[...] = jnp.zeros_like(dk_ref)
        dv_ref[...] = jnp.zeros_like(dv_ref)
        h0 = _head0_lanes()
        for pi, d in enumerate(DILATIONS):
            first = pi == 0

            def load(rows, keys, which):
                return dict(rows=rows, keys=keys, q=q_ref[rows, :], g=do_ref[rows, :], lse=lse_ref[rows, :],
                            delta=del_ref[rows, :], k=k_ref[keys, :].astype(BF16),
                            v=v_ref[keys, :].astype(BF16), bias=bias_ref[which])

            def per_head(t):
                swapped = pltpu.roll(t, HEAD_DIM, 1)
                a, b = jnp.where(h0, t, swapped), jnp.where(h0, swapped, t)
                return jnp.concatenate([a, a], axis=1), jnp.concatenate([b, b], axis=1)

            def compute(item):
                dq_h = []
                dk_c = jnp.zeros((2 * ATTN_BLOCK, LANES), F32)
                dv_c = jnp.zeros((2 * ATTN_BLOCK, LANES), F32)
                lse_h, delta_h = per_head(item["lse"]), per_head(item["delta"])
                for h in range(2):
                    hm = h0 if h == 0 else jnp.logical_not(h0)
                    qh = jnp.where(hm, item["q"], 0.0).astype(BF16)
                    gh = jnp.where(hm, item["g"], 0.0).astype(BF16)
                    s = _mm_nt(qh, item["k"]) * 0.125 + item["bias"]
                    p = jnp.exp(s - lse_h[h])
                    dp = _mm_nt(gh, item["v"])
                    ds = (p * (dp - delta_h[h]) * 0.125).astype(BF16)
                    dq_h.append(_mm(ds, item["k"]))
                    dk_c = dk_c + _mm_tn(ds, qh)
                    dv_c = dv_c + _mm_tn(p.astype(BF16), gh)
                return jnp.where(h0, dq_h[0], dq_h[1]), dk_c, dv_c

            def store(item, res, first=first):
                rows, keys = item["rows"], item["keys"]
                if first:
                    dq_ref[rows, :] = res[0]
                else:
                    dq_ref[rows, :] += res[0]
                dk_ref[keys, :] += res[1]
                dv_ref[keys, :] += res[2]

            _for_each_group(d, load, compute, store)

    slab = lambda g: pl.BlockSpec((SEQ, LANES), functools.partial(lambda hp, g: (0, 4 * g + hp), g=g))
    wide = jax.ShapeDtypeStruct((SEQ, ATTN_WIDTH), F32)
    return pl.pallas_call(
        body, name="attn_bwd", grid=(4,), out_shape=(wide, wide, wide),
        scratch_shapes=[pltpu.VMEM((2, ATTN_BLOCK, 2 * ATTN_BLOCK), F32)],
        in_specs=[slab(0), slab(1), slab(2), slab(0), slab(0), slab(0)], out_specs=(slab(0), slab(0), slab(0)),
        compiler_params=_params(("parallel",)),
    )(proj, proj, proj, d_out, lse, delta)


def _hgrn_lower_bound(lb_ref):
    r0, r1 = lb_ref[0:1, :], lb_ref[1:2, :]
    mx = jnp.maximum(r0, r1)
    e0, e1 = jnp.exp(r0 - mx), jnp.exp(r1 - mx)
    return e0 / (e0 + e1)


def _hgrn_gates(hq, hf, lb):
    sq = _sigmoid(hq)
    sg = _sigmoid(hf)
    f = lb + (1.0 - lb) * sg
    return hq * sq, sq, sg, f, 1.0 - f, jnp.log(f)


HGRN_PAIR = 2
HGRN_SEQ_BLOCK = 1024
HGRN_GROUP = 4
HGRN_ROWS = HGRN_GROUP * HGRN_CHUNK


def _hgrn_specs(reverse):
    n_blocks = SEQ // HGRN_SEQ_BLOCK
    width = HGRN_PAIR * HGRN_DIM
    blk = (lambda s: n_blocks - 1 - s) if reverse else (lambda s: s)
    cols = lambda g: pl.BlockSpec((HGRN_SEQ_BLOCK, width),
                                  functools.partial(lambda p, s, g: (blk(s), HGRN_PAIR * g + p), g=g))
    pair = pl.BlockSpec((HGRN_SEQ_BLOCK, width), lambda p, s: (blk(s), p))
    lb = pl.BlockSpec((2, width), lambda p, s: (0, p))
    states = pl.BlockSpec((HGRN_PAIR, HGRN_SEQ_BLOCK // HGRN_CHUNK, HGRN_DIM, HGRN_DIM),
                          lambda p, s: (p, blk(s), 0, 0))
    return cols, pair, lb, states


def _chunk_masks():
    ri = lax.broadcasted_iota(jnp.int32, (HGRN_ROWS, HGRN_ROWS), 0)
    ci = lax.broadcasted_iota(jnp.int32, (HGRN_ROWS, HGRN_ROWS), 1)
    same = (ri // HGRN_CHUNK) == (ci // HGRN_CHUNK)
    return same, same & (ri >= ci), same & (ri <= ci)


def _mm_select(sel, v):
    hi = v.astype(BF16)
    r1 = v - hi.astype(F32)
    mid = r1.astype(BF16)
    lo = (r1 - mid.astype(F32)).astype(BF16)
    return _mm(sel, hi) + _mm(sel, mid) + _mm(sel, lo)


def _head_cols(a, h):
    return a[:, HGRN_DIM * h:HGRN_DIM * (h + 1)]


def _hgrn_fwd(proj, lb_raw):
    t, rws = HGRN_CHUNK, HGRN_ROWS

    def body(hq_ref, hf_ref, hi_ref, lb_ref, rec_ref, st_ref, state):
        @pl.when(pl.program_id(1) == 0)
        def _():
            state[...] = jnp.zeros_like(state)

        lb = _hgrn_lower_bound(lb_ref)
        same, causal, _ = _chunk_masks()
        sel = jnp.concatenate([causal, same], axis=0).astype(BF16)

        def group(g, sts):
            rows = pl.ds(pl.multiple_of(g * rws, rws), rws)
            q, _, _, _, k, lf = _hgrn_gates(hq_ref[rows, :], hf_ref[rows, :], lb)
            sums = _mm_select(sel, lf)
            cum, last = sums[:rws], sums[rws:]
            qd = (q * jnp.exp(cum)).astype(BF16)
            ki = (k * jnp.exp(-cum)).astype(BF16)
            ke = (k * jnp.exp(last - cum)).astype(BF16)
            vb = hi_ref[rows, :].astype(BF16)
            dec = jnp.exp(last)
            new_sts, recs = [], []
            for h in range(HGRN_PAIR):
                qd_h, ke_h, vb_h = _head_cols(qd, h), _head_cols(ke, h), _head_cols(vb, h)
                att = jnp.where(causal, _mm_nt(qd_h, _head_cols(ki, h)), 0.0).astype(BF16)
                intra = _mm(att, vb_h)
                st = sts[h]
                outs = []
                for c in range(HGRN_GROUP):
                    sl = slice(c * t, (c + 1) * t)
                    st_ref[h, g * HGRN_GROUP + c] = st
                    outs.append(intra[sl] + _mm_nt(qd_h[sl], st.astype(BF16)))
                    st = st * _head_cols(dec[c * t:c * t + 1, :], h) + _mm_tn(vb_h[sl], ke_h[sl])
                new_sts.append(st)
                recs.append(jnp.concatenate(outs, axis=0))
            rec_ref[rows, :] = jnp.concatenate(recs, axis=1)
            return tuple(new_sts)

        sts = lax.fori_loop(0, HGRN_SEQ_BLOCK // rws, group, tuple(state[h] for h in range(HGRN_PAIR)))
        for h in range(HGRN_PAIR):
            state[h] = sts[h]

    cols, pair, lb, states = _hgrn_specs(reverse=False)
    return pl.pallas_call(
        body, name="hgrn_fwd", grid=(HGRN_HEADS // HGRN_PAIR, SEQ // HGRN_SEQ_BLOCK),
        out_shape=(jax.ShapeDtypeStruct((SEQ, HGRN_WIDTH), F32),
                   jax.ShapeDtypeStruct((HGRN_HEADS, N_CHUNKS, HGRN_DIM, HGRN_DIM), F32)),
        in_specs=[cols(4), cols(5), cols(6), lb], out_specs=(pair, states),
        scratch_shapes=[pltpu.VMEM((HGRN_PAIR, HGRN_DIM, HGRN_DIM), F32)],
        compiler_params=_params(("parallel", "arbitrary")),
    )(proj, proj, proj, lb_raw)


def _hgrn_bwd(proj, lb_raw, d_rec, states):
    t, rws = HGRN_CHUNK, HGRN_ROWS

    def body(hq_ref, hf_ref, hi_ref, lb_ref, do_ref, st_ref, dhq_ref, dhf_ref, dhi_ref, dlb_ref,
             dstate, dlb_acc):
        lb = _hgrn_lower_bound(lb_ref)
        same, causal, anti = _chunk_masks()
        sel = jnp.concatenate([causal, same], axis=0).astype(BF16)
        sel_t = jnp.concatenate([anti, same], axis=1).astype(BF16)
        @pl.when(pl.program_id(1) == 0)
        def _():
            dstate[...] = jnp.zeros_like(dstate)
            dlb_acc[...] = jnp.zeros_like(dlb_acc)

        n_groups = HGRN_SEQ_BLOCK // rws
        chunks = [slice(c * t, (c + 1) * t) for c in range(HGRN_GROUP)]

        def group(i, dsts_in):
            g = n_groups - 1 - i
            rows = pl.ds(pl.multiple_of(g * rws, rws), rws)
            hq = hq_ref[rows, :]
            q, sq, sg, f, k, lf = _hgrn_gates(hq, hf_ref[rows, :], lb)
            sums = _mm_select(sel, lf)
            cum, last = sums[:rws], sums[rws:]
            e_cum, e_inv, e_end, dec = jnp.exp(cum), jnp.exp(-cum), jnp.exp(last - cum), jnp.exp(last)
            qd, ki, ke = q * e_cum, k * e_inv, k * e_end
            qdb, kib, keb = qd.astype(BF16), ki.astype(BF16), ke.astype(BF16)
            vb = hi_ref[rows, :].astype(BF16)
            gb = do_ref[rows, :].astype(BF16)

            dsts_out, per_head = [], []
            for h in range(HGRN_PAIR):
                qdb_h, kib_h, keb_h = _head_cols(qdb, h), _head_cols(kib, h), _head_cols(keb, h)
                vb_h, gb_h = _head_cols(vb, h), _head_cols(gb, h)
                att = jnp.where(causal, _mm_nt(qdb_h, kib_h), 0.0).astype(BF16)
                datt = jnp.where(causal, _mm_nt(gb_h, vb_h), 0.0).astype(BF16)
                dv = _mm_tn(att, gb_h)
                dqd = _mm(datt, kib_h)
                dki = _mm_tn(datt, qdb_h)

                decs = [_head_cols(dec[c * t:c * t + 1, :], h) for c in range(HGRN_GROUP)]
                dsts = [None] * HGRN_GROUP
                dst = dsts_in[h]
                for c in reversed(range(HGRN_GROUP)):
                    dsts[c] = dst
                    dst = dst * decs[c] + _mm_tn(gb_h[chunks[c]], qdb_h[chunks[c]])
                dsts_out.append(dst)

                dv_x, dqd_x, dke, dlast_x = [], [], [], []
                for c, sl in enumerate(chunks):
                    st_prev = st_ref[h, g * HGRN_GROUP + c]
                    dstb = dsts[c].astype(BF16)
                    dv_x.append(_mm_nt(keb_h[sl], dstb))
                    dqd_x.append(_mm(gb_h[sl], st_prev.astype(BF16)))
                    dke.append(_mm(vb_h[sl], dstb))
                    ddec = jnp.sum(dsts[c] * st_prev, axis=0, keepdims=True)
                    dlast_x.append(jnp.broadcast_to(ddec * decs[c], (t, HGRN_DIM)))
                per_head.append((dv + jnp.concatenate(dv_x, axis=0), dqd + jnp.concatenate(dqd_x, axis=0),
                                 dki, jnp.concatenate(dke, axis=0), jnp.concatenate(dlast_x, axis=0)))
            dv, dqd, dki, dke, dlast = (jnp.concatenate([a, b], axis=1) for a, b in zip(*per_head))

            dq = dqd * e_cum
            dk = dki * e_inv + dke * e_end
            dke_ke = dke * ke
            dcum = dqd * qd - dki * ki - dke_ke
            dlf = _mm_select(sel_t, jnp.concatenate([dcum, dke_ke], axis=0)) + dlast
            df = dlf / f - dk
            dhq_ref[rows, :] = dq * (sq * (1.0 + hq * (1.0 - sq)))
            dhf_ref[rows, :] = df * (1.0 - lb) * (sg * (1.0 - sg))
            dhi_ref[rows, :] = dv
            dlb_acc[...] += jnp.sum(df * (1.0 - sg), axis=0, keepdims=True)
            return tuple(dsts_out)

        dsts = lax.fori_loop(0, n_groups, group, tuple(dstate[h] for h in range(HGRN_PAIR)))
        for h in range(HGRN_PAIR):
            dstate[h] = dsts[h]
        g0 = dlb_acc[...] * lb * (1.0 - lb)
        dlb_ref[...] = jnp.concatenate([g0, -g0], axis=0)

    cols, pair, lb_spec, st_spec = _hgrn_specs(reverse=True)
    wide = jax.ShapeDtypeStruct((SEQ, HGRN_WIDTH), F32)
    return pl.pallas_call(
        body, name="hgrn_bwd", grid=(HGRN_HEADS // HGRN_PAIR, SEQ // HGRN_SEQ_BLOCK),
        out_shape=(wide, wide, wide, jax.ShapeDtypeStruct((2, HGRN_WIDTH), F32)),
        in_specs=[cols(4), cols(5), cols(6), lb_spec, pair, st_spec],
        out_specs=(pair, pair, pair, lb_spec),
        scratch_shapes=[pltpu.VMEM((HGRN_PAIR, HGRN_DIM, HGRN_DIM), F32),
                        pltpu.VMEM((1, HGRN_PAIR * HGRN_DIM), F32)],
        compiler_params=_params(("parallel", "arbitrary")),
    )(proj, proj, proj, lb_raw, d_rec, states)


def _group_sum(v, group):
    parts = []
    for s in range(v.shape[1] // LANES):
        slab = v[:, LANES * s:LANES * (s + 1)]
        if group == LANES:
            parts.append(jnp.broadcast_to(jnp.sum(slab, axis=-1, keepdims=True), slab.shape))
        else:
            h0 = lax.broadcasted_iota(jnp.int32, slab.shape, 1) < HEAD_DIM
            s0 = jnp.sum(jnp.where(h0, slab, 0.0), axis=-1, keepdims=True)
            s1 = jnp.sum(jnp.where(h0, 0.0, slab), axis=-1, keepdims=True)
            parts.append(jnp.where(h0, s0, s1))
    return jnp.concatenate(parts, axis=1)


def _mid(attn_o, rec, proj, x, target, w_out_g, attn_w, hgrn_w, final_w):
    tm = 256

    def branch_fwd(o, gate, w, group):
        r = lax.rsqrt(_group_sum(o * o, group) * (1.0 / group) + NORM_EPS)
        nrm = o * r
        sg = _sigmoid(gate)
        return r, nrm, sg, nrm * w * (gate * sg)

    def branch_bwd(dy, r, nrm, sg, gate, w, group):
        silu = gate * sg
        d_gate = dy * nrm * w * (sg * (1.0 + gate * (1.0 - sg)))
        d_w = jnp.sum(dy * nrm * silu, axis=0, keepdims=True)
        dn = dy * w * silu
        d_o = r * (dn - nrm * (_group_sum(dn * nrm, group) * (1.0 / group)))
        return d_o, d_gate, d_w

    def body(o_ref, rec_ref, ag_ref, hg_ref, x_ref, tgt_ref, wout_ref, aw_ref, hw_ref, fw_ref,
             dx2_ref, do_ref, delta_ref, dag_ref, drec_ref, dhg_ref, dwout_ref, dfw_ref, daw_ref, dhw_ref,
             loss_ref):
        i = pl.program_id(0)

        @pl.when(i == 0)
        def _():
            dwout_ref[...] = jnp.zeros_like(dwout_ref)
            dfw_ref[...] = jnp.zeros_like(dfw_ref)
            daw_ref[...] = jnp.zeros_like(daw_ref)
            dhw_ref[...] = jnp.zeros_like(dhw_ref)
            loss_ref[...] = jnp.zeros_like(loss_ref)

        o, rc, ag, hg = o_ref[...], rec_ref[...], ag_ref[...], hg_ref[...]
        aw, hw, fw = aw_ref[...], hw_ref[...], fw_ref[...]
        ra, na, sga, ya = branch_fwd(o, ag, aw, HEAD_DIM)
        rh, nh, sgh, yh = branch_fwd(rc, hg, hw, HGRN_DIM)
        mixed = jnp.concatenate([ya, yh], axis=1).astype(BF16)
        wout = wout_ref[...]
        x2 = x_ref[...] + _mm(mixed, wout)
        rstd = lax.rsqrt(jnp.mean(x2 * x2, axis=-1, keepdims=True) + NORM_EPS)
        xn = x2 * rstd
        err = xn * fw - tgt_ref[...]
        row_loss = jnp.mean(err * err, axis=-1, keepdims=True)
        loss_ref[...] += 0.5 * jnp.sum(row_loss, axis=0, keepdims=True)
        dy = err * (1.0 / D_MODEL)
        dfw_ref[...] += jnp.sum(dy * xn, axis=0, keepdims=True)
        dxn = dy * fw
        dx2 = rstd * (dxn - xn * jnp.mean(dxn * xn, axis=-1, keepdims=True))
        dx2_ref[...] = dx2
        dx2b = dx2.astype(BF16)
        dwout_ref[...] += _mm_tn(mixed, dx2b)
        dmixed = _mm_nt(dx2b, wout)

        d_o, d_ag, d_aw = branch_bwd(dmixed[:, :ATTN_WIDTH], ra, na, sga, ag, aw, HEAD_DIM)
        d_rec, d_hg, d_hw = branch_bwd(dmixed[:, ATTN_WIDTH:], rh, nh, sgh, hg, hw, HGRN_DIM)
        do_ref[...] = d_o
        delta_ref[...] = _group_sum(d_o * o, HEAD_DIM)
        dag_ref[...] = d_ag
        drec_ref[...] = d_rec
        dhg_ref[...] = d_hg
        daw_ref[...] += d_aw
        dhw_ref[...] += d_hw

    half = lambda: pl.BlockSpec((tm, COL_BLOCK), lambda i: (i, 0))
    full = lambda: pl.BlockSpec((tm, D_MODEL), lambda i: (i, 0))
    fixed = lambda r, c: pl.BlockSpec((r, c), lambda i: (0, 0))
    wide = jax.ShapeDtypeStruct((SEQ, COL_BLOCK), F32)
    return pl.pallas_call(
        body, name="mid", grid=(SEQ // tm,),
        out_shape=(jax.ShapeDtypeStruct((SEQ, D_MODEL), F32), wide, wide, wide, wide, wide,
                   jax.ShapeDtypeStruct((D_MODEL, D_MODEL), F32),
                   jax.ShapeDtypeStruct((1, D_MODEL), F32), jax.ShapeDtypeStruct((1, COL_BLOCK), F32),
                   jax.ShapeDtypeStruct((1, COL_BLOCK), F32), jax.ShapeDtypeStruct((1, 1), F32)),
        in_specs=[half(), half(),
                  pl.BlockSpec((tm, COL_BLOCK), lambda i: (i, 3)), pl.BlockSpec((tm, COL_BLOCK), lambda i: (i, 7)),
                  full(), full(), fixed(D_MODEL, D_MODEL), fixed(1, COL_BLOCK), fixed(1, COL_BLOCK),
                  fixed(1, D_MODEL)],
        out_specs=(full(), half(), half(), half(), half(), half(), fixed(D_MODEL, D_MODEL),
                   fixed(1, D_MODEL), fixed(1, COL_BLOCK), fixed(1, COL_BLOCK), fixed(1, 1)),
        compiler_params=_params(("arbitrary",)),
    )(attn_o, rec, proj, proj, x, target, w_out_g, attn_w, hgrn_w, final_w)


def _in_proj_bwd(d_groups, hn_t, w_g, x, dx2, mix_w, rc, rsa, rsb):
    tm = 256
    n_tiles = SEQ // tm
    last_j = N_DEV - 1

    def body(*refs):
        dg_refs = refs[:N_DEV]
        hnt_ref, wg_ref, x_ref, dx2_ref, w_ref, c_ref, sa_ref, sb_ref = refs[N_DEV:N_DEV + 8]
        gx_ref, dwin_ref, dmw_ref, dhn = refs[N_DEV + 8:]
        j, i = pl.program_id(0), pl.program_id(1)
        rows = pl.ds(pl.multiple_of(i * tm, tm), tm)

        for jj in range(N_DEV):
            @pl.when(j == jj)
            def _(jj=jj):
                dp = dg_refs[jj][...]
                if jj < 2:
                    dp = _rot_transposed(dp, jnp.tile(c_ref[...], (1, 4)), jnp.tile(sa_ref[...], (1, 4)),
                                         jnp.tile(sb_ref[...], (1, 4)))
                dpb = dp.astype(BF16)
                contrib = _mm_nt(dpb, wg_ref[0])
                if jj == 0:
                    dhn[rows, :] = contrib
                else:
                    dhn[rows, :] += contrib
                dw = _mm(hnt_ref[...], dpb)

                @pl.when(i == 0)
                def _():
                    dwin_ref[0] = dw

                @pl.when(i > 0)
                def _():
                    dwin_ref[0] += dw

        @pl.when((j == last_j) & (i == 0))
        def _():
            dmw_ref[...] = jnp.zeros_like(dmw_ref)

        @pl.when(j == last_j)
        def _():
            xf = x_ref[...]
            w = w_ref[...]
            rstd = lax.rsqrt(jnp.mean(xf * xf, axis=-1, keepdims=True) + NORM_EPS)
            xn = xf * rstd
            g = dhn[rows, :]
            dmw_ref[...] += jnp.sum(g * xn, axis=0, keepdims=True)
            gw = g * w
            gx_ref[...] = dx2_ref[...] + rstd * (gw - xn * jnp.mean(gw * xn, axis=-1, keepdims=True))

    def group_spec(jj):
        return pl.BlockSpec((tm, COL_BLOCK), functools.partial(lambda j, i, jj: (jnp.where(j == jj, i, 0), 0), jj=jj))

    at_end = lambda cols: pl.BlockSpec((tm, cols), lambda j, i: (jnp.where(j == last_j, i, 0), 0))
    tab = lambda: pl.BlockSpec((tm, LANES), lambda j, i: (jnp.where(j < 2, i, 0), 0))
    return pl.pallas_call(
        body, name="in_proj_bwd", grid=(N_DEV, n_tiles),
        out_shape=(jax.ShapeDtypeStruct((SEQ, D_MODEL), F32),
                   jax.ShapeDtypeStruct((N_DEV, D_MODEL, COL_BLOCK), F32),
                   jax.ShapeDtypeStruct((1, D_MODEL), F32)),
        in_specs=[group_spec(jj) for jj in range(N_DEV)] + [
            pl.BlockSpec((D_MODEL, tm), lambda j, i: (0, i)),
            pl.BlockSpec((1, D_MODEL, COL_BLOCK), lambda j, i: (j, 0, 0)),
            at_end(D_MODEL), at_end(D_MODEL), pl.BlockSpec((1, D_MODEL), lambda j, i: (0, 0)),
            tab(), tab(), tab()],
        out_specs=(at_end(D_MODEL), pl.BlockSpec((1, D_MODEL, COL_BLOCK), lambda j, i: (j, 0, 0)),
                   pl.BlockSpec((1, D_MODEL), lambda j, i: (0, 0))),
        scratch_shapes=[pltpu.VMEM((SEQ, D_MODEL), F32)],
        compiler_params=_params(("arbitrary", "arbitrary")),
    )(*d_groups, hn_t, w_g, x, dx2, mix_w, rc, rsa, rsb)


def _adamw(w, g, m, v):
    m = ADAM_B1 * m + (1.0 - ADAM_B1) * g
    v = ADAM_B2 * v + (1.0 - ADAM_B2) * (g * g)
    m_hat = m / (1.0 - ADAM_B1 ** ADAM_STEP)
    v_hat = v / (1.0 - ADAM_B2 ** ADAM_STEP)
    delta = -ADAM_LR * (m_hat / (jnp.sqrt(v_hat) + ADAM_EPS) + ADAM_WD * w)
    return delta, m, v


def _exchange_update(dwin_p, dwout_p, small_p, w_in, m_in, v_in, w_out, m_out, v_out, w_s, m_s, v_s):
    rb = 128

    def body(dwin_hbm, dwout_hbm, small_ref, win_ref, min_ref, vin_ref, wout_ref, mout_ref, vout_ref,
             ws_ref, ms_ref, vs_ref,
             gin_ref, din_ref, nmin_ref, nvin_ref, gout_ref, dout_ref, nmout_ref, nvout_ref,
             gs_ref, ds_ref, nms_ref, nvs_ref,
             land_in, land_out, land_s, send_sems, recv_sems, local_sems):
        me = _my_place()
        my_flat = _flat(me)

        def out_rows(idx):
            return pl.ds(pl.multiple_of(idx * WOUT_ROWS, WOUT_ROWS), WOUT_ROWS)

        own_in = pltpu.make_async_copy(dwin_hbm.at[my_flat], land_in.at[0], local_sems.at[0])
        own_out = pltpu.make_async_copy(dwout_hbm.at[out_rows(my_flat), :], land_out.at[0], local_sems.at[1])
        own_in.start()
        own_out.start()
        land_s[0] = small_ref[...]

        def copies(rel):
            peer = _peer(me, rel)
            pf = _flat(peer)
            mk = lambda which, src, dst: pltpu.make_async_remote_copy(
                src_ref=src, dst_ref=dst, send_sem=send_sems.at[7 * which + rel - 1],
                recv_sem=recv_sems.at[7 * which + rel - 1], device_id=peer, device_id_type=MESH)
            return [mk(0, dwin_hbm.at[pf], land_in.at[rel]),
                    mk(1, dwout_hbm.at[out_rows(pf), :], land_out.at[rel]),
                    mk(2, small_ref, land_s.at[rel])]

        sent = []
        for rel in range(1, N_DEV):
            for cp in copies(rel):
                cp.start()
                sent.append(cp)
        for cp in sent:
            cp.wait_recv()
        for cp in sent:
            cp.wait_send()
        own_in.wait()
        own_out.wait()

        def update(land, w_ref, m_ref, v_ref, g_ref, d_ref, nm_ref, nv_ref, n_rows):
            def step(b, carry):
                rows = pl.ds(pl.multiple_of(b * rb, rb), rb)
                g = land[0, rows, :]
                for rel in range(1, N_DEV):
                    g = g + land[rel, rows, :]
                delta, nm, nv = _adamw(w_ref[rows, :], g, m_ref[rows, :], v_ref[rows, :])
                g_ref[rows, :] = g
                d_ref[rows, :] = delta
                nm_ref[rows, :] = nm
                nv_ref[rows, :] = nv
                return carry
            lax.fori_loop(0, n_rows // rb, step, 0)

        update(land_in, win_ref, min_ref, vin_ref, gin_ref, din_ref, nmin_ref, nvin_ref, D_MODEL)
        update(land_out, wout_ref, mout_ref, vout_ref, gout_ref, dout_ref, nmout_ref, nvout_ref, WOUT_ROWS)

        g = land_s[my_flat ^ 0]
        for dev in range(1, N_DEV):
            g = g + land_s[my_flat ^ dev]
        delta, nm, nv = _adamw(ws_ref[...], g, ms_ref[...], vs_ref[...])
        gs_ref[...] = g
        ds_ref[...] = delta
        nms_ref[...] = nm
        nvs_ref[...] = nv

    vm = lambda: pl.BlockSpec(memory_space=pltpu.VMEM)
    anyspace = lambda: pl.BlockSpec(memory_space=pl.ANY)
    big = jax.ShapeDtypeStruct((D_MODEL, COL_BLOCK), F32)
    flat = jax.ShapeDtypeStruct((WOUT_ROWS, D_MODEL), F32)
    small = jax.ShapeDtypeStruct((SMALL_ROWS, LANES), F32)
    return pl.pallas_call(
        body, name="exchange_update",
        out_shape=tuple([big] * 4 + [flat] * 4 + [small] * 4),
        in_specs=[anyspace(), anyspace()] + [vm() for _ in range(10)],
        out_specs=tuple(vm() for _ in range(12)),
        scratch_shapes=[pltpu.VMEM((N_DEV, D_MODEL, COL_BLOCK), F32),
                        pltpu.VMEM((N_DEV, WOUT_ROWS, D_MODEL), F32),
                        pltpu.VMEM((N_DEV, SMALL_ROWS, LANES), F32),
                        pltpu.SemaphoreType.DMA((21,)), pltpu.SemaphoreType.DMA((21,)),
                        pltpu.SemaphoreType.DMA((2,))],
        compiler_params=_params(),
    )(dwin_p, dwout_p, small_p, w_in, m_in, v_in, w_out, m_out, v_out, w_s, m_s, v_s)


def _pack_small(mix, attn, hgrn, lb, final, loss=None):
    def rows8(a):
        a = a.reshape(-1, LANES)
        return jnp.pad(a, ((0, 8 - a.shape[0]), (0, 0)))
    last = jnp.zeros((8, LANES), F32) if loss is None else jnp.pad(loss.reshape(1, 1), ((0, 7), (0, LANES - 1)))
    return jnp.concatenate([rows8(mix), rows8(attn), rows8(hgrn), rows8(lb), rows8(final), last], axis=0)


def _unpack_small(slab):
    return (slab[ROW_MIX:ROW_MIX + 8].reshape(1, D_MODEL), slab[ROW_ATTN:ROW_ATTN + 4].reshape(1, ATTN_WIDTH),
            slab[ROW_HGRN:ROW_HGRN + 4].reshape(1, HGRN_WIDTH), slab[ROW_LB:ROW_LB + 8].reshape(2, HGRN_WIDTH),
            slab[ROW_FINAL:ROW_FINAL + 8].reshape(D_MODEL))


def _local_step(x, pos_col, w_in_g, w_out_g, mix_w, attn_w, hgrn_w, lb_raw, final_w, target):
    inv = ROPE_THETA ** (-jnp.arange(ROPE_HALF, dtype=F32) * (2.0 / ROPE_DIMS))
    lane_e = jnp.arange(LANES) % HEAD_DIM
    inv_lanes = jnp.where(lane_e < ROPE_DIMS, inv[lane_e % ROPE_HALF], 0.0).reshape(1, LANES)
    rc, rsa, rsb = _rope_tables(pos_col, inv_lanes)

    proj, hn_t = _in_proj_fwd(x, mix_w, w_in_g, rc, rsa, rsb)
    attn_o, lse = _attn_fwd_fused(proj)
    rec, states = _hgrn_fwd(proj, lb_raw)

    (dx2, d_o, delta, d_ag, d_rec, d_hg, dwout_p, d_final, d_attn_w, d_hgrn_w, loss) = _mid(
        attn_o, rec, proj, x, target, w_out_g, attn_w, hgrn_w, final_w.reshape(1, D_MODEL))

    dqkv = _attn_bwd_fused(proj, d_o, lse, delta)
    d_hq, d_hf, d_hi, d_lb = _hgrn_bwd(proj, lb_raw, d_rec, states)

    grad_x, dwin_p, d_mix = _in_proj_bwd(
        (dqkv[0], dqkv[1], dqkv[2], d_ag, d_hq, d_hf, d_hi, d_hg), hn_t, w_in_g, x, dx2, mix_w, rc, rsa, rsb)
    small_p = _pack_small(d_mix, d_attn_w, d_hgrn_w, d_lb, d_final, loss)
    return grad_x, dwin_p, dwout_p, small_p


def kernel(x, positions, w_in, w_out, mix_norm_w, attn_out_norm_w, hgrn_out_norm_w, hgrn_lb_raw, final_norm_w, loss_target, m_w_in, m_w_out, m_mix_norm_w, m_attn_out_norm_w, m_hgrn_out_norm_w, m_hgrn_lb_raw, m_final_norm_w, v_w_in, v_w_out, v_mix_norm_w, v_attn_out_norm_w, v_hgrn_out_norm_w, v_hgrn_lb_raw, v_final_norm_w):
    w_in_g, w_out_g = _gather_weights(w_in[0], w_out[0])
    grad_x, dwin_p, dwout_p, small_p = _local_step(
        x[0], positions.reshape(SEQ, 1), w_in_g, w_out_g, mix_norm_w, attn_out_norm_w, hgrn_out_norm_w,
        hgrn_lb_raw, final_norm_w, loss_target[0])

    w_s = _pack_small(mix_norm_w, attn_out_norm_w, hgrn_out_norm_w, hgrn_lb_raw, final_norm_w)
    m_s = _pack_small(m_mix_norm_w, m_attn_out_norm_w, m_hgrn_out_norm_w, m_hgrn_lb_raw, m_final_norm_w)
    v_s = _pack_small(v_mix_norm_w, v_attn_out_norm_w, v_hgrn_out_norm_w, v_hgrn_lb_raw, v_final_norm_w)
    (g_in, d_in, nm_in, nv_in, g_out, d_out, nm_out, nv_out, g_s, d_s, nm_s, nv_s) = _exchange_update(
        dwin_p, dwout_p, small_p, w_in[0], m_w_in[0], v_w_in[0], w_out[0], m_w_out[0], v_w_out[0], w_s, m_s, v_s)

    loss = g_s[ROW_LOSS, 0]
    return (loss, grad_x[None], g_in[None], g_out[None], *_unpack_small(g_s),
            d_in[None], d_out[None], *_unpack_small(d_s),
            nm_in[None], nm_out[None], *_unpack_small(nm_s),
            nv_in[None], nv_out[None], *_unpack_small(nv_s))
```

```python
import functools

import jax
import jax.numpy as jnp
from jax import lax
from jax.experimental import pallas as pl
from jax.experimental.pallas import tpu as pltpu

F32 = jnp.float32
BF16 = jnp.bfloat16

SEQ = 4096
D_MODEL = 1024
ATTN_WIDTH = 512
HGRN_WIDTH = 512
HEAD_DIM = 64
HGRN_HEADS = 4
HGRN_DIM = 128
HGRN_CHUNK = 64
N_CHUNKS = SEQ // HGRN_CHUNK
IN_COLS = 4096
COL_BLOCK = 512
N_DEV = 8
WOUT_ROWS = D_MODEL // N_DEV
ATTN_BLOCK = 128
DILATIONS = (1, 4, 16)
ROPE_THETA = 500000.0
ROPE_DIMS = 16
ROPE_HALF = 8
NORM_EPS = 1e-6
NEG_BIG = -1e30
LANES = 128

ADAM_LR = 0.001
ADAM_B1 = 0.9
ADAM_B2 = 0.999
ADAM_EPS = 1e-08
ADAM_WD = 0.01
ADAM_STEP = 10

SMALL_ROWS = 48
ROW_MIX, ROW_ATTN, ROW_HGRN, ROW_LB, ROW_FINAL, ROW_LOSS = 0, 8, 16, 24, 32, 40

VMEM_LIMIT = 56 * 1024 * 1024
MESH = pl.DeviceIdType.MESH


def _mm(a, b):
    return lax.dot_general(a, b, (((1,), (0,)), ((), ())), preferred_element_type=F32)


def _mm_nt(a, b):
    return lax.dot_general(a, b, (((1,), (1,)), ((), ())), preferred_element_type=F32)


def _mm_tn(a, b):
    return lax.dot_general(a, b, (((0,), (0,)), ((), ())), preferred_element_type=F32)


def _mm_exact(a, b):
    return lax.dot_general(a, b, (((1,), (0,)), ((), ())), preferred_element_type=F32,
                           precision=lax.Precision.HIGHEST)


def _sigmoid(v):
    return 1.0 / (1.0 + jnp.exp(-v))


def _params(sem=None, **kw):
    return pltpu.CompilerParams(dimension_semantics=sem, vmem_limit_bytes=VMEM_LIMIT, **kw)


def _my_place():
    return lax.axis_index("x"), lax.axis_index("y"), lax.axis_index("c")


def _peer(place, rel):
    x, y, c = place
    return (x ^ ((rel >> 2) & 1), y ^ ((rel >> 1) & 1), c ^ (rel & 1))


def _flat(place):
    x, y, c = place
    return 4 * x + 2 * y + c


def _gather_weights(w_in, w_out):
    def body(win_ref, wout_ref, gin_ref, gout_ref, send_sems, recv_sems):
        me = _my_place()
        x, y, c = me
        sibling = (x, y, 1 - c)
        chips = [(1 - x, y), (x, 1 - y), (1 - x, 1 - y)]

        def slab(which, place):
            idx = _flat(place)
            if which == 0:
                return gin_ref.at[idx]
            return gout_ref.at[pl.ds(pl.multiple_of(idx * WOUT_ROWS, WOUT_ROWS), WOUT_ROWS), :]

        def copy(which, k, block, to):
            ref = slab(which, block)
            return pltpu.make_async_remote_copy(
                src_ref=ref, dst_ref=ref, send_sem=send_sems.at[7 * which + k],
                recv_sem=recv_sems.at[7 * which + k], device_id=to, device_id_type=MESH)

        gin_ref[_flat(me)] = win_ref[...].astype(BF16)
        gout_ref[pl.ds(pl.multiple_of(_flat(me) * WOUT_ROWS, WOUT_ROWS), WOUT_ROWS), :] = (
            wout_ref[...].astype(BF16))

        started = []
        for which in (0, 1):
            first = [copy(which, 0, me, sibling)]
            first += [copy(which, 1 + j, me, (*chip, c)) for j, chip in enumerate(chips)]
            for cp in first:
                cp.start()
            started += first
        for which in (0, 1):
            for j, chip in enumerate(chips):
                copy(which, 1 + j, (*chip, c), me).wait_recv()
                fwd = copy(which, 4 + j, (*chip, c), sibling)
                fwd.start()
                started.append(fwd)
        for which in (0, 1):
            copy(which, 0, sibling, me).wait_recv()
            for j, chip in enumerate(chips):
                copy(which, 4 + j, (*chip, 1 - c), me).wait_recv()
        for cp in started:
            cp.wait_send()

    return pl.pallas_call(
        body, name="gather_weights",
        out_shape=(jax.ShapeDtypeStruct((N_DEV, D_MODEL, COL_BLOCK), BF16),
                   jax.ShapeDtypeStruct((D_MODEL, D_MODEL), BF16)),
        in_specs=[pl.BlockSpec(memory_space=pltpu.VMEM), pl.BlockSpec(memory_space=pltpu.VMEM)],
        out_specs=(pl.BlockSpec(memory_space=pltpu.VMEM), pl.BlockSpec(memory_space=pltpu.VMEM)),
        scratch_shapes=[pltpu.SemaphoreType.DMA((14,)), pltpu.SemaphoreType.DMA((14,))],
        compiler_params=_params(),
    )(w_in, w_out)


def _rope_tables(pos_col, inv_freq_lanes):
    tm = 512

    def body(pos_ref, invf_ref, c_ref, sa_ref, sb_ref):
        ang = pos_ref[...].astype(F32) * invf_ref[...]
        e = lax.broadcasted_iota(jnp.int32, (tm, LANES), 1) & (HEAD_DIM - 1)
        cos, sin = jnp.cos(ang), jnp.sin(ang)
        c_ref[...] = jnp.where(e < ROPE_DIMS, cos, 1.0)
        sa_ref[...] = jnp.where((e >= ROPE_HALF) & (e < ROPE_DIMS), sin, 0.0)
        sb_ref[...] = jnp.where(e < ROPE_HALF, -sin, 0.0)

    tab = jax.ShapeDtypeStruct((SEQ, LANES), F32)
    spec = pl.BlockSpec((tm, LANES), lambda i: (i, 0))
    return pl.pallas_call(
        body, name="rope_tables", grid=(SEQ // tm,), out_shape=(tab, tab, tab),
        in_specs=[pl.BlockSpec((tm, 1), lambda i: (i, 0)), pl.BlockSpec((1, LANES), lambda i: (0, 0))],
        out_specs=(spec, spec, spec), compiler_params=_params(("parallel",)),
    )(pos_col, inv_freq_lanes)


def _rot(t, c, sa, sb):
    n = t.shape[1]
    return t * c + pltpu.roll(t, ROPE_HALF, 1) * sa + pltpu.roll(t, n - ROPE_HALF, 1) * sb


def _rot_transposed(g, c, sa, sb):
    n = g.shape[1]
    return g * c + pltpu.roll(g * sa, n - ROPE_HALF, 1) + pltpu.roll(g * sb, ROPE_HALF, 1)


def _in_proj_fwd(x, mix_w, w_g, rc, rsa, rsb):
    tm = 256

    def body(x_ref, w_ref, wg_ref, c_ref, sa_ref, sb_ref, proj_ref, hnt_ref):
        xf = x_ref[...]
        ms = jnp.mean(xf * xf, axis=-1, keepdims=True)
        hn = xf * lax.rsqrt(ms + NORM_EPS) * w_ref[...]
        hnt_ref[...] = hn.T.astype(BF16)
        hb = hn.astype(BF16)
        c = jnp.tile(c_ref[...], (1, 4))
        sa = jnp.tile(sa_ref[...], (1, 4))
        sb = jnp.tile(sb_ref[...], (1, 4))
        for j in range(N_DEV):
            acc = _mm(hb, wg_ref[j])
            if j < 2:
                acc = _rot(acc, c, sa, sb)
            proj_ref[:, COL_BLOCK * j:COL_BLOCK * (j + 1)] = acc

    tab = pl.BlockSpec((tm, LANES), lambda i: (i, 0))
    return pl.pallas_call(
        body, name="in_proj_fwd", grid=(SEQ // tm,),
        out_shape=(jax.ShapeDtypeStruct((SEQ, IN_COLS), F32),
                   jax.ShapeDtypeStruct((D_MODEL, SEQ), BF16)),
        in_specs=[pl.BlockSpec((tm, D_MODEL), lambda i: (i, 0)),
                  pl.BlockSpec((1, D_MODEL), lambda i: (0, 0)),
                  pl.BlockSpec((N_DEV, D_MODEL, COL_BLOCK), lambda i: (0, 0, 0)),
                  tab, tab, tab],
        out_specs=(pl.BlockSpec((tm, IN_COLS), lambda i: (i, 0)),
                   pl.BlockSpec((D_MODEL, tm), lambda i: (0, i))),
        compiler_params=_params(("parallel",)),
    )(x, mix_w, w_g, rc, rsa, rsb)


ATTN_GROUP = 4
BLOCKS_PER_PATTERN = SEQ // ATTN_BLOCK


def _write_band_bias(bias_ref):
    qi = lax.broadcasted_iota(jnp.int32, (ATTN_BLOCK, 2 * ATTN_BLOCK), 0)
    kj = lax.broadcasted_iota(jnp.int32, (ATTN_BLOCK, 2 * ATTN_BLOCK), 1)
    bias_ref[0] = jnp.where((kj >= qi) & (kj <= qi + ATTN_BLOCK), 0.0, NEG_BIG)
    bias_ref[1] = jnp.where(kj <= qi, 0.0, NEG_BIG)


def _head0_lanes():
    return lax.broadcasted_iota(jnp.int32, (ATTN_BLOCK, LANES), 1) < HEAD_DIM


def _strided(start, size, d):
    return pl.ds(start, size) if d == 1 else pl.ds(start, size, stride=d)


def _block_place(i, d):
    nblk = BLOCKS_PER_PATTERN // d
    r, n = i // nblk, i % nblk
    kn = jnp.maximum(n - 1, 0)
    row0, key0 = n * (d * ATTN_BLOCK) + r, kn * (d * ATTN_BLOCK) + r
    if d == 1:
        row0, key0 = pl.multiple_of(row0, ATTN_BLOCK), pl.multiple_of(key0, ATTN_BLOCK)
    return _strided(row0, ATTN_BLOCK, d), _strided(key0, 2 * ATTN_BLOCK, d), (n == 0).astype(jnp.int32)


def _for_each_group(d, load, compute, store):
    def group(g, carry):
        items = [load(*_block_place(g * ATTN_GROUP + u, d)) for u in range(ATTN_GROUP)]
        results = [compute(item) for item in items]
        for item, res in zip(items, results):
            store(item, res)
        return carry

    lax.fori_loop(0, BLOCKS_PER_PATTERN // ATTN_GROUP, group, 0)


def _attn_fwd_fused(proj):
    n_pat = len(DILATIONS)
    tile = (ATTN_BLOCK, LANES)

    def body(q_ref, k_ref, v_ref, o_ref, lse_ref, m_acc, l_acc, bias_ref):
        _write_band_bias(bias_ref)
        h0 = _head0_lanes()
        for pi, d in enumerate(DILATIONS):
            first, last = pi == 0, pi == n_pat - 1

            def load(rows, keys, which, first=first):
                item = dict(rows=rows, keys=keys, which=which)
                if not first:
                    item.update(o=o_ref[rows, :], m=[m_acc.at[h][rows, :] for h in range(2)],
                                l=[l_acc.at[h][rows, :] for h in range(2)])
                return item

            def compute(item, first=first):
                res = []
                q = q_ref[item["rows"], :]
                kb = k_ref[item["keys"], :].astype(BF16)
                vb = v_ref[item["keys"], :].astype(BF16)
                for h in range(2):
                    hm = h0 if h == 0 else jnp.logical_not(h0)
                    qh = jnp.where(hm, q, 0.0).astype(BF16)
                    s = _mm_nt(qh, kb) * 0.125 + bias_ref[item["which"]]
                    mb = jnp.max(s, axis=-1, keepdims=True)
                    if first:
                        p = jnp.exp(s - mb)
                        mn = jnp.broadcast_to(mb, tile)
                    else:
                        mn = jnp.maximum(item["m"][h], mb)
                        alpha = jnp.exp(item["m"][h] - mn)
                        p = jnp.exp(s - jnp.concatenate([mn, mn], axis=1))
                    ls = jnp.sum(p, axis=-1, keepdims=True)
                    pv = _mm(p.astype(BF16), vb)
                    if first:
                        res.append((pv, mn, jnp.broadcast_to(ls, tile)))
                    else:
                        res.append((alpha * item["o"] + pv, mn, alpha * item["l"][h] + ls))
                return res

            def store(item, res, last=last):
                rows = item["rows"]
                (o0, m0, l0), (o1, m1, l1) = res
                if last:
                    o_ref[rows, :] = jnp.where(h0, o0 / l0, o1 / l1)
                    lse_ref[rows, :] = jnp.where(h0, m0 + jnp.log(l0), m1 + jnp.log(l1))
                else:
                    o_ref[rows, :] = jnp.where(h0, o0, o1)
                    m_acc.at[0][rows, :], m_acc.at[1][rows, :] = m0, m1
                    l_acc.at[0][rows, :], l_acc.at[1][rows, :] = l0, l1

            _for_each_group(d, load, compute, store)

    slab = lambda g: pl.BlockSpec((SEQ, LANES), functools.partial(lambda hp, g: (0, 4 * g + hp), g=g))
    wide = jax.ShapeDtypeStruct((SEQ, ATTN_WIDTH), F32)
    return pl.pallas_call(
        body, name="attn_fwd", grid=(4,), out_shape=(wide, wide),
        in_specs=[slab(0), slab(1), slab(2)], out_specs=(slab(0), slab(0)),
        scratch_shapes=[pltpu.VMEM((2, SEQ, LANES), F32), pltpu.VMEM((2, SEQ, LANES), F32),
                        pltpu.VMEM((2, ATTN_BLOCK, 2 * ATTN_BLOCK), F32)],
        compiler_params=_params(("parallel",)),
    )(proj, proj, proj)


def _attn_bwd_fused(proj, d_out, lse, delta):
    def body(q_ref, k_ref, v_ref, do_ref, lse_ref, del_ref, dq_ref, dk_ref, dv_ref, bias_ref):
        _write_band_bias(bias_ref)
        dk_ref[...] = jnp.zeros_like(dk_ref)
        dv_ref[...] = jnp.zeros_like(dv_ref)
        h0 = _head0_lanes()
        for pi, d in enumerate(DILATIONS):
            first = pi == 0

            def load(rows, keys, which):
                return dict(rows=rows, keys=keys, q=q_ref[rows, :], g=do_ref[rows, :], lse=lse_ref[rows, :],
                            delta=del_ref[rows, :], k=k_ref[keys, :].astype(BF16),
                            v=v_ref[keys, :].astype(BF16), bias=bias_ref[which])

            def per_head(t):
                swapped = pltpu.roll(t, HEAD_DIM, 1)
                a, b = jnp.where(h0, t, swapped), jnp.where(h0, swapped, t)
                return jnp.concatenate([a, a], axis=1), jnp.concatenate([b, b], axis=1)

            def compute(item):
                dq_h = []
                dk_c = jnp.zeros((2 * ATTN_BLOCK, LANES), F32)
                dv_c = jnp.zeros((2 * ATTN_BLOCK, LANES), F32)
                lse_h, delta_h = per_head(item["lse"]), per_head(item["delta"])
                for h in range(2):
                    hm = h0 if h == 0 else jnp.logical_not(h0)
                    qh = jnp.where(hm, item["q"], 0.0).astype(BF16)
                    gh = jnp.where(hm, item["g"], 0.0).astype(BF16)
                    s = _mm_nt(qh, item["k"]) * 0.125 + item["bias"]
                    p = jnp.exp(s - lse_h[h])
                    dp = _mm_nt(gh, item["v"])
                    ds = (p * (dp - delta_h[h]) * 0.125).astype(BF16)
                    dq_h.append(_mm(ds, item["k"]))
                    dk_c = dk_c + _mm_tn(ds, qh)
                    dv_c = dv_c + _mm_tn(p.astype(BF16), gh)
                return jnp.where(h0, dq_h[0], dq_h[1]), dk_c, dv_c

            def store(item, res, first=first):
                rows, keys = item["rows"], item["keys"]
                if first:
                    dq_ref[rows, :] = res[0]
                else:
                    dq_ref[rows, :] += res[0]
                dk_ref[keys, :] += res[1]
                dv_ref[keys, :] += res[2]

            _for_each_group(d, load, compute, store)

    slab = lambda g: pl.BlockSpec((SEQ, LANES), functools.partial(lambda hp, g: (0, 4 * g + hp), g=g))
    wide = jax.ShapeDtypeStruct((SEQ, ATTN_WIDTH), F32)
    return pl.pallas_call(
        body, name="attn_bwd", grid=(4,), out_shape=(wide, wide, wide),
        scratch_shapes=[pltpu.VMEM((2, ATTN_BLOCK, 2 * ATTN_BLOCK), F32)],
        in_specs=[slab(0), slab(1), slab(2), slab(0), slab(0), slab(0)], out_specs=(slab(0), slab(0), slab(0)),
        compiler_params=_params(("parallel",)),
    )(proj, proj, proj, d_out, lse, delta)


def _hgrn_lower_bound(lb_ref):
    r0, r1 = lb_ref[0:1, :], lb_ref[1:2, :]
    mx = jnp.maximum(r0, r1)
    e0, e1 = jnp.exp(r0 - mx), jnp.exp(r1 - mx)
    return e0 / (e0 + e1)


def _hgrn_gates(hq, hf, lb):
    sq = _sigmoid(hq)
    sg = _sigmoid(hf)
    f = lb + (1.0 - lb) * sg
    return hq * sq, sq, sg, f, 1.0 - f, jnp.log(f)


HGRN_PAIR = 2
HGRN_SEQ_BLOCK = 1024
HGRN_GROUP = 4
HGRN_ROWS = HGRN_GROUP * HGRN_CHUNK


def _hgrn_specs(reverse):
    n_blocks = SEQ // HGRN_SEQ_BLOCK
    width = HGRN_PAIR * HGRN_DIM
    blk = (lambda s: n_blocks - 1 - s) if reverse else (lambda s: s)
    cols = lambda g: pl.BlockSpec((HGRN_SEQ_BLOCK, width),
                                  functools.partial(lambda p, s, g: (blk(s), HGRN_PAIR * g + p), g=g))
    pair = pl.BlockSpec((HGRN_SEQ_BLOCK, width), lambda p, s: (blk(s), p))
    lb = pl.BlockSpec((2, width), lambda p, s: (0, p))
    states = pl.BlockSpec((HGRN_PAIR, HGRN_SEQ_BLOCK // HGRN_CHUNK, HGRN_DIM, HGRN_DIM),
                          lambda p, s: (p, blk(s), 0, 0))
    return cols, pair, lb, states


def _chunk_masks():
    ri = lax.broadcasted_iota(jnp.int32, (HGRN_ROWS, HGRN_ROWS), 0)
    ci = lax.broadcasted_iota(jnp.int32, (HGRN_ROWS, HGRN_ROWS), 1)
    same = (ri // HGRN_CHUNK) == (ci // HGRN_CHUNK)
    return same, same & (ri >= ci), same & (ri <= ci)


def _mm_select(sel, v):
    hi = v.astype(BF16)
    r1 = v - hi.astype(F32)
    mid = r1.astype(BF16)
    lo = (r1 - mid.astype(F32)).astype(BF16)
    return _mm(sel, hi) + _mm(sel, mid) + _mm(sel, lo)


def _head_cols(a, h):
    return a[:, HGRN_DIM * h:HGRN_DIM * (h + 1)]


def _hgrn_fwd(proj, lb_raw):
    t, rws = HGRN_CHUNK, HGRN_ROWS

    def body(hq_ref, hf_ref, hi_ref, lb_ref, rec_ref, st_ref, state):
        @pl.when(pl.program_id(1) == 0)
        def _():
            state[...] = jnp.zeros_like(state)

        lb = _hgrn_lower_bound(lb_ref)
        same, causal, _ = _chunk_masks()
        sel = jnp.concatenate([causal, same], axis=0).astype(BF16)

        def group(g, sts):
            rows = pl.ds(pl.multiple_of(g * rws, rws), rws)
            q, _, _, _, k, lf = _hgrn_gates(hq_ref[rows, :], hf_ref[rows, :], lb)
            sums = _mm_select(sel, lf)
            cum, last = sums[:rws], sums[rws:]
            qd = (q * jnp.exp(cum)).astype(BF16)
            ki = (k * jnp.exp(-cum)).astype(BF16)
            ke = (k * jnp.exp(last - cum)).astype(BF16)
            vb = hi_ref[rows, :].astype(BF16)
            dec = jnp.exp(last)
            new_sts, recs = [], []
            for h in range(HGRN_PAIR):
                qd_h, ke_h, vb_h = _head_cols(qd, h), _head_cols(ke, h), _head_cols(vb, h)
                att = jnp.where(causal, _mm_nt(qd_h, _head_cols(ki, h)), 0.0).astype(BF16)
                intra = _mm(att, vb_h)
                st = sts[h]
                outs = []
                for c in range(HGRN_GROUP):
                    sl = slice(c * t, (c + 1) * t)
                    st_ref[h, g * HGRN_GROUP + c] = st
                    outs.append(intra[sl] + _mm_nt(qd_h[sl], st.astype(BF16)))
                    st = st * _head_cols(dec[c * t:c * t + 1, :], h) + _mm_tn(vb_h[sl], ke_h[sl])
                new_sts.append(st)
                recs.append(jnp.concatenate(outs, axis=0))
            rec_ref[rows, :] = jnp.concatenate(recs, axis=1)
            return tuple(new_sts)

        sts = lax.fori_loop(0, HGRN_SEQ_BLOCK // rws, group, tuple(state[h] for h in range(HGRN_PAIR)))
        for h in range(HGRN_PAIR):
            state[h] = sts[h]

    cols, pair, lb, states = _hgrn_specs(reverse=False)
    return pl.pallas_call(
        body, name="hgrn_fwd", grid=(HGRN_HEADS // HGRN_PAIR, SEQ // HGRN_SEQ_BLOCK),
        out_shape=(jax.ShapeDtypeStruct((SEQ, HGRN_WIDTH), F32),
                   jax.ShapeDtypeStruct((HGRN_HEADS, N_CHUNKS, HGRN_DIM, HGRN_DIM), F32)),
        in_specs=[cols(4), cols(5), cols(6), lb], out_specs=(pair, states),
        scratch_shapes=[pltpu.VMEM((HGRN_PAIR, HGRN_DIM, HGRN_DIM), F32)],
        compiler_params=_params(("parallel", "arbitrary")),
    )(proj, proj, proj, lb_raw)


def _hgrn_bwd(proj, lb_raw, d_rec, states):
    t, rws = HGRN_CHUNK, HGRN_ROWS

    def body(hq_ref, hf_ref, hi_ref, lb_ref, do_ref, st_ref, dhq_ref, dhf_ref, dhi_ref, dlb_ref,
             dstate, dlb_acc):
        lb = _hgrn_lower_bound(lb_ref)
        same, causal, anti = _chunk_masks()
        sel = jnp.concatenate([causal, same], axis=0).astype(BF16)
        sel_t = jnp.concatenate([anti, same], axis=1).astype(BF16)
        @pl.when(pl.program_id(1) == 0)
        def _():
            dstate[...] = jnp.zeros_like(dstate)
            dlb_acc[...] = jnp.zeros_like(dlb_acc)

        n_groups = HGRN_SEQ_BLOCK // rws
        chunks = [slice(c * t, (c + 1) * t) for c in range(HGRN_GROUP)]

        def group(i, dsts_in):
            g = n_groups - 1 - i
            rows = pl.ds(pl.multiple_of(g * rws, rws), rws)
            hq = hq_ref[rows, :]
            q, sq, sg, f, k, lf = _hgrn_gates(hq, hf_ref[rows, :], lb)
            sums = _mm_select(sel, lf)
            cum, last = sums[:rws], sums[rws:]
            e_cum, e_inv, e_end, dec = jnp.exp(cum), jnp.exp(-cum), jnp.exp(last - cum), jnp.exp(last)
            qd, ki, ke = q * e_cum, k * e_inv, k * e_end
            qdb, kib, keb = qd.astype(BF16), ki.astype(BF16), ke.astype(BF16)
            vb = hi_ref[rows, :].astype(BF16)
            gb = do_ref[rows, :].astype(BF16)

            dsts_out, per_head = [], []
            for h in range(HGRN_PAIR):
                qdb_h, kib_h, keb_h = _head_cols(qdb, h), _head_cols(kib, h), _head_cols(keb, h)
                vb_h, gb_h = _head_cols(vb, h), _head_cols(gb, h)
                att = jnp.where(causal, _mm_nt(qdb_h, kib_h), 0.0).astype(BF16)
                datt = jnp.where(causal, _mm_nt(gb_h, vb_h), 0.0).astype(BF16)
                dv = _mm_tn(att, gb_h)
                dqd = _mm(datt, kib_h)
                dki = _mm_tn(datt, qdb_h)

                decs = [_head_cols(dec[c * t:c * t + 1, :], h) for c in range(HGRN_GROUP)]
                dsts = [None] * HGRN_GROUP
                dst = dsts_in[h]
                for c in reversed(range(HGRN_GROUP)):
                    dsts[c] = dst
                    dst = dst * decs[c] + _mm_tn(gb_h[chunks[c]], qdb_h[chunks[c]])
                dsts_out.append(dst)

                dv_x, dqd_x, dke, dlast_x = [], [], [], []
                for c, sl in enumerate(chunks):
                    st_prev = st_ref[h, g * HGRN_GROUP + c]
                    dstb = dsts[c].astype(BF16)
                    dv_x.append(_mm_nt(keb_h[sl], dstb))
                    dqd_x.append(_mm(gb_h[sl], st_prev.astype(BF16)))
                    dke.append(_mm(vb_h[sl], dstb))
                    ddec = jnp.sum(dsts[c] * st_prev, axis=0, keepdims=True)
                    dlast_x.append(jnp.broadcast_to(ddec * decs[c], (t, HGRN_DIM)))
                per_head.append((dv + jnp.concatenate(dv_x, axis=0), dqd + jnp.concatenate(dqd_x, axis=0),
                                 dki, jnp.concatenate(dke, axis=0), jnp.concatenate(dlast_x, axis=0)))
            dv, dqd, dki, dke, dlast = (jnp.concatenate([a, b], axis=1) for a, b in zip(*per_head))

            dq = dqd * e_cum
            dk = dki * e_inv + dke * e_end
            dke_ke = dke * ke
            dcum = dqd * qd - dki * ki - dke_ke
            dlf = _mm_select(sel_t, jnp.concatenate([dcum, dke_ke], axis=0)) + dlast
            df = dlf / f - dk
            dhq_ref[rows, :] = dq * (sq * (1.0 + hq * (1.0 - sq)))
            dhf_ref[rows, :] = df * (1.0 - lb) * (sg * (1.0 - sg))
            dhi_ref[rows, :] = dv
            dlb_acc[...] += jnp.sum(df * (1.0 - sg), axis=0, keepdims=True)
            return tuple(dsts_out)

        dsts = lax.fori_loop(0, n_groups, group, tuple(dstate[h] for h in range(HGRN_PAIR)))
        for h in range(HGRN_PAIR):
            dstate[h] = dsts[h]
        g0 = dlb_acc[...] * lb * (1.0 - lb)
        dlb_ref[...] = jnp.concatenate([g0, -g0], axis=0)

    cols, pair, lb_spec, st_spec = _hgrn_specs(reverse=True)
    wide = jax.ShapeDtypeStruct((SEQ, HGRN_WIDTH), F32)
    return pl.pallas_call(
        body, name="hgrn_bwd", grid=(HGRN_HEADS // HGRN_PAIR, SEQ // HGRN_SEQ_BLOCK),
        out_shape=(wide, wide, wide, jax.ShapeDtypeStruct((2, HGRN_WIDTH), F32)),
        in_specs=[cols(4), cols(5), cols(6), lb_spec, pair, st_spec],
        out_specs=(pair, pair, pair, lb_spec),
        scratch_shapes=[pltpu.VMEM((HGRN_PAIR, HGRN_DIM, HGRN_DIM), F32),
                        pltpu.VMEM((1, HGRN_PAIR * HGRN_DIM), F32)],
        compiler_params=_params(("parallel", "arbitrary")),
    )(proj, proj, proj, lb_raw, d_rec, states)


def _group_sum(v, group):
    parts = []
    for s in range(v.shape[1] // LANES):
        slab = v[:, LANES * s:LANES * (s + 1)]
        if group == LANES:
            parts.append(jnp.broadcast_to(jnp.sum(slab, axis=-1, keepdims=True), slab.shape))
        else:
            h0 = lax.broadcasted_iota(jnp.int32, slab.shape, 1) < HEAD_DIM
            s0 = jnp.sum(jnp.where(h0, slab, 0.0), axis=-1, keepdims=True)
            s1 = jnp.sum(jnp.where(h0, 0.0, slab), axis=-1, keepdims=True)
            parts.append(jnp.where(h0, s0, s1))
    return jnp.concatenate(parts, axis=1)


def _mid(attn_o, rec, proj, x, target, w_out_g, attn_w, hgrn_w, final_w):
    tm = 256

    def branch_fwd(o, gate, w, group):
        r = lax.rsqrt(_group_sum(o * o, group) * (1.0 / group) + NORM_EPS)
        nrm = o * r
        sg = _sigmoid(gate)
        return r, nrm, sg, nrm * w * (gate * sg)

    def branch_bwd(dy, r, nrm, sg, gate, w, group):
        silu = gate * sg
        d_gate = dy * nrm * w * (sg * (1.0 + gate * (1.0 - sg)))
        d_w = jnp.sum(dy * nrm * silu, axis=0, keepdims=True)
        dn = dy * w * silu
        d_o = r * (dn - nrm * (_group_sum(dn * nrm, group) * (1.0 / group)))
        return d_o, d_gate, d_w

    def body(o_ref, rec_ref, ag_ref, hg_ref, x_ref, tgt_ref, wout_ref, aw_ref, hw_ref, fw_ref,
             dx2_ref, do_ref, delta_ref, dag_ref, drec_ref, dhg_ref, dwout_ref, dfw_ref, daw_ref, dhw_ref,
             loss_ref, dwout_acc):
        i = pl.program_id(0)

        @pl.when(i == 0)
        def _():
            dwout_acc[...] = jnp.zeros_like(dwout_acc)
            dfw_ref[...] = jnp.zeros_like(dfw_ref)
            daw_ref[...] = jnp.zeros_like(daw_ref)
            dhw_ref[...] = jnp.zeros_like(dhw_ref)
            loss_ref[...] = jnp.zeros_like(loss_ref)

        o, rc, ag, hg = o_ref[...], rec_ref[...], ag_ref[...], hg_ref[...]
        aw, hw, fw = aw_ref[...], hw_ref[...], fw_ref[...]
        ra, na, sga, ya = branch_fwd(o, ag, aw, HEAD_DIM)
        rh, nh, sgh, yh = branch_fwd(rc, hg, hw, HGRN_DIM)
        mixed = jnp.concatenate([ya, yh], axis=1).astype(BF16)
        wout = wout_ref[...]
        x2 = x_ref[...] + _mm(mixed, wout)
        rstd = lax.rsqrt(jnp.mean(x2 * x2, axis=-1, keepdims=True) + NORM_EPS)
        xn = x2 * rstd
        err = xn * fw - tgt_ref[...]
        row_loss = jnp.mean(err * err, axis=-1, keepdims=True)
        loss_ref[...] += 0.5 * jnp.sum(row_loss, axis=0, keepdims=True)
        dy = err * (1.0 / D_MODEL)
        dfw_ref[...] += jnp.sum(dy * xn, axis=0, keepdims=True)
        dxn = dy * fw
        dx2 = rstd * (dxn - xn * jnp.mean(dxn * xn, axis=-1, keepdims=True))
        dx2_ref[...] = dx2
        dx2b = dx2.astype(BF16)
        dwout_acc[...] += _mm_tn(mixed, dx2b)

        @pl.when(i == pl.num_programs(0) - 1)
        def _():
            dwout_ref[...] = dwout_acc[...].astype(BF16)

        dmixed = _mm_nt(dx2b, wout)

        d_o, d_ag, d_aw = branch_bwd(dmixed[:, :ATTN_WIDTH], ra, na, sga, ag, aw, HEAD_DIM)
        d_rec, d_hg, d_hw = branch_bwd(dmixed[:, ATTN_WIDTH:], rh, nh, sgh, hg, hw, HGRN_DIM)
        do_ref[...] = d_o
        delta_ref[...] = _group_sum(d_o * o, HEAD_DIM)
        dag_ref[...] = d_ag
        drec_ref[...] = d_rec
        dhg_ref[...] = d_hg
        daw_ref[...] += d_aw
        dhw_ref[...] += d_hw

    half = lambda: pl.BlockSpec((tm, COL_BLOCK), lambda i: (i, 0))
    full = lambda: pl.BlockSpec((tm, D_MODEL), lambda i: (i, 0))
    fixed = lambda r, c: pl.BlockSpec((r, c), lambda i: (0, 0))
    wide = jax.ShapeDtypeStruct((SEQ, COL_BLOCK), F32)
    return pl.pallas_call(
        body, name="mid", grid=(SEQ // tm,),
        out_shape=(jax.ShapeDtypeStruct((SEQ, D_MODEL), F32), wide, wide, wide, wide, wide,
                   jax.ShapeDtypeStruct((D_MODEL, D_MODEL), BF16),
                   jax.ShapeDtypeStruct((1, D_MODEL), F32), jax.ShapeDtypeStruct((1, COL_BLOCK), F32),
                   jax.ShapeDtypeStruct((1, COL_BLOCK), F32), jax.ShapeDtypeStruct((1, 1), F32)),
        scratch_shapes=[pltpu.VMEM((D_MODEL, D_MODEL), F32)],
        in_specs=[half(), half(),
                  pl.BlockSpec((tm, COL_BLOCK), lambda i: (i, 3)), pl.BlockSpec((tm, COL_BLOCK), lambda i: (i, 7)),
                  full(), full(), fixed(D_MODEL, D_MODEL), fixed(1, COL_BLOCK), fixed(1, COL_BLOCK),
                  fixed(1, D_MODEL)],
        out_specs=(full(), half(), half(), half(), half(), half(), fixed(D_MODEL, D_MODEL),
                   fixed(1, D_MODEL), fixed(1, COL_BLOCK), fixed(1, COL_BLOCK), fixed(1, 1)),
        compiler_params=_params(("arbitrary",)),
    )(attn_o, rec, proj, proj, x, target, w_out_g, attn_w, hgrn_w, final_w)


def _in_proj_bwd(d_groups, hn_t, w_g, x, dx2, mix_w, rc, rsa, rsb):
    tm = 256
    n_tiles = SEQ // tm
    last_j = N_DEV - 1

    def body(*refs):
        dg_refs = refs[:N_DEV]
        hnt_ref, wg_ref, x_ref, dx2_ref, w_ref, c_ref, sa_ref, sb_ref = refs[N_DEV:N_DEV + 8]
        gx_ref, dwin_ref, dmw_ref, dhn, dw_acc = refs[N_DEV + 8:]
        j, i = pl.program_id(0), pl.program_id(1)
        rows = pl.ds(pl.multiple_of(i * tm, tm), tm)

        for jj in range(N_DEV):
            @pl.when(j == jj)
            def _(jj=jj):
                dp = dg_refs[jj][...]
                if jj < 2:
                    dp = _rot_transposed(dp, jnp.tile(c_ref[...], (1, 4)), jnp.tile(sa_ref[...], (1, 4)),
                                         jnp.tile(sb_ref[...], (1, 4)))
                dpb = dp.astype(BF16)
                contrib = _mm_nt(dpb, wg_ref[0])
                if jj == 0:
                    dhn[rows, :] = contrib
                else:
                    dhn[rows, :] += contrib
                dw = _mm(hnt_ref[...], dpb)

                @pl.when(i == 0)
                def _():
                    dw_acc[...] = dw

                @pl.when(i > 0)
                def _():
                    dw_acc[...] += dw

                @pl.when(i == n_tiles - 1)
                def _():
                    dwin_ref[0] = dw_acc[...].astype(BF16)

        @pl.when((j == last_j) & (i == 0))
        def _():
            dmw_ref[...] = jnp.zeros_like(dmw_ref)

        @pl.when(j == last_j)
        def _():
            xf = x_ref[...]
            w = w_ref[...]
            rstd = lax.rsqrt(jnp.mean(xf * xf, axis=-1, keepdims=True) + NORM_EPS)
            xn = xf * rstd
            g = dhn[rows, :]
            dmw_ref[...] += jnp.sum(g * xn, axis=0, keepdims=True)
            gw = g * w
            gx_ref[...] = dx2_ref[...] + rstd * (gw - xn * jnp.mean(gw * xn, axis=-1, keepdims=True))

    def group_spec(jj):
        return pl.BlockSpec((tm, COL_BLOCK), functools.partial(lambda j, i, jj: (jnp.where(j == jj, i, 0), 0), jj=jj))

    at_end = lambda cols: pl.BlockSpec((tm, cols), lambda j, i: (jnp.where(j == last_j, i, 0), 0))
    tab = lambda: pl.BlockSpec((tm, LANES), lambda j, i: (jnp.where(j < 2, i, 0), 0))
    return pl.pallas_call(
        body, name="in_proj_bwd", grid=(N_DEV, n_tiles),
        out_shape=(jax.ShapeDtypeStruct((SEQ, D_MODEL), F32),
                   jax.ShapeDtypeStruct((N_DEV, D_MODEL, COL_BLOCK), BF16),
                   jax.ShapeDtypeStruct((1, D_MODEL), F32)),
        in_specs=[group_spec(jj) for jj in range(N_DEV)] + [
            pl.BlockSpec((D_MODEL, tm), lambda j, i: (0, i)),
            pl.BlockSpec((1, D_MODEL, COL_BLOCK), lambda j, i: (j, 0, 0)),
            at_end(D_MODEL), at_end(D_MODEL), pl.BlockSpec((1, D_MODEL), lambda j, i: (0, 0)),
            tab(), tab(), tab()],
        out_specs=(at_end(D_MODEL), pl.BlockSpec((1, D_MODEL, COL_BLOCK), lambda j, i: (j, 0, 0)),
                   pl.BlockSpec((1, D_MODEL), lambda j, i: (0, 0))),
        scratch_shapes=[pltpu.VMEM((SEQ, D_MODEL), F32), pltpu.VMEM((D_MODEL, COL_BLOCK), F32)],
        compiler_params=_params(("arbitrary", "arbitrary")),
    )(*d_groups, hn_t, w_g, x, dx2, mix_w, rc, rsa, rsb)


def _adamw(w, g, m, v):
    m = ADAM_B1 * m + (1.0 - ADAM_B1) * g
    v = ADAM_B2 * v + (1.0 - ADAM_B2) * (g * g)
    m_hat = m / (1.0 - ADAM_B1 ** ADAM_STEP)
    v_hat = v / (1.0 - ADAM_B2 ** ADAM_STEP)
    delta = -ADAM_LR * (m_hat / (jnp.sqrt(v_hat) + ADAM_EPS) + ADAM_WD * w)
    return delta, m, v


def _exchange_update(dwin_p, dwout_p, small_p, w_in, m_in, v_in, w_out, m_out, v_out, w_s, m_s, v_s):
    rb = 128
    n_chips = N_DEV // 2
    S1_IN, S1_OUT, SMALL, S2_IN, S2_OUT = 0, 4, 8, 15, 18

    def body(dwin_hbm, dwout_hbm, small_ref, win_ref, min_ref, vin_ref, wout_ref, mout_ref, vout_ref,
             ws_ref, ms_ref, vs_ref,
             gin_ref, din_ref, nmin_ref, nvin_ref, gout_ref, dout_ref, nmout_ref, nvout_ref,
             gs_ref, ds_ref, nms_ref, nvs_ref,
             own_in, own_out, s1_in, s1_out, fwd_in, fwd_out, s2_in, s2_out, land_s,
             send_sems, recv_sems, local_sems):
        me = _my_place()
        x, y, c = me
        my_chip = 2 * x + y
        sibling = (x, y, 1 - c)

        def remote(slot, src, dst, to):
            return pltpu.make_async_remote_copy(src_ref=src, dst_ref=dst, send_sem=send_sems.at[slot],
                                                recv_sem=recv_sems.at[slot], device_id=to, device_id_type=MESH)

        def stage1(q):
            return [remote(S1_IN + q, dwin_hbm.at[q, 1 - c], s1_in.at[q], sibling),
                    remote(S1_OUT + q, dwout_hbm.at[q, 1 - c], s1_out.at[q], sibling)]

        def stage2(rel):
            peer = _peer(me, 2 * rel)
            return [remote(S2_IN + rel - 1, fwd_in.at[rel - 1], s2_in.at[rel - 1], peer),
                    remote(S2_OUT + rel - 1, fwd_out.at[rel - 1], s2_out.at[rel - 1], peer)]

        def small_copy(rel):
            return remote(SMALL + rel - 1, small_ref, land_s.at[rel], _peer(me, rel))

        mine = [pltpu.make_async_copy(dwin_hbm.at[:, c], own_in, local_sems.at[0]),
                pltpu.make_async_copy(dwout_hbm.at[:, c], own_out, local_sems.at[1])]
        for cp in mine:
            cp.start()
        sent = []
        for q in range(n_chips):
            sent += stage1(q)
        land_s[0] = small_ref[...]
        sent += [small_copy(rel) for rel in range(1, N_DEV)]
        for cp in sent:
            cp.start()
        for cp in mine:
            cp.wait()

        def add_blocks(q, own, got, n_rows, dst):
            def step(b, carry):
                rows = pl.ds(pl.multiple_of(b * rb, rb), rb)
                dst[rows, :] = (own[q, rows, :].astype(F32) + got[q, rows, :].astype(F32)).astype(dst.dtype)
                return carry
            lax.fori_loop(0, n_rows // rb, step, 0)

        for rel in range(1, n_chips):
            q = my_chip ^ rel
            for cp in stage1(q):
                cp.wait_recv()
            add_blocks(q, own_in, s1_in, D_MODEL, fwd_in.at[rel - 1])
            add_blocks(q, own_out, s1_out, WOUT_ROWS, fwd_out.at[rel - 1])
            for cp in stage2(rel):
                cp.start()
                sent.append(cp)
        for cp in stage1(my_chip):
            cp.wait_recv()
        add_blocks(my_chip, own_in, s1_in, D_MODEL, gin_ref)
        add_blocks(my_chip, own_out, s1_out, WOUT_ROWS, gout_ref)
        for rel in range(1, n_chips):
            for cp in stage2(rel):
                cp.wait_recv()

        def update(got, w_ref, m_ref, v_ref, g_ref, d_ref, nm_ref, nv_ref, n_rows):
            def step(b, carry):
                rows = pl.ds(pl.multiple_of(b * rb, rb), rb)
                g = g_ref[rows, :]
                for rel in range(1, n_chips):
                    g = g + got[rel - 1, rows, :].astype(F32)
                delta, nm, nv = _adamw(w_ref[rows, :], g, m_ref[rows, :], v_ref[rows, :])
                g_ref[rows, :] = g
                d_ref[rows, :] = delta
                nm_ref[rows, :] = nm
                nv_ref[rows, :] = nv
                return carry
            lax.fori_loop(0, n_rows // rb, step, 0)

        update(s2_in, win_ref, min_ref, vin_ref, gin_ref, din_ref, nmin_ref, nvin_ref, D_MODEL)
        update(s2_out, wout_ref, mout_ref, vout_ref, gout_ref, dout_ref, nmout_ref, nvout_ref, WOUT_ROWS)

        for rel in range(1, N_DEV):
            small_copy(rel).wait_recv()
        my_flat = _flat(me)
        g = land_s[my_flat ^ 0]
        for dev in range(1, N_DEV):
            g = g + land_s[my_flat ^ dev]
        delta, nm, nv = _adamw(ws_ref[...], g, ms_ref[...], vs_ref[...])
        gs_ref[...] = g
        ds_ref[...] = delta
        nms_ref[...] = nm
        nvs_ref[...] = nv
        for cp in sent:
            cp.wait_send()

    vm = lambda: pl.BlockSpec(memory_space=pltpu.VMEM)
    anyspace = lambda: pl.BlockSpec(memory_space=pl.ANY)
    big = jax.ShapeDtypeStruct((D_MODEL, COL_BLOCK), F32)
    flat = jax.ShapeDtypeStruct((WOUT_ROWS, D_MODEL), F32)
    small = jax.ShapeDtypeStruct((SMALL_ROWS, LANES), F32)
    in_blocks = lambda n: pltpu.VMEM((n, D_MODEL, COL_BLOCK), BF16)
    out_blocks = lambda n: pltpu.VMEM((n, WOUT_ROWS, D_MODEL), BF16)
    return pl.pallas_call(
        body, name="exchange_update",
        out_shape=tuple([big] * 4 + [flat] * 4 + [small] * 4),
        in_specs=[anyspace(), anyspace()] + [vm() for _ in range(10)],
        out_specs=tuple(vm() for _ in range(12)),
        scratch_shapes=[in_blocks(n_chips), out_blocks(n_chips), in_blocks(n_chips), out_blocks(n_chips),
                        in_blocks(n_chips - 1), out_blocks(n_chips - 1),
                        in_blocks(n_chips - 1), out_blocks(n_chips - 1),
                        pltpu.VMEM((N_DEV, SMALL_ROWS, LANES), F32),
                        pltpu.SemaphoreType.DMA((21,)), pltpu.SemaphoreType.DMA((21,)),
                        pltpu.SemaphoreType.DMA((2,))],
        compiler_params=_params(),
    )(dwin_p.reshape(n_chips, 2, D_MODEL, COL_BLOCK), dwout_p.reshape(n_chips, 2, WOUT_ROWS, D_MODEL),
      small_p, w_in, m_in, v_in, w_out, m_out, v_out, w_s, m_s, v_s)


def _pack_small(mix, attn, hgrn, lb, final, loss=None):
    def rows8(a):
        a = a.reshape(-1, LANES)
        return jnp.pad(a, ((0, 8 - a.shape[0]), (0, 0)))
    last = jnp.zeros((8, LANES), F32) if loss is None else jnp.pad(loss.reshape(1, 1), ((0, 7), (0, LANES - 1)))
    return jnp.concatenate([rows8(mix), rows8(attn), rows8(hgrn), rows8(lb), rows8(final), last], axis=0)


def _unpack_small(slab):
    return (slab[ROW_MIX:ROW_MIX + 8].reshape(1, D_MODEL), slab[ROW_ATTN:ROW_ATTN + 4].reshape(1, ATTN_WIDTH),
            slab[ROW_HGRN:ROW_HGRN + 4].reshape(1, HGRN_WIDTH), slab[ROW_LB:ROW_LB + 8].reshape(2, HGRN_WIDTH),
            slab[ROW_FINAL:ROW_FINAL + 8].reshape(D_MODEL))


def _local_step(x, pos_col, w_in_g, w_out_g, mix_w, attn_w, hgrn_w, lb_raw, final_w, target):
    inv = ROPE_THETA ** (-jnp.arange(ROPE_HALF, dtype=F32) * (2.0 / ROPE_DIMS))
    lane_e = jnp.arange(LANES) % HEAD_DIM
    inv_lanes = jnp.where(lane_e < ROPE_DIMS, inv[lane_e % ROPE_HALF], 0.0).reshape(1, LANES)
    rc, rsa, rsb = _rope_tables(pos_col, inv_lanes)

    proj, hn_t = _in_proj_fwd(x, mix_w, w_in_g, rc, rsa, rsb)
    attn_o, lse = _attn_fwd_fused(proj)
    rec, states = _hgrn_fwd(proj, lb_raw)

    (dx2, d_o, delta, d_ag, d_rec, d_hg, dwout_p, d_final, d_attn_w, d_hgrn_w, loss) = _mid(
        attn_o, rec, proj, x, target, w_out_g, attn_w, hgrn_w, final_w.reshape(1, D_MODEL))

    dqkv = _attn_bwd_fused(proj, d_o, lse, delta)
    d_hq, d_hf, d_hi, d_lb = _hgrn_bwd(proj, lb_raw, d_rec, states)

    grad_x, dwin_p, d_mix = _in_proj_bwd(
        (dqkv[0], dqkv[1], dqkv[2], d_ag, d_hq, d_hf, d_hi, d_hg), hn_t, w_in_g, x, dx2, mix_w, rc, rsa, rsb)
    small_p = _pack_small(d_mix, d_attn_w, d_hgrn_w, d_lb, d_final, loss)
    return grad_x, dwin_p, dwout_p, small_p


def kernel(x, positions, w_in, w_out, mix_norm_w, attn_out_norm_w, hgrn_out_norm_w, hgrn_lb_raw, final_norm_w, loss_target, m_w_in, m_w_out, m_mix_norm_w, m_attn_out_norm_w, m_hgrn_out_norm_w, m_hgrn_lb_raw, m_final_norm_w, v_w_in, v_w_out, v_mix_norm_w, v_attn_out_norm_w, v_hgrn_out_norm_w, v_hgrn_lb_raw, v_final_norm_w):
    w_in_g, w_out_g = _gather_weights(w_in[0], w_out[0])
    grad_x, dwin_p, dwout_p, small_p = _local_step(
        x[0], positions.reshape(SEQ, 1), w_in_g, w_out_g, mix_norm_w, attn_out_norm_w, hgrn_out_norm_w,
        hgrn_lb_raw, final_norm_w, loss_target[0])

    w_s = _pack_small(mix_norm_w, attn_out_norm_w, hgrn_out_norm_w, hgrn_lb_raw, final_norm_w)
    m_s = _pack_small(m_mix_norm_w, m_attn_out_norm_w, m_hgrn_out_norm_w, m_hgrn_lb_raw, m_final_norm_w)
    v_s = _pack_small(v_mix_norm_w, v_attn_out_norm_w, v_hgrn_out_norm_w, v_hgrn_lb_raw, v_final_norm_w)
    (g_in, d_in, nm_in, nv_in, g_out, d_out, nm_out, nv_out, g_s, d_s, nm_s, nv_s) = _exchange_update(
        dwin_p, dwout_p, small_p, w_in[0], m_w_in[0], v_w_in[0], w_out[0], m_w_out[0], v_w_out[0], w_s, m_s, v_s)

    loss = g_s[ROW_LOSS, 0]
    return (loss, grad_x[None], g_in[None], g_out[None], *_unpack_small(g_s),
            d_in[None], d_out[None], *_unpack_small(d_s),
            nm_in[None], nm_out[None], *_unpack_small(nm_s),
            nv_in[None], nv_out[None], *_unpack_small(nv_s))
```

```python
import functools

import jax
import jax.numpy as jnp
from jax import lax
from jax.experimental import pallas as pl
from jax.experimental.pallas import tpu as pltpu

F32 = jnp.float32
BF16 = jnp.bfloat16

SEQ = 4096
D_MODEL = 1024
ATTN_WIDTH = 512
HGRN_WIDTH = 512
HEAD_DIM = 64
HGRN_HEADS = 4
HGRN_DIM = 128
HGRN_CHUNK = 64
N_CHUNKS = SEQ // HGRN_CHUNK
IN_COLS = 4096
COL_BLOCK = 512
N_DEV = 8
WOUT_ROWS = D_MODEL // N_DEV
ATTN_BLOCK = 128
DILATIONS = (1, 4, 16)
ROPE_THETA = 500000.0
ROPE_DIMS = 16
ROPE_HALF = 8
NORM_EPS = 1e-6
NEG_BIG = -1e30
LANES = 128

ADAM_LR = 0.001
ADAM_B1 = 0.9
ADAM_B2 = 0.999
ADAM_EPS = 1e-08
ADAM_WD = 0.01
ADAM_STEP = 10

SMALL_ROWS = 48
ROW_MIX, ROW_ATTN, ROW_HGRN, ROW_LB, ROW_FINAL, ROW_LOSS = 0, 8, 16, 24, 32, 40

VMEM_LIMIT = 56 * 1024 * 1024
MESH = pl.DeviceIdType.MESH


def _mm(a, b):
    return lax.dot_general(a, b, (((1,), (0,)), ((), ())), preferred_element_type=F32)


def _mm_nt(a, b):
    return lax.dot_general(a, b, (((1,), (1,)), ((), ())), preferred_element_type=F32)


def _mm_tn(a, b):
    return lax.dot_general(a, b, (((0,), (0,)), ((), ())), preferred_element_type=F32)


def _mm_exact(a, b):
    return lax.dot_general(a, b, (((1,), (0,)), ((), ())), preferred_element_type=F32,
                           precision=lax.Precision.HIGHEST)


def _sigmoid(v):
    return 1.0 / (1.0 + jnp.exp(-v))


def _params(sem=None, **kw):
    return pltpu.CompilerParams(dimension_semantics=sem, vmem_limit_bytes=VMEM_LIMIT, **kw)


def _my_place():
    return lax.axis_index("x"), lax.axis_index("y"), lax.axis_index("c")


def _peer(place, rel):
    x, y, c = place
    return (x ^ ((rel >> 2) & 1), y ^ ((rel >> 1) & 1), c ^ (rel & 1))


def _flat(place):
    x, y, c = place
    return 4 * x + 2 * y + c


def _gather_weights(w_in, w_out):
    def body(win_ref, wout_ref, gin_ref, gout_ref, stage, send_sems, recv_sems, local_sem):
        me = _my_place()
        x, y, c = me
        sibling = (x, y, 1 - c)
        chips = [(1 - x, y), (x, 1 - y), (1 - x, 1 - y)]

        def slab(which, place):
            idx = _flat(place)
            if which == 0:
                return gin_ref.at[:, pl.ds(pl.multiple_of(idx * COL_BLOCK, COL_BLOCK), COL_BLOCK)]
            return gout_ref.at[pl.ds(pl.multiple_of(idx * WOUT_ROWS, WOUT_ROWS), WOUT_ROWS), :]

        def copy(which, k, block, to, src=None):
            ref = slab(which, block)
            return pltpu.make_async_remote_copy(
                src_ref=ref if src is None else src, dst_ref=ref, send_sem=send_sems.at[7 * which + k],
                recv_sem=recv_sems.at[7 * which + k], device_id=to, device_id_type=MESH)

        stage[...] = win_ref[...].astype(BF16)
        own = pltpu.make_async_copy(stage, slab(0, me), local_sem)
        own.start()
        gout_ref[pl.ds(pl.multiple_of(_flat(me) * WOUT_ROWS, WOUT_ROWS), WOUT_ROWS), :] = (
            wout_ref[...].astype(BF16))

        started = []
        for which in (0, 1):
            src = stage if which == 0 else None
            first = [copy(which, 0, me, sibling, src)]
            first += [copy(which, 1 + j, me, (*chip, c), src) for j, chip in enumerate(chips)]
            for cp in first:
                cp.start()
            started += first
        for which in (0, 1):
            for j, chip in enumerate(chips):
                copy(which, 1 + j, (*chip, c), me).wait_recv()
                fwd = copy(which, 4 + j, (*chip, c), sibling)
                fwd.start()
                started.append(fwd)
        for which in (0, 1):
            copy(which, 0, sibling, me).wait_recv()
            for j, chip in enumerate(chips):
                copy(which, 4 + j, (*chip, 1 - c), me).wait_recv()
        for cp in started:
            cp.wait_send()
        own.wait()

    return pl.pallas_call(
        body, name="gather_weights",
        out_shape=(jax.ShapeDtypeStruct((D_MODEL, IN_COLS), BF16),
                   jax.ShapeDtypeStruct((D_MODEL, D_MODEL), BF16)),
        in_specs=[pl.BlockSpec(memory_space=pltpu.VMEM), pl.BlockSpec(memory_space=pltpu.VMEM)],
        out_specs=(pl.BlockSpec(memory_space=pltpu.VMEM), pl.BlockSpec(memory_space=pltpu.VMEM)),
        scratch_shapes=[pltpu.VMEM((D_MODEL, COL_BLOCK), BF16),
                        pltpu.SemaphoreType.DMA((14,)), pltpu.SemaphoreType.DMA((14,)),
                        pltpu.SemaphoreType.DMA(())],
        compiler_params=_params(),
    )(w_in, w_out)


def _rope_tables(pos_col, inv_freq_lanes):
    tm = 512

    def body(pos_ref, invf_ref, c_ref, sa_ref, sb_ref):
        ang = pos_ref[...].astype(F32) * invf_ref[...]
        e = lax.broadcasted_iota(jnp.int32, (tm, LANES), 1) & (HEAD_DIM - 1)
        cos, sin = jnp.cos(ang), jnp.sin(ang)
        c_ref[...] = jnp.where(e < ROPE_DIMS, cos, 1.0)
        sa_ref[...] = jnp.where((e >= ROPE_HALF) & (e < ROPE_DIMS), sin, 0.0)
        sb_ref[...] = jnp.where(e < ROPE_HALF, -sin, 0.0)

    tab = jax.ShapeDtypeStruct((SEQ, LANES), F32)
    spec = pl.BlockSpec((tm, LANES), lambda i: (i, 0))
    return pl.pallas_call(
        body, name="rope_tables", grid=(SEQ // tm,), out_shape=(tab, tab, tab),
        in_specs=[pl.BlockSpec((tm, 1), lambda i: (i, 0)), pl.BlockSpec((1, LANES), lambda i: (0, 0))],
        out_specs=(spec, spec, spec), compiler_params=_params(("parallel",)),
    )(pos_col, inv_freq_lanes)


def _rot(t, c, sa, sb):
    n = t.shape[1]
    return t * c + pltpu.roll(t, ROPE_HALF, 1) * sa + pltpu.roll(t, n - ROPE_HALF, 1) * sb


def _rot_transposed(g, c, sa, sb):
    n = g.shape[1]
    return g * c + pltpu.roll(g * sa, n - ROPE_HALF, 1) + pltpu.roll(g * sb, ROPE_HALF, 1)


def _in_proj_fwd(x, mix_w, w_g, rc, rsa, rsb):
    tm = 256

    def body(x_ref, w_ref, wg_ref, c_ref, sa_ref, sb_ref, proj_ref, hnt_ref):
        xf = x_ref[...]
        ms = jnp.mean(xf * xf, axis=-1, keepdims=True)
        hn = xf * lax.rsqrt(ms + NORM_EPS) * w_ref[...]
        hnt_ref[...] = hn.T.astype(BF16)
        hb = hn.astype(BF16)
        c = jnp.tile(c_ref[...], (1, 4))
        sa = jnp.tile(sa_ref[...], (1, 4))
        sb = jnp.tile(sb_ref[...], (1, 4))
        for j in range(N_DEV):
            acc = _mm(hb, wg_ref[:, COL_BLOCK * j:COL_BLOCK * (j + 1)])
            if j < 2:
                acc = _rot(acc, c, sa, sb)
            proj_ref[:, COL_BLOCK * j:COL_BLOCK * (j + 1)] = acc

    tab = pl.BlockSpec((tm, LANES), lambda i: (i, 0))
    return pl.pallas_call(
        body, name="in_proj_fwd", grid=(SEQ // tm,),
        out_shape=(jax.ShapeDtypeStruct((SEQ, IN_COLS), F32),
                   jax.ShapeDtypeStruct((D_MODEL, SEQ), BF16)),
        in_specs=[pl.BlockSpec((tm, D_MODEL), lambda i: (i, 0)),
                  pl.BlockSpec((1, D_MODEL), lambda i: (0, 0)),
                  pl.BlockSpec((D_MODEL, IN_COLS), lambda i: (0, 0)),
                  tab, tab, tab],
        out_specs=(pl.BlockSpec((tm, IN_COLS), lambda i: (i, 0)),
                   pl.BlockSpec((D_MODEL, tm), lambda i: (0, i))),
        compiler_params=_params(("parallel",)),
    )(x, mix_w, w_g, rc, rsa, rsb)


ATTN_GROUP = 4
BLOCKS_PER_PATTERN = SEQ // ATTN_BLOCK


def _write_band_bias(bias_ref):
    qi = lax.broadcasted_iota(jnp.int32, (ATTN_BLOCK, 2 * ATTN_BLOCK), 0)
    kj = lax.broadcasted_iota(jnp.int32, (ATTN_BLOCK, 2 * ATTN_BLOCK), 1)
    bias_ref[0] = jnp.where((kj >= qi) & (kj <= qi + ATTN_BLOCK), 0.0, NEG_BIG)
    bias_ref[1] = jnp.where(kj <= qi, 0.0, NEG_BIG)


def _head0_lanes():
    return lax.broadcasted_iota(jnp.int32, (ATTN_BLOCK, LANES), 1) < HEAD_DIM


def _strided(start, size, d):
    return pl.ds(start, size) if d == 1 else pl.ds(start, size, stride=d)


def _block_place(i, d):
    nblk = BLOCKS_PER_PATTERN // d
    r, n = i // nblk, i % nblk
    kn = jnp.maximum(n - 1, 0)
    row0, key0 = n * (d * ATTN_BLOCK) + r, kn * (d * ATTN_BLOCK) + r
    if d == 1:
        row0, key0 = pl.multiple_of(row0, ATTN_BLOCK), pl.multiple_of(key0, ATTN_BLOCK)
    return _strided(row0, ATTN_BLOCK, d), _strided(key0, 2 * ATTN_BLOCK, d), (n == 0).astype(jnp.int32)


def _for_each_group(d, load, compute, store):
    def group(g, carry):
        items = [load(*_block_place(g * ATTN_GROUP + u, d)) for u in range(ATTN_GROUP)]
        results = [compute(item) for item in items]
        for item, res in zip(items, results):
            store(item, res)
        return carry

    lax.fori_loop(0, BLOCKS_PER_PATTERN // ATTN_GROUP, group, 0)


def _attn_fwd_fused(proj):
    n_pat = len(DILATIONS)
    tile = (ATTN_BLOCK, LANES)

    def body(q_ref, k_ref, v_ref, o_ref, lse_ref, m_acc, l_acc, bias_ref):
        _write_band_bias(bias_ref)
        h0 = _head0_lanes()
        for pi, d in enumerate(DILATIONS):
            first, last = pi == 0, pi == n_pat - 1

            def load(rows, keys, which, first=first):
                item = dict(rows=rows, keys=keys, which=which)
                if not first:
                    item.update(o=o_ref[rows, :], m=[m_acc.at[h][rows, :] for h in range(2)],
                                l=[l_acc.at[h][rows, :] for h in range(2)])
                return item

            def compute(item, first=first):
                res = []
                q = q_ref[item["rows"], :]
                kb = k_ref[item["keys"], :].astype(BF16)
                vb = v_ref[item["keys"], :].astype(BF16)
                for h in range(2):
                    hm = h0 if h == 0 else jnp.logical_not(h0)
                    qh = jnp.where(hm, q, 0.0).astype(BF16)
                    s = _mm_nt(qh, kb) * 0.125 + bias_ref[item["which"]]
                    mb = jnp.max(s, axis=-1, keepdims=True)
                    if first:
                        p = jnp.exp(s - mb)
                        mn = jnp.broadcast_to(mb, tile)
                    else:
                        mn = jnp.maximum(item["m"][h], mb)
                        alpha = jnp.exp(item["m"][h] - mn)
                        p = jnp.exp(s - jnp.concatenate([mn, mn], axis=1))
                    ls = jnp.sum(p, axis=-1, keepdims=True)
                    pv = _mm(p.astype(BF16), vb)
                    if first:
                        res.append((pv, mn, jnp.broadcast_to(ls, tile)))
                    else:
                        res.append((alpha * item["o"] + pv, mn, alpha * item["l"][h] + ls))
                return res

            def store(item, res, last=last):
                rows = item["rows"]
                (o0, m0, l0), (o1, m1, l1) = res
                if last:
                    o_ref[rows, :] = jnp.where(h0, o0 / l0, o1 / l1)
                    lse_ref[rows, :] = jnp.where(h0, m0 + jnp.log(l0), m1 + jnp.log(l1))
                else:
                    o_ref[rows, :] = jnp.where(h0, o0, o1)
                    m_acc.at[0][rows, :], m_acc.at[1][rows, :] = m0, m1
                    l_acc.at[0][rows, :], l_acc.at[1][rows, :] = l0, l1

            _for_each_group(d, load, compute, store)

    slab = lambda g: pl.BlockSpec((SEQ, LANES), functools.partial(lambda hp, g: (0, 4 * g + hp), g=g))
    wide = jax.ShapeDtypeStruct((SEQ, ATTN_WIDTH), F32)
    return pl.pallas_call(
        body, name="attn_fwd", grid=(4,), out_shape=(wide, wide),
        in_specs=[slab(0), slab(1), slab(2)], out_specs=(slab(0), slab(0)),
        scratch_shapes=[pltpu.VMEM((2, SEQ, LANES), F32), pltpu.VMEM((2, SEQ, LANES), F32),
                        pltpu.VMEM((2, ATTN_BLOCK, 2 * ATTN_BLOCK), F32)],
        compiler_params=_params(("parallel",)),
    )(proj, proj, proj)


def _attn_bwd_fused(proj, d_out, lse, delta):
    def body(q_ref, k_ref, v_ref, do_ref, lse_ref, del_ref, dq_ref, dk_ref, dv_ref, bias_ref):
        _write_band_bias(bias_ref)
        dk_ref[...] = jnp.zeros_like(dk_ref)
        dv_ref[...] = jnp.zeros_like(dv_ref)
        h0 = _head0_lanes()
        for pi, d in enumerate(DILATIONS):
            first = pi == 0

            def load(rows, keys, which):
                return dict(rows=rows, keys=keys, q=q_ref[rows, :], g=do_ref[rows, :], lse=lse_ref[rows, :],
                            delta=del_ref[rows, :], k=k_ref[keys, :].astype(BF16),
                            v=v_ref[keys, :].astype(BF16), bias=bias_ref[which])

            def per_head(t):
                swapped = pltpu.roll(t, HEAD_DIM, 1)
                a, b = jnp.where(h0, t, swapped), jnp.where(h0, swapped, t)
                return jnp.concatenate([a, a], axis=1), jnp.concatenate([b, b], axis=1)

            def compute(item):
                dq_h = []
                dk_c = jnp.zeros((2 * ATTN_BLOCK, LANES), F32)
                dv_c = jnp.zeros((2 * ATTN_BLOCK, LANES), F32)
                lse_h, delta_h = per_head(item["lse"]), per_head(item["delta"])
                for h in range(2):
                    hm = h0 if h == 0 else jnp.logical_not(h0)
                    qh = jnp.where(hm, item["q"], 0.0).astype(BF16)
                    gh = jnp.where(hm, item["g"], 0.0).astype(BF16)
                    s = _mm_nt(qh, item["k"]) * 0.125 + item["bias"]
                    p = jnp.exp(s - lse_h[h])
                    dp = _mm_nt(gh, item["v"])
                    ds = (p * (dp - delta_h[h]) * 0.125).astype(BF16)
                    dq_h.append(_mm(ds, item["k"]))
                    dk_c = dk_c + _mm_tn(ds, qh)
                    dv_c = dv_c + _mm_tn(p.astype(BF16), gh)
                return jnp.where(h0, dq_h[0], dq_h[1]), dk_c, dv_c

            def store(item, res, first=first):
                rows, keys = item["rows"], item["keys"]
                if first:
                    dq_ref[rows, :] = res[0]
                else:
                    dq_ref[rows, :] += res[0]
                dk_ref[keys, :] += res[1]
                dv_ref[keys, :] += res[2]

            _for_each_group(d, load, compute, store)

    slab = lambda g: pl.BlockSpec((SEQ, LANES), functools.partial(lambda hp, g: (0, 4 * g + hp), g=g))
    wide = jax.ShapeDtypeStruct((SEQ, ATTN_WIDTH), F32)
    return pl.pallas_call(
        body, name="attn_bwd", grid=(4,), out_shape=(wide, wide, wide),
        scratch_shapes=[pltpu.VMEM((2, ATTN_BLOCK, 2 * ATTN_BLOCK), F32)],
        in_specs=[slab(0), slab(1), slab(2), slab(0), slab(0), slab(0)], out_specs=(slab(0), slab(0), slab(0)),
        compiler_params=_params(("parallel",)),
    )(proj, proj, proj, d_out, lse, delta)


def _hgrn_lower_bound(lb_ref):
    r0, r1 = lb_ref[0:1, :], lb_ref[1:2, :]
    mx = jnp.maximum(r0, r1)
    e0, e1 = jnp.exp(r0 - mx), jnp.exp(r1 - mx)
    return e0 / (e0 + e1)


def _hgrn_gates(hq, hf, lb):
    sq = _sigmoid(hq)
    sg = _sigmoid(hf)
    f = lb + (1.0 - lb) * sg
    return hq * sq, sq, sg, f, 1.0 - f, jnp.log(f)


HGRN_PAIR = 2
HGRN_SEQ_BLOCK = 1024
HGRN_GROUP = 4
HGRN_ROWS = HGRN_GROUP * HGRN_CHUNK


def _hgrn_specs(reverse):
    n_blocks = SEQ // HGRN_SEQ_BLOCK
    width = HGRN_PAIR * HGRN_DIM
    blk = (lambda s: n_blocks - 1 - s) if reverse else (lambda s: s)
    cols = lambda g: pl.BlockSpec((HGRN_SEQ_BLOCK, width),
                                  functools.partial(lambda p, s, g: (blk(s), HGRN_PAIR * g + p), g=g))
    pair = pl.BlockSpec((HGRN_SEQ_BLOCK, width), lambda p, s: (blk(s), p))
    lb = pl.BlockSpec((2, width), lambda p, s: (0, p))
    states = pl.BlockSpec((HGRN_PAIR, HGRN_SEQ_BLOCK // HGRN_CHUNK, HGRN_DIM, HGRN_DIM),
                          lambda p, s: (p, blk(s), 0, 0))
    return cols, pair, lb, states


def _chunk_masks():
    ri = lax.broadcasted_iota(jnp.int32, (HGRN_ROWS, HGRN_ROWS), 0)
    ci = lax.broadcasted_iota(jnp.int32, (HGRN_ROWS, HGRN_ROWS), 1)
    same = (ri // HGRN_CHUNK) == (ci // HGRN_CHUNK)
    return same, same & (ri >= ci), same & (ri <= ci)


def _mm_select(sel, v):
    hi = v.astype(BF16)
    r1 = v - hi.astype(F32)
    mid = r1.astype(BF16)
    lo = (r1 - mid.astype(F32)).astype(BF16)
    return _mm(sel, hi) + _mm(sel, mid) + _mm(sel, lo)


def _head_cols(a, h):
    return a[:, HGRN_DIM * h:HGRN_DIM * (h + 1)]


def _hgrn_fwd(proj, lb_raw):
    t, rws = HGRN_CHUNK, HGRN_ROWS

    def body(hq_ref, hf_ref, hi_ref, lb_ref, rec_ref, st_ref, state):
        @pl.when(pl.program_id(1) == 0)
        def _():
            state[...] = jnp.zeros_like(state)

        lb = _hgrn_lower_bound(lb_ref)
        same, causal, _ = _chunk_masks()
        sel = jnp.concatenate([causal, same], axis=0).astype(BF16)

        def group(g, sts):
            rows = pl.ds(pl.multiple_of(g * rws, rws), rws)
            q, _, _, _, k, lf = _hgrn_gates(hq_ref[rows, :], hf_ref[rows, :], lb)
            sums = _mm_select(sel, lf)
            cum, last = sums[:rws], sums[rws:]
            qd = (q * jnp.exp(cum)).astype(BF16)
            ki = (k * jnp.exp(-cum)).astype(BF16)
            ke = (k * jnp.exp(last - cum)).astype(BF16)
            vb = hi_ref[rows, :].astype(BF16)
            dec = jnp.exp(last)
            new_sts, recs = [], []
            for h in range(HGRN_PAIR):
                qd_h, ke_h, vb_h = _head_cols(qd, h), _head_cols(ke, h), _head_cols(vb, h)
                att = jnp.where(causal, _mm_nt(qd_h, _head_cols(ki, h)), 0.0).astype(BF16)
                intra = _mm(att, vb_h)
                st = sts[h]
                outs = []
                for c in range(HGRN_GROUP):
                    sl = slice(c * t, (c + 1) * t)
                    st_ref[h, g * HGRN_GROUP + c] = st
                    outs.append(intra[sl] + _mm_nt(qd_h[sl], st.astype(BF16)))
                    st = st * _head_cols(dec[c * t:c * t + 1, :], h) + _mm_tn(vb_h[sl], ke_h[sl])
                new_sts.append(st)
                recs.append(jnp.concatenate(outs, axis=0))
            rec_ref[rows, :] = jnp.concatenate(recs, axis=1)
            return tuple(new_sts)

        sts = lax.fori_loop(0, HGRN_SEQ_BLOCK // rws, group, tuple(state[h] for h in range(HGRN_PAIR)))
        for h in range(HGRN_PAIR):
            state[h] = sts[h]

    cols, pair, lb, states = _hgrn_specs(reverse=False)
    return pl.pallas_call(
        body, name="hgrn_fwd", grid=(HGRN_HEADS // HGRN_PAIR, SEQ // HGRN_SEQ_BLOCK),
        out_shape=(jax.ShapeDtypeStruct((SEQ, HGRN_WIDTH), F32),
                   jax.ShapeDtypeStruct((HGRN_HEADS, N_CHUNKS, HGRN_DIM, HGRN_DIM), F32)),
        in_specs=[cols(4), cols(5), cols(6), lb], out_specs=(pair, states),
        scratch_shapes=[pltpu.VMEM((HGRN_PAIR, HGRN_DIM, HGRN_DIM), F32)],
        compiler_params=_params(("parallel", "arbitrary")),
    )(proj, proj, proj, lb_raw)


def _hgrn_bwd(proj, lb_raw, d_rec, states):
    t, rws = HGRN_CHUNK, HGRN_ROWS

    def body(hq_ref, hf_ref, hi_ref, lb_ref, do_ref, st_ref, dhq_ref, dhf_ref, dhi_ref, dlb_ref,
             dstate, dlb_acc):
        lb = _hgrn_lower_bound(lb_ref)
        same, causal, anti = _chunk_masks()
        sel = jnp.concatenate([causal, same], axis=0).astype(BF16)
        sel_t = jnp.concatenate([anti, same], axis=1).astype(BF16)
        @pl.when(pl.program_id(1) == 0)
        def _():
            dstate[...] = jnp.zeros_like(dstate)
            dlb_acc[...] = jnp.zeros_like(dlb_acc)

        n_groups = HGRN_SEQ_BLOCK // rws
        chunks = [slice(c * t, (c + 1) * t) for c in range(HGRN_GROUP)]

        def group(i, dsts_in):
            g = n_groups - 1 - i
            rows = pl.ds(pl.multiple_of(g * rws, rws), rws)
            hq = hq_ref[rows, :]
            q, sq, sg, f, k, lf = _hgrn_gates(hq, hf_ref[rows, :], lb)
            sums = _mm_select(sel, lf)
            cum, last = sums[:rws], sums[rws:]
            e_cum, e_inv, e_end, dec = jnp.exp(cum), jnp.exp(-cum), jnp.exp(last - cum), jnp.exp(last)
            qd, ki, ke = q * e_cum, k * e_inv, k * e_end
            qdb, kib, keb = qd.astype(BF16), ki.astype(BF16), ke.astype(BF16)
            vb = hi_ref[rows, :].astype(BF16)
            gb = do_ref[rows, :].astype(BF16)

            dsts_out, per_head = [], []
            for h in range(HGRN_PAIR):
                qdb_h, kib_h, keb_h = _head_cols(qdb, h), _head_cols(kib, h), _head_cols(keb, h)
                vb_h, gb_h = _head_cols(vb, h), _head_cols(gb, h)
                att = jnp.where(causal, _mm_nt(qdb_h, kib_h), 0.0).astype(BF16)
                datt = jnp.where(causal, _mm_nt(gb_h, vb_h), 0.0).astype(BF16)
                dv = _mm_tn(att, gb_h)
                dqd = _mm(datt, kib_h)
                dki = _mm_tn(datt, qdb_h)

                decs = [_head_cols(dec[c * t:c * t + 1, :], h) for c in range(HGRN_GROUP)]
                dsts = [None] * HGRN_GROUP
                dst = dsts_in[h]
                for c in reversed(range(HGRN_GROUP)):
                    dsts[c] = dst
                    dst = dst * decs[c] + _mm_tn(gb_h[chunks[c]], qdb_h[chunks[c]])
                dsts_out.append(dst)

                dv_x, dqd_x, dke, dlast_x = [], [], [], []
                for c, sl in enumerate(chunks):
                    st_prev = st_ref[h, g * HGRN_GROUP + c]
                    dstb = dsts[c].astype(BF16)
                    dv_x.append(_mm_nt(keb_h[sl], dstb))
                    dqd_x.append(_mm(gb_h[sl], st_prev.astype(BF16)))
                    dke.append(_mm(vb_h[sl], dstb))
                    ddec = jnp.sum(dsts[c] * st_prev, axis=0, keepdims=True)
                    dlast_x.append(jnp.broadcast_to(ddec * decs[c], (t, HGRN_DIM)))
                per_head.append((dv + jnp.concatenate(dv_x, axis=0), dqd + jnp.concatenate(dqd_x, axis=0),
                                 dki, jnp.concatenate(dke, axis=0), jnp.concatenate(dlast_x, axis=0)))
            dv, dqd, dki, dke, dlast = (jnp.concatenate([a, b], axis=1) for a, b in zip(*per_head))

            dq = dqd * e_cum
            dk = dki * e_inv + dke * e_end
            dke_ke = dke * ke
            dcum = dqd * qd - dki * ki - dke_ke
            dlf = _mm_select(sel_t, jnp.concatenate([dcum, dke_ke], axis=0)) + dlast
            df = dlf / f - dk
            dhq_ref[rows, :] = dq * (sq * (1.0 + hq * (1.0 - sq)))
            dhf_ref[rows, :] = df * (1.0 - lb) * (sg * (1.0 - sg))
            dhi_ref[rows, :] = dv
            dlb_acc[...] += jnp.sum(df * (1.0 - sg), axis=0, keepdims=True)
            return tuple(dsts_out)

        dsts = lax.fori_loop(0, n_groups, group, tuple(dstate[h] for h in range(HGRN_PAIR)))
        for h in range(HGRN_PAIR):
            dstate[h] = dsts[h]
        g0 = dlb_acc[...] * lb * (1.0 - lb)
        dlb_ref[...] = jnp.concatenate([g0, -g0], axis=0)

    cols, pair, lb_spec, st_spec = _hgrn_specs(reverse=True)
    wide = jax.ShapeDtypeStruct((SEQ, HGRN_WIDTH), F32)
    return pl.pallas_call(
        body, name="hgrn_bwd", grid=(HGRN_HEADS // HGRN_PAIR, SEQ // HGRN_SEQ_BLOCK),
        out_shape=(wide, wide, wide, jax.ShapeDtypeStruct((2, HGRN_WIDTH), F32)),
        in_specs=[cols(4), cols(5), cols(6), lb_spec, pair, st_spec],
        out_specs=(pair, pair, pair, lb_spec),
        scratch_shapes=[pltpu.VMEM((HGRN_PAIR, HGRN_DIM, HGRN_DIM), F32),
                        pltpu.VMEM((1, HGRN_PAIR * HGRN_DIM), F32)],
        compiler_params=_params(("parallel", "arbitrary")),
    )(proj, proj, proj, lb_raw, d_rec, states)


def _group_sum(v, group):
    parts = []
    for s in range(v.shape[1] // LANES):
        slab = v[:, LANES * s:LANES * (s + 1)]
        if group == LANES:
            parts.append(jnp.broadcast_to(jnp.sum(slab, axis=-1, keepdims=True), slab.shape))
        else:
            h0 = lax.broadcasted_iota(jnp.int32, slab.shape, 1) < HEAD_DIM
            s0 = jnp.sum(jnp.where(h0, slab, 0.0), axis=-1, keepdims=True)
            s1 = jnp.sum(jnp.where(h0, 0.0, slab), axis=-1, keepdims=True)
            parts.append(jnp.where(h0, s0, s1))
    return jnp.concatenate(parts, axis=1)


def _mid(attn_o, rec, proj, x, target, w_out_g, attn_w, hgrn_w, final_w):
    tm = 256

    def branch_fwd(o, gate, w, group):
        r = lax.rsqrt(_group_sum(o * o, group) * (1.0 / group) + NORM_EPS)
        nrm = o * r
        sg = _sigmoid(gate)
        return r, nrm, sg, nrm * w * (gate * sg)

    def branch_bwd(dy, r, nrm, sg, gate, w, group):
        silu = gate * sg
        d_gate = dy * nrm * w * (sg * (1.0 + gate * (1.0 - sg)))
        d_w = jnp.sum(dy * nrm * silu, axis=0, keepdims=True)
        dn = dy * w * silu
        d_o = r * (dn - nrm * (_group_sum(dn * nrm, group) * (1.0 / group)))
        return d_o, d_gate, d_w

    def body(o_ref, rec_ref, ag_ref, hg_ref, x_ref, tgt_ref, wout_ref, aw_ref, hw_ref, fw_ref,
             dx2_ref, do_ref, delta_ref, dag_ref, drec_ref, dhg_ref, dwout_ref, dfw_ref, daw_ref, dhw_ref,
             loss_ref, dwout_acc):
        i = pl.program_id(0)

        @pl.when(i == 0)
        def _():
            dwout_acc[...] = jnp.zeros_like(dwout_acc)
            dfw_ref[...] = jnp.zeros_like(dfw_ref)
            daw_ref[...] = jnp.zeros_like(daw_ref)
            dhw_ref[...] = jnp.zeros_like(dhw_ref)
            loss_ref[...] = jnp.zeros_like(loss_ref)

        o, rc, ag, hg = o_ref[...], rec_ref[...], ag_ref[...], hg_ref[...]
        aw, hw, fw = aw_ref[...], hw_ref[...], fw_ref[...]
        ra, na, sga, ya = branch_fwd(o, ag, aw, HEAD_DIM)
        rh, nh, sgh, yh = branch_fwd(rc, hg, hw, HGRN_DIM)
        mixed = jnp.concatenate([ya, yh], axis=1).astype(BF16)
        wout = wout_ref[...]
        x2 = x_ref[...] + _mm(mixed, wout)
        rstd = lax.rsqrt(jnp.mean(x2 * x2, axis=-1, keepdims=True) + NORM_EPS)
        xn = x2 * rstd
        err = xn * fw - tgt_ref[...]
        row_loss = jnp.mean(err * err, axis=-1, keepdims=True)
        loss_ref[...] += 0.5 * jnp.sum(row_loss, axis=0, keepdims=True)
        dy = err * (1.0 / D_MODEL)
        dfw_ref[...] += jnp.sum(dy * xn, axis=0, keepdims=True)
        dxn = dy * fw
        dx2 = rstd * (dxn - xn * jnp.mean(dxn * xn, axis=-1, keepdims=True))
        dx2_ref[...] = dx2
        dx2b = dx2.astype(BF16)
        dwout_acc[...] += _mm_tn(mixed, dx2b)

        @pl.when(i == pl.num_programs(0) - 1)
        def _():
            dwout_ref[...] = dwout_acc[...].astype(BF16)

        dmixed = _mm_nt(dx2b, wout)

        d_o, d_ag, d_aw = branch_bwd(dmixed[:, :ATTN_WIDTH], ra, na, sga, ag, aw, HEAD_DIM)
        d_rec, d_hg, d_hw = branch_bwd(dmixed[:, ATTN_WIDTH:], rh, nh, sgh, hg, hw, HGRN_DIM)
        do_ref[...] = d_o
        delta_ref[...] = _group_sum(d_o * o, HEAD_DIM)
        dag_ref[...] = d_ag
        drec_ref[...] = d_rec
        dhg_ref[...] = d_hg
        daw_ref[...] += d_aw
        dhw_ref[...] += d_hw

    half = lambda: pl.BlockSpec((tm, COL_BLOCK), lambda i: (i, 0))
    full = lambda: pl.BlockSpec((tm, D_MODEL), lambda i: (i, 0))
    fixed = lambda r, c: pl.BlockSpec((r, c), lambda i: (0, 0))
    wide = jax.ShapeDtypeStruct((SEQ, COL_BLOCK), F32)
    return pl.pallas_call(
        body, name="mid", grid=(SEQ // tm,),
        out_shape=(jax.ShapeDtypeStruct((SEQ, D_MODEL), F32), wide, wide, wide, wide, wide,
                   jax.ShapeDtypeStruct((D_MODEL, D_MODEL), BF16),
                   jax.ShapeDtypeStruct((1, D_MODEL), F32), jax.ShapeDtypeStruct((1, COL_BLOCK), F32),
                   jax.ShapeDtypeStruct((1, COL_BLOCK), F32), jax.ShapeDtypeStruct((1, 1), F32)),
        scratch_shapes=[pltpu.VMEM((D_MODEL, D_MODEL), F32)],
        in_specs=[half(), half(),
                  pl.BlockSpec((tm, COL_BLOCK), lambda i: (i, 3)), pl.BlockSpec((tm, COL_BLOCK), lambda i: (i, 7)),
                  full(), full(), fixed(D_MODEL, D_MODEL), fixed(1, COL_BLOCK), fixed(1, COL_BLOCK),
                  fixed(1, D_MODEL)],
        out_specs=(full(), half(), half(), half(), half(), half(), fixed(D_MODEL, D_MODEL),
                   fixed(1, D_MODEL), fixed(1, COL_BLOCK), fixed(1, COL_BLOCK), fixed(1, 1)),
        compiler_params=_params(("arbitrary",)),
    )(attn_o, rec, proj, proj, x, target, w_out_g, attn_w, hgrn_w, final_w)


def _in_proj_bwd_rows(d_groups, w_g, x, dx2, mix_w, rc, rsa, rsb):
    tm = 256

    def body(*refs):
        dg_refs = refs[:N_DEV]
        wg_ref, x_ref, dx2_ref, w_ref, c_ref, sa_ref, sb_ref, gx_ref, dpb_ref, dmw_ref = refs[N_DEV:]

        @pl.when(pl.program_id(0) == 0)
        def _():
            dmw_ref[...] = jnp.zeros_like(dmw_ref)

        parts = []
        for j in range(N_DEV):
            dp = dg_refs[j][...]
            if j < 2:
                dp = _rot_transposed(dp, jnp.tile(c_ref[...], (1, 4)), jnp.tile(sa_ref[...], (1, 4)),
                                     jnp.tile(sb_ref[...], (1, 4)))
            parts.append(dp.astype(BF16))
        dpb = jnp.concatenate(parts, axis=1)
        dpb_ref[...] = dpb
        g = _mm_nt(dpb, wg_ref[...])
        xf = x_ref[...]
        rstd = lax.rsqrt(jnp.mean(xf * xf, axis=-1, keepdims=True) + NORM_EPS)
        xn = xf * rstd
        dmw_ref[...] += jnp.sum(g * xn, axis=0, keepdims=True)
        gw = g * w_ref[...]
        gx_ref[...] = dx2_ref[...] + rstd * (gw - xn * jnp.mean(gw * xn, axis=-1, keepdims=True))

    tile = lambda cols: pl.BlockSpec((tm, cols), lambda i: (i, 0))
    fixed = lambda r, c: pl.BlockSpec((r, c), lambda i: (0, 0))
    return pl.pallas_call(
        body, name="in_proj_bwd_rows", grid=(SEQ // tm,),
        out_shape=(jax.ShapeDtypeStruct((SEQ, D_MODEL), F32), jax.ShapeDtypeStruct((SEQ, IN_COLS), BF16),
                   jax.ShapeDtypeStruct((1, D_MODEL), F32)),
        in_specs=[tile(COL_BLOCK) for _ in range(N_DEV)] + [
            fixed(D_MODEL, IN_COLS), tile(D_MODEL), tile(D_MODEL), fixed(1, D_MODEL),
            tile(LANES), tile(LANES), tile(LANES)],
        out_specs=(tile(D_MODEL), tile(IN_COLS), fixed(1, D_MODEL)),
        compiler_params=_params(("arbitrary",)),
    )(*d_groups, w_g, x, dx2, mix_w, rc, rsa, rsb)


def _in_proj_bwd_weights(hn_t, dproj_b):
    def body(hnt_ref, dp_ref, dwin_ref):
        dwin_ref[0] = _mm(hnt_ref[...], dp_ref[...]).astype(BF16)

    return pl.pallas_call(
        body, name="in_proj_bwd_weights", grid=(N_DEV,),
        out_shape=jax.ShapeDtypeStruct((N_DEV, D_MODEL, COL_BLOCK), BF16),
        in_specs=[pl.BlockSpec((D_MODEL, SEQ), lambda j: (0, 0)), pl.BlockSpec((SEQ, COL_BLOCK), lambda j: (0, j))],
        out_specs=pl.BlockSpec((1, D_MODEL, COL_BLOCK), lambda j: (j, 0, 0)),
        compiler_params=_params(("parallel",)),
    )(hn_t, dproj_b)


def _adamw(w, g, m, v):
    m = ADAM_B1 * m + (1.0 - ADAM_B1) * g
    v = ADAM_B2 * v + (1.0 - ADAM_B2) * (g * g)
    m_hat = m / (1.0 - ADAM_B1 ** ADAM_STEP)
    v_hat = v / (1.0 - ADAM_B2 ** ADAM_STEP)
    delta = -ADAM_LR * (m_hat / (jnp.sqrt(v_hat) + ADAM_EPS) + ADAM_WD * w)
    return delta, m, v


def _exchange_update(dwin_p, dwout_p, small_p, w_in, m_in, v_in, w_out, m_out, v_out, w_s, m_s, v_s):
    rb = 128
    n_chips = N_DEV // 2
    S1_IN, S1_OUT, SMALL, S2_IN, S2_OUT = 0, 4, 8, 15, 18

    def body(dwin_hbm, dwout_hbm, small_ref, win_ref, min_ref, vin_ref, wout_ref, mout_ref, vout_ref,
             ws_ref, ms_ref, vs_ref,
             gin_ref, din_ref, nmin_ref, nvin_ref, gout_ref, dout_ref, nmout_ref, nvout_ref,
             gs_ref, ds_ref, nms_ref, nvs_ref,
             own_in, own_out, s1_in, s1_out, fwd_in, fwd_out, s2_in, s2_out, land_s,
             send_sems, recv_sems, local_sems):
        me = _my_place()
        x, y, c = me
        my_chip = 2 * x + y
        sibling = (x, y, 1 - c)

        def remote(slot, src, dst, to):
            return pltpu.make_async_remote_copy(src_ref=src, dst_ref=dst, send_sem=send_sems.at[slot],
                                                recv_sem=recv_sems.at[slot], device_id=to, device_id_type=MESH)

        def stage1(q):
            return [remote(S1_IN + q, dwin_hbm.at[q, 1 - c], s1_in.at[q], sibling),
                    remote(S1_OUT + q, dwout_hbm.at[q, 1 - c], s1_out.at[q], sibling)]

        def stage2(rel):
            peer = _peer(me, 2 * rel)
            return [remote(S2_IN + rel - 1, fwd_in.at[rel - 1], s2_in.at[rel - 1], peer),
                    remote(S2_OUT + rel - 1, fwd_out.at[rel - 1], s2_out.at[rel - 1], peer)]

        def small_copy(rel):
            return remote(SMALL + rel - 1, small_ref, land_s.at[rel], _peer(me, rel))

        mine = [pltpu.make_async_copy(dwin_hbm.at[:, c], own_in, local_sems.at[0]),
                pltpu.make_async_copy(dwout_hbm.at[:, c], own_out, local_sems.at[1])]
        for cp in mine:
            cp.start()
        sent = []
        for q in range(n_chips):
            sent += stage1(q)
        land_s[0] = small_ref[...]
        sent += [small_copy(rel) for rel in range(1, N_DEV)]
        for cp in sent:
            cp.start()
        for cp in mine:
            cp.wait()

        def add_blocks(q, own, got, n_rows, dst):
            def step(b, carry):
                rows = pl.ds(pl.multiple_of(b * rb, rb), rb)
                dst[rows, :] = (own[q, rows, :].astype(F32) + got[q, rows, :].astype(F32)).astype(dst.dtype)
                return carry
            lax.fori_loop(0, n_rows // rb, step, 0)

        for rel in range(1, n_chips):
            q = my_chip ^ rel
            for cp in stage1(q):
                cp.wait_recv()
            add_blocks(q, own_in, s1_in, D_MODEL, fwd_in.at[rel - 1])
            add_blocks(q, own_out, s1_out, WOUT_ROWS, fwd_out.at[rel - 1])
            for cp in stage2(rel):
                cp.start()
                sent.append(cp)
        for cp in stage1(my_chip):
            cp.wait_recv()
        add_blocks(my_chip, own_in, s1_in, D_MODEL, gin_ref)
        add_blocks(my_chip, own_out, s1_out, WOUT_ROWS, gout_ref)
        for rel in range(1, n_chips):
            for cp in stage2(rel):
                cp.wait_recv()

        def update(got, w_ref, m_ref, v_ref, g_ref, d_ref, nm_ref, nv_ref, n_rows):
            def step(b, carry):
                rows = pl.ds(pl.multiple_of(b * rb, rb), rb)
                g = g_ref[rows, :]
                for rel in range(1, n_chips):
                    g = g + got[rel - 1, rows, :].astype(F32)
                delta, nm, nv = _adamw(w_ref[rows, :], g, m_ref[rows, :], v_ref[rows, :])
                g_ref[rows, :] = g
                d_ref[rows, :] = delta
                nm_ref[rows, :] = nm
                nv_ref[rows, :] = nv
                return carry
            lax.fori_loop(0, n_rows // rb, step, 0)

        update(s2_in, win_ref, min_ref, vin_ref, gin_ref, din_ref, nmin_ref, nvin_ref, D_MODEL)
        update(s2_out, wout_ref, mout_ref, vout_ref, gout_ref, dout_ref, nmout_ref, nvout_ref, WOUT_ROWS)

        for rel in range(1, N_DEV):
            small_copy(rel).wait_recv()
        my_flat = _flat(me)
        g = land_s[my_flat ^ 0]
        for dev in range(1, N_DEV):
            g = g + land_s[my_flat ^ dev]
        delta, nm, nv = _adamw(ws_ref[...], g, ms_ref[...], vs_ref[...])
        gs_ref[...] = g
        ds_ref[...] = delta
        nms_ref[...] = nm
        nvs_ref[...] = nv
        for cp in sent:
            cp.wait_send()

    vm = lambda: pl.BlockSpec(memory_space=pltpu.VMEM)
    anyspace = lambda: pl.BlockSpec(memory_space=pl.ANY)
    big = jax.ShapeDtypeStruct((D_MODEL, COL_BLOCK), F32)
    flat = jax.ShapeDtypeStruct((WOUT_ROWS, D_MODEL), F32)
    small = jax.ShapeDtypeStruct((SMALL_ROWS, LANES), F32)
    in_blocks = lambda n: pltpu.VMEM((n, D_MODEL, COL_BLOCK), BF16)
    out_blocks = lambda n: pltpu.VMEM((n, WOUT_ROWS, D_MODEL), BF16)
    return pl.pallas_call(
        body, name="exchange_update",
        out_shape=tuple([big] * 4 + [flat] * 4 + [small] * 4),
        in_specs=[anyspace(), anyspace()] + [vm() for _ in range(10)],
        out_specs=tuple(vm() for _ in range(12)),
        scratch_shapes=[in_blocks(n_chips), out_blocks(n_chips), in_blocks(n_chips), out_blocks(n_chips),
                        in_blocks(n_chips - 1), out_blocks(n_chips - 1),
                        in_blocks(n_chips - 1), out_blocks(n_chips - 1),
                        pltpu.VMEM((N_DEV, SMALL_ROWS, LANES), F32),
                        pltpu.SemaphoreType.DMA((21,)), pltpu.SemaphoreType.DMA((21,)),
                        pltpu.SemaphoreType.DMA((2,))],
        compiler_params=_params(),
    )(dwin_p.reshape(n_chips, 2, D_MODEL, COL_BLOCK), dwout_p.reshape(n_chips, 2, WOUT_ROWS, D_MODEL),
      small_p, w_in, m_in, v_in, w_out, m_out, v_out, w_s, m_s, v_s)


def _pack_small(mix, attn, hgrn, lb, final, loss=None):
    def rows8(a):
        a = a.reshape(-1, LANES)
        return jnp.pad(a, ((0, 8 - a.shape[0]), (0, 0)))
    last = jnp.zeros((8, LANES), F32) if loss is None else jnp.pad(loss.reshape(1, 1), ((0, 7), (0, LANES - 1)))
    return jnp.concatenate([rows8(mix), rows8(attn), rows8(hgrn), rows8(lb), rows8(final), last], axis=0)


def _unpack_small(slab):
    return (slab[ROW_MIX:ROW_MIX + 8].reshape(1, D_MODEL), slab[ROW_ATTN:ROW_ATTN + 4].reshape(1, ATTN_WIDTH),
            slab[ROW_HGRN:ROW_HGRN + 4].reshape(1, HGRN_WIDTH), slab[ROW_LB:ROW_LB + 8].reshape(2, HGRN_WIDTH),
            slab[ROW_FINAL:ROW_FINAL + 8].reshape(D_MODEL))


def _local_step(x, pos_col, w_in_g, w_out_g, mix_w, attn_w, hgrn_w, lb_raw, final_w, target):
    inv = ROPE_THETA ** (-jnp.arange(ROPE_HALF, dtype=F32) * (2.0 / ROPE_DIMS))
    lane_e = jnp.arange(LANES) % HEAD_DIM
    inv_lanes = jnp.where(lane_e < ROPE_DIMS, inv[lane_e % ROPE_HALF], 0.0).reshape(1, LANES)
    rc, rsa, rsb = _rope_tables(pos_col, inv_lanes)

    proj, hn_t = _in_proj_fwd(x, mix_w, w_in_g, rc, rsa, rsb)
    attn_o, lse = _attn_fwd_fused(proj)
    rec, states = _hgrn_fwd(proj, lb_raw)

    (dx2, d_o, delta, d_ag, d_rec, d_hg, dwout_p, d_final, d_attn_w, d_hgrn_w, loss) = _mid(
        attn_o, rec, proj, x, target, w_out_g, attn_w, hgrn_w, final_w.reshape(1, D_MODEL))

    dqkv = _attn_bwd_fused(proj, d_o, lse, delta)
    d_hq, d_hf, d_hi, d_lb = _hgrn_bwd(proj, lb_raw, d_rec, states)

    grad_x, dproj_b, d_mix = _in_proj_bwd_rows(
        (dqkv[0], dqkv[1], dqkv[2], d_ag, d_hq, d_hf, d_hi, d_hg), w_in_g, x, dx2, mix_w, rc, rsa, rsb)
    dwin_p = _in_proj_bwd_weights(hn_t, dproj_b)
    small_p = _pack_small(d_mix, d_attn_w, d_hgrn_w, d_lb, d_final, loss)
    return grad_x, dwin_p, dwout_p, small_p


def kernel(x, positions, w_in, w_out, mix_norm_w, attn_out_norm_w, hgrn_out_norm_w, hgrn_lb_raw, final_norm_w, loss_target, m_w_in, m_w_out, m_mix_norm_w, m_attn_out_norm_w, m_hgrn_out_norm_w, m_hgrn_lb_raw, m_final_norm_w, v_w_in, v_w_out, v_mix_norm_w, v_attn_out_norm_w, v_hgrn_out_norm_w, v_hgrn_lb_raw, v_final_norm_w):
    w_in_g, w_out_g = _gather_weights(w_in[0], w_out[0])
    grad_x, dwin_p, dwout_p, small_p = _local_step(
        x[0], positions.reshape(SEQ, 1), w_in_g, w_out_g, mix_norm_w, attn_out_norm_w, hgrn_out_norm_w,
        hgrn_lb_raw, final_norm_w, loss_target[0])

    w_s = _pack_small(mix_norm_w, attn_out_norm_w, hgrn_out_norm_w, hgrn_lb_raw, final_norm_w)
    m_s = _pack_small(m_mix_norm_w, m_attn_out_norm_w, m_hgrn_out_norm_w, m_hgrn_lb_raw, m_final_norm_w)
    v_s = _pack_small(v_mix_norm_w, v_attn_out_norm_w, v_hgrn_out_norm_w, v_hgrn_lb_raw, v_final_norm_w)
    (g_in, d_in, nm_in, nv_in, g_out, d_out, nm_out, nv_out, g_s, d_s, nm_s, nv_s) = _exchange_update(
        dwin_p, dwout_p, small_p, w_in[0], m_w_in[0], v_w_in[0], w_out[0], m_w_out[0], v_w_out[0], w_s, m_s, v_s)

    loss = g_s[ROW_LOSS, 0]
    return (loss, grad_x[None], g_in[None], g_out[None], *_unpack_small(g_s),
            d_in[None], d_out[None], *_unpack_small(d_s),
            nm_in[None], nm_out[None], *_unpack_small(nm_s),
            nv_in[None], nv_out[None], *_unpack_small(nv_s))
```

```python
import functools

import jax
import jax.numpy as jnp
from jax import lax
from jax.experimental import pallas as pl
from jax.experimental.pallas import tpu as pltpu

F32 = jnp.float32
BF16 = jnp.bfloat16

SEQ = 4096
D_MODEL = 1024
ATTN_WIDTH = 512
HGRN_WIDTH = 512
HEAD_DIM = 64
HGRN_HEADS = 4
HGRN_DIM = 128
HGRN_CHUNK = 64
N_CHUNKS = SEQ // HGRN_CHUNK
IN_COLS = 4096
COL_BLOCK = 512
N_DEV = 8
WOUT_ROWS = D_MODEL // N_DEV
ATTN_BLOCK = 128
DILATIONS = (1, 4, 16)
ROPE_THETA = 500000.0
ROPE_DIMS = 16
ROPE_HALF = 8
NORM_EPS = 1e-6
NEG_BIG = -1e30
LANES = 128

ADAM_LR = 0.001
ADAM_B1 = 0.9
ADAM_B2 = 0.999
ADAM_EPS = 1e-08
ADAM_WD = 0.01
ADAM_STEP = 10

SMALL_ROWS = 48
ROW_MIX, ROW_ATTN, ROW_HGRN, ROW_LB, ROW_FINAL, ROW_LOSS = 0, 8, 16, 24, 32, 40

VMEM_LIMIT = 56 * 1024 * 1024
MESH = pl.DeviceIdType.MESH


def _mm(a, b):
    return lax.dot_general(a, b, (((1,), (0,)), ((), ())), preferred_element_type=F32)


def _mm_nt(a, b):
    return lax.dot_general(a, b, (((1,), (1,)), ((), ())), preferred_element_type=F32)


def _mm_tn(a, b):
    return lax.dot_general(a, b, (((0,), (0,)), ((), ())), preferred_element_type=F32)


def _mm_exact(a, b):
    return lax.dot_general(a, b, (((1,), (0,)), ((), ())), preferred_element_type=F32,
                           precision=lax.Precision.HIGHEST)


def _sigmoid(v):
    return 1.0 / (1.0 + jnp.exp(-v))


def _params(sem=None, **kw):
    return pltpu.CompilerParams(dimension_semantics=sem, vmem_limit_bytes=VMEM_LIMIT, **kw)


def _my_place():
    return lax.axis_index("x"), lax.axis_index("y"), lax.axis_index("c")


def _peer(place, rel):
    x, y, c = place
    return (x ^ ((rel >> 2) & 1), y ^ ((rel >> 1) & 1), c ^ (rel & 1))


def _flat(place):
    x, y, c = place
    return 4 * x + 2 * y + c


def _gather_weights(w_in, w_out):
    def body(win_ref, wout_ref, gin_ref, gout_ref, stage, send_sems, recv_sems, local_sem):
        me = _my_place()
        x, y, c = me
        sibling = (x, y, 1 - c)
        chips = [(1 - x, y), (x, 1 - y), (1 - x, 1 - y)]

        def slab(which, place):
            idx = _flat(place)
            if which == 0:
                return gin_ref.at[:, pl.ds(pl.multiple_of(idx * COL_BLOCK, COL_BLOCK), COL_BLOCK)]
            return gout_ref.at[pl.ds(pl.multiple_of(idx * WOUT_ROWS, WOUT_ROWS), WOUT_ROWS), :]

        def copy(which, k, block, to, src=None):
            ref = slab(which, block)
            return pltpu.make_async_remote_copy(
                src_ref=ref if src is None else src, dst_ref=ref, send_sem=send_sems.at[7 * which + k],
                recv_sem=recv_sems.at[7 * which + k], device_id=to, device_id_type=MESH)

        stage[...] = win_ref[...].astype(BF16)
        own = pltpu.make_async_copy(stage, slab(0, me), local_sem)
        own.start()
        gout_ref[pl.ds(pl.multiple_of(_flat(me) * WOUT_ROWS, WOUT_ROWS), WOUT_ROWS), :] = (
            wout_ref[...].astype(BF16))

        started = []
        for which in (0, 1):
            src = stage if which == 0 else None
            first = [copy(which, 0, me, sibling, src)]
            first += [copy(which, 1 + j, me, (*chip, c), src) for j, chip in enumerate(chips)]
            for cp in first:
                cp.start()
            started += first
        for which in (0, 1):
            for j, chip in enumerate(chips):
                copy(which, 1 + j, (*chip, c), me).wait_recv()
                fwd = copy(which, 4 + j, (*chip, c), sibling)
                fwd.start()
                started.append(fwd)
        for which in (0, 1):
            copy(which, 0, sibling, me).wait_recv()
            for j, chip in enumerate(chips):
                copy(which, 4 + j, (*chip, 1 - c), me).wait_recv()
        for cp in started:
            cp.wait_send()
        own.wait()

    return pl.pallas_call(
        body, name="gather_weights",
        out_shape=(jax.ShapeDtypeStruct((D_MODEL, IN_COLS), BF16),
                   jax.ShapeDtypeStruct((D_MODEL, D_MODEL), BF16)),
        in_specs=[pl.BlockSpec(memory_space=pltpu.VMEM), pl.BlockSpec(memory_space=pltpu.VMEM)],
        out_specs=(pl.BlockSpec(memory_space=pltpu.VMEM), pl.BlockSpec(memory_space=pltpu.VMEM)),
        scratch_shapes=[pltpu.VMEM((D_MODEL, COL_BLOCK), BF16),
                        pltpu.SemaphoreType.DMA((14,)), pltpu.SemaphoreType.DMA((14,)),
                        pltpu.SemaphoreType.DMA(())],
        compiler_params=_params(),
    )(w_in, w_out)


def _rope_tables(pos_col, inv_freq_lanes):
    tm = 512

    def body(pos_ref, invf_ref, c_ref, sa_ref, sb_ref):
        ang = pos_ref[...].astype(F32) * invf_ref[...]
        e = lax.broadcasted_iota(jnp.int32, (tm, LANES), 1) & (HEAD_DIM - 1)
        cos, sin = jnp.cos(ang), jnp.sin(ang)
        c_ref[...] = jnp.where(e < ROPE_DIMS, cos, 1.0)
        sa_ref[...] = jnp.where((e >= ROPE_HALF) & (e < ROPE_DIMS), sin, 0.0)
        sb_ref[...] = jnp.where(e < ROPE_HALF, -sin, 0.0)

    tab = jax.ShapeDtypeStruct((SEQ, LANES), F32)
    spec = pl.BlockSpec((tm, LANES), lambda i: (i, 0))
    return pl.pallas_call(
        body, name="rope_tables", grid=(SEQ // tm,), out_shape=(tab, tab, tab),
        in_specs=[pl.BlockSpec((tm, 1), lambda i: (i, 0)), pl.BlockSpec((1, LANES), lambda i: (0, 0))],
        out_specs=(spec, spec, spec), compiler_params=_params(("parallel",)),
    )(pos_col, inv_freq_lanes)


def _rot(t, c, sa, sb):
    n = t.shape[1]
    return t * c + pltpu.roll(t, ROPE_HALF, 1) * sa + pltpu.roll(t, n - ROPE_HALF, 1) * sb


def _rot_transposed(g, c, sa, sb):
    n = g.shape[1]
    return g * c + pltpu.roll(g * sa, n - ROPE_HALF, 1) + pltpu.roll(g * sb, ROPE_HALF, 1)


def _in_proj_fwd(x, mix_w, w_g, rc, rsa, rsb):
    tm = 256

    def body(x_ref, w_ref, wg_ref, c_ref, sa_ref, sb_ref, proj_ref, hnt_ref):
        xf = x_ref[...]
        ms = jnp.mean(xf * xf, axis=-1, keepdims=True)
        hn = xf * lax.rsqrt(ms + NORM_EPS) * w_ref[...]
        hnt_ref[...] = hn.T.astype(BF16)
        hb = hn.astype(BF16)
        c = jnp.tile(c_ref[...], (1, 4))
        sa = jnp.tile(sa_ref[...], (1, 4))
        sb = jnp.tile(sb_ref[...], (1, 4))
        for j in range(N_DEV):
            acc = _mm(hb, wg_ref[:, COL_BLOCK * j:COL_BLOCK * (j + 1)])
            if j < 2:
                acc = _rot(acc, c, sa, sb)
            proj_ref[:, COL_BLOCK * j:COL_BLOCK * (j + 1)] = acc

    tab = pl.BlockSpec((tm, LANES), lambda i: (i, 0))
    return pl.pallas_call(
        body, name="in_proj_fwd", grid=(SEQ // tm,),
        out_shape=(jax.ShapeDtypeStruct((SEQ, IN_COLS), F32),
                   jax.ShapeDtypeStruct((D_MODEL, SEQ), BF16)),
        in_specs=[pl.BlockSpec((tm, D_MODEL), lambda i: (i, 0)),
                  pl.BlockSpec((1, D_MODEL), lambda i: (0, 0)),
                  pl.BlockSpec((D_MODEL, IN_COLS), lambda i: (0, 0)),
                  tab, tab, tab],
        out_specs=(pl.BlockSpec((tm, IN_COLS), lambda i: (i, 0)),
                   pl.BlockSpec((D_MODEL, tm), lambda i: (0, i))),
        compiler_params=_params(("parallel",)),
    )(x, mix_w, w_g, rc, rsa, rsb)


ATTN_GROUP = 8
BLOCKS_PER_PATTERN = SEQ // ATTN_BLOCK


def _write_band_bias(bias_ref):
    qi = lax.broadcasted_iota(jnp.int32, (2 * ATTN_BLOCK, 2 * ATTN_BLOCK), 0) & (ATTN_BLOCK - 1)
    kj = lax.broadcasted_iota(jnp.int32, (2 * ATTN_BLOCK, 2 * ATTN_BLOCK), 1)
    bias_ref[0] = jnp.where((kj >= qi) & (kj <= qi + ATTN_BLOCK), 0.0, NEG_BIG)
    bias_ref[1] = jnp.where(kj <= qi, 0.0, NEG_BIG)


def _head0_lanes():
    return lax.broadcasted_iota(jnp.int32, (ATTN_BLOCK, LANES), 1) < HEAD_DIM


def _stack_heads(t, h0):
    return jnp.concatenate([jnp.where(h0, t, 0.0), jnp.where(h0, 0.0, t)], axis=0).astype(BF16)


def _strided(start, size, d):
    return pl.ds(start, size) if d == 1 else pl.ds(start, size, stride=d)


def _block_place(i, d):
    nblk = BLOCKS_PER_PATTERN // d
    r, n = i // nblk, i % nblk
    kn = jnp.maximum(n - 1, 0)
    row0, key0 = n * (d * ATTN_BLOCK) + r, kn * (d * ATTN_BLOCK) + r
    if d == 1:
        row0, key0 = pl.multiple_of(row0, ATTN_BLOCK), pl.multiple_of(key0, ATTN_BLOCK)
    return _strided(row0, ATTN_BLOCK, d), _strided(key0, 2 * ATTN_BLOCK, d), (n == 0).astype(jnp.int32)


def _for_each_group(d, load, compute, store):
    def group(g, carry):
        items = [load(*_block_place(g * ATTN_GROUP + u, d)) for u in range(ATTN_GROUP)]
        results = [compute(item) for item in items]
        for item, res in zip(items, results):
            store(item, res)
        return carry

    lax.fori_loop(0, BLOCKS_PER_PATTERN // ATTN_GROUP, group, 0)


def _attn_fwd_fused(proj):
    n_pat = len(DILATIONS)
    tile2 = (2 * ATTN_BLOCK, LANES)

    def body(q_ref, k_ref, v_ref, o_ref, lse_ref, m_acc, l_acc, bias_ref):
        _write_band_bias(bias_ref)
        h0 = _head0_lanes()
        for pi, d in enumerate(DILATIONS):
            first, last = pi == 0, pi == n_pat - 1

            def load(rows, keys, which, first=first):
                item = dict(rows=rows, keys=keys, which=which)
                if not first:
                    item.update(o=o_ref[rows, :], m=[m_acc.at[h][rows, :] for h in range(2)],
                                l=[l_acc.at[h][rows, :] for h in range(2)])
                return item

            def compute(item, first=first):
                kb = k_ref[item["keys"], :].astype(BF16)
                vb = v_ref[item["keys"], :].astype(BF16)
                s = _mm_nt(_stack_heads(q_ref[item["rows"], :], h0), kb) * 0.125 + bias_ref[item["which"]]
                mb = jnp.max(s, axis=-1, keepdims=True)
                if first:
                    p = jnp.exp(s - mb)
                    mn = jnp.broadcast_to(mb, tile2)
                else:
                    m_old = jnp.concatenate(item["m"], axis=0)
                    mn = jnp.maximum(m_old, mb)
                    alpha = jnp.exp(m_old - mn)
                    p = jnp.exp(s - jnp.concatenate([mn, mn], axis=1))
                ls = jnp.sum(p, axis=-1, keepdims=True)
                pv = _mm(p.astype(BF16), vb)
                if first:
                    return pv, mn, jnp.broadcast_to(ls, tile2)
                o_old = jnp.concatenate([item["o"], item["o"]], axis=0)
                return alpha * o_old + pv, mn, alpha * jnp.concatenate(item["l"], axis=0) + ls

            def store(item, res, last=last):
                rows = item["rows"]
                (o0, o1), (m0, m1), (l0, l1) = ((a[:ATTN_BLOCK], a[ATTN_BLOCK:]) for a in res)
                if last:
                    o_ref[rows, :] = jnp.where(h0, o0 / l0, o1 / l1)
                    lse_ref[rows, :] = jnp.where(h0, m0 + jnp.log(l0), m1 + jnp.log(l1))
                else:
                    o_ref[rows, :] = jnp.where(h0, o0, o1)
                    m_acc.at[0][rows, :], m_acc.at[1][rows, :] = m0, m1
                    l_acc.at[0][rows, :], l_acc.at[1][rows, :] = l0, l1

            _for_each_group(d, load, compute, store)

    slab = lambda g: pl.BlockSpec((SEQ, LANES), functools.partial(lambda hp, g: (0, 4 * g + hp), g=g))
    wide = jax.ShapeDtypeStruct((SEQ, ATTN_WIDTH), F32)
    return pl.pallas_call(
        body, name="attn_fwd", grid=(4,), out_shape=(wide, wide),
        in_specs=[slab(0), slab(1), slab(2)], out_specs=(slab(0), slab(0)),
        scratch_shapes=[pltpu.VMEM((2, SEQ, LANES), F32), pltpu.VMEM((2, SEQ, LANES), F32),
                        pltpu.VMEM((2, 2 * ATTN_BLOCK, 2 * ATTN_BLOCK), F32)],
        compiler_params=_params(("parallel",)),
    )(proj, proj, proj)


def _attn_bwd_fused(proj, d_out, lse, delta):
    def body(q_ref, k_ref, v_ref, do_ref, lse_ref, del_ref, dq_ref, dk_ref, dv_ref, bias_ref):
        _write_band_bias(bias_ref)
        dk_ref[...] = jnp.zeros_like(dk_ref)
        dv_ref[...] = jnp.zeros_like(dv_ref)
        h0 = _head0_lanes()
        for pi, d in enumerate(DILATIONS):
            first = pi == 0

            def load(rows, keys, which):
                return dict(rows=rows, keys=keys, q=q_ref[rows, :], g=do_ref[rows, :], lse=lse_ref[rows, :],
                            delta=del_ref[rows, :], k=k_ref[keys, :].astype(BF16),
                            v=v_ref[keys, :].astype(BF16), bias=bias_ref[which])

            def per_head(t):
                swapped = pltpu.roll(t, HEAD_DIM, 1)
                both = jnp.concatenate([jnp.where(h0, t, swapped), jnp.where(h0, swapped, t)], axis=0)
                return jnp.concatenate([both, both], axis=1)

            def compute(item):
                q2, g2 = _stack_heads(item["q"], h0), _stack_heads(item["g"], h0)
                s = _mm_nt(q2, item["k"]) * 0.125 + item["bias"]
                p = jnp.exp(s - per_head(item["lse"]))
                dp = _mm_nt(g2, item["v"])
                ds = (p * (dp - per_head(item["delta"])) * 0.125).astype(BF16)
                dq2 = _mm(ds, item["k"])
                dq = jnp.where(h0, dq2[:ATTN_BLOCK], dq2[ATTN_BLOCK:])
                return dq, _mm_tn(ds, q2), _mm_tn(p.astype(BF16), g2)

            def store(item, res, first=first):
                rows, keys = item["rows"], item["keys"]
                if first:
                    dq_ref[rows, :] = res[0]
                else:
                    dq_ref[rows, :] += res[0]
                dk_ref[keys, :] += res[1]
                dv_ref[keys, :] += res[2]

            _for_each_group(d, load, compute, store)

    slab = lambda g: pl.BlockSpec((SEQ, LANES), functools.partial(lambda hp, g: (0, 4 * g + hp), g=g))
    wide = jax.ShapeDtypeStruct((SEQ, ATTN_WIDTH), F32)
    return pl.pallas_call(
        body, name="attn_bwd", grid=(4,), out_shape=(wide, wide, wide),
        scratch_shapes=[pltpu.VMEM((2, 2 * ATTN_BLOCK, 2 * ATTN_BLOCK), F32)],
        in_specs=[slab(0), slab(1), slab(2), slab(0), slab(0), slab(0)], out_specs=(slab(0), slab(0), slab(0)),
        compiler_params=_params(("parallel",)),
    )(proj, proj, proj, d_out, lse, delta)


def _hgrn_lower_bound(lb_ref):
    r0, r1 = lb_ref[0:1, :], lb_ref[1:2, :]
    mx = jnp.maximum(r0, r1)
    e0, e1 = jnp.exp(r0 - mx), jnp.exp(r1 - mx)
    return e0 / (e0 + e1)


def _hgrn_gates(hq, hf, lb):
    sq = _sigmoid(hq)
    sg = _sigmoid(hf)
    f = lb + (1.0 - lb) * sg
    return hq * sq, sq, sg, f, 1.0 - f, jnp.log(f)


HGRN_PAIR = 2
HGRN_SEQ_BLOCK = 1024
HGRN_GROUP = 4
HGRN_ROWS = HGRN_GROUP * HGRN_CHUNK


def _hgrn_specs(reverse):
    n_blocks = SEQ // HGRN_SEQ_BLOCK
    width = HGRN_PAIR * HGRN_DIM
    blk = (lambda s: n_blocks - 1 - s) if reverse else (lambda s: s)
    cols = lambda g: pl.BlockSpec((HGRN_SEQ_BLOCK, width),
                                  functools.partial(lambda p, s, g: (blk(s), HGRN_PAIR * g + p), g=g))
    pair = pl.BlockSpec((HGRN_SEQ_BLOCK, width), lambda p, s: (blk(s), p))
    lb = pl.BlockSpec((2, width), lambda p, s: (0, p))
    states = pl.BlockSpec((HGRN_PAIR, HGRN_SEQ_BLOCK // HGRN_CHUNK, HGRN_DIM, HGRN_DIM),
                          lambda p, s: (p, blk(s), 0, 0))
    return cols, pair, lb, states


def _chunk_masks():
    ri = lax.broadcasted_iota(jnp.int32, (HGRN_ROWS, HGRN_ROWS), 0)
    ci = lax.broadcasted_iota(jnp.int32, (HGRN_ROWS, HGRN_ROWS), 1)
    same = (ri // HGRN_CHUNK) == (ci // HGRN_CHUNK)
    return same, same & (ri >= ci), same & (ri <= ci)


def _mm_select(sel, v):
    hi = v.astype(BF16)
    r1 = v - hi.astype(F32)
    mid = r1.astype(BF16)
    lo = (r1 - mid.astype(F32)).astype(BF16)
    return _mm(sel, hi) + _mm(sel, mid) + _mm(sel, lo)


def _head_cols(a, h):
    return a[:, HGRN_DIM * h:HGRN_DIM * (h + 1)]


def _hgrn_fwd(proj, lb_raw):
    t, rws = HGRN_CHUNK, HGRN_ROWS

    def body(hq_ref, hf_ref, hi_ref, lb_ref, rec_ref, st_ref, state):
        @pl.when(pl.program_id(1) == 0)
        def _():
            state[...] = jnp.zeros_like(state)

        lb = _hgrn_lower_bound(lb_ref)
        same, causal, _ = _chunk_masks()
        sel = jnp.concatenate([causal, same], axis=0).astype(BF16)

        def group(g, sts):
            rows = pl.ds(pl.multiple_of(g * rws, rws), rws)
            q, _, _, _, k, lf = _hgrn_gates(hq_ref[rows, :], hf_ref[rows, :], lb)
            sums = _mm_select(sel, lf)
            cum, last = sums[:rws], sums[rws:]
            qd = (q * jnp.exp(cum)).astype(BF16)
            ki = (k * jnp.exp(-cum)).astype(BF16)
            ke = (k * jnp.exp(last - cum)).astype(BF16)
            vb = hi_ref[rows, :].astype(BF16)
            dec = jnp.exp(last)
            new_sts, recs = [], []
            for h in range(HGRN_PAIR):
                qd_h, ke_h, vb_h = _head_cols(qd, h), _head_cols(ke, h), _head_cols(vb, h)
                att = jnp.where(causal, _mm_nt(qd_h, _head_cols(ki, h)), 0.0).astype(BF16)
                intra = _mm(att, vb_h)
                st = sts[h]
                outs = []
                for c in range(HGRN_GROUP):
                    sl = slice(c * t, (c + 1) * t)
                    st_ref[h, g * HGRN_GROUP + c] = st
                    outs.append(intra[sl] + _mm_nt(qd_h[sl], st.astype(BF16)))
                    st = st * _head_cols(dec[c * t:c * t + 1, :], h) + _mm_tn(vb_h[sl], ke_h[sl])
                new_sts.append(st)
                recs.append(jnp.concatenate(outs, axis=0))
            rec_ref[rows, :] = jnp.concatenate(recs, axis=1)
            return tuple(new_sts)

        sts = lax.fori_loop(0, HGRN_SEQ_BLOCK // rws, group, tuple(state[h] for h in range(HGRN_PAIR)))
        for h in range(HGRN_PAIR):
            state[h] = sts[h]

    cols, pair, lb, states = _hgrn_specs(reverse=False)
    return pl.pallas_call(
        body, name="hgrn_fwd", grid=(HGRN_HEADS // HGRN_PAIR, SEQ // HGRN_SEQ_BLOCK),
        out_shape=(jax.ShapeDtypeStruct((SEQ, HGRN_WIDTH), F32),
                   jax.ShapeDtypeStruct((HGRN_HEADS, N_CHUNKS, HGRN_DIM, HGRN_DIM), F32)),
        in_specs=[cols(4), cols(5), cols(6), lb], out_specs=(pair, states),
        scratch_shapes=[pltpu.VMEM((HGRN_PAIR, HGRN_DIM, HGRN_DIM), F32)],
        compiler_params=_params(("parallel", "arbitrary")),
    )(proj, proj, proj, lb_raw)


def _hgrn_bwd(proj, lb_raw, d_rec, states):
    t, rws = HGRN_CHUNK, HGRN_ROWS

    def body(hq_ref, hf_ref, hi_ref, lb_ref, do_ref, st_ref, dhq_ref, dhf_ref, dhi_ref, dlb_ref,
             dstate, dlb_acc):
        lb = _hgrn_lower_bound(lb_ref)
        same, causal, anti = _chunk_masks()
        sel = jnp.concatenate([causal, same], axis=0).astype(BF16)
        sel_t = jnp.concatenate([anti, same], axis=1).astype(BF16)
        @pl.when(pl.program_id(1) == 0)
        def _():
            dstate[...] = jnp.zeros_like(dstate)
            dlb_acc[...] = jnp.zeros_like(dlb_acc)

        n_groups = HGRN_SEQ_BLOCK // rws
        chunks = [slice(c * t, (c + 1) * t) for c in range(HGRN_GROUP)]

        def group(i, dsts_in):
            g = n_groups - 1 - i
            rows = pl.ds(pl.multiple_of(g * rws, rws), rws)
            hq = hq_ref[rows, :]
            q, sq, sg, f, k, lf = _hgrn_gates(hq, hf_ref[rows, :], lb)
            sums = _mm_select(sel, lf)
            cum, last = sums[:rws], sums[rws:]
            e_cum, e_inv, e_end, dec = jnp.exp(cum), jnp.exp(-cum), jnp.exp(last - cum), jnp.exp(last)
            qd, ki, ke = q * e_cum, k * e_inv, k * e_end
            qdb, kib, keb = qd.astype(BF16), ki.astype(BF16), ke.astype(BF16)
            vb = hi_ref[rows, :].astype(BF16)
            gb = do_ref[rows, :].astype(BF16)

            dsts_out, per_head = [], []
            for h in range(HGRN_PAIR):
                qdb_h, kib_h, keb_h = _head_cols(qdb, h), _head_cols(kib, h), _head_cols(keb, h)
                vb_h, gb_h = _head_cols(vb, h), _head_cols(gb, h)
                att = jnp.where(causal, _mm_nt(qdb_h, kib_h), 0.0).astype(BF16)
                datt = jnp.where(causal, _mm_nt(gb_h, vb_h), 0.0).astype(BF16)
                dv = _mm_tn(att, gb_h)
                dqd = _mm(datt, kib_h)
                dki = _mm_tn(datt, qdb_h)

                decs = [_head_cols(dec[c * t:c * t + 1, :], h) for c in range(HGRN_GROUP)]
                dsts = [None] * HGRN_GROUP
                dst = dsts_in[h]
                for c in reversed(range(HGRN_GROUP)):
                    dsts[c] = dst
                    dst = dst * decs[c] + _mm_tn(gb_h[chunks[c]], qdb_h[chunks[c]])
                dsts_out.append(dst)

                dv_x, dqd_x, dke, dlast_x = [], [], [], []
                for c, sl in enumerate(chunks):
                    st_prev = st_ref[h, g * HGRN_GROUP + c]
                    dstb = dsts[c].astype(BF16)
                    dv_x.append(_mm_nt(keb_h[sl], dstb))
                    dqd_x.append(_mm(gb_h[sl], st_prev.astype(BF16)))
                    dke.append(_mm(vb_h[sl], dstb))
                    ddec = jnp.sum(dsts[c] * st_prev, axis=0, keepdims=True)
                    dlast_x.append(jnp.broadcast_to(ddec * decs[c], (t, HGRN_DIM)))
                per_head.append((dv + jnp.concatenate(dv_x, axis=0), dqd + jnp.concatenate(dqd_x, axis=0),
                                 dki, jnp.concatenate(dke, axis=0), jnp.concatenate(dlast_x, axis=0)))
            dv, dqd, dki, dke, dlast = (jnp.concatenate([a, b], axis=1) for a, b in zip(*per_head))

            dq = dqd * e_cum
            dk = dki * e_inv + dke * e_end
            dke_ke = dke * ke
            dcum = dqd * qd - dki * ki - dke_ke
            dlf = _mm_select(sel_t, jnp.concatenate([dcum, dke_ke], axis=0)) + dlast
            df = dlf / f - dk
            dhq_ref[rows, :] = dq * (sq * (1.0 + hq * (1.0 - sq)))
            dhf_ref[rows, :] = df * (1.0 - lb) * (sg * (1.0 - sg))
            dhi_ref[rows, :] = dv
            dlb_acc[...] += jnp.sum(df * (1.0 - sg), axis=0, keepdims=True)
            return tuple(dsts_out)

        dsts = lax.fori_loop(0, n_groups, group, tuple(dstate[h] for h in range(HGRN_PAIR)))
        for h in range(HGRN_PAIR):
            dstate[h] = dsts[h]
        g0 = dlb_acc[...] * lb * (1.0 - lb)
        dlb_ref[...] = jnp.concatenate([g0, -g0], axis=0)

    cols, pair, lb_spec, st_spec = _hgrn_specs(reverse=True)
    wide = jax.ShapeDtypeStruct((SEQ, HGRN_WIDTH), F32)
    return pl.pallas_call(
        body, name="hgrn_bwd", grid=(HGRN_HEADS // HGRN_PAIR, SEQ // HGRN_SEQ_BLOCK),
        out_shape=(wide, wide, wide, jax.ShapeDtypeStruct((2, HGRN_WIDTH), F32)),
        in_specs=[cols(4), cols(5), cols(6), lb_spec, pair, st_spec],
        out_specs=(pair, pair, pair, lb_spec),
        scratch_shapes=[pltpu.VMEM((HGRN_PAIR, HGRN_DIM, HGRN_DIM), F32),
                        pltpu.VMEM((1, HGRN_PAIR * HGRN_DIM), F32)],
        compiler_params=_params(("parallel", "arbitrary")),
    )(proj, proj, proj, lb_raw, d_rec, states)


def _group_sum(v, group):
    parts = []
    for s in range(v.shape[1] // LANES):
        slab = v[:, LANES * s:LANES * (s + 1)]
        if group == LANES:
            parts.append(jnp.broadcast_to(jnp.sum(slab, axis=-1, keepdims=True), slab.shape))
        else:
            h0 = lax.broadcasted_iota(jnp.int32, slab.shape, 1) < HEAD_DIM
            s0 = jnp.sum(jnp.where(h0, slab, 0.0), axis=-1, keepdims=True)
            s1 = jnp.sum(jnp.where(h0, 0.0, slab), axis=-1, keepdims=True)
            parts.append(jnp.where(h0, s0, s1))
    return jnp.concatenate(parts, axis=1)


def _mid(attn_o, rec, proj, x, target, w_out_g, attn_w, hgrn_w, final_w):
    tm = 256

    def branch_fwd(o, gate, w, group):
        r = lax.rsqrt(_group_sum(o * o, group) * (1.0 / group) + NORM_EPS)
        nrm = o * r
        sg = _sigmoid(gate)
        return r, nrm, sg, nrm * w * (gate * sg)

    def branch_bwd(dy, r, nrm, sg, gate, w, group):
        silu = gate * sg
        d_gate = dy * nrm * w * (sg * (1.0 + gate * (1.0 - sg)))
        d_w = jnp.sum(dy * nrm * silu, axis=0, keepdims=True)
        dn = dy * w * silu
        d_o = r * (dn - nrm * (_group_sum(dn * nrm, group) * (1.0 / group)))
        return d_o, d_gate, d_w

    def body(o_ref, rec_ref, ag_ref, hg_ref, x_ref, tgt_ref, wout_ref, aw_ref, hw_ref, fw_ref,
             dx2_ref, do_ref, delta_ref, dag_ref, drec_ref, dhg_ref, dwout_ref, dfw_ref, daw_ref, dhw_ref,
             loss_ref, dwout_acc):
        i = pl.program_id(0)

        @pl.when(i == 0)
        def _():
            dwout_acc[...] = jnp.zeros_like(dwout_acc)
            dfw_ref[...] = jnp.zeros_like(dfw_ref)
            daw_ref[...] = jnp.zeros_like(daw_ref)
            dhw_ref[...] = jnp.zeros_like(dhw_ref)
            loss_ref[...] = jnp.zeros_like(loss_ref)

        o, rc, ag, hg = o_ref[...], rec_ref[...], ag_ref[...], hg_ref[...]
        aw, hw, fw = aw_ref[...], hw_ref[...], fw_ref[...]
        ra, na, sga, ya = branch_fwd(o, ag, aw, HEAD_DIM)
        rh, nh, sgh, yh = branch_fwd(rc, hg, hw, HGRN_DIM)
        mixed = jnp.concatenate([ya, yh], axis=1).astype(BF16)
        wout = wout_ref[...]
        x2 = x_ref[...] + _mm(mixed, wout)
        rstd = lax.rsqrt(jnp.mean(x2 * x2, axis=-1, keepdims=True) + NORM_EPS)
        xn = x2 * rstd
        err = xn * fw - tgt_ref[...]
        row_loss = jnp.mean(err * err, axis=-1, keepdims=True)
        loss_ref[...] += 0.5 * jnp.sum(row_loss, axis=0, keepdims=True)
        dy = err * (1.0 / D_MODEL)
        dfw_ref[...] += jnp.sum(dy * xn, axis=0, keepdims=True)
        dxn = dy * fw
        dx2 = rstd * (dxn - xn * jnp.mean(dxn * xn, axis=-1, keepdims=True))
        dx2_ref[...] = dx2
        dx2b = dx2.astype(BF16)
        dwout_acc[...] += _mm_tn(mixed, dx2b)

        @pl.when(i == pl.num_programs(0) - 1)
        def _():
            dwout_ref[...] = dwout_acc[...].astype(BF16)

        dmixed = _mm_nt(dx2b, wout)

        d_o, d_ag, d_aw = branch_bwd(dmixed[:, :ATTN_WIDTH], ra, na, sga, ag, aw, HEAD_DIM)
        d_rec, d_hg, d_hw = branch_bwd(dmixed[:, ATTN_WIDTH:], rh, nh, sgh, hg, hw, HGRN_DIM)
        do_ref[...] = d_o
        delta_ref[...] = _group_sum(d_o * o, HEAD_DIM)
        dag_ref[...] = d_ag
        drec_ref[...] = d_rec
        dhg_ref[...] = d_hg
        daw_ref[...] += d_aw
        dhw_ref[...] += d_hw

    half = lambda: pl.BlockSpec((tm, COL_BLOCK), lambda i: (i, 0))
    full = lambda: pl.BlockSpec((tm, D_MODEL), lambda i: (i, 0))
    fixed = lambda r, c: pl.BlockSpec((r, c), lambda i: (0, 0))
    wide = jax.ShapeDtypeStruct((SEQ, COL_BLOCK), F32)
    return pl.pallas_call(
        body, name="mid", grid=(SEQ // tm,),
        out_shape=(jax.ShapeDtypeStruct((SEQ, D_MODEL), F32), wide, wide, wide, wide, wide,
                   jax.ShapeDtypeStruct((D_MODEL, D_MODEL), BF16),
                   jax.ShapeDtypeStruct((1, D_MODEL), F32), jax.ShapeDtypeStruct((1, COL_BLOCK), F32),
                   jax.ShapeDtypeStruct((1, COL_BLOCK), F32), jax.ShapeDtypeStruct((1, 1), F32)),
        scratch_shapes=[pltpu.VMEM((D_MODEL, D_MODEL), F32)],
        in_specs=[half(), half(),
                  pl.BlockSpec((tm, COL_BLOCK), lambda i: (i, 3)), pl.BlockSpec((tm, COL_BLOCK), lambda i: (i, 7)),
                  full(), full(), fixed(D_MODEL, D_MODEL), fixed(1, COL_BLOCK), fixed(1, COL_BLOCK),
                  fixed(1, D_MODEL)],
        out_specs=(full(), half(), half(), half(), half(), half(), fixed(D_MODEL, D_MODEL),
                   fixed(1, D_MODEL), fixed(1, COL_BLOCK), fixed(1, COL_BLOCK), fixed(1, 1)),
        compiler_params=_params(("arbitrary",)),
    )(attn_o, rec, proj, proj, x, target, w_out_g, attn_w, hgrn_w, final_w)


def _in_proj_bwd_rows(d_groups, w_g, x, dx2, mix_w, rc, rsa, rsb):
    tm = 256

    def body(*refs):
        dg_refs = refs[:N_DEV]
        wg_ref, x_ref, dx2_ref, w_ref, c_ref, sa_ref, sb_ref, gx_ref, dpb_ref, dmw_ref = refs[N_DEV:]

        @pl.when(pl.program_id(0) == 0)
        def _():
            dmw_ref[...] = jnp.zeros_like(dmw_ref)

        parts = []
        for j in range(N_DEV):
            dp = dg_refs[j][...]
            if j < 2:
                dp = _rot_transposed(dp, jnp.tile(c_ref[...], (1, 4)), jnp.tile(sa_ref[...], (1, 4)),
                                     jnp.tile(sb_ref[...], (1, 4)))
            parts.append(dp.astype(BF16))
        dpb = jnp.concatenate(parts, axis=1)
        dpb_ref[...] = dpb
        g = _mm_nt(dpb, wg_ref[...])
        xf = x_ref[...]
        rstd = lax.rsqrt(jnp.mean(xf * xf, axis=-1, keepdims=True) + NORM_EPS)
        xn = xf * rstd
        dmw_ref[...] += jnp.sum(g * xn, axis=0, keepdims=True)
        gw = g * w_ref[...]
        gx_ref[...] = dx2_ref[...] + rstd * (gw - xn * jnp.mean(gw * xn, axis=-1, keepdims=True))

    tile = lambda cols: pl.BlockSpec((tm, cols), lambda i: (i, 0))
    fixed = lambda r, c: pl.BlockSpec((r, c), lambda i: (0, 0))
    return pl.pallas_call(
        body, name="in_proj_bwd_rows", grid=(SEQ // tm,),
        out_shape=(jax.ShapeDtypeStruct((SEQ, D_MODEL), F32), jax.ShapeDtypeStruct((SEQ, IN_COLS), BF16),
                   jax.ShapeDtypeStruct((1, D_MODEL), F32)),
        in_specs=[tile(COL_BLOCK) for _ in range(N_DEV)] + [
            fixed(D_MODEL, IN_COLS), tile(D_MODEL), tile(D_MODEL), fixed(1, D_MODEL),
            tile(LANES), tile(LANES), tile(LANES)],
        out_specs=(tile(D_MODEL), tile(IN_COLS), fixed(1, D_MODEL)),
        compiler_params=_params(("arbitrary",)),
    )(*d_groups, w_g, x, dx2, mix_w, rc, rsa, rsb)


def _in_proj_bwd_weights(hn_t, dproj_b):
    def body(hnt_ref, dp_ref, dwin_ref):
        dwin_ref[0] = _mm(hnt_ref[...], dp_ref[...]).astype(BF16)

    return pl.pallas_call(
        body, name="in_proj_bwd_weights", grid=(N_DEV,),
        out_shape=jax.ShapeDtypeStruct((N_DEV, D_MODEL, COL_BLOCK), BF16),
        in_specs=[pl.BlockSpec((D_MODEL, SEQ), lambda j: (0, 0)), pl.BlockSpec((SEQ, COL_BLOCK), lambda j: (0, j))],
        out_specs=pl.BlockSpec((1, D_MODEL, COL_BLOCK), lambda j: (j, 0, 0)),
        compiler_params=_params(("parallel",)),
    )(hn_t, dproj_b)


def _adamw(w, g, m, v):
    m = ADAM_B1 * m + (1.0 - ADAM_B1) * g
    v = ADAM_B2 * v + (1.0 - ADAM_B2) * (g * g)
    m_hat = m / (1.0 - ADAM_B1 ** ADAM_STEP)
    v_hat = v / (1.0 - ADAM_B2 ** ADAM_STEP)
    delta = -ADAM_LR * (m_hat / (jnp.sqrt(v_hat) + ADAM_EPS) + ADAM_WD * w)
    return delta, m, v


def _exchange_update(dwin_p, dwout_p, small_p, w_in, m_in, v_in, w_out, m_out, v_out, w_s, m_s, v_s):
    rb = 128
    n_chips = N_DEV // 2
    S1_IN, S1_OUT, SMALL, S2_IN, S2_OUT = 0, 4, 8, 15, 18

    def body(dwin_hbm, dwout_hbm, small_ref, win_ref, min_ref, vin_ref, wout_ref, mout_ref, vout_ref,
             ws_ref, ms_ref, vs_ref,
             gin_ref, din_ref, nmin_ref, nvin_ref, gout_ref, dout_ref, nmout_ref, nvout_ref,
             gs_ref, ds_ref, nms_ref, nvs_ref,
             own_in, own_out, s1_in, s1_out, fwd_in, fwd_out, s2_in, s2_out, land_s,
             send_sems, recv_sems, local_sems):
        me = _my_place()
        x, y, c = me
        my_chip = 2 * x + y
        sibling = (x, y, 1 - c)

        def remote(slot, src, dst, to):
            return pltpu.make_async_remote_copy(src_ref=src, dst_ref=dst, send_sem=send_sems.at[slot],
                                                recv_sem=recv_sems.at[slot], device_id=to, device_id_type=MESH)

        def stage1(q):
            return [remote(S1_IN + q, dwin_hbm.at[q, 1 - c], s1_in.at[q], sibling),
                    remote(S1_OUT + q, dwout_hbm.at[q, 1 - c], s1_out.at[q], sibling)]

        def stage2(rel):
            peer = _peer(me, 2 * rel)
            return [remote(S2_IN + rel - 1, fwd_in.at[rel - 1], s2_in.at[rel - 1], peer),
                    remote(S2_OUT + rel - 1, fwd_out.at[rel - 1], s2_out.at[rel - 1], peer)]

        def small_copy(rel):
            return remote(SMALL + rel - 1, small_ref, land_s.at[rel], _peer(me, rel))

        mine = [pltpu.make_async_copy(dwin_hbm.at[:, c], own_in, local_sems.at[0]),
                pltpu.make_async_copy(dwout_hbm.at[:, c], own_out, local_sems.at[1])]
        for cp in mine:
            cp.start()
        sent = []
        for q in range(n_chips):
            sent += stage1(q)
        land_s[0] = small_ref[...]
        sent += [small_copy(rel) for rel in range(1, N_DEV)]
        for cp in sent:
            cp.start()
        for cp in mine:
            cp.wait()

        def add_blocks(q, own, got, n_rows, dst):
            def step(b, carry):
                rows = pl.ds(pl.multiple_of(b * rb, rb), rb)
                dst[rows, :] = (own[q, rows, :].astype(F32) + got[q, rows, :].astype(F32)).astype(dst.dtype)
                return carry
            lax.fori_loop(0, n_rows // rb, step, 0)

        for rel in range(1, n_chips):
            q = my_chip ^ rel
            for cp in stage1(q):
                cp.wait_recv()
            add_blocks(q, own_in, s1_in, D_MODEL, fwd_in.at[rel - 1])
            add_blocks(q, own_out, s1_out, WOUT_ROWS, fwd_out.at[rel - 1])
            for cp in stage2(rel):
                cp.start()
                sent.append(cp)
        for cp in stage1(my_chip):
            cp.wait_recv()
        add_blocks(my_chip, own_in, s1_in, D_MODEL, gin_ref)
        add_blocks(my_chip, own_out, s1_out, WOUT_ROWS, gout_ref)
        for rel in range(1, n_chips):
            for cp in stage2(rel):
                cp.wait_recv()

        def update(got, w_ref, m_ref, v_ref, g_ref, d_ref, nm_ref, nv_ref, n_rows):
            def step(b, carry):
                rows = pl.ds(pl.multiple_of(b * rb, rb), rb)
                g = g_ref[rows, :]
                for rel in range(1, n_chips):
                    g = g + got[rel - 1, rows, :].astype(F32)
                delta, nm, nv = _adamw(w_ref[rows, :], g, m_ref[rows, :], v_ref[rows, :])
                g_ref[rows, :] = g
                d_ref[rows, :] = delta
                nm_ref[rows, :] = nm
                nv_ref[rows, :] = nv
                return carry
            lax.fori_loop(0, n_rows // rb, step, 0)

        update(s2_in, win_ref, min_ref, vin_ref, gin_ref, din_ref, nmin_ref, nvin_ref, D_MODEL)
        update(s2_out, wout_ref, mout_ref, vout_ref, gout_ref, dout_ref, nmout_ref, nvout_ref, WOUT_ROWS)

        for rel in range(1, N_DEV):
            small_copy(rel).wait_recv()
        my_flat = _flat(me)
        g = land_s[my_flat ^ 0]
        for dev in range(1, N_DEV):
            g = g + land_s[my_flat ^ dev]
        delta, nm, nv = _adamw(ws_ref[...], g, ms_ref[...], vs_ref[...])
        gs_ref[...] = g
        ds_ref[...] = delta
        nms_ref[...] = nm
        nvs_ref[...] = nv
        for cp in sent:
            cp.wait_send()

    vm = lambda: pl.BlockSpec(memory_space=pltpu.VMEM)
    anyspace = lambda: pl.BlockSpec(memory_space=pl.ANY)
    big = jax.ShapeDtypeStruct((D_MODEL, COL_BLOCK), F32)
    flat = jax.ShapeDtypeStruct((WOUT_ROWS, D_MODEL), F32)
    small = jax.ShapeDtypeStruct((SMALL_ROWS, LANES), F32)
    in_blocks = lambda n: pltpu.VMEM((n, D_MODEL, COL_BLOCK), BF16)
    out_blocks = lambda n: pltpu.VMEM((n, WOUT_ROWS, D_MODEL), BF16)
    return pl.pallas_call(
        body, name="exchange_update",
        out_shape=tuple([big] * 4 + [flat] * 4 + [small] * 4),
        in_specs=[anyspace(), anyspace()] + [vm() for _ in range(10)],
        out_specs=tuple(vm() for _ in range(12)),
        scratch_shapes=[in_blocks(n_chips), out_blocks(n_chips), in_blocks(n_chips), out_blocks(n_chips),
                        in_blocks(n_chips - 1), out_blocks(n_chips - 1),
                        in_blocks(n_chips - 1), out_blocks(n_chips - 1),
                        pltpu.VMEM((N_DEV, SMALL_ROWS, LANES), F32),
                        pltpu.SemaphoreType.DMA((21,)), pltpu.SemaphoreType.DMA((21,)),
                        pltpu.SemaphoreType.DMA((2,))],
        compiler_params=_params(),
    )(dwin_p.reshape(n_chips, 2, D_MODEL, COL_BLOCK), dwout_p.reshape(n_chips, 2, WOUT_ROWS, D_MODEL),
      small_p, w_in, m_in, v_in, w_out, m_out, v_out, w_s, m_s, v_s)


def _pack_small(mix, attn, hgrn, lb, final, loss=None):
    def rows8(a):
        a = a.reshape(-1, LANES)
        return jnp.pad(a, ((0, 8 - a.shape[0]), (0, 0)))
    last = jnp.zeros((8, LANES), F32) if loss is None else jnp.pad(loss.reshape(1, 1), ((0, 7), (0, LANES - 1)))
    return jnp.concatenate([rows8(mix), rows8(attn), rows8(hgrn), rows8(lb), rows8(final), last], axis=0)


def _unpack_small(slab):
    return (slab[ROW_MIX:ROW_MIX + 8].reshape(1, D_MODEL), slab[ROW_ATTN:ROW_ATTN + 4].reshape(1, ATTN_WIDTH),
            slab[ROW_HGRN:ROW_HGRN + 4].reshape(1, HGRN_WIDTH), slab[ROW_LB:ROW_LB + 8].reshape(2, HGRN_WIDTH),
            slab[ROW_FINAL:ROW_FINAL + 8].reshape(D_MODEL))


def _local_step(x, pos_col, w_in_g, w_out_g, mix_w, attn_w, hgrn_w, lb_raw, final_w, target):
    inv = ROPE_THETA ** (-jnp.arange(ROPE_HALF, dtype=F32) * (2.0 / ROPE_DIMS))
    lane_e = jnp.arange(LANES) % HEAD_DIM
    inv_lanes = jnp.where(lane_e < ROPE_DIMS, inv[lane_e % ROPE_HALF], 0.0).reshape(1, LANES)
    rc, rsa, rsb = _rope_tables(pos_col, inv_lanes)

    proj, hn_t = _in_proj_fwd(x, mix_w, w_in_g, rc, rsa, rsb)
    attn_o, lse = _attn_fwd_fused(proj)
    rec, states = _hgrn_fwd(proj, lb_raw)

    (dx2, d_o, delta, d_ag, d_rec, d_hg, dwout_p, d_final, d_attn_w, d_hgrn_w, loss) = _mid(
        attn_o, rec, proj, x, target, w_out_g, attn_w, hgrn_w, final_w.reshape(1, D_MODEL))

    dqkv = _attn_bwd_fused(proj, d_o, lse, delta)
    d_hq, d_hf, d_hi, d_lb = _hgrn_bwd(proj, lb_raw, d_rec, states)

    grad_x, dproj_b, d_mix = _in_proj_bwd_rows(
        (dqkv[0], dqkv[1], dqkv[2], d_ag, d_hq, d_hf, d_hi, d_hg), w_in_g, x, dx2, mix_w, rc, rsa, rsb)
    dwin_p = _in_proj_bwd_weights(hn_t, dproj_b)
    small_p = _pack_small(d_mix, d_attn_w, d_hgrn_w, d_lb, d_final, loss)
    return grad_x, dwin_p, dwout_p, small_p


def kernel(x, positions, w_in, w_out, mix_norm_w, attn_out_norm_w, hgrn_out_norm_w, hgrn_lb_raw, final_norm_w, loss_target, m_w_in, m_w_out, m_mix_norm_w, m_attn_out_norm_w, m_hgrn_out_norm_w, m_hgrn_lb_raw, m_final_norm_w, v_w_in, v_w_out, v_mix_norm_w, v_attn_out_norm_w, v_hgrn_out_norm_w, v_hgrn_lb_raw, v_final_norm_w):
    w_in_g, w_out_g = _gather_weights(w_in[0], w_out[0])
    grad_x, dwin_p, dwout_p, small_p = _local_step(
        x[0], positions.reshape(SEQ, 1), w_in_g, w_out_g, mix_norm_w, attn_out_norm_w, hgrn_out_norm_w,
        hgrn_lb_raw, final_norm_w, loss_target[0])

    w_s = _pack_small(mix_norm_w, attn_out_norm_w, hgrn_out_norm_w, hgrn_lb_raw, final_norm_w)
    m_s = _pack_small(m_mix_norm_w, m_attn_out_norm_w, m_hgrn_out_norm_w, m_hgrn_lb_raw, m_final_norm_w)
    v_s = _pack_small(v_mix_norm_w, v_attn_out_norm_w, v_hgrn_out_norm_w, v_hgrn_lb_raw, v_final_norm_w)
    (g_in, d_in, nm_in, nv_in, g_out, d_out, nm_out, nv_out, g_s, d_s, nm_s, nv_s) = _exchange_update(
        dwin_p, dwout_p, small_p, w_in[0], m_w_in[0], v_w_in[0], w_out[0], m_w_out[0], v_w_out[0], w_s, m_s, v_s)

    loss = g_s[ROW_LOSS, 0]
    return (loss, grad_x[None], g_in[None], g_out[None], *_unpack_small(g_s),
            d_in[None], d_out[None], *_unpack_small(d_s),
            nm_in[None], nm_out[None], *_unpack_small(nm_s),
            nv_in[None], nv_out[None], *_unpack_small(nv_s))
```

```python
import functools

import jax
import jax.numpy as jnp
from jax import lax
from jax.experimental import pallas as pl
from jax.experimental.pallas import tpu as pltpu

F32 = jnp.float32
BF16 = jnp.bfloat16

SEQ = 4096
D_MODEL = 1024
ATTN_WIDTH = 512
HGRN_WIDTH = 512
HEAD_DIM = 64
HGRN_HEADS = 4
HGRN_DIM = 128
HGRN_CHUNK = 64
N_CHUNKS = SEQ // HGRN_CHUNK
IN_COLS = 4096
COL_BLOCK = 512
N_DEV = 8
WOUT_ROWS = D_MODEL // N_DEV
ATTN_BLOCK = 128
DILATIONS = (1, 4, 16)
ROPE_THETA = 500000.0
ROPE_DIMS = 16
ROPE_HALF = 8
NORM_EPS = 1e-6
NEG_BIG = -1e30
LANES = 128

ADAM_LR = 0.001
ADAM_B1 = 0.9
ADAM_B2 = 0.999
ADAM_EPS = 1e-08
ADAM_WD = 0.01
ADAM_STEP = 10

SMALL_ROWS = 48
ROW_MIX, ROW_ATTN, ROW_HGRN, ROW_LB, ROW_FINAL, ROW_LOSS = 0, 8, 16, 24, 32, 40

VMEM_LIMIT = 56 * 1024 * 1024
MESH = pl.DeviceIdType.MESH


def _mm(a, b):
    return lax.dot_general(a, b, (((1,), (0,)), ((), ())), preferred_element_type=F32)


def _mm_nt(a, b):
    return lax.dot_general(a, b, (((1,), (1,)), ((), ())), preferred_element_type=F32)


def _mm_tn(a, b):
    return lax.dot_general(a, b, (((0,), (0,)), ((), ())), preferred_element_type=F32)


def _mm_exact(a, b):
    return lax.dot_general(a, b, (((1,), (0,)), ((), ())), preferred_element_type=F32,
                           precision=lax.Precision.HIGHEST)


def _sigmoid(v):
    return 1.0 / (1.0 + jnp.exp(-v))


def _params(sem=None, **kw):
    return pltpu.CompilerParams(dimension_semantics=sem, vmem_limit_bytes=VMEM_LIMIT, **kw)


def _my_place():
    return lax.axis_index("x"), lax.axis_index("y"), lax.axis_index("c")


def _peer(place, rel):
    x, y, c = place
    return (x ^ ((rel >> 2) & 1), y ^ ((rel >> 1) & 1), c ^ (rel & 1))


def _flat(place):
    x, y, c = place
    return 4 * x + 2 * y + c


def _rope_tables(pos_col, inv_freq_lanes):
    tm = 512

    def body(pos_ref, invf_ref, c_ref, sa_ref, sb_ref):
        ang = pos_ref[...].astype(F32) * invf_ref[...]
        e = lax.broadcasted_iota(jnp.int32, (tm, LANES), 1) & (HEAD_DIM - 1)
        cos, sin = jnp.cos(ang), jnp.sin(ang)
        c_ref[...] = jnp.where(e < ROPE_DIMS, cos, 1.0)
        sa_ref[...] = jnp.where((e >= ROPE_HALF) & (e < ROPE_DIMS), sin, 0.0)
        sb_ref[...] = jnp.where(e < ROPE_HALF, -sin, 0.0)

    tab = jax.ShapeDtypeStruct((SEQ, LANES), F32)
    spec = pl.BlockSpec((tm, LANES), lambda i: (i, 0))
    return pl.pallas_call(
        body, name="rope_tables", grid=(SEQ // tm,), out_shape=(tab, tab, tab),
        in_specs=[pl.BlockSpec((tm, 1), lambda i: (i, 0)), pl.BlockSpec((1, LANES), lambda i: (0, 0))],
        out_specs=(spec, spec, spec), compiler_params=_params(("parallel",)),
    )(pos_col, inv_freq_lanes)


def _rot(t, c, sa, sb):
    n = t.shape[1]
    return t * c + pltpu.roll(t, ROPE_HALF, 1) * sa + pltpu.roll(t, n - ROPE_HALF, 1) * sb


def _rot_transposed(g, c, sa, sb):
    n = g.shape[1]
    return g * c + pltpu.roll(g * sa, n - ROPE_HALF, 1) + pltpu.roll(g * sb, ROPE_HALF, 1)


def _gather_project(x, mix_w, w_in, w_out, rc, rsa, rsb):
    tm = 1024
    n_tiles = SEQ // tm
    arrival_of_step = (None, 0, 1, 2, 4, 5, 3, 6)

    def body(order_ref, x_ref, w_ref, win_ref, wout_ref, c_ref, sa_ref, sb_ref,
             proj_ref, hnt_ref, gin_hbm, gout_hbm,
             hn_s, w_land, wout_land, stage, send_sems, recv_sems, local_sems):
        g, i = pl.program_id(0), pl.program_id(1)
        me = _my_place()
        x_, y_, c_ = me
        sibling = (x_, y_, 1 - c_)
        chips = [(1 - x_, y_), (x_, 1 - y_), (1 - x_, 1 - y_)]

        def slab(which, place):
            idx = _flat(place)
            if which == 0:
                return w_land.at[idx]
            return wout_land.at[pl.ds(pl.multiple_of(idx * WOUT_ROWS, WOUT_ROWS), WOUT_ROWS), :]

        def copy(which, k, block, to, src=None):
            ref = slab(which, block)
            return pltpu.make_async_remote_copy(
                src_ref=ref if src is None else src, dst_ref=ref, send_sem=send_sems.at[7 * which + k],
                recv_sem=recv_sems.at[7 * which + k], device_id=to, device_id_type=MESH)

        def first_copies(which):
            src = stage if which == 0 else None
            return ([copy(which, 0, me, sibling, src)]
                    + [copy(which, 1 + j, me, (*chip, c_), src) for j, chip in enumerate(chips)])

        def pass_on(which, j):
            return copy(which, 4 + j, (*chips[j], c_), sibling)

        def arrival(which, k):
            if k == 0:
                return copy(which, 0, sibling, me)
            if k <= 3:
                return copy(which, k, (*chips[k - 1], c_), me)
            return copy(which, k, (*chips[k - 4], 1 - c_), me)

        @pl.when((g == 0) & (i == 0))
        def _():
            stage[...] = win_ref[...].astype(BF16)
            w_land[_flat(me)] = stage[...]
            wout_land[pl.ds(pl.multiple_of(_flat(me) * WOUT_ROWS, WOUT_ROWS), WOUT_ROWS), :] = (
                wout_ref[...].astype(BF16))
            for which in (0, 1):
                for cp in first_copies(which):
                    cp.start()

        for step, k in enumerate(arrival_of_step):
            if k is None:
                continue

            @pl.when((g == step) & (i == 0))
            def _(k=k):
                arrival(0, k).wait_recv()
                if 1 <= k <= 3:
                    arrival(1, k).wait_recv()
                    pass_on(0, k - 1).start()
                    pass_on(1, k - 1).start()

        rows = pl.ds(pl.multiple_of(i * tm, tm), tm)

        @pl.when(g == 0)
        def _():
            xf = x_ref[...]
            ms = jnp.mean(xf * xf, axis=-1, keepdims=True)
            hn = xf * lax.rsqrt(ms + NORM_EPS) * w_ref[...]
            hnt_ref[...] = hn.T.astype(BF16)
            hn_s[rows, :] = hn.astype(BF16)

        group = order_ref[g]
        acc = _mm(hn_s[rows, :], w_land[group])

        @pl.when(group < 2)
        def _():
            proj_ref[...] = _rot(acc, jnp.tile(c_ref[...], (1, 4)), jnp.tile(sa_ref[...], (1, 4)),
                                 jnp.tile(sb_ref[...], (1, 4)))

        @pl.when(group >= 2)
        def _():
            proj_ref[...] = acc

        @pl.when((g == N_DEV - 1) & (i == n_tiles - 1))
        def _():
            for k in (0, 4, 5, 6):
                arrival(1, k).wait_recv()
            for which in (0, 1):
                for cp in first_copies(which) + [pass_on(which, j) for j in range(3)]:
                    cp.wait_send()
            outs = [pltpu.make_async_copy(w_land.at[j], gin_hbm.at[:, COL_BLOCK * j:COL_BLOCK * (j + 1)],
                                          local_sems.at[j]) for j in range(N_DEV)]
            outs.append(pltpu.make_async_copy(wout_land, gout_hbm, local_sems.at[N_DEV]))
            for cp in outs:
                cp.start()
            for cp in outs:
                cp.wait()

    me = _my_place()
    x_, y_, c_ = me
    chips = [(1 - x_, y_), (x_, 1 - y_), (1 - x_, 1 - y_)]
    order = jnp.stack([_flat(p) for p in (
        me, (x_, y_, 1 - c_), (*chips[0], c_), (*chips[1], c_), (*chips[0], 1 - c_), (*chips[1], 1 - c_),
        (*chips[2], c_), (*chips[2], 1 - c_))]).astype(jnp.int32)

    first_sweep = lambda g, i, order: (jnp.where(g == 0, i, n_tiles - 1), 0)
    tab = pl.BlockSpec((tm, LANES), lambda g, i, order: (jnp.where(order[g] < 2, i, 0), 0))
    whole = lambda: pl.BlockSpec(memory_space=pltpu.VMEM)
    grid_spec = pltpu.PrefetchScalarGridSpec(
        num_scalar_prefetch=1, grid=(N_DEV, n_tiles),
        in_specs=[pl.BlockSpec((tm, D_MODEL), first_sweep),
                  pl.BlockSpec((1, D_MODEL), lambda g, i, order: (0, 0)),
                  whole(), whole(), tab, tab, tab],
        out_specs=(pl.BlockSpec((tm, COL_BLOCK), lambda g, i, order: (i, order[g])),
                   pl.BlockSpec((D_MODEL, tm), lambda g, i, order: (0, jnp.where(g == 0, i, n_tiles - 1))),
                   pl.BlockSpec(memory_space=pl.ANY), pl.BlockSpec(memory_space=pl.ANY)),
        scratch_shapes=[pltpu.VMEM((SEQ, D_MODEL), BF16),
                        pltpu.VMEM((N_DEV, D_MODEL, COL_BLOCK), BF16),
                        pltpu.VMEM((D_MODEL, D_MODEL), BF16),
                        pltpu.VMEM((D_MODEL, COL_BLOCK), BF16),
                        pltpu.SemaphoreType.DMA((14,)), pltpu.SemaphoreType.DMA((14,)),
                        pltpu.SemaphoreType.DMA((N_DEV + 1,))])
    return pl.pallas_call(
        body, name="gather_project", grid_spec=grid_spec,
        out_shape=(jax.ShapeDtypeStruct((SEQ, IN_COLS), F32), jax.ShapeDtypeStruct((D_MODEL, SEQ), BF16),
                   jax.ShapeDtypeStruct((D_MODEL, IN_COLS), BF16), jax.ShapeDtypeStruct((D_MODEL, D_MODEL), BF16)),
        compiler_params=_params(("arbitrary", "arbitrary")),
    )(order, x, mix_w, w_in, w_out, rc, rsa, rsb)


ATTN_GROUP = 8
BLOCKS_PER_PATTERN = SEQ // ATTN_BLOCK


def _write_band_bias(bias_ref):
    qi = lax.broadcasted_iota(jnp.int32, (2 * ATTN_BLOCK, 2 * ATTN_BLOCK), 0) & (ATTN_BLOCK - 1)
    kj = lax.broadcasted_iota(jnp.int32, (2 * ATTN_BLOCK, 2 * ATTN_BLOCK), 1)
    bias_ref[0] = jnp.where((kj >= qi) & (kj <= qi + ATTN_BLOCK), 0.0, NEG_BIG)
    bias_ref[1] = jnp.where(kj <= qi, 0.0, NEG_BIG)


def _head0_lanes():
    return lax.broadcasted_iota(jnp.int32, (ATTN_BLOCK, LANES), 1) < HEAD_DIM


def _stack_heads(t, h0):
    return jnp.concatenate([jnp.where(h0, t, 0.0), jnp.where(h0, 0.0, t)], axis=0).astype(BF16)


def _strided(start, size, d):
    return pl.ds(start, size) if d == 1 else pl.ds(start, size, stride=d)


def _block_place(i, d):
    nblk = BLOCKS_PER_PATTERN // d
    r, n = i // nblk, i % nblk
    kn = jnp.maximum(n - 1, 0)
    row0, key0 = n * (d * ATTN_BLOCK) + r, kn * (d * ATTN_BLOCK) + r
    if d == 1:
        row0, key0 = pl.multiple_of(row0, ATTN_BLOCK), pl.multiple_of(key0, ATTN_BLOCK)
    return _strided(row0, ATTN_BLOCK, d), _strided(key0, 2 * ATTN_BLOCK, d), (n == 0).astype(jnp.int32)


def _for_each_group(d, load, compute, store):
    def group(g, carry):
        items = [load(*_block_place(g * ATTN_GROUP + u, d)) for u in range(ATTN_GROUP)]
        results = [compute(item) for item in items]
        for item, res in zip(items, results):
            store(item, res)
        return carry

    lax.fori_loop(0, BLOCKS_PER_PATTERN // ATTN_GROUP, group, 0)


def _attn_fwd_fused(proj):
    n_pat = len(DILATIONS)
    tile2 = (2 * ATTN_BLOCK, LANES)

    def body(q_ref, k_ref, v_ref, o_ref, lse_ref, m_acc, l_acc, bias_ref):
        _write_band_bias(bias_ref)
        h0 = _head0_lanes()
        for pi, d in enumerate(DILATIONS):
            first, last = pi == 0, pi == n_pat - 1

            def load(rows, keys, which, first=first):
                item = dict(rows=rows, keys=keys, which=which)
                if not first:
                    item.update(o=o_ref[rows, :], m=[m_acc.at[h][rows, :] for h in range(2)],
                                l=[l_acc.at[h][rows, :] for h in range(2)])
                return item

            def compute(item, first=first):
                kb = k_ref[item["keys"], :].astype(BF16)
                vb = v_ref[item["keys"], :].astype(BF16)
                s = _mm_nt(_stack_heads(q_ref[item["rows"], :], h0), kb) * 0.125 + bias_ref[item["which"]]
                mb = jnp.max(s, axis=-1, keepdims=True)
                if first:
                    p = jnp.exp(s - mb)
                    mn = jnp.broadcast_to(mb, tile2)
                else:
                    m_old = jnp.concatenate(item["m"], axis=0)
                    mn = jnp.maximum(m_old, mb)
                    alpha = jnp.exp(m_old - mn)
                    p = jnp.exp(s - jnp.concatenate([mn, mn], axis=1))
                ls = jnp.sum(p, axis=-1, keepdims=True)
                pv = _mm(p.astype(BF16), vb)
                if first:
                    return pv, mn, jnp.broadcast_to(ls, tile2)
                o_old = jnp.concatenate([item["o"], item["o"]], axis=0)
                return alpha * o_old + pv, mn, alpha * jnp.concatenate(item["l"], axis=0) + ls

            def store(item, res, last=last):
                rows = item["rows"]
                (o0, o1), (m0, m1), (l0, l1) = ((a[:ATTN_BLOCK], a[ATTN_BLOCK:]) for a in res)
                if last:
                    o_ref[rows, :] = jnp.where(h0, o0 / l0, o1 / l1)
                    lse_ref[rows, :] = jnp.where(h0, m0 + jnp.log(l0), m1 + jnp.log(l1))
                else:
                    o_ref[rows, :] = jnp.where(h0, o0, o1)
                    m_acc.at[0][rows, :], m_acc.at[1][rows, :] = m0, m1
                    l_acc.at[0][rows, :], l_acc.at[1][rows, :] = l0, l1

            _for_each_group(d, load, compute, store)

    slab = lambda g: pl.BlockSpec((SEQ, LANES), functools.partial(lambda hp, g: (0, 4 * g + hp), g=g))
    wide = jax.ShapeDtypeStruct((SEQ, ATTN_WIDTH), F32)
    return pl.pallas_call(
        body, name="attn_fwd", grid=(4,), out_shape=(wide, wide),
        in_specs=[slab(0), slab(1), slab(2)], out_specs=(slab(0), slab(0)),
        scratch_shapes=[pltpu.VMEM((2, SEQ, LANES), F32), pltpu.VMEM((2, SEQ, LANES), F32),
                        pltpu.VMEM((2, 2 * ATTN_BLOCK, 2 * ATTN_BLOCK), F32)],
        compiler_params=_params(("parallel",)),
    )(proj, proj, proj)


def _attn_bwd_fused(proj, d_out, lse, delta):
    def body(q_ref, k_ref, v_ref, do_ref, lse_ref, del_ref, dq_ref, dk_ref, dv_ref, bias_ref):
        _write_band_bias(bias_ref)
        dk_ref[...] = jnp.zeros_like(dk_ref)
        dv_ref[...] = jnp.zeros_like(dv_ref)
        h0 = _head0_lanes()
        for pi, d in enumerate(DILATIONS):
            first = pi == 0

            def load(rows, keys, which):
                return dict(rows=rows, keys=keys, q=q_ref[rows, :], g=do_ref[rows, :], lse=lse_ref[rows, :],
                            delta=del_ref[rows, :], k=k_ref[keys, :].astype(BF16),
                            v=v_ref[keys, :].astype(BF16), bias=bias_ref[which])

            def per_head(t):
                swapped = pltpu.roll(t, HEAD_DIM, 1)
                both = jnp.concatenate([jnp.where(h0, t, swapped), jnp.where(h0, swapped, t)], axis=0)
                return jnp.concatenate([both, both], axis=1)

            def compute(item):
                q2, g2 = _stack_heads(item["q"], h0), _stack_heads(item["g"], h0)
                s = _mm_nt(q2, item["k"]) * 0.125 + item["bias"]
                p = jnp.exp(s - per_head(item["lse"]))
                dp = _mm_nt(g2, item["v"])
                ds = (p * (dp - per_head(item["delta"])) * 0.125).astype(BF16)
                dq2 = _mm(ds, item["k"])
                dq = jnp.where(h0, dq2[:ATTN_BLOCK], dq2[ATTN_BLOCK:])
                return dq, _mm_tn(ds, q2), _mm_tn(p.astype(BF16), g2)

            def store(item, res, first=first):
                rows, keys = item["rows"], item["keys"]
                if first:
                    dq_ref[rows, :] = res[0]
                else:
                    dq_ref[rows, :] += res[0]
                dk_ref[keys, :] += res[1]
                dv_ref[keys, :] += res[2]

            _for_each_group(d, load, compute, store)

    slab = lambda g: pl.BlockSpec((SEQ, LANES), functools.partial(lambda hp, g: (0, 4 * g + hp), g=g))
    wide = jax.ShapeDtypeStruct((SEQ, ATTN_WIDTH), F32)
    return pl.pallas_call(
        body, name="attn_bwd", grid=(4,), out_shape=(wide, wide, wide),
        scratch_shapes=[pltpu.VMEM((2, 2 * ATTN_BLOCK, 2 * ATTN_BLOCK), F32)],
        in_specs=[slab(0), slab(1), slab(2), slab(0), slab(0), slab(0)], out_specs=(slab(0), slab(0), slab(0)),
        compiler_params=_params(("parallel",)),
    )(proj, proj, proj, d_out, lse, delta)


def _hgrn_lower_bound(lb_ref):
    r0, r1 = lb_ref[0:1, :], lb_ref[1:2, :]
    mx = jnp.maximum(r0, r1)
    e0, e1 = jnp.exp(r0 - mx), jnp.exp(r1 - mx)
    return e0 / (e0 + e1)


def _hgrn_gates(hq, hf, lb):
    sq = _sigmoid(hq)
    sg = _sigmoid(hf)
    f = lb + (1.0 - lb) * sg
    return hq * sq, sq, sg, f, 1.0 - f, jnp.log(f)


HGRN_PAIR = 2
HGRN_SEQ_BLOCK = 1024
HGRN_GROUP = 4
HGRN_ROWS = HGRN_GROUP * HGRN_CHUNK


def _hgrn_specs(reverse):
    n_blocks = SEQ // HGRN_SEQ_BLOCK
    width = HGRN_PAIR * HGRN_DIM
    blk = (lambda s: n_blocks - 1 - s) if reverse else (lambda s: s)
    cols = lambda g: pl.BlockSpec((HGRN_SEQ_BLOCK, width),
                                  functools.partial(lambda p, s, g: (blk(s), HGRN_PAIR * g + p), g=g))
    pair = pl.BlockSpec((HGRN_SEQ_BLOCK, width), lambda p, s: (blk(s), p))
    lb = pl.BlockSpec((2, width), lambda p, s: (0, p))
    states = pl.BlockSpec((HGRN_PAIR, HGRN_SEQ_BLOCK // HGRN_CHUNK, HGRN_DIM, HGRN_DIM),
                          lambda p, s: (p, blk(s), 0, 0))
    return cols, pair, lb, states


def _chunk_masks():
    ri = lax.broadcasted_iota(jnp.int32, (HGRN_ROWS, HGRN_ROWS), 0)
    ci = lax.broadcasted_iota(jnp.int32, (HGRN_ROWS, HGRN_ROWS), 1)
    same = (ri // HGRN_CHUNK) == (ci // HGRN_CHUNK)
    return same, same & (ri >= ci), same & (ri <= ci)


def _mm_select(sel, v):
    hi = v.astype(BF16)
    r1 = v - hi.astype(F32)
    mid = r1.astype(BF16)
    lo = (r1 - mid.astype(F32)).astype(BF16)
    return _mm(sel, hi) + _mm(sel, mid) + _mm(sel, lo)


def _head_cols(a, h):
    return a[:, HGRN_DIM * h:HGRN_DIM * (h + 1)]


def _hgrn_fwd(proj, lb_raw):
    t, rws = HGRN_CHUNK, HGRN_ROWS

    def body(hq_ref, hf_ref, hi_ref, lb_ref, rec_ref, st_ref, state):
        @pl.when(pl.program_id(1) == 0)
        def _():
            state[...] = jnp.zeros_like(state)

        lb = _hgrn_lower_bound(lb_ref)
        same, causal, _ = _chunk_masks()
        sel = jnp.concatenate([causal, same], axis=0).astype(BF16)

        def group(g, sts):
            rows = pl.ds(pl.multiple_of(g * rws, rws), rws)
            q, _, _, _, k, lf = _hgrn_gates(hq_ref[rows, :], hf_ref[rows, :], lb)
            sums = _mm_select(sel, lf)
            cum, last = sums[:rws], sums[rws:]
            qd = (q * jnp.exp(cum)).astype(BF16)
            ki = (k * jnp.exp(-cum)).astype(BF16)
            ke = (k * jnp.exp(last - cum)).astype(BF16)
            vb = hi_ref[rows, :].astype(BF16)
            dec = jnp.exp(last)
            new_sts, recs = [], []
            for h in range(HGRN_PAIR):
                qd_h, ke_h, vb_h = _head_cols(qd, h), _head_cols(ke, h), _head_cols(vb, h)
                att = jnp.where(causal, _mm_nt(qd_h, _head_cols(ki, h)), 0.0).astype(BF16)
                intra = _mm(att, vb_h)
                st = sts[h]
                outs = []
                for c in range(HGRN_GROUP):
                    sl = slice(c * t, (c + 1) * t)
                    st_ref[h, g * HGRN_GROUP + c] = st
                    outs.append(intra[sl] + _mm_nt(qd_h[sl], st.astype(BF16)))
                    st = st * _head_cols(dec[c * t:c * t + 1, :], h) + _mm_tn(vb_h[sl], ke_h[sl])
                new_sts.append(st)
                recs.append(jnp.concatenate(outs, axis=0))
            rec_ref[rows, :] = jnp.concatenate(recs, axis=1)
            return tuple(new_sts)

        sts = lax.fori_loop(0, HGRN_SEQ_BLOCK // rws, group, tuple(state[h] for h in range(HGRN_PAIR)))
        for h in range(HGRN_PAIR):
            state[h] = sts[h]

    cols, pair, lb, states = _hgrn_specs(reverse=False)
    return pl.pallas_call(
        body, name="hgrn_fwd", grid=(HGRN_HEADS // HGRN_PAIR, SEQ // HGRN_SEQ_BLOCK),
        out_shape=(jax.ShapeDtypeStruct((SEQ, HGRN_WIDTH), F32),
                   jax.ShapeDtypeStruct((HGRN_HEADS, N_CHUNKS, HGRN_DIM, HGRN_DIM), F32)),
        in_specs=[cols(4), cols(5), cols(6), lb], out_specs=(pair, states),
        scratch_shapes=[pltpu.VMEM((HGRN_PAIR, HGRN_DIM, HGRN_DIM), F32)],
        compiler_params=_params(("parallel", "arbitrary")),
    )(proj, proj, proj, lb_raw)


def _hgrn_bwd(proj, lb_raw, d_rec, states):
    t, rws = HGRN_CHUNK, HGRN_ROWS

    def body(hq_ref, hf_ref, hi_ref, lb_ref, do_ref, st_ref, dhq_ref, dhf_ref, dhi_ref, dlb_ref,
             dstate, dlb_acc):
        lb = _hgrn_lower_bound(lb_ref)
        same, causal, anti = _chunk_masks()
        sel = jnp.concatenate([causal, same], axis=0).astype(BF16)
        sel_t = jnp.concatenate([anti, same], axis=1).astype(BF16)
        @pl.when(pl.program_id(1) == 0)
        def _():
            dstate[...] = jnp.zeros_like(dstate)
            dlb_acc[...] = jnp.zeros_like(dlb_acc)

        n_groups = HGRN_SEQ_BLOCK // rws
        chunks = [slice(c * t, (c + 1) * t) for c in range(HGRN_GROUP)]

        def group(i, dsts_in):
            g = n_groups - 1 - i
            rows = pl.ds(pl.multiple_of(g * rws, rws), rws)
            hq = hq_ref[rows, :]
            q, sq, sg, f, k, lf = _hgrn_gates(hq, hf_ref[rows, :], lb)
            sums = _mm_select(sel, lf)
            cum, last = sums[:rws], sums[rws:]
            e_cum, e_inv, e_end, dec = jnp.exp(cum), jnp.exp(-cum), jnp.exp(last - cum), jnp.exp(last)
            qd, ki, ke = q * e_cum, k * e_inv, k * e_end
            qdb, kib, keb = qd.astype(BF16), ki.astype(BF16), ke.astype(BF16)
            vb = hi_ref[rows, :].astype(BF16)
            gb = do_ref[rows, :].astype(BF16)

            dsts_out, per_head = [], []
            for h in range(HGRN_PAIR):
                qdb_h, kib_h, keb_h = _head_cols(qdb, h), _head_cols(kib, h), _head_cols(keb, h)
                vb_h, gb_h = _head_cols(vb, h), _head_cols(gb, h)
                att = jnp.where(causal, _mm_nt(qdb_h, kib_h), 0.0).astype(BF16)
                datt = jnp.where(causal, _mm_nt(gb_h, vb_h), 0.0).astype(BF16)
                dv = _mm_tn(att, gb_h)
                dqd = _mm(datt, kib_h)
                dki = _mm_tn(datt, qdb_h)

                decs = [_head_cols(dec[c * t:c * t + 1, :], h) for c in range(HGRN_GROUP)]
                dsts = [None] * HGRN_GROUP
                dst = dsts_in[h]
                for c in reversed(range(HGRN_GROUP)):
                    dsts[c] = dst
                    dst = dst * decs[c] + _mm_tn(gb_h[chunks[c]], qdb_h[chunks[c]])
                dsts_out.append(dst)

                dv_x, dqd_x, dke, dlast_x = [], [], [], []
                for c, sl in enumerate(chunks):
                    st_prev = st_ref[h, g * HGRN_GROUP + c]
                    dstb = dsts[c].astype(BF16)
                    dv_x.append(_mm_nt(keb_h[sl], dstb))
                    dqd_x.append(_mm(gb_h[sl], st_prev.astype(BF16)))
                    dke.append(_mm(vb_h[sl], dstb))
                    ddec = jnp.sum(dsts[c] * st_prev, axis=0, keepdims=True)
                    dlast_x.append(jnp.broadcast_to(ddec * decs[c], (t, HGRN_DIM)))
                per_head.append((dv + jnp.concatenate(dv_x, axis=0), dqd + jnp.concatenate(dqd_x, axis=0),
                                 dki, jnp.concatenate(dke, axis=0), jnp.concatenate(dlast_x, axis=0)))
            dv, dqd, dki, dke, dlast = (jnp.concatenate([a, b], axis=1) for a, b in zip(*per_head))

            dq = dqd * e_cum
            dk = dki * e_inv + dke * e_end
            dke_ke = dke * ke
            dcum = dqd * qd - dki * ki - dke_ke
            dlf = _mm_select(sel_t, jnp.concatenate([dcum, dke_ke], axis=0)) + dlast
            df = dlf / f - dk
            dhq_ref[rows, :] = dq * (sq * (1.0 + hq * (1.0 - sq)))
            dhf_ref[rows, :] = df * (1.0 - lb) * (sg * (1.0 - sg))
            dhi_ref[rows, :] = dv
            dlb_acc[...] += jnp.sum(df * (1.0 - sg), axis=0, keepdims=True)
            return tuple(dsts_out)

        dsts = lax.fori_loop(0, n_groups, group, tuple(dstate[h] for h in range(HGRN_PAIR)))
        for h in range(HGRN_PAIR):
            dstate[h] = dsts[h]
        g0 = dlb_acc[...] * lb * (1.0 - lb)
        dlb_ref[...] = jnp.concatenate([g0, -g0], axis=0)

    cols, pair, lb_spec, st_spec = _hgrn_specs(reverse=True)
    wide = jax.ShapeDtypeStruct((SEQ, HGRN_WIDTH), F32)
    return pl.pallas_call(
        body, name="hgrn_bwd", grid=(HGRN_HEADS // HGRN_PAIR, SEQ // HGRN_SEQ_BLOCK),
        out_shape=(wide, wide, wide, jax.ShapeDtypeStruct((2, HGRN_WIDTH), F32)),
        in_specs=[cols(4), cols(5), cols(6), lb_spec, pair, st_spec],
        out_specs=(pair, pair, pair, lb_spec),
        scratch_shapes=[pltpu.VMEM((HGRN_PAIR, HGRN_DIM, HGRN_DIM), F32),
                        pltpu.VMEM((1, HGRN_PAIR * HGRN_DIM), F32)],
        compiler_params=_params(("parallel", "arbitrary")),
    )(proj, proj, proj, lb_raw, d_rec, states)


def _group_sum(v, group):
    parts = []
    for s in range(v.shape[1] // LANES):
        slab = v[:, LANES * s:LANES * (s + 1)]
        if group == LANES:
            parts.append(jnp.broadcast_to(jnp.sum(slab, axis=-1, keepdims=True), slab.shape))
        else:
            h0 = lax.broadcasted_iota(jnp.int32, slab.shape, 1) < HEAD_DIM
            s0 = jnp.sum(jnp.where(h0, slab, 0.0), axis=-1, keepdims=True)
            s1 = jnp.sum(jnp.where(h0, 0.0, slab), axis=-1, keepdims=True)
            parts.append(jnp.where(h0, s0, s1))
    return jnp.concatenate(parts, axis=1)


def _mid(attn_o, rec, proj, x, target, w_out_g, attn_w, hgrn_w, final_w):
    tm = 256

    def branch_fwd(o, gate, w, group):
        r = lax.rsqrt(_group_sum(o * o, group) * (1.0 / group) + NORM_EPS)
        nrm = o * r
        sg = _sigmoid(gate)
        return r, nrm, sg, nrm * w * (gate * sg)

    def branch_bwd(dy, r, nrm, sg, gate, w, group):
        silu = gate * sg
        d_gate = dy * nrm * w * (sg * (1.0 + gate * (1.0 - sg)))
        d_w = jnp.sum(dy * nrm * silu, axis=0, keepdims=True)
        dn = dy * w * silu
        d_o = r * (dn - nrm * (_group_sum(dn * nrm, group) * (1.0 / group)))
        return d_o, d_gate, d_w

    def body(o_ref, rec_ref, ag_ref, hg_ref, x_ref, tgt_ref, wout_ref, aw_ref, hw_ref, fw_ref,
             dx2_ref, do_ref, delta_ref, dag_ref, drec_ref, dhg_ref, dwout_ref, dfw_ref, daw_ref, dhw_ref,
             loss_ref, dwout_acc):
        i = pl.program_id(0)

        @pl.when(i == 0)
        def _():
            dwout_acc[...] = jnp.zeros_like(dwout_acc)
            dfw_ref[...] = jnp.zeros_like(dfw_ref)
            daw_ref[...] = jnp.zeros_like(daw_ref)
            dhw_ref[...] = jnp.zeros_like(dhw_ref)
            loss_ref[...] = jnp.zeros_like(loss_ref)

        o, rc, ag, hg = o_ref[...], rec_ref[...], ag_ref[...], hg_ref[...]
        aw, hw, fw = aw_ref[...], hw_ref[...], fw_ref[...]
        ra, na, sga, ya = branch_fwd(o, ag, aw, HEAD_DIM)
        rh, nh, sgh, yh = branch_fwd(rc, hg, hw, HGRN_DIM)
        mixed = jnp.concatenate([ya, yh], axis=1).astype(BF16)
        wout = wout_ref[...]
        x2 = x_ref[...] + _mm(mixed, wout)
        rstd = lax.rsqrt(jnp.mean(x2 * x2, axis=-1, keepdims=True) + NORM_EPS)
        xn = x2 * rstd
        err = xn * fw - tgt_ref[...]
        row_loss = jnp.mean(err * err, axis=-1, keepdims=True)
        loss_ref[...] += 0.5 * jnp.sum(row_loss, axis=0, keepdims=True)
        dy = err * (1.0 / D_MODEL)
        dfw_ref[...] += jnp.sum(dy * xn, axis=0, keepdims=True)
        dxn = dy * fw
        dx2 = rstd * (dxn - xn * jnp.mean(dxn * xn, axis=-1, keepdims=True))
        dx2_ref[...] = dx2
        dx2b = dx2.astype(BF16)
        dwout_acc[...] += _mm_tn(mixed, dx2b)

        @pl.when(i == pl.num_programs(0) - 1)
        def _():
            dwout_ref[...] = dwout_acc[...].astype(BF16)

        dmixed = _mm_nt(dx2b, wout)

        d_o, d_ag, d_aw = branch_bwd(dmixed[:, :ATTN_WIDTH], ra, na, sga, ag, aw, HEAD_DIM)
        d_rec, d_hg, d_hw = branch_bwd(dmixed[:, ATTN_WIDTH:], rh, nh, sgh, hg, hw, HGRN_DIM)
        do_ref[...] = d_o
        delta_ref[...] = _group_sum(d_o * o, HEAD_DIM)
        dag_ref[...] = d_ag
        drec_ref[...] = d_rec
        dhg_ref[...] = d_hg
        daw_ref[...] += d_aw
        dhw_ref[...] += d_hw

    half = lambda: pl.BlockSpec((tm, COL_BLOCK), lambda i: (i, 0))
    full = lambda: pl.BlockSpec((tm, D_MODEL), lambda i: (i, 0))
    fixed = lambda r, c: pl.BlockSpec((r, c), lambda i: (0, 0))
    wide = jax.ShapeDtypeStruct((SEQ, COL_BLOCK), F32)
    return pl.pallas_call(
        body, name="mid", grid=(SEQ // tm,),
        out_shape=(jax.ShapeDtypeStruct((SEQ, D_MODEL), F32), wide, wide, wide, wide, wide,
                   jax.ShapeDtypeStruct((D_MODEL, D_MODEL), BF16),
                   jax.ShapeDtypeStruct((1, D_MODEL), F32), jax.ShapeDtypeStruct((1, COL_BLOCK), F32),
                   jax.ShapeDtypeStruct((1, COL_BLOCK), F32), jax.ShapeDtypeStruct((1, 1), F32)),
        scratch_shapes=[pltpu.VMEM((D_MODEL, D_MODEL), F32)],
        in_specs=[half(), half(),
                  pl.BlockSpec((tm, COL_BLOCK), lambda i: (i, 3)), pl.BlockSpec((tm, COL_BLOCK), lambda i: (i, 7)),
                  full(), full(), fixed(D_MODEL, D_MODEL), fixed(1, COL_BLOCK), fixed(1, COL_BLOCK),
                  fixed(1, D_MODEL)],
        out_specs=(full(), half(), half(), half(), half(), half(), fixed(D_MODEL, D_MODEL),
                   fixed(1, D_MODEL), fixed(1, COL_BLOCK), fixed(1, COL_BLOCK), fixed(1, 1)),
        compiler_params=_params(("arbitrary",)),
    )(attn_o, rec, proj, proj, x, target, w_out_g, attn_w, hgrn_w, final_w)


def _in_proj_bwd_rows(d_groups, w_g, x, dx2, mix_w, rc, rsa, rsb):
    tm = 256

    def body(*refs):
        dg_refs = refs[:N_DEV]
        wg_ref, x_ref, dx2_ref, w_ref, c_ref, sa_ref, sb_ref, gx_ref, dpb_ref, dmw_ref = refs[N_DEV:]

        @pl.when(pl.program_id(0) == 0)
        def _():
            dmw_ref[...] = jnp.zeros_like(dmw_ref)

        parts = []
        for j in range(N_DEV):
            dp = dg_refs[j][...]
            if j < 2:
                dp = _rot_transposed(dp, jnp.tile(c_ref[...], (1, 4)), jnp.tile(sa_ref[...], (1, 4)),
                                     jnp.tile(sb_ref[...], (1, 4)))
            parts.append(dp.astype(BF16))
        dpb = jnp.concatenate(parts, axis=1)
        dpb_ref[...] = dpb
        g = _mm_nt(dpb, wg_ref[...])
        xf = x_ref[...]
        rstd = lax.rsqrt(jnp.mean(xf * xf, axis=-1, keepdims=True) + NORM_EPS)
        xn = xf * rstd
        dmw_ref[...] += jnp.sum(g * xn, axis=0, keepdims=True)
        gw = g * w_ref[...]
        gx_ref[...] = dx2_ref[...] + rstd * (gw - xn * jnp.mean(gw * xn, axis=-1, keepdims=True))

    tile = lambda cols: pl.BlockSpec((tm, cols), lambda i: (i, 0))
    fixed = lambda r, c: pl.BlockSpec((r, c), lambda i: (0, 0))
    return pl.pallas_call(
        body, name="in_proj_bwd_rows", grid=(SEQ // tm,),
        out_shape=(jax.ShapeDtypeStruct((SEQ, D_MODEL), F32), jax.ShapeDtypeStruct((SEQ, IN_COLS), BF16),
                   jax.ShapeDtypeStruct((1, D_MODEL), F32)),
        in_specs=[tile(COL_BLOCK) for _ in range(N_DEV)] + [
            fixed(D_MODEL, IN_COLS), tile(D_MODEL), tile(D_MODEL), fixed(1, D_MODEL),
            tile(LANES), tile(LANES), tile(LANES)],
        out_specs=(tile(D_MODEL), tile(IN_COLS), fixed(1, D_MODEL)),
        compiler_params=_params(("arbitrary",)),
    )(*d_groups, w_g, x, dx2, mix_w, rc, rsa, rsb)


def _in_proj_bwd_weights(hn_t, dproj_b):
    def body(hnt_ref, dp_ref, dwin_ref):
        dwin_ref[0] = _mm(hnt_ref[...], dp_ref[...]).astype(BF16)

    return pl.pallas_call(
        body, name="in_proj_bwd_weights", grid=(N_DEV,),
        out_shape=jax.ShapeDtypeStruct((N_DEV, D_MODEL, COL_BLOCK), BF16),
        in_specs=[pl.BlockSpec((D_MODEL, SEQ), lambda j: (0, 0)), pl.BlockSpec((SEQ, COL_BLOCK), lambda j: (0, j))],
        out_specs=pl.BlockSpec((1, D_MODEL, COL_BLOCK), lambda j: (j, 0, 0)),
        compiler_params=_params(("parallel",)),
    )(hn_t, dproj_b)


def _adamw(w, g, m, v):
    m = ADAM_B1 * m + (1.0 - ADAM_B1) * g
    v = ADAM_B2 * v + (1.0 - ADAM_B2) * (g * g)
    m_hat = m / (1.0 - ADAM_B1 ** ADAM_STEP)
    v_hat = v / (1.0 - ADAM_B2 ** ADAM_STEP)
    delta = -ADAM_LR * (m_hat / (jnp.sqrt(v_hat) + ADAM_EPS) + ADAM_WD * w)
    return delta, m, v


def _exchange_update(dwin_p, dwout_p, small_p, w_in, m_in, v_in, w_out, m_out, v_out, w_s, m_s, v_s):
    rb = 128
    n_chips = N_DEV // 2
    S1_IN, S1_OUT, SMALL, S2_IN, S2_OUT = 0, 4, 8, 15, 18

    def body(dwin_hbm, dwout_hbm, small_ref, win_ref, min_ref, vin_ref, wout_ref, mout_ref, vout_ref,
             ws_ref, ms_ref, vs_ref,
             gin_ref, din_ref, nmin_ref, nvin_ref, gout_ref, dout_ref, nmout_ref, nvout_ref,
             gs_ref, ds_ref, nms_ref, nvs_ref,
             own_in, own_out, s1_in, s1_out, fwd_in, fwd_out, s2_in, s2_out, land_s,
             send_sems, recv_sems, local_sems):
        me = _my_place()
        x, y, c = me
        my_chip = 2 * x + y
        sibling = (x, y, 1 - c)

        def remote(slot, src, dst, to):
            return pltpu.make_async_remote_copy(src_ref=src, dst_ref=dst, send_sem=send_sems.at[slot],
                                                recv_sem=recv_sems.at[slot], device_id=to, device_id_type=MESH)

        def stage1(q):
            return [remote(S1_IN + q, dwin_hbm.at[q, 1 - c], s1_in.at[q], sibling),
                    remote(S1_OUT + q, dwout_hbm.at[q, 1 - c], s1_out.at[q], sibling)]

        def stage2(rel):
            peer = _peer(me, 2 * rel)
            return [remote(S2_IN + rel - 1, fwd_in.at[rel - 1], s2_in.at[rel - 1], peer),
                    remote(S2_OUT + rel - 1, fwd_out.at[rel - 1], s2_out.at[rel - 1], peer)]

        def small_copy(rel):
            return remote(SMALL + rel - 1, small_ref, land_s.at[rel], _peer(me, rel))

        mine = [pltpu.make_async_copy(dwin_hbm.at[:, c], own_in, local_sems.at[0]),
                pltpu.make_async_copy(dwout_hbm.at[:, c], own_out, local_sems.at[1])]
        for cp in mine:
            cp.start()
        sent = []
        for q in range(n_chips):
            sent += stage1(q)
        land_s[0] = small_ref[...]
        sent += [small_copy(rel) for rel in range(1, N_DEV)]
        for cp in sent:
            cp.start()
        for cp in mine:
            cp.wait()

        def add_blocks(q, own, got, n_rows, dst):
            def step(b, carry):
                rows = pl.ds(pl.multiple_of(b * rb, rb), rb)
                dst[rows, :] = (own[q, rows, :].astype(F32) + got[q, rows, :].astype(F32)).astype(dst.dtype)
                return carry
            lax.fori_loop(0, n_rows // rb, step, 0)

        for rel in range(1, n_chips):
            q = my_chip ^ rel
            for cp in stage1(q):
                cp.wait_recv()
            add_blocks(q, own_in, s1_in, D_MODEL, fwd_in.at[rel - 1])
            add_blocks(q, own_out, s1_out, WOUT_ROWS, fwd_out.at[rel - 1])
            for cp in stage2(rel):
                cp.start()
                sent.append(cp)
        for cp in stage1(my_chip):
            cp.wait_recv()
        add_blocks(my_chip, own_in, s1_in, D_MODEL, gin_ref)
        add_blocks(my_chip, own_out, s1_out, WOUT_ROWS, gout_ref)
        for rel in range(1, n_chips):
            for cp in stage2(rel):
                cp.wait_recv()

        def update(got, w_ref, m_ref, v_ref, g_ref, d_ref, nm_ref, nv_ref, n_rows):
            def step(b, carry):
                rows = pl.ds(pl.multiple_of(b * rb, rb), rb)
                g = g_ref[rows, :]
                for rel in range(1, n_chips):
                    g = g + got[rel - 1, rows, :].astype(F32)
                delta, nm, nv = _adamw(w_ref[rows, :], g, m_ref[rows, :], v_ref[rows, :])
                g_ref[rows, :] = g
                d_ref[rows, :] = delta
                nm_ref[rows, :] = nm
                nv_ref[rows, :] = nv
                return carry
            lax.fori_loop(0, n_rows // rb, step, 0)

        update(s2_in, win_ref, min_ref, vin_ref, gin_ref, din_ref, nmin_ref, nvin_ref, D_MODEL)
        update(s2_out, wout_ref, mout_ref, vout_ref, gout_ref, dout_ref, nmout_ref, nvout_ref, WOUT_ROWS)

        for rel in range(1, N_DEV):
            small_copy(rel).wait_recv()
        my_flat = _flat(me)
        g = land_s[my_flat ^ 0]
        for dev in range(1, N_DEV):
            g = g + land_s[my_flat ^ dev]
        delta, nm, nv = _adamw(ws_ref[...], g, ms_ref[...], vs_ref[...])
        gs_ref[...] = g
        ds_ref[...] = delta
        nms_ref[...] = nm
        nvs_ref[...] = nv
        for cp in sent:
            cp.wait_send()

    vm = lambda: pl.BlockSpec(memory_space=pltpu.VMEM)
    anyspace = lambda: pl.BlockSpec(memory_space=pl.ANY)
    big = jax.ShapeDtypeStruct((D_MODEL, COL_BLOCK), F32)
    flat = jax.ShapeDtypeStruct((WOUT_ROWS, D_MODEL), F32)
    small = jax.ShapeDtypeStruct((SMALL_ROWS, LANES), F32)
    in_blocks = lambda n: pltpu.VMEM((n, D_MODEL, COL_BLOCK), BF16)
    out_blocks = lambda n: pltpu.VMEM((n, WOUT_ROWS, D_MODEL), BF16)
    return pl.pallas_call(
        body, name="exchange_update",
        out_shape=tuple([big] * 4 + [flat] * 4 + [small] * 4),
        in_specs=[anyspace(), anyspace()] + [vm() for _ in range(10)],
        out_specs=tuple(vm() for _ in range(12)),
        scratch_shapes=[in_blocks(n_chips), out_blocks(n_chips), in_blocks(n_chips), out_blocks(n_chips),
                        in_blocks(n_chips - 1), out_blocks(n_chips - 1),
                        in_blocks(n_chips - 1), out_blocks(n_chips - 1),
                        pltpu.VMEM((N_DEV, SMALL_ROWS, LANES), F32),
                        pltpu.SemaphoreType.DMA((21,)), pltpu.SemaphoreType.DMA((21,)),
                        pltpu.SemaphoreType.DMA((2,))],
        compiler_params=_params(),
    )(dwin_p.reshape(n_chips, 2, D_MODEL, COL_BLOCK), dwout_p.reshape(n_chips, 2, WOUT_ROWS, D_MODEL),
      small_p, w_in, m_in, v_in, w_out, m_out, v_out, w_s, m_s, v_s)


def _pack_small(mix, attn, hgrn, lb, final, loss=None):
    def rows8(a):
        a = a.reshape(-1, LANES)
        return jnp.pad(a, ((0, 8 - a.shape[0]), (0, 0)))
    last = jnp.zeros((8, LANES), F32) if loss is None else jnp.pad(loss.reshape(1, 1), ((0, 7), (0, LANES - 1)))
    return jnp.concatenate([rows8(mix), rows8(attn), rows8(hgrn), rows8(lb), rows8(final), last], axis=0)


def _unpack_small(slab):
    return (slab[ROW_MIX:ROW_MIX + 8].reshape(1, D_MODEL), slab[ROW_ATTN:ROW_ATTN + 4].reshape(1, ATTN_WIDTH),
            slab[ROW_HGRN:ROW_HGRN + 4].reshape(1, HGRN_WIDTH), slab[ROW_LB:ROW_LB + 8].reshape(2, HGRN_WIDTH),
            slab[ROW_FINAL:ROW_FINAL + 8].reshape(D_MODEL))


def _rope(pos_col):
    inv = ROPE_THETA ** (-jnp.arange(ROPE_HALF, dtype=F32) * (2.0 / ROPE_DIMS))
    lane_e = jnp.arange(LANES) % HEAD_DIM
    inv_lanes = jnp.where(lane_e < ROPE_DIMS, inv[lane_e % ROPE_HALF], 0.0).reshape(1, LANES)
    return _rope_tables(pos_col, inv_lanes)


def _local_step(x, proj, hn_t, w_in_g, w_out_g, tables, mix_w, attn_w, hgrn_w, lb_raw, final_w, target):
    rc, rsa, rsb = tables
    attn_o, lse = _attn_fwd_fused(proj)
    rec, states = _hgrn_fwd(proj, lb_raw)

    (dx2, d_o, delta, d_ag, d_rec, d_hg, dwout_p, d_final, d_attn_w, d_hgrn_w, loss) = _mid(
        attn_o, rec, proj, x, target, w_out_g, attn_w, hgrn_w, final_w.reshape(1, D_MODEL))

    dqkv = _attn_bwd_fused(proj, d_o, lse, delta)
    d_hq, d_hf, d_hi, d_lb = _hgrn_bwd(proj, lb_raw, d_rec, states)

    grad_x, dproj_b, d_mix = _in_proj_bwd_rows(
        (dqkv[0], dqkv[1], dqkv[2], d_ag, d_hq, d_hf, d_hi, d_hg), w_in_g, x, dx2, mix_w, rc, rsa, rsb)
    dwin_p = _in_proj_bwd_weights(hn_t, dproj_b)
    small_p = _pack_small(d_mix, d_attn_w, d_hgrn_w, d_lb, d_final, loss)
    return grad_x, dwin_p, dwout_p, small_p


def kernel(x, positions, w_in, w_out, mix_norm_w, attn_out_norm_w, hgrn_out_norm_w, hgrn_lb_raw, final_norm_w, loss_target, m_w_in, m_w_out, m_mix_norm_w, m_attn_out_norm_w, m_hgrn_out_norm_w, m_hgrn_lb_raw, m_final_norm_w, v_w_in, v_w_out, v_mix_norm_w, v_attn_out_norm_w, v_hgrn_out_norm_w, v_hgrn_lb_raw, v_final_norm_w):
    tables = _rope(positions.reshape(SEQ, 1))
    proj, hn_t, w_in_g, w_out_g = _gather_project(x[0], mix_norm_w, w_in[0], w_out[0], *tables)
    grad_x, dwin_p, dwout_p, small_p = _local_step(
        x[0], proj, hn_t, w_in_g, w_out_g, tables, mix_norm_w, attn_out_norm_w, hgrn_out_norm_w,
        hgrn_lb_raw, final_norm_w, loss_target[0])

    w_s = _pack_small(mix_norm_w, attn_out_norm_w, hgrn_out_norm_w, hgrn_lb_raw, final_norm_w)
    m_s = _pack_small(m_mix_norm_w, m_attn_out_norm_w, m_hgrn_out_norm_w, m_hgrn_lb_raw, m_final_norm_w)
    v_s = _pack_small(v_mix_norm_w, v_attn_out_norm_w, v_hgrn_out_norm_w, v_hgrn_lb_raw, v_final_norm_w)
    (g_in, d_in, nm_in, nv_in, g_out, d_out, nm_out, nv_out, g_s, d_s, nm_s, nv_s) = _exchange_update(
        dwin_p, dwout_p, small_p, w_in[0], m_w_in[0], v_w_in[0], w_out[0], m_w_out[0], v_w_out[0], w_s, m_s, v_s)

    loss = g_s[ROW_LOSS, 0]
    return (loss, grad_x[None], g_in[None], g_out[None], *_unpack_small(g_s),
            d_in[None], d_out[None], *_unpack_small(d_s),
            nm_in[None], nm_out[None], *_unpack_small(nm_s),
            nv_in[None], nv_out[None], *_unpack_small(nv_s))
```

```python
import functools

import jax
import jax.numpy as jnp
from jax import lax
from jax.experimental import pallas as pl
from jax.experimental.pallas import tpu as pltpu

F32 = jnp.float32
BF16 = jnp.bfloat16

SEQ = 4096
D_MODEL = 1024
ATTN_WIDTH = 512
HGRN_WIDTH = 512
HEAD_DIM = 64
HGRN_HEADS = 4
HGRN_DIM = 128
HGRN_CHUNK = 64
N_CHUNKS = SEQ // HGRN_CHUNK
IN_COLS = 4096
COL_BLOCK = 512
N_DEV = 8
WOUT_ROWS = D_MODEL // N_DEV
ATTN_BLOCK = 128
DILATIONS = (1, 4, 16)
ROPE_THETA = 500000.0
ROPE_DIMS = 16
ROPE_HALF = 8
NORM_EPS = 1e-6
NEG_BIG = -1e30
LANES = 128

ADAM_LR = 0.001
ADAM_B1 = 0.9
ADAM_B2 = 0.999
ADAM_EPS = 1e-08
ADAM_WD = 0.01
ADAM_STEP = 10

SMALL_ROWS = 48
ROW_MIX, ROW_ATTN, ROW_HGRN, ROW_LB, ROW_FINAL, ROW_LOSS = 0, 8, 16, 24, 32, 40

VMEM_LIMIT = 56 * 1024 * 1024
MESH = pl.DeviceIdType.MESH


def _mm(a, b):
    return lax.dot_general(a, b, (((1,), (0,)), ((), ())), preferred_element_type=F32)


def _mm_nt(a, b):
    return lax.dot_general(a, b, (((1,), (1,)), ((), ())), preferred_element_type=F32)


def _mm_tn(a, b):
    return lax.dot_general(a, b, (((0,), (0,)), ((), ())), preferred_element_type=F32)


def _mm_exact(a, b):
    return lax.dot_general(a, b, (((1,), (0,)), ((), ())), preferred_element_type=F32,
                           precision=lax.Precision.HIGHEST)


def _sigmoid(v):
    return 1.0 / (1.0 + jnp.exp(-v))


def _params(sem=None, **kw):
    return pltpu.CompilerParams(dimension_semantics=sem, vmem_limit_bytes=VMEM_LIMIT, **kw)


def _my_place():
    return lax.axis_index("x"), lax.axis_index("y"), lax.axis_index("c")


def _peer(place, rel):
    x, y, c = place
    return (x ^ ((rel >> 2) & 1), y ^ ((rel >> 1) & 1), c ^ (rel & 1))


def _flat(place):
    x, y, c = place
    return 4 * x + 2 * y + c


def _rope_tables(pos_col, inv_freq_lanes):
    tm = 512

    def body(pos_ref, invf_ref, c_ref, sa_ref, sb_ref):
        ang = pos_ref[...].astype(F32) * invf_ref[...]
        e = lax.broadcasted_iota(jnp.int32, (tm, LANES), 1) & (HEAD_DIM - 1)
        cos, sin = jnp.cos(ang), jnp.sin(ang)
        c_ref[...] = jnp.where(e < ROPE_DIMS, cos, 1.0)
        sa_ref[...] = jnp.where((e >= ROPE_HALF) & (e < ROPE_DIMS), sin, 0.0)
        sb_ref[...] = jnp.where(e < ROPE_HALF, -sin, 0.0)

    tab = jax.ShapeDtypeStruct((SEQ, LANES), F32)
    spec = pl.BlockSpec((tm, LANES), lambda i: (i, 0))
    return pl.pallas_call(
        body, name="rope_tables", grid=(SEQ // tm,), out_shape=(tab, tab, tab),
        in_specs=[pl.BlockSpec((tm, 1), lambda i: (i, 0)), pl.BlockSpec((1, LANES), lambda i: (0, 0))],
        out_specs=(spec, spec, spec), compiler_params=_params(("parallel",)),
    )(pos_col, inv_freq_lanes)


def _per_slab(fn, t):
    return jnp.concatenate([fn(t[:, LANES * s:LANES * (s + 1)]) for s in range(t.shape[1] // LANES)], axis=1)


def _rot(t, c, sa, sb):
    return _per_slab(lambda u: u * c + pltpu.roll(u, ROPE_HALF, 1) * sa + pltpu.roll(u, LANES - ROPE_HALF, 1) * sb, t)


def _rot_transposed(g, c, sa, sb):
    return _per_slab(
        lambda u: u * c + pltpu.roll(u * sa, LANES - ROPE_HALF, 1) + pltpu.roll(u * sb, ROPE_HALF, 1), g)


def _gather_project(x, mix_w, w_in, w_out, rc, rsa, rsb):
    tm = 1024
    n_tiles = SEQ // tm
    arrival_of_step = (None, 0, 1, 2, 4, 5, 3, 6)

    def body(order_ref, x_ref, w_ref, win_ref, wout_ref, c_ref, sa_ref, sb_ref,
             proj_ref, hnt_ref, gin_hbm, gout_hbm,
             hn_s, w_land, wout_land, stage, send_sems, recv_sems, local_sems):
        g, i = pl.program_id(0), pl.program_id(1)
        me = _my_place()
        x_, y_, c_ = me
        sibling = (x_, y_, 1 - c_)
        chips = [(1 - x_, y_), (x_, 1 - y_), (1 - x_, 1 - y_)]

        def slab(which, place):
            idx = _flat(place)
            if which == 0:
                return w_land.at[idx]
            return wout_land.at[pl.ds(pl.multiple_of(idx * WOUT_ROWS, WOUT_ROWS), WOUT_ROWS), :]

        def copy(which, k, block, to, src=None):
            ref = slab(which, block)
            return pltpu.make_async_remote_copy(
                src_ref=ref if src is None else src, dst_ref=ref, send_sem=send_sems.at[7 * which + k],
                recv_sem=recv_sems.at[7 * which + k], device_id=to, device_id_type=MESH)

        def first_copies(which):
            src = stage if which == 0 else None
            return ([copy(which, 0, me, sibling, src)]
                    + [copy(which, 1 + j, me, (*chip, c_), src) for j, chip in enumerate(chips)])

        def pass_on(which, j):
            return copy(which, 4 + j, (*chips[j], c_), sibling)

        def arrival(which, k):
            if k == 0:
                return copy(which, 0, sibling, me)
            if k <= 3:
                return copy(which, k, (*chips[k - 1], c_), me)
            return copy(which, k, (*chips[k - 4], 1 - c_), me)

        def to_hbm(step):
            idx = order_ref[step]
            cols = pl.ds(pl.multiple_of(idx * COL_BLOCK, COL_BLOCK), COL_BLOCK)
            return pltpu.make_async_copy(w_land.at[idx], gin_hbm.at[:, cols], local_sems.at[step])

        @pl.when((g == 0) & (i == 0))
        def _():
            stage[...] = win_ref[...].astype(BF16)
            w_land[_flat(me)] = stage[...]
            wout_land[pl.ds(pl.multiple_of(_flat(me) * WOUT_ROWS, WOUT_ROWS), WOUT_ROWS), :] = (
                wout_ref[...].astype(BF16))
            for which in (0, 1):
                for cp in first_copies(which):
                    cp.start()
            to_hbm(0).start()

        for step, k in enumerate(arrival_of_step):
            if k is None:
                continue

            @pl.when((g == step) & (i == 0))
            def _(k=k, step=step):
                arrival(0, k).wait_recv()
                to_hbm(step).start()
                if 1 <= k <= 3:
                    arrival(1, k).wait_recv()
                    pass_on(0, k - 1).start()
                    pass_on(1, k - 1).start()

        rows = pl.ds(pl.multiple_of(i * tm, tm), tm)

        @pl.when(g == 0)
        def _():
            xf = x_ref[...]
            ms = jnp.mean(xf * xf, axis=-1, keepdims=True)
            hn = xf * lax.rsqrt(ms + NORM_EPS) * w_ref[...]
            hnt_ref[...] = hn.T.astype(BF16)
            hn_s[rows, :] = hn.astype(BF16)

        group = order_ref[g]

        @pl.when(group < 2)
        def _():
            proj_ref[...] = _rot(_mm(hn_s[rows, :], w_land[group]), c_ref[...], sa_ref[...], sb_ref[...])

        @pl.when(group >= 2)
        def _():
            proj_ref[...] = _mm(hn_s[rows, :], w_land[group])

        @pl.when((g == N_DEV - 1) & (i == n_tiles - 1))
        def _():
            for k in (0, 4, 5, 6):
                arrival(1, k).wait_recv()
            for which in (0, 1):
                for cp in first_copies(which) + [pass_on(which, j) for j in range(3)]:
                    cp.wait_send()
            wout_copy = pltpu.make_async_copy(wout_land, gout_hbm, local_sems.at[N_DEV])
            wout_copy.start()
            for step in range(N_DEV):
                to_hbm(step).wait()
            wout_copy.wait()

    me = _my_place()
    x_, y_, c_ = me
    chips = [(1 - x_, y_), (x_, 1 - y_), (1 - x_, 1 - y_)]
    order = jnp.stack([_flat(p) for p in (
        me, (x_, y_, 1 - c_), (*chips[0], c_), (*chips[1], c_), (*chips[0], 1 - c_), (*chips[1], 1 - c_),
        (*chips[2], c_), (*chips[2], 1 - c_))]).astype(jnp.int32)

    first_sweep = lambda g, i, order: (jnp.where(g == 0, i, n_tiles - 1), 0)
    tab = pl.BlockSpec((tm, LANES), lambda g, i, order: (jnp.where(order[g] < 2, i, 0), 0))
    whole = lambda: pl.BlockSpec(memory_space=pltpu.VMEM)
    grid_spec = pltpu.PrefetchScalarGridSpec(
        num_scalar_prefetch=1, grid=(N_DEV, n_tiles),
        in_specs=[pl.BlockSpec((tm, D_MODEL), first_sweep),
                  pl.BlockSpec((1, D_MODEL), lambda g, i, order: (0, 0)),
                  whole(), whole(), tab, tab, tab],
        out_specs=(pl.BlockSpec((tm, COL_BLOCK), lambda g, i, order: (i, order[g])),
                   pl.BlockSpec((D_MODEL, tm), lambda g, i, order: (0, jnp.where(g == 0, i, n_tiles - 1))),
                   pl.BlockSpec(memory_space=pl.ANY), pl.BlockSpec(memory_space=pl.ANY)),
        scratch_shapes=[pltpu.VMEM((SEQ, D_MODEL), BF16),
                        pltpu.VMEM((N_DEV, D_MODEL, COL_BLOCK), BF16),
                        pltpu.VMEM((D_MODEL, D_MODEL), BF16),
                        pltpu.VMEM((D_MODEL, COL_BLOCK), BF16),
                        pltpu.SemaphoreType.DMA((14,)), pltpu.SemaphoreType.DMA((14,)),
                        pltpu.SemaphoreType.DMA((N_DEV + 1,))])
    return pl.pallas_call(
        body, name="gather_project", grid_spec=grid_spec,
        out_shape=(jax.ShapeDtypeStruct((SEQ, IN_COLS), F32), jax.ShapeDtypeStruct((D_MODEL, SEQ), BF16),
                   jax.ShapeDtypeStruct((D_MODEL, IN_COLS), BF16), jax.ShapeDtypeStruct((D_MODEL, D_MODEL), BF16)),
        compiler_params=_params(("arbitrary", "arbitrary")),
    )(order, x, mix_w, w_in, w_out, rc, rsa, rsb)


ATTN_GROUP = 8
BLOCKS_PER_PATTERN = SEQ // ATTN_BLOCK


def _write_band_bias(bias_ref):
    qi = lax.broadcasted_iota(jnp.int32, (2 * ATTN_BLOCK, 2 * ATTN_BLOCK), 0) & (ATTN_BLOCK - 1)
    kj = lax.broadcasted_iota(jnp.int32, (2 * ATTN_BLOCK, 2 * ATTN_BLOCK), 1)
    bias_ref[0] = jnp.where((kj >= qi) & (kj <= qi + ATTN_BLOCK), 0.0, NEG_BIG)
    bias_ref[1] = jnp.where(kj <= qi, 0.0, NEG_BIG)


def _head0_lanes():
    return lax.broadcasted_iota(jnp.int32, (ATTN_BLOCK, LANES), 1) < HEAD_DIM


def _stack_heads(t, h0):
    return jnp.concatenate([jnp.where(h0, t, 0.0), jnp.where(h0, 0.0, t)], axis=0).astype(BF16)


def _strided(start, size, d):
    return pl.ds(start, size) if d == 1 else pl.ds(start, size, stride=d)


def _block_place(i, d):
    nblk = BLOCKS_PER_PATTERN // d
    r, n = i // nblk, i % nblk
    kn = jnp.maximum(n - 1, 0)
    row0, key0 = n * (d * ATTN_BLOCK) + r, kn * (d * ATTN_BLOCK) + r
    if d == 1:
        row0, key0 = pl.multiple_of(row0, ATTN_BLOCK), pl.multiple_of(key0, ATTN_BLOCK)
    return _strided(row0, ATTN_BLOCK, d), _strided(key0, 2 * ATTN_BLOCK, d), (n == 0).astype(jnp.int32)


def _for_each_group(d, load, compute, store):
    def group(g, carry):
        items = [load(*_block_place(g * ATTN_GROUP + u, d)) for u in range(ATTN_GROUP)]
        results = [compute(item) for item in items]
        for item, res in zip(items, results):
            store(item, res)
        return carry

    lax.fori_loop(0, BLOCKS_PER_PATTERN // ATTN_GROUP, group, 0)


def _attn_fwd_fused(proj):
    n_pat = len(DILATIONS)
    tile2 = (2 * ATTN_BLOCK, LANES)

    def body(q_ref, k_ref, v_ref, o_ref, lse_ref, m_acc, l_acc, bias_ref):
        _write_band_bias(bias_ref)
        h0 = _head0_lanes()
        for pi, d in enumerate(DILATIONS):
            first, last = pi == 0, pi == n_pat - 1

            def load(rows, keys, which, first=first):
                item = dict(rows=rows, keys=keys, which=which)
                if not first:
                    item.update(o=o_ref[rows, :], m=[m_acc.at[h][rows, :] for h in range(2)],
                                l=[l_acc.at[h][rows, :] for h in range(2)])
                return item

            def compute(item, first=first):
                kb = k_ref[item["keys"], :].astype(BF16)
                vb = v_ref[item["keys"], :].astype(BF16)
                s = _mm_nt(_stack_heads(q_ref[item["rows"], :], h0), kb) * 0.125 + bias_ref[item["which"]]
                mb = jnp.max(s, axis=-1, keepdims=True)
                if first:
                    p = jnp.exp(s - mb)
                    mn = jnp.broadcast_to(mb, tile2)
                else:
                    m_old = jnp.concatenate(item["m"], axis=0)
                    mn = jnp.maximum(m_old, mb)
                    alpha = jnp.exp(m_old - mn)
                    p = jnp.exp(s - jnp.concatenate([mn, mn], axis=1))
                ls = jnp.sum(p, axis=-1, keepdims=True)
                pv = _mm(p.astype(BF16), vb)
                if first:
                    return pv, mn, jnp.broadcast_to(ls, tile2)
                o_old = jnp.concatenate([item["o"], item["o"]], axis=0)
                return alpha * o_old + pv, mn, alpha * jnp.concatenate(item["l"], axis=0) + ls

            def store(item, res, last=last):
                rows = item["rows"]
                (o0, o1), (m0, m1), (l0, l1) = ((a[:ATTN_BLOCK], a[ATTN_BLOCK:]) for a in res)
                if last:
                    o_ref[rows, :] = jnp.where(h0, o0 / l0, o1 / l1)
                    lse_ref[rows, :] = jnp.where(h0, m0 + jnp.log(l0), m1 + jnp.log(l1))
                else:
                    o_ref[rows, :] = jnp.where(h0, o0, o1)
                    m_acc.at[0][rows, :], m_acc.at[1][rows, :] = m0, m1
                    l_acc.at[0][rows, :], l_acc.at[1][rows, :] = l0, l1

            _for_each_group(d, load, compute, store)

    slab = lambda g: pl.BlockSpec((SEQ, LANES), functools.partial(lambda hp, g: (0, 4 * g + hp), g=g))
    wide = jax.ShapeDtypeStruct((SEQ, ATTN_WIDTH), F32)
    return pl.pallas_call(
        body, name="attn_fwd", grid=(4,), out_shape=(wide, wide),
        in_specs=[slab(0), slab(1), slab(2)], out_specs=(slab(0), slab(0)),
        scratch_shapes=[pltpu.VMEM((2, SEQ, LANES), F32), pltpu.VMEM((2, SEQ, LANES), F32),
                        pltpu.VMEM((2, 2 * ATTN_BLOCK, 2 * ATTN_BLOCK), F32)],
        compiler_params=_params(("parallel",)),
    )(proj, proj, proj)


def _attn_bwd_fused(proj, d_out, lse, delta):
    def body(q_ref, k_ref, v_ref, do_ref, lse_ref, del_ref, dq_ref, dk_ref, dv_ref, bias_ref):
        _write_band_bias(bias_ref)
        dk_ref[...] = jnp.zeros_like(dk_ref)
        dv_ref[...] = jnp.zeros_like(dv_ref)
        h0 = _head0_lanes()
        for pi, d in enumerate(DILATIONS):
            first = pi == 0

            def load(rows, keys, which):
                return dict(rows=rows, keys=keys, q=q_ref[rows, :], g=do_ref[rows, :], lse=lse_ref[rows, :],
                            delta=del_ref[rows, :], k=k_ref[keys, :].astype(BF16),
                            v=v_ref[keys, :].astype(BF16), bias=bias_ref[which])

            def per_head(t):
                swapped = pltpu.roll(t, HEAD_DIM, 1)
                both = jnp.concatenate([jnp.where(h0, t, swapped), jnp.where(h0, swapped, t)], axis=0)
                return jnp.concatenate([both, both], axis=1)

            def compute(item):
                q2, g2 = _stack_heads(item["q"], h0), _stack_heads(item["g"], h0)
                s = _mm_nt(q2, item["k"]) * 0.125 + item["bias"]
                p = jnp.exp(s - per_head(item["lse"]))
                dp = _mm_nt(g2, item["v"])
                ds = (p * (dp - per_head(item["delta"])) * 0.125).astype(BF16)
                dq2 = _mm(ds, item["k"])
                dq = jnp.where(h0, dq2[:ATTN_BLOCK], dq2[ATTN_BLOCK:])
                return dq, _mm_tn(ds, q2), _mm_tn(p.astype(BF16), g2)

            def store(item, res, first=first):
                rows, keys = item["rows"], item["keys"]
                if first:
                    dq_ref[rows, :] = res[0]
                else:
                    dq_ref[rows, :] += res[0]
                dk_ref[keys, :] += res[1]
                dv_ref[keys, :] += res[2]

            _for_each_group(d, load, compute, store)

    slab = lambda g: pl.BlockSpec((SEQ, LANES), functools.partial(lambda hp, g: (0, 4 * g + hp), g=g))
    wide = jax.ShapeDtypeStruct((SEQ, ATTN_WIDTH), F32)
    return pl.pallas_call(
        body, name="attn_bwd", grid=(4,), out_shape=(wide, wide, wide),
        scratch_shapes=[pltpu.VMEM((2, 2 * ATTN_BLOCK, 2 * ATTN_BLOCK), F32)],
        in_specs=[slab(0), slab(1), slab(2), slab(0), slab(0), slab(0)], out_specs=(slab(0), slab(0), slab(0)),
        compiler_params=_params(("parallel",)),
    )(proj, proj, proj, d_out, lse, delta)


def _hgrn_lower_bound(lb_ref):
    r0, r1 = lb_ref[0:1, :], lb_ref[1:2, :]
    mx = jnp.maximum(r0, r1)
    e0, e1 = jnp.exp(r0 - mx), jnp.exp(r1 - mx)
    return e0 / (e0 + e1)


def _hgrn_gates(hq, hf, lb):
    sq = _sigmoid(hq)
    sg = _sigmoid(hf)
    f = lb + (1.0 - lb) * sg
    return hq * sq, sq, sg, f, 1.0 - f, jnp.log(f)


HGRN_PAIR = 2
HGRN_SEQ_BLOCK = 1024
HGRN_GROUP = 4
HGRN_ROWS = HGRN_GROUP * HGRN_CHUNK


def _hgrn_specs(reverse):
    n_blocks = SEQ // HGRN_SEQ_BLOCK
    width = HGRN_PAIR * HGRN_DIM
    blk = (lambda s: n_blocks - 1 - s) if reverse else (lambda s: s)
    cols = lambda g: pl.BlockSpec((HGRN_SEQ_BLOCK, width),
                                  functools.partial(lambda p, s, g: (blk(s), HGRN_PAIR * g + p), g=g))
    pair = pl.BlockSpec((HGRN_SEQ_BLOCK, width), lambda p, s: (blk(s), p))
    lb = pl.BlockSpec((2, width), lambda p, s: (0, p))
    states = pl.BlockSpec((HGRN_PAIR, HGRN_SEQ_BLOCK // HGRN_CHUNK, HGRN_DIM, HGRN_DIM),
                          lambda p, s: (p, blk(s), 0, 0))
    return cols, pair, lb, states


def _chunk_masks():
    ri = lax.broadcasted_iota(jnp.int32, (HGRN_ROWS, HGRN_ROWS), 0)
    ci = lax.broadcasted_iota(jnp.int32, (HGRN_ROWS, HGRN_ROWS), 1)
    same = (ri // HGRN_CHUNK) == (ci // HGRN_CHUNK)
    return same, same & (ri >= ci), same & (ri <= ci)


def _mm_select(sel, v):
    hi = v.astype(BF16)
    r1 = v - hi.astype(F32)
    mid = r1.astype(BF16)
    lo = (r1 - mid.astype(F32)).astype(BF16)
    return _mm(sel, hi) + _mm(sel, mid) + _mm(sel, lo)


def _head_cols(a, h):
    return a[:, HGRN_DIM * h:HGRN_DIM * (h + 1)]


def _hgrn_fwd(proj, lb_raw):
    t, rws = HGRN_CHUNK, HGRN_ROWS

    def body(hq_ref, hf_ref, hi_ref, lb_ref, rec_ref, st_ref, state):
        @pl.when(pl.program_id(1) == 0)
        def _():
            state[...] = jnp.zeros_like(state)

        lb = _hgrn_lower_bound(lb_ref)
        same, causal, _ = _chunk_masks()
        sel = jnp.concatenate([causal, same], axis=0).astype(BF16)

        def group(g, sts):
            rows = pl.ds(pl.multiple_of(g * rws, rws), rws)
            q, _, _, _, k, lf = _hgrn_gates(hq_ref[rows, :], hf_ref[rows, :], lb)
            sums = _mm_select(sel, lf)
            cum, last = sums[:rws], sums[rws:]
            qd = (q * jnp.exp(cum)).astype(BF16)
            ki = (k * jnp.exp(-cum)).astype(BF16)
            ke = (k * jnp.exp(last - cum)).astype(BF16)
            vb = hi_ref[rows, :].astype(BF16)
            dec = jnp.exp(last)
            new_sts, recs = [], []
            for h in range(HGRN_PAIR):
                qd_h, ke_h, vb_h = _head_cols(qd, h), _head_cols(ke, h), _head_cols(vb, h)
                att = jnp.where(causal, _mm_nt(qd_h, _head_cols(ki, h)), 0.0).astype(BF16)
                intra = _mm(att, vb_h)
                st = sts[h]
                outs = []
                for c in range(HGRN_GROUP):
                    sl = slice(c * t, (c + 1) * t)
                    st_ref[h, g * HGRN_GROUP + c] = st
                    outs.append(intra[sl] + _mm_nt(qd_h[sl], st.astype(BF16)))
                    st = st * _head_cols(dec[c * t:c * t + 1, :], h) + _mm_tn(vb_h[sl], ke_h[sl])
                new_sts.append(st)
                recs.append(jnp.concatenate(outs, axis=0))
            rec_ref[rows, :] = jnp.concatenate(recs, axis=1)
            return tuple(new_sts)

        sts = lax.fori_loop(0, HGRN_SEQ_BLOCK // rws, group, tuple(state[h] for h in range(HGRN_PAIR)))
        for h in range(HGRN_PAIR):
            state[h] = sts[h]

    cols, pair, lb, states = _hgrn_specs(reverse=False)
    return pl.pallas_call(
        body, name="hgrn_fwd", grid=(HGRN_HEADS // HGRN_PAIR, SEQ // HGRN_SEQ_BLOCK),
        out_shape=(jax.ShapeDtypeStruct((SEQ, HGRN_WIDTH), F32),
                   jax.ShapeDtypeStruct((HGRN_HEADS, N_CHUNKS, HGRN_DIM, HGRN_DIM), F32)),
        in_specs=[cols(4), cols(5), cols(6), lb], out_specs=(pair, states),
        scratch_shapes=[pltpu.VMEM((HGRN_PAIR, HGRN_DIM, HGRN_DIM), F32)],
        compiler_params=_params(("parallel", "arbitrary")),
    )(proj, proj, proj, lb_raw)


def _hgrn_bwd(proj, lb_raw, d_rec, states):
    t, rws = HGRN_CHUNK, HGRN_ROWS

    def body(hq_ref, hf_ref, hi_ref, lb_ref, do_ref, st_ref, dhq_ref, dhf_ref, dhi_ref, dlb_ref,
             dstate, dlb_acc):
        lb = _hgrn_lower_bound(lb_ref)
        same, causal, anti = _chunk_masks()
        sel = jnp.concatenate([causal, same], axis=0).astype(BF16)
        sel_t = jnp.concatenate([anti, same], axis=1).astype(BF16)
        @pl.when(pl.program_id(1) == 0)
        def _():
            dstate[...] = jnp.zeros_like(dstate)
            dlb_acc[...] = jnp.zeros_like(dlb_acc)

        n_groups = HGRN_SEQ_BLOCK // rws
        chunks = [slice(c * t, (c + 1) * t) for c in range(HGRN_GROUP)]

        def group(i, dsts_in):
            g = n_groups - 1 - i
            rows = pl.ds(pl.multiple_of(g * rws, rws), rws)
            hq = hq_ref[rows, :]
            q, sq, sg, f, k, lf = _hgrn_gates(hq, hf_ref[rows, :], lb)
            sums = _mm_select(sel, lf)
            cum, last = sums[:rws], sums[rws:]
            e_cum, e_inv, e_end, dec = jnp.exp(cum), jnp.exp(-cum), jnp.exp(last - cum), jnp.exp(last)
            qd, ki, ke = q * e_cum, k * e_inv, k * e_end
            qdb, kib, keb = qd.astype(BF16), ki.astype(BF16), ke.astype(BF16)
            vb = hi_ref[rows, :].astype(BF16)
            gb = do_ref[rows, :].astype(BF16)

            dsts_out, per_head = [], []
            for h in range(HGRN_PAIR):
                qdb_h, kib_h, keb_h = _head_cols(qdb, h), _head_cols(kib, h), _head_cols(keb, h)
                vb_h, gb_h = _head_cols(vb, h), _head_cols(gb, h)
                att = jnp.where(causal, _mm_nt(qdb_h, kib_h), 0.0).astype(BF16)
                datt = jnp.where(causal, _mm_nt(gb_h, vb_h), 0.0).astype(BF16)
                dv = _mm_tn(att, gb_h)
                dqd = _mm(datt, kib_h)
                dki = _mm_tn(datt, qdb_h)

                decs = [_head_cols(dec[c * t:c * t + 1, :], h) for c in range(HGRN_GROUP)]
                dsts = [None] * HGRN_GROUP
                dst = dsts_in[h]
                for c in reversed(range(HGRN_GROUP)):
                    dsts[c] = dst
                    dst = dst * decs[c] + _mm_tn(gb_h[chunks[c]], qdb_h[chunks[c]])
                dsts_out.append(dst)

                dv_x, dqd_x, dke, dlast_x = [], [], [], []
                for c, sl in enumerate(chunks):
                    st_prev = st_ref[h, g * HGRN_GROUP + c]
                    dstb = dsts[c].astype(BF16)
                    dv_x.append(_mm_nt(keb_h[sl], dstb))
                    dqd_x.append(_mm(gb_h[sl], st_prev.astype(BF16)))
                    dke.append(_mm(vb_h[sl], dstb))
                    ddec = jnp.sum(dsts[c] * st_prev, axis=0, keepdims=True)
                    dlast_x.append(jnp.broadcast_to(ddec * decs[c], (t, HGRN_DIM)))
                per_head.append((dv + jnp.concatenate(dv_x, axis=0), dqd + jnp.concatenate(dqd_x, axis=0),
                                 dki, jnp.concatenate(dke, axis=0), jnp.concatenate(dlast_x, axis=0)))
            dv, dqd, dki, dke, dlast = (jnp.concatenate([a, b], axis=1) for a, b in zip(*per_head))

            dq = dqd * e_cum
            dk = dki * e_inv + dke * e_end
            dke_ke = dke * ke
            dcum = dqd * qd - dki * ki - dke_ke
            dlf = _mm_select(sel_t, jnp.concatenate([dcum, dke_ke], axis=0)) + dlast
            df = dlf / f - dk
            dhq_ref[rows, :] = dq * (sq * (1.0 + hq * (1.0 - sq)))
            dhf_ref[rows, :] = df * (1.0 - lb) * (sg * (1.0 - sg))
            dhi_ref[rows, :] = dv
            dlb_acc[...] += jnp.sum(df * (1.0 - sg), axis=0, keepdims=True)
            return tuple(dsts_out)

        dsts = lax.fori_loop(0, n_groups, group, tuple(dstate[h] for h in range(HGRN_PAIR)))
        for h in range(HGRN_PAIR):
            dstate[h] = dsts[h]
        g0 = dlb_acc[...] * lb * (1.0 - lb)
        dlb_ref[...] = jnp.concatenate([g0, -g0], axis=0)

    cols, pair, lb_spec, st_spec = _hgrn_specs(reverse=True)
    wide = jax.ShapeDtypeStruct((SEQ, HGRN_WIDTH), F32)
    return pl.pallas_call(
        body, name="hgrn_bwd", grid=(HGRN_HEADS // HGRN_PAIR, SEQ // HGRN_SEQ_BLOCK),
        out_shape=(wide, wide, wide, jax.ShapeDtypeStruct((2, HGRN_WIDTH), F32)),
        in_specs=[cols(4), cols(5), cols(6), lb_spec, pair, st_spec],
        out_specs=(pair, pair, pair, lb_spec),
        scratch_shapes=[pltpu.VMEM((HGRN_PAIR, HGRN_DIM, HGRN_DIM), F32),
                        pltpu.VMEM((1, HGRN_PAIR * HGRN_DIM), F32)],
        compiler_params=_params(("parallel", "arbitrary")),
    )(proj, proj, proj, lb_raw, d_rec, states)


def _group_sum(v, group):
    parts = []
    for s in range(v.shape[1] // LANES):
        slab = v[:, LANES * s:LANES * (s + 1)]
        if group == LANES:
            parts.append(jnp.broadcast_to(jnp.sum(slab, axis=-1, keepdims=True), slab.shape))
        else:
            h0 = lax.broadcasted_iota(jnp.int32, slab.shape, 1) < HEAD_DIM
            s0 = jnp.sum(jnp.where(h0, slab, 0.0), axis=-1, keepdims=True)
            s1 = jnp.sum(jnp.where(h0, 0.0, slab), axis=-1, keepdims=True)
            parts.append(jnp.where(h0, s0, s1))
    return jnp.concatenate(parts, axis=1)


def _mid(attn_o, rec, proj, x, target, w_out_g, attn_w, hgrn_w, final_w):
    tm = 256

    def branch_fwd(o, gate, w, group):
        r = lax.rsqrt(_group_sum(o * o, group) * (1.0 / group) + NORM_EPS)
        nrm = o * r
        sg = _sigmoid(gate)
        return r, nrm, sg, nrm * w * (gate * sg)

    def branch_bwd(dy, r, nrm, sg, gate, w, group):
        silu = gate * sg
        d_gate = dy * nrm * w * (sg * (1.0 + gate * (1.0 - sg)))
        d_w = jnp.sum(dy * nrm * silu, axis=0, keepdims=True)
        dn = dy * w * silu
        d_o = r * (dn - nrm * (_group_sum(dn * nrm, group) * (1.0 / group)))
        return d_o, d_gate, d_w

    def body(o_ref, rec_ref, ag_ref, hg_ref, x_ref, tgt_ref, wout_ref, aw_ref, hw_ref, fw_ref,
             dx2_ref, do_ref, delta_ref, dag_ref, drec_ref, dhg_ref, dwout_ref, dfw_ref, daw_ref, dhw_ref,
             loss_ref, dwout_acc):
        i = pl.program_id(0)

        @pl.when(i == 0)
        def _():
            dwout_acc[...] = jnp.zeros_like(dwout_acc)
            dfw_ref[...] = jnp.zeros_like(dfw_ref)
            daw_ref[...] = jnp.zeros_like(daw_ref)
            dhw_ref[...] = jnp.zeros_like(dhw_ref)
            loss_ref[...] = jnp.zeros_like(loss_ref)

        o, rc, ag, hg = o_ref[...], rec_ref[...], ag_ref[...], hg_ref[...]
        aw, hw, fw = aw_ref[...], hw_ref[...], fw_ref[...]
        ra, na, sga, ya = branch_fwd(o, ag, aw, HEAD_DIM)
        rh, nh, sgh, yh = branch_fwd(rc, hg, hw, HGRN_DIM)
        mixed = jnp.concatenate([ya, yh], axis=1).astype(BF16)
        wout = wout_ref[...]
        x2 = x_ref[...] + _mm(mixed, wout)
        rstd = lax.rsqrt(jnp.mean(x2 * x2, axis=-1, keepdims=True) + NORM_EPS)
        xn = x2 * rstd
        err = xn * fw - tgt_ref[...]
        row_loss = jnp.mean(err * err, axis=-1, keepdims=True)
        loss_ref[...] += 0.5 * jnp.sum(row_loss, axis=0, keepdims=True)
        dy = err * (1.0 / D_MODEL)
        dfw_ref[...] += jnp.sum(dy * xn, axis=0, keepdims=True)
        dxn = dy * fw
        dx2 = rstd * (dxn - xn * jnp.mean(dxn * xn, axis=-1, keepdims=True))
        dx2_ref[...] = dx2
        dx2b = dx2.astype(BF16)
        dwout_acc[...] += _mm_tn(mixed, dx2b)

        @pl.when(i == pl.num_programs(0) - 1)
        def _():
            dwout_ref[...] = dwout_acc[...].astype(BF16)

        dmixed = _mm_nt(dx2b, wout)

        d_o, d_ag, d_aw = branch_bwd(dmixed[:, :ATTN_WIDTH], ra, na, sga, ag, aw, HEAD_DIM)
        d_rec, d_hg, d_hw = branch_bwd(dmixed[:, ATTN_WIDTH:], rh, nh, sgh, hg, hw, HGRN_DIM)
        do_ref[...] = d_o
        delta_ref[...] = _group_sum(d_o * o, HEAD_DIM)
        dag_ref[...] = d_ag
        drec_ref[...] = d_rec
        dhg_ref[...] = d_hg
        daw_ref[...] += d_aw
        dhw_ref[...] += d_hw

    half = lambda: pl.BlockSpec((tm, COL_BLOCK), lambda i: (i, 0))
    full = lambda: pl.BlockSpec((tm, D_MODEL), lambda i: (i, 0))
    fixed = lambda r, c: pl.BlockSpec((r, c), lambda i: (0, 0))
    wide = jax.ShapeDtypeStruct((SEQ, COL_BLOCK), F32)
    return pl.pallas_call(
        body, name="mid", grid=(SEQ // tm,),
        out_shape=(jax.ShapeDtypeStruct((SEQ, D_MODEL), F32), wide, wide, wide, wide, wide,
                   jax.ShapeDtypeStruct((D_MODEL, D_MODEL), BF16),
                   jax.ShapeDtypeStruct((1, D_MODEL), F32), jax.ShapeDtypeStruct((1, COL_BLOCK), F32),
                   jax.ShapeDtypeStruct((1, COL_BLOCK), F32), jax.ShapeDtypeStruct((1, 1), F32)),
        scratch_shapes=[pltpu.VMEM((D_MODEL, D_MODEL), F32)],
        in_specs=[half(), half(),
                  pl.BlockSpec((tm, COL_BLOCK), lambda i: (i, 3)), pl.BlockSpec((tm, COL_BLOCK), lambda i: (i, 7)),
                  full(), full(), fixed(D_MODEL, D_MODEL), fixed(1, COL_BLOCK), fixed(1, COL_BLOCK),
                  fixed(1, D_MODEL)],
        out_specs=(full(), half(), half(), half(), half(), half(), fixed(D_MODEL, D_MODEL),
                   fixed(1, D_MODEL), fixed(1, COL_BLOCK), fixed(1, COL_BLOCK), fixed(1, 1)),
        compiler_params=_params(("arbitrary",)),
    )(attn_o, rec, proj, proj, x, target, w_out_g, attn_w, hgrn_w, final_w)


def _in_proj_bwd_rows(d_groups, w_g, x, dx2, mix_w, rc, rsa, rsb):
    tm = 256

    def body(*refs):
        dg_refs = refs[:N_DEV]
        wg_ref, x_ref, dx2_ref, w_ref, c_ref, sa_ref, sb_ref, gx_ref, dpb_ref, dmw_ref = refs[N_DEV:]

        @pl.when(pl.program_id(0) == 0)
        def _():
            dmw_ref[...] = jnp.zeros_like(dmw_ref)

        parts = []
        for j in range(N_DEV):
            dp = dg_refs[j][...]
            if j < 2:
                dp = _rot_transposed(dp, c_ref[...], sa_ref[...], sb_ref[...])
            parts.append(dp.astype(BF16))
        dpb = jnp.concatenate(parts, axis=1)
        dpb_ref[...] = dpb
        g = _mm_nt(dpb, wg_ref[...])
        xf = x_ref[...]
        rstd = lax.rsqrt(jnp.mean(xf * xf, axis=-1, keepdims=True) + NORM_EPS)
        xn = xf * rstd
        dmw_ref[...] += jnp.sum(g * xn, axis=0, keepdims=True)
        gw = g * w_ref[...]
        gx_ref[...] = dx2_ref[...] + rstd * (gw - xn * jnp.mean(gw * xn, axis=-1, keepdims=True))

    tile = lambda cols: pl.BlockSpec((tm, cols), lambda i: (i, 0))
    fixed = lambda r, c: pl.BlockSpec((r, c), lambda i: (0, 0))
    return pl.pallas_call(
        body, name="in_proj_bwd_rows", grid=(SEQ // tm,),
        out_shape=(jax.ShapeDtypeStruct((SEQ, D_MODEL), F32), jax.ShapeDtypeStruct((SEQ, IN_COLS), BF16),
                   jax.ShapeDtypeStruct((1, D_MODEL), F32)),
        in_specs=[tile(COL_BLOCK) for _ in range(N_DEV)] + [
            fixed(D_MODEL, IN_COLS), tile(D_MODEL), tile(D_MODEL), fixed(1, D_MODEL),
            tile(LANES), tile(LANES), tile(LANES)],
        out_specs=(tile(D_MODEL), tile(IN_COLS), fixed(1, D_MODEL)),
        compiler_params=_params(("arbitrary",)),
    )(*d_groups, w_g, x, dx2, mix_w, rc, rsa, rsb)


def _in_proj_bwd_weights(hn_t, dproj_b):
    def body(hnt_ref, dp_ref, dwin_ref):
        dwin_ref[0] = _mm(hnt_ref[...], dp_ref[...]).astype(BF16)

    return pl.pallas_call(
        body, name="in_proj_bwd_weights", grid=(N_DEV,),
        out_shape=jax.ShapeDtypeStruct((N_DEV, D_MODEL, COL_BLOCK), BF16),
        in_specs=[pl.BlockSpec((D_MODEL, SEQ), lambda j: (0, 0)), pl.BlockSpec((SEQ, COL_BLOCK), lambda j: (0, j))],
        out_specs=pl.BlockSpec((1, D_MODEL, COL_BLOCK), lambda j: (j, 0, 0)),
        compiler_params=_params(("parallel",)),
    )(hn_t, dproj_b)


def _adamw(w, g, m, v):
    m = ADAM_B1 * m + (1.0 - ADAM_B1) * g
    v = ADAM_B2 * v + (1.0 - ADAM_B2) * (g * g)
    m_hat = m / (1.0 - ADAM_B1 ** ADAM_STEP)
    v_hat = v / (1.0 - ADAM_B2 ** ADAM_STEP)
    delta = -ADAM_LR * (m_hat / (jnp.sqrt(v_hat) + ADAM_EPS) + ADAM_WD * w)
    return delta, m, v


def _exchange_update(dwin_p, dwout_p, small_p, w_in, m_in, v_in, w_out, m_out, v_out, w_s, m_s, v_s):
    rb = 128
    n_chips = N_DEV // 2
    S1_IN, S1_OUT, SMALL, S2_IN, S2_OUT = 0, 4, 8, 15, 18

    def body(dwin_hbm, dwout_hbm, small_ref, win_ref, min_ref, vin_ref, wout_ref, mout_ref, vout_ref,
             ws_ref, ms_ref, vs_ref,
             gin_ref, din_ref, nmin_ref, nvin_ref, gout_ref, dout_ref, nmout_ref, nvout_ref,
             gs_ref, ds_ref, nms_ref, nvs_ref,
             own_in, own_out, s1_in, s1_out, fwd_in, fwd_out, s2_in, s2_out, land_s,
             send_sems, recv_sems, local_sems):
        me = _my_place()
        x, y, c = me
        my_chip = 2 * x + y
        sibling = (x, y, 1 - c)

        def remote(slot, src, dst, to):
            return pltpu.make_async_remote_copy(src_ref=src, dst_ref=dst, send_sem=send_sems.at[slot],
                                                recv_sem=recv_sems.at[slot], device_id=to, device_id_type=MESH)

        def stage1(q):
            return [remote(S1_IN + q, dwin_hbm.at[q, 1 - c], s1_in.at[q], sibling),
                    remote(S1_OUT + q, dwout_hbm.at[q, 1 - c], s1_out.at[q], sibling)]

        def stage2(rel):
            peer = _peer(me, 2 * rel)
            return [remote(S2_IN + rel - 1, fwd_in.at[rel - 1], s2_in.at[rel - 1], peer),
                    remote(S2_OUT + rel - 1, fwd_out.at[rel - 1], s2_out.at[rel - 1], peer)]

        def small_copy(rel):
            return remote(SMALL + rel - 1, small_ref, land_s.at[rel], _peer(me, rel))

        mine = [pltpu.make_async_copy(dwin_hbm.at[:, c], own_in, local_sems.at[0]),
                pltpu.make_async_copy(dwout_hbm.at[:, c], own_out, local_sems.at[1])]
        for cp in mine:
            cp.start()
        sent = []
        for q in range(n_chips):
            sent += stage1(q)
        land_s[0] = small_ref[...]
        sent += [small_copy(rel) for rel in range(1, N_DEV)]
        for cp in sent:
            cp.start()
        for cp in mine:
            cp.wait()

        def add_blocks(q, own, got, n_rows, dst):
            def step(b, carry):
                rows = pl.ds(pl.multiple_of(b * rb, rb), rb)
                dst[rows, :] = (own[q, rows, :].astype(F32) + got[q, rows, :].astype(F32)).astype(dst.dtype)
                return carry
            lax.fori_loop(0, n_rows // rb, step, 0)

        for rel in range(1, n_chips):
            q = my_chip ^ rel
            for cp in stage1(q):
                cp.wait_recv()
            add_blocks(q, own_in, s1_in, D_MODEL, fwd_in.at[rel - 1])
            add_blocks(q, own_out, s1_out, WOUT_ROWS, fwd_out.at[rel - 1])
            for cp in stage2(rel):
                cp.start()
                sent.append(cp)
        for cp in stage1(my_chip):
            cp.wait_recv()
        add_blocks(my_chip, own_in, s1_in, D_MODEL, gin_ref)
        add_blocks(my_chip, own_out, s1_out, WOUT_ROWS, gout_ref)
        for rel in range(1, n_chips):
            for cp in stage2(rel):
                cp.wait_recv()

        def update(got, w_ref, m_ref, v_ref, g_ref, d_ref, nm_ref, nv_ref, n_rows):
            def step(b, carry):
                rows = pl.ds(pl.multiple_of(b * rb, rb), rb)
                g = g_ref[rows, :]
                for rel in range(1, n_chips):
                    g = g + got[rel - 1, rows, :].astype(F32)
                delta, nm, nv = _adamw(w_ref[rows, :], g, m_ref[rows, :], v_ref[rows, :])
                g_ref[rows, :] = g
                d_ref[rows, :] = delta
                nm_ref[rows, :] = nm
                nv_ref[rows, :] = nv
                return carry
            lax.fori_loop(0, n_rows // rb, step, 0)

        update(s2_in, win_ref, min_ref, vin_ref, gin_ref, din_ref, nmin_ref, nvin_ref, D_MODEL)
        update(s2_out, wout_ref, mout_ref, vout_ref, gout_ref, dout_ref, nmout_ref, nvout_ref, WOUT_ROWS)

        for rel in range(1, N_DEV):
            small_copy(rel).wait_recv()
        my_flat = _flat(me)
        g = land_s[my_flat ^ 0]
        for dev in range(1, N_DEV):
            g = g + land_s[my_flat ^ dev]
        delta, nm, nv = _adamw(ws_ref[...], g, ms_ref[...], vs_ref[...])
        gs_ref[...] = g
        ds_ref[...] = delta
        nms_ref[...] = nm
        nvs_ref[...] = nv
        for cp in sent:
            cp.wait_send()

    vm = lambda: pl.BlockSpec(memory_space=pltpu.VMEM)
    anyspace = lambda: pl.BlockSpec(memory_space=pl.ANY)
    big = jax.ShapeDtypeStruct((D_MODEL, COL_BLOCK), F32)
    flat = jax.ShapeDtypeStruct((WOUT_ROWS, D_MODEL), F32)
    small = jax.ShapeDtypeStruct((SMALL_ROWS, LANES), F32)
    in_blocks = lambda n: pltpu.VMEM((n, D_MODEL, COL_BLOCK), BF16)
    out_blocks = lambda n: pltpu.VMEM((n, WOUT_ROWS, D_MODEL), BF16)
    return pl.pallas_call(
        body, name="exchange_update",
        out_shape=tuple([big] * 4 + [flat] * 4 + [small] * 4),
        in_specs=[anyspace(), anyspace()] + [vm() for _ in range(10)],
        out_specs=tuple(vm() for _ in range(12)),
        scratch_shapes=[in_blocks(n_chips), out_blocks(n_chips), in_blocks(n_chips), out_blocks(n_chips),
                        in_blocks(n_chips - 1), out_blocks(n_chips - 1),
                        in_blocks(n_chips - 1), out_blocks(n_chips - 1),
                        pltpu.VMEM((N_DEV, SMALL_ROWS, LANES), F32),
                        pltpu.SemaphoreType.DMA((21,)), pltpu.SemaphoreType.DMA((21,)),
                        pltpu.SemaphoreType.DMA((2,))],
        compiler_params=_params(),
    )(dwin_p.reshape(n_chips, 2, D_MODEL, COL_BLOCK), dwout_p.reshape(n_chips, 2, WOUT_ROWS, D_MODEL),
      small_p, w_in, m_in, v_in, w_out, m_out, v_out, w_s, m_s, v_s)


def _pack_small(mix, attn, hgrn, lb, final, loss=None):
    def rows8(a):
        a = a.reshape(-1, LANES)
        return jnp.pad(a, ((0, 8 - a.shape[0]), (0, 0)))
    last = jnp.zeros((8, LANES), F32) if loss is None else jnp.pad(loss.reshape(1, 1), ((0, 7), (0, LANES - 1)))
    return jnp.concatenate([rows8(mix), rows8(attn), rows8(hgrn), rows8(lb), rows8(final), last], axis=0)


def _unpack_small(slab):
    return (slab[ROW_MIX:ROW_MIX + 8].reshape(1, D_MODEL), slab[ROW_ATTN:ROW_ATTN + 4].reshape(1, ATTN_WIDTH),
            slab[ROW_HGRN:ROW_HGRN + 4].reshape(1, HGRN_WIDTH), slab[ROW_LB:ROW_LB + 8].reshape(2, HGRN_WIDTH),
            slab[ROW_FINAL:ROW_FINAL + 8].reshape(D_MODEL))


def _rope(pos_col):
    inv = ROPE_THETA ** (-jnp.arange(ROPE_HALF, dtype=F32) * (2.0 / ROPE_DIMS))
    lane_e = jnp.arange(LANES) % HEAD_DIM
    inv_lanes = jnp.where(lane_e < ROPE_DIMS, inv[lane_e % ROPE_HALF], 0.0).reshape(1, LANES)
    return _rope_tables(pos_col, inv_lanes)


def _local_step(x, proj, hn_t, w_in_g, w_out_g, tables, mix_w, attn_w, hgrn_w, lb_raw, final_w, target):
    rc, rsa, rsb = tables
    attn_o, lse = _attn_fwd_fused(proj)
    rec, states = _hgrn_fwd(proj, lb_raw)

    (dx2, d_o, delta, d_ag, d_rec, d_hg, dwout_p, d_final, d_attn_w, d_hgrn_w, loss) = _mid(
        attn_o, rec, proj, x, target, w_out_g, attn_w, hgrn_w, final_w.reshape(1, D_MODEL))

    dqkv = _attn_bwd_fused(proj, d_o, lse, delta)
    d_hq, d_hf, d_hi, d_lb = _hgrn_bwd(proj, lb_raw, d_rec, states)

    grad_x, dproj_b, d_mix = _in_proj_bwd_rows(
        (dqkv[0], dqkv[1], dqkv[2], d_ag, d_hq, d_hf, d_hi, d_hg), w_in_g, x, dx2, mix_w, rc, rsa, rsb)
    dwin_p = _in_proj_bwd_weights(hn_t, dproj_b)
    small_p = _pack_small(d_mix, d_attn_w, d_hgrn_w, d_lb, d_final, loss)
    return grad_x, dwin_p, dwout_p, small_p


def kernel(x, positions, w_in, w_out, mix_norm_w, attn_out_norm_w, hgrn_out_norm_w, hgrn_lb_raw, final_norm_w, loss_target, m_w_in, m_w_out, m_mix_norm_w, m_attn_out_norm_w, m_hgrn_out_norm_w, m_hgrn_lb_raw, m_final_norm_w, v_w_in, v_w_out, v_mix_norm_w, v_attn_out_norm_w, v_hgrn_out_norm_w, v_hgrn_lb_raw, v_final_norm_w):
    tables = _rope(positions.reshape(SEQ, 1))
    proj, hn_t, w_in_g, w_out_g = _gather_project(x[0], mix_norm_w, w_in[0], w_out[0], *tables)
    grad_x, dwin_p, dwout_p, small_p = _local_step(
        x[0], proj, hn_t, w_in_g, w_out_g, tables, mix_norm_w, attn_out_norm_w, hgrn_out_norm_w,
        hgrn_lb_raw, final_norm_w, loss_target[0])

    w_s = _pack_small(mix_norm_w, attn_out_norm_w, hgrn_out_norm_w, hgrn_lb_raw, final_norm_w)
    m_s = _pack_small(m_mix_norm_w, m_attn_out_norm_w, m_hgrn_out_norm_w, m_hgrn_lb_raw, m_final_norm_w)
    v_s = _pack_small(v_mix_norm_w, v_attn_out_norm_w, v_hgrn_out_norm_w, v_hgrn_lb_raw, v_final_norm_w)
    (g_in, d_in, nm_in, nv_in, g_out, d_out, nm_out, nv_out, g_s, d_s, nm_s, nv_s) = _exchange_update(
        dwin_p, dwout_p, small_p, w_in[0], m_w_in[0], v_w_in[0], w_out[0], m_w_out[0], v_w_out[0], w_s, m_s, v_s)

    loss = g_s[ROW_LOSS, 0]
    return (loss, grad_x[None], g_in[None], g_out[None], *_unpack_small(g_s),
            d_in[None], d_out[None], *_unpack_small(d_s),
            nm_in[None], nm_out[None], *_unpack_small(nm_s),
            nv_in[None], nv_out[None], *_unpack_small(nv_s))
```

```python
import functools

import jax
import jax.numpy as jnp
import numpy as np
from jax import lax
from jax.experimental import pallas as pl
from jax.experimental.pallas import tpu as pltpu

F32 = jnp.float32
BF16 = jnp.bfloat16

SEQ = 4096
D_MODEL = 1024
ATTN_WIDTH = 512
HGRN_WIDTH = 512
HEAD_DIM = 64
HGRN_HEADS = 4
HGRN_DIM = 128
HGRN_CHUNK = 64
N_CHUNKS = SEQ // HGRN_CHUNK
IN_COLS = 4096
COL_BLOCK = 512
N_DEV = 8
WOUT_ROWS = D_MODEL // N_DEV
ATTN_BLOCK = 128
DILATIONS = (1, 4, 16)
ROPE_THETA = 500000.0
ROPE_DIMS = 16
ROPE_HALF = 8
NORM_EPS = 1e-6
NEG_BIG = -1e30
LANES = 128

ADAM_LR = 0.001
ADAM_B1 = 0.9
ADAM_B2 = 0.999
ADAM_EPS = 1e-08
ADAM_WD = 0.01
ADAM_STEP = 10

SMALL_ROWS = 48
ROW_MIX, ROW_ATTN, ROW_HGRN, ROW_LB, ROW_FINAL, ROW_LOSS = 0, 8, 16, 24, 32, 40

VMEM_LIMIT = 56 * 1024 * 1024
MESH = pl.DeviceIdType.MESH


def _mm(a, b):
    return lax.dot_general(a, b, (((1,), (0,)), ((), ())), preferred_element_type=F32)


def _mm_nt(a, b):
    return lax.dot_general(a, b, (((1,), (1,)), ((), ())), preferred_element_type=F32)


def _mm_tn(a, b):
    return lax.dot_general(a, b, (((0,), (0,)), ((), ())), preferred_element_type=F32)


def _mm_exact(a, b):
    return lax.dot_general(a, b, (((1,), (0,)), ((), ())), preferred_element_type=F32,
                           precision=lax.Precision.HIGHEST)


def _sigmoid(v):
    return 1.0 / (1.0 + jnp.exp(-v))


def _params(sem=None, **kw):
    return pltpu.CompilerParams(dimension_semantics=sem, vmem_limit_bytes=VMEM_LIMIT, **kw)


def _my_place():
    return lax.axis_index("x"), lax.axis_index("y"), lax.axis_index("c")


def _peer(place, rel):
    x, y, c = place
    return (x ^ ((rel >> 2) & 1), y ^ ((rel >> 1) & 1), c ^ (rel & 1))


def _flat(place):
    x, y, c = place
    return 4 * x + 2 * y + c


def _rope_tables(pos_col, inv_freq_lanes):
    tm = 512

    def body(pos_ref, invf_ref, c_ref, sa_ref, sb_ref):
        ang = pos_ref[...].astype(F32) * invf_ref[...]
        e = lax.broadcasted_iota(jnp.int32, (tm, LANES), 1) & (HEAD_DIM - 1)
        cos, sin = jnp.cos(ang), jnp.sin(ang)
        c_ref[...] = jnp.where(e < ROPE_DIMS, cos, 1.0)
        sa_ref[...] = jnp.where((e >= ROPE_HALF) & (e < ROPE_DIMS), sin, 0.0)
        sb_ref[...] = jnp.where(e < ROPE_HALF, -sin, 0.0)

    tab = jax.ShapeDtypeStruct((SEQ, LANES), F32)
    spec = pl.BlockSpec((tm, LANES), lambda i: (i, 0))
    return pl.pallas_call(
        body, name="rope_tables", grid=(SEQ // tm,), out_shape=(tab, tab, tab),
        in_specs=[pl.BlockSpec((tm, 1), lambda i: (i, 0)), pl.BlockSpec((1, LANES), lambda i: (0, 0))],
        out_specs=(spec, spec, spec), compiler_params=_params(("parallel",)),
    )(pos_col, inv_freq_lanes)


def _per_slab(fn, t):
    return jnp.concatenate([fn(t[:, LANES * s:LANES * (s + 1)]) for s in range(t.shape[1] // LANES)], axis=1)


def _rot(t, c, sa, sb):
    return _per_slab(lambda u: u * c + pltpu.roll(u, ROPE_HALF, 1) * sa + pltpu.roll(u, LANES - ROPE_HALF, 1) * sb, t)


def _rot_transposed(g, c, sa, sb):
    return _per_slab(
        lambda u: u * c + pltpu.roll(u * sa, LANES - ROPE_HALF, 1) + pltpu.roll(u * sb, ROPE_HALF, 1), g)


def _gather_project(x, mix_w, w_in, w_out, rc, rsa, rsb):
    tm = 1024
    n_tiles = SEQ // tm
    arrival_of_step = (None, 0, 1, 2, 4, 5, 3, 6)

    def body(order_ref, x_ref, w_ref, win_ref, wout_ref, c_ref, sa_ref, sb_ref,
             proj_ref, hnt_ref, gin_hbm, gout_hbm,
             hn_s, w_land, wout_land, stage, send_sems, recv_sems, local_sems):
        g, i = pl.program_id(0), pl.program_id(1)
        me = _my_place()
        x_, y_, c_ = me
        sibling = (x_, y_, 1 - c_)
        chips = [(1 - x_, y_), (x_, 1 - y_), (1 - x_, 1 - y_)]

        def slab(which, place):
            idx = _flat(place)
            if which == 0:
                return w_land.at[idx]
            return wout_land.at[pl.ds(pl.multiple_of(idx * WOUT_ROWS, WOUT_ROWS), WOUT_ROWS), :]

        def copy(which, k, block, to, src=None):
            ref = slab(which, block)
            return pltpu.make_async_remote_copy(
                src_ref=ref if src is None else src, dst_ref=ref, send_sem=send_sems.at[7 * which + k],
                recv_sem=recv_sems.at[7 * which + k], device_id=to, device_id_type=MESH)

        def first_copies(which):
            src = stage if which == 0 else None
            return ([copy(which, 0, me, sibling, src)]
                    + [copy(which, 1 + j, me, (*chip, c_), src) for j, chip in enumerate(chips)])

        def pass_on(which, j):
            return copy(which, 4 + j, (*chips[j], c_), sibling)

        def arrival(which, k):
            if k == 0:
                return copy(which, 0, sibling, me)
            if k <= 3:
                return copy(which, k, (*chips[k - 1], c_), me)
            return copy(which, k, (*chips[k - 4], 1 - c_), me)

        def to_hbm(step):
            idx = order_ref[step]
            cols = pl.ds(pl.multiple_of(idx * COL_BLOCK, COL_BLOCK), COL_BLOCK)
            return pltpu.make_async_copy(w_land.at[idx], gin_hbm.at[:, cols], local_sems.at[step])

        @pl.when((g == 0) & (i == 0))
        def _():
            stage[...] = win_ref[...].astype(BF16)
            w_land[_flat(me)] = stage[...]
            wout_land[pl.ds(pl.multiple_of(_flat(me) * WOUT_ROWS, WOUT_ROWS), WOUT_ROWS), :] = (
                wout_ref[...].astype(BF16))
            for which in (0, 1):
                for cp in first_copies(which):
                    cp.start()
            to_hbm(0).start()

        for step, k in enumerate(arrival_of_step):
            if k is None:
                continue

            @pl.when((g == step) & (i == 0))
            def _(k=k, step=step):
                arrival(0, k).wait_recv()
                to_hbm(step).start()
                if 1 <= k <= 3:
                    arrival(1, k).wait_recv()
                    pass_on(0, k - 1).start()
                    pass_on(1, k - 1).start()

        rows = pl.ds(pl.multiple_of(i * tm, tm), tm)

        @pl.when(g == 0)
        def _():
            xf = x_ref[...]
            ms = jnp.mean(xf * xf, axis=-1, keepdims=True)
            hn = xf * lax.rsqrt(ms + NORM_EPS) * w_ref[...]
            hnt_ref[...] = hn.T.astype(BF16)
            hn_s[rows, :] = hn.astype(BF16)

        group = order_ref[g]

        @pl.when(group < 2)
        def _():
            proj_ref[...] = _rot(_mm(hn_s[rows, :], w_land[group]), c_ref[...], sa_ref[...], sb_ref[...])

        @pl.when(group >= 2)
        def _():
            proj_ref[...] = _mm(hn_s[rows, :], w_land[group])

        @pl.when((g == N_DEV - 1) & (i == n_tiles - 1))
        def _():
            for k in (0, 4, 5, 6):
                arrival(1, k).wait_recv()
            for which in (0, 1):
                for cp in first_copies(which) + [pass_on(which, j) for j in range(3)]:
                    cp.wait_send()
            wout_copy = pltpu.make_async_copy(wout_land, gout_hbm, local_sems.at[N_DEV])
            wout_copy.start()
            for step in range(N_DEV):
                to_hbm(step).wait()
            wout_copy.wait()

    me = _my_place()
    x_, y_, c_ = me
    chips = [(1 - x_, y_), (x_, 1 - y_), (1 - x_, 1 - y_)]
    order = jnp.stack([_flat(p) for p in (
        me, (x_, y_, 1 - c_), (*chips[0], c_), (*chips[1], c_), (*chips[0], 1 - c_), (*chips[1], 1 - c_),
        (*chips[2], c_), (*chips[2], 1 - c_))]).astype(jnp.int32)

    first_sweep = lambda g, i, order: (jnp.where(g == 0, i, n_tiles - 1), 0)
    tab = pl.BlockSpec((tm, LANES), lambda g, i, order: (jnp.where(order[g] < 2, i, 0), 0))
    whole = lambda: pl.BlockSpec(memory_space=pltpu.VMEM)
    grid_spec = pltpu.PrefetchScalarGridSpec(
        num_scalar_prefetch=1, grid=(N_DEV, n_tiles),
        in_specs=[pl.BlockSpec((tm, D_MODEL), first_sweep),
                  pl.BlockSpec((1, D_MODEL), lambda g, i, order: (0, 0)),
                  whole(), whole(), tab, tab, tab],
        out_specs=(pl.BlockSpec((tm, COL_BLOCK), lambda g, i, order: (i, order[g])),
                   pl.BlockSpec((D_MODEL, tm), lambda g, i, order: (0, jnp.where(g == 0, i, n_tiles - 1))),
                   pl.BlockSpec(memory_space=pl.ANY), pl.BlockSpec(memory_space=pl.ANY)),
        scratch_shapes=[pltpu.VMEM((SEQ, D_MODEL), BF16),
                        pltpu.VMEM((N_DEV, D_MODEL, COL_BLOCK), BF16),
                        pltpu.VMEM((D_MODEL, D_MODEL), BF16),
                        pltpu.VMEM((D_MODEL, COL_BLOCK), BF16),
                        pltpu.SemaphoreType.DMA((14,)), pltpu.SemaphoreType.DMA((14,)),
                        pltpu.SemaphoreType.DMA((N_DEV + 1,))])
    return pl.pallas_call(
        body, name="gather_project", grid_spec=grid_spec,
        out_shape=(jax.ShapeDtypeStruct((SEQ, IN_COLS), F32), jax.ShapeDtypeStruct((D_MODEL, SEQ), BF16),
                   jax.ShapeDtypeStruct((D_MODEL, IN_COLS), BF16), jax.ShapeDtypeStruct((D_MODEL, D_MODEL), BF16)),
        compiler_params=_params(("arbitrary", "arbitrary")),
    )(order, x, mix_w, w_in, w_out, rc, rsa, rsb)


ATTN_GROUP = 8
BLOCKS_PER_PATTERN = SEQ // ATTN_BLOCK


def _write_band_bias(bias_ref):
    qi = lax.broadcasted_iota(jnp.int32, (2 * ATTN_BLOCK, 2 * ATTN_BLOCK), 0) & (ATTN_BLOCK - 1)
    kj = lax.broadcasted_iota(jnp.int32, (2 * ATTN_BLOCK, 2 * ATTN_BLOCK), 1)
    bias_ref[0] = jnp.where((kj >= qi) & (kj <= qi + ATTN_BLOCK), 0.0, NEG_BIG)
    bias_ref[1] = jnp.where(kj <= qi, 0.0, NEG_BIG)


def _head0_lanes():
    return lax.broadcasted_iota(jnp.int32, (ATTN_BLOCK, LANES), 1) < HEAD_DIM


def _stack_heads(t, h0):
    return jnp.concatenate([jnp.where(h0, t, 0.0), jnp.where(h0, 0.0, t)], axis=0).astype(BF16)


def _strided(start, size, d):
    return pl.ds(start, size) if d == 1 else pl.ds(start, size, stride=d)


def _block_place(i, d):
    nblk = BLOCKS_PER_PATTERN // d
    r, n = i // nblk, i % nblk
    kn = jnp.maximum(n - 1, 0)
    row0, key0 = n * (d * ATTN_BLOCK) + r, kn * (d * ATTN_BLOCK) + r
    if d == 1:
        row0, key0 = pl.multiple_of(row0, ATTN_BLOCK), pl.multiple_of(key0, ATTN_BLOCK)
    return _strided(row0, ATTN_BLOCK, d), _strided(key0, 2 * ATTN_BLOCK, d), (n == 0).astype(jnp.int32)


def _for_each_group(d, load, compute, store):
    def group(g, carry):
        items = [load(*_block_place(g * ATTN_GROUP + u, d)) for u in range(ATTN_GROUP)]
        results = [compute(item) for item in items]
        for item, res in zip(items, results):
            store(item, res)
        return carry

    lax.fori_loop(0, BLOCKS_PER_PATTERN // ATTN_GROUP, group, 0)


def _attn_fwd_fused(proj):
    n_pat = len(DILATIONS)
    tile2 = (2 * ATTN_BLOCK, LANES)

    def body(q_ref, k_ref, v_ref, o_ref, lse_ref, m_acc, l_acc, bias_ref):
        _write_band_bias(bias_ref)
        h0 = _head0_lanes()
        for pi, d in enumerate(DILATIONS):
            first, last = pi == 0, pi == n_pat - 1

            def load(rows, keys, which, first=first):
                item = dict(rows=rows, keys=keys, which=which)
                if not first:
                    item.update(o=o_ref[rows, :], m=[m_acc.at[h][rows, :] for h in range(2)],
                                l=[l_acc.at[h][rows, :] for h in range(2)])
                return item

            def compute(item, first=first):
                kb = k_ref[item["keys"], :].astype(BF16)
                vb = v_ref[item["keys"], :].astype(BF16)
                s = _mm_nt(_stack_heads(q_ref[item["rows"], :], h0), kb) * 0.125 + bias_ref[item["which"]]
                mb = jnp.max(s, axis=-1, keepdims=True)
                if first:
                    p = jnp.exp(s - mb)
                    mn = jnp.broadcast_to(mb, tile2)
                else:
                    m_old = jnp.concatenate(item["m"], axis=0)
                    mn = jnp.maximum(m_old, mb)
                    alpha = jnp.exp(m_old - mn)
                    p = jnp.exp(s - jnp.concatenate([mn, mn], axis=1))
                ls = jnp.sum(p, axis=-1, keepdims=True)
                pv = _mm(p.astype(BF16), vb)
                if first:
                    return pv, mn, jnp.broadcast_to(ls, tile2)
                o_old = jnp.concatenate([item["o"], item["o"]], axis=0)
                return alpha * o_old + pv, mn, alpha * jnp.concatenate(item["l"], axis=0) + ls

            def store(item, res, last=last):
                rows = item["rows"]
                (o0, o1), (m0, m1), (l0, l1) = ((a[:ATTN_BLOCK], a[ATTN_BLOCK:]) for a in res)
                if last:
                    o_ref[rows, :] = jnp.where(h0, o0 / l0, o1 / l1)
                    lse_ref[rows, :] = jnp.where(h0, m0 + jnp.log(l0), m1 + jnp.log(l1))
                else:
                    o_ref[rows, :] = jnp.where(h0, o0, o1)
                    m_acc.at[0][rows, :], m_acc.at[1][rows, :] = m0, m1
                    l_acc.at[0][rows, :], l_acc.at[1][rows, :] = l0, l1

            _for_each_group(d, load, compute, store)

    slab = lambda g: pl.BlockSpec((SEQ, LANES), functools.partial(lambda hp, g: (0, 4 * g + hp), g=g))
    wide = jax.ShapeDtypeStruct((SEQ, ATTN_WIDTH), F32)
    return pl.pallas_call(
        body, name="attn_fwd", grid=(4,), out_shape=(wide, wide),
        in_specs=[slab(0), slab(1), slab(2)], out_specs=(slab(0), slab(0)),
        scratch_shapes=[pltpu.VMEM((2, SEQ, LANES), F32), pltpu.VMEM((2, SEQ, LANES), F32),
                        pltpu.VMEM((2, 2 * ATTN_BLOCK, 2 * ATTN_BLOCK), F32)],
        compiler_params=_params(("parallel",)),
    )(proj, proj, proj)


def _attn_bwd_fused(proj, d_out, lse, delta):
    def body(q_ref, k_ref, v_ref, do_ref, lse_ref, del_ref, dq_ref, dk_ref, dv_ref, bias_ref):
        _write_band_bias(bias_ref)
        dk_ref[...] = jnp.zeros_like(dk_ref)
        dv_ref[...] = jnp.zeros_like(dv_ref)
        h0 = _head0_lanes()
        for pi, d in enumerate(DILATIONS):
            first = pi == 0

            def load(rows, keys, which):
                return dict(rows=rows, keys=keys, q=q_ref[rows, :], g=do_ref[rows, :], lse=lse_ref[rows, :],
                            delta=del_ref[rows, :], k=k_ref[keys, :].astype(BF16),
                            v=v_ref[keys, :].astype(BF16), bias=bias_ref[which])

            def per_head(t):
                swapped = pltpu.roll(t, HEAD_DIM, 1)
                both = jnp.concatenate([jnp.where(h0, t, swapped), jnp.where(h0, swapped, t)], axis=0)
                return jnp.concatenate([both, both], axis=1)

            def compute(item):
                q2, g2 = _stack_heads(item["q"], h0), _stack_heads(item["g"], h0)
                s = _mm_nt(q2, item["k"]) * 0.125 + item["bias"]
                p = jnp.exp(s - per_head(item["lse"]))
                dp = _mm_nt(g2, item["v"])
                ds = (p * (dp - per_head(item["delta"])) * 0.125).astype(BF16)
                dq2 = _mm(ds, item["k"])
                dq = jnp.where(h0, dq2[:ATTN_BLOCK], dq2[ATTN_BLOCK:])
                return dq, _mm_tn(ds, q2), _mm_tn(p.astype(BF16), g2)

            def store(item, res, first=first):
                rows, keys = item["rows"], item["keys"]
                if first:
                    dq_ref[rows, :] = res[0]
                else:
                    dq_ref[rows, :] += res[0]
                dk_ref[keys, :] += res[1]
                dv_ref[keys, :] += res[2]

            _for_each_group(d, load, compute, store)

    slab = lambda g: pl.BlockSpec((SEQ, LANES), functools.partial(lambda hp, g: (0, 4 * g + hp), g=g))
    wide = jax.ShapeDtypeStruct((SEQ, ATTN_WIDTH), F32)
    return pl.pallas_call(
        body, name="attn_bwd", grid=(4,), out_shape=(wide, wide, wide),
        scratch_shapes=[pltpu.VMEM((2, 2 * ATTN_BLOCK, 2 * ATTN_BLOCK), F32)],
        in_specs=[slab(0), slab(1), slab(2), slab(0), slab(0), slab(0)], out_specs=(slab(0), slab(0), slab(0)),
        compiler_params=_params(("parallel",)),
    )(proj, proj, proj, d_out, lse, delta)


def _hgrn_lower_bound(lb_ref):
    r0, r1 = lb_ref[0:1, :], lb_ref[1:2, :]
    mx = jnp.maximum(r0, r1)
    e0, e1 = jnp.exp(r0 - mx), jnp.exp(r1 - mx)
    return e0 / (e0 + e1)


def _hgrn_gates(hq, hf, lb):
    sq = _sigmoid(hq)
    sg = _sigmoid(hf)
    f = lb + (1.0 - lb) * sg
    return hq * sq, sq, sg, f, 1.0 - f, jnp.log(f)


HGRN_PAIR = 4
HGRN_SEQ_BLOCK = 1024
HGRN_GROUP = 4
HGRN_ROWS = HGRN_GROUP * HGRN_CHUNK


def _hgrn_specs(reverse):
    n_blocks = SEQ // HGRN_SEQ_BLOCK
    width = HGRN_PAIR * HGRN_DIM
    blk = (lambda s: n_blocks - 1 - s) if reverse else (lambda s: s)
    cols = lambda g: pl.BlockSpec((HGRN_SEQ_BLOCK, width),
                                  functools.partial(lambda p, s, g: (blk(s), (HGRN_HEADS // HGRN_PAIR) * g + p), g=g))
    pair = pl.BlockSpec((HGRN_SEQ_BLOCK, width), lambda p, s: (blk(s), p))
    lb = pl.BlockSpec((2, width), lambda p, s: (0, p))
    states = pl.BlockSpec((HGRN_PAIR, HGRN_SEQ_BLOCK // HGRN_CHUNK, HGRN_DIM, HGRN_DIM),
                          lambda p, s: (p, blk(s), 0, 0))
    return cols, pair, lb, states


def _chunk_masks():
    ri = lax.broadcasted_iota(jnp.int32, (HGRN_ROWS, HGRN_ROWS), 0)
    ci = lax.broadcasted_iota(jnp.int32, (HGRN_ROWS, HGRN_ROWS), 1)
    same = (ri // HGRN_CHUNK) == (ci // HGRN_CHUNK)
    return same, same & (ri >= ci), same & (ri <= ci)


def _mm_select(sel, v):
    hi = v.astype(BF16)
    r1 = v - hi.astype(F32)
    mid = r1.astype(BF16)
    lo = (r1 - mid.astype(F32)).astype(BF16)
    return _mm(sel, hi) + _mm(sel, mid) + _mm(sel, lo)


def _head_cols(a, h):
    return a[:, HGRN_DIM * h:HGRN_DIM * (h + 1)]


def _hgrn_fwd(proj, lb_raw):
    t, rws = HGRN_CHUNK, HGRN_ROWS

    def body(hq_ref, hf_ref, hi_ref, lb_ref, rec_ref, st_ref, state):
        @pl.when(pl.program_id(1) == 0)
        def _():
            state[...] = jnp.zeros_like(state)

        lb = _hgrn_lower_bound(lb_ref)
        same, causal, _ = _chunk_masks()
        sel = jnp.concatenate([causal, same], axis=0).astype(BF16)

        def group(g, sts):
            rows = pl.ds(pl.multiple_of(g * rws, rws), rws)
            q, _, _, _, k, lf = _hgrn_gates(hq_ref[rows, :], hf_ref[rows, :], lb)
            sums = _mm_select(sel, lf)
            cum, last = sums[:rws], sums[rws:]
            qd = (q * jnp.exp(cum)).astype(BF16)
            ki = (k * jnp.exp(-cum)).astype(BF16)
            ke = (k * jnp.exp(last - cum)).astype(BF16)
            vb = hi_ref[rows, :].astype(BF16)
            dec = jnp.exp(last)
            new_sts, recs = [], []
            for h in range(HGRN_PAIR):
                qd_h, ke_h, vb_h = _head_cols(qd, h), _head_cols(ke, h), _head_cols(vb, h)
                att = jnp.where(causal, _mm_nt(qd_h, _head_cols(ki, h)), 0.0).astype(BF16)
                intra = _mm(att, vb_h)
                st = sts[h]
                outs = []
                for c in range(HGRN_GROUP):
                    sl = slice(c * t, (c + 1) * t)
                    st_ref[h, g * HGRN_GROUP + c] = st
                    outs.append(intra[sl] + _mm_nt(qd_h[sl], st.astype(BF16)))
                    st = st * _head_cols(dec[c * t:c * t + 1, :], h) + _mm_tn(vb_h[sl], ke_h[sl])
                new_sts.append(st)
                recs.append(jnp.concatenate(outs, axis=0))
            rec_ref[rows, :] = jnp.concatenate(recs, axis=1)
            return tuple(new_sts)

        sts = lax.fori_loop(0, HGRN_SEQ_BLOCK // rws, group, tuple(state[h] for h in range(HGRN_PAIR)))
        for h in range(HGRN_PAIR):
            state[h] = sts[h]

    cols, pair, lb, states = _hgrn_specs(reverse=False)
    return pl.pallas_call(
        body, name="hgrn_fwd", grid=(HGRN_HEADS // HGRN_PAIR, SEQ // HGRN_SEQ_BLOCK),
        out_shape=(jax.ShapeDtypeStruct((SEQ, HGRN_WIDTH), F32),
                   jax.ShapeDtypeStruct((HGRN_HEADS, N_CHUNKS, HGRN_DIM, HGRN_DIM), F32)),
        in_specs=[cols(4), cols(5), cols(6), lb], out_specs=(pair, states),
        scratch_shapes=[pltpu.VMEM((HGRN_PAIR, HGRN_DIM, HGRN_DIM), F32)],
        compiler_params=_params(("parallel", "arbitrary")),
    )(proj, proj, proj, lb_raw)


def _hgrn_bwd(proj, lb_raw, d_rec, states):
    t, rws = HGRN_CHUNK, HGRN_ROWS

    def body(hq_ref, hf_ref, hi_ref, lb_ref, do_ref, st_ref, dhq_ref, dhf_ref, dhi_ref, dlb_ref,
             dstate, dlb_acc):
        lb = _hgrn_lower_bound(lb_ref)
        same, causal, anti = _chunk_masks()
        sel = jnp.concatenate([causal, same], axis=0).astype(BF16)
        sel_t = jnp.concatenate([anti, same], axis=1).astype(BF16)
        @pl.when(pl.program_id(1) == 0)
        def _():
            dstate[...] = jnp.zeros_like(dstate)
            dlb_acc[...] = jnp.zeros_like(dlb_acc)

        n_groups = HGRN_SEQ_BLOCK // rws
        chunks = [slice(c * t, (c + 1) * t) for c in range(HGRN_GROUP)]

        def group(i, dsts_in):
            g = n_groups - 1 - i
            rows = pl.ds(pl.multiple_of(g * rws, rws), rws)
            hq = hq_ref[rows, :]
            q, sq, sg, f, k, lf = _hgrn_gates(hq, hf_ref[rows, :], lb)
            sums = _mm_select(sel, lf)
            cum, last = sums[:rws], sums[rws:]
            e_cum, e_inv, e_end, dec = jnp.exp(cum), jnp.exp(-cum), jnp.exp(last - cum), jnp.exp(last)
            qd, ki, ke = q * e_cum, k * e_inv, k * e_end
            qdb, kib, keb = qd.astype(BF16), ki.astype(BF16), ke.astype(BF16)
            vb = hi_ref[rows, :].astype(BF16)
            gb = do_ref[rows, :].astype(BF16)

            dsts_out, per_head = [], []
            for h in range(HGRN_PAIR):
                qdb_h, kib_h, keb_h = _head_cols(qdb, h), _head_cols(kib, h), _head_cols(keb, h)
                vb_h, gb_h = _head_cols(vb, h), _head_cols(gb, h)
                att = jnp.where(causal, _mm_nt(qdb_h, kib_h), 0.0).astype(BF16)
                datt = jnp.where(causal, _mm_nt(gb_h, vb_h), 0.0).astype(BF16)
                dv = _mm_tn(att, gb_h)
                dqd = _mm(datt, kib_h)
                dki = _mm_tn(datt, qdb_h)

                decs = [_head_cols(dec[c * t:c * t + 1, :], h) for c in range(HGRN_GROUP)]
                dsts = [None] * HGRN_GROUP
                dst = dsts_in[h]
                for c in reversed(range(HGRN_GROUP)):
                    dsts[c] = dst
                    dst = dst * decs[c] + _mm_tn(gb_h[chunks[c]], qdb_h[chunks[c]])
                dsts_out.append(dst)

                dv_x, dqd_x, dke, dlast_x = [], [], [], []
                for c, sl in enumerate(chunks):
                    st_prev = st_ref[h, g * HGRN_GROUP + c]
                    dstb = dsts[c].astype(BF16)
                    dv_x.append(_mm_nt(keb_h[sl], dstb))
                    dqd_x.append(_mm(gb_h[sl], st_prev.astype(BF16)))
                    dke.append(_mm(vb_h[sl], dstb))
                    ddec = jnp.sum(dsts[c] * st_prev, axis=0, keepdims=True)
                    dlast_x.append(jnp.broadcast_to(ddec * decs[c], (t, HGRN_DIM)))
                per_head.append((dv + jnp.concatenate(dv_x, axis=0), dqd + jnp.concatenate(dqd_x, axis=0),
                                 dki, jnp.concatenate(dke, axis=0), jnp.concatenate(dlast_x, axis=0)))
            dv, dqd, dki, dke, dlast = (jnp.concatenate(list(parts), axis=1) for parts in zip(*per_head))

            dq = dqd * e_cum
            dk = dki * e_inv + dke * e_end
            dke_ke = dke * ke
            dcum = dqd * qd - dki * ki - dke_ke
            dlf = _mm_select(sel_t, jnp.concatenate([dcum, dke_ke], axis=0)) + dlast
            df = dlf / f - dk
            dhq_ref[rows, :] = dq * (sq * (1.0 + hq * (1.0 - sq)))
            dhf_ref[rows, :] = df * (1.0 - lb) * (sg * (1.0 - sg))
            dhi_ref[rows, :] = dv
            dlb_acc[...] += jnp.sum(df * (1.0 - sg), axis=0, keepdims=True)
            return tuple(dsts_out)

        dsts = lax.fori_loop(0, n_groups, group, tuple(dstate[h] for h in range(HGRN_PAIR)))
        for h in range(HGRN_PAIR):
            dstate[h] = dsts[h]
        g0 = dlb_acc[...] * lb * (1.0 - lb)
        dlb_ref[...] = jnp.concatenate([g0, -g0], axis=0)

    cols, pair, lb_spec, st_spec = _hgrn_specs(reverse=True)
    wide = jax.ShapeDtypeStruct((SEQ, HGRN_WIDTH), F32)
    return pl.pallas_call(
        body, name="hgrn_bwd", grid=(HGRN_HEADS // HGRN_PAIR, SEQ // HGRN_SEQ_BLOCK),
        out_shape=(wide, wide, wide, jax.ShapeDtypeStruct((2, HGRN_WIDTH), F32)),
        in_specs=[cols(4), cols(5), cols(6), lb_spec, pair, st_spec],
        out_specs=(pair, pair, pair, lb_spec),
        scratch_shapes=[pltpu.VMEM((HGRN_PAIR, HGRN_DIM, HGRN_DIM), F32),
                        pltpu.VMEM((1, HGRN_PAIR * HGRN_DIM), F32)],
        compiler_params=_params(("parallel", "arbitrary")),
    )(proj, proj, proj, lb_raw, d_rec, states)


def _group_sum(v, group):
    parts = []
    for s in range(v.shape[1] // LANES):
        slab = v[:, LANES * s:LANES * (s + 1)]
        if group == LANES:
            parts.append(jnp.broadcast_to(jnp.sum(slab, axis=-1, keepdims=True), slab.shape))
        else:
            h0 = lax.broadcasted_iota(jnp.int32, slab.shape, 1) < HEAD_DIM
            s0 = jnp.sum(jnp.where(h0, slab, 0.0), axis=-1, keepdims=True)
            s1 = jnp.sum(jnp.where(h0, 0.0, slab), axis=-1, keepdims=True)
            parts.append(jnp.where(h0, s0, s1))
    return jnp.concatenate(parts, axis=1)


def _mid(attn_o, rec, proj, x, target, w_out_g, attn_w, hgrn_w, final_w):
    tm = 256

    def branch_fwd(o, gate, w, group):
        r = lax.rsqrt(_group_sum(o * o, group) * (1.0 / group) + NORM_EPS)
        nrm = o * r
        sg = _sigmoid(gate)
        return r, nrm, sg, nrm * w * (gate * sg)

    def branch_bwd(dy, r, nrm, sg, gate, w, group):
        silu = gate * sg
        d_gate = dy * nrm * w * (sg * (1.0 + gate * (1.0 - sg)))
        d_w = jnp.sum(dy * nrm * silu, axis=0, keepdims=True)
        dn = dy * w * silu
        d_o = r * (dn - nrm * (_group_sum(dn * nrm, group) * (1.0 / group)))
        return d_o, d_gate, d_w

    def body(o_ref, rec_ref, ag_ref, hg_ref, x_ref, tgt_ref, wout_ref, aw_ref, hw_ref, fw_ref,
             dx2_ref, do_ref, delta_ref, dag_ref, drec_ref, dhg_ref, dwout_ref, dfw_ref, daw_ref, dhw_ref,
             loss_ref, dwout_acc):
        i = pl.program_id(0)

        @pl.when(i == 0)
        def _():
            dwout_acc[...] = jnp.zeros_like(dwout_acc)
            dfw_ref[...] = jnp.zeros_like(dfw_ref)
            daw_ref[...] = jnp.zeros_like(daw_ref)
            dhw_ref[...] = jnp.zeros_like(dhw_ref)
            loss_ref[...] = jnp.zeros_like(loss_ref)

        o, rc, ag, hg = o_ref[...], rec_ref[...], ag_ref[...], hg_ref[...]
        aw, hw, fw = aw_ref[...], hw_ref[...], fw_ref[...]
        ra, na, sga, ya = branch_fwd(o, ag, aw, HEAD_DIM)
        rh, nh, sgh, yh = branch_fwd(rc, hg, hw, HGRN_DIM)
        mixed = jnp.concatenate([ya, yh], axis=1).astype(BF16)
        wout = wout_ref[...]
        x2 = x_ref[...] + _mm(mixed, wout)
        rstd = lax.rsqrt(jnp.mean(x2 * x2, axis=-1, keepdims=True) + NORM_EPS)
        xn = x2 * rstd
        err = xn * fw - tgt_ref[...]
        row_loss = jnp.mean(err * err, axis=-1, keepdims=True)
        loss_ref[...] += 0.5 * jnp.sum(row_loss, axis=0, keepdims=True)
        dy = err * (1.0 / D_MODEL)
        dfw_ref[...] += jnp.sum(dy * xn, axis=0, keepdims=True)
        dxn = dy * fw
        dx2 = rstd * (dxn - xn * jnp.mean(dxn * xn, axis=-1, keepdims=True))
        dx2_ref[...] = dx2
        dx2b = dx2.astype(BF16)
        dwout_acc[...] += _mm_tn(mixed, dx2b)

        @pl.when(i == pl.num_programs(0) - 1)
        def _():
            dwout_ref[...] = dwout_acc[...].astype(BF16)

        dmixed = _mm_nt(dx2b, wout)

        d_o, d_ag, d_aw = branch_bwd(dmixed[:, :ATTN_WIDTH], ra, na, sga, ag, aw, HEAD_DIM)
        d_rec, d_hg, d_hw = branch_bwd(dmixed[:, ATTN_WIDTH:], rh, nh, sgh, hg, hw, HGRN_DIM)
        do_ref[...] = d_o
        delta_ref[...] = _group_sum(d_o * o, HEAD_DIM)
        dag_ref[...] = d_ag
        drec_ref[...] = d_rec
        dhg_ref[...] = d_hg
        daw_ref[...] += d_aw
        dhw_ref[...] += d_hw

    half = lambda: pl.BlockSpec((tm, COL_BLOCK), lambda i: (i, 0))
    full = lambda: pl.BlockSpec((tm, D_MODEL), lambda i: (i, 0))
    fixed = lambda r, c: pl.BlockSpec((r, c), lambda i: (0, 0))
    wide = jax.ShapeDtypeStruct((SEQ, COL_BLOCK), F32)
    return pl.pallas_call(
        body, name="mid", grid=(SEQ // tm,),
        out_shape=(jax.ShapeDtypeStruct((SEQ, D_MODEL), F32), wide, wide, wide, wide, wide,
                   jax.ShapeDtypeStruct((D_MODEL, D_MODEL), BF16),
                   jax.ShapeDtypeStruct((1, D_MODEL), F32), jax.ShapeDtypeStruct((1, COL_BLOCK), F32),
                   jax.ShapeDtypeStruct((1, COL_BLOCK), F32), jax.ShapeDtypeStruct((1, 1), F32)),
        scratch_shapes=[pltpu.VMEM((D_MODEL, D_MODEL), F32)],
        in_specs=[half(), half(),
                  pl.BlockSpec((tm, COL_BLOCK), lambda i: (i, 3)), pl.BlockSpec((tm, COL_BLOCK), lambda i: (i, 7)),
                  full(), full(), fixed(D_MODEL, D_MODEL), fixed(1, COL_BLOCK), fixed(1, COL_BLOCK),
                  fixed(1, D_MODEL)],
        out_specs=(full(), half(), half(), half(), half(), half(), fixed(D_MODEL, D_MODEL),
                   fixed(1, D_MODEL), fixed(1, COL_BLOCK), fixed(1, COL_BLOCK), fixed(1, 1)),
        compiler_params=_params(("arbitrary",)),
    )(attn_o, rec, proj, proj, x, target, w_out_g, attn_w, hgrn_w, final_w)


def _in_proj_bwd_rows(d_groups, w_g, x, dx2, mix_w, rc, rsa, rsb):
    tm = 256

    def body(*refs):
        dg_refs = refs[:N_DEV]
        wg_ref, x_ref, dx2_ref, w_ref, c_ref, sa_ref, sb_ref, gx_ref, dpb_ref, dmw_ref = refs[N_DEV:]

        @pl.when(pl.program_id(0) == 0)
        def _():
            dmw_ref[...] = jnp.zeros_like(dmw_ref)

        parts = []
        for j in range(N_DEV):
            dp = dg_refs[j][...]
            if j < 2:
                dp = _rot_transposed(dp, c_ref[...], sa_ref[...], sb_ref[...])
            parts.append(dp.astype(BF16))
        dpb = jnp.concatenate(parts, axis=1)
        dpb_ref[...] = dpb
        g = _mm_nt(dpb, wg_ref[...])
        xf = x_ref[...]
        rstd = lax.rsqrt(jnp.mean(xf * xf, axis=-1, keepdims=True) + NORM_EPS)
        xn = xf * rstd
        dmw_ref[...] += jnp.sum(g * xn, axis=0, keepdims=True)
        gw = g * w_ref[...]
        gx_ref[...] = dx2_ref[...] + rstd * (gw - xn * jnp.mean(gw * xn, axis=-1, keepdims=True))

    tile = lambda cols: pl.BlockSpec((tm, cols), lambda i: (i, 0))
    fixed = lambda r, c: pl.BlockSpec((r, c), lambda i: (0, 0))
    return pl.pallas_call(
        body, name="in_proj_bwd_rows", grid=(SEQ // tm,),
        out_shape=(jax.ShapeDtypeStruct((SEQ, D_MODEL), F32), jax.ShapeDtypeStruct((SEQ, IN_COLS), BF16),
                   jax.ShapeDtypeStruct((1, D_MODEL), F32)),
        in_specs=[tile(COL_BLOCK) for _ in range(N_DEV)] + [
            fixed(D_MODEL, IN_COLS), tile(D_MODEL), tile(D_MODEL), fixed(1, D_MODEL),
            tile(LANES), tile(LANES), tile(LANES)],
        out_specs=(tile(D_MODEL), tile(IN_COLS), fixed(1, D_MODEL)),
        compiler_params=_params(("arbitrary",)),
    )(*d_groups, w_g, x, dx2, mix_w, rc, rsa, rsb)


def _in_proj_bwd_weights(hn_t, dproj_b):
    def body(hnt_ref, dp_ref, dwin_ref):
        dwin_ref[0] = _mm(hnt_ref[...], dp_ref[...]).astype(BF16)

    return pl.pallas_call(
        body, name="in_proj_bwd_weights", grid=(N_DEV,),
        out_shape=jax.ShapeDtypeStruct((N_DEV, D_MODEL, COL_BLOCK), BF16),
        in_specs=[pl.BlockSpec((D_MODEL, SEQ), lambda j: (0, 0)), pl.BlockSpec((SEQ, COL_BLOCK), lambda j: (0, j))],
        out_specs=pl.BlockSpec((1, D_MODEL, COL_BLOCK), lambda j: (j, 0, 0)),
        compiler_params=_params(("parallel",)),
    )(hn_t, dproj_b)


def _adamw(w, g, m, v):
    m = ADAM_B1 * m + (1.0 - ADAM_B1) * g
    v = ADAM_B2 * v + (1.0 - ADAM_B2) * (g * g)
    m_hat = m / (1.0 - ADAM_B1 ** ADAM_STEP)
    v_hat = v / (1.0 - ADAM_B2 ** ADAM_STEP)
    delta = -ADAM_LR * (m_hat / (jnp.sqrt(v_hat) + ADAM_EPS) + ADAM_WD * w)
    return delta, m, v


def _exchange_update(dwin_p, dwout_p, small_p, w_in, m_in, v_in, w_out, m_out, v_out, w_s, m_s, v_s):
    rb = 128
    n_chips = N_DEV // 2
    S1_IN, S1_OUT, SMALL, S2_IN, S2_OUT = 0, 4, 8, 15, 18

    def body(dwin_hbm, dwout_hbm, small_ref, win_ref, min_ref, vin_ref, wout_ref, mout_ref, vout_ref,
             ws_ref, ms_ref, vs_ref,
             gin_ref, din_ref, nmin_ref, nvin_ref, gout_ref, dout_ref, nmout_ref, nvout_ref,
             gs_ref, ds_ref, nms_ref, nvs_ref,
             own_in, own_out, s1_in, s1_out, fwd_in, fwd_out, s2_in, s2_out, land_s,
             send_sems, recv_sems, local_sems):
        me = _my_place()
        x, y, c = me
        my_chip = 2 * x + y
        sibling = (x, y, 1 - c)

        def remote(slot, src, dst, to):
            return pltpu.make_async_remote_copy(src_ref=src, dst_ref=dst, send_sem=send_sems.at[slot],
                                                recv_sem=recv_sems.at[slot], device_id=to, device_id_type=MESH)

        def stage1(q):
            return [remote(S1_IN + q, dwin_hbm.at[q, 1 - c], s1_in.at[q], sibling),
                    remote(S1_OUT + q, dwout_hbm.at[q, 1 - c], s1_out.at[q], sibling)]

        def stage2(rel):
            peer = _peer(me, 2 * rel)
            return [remote(S2_IN + rel - 1, fwd_in.at[rel - 1], s2_in.at[rel - 1], peer),
                    remote(S2_OUT + rel - 1, fwd_out.at[rel - 1], s2_out.at[rel - 1], peer)]

        def small_copy(rel):
            return remote(SMALL + rel - 1, small_ref, land_s.at[rel], _peer(me, rel))

        mine = [pltpu.make_async_copy(dwin_hbm.at[:, c], own_in, local_sems.at[0]),
                pltpu.make_async_copy(dwout_hbm.at[:, c], own_out, local_sems.at[1])]
        for cp in mine:
            cp.start()
        sent = []
        for q in range(n_chips):
            sent += stage1(q)
        land_s[0] = small_ref[...]
        sent += [small_copy(rel) for rel in range(1, N_DEV)]
        for cp in sent:
            cp.start()
        for cp in mine:
            cp.wait()

        def add_blocks(q, own, got, n_rows, dst):
            def step(b, carry):
                rows = pl.ds(pl.multiple_of(b * rb, rb), rb)
                dst[rows, :] = (own[q, rows, :].astype(F32) + got[q, rows, :].astype(F32)).astype(dst.dtype)
                return carry
            lax.fori_loop(0, n_rows // rb, step, 0)

        for rel in range(1, n_chips):
            q = my_chip ^ rel
            for cp in stage1(q):
                cp.wait_recv()
            add_blocks(q, own_in, s1_in, D_MODEL, fwd_in.at[rel - 1])
            add_blocks(q, own_out, s1_out, WOUT_ROWS, fwd_out.at[rel - 1])
            for cp in stage2(rel):
                cp.start()
                sent.append(cp)
        for cp in stage1(my_chip):
            cp.wait_recv()
        add_blocks(my_chip, own_in, s1_in, D_MODEL, gin_ref)
        add_blocks(my_chip, own_out, s1_out, WOUT_ROWS, gout_ref)
        for rel in range(1, n_chips):
            for cp in stage2(rel):
                cp.wait_recv()

        def update(got, w_ref, m_ref, v_ref, g_ref, d_ref, nm_ref, nv_ref, n_rows):
            def step(b, carry):
                rows = pl.ds(pl.multiple_of(b * rb, rb), rb)
                g = g_ref[rows, :]
                for rel in range(1, n_chips):
                    g = g + got[rel - 1, rows, :].astype(F32)
                delta, nm, nv = _adamw(w_ref[rows, :], g, m_ref[rows, :], v_ref[rows, :])
                g_ref[rows, :] = g
                d_ref[rows, :] = delta
                nm_ref[rows, :] = nm
                nv_ref[rows, :] = nv
                return carry
            lax.fori_loop(0, n_rows // rb, step, 0)

        update(s2_in, win_ref, min_ref, vin_ref, gin_ref, din_ref, nmin_ref, nvin_ref, D_MODEL)
        update(s2_out, wout_ref, mout_ref, vout_ref, gout_ref, dout_ref, nmout_ref, nvout_ref, WOUT_ROWS)

        for rel in range(1, N_DEV):
            small_copy(rel).wait_recv()
        my_flat = _flat(me)
        g = land_s[my_flat ^ 0]
        for dev in range(1, N_DEV):
            g = g + land_s[my_flat ^ dev]
        delta, nm, nv = _adamw(ws_ref[...], g, ms_ref[...], vs_ref[...])
        gs_ref[...] = g
        ds_ref[...] = delta
        nms_ref[...] = nm
        nvs_ref[...] = nv
        for cp in sent:
            cp.wait_send()

    vm = lambda: pl.BlockSpec(memory_space=pltpu.VMEM)
    anyspace = lambda: pl.BlockSpec(memory_space=pl.ANY)
    big = jax.ShapeDtypeStruct((D_MODEL, COL_BLOCK), F32)
    flat = jax.ShapeDtypeStruct((WOUT_ROWS, D_MODEL), F32)
    small = jax.ShapeDtypeStruct((SMALL_ROWS, LANES), F32)
    in_blocks = lambda n: pltpu.VMEM((n, D_MODEL, COL_BLOCK), BF16)
    out_blocks = lambda n: pltpu.VMEM((n, WOUT_ROWS, D_MODEL), BF16)
    return pl.pallas_call(
        body, name="exchange_update",
        out_shape=tuple([big] * 4 + [flat] * 4 + [small] * 4),
        in_specs=[anyspace(), anyspace()] + [vm() for _ in range(10)],
        out_specs=tuple(vm() for _ in range(12)),
        scratch_shapes=[in_blocks(n_chips), out_blocks(n_chips), in_blocks(n_chips), out_blocks(n_chips),
                        in_blocks(n_chips - 1), out_blocks(n_chips - 1),
                        in_blocks(n_chips - 1), out_blocks(n_chips - 1),
                        pltpu.VMEM((N_DEV, SMALL_ROWS, LANES), F32),
                        pltpu.SemaphoreType.DMA((21,)), pltpu.SemaphoreType.DMA((21,)),
                        pltpu.SemaphoreType.DMA((2,))],
        compiler_params=_params(),
    )(dwin_p.reshape(n_chips, 2, D_MODEL, COL_BLOCK), dwout_p.reshape(n_chips, 2, WOUT_ROWS, D_MODEL),
      small_p, w_in, m_in, v_in, w_out, m_out, v_out, w_s, m_s, v_s)


def _pack_small(mix, attn, hgrn, lb, final, loss=None):
    def rows8(a):
        a = a.reshape(-1, LANES)
        return jnp.pad(a, ((0, 8 - a.shape[0]), (0, 0)))
    last = jnp.zeros((8, LANES), F32) if loss is None else jnp.pad(loss.reshape(1, 1), ((0, 7), (0, LANES - 1)))
    return jnp.concatenate([rows8(mix), rows8(attn), rows8(hgrn), rows8(lb), rows8(final), last], axis=0)


def _unpack_small(slab):
    return (slab[ROW_MIX:ROW_MIX + 8].reshape(1, D_MODEL), slab[ROW_ATTN:ROW_ATTN + 4].reshape(1, ATTN_WIDTH),
            slab[ROW_HGRN:ROW_HGRN + 4].reshape(1, HGRN_WIDTH), slab[ROW_LB:ROW_LB + 8].reshape(2, HGRN_WIDTH),
            slab[ROW_FINAL:ROW_FINAL + 8].reshape(D_MODEL))


def _rope(pos_col):
    lane_e = np.arange(LANES) % HEAD_DIM
    inv = ROPE_THETA ** (-(lane_e % ROPE_HALF) * (2.0 / ROPE_DIMS))
    inv_lanes = np.where(lane_e < ROPE_DIMS, inv, 0.0).astype(np.float32).reshape(1, LANES)
    return _rope_tables(pos_col, jnp.asarray(inv_lanes))


def _local_step(x, proj, hn_t, w_in_g, w_out_g, tables, mix_w, attn_w, hgrn_w, lb_raw, final_w, target):
    rc, rsa, rsb = tables
    attn_o, lse = _attn_fwd_fused(proj)
    rec, states = _hgrn_fwd(proj, lb_raw)

    (dx2, d_o, delta, d_ag, d_rec, d_hg, dwout_p, d_final, d_attn_w, d_hgrn_w, loss) = _mid(
        attn_o, rec, proj, x, target, w_out_g, attn_w, hgrn_w, final_w.reshape(1, D_MODEL))

    dqkv = _attn_bwd_fused(proj, d_o, lse, delta)
    d_hq, d_hf, d_hi, d_lb = _hgrn_bwd(proj, lb_raw, d_rec, states)

    grad_x, dproj_b, d_mix = _in_proj_bwd_rows(
        (dqkv[0], dqkv[1], dqkv[2], d_ag, d_hq, d_hf, d_hi, d_hg), w_in_g, x, dx2, mix_w, rc, rsa, rsb)
    dwin_p = _in_proj_bwd_weights(hn_t, dproj_b)
    small_p = _pack_small(d_mix, d_attn_w, d_hgrn_w, d_lb, d_final, loss)
    return grad_x, dwin_p, dwout_p, small_p


def kernel(x, positions, w_in, w_out, mix_norm_w, attn_out_norm_w, hgrn_out_norm_w, hgrn_lb_raw, final_norm_w, loss_target, m_w_in, m_w_out, m_mix_norm_w, m_attn_out_norm_w, m_hgrn_out_norm_w, m_hgrn_lb_raw, m_final_norm_w, v_w_in, v_w_out, v_mix_norm_w, v_attn_out_norm_w, v_hgrn_out_norm_w, v_hgrn_lb_raw, v_final_norm_w):
    tables = _rope(positions.reshape(SEQ, 1))
    proj, hn_t, w_in_g, w_out_g = _gather_project(x[0], mix_norm_w, w_in[0], w_out[0], *tables)
    grad_x, dwin_p, dwout_p, small_p = _local_step(
        x[0], proj, hn_t, w_in_g, w_out_g, tables, mix_norm_w, attn_out_norm_w, hgrn_out_norm_w,
        hgrn_lb_raw, final_norm_w, loss_target[0])

    w_s = _pack_small(mix_norm_w, attn_out_norm_w, hgrn_out_norm_w, hgrn_lb_raw, final_norm_w)
    m_s = _pack_small(m_mix_norm_w, m_attn_out_norm_w, m_hgrn_out_norm_w, m_hgrn_lb_raw, m_final_norm_w)
    v_s = _pack_small(v_mix_norm_w, v_attn_out_norm_w, v_hgrn_out_norm_w, v_hgrn_lb_raw, v_final_norm_w)
    (g_in, d_in, nm_in, nv_in, g_out, d_out, nm_out, nv_out, g_s, d_s, nm_s, nv_s) = _exchange_update(
        dwin_p, dwout_p, small_p, w_in[0], m_w_in[0], v_w_in[0], w_out[0], m_w_out[0], v_w_out[0], w_s, m_s, v_s)

    loss = g_s[ROW_LOSS, 0]
    return (loss, grad_x[None], g_in[None], g_out[None], *_unpack_small(g_s),
            d_in[None], d_out[None], *_unpack_small(d_s),
            nm_in[None], nm_out[None], *_unpack_small(nm_s),
            nv_in[None], nv_out[None], *_unpack_small(nv_s))
```

```python
import functools

import jax
import jax.numpy as jnp
import numpy as np
from jax import lax
from jax.experimental import pallas as pl
from jax.experimental.pallas import tpu as pltpu

F32 = jnp.float32
BF16 = jnp.bfloat16

SEQ = 4096
D_MODEL = 1024
ATTN_WIDTH = 512
HGRN_WIDTH = 512
HEAD_DIM = 64
HGRN_HEADS = 4
HGRN_DIM = 128
HGRN_CHUNK = 64
N_CHUNKS = SEQ // HGRN_CHUNK
IN_COLS = 4096
COL_BLOCK = 512
N_DEV = 8
WOUT_ROWS = D_MODEL // N_DEV
ATTN_BLOCK = 128
DILATIONS = (1, 4, 16)
ROPE_THETA = 500000.0
ROPE_DIMS = 16
ROPE_HALF = 8
NORM_EPS = 1e-6
NEG_BIG = -1e30
LANES = 128

ADAM_LR = 0.001
ADAM_B1 = 0.9
ADAM_B2 = 0.999
ADAM_EPS = 1e-08
ADAM_WD = 0.01
ADAM_STEP = 10

SMALL_ROWS = 48
ROW_MIX, ROW_ATTN, ROW_HGRN, ROW_LB, ROW_FINAL, ROW_LOSS = 0, 8, 16, 24, 32, 40

VMEM_LIMIT = 56 * 1024 * 1024
MESH = pl.DeviceIdType.MESH


def _mm(a, b):
    return lax.dot_general(a, b, (((1,), (0,)), ((), ())), preferred_element_type=F32)


def _mm_nt(a, b):
    return lax.dot_general(a, b, (((1,), (1,)), ((), ())), preferred_element_type=F32)


def _mm_tn(a, b):
    return lax.dot_general(a, b, (((0,), (0,)), ((), ())), preferred_element_type=F32)


def _mm_exact(a, b):
    return lax.dot_general(a, b, (((1,), (0,)), ((), ())), preferred_element_type=F32,
                           precision=lax.Precision.HIGHEST)


def _sigmoid(v):
    return 1.0 / (1.0 + jnp.exp(-v))


def _params(sem=None, **kw):
    return pltpu.CompilerParams(dimension_semantics=sem, vmem_limit_bytes=VMEM_LIMIT, **kw)


def _my_place():
    return lax.axis_index("x"), lax.axis_index("y"), lax.axis_index("c")


def _peer(place, rel):
    x, y, c = place
    return (x ^ ((rel >> 2) & 1), y ^ ((rel >> 1) & 1), c ^ (rel & 1))


def _flat(place):
    x, y, c = place
    return 4 * x + 2 * y + c


def _rope_tables(pos_col, inv_freq_lanes):
    tm = 512

    def body(pos_ref, invf_ref, c_ref, sa_ref, sb_ref):
        ang = pos_ref[...].astype(F32) * invf_ref[...]
        e = lax.broadcasted_iota(jnp.int32, (tm, LANES), 1) & (HEAD_DIM - 1)
        cos, sin = jnp.cos(ang), jnp.sin(ang)
        c_ref[...] = jnp.where(e < ROPE_DIMS, cos, 1.0)
        sa_ref[...] = jnp.where((e >= ROPE_HALF) & (e < ROPE_DIMS), sin, 0.0)
        sb_ref[...] = jnp.where(e < ROPE_HALF, -sin, 0.0)

    tab = jax.ShapeDtypeStruct((SEQ, LANES), F32)
    spec = pl.BlockSpec((tm, LANES), lambda i: (i, 0))
    return pl.pallas_call(
        body, name="rope_tables", grid=(SEQ // tm,), out_shape=(tab, tab, tab),
        in_specs=[pl.BlockSpec((tm, 1), lambda i: (i, 0)), pl.BlockSpec((1, LANES), lambda i: (0, 0))],
        out_specs=(spec, spec, spec), compiler_params=_params(("parallel",)),
    )(pos_col, inv_freq_lanes)


def _per_slab(fn, t):
    return jnp.concatenate([fn(t[:, LANES * s:LANES * (s + 1)]) for s in range(t.shape[1] // LANES)], axis=1)


def _rot(t, c, sa, sb):
    return _per_slab(lambda u: u * c + pltpu.roll(u, ROPE_HALF, 1) * sa + pltpu.roll(u, LANES - ROPE_HALF, 1) * sb, t)


def _rot_transposed(g, c, sa, sb):
    return _per_slab(
        lambda u: u * c + pltpu.roll(u * sa, LANES - ROPE_HALF, 1) + pltpu.roll(u * sb, ROPE_HALF, 1), g)


def _gather_project(x, mix_w, w_in, w_out, rc, rsa, rsb):
    tm = 1024
    n_tiles = SEQ // tm
    arrival_of_step = (None, 0, 1, 2, 4, 5, 3, 6)

    def body(order_ref, x_ref, w_ref, win_ref, wout_ref, c_ref, sa_ref, sb_ref,
             proj_ref, hnt_ref, gin_hbm, gout_hbm,
             hn_s, w_land, wout_land, stage, send_sems, recv_sems, local_sems):
        g, i = pl.program_id(0), pl.program_id(1)
        me = _my_place()
        x_, y_, c_ = me
        sibling = (x_, y_, 1 - c_)
        chips = [(1 - x_, y_), (x_, 1 - y_), (1 - x_, 1 - y_)]

        def slab(which, place):
            idx = _flat(place)
            if which == 0:
                return w_land.at[idx]
            return wout_land.at[pl.ds(pl.multiple_of(idx * WOUT_ROWS, WOUT_ROWS), WOUT_ROWS), :]

        def copy(which, k, block, to, src=None):
            ref = slab(which, block)
            return pltpu.make_async_remote_copy(
                src_ref=ref if src is None else src, dst_ref=ref, send_sem=send_sems.at[7 * which + k],
                recv_sem=recv_sems.at[7 * which + k], device_id=to, device_id_type=MESH)

        def first_copies(which):
            src = stage if which == 0 else None
            return ([copy(which, 0, me, sibling, src)]
                    + [copy(which, 1 + j, me, (*chip, c_), src) for j, chip in enumerate(chips)])

        def pass_on(which, j):
            return copy(which, 4 + j, (*chips[j], c_), sibling)

        def arrival(which, k):
            if k == 0:
                return copy(which, 0, sibling, me)
            if k <= 3:
                return copy(which, k, (*chips[k - 1], c_), me)
            return copy(which, k, (*chips[k - 4], 1 - c_), me)

        def to_hbm(step):
            idx = order_ref[step]
            cols = pl.ds(pl.multiple_of(idx * COL_BLOCK, COL_BLOCK), COL_BLOCK)
            return pltpu.make_async_copy(w_land.at[idx], gin_hbm.at[:, cols], local_sems.at[step])

        @pl.when((g == 0) & (i == 0))
        def _():
            stage[...] = win_ref[...].astype(BF16)
            w_land[_flat(me)] = stage[...]
            wout_land[pl.ds(pl.multiple_of(_flat(me) * WOUT_ROWS, WOUT_ROWS), WOUT_ROWS), :] = (
                wout_ref[...].astype(BF16))
            for which in (0, 1):
                for cp in first_copies(which):
                    cp.start()
            to_hbm(0).start()

        for step, k in enumerate(arrival_of_step):
            if k is None:
                continue

            @pl.when((g == step) & (i == 0))
            def _(k=k, step=step):
                arrival(0, k).wait_recv()
                to_hbm(step).start()
                if 1 <= k <= 3:
                    arrival(1, k).wait_recv()
                    pass_on(0, k - 1).start()
                    pass_on(1, k - 1).start()

        rows = pl.ds(pl.multiple_of(i * tm, tm), tm)

        @pl.when(g == 0)
        def _():
            xf = x_ref[...]
            ms = jnp.mean(xf * xf, axis=-1, keepdims=True)
            hn = xf * lax.rsqrt(ms + NORM_EPS) * w_ref[...]
            hnt_ref[...] = hn.T.astype(BF16)
            hn_s[rows, :] = hn.astype(BF16)

        group = order_ref[g]

        @pl.when(group < 2)
        def _():
            proj_ref[...] = _rot(_mm(hn_s[rows, :], w_land[group]), c_ref[...], sa_ref[...], sb_ref[...])

        @pl.when(group >= 2)
        def _():
            proj_ref[...] = _mm(hn_s[rows, :], w_land[group])

        @pl.when((g == N_DEV - 1) & (i == n_tiles - 1))
        def _():
            for k in (0, 4, 5, 6):
                arrival(1, k).wait_recv()
            for which in (0, 1):
                for cp in first_copies(which) + [pass_on(which, j) for j in range(3)]:
                    cp.wait_send()
            wout_copy = pltpu.make_async_copy(wout_land, gout_hbm, local_sems.at[N_DEV])
            wout_copy.start()
            for step in range(N_DEV):
                to_hbm(step).wait()
            wout_copy.wait()

    me = _my_place()
    x_, y_, c_ = me
    chips = [(1 - x_, y_), (x_, 1 - y_), (1 - x_, 1 - y_)]
    order = jnp.stack([_flat(p) for p in (
        me, (x_, y_, 1 - c_), (*chips[0], c_), (*chips[1], c_), (*chips[0], 1 - c_), (*chips[1], 1 - c_),
        (*chips[2], c_), (*chips[2], 1 - c_))]).astype(jnp.int32)

    first_sweep = lambda g, i, order: (jnp.where(g == 0, i, n_tiles - 1), 0)
    tab = pl.BlockSpec((tm, LANES), lambda g, i, order: (jnp.where(order[g] < 2, i, 0), 0))
    whole = lambda: pl.BlockSpec(memory_space=pltpu.VMEM)
    grid_spec = pltpu.PrefetchScalarGridSpec(
        num_scalar_prefetch=1, grid=(N_DEV, n_tiles),
        in_specs=[pl.BlockSpec((tm, D_MODEL), first_sweep),
                  pl.BlockSpec((1, D_MODEL), lambda g, i, order: (0, 0)),
                  whole(), whole(), tab, tab, tab],
        out_specs=(pl.BlockSpec((tm, COL_BLOCK), lambda g, i, order: (i, order[g])),
                   pl.BlockSpec((D_MODEL, tm), lambda g, i, order: (0, jnp.where(g == 0, i, n_tiles - 1))),
                   pl.BlockSpec(memory_space=pl.ANY), pl.BlockSpec(memory_space=pl.ANY)),
        scratch_shapes=[pltpu.VMEM((SEQ, D_MODEL), BF16),
                        pltpu.VMEM((N_DEV, D_MODEL, COL_BLOCK), BF16),
                        pltpu.VMEM((D_MODEL, D_MODEL), BF16),
                        pltpu.VMEM((D_MODEL, COL_BLOCK), BF16),
                        pltpu.SemaphoreType.DMA((14,)), pltpu.SemaphoreType.DMA((14,)),
                        pltpu.SemaphoreType.DMA((N_DEV + 1,))])
    return pl.pallas_call(
        body, name="gather_project", grid_spec=grid_spec,
        out_shape=(jax.ShapeDtypeStruct((SEQ, IN_COLS), F32), jax.ShapeDtypeStruct((D_MODEL, SEQ), BF16),
                   jax.ShapeDtypeStruct((D_MODEL, IN_COLS), BF16), jax.ShapeDtypeStruct((D_MODEL, D_MODEL), BF16)),
        compiler_params=_params(("arbitrary", "arbitrary")),
    )(order, x, mix_w, w_in, w_out, rc, rsa, rsb)


ATTN_GROUP = 8
BLOCKS_PER_PATTERN = SEQ // ATTN_BLOCK


def _write_band_bias(bias_ref):
    qi = lax.broadcasted_iota(jnp.int32, (2 * ATTN_BLOCK, 2 * ATTN_BLOCK), 0) & (ATTN_BLOCK - 1)
    kj = lax.broadcasted_iota(jnp.int32, (2 * ATTN_BLOCK, 2 * ATTN_BLOCK), 1)
    bias_ref[0] = jnp.where((kj >= qi) & (kj <= qi + ATTN_BLOCK), 0.0, NEG_BIG)
    bias_ref[1] = jnp.where(kj <= qi, 0.0, NEG_BIG)


def _head0_lanes():
    return lax.broadcasted_iota(jnp.int32, (ATTN_BLOCK, LANES), 1) < HEAD_DIM


def _stack_heads(t, h0):
    return jnp.concatenate([jnp.where(h0, t, 0.0), jnp.where(h0, 0.0, t)], axis=0).astype(BF16)


def _strided(start, size, d):
    return pl.ds(start, size) if d == 1 else pl.ds(start, size, stride=d)


def _block_place(i, d):
    nblk = BLOCKS_PER_PATTERN // d
    r, n = i // nblk, i % nblk
    kn = jnp.maximum(n - 1, 0)
    row0, key0 = n * (d * ATTN_BLOCK) + r, kn * (d * ATTN_BLOCK) + r
    if d == 1:
        row0, key0 = pl.multiple_of(row0, ATTN_BLOCK), pl.multiple_of(key0, ATTN_BLOCK)
    return _strided(row0, ATTN_BLOCK, d), _strided(key0, 2 * ATTN_BLOCK, d), (n == 0).astype(jnp.int32)


def _for_each_group(d, load, compute, store):
    def group(g, carry):
        items = [load(*_block_place(g * ATTN_GROUP + u, d)) for u in range(ATTN_GROUP)]
        results = [compute(item) for item in items]
        for item, res in zip(items, results):
            store(item, res)
        return carry

    lax.fori_loop(0, BLOCKS_PER_PATTERN // ATTN_GROUP, group, 0)


def _attn_fwd_fused(proj):
    n_pat = len(DILATIONS)
    tile2 = (2 * ATTN_BLOCK, LANES)

    def body(q_ref, k_ref, v_ref, o_ref, lse_ref, m_acc, l_acc, bias_ref):
        _write_band_bias(bias_ref)
        h0 = _head0_lanes()
        for pi, d in enumerate(DILATIONS):
            first, last = pi == 0, pi == n_pat - 1

            def load(rows, keys, which, first=first):
                item = dict(rows=rows, keys=keys, which=which)
                if not first:
                    item.update(o=o_ref[rows, :], m=[m_acc.at[h][rows, :] for h in range(2)],
                                l=[l_acc.at[h][rows, :] for h in range(2)])
                return item

            def compute(item, first=first):
                kb = k_ref[item["keys"], :].astype(BF16)
                vb = v_ref[item["keys"], :].astype(BF16)
                s = _mm_nt(_stack_heads(q_ref[item["rows"], :], h0), kb) * 0.125 + bias_ref[item["which"]]
                mb = jnp.max(s, axis=-1, keepdims=True)
                if first:
                    p = jnp.exp(s - mb)
                    mn = jnp.broadcast_to(mb, tile2)
                else:
                    m_old = jnp.concatenate(item["m"], axis=0)
                    mn = jnp.maximum(m_old, mb)
                    alpha = jnp.exp(m_old - mn)
                    p = jnp.exp(s - jnp.concatenate([mn, mn], axis=1))
                ls = jnp.sum(p, axis=-1, keepdims=True)
                pv = _mm(p.astype(BF16), vb)
                if first:
                    return pv, mn, jnp.broadcast_to(ls, tile2)
                o_old = jnp.concatenate([item["o"], item["o"]], axis=0)
                return alpha * o_old + pv, mn, alpha * jnp.concatenate(item["l"], axis=0) + ls

            def store(item, res, last=last):
                rows = item["rows"]
                (o0, o1), (m0, m1), (l0, l1) = ((a[:ATTN_BLOCK], a[ATTN_BLOCK:]) for a in res)
                if last:
                    o_ref[rows, :] = jnp.where(h0, o0 / l0, o1 / l1)
                    lse_ref[rows, :] = jnp.where(h0, m0 + jnp.log(l0), m1 + jnp.log(l1))
                else:
                    o_ref[rows, :] = jnp.where(h0, o0, o1)
                    m_acc.at[0][rows, :], m_acc.at[1][rows, :] = m0, m1
                    l_acc.at[0][rows, :], l_acc.at[1][rows, :] = l0, l1

            _for_each_group(d, load, compute, store)

    slab = lambda g: pl.BlockSpec((SEQ, LANES), functools.partial(lambda hp, g: (0, 4 * g + hp), g=g))
    wide = jax.ShapeDtypeStruct((SEQ, ATTN_WIDTH), F32)
    return pl.pallas_call(
        body, name="attn_fwd", grid=(4,), out_shape=(wide, wide),
        in_specs=[slab(0), slab(1), slab(2)], out_specs=(slab(0), slab(0)),
        scratch_shapes=[pltpu.VMEM((2, SEQ, LANES), F32), pltpu.VMEM((2, SEQ, LANES), F32),
                        pltpu.VMEM((2, 2 * ATTN_BLOCK, 2 * ATTN_BLOCK), F32)],
        compiler_params=_params(("parallel",)),
    )(proj, proj, proj)


def _attn_bwd_fused(proj, d_out, lse, delta):
    def body(q_ref, k_ref, v_ref, do_ref, lse_ref, del_ref, dq_ref, dk_ref, dv_ref, bias_ref):
        _write_band_bias(bias_ref)
        dk_ref[...] = jnp.zeros_like(dk_ref)
        dv_ref[...] = jnp.zeros_like(dv_ref)
        h0 = _head0_lanes()
        for pi, d in enumerate(DILATIONS):
            first = pi == 0

            def load(rows, keys, which):
                return dict(rows=rows, keys=keys, q=q_ref[rows, :], g=do_ref[rows, :], lse=lse_ref[rows, :],
                            delta=del_ref[rows, :], k=k_ref[keys, :].astype(BF16),
                            v=v_ref[keys, :].astype(BF16), bias=bias_ref[which])

            def per_head(t):
                swapped = pltpu.roll(t, HEAD_DIM, 1)
                both = jnp.concatenate([jnp.where(h0, t, swapped), jnp.where(h0, swapped, t)], axis=0)
                return jnp.concatenate([both, both], axis=1)

            def compute(item):
                q2, g2 = _stack_heads(item["q"], h0), _stack_heads(item["g"], h0)
                s = _mm_nt(q2, item["k"]) * 0.125 + item["bias"]
                p = jnp.exp(s - per_head(item["lse"]))
                dp = _mm_nt(g2, item["v"])
                ds = (p * (dp - per_head(item["delta"])) * 0.125).astype(BF16)
                dq2 = _mm(ds, item["k"])
                dq = jnp.where(h0, dq2[:ATTN_BLOCK], dq2[ATTN_BLOCK:])
                return dq, _mm_tn(ds, q2), _mm_tn(p.astype(BF16), g2)

            def store(item, res, first=first):
                rows, keys = item["rows"], item["keys"]
                if first:
                    dq_ref[rows, :] = res[0]
                else:
                    dq_ref[rows, :] += res[0]
                dk_ref[keys, :] += res[1]
                dv_ref[keys, :] += res[2]

            _for_each_group(d, load, compute, store)

    slab = lambda g: pl.BlockSpec((SEQ, LANES), functools.partial(lambda hp, g: (0, 4 * g + hp), g=g))
    wide = jax.ShapeDtypeStruct((SEQ, ATTN_WIDTH), F32)
    return pl.pallas_call(
        body, name="attn_bwd", grid=(4,), out_shape=(wide, wide, wide),
        scratch_shapes=[pltpu.VMEM((2, 2 * ATTN_BLOCK, 2 * ATTN_BLOCK), F32)],
        in_specs=[slab(0), slab(1), slab(2), slab(0), slab(0), slab(0)], out_specs=(slab(0), slab(0), slab(0)),
        compiler_params=_params(("parallel",)),
    )(proj, proj, proj, d_out, lse, delta)


def _hgrn_lower_bound(lb_ref):
    r0, r1 = lb_ref[0:1, :], lb_ref[1:2, :]
    mx = jnp.maximum(r0, r1)
    e0, e1 = jnp.exp(r0 - mx), jnp.exp(r1 - mx)
    return e0 / (e0 + e1)


def _hgrn_gates(hq, hf, lb):
    sq = _sigmoid(hq)
    sg = _sigmoid(hf)
    f = lb + (1.0 - lb) * sg
    return hq * sq, sq, sg, f, 1.0 - f, jnp.log(f)


HGRN_PAIR = 4
HGRN_SEQ_BLOCK = 1024
HGRN_GROUP = 4
HGRN_ROWS = HGRN_GROUP * HGRN_CHUNK


def _hgrn_specs(reverse):
    n_blocks = SEQ // HGRN_SEQ_BLOCK
    width = HGRN_PAIR * HGRN_DIM
    blk = (lambda s: n_blocks - 1 - s) if reverse else (lambda s: s)
    cols = lambda g: pl.BlockSpec((HGRN_SEQ_BLOCK, width),
                                  functools.partial(lambda p, s, g: (blk(s), (HGRN_HEADS // HGRN_PAIR) * g + p), g=g))
    pair = pl.BlockSpec((HGRN_SEQ_BLOCK, width), lambda p, s: (blk(s), p))
    lb = pl.BlockSpec((2, width), lambda p, s: (0, p))
    states = pl.BlockSpec((HGRN_PAIR, HGRN_SEQ_BLOCK // HGRN_CHUNK, HGRN_DIM, HGRN_DIM),
                          lambda p, s: (p, blk(s), 0, 0))
    return cols, pair, lb, states


def _chunk_masks():
    ri = lax.broadcasted_iota(jnp.int32, (HGRN_ROWS, HGRN_ROWS), 0)
    ci = lax.broadcasted_iota(jnp.int32, (HGRN_ROWS, HGRN_ROWS), 1)
    same = (ri // HGRN_CHUNK) == (ci // HGRN_CHUNK)
    return same, same & (ri >= ci), same & (ri <= ci)


def _mm_select(sel, v):
    hi = v.astype(BF16)
    r1 = v - hi.astype(F32)
    mid = r1.astype(BF16)
    lo = (r1 - mid.astype(F32)).astype(BF16)
    return _mm(sel, hi) + _mm(sel, mid) + _mm(sel, lo)


def _head_cols(a, h):
    return a[:, HGRN_DIM * h:HGRN_DIM * (h + 1)]


def _hgrn_fwd(proj, lb_raw):
    t, rws = HGRN_CHUNK, HGRN_ROWS

    def body(hq_ref, hf_ref, hi_ref, lb_ref, rec_ref, st_ref, state):
        @pl.when(pl.program_id(1) == 0)
        def _():
            state[...] = jnp.zeros_like(state)

        lb = _hgrn_lower_bound(lb_ref)
        same, causal, _ = _chunk_masks()
        sel = jnp.concatenate([causal, same], axis=0).astype(BF16)

        def group(g, sts):
            rows = pl.ds(pl.multiple_of(g * rws, rws), rws)
            q, _, _, _, k, lf = _hgrn_gates(hq_ref[rows, :], hf_ref[rows, :], lb)
            sums = _mm_select(sel, lf)
            cum, last = sums[:rws], sums[rws:]
            qd = (q * jnp.exp(cum)).astype(BF16)
            ki = (k * jnp.exp(-cum)).astype(BF16)
            ke = (k * jnp.exp(last - cum)).astype(BF16)
            vb = hi_ref[rows, :].astype(BF16)
            dec = jnp.exp(last)
            new_sts, recs = [], []
            for h in range(HGRN_PAIR):
                qd_h, ke_h, vb_h = _head_cols(qd, h), _head_cols(ke, h), _head_cols(vb, h)
                att = jnp.where(causal, _mm_nt(qd_h, _head_cols(ki, h)), 0.0).astype(BF16)
                intra = _mm(att, vb_h)
                st = sts[h]
                outs = []
                for c in range(HGRN_GROUP):
                    sl = slice(c * t, (c + 1) * t)
                    st_ref[h, g * HGRN_GROUP + c] = st
                    outs.append(intra[sl] + _mm_nt(qd_h[sl], st.astype(BF16)))
                    st = st * _head_cols(dec[c * t:c * t + 1, :], h) + _mm_tn(vb_h[sl], ke_h[sl])
                new_sts.append(st)
                recs.append(jnp.concatenate(outs, axis=0))
            rec_ref[rows, :] = jnp.concatenate(recs, axis=1)
            return tuple(new_sts)

        sts = lax.fori_loop(0, HGRN_SEQ_BLOCK // rws, group, tuple(state[h] for h in range(HGRN_PAIR)))
        for h in range(HGRN_PAIR):
            state[h] = sts[h]

    cols, pair, lb, states = _hgrn_specs(reverse=False)
    return pl.pallas_call(
        body, name="hgrn_fwd", grid=(HGRN_HEADS // HGRN_PAIR, SEQ // HGRN_SEQ_BLOCK),
        out_shape=(jax.ShapeDtypeStruct((SEQ, HGRN_WIDTH), F32),
                   jax.ShapeDtypeStruct((HGRN_HEADS, N_CHUNKS, HGRN_DIM, HGRN_DIM), F32)),
        in_specs=[cols(4), cols(5), cols(6), lb], out_specs=(pair, states),
        scratch_shapes=[pltpu.VMEM((HGRN_PAIR, HGRN_DIM, HGRN_DIM), F32)],
        compiler_params=_params(("parallel", "arbitrary")),
    )(proj, proj, proj, lb_raw)


def _hgrn_bwd(proj, lb_raw, d_rec, states):
    t, rws = HGRN_CHUNK, HGRN_ROWS

    def body(hq_ref, hf_ref, hi_ref, lb_ref, do_ref, st_ref, dhq_ref, dhf_ref, dhi_ref, dlb_ref,
             dstate, dlb_acc):
        lb = _hgrn_lower_bound(lb_ref)
        same, causal, anti = _chunk_masks()
        sel = jnp.concatenate([causal, same], axis=0).astype(BF16)
        sel_t = jnp.concatenate([anti, same], axis=1).astype(BF16)
        @pl.when(pl.program_id(1) == 0)
        def _():
            dstate[...] = jnp.zeros_like(dstate)
            dlb_acc[...] = jnp.zeros_like(dlb_acc)

        n_groups = HGRN_SEQ_BLOCK // rws
        chunks = [slice(c * t, (c + 1) * t) for c in range(HGRN_GROUP)]

        def group(i, dsts_in):
            g = n_groups - 1 - i
            rows = pl.ds(pl.multiple_of(g * rws, rws), rws)
            hq = hq_ref[rows, :]
            q, sq, sg, f, k, lf = _hgrn_gates(hq, hf_ref[rows, :], lb)
            sums = _mm_select(sel, lf)
            cum, last = sums[:rws], sums[rws:]
            e_cum, e_inv, e_end, dec = jnp.exp(cum), jnp.exp(-cum), jnp.exp(last - cum), jnp.exp(last)
            qd, ki, ke = q * e_cum, k * e_inv, k * e_end
            qdb, kib, keb = qd.astype(BF16), ki.astype(BF16), ke.astype(BF16)
            vb = hi_ref[rows, :].astype(BF16)
            gb = do_ref[rows, :].astype(BF16)

            dsts_out, per_head = [], []
            for h in range(HGRN_PAIR):
                qdb_h, kib_h, keb_h = _head_cols(qdb, h), _head_cols(kib, h), _head_cols(keb, h)
                vb_h, gb_h = _head_cols(vb, h), _head_cols(gb, h)
                att = jnp.where(causal, _mm_nt(qdb_h, kib_h), 0.0).astype(BF16)
                datt = jnp.where(causal, _mm_nt(gb_h, vb_h), 0.0).astype(BF16)
                dv = _mm_tn(att, gb_h)
                dqd = _mm(datt, kib_h)
                dki = _mm_tn(datt, qdb_h)

                decs = [_head_cols(dec[c * t:c * t + 1, :], h) for c in range(HGRN_GROUP)]
                dsts = [None] * HGRN_GROUP
                dst = dsts_in[h]
                for c in reversed(range(HGRN_GROUP)):
                    dsts[c] = dst
                    dst = dst * decs[c] + _mm_tn(gb_h[chunks[c]], qdb_h[chunks[c]])
                dsts_out.append(dst)

                dv_x, dqd_x, dke, dlast_x = [], [], [], []
                for c, sl in enumerate(chunks):
                    st_prev = st_ref[h, g * HGRN_GROUP + c]
                    dstb = dsts[c].astype(BF16)
                    dv_x.append(_mm_nt(keb_h[sl], dstb))
                    dqd_x.append(_mm(gb_h[sl], st_prev.astype(BF16)))
                    dke.append(_mm(vb_h[sl], dstb))
                    ddec = jnp.sum(dsts[c] * st_prev, axis=0, keepdims=True)
                    dlast_x.append(jnp.broadcast_to(ddec * decs[c], (t, HGRN_DIM)))
                per_head.append((dv + jnp.concatenate(dv_x, axis=0), dqd + jnp.concatenate(dqd_x, axis=0),
                                 dki, jnp.concatenate(dke, axis=0), jnp.concatenate(dlast_x, axis=0)))
            dv, dqd, dki, dke, dlast = (jnp.concatenate(list(parts), axis=1) for parts in zip(*per_head))

            dq = dqd * e_cum
            dk = dki * e_inv + dke * e_end
            dke_ke = dke * ke
            dcum = dqd * qd - dki * ki - dke_ke
            dlf = _mm_select(sel_t, jnp.concatenate([dcum, dke_ke], axis=0)) + dlast
            df = dlf / f - dk
            dhq_ref[rows, :] = dq * (sq * (1.0 + hq * (1.0 - sq)))
            dhf_ref[rows, :] = df * (1.0 - lb) * (sg * (1.0 - sg))
            dhi_ref[rows, :] = dv
            dlb_acc[...] += jnp.sum(df * (1.0 - sg), axis=0, keepdims=True)
            return tuple(dsts_out)

        dsts = lax.fori_loop(0, n_groups, group, tuple(dstate[h] for h in range(HGRN_PAIR)))
        for h in range(HGRN_PAIR):
            dstate[h] = dsts[h]
        g0 = dlb_acc[...] * lb * (1.0 - lb)
        dlb_ref[...] = jnp.concatenate([g0, -g0], axis=0)

    cols, pair, lb_spec, st_spec = _hgrn_specs(reverse=True)
    wide = jax.ShapeDtypeStruct((SEQ, HGRN_WIDTH), F32)
    return pl.pallas_call(
        body, name="hgrn_bwd", grid=(HGRN_HEADS // HGRN_PAIR, SEQ // HGRN_SEQ_BLOCK),
        out_shape=(wide, wide, wide, jax.ShapeDtypeStruct((2, HGRN_WIDTH), F32)),
        in_specs=[cols(4), cols(5), cols(6), lb_spec, pair, st_spec],
        out_specs=(pair, pair, pair, lb_spec),
        scratch_shapes=[pltpu.VMEM((HGRN_PAIR, HGRN_DIM, HGRN_DIM), F32),
                        pltpu.VMEM((1, HGRN_PAIR * HGRN_DIM), F32)],
        compiler_params=_params(("parallel", "arbitrary")),
    )(proj, proj, proj, lb_raw, d_rec, states)


def _group_sum(v, group):
    parts = []
    for s in range(v.shape[1] // LANES):
        slab = v[:, LANES * s:LANES * (s + 1)]
        if group == LANES:
            parts.append(jnp.broadcast_to(jnp.sum(slab, axis=-1, keepdims=True), slab.shape))
        else:
            h0 = lax.broadcasted_iota(jnp.int32, slab.shape, 1) < HEAD_DIM
            s0 = jnp.sum(jnp.where(h0, slab, 0.0), axis=-1, keepdims=True)
            s1 = jnp.sum(jnp.where(h0, 0.0, slab), axis=-1, keepdims=True)
            parts.append(jnp.where(h0, s0, s1))
    return jnp.concatenate(parts, axis=1)


def _mid(attn_o, rec, proj, x, target, w_out_g, attn_w, hgrn_w, final_w):
    tm = 256

    def branch_fwd(o, gate, w, group):
        r = lax.rsqrt(_group_sum(o * o, group) * (1.0 / group) + NORM_EPS)
        nrm = o * r
        sg = _sigmoid(gate)
        return r, nrm, sg, nrm * w * (gate * sg)

    def branch_bwd(dy, r, nrm, sg, gate, w, group):
        silu = gate * sg
        d_gate = dy * nrm * w * (sg * (1.0 + gate * (1.0 - sg)))
        d_w = jnp.sum(dy * nrm * silu, axis=0, keepdims=True)
        dn = dy * w * silu
        d_o = r * (dn - nrm * (_group_sum(dn * nrm, group) * (1.0 / group)))
        return d_o, d_gate, d_w

    def body(o_ref, rec_ref, ag_ref, hg_ref, x_ref, tgt_ref, wout_ref, aw_ref, hw_ref, fw_ref,
             dx2_ref, do_ref, delta_ref, dag_ref, drec_ref, dhg_ref, dwout_ref, dfw_ref, daw_ref, dhw_ref,
             loss_ref, dwout_acc):
        i = pl.program_id(0)

        @pl.when(i == 0)
        def _():
            dwout_acc[...] = jnp.zeros_like(dwout_acc)
            dfw_ref[...] = jnp.zeros_like(dfw_ref)
            daw_ref[...] = jnp.zeros_like(daw_ref)
            dhw_ref[...] = jnp.zeros_like(dhw_ref)
            loss_ref[...] = jnp.zeros_like(loss_ref)

        o, rc, ag, hg = o_ref[...], rec_ref[...], ag_ref[...], hg_ref[...]
        aw, hw, fw = aw_ref[...], hw_ref[...], fw_ref[...]
        ra, na, sga, ya = branch_fwd(o, ag, aw, HEAD_DIM)
        rh, nh, sgh, yh = branch_fwd(rc, hg, hw, HGRN_DIM)
        mixed = jnp.concatenate([ya, yh], axis=1).astype(BF16)
        wout = wout_ref[...]
        x2 = x_ref[...] + _mm(mixed, wout)
        rstd = lax.rsqrt(jnp.mean(x2 * x2, axis=-1, keepdims=True) + NORM_EPS)
        xn = x2 * rstd
        err = xn * fw - tgt_ref[...]
        row_loss = jnp.mean(err * err, axis=-1, keepdims=True)
        loss_ref[...] += 0.5 * jnp.sum(row_loss, axis=0, keepdims=True)
        dy = err * (1.0 / D_MODEL)
        dfw_ref[...] += jnp.sum(dy * xn, axis=0, keepdims=True)
        dxn = dy * fw
        dx2 = rstd * (dxn - xn * jnp.mean(dxn * xn, axis=-1, keepdims=True))
        dx2_ref[...] = dx2
        dx2b = dx2.astype(BF16)
        dwout_acc[...] += _mm_tn(mixed, dx2b)

        @pl.when(i == pl.num_programs(0) - 1)
        def _():
            dwout_ref[...] = dwout_acc[...].astype(BF16)

        dmixed = _mm_nt(dx2b, wout)

        d_o, d_ag, d_aw = branch_bwd(dmixed[:, :ATTN_WIDTH], ra, na, sga, ag, aw, HEAD_DIM)
        d_rec, d_hg, d_hw = branch_bwd(dmixed[:, ATTN_WIDTH:], rh, nh, sgh, hg, hw, HGRN_DIM)
        do_ref[...] = d_o
        delta_ref[...] = _group_sum(d_o * o, HEAD_DIM)
        dag_ref[...] = d_ag
        drec_ref[...] = d_rec
        dhg_ref[...] = d_hg
        daw_ref[...] += d_aw
        dhw_ref[...] += d_hw

    half = lambda: pl.BlockSpec((tm, COL_BLOCK), lambda i: (i, 0))
    full = lambda: pl.BlockSpec((tm, D_MODEL), lambda i: (i, 0))
    fixed = lambda r, c: pl.BlockSpec((r, c), lambda i: (0, 0))
    wide = jax.ShapeDtypeStruct((SEQ, COL_BLOCK), F32)
    return pl.pallas_call(
        body, name="mid", grid=(SEQ // tm,),
        out_shape=(jax.ShapeDtypeStruct((SEQ, D_MODEL), F32), wide, wide, wide, wide, wide,
                   jax.ShapeDtypeStruct((D_MODEL, D_MODEL), BF16),
                   jax.ShapeDtypeStruct((1, D_MODEL), F32), jax.ShapeDtypeStruct((1, COL_BLOCK), F32),
                   jax.ShapeDtypeStruct((1, COL_BLOCK), F32), jax.ShapeDtypeStruct((1, 1), F32)),
        scratch_shapes=[pltpu.VMEM((D_MODEL, D_MODEL), F32)],
        in_specs=[half(), half(),
                  pl.BlockSpec((tm, COL_BLOCK), lambda i: (i, 3)), pl.BlockSpec((tm, COL_BLOCK), lambda i: (i, 7)),
                  full(), full(), fixed(D_MODEL, D_MODEL), fixed(1, COL_BLOCK), fixed(1, COL_BLOCK),
                  fixed(1, D_MODEL)],
        out_specs=(full(), half(), half(), half(), half(), half(), fixed(D_MODEL, D_MODEL),
                   fixed(1, D_MODEL), fixed(1, COL_BLOCK), fixed(1, COL_BLOCK), fixed(1, 1)),
        compiler_params=_params(("arbitrary",)),
    )(attn_o, rec, proj, proj, x, target, w_out_g, attn_w, hgrn_w, final_w)


def _in_proj_bwd_rows(d_groups, w_g, x, dx2, mix_w, rc, rsa, rsb):
    tm = 256

    def body(*refs):
        dg_refs = refs[:N_DEV]
        wg_ref, x_ref, dx2_ref, w_ref, c_ref, sa_ref, sb_ref, gx_ref, dpb_ref, dmw_ref = refs[N_DEV:]

        @pl.when(pl.program_id(0) == 0)
        def _():
            dmw_ref[...] = jnp.zeros_like(dmw_ref)

        parts = []
        for j in range(N_DEV):
            dp = dg_refs[j][...]
            if j < 2:
                dp = _rot_transposed(dp, c_ref[...], sa_ref[...], sb_ref[...])
            parts.append(dp.astype(BF16))
        dpb = jnp.concatenate(parts, axis=1)
        dpb_ref[...] = dpb
        g = _mm_nt(dpb, wg_ref[...])
        xf = x_ref[...]
        rstd = lax.rsqrt(jnp.mean(xf * xf, axis=-1, keepdims=True) + NORM_EPS)
        xn = xf * rstd
        dmw_ref[...] += jnp.sum(g * xn, axis=0, keepdims=True)
        gw = g * w_ref[...]
        gx_ref[...] = dx2_ref[...] + rstd * (gw - xn * jnp.mean(gw * xn, axis=-1, keepdims=True))

    tile = lambda cols: pl.BlockSpec((tm, cols), lambda i: (i, 0))
    fixed = lambda r, c: pl.BlockSpec((r, c), lambda i: (0, 0))
    return pl.pallas_call(
        body, name="in_proj_bwd_rows", grid=(SEQ // tm,),
        out_shape=(jax.ShapeDtypeStruct((SEQ, D_MODEL), F32), jax.ShapeDtypeStruct((SEQ, IN_COLS), BF16),
                   jax.ShapeDtypeStruct((1, D_MODEL), F32)),
        in_specs=[tile(COL_BLOCK) for _ in range(N_DEV)] + [
            fixed(D_MODEL, IN_COLS), tile(D_MODEL), tile(D_MODEL), fixed(1, D_MODEL),
            tile(LANES), tile(LANES), tile(LANES)],
        out_specs=(tile(D_MODEL), tile(IN_COLS), fixed(1, D_MODEL)),
        compiler_params=_params(("arbitrary",)),
    )(*d_groups, w_g, x, dx2, mix_w, rc, rsa, rsb)


def _weights_exchange(hn_t, dproj_b, dwout_p, small_p):
    n_chips = N_DEV // 2
    rb = 128
    S1_IN, S1_OUT, SMALL, S2_IN, S2_OUT = 0, 4, 8, 15, 18
    rel_of_pair = (1, 2, 3, 0)

    def body(order_ref, hnt_ref, dp_ref, dwout_ref, small_ref, gin_ref, gout_ref, gs_ref,
             part, s1_send, s1_in, s1_out, fwd_in, fwd_out, s2_in, s2_out, land_s, send_sems, recv_sems):
        t = pl.program_id(0)
        me = _my_place()
        x, y, c = me
        my_chip = 2 * x + y
        sibling = (x, y, 1 - c)

        def remote(slot, src, dst, to):
            return pltpu.make_async_remote_copy(src_ref=src, dst_ref=dst, send_sem=send_sems.at[slot],
                                                recv_sem=recv_sems.at[slot], device_id=to, device_id_type=MESH)

        def s1_in_copy(pair):
            return remote(S1_IN + pair, s1_send.at[pair], s1_in.at[pair], sibling)

        def s1_out_copy(pair):
            q = my_chip ^ rel_of_pair[pair]
            return remote(S1_OUT + pair, dwout_ref.at[q, 1 - c], s1_out.at[pair], sibling)

        def s2_copies(rel):
            peer = _peer(me, 2 * rel)
            return [remote(S2_IN + rel - 1, fwd_in.at[rel - 1], s2_in.at[rel - 1], peer),
                    remote(S2_OUT + rel - 1, fwd_out.at[rel - 1], s2_out.at[rel - 1], peer)]

        def small_copy(rel):
            return remote(SMALL + rel - 1, small_ref, land_s.at[rel], _peer(me, rel))

        @pl.when(t == 0)
        def _():
            land_s[0] = small_ref[...]
            for pair in range(n_chips):
                s1_out_copy(pair).start()
            for rel in range(1, N_DEV):
                small_copy(rel).start()

        part[...] = _mm(hnt_ref[...], dp_ref[...])

        def rows_loop(n_rows, fn):
            def step(b, carry):
                fn(pl.ds(pl.multiple_of(b * rb, rb), rb))
                return carry
            lax.fori_loop(0, n_rows // rb, step, 0)

        for pair, rel in enumerate(rel_of_pair):
            @pl.when(t == 2 * pair)
            def _(pair=pair):
                s1_send[pair] = part[...].astype(BF16)
                s1_in_copy(pair).start()

            @pl.when(t == 2 * pair + 1)
            def _(pair=pair, rel=rel):
                q = my_chip ^ rel
                s1_in_copy(pair).wait_recv()
                s1_out_copy(pair).wait_recv()
                dst_in = fwd_in.at[rel - 1] if rel else gin_ref
                dst_out = fwd_out.at[rel - 1] if rel else gout_ref

                def add_in(rows):
                    dst_in[rows, :] = (part[rows, :] + s1_in[pair, rows, :].astype(F32)).astype(dst_in.dtype)

                def add_out(rows):
                    dst_out[rows, :] = (dwout_ref[q, c, rows, :].astype(F32)
                                        + s1_out[pair, rows, :].astype(F32)).astype(dst_out.dtype)

                rows_loop(D_MODEL, add_in)
                rows_loop(WOUT_ROWS, add_out)
                if rel:
                    for cp in s2_copies(rel):
                        cp.start()

        @pl.when(t == N_DEV - 1)
        def _():
            for rel in range(1, n_chips):
                for cp in s2_copies(rel):
                    cp.wait_recv()

            def total_in(rows):
                g = gin_ref[rows, :]
                for rel in range(1, n_chips):
                    g = g + s2_in[rel - 1, rows, :].astype(F32)
                gin_ref[rows, :] = g

            def total_out(rows):
                g = gout_ref[rows, :]
                for rel in range(1, n_chips):
                    g = g + s2_out[rel - 1, rows, :].astype(F32)
                gout_ref[rows, :] = g

            rows_loop(D_MODEL, total_in)
            rows_loop(WOUT_ROWS, total_out)

            for rel in range(1, N_DEV):
                small_copy(rel).wait_recv()
            my_flat = _flat(me)
            g = land_s[my_flat ^ 0]
            for dev in range(1, N_DEV):
                g = g + land_s[my_flat ^ dev]
            gs_ref[...] = g

            for pair in range(n_chips):
                s1_in_copy(pair).wait_send()
                s1_out_copy(pair).wait_send()
            for rel in range(1, n_chips):
                for cp in s2_copies(rel):
                    cp.wait_send()
            for rel in range(1, N_DEV):
                small_copy(rel).wait_send()

    me = _my_place()
    x, y, c = me
    my_chip = 2 * x + y
    order = jnp.stack([2 * (my_chip ^ rel) + core for rel in rel_of_pair for core in (1 - c, c)]).astype(jnp.int32)

    whole = lambda: pl.BlockSpec(memory_space=pltpu.VMEM)
    in_blocks = lambda n: pltpu.VMEM((n, D_MODEL, COL_BLOCK), BF16)
    out_blocks = lambda n: pltpu.VMEM((n, WOUT_ROWS, D_MODEL), BF16)
    grid_spec = pltpu.PrefetchScalarGridSpec(
        num_scalar_prefetch=1, grid=(N_DEV,),
        in_specs=[pl.BlockSpec((D_MODEL, SEQ), lambda t, order: (0, 0)),
                  pl.BlockSpec((SEQ, COL_BLOCK), lambda t, order: (0, order[t])), whole(), whole()],
        out_specs=(whole(), whole(), whole()),
        scratch_shapes=[pltpu.VMEM((D_MODEL, COL_BLOCK), F32), in_blocks(n_chips), in_blocks(n_chips),
                        out_blocks(n_chips), in_blocks(n_chips - 1), out_blocks(n_chips - 1),
                        in_blocks(n_chips - 1), out_blocks(n_chips - 1),
                        pltpu.VMEM((N_DEV, SMALL_ROWS, LANES), F32),
                        pltpu.SemaphoreType.DMA((21,)), pltpu.SemaphoreType.DMA((21,))])
    return pl.pallas_call(
        body, name="weights_exchange", grid_spec=grid_spec,
        out_shape=(jax.ShapeDtypeStruct((D_MODEL, COL_BLOCK), F32), jax.ShapeDtypeStruct((WOUT_ROWS, D_MODEL), F32),
                   jax.ShapeDtypeStruct((SMALL_ROWS, LANES), F32)),
        compiler_params=_params(("arbitrary",)),
    )(order, hn_t, dproj_b, dwout_p.reshape(n_chips, 2, WOUT_ROWS, D_MODEL), small_p)


def _adamw(w, g, m, v):
    m = ADAM_B1 * m + (1.0 - ADAM_B1) * g
    v = ADAM_B2 * v + (1.0 - ADAM_B2) * (g * g)
    m_hat = m / (1.0 - ADAM_B1 ** ADAM_STEP)
    v_hat = v / (1.0 - ADAM_B2 ** ADAM_STEP)
    delta = -ADAM_LR * (m_hat / (jnp.sqrt(v_hat) + ADAM_EPS) + ADAM_WD * w)
    return delta, m, v


def _adamw_update(grads, weights, m_old, v_old):
    rb = 256

    def body(*refs):
        g_refs, w_refs, m_refs, v_refs = refs[0:3], refs[3:6], refs[6:9], refs[9:12]
        d_refs, nm_refs, nv_refs = refs[12:15], refs[15:18], refs[18:21]
        for k in range(3):
            n_rows = g_refs[k].shape[0]
            step_rows = min(rb, n_rows)

            def step(b, carry, k=k, step_rows=step_rows):
                rows = pl.ds(pl.multiple_of(b * step_rows, 8), step_rows)
                delta, nm, nv = _adamw(w_refs[k][rows, :], g_refs[k][rows, :], m_refs[k][rows, :], v_refs[k][rows, :])
                d_refs[k][rows, :] = delta
                nm_refs[k][rows, :] = nm
                nv_refs[k][rows, :] = nv
                return carry

            lax.fori_loop(0, n_rows // step_rows, step, 0)

    shapes = tuple(jax.ShapeDtypeStruct(g.shape, F32) for g in grads)
    vm = lambda: pl.BlockSpec(memory_space=pltpu.VMEM)
    outs = pl.pallas_call(
        body, name="adamw_update", out_shape=shapes * 3,
        in_specs=[vm() for _ in range(12)], out_specs=tuple(vm() for _ in range(9)),
        compiler_params=_params(),
    )(*grads, *weights, *m_old, *v_old)
    return outs[0:3], outs[3:6], outs[6:9]


def _pack_small(mix, attn, hgrn, lb, final, loss=None):
    def rows8(a):
        a = a.reshape(-1, LANES)
        return jnp.pad(a, ((0, 8 - a.shape[0]), (0, 0)))
    last = jnp.zeros((8, LANES), F32) if loss is None else jnp.pad(loss.reshape(1, 1), ((0, 7), (0, LANES - 1)))
    return jnp.concatenate([rows8(mix), rows8(attn), rows8(hgrn), rows8(lb), rows8(final), last], axis=0)


def _unpack_small(slab):
    return (slab[ROW_MIX:ROW_MIX + 8].reshape(1, D_MODEL), slab[ROW_ATTN:ROW_ATTN + 4].reshape(1, ATTN_WIDTH),
            slab[ROW_HGRN:ROW_HGRN + 4].reshape(1, HGRN_WIDTH), slab[ROW_LB:ROW_LB + 8].reshape(2, HGRN_WIDTH),
            slab[ROW_FINAL:ROW_FINAL + 8].reshape(D_MODEL))


def _rope(pos_col):
    lane_e = np.arange(LANES) % HEAD_DIM
    inv = ROPE_THETA ** (-(lane_e % ROPE_HALF) * (2.0 / ROPE_DIMS))
    inv_lanes = np.where(lane_e < ROPE_DIMS, inv, 0.0).astype(np.float32).reshape(1, LANES)
    return _rope_tables(pos_col, jnp.asarray(inv_lanes))


def _local_step(x, proj, hn_t, w_in_g, w_out_g, tables, mix_w, attn_w, hgrn_w, lb_raw, final_w, target):
    rc, rsa, rsb = tables
    attn_o, lse = _attn_fwd_fused(proj)
    rec, states = _hgrn_fwd(proj, lb_raw)

    (dx2, d_o, delta, d_ag, d_rec, d_hg, dwout_p, d_final, d_attn_w, d_hgrn_w, loss) = _mid(
        attn_o, rec, proj, x, target, w_out_g, attn_w, hgrn_w, final_w.reshape(1, D_MODEL))

    dqkv = _attn_bwd_fused(proj, d_o, lse, delta)
    d_hq, d_hf, d_hi, d_lb = _hgrn_bwd(proj, lb_raw, d_rec, states)

    grad_x, dproj_b, d_mix = _in_proj_bwd_rows(
        (dqkv[0], dqkv[1], dqkv[2], d_ag, d_hq, d_hf, d_hi, d_hg), w_in_g, x, dx2, mix_w, rc, rsa, rsb)
    small_p = _pack_small(d_mix, d_attn_w, d_hgrn_w, d_lb, d_final, loss)
    return grad_x, dproj_b, dwout_p, small_p


def kernel(x, positions, w_in, w_out, mix_norm_w, attn_out_norm_w, hgrn_out_norm_w, hgrn_lb_raw, final_norm_w, loss_target, m_w_in, m_w_out, m_mix_norm_w, m_attn_out_norm_w, m_hgrn_out_norm_w, m_hgrn_lb_raw, m_final_norm_w, v_w_in, v_w_out, v_mix_norm_w, v_attn_out_norm_w, v_hgrn_out_norm_w, v_hgrn_lb_raw, v_final_norm_w):
    tables = _rope(positions.reshape(SEQ, 1))
    proj, hn_t, w_in_g, w_out_g = _gather_project(x[0], mix_norm_w, w_in[0], w_out[0], *tables)
    grad_x, dproj_b, dwout_p, small_p = _local_step(
        x[0], proj, hn_t, w_in_g, w_out_g, tables, mix_norm_w, attn_out_norm_w, hgrn_out_norm_w,
        hgrn_lb_raw, final_norm_w, loss_target[0])
    g_in, g_out, g_s = _weights_exchange(hn_t, dproj_b, dwout_p, small_p)

    w_s = _pack_small(mix_norm_w, attn_out_norm_w, hgrn_out_norm_w, hgrn_lb_raw, final_norm_w)
    m_s = _pack_small(m_mix_norm_w, m_attn_out_norm_w, m_hgrn_out_norm_w, m_hgrn_lb_raw, m_final_norm_w)
    v_s = _pack_small(v_mix_norm_w, v_attn_out_norm_w, v_hgrn_out_norm_w, v_hgrn_lb_raw, v_final_norm_w)
    (d_in, d_out, d_s), (nm_in, nm_out, nm_s), (nv_in, nv_out, nv_s) = _adamw_update(
        (g_in, g_out, g_s), (w_in[0], w_out[0], w_s), (m_w_in[0], m_w_out[0], m_s), (v_w_in[0], v_w_out[0], v_s))

    loss = g_s[ROW_LOSS, 0]
    return (loss, grad_x[None], g_in[None], g_out[None], *_unpack_small(g_s),
            d_in[None], d_out[None], *_unpack_small(d_s),
            nm_in[None], nm_out[None], *_unpack_small(nm_s),
            nv_in[None], nv_out[None], *_unpack_small(nv_s))
```

```python
import functools

import jax
import jax.numpy as jnp
import numpy as np
from jax import lax
from jax.experimental import pallas as pl
from jax.experimental.pallas import tpu as pltpu

F32 = jnp.float32
BF16 = jnp.bfloat16

SEQ = 4096
D_MODEL = 1024
ATTN_WIDTH = 512
HGRN_WIDTH = 512
HEAD_DIM = 64
HGRN_HEADS = 4
HGRN_DIM = 128
HGRN_CHUNK = 64
N_CHUNKS = SEQ // HGRN_CHUNK
IN_COLS = 4096
COL_BLOCK = 512
N_DEV = 8
WOUT_ROWS = D_MODEL // N_DEV
ATTN_BLOCK = 128
DILATIONS = (1, 4, 16)
ROPE_THETA = 500000.0
ROPE_DIMS = 16
ROPE_HALF = 8
NORM_EPS = 1e-6
NEG_BIG = -1e30
LANES = 128

ADAM_LR = 0.001
ADAM_B1 = 0.9
ADAM_B2 = 0.999
ADAM_EPS = 1e-08
ADAM_WD = 0.01
ADAM_STEP = 10

SMALL_ROWS = 48
ROW_MIX, ROW_ATTN, ROW_HGRN, ROW_LB, ROW_FINAL, ROW_LOSS = 0, 8, 16, 24, 32, 40

VMEM_LIMIT = 56 * 1024 * 1024
MESH = pl.DeviceIdType.MESH


def _mm(a, b):
    return lax.dot_general(a, b, (((1,), (0,)), ((), ())), preferred_element_type=F32)


def _mm_nt(a, b):
    return lax.dot_general(a, b, (((1,), (1,)), ((), ())), preferred_element_type=F32)


def _mm_tn(a, b):
    return lax.dot_general(a, b, (((0,), (0,)), ((), ())), preferred_element_type=F32)


def _mm_exact(a, b):
    return lax.dot_general(a, b, (((1,), (0,)), ((), ())), preferred_element_type=F32,
                           precision=lax.Precision.HIGHEST)


def _sigmoid(v):
    return 1.0 / (1.0 + jnp.exp(-v))


def _params(sem=None, **kw):
    return pltpu.CompilerParams(dimension_semantics=sem, vmem_limit_bytes=VMEM_LIMIT, **kw)


def _my_place():
    return lax.axis_index("x"), lax.axis_index("y"), lax.axis_index("c")


def _peer(place, rel):
    x, y, c = place
    return (x ^ ((rel >> 2) & 1), y ^ ((rel >> 1) & 1), c ^ (rel & 1))


def _flat(place):
    x, y, c = place
    return 4 * x + 2 * y + c


def _rope_tables(pos_col, inv_freq_lanes):
    tm = 512

    def body(pos_ref, invf_ref, c_ref, sa_ref, sb_ref):
        ang = pos_ref[...].astype(F32) * invf_ref[...]
        e = lax.broadcasted_iota(jnp.int32, (tm, LANES), 1) & (HEAD_DIM - 1)
        cos, sin = jnp.cos(ang), jnp.sin(ang)
        c_ref[...] = jnp.where(e < ROPE_DIMS, cos, 1.0)
        sa_ref[...] = jnp.where((e >= ROPE_HALF) & (e < ROPE_DIMS), sin, 0.0)
        sb_ref[...] = jnp.where(e < ROPE_HALF, -sin, 0.0)

    tab = jax.ShapeDtypeStruct((SEQ, LANES), F32)
    spec = pl.BlockSpec((tm, LANES), lambda i: (i, 0))
    return pl.pallas_call(
        body, name="rope_tables", grid=(SEQ // tm,), out_shape=(tab, tab, tab),
        in_specs=[pl.BlockSpec((tm, 1), lambda i: (i, 0)), pl.BlockSpec((1, LANES), lambda i: (0, 0))],
        out_specs=(spec, spec, spec), compiler_params=_params(("parallel",)),
    )(pos_col, inv_freq_lanes)


def _per_slab(fn, t):
    return jnp.concatenate([fn(t[:, LANES * s:LANES * (s + 1)]) for s in range(t.shape[1] // LANES)], axis=1)


def _rot(t, c, sa, sb):
    return _per_slab(lambda u: u * c + pltpu.roll(u, ROPE_HALF, 1) * sa + pltpu.roll(u, LANES - ROPE_HALF, 1) * sb, t)


def _rot_transposed(g, c, sa, sb):
    return _per_slab(
        lambda u: u * c + pltpu.roll(u * sa, LANES - ROPE_HALF, 1) + pltpu.roll(u * sb, ROPE_HALF, 1), g)


def _gather_project(x, mix_w, w_in, w_out, rc, rsa, rsb):
    tm = 1024
    n_tiles = SEQ // tm
    arrival_of_step = (None, 0, 1, 2, 4, 5, 3, 6)

    def body(order_ref, x_ref, w_ref, win_ref, wout_ref, c_ref, sa_ref, sb_ref,
             proj_ref, hnt_ref, gin_hbm, gout_hbm, qkv_hbm,
             hn_s, w_land, wout_land, stage, sort_stage, slab_tmp, send_sems, recv_sems, local_sems):
        g, i = pl.program_id(0), pl.program_id(1)
        me = _my_place()
        x_, y_, c_ = me
        sibling = (x_, y_, 1 - c_)
        chips = [(1 - x_, y_), (x_, 1 - y_), (1 - x_, 1 - y_)]

        def slab(which, place):
            idx = _flat(place)
            if which == 0:
                return w_land.at[idx]
            return wout_land.at[pl.ds(pl.multiple_of(idx * WOUT_ROWS, WOUT_ROWS), WOUT_ROWS), :]

        def copy(which, k, block, to, src=None):
            ref = slab(which, block)
            return pltpu.make_async_remote_copy(
                src_ref=ref if src is None else src, dst_ref=ref, send_sem=send_sems.at[7 * which + k],
                recv_sem=recv_sems.at[7 * which + k], device_id=to, device_id_type=MESH)

        def first_copies(which):
            src = stage if which == 0 else None
            return ([copy(which, 0, me, sibling, src)]
                    + [copy(which, 1 + j, me, (*chip, c_), src) for j, chip in enumerate(chips)])

        def pass_on(which, j):
            return copy(which, 4 + j, (*chips[j], c_), sibling)

        def arrival(which, k):
            if k == 0:
                return copy(which, 0, sibling, me)
            if k <= 3:
                return copy(which, k, (*chips[k - 1], c_), me)
            return copy(which, k, (*chips[k - 4], 1 - c_), me)

        def to_hbm(step):
            idx = order_ref[step]
            cols = pl.ds(pl.multiple_of(idx * COL_BLOCK, COL_BLOCK), COL_BLOCK)
            return pltpu.make_async_copy(w_land.at[idx], gin_hbm.at[:, cols], local_sems.at[step])

        @pl.when((g == 0) & (i == 0))
        def _():
            stage[...] = win_ref[...].astype(BF16)
            w_land[_flat(me)] = stage[...]
            wout_land[pl.ds(pl.multiple_of(_flat(me) * WOUT_ROWS, WOUT_ROWS), WOUT_ROWS), :] = (
                wout_ref[...].astype(BF16))
            for which in (0, 1):
                for cp in first_copies(which):
                    cp.start()
            to_hbm(0).start()

        for step, k in enumerate(arrival_of_step):
            if k is None:
                continue

            @pl.when((g == step) & (i == 0))
            def _(k=k, step=step):
                arrival(0, k).wait_recv()
                to_hbm(step).start()
                if 1 <= k <= 3:
                    arrival(1, k).wait_recv()
                    pass_on(0, k - 1).start()
                    pass_on(1, k - 1).start()

        rows = pl.ds(pl.multiple_of(i * tm, tm), tm)

        @pl.when(g == 0)
        def _():
            xf = x_ref[...]
            ms = jnp.mean(xf * xf, axis=-1, keepdims=True)
            hn = xf * lax.rsqrt(ms + NORM_EPS) * w_ref[...]
            hnt_ref[...] = hn.T.astype(BF16)
            hn_s[rows, :] = hn.astype(BF16)

        group = order_ref[g]

        def sorted_copy():
            per = tm // SORT_RESIDUES
            for s in range(COL_BLOCK // LANES):
                slab_tmp[s] = proj_ref[:, LANES * s:LANES * (s + 1)]
            for r in range(SORT_RESIDUES):
                for s in range(COL_BLOCK // LANES):
                    sort_stage[r, :, LANES * s:LANES * (s + 1)] = (
                        slab_tmp.at[s][pl.ds(r, per, stride=SORT_RESIDUES), :])
            cols = pl.ds(pl.multiple_of(group * COL_BLOCK, COL_BLOCK), COL_BLOCK)
            cp = pltpu.make_async_copy(
                sort_stage, qkv_hbm.at[:, pl.ds(pl.multiple_of(i * per, per), per), cols], local_sems.at[N_DEV + 1])
            cp.start()
            cp.wait()

        @pl.when(group < 2)
        def _():
            proj_ref[...] = _rot(_mm(hn_s[rows, :], w_land[group]), c_ref[...], sa_ref[...], sb_ref[...])
            sorted_copy()

        @pl.when(group == 2)
        def _():
            proj_ref[...] = _mm(hn_s[rows, :], w_land[group])
            sorted_copy()

        @pl.when(group > 2)
        def _():
            proj_ref[...] = _mm(hn_s[rows, :], w_land[group])

        @pl.when((g == N_DEV - 1) & (i == n_tiles - 1))
        def _():
            for k in (0, 4, 5, 6):
                arrival(1, k).wait_recv()
            for which in (0, 1):
                for cp in first_copies(which) + [pass_on(which, j) for j in range(3)]:
                    cp.wait_send()
            wout_copy = pltpu.make_async_copy(wout_land, gout_hbm, local_sems.at[N_DEV])
            wout_copy.start()
            for step in range(N_DEV):
                to_hbm(step).wait()
            wout_copy.wait()

    me = _my_place()
    x_, y_, c_ = me
    chips = [(1 - x_, y_), (x_, 1 - y_), (1 - x_, 1 - y_)]
    order = jnp.stack([_flat(p) for p in (
        me, (x_, y_, 1 - c_), (*chips[0], c_), (*chips[1], c_), (*chips[0], 1 - c_), (*chips[1], 1 - c_),
        (*chips[2], c_), (*chips[2], 1 - c_))]).astype(jnp.int32)

    first_sweep = lambda g, i, order: (jnp.where(g == 0, i, n_tiles - 1), 0)
    tab = pl.BlockSpec((tm, LANES), lambda g, i, order: (jnp.where(order[g] < 2, i, 0), 0))
    whole = lambda: pl.BlockSpec(memory_space=pltpu.VMEM)
    grid_spec = pltpu.PrefetchScalarGridSpec(
        num_scalar_prefetch=1, grid=(N_DEV, n_tiles),
        in_specs=[pl.BlockSpec((tm, D_MODEL), first_sweep),
                  pl.BlockSpec((1, D_MODEL), lambda g, i, order: (0, 0)),
                  whole(), whole(), tab, tab, tab],
        out_specs=(pl.BlockSpec((tm, COL_BLOCK), lambda g, i, order: (i, order[g])),
                   pl.BlockSpec((D_MODEL, tm), lambda g, i, order: (0, jnp.where(g == 0, i, n_tiles - 1))),
                   pl.BlockSpec(memory_space=pl.ANY), pl.BlockSpec(memory_space=pl.ANY),
                   pl.BlockSpec(memory_space=pl.ANY)),
        scratch_shapes=[pltpu.VMEM((SEQ, D_MODEL), BF16),
                        pltpu.VMEM((N_DEV, D_MODEL, COL_BLOCK), BF16),
                        pltpu.VMEM((D_MODEL, D_MODEL), BF16),
                        pltpu.VMEM((D_MODEL, COL_BLOCK), BF16),
                        pltpu.VMEM((SORT_RESIDUES, tm // SORT_RESIDUES, COL_BLOCK), F32),
                        pltpu.VMEM((COL_BLOCK // LANES, tm, LANES), F32),
                        pltpu.SemaphoreType.DMA((14,)), pltpu.SemaphoreType.DMA((14,)),
                        pltpu.SemaphoreType.DMA((N_DEV + 2,))])
    proj, hn_t, w_in_g, w_out_g, qkv_sorted = pl.pallas_call(
        body, name="gather_project", grid_spec=grid_spec,
        out_shape=(jax.ShapeDtypeStruct((SEQ, IN_COLS), F32), jax.ShapeDtypeStruct((D_MODEL, SEQ), BF16),
                   jax.ShapeDtypeStruct((D_MODEL, IN_COLS), BF16), jax.ShapeDtypeStruct((D_MODEL, D_MODEL), BF16),
                   jax.ShapeDtypeStruct((SORT_RESIDUES, SORT_ROWS, 3 * COL_BLOCK), F32)),
        compiler_params=_params(("arbitrary", "arbitrary")),
    )(order, x, mix_w, w_in, w_out, rc, rsa, rsb)
    return proj, hn_t, w_in_g, w_out_g, qkv_sorted.reshape(SEQ, 3 * COL_BLOCK)


ATTN_GROUP = 8
BLOCKS_PER_PATTERN = SEQ // ATTN_BLOCK
SORT_RESIDUES = 16
SORT_ROWS = SEQ // SORT_RESIDUES


def _write_band_bias(bias_ref):
    row = lax.broadcasted_iota(jnp.int32, (2 * ATTN_BLOCK, 2 * ATTN_BLOCK), 0) & (ATTN_BLOCK - 1)
    col = lax.broadcasted_iota(jnp.int32, (2 * ATTN_BLOCK, 2 * ATTN_BLOCK), 1)
    for pi, d in enumerate(DILATIONS):
        per = SORT_RESIDUES // d
        ahead = per * (row % (8 * d) - col % (16 * d)) + (row // (8 * d) - col // (16 * d))
        dist = ATTN_BLOCK + ahead
        bias_ref[2 * pi] = jnp.where((dist >= 0) & (dist <= ATTN_BLOCK), 0.0, NEG_BIG)
        bias_ref[2 * pi + 1] = jnp.where(ahead >= 0, 0.0, NEG_BIG)


def _head0_lanes():
    return lax.broadcasted_iota(jnp.int32, (ATTN_BLOCK, LANES), 1) < HEAD_DIM


def _stack_heads(t, h0):
    return jnp.concatenate([jnp.where(h0, t, 0.0), jnp.where(h0, 0.0, t)], axis=0).astype(BF16)


def _block_runs(i, d):
    nblk = BLOCKS_PER_PATTERN // d
    r, n = i // nblk, i % nblk
    kn = jnp.maximum(n - 1, 0)
    rows, keys = [], []
    for c in range(SORT_RESIDUES // d):
        base = SORT_ROWS * (c * d + r)
        rows.append(pl.ds(pl.multiple_of(base + 8 * d * n, 8), 8 * d))
        keys.append(pl.ds(pl.multiple_of(base + 8 * d * kn, 8), 16 * d))
    return rows, keys, (n == 0).astype(jnp.int32)


def _take(ref, runs):
    return jnp.concatenate([ref[run, :] for run in runs], axis=0)


def _put(ref, runs, value, add=False):
    at = 0
    for run in runs:
        piece = value[at:at + run.size]
        if add:
            ref[run, :] += piece
        else:
            ref[run, :] = piece
        at += run.size


def _sort_rows(src_ref, dst_ref):
    for r in range(SORT_RESIDUES):
        dst_ref[SORT_ROWS * r:SORT_ROWS * (r + 1), :] = src_ref[pl.ds(r, SORT_ROWS, stride=SORT_RESIDUES), :]


def _unsort_rows(src_ref, dst_ref):
    for r in range(SORT_RESIDUES):
        dst_ref[pl.ds(r, SORT_ROWS, stride=SORT_RESIDUES), :] = src_ref[SORT_ROWS * r:SORT_ROWS * (r + 1), :]


def _for_each_group(d, load, compute, store):
    def group(g, carry):
        items = [load(*_block_runs(g * ATTN_GROUP + u, d)) for u in range(ATTN_GROUP)]
        results = [compute(item) for item in items]
        for item, res in zip(items, results):
            store(item, res)
        return carry

    lax.fori_loop(0, BLOCKS_PER_PATTERN // ATTN_GROUP, group, 0)


def _attn_fwd_fused(qkv_sorted):
    n_pat = len(DILATIONS)
    tile2 = (2 * ATTN_BLOCK, LANES)

    def body(q_ref, k_ref, v_ref, o_ref, lse_ref, o_acc, m_acc, l_acc, bias_ref):
        pl.when(pl.program_id(0) == 0)(lambda: _write_band_bias(bias_ref))
        h0 = _head0_lanes()
        for pi, d in enumerate(DILATIONS):
            first, last = pi == 0, pi == n_pat - 1

            def load(rows, keys, which, first=first, pi=pi):
                item = dict(rows=rows, keys=keys, which=2 * pi + which)
                if not first:
                    item.update(o=_take(o_acc, rows), m=[_take(m_acc.at[h], rows) for h in range(2)],
                                l=[_take(l_acc.at[h], rows) for h in range(2)])
                return item

            def compute(item, first=first):
                kb = _take(k_ref, item["keys"]).astype(BF16)
                vb = _take(v_ref, item["keys"]).astype(BF16)
                s = _mm_nt(_stack_heads(_take(q_ref, item["rows"]), h0), kb) * 0.125 + bias_ref[item["which"]]
                mb = jnp.max(s, axis=-1, keepdims=True)
                if first:
                    p = jnp.exp(s - mb)
                    mn = jnp.broadcast_to(mb, tile2)
                else:
                    m_old = jnp.concatenate(item["m"], axis=0)
                    mn = jnp.maximum(m_old, mb)
                    alpha = jnp.exp(m_old - mn)
                    p = jnp.exp(s - jnp.concatenate([mn, mn], axis=1))
                ls = jnp.sum(p, axis=-1, keepdims=True)
                pv = _mm(p.astype(BF16), vb)
                if first:
                    return pv, mn, jnp.broadcast_to(ls, tile2)
                o_old = jnp.concatenate([item["o"], item["o"]], axis=0)
                return alpha * o_old + pv, mn, alpha * jnp.concatenate(item["l"], axis=0) + ls

            def store(item, res, last=last):
                rows = item["rows"]
                (o0, o1), (m0, m1), (l0, l1) = ((a[:ATTN_BLOCK], a[ATTN_BLOCK:]) for a in res)
                if last:
                    _put(o_acc, rows, jnp.where(h0, o0 / l0, o1 / l1))
                    _put(lse_ref, rows, jnp.where(h0, m0 + jnp.log(l0), m1 + jnp.log(l1)))
                else:
                    _put(o_acc, rows, jnp.where(h0, o0, o1))
                    for h, (m, l) in enumerate(((m0, l0), (m1, l1))):
                        _put(m_acc.at[h], rows, m)
                        _put(l_acc.at[h], rows, l)

            _for_each_group(d, load, compute, store)
        _unsort_rows(o_acc, o_ref)

    slab = lambda g: pl.BlockSpec((SEQ, LANES), functools.partial(lambda hp, g: (0, 4 * g + hp), g=g))
    wide = jax.ShapeDtypeStruct((SEQ, ATTN_WIDTH), F32)
    return pl.pallas_call(
        body, name="attn_fwd", grid=(4,), out_shape=(wide, wide),
        in_specs=[slab(0), slab(1), slab(2)], out_specs=(slab(0), slab(0)),
        scratch_shapes=[pltpu.VMEM((SEQ, LANES), F32), pltpu.VMEM((2, SEQ, LANES), F32),
                        pltpu.VMEM((2, SEQ, LANES), F32),
                        pltpu.VMEM((2 * len(DILATIONS), 2 * ATTN_BLOCK, 2 * ATTN_BLOCK), F32)],
        compiler_params=_params(("arbitrary",)),
    )(qkv_sorted, qkv_sorted, qkv_sorted)


def _attn_bwd_fused(qkv_sorted, d_out, lse_sorted, delta):
    def body(q_ref, k_ref, v_ref, do_ref, lse_ref, del_ref, dq_ref, dk_ref, dv_ref,
             do_s, del_s, dq_s, dk_s, dv_s, bias_ref):
        pl.when(pl.program_id(0) == 0)(lambda: _write_band_bias(bias_ref))
        _sort_rows(do_ref, do_s)
        _sort_rows(del_ref, del_s)
        dk_s[...] = jnp.zeros_like(dk_s)
        dv_s[...] = jnp.zeros_like(dv_s)
        h0 = _head0_lanes()
        for pi, d in enumerate(DILATIONS):
            first = pi == 0

            def load(rows, keys, which, pi=pi):
                return dict(rows=rows, keys=keys, q=_take(q_ref, rows), g=_take(do_s, rows),
                            lse=_take(lse_ref, rows), delta=_take(del_s, rows),
                            k=_take(k_ref, keys).astype(BF16), v=_take(v_ref, keys).astype(BF16),
                            bias=bias_ref[2 * pi + which])

            def per_head(t):
                swapped = pltpu.roll(t, HEAD_DIM, 1)
                both = jnp.concatenate([jnp.where(h0, t, swapped), jnp.where(h0, swapped, t)], axis=0)
                return jnp.concatenate([both, both], axis=1)

            def compute(item):
                q2, g2 = _stack_heads(item["q"], h0), _stack_heads(item["g"], h0)
                s = _mm_nt(q2, item["k"]) * 0.125 + item["bias"]
                p = jnp.exp(s - per_head(item["lse"]))
                dp = _mm_nt(g2, item["v"])
                ds = (p * (dp - per_head(item["delta"])) * 0.125).astype(BF16)
                dq2 = _mm(ds, item["k"])
                dq = jnp.where(h0, dq2[:ATTN_BLOCK], dq2[ATTN_BLOCK:])
                return dq, _mm_tn(ds, q2), _mm_tn(p.astype(BF16), g2)

            def store(item, res, first=first):
                _put(dq_s, item["rows"], res[0], add=not first)
                _put(dk_s, item["keys"], res[1], add=True)
                _put(dv_s, item["keys"], res[2], add=True)

            _for_each_group(d, load, compute, store)
        _unsort_rows(dq_s, dq_ref)
        _unsort_rows(dk_s, dk_ref)
        _unsort_rows(dv_s, dv_ref)

    slab = lambda g: pl.BlockSpec((SEQ, LANES), functools.partial(lambda hp, g: (0, 4 * g + hp), g=g))
    wide = jax.ShapeDtypeStruct((SEQ, ATTN_WIDTH), F32)
    sorted_slab = pltpu.VMEM((SEQ, LANES), F32)
    return pl.pallas_call(
        body, name="attn_bwd", grid=(4,), out_shape=(wide, wide, wide),
        scratch_shapes=[sorted_slab] * 5 + [pltpu.VMEM((2 * len(DILATIONS), 2 * ATTN_BLOCK, 2 * ATTN_BLOCK), F32)],
        in_specs=[slab(0), slab(1), slab(2), slab(0), slab(0), slab(0)], out_specs=(slab(0), slab(0), slab(0)),
        compiler_params=_params(("arbitrary",)),
    )(qkv_sorted, qkv_sorted, qkv_sorted, d_out, lse_sorted, delta)


def _hgrn_lower_bound(lb_ref):
    r0, r1 = lb_ref[0:1, :], lb_ref[1:2, :]
    mx = jnp.maximum(r0, r1)
    e0, e1 = jnp.exp(r0 - mx), jnp.exp(r1 - mx)
    return e0 / (e0 + e1)


def _hgrn_gates(hq, hf, lb):
    sq = _sigmoid(hq)
    sg = _sigmoid(hf)
    f = lb + (1.0 - lb) * sg
    return hq * sq, sq, sg, f, 1.0 - f, jnp.log(f)


HGRN_PAIR = 4
HGRN_SEQ_BLOCK = 1024
HGRN_GROUP = 4
HGRN_ROWS = HGRN_GROUP * HGRN_CHUNK


def _hgrn_specs(reverse):
    n_blocks = SEQ // HGRN_SEQ_BLOCK
    width = HGRN_PAIR * HGRN_DIM
    blk = (lambda s: n_blocks - 1 - s) if reverse else (lambda s: s)
    cols = lambda g: pl.BlockSpec((HGRN_SEQ_BLOCK, width),
                                  functools.partial(lambda p, s, g: (blk(s), (HGRN_HEADS // HGRN_PAIR) * g + p), g=g))
    pair = pl.BlockSpec((HGRN_SEQ_BLOCK, width), lambda p, s: (blk(s), p))
    lb = pl.BlockSpec((2, width), lambda p, s: (0, p))
    states = pl.BlockSpec((HGRN_PAIR, HGRN_SEQ_BLOCK // HGRN_CHUNK, HGRN_DIM, HGRN_DIM),
                          lambda p, s: (p, blk(s), 0, 0))
    return cols, pair, lb, states


def _chunk_masks():
    ri = lax.broadcasted_iota(jnp.int32, (HGRN_ROWS, HGRN_ROWS), 0)
    ci = lax.broadcasted_iota(jnp.int32, (HGRN_ROWS, HGRN_ROWS), 1)
    same = (ri // HGRN_CHUNK) == (ci // HGRN_CHUNK)
    return same, same & (ri >= ci), same & (ri <= ci)


def _mm_select(sel, v):
    hi = v.astype(BF16)
    r1 = v - hi.astype(F32)
    mid = r1.astype(BF16)
    lo = (r1 - mid.astype(F32)).astype(BF16)
    return _mm(sel, hi) + _mm(sel, mid) + _mm(sel, lo)


def _head_cols(a, h):
    return a[:, HGRN_DIM * h:HGRN_DIM * (h + 1)]


def _hgrn_fwd(proj, lb_raw):
    t, rws = HGRN_CHUNK, HGRN_ROWS

    def body(hq_ref, hf_ref, hi_ref, lb_ref, rec_ref, st_ref, state):
        @pl.when(pl.program_id(1) == 0)
        def _():
            state[...] = jnp.zeros_like(state)

        lb = _hgrn_lower_bound(lb_ref)
        same, causal, _ = _chunk_masks()
        sel = jnp.concatenate([causal, same], axis=0).astype(BF16)

        def group(g, sts):
            rows = pl.ds(pl.multiple_of(g * rws, rws), rws)
            q, _, _, _, k, lf = _hgrn_gates(hq_ref[rows, :], hf_ref[rows, :], lb)
            sums = _mm_select(sel, lf)
            cum, last = sums[:rws], sums[rws:]
            qd = (q * jnp.exp(cum)).astype(BF16)
            ki = (k * jnp.exp(-cum)).astype(BF16)
            ke = (k * jnp.exp(last - cum)).astype(BF16)
            vb = hi_ref[rows, :].astype(BF16)
            dec = jnp.exp(last)
            new_sts, recs = [], []
            for h in range(HGRN_PAIR):
                qd_h, ke_h, vb_h = _head_cols(qd, h), _head_cols(ke, h), _head_cols(vb, h)
                att = jnp.where(causal, _mm_nt(qd_h, _head_cols(ki, h)), 0.0).astype(BF16)
                intra = _mm(att, vb_h)
                st = sts[h]
                outs = []
                for c in range(HGRN_GROUP):
                    sl = slice(c * t, (c + 1) * t)
                    st_ref[h, g * HGRN_GROUP + c] = st
                    outs.append(intra[sl] + _mm_nt(qd_h[sl], st.astype(BF16)))
                    st = st * _head_cols(dec[c * t:c * t + 1, :], h) + _mm_tn(vb_h[sl], ke_h[sl])
                new_sts.append(st)
                recs.append(jnp.concatenate(outs, axis=0))
            rec_ref[rows, :] = jnp.concatenate(recs, axis=1)
            return tuple(new_sts)

        sts = lax.fori_loop(0, HGRN_SEQ_BLOCK // rws, group, tuple(state[h] for h in range(HGRN_PAIR)))
        for h in range(HGRN_PAIR):
            state[h] = sts[h]

    cols, pair, lb, states = _hgrn_specs(reverse=False)
    return pl.pallas_call(
        body, name="hgrn_fwd", grid=(HGRN_HEADS // HGRN_PAIR, SEQ // HGRN_SEQ_BLOCK),
        out_shape=(jax.ShapeDtypeStruct((SEQ, HGRN_WIDTH), F32),
                   jax.ShapeDtypeStruct((HGRN_HEADS, N_CHUNKS, HGRN_DIM, HGRN_DIM), F32)),
        in_specs=[cols(4), cols(5), cols(6), lb], out_specs=(pair, states),
        scratch_shapes=[pltpu.VMEM((HGRN_PAIR, HGRN_DIM, HGRN_DIM), F32)],
        compiler_params=_params(("parallel", "arbitrary")),
    )(proj, proj, proj, lb_raw)


def _hgrn_bwd(proj, lb_raw, d_rec, states):
    t, rws = HGRN_CHUNK, HGRN_ROWS

    def body(hq_ref, hf_ref, hi_ref, lb_ref, do_ref, st_ref, dhq_ref, dhf_ref, dhi_ref, dlb_ref,
             dstate, dlb_acc):
        lb = _hgrn_lower_bound(lb_ref)
        same, causal, anti = _chunk_masks()
        sel = jnp.concatenate([causal, same], axis=0).astype(BF16)
        sel_t = jnp.concatenate([anti, same], axis=1).astype(BF16)
        @pl.when(pl.program_id(1) == 0)
        def _():
            dstate[...] = jnp.zeros_like(dstate)
            dlb_acc[...] = jnp.zeros_like(dlb_acc)

        n_groups = HGRN_SEQ_BLOCK // rws
        chunks = [slice(c * t, (c + 1) * t) for c in range(HGRN_GROUP)]

        def group(i, dsts_in):
            g = n_groups - 1 - i
            rows = pl.ds(pl.multiple_of(g * rws, rws), rws)
            hq = hq_ref[rows, :]
            q, sq, sg, f, k, lf = _hgrn_gates(hq, hf_ref[rows, :], lb)
            sums = _mm_select(sel, lf)
            cum, last = sums[:rws], sums[rws:]
            e_cum, e_inv, e_end, dec = jnp.exp(cum), jnp.exp(-cum), jnp.exp(last - cum), jnp.exp(last)
            qd, ki, ke = q * e_cum, k * e_inv, k * e_end
            qdb, kib, keb = qd.astype(BF16), ki.astype(BF16), ke.astype(BF16)
            vb = hi_ref[rows, :].astype(BF16)
            gb = do_ref[rows, :].astype(BF16)

            dsts_out, per_head = [], []
            for h in range(HGRN_PAIR):
                qdb_h, kib_h, keb_h = _head_cols(qdb, h), _head_cols(kib, h), _head_cols(keb, h)
                vb_h, gb_h = _head_cols(vb, h), _head_cols(gb, h)
                att = jnp.where(causal, _mm_nt(qdb_h, kib_h), 0.0).astype(BF16)
                datt = jnp.where(causal, _mm_nt(gb_h, vb_h), 0.0).astype(BF16)
                dv = _mm_tn(att, gb_h)
                dqd = _mm(datt, kib_h)
                dki = _mm_tn(datt, qdb_h)

                decs = [_head_cols(dec[c * t:c * t + 1, :], h) for c in range(HGRN_GROUP)]
                dsts = [None] * HGRN_GROUP
                dst = dsts_in[h]
                for c in reversed(range(HGRN_GROUP)):
                    dsts[c] = dst
                    dst = dst * decs[c] + _mm_tn(gb_h[chunks[c]], qdb_h[chunks[c]])
                dsts_out.append(dst)

                dv_x, dqd_x, dke, dlast_x = [], [], [], []
                for c, sl in enumerate(chunks):
                    st_prev = st_ref[h, g * HGRN_GROUP + c]
                    dstb = dsts[c].astype(BF16)
                    dv_x.append(_mm_nt(keb_h[sl], dstb))
                    dqd_x.append(_mm(gb_h[sl], st_prev.astype(BF16)))
                    dke.append(_mm(vb_h[sl], dstb))
                    ddec = jnp.sum(dsts[c] * st_prev, axis=0, keepdims=True)
                    dlast_x.append(jnp.broadcast_to(ddec * decs[c], (t, HGRN_DIM)))
                per_head.append((dv + jnp.concatenate(dv_x, axis=0), dqd + jnp.concatenate(dqd_x, axis=0),
                                 dki, jnp.concatenate(dke, axis=0), jnp.concatenate(dlast_x, axis=0)))
            dv, dqd, dki, dke, dlast = (jnp.concatenate(list(parts), axis=1) for parts in zip(*per_head))

            dq = dqd * e_cum
            dk = dki * e_inv + dke * e_end
            dke_ke = dke * ke
            dcum = dqd * qd - dki * ki - dke_ke
            dlf = _mm_select(sel_t, jnp.concatenate([dcum, dke_ke], axis=0)) + dlast
            df = dlf / f - dk
            dhq_ref[rows, :] = dq * (sq * (1.0 + hq * (1.0 - sq)))
            dhf_ref[rows, :] = df * (1.0 - lb) * (sg * (1.0 - sg))
            dhi_ref[rows, :] = dv
            dlb_acc[...] += jnp.sum(df * (1.0 - sg), axis=0, keepdims=True)
            return tuple(dsts_out)

        dsts = lax.fori_loop(0, n_groups, group, tuple(dstate[h] for h in range(HGRN_PAIR)))
        for h in range(HGRN_PAIR):
            dstate[h] = dsts[h]
        g0 = dlb_acc[...] * lb * (1.0 - lb)
        dlb_ref[...] = jnp.concatenate([g0, -g0], axis=0)

    cols, pair, lb_spec, st_spec = _hgrn_specs(reverse=True)
    wide = jax.ShapeDtypeStruct((SEQ, HGRN_WIDTH), F32)
    return pl.pallas_call(
        body, name="hgrn_bwd", grid=(HGRN_HEADS // HGRN_PAIR, SEQ // HGRN_SEQ_BLOCK),
        out_shape=(wide, wide, wide, jax.ShapeDtypeStruct((2, HGRN_WIDTH), F32)),
        in_specs=[cols(4), cols(5), cols(6), lb_spec, pair, st_spec],
        out_specs=(pair, pair, pair, lb_spec),
        scratch_shapes=[pltpu.VMEM((HGRN_PAIR, HGRN_DIM, HGRN_DIM), F32),
                        pltpu.VMEM((1, HGRN_PAIR * HGRN_DIM), F32)],
        compiler_params=_params(("parallel", "arbitrary")),
    )(proj, proj, proj, lb_raw, d_rec, states)


def _group_sum(v, group):
    parts = []
    for s in range(v.shape[1] // LANES):
        slab = v[:, LANES * s:LANES * (s + 1)]
        if group == LANES:
            parts.append(jnp.broadcast_to(jnp.sum(slab, axis=-1, keepdims=True), slab.shape))
        else:
            h0 = lax.broadcasted_iota(jnp.int32, slab.shape, 1) < HEAD_DIM
            s0 = jnp.sum(jnp.where(h0, slab, 0.0), axis=-1, keepdims=True)
            s1 = jnp.sum(jnp.where(h0, 0.0, slab), axis=-1, keepdims=True)
            parts.append(jnp.where(h0, s0, s1))
    return jnp.concatenate(parts, axis=1)


def _mid(attn_o, rec, proj, x, target, w_out_g, attn_w, hgrn_w, final_w):
    tm = 256

    def branch_fwd(o, gate, w, group):
        r = lax.rsqrt(_group_sum(o * o, group) * (1.0 / group) + NORM_EPS)
        nrm = o * r
        sg = _sigmoid(gate)
        return r, nrm, sg, nrm * w * (gate * sg)

    def branch_bwd(dy, r, nrm, sg, gate, w, group):
        silu = gate * sg
        d_gate = dy * nrm * w * (sg * (1.0 + gate * (1.0 - sg)))
        d_w = jnp.sum(dy * nrm * silu, axis=0, keepdims=True)
        dn = dy * w * silu
        d_o = r * (dn - nrm * (_group_sum(dn * nrm, group) * (1.0 / group)))
        return d_o, d_gate, d_w

    def body(o_ref, rec_ref, ag_ref, hg_ref, x_ref, tgt_ref, wout_ref, aw_ref, hw_ref, fw_ref,
             dx2_ref, do_ref, delta_ref, dag_ref, drec_ref, dhg_ref, dwout_ref, dfw_ref, daw_ref, dhw_ref,
             loss_ref, dwout_acc):
        i = pl.program_id(0)

        @pl.when(i == 0)
        def _():
            dwout_acc[...] = jnp.zeros_like(dwout_acc)
            dfw_ref[...] = jnp.zeros_like(dfw_ref)
            daw_ref[...] = jnp.zeros_like(daw_ref)
            dhw_ref[...] = jnp.zeros_like(dhw_ref)
            loss_ref[...] = jnp.zeros_like(loss_ref)

        o, rc, ag, hg = o_ref[...], rec_ref[...], ag_ref[...], hg_ref[...]
        aw, hw, fw = aw_ref[...], hw_ref[...], fw_ref[...]
        ra, na, sga, ya = branch_fwd(o, ag, aw, HEAD_DIM)
        rh, nh, sgh, yh = branch_fwd(rc, hg, hw, HGRN_DIM)
        mixed = jnp.concatenate([ya, yh], axis=1).astype(BF16)
        wout = wout_ref[...]
        x2 = x_ref[...] + _mm(mixed, wout)
        rstd = lax.rsqrt(jnp.mean(x2 * x2, axis=-1, keepdims=True) + NORM_EPS)
        xn = x2 * rstd
        err = xn * fw - tgt_ref[...]
        row_loss = jnp.mean(err * err, axis=-1, keepdims=True)
        loss_ref[...] += 0.5 * jnp.sum(row_loss, axis=0, keepdims=True)
        dy = err * (1.0 / D_MODEL)
        dfw_ref[...] += jnp.sum(dy * xn, axis=0, keepdims=True)
        dxn = dy * fw
        dx2 = rstd * (dxn - xn * jnp.mean(dxn * xn, axis=-1, keepdims=True))
        dx2_ref[...] = dx2
        dx2b = dx2.astype(BF16)
        dwout_acc[...] += _mm_tn(mixed, dx2b)

        @pl.when(i == pl.num_programs(0) - 1)
        def _():
            dwout_ref[...] = dwout_acc[...].astype(BF16)

        dmixed = _mm_nt(dx2b, wout)

        d_o, d_ag, d_aw = branch_bwd(dmixed[:, :ATTN_WIDTH], ra, na, sga, ag, aw, HEAD_DIM)
        d_rec, d_hg, d_hw = branch_bwd(dmixed[:, ATTN_WIDTH:], rh, nh, sgh, hg, hw, HGRN_DIM)
        do_ref[...] = d_o
        delta_ref[...] = _group_sum(d_o * o, HEAD_DIM)
        dag_ref[...] = d_ag
        drec_ref[...] = d_rec
        dhg_ref[...] = d_hg
        daw_ref[...] += d_aw
        dhw_ref[...] += d_hw

    half = lambda: pl.BlockSpec((tm, COL_BLOCK), lambda i: (i, 0))
    full = lambda: pl.BlockSpec((tm, D_MODEL), lambda i: (i, 0))
    fixed = lambda r, c: pl.BlockSpec((r, c), lambda i: (0, 0))
    wide = jax.ShapeDtypeStruct((SEQ, COL_BLOCK), F32)
    return pl.pallas_call(
        body, name="mid", grid=(SEQ // tm,),
        out_shape=(jax.ShapeDtypeStruct((SEQ, D_MODEL), F32), wide, wide, wide, wide, wide,
                   jax.ShapeDtypeStruct((D_MODEL, D_MODEL), BF16),
                   jax.ShapeDtypeStruct((1, D_MODEL), F32), jax.ShapeDtypeStruct((1, COL_BLOCK), F32),
                   jax.ShapeDtypeStruct((1, COL_BLOCK), F32), jax.ShapeDtypeStruct((1, 1), F32)),
        scratch_shapes=[pltpu.VMEM((D_MODEL, D_MODEL), F32)],
        in_specs=[half(), half(),
                  pl.BlockSpec((tm, COL_BLOCK), lambda i: (i, 3)), pl.BlockSpec((tm, COL_BLOCK), lambda i: (i, 7)),
                  full(), full(), fixed(D_MODEL, D_MODEL), fixed(1, COL_BLOCK), fixed(1, COL_BLOCK),
                  fixed(1, D_MODEL)],
        out_specs=(full(), half(), half(), half(), half(), half(), fixed(D_MODEL, D_MODEL),
                   fixed(1, D_MODEL), fixed(1, COL_BLOCK), fixed(1, COL_BLOCK), fixed(1, 1)),
        compiler_params=_params(("arbitrary",)),
    )(attn_o, rec, proj, proj, x, target, w_out_g, attn_w, hgrn_w, final_w)


def _in_proj_bwd_rows(d_groups, w_g, x, dx2, mix_w, rc, rsa, rsb):
    tm = 256

    def body(*refs):
        dg_refs = refs[:N_DEV]
        wg_ref, x_ref, dx2_ref, w_ref, c_ref, sa_ref, sb_ref, gx_ref, dpb_ref, dmw_ref = refs[N_DEV:]

        @pl.when(pl.program_id(0) == 0)
        def _():
            dmw_ref[...] = jnp.zeros_like(dmw_ref)

        parts = []
        for j in range(N_DEV):
            dp = dg_refs[j][...]
            if j < 2:
                dp = _rot_transposed(dp, c_ref[...], sa_ref[...], sb_ref[...])
            parts.append(dp.astype(BF16))
        dpb = jnp.concatenate(parts, axis=1)
        dpb_ref[...] = dpb
        g = _mm_nt(dpb, wg_ref[...])
        xf = x_ref[...]
        rstd = lax.rsqrt(jnp.mean(xf * xf, axis=-1, keepdims=True) + NORM_EPS)
        xn = xf * rstd
        dmw_ref[...] += jnp.sum(g * xn, axis=0, keepdims=True)
        gw = g * w_ref[...]
        gx_ref[...] = dx2_ref[...] + rstd * (gw - xn * jnp.mean(gw * xn, axis=-1, keepdims=True))

    tile = lambda cols: pl.BlockSpec((tm, cols), lambda i: (i, 0))
    fixed = lambda r, c: pl.BlockSpec((r, c), lambda i: (0, 0))
    return pl.pallas_call(
        body, name="in_proj_bwd_rows", grid=(SEQ // tm,),
        out_shape=(jax.ShapeDtypeStruct((SEQ, D_MODEL), F32), jax.ShapeDtypeStruct((SEQ, IN_COLS), BF16),
                   jax.ShapeDtypeStruct((1, D_MODEL), F32)),
        in_specs=[tile(COL_BLOCK) for _ in range(N_DEV)] + [
            fixed(D_MODEL, IN_COLS), tile(D_MODEL), tile(D_MODEL), fixed(1, D_MODEL),
            tile(LANES), tile(LANES), tile(LANES)],
        out_specs=(tile(D_MODEL), tile(IN_COLS), fixed(1, D_MODEL)),
        compiler_params=_params(("arbitrary",)),
    )(*d_groups, w_g, x, dx2, mix_w, rc, rsa, rsb)


def _weights_exchange(hn_t, dproj_b, dwout_p, small_p):
    n_chips = N_DEV // 2
    rb = 128
    S1_IN, S1_OUT, SMALL, S2_IN, S2_OUT = 0, 4, 8, 15, 18
    rel_of_pair = (1, 2, 3, 0)

    def body(order_ref, hnt_ref, dp_ref, dwout_ref, small_ref, gin_ref, gout_ref, gs_ref,
             part, s1_send, s1_in, s1_out, fwd_in, fwd_out, s2_in, s2_out, land_s, send_sems, recv_sems):
        t = pl.program_id(0)
        me = _my_place()
        x, y, c = me
        my_chip = 2 * x + y
        sibling = (x, y, 1 - c)

        def remote(slot, src, dst, to):
            return pltpu.make_async_remote_copy(src_ref=src, dst_ref=dst, send_sem=send_sems.at[slot],
                                                recv_sem=recv_sems.at[slot], device_id=to, device_id_type=MESH)

        def s1_in_copy(pair):
            return remote(S1_IN + pair, s1_send.at[pair], s1_in.at[pair], sibling)

        def s1_out_copy(pair):
            q = my_chip ^ rel_of_pair[pair]
            return remote(S1_OUT + pair, dwout_ref.at[q, 1 - c], s1_out.at[pair], sibling)

        def s2_copies(rel):
            peer = _peer(me, 2 * rel)
            return [remote(S2_IN + rel - 1, fwd_in.at[rel - 1], s2_in.at[rel - 1], peer),
                    remote(S2_OUT + rel - 1, fwd_out.at[rel - 1], s2_out.at[rel - 1], peer)]

        def small_copy(rel):
            return remote(SMALL + rel - 1, small_ref, land_s.at[rel], _peer(me, rel))

        @pl.when(t == 0)
        def _():
            land_s[0] = small_ref[...]
            for pair in range(n_chips):
                s1_out_copy(pair).start()
            for rel in range(1, N_DEV):
                small_copy(rel).start()

        part[...] = _mm(hnt_ref[...], dp_ref[...])

        def rows_loop(n_rows, fn):
            def step(b, carry):
                fn(pl.ds(pl.multiple_of(b * rb, rb), rb))
                return carry
            lax.fori_loop(0, n_rows // rb, step, 0)

        for pair, rel in enumerate(rel_of_pair):
            @pl.when(t == 2 * pair)
            def _(pair=pair):
                s1_send[pair] = part[...].astype(BF16)
                s1_in_copy(pair).start()

            @pl.when(t == 2 * pair + 1)
            def _(pair=pair, rel=rel):
                q = my_chip ^ rel
                s1_in_copy(pair).wait_recv()
                s1_out_copy(pair).wait_recv()
                dst_in = fwd_in.at[rel - 1] if rel else gin_ref
                dst_out = fwd_out.at[rel - 1] if rel else gout_ref

                def add_in(rows):
                    dst_in[rows, :] = (part[rows, :] + s1_in[pair, rows, :].astype(F32)).astype(dst_in.dtype)

                def add_out(rows):
                    dst_out[rows, :] = (dwout_ref[q, c, rows, :].astype(F32)
                                        + s1_out[pair, rows, :].astype(F32)).astype(dst_out.dtype)

                rows_loop(D_MODEL, add_in)
                rows_loop(WOUT_ROWS, add_out)
                if rel:
                    for cp in s2_copies(rel):
                        cp.start()

        @pl.when(t == N_DEV - 1)
        def _():
            for rel in range(1, n_chips):
                for cp in s2_copies(rel):
                    cp.wait_recv()

            def total_in(rows):
                g = gin_ref[rows, :]
                for rel in range(1, n_chips):
                    g = g + s2_in[rel - 1, rows, :].astype(F32)
                gin_ref[rows, :] = g

            def total_out(rows):
                g = gout_ref[rows, :]
                for rel in range(1, n_chips):
                    g = g + s2_out[rel - 1, rows, :].astype(F32)
                gout_ref[rows, :] = g

            rows_loop(D_MODEL, total_in)
            rows_loop(WOUT_ROWS, total_out)

            for rel in range(1, N_DEV):
                small_copy(rel).wait_recv()
            my_flat = _flat(me)
            g = land_s[my_flat ^ 0]
            for dev in range(1, N_DEV):
                g = g + land_s[my_flat ^ dev]
            gs_ref[...] = g

            for pair in range(n_chips):
                s1_in_copy(pair).wait_send()
                s1_out_copy(pair).wait_send()
            for rel in range(1, n_chips):
                for cp in s2_copies(rel):
                    cp.wait_send()
            for rel in range(1, N_DEV):
                small_copy(rel).wait_send()

    me = _my_place()
    x, y, c = me
    my_chip = 2 * x + y
    order = jnp.stack([2 * (my_chip ^ rel) + core for rel in rel_of_pair for core in (1 - c, c)]).astype(jnp.int32)

    whole = lambda: pl.BlockSpec(memory_space=pltpu.VMEM)
    in_blocks = lambda n: pltpu.VMEM((n, D_MODEL, COL_BLOCK), BF16)
    out_blocks = lambda n: pltpu.VMEM((n, WOUT_ROWS, D_MODEL), BF16)
    grid_spec = pltpu.PrefetchScalarGridSpec(
        num_scalar_prefetch=1, grid=(N_DEV,),
        in_specs=[pl.BlockSpec((D_MODEL, SEQ), lambda t, order: (0, 0)),
                  pl.BlockSpec((SEQ, COL_BLOCK), lambda t, order: (0, order[t])), whole(), whole()],
        out_specs=(whole(), whole(), whole()),
        scratch_shapes=[pltpu.VMEM((D_MODEL, COL_BLOCK), F32), in_blocks(n_chips), in_blocks(n_chips),
                        out_blocks(n_chips), in_blocks(n_chips - 1), out_blocks(n_chips - 1),
                        in_blocks(n_chips - 1), out_blocks(n_chips - 1),
                        pltpu.VMEM((N_DEV, SMALL_ROWS, LANES), F32),
                        pltpu.SemaphoreType.DMA((21,)), pltpu.SemaphoreType.DMA((21,))])
    return pl.pallas_call(
        body, name="weights_exchange", grid_spec=grid_spec,
        out_shape=(jax.ShapeDtypeStruct((D_MODEL, COL_BLOCK), F32), jax.ShapeDtypeStruct((WOUT_ROWS, D_MODEL), F32),
                   jax.ShapeDtypeStruct((SMALL_ROWS, LANES), F32)),
        compiler_params=_params(("arbitrary",)),
    )(order, hn_t, dproj_b, dwout_p.reshape(n_chips, 2, WOUT_ROWS, D_MODEL), small_p)


def _adamw(w, g, m, v):
    m = ADAM_B1 * m + (1.0 - ADAM_B1) * g
    v = ADAM_B2 * v + (1.0 - ADAM_B2) * (g * g)
    m_hat = m / (1.0 - ADAM_B1 ** ADAM_STEP)
    v_hat = v / (1.0 - ADAM_B2 ** ADAM_STEP)
    delta = -ADAM_LR * (m_hat / (jnp.sqrt(v_hat) + ADAM_EPS) + ADAM_WD * w)
    return delta, m, v


def _adamw_update(grads, weights, m_old, v_old):
    rb = 256

    def body(*refs):
        g_refs, w_refs, m_refs, v_refs = refs[0:3], refs[3:6], refs[6:9], refs[9:12]
        d_refs, nm_refs, nv_refs = refs[12:15], refs[15:18], refs[18:21]
        for k in range(3):
            n_rows = g_refs[k].shape[0]
            step_rows = min(rb, n_rows)

            def step(b, carry, k=k, step_rows=step_rows):
                rows = pl.ds(pl.multiple_of(b * step_rows, 8), step_rows)
                delta, nm, nv = _adamw(w_refs[k][rows, :], g_refs[k][rows, :], m_refs[k][rows, :], v_refs[k][rows, :])
                d_refs[k][rows, :] = delta
                nm_refs[k][rows, :] = nm
                nv_refs[k][rows, :] = nv
                return carry

            lax.fori_loop(0, n_rows // step_rows, step, 0)

    shapes = tuple(jax.ShapeDtypeStruct(g.shape, F32) for g in grads)
    vm = lambda: pl.BlockSpec(memory_space=pltpu.VMEM)
    outs = pl.pallas_call(
        body, name="adamw_update", out_shape=shapes * 3,
        in_specs=[vm() for _ in range(12)], out_specs=tuple(vm() for _ in range(9)),
        compiler_params=_params(),
    )(*grads, *weights, *m_old, *v_old)
    return outs[0:3], outs[3:6], outs[6:9]


def _pack_small(mix, attn, hgrn, lb, final, loss=None):
    def rows8(a):
        a = a.reshape(-1, LANES)
        return jnp.pad(a, ((0, 8 - a.shape[0]), (0, 0)))
    last = jnp.zeros((8, LANES), F32) if loss is None else jnp.pad(loss.reshape(1, 1), ((0, 7), (0, LANES - 1)))
    return jnp.concatenate([rows8(mix), rows8(attn), rows8(hgrn), rows8(lb), rows8(final), last], axis=0)


def _unpack_small(slab):
    return (slab[ROW_MIX:ROW_MIX + 8].reshape(1, D_MODEL), slab[ROW_ATTN:ROW_ATTN + 4].reshape(1, ATTN_WIDTH),
            slab[ROW_HGRN:ROW_HGRN + 4].reshape(1, HGRN_WIDTH), slab[ROW_LB:ROW_LB + 8].reshape(2, HGRN_WIDTH),
            slab[ROW_FINAL:ROW_FINAL + 8].reshape(D_MODEL))


def _rope(pos_col):
    lane_e = np.arange(LANES) % HEAD_DIM
    inv = ROPE_THETA ** (-(lane_e % ROPE_HALF) * (2.0 / ROPE_DIMS))
    inv_lanes = np.where(lane_e < ROPE_DIMS, inv, 0.0).astype(np.float32).reshape(1, LANES)
    return _rope_tables(pos_col, jnp.asarray(inv_lanes))


def _local_step(x, proj, qkv_sorted, w_in_g, w_out_g, tables, mix_w, attn_w, hgrn_w, lb_raw, final_w, target):
    rc, rsa, rsb = tables
    attn_o, lse = _attn_fwd_fused(qkv_sorted)
    rec, states = _hgrn_fwd(proj, lb_raw)

    (dx2, d_o, delta, d_ag, d_rec, d_hg, dwout_p, d_final, d_attn_w, d_hgrn_w, loss) = _mid(
        attn_o, rec, proj, x, target, w_out_g, attn_w, hgrn_w, final_w.reshape(1, D_MODEL))

    dqkv = _attn_bwd_fused(qkv_sorted, d_o, lse, delta)
    d_hq, d_hf, d_hi, d_lb = _hgrn_bwd(proj, lb_raw, d_rec, states)

    grad_x, dproj_b, d_mix = _in_proj_bwd_rows(
        (dqkv[0], dqkv[1], dqkv[2], d_ag, d_hq, d_hf, d_hi, d_hg), w_in_g, x, dx2, mix_w, rc, rsa, rsb)
    small_p = _pack_small(d_mix, d_attn_w, d_hgrn_w, d_lb, d_final, loss)
    return grad_x, dproj_b, dwout_p, small_p


def kernel(x, positions, w_in, w_out, mix_norm_w, attn_out_norm_w, hgrn_out_norm_w, hgrn_lb_raw, final_norm_w, loss_target, m_w_in, m_w_out, m_mix_norm_w, m_attn_out_norm_w, m_hgrn_out_norm_w, m_hgrn_lb_raw, m_final_norm_w, v_w_in, v_w_out, v_mix_norm_w, v_attn_out_norm_w, v_hgrn_out_norm_w, v_hgrn_lb_raw, v_final_norm_w):
    tables = _rope(positions.reshape(SEQ, 1))
    proj, hn_t, w_in_g, w_out_g, qkv_sorted = _gather_project(x[0], mix_norm_w, w_in[0], w_out[0], *tables)
    grad_x, dproj_b, dwout_p, small_p = _local_step(
        x[0], proj, qkv_sorted, w_in_g, w_out_g, tables, mix_norm_w, attn_out_norm_w, hgrn_out_norm_w,
        hgrn_lb_raw, final_norm_w, loss_target[0])
    g_in, g_out, g_s = _weights_exchange(hn_t, dproj_b, dwout_p, small_p)

    w_s = _pack_small(mix_norm_w, attn_out_norm_w, hgrn_out_norm_w, hgrn_lb_raw, final_norm_w)
    m_s = _pack_small(m_mix_norm_w, m_attn_out_norm_w, m_hgrn_out_norm_w, m_hgrn_lb_raw, m_final_norm_w)
    v_s = _pack_small(v_mix_norm_w, v_attn_out_norm_w, v_hgrn_out_norm_w, v_hgrn_lb_raw, v_final_norm_w)
    (d_in, d_out, d_s), (nm_in, nm_out, nm_s), (nv_in, nv_out, nv_s) = _adamw_update(
        (g_in, g_out, g_s), (w_in[0], w_out[0], w_s), (m_w_in[0], m_w_out[0], m_s), (v_w_in[0], v_w_out[0], v_s))

    loss = g_s[ROW_LOSS, 0]
    return (loss, grad_x[None], g_in[None], g_out[None], *_unpack_small(g_s),
            d_in[None], d_out[None], *_unpack_small(d_s),
            nm_in[None], nm_out[None], *_unpack_small(nm_s),
            nv_in[None], nv_out[None], *_unpack_small(nv_s))
```

```python
import functools

import jax
import jax.numpy as jnp
import numpy as np
from jax import lax
from jax.experimental import pallas as pl
from jax.experimental.pallas import tpu as pltpu

F32 = jnp.float32
BF16 = jnp.bfloat16

SEQ = 4096
D_MODEL = 1024
ATTN_WIDTH = 512
HGRN_WIDTH = 512
HEAD_DIM = 64
HGRN_HEADS = 4
HGRN_DIM = 128
HGRN_CHUNK = 64
N_CHUNKS = SEQ // HGRN_CHUNK
IN_COLS = 4096
COL_BLOCK = 512
N_DEV = 8
WOUT_ROWS = D_MODEL // N_DEV
ATTN_BLOCK = 128
DILATIONS = (1, 4, 16)
ROPE_THETA = 500000.0
ROPE_DIMS = 16
ROPE_HALF = 8
NORM_EPS = 1e-6
NEG_BIG = -1e30
LANES = 128

ADAM_LR = 0.001
ADAM_B1 = 0.9
ADAM_B2 = 0.999
ADAM_EPS = 1e-08
ADAM_WD = 0.01
ADAM_STEP = 10

SMALL_ROWS = 48
ROW_MIX, ROW_ATTN, ROW_HGRN, ROW_LB, ROW_FINAL, ROW_LOSS = 0, 8, 16, 24, 32, 40

VMEM_LIMIT = 56 * 1024 * 1024
MESH = pl.DeviceIdType.MESH


def _mm(a, b):
    return lax.dot_general(a, b, (((1,), (0,)), ((), ())), preferred_element_type=F32)


def _mm_nt(a, b):
    return lax.dot_general(a, b, (((1,), (1,)), ((), ())), preferred_element_type=F32)


def _mm_tn(a, b):
    return lax.dot_general(a, b, (((0,), (0,)), ((), ())), preferred_element_type=F32)


def _mm_exact(a, b):
    return lax.dot_general(a, b, (((1,), (0,)), ((), ())), preferred_element_type=F32,
                           precision=lax.Precision.HIGHEST)


def _sigmoid(v):
    return 1.0 / (1.0 + jnp.exp(-v))


def _params(sem=None, **kw):
    return pltpu.CompilerParams(dimension_semantics=sem, vmem_limit_bytes=VMEM_LIMIT, **kw)


def _my_place():
    return lax.axis_index("x"), lax.axis_index("y"), lax.axis_index("c")


def _peer(place, rel):
    x, y, c = place
    return (x ^ ((rel >> 2) & 1), y ^ ((rel >> 1) & 1), c ^ (rel & 1))


def _flat(place):
    x, y, c = place
    return 4 * x + 2 * y + c


def _rope_tables(pos_col, inv_freq_lanes):
    tm = 512

    def body(pos_ref, invf_ref, c_ref, sa_ref, sb_ref):
        ang = pos_ref[...].astype(F32) * invf_ref[...]
        e = lax.broadcasted_iota(jnp.int32, (tm, LANES), 1) & (HEAD_DIM - 1)
        cos, sin = jnp.cos(ang), jnp.sin(ang)
        c_ref[...] = jnp.where(e < ROPE_DIMS, cos, 1.0)
        sa_ref[...] = jnp.where((e >= ROPE_HALF) & (e < ROPE_DIMS), sin, 0.0)
        sb_ref[...] = jnp.where(e < ROPE_HALF, -sin, 0.0)

    tab = jax.ShapeDtypeStruct((SEQ, LANES), F32)
    spec = pl.BlockSpec((tm, LANES), lambda i: (i, 0))
    return pl.pallas_call(
        body, name="rope_tables", grid=(SEQ // tm,), out_shape=(tab, tab, tab),
        in_specs=[pl.BlockSpec((tm, 1), lambda i: (i, 0)), pl.BlockSpec((1, LANES), lambda i: (0, 0))],
        out_specs=(spec, spec, spec), compiler_params=_params(("parallel",)),
    )(pos_col, inv_freq_lanes)


def _per_slab(fn, t):
    return jnp.concatenate([fn(t[:, LANES * s:LANES * (s + 1)]) for s in range(t.shape[1] // LANES)], axis=1)


def _rot(t, c, sa, sb):
    return _per_slab(lambda u: u * c + pltpu.roll(u, ROPE_HALF, 1) * sa + pltpu.roll(u, LANES - ROPE_HALF, 1) * sb, t)


def _rot_transposed(g, c, sa, sb):
    return _per_slab(
        lambda u: u * c + pltpu.roll(u * sa, LANES - ROPE_HALF, 1) + pltpu.roll(u * sb, ROPE_HALF, 1), g)


def _gather_project(x, mix_w, w_in, w_out, rc, rsa, rsb):
    tm = 1024
    n_tiles = SEQ // tm
    arrival_of_step = (None, 0, 1, 2, 4, 5, 3, 6)

    def body(order_ref, x_ref, w_ref, win_ref, wout_ref, c_ref, sa_ref, sb_ref,
             proj_ref, hnt_ref, gin_hbm, gout_hbm, qkv_hbm,
             hn_s, w_land, wout_land, stage, sort_stage, slab_tmp, send_sems, recv_sems, local_sems):
        g, i = pl.program_id(0), pl.program_id(1)
        me = _my_place()
        x_, y_, c_ = me
        sibling = (x_, y_, 1 - c_)
        chips = [(1 - x_, y_), (x_, 1 - y_), (1 - x_, 1 - y_)]

        def slab(which, place):
            idx = _flat(place)
            if which == 0:
                return w_land.at[idx]
            return wout_land.at[pl.ds(pl.multiple_of(idx * WOUT_ROWS, WOUT_ROWS), WOUT_ROWS), :]

        def copy(which, k, block, to, src=None):
            ref = slab(which, block)
            return pltpu.make_async_remote_copy(
                src_ref=ref if src is None else src, dst_ref=ref, send_sem=send_sems.at[7 * which + k],
                recv_sem=recv_sems.at[7 * which + k], device_id=to, device_id_type=MESH)

        def first_copies(which):
            src = stage if which == 0 else None
            return ([copy(which, 0, me, sibling, src)]
                    + [copy(which, 1 + j, me, (*chip, c_), src) for j, chip in enumerate(chips)])

        def pass_on(which, j):
            return copy(which, 4 + j, (*chips[j], c_), sibling)

        def arrival(which, k):
            if k == 0:
                return copy(which, 0, sibling, me)
            if k <= 3:
                return copy(which, k, (*chips[k - 1], c_), me)
            return copy(which, k, (*chips[k - 4], 1 - c_), me)

        def to_hbm(step):
            idx = order_ref[step]
            cols = pl.ds(pl.multiple_of(idx * COL_BLOCK, COL_BLOCK), COL_BLOCK)
            return pltpu.make_async_copy(w_land.at[idx], gin_hbm.at[:, cols], local_sems.at[step])

        @pl.when((g == 0) & (i == 0))
        def _():
            stage[...] = win_ref[...].astype(BF16)
            w_land[_flat(me)] = stage[...]
            wout_land[pl.ds(pl.multiple_of(_flat(me) * WOUT_ROWS, WOUT_ROWS), WOUT_ROWS), :] = (
                wout_ref[...].astype(BF16))
            for which in (0, 1):
                for cp in first_copies(which):
                    cp.start()
            to_hbm(0).start()

        for step, k in enumerate(arrival_of_step):
            if k is None:
                continue

            @pl.when((g == step) & (i == 0))
            def _(k=k, step=step):
                arrival(0, k).wait_recv()
                to_hbm(step).start()
                if 1 <= k <= 3:
                    arrival(1, k).wait_recv()
                    pass_on(0, k - 1).start()
                    pass_on(1, k - 1).start()

        rows = pl.ds(pl.multiple_of(i * tm, tm), tm)

        @pl.when(g == 0)
        def _():
            xf = x_ref[...]
            ms = jnp.mean(xf * xf, axis=-1, keepdims=True)
            hn = xf * lax.rsqrt(ms + NORM_EPS) * w_ref[...]
            hnt_ref[...] = hn.T.astype(BF16)
            hn_s[rows, :] = hn.astype(BF16)

        group = order_ref[g]

        def sorted_copy():
            per = tm // SORT_RESIDUES
            for s in range(COL_BLOCK // LANES):
                slab_tmp[s] = proj_ref[:, LANES * s:LANES * (s + 1)]
            for r in range(SORT_RESIDUES):
                for s in range(COL_BLOCK // LANES):
                    sort_stage[r, :, LANES * s:LANES * (s + 1)] = (
                        slab_tmp.at[s][pl.ds(r, per, stride=SORT_RESIDUES), :])
            cols = pl.ds(pl.multiple_of(group * COL_BLOCK, COL_BLOCK), COL_BLOCK)
            cp = pltpu.make_async_copy(
                sort_stage, qkv_hbm.at[:, pl.ds(pl.multiple_of(i * per, per), per), cols], local_sems.at[N_DEV + 1])
            cp.start()
            cp.wait()

        @pl.when(group < 2)
        def _():
            proj_ref[...] = _rot(_mm(hn_s[rows, :], w_land[group]), c_ref[...], sa_ref[...], sb_ref[...])
            sorted_copy()

        @pl.when(group == 2)
        def _():
            proj_ref[...] = _mm(hn_s[rows, :], w_land[group])
            sorted_copy()

        @pl.when(group > 2)
        def _():
            proj_ref[...] = _mm(hn_s[rows, :], w_land[group])

        @pl.when((g == N_DEV - 1) & (i == n_tiles - 1))
        def _():
            for k in (0, 4, 5, 6):
                arrival(1, k).wait_recv()
            for which in (0, 1):
                for cp in first_copies(which) + [pass_on(which, j) for j in range(3)]:
                    cp.wait_send()
            wout_copy = pltpu.make_async_copy(wout_land, gout_hbm, local_sems.at[N_DEV])
            wout_copy.start()
            for step in range(N_DEV):
                to_hbm(step).wait()
            wout_copy.wait()

    me = _my_place()
    x_, y_, c_ = me
    chips = [(1 - x_, y_), (x_, 1 - y_), (1 - x_, 1 - y_)]
    order = jnp.stack([_flat(p) for p in (
        me, (x_, y_, 1 - c_), (*chips[0], c_), (*chips[1], c_), (*chips[0], 1 - c_), (*chips[1], 1 - c_),
        (*chips[2], c_), (*chips[2], 1 - c_))]).astype(jnp.int32)

    first_sweep = lambda g, i, order: (jnp.where(g == 0, i, n_tiles - 1), 0)
    tab = pl.BlockSpec((tm, LANES), lambda g, i, order: (jnp.where(order[g] < 2, i, 0), 0))
    whole = lambda: pl.BlockSpec(memory_space=pltpu.VMEM)
    grid_spec = pltpu.PrefetchScalarGridSpec(
        num_scalar_prefetch=1, grid=(N_DEV, n_tiles),
        in_specs=[pl.BlockSpec((tm, D_MODEL), first_sweep),
                  pl.BlockSpec((1, D_MODEL), lambda g, i, order: (0, 0)),
                  whole(), whole(), tab, tab, tab],
        out_specs=(pl.BlockSpec((tm, COL_BLOCK), lambda g, i, order: (i, order[g])),
                   pl.BlockSpec((D_MODEL, tm), lambda g, i, order: (0, jnp.where(g == 0, i, n_tiles - 1))),
                   pl.BlockSpec(memory_space=pl.ANY), pl.BlockSpec(memory_space=pl.ANY),
                   pl.BlockSpec(memory_space=pl.ANY)),
        scratch_shapes=[pltpu.VMEM((SEQ, D_MODEL), BF16),
                        pltpu.VMEM((N_DEV, D_MODEL, COL_BLOCK), BF16),
                        pltpu.VMEM((D_MODEL, D_MODEL), BF16),
                        pltpu.VMEM((D_MODEL, COL_BLOCK), BF16),
                        pltpu.VMEM((SORT_RESIDUES, tm // SORT_RESIDUES, COL_BLOCK), F32),
                        pltpu.VMEM((COL_BLOCK // LANES, tm, LANES), F32),
                        pltpu.SemaphoreType.DMA((14,)), pltpu.SemaphoreType.DMA((14,)),
                        pltpu.SemaphoreType.DMA((N_DEV + 2,))])
    proj, hn_t, w_in_g, w_out_g, qkv_sorted = pl.pallas_call(
        body, name="gather_project", grid_spec=grid_spec,
        out_shape=(jax.ShapeDtypeStruct((SEQ, IN_COLS), F32), jax.ShapeDtypeStruct((D_MODEL, SEQ), BF16),
                   jax.ShapeDtypeStruct((D_MODEL, IN_COLS), BF16), jax.ShapeDtypeStruct((D_MODEL, D_MODEL), BF16),
                   jax.ShapeDtypeStruct((SORT_RESIDUES, SORT_ROWS, 3 * COL_BLOCK), F32)),
        compiler_params=_params(("arbitrary", "arbitrary")),
    )(order, x, mix_w, w_in, w_out, rc, rsa, rsb)
    return proj, hn_t, w_in_g, w_out_g, qkv_sorted.reshape(SEQ, 3 * COL_BLOCK)


ATTN_GROUP = 8
BLOCKS_PER_PATTERN = SEQ // ATTN_BLOCK
SORT_RESIDUES = 16
SORT_ROWS = SEQ // SORT_RESIDUES


def _write_band_bias(bias_ref):
    row = lax.broadcasted_iota(jnp.int32, (2 * ATTN_BLOCK, 2 * ATTN_BLOCK), 0) & (ATTN_BLOCK - 1)
    col = lax.broadcasted_iota(jnp.int32, (2 * ATTN_BLOCK, 2 * ATTN_BLOCK), 1)
    for pi, d in enumerate(DILATIONS):
        per = SORT_RESIDUES // d
        ahead = per * (row % (8 * d) - col % (16 * d)) + (row // (8 * d) - col // (16 * d))
        dist = ATTN_BLOCK + ahead
        bias_ref[2 * pi] = jnp.where((dist >= 0) & (dist <= ATTN_BLOCK), 0.0, NEG_BIG)
        bias_ref[2 * pi + 1] = jnp.where(ahead >= 0, 0.0, NEG_BIG)


def _head0_lanes():
    return lax.broadcasted_iota(jnp.int32, (ATTN_BLOCK, LANES), 1) < HEAD_DIM


def _stack_heads(t, h0):
    return jnp.concatenate([jnp.where(h0, t, 0.0), jnp.where(h0, 0.0, t)], axis=0).astype(BF16)


def _block_runs(i, d):
    nblk = BLOCKS_PER_PATTERN // d
    r, n = i // nblk, i % nblk
    kn = jnp.maximum(n - 1, 0)
    rows, keys = [], []
    for c in range(SORT_RESIDUES // d):
        base = SORT_ROWS * (c * d + r)
        rows.append(pl.ds(pl.multiple_of(base + 8 * d * n, 8), 8 * d))
        keys.append(pl.ds(pl.multiple_of(base + 8 * d * kn, 8), 16 * d))
    return rows, keys, (n == 0).astype(jnp.int32)


def _take(ref, runs):
    return jnp.concatenate([ref[run, :] for run in runs], axis=0)


def _put(ref, runs, value, add=False):
    at = 0
    for run in runs:
        piece = value[at:at + run.size]
        if add:
            ref[run, :] += piece
        else:
            ref[run, :] = piece
        at += run.size


def _sort_rows(src_ref, dst_ref):
    for r in range(SORT_RESIDUES):
        dst_ref[SORT_ROWS * r:SORT_ROWS * (r + 1), :] = src_ref[pl.ds(r, SORT_ROWS, stride=SORT_RESIDUES), :]


def _unsort_rows(src_ref, dst_ref):
    for r in range(SORT_RESIDUES):
        dst_ref[pl.ds(r, SORT_ROWS, stride=SORT_RESIDUES), :] = src_ref[SORT_ROWS * r:SORT_ROWS * (r + 1), :]


def _for_each_group(d, load, compute, store):
    def group(g, carry):
        items = [load(*_block_runs(g * ATTN_GROUP + u, d)) for u in range(ATTN_GROUP)]
        results = [compute(item) for item in items]
        for item, res in zip(items, results):
            store(item, res)
        return carry

    lax.fori_loop(0, BLOCKS_PER_PATTERN // ATTN_GROUP, group, 0)


def _attn_fwd_fused(qkv_sorted):
    n_pat = len(DILATIONS)
    tile2 = (2 * ATTN_BLOCK, LANES)

    def body(q_ref, k_ref, v_ref, o_ref, lse_ref, o_acc, m_acc, l_acc, bias_ref):
        pl.when(pl.program_id(0) == 0)(lambda: _write_band_bias(bias_ref))
        h0 = _head0_lanes()
        for pi, d in enumerate(DILATIONS):
            first, last = pi == 0, pi == n_pat - 1

            def load(rows, keys, which, first=first, pi=pi):
                item = dict(rows=rows, keys=keys, which=2 * pi + which)
                if not first:
                    item.update(o=_take(o_acc, rows), m=[_take(m_acc.at[h], rows) for h in range(2)],
                                l=[_take(l_acc.at[h], rows) for h in range(2)])
                return item

            def compute(item, first=first):
                kb = _take(k_ref, item["keys"]).astype(BF16)
                vb = _take(v_ref, item["keys"]).astype(BF16)
                s = _mm_nt(_stack_heads(_take(q_ref, item["rows"]), h0), kb) * 0.125 + bias_ref[item["which"]]
                mb = jnp.max(s, axis=-1, keepdims=True)
                if first:
                    p = jnp.exp(s - mb)
                    mn = jnp.broadcast_to(mb, tile2)
                else:
                    m_old = jnp.concatenate(item["m"], axis=0)
                    mn = jnp.maximum(m_old, mb)
                    alpha = jnp.exp(m_old - mn)
                    p = jnp.exp(s - jnp.concatenate([mn, mn], axis=1))
                ls = jnp.sum(p, axis=-1, keepdims=True)
                pv = _mm(p.astype(BF16), vb)
                if first:
                    return pv, mn, jnp.broadcast_to(ls, tile2)
                o_old = jnp.concatenate([item["o"], item["o"]], axis=0)
                return alpha * o_old + pv, mn, alpha * jnp.concatenate(item["l"], axis=0) + ls

            def store(item, res, last=last):
                rows = item["rows"]
                (o0, o1), (m0, m1), (l0, l1) = ((a[:ATTN_BLOCK], a[ATTN_BLOCK:]) for a in res)
                if last:
                    _put(o_acc, rows, jnp.where(h0, o0 / l0, o1 / l1))
                    _put(lse_ref, rows, jnp.where(h0, m0 + jnp.log(l0), m1 + jnp.log(l1)))
                else:
                    _put(o_acc, rows, jnp.where(h0, o0, o1))
                    for h, (m, l) in enumerate(((m0, l0), (m1, l1))):
                        _put(m_acc.at[h], rows, m)
                        _put(l_acc.at[h], rows, l)

            _for_each_group(d, load, compute, store)
        _unsort_rows(o_acc, o_ref)

    slab = lambda g: pl.BlockSpec((SEQ, LANES), functools.partial(lambda hp, g: (0, 4 * g + hp), g=g))
    wide = jax.ShapeDtypeStruct((SEQ, ATTN_WIDTH), F32)
    return pl.pallas_call(
        body, name="attn_fwd", grid=(4,), out_shape=(wide, wide),
        in_specs=[slab(0), slab(1), slab(2)], out_specs=(slab(0), slab(0)),
        scratch_shapes=[pltpu.VMEM((SEQ, LANES), F32), pltpu.VMEM((2, SEQ, LANES), F32),
                        pltpu.VMEM((2, SEQ, LANES), F32),
                        pltpu.VMEM((2 * len(DILATIONS), 2 * ATTN_BLOCK, 2 * ATTN_BLOCK), F32)],
        compiler_params=_params(("arbitrary",)),
    )(qkv_sorted, qkv_sorted, qkv_sorted)


def _attn_bwd_fused(qkv_sorted, d_out, lse_sorted, delta):
    def body(q_ref, k_ref, v_ref, do_ref, lse_ref, del_ref, dq_ref, dk_ref, dv_ref,
             do_s, del_s, dq_s, dk_s, dv_s, bias_ref):
        pl.when(pl.program_id(0) == 0)(lambda: _write_band_bias(bias_ref))
        _sort_rows(do_ref, do_s)
        _sort_rows(del_ref, del_s)
        dk_s[...] = jnp.zeros_like(dk_s)
        dv_s[...] = jnp.zeros_like(dv_s)
        h0 = _head0_lanes()
        for pi, d in enumerate(DILATIONS):
            first = pi == 0

            def load(rows, keys, which, pi=pi):
                return dict(rows=rows, keys=keys, q=_take(q_ref, rows), g=_take(do_s, rows),
                            lse=_take(lse_ref, rows), delta=_take(del_s, rows),
                            k=_take(k_ref, keys).astype(BF16), v=_take(v_ref, keys).astype(BF16),
                            bias=bias_ref[2 * pi + which])

            def per_head(t):
                swapped = pltpu.roll(t, HEAD_DIM, 1)
                both = jnp.concatenate([jnp.where(h0, t, swapped), jnp.where(h0, swapped, t)], axis=0)
                return jnp.concatenate([both, both], axis=1)

            def compute(item):
                q2, g2 = _stack_heads(item["q"], h0), _stack_heads(item["g"], h0)
                s = _mm_nt(q2, item["k"]) * 0.125 + item["bias"]
                p = jnp.exp(s - per_head(item["lse"]))
                dp = _mm_nt(g2, item["v"])
                ds = (p * (dp - per_head(item["delta"])) * 0.125).astype(BF16)
                dq2 = _mm(ds, item["k"])
                dq = jnp.where(h0, dq2[:ATTN_BLOCK], dq2[ATTN_BLOCK:])
                return dq, _mm_tn(ds, q2), _mm_tn(p.astype(BF16), g2)

            def store(item, res, first=first):
                _put(dq_s, item["rows"], res[0], add=not first)
                _put(dk_s, item["keys"], res[1], add=True)
                _put(dv_s, item["keys"], res[2], add=True)

            _for_each_group(d, load, compute, store)
        _unsort_rows(dq_s, dq_ref)
        _unsort_rows(dk_s, dk_ref)
        _unsort_rows(dv_s, dv_ref)

    slab = lambda g: pl.BlockSpec((SEQ, LANES), functools.partial(lambda hp, g: (0, 4 * g + hp), g=g))
    wide = jax.ShapeDtypeStruct((SEQ, ATTN_WIDTH), F32)
    sorted_slab = pltpu.VMEM((SEQ, LANES), F32)
    return pl.pallas_call(
        body, name="attn_bwd", grid=(4,), out_shape=(wide, wide, wide),
        scratch_shapes=[sorted_slab] * 5 + [pltpu.VMEM((2 * len(DILATIONS), 2 * ATTN_BLOCK, 2 * ATTN_BLOCK), F32)],
        in_specs=[slab(0), slab(1), slab(2), slab(0), slab(0), slab(0)], out_specs=(slab(0), slab(0), slab(0)),
        compiler_params=_params(("arbitrary",)),
    )(qkv_sorted, qkv_sorted, qkv_sorted, d_out, lse_sorted, delta)


def _hgrn_lower_bound(lb_ref):
    r0, r1 = lb_ref[0:1, :], lb_ref[1:2, :]
    mx = jnp.maximum(r0, r1)
    e0, e1 = jnp.exp(r0 - mx), jnp.exp(r1 - mx)
    return e0 / (e0 + e1)


def _hgrn_gates(hq, hf, lb):
    sq = _sigmoid(hq)
    sg = _sigmoid(hf)
    f = lb + (1.0 - lb) * sg
    return hq * sq, sq, sg, f, 1.0 - f, jnp.log(f)


HGRN_PAIR = 4
HGRN_SEQ_BLOCK = 1024
HGRN_GROUP = 4
HGRN_ROWS = HGRN_GROUP * HGRN_CHUNK


def _hgrn_specs(reverse):
    n_blocks = SEQ // HGRN_SEQ_BLOCK
    width = HGRN_PAIR * HGRN_DIM
    blk = (lambda s: n_blocks - 1 - s) if reverse else (lambda s: s)
    cols = lambda g: pl.BlockSpec((HGRN_SEQ_BLOCK, width),
                                  functools.partial(lambda p, s, g: (blk(s), (HGRN_HEADS // HGRN_PAIR) * g + p), g=g))
    pair = pl.BlockSpec((HGRN_SEQ_BLOCK, width), lambda p, s: (blk(s), p))
    lb = pl.BlockSpec((2, width), lambda p, s: (0, p))
    states = pl.BlockSpec((HGRN_PAIR, HGRN_SEQ_BLOCK // HGRN_CHUNK, HGRN_DIM, HGRN_DIM),
                          lambda p, s: (p, blk(s), 0, 0))
    return cols, pair, lb, states


def _chunk_masks():
    ri = lax.broadcasted_iota(jnp.int32, (HGRN_ROWS, HGRN_ROWS), 0)
    ci = lax.broadcasted_iota(jnp.int32, (HGRN_ROWS, HGRN_ROWS), 1)
    same = (ri // HGRN_CHUNK) == (ci // HGRN_CHUNK)
    return same, same & (ri >= ci), same & (ri <= ci)


def _mm_select(sel, v):
    hi = v.astype(BF16)
    r1 = v - hi.astype(F32)
    mid = r1.astype(BF16)
    lo = (r1 - mid.astype(F32)).astype(BF16)
    return _mm(sel, hi) + _mm(sel, mid) + _mm(sel, lo)


def _head_cols(a, h):
    return a[:, HGRN_DIM * h:HGRN_DIM * (h + 1)]


def _hgrn_fwd(proj, lb_raw):
    t, rws = HGRN_CHUNK, HGRN_ROWS

    def body(hq_ref, hf_ref, hi_ref, lb_ref, rec_ref, st_ref, state):
        @pl.when(pl.program_id(1) == 0)
        def _():
            state[...] = jnp.zeros_like(state)

        lb = _hgrn_lower_bound(lb_ref)
        same, causal, _ = _chunk_masks()
        sel = jnp.concatenate([causal, same], axis=0).astype(BF16)

        def group(g, sts):
            rows = pl.ds(pl.multiple_of(g * rws, rws), rws)
            q, _, _, _, k, lf = _hgrn_gates(hq_ref[rows, :], hf_ref[rows, :], lb)
            sums = _mm_select(sel, lf)
            cum, last = sums[:rws], sums[rws:]
            qd = (q * jnp.exp(cum)).astype(BF16)
            ki = (k * jnp.exp(-cum)).astype(BF16)
            ke = (k * jnp.exp(last - cum)).astype(BF16)
            vb = hi_ref[rows, :].astype(BF16)
            dec = jnp.exp(last)
            new_sts, recs = [], []
            for h in range(HGRN_PAIR):
                qd_h, ke_h, vb_h = _head_cols(qd, h), _head_cols(ke, h), _head_cols(vb, h)
                att = jnp.where(causal, _mm_nt(qd_h, _head_cols(ki, h)), 0.0).astype(BF16)
                intra = _mm(att, vb_h)
                st = sts[h]
                outs = []
                for c in range(HGRN_GROUP):
                    sl = slice(c * t, (c + 1) * t)
                    st_ref[h, g * HGRN_GROUP + c] = st
                    outs.append(intra[sl] + _mm_nt(qd_h[sl], st.astype(BF16)))
                    st = st * _head_cols(dec[c * t:c * t + 1, :], h) + _mm_tn(vb_h[sl], ke_h[sl])
                new_sts.append(st)
                recs.append(jnp.concatenate(outs, axis=0))
            rec_ref[rows, :] = jnp.concatenate(recs, axis=1)
            return tuple(new_sts)

        sts = lax.fori_loop(0, HGRN_SEQ_BLOCK // rws, group, tuple(state[h] for h in range(HGRN_PAIR)))
        for h in range(HGRN_PAIR):
            state[h] = sts[h]

    cols, pair, lb, states = _hgrn_specs(reverse=False)
    return pl.pallas_call(
        body, name="hgrn_fwd", grid=(HGRN_HEADS // HGRN_PAIR, SEQ // HGRN_SEQ_BLOCK),
        out_shape=(jax.ShapeDtypeStruct((SEQ, HGRN_WIDTH), F32),
                   jax.ShapeDtypeStruct((HGRN_HEADS, N_CHUNKS, HGRN_DIM, HGRN_DIM), F32)),
        in_specs=[cols(4), cols(5), cols(6), lb], out_specs=(pair, states),
        scratch_shapes=[pltpu.VMEM((HGRN_PAIR, HGRN_DIM, HGRN_DIM), F32)],
        compiler_params=_params(("parallel", "arbitrary")),
    )(proj, proj, proj, lb_raw)


def _hgrn_bwd(proj, lb_raw, d_rec, states):
    t, rws = HGRN_CHUNK, HGRN_ROWS

    def body(hq_ref, hf_ref, hi_ref, lb_ref, do_ref, st_ref, dhq_ref, dhf_ref, dhi_ref, dlb_ref,
             dstate, dlb_acc):
        lb = _hgrn_lower_bound(lb_ref)
        same, causal, anti = _chunk_masks()
        sel = jnp.concatenate([causal, same], axis=0).astype(BF16)
        sel_t = jnp.concatenate([anti, same], axis=1).astype(BF16)
        @pl.when(pl.program_id(1) == 0)
        def _():
            dstate[...] = jnp.zeros_like(dstate)
            dlb_acc[...] = jnp.zeros_like(dlb_acc)

        n_groups = HGRN_SEQ_BLOCK // rws
        chunks = [slice(c * t, (c + 1) * t) for c in range(HGRN_GROUP)]

        def group(i, dsts_in):
            g = n_groups - 1 - i
            rows = pl.ds(pl.multiple_of(g * rws, rws), rws)
            hq = hq_ref[rows, :]
            q, sq, sg, f, k, lf = _hgrn_gates(hq, hf_ref[rows, :], lb)
            sums = _mm_select(sel, lf)
            cum, last = sums[:rws], sums[rws:]
            e_cum, e_inv, e_end, dec = jnp.exp(cum), jnp.exp(-cum), jnp.exp(last - cum), jnp.exp(last)
            qd, ki, ke = q * e_cum, k * e_inv, k * e_end
            qdb, kib, keb = qd.astype(BF16), ki.astype(BF16), ke.astype(BF16)
            vb = hi_ref[rows, :].astype(BF16)
            gb = do_ref[rows, :].astype(BF16)

            dsts_out, per_head = [], []
            for h in range(HGRN_PAIR):
                qdb_h, kib_h, keb_h = _head_cols(qdb, h), _head_cols(kib, h), _head_cols(keb, h)
                vb_h, gb_h = _head_cols(vb, h), _head_cols(gb, h)
                att = jnp.where(causal, _mm_nt(qdb_h, kib_h), 0.0).astype(BF16)
                datt = jnp.where(causal, _mm_nt(gb_h, vb_h), 0.0).astype(BF16)
                dv = _mm_tn(att, gb_h)
                dqd = _mm(datt, kib_h)
                dki = _mm_tn(datt, qdb_h)

                decs = [_head_cols(dec[c * t:c * t + 1, :], h) for c in range(HGRN_GROUP)]
                dsts = [None] * HGRN_GROUP
                dst = dsts_in[h]
                for c in reversed(range(HGRN_GROUP)):
                    dsts[c] = dst
                    dst = dst * decs[c] + _mm_tn(gb_h[chunks[c]], qdb_h[chunks[c]])
                dsts_out.append(dst)

                dv_x, dqd_x, dke, dlast_x = [], [], [], []
                for c, sl in enumerate(chunks):
                    st_prev = st_ref[h, g * HGRN_GROUP + c]
                    dstb = dsts[c].astype(BF16)
                    dv_x.append(_mm_nt(keb_h[sl], dstb))
                    dqd_x.append(_mm(gb_h[sl], st_prev.astype(BF16)))
                    dke.append(_mm(vb_h[sl], dstb))
                    ddec = jnp.sum(dsts[c] * st_prev, axis=0, keepdims=True)
                    dlast_x.append(jnp.broadcast_to(ddec * decs[c], (t, HGRN_DIM)))
                per_head.append((dv + jnp.concatenate(dv_x, axis=0), dqd + jnp.concatenate(dqd_x, axis=0),
                                 dki, jnp.concatenate(dke, axis=0), jnp.concatenate(dlast_x, axis=0)))
            dv, dqd, dki, dke, dlast = (jnp.concatenate(list(parts), axis=1) for parts in zip(*per_head))

            dq = dqd * e_cum
            dk = dki * e_inv + dke * e_end
            dke_ke = dke * ke
            dcum = dqd * qd - dki * ki - dke_ke
            dlf = _mm_select(sel_t, jnp.concatenate([dcum, dke_ke], axis=0)) + dlast
            df = dlf / f - dk
            dhq_ref[rows, :] = dq * (sq * (1.0 + hq * (1.0 - sq)))
            dhf_ref[rows, :] = df * (1.0 - lb) * (sg * (1.0 - sg))
            dhi_ref[rows, :] = dv
            dlb_acc[...] += jnp.sum(df * (1.0 - sg), axis=0, keepdims=True)
            return tuple(dsts_out)

        dsts = lax.fori_loop(0, n_groups, group, tuple(dstate[h] for h in range(HGRN_PAIR)))
        for h in range(HGRN_PAIR):
            dstate[h] = dsts[h]
        g0 = dlb_acc[...] * lb * (1.0 - lb)
        dlb_ref[...] = jnp.concatenate([g0, -g0], axis=0)

    cols, pair, lb_spec, st_spec = _hgrn_specs(reverse=True)
    wide = jax.ShapeDtypeStruct((SEQ, HGRN_WIDTH), F32)
    return pl.pallas_call(
        body, name="hgrn_bwd", grid=(HGRN_HEADS // HGRN_PAIR, SEQ // HGRN_SEQ_BLOCK),
        out_shape=(wide, wide, wide, jax.ShapeDtypeStruct((2, HGRN_WIDTH), F32)),
        in_specs=[cols(4), cols(5), cols(6), lb_spec, pair, st_spec],
        out_specs=(pair, pair, pair, lb_spec),
        scratch_shapes=[pltpu.VMEM((HGRN_PAIR, HGRN_DIM, HGRN_DIM), F32),
                        pltpu.VMEM((1, HGRN_PAIR * HGRN_DIM), F32)],
        compiler_params=_params(("parallel", "arbitrary")),
    )(proj, proj, proj, lb_raw, d_rec, states)


def _group_sum(v, group):
    parts = []
    for s in range(v.shape[1] // LANES):
        slab = v[:, LANES * s:LANES * (s + 1)]
        if group == LANES:
            parts.append(jnp.broadcast_to(jnp.sum(slab, axis=-1, keepdims=True), slab.shape))
        else:
            h0 = lax.broadcasted_iota(jnp.int32, slab.shape, 1) < HEAD_DIM
            s0 = jnp.sum(jnp.where(h0, slab, 0.0), axis=-1, keepdims=True)
            s1 = jnp.sum(jnp.where(h0, 0.0, slab), axis=-1, keepdims=True)
            parts.append(jnp.where(h0, s0, s1))
    return jnp.concatenate(parts, axis=1)


def _mid(attn_o, rec, proj, x, target, w_out_g, attn_w, hgrn_w, final_w):
    tm = 256

    def branch_fwd(o, gate, w, group):
        r = lax.rsqrt(_group_sum(o * o, group) * (1.0 / group) + NORM_EPS)
        nrm = o * r
        sg = _sigmoid(gate)
        return r, nrm, sg, nrm * w * (gate * sg)

    def branch_bwd(dy, r, nrm, sg, gate, w, group):
        silu = gate * sg
        d_gate = dy * nrm * w * (sg * (1.0 + gate * (1.0 - sg)))
        d_w = jnp.sum(dy * nrm * silu, axis=0, keepdims=True)
        dn = dy * w * silu
        d_o = r * (dn - nrm * (_group_sum(dn * nrm, group) * (1.0 / group)))
        return d_o, d_gate, d_w

    def body(o_ref, rec_ref, ag_ref, hg_ref, x_ref, tgt_ref, wout_ref, aw_ref, hw_ref, fw_ref,
             dx2_ref, do_ref, delta_ref, dag_ref, drec_ref, dhg_ref, dwout_ref, dfw_ref, daw_ref, dhw_ref,
             loss_ref, dwout_acc):
        i = pl.program_id(0)

        @pl.when(i == 0)
        def _():
            dwout_acc[...] = jnp.zeros_like(dwout_acc)
            dfw_ref[...] = jnp.zeros_like(dfw_ref)
            daw_ref[...] = jnp.zeros_like(daw_ref)
            dhw_ref[...] = jnp.zeros_like(dhw_ref)
            loss_ref[...] = jnp.zeros_like(loss_ref)

        o, rc, ag, hg = o_ref[...], rec_ref[...], ag_ref[...], hg_ref[...]
        aw, hw, fw = aw_ref[...], hw_ref[...], fw_ref[...]
        ra, na, sga, ya = branch_fwd(o, ag, aw, HEAD_DIM)
        rh, nh, sgh, yh = branch_fwd(rc, hg, hw, HGRN_DIM)
        mixed = jnp.concatenate([ya, yh], axis=1).astype(BF16)
        wout = wout_ref[...]
        x2 = x_ref[...] + _mm(mixed, wout)
        rstd = lax.rsqrt(jnp.mean(x2 * x2, axis=-1, keepdims=True) + NORM_EPS)
        xn = x2 * rstd
        err = xn * fw - tgt_ref[...]
        row_loss = jnp.mean(err * err, axis=-1, keepdims=True)
        loss_ref[...] += 0.5 * jnp.sum(row_loss, axis=0, keepdims=True)
        dy = err * (1.0 / D_MODEL)
        dfw_ref[...] += jnp.sum(dy * xn, axis=0, keepdims=True)
        dxn = dy * fw
        dx2 = rstd * (dxn - xn * jnp.mean(dxn * xn, axis=-1, keepdims=True))
        dx2_ref[...] = dx2
        dx2b = dx2.astype(BF16)
        dwout_acc[...] += _mm_tn(mixed, dx2b)

        @pl.when(i == pl.num_programs(0) - 1)
        def _():
            dwout_ref[...] = dwout_acc[...].astype(BF16)

        dmixed = _mm_nt(dx2b, wout)

        d_o, d_ag, d_aw = branch_bwd(dmixed[:, :ATTN_WIDTH], ra, na, sga, ag, aw, HEAD_DIM)
        d_rec, d_hg, d_hw = branch_bwd(dmixed[:, ATTN_WIDTH:], rh, nh, sgh, hg, hw, HGRN_DIM)
        do_ref[...] = d_o
        delta_ref[...] = _group_sum(d_o * o, HEAD_DIM)
        dag_ref[...] = d_ag
        drec_ref[...] = d_rec
        dhg_ref[...] = d_hg
        daw_ref[...] += d_aw
        dhw_ref[...] += d_hw

    half = lambda: pl.BlockSpec((tm, COL_BLOCK), lambda i: (i, 0))
    full = lambda: pl.BlockSpec((tm, D_MODEL), lambda i: (i, 0))
    fixed = lambda r, c: pl.BlockSpec((r, c), lambda i: (0, 0))
    wide = jax.ShapeDtypeStruct((SEQ, COL_BLOCK), F32)
    return pl.pallas_call(
        body, name="mid", grid=(SEQ // tm,),
        out_shape=(jax.ShapeDtypeStruct((SEQ, D_MODEL), F32), wide, wide, wide, wide, wide,
                   jax.ShapeDtypeStruct((D_MODEL, D_MODEL), BF16),
                   jax.ShapeDtypeStruct((1, D_MODEL), F32), jax.ShapeDtypeStruct((1, COL_BLOCK), F32),
                   jax.ShapeDtypeStruct((1, COL_BLOCK), F32), jax.ShapeDtypeStruct((1, 1), F32)),
        scratch_shapes=[pltpu.VMEM((D_MODEL, D_MODEL), F32)],
        in_specs=[half(), half(),
                  pl.BlockSpec((tm, COL_BLOCK), lambda i: (i, 3)), pl.BlockSpec((tm, COL_BLOCK), lambda i: (i, 7)),
                  full(), full(), fixed(D_MODEL, D_MODEL), fixed(1, COL_BLOCK), fixed(1, COL_BLOCK),
                  fixed(1, D_MODEL)],
        out_specs=(full(), half(), half(), half(), half(), half(), fixed(D_MODEL, D_MODEL),
                   fixed(1, D_MODEL), fixed(1, COL_BLOCK), fixed(1, COL_BLOCK), fixed(1, 1)),
        compiler_params=_params(("arbitrary",)),
    )(attn_o, rec, proj, proj, x, target, w_out_g, attn_w, hgrn_w, final_w)


def _dproj_prep(d_groups, rc, rsa, rsb):
    tm = 512

    def body(*refs):
        dg_refs = refs[:N_DEV]
        c_ref, sa_ref, sb_ref, dpb_ref = refs[N_DEV:]
        for j in range(N_DEV):
            dp = dg_refs[j][...]
            if j < 2:
                dp = _rot_transposed(dp, c_ref[...], sa_ref[...], sb_ref[...])
            dpb_ref[:, COL_BLOCK * j:COL_BLOCK * (j + 1)] = dp.astype(BF16)

    tile = lambda cols: pl.BlockSpec((tm, cols), lambda i: (i, 0))
    return pl.pallas_call(
        body, name="dproj_prep", grid=(SEQ // tm,),
        out_shape=jax.ShapeDtypeStruct((SEQ, IN_COLS), BF16),
        in_specs=[tile(COL_BLOCK) for _ in range(N_DEV)] + [tile(LANES), tile(LANES), tile(LANES)],
        out_specs=tile(IN_COLS),
        compiler_params=_params(("parallel",)),
    )(*d_groups, rc, rsa, rsb)


def _in_proj_bwd_exchange(hn_t, dproj_b, w_g, x, dx2, mix_w, dwout_p, small_p):
    n_chips = N_DEV // 2
    rb = 128
    tm = 256
    n_tiles = SEQ // tm
    last_step = N_DEV + n_tiles - 1
    S1_IN, S1_OUT, SMALL, S2_IN, S2_OUT = 0, 4, 8, 15, 18
    rel_of_pair = (1, 2, 3, 0)

    def body(order_ref, hnt_hbm, wg_hbm, dpc_ref, dpr_ref, x_ref, dx2_ref, w_ref, dwout_ref, small_ref,
             gx_ref, gin_ref, gout_ref, gs_ref,
             big, part, dmw, s1_send, s1_in, s1_out, fwd_in, fwd_out, s2_in, s2_out, land_s,
             send_sems, recv_sems, local_sem):
        t = pl.program_id(0)
        me = _my_place()
        x, y, c = me
        my_chip = 2 * x + y
        sibling = (x, y, 1 - c)

        def remote(slot, src, dst, to):
            return pltpu.make_async_remote_copy(src_ref=src, dst_ref=dst, send_sem=send_sems.at[slot],
                                                recv_sem=recv_sems.at[slot], device_id=to, device_id_type=MESH)

        def s1_in_copy(pair):
            return remote(S1_IN + pair, s1_send.at[pair], s1_in.at[pair], sibling)

        def s1_out_copy(pair):
            q = my_chip ^ rel_of_pair[pair]
            return remote(S1_OUT + pair, dwout_ref.at[q, 1 - c], s1_out.at[pair], sibling)

        def s2_copies(rel):
            peer = _peer(me, 2 * rel)
            return [remote(S2_IN + rel - 1, fwd_in.at[rel - 1], s2_in.at[rel - 1], peer),
                    remote(S2_OUT + rel - 1, fwd_out.at[rel - 1], s2_out.at[rel - 1], peer)]

        def small_copy(rel):
            return remote(SMALL + rel - 1, land_s.at[0], land_s.at[rel], _peer(me, rel))

        def fill_big(src):
            cp = pltpu.make_async_copy(src, big, local_sem)
            cp.start()
            cp.wait()

        @pl.when(t == 0)
        def _():
            for pair in range(n_chips):
                s1_out_copy(pair).start()
            fill_big(hnt_hbm)

        @pl.when(t < N_DEV)
        def _():
            part[...] = _mm(big[...], dpc_ref[...])

        def rows_loop(n_rows, fn):
            def step(b, carry):
                fn(pl.ds(pl.multiple_of(b * rb, rb), rb))
                return carry
            lax.fori_loop(0, n_rows // rb, step, 0)

        for pair, rel in enumerate(rel_of_pair):
            @pl.when(t == 2 * pair)
            def _(pair=pair):
                s1_send[pair] = part[...].astype(BF16)
                s1_in_copy(pair).start()

            @pl.when(t == 2 * pair + 1)
            def _(pair=pair, rel=rel):
                q = my_chip ^ rel
                s1_in_copy(pair).wait_recv()
                s1_out_copy(pair).wait_recv()
                dst_in = fwd_in.at[rel - 1] if rel else gin_ref
                dst_out = fwd_out.at[rel - 1] if rel else gout_ref

                def add_in(rows):
                    dst_in[rows, :] = (part[rows, :] + s1_in[pair, rows, :].astype(F32)).astype(dst_in.dtype)

                def add_out(rows):
                    dst_out[rows, :] = (dwout_ref[q, c, rows, :].astype(F32)
                                        + s1_out[pair, rows, :].astype(F32)).astype(dst_out.dtype)

                rows_loop(D_MODEL, add_in)
                rows_loop(WOUT_ROWS, add_out)
                if rel:
                    for cp in s2_copies(rel):
                        cp.start()

        @pl.when(t == N_DEV)
        def _():
            fill_big(wg_hbm)
            dmw[...] = jnp.zeros_like(dmw)

        @pl.when(t >= N_DEV)
        def _():
            g = _mm_nt(dpr_ref[...], big[...])
            xf = x_ref[...]
            rstd = lax.rsqrt(jnp.mean(xf * xf, axis=-1, keepdims=True) + NORM_EPS)
            xn = xf * rstd
            dmw[...] += jnp.sum(g * xn, axis=0, keepdims=True)
            gw = g * w_ref[...]
            gx_ref[...] = dx2_ref[...] + rstd * (gw - xn * jnp.mean(gw * xn, axis=-1, keepdims=True))

        @pl.when(t == last_step)
        def _():
            land_s[0] = small_ref[...]
            for r in range(D_MODEL // LANES):
                land_s[0, ROW_MIX + r:ROW_MIX + r + 1, :] = dmw[:, LANES * r:LANES * (r + 1)]
            for rel in range(1, N_DEV):
                small_copy(rel).start()

            for rel in range(1, n_chips):
                for cp in s2_copies(rel):
                    cp.wait_recv()

            def total_in(rows):
                g = gin_ref[rows, :]
                for rel in range(1, n_chips):
                    g = g + s2_in[rel - 1, rows, :].astype(F32)
                gin_ref[rows, :] = g

            def total_out(rows):
                g = gout_ref[rows, :]
                for rel in range(1, n_chips):
                    g = g + s2_out[rel - 1, rows, :].astype(F32)
                gout_ref[rows, :] = g

            rows_loop(D_MODEL, total_in)
            rows_loop(WOUT_ROWS, total_out)

            for rel in range(1, N_DEV):
                small_copy(rel).wait_recv()
            my_flat = _flat(me)
            g = land_s[my_flat ^ 0]
            for dev in range(1, N_DEV):
                g = g + land_s[my_flat ^ dev]
            gs_ref[...] = g

            for pair in range(n_chips):
                s1_in_copy(pair).wait_send()
                s1_out_copy(pair).wait_send()
            for rel in range(1, n_chips):
                for cp in s2_copies(rel):
                    cp.wait_send()
            for rel in range(1, N_DEV):
                small_copy(rel).wait_send()

    place_x, place_y, place_c = _my_place()
    my_chip = 2 * place_x + place_y
    order = jnp.stack([2 * (my_chip ^ rel) + core for rel in rel_of_pair
                       for core in (1 - place_c, place_c)]).astype(jnp.int32)

    whole = lambda: pl.BlockSpec(memory_space=pltpu.VMEM)
    hbm = lambda: pl.BlockSpec(memory_space=pl.ANY)
    in_blocks = lambda n: pltpu.VMEM((n, D_MODEL, COL_BLOCK), BF16)
    out_blocks = lambda n: pltpu.VMEM((n, WOUT_ROWS, D_MODEL), BF16)
    tile_of = lambda t: jnp.maximum(t - N_DEV, 0)
    row_tile = lambda cols: pl.BlockSpec((tm, cols), lambda t, order: (tile_of(t), 0))
    grid_spec = pltpu.PrefetchScalarGridSpec(
        num_scalar_prefetch=1, grid=(N_DEV + n_tiles,),
        in_specs=[hbm(), hbm(),
                  pl.BlockSpec((SEQ, COL_BLOCK), lambda t, order: (0, order[jnp.minimum(t, N_DEV - 1)]),
                               pipeline_mode=pl.Buffered(1)),
                  row_tile(IN_COLS), row_tile(D_MODEL), row_tile(D_MODEL),
                  pl.BlockSpec((1, D_MODEL), lambda t, order: (0, 0)), whole(), whole()],
        out_specs=(row_tile(D_MODEL), whole(), whole(), whole()),
        scratch_shapes=[pltpu.VMEM((D_MODEL, IN_COLS), BF16), pltpu.VMEM((D_MODEL, COL_BLOCK), F32),
                        pltpu.VMEM((1, D_MODEL), F32), in_blocks(n_chips), in_blocks(n_chips),
                        out_blocks(n_chips), in_blocks(n_chips - 1), out_blocks(n_chips - 1),
                        in_blocks(n_chips - 1), out_blocks(n_chips - 1),
                        pltpu.VMEM((N_DEV, SMALL_ROWS, LANES), F32),
                        pltpu.SemaphoreType.DMA((21,)), pltpu.SemaphoreType.DMA((21,)),
                        pltpu.SemaphoreType.DMA(())])
    return pl.pallas_call(
        body, name="in_proj_bwd_exchange", grid_spec=grid_spec,
        out_shape=(jax.ShapeDtypeStruct((SEQ, D_MODEL), F32),
                   jax.ShapeDtypeStruct((D_MODEL, COL_BLOCK), F32), jax.ShapeDtypeStruct((WOUT_ROWS, D_MODEL), F32),
                   jax.ShapeDtypeStruct((SMALL_ROWS, LANES), F32)),
        compiler_params=_params(("arbitrary",)),
    )(order, hn_t, w_g, dproj_b, dproj_b, x, dx2, mix_w, dwout_p.reshape(n_chips, 2, WOUT_ROWS, D_MODEL), small_p)


def _adamw(w, g, m, v):
    m = ADAM_B1 * m + (1.0 - ADAM_B1) * g
    v = ADAM_B2 * v + (1.0 - ADAM_B2) * (g * g)
    m_hat = m / (1.0 - ADAM_B1 ** ADAM_STEP)
    v_hat = v / (1.0 - ADAM_B2 ** ADAM_STEP)
    delta = -ADAM_LR * (m_hat / (jnp.sqrt(v_hat) + ADAM_EPS) + ADAM_WD * w)
    return delta, m, v


def _adamw_update(grads, weights, m_old, v_old):
    rb = 256

    def body(*refs):
        g_refs, w_refs, m_refs, v_refs = refs[0:3], refs[3:6], refs[6:9], refs[9:12]
        d_refs, nm_refs, nv_refs = refs[12:15], refs[15:18], refs[18:21]
        for k in range(3):
            n_rows = g_refs[k].shape[0]
            step_rows = min(rb, n_rows)

            def step(b, carry, k=k, step_rows=step_rows):
                rows = pl.ds(pl.multiple_of(b * step_rows, 8), step_rows)
                delta, nm, nv = _adamw(w_refs[k][rows, :], g_refs[k][rows, :], m_refs[k][rows, :], v_refs[k][rows, :])
                d_refs[k][rows, :] = delta
                nm_refs[k][rows, :] = nm
                nv_refs[k][rows, :] = nv
                return carry

            lax.fori_loop(0, n_rows // step_rows, step, 0)

    shapes = tuple(jax.ShapeDtypeStruct(g.shape, F32) for g in grads)
    vm = lambda: pl.BlockSpec(memory_space=pltpu.VMEM)
    outs = pl.pallas_call(
        body, name="adamw_update", out_shape=shapes * 3,
        in_specs=[vm() for _ in range(12)], out_specs=tuple(vm() for _ in range(9)),
        compiler_params=_params(),
    )(*grads, *weights, *m_old, *v_old)
    return outs[0:3], outs[3:6], outs[6:9]


def _pack_small(mix, attn, hgrn, lb, final, loss=None):
    def rows8(a):
        a = a.reshape(-1, LANES)
        return jnp.pad(a, ((0, 8 - a.shape[0]), (0, 0)))
    last = jnp.zeros((8, LANES), F32) if loss is None else jnp.pad(loss.reshape(1, 1), ((0, 7), (0, LANES - 1)))
    return jnp.concatenate([rows8(mix), rows8(attn), rows8(hgrn), rows8(lb), rows8(final), last], axis=0)


def _unpack_small(slab):
    return (slab[ROW_MIX:ROW_MIX + 8].reshape(1, D_MODEL), slab[ROW_ATTN:ROW_ATTN + 4].reshape(1, ATTN_WIDTH),
            slab[ROW_HGRN:ROW_HGRN + 4].reshape(1, HGRN_WIDTH), slab[ROW_LB:ROW_LB + 8].reshape(2, HGRN_WIDTH),
            slab[ROW_FINAL:ROW_FINAL + 8].reshape(D_MODEL))


def _rope(pos_col):
    lane_e = np.arange(LANES) % HEAD_DIM
    inv = ROPE_THETA ** (-(lane_e % ROPE_HALF) * (2.0 / ROPE_DIMS))
    inv_lanes = np.where(lane_e < ROPE_DIMS, inv, 0.0).astype(np.float32).reshape(1, LANES)
    return _rope_tables(pos_col, jnp.asarray(inv_lanes))


def _local_step(x, proj, qkv_sorted, w_in_g, w_out_g, tables, mix_w, attn_w, hgrn_w, lb_raw, final_w, target):
    rc, rsa, rsb = tables
    attn_o, lse = _attn_fwd_fused(qkv_sorted)
    rec, states = _hgrn_fwd(proj, lb_raw)

    (dx2, d_o, delta, d_ag, d_rec, d_hg, dwout_p, d_final, d_attn_w, d_hgrn_w, loss) = _mid(
        attn_o, rec, proj, x, target, w_out_g, attn_w, hgrn_w, final_w.reshape(1, D_MODEL))

    dqkv = _attn_bwd_fused(qkv_sorted, d_o, lse, delta)
    d_hq, d_hf, d_hi, d_lb = _hgrn_bwd(proj, lb_raw, d_rec, states)

    dproj_b = _dproj_prep((dqkv[0], dqkv[1], dqkv[2], d_ag, d_hq, d_hf, d_hi, d_hg), rc, rsa, rsb)
    small_p = _pack_small(jnp.zeros((1, D_MODEL), F32), d_attn_w, d_hgrn_w, d_lb, d_final, loss)
    return dx2, dproj_b, dwout_p, small_p


def kernel(x, positions, w_in, w_out, mix_norm_w, attn_out_norm_w, hgrn_out_norm_w, hgrn_lb_raw, final_norm_w, loss_target, m_w_in, m_w_out, m_mix_norm_w, m_attn_out_norm_w, m_hgrn_out_norm_w, m_hgrn_lb_raw, m_final_norm_w, v_w_in, v_w_out, v_mix_norm_w, v_attn_out_norm_w, v_hgrn_out_norm_w, v_hgrn_lb_raw, v_final_norm_w):
    tables = _rope(positions.reshape(SEQ, 1))
    proj, hn_t, w_in_g, w_out_g, qkv_sorted = _gather_project(x[0], mix_norm_w, w_in[0], w_out[0], *tables)
    dx2, dproj_b, dwout_p, small_p = _local_step(
        x[0], proj, qkv_sorted, w_in_g, w_out_g, tables, mix_norm_w, attn_out_norm_w, hgrn_out_norm_w,
        hgrn_lb_raw, final_norm_w, loss_target[0])
    grad_x, g_in, g_out, g_s = _in_proj_bwd_exchange(
        hn_t, dproj_b, w_in_g, x[0], dx2, mix_norm_w, dwout_p, small_p)

    w_s = _pack_small(mix_norm_w, attn_out_norm_w, hgrn_out_norm_w, hgrn_lb_raw, final_norm_w)
    m_s = _pack_small(m_mix_norm_w, m_attn_out_norm_w, m_hgrn_out_norm_w, m_hgrn_lb_raw, m_final_norm_w)
    v_s = _pack_small(v_mix_norm_w, v_attn_out_norm_w, v_hgrn_out_norm_w, v_hgrn_lb_raw, v_final_norm_w)
    (d_in, d_out, d_s), (nm_in, nm_out, nm_s), (nv_in, nv_out, nv_s) = _adamw_update(
        (g_in, g_out, g_s), (w_in[0], w_out[0], w_s), (m_w_in[0], m_w_out[0], m_s), (v_w_in[0], v_w_out[0], v_s))

    loss = g_s[ROW_LOSS, 0]
    return (loss, grad_x[None], g_in[None], g_out[None], *_unpack_small(g_s),
            d_in[None], d_out[None], *_unpack_small(d_s),
            nm_in[None], nm_out[None], *_unpack_small(nm_s),
            nv_in[None], nv_out[None], *_unpack_small(nv_s))
```

```python
import functools

import jax
import jax.numpy as jnp
import numpy as np
from jax import lax
from jax.experimental import pallas as pl
from jax.experimental.pallas import tpu as pltpu

F32 = jnp.float32
BF16 = jnp.bfloat16

SEQ = 4096
D_MODEL = 1024
ATTN_WIDTH = 512
HGRN_WIDTH = 512
HEAD_DIM = 64
HGRN_HEADS = 4
HGRN_DIM = 128
HGRN_CHUNK = 64
N_CHUNKS = SEQ // HGRN_CHUNK
IN_COLS = 4096
COL_BLOCK = 512
N_DEV = 8
WOUT_ROWS = D_MODEL // N_DEV
ATTN_BLOCK = 128
DILATIONS = (1, 4, 16)
ROPE_THETA = 500000.0
ROPE_DIMS = 16
ROPE_HALF = 8
NORM_EPS = 1e-6
NEG_BIG = -1e30
LANES = 128

ADAM_LR = 0.001
ADAM_B1 = 0.9
ADAM_B2 = 0.999
ADAM_EPS = 1e-08
ADAM_WD = 0.01
ADAM_STEP = 10

SMALL_ROWS = 48
ROW_MIX, ROW_ATTN, ROW_HGRN, ROW_LB, ROW_FINAL, ROW_LOSS = 0, 8, 16, 24, 32, 40

VMEM_LIMIT = 56 * 1024 * 1024
MESH = pl.DeviceIdType.MESH


def _mm(a, b):
    return lax.dot_general(a, b, (((1,), (0,)), ((), ())), preferred_element_type=F32)


def _mm_nt(a, b):
    return lax.dot_general(a, b, (((1,), (1,)), ((), ())), preferred_element_type=F32)


def _mm_tn(a, b):
    return lax.dot_general(a, b, (((0,), (0,)), ((), ())), preferred_element_type=F32)


def _mm_exact(a, b):
    return lax.dot_general(a, b, (((1,), (0,)), ((), ())), preferred_element_type=F32,
                           precision=lax.Precision.HIGHEST)


def _sigmoid(v):
    return 1.0 / (1.0 + jnp.exp(-v))


def _params(sem=None, **kw):
    return pltpu.CompilerParams(dimension_semantics=sem, vmem_limit_bytes=VMEM_LIMIT, **kw)


def _my_place():
    return lax.axis_index("x"), lax.axis_index("y"), lax.axis_index("c")


def _peer(place, rel):
    x, y, c = place
    return (x ^ ((rel >> 2) & 1), y ^ ((rel >> 1) & 1), c ^ (rel & 1))


def _flat(place):
    x, y, c = place
    return 4 * x + 2 * y + c


def _rope_tables(pos_col, inv_freq_lanes):
    tm = 512

    def body(pos_ref, invf_ref, c_ref, sa_ref, sb_ref):
        ang = pos_ref[...].astype(F32) * invf_ref[...]
        e = lax.broadcasted_iota(jnp.int32, (tm, LANES), 1) & (HEAD_DIM - 1)
        cos, sin = jnp.cos(ang), jnp.sin(ang)
        c_ref[...] = jnp.where(e < ROPE_DIMS, cos, 1.0)
        sa_ref[...] = jnp.where((e >= ROPE_HALF) & (e < ROPE_DIMS), sin, 0.0)
        sb_ref[...] = jnp.where(e < ROPE_HALF, -sin, 0.0)

    tab = jax.ShapeDtypeStruct((SEQ, LANES), F32)
    spec = pl.BlockSpec((tm, LANES), lambda i: (i, 0))
    return pl.pallas_call(
        body, name="rope_tables", grid=(SEQ // tm,), out_shape=(tab, tab, tab),
        in_specs=[pl.BlockSpec((tm, 1), lambda i: (i, 0)), pl.BlockSpec((1, LANES), lambda i: (0, 0))],
        out_specs=(spec, spec, spec), compiler_params=_params(("parallel",)),
    )(pos_col, inv_freq_lanes)


def _per_slab(fn, t):
    return jnp.concatenate([fn(t[:, LANES * s:LANES * (s + 1)]) for s in range(t.shape[1] // LANES)], axis=1)


def _rot(t, c, sa, sb):
    return _per_slab(lambda u: u * c + pltpu.roll(u, ROPE_HALF, 1) * sa + pltpu.roll(u, LANES - ROPE_HALF, 1) * sb, t)


def _rot_transposed(g, c, sa, sb):
    return _per_slab(
        lambda u: u * c + pltpu.roll(u * sa, LANES - ROPE_HALF, 1) + pltpu.roll(u * sb, ROPE_HALF, 1), g)


def _gather_project(x, mix_w, w_in, w_out, rc, rsa, rsb):
    tm = 1024
    n_tiles = SEQ // tm
    arrival_of_step = (None, 0, 1, 2, 4, 5, 3, 6)

    def body(order_ref, x_ref, w_ref, win_ref, wout_ref, c_ref, sa_ref, sb_ref,
             proj_ref, hnt_ref, gin_hbm, gout_hbm, qkv_hbm,
             hn_s, w_land, wout_land, stage, sort_stage, slab_tmp, send_sems, recv_sems, local_sems):
        g, i = pl.program_id(0), pl.program_id(1)
        me = _my_place()
        x_, y_, c_ = me
        sibling = (x_, y_, 1 - c_)
        chips = [(1 - x_, y_), (x_, 1 - y_), (1 - x_, 1 - y_)]

        def slab(which, place):
            idx = _flat(place)
            if which == 0:
                return w_land.at[idx]
            return wout_land.at[pl.ds(pl.multiple_of(idx * WOUT_ROWS, WOUT_ROWS), WOUT_ROWS), :]

        def copy(which, k, block, to, src=None):
            ref = slab(which, block)
            return pltpu.make_async_remote_copy(
                src_ref=ref if src is None else src, dst_ref=ref, send_sem=send_sems.at[7 * which + k],
                recv_sem=recv_sems.at[7 * which + k], device_id=to, device_id_type=MESH)

        def first_copies(which):
            src = stage if which == 0 else None
            return ([copy(which, 0, me, sibling, src)]
                    + [copy(which, 1 + j, me, (*chip, c_), src) for j, chip in enumerate(chips)])

        def pass_on(which, j):
            return copy(which, 4 + j, (*chips[j], c_), sibling)

        def arrival(which, k):
            if k == 0:
                return copy(which, 0, sibling, me)
            if k <= 3:
                return copy(which, k, (*chips[k - 1], c_), me)
            return copy(which, k, (*chips[k - 4], 1 - c_), me)

        def to_hbm(step):
            idx = order_ref[step]
            cols = pl.ds(pl.multiple_of(idx * COL_BLOCK, COL_BLOCK), COL_BLOCK)
            return pltpu.make_async_copy(w_land.at[idx], gin_hbm.at[:, cols], local_sems.at[step])

        @pl.when((g == 0) & (i == 0))
        def _():
            stage[...] = win_ref[...].astype(BF16)
            w_land[_flat(me)] = stage[...]
            wout_land[pl.ds(pl.multiple_of(_flat(me) * WOUT_ROWS, WOUT_ROWS), WOUT_ROWS), :] = (
                wout_ref[...].astype(BF16))
            for cp in first_copies(0)[:3] + first_copies(1)[:1]:
                cp.start()
            to_hbm(0).start()

        for step, k in enumerate(arrival_of_step):
            if k is None:
                continue

            @pl.when((g == step) & (i == 0))
            def _(k=k, step=step):
                arrival(0, k).wait_recv()
                to_hbm(step).start()
                if k == 1:
                    for cp in first_copies(0)[3:] + first_copies(1)[1:]:
                        cp.start()
                if 1 <= k <= 3:
                    pass_on(0, k - 1).start()

        rows = pl.ds(pl.multiple_of(i * tm, tm), tm)

        @pl.when(g == 0)
        def _():
            xf = x_ref[...]
            ms = jnp.mean(xf * xf, axis=-1, keepdims=True)
            hn = xf * lax.rsqrt(ms + NORM_EPS) * w_ref[...]
            hnt_ref[...] = hn.T.astype(BF16)
            hn_s[rows, :] = hn.astype(BF16)

        group = order_ref[g]

        def sorted_copy():
            per = tm // SORT_RESIDUES
            for s in range(COL_BLOCK // LANES):
                slab_tmp[s] = proj_ref[:, LANES * s:LANES * (s + 1)]
            for r in range(SORT_RESIDUES):
                for s in range(COL_BLOCK // LANES):
                    sort_stage[r, :, LANES * s:LANES * (s + 1)] = (
                        slab_tmp.at[s][pl.ds(r, per, stride=SORT_RESIDUES), :])
            cols = pl.ds(pl.multiple_of(group * COL_BLOCK, COL_BLOCK), COL_BLOCK)
            cp = pltpu.make_async_copy(
                sort_stage, qkv_hbm.at[:, pl.ds(pl.multiple_of(i * per, per), per), cols], local_sems.at[N_DEV + 1])
            cp.start()
            cp.wait()

        @pl.when(group < 2)
        def _():
            proj_ref[...] = _rot(_mm(hn_s[rows, :], w_land[group]), c_ref[...], sa_ref[...], sb_ref[...])
            sorted_copy()

        @pl.when(group == 2)
        def _():
            proj_ref[...] = _mm(hn_s[rows, :], w_land[group])
            sorted_copy()

        @pl.when(group > 2)
        def _():
            proj_ref[...] = _mm(hn_s[rows, :], w_land[group])

        @pl.when((g == N_DEV - 1) & (i == n_tiles - 1))
        def _():
            for j in range(3):
                arrival(1, 1 + j).wait_recv()
                pass_on(1, j).start()
            for k in (0, 4, 5, 6):
                arrival(1, k).wait_recv()
            for which in (0, 1):
                for cp in first_copies(which) + [pass_on(which, j) for j in range(3)]:
                    cp.wait_send()
            wout_copy = pltpu.make_async_copy(wout_land, gout_hbm, local_sems.at[N_DEV])
            wout_copy.start()
            for step in range(N_DEV):
                to_hbm(step).wait()
            wout_copy.wait()

    me = _my_place()
    x_, y_, c_ = me
    chips = [(1 - x_, y_), (x_, 1 - y_), (1 - x_, 1 - y_)]
    order = jnp.stack([_flat(p) for p in (
        me, (x_, y_, 1 - c_), (*chips[0], c_), (*chips[1], c_), (*chips[0], 1 - c_), (*chips[1], 1 - c_),
        (*chips[2], c_), (*chips[2], 1 - c_))]).astype(jnp.int32)

    first_sweep = lambda g, i, order: (jnp.where(g == 0, i, n_tiles - 1), 0)
    tab = pl.BlockSpec((tm, LANES), lambda g, i, order: (jnp.where(order[g] < 2, i, 0), 0))
    whole = lambda: pl.BlockSpec(memory_space=pltpu.VMEM)
    grid_spec = pltpu.PrefetchScalarGridSpec(
        num_scalar_prefetch=1, grid=(N_DEV, n_tiles),
        in_specs=[pl.BlockSpec((tm, D_MODEL), first_sweep),
                  pl.BlockSpec((1, D_MODEL), lambda g, i, order: (0, 0)),
                  whole(), whole(), tab, tab, tab],
        out_specs=(pl.BlockSpec((tm, COL_BLOCK), lambda g, i, order: (i, order[g])),
                   pl.BlockSpec((D_MODEL, tm), lambda g, i, order: (0, jnp.where(g == 0, i, n_tiles - 1))),
                   pl.BlockSpec(memory_space=pl.ANY), pl.BlockSpec(memory_space=pl.ANY),
                   pl.BlockSpec(memory_space=pl.ANY)),
        scratch_shapes=[pltpu.VMEM((SEQ, D_MODEL), BF16),
                        pltpu.VMEM((N_DEV, D_MODEL, COL_BLOCK), BF16),
                        pltpu.VMEM((D_MODEL, D_MODEL), BF16),
                        pltpu.VMEM((D_MODEL, COL_BLOCK), BF16),
                        pltpu.VMEM((SORT_RESIDUES, tm // SORT_RESIDUES, COL_BLOCK), F32),
                        pltpu.VMEM((COL_BLOCK // LANES, tm, LANES), F32),
                        pltpu.SemaphoreType.DMA((14,)), pltpu.SemaphoreType.DMA((14,)),
                        pltpu.SemaphoreType.DMA((N_DEV + 2,))])
    proj, hn_t, w_in_g, w_out_g, qkv_sorted = pl.pallas_call(
        body, name="gather_project", grid_spec=grid_spec,
        out_shape=(jax.ShapeDtypeStruct((SEQ, IN_COLS), F32), jax.ShapeDtypeStruct((D_MODEL, SEQ), BF16),
                   jax.ShapeDtypeStruct((D_MODEL, IN_COLS), BF16), jax.ShapeDtypeStruct((D_MODEL, D_MODEL), BF16),
                   jax.ShapeDtypeStruct((SORT_RESIDUES, SORT_ROWS, 3 * COL_BLOCK), F32)),
        compiler_params=_params(("arbitrary", "arbitrary")),
    )(order, x, mix_w, w_in, w_out, rc, rsa, rsb)
    return proj, hn_t, w_in_g, w_out_g, qkv_sorted.reshape(SEQ, 3 * COL_BLOCK)


ATTN_GROUP = 8
BLOCKS_PER_PATTERN = SEQ // ATTN_BLOCK
SORT_RESIDUES = 16
SORT_ROWS = SEQ // SORT_RESIDUES


def _write_band_bias(bias_ref):
    row = lax.broadcasted_iota(jnp.int32, (2 * ATTN_BLOCK, 2 * ATTN_BLOCK), 0) & (ATTN_BLOCK - 1)
    col = lax.broadcasted_iota(jnp.int32, (2 * ATTN_BLOCK, 2 * ATTN_BLOCK), 1)
    for pi, d in enumerate(DILATIONS):
        per = SORT_RESIDUES // d
        ahead = per * (row % (8 * d) - col % (16 * d)) + (row // (8 * d) - col // (16 * d))
        dist = ATTN_BLOCK + ahead
        bias_ref[2 * pi] = jnp.where((dist >= 0) & (dist <= ATTN_BLOCK), 0.0, NEG_BIG)
        bias_ref[2 * pi + 1] = jnp.where(ahead >= 0, 0.0, NEG_BIG)


def _head0_lanes():
    return lax.broadcasted_iota(jnp.int32, (ATTN_BLOCK, LANES), 1) < HEAD_DIM


def _stack_heads(t, h0):
    return jnp.concatenate([jnp.where(h0, t, 0.0), jnp.where(h0, 0.0, t)], axis=0).astype(BF16)


def _block_runs(i, d):
    nblk = BLOCKS_PER_PATTERN // d
    r, n = i // nblk, i % nblk
    kn = jnp.maximum(n - 1, 0)
    rows, keys = [], []
    for c in range(SORT_RESIDUES // d):
        base = SORT_ROWS * (c * d + r)
        rows.append(pl.ds(pl.multiple_of(base + 8 * d * n, 8), 8 * d))
        keys.append(pl.ds(pl.multiple_of(base + 8 * d * kn, 8), 16 * d))
    return rows, keys, (n == 0).astype(jnp.int32)


def _take(ref, runs):
    return jnp.concatenate([ref[run, :] for run in runs], axis=0)


def _put(ref, runs, value, add=False):
    at = 0
    for run in runs:
        piece = value[at:at + run.size]
        if add:
            ref[run, :] += piece
        else:
            ref[run, :] = piece
        at += run.size


def _sort_rows(src_ref, dst_ref):
    for r in range(SORT_RESIDUES):
        dst_ref[SORT_ROWS * r:SORT_ROWS * (r + 1), :] = src_ref[pl.ds(r, SORT_ROWS, stride=SORT_RESIDUES), :]


def _unsort_rows(src_ref, dst_ref):
    for r in range(SORT_RESIDUES):
        dst_ref[pl.ds(r, SORT_ROWS, stride=SORT_RESIDUES), :] = src_ref[SORT_ROWS * r:SORT_ROWS * (r + 1), :]


def _for_each_group(d, load, compute, store):
    def group(g, carry):
        items = [load(*_block_runs(g * ATTN_GROUP + u, d)) for u in range(ATTN_GROUP)]
        results = [compute(item) for item in items]
        for item, res in zip(items, results):
            store(item, res)
        return carry

    lax.fori_loop(0, BLOCKS_PER_PATTERN // ATTN_GROUP, group, 0)


def _attn_fwd_fused(qkv_sorted):
    n_pat = len(DILATIONS)
    tile2 = (2 * ATTN_BLOCK, LANES)

    def body(q_ref, k_ref, v_ref, o_ref, lse_ref, o_acc, m_acc, l_acc, bias_ref):
        pl.when(pl.program_id(0) == 0)(lambda: _write_band_bias(bias_ref))
        h0 = _head0_lanes()
        for pi, d in enumerate(DILATIONS):
            first, last = pi == 0, pi == n_pat - 1

            def load(rows, keys, which, first=first, pi=pi):
                item = dict(rows=rows, keys=keys, which=2 * pi + which)
                if not first:
                    item.update(o=_take(o_acc, rows), m=[_take(m_acc.at[h], rows) for h in range(2)],
                                l=[_take(l_acc.at[h], rows) for h in range(2)])
                return item

            def compute(item, first=first):
                kb = _take(k_ref, item["keys"]).astype(BF16)
                vb = _take(v_ref, item["keys"]).astype(BF16)
                s = _mm_nt(_stack_heads(_take(q_ref, item["rows"]), h0), kb) * 0.125 + bias_ref[item["which"]]
                mb = jnp.max(s, axis=-1, keepdims=True)
                if first:
                    p = jnp.exp(s - mb)
                    mn = jnp.broadcast_to(mb, tile2)
                else:
                    m_old = jnp.concatenate(item["m"], axis=0)
                    mn = jnp.maximum(m_old, mb)
                    alpha = jnp.exp(m_old - mn)
                    p = jnp.exp(s - jnp.concatenate([mn, mn], axis=1))
                ls = jnp.sum(p, axis=-1, keepdims=True)
                pv = _mm(p.astype(BF16), vb)
                if first:
                    return pv, mn, jnp.broadcast_to(ls, tile2)
                o_old = jnp.concatenate([item["o"], item["o"]], axis=0)
                return alpha * o_old + pv, mn, alpha * jnp.concatenate(item["l"], axis=0) + ls

            def store(item, res, last=last):
                rows = item["rows"]
                (o0, o1), (m0, m1), (l0, l1) = ((a[:ATTN_BLOCK], a[ATTN_BLOCK:]) for a in res)
                if last:
                    _put(o_acc, rows, jnp.where(h0, o0 / l0, o1 / l1))
                    _put(lse_ref, rows, jnp.where(h0, m0 + jnp.log(l0), m1 + jnp.log(l1)))
                else:
                    _put(o_acc, rows, jnp.where(h0, o0, o1))
                    for h, (m, l) in enumerate(((m0, l0), (m1, l1))):
                        _put(m_acc.at[h], rows, m)
                        _put(l_acc.at[h], rows, l)

            _for_each_group(d, load, compute, store)
        _unsort_rows(o_acc, o_ref)

    slab = lambda g: pl.BlockSpec((SEQ, LANES), functools.partial(lambda hp, g: (0, 4 * g + hp), g=g))
    wide = jax.ShapeDtypeStruct((SEQ, ATTN_WIDTH), F32)
    return pl.pallas_call(
        body, name="attn_fwd", grid=(4,), out_shape=(wide, wide),
        in_specs=[slab(0), slab(1), slab(2)], out_specs=(slab(0), slab(0)),
        scratch_shapes=[pltpu.VMEM((SEQ, LANES), F32), pltpu.VMEM((2, SEQ, LANES), F32),
                        pltpu.VMEM((2, SEQ, LANES), F32),
                        pltpu.VMEM((2 * len(DILATIONS), 2 * ATTN_BLOCK, 2 * ATTN_BLOCK), F32)],
        compiler_params=_params(("arbitrary",)),
    )(qkv_sorted, qkv_sorted, qkv_sorted)


def _attn_bwd_fused(qkv_sorted, d_out, lse_sorted, delta):
    def body(q_ref, k_ref, v_ref, do_ref, lse_ref, del_ref, dq_ref, dk_ref, dv_ref,
             do_s, del_s, dq_s, dk_s, dv_s, bias_ref):
        pl.when(pl.program_id(0) == 0)(lambda: _write_band_bias(bias_ref))
        _sort_rows(do_ref, do_s)
        _sort_rows(del_ref, del_s)
        dk_s[...] = jnp.zeros_like(dk_s)
        dv_s[...] = jnp.zeros_like(dv_s)
        h0 = _head0_lanes()
        for pi, d in enumerate(DILATIONS):
            first = pi == 0

            def load(rows, keys, which, pi=pi):
                return dict(rows=rows, keys=keys, q=_take(q_ref, rows), g=_take(do_s, rows),
                            lse=_take(lse_ref, rows), delta=_take(del_s, rows),
                            k=_take(k_ref, keys).astype(BF16), v=_take(v_ref, keys).astype(BF16),
                            bias=bias_ref[2 * pi + which])

            def per_head(t):
                swapped = pltpu.roll(t, HEAD_DIM, 1)
                both = jnp.concatenate([jnp.where(h0, t, swapped), jnp.where(h0, swapped, t)], axis=0)
                return jnp.concatenate([both, both], axis=1)

            def compute(item):
                q2, g2 = _stack_heads(item["q"], h0), _stack_heads(item["g"], h0)
                s = _mm_nt(q2, item["k"]) * 0.125 + item["bias"]
                p = jnp.exp(s - per_head(item["lse"]))
                dp = _mm_nt(g2, item["v"])
                ds = (p * (dp - per_head(item["delta"])) * 0.125).astype(BF16)
                dq2 = _mm(ds, item["k"])
                dq = jnp.where(h0, dq2[:ATTN_BLOCK], dq2[ATTN_BLOCK:])
                return dq, _mm_tn(ds, q2), _mm_tn(p.astype(BF16), g2)

            def store(item, res, first=first):
                _put(dq_s, item["rows"], res[0], add=not first)
                _put(dk_s, item["keys"], res[1], add=True)
                _put(dv_s, item["keys"], res[2], add=True)

            _for_each_group(d, load, compute, store)
        _unsort_rows(dq_s, dq_ref)
        _unsort_rows(dk_s, dk_ref)
        _unsort_rows(dv_s, dv_ref)

    slab = lambda g: pl.BlockSpec((SEQ, LANES), functools.partial(lambda hp, g: (0, 4 * g + hp), g=g))
    wide = jax.ShapeDtypeStruct((SEQ, ATTN_WIDTH), F32)
    sorted_slab = pltpu.VMEM((SEQ, LANES), F32)
    return pl.pallas_call(
        body, name="attn_bwd", grid=(4,), out_shape=(wide, wide, wide),
        scratch_shapes=[sorted_slab] * 5 + [pltpu.VMEM((2 * len(DILATIONS), 2 * ATTN_BLOCK, 2 * ATTN_BLOCK), F32)],
        in_specs=[slab(0), slab(1), slab(2), slab(0), slab(0), slab(0)], out_specs=(slab(0), slab(0), slab(0)),
        compiler_params=_params(("arbitrary",)),
    )(qkv_sorted, qkv_sorted, qkv_sorted, d_out, lse_sorted, delta)


def _hgrn_lower_bound(lb_ref):
    r0, r1 = lb_ref[0:1, :], lb_ref[1:2, :]
    mx = jnp.maximum(r0, r1)
    e0, e1 = jnp.exp(r0 - mx), jnp.exp(r1 - mx)
    return e0 / (e0 + e1)


def _hgrn_gates(hq, hf, lb):
    sq = _sigmoid(hq)
    sg = _sigmoid(hf)
    f = lb + (1.0 - lb) * sg
    return hq * sq, sq, sg, f, 1.0 - f, jnp.log(f)


HGRN_PAIR = 4
HGRN_SEQ_BLOCK = 1024
HGRN_GROUP = 4
HGRN_ROWS = HGRN_GROUP * HGRN_CHUNK


def _hgrn_specs(reverse):
    n_blocks = SEQ // HGRN_SEQ_BLOCK
    width = HGRN_PAIR * HGRN_DIM
    blk = (lambda s: n_blocks - 1 - s) if reverse else (lambda s: s)
    cols = lambda g: pl.BlockSpec((HGRN_SEQ_BLOCK, width),
                                  functools.partial(lambda p, s, g: (blk(s), (HGRN_HEADS // HGRN_PAIR) * g + p), g=g))
    pair = pl.BlockSpec((HGRN_SEQ_BLOCK, width), lambda p, s: (blk(s), p))
    lb = pl.BlockSpec((2, width), lambda p, s: (0, p))
    states = pl.BlockSpec((HGRN_PAIR, HGRN_SEQ_BLOCK // HGRN_CHUNK, HGRN_DIM, HGRN_DIM),
                          lambda p, s: (p, blk(s), 0, 0))
    return cols, pair, lb, states


def _chunk_masks():
    ri = lax.broadcasted_iota(jnp.int32, (HGRN_ROWS, HGRN_ROWS), 0)
    ci = lax.broadcasted_iota(jnp.int32, (HGRN_ROWS, HGRN_ROWS), 1)
    same = (ri // HGRN_CHUNK) == (ci // HGRN_CHUNK)
    return same, same & (ri >= ci), same & (ri <= ci)


def _mm_select(sel, v):
    hi = v.astype(BF16)
    r1 = v - hi.astype(F32)
    mid = r1.astype(BF16)
    lo = (r1 - mid.astype(F32)).astype(BF16)
    return _mm(sel, hi) + _mm(sel, mid) + _mm(sel, lo)


def _head_cols(a, h):
    return a[:, HGRN_DIM * h:HGRN_DIM * (h + 1)]


def _hgrn_fwd(proj, lb_raw):
    t, rws = HGRN_CHUNK, HGRN_ROWS

    def body(hq_ref, hf_ref, hi_ref, lb_ref, rec_ref, st_ref, state):
        @pl.when(pl.program_id(1) == 0)
        def _():
            state[...] = jnp.zeros_like(state)

        lb = _hgrn_lower_bound(lb_ref)
        same, causal, _ = _chunk_masks()
        sel = jnp.concatenate([causal, same], axis=0).astype(BF16)

        def group(g, sts):
            rows = pl.ds(pl.multiple_of(g * rws, rws), rws)
            q, _, _, _, k, lf = _hgrn_gates(hq_ref[rows, :], hf_ref[rows, :], lb)
            sums = _mm_select(sel, lf)
            cum, last = sums[:rws], sums[rws:]
            qd = (q * jnp.exp(cum)).astype(BF16)
            ki = (k * jnp.exp(-cum)).astype(BF16)
            ke = (k * jnp.exp(last - cum)).astype(BF16)
            vb = hi_ref[rows, :].astype(BF16)
            dec = jnp.exp(last)
            new_sts, recs = [], []
            for h in range(HGRN_PAIR):
                qd_h, ke_h, vb_h = _head_cols(qd, h), _head_cols(ke, h), _head_cols(vb, h)
                att = jnp.where(causal, _mm_nt(qd_h, _head_cols(ki, h)), 0.0).astype(BF16)
                intra = _mm(att, vb_h)
                st = sts[h]
                outs = []
                for c in range(HGRN_GROUP):
                    sl = slice(c * t, (c + 1) * t)
                    st_ref[h, g * HGRN_GROUP + c] = st
                    outs.append(intra[sl] + _mm_nt(qd_h[sl], st.astype(BF16)))
                    st = st * _head_cols(dec[c * t:c * t + 1, :], h) + _mm_tn(vb_h[sl], ke_h[sl])
                new_sts.append(st)
                recs.append(jnp.concatenate(outs, axis=0))
            rec_ref[rows, :] = jnp.concatenate(recs, axis=1)
            return tuple(new_sts)

        sts = lax.fori_loop(0, HGRN_SEQ_BLOCK // rws, group, tuple(state[h] for h in range(HGRN_PAIR)))
        for h in range(HGRN_PAIR):
            state[h] = sts[h]

    cols, pair, lb, states = _hgrn_specs(reverse=False)
    return pl.pallas_call(
        body, name="hgrn_fwd", grid=(HGRN_HEADS // HGRN_PAIR, SEQ // HGRN_SEQ_BLOCK),
        out_shape=(jax.ShapeDtypeStruct((SEQ, HGRN_WIDTH), F32),
                   jax.ShapeDtypeStruct((HGRN_HEADS, N_CHUNKS, HGRN_DIM, HGRN_DIM), F32)),
        in_specs=[cols(4), cols(5), cols(6), lb], out_specs=(pair, states),
        scratch_shapes=[pltpu.VMEM((HGRN_PAIR, HGRN_DIM, HGRN_DIM), F32)],
        compiler_params=_params(("parallel", "arbitrary")),
    )(proj, proj, proj, lb_raw)


def _hgrn_bwd(proj, lb_raw, d_rec, states):
    t, rws = HGRN_CHUNK, HGRN_ROWS

    def body(hq_ref, hf_ref, hi_ref, lb_ref, do_ref, st_ref, dhq_ref, dhf_ref, dhi_ref, dlb_ref,
             dstate, dlb_acc):
        lb = _hgrn_lower_bound(lb_ref)
        same, causal, anti = _chunk_masks()
        sel = jnp.concatenate([causal, same], axis=0).astype(BF16)
        sel_t = jnp.concatenate([anti, same], axis=1).astype(BF16)
        @pl.when(pl.program_id(1) == 0)
        def _():
            dstate[...] = jnp.zeros_like(dstate)
            dlb_acc[...] = jnp.zeros_like(dlb_acc)

        n_groups = HGRN_SEQ_BLOCK // rws
        chunks = [slice(c * t, (c + 1) * t) for c in range(HGRN_GROUP)]

        def group(i, dsts_in):
            g = n_groups - 1 - i
            rows = pl.ds(pl.multiple_of(g * rws, rws), rws)
            hq = hq_ref[rows, :]
            q, sq, sg, f, k, lf = _hgrn_gates(hq, hf_ref[rows, :], lb)
            sums = _mm_select(sel, lf)
            cum, last = sums[:rws], sums[rws:]
            e_cum, e_inv, e_end, dec = jnp.exp(cum), jnp.exp(-cum), jnp.exp(last - cum), jnp.exp(last)
            qd, ki, ke = q * e_cum, k * e_inv, k * e_end
            qdb, kib, keb = qd.astype(BF16), ki.astype(BF16), ke.astype(BF16)
            vb = hi_ref[rows, :].astype(BF16)
            gb = do_ref[rows, :].astype(BF16)

            dsts_out, per_head = [], []
            for h in range(HGRN_PAIR):
                qdb_h, kib_h, keb_h = _head_cols(qdb, h), _head_cols(kib, h), _head_cols(keb, h)
                vb_h, gb_h = _head_cols(vb, h), _head_cols(gb, h)
                att = jnp.where(causal, _mm_nt(qdb_h, kib_h), 0.0).astype(BF16)
                datt = jnp.where(causal, _mm_nt(gb_h, vb_h), 0.0).astype(BF16)
                dv = _mm_tn(att, gb_h)
                dqd = _mm(datt, kib_h)
                dki = _mm_tn(datt, qdb_h)

                decs = [_head_cols(dec[c * t:c * t + 1, :], h) for c in range(HGRN_GROUP)]
                dsts = [None] * HGRN_GROUP
                dst = dsts_in[h]
                for c in reversed(range(HGRN_GROUP)):
                    dsts[c] = dst
                    dst = dst * decs[c] + _mm_tn(gb_h[chunks[c]], qdb_h[chunks[c]])
                dsts_out.append(dst)

                dv_x, dqd_x, dke, dlast_x = [], [], [], []
                for c, sl in enumerate(chunks):
                    st_prev = st_ref[h, g * HGRN_GROUP + c]
                    dstb = dsts[c].astype(BF16)
                    dv_x.append(_mm_nt(keb_h[sl], dstb))
                    dqd_x.append(_mm(gb_h[sl], st_prev.astype(BF16)))
                    dke.append(_mm(vb_h[sl], dstb))
                    ddec = jnp.sum(dsts[c] * st_prev, axis=0, keepdims=True)
                    dlast_x.append(jnp.broadcast_to(ddec * decs[c], (t, HGRN_DIM)))
                per_head.append((dv + jnp.concatenate(dv_x, axis=0), dqd + jnp.concatenate(dqd_x, axis=0),
                                 dki, jnp.concatenate(dke, axis=0), jnp.concatenate(dlast_x, axis=0)))
            dv, dqd, dki, dke, dlast = (jnp.concatenate(list(parts), axis=1) for parts in zip(*per_head))

            dq = dqd * e_cum
            dk = dki * e_inv + dke * e_end
            dke_ke = dke * ke
            dcum = dqd * qd - dki * ki - dke_ke
            dlf = _mm_select(sel_t, jnp.concatenate([dcum, dke_ke], axis=0)) + dlast
            df = dlf / f - dk
            dhq_ref[rows, :] = dq * (sq * (1.0 + hq * (1.0 - sq)))
            dhf_ref[rows, :] = df * (1.0 - lb) * (sg * (1.0 - sg))
            dhi_ref[rows, :] = dv
            dlb_acc[...] += jnp.sum(df * (1.0 - sg), axis=0, keepdims=True)
            return tuple(dsts_out)

        dsts = lax.fori_loop(0, n_groups, group, tuple(dstate[h] for h in range(HGRN_PAIR)))
        for h in range(HGRN_PAIR):
            dstate[h] = dsts[h]
        g0 = dlb_acc[...] * lb * (1.0 - lb)
        dlb_ref[...] = jnp.concatenate([g0, -g0], axis=0)

    cols, pair, lb_spec, st_spec = _hgrn_specs(reverse=True)
    wide = jax.ShapeDtypeStruct((SEQ, HGRN_WIDTH), F32)
    return pl.pallas_call(
        body, name="hgrn_bwd", grid=(HGRN_HEADS // HGRN_PAIR, SEQ // HGRN_SEQ_BLOCK),
        out_shape=(wide, wide, wide, jax.ShapeDtypeStruct((2, HGRN_WIDTH), F32)),
        in_specs=[cols(4), cols(5), cols(6), lb_spec, pair, st_spec],
        out_specs=(pair, pair, pair, lb_spec),
        scratch_shapes=[pltpu.VMEM((HGRN_PAIR, HGRN_DIM, HGRN_DIM), F32),
                        pltpu.VMEM((1, HGRN_PAIR * HGRN_DIM), F32)],
        compiler_params=_params(("parallel", "arbitrary")),
    )(proj, proj, proj, lb_raw, d_rec, states)


def _group_sum(v, group):
    parts = []
    for s in range(v.shape[1] // LANES):
        slab = v[:, LANES * s:LANES * (s + 1)]
        if group == LANES:
            parts.append(jnp.broadcast_to(jnp.sum(slab, axis=-1, keepdims=True), slab.shape))
        else:
            h0 = lax.broadcasted_iota(jnp.int32, slab.shape, 1) < HEAD_DIM
            s0 = jnp.sum(jnp.where(h0, slab, 0.0), axis=-1, keepdims=True)
            s1 = jnp.sum(jnp.where(h0, 0.0, slab), axis=-1, keepdims=True)
            parts.append(jnp.where(h0, s0, s1))
    return jnp.concatenate(parts, axis=1)


def _mid(attn_o, rec, proj, x, target, w_out_g, attn_w, hgrn_w, final_w):
    tm = 256

    def branch_fwd(o, gate, w, group):
        r = lax.rsqrt(_group_sum(o * o, group) * (1.0 / group) + NORM_EPS)
        nrm = o * r
        sg = _sigmoid(gate)
        return r, nrm, sg, nrm * w * (gate * sg)

    def branch_bwd(dy, r, nrm, sg, gate, w, group):
        silu = gate * sg
        d_gate = dy * nrm * w * (sg * (1.0 + gate * (1.0 - sg)))
        d_w = jnp.sum(dy * nrm * silu, axis=0, keepdims=True)
        dn = dy * w * silu
        d_o = r * (dn - nrm * (_group_sum(dn * nrm, group) * (1.0 / group)))
        return d_o, d_gate, d_w

    def body(o_ref, rec_ref, ag_ref, hg_ref, x_ref, tgt_ref, wout_ref, aw_ref, hw_ref, fw_ref,
             dx2_ref, do_ref, delta_ref, dag_ref, drec_ref, dhg_ref, dwout_ref, dfw_ref, daw_ref, dhw_ref,
             loss_ref, dwout_acc):
        i = pl.program_id(0)

        @pl.when(i == 0)
        def _():
            dwout_acc[...] = jnp.zeros_like(dwout_acc)
            dfw_ref[...] = jnp.zeros_like(dfw_ref)
            daw_ref[...] = jnp.zeros_like(daw_ref)
            dhw_ref[...] = jnp.zeros_like(dhw_ref)
            loss_ref[...] = jnp.zeros_like(loss_ref)

        o, rc, ag, hg = o_ref[...], rec_ref[...], ag_ref[...], hg_ref[...]
        aw, hw, fw = aw_ref[...], hw_ref[...], fw_ref[...]
        ra, na, sga, ya = branch_fwd(o, ag, aw, HEAD_DIM)
        rh, nh, sgh, yh = branch_fwd(rc, hg, hw, HGRN_DIM)
        mixed = jnp.concatenate([ya, yh], axis=1).astype(BF16)
        wout = wout_ref[...]
        x2 = x_ref[...] + _mm(mixed, wout)
        rstd = lax.rsqrt(jnp.mean(x2 * x2, axis=-1, keepdims=True) + NORM_EPS)
        xn = x2 * rstd
        err = xn * fw - tgt_ref[...]
        row_loss = jnp.mean(err * err, axis=-1, keepdims=True)
        loss_ref[...] += 0.5 * jnp.sum(row_loss, axis=0, keepdims=True)
        dy = err * (1.0 / D_MODEL)
        dfw_ref[...] += jnp.sum(dy * xn, axis=0, keepdims=True)
        dxn = dy * fw
        dx2 = rstd * (dxn - xn * jnp.mean(dxn * xn, axis=-1, keepdims=True))
        dx2_ref[...] = dx2
        dx2b = dx2.astype(BF16)
        dwout_acc[...] += _mm_tn(mixed, dx2b)

        @pl.when(i == pl.num_programs(0) - 1)
        def _():
            dwout_ref[...] = dwout_acc[...].astype(BF16)

        dmixed = _mm_nt(dx2b, wout)

        d_o, d_ag, d_aw = branch_bwd(dmixed[:, :ATTN_WIDTH], ra, na, sga, ag, aw, HEAD_DIM)
        d_rec, d_hg, d_hw = branch_bwd(dmixed[:, ATTN_WIDTH:], rh, nh, sgh, hg, hw, HGRN_DIM)
        do_ref[...] = d_o
        delta_ref[...] = _group_sum(d_o * o, HEAD_DIM)
        dag_ref[...] = d_ag
        drec_ref[...] = d_rec
        dhg_ref[...] = d_hg
        daw_ref[...] += d_aw
        dhw_ref[...] += d_hw

    half = lambda: pl.BlockSpec((tm, COL_BLOCK), lambda i: (i, 0))
    full = lambda: pl.BlockSpec((tm, D_MODEL), lambda i: (i, 0))
    fixed = lambda r, c: pl.BlockSpec((r, c), lambda i: (0, 0))
    wide = jax.ShapeDtypeStruct((SEQ, COL_BLOCK), F32)
    return pl.pallas_call(
        body, name="mid", grid=(SEQ // tm,),
        out_shape=(jax.ShapeDtypeStruct((SEQ, D_MODEL), F32), wide, wide, wide, wide, wide,
                   jax.ShapeDtypeStruct((D_MODEL, D_MODEL), BF16),
                   jax.ShapeDtypeStruct((1, D_MODEL), F32), jax.ShapeDtypeStruct((1, COL_BLOCK), F32),
                   jax.ShapeDtypeStruct((1, COL_BLOCK), F32), jax.ShapeDtypeStruct((1, 1), F32)),
        scratch_shapes=[pltpu.VMEM((D_MODEL, D_MODEL), F32)],
        in_specs=[half(), half(),
                  pl.BlockSpec((tm, COL_BLOCK), lambda i: (i, 3)), pl.BlockSpec((tm, COL_BLOCK), lambda i: (i, 7)),
                  full(), full(), fixed(D_MODEL, D_MODEL), fixed(1, COL_BLOCK), fixed(1, COL_BLOCK),
                  fixed(1, D_MODEL)],
        out_specs=(full(), half(), half(), half(), half(), half(), fixed(D_MODEL, D_MODEL),
                   fixed(1, D_MODEL), fixed(1, COL_BLOCK), fixed(1, COL_BLOCK), fixed(1, 1)),
        compiler_params=_params(("arbitrary",)),
    )(attn_o, rec, proj, proj, x, target, w_out_g, attn_w, hgrn_w, final_w)


def _in_proj_bwd_rows(d_groups, w_g, x, dx2, mix_w, rc, rsa, rsb):
    tm = 256

    def body(*refs):
        dg_refs = refs[:N_DEV]
        wg_ref, x_ref, dx2_ref, w_ref, c_ref, sa_ref, sb_ref, gx_ref, dpb_ref, dmw_ref = refs[N_DEV:]

        @pl.when(pl.program_id(0) == 0)
        def _():
            dmw_ref[...] = jnp.zeros_like(dmw_ref)

        parts = []
        for j in range(N_DEV):
            dp = dg_refs[j][...]
            if j < 2:
                dp = _rot_transposed(dp, c_ref[...], sa_ref[...], sb_ref[...])
            parts.append(dp.astype(BF16))
        dpb = jnp.concatenate(parts, axis=1)
        dpb_ref[...] = dpb
        g = _mm_nt(dpb, wg_ref[...])
        xf = x_ref[...]
        rstd = lax.rsqrt(jnp.mean(xf * xf, axis=-1, keepdims=True) + NORM_EPS)
        xn = xf * rstd
        dmw_ref[...] += jnp.sum(g * xn, axis=0, keepdims=True)
        gw = g * w_ref[...]
        gx_ref[...] = dx2_ref[...] + rstd * (gw - xn * jnp.mean(gw * xn, axis=-1, keepdims=True))

    tile = lambda cols: pl.BlockSpec((tm, cols), lambda i: (i, 0))
    fixed = lambda r, c: pl.BlockSpec((r, c), lambda i: (0, 0))
    return pl.pallas_call(
        body, name="in_proj_bwd_rows", grid=(SEQ // tm,),
        out_shape=(jax.ShapeDtypeStruct((SEQ, D_MODEL), F32), jax.ShapeDtypeStruct((SEQ, IN_COLS), BF16),
                   jax.ShapeDtypeStruct((1, D_MODEL), F32)),
        in_specs=[tile(COL_BLOCK) for _ in range(N_DEV)] + [
            pl.BlockSpec((D_MODEL, IN_COLS), lambda i: (0, 0), pipeline_mode=pl.Buffered(1)),
            tile(D_MODEL), tile(D_MODEL), fixed(1, D_MODEL), tile(LANES), tile(LANES), tile(LANES)],
        out_specs=(tile(D_MODEL), tile(IN_COLS), fixed(1, D_MODEL)),
        compiler_params=_params(("arbitrary",)),
    )(*d_groups, w_g, x, dx2, mix_w, rc, rsa, rsb)


def _weights_exchange(hn_t, dproj_b, dwout_p, small_p):
    n_chips = N_DEV // 2
    rb = 128
    S1_IN, S1_OUT, SMALL, S2_IN, S2_OUT = 0, 4, 8, 15, 18
    rel_of_pair = (1, 2, 3, 0)

    def body(order_ref, hnt_ref, dp_ref, dwout_ref, small_ref, gin_ref, gout_ref, gs_ref,
             part, s1_send, s1_in, s1_out, fwd_in, fwd_out, s2_in, s2_out, land_s, send_sems, recv_sems):
        t = pl.program_id(0)
        me = _my_place()
        x, y, c = me
        my_chip = 2 * x + y
        sibling = (x, y, 1 - c)

        def remote(slot, src, dst, to):
            return pltpu.make_async_remote_copy(src_ref=src, dst_ref=dst, send_sem=send_sems.at[slot],
                                                recv_sem=recv_sems.at[slot], device_id=to, device_id_type=MESH)

        def s1_in_copy(pair):
            return remote(S1_IN + pair, s1_send.at[pair], s1_in.at[pair], sibling)

        def s1_out_copy(pair):
            q = my_chip ^ rel_of_pair[pair]
            return remote(S1_OUT + pair, dwout_ref.at[q, 1 - c], s1_out.at[pair], sibling)

        def s2_copies(rel):
            peer = _peer(me, 2 * rel)
            return [remote(S2_IN + rel - 1, fwd_in.at[rel - 1], s2_in.at[rel - 1], peer),
                    remote(S2_OUT + rel - 1, fwd_out.at[rel - 1], s2_out.at[rel - 1], peer)]

        def small_copy(rel):
            return remote(SMALL + rel - 1, small_ref, land_s.at[rel], _peer(me, rel))

        @pl.when(t == 0)
        def _():
            land_s[0] = small_ref[...]
            for pair in range(n_chips):
                s1_out_copy(pair).start()
            for rel in range(1, N_DEV):
                small_copy(rel).start()

        part[...] = _mm(hnt_ref[...], dp_ref[...])

        def rows_loop(n_rows, fn):
            def step(b, carry):
                fn(pl.ds(pl.multiple_of(b * rb, rb), rb))
                return carry
            lax.fori_loop(0, n_rows // rb, step, 0)

        for pair, rel in enumerate(rel_of_pair):
            @pl.when(t == 2 * pair)
            def _(pair=pair):
                s1_send[pair] = part[...].astype(BF16)
                s1_in_copy(pair).start()

            @pl.when(t == 2 * pair + 1)
            def _(pair=pair, rel=rel):
                q = my_chip ^ rel
                s1_in_copy(pair).wait_recv()
                s1_out_copy(pair).wait_recv()
                dst_in = fwd_in.at[rel - 1] if rel else gin_ref
                dst_out = fwd_out.at[rel - 1] if rel else gout_ref

                def add_in(rows):
                    dst_in[rows, :] = (part[rows, :] + s1_in[pair, rows, :].astype(F32)).astype(dst_in.dtype)

                def add_out(rows):
                    dst_out[rows, :] = (dwout_ref[q, c, rows, :].astype(F32)
                                        + s1_out[pair, rows, :].astype(F32)).astype(dst_out.dtype)

                rows_loop(D_MODEL, add_in)
                rows_loop(WOUT_ROWS, add_out)
                if rel:
                    for cp in s2_copies(rel):
                        cp.start()

        @pl.when(t == N_DEV - 1)
        def _():
            for rel in range(1, n_chips):
                for cp in s2_copies(rel):
                    cp.wait_recv()

            def total_in(rows):
                g = gin_ref[rows, :]
                for rel in range(1, n_chips):
                    g = g + s2_in[rel - 1, rows, :].astype(F32)
                gin_ref[rows, :] = g

            def total_out(rows):
                g = gout_ref[rows, :]
                for rel in range(1, n_chips):
                    g = g + s2_out[rel - 1, rows, :].astype(F32)
                gout_ref[rows, :] = g

            rows_loop(D_MODEL, total_in)
            rows_loop(WOUT_ROWS, total_out)

            for rel in range(1, N_DEV):
                small_copy(rel).wait_recv()
            my_flat = _flat(me)
            g = land_s[my_flat ^ 0]
            for dev in range(1, N_DEV):
                g = g + land_s[my_flat ^ dev]
            gs_ref[...] = g

            for pair in range(n_chips):
                s1_in_copy(pair).wait_send()
                s1_out_copy(pair).wait_send()
            for rel in range(1, n_chips):
                for cp in s2_copies(rel):
                    cp.wait_send()
            for rel in range(1, N_DEV):
                small_copy(rel).wait_send()

    place_x, place_y, place_c = _my_place()
    my_chip = 2 * place_x + place_y
    order = jnp.stack([2 * (my_chip ^ rel) + core for rel in rel_of_pair
                       for core in (1 - place_c, place_c)]).astype(jnp.int32)

    whole = lambda: pl.BlockSpec(memory_space=pltpu.VMEM)
    in_blocks = lambda n: pltpu.VMEM((n, D_MODEL, COL_BLOCK), BF16)
    out_blocks = lambda n: pltpu.VMEM((n, WOUT_ROWS, D_MODEL), BF16)
    grid_spec = pltpu.PrefetchScalarGridSpec(
        num_scalar_prefetch=1, grid=(N_DEV,),
        in_specs=[pl.BlockSpec((D_MODEL, SEQ), lambda t, order: (0, 0), pipeline_mode=pl.Buffered(1)),
                  pl.BlockSpec((SEQ, COL_BLOCK), lambda t, order: (0, order[t])), whole(), whole()],
        out_specs=(whole(), whole(), whole()),
        scratch_shapes=[pltpu.VMEM((D_MODEL, COL_BLOCK), F32), in_blocks(n_chips), in_blocks(n_chips),
                        out_blocks(n_chips), in_blocks(n_chips - 1), out_blocks(n_chips - 1),
                        in_blocks(n_chips - 1), out_blocks(n_chips - 1),
                        pltpu.VMEM((N_DEV, SMALL_ROWS, LANES), F32),
                        pltpu.SemaphoreType.DMA((21,)), pltpu.SemaphoreType.DMA((21,))])
    return pl.pallas_call(
        body, name="weights_exchange", grid_spec=grid_spec,
        out_shape=(jax.ShapeDtypeStruct((D_MODEL, COL_BLOCK), F32), jax.ShapeDtypeStruct((WOUT_ROWS, D_MODEL), F32),
                   jax.ShapeDtypeStruct((SMALL_ROWS, LANES), F32)),
        compiler_params=_params(("arbitrary",)),
    )(order, hn_t, dproj_b, dwout_p.reshape(n_chips, 2, WOUT_ROWS, D_MODEL), small_p)


def _adamw(w, g, m, v):
    m = ADAM_B1 * m + (1.0 - ADAM_B1) * g
    v = ADAM_B2 * v + (1.0 - ADAM_B2) * (g * g)
    m_hat = m / (1.0 - ADAM_B1 ** ADAM_STEP)
    v_hat = v / (1.0 - ADAM_B2 ** ADAM_STEP)
    delta = -ADAM_LR * (m_hat / (jnp.sqrt(v_hat) + ADAM_EPS) + ADAM_WD * w)
    return delta, m, v


def _adamw_update(grads, weights, m_old, v_old):
    rb = 256

    def body(*refs):
        g_refs, w_refs, m_refs, v_refs = refs[0:3], refs[3:6], refs[6:9], refs[9:12]
        d_refs, nm_refs, nv_refs = refs[12:15], refs[15:18], refs[18:21]
        for k in range(3):
            n_rows = g_refs[k].shape[0]
            step_rows = min(rb, n_rows)

            def step(b, carry, k=k, step_rows=step_rows):
                rows = pl.ds(pl.multiple_of(b * step_rows, 8), step_rows)
                delta, nm, nv = _adamw(w_refs[k][rows, :], g_refs[k][rows, :], m_refs[k][rows, :], v_refs[k][rows, :])
                d_refs[k][rows, :] = delta
                nm_refs[k][rows, :] = nm
                nv_refs[k][rows, :] = nv
                return carry

            lax.fori_loop(0, n_rows // step_rows, step, 0)

    shapes = tuple(jax.ShapeDtypeStruct(g.shape, F32) for g in grads)
    vm = lambda: pl.BlockSpec(memory_space=pltpu.VMEM)
    outs = pl.pallas_call(
        body, name="adamw_update", out_shape=shapes * 3,
        in_specs=[vm() for _ in range(12)], out_specs=tuple(vm() for _ in range(9)),
        compiler_params=_params(),
    )(*grads, *weights, *m_old, *v_old)
    return outs[0:3], outs[3:6], outs[6:9]


def _pack_small(mix, attn, hgrn, lb, final, loss=None):
    def rows8(a):
        a = a.reshape(-1, LANES)
        return jnp.pad(a, ((0, 8 - a.shape[0]), (0, 0)))
    last = jnp.zeros((8, LANES), F32) if loss is None else jnp.pad(loss.reshape(1, 1), ((0, 7), (0, LANES - 1)))
    return jnp.concatenate([rows8(mix), rows8(attn), rows8(hgrn), rows8(lb), rows8(final), last], axis=0)


def _unpack_small(slab):
    return (slab[ROW_MIX:ROW_MIX + 8].reshape(1, D_MODEL), slab[ROW_ATTN:ROW_ATTN + 4].reshape(1, ATTN_WIDTH),
            slab[ROW_HGRN:ROW_HGRN + 4].reshape(1, HGRN_WIDTH), slab[ROW_LB:ROW_LB + 8].reshape(2, HGRN_WIDTH),
            slab[ROW_FINAL:ROW_FINAL + 8].reshape(D_MODEL))


def _rope(pos_col):
    lane_e = np.arange(LANES) % HEAD_DIM
    inv = ROPE_THETA ** (-(lane_e % ROPE_HALF) * (2.0 / ROPE_DIMS))
    inv_lanes = np.where(lane_e < ROPE_DIMS, inv, 0.0).astype(np.float32).reshape(1, LANES)
    return _rope_tables(pos_col, jnp.asarray(inv_lanes))


def _local_step(x, proj, qkv_sorted, w_in_g, w_out_g, tables, mix_w, attn_w, hgrn_w, lb_raw, final_w, target):
    rc, rsa, rsb = tables
    attn_o, lse = _attn_fwd_fused(qkv_sorted)
    rec, states = _hgrn_fwd(proj, lb_raw)

    (dx2, d_o, delta, d_ag, d_rec, d_hg, dwout_p, d_final, d_attn_w, d_hgrn_w, loss) = _mid(
        attn_o, rec, proj, x, target, w_out_g, attn_w, hgrn_w, final_w.reshape(1, D_MODEL))

    dqkv = _attn_bwd_fused(qkv_sorted, d_o, lse, delta)
    d_hq, d_hf, d_hi, d_lb = _hgrn_bwd(proj, lb_raw, d_rec, states)

    grad_x, dproj_b, d_mix = _in_proj_bwd_rows(
        (dqkv[0], dqkv[1], dqkv[2], d_ag, d_hq, d_hf, d_hi, d_hg), w_in_g, x, dx2, mix_w, rc, rsa, rsb)
    small_p = _pack_small(d_mix, d_attn_w, d_hgrn_w, d_lb, d_final, loss)
    return grad_x, dproj_b, dwout_p, small_p


def kernel(x, positions, w_in, w_out, mix_norm_w, attn_out_norm_w, hgrn_out_norm_w, hgrn_lb_raw, final_norm_w, loss_target, m_w_in, m_w_out, m_mix_norm_w, m_attn_out_norm_w, m_hgrn_out_norm_w, m_hgrn_lb_raw, m_final_norm_w, v_w_in, v_w_out, v_mix_norm_w, v_attn_out_norm_w, v_hgrn_out_norm_w, v_hgrn_lb_raw, v_final_norm_w):
    tables = _rope(positions.reshape(SEQ, 1))
    proj, hn_t, w_in_g, w_out_g, qkv_sorted = _gather_project(x[0], mix_norm_w, w_in[0], w_out[0], *tables)
    grad_x, dproj_b, dwout_p, small_p = _local_step(
        x[0], proj, qkv_sorted, w_in_g, w_out_g, tables, mix_norm_w, attn_out_norm_w, hgrn_out_norm_w,
        hgrn_lb_raw, final_norm_w, loss_target[0])
    g_in, g_out, g_s = _weights_exchange(hn_t, dproj_b, dwout_p, small_p)

    w_s = _pack_small(mix_norm_w, attn_out_norm_w, hgrn_out_norm_w, hgrn_lb_raw, final_norm_w)
    m_s = _pack_small(m_mix_norm_w, m_attn_out_norm_w, m_hgrn_out_norm_w, m_hgrn_lb_raw, m_final_norm_w)
    v_s = _pack_small(v_mix_norm_w, v_attn_out_norm_w, v_hgrn_out_norm_w, v_hgrn_lb_raw, v_final_norm_w)
    (d_in, d_out, d_s), (nm_in, nm_out, nm_s), (nv_in, nv_out, nv_s) = _adamw_update(
        (g_in, g_out, g_s), (w_in[0], w_out[0], w_s), (m_w_in[0], m_w_out[0], m_s), (v_w_in[0], v_w_out[0], v_s))

    loss = g_s[ROW_LOSS, 0]
    return (loss, grad_x[None], g_in[None], g_out[None], *_unpack_small(g_s),
            d_in[None], d_out[None], *_unpack_small(d_s),
            nm_in[None], nm_out[None], *_unpack_small(nm_s),
            nv_in[None], nv_out[None], *_unpack_small(nv_s))
```

```python
import functools

import jax
import jax.numpy as jnp
import numpy as np
from jax import lax
from jax.experimental import pallas as pl
from jax.experimental.pallas import tpu as pltpu

F32 = jnp.float32
BF16 = jnp.bfloat16

SEQ = 4096
D_MODEL = 1024
ATTN_WIDTH = 512
HGRN_WIDTH = 512
HEAD_DIM = 64
HGRN_HEADS = 4
HGRN_DIM = 128
HGRN_CHUNK = 64
N_CHUNKS = SEQ // HGRN_CHUNK
IN_COLS = 4096
COL_BLOCK = 512
N_DEV = 8
WOUT_ROWS = D_MODEL // N_DEV
ATTN_BLOCK = 128
DILATIONS = (1, 4, 16)
ROPE_THETA = 500000.0
ROPE_DIMS = 16
ROPE_HALF = 8
NORM_EPS = 1e-6
NEG_BIG = -1e30
LANES = 128

ADAM_LR = 0.001
ADAM_B1 = 0.9
ADAM_B2 = 0.999
ADAM_EPS = 1e-08
ADAM_WD = 0.01
ADAM_STEP = 10

SMALL_ROWS = 48
ROW_MIX, ROW_ATTN, ROW_HGRN, ROW_LB, ROW_FINAL, ROW_LOSS = 0, 8, 16, 24, 32, 40

VMEM_LIMIT = 56 * 1024 * 1024
MESH = pl.DeviceIdType.MESH


def _mm(a, b):
    return lax.dot_general(a, b, (((1,), (0,)), ((), ())), preferred_element_type=F32)


def _mm_nt(a, b):
    return lax.dot_general(a, b, (((1,), (1,)), ((), ())), preferred_element_type=F32)


def _mm_tn(a, b):
    return lax.dot_general(a, b, (((0,), (0,)), ((), ())), preferred_element_type=F32)


def _mm_exact(a, b):
    return lax.dot_general(a, b, (((1,), (0,)), ((), ())), preferred_element_type=F32,
                           precision=lax.Precision.HIGHEST)


def _sigmoid(v):
    return 1.0 / (1.0 + jnp.exp(-v))


def _params(sem=None, **kw):
    return pltpu.CompilerParams(dimension_semantics=sem, vmem_limit_bytes=VMEM_LIMIT, **kw)


def _my_place():
    return lax.axis_index("x"), lax.axis_index("y"), lax.axis_index("c")


def _peer(place, rel):
    x, y, c = place
    return (x ^ ((rel >> 2) & 1), y ^ ((rel >> 1) & 1), c ^ (rel & 1))


def _flat(place):
    x, y, c = place
    return 4 * x + 2 * y + c


def _rope_tables(pos_col, inv_freq_lanes):
    tm = 512

    def body(pos_ref, invf_ref, c_ref, sa_ref, sb_ref):
        ang = pos_ref[...].astype(F32) * invf_ref[...]
        e = lax.broadcasted_iota(jnp.int32, (tm, LANES), 1) & (HEAD_DIM - 1)
        cos, sin = jnp.cos(ang), jnp.sin(ang)
        c_ref[...] = jnp.where(e < ROPE_DIMS, cos, 1.0)
        sa_ref[...] = jnp.where((e >= ROPE_HALF) & (e < ROPE_DIMS), sin, 0.0)
        sb_ref[...] = jnp.where(e < ROPE_HALF, -sin, 0.0)

    tab = jax.ShapeDtypeStruct((SEQ, LANES), F32)
    spec = pl.BlockSpec((tm, LANES), lambda i: (i, 0))
    return pl.pallas_call(
        body, name="rope_tables", grid=(SEQ // tm,), out_shape=(tab, tab, tab),
        in_specs=[pl.BlockSpec((tm, 1), lambda i: (i, 0)), pl.BlockSpec((1, LANES), lambda i: (0, 0))],
        out_specs=(spec, spec, spec), compiler_params=_params(("parallel",)),
    )(pos_col, inv_freq_lanes)


def _per_slab(fn, t):
    return jnp.concatenate([fn(t[:, LANES * s:LANES * (s + 1)]) for s in range(t.shape[1] // LANES)], axis=1)


def _rot(t, c, sa, sb):
    return _per_slab(lambda u: u * c + pltpu.roll(u, ROPE_HALF, 1) * sa + pltpu.roll(u, LANES - ROPE_HALF, 1) * sb, t)


def _rot_transposed(g, c, sa, sb):
    return _per_slab(
        lambda u: u * c + pltpu.roll(u * sa, LANES - ROPE_HALF, 1) + pltpu.roll(u * sb, ROPE_HALF, 1), g)


def _gather_project(x, mix_w, w_in, w_out, rc, rsa, rsb):
    tm = 1024
    n_tiles = SEQ // tm
    arrival_of_step = (None, 0, 1, 2, 4, 5, 3, 6)

    def body(order_ref, x_ref, w_ref, win_ref, wout_ref, c_ref, sa_ref, sb_ref,
             proj_ref, hnt_ref, gin_hbm, gout_hbm, qkv_hbm,
             hn_s, w_land, wout_land, stage, sort_stage, slab_tmp, send_sems, recv_sems, local_sems):
        g, i = pl.program_id(0), pl.program_id(1)
        me = _my_place()
        x_, y_, c_ = me
        sibling = (x_, y_, 1 - c_)
        chips = [(1 - x_, y_), (x_, 1 - y_), (1 - x_, 1 - y_)]

        def slab(which, place):
            idx = _flat(place)
            if which == 0:
                return w_land.at[idx]
            return wout_land.at[pl.ds(pl.multiple_of(idx * WOUT_ROWS, WOUT_ROWS), WOUT_ROWS), :]

        def copy(which, k, block, to, src=None):
            ref = slab(which, block)
            return pltpu.make_async_remote_copy(
                src_ref=ref if src is None else src, dst_ref=ref, send_sem=send_sems.at[7 * which + k],
                recv_sem=recv_sems.at[7 * which + k], device_id=to, device_id_type=MESH)

        def first_copies(which):
            src = stage if which == 0 else None
            return ([copy(which, 0, me, sibling, src)]
                    + [copy(which, 1 + j, me, (*chip, c_), src) for j, chip in enumerate(chips)])

        def pass_on(which, j):
            return copy(which, 4 + j, (*chips[j], c_), sibling)

        def arrival(which, k):
            if k == 0:
                return copy(which, 0, sibling, me)
            if k <= 3:
                return copy(which, k, (*chips[k - 1], c_), me)
            return copy(which, k, (*chips[k - 4], 1 - c_), me)

        def to_hbm(step):
            idx = order_ref[step]
            cols = pl.ds(pl.multiple_of(idx * COL_BLOCK, COL_BLOCK), COL_BLOCK)
            return pltpu.make_async_copy(w_land.at[idx], gin_hbm.at[:, cols], local_sems.at[step])

        @pl.when((g == 0) & (i == 0))
        def _():
            stage[...] = win_ref[...].astype(BF16)
            w_land[_flat(me)] = stage[...]
            wout_land[pl.ds(pl.multiple_of(_flat(me) * WOUT_ROWS, WOUT_ROWS), WOUT_ROWS), :] = (
                wout_ref[...].astype(BF16))
            for cp in first_copies(0)[:1] + first_copies(1)[:1]:
                cp.start()
            to_hbm(0).start()

        sidx = g * n_tiles + i
        early, late = 2 * n_tiles, 6 * n_tiles

        def piece(j, ci, n_pieces):
            rows = pl.ds(pl.multiple_of(ci * (D_MODEL // n_pieces), D_MODEL // n_pieces), D_MODEL // n_pieces)
            return pltpu.make_async_remote_copy(
                src_ref=stage.at[rows, :], dst_ref=w_land.at[_flat(me), rows, :], send_sem=send_sems.at[1 + j],
                recv_sem=recv_sems.at[1 + j], device_id=(*chips[j], c_), device_id_type=MESH)

        @pl.when(sidx < early)
        def _():
            per_step = 2
            for u in range(per_step):
                for j in (0, 1):
                    piece(j, sidx * per_step + u, per_step * early).start()

        @pl.when((sidx >= early) & (sidx < late))
        def _():
            piece(2, sidx - early, late - early).start()

        for j in range(3):
            @pl.when(sidx == early + 1 + 4 * j)
            def _(j=j):
                first_copies(1)[1 + j].start()

        for step, k in enumerate(arrival_of_step):
            if k is None:
                continue

            @pl.when((g == step) & (i == 0))
            def _(k=k, step=step):
                arrival(0, k).wait_recv()
                to_hbm(step).start()
                if 1 <= k <= 3:
                    pass_on(0, k - 1).start()

        rows = pl.ds(pl.multiple_of(i * tm, tm), tm)

        @pl.when(g == 0)
        def _():
            xf = x_ref[...]
            ms = jnp.mean(xf * xf, axis=-1, keepdims=True)
            hn = xf * lax.rsqrt(ms + NORM_EPS) * w_ref[...]
            hnt_ref[...] = hn.T.astype(BF16)
            hn_s[rows, :] = hn.astype(BF16)

        group = order_ref[g]

        def sorted_copy():
            per = tm // SORT_RESIDUES
            for s in range(COL_BLOCK // LANES):
                slab_tmp[s] = proj_ref[:, LANES * s:LANES * (s + 1)]
            for r in range(SORT_RESIDUES):
                for s in range(COL_BLOCK // LANES):
                    sort_stage[r, :, LANES * s:LANES * (s + 1)] = (
                        slab_tmp.at[s][pl.ds(r, per, stride=SORT_RESIDUES), :])
            cols = pl.ds(pl.multiple_of(group * COL_BLOCK, COL_BLOCK), COL_BLOCK)
            cp = pltpu.make_async_copy(
                sort_stage, qkv_hbm.at[:, pl.ds(pl.multiple_of(i * per, per), per), cols], local_sems.at[N_DEV + 1])
            cp.start()
            cp.wait()

        @pl.when(group < 2)
        def _():
            proj_ref[...] = _rot(_mm(hn_s[rows, :], w_land[group]), c_ref[...], sa_ref[...], sb_ref[...])
            sorted_copy()

        @pl.when(group == 2)
        def _():
            proj_ref[...] = _mm(hn_s[rows, :], w_land[group])
            sorted_copy()

        @pl.when(group > 2)
        def _():
            proj_ref[...] = _mm(hn_s[rows, :], w_land[group])

        @pl.when((g == N_DEV - 1) & (i == n_tiles - 1))
        def _():
            for j in range(3):
                arrival(1, 1 + j).wait_recv()
                pass_on(1, j).start()
            for k in (0, 4, 5, 6):
                arrival(1, k).wait_recv()
            for which in (0, 1):
                for cp in first_copies(which) + [pass_on(which, j) for j in range(3)]:
                    cp.wait_send()
            wout_copy = pltpu.make_async_copy(wout_land, gout_hbm, local_sems.at[N_DEV])
            wout_copy.start()
            for step in range(N_DEV):
                to_hbm(step).wait()
            wout_copy.wait()

    me = _my_place()
    x_, y_, c_ = me
    chips = [(1 - x_, y_), (x_, 1 - y_), (1 - x_, 1 - y_)]
    order = jnp.stack([_flat(p) for p in (
        me, (x_, y_, 1 - c_), (*chips[0], c_), (*chips[1], c_), (*chips[0], 1 - c_), (*chips[1], 1 - c_),
        (*chips[2], c_), (*chips[2], 1 - c_))]).astype(jnp.int32)

    first_sweep = lambda g, i, order: (jnp.where(g == 0, i, n_tiles - 1), 0)
    tab = pl.BlockSpec((tm, LANES), lambda g, i, order: (jnp.where(order[g] < 2, i, 0), 0))
    whole = lambda: pl.BlockSpec(memory_space=pltpu.VMEM)
    grid_spec = pltpu.PrefetchScalarGridSpec(
        num_scalar_prefetch=1, grid=(N_DEV, n_tiles),
        in_specs=[pl.BlockSpec((tm, D_MODEL), first_sweep),
                  pl.BlockSpec((1, D_MODEL), lambda g, i, order: (0, 0)),
                  whole(), whole(), tab, tab, tab],
        out_specs=(pl.BlockSpec((tm, COL_BLOCK), lambda g, i, order: (i, order[g])),
                   pl.BlockSpec((D_MODEL, tm), lambda g, i, order: (0, jnp.where(g == 0, i, n_tiles - 1))),
                   pl.BlockSpec(memory_space=pl.ANY), pl.BlockSpec(memory_space=pl.ANY),
                   pl.BlockSpec(memory_space=pl.ANY)),
        scratch_shapes=[pltpu.VMEM((SEQ, D_MODEL), BF16),
                        pltpu.VMEM((N_DEV, D_MODEL, COL_BLOCK), BF16),
                        pltpu.VMEM((D_MODEL, D_MODEL), BF16),
                        pltpu.VMEM((D_MODEL, COL_BLOCK), BF16),
                        pltpu.VMEM((SORT_RESIDUES, tm // SORT_RESIDUES, COL_BLOCK), F32),
                        pltpu.VMEM((COL_BLOCK // LANES, tm, LANES), F32),
                        pltpu.SemaphoreType.DMA((14,)), pltpu.SemaphoreType.DMA((14,)),
                        pltpu.SemaphoreType.DMA((N_DEV + 2,))])
    proj, hn_t, w_in_g, w_out_g, qkv_sorted = pl.pallas_call(
        body, name="gather_project", grid_spec=grid_spec,
        out_shape=(jax.ShapeDtypeStruct((SEQ, IN_COLS), F32), jax.ShapeDtypeStruct((D_MODEL, SEQ), BF16),
                   jax.ShapeDtypeStruct((D_MODEL, IN_COLS), BF16), jax.ShapeDtypeStruct((D_MODEL, D_MODEL), BF16),
                   jax.ShapeDtypeStruct((SORT_RESIDUES, SORT_ROWS, 3 * COL_BLOCK), F32)),
        compiler_params=_params(("arbitrary", "arbitrary")),
    )(order, x, mix_w, w_in, w_out, rc, rsa, rsb)
    return proj, hn_t, w_in_g, w_out_g, qkv_sorted.reshape(SEQ, 3 * COL_BLOCK)


ATTN_GROUP = 8
BLOCKS_PER_PATTERN = SEQ // ATTN_BLOCK
SORT_RESIDUES = 16
SORT_ROWS = SEQ // SORT_RESIDUES


def _write_band_bias(bias_ref):
    row = lax.broadcasted_iota(jnp.int32, (2 * ATTN_BLOCK, 2 * ATTN_BLOCK), 0) & (ATTN_BLOCK - 1)
    col = lax.broadcasted_iota(jnp.int32, (2 * ATTN_BLOCK, 2 * ATTN_BLOCK), 1)
    for pi, d in enumerate(DILATIONS):
        per = SORT_RESIDUES // d
        ahead = per * (row % (8 * d) - col % (16 * d)) + (row // (8 * d) - col // (16 * d))
        dist = ATTN_BLOCK + ahead
        bias_ref[2 * pi] = jnp.where((dist >= 0) & (dist <= ATTN_BLOCK), 0.0, NEG_BIG)
        bias_ref[2 * pi + 1] = jnp.where(ahead >= 0, 0.0, NEG_BIG)


def _head0_lanes():
    return lax.broadcasted_iota(jnp.int32, (ATTN_BLOCK, LANES), 1) < HEAD_DIM


def _stack_heads(t, h0):
    return jnp.concatenate([jnp.where(h0, t, 0.0), jnp.where(h0, 0.0, t)], axis=0).astype(BF16)


def _block_runs(i, d):
    nblk = BLOCKS_PER_PATTERN // d
    r, n = i // nblk, i % nblk
    kn = jnp.maximum(n - 1, 0)
    rows, keys = [], []
    for c in range(SORT_RESIDUES // d):
        base = SORT_ROWS * (c * d + r)
        rows.append(pl.ds(pl.multiple_of(base + 8 * d * n, 8), 8 * d))
        keys.append(pl.ds(pl.multiple_of(base + 8 * d * kn, 8), 16 * d))
    return rows, keys, (n == 0).astype(jnp.int32)


def _take(ref, runs):
    return jnp.concatenate([ref[run, :] for run in runs], axis=0)


def _put(ref, runs, value, add=False):
    at = 0
    for run in runs:
        piece = value[at:at + run.size]
        if add:
            ref[run, :] += piece
        else:
            ref[run, :] = piece
        at += run.size


def _sort_rows(src_ref, dst_ref):
    for r in range(SORT_RESIDUES):
        dst_ref[SORT_ROWS * r:SORT_ROWS * (r + 1), :] = src_ref[pl.ds(r, SORT_ROWS, stride=SORT_RESIDUES), :]


def _unsort_rows(src_ref, dst_ref):
    for r in range(SORT_RESIDUES):
        dst_ref[pl.ds(r, SORT_ROWS, stride=SORT_RESIDUES), :] = src_ref[SORT_ROWS * r:SORT_ROWS * (r + 1), :]


def _for_each_group(d, load, compute, store):
    def group(g, carry):
        items = [load(*_block_runs(g * ATTN_GROUP + u, d)) for u in range(ATTN_GROUP)]
        results = [compute(item) for item in items]
        for item, res in zip(items, results):
            store(item, res)
        return carry

    lax.fori_loop(0, BLOCKS_PER_PATTERN // ATTN_GROUP, group, 0)


def _attn_fwd_fused(qkv_sorted):
    n_pat = len(DILATIONS)
    tile2 = (2 * ATTN_BLOCK, LANES)

    def body(q_ref, k_ref, v_ref, o_ref, lse_ref, o_acc, m_acc, l_acc, bias_ref):
        pl.when(pl.program_id(0) == 0)(lambda: _write_band_bias(bias_ref))
        h0 = _head0_lanes()
        for pi, d in enumerate(DILATIONS):
            first, last = pi == 0, pi == n_pat - 1

            def load(rows, keys, which, first=first, pi=pi):
                item = dict(rows=rows, keys=keys, which=2 * pi + which)
                if not first:
                    item.update(o=_take(o_acc, rows), m=[_take(m_acc.at[h], rows) for h in range(2)],
                                l=[_take(l_acc.at[h], rows) for h in range(2)])
                return item

            def compute(item, first=first):
                kb = _take(k_ref, item["keys"]).astype(BF16)
                vb = _take(v_ref, item["keys"]).astype(BF16)
                s = _mm_nt(_stack_heads(_take(q_ref, item["rows"]), h0), kb) * 0.125 + bias_ref[item["which"]]
                mb = jnp.max(s, axis=-1, keepdims=True)
                if first:
                    p = jnp.exp(s - mb)
                    mn = jnp.broadcast_to(mb, tile2)
                else:
                    m_old = jnp.concatenate(item["m"], axis=0)
                    mn = jnp.maximum(m_old, mb)
                    alpha = jnp.exp(m_old - mn)
                    p = jnp.exp(s - jnp.concatenate([mn, mn], axis=1))
                ls = jnp.sum(p, axis=-1, keepdims=True)
                pv = _mm(p.astype(BF16), vb)
                if first:
                    return pv, mn, jnp.broadcast_to(ls, tile2)
                o_old = jnp.concatenate([item["o"], item["o"]], axis=0)
                return alpha * o_old + pv, mn, alpha * jnp.concatenate(item["l"], axis=0) + ls

            def store(item, res, last=last):
                rows = item["rows"]
                (o0, o1), (m0, m1), (l0, l1) = ((a[:ATTN_BLOCK], a[ATTN_BLOCK:]) for a in res)
                if last:
                    _put(o_acc, rows, jnp.where(h0, o0 / l0, o1 / l1))
                    _put(lse_ref, rows, jnp.where(h0, m0 + jnp.log(l0), m1 + jnp.log(l1)))
                else:
                    _put(o_acc, rows, jnp.where(h0, o0, o1))
                    for h, (m, l) in enumerate(((m0, l0), (m1, l1))):
                        _put(m_acc.at[h], rows, m)
                        _put(l_acc.at[h], rows, l)

            _for_each_group(d, load, compute, store)
        _unsort_rows(o_acc, o_ref)

    slab = lambda g: pl.BlockSpec((SEQ, LANES), functools.partial(lambda hp, g: (0, 4 * g + hp), g=g))
    wide = jax.ShapeDtypeStruct((SEQ, ATTN_WIDTH), F32)
    return pl.pallas_call(
        body, name="attn_fwd", grid=(4,), out_shape=(wide, wide),
        in_specs=[slab(0), slab(1), slab(2)], out_specs=(slab(0), slab(0)),
        scratch_shapes=[pltpu.VMEM((SEQ, LANES), F32), pltpu.VMEM((2, SEQ, LANES), F32),
                        pltpu.VMEM((2, SEQ, LANES), F32),
                        pltpu.VMEM((2 * len(DILATIONS), 2 * ATTN_BLOCK, 2 * ATTN_BLOCK), F32)],
        compiler_params=_params(("arbitrary",)),
    )(qkv_sorted, qkv_sorted, qkv_sorted)


def _attn_bwd_fused(qkv_sorted, d_out, lse_sorted, delta):
    def body(q_ref, k_ref, v_ref, do_ref, lse_ref, del_ref, dq_ref, dk_ref, dv_ref,
             do_s, del_s, dq_s, dk_s, dv_s, bias_ref):
        pl.when(pl.program_id(0) == 0)(lambda: _write_band_bias(bias_ref))
        _sort_rows(do_ref, do_s)
        _sort_rows(del_ref, del_s)
        dk_s[...] = jnp.zeros_like(dk_s)
        dv_s[...] = jnp.zeros_like(dv_s)
        h0 = _head0_lanes()
        for pi, d in enumerate(DILATIONS):
            first = pi == 0

            def load(rows, keys, which, pi=pi):
                return dict(rows=rows, keys=keys, q=_take(q_ref, rows), g=_take(do_s, rows),
                            lse=_take(lse_ref, rows), delta=_take(del_s, rows),
                            k=_take(k_ref, keys).astype(BF16), v=_take(v_ref, keys).astype(BF16),
                            bias=bias_ref[2 * pi + which])

            def per_head(t):
                swapped = pltpu.roll(t, HEAD_DIM, 1)
                both = jnp.concatenate([jnp.where(h0, t, swapped), jnp.where(h0, swapped, t)], axis=0)
                return jnp.concatenate([both, both], axis=1)

            def compute(item):
                q2, g2 = _stack_heads(item["q"], h0), _stack_heads(item["g"], h0)
                s = _mm_nt(q2, item["k"]) * 0.125 + item["bias"]
                p = jnp.exp(s - per_head(item["lse"]))
                dp = _mm_nt(g2, item["v"])
                ds = (p * (dp - per_head(item["delta"])) * 0.125).astype(BF16)
                dq2 = _mm(ds, item["k"])
                dq = jnp.where(h0, dq2[:ATTN_BLOCK], dq2[ATTN_BLOCK:])
                return dq, _mm_tn(ds, q2), _mm_tn(p.astype(BF16), g2)

            def store(item, res, first=first):
                _put(dq_s, item["rows"], res[0], add=not first)
                _put(dk_s, item["keys"], res[1], add=True)
                _put(dv_s, item["keys"], res[2], add=True)

            _for_each_group(d, load, compute, store)
        _unsort_rows(dq_s, dq_ref)
        _unsort_rows(dk_s, dk_ref)
        _unsort_rows(dv_s, dv_ref)

    slab = lambda g: pl.BlockSpec((SEQ, LANES), functools.partial(lambda hp, g: (0, 4 * g + hp), g=g))
    wide = jax.ShapeDtypeStruct((SEQ, ATTN_WIDTH), F32)
    sorted_slab = pltpu.VMEM((SEQ, LANES), F32)
    return pl.pallas_call(
        body, name="attn_bwd", grid=(4,), out_shape=(wide, wide, wide),
        scratch_shapes=[sorted_slab] * 5 + [pltpu.VMEM((2 * len(DILATIONS), 2 * ATTN_BLOCK, 2 * ATTN_BLOCK), F32)],
        in_specs=[slab(0), slab(1), slab(2), slab(0), slab(0), slab(0)], out_specs=(slab(0), slab(0), slab(0)),
        compiler_params=_params(("arbitrary",)),
    )(qkv_sorted, qkv_sorted, qkv_sorted, d_out, lse_sorted, delta)


def _hgrn_lower_bound(lb_ref):
    r0, r1 = lb_ref[0:1, :], lb_ref[1:2, :]
    mx = jnp.maximum(r0, r1)
    e0, e1 = jnp.exp(r0 - mx), jnp.exp(r1 - mx)
    return e0 / (e0 + e1)


def _hgrn_gates(hq, hf, lb):
    sq = _sigmoid(hq)
    sg = _sigmoid(hf)
    f = lb + (1.0 - lb) * sg
    return hq * sq, sq, sg, f, 1.0 - f, jnp.log(f)


HGRN_PAIR = 4
HGRN_SEQ_BLOCK = 1024
HGRN_GROUP = 4
HGRN_ROWS = HGRN_GROUP * HGRN_CHUNK


def _hgrn_specs(reverse):
    n_blocks = SEQ // HGRN_SEQ_BLOCK
    width = HGRN_PAIR * HGRN_DIM
    blk = (lambda s: n_blocks - 1 - s) if reverse else (lambda s: s)
    cols = lambda g: pl.BlockSpec((HGRN_SEQ_BLOCK, width),
                                  functools.partial(lambda p, s, g: (blk(s), (HGRN_HEADS // HGRN_PAIR) * g + p), g=g))
    pair = pl.BlockSpec((HGRN_SEQ_BLOCK, width), lambda p, s: (blk(s), p))
    lb = pl.BlockSpec((2, width), lambda p, s: (0, p))
    states = pl.BlockSpec((HGRN_PAIR, HGRN_SEQ_BLOCK // HGRN_CHUNK, HGRN_DIM, HGRN_DIM),
                          lambda p, s: (p, blk(s), 0, 0))
    return cols, pair, lb, states


def _chunk_masks():
    ri = lax.broadcasted_iota(jnp.int32, (HGRN_ROWS, HGRN_ROWS), 0)
    ci = lax.broadcasted_iota(jnp.int32, (HGRN_ROWS, HGRN_ROWS), 1)
    same = (ri // HGRN_CHUNK) == (ci // HGRN_CHUNK)
    return same, same & (ri >= ci), same & (ri <= ci)


def _mm_select(sel, v):
    hi = v.astype(BF16)
    r1 = v - hi.astype(F32)
    mid = r1.astype(BF16)
    lo = (r1 - mid.astype(F32)).astype(BF16)
    return _mm(sel, hi) + _mm(sel, mid) + _mm(sel, lo)


def _head_cols(a, h):
    return a[:, HGRN_DIM * h:HGRN_DIM * (h + 1)]


def _hgrn_fwd(proj, lb_raw):
    t, rws = HGRN_CHUNK, HGRN_ROWS

    def body(hq_ref, hf_ref, hi_ref, lb_ref, rec_ref, st_ref, state):
        @pl.when(pl.program_id(1) == 0)
        def _():
            state[...] = jnp.zeros_like(state)

        lb = _hgrn_lower_bound(lb_ref)
        same, causal, _ = _chunk_masks()
        sel = jnp.concatenate([causal, same], axis=0).astype(BF16)

        def group(g, sts):
            rows = pl.ds(pl.multiple_of(g * rws, rws), rws)
            q, _, _, _, k, lf = _hgrn_gates(hq_ref[rows, :], hf_ref[rows, :], lb)
            sums = _mm_select(sel, lf)
            cum, last = sums[:rws], sums[rws:]
            qd = (q * jnp.exp(cum)).astype(BF16)
            ki = (k * jnp.exp(-cum)).astype(BF16)
            ke = (k * jnp.exp(last - cum)).astype(BF16)
            vb = hi_ref[rows, :].astype(BF16)
            dec = jnp.exp(last)
            new_sts, recs = [], []
            for h in range(HGRN_PAIR):
                qd_h, ke_h, vb_h = _head_cols(qd, h), _head_cols(ke, h), _head_cols(vb, h)
                att = jnp.where(causal, _mm_nt(qd_h, _head_cols(ki, h)), 0.0).astype(BF16)
                intra = _mm(att, vb_h)
                st = sts[h]
                outs = []
                for c in range(HGRN_GROUP):
                    sl = slice(c * t, (c + 1) * t)
                    st_ref[h, g * HGRN_GROUP + c] = st
                    outs.append(intra[sl] + _mm_nt(qd_h[sl], st.astype(BF16)))
                    st = st * _head_cols(dec[c * t:c * t + 1, :], h) + _mm_tn(vb_h[sl], ke_h[sl])
                new_sts.append(st)
                recs.append(jnp.concatenate(outs, axis=0))
            rec_ref[rows, :] = jnp.concatenate(recs, axis=1)
            return tuple(new_sts)

        sts = lax.fori_loop(0, HGRN_SEQ_BLOCK // rws, group, tuple(state[h] for h in range(HGRN_PAIR)))
        for h in range(HGRN_PAIR):
            state[h] = sts[h]

    cols, pair, lb, states = _hgrn_specs(reverse=False)
    return pl.pallas_call(
        body, name="hgrn_fwd", grid=(HGRN_HEADS // HGRN_PAIR, SEQ // HGRN_SEQ_BLOCK),
        out_shape=(jax.ShapeDtypeStruct((SEQ, HGRN_WIDTH), F32),
                   jax.ShapeDtypeStruct((HGRN_HEADS, N_CHUNKS, HGRN_DIM, HGRN_DIM), F32)),
        in_specs=[cols(4), cols(5), cols(6), lb], out_specs=(pair, states),
        scratch_shapes=[pltpu.VMEM((HGRN_PAIR, HGRN_DIM, HGRN_DIM), F32)],
        compiler_params=_params(("parallel", "arbitrary")),
    )(proj, proj, proj, lb_raw)


def _hgrn_bwd(proj, lb_raw, d_rec, states):
    t, rws = HGRN_CHUNK, HGRN_ROWS

    def body(hq_ref, hf_ref, hi_ref, lb_ref, do_ref, st_ref, dhq_ref, dhf_ref, dhi_ref, dlb_ref,
             dstate, dlb_acc):
        lb = _hgrn_lower_bound(lb_ref)
        same, causal, anti = _chunk_masks()
        sel = jnp.concatenate([causal, same], axis=0).astype(BF16)
        sel_t = jnp.concatenate([anti, same], axis=1).astype(BF16)
        @pl.when(pl.program_id(1) == 0)
        def _():
            dstate[...] = jnp.zeros_like(dstate)
            dlb_acc[...] = jnp.zeros_like(dlb_acc)

        n_groups = HGRN_SEQ_BLOCK // rws
        chunks = [slice(c * t, (c + 1) * t) for c in range(HGRN_GROUP)]

        def group(i, dsts_in):
            g = n_groups - 1 - i
            rows = pl.ds(pl.multiple_of(g * rws, rws), rws)
            hq = hq_ref[rows, :]
            q, sq, sg, f, k, lf = _hgrn_gates(hq, hf_ref[rows, :], lb)
            sums = _mm_select(sel, lf)
            cum, last = sums[:rws], sums[rws:]
            e_cum, e_inv, e_end, dec = jnp.exp(cum), jnp.exp(-cum), jnp.exp(last - cum), jnp.exp(last)
            qd, ki, ke = q * e_cum, k * e_inv, k * e_end
            qdb, kib, keb = qd.astype(BF16), ki.astype(BF16), ke.astype(BF16)
            vb = hi_ref[rows, :].astype(BF16)
            gb = do_ref[rows, :].astype(BF16)

            dsts_out, per_head = [], []
            for h in range(HGRN_PAIR):
                qdb_h, kib_h, keb_h = _head_cols(qdb, h), _head_cols(kib, h), _head_cols(keb, h)
                vb_h, gb_h = _head_cols(vb, h), _head_cols(gb, h)
                att = jnp.where(causal, _mm_nt(qdb_h, kib_h), 0.0).astype(BF16)
                datt = jnp.where(causal, _mm_nt(gb_h, vb_h), 0.0).astype(BF16)
                dv = _mm_tn(att, gb_h)
                dqd = _mm(datt, kib_h)
                dki = _mm_tn(datt, qdb_h)

                decs = [_head_cols(dec[c * t:c * t + 1, :], h) for c in range(HGRN_GROUP)]
                dsts = [None] * HGRN_GROUP
                dst = dsts_in[h]
                for c in reversed(range(HGRN_GROUP)):
                    dsts[c] = dst
                    dst = dst * decs[c] + _mm_tn(gb_h[chunks[c]], qdb_h[chunks[c]])
                dsts_out.append(dst)

                dv_x, dqd_x, dke, dlast_x = [], [], [], []
                for c, sl in enumerate(chunks):
                    st_prev = st_ref[h, g * HGRN_GROUP + c]
                    dstb = dsts[c].astype(BF16)
                    dv_x.append(_mm_nt(keb_h[sl], dstb))
                    dqd_x.append(_mm(gb_h[sl], st_prev.astype(BF16)))
                    dke.append(_mm(vb_h[sl], dstb))
                    ddec = jnp.sum(dsts[c] * st_prev, axis=0, keepdims=True)
                    dlast_x.append(jnp.broadcast_to(ddec * decs[c], (t, HGRN_DIM)))
                per_head.append((dv + jnp.concatenate(dv_x, axis=0), dqd + jnp.concatenate(dqd_x, axis=0),
                                 dki, jnp.concatenate(dke, axis=0), jnp.concatenate(dlast_x, axis=0)))
            dv, dqd, dki, dke, dlast = (jnp.concatenate(list(parts), axis=1) for parts in zip(*per_head))

            dq = dqd * e_cum
            dk = dki * e_inv + dke * e_end
            dke_ke = dke * ke
            dcum = dqd * qd - dki * ki - dke_ke
            dlf = _mm_select(sel_t, jnp.concatenate([dcum, dke_ke], axis=0)) + dlast
            df = dlf / f - dk
            dhq_ref[rows, :] = dq * (sq * (1.0 + hq * (1.0 - sq)))
            dhf_ref[rows, :] = df * (1.0 - lb) * (sg * (1.0 - sg))
            dhi_ref[rows, :] = dv
            dlb_acc[...] += jnp.sum(df * (1.0 - sg), axis=0, keepdims=True)
            return tuple(dsts_out)

        dsts = lax.fori_loop(0, n_groups, group, tuple(dstate[h] for h in range(HGRN_PAIR)))
        for h in range(HGRN_PAIR):
            dstate[h] = dsts[h]
        g0 = dlb_acc[...] * lb * (1.0 - lb)
        dlb_ref[...] = jnp.concatenate([g0, -g0], axis=0)

    cols, pair, lb_spec, st_spec = _hgrn_specs(reverse=True)
    wide = jax.ShapeDtypeStruct((SEQ, HGRN_WIDTH), F32)
    return pl.pallas_call(
        body, name="hgrn_bwd", grid=(HGRN_HEADS // HGRN_PAIR, SEQ // HGRN_SEQ_BLOCK),
        out_shape=(wide, wide, wide, jax.ShapeDtypeStruct((2, HGRN_WIDTH), F32)),
        in_specs=[cols(4), cols(5), cols(6), lb_spec, pair, st_spec],
        out_specs=(pair, pair, pair, lb_spec),
        scratch_shapes=[pltpu.VMEM((HGRN_PAIR, HGRN_DIM, HGRN_DIM), F32),
                        pltpu.VMEM((1, HGRN_PAIR * HGRN_DIM), F32)],
        compiler_params=_params(("parallel", "arbitrary")),
    )(proj, proj, proj, lb_raw, d_rec, states)


def _group_sum(v, group):
    parts = []
    for s in range(v.shape[1] // LANES):
        slab = v[:, LANES * s:LANES * (s + 1)]
        if group == LANES:
            parts.append(jnp.broadcast_to(jnp.sum(slab, axis=-1, keepdims=True), slab.shape))
        else:
            h0 = lax.broadcasted_iota(jnp.int32, slab.shape, 1) < HEAD_DIM
            s0 = jnp.sum(jnp.where(h0, slab, 0.0), axis=-1, keepdims=True)
            s1 = jnp.sum(jnp.where(h0, 0.0, slab), axis=-1, keepdims=True)
            parts.append(jnp.where(h0, s0, s1))
    return jnp.concatenate(parts, axis=1)


def _mid(attn_o, rec, proj, x, target, w_out_g, attn_w, hgrn_w, final_w):
    tm = 256

    def branch_fwd(o, gate, w, group):
        r = lax.rsqrt(_group_sum(o * o, group) * (1.0 / group) + NORM_EPS)
        nrm = o * r
        sg = _sigmoid(gate)
        return r, nrm, sg, nrm * w * (gate * sg)

    def branch_bwd(dy, r, nrm, sg, gate, w, group):
        silu = gate * sg
        d_gate = dy * nrm * w * (sg * (1.0 + gate * (1.0 - sg)))
        d_w = jnp.sum(dy * nrm * silu, axis=0, keepdims=True)
        dn = dy * w * silu
        d_o = r * (dn - nrm * (_group_sum(dn * nrm, group) * (1.0 / group)))
        return d_o, d_gate, d_w

    def body(o_ref, rec_ref, ag_ref, hg_ref, x_ref, tgt_ref, wout_ref, aw_ref, hw_ref, fw_ref,
             dx2_ref, do_ref, delta_ref, dag_ref, drec_ref, dhg_ref, dwout_ref, dfw_ref, daw_ref, dhw_ref,
             loss_ref, dwout_acc):
        i = pl.program_id(0)

        @pl.when(i == 0)
        def _():
            dwout_acc[...] = jnp.zeros_like(dwout_acc)
            dfw_ref[...] = jnp.zeros_like(dfw_ref)
            daw_ref[...] = jnp.zeros_like(daw_ref)
            dhw_ref[...] = jnp.zeros_like(dhw_ref)
            loss_ref[...] = jnp.zeros_like(loss_ref)

        o, rc, ag, hg = o_ref[...], rec_ref[...], ag_ref[...], hg_ref[...]
        aw, hw, fw = aw_ref[...], hw_ref[...], fw_ref[...]
        ra, na, sga, ya = branch_fwd(o, ag, aw, HEAD_DIM)
        rh, nh, sgh, yh = branch_fwd(rc, hg, hw, HGRN_DIM)
        mixed = jnp.concatenate([ya, yh], axis=1).astype(BF16)
        wout = wout_ref[...]
        x2 = x_ref[...] + _mm(mixed, wout)
        rstd = lax.rsqrt(jnp.mean(x2 * x2, axis=-1, keepdims=True) + NORM_EPS)
        xn = x2 * rstd
        err = xn * fw - tgt_ref[...]
        row_loss = jnp.mean(err * err, axis=-1, keepdims=True)
        loss_ref[...] += 0.5 * jnp.sum(row_loss, axis=0, keepdims=True)
        dy = err * (1.0 / D_MODEL)
        dfw_ref[...] += jnp.sum(dy * xn, axis=0, keepdims=True)
        dxn = dy * fw
        dx2 = rstd * (dxn - xn * jnp.mean(dxn * xn, axis=-1, keepdims=True))
        dx2_ref[...] = dx2
        dx2b = dx2.astype(BF16)
        dwout_acc[...] += _mm_tn(mixed, dx2b)

        @pl.when(i == pl.num_programs(0) - 1)
        def _():
            dwout_ref[...] = dwout_acc[...].astype(BF16)

        dmixed = _mm_nt(dx2b, wout)

        d_o, d_ag, d_aw = branch_bwd(dmixed[:, :ATTN_WIDTH], ra, na, sga, ag, aw, HEAD_DIM)
        d_rec, d_hg, d_hw = branch_bwd(dmixed[:, ATTN_WIDTH:], rh, nh, sgh, hg, hw, HGRN_DIM)
        do_ref[...] = d_o
        delta_ref[...] = _group_sum(d_o * o, HEAD_DIM)
        dag_ref[...] = d_ag
        drec_ref[...] = d_rec
        dhg_ref[...] = d_hg
        daw_ref[...] += d_aw
        dhw_ref[...] += d_hw

    half = lambda: pl.BlockSpec((tm, COL_BLOCK), lambda i: (i, 0))
    full = lambda: pl.BlockSpec((tm, D_MODEL), lambda i: (i, 0))
    fixed = lambda r, c: pl.BlockSpec((r, c), lambda i: (0, 0))
    wide = jax.ShapeDtypeStruct((SEQ, COL_BLOCK), F32)
    return pl.pallas_call(
        body, name="mid", grid=(SEQ // tm,),
        out_shape=(jax.ShapeDtypeStruct((SEQ, D_MODEL), F32), wide, wide, wide, wide, wide,
                   jax.ShapeDtypeStruct((D_MODEL, D_MODEL), BF16),
                   jax.ShapeDtypeStruct((1, D_MODEL), F32), jax.ShapeDtypeStruct((1, COL_BLOCK), F32),
                   jax.ShapeDtypeStruct((1, COL_BLOCK), F32), jax.ShapeDtypeStruct((1, 1), F32)),
        scratch_shapes=[pltpu.VMEM((D_MODEL, D_MODEL), F32)],
        in_specs=[half(), half(),
                  pl.BlockSpec((tm, COL_BLOCK), lambda i: (i, 3)), pl.BlockSpec((tm, COL_BLOCK), lambda i: (i, 7)),
                  full(), full(), fixed(D_MODEL, D_MODEL), fixed(1, COL_BLOCK), fixed(1, COL_BLOCK),
                  fixed(1, D_MODEL)],
        out_specs=(full(), half(), half(), half(), half(), half(), fixed(D_MODEL, D_MODEL),
                   fixed(1, D_MODEL), fixed(1, COL_BLOCK), fixed(1, COL_BLOCK), fixed(1, 1)),
        compiler_params=_params(("arbitrary",)),
    )(attn_o, rec, proj, proj, x, target, w_out_g, attn_w, hgrn_w, final_w)


def _in_proj_bwd_rows(d_groups, w_g, x, dx2, mix_w, rc, rsa, rsb):
    tm = 256

    def body(*refs):
        dg_refs = refs[:N_DEV]
        wg_ref, x_ref, dx2_ref, w_ref, c_ref, sa_ref, sb_ref, gx_ref, dpb_ref, dmw_ref = refs[N_DEV:]

        @pl.when(pl.program_id(0) == 0)
        def _():
            dmw_ref[...] = jnp.zeros_like(dmw_ref)

        parts = []
        for j in range(N_DEV):
            dp = dg_refs[j][...]
            if j < 2:
                dp = _rot_transposed(dp, c_ref[...], sa_ref[...], sb_ref[...])
            parts.append(dp.astype(BF16))
        dpb = jnp.concatenate(parts, axis=1)
        dpb_ref[...] = dpb
        g = _mm_nt(dpb, wg_ref[...])
        xf = x_ref[...]
        rstd = lax.rsqrt(jnp.mean(xf * xf, axis=-1, keepdims=True) + NORM_EPS)
        xn = xf * rstd
        dmw_ref[...] += jnp.sum(g * xn, axis=0, keepdims=True)
        gw = g * w_ref[...]
        gx_ref[...] = dx2_ref[...] + rstd * (gw - xn * jnp.mean(gw * xn, axis=-1, keepdims=True))

    tile = lambda cols: pl.BlockSpec((tm, cols), lambda i: (i, 0))
    fixed = lambda r, c: pl.BlockSpec((r, c), lambda i: (0, 0))
    return pl.pallas_call(
        body, name="in_proj_bwd_rows", grid=(SEQ // tm,),
        out_shape=(jax.ShapeDtypeStruct((SEQ, D_MODEL), F32), jax.ShapeDtypeStruct((SEQ, IN_COLS), BF16),
                   jax.ShapeDtypeStruct((1, D_MODEL), F32)),
        in_specs=[tile(COL_BLOCK) for _ in range(N_DEV)] + [
            pl.BlockSpec((D_MODEL, IN_COLS), lambda i: (0, 0), pipeline_mode=pl.Buffered(1)),
            tile(D_MODEL), tile(D_MODEL), fixed(1, D_MODEL), tile(LANES), tile(LANES), tile(LANES)],
        out_specs=(tile(D_MODEL), tile(IN_COLS), fixed(1, D_MODEL)),
        compiler_params=_params(("arbitrary",)),
    )(*d_groups, w_g, x, dx2, mix_w, rc, rsa, rsb)


def _weights_exchange(hn_t, dproj_b, dwout_p, small_p):
    n_chips = N_DEV // 2
    rb = 128
    S1_IN, S1_OUT, SMALL, S2_IN, S2_OUT = 0, 4, 8, 15, 18
    rel_of_pair = (1, 2, 3, 0)

    def body(order_ref, hnt_ref, dp_ref, dwout_ref, small_ref, gin_ref, gout_ref, gs_ref,
             part, s1_send, s1_in, s1_out, fwd_in, fwd_out, s2_in, s2_out, land_s, send_sems, recv_sems):
        t = pl.program_id(0)
        me = _my_place()
        x, y, c = me
        my_chip = 2 * x + y
        sibling = (x, y, 1 - c)

        def remote(slot, src, dst, to):
            return pltpu.make_async_remote_copy(src_ref=src, dst_ref=dst, send_sem=send_sems.at[slot],
                                                recv_sem=recv_sems.at[slot], device_id=to, device_id_type=MESH)

        def s1_in_copy(pair):
            return remote(S1_IN + pair, s1_send.at[pair], s1_in.at[pair], sibling)

        def s1_out_copy(pair):
            q = my_chip ^ rel_of_pair[pair]
            return remote(S1_OUT + pair, dwout_ref.at[q, 1 - c], s1_out.at[pair], sibling)

        def s2_copies(rel):
            peer = _peer(me, 2 * rel)
            return [remote(S2_IN + rel - 1, fwd_in.at[rel - 1], s2_in.at[rel - 1], peer),
                    remote(S2_OUT + rel - 1, fwd_out.at[rel - 1], s2_out.at[rel - 1], peer)]

        def small_copy(rel):
            return remote(SMALL + rel - 1, small_ref, land_s.at[rel], _peer(me, rel))

        @pl.when(t == 0)
        def _():
            land_s[0] = small_ref[...]
            for pair in range(n_chips):
                s1_out_copy(pair).start()
            for rel in range(1, N_DEV):
                small_copy(rel).start()

        part[...] = _mm(hnt_ref[...], dp_ref[...])

        def rows_loop(n_rows, fn):
            def step(b, carry):
                fn(pl.ds(pl.multiple_of(b * rb, rb), rb))
                return carry
            lax.fori_loop(0, n_rows // rb, step, 0)

        for pair, rel in enumerate(rel_of_pair):
            @pl.when(t == 2 * pair)
            def _(pair=pair):
                s1_send[pair] = part[...].astype(BF16)
                s1_in_copy(pair).start()

            @pl.when(t == 2 * pair + 1)
            def _(pair=pair, rel=rel):
                q = my_chip ^ rel
                s1_in_copy(pair).wait_recv()
                s1_out_copy(pair).wait_recv()
                dst_in = fwd_in.at[rel - 1] if rel else gin_ref
                dst_out = fwd_out.at[rel - 1] if rel else gout_ref

                def add_in(rows):
                    dst_in[rows, :] = (part[rows, :] + s1_in[pair, rows, :].astype(F32)).astype(dst_in.dtype)

                def add_out(rows):
                    dst_out[rows, :] = (dwout_ref[q, c, rows, :].astype(F32)
                                        + s1_out[pair, rows, :].astype(F32)).astype(dst_out.dtype)

                rows_loop(D_MODEL, add_in)
                rows_loop(WOUT_ROWS, add_out)
                if rel:
                    for cp in s2_copies(rel):
                        cp.start()

        @pl.when(t == N_DEV - 1)
        def _():
            for rel in range(1, n_chips):
                for cp in s2_copies(rel):
                    cp.wait_recv()

            def total_in(rows):
                g = gin_ref[rows, :]
                for rel in range(1, n_chips):
                    g = g + s2_in[rel - 1, rows, :].astype(F32)
                gin_ref[rows, :] = g

            def total_out(rows):
                g = gout_ref[rows, :]
                for rel in range(1, n_chips):
                    g = g + s2_out[rel - 1, rows, :].astype(F32)
                gout_ref[rows, :] = g

            rows_loop(D_MODEL, total_in)
            rows_loop(WOUT_ROWS, total_out)

            for rel in range(1, N_DEV):
                small_copy(rel).wait_recv()
            my_flat = _flat(me)
            g = land_s[my_flat ^ 0]
            for dev in range(1, N_DEV):
                g = g + land_s[my_flat ^ dev]
            gs_ref[...] = g

            for pair in range(n_chips):
                s1_in_copy(pair).wait_send()
                s1_out_copy(pair).wait_send()
            for rel in range(1, n_chips):
                for cp in s2_copies(rel):
                    cp.wait_send()
            for rel in range(1, N_DEV):
                small_copy(rel).wait_send()

    place_x, place_y, place_c = _my_place()
    my_chip = 2 * place_x + place_y
    order = jnp.stack([2 * (my_chip ^ rel) + core for rel in rel_of_pair
                       for core in (1 - place_c, place_c)]).astype(jnp.int32)

    whole = lambda: pl.BlockSpec(memory_space=pltpu.VMEM)
    in_blocks = lambda n: pltpu.VMEM((n, D_MODEL, COL_BLOCK), BF16)
    out_blocks = lambda n: pltpu.VMEM((n, WOUT_ROWS, D_MODEL), BF16)
    grid_spec = pltpu.PrefetchScalarGridSpec(
        num_scalar_prefetch=1, grid=(N_DEV,),
        in_specs=[pl.BlockSpec((D_MODEL, SEQ), lambda t, order: (0, 0), pipeline_mode=pl.Buffered(1)),
                  pl.BlockSpec((SEQ, COL_BLOCK), lambda t, order: (0, order[t])), whole(), whole()],
        out_specs=(whole(), whole(), whole()),
        scratch_shapes=[pltpu.VMEM((D_MODEL, COL_BLOCK), F32), in_blocks(n_chips), in_blocks(n_chips),
                        out_blocks(n_chips), in_blocks(n_chips - 1), out_blocks(n_chips - 1),
                        in_blocks(n_chips - 1), out_blocks(n_chips - 1),
                        pltpu.VMEM((N_DEV, SMALL_ROWS, LANES), F32),
                        pltpu.SemaphoreType.DMA((21,)), pltpu.SemaphoreType.DMA((21,))])
    return pl.pallas_call(
        body, name="weights_exchange", grid_spec=grid_spec,
        out_shape=(jax.ShapeDtypeStruct((D_MODEL, COL_BLOCK), F32), jax.ShapeDtypeStruct((WOUT_ROWS, D_MODEL), F32),
                   jax.ShapeDtypeStruct((SMALL_ROWS, LANES), F32)),
        compiler_params=_params(("arbitrary",)),
    )(order, hn_t, dproj_b, dwout_p.reshape(n_chips, 2, WOUT_ROWS, D_MODEL), small_p)


def _adamw(w, g, m, v):
    m = ADAM_B1 * m + (1.0 - ADAM_B1) * g
    v = ADAM_B2 * v + (1.0 - ADAM_B2) * (g * g)
    m_hat = m / (1.0 - ADAM_B1 ** ADAM_STEP)
    v_hat = v / (1.0 - ADAM_B2 ** ADAM_STEP)
    delta = -ADAM_LR * (m_hat / (jnp.sqrt(v_hat) + ADAM_EPS) + ADAM_WD * w)
    return delta, m, v


def _adamw_update(grads, weights, m_old, v_old):
    rb = 256

    def body(*refs):
        g_refs, w_refs, m_refs, v_refs = refs[0:3], refs[3:6], refs[6:9], refs[9:12]
        d_refs, nm_refs, nv_refs = refs[12:15], refs[15:18], refs[18:21]
        for k in range(3):
            n_rows = g_refs[k].shape[0]
            step_rows = min(rb, n_rows)

            def step(b, carry, k=k, step_rows=step_rows):
                rows = pl.ds(pl.multiple_of(b * step_rows, 8), step_rows)
                delta, nm, nv = _adamw(w_refs[k][rows, :], g_refs[k][rows, :], m_refs[k][rows, :], v_refs[k][rows, :])
                d_refs[k][rows, :] = delta
                nm_refs[k][rows, :] = nm
                nv_refs[k][rows, :] = nv
                return carry

            lax.fori_loop(0, n_rows // step_rows, step, 0)

    shapes = tuple(jax.ShapeDtypeStruct(g.shape, F32) for g in grads)
    vm = lambda: pl.BlockSpec(memory_space=pltpu.VMEM)
    outs = pl.pallas_call(
        body, name="adamw_update", out_shape=shapes * 3,
        in_specs=[vm() for _ in range(12)], out_specs=tuple(vm() for _ in range(9)),
        compiler_params=_params(),
    )(*grads, *weights, *m_old, *v_old)
    return outs[0:3], outs[3:6], outs[6:9]


def _pack_small(mix, attn, hgrn, lb, final, loss=None):
    def rows8(a):
        a = a.reshape(-1, LANES)
        return jnp.pad(a, ((0, 8 - a.shape[0]), (0, 0)))
    last = jnp.zeros((8, LANES), F32) if loss is None else jnp.pad(loss.reshape(1, 1), ((0, 7), (0, LANES - 1)))
    return jnp.concatenate([rows8(mix), rows8(attn), rows8(hgrn), rows8(lb), rows8(final), last], axis=0)


def _unpack_small(slab):
    return (slab[ROW_MIX:ROW_MIX + 8].reshape(1, D_MODEL), slab[ROW_ATTN:ROW_ATTN + 4].reshape(1, ATTN_WIDTH),
            slab[ROW_HGRN:ROW_HGRN + 4].reshape(1, HGRN_WIDTH), slab[ROW_LB:ROW_LB + 8].reshape(2, HGRN_WIDTH),
            slab[ROW_FINAL:ROW_FINAL + 8].reshape(D_MODEL))


def _rope(pos_col):
    lane_e = np.arange(LANES) % HEAD_DIM
    inv = ROPE_THETA ** (-(lane_e % ROPE_HALF) * (2.0 / ROPE_DIMS))
    inv_lanes = np.where(lane_e < ROPE_DIMS, inv, 0.0).astype(np.float32).reshape(1, LANES)
    return _rope_tables(pos_col, jnp.asarray(inv_lanes))


def _local_step(x, proj, qkv_sorted, w_in_g, w_out_g, tables, mix_w, attn_w, hgrn_w, lb_raw, final_w, target):
    rc, rsa, rsb = tables
    attn_o, lse = _attn_fwd_fused(qkv_sorted)
    rec, states = _hgrn_fwd(proj, lb_raw)

    (dx2, d_o, delta, d_ag, d_rec, d_hg, dwout_p, d_final, d_attn_w, d_hgrn_w, loss) = _mid(
        attn_o, rec, proj, x, target, w_out_g, attn_w, hgrn_w, final_w.reshape(1, D_MODEL))

    dqkv = _attn_bwd_fused(qkv_sorted, d_o, lse, delta)
    d_hq, d_hf, d_hi, d_lb = _hgrn_bwd(proj, lb_raw, d_rec, states)

    grad_x, dproj_b, d_mix = _in_proj_bwd_rows(
        (dqkv[0], dqkv[1], dqkv[2], d_ag, d_hq, d_hf, d_hi, d_hg), w_in_g, x, dx2, mix_w, rc, rsa, rsb)
    small_p = _pack_small(d_mix, d_attn_w, d_hgrn_w, d_lb, d_final, loss)
    return grad_x, dproj_b, dwout_p, small_p


def kernel(x, positions, w_in, w_out, mix_norm_w, attn_out_norm_w, hgrn_out_norm_w, hgrn_lb_raw, final_norm_w, loss_target, m_w_in, m_w_out, m_mix_norm_w, m_attn_out_norm_w, m_hgrn_out_norm_w, m_hgrn_lb_raw, m_final_norm_w, v_w_in, v_w_out, v_mix_norm_w, v_attn_out_norm_w, v_hgrn_out_norm_w, v_hgrn_lb_raw, v_final_norm_w):
    tables = _rope(positions.reshape(SEQ, 1))
    proj, hn_t, w_in_g, w_out_g, qkv_sorted = _gather_project(x[0], mix_norm_w, w_in[0], w_out[0], *tables)
    grad_x, dproj_b, dwout_p, small_p = _local_step(
        x[0], proj, qkv_sorted, w_in_g, w_out_g, tables, mix_norm_w, attn_out_norm_w, hgrn_out_norm_w,
        hgrn_lb_raw, final_norm_w, loss_target[0])
    g_in, g_out, g_s = _weights_exchange(hn_t, dproj_b, dwout_p, small_p)

    w_s = _pack_small(mix_norm_w, attn_out_norm_w, hgrn_out_norm_w, hgrn_lb_raw, final_norm_w)
    m_s = _pack_small(m_mix_norm_w, m_attn_out_norm_w, m_hgrn_out_norm_w, m_hgrn_lb_raw, m_final_norm_w)
    v_s = _pack_small(v_mix_norm_w, v_attn_out_norm_w, v_hgrn_out_norm_w, v_hgrn_lb_raw, v_final_norm_w)
    (d_in, d_out, d_s), (nm_in, nm_out, nm_s), (nv_in, nv_out, nv_s) = _adamw_update(
        (g_in, g_out, g_s), (w_in[0], w_out[0], w_s), (m_w_in[0], m_w_out[0], m_s), (v_w_in[0], v_w_out[0], v_s))

    loss = g_s[ROW_LOSS, 0]
    return (loss, grad_x[None], g_in[None], g_out[None], *_unpack_small(g_s),
            d_in[None], d_out[None], *_unpack_small(d_s),
            nm_in[None], nm_out[None], *_unpack_small(nm_s),
            nv_in[None], nv_out[None], *_unpack_small(nv_s))
```

```python
import functools

import jax
import jax.numpy as jnp
import numpy as np
from jax import lax
from jax.experimental import pallas as pl
from jax.experimental.pallas import tpu as pltpu

F32 = jnp.float32
BF16 = jnp.bfloat16

SEQ = 4096
D_MODEL = 1024
ATTN_WIDTH = 512
HGRN_WIDTH = 512
HEAD_DIM = 64
HGRN_HEADS = 4
HGRN_DIM = 128
HGRN_CHUNK = 64
N_CHUNKS = SEQ // HGRN_CHUNK
IN_COLS = 4096
COL_BLOCK = 512
N_DEV = 8
WOUT_ROWS = D_MODEL // N_DEV
ATTN_BLOCK = 128
DILATIONS = (1, 4, 16)
ROPE_THETA = 500000.0
ROPE_DIMS = 16
ROPE_HALF = 8
NORM_EPS = 1e-6
NEG_BIG = -1e30
LANES = 128

ADAM_LR = 0.001
ADAM_B1 = 0.9
ADAM_B2 = 0.999
ADAM_EPS = 1e-08
ADAM_WD = 0.01
ADAM_STEP = 10

SMALL_ROWS = 48
ROW_MIX, ROW_ATTN, ROW_HGRN, ROW_LB, ROW_FINAL, ROW_LOSS = 0, 8, 16, 24, 32, 40

VMEM_LIMIT = 56 * 1024 * 1024
MESH = pl.DeviceIdType.MESH


def _mm(a, b):
    return lax.dot_general(a, b, (((1,), (0,)), ((), ())), preferred_element_type=F32)


def _mm_nt(a, b):
    return lax.dot_general(a, b, (((1,), (1,)), ((), ())), preferred_element_type=F32)


def _mm_tn(a, b):
    return lax.dot_general(a, b, (((0,), (0,)), ((), ())), preferred_element_type=F32)


def _mm_exact(a, b):
    return lax.dot_general(a, b, (((1,), (0,)), ((), ())), preferred_element_type=F32,
                           precision=lax.Precision.HIGHEST)


def _sigmoid(v):
    return 1.0 / (1.0 + jnp.exp(-v))


def _params(sem=None, **kw):
    return pltpu.CompilerParams(dimension_semantics=sem, vmem_limit_bytes=VMEM_LIMIT, **kw)


def _my_place():
    return lax.axis_index("x"), lax.axis_index("y"), lax.axis_index("c")


def _peer(place, rel):
    x, y, c = place
    return (x ^ ((rel >> 2) & 1), y ^ ((rel >> 1) & 1), c ^ (rel & 1))


def _flat(place):
    x, y, c = place
    return 4 * x + 2 * y + c


def _rope_tables(pos_col, inv_freq_lanes):
    tm = 512

    def body(pos_ref, invf_ref, c_ref, sa_ref, sb_ref):
        ang = pos_ref[...].astype(F32) * invf_ref[...]
        e = lax.broadcasted_iota(jnp.int32, (tm, LANES), 1) & (HEAD_DIM - 1)
        cos, sin = jnp.cos(ang), jnp.sin(ang)
        c_ref[...] = jnp.where(e < ROPE_DIMS, cos, 1.0)
        sa_ref[...] = jnp.where((e >= ROPE_HALF) & (e < ROPE_DIMS), sin, 0.0)
        sb_ref[...] = jnp.where(e < ROPE_HALF, -sin, 0.0)

    tab = jax.ShapeDtypeStruct((SEQ, LANES), F32)
    spec = pl.BlockSpec((tm, LANES), lambda i: (i, 0))
    return pl.pallas_call(
        body, name="rope_tables", grid=(SEQ // tm,), out_shape=(tab, tab, tab),
        in_specs=[pl.BlockSpec((tm, 1), lambda i: (i, 0)), pl.BlockSpec((1, LANES), lambda i: (0, 0))],
        out_specs=(spec, spec, spec), compiler_params=_params(("parallel",)),
    )(pos_col, inv_freq_lanes)


def _per_slab(fn, t):
    return jnp.concatenate([fn(t[:, LANES * s:LANES * (s + 1)]) for s in range(t.shape[1] // LANES)], axis=1)


def _rot(t, c, sa, sb):
    return _per_slab(lambda u: u * c + pltpu.roll(u, ROPE_HALF, 1) * sa + pltpu.roll(u, LANES - ROPE_HALF, 1) * sb, t)


def _rot_transposed(g, c, sa, sb):
    return _per_slab(
        lambda u: u * c + pltpu.roll(u * sa, LANES - ROPE_HALF, 1) + pltpu.roll(u * sb, ROPE_HALF, 1), g)


def _gather_project(x, mix_w, w_in, w_out, rc, rsa, rsb):
    tm = 1024
    n_tiles = SEQ // tm
    arrival_of_step = (None, 0, 1, 2, 4, 5, 3, 6)

    def body(order_ref, x_ref, w_ref, win_ref, wout_ref, c_ref, sa_ref, sb_ref,
             proj_ref, hnt_ref, gin_hbm, gout_hbm, qkv_hbm,
             hn_s, w_land, wout_land, stage, sort_stage, slab_tmp, send_sems, recv_sems, local_sems):
        g, i = pl.program_id(0), pl.program_id(1)
        me = _my_place()
        x_, y_, c_ = me
        sibling = (x_, y_, 1 - c_)
        chips = [(1 - x_, y_), (x_, 1 - y_), (1 - x_, 1 - y_)]

        def slab(which, place):
            idx = _flat(place)
            if which == 0:
                return w_land.at[idx]
            return wout_land.at[pl.ds(pl.multiple_of(idx * WOUT_ROWS, WOUT_ROWS), WOUT_ROWS), :]

        def copy(which, k, block, to, src=None):
            ref = slab(which, block)
            return pltpu.make_async_remote_copy(
                src_ref=ref if src is None else src, dst_ref=ref, send_sem=send_sems.at[7 * which + k],
                recv_sem=recv_sems.at[7 * which + k], device_id=to, device_id_type=MESH)

        def first_copies(which):
            src = stage if which == 0 else None
            return ([copy(which, 0, me, sibling, src)]
                    + [copy(which, 1 + j, me, (*chip, c_), src) for j, chip in enumerate(chips)])

        def pass_on(which, j):
            return copy(which, 4 + j, (*chips[j], c_), sibling)

        def arrival(which, k):
            if k == 0:
                return copy(which, 0, sibling, me)
            if k <= 3:
                return copy(which, k, (*chips[k - 1], c_), me)
            return copy(which, k, (*chips[k - 4], 1 - c_), me)

        def to_hbm(step):
            idx = order_ref[step]
            cols = pl.ds(pl.multiple_of(idx * COL_BLOCK, COL_BLOCK), COL_BLOCK)
            return pltpu.make_async_copy(w_land.at[idx], gin_hbm.at[:, cols], local_sems.at[step])

        @pl.when((g == 0) & (i == 0))
        def _():
            stage[...] = win_ref[...].astype(BF16)
            w_land[_flat(me)] = stage[...]
            wout_land[pl.ds(pl.multiple_of(_flat(me) * WOUT_ROWS, WOUT_ROWS), WOUT_ROWS), :] = (
                wout_ref[...].astype(BF16))
            for cp in first_copies(0)[:3] + first_copies(1)[:1]:
                cp.start()
            to_hbm(0).start()

        for step, k in enumerate(arrival_of_step):
            if k is None:
                continue

            @pl.when((g == step) & (i == 0))
            def _(k=k, step=step):
                arrival(0, k).wait_recv()
                to_hbm(step).start()
                if k == 1:
                    for cp in first_copies(0)[3:] + first_copies(1)[1:]:
                        cp.start()
                if 1 <= k <= 3:
                    pass_on(0, k - 1).start()

        rows = pl.ds(pl.multiple_of(i * tm, tm), tm)

        @pl.when(g == 0)
        def _():
            xf = x_ref[...]
            ms = jnp.mean(xf * xf, axis=-1, keepdims=True)
            hn = xf * lax.rsqrt(ms + NORM_EPS) * w_ref[...]
            hnt_ref[...] = hn.T.astype(BF16)
            hn_s[rows, :] = hn.astype(BF16)

        group = order_ref[g]

        def sorted_copy():
            per = tm // SORT_RESIDUES
            for s in range(COL_BLOCK // LANES):
                slab_tmp[s] = proj_ref[:, LANES * s:LANES * (s + 1)]
            for r in range(SORT_RESIDUES):
                for s in range(COL_BLOCK // LANES):
                    sort_stage[r, :, LANES * s:LANES * (s + 1)] = (
                        slab_tmp.at[s][pl.ds(r, per, stride=SORT_RESIDUES), :])
            cols = pl.ds(pl.multiple_of(group * COL_BLOCK, COL_BLOCK), COL_BLOCK)
            cp = pltpu.make_async_copy(
                sort_stage, qkv_hbm.at[:, pl.ds(pl.multiple_of(i * per, per), per), cols], local_sems.at[N_DEV + 1])
            cp.start()
            cp.wait()

        @pl.when(group < 2)
        def _():
            proj_ref[...] = _rot(_mm(hn_s[rows, :], w_land[group]), c_ref[...], sa_ref[...], sb_ref[...])
            sorted_copy()

        @pl.when(group == 2)
        def _():
            proj_ref[...] = _mm(hn_s[rows, :], w_land[group])
            sorted_copy()

        @pl.when(group > 2)
        def _():
            proj_ref[...] = _mm(hn_s[rows, :], w_land[group])

        @pl.when((g == N_DEV - 1) & (i == n_tiles - 1))
        def _():
            for j in range(3):
                arrival(1, 1 + j).wait_recv()
                pass_on(1, j).start()
            for k in (0, 4, 5, 6):
                arrival(1, k).wait_recv()
            for which in (0, 1):
                for cp in first_copies(which) + [pass_on(which, j) for j in range(3)]:
                    cp.wait_send()
            wout_copy = pltpu.make_async_copy(wout_land, gout_hbm, local_sems.at[N_DEV])
            wout_copy.start()
            for step in range(N_DEV):
                to_hbm(step).wait()
            wout_copy.wait()

    me = _my_place()
    x_, y_, c_ = me
    chips = [(1 - x_, y_), (x_, 1 - y_), (1 - x_, 1 - y_)]
    order = jnp.stack([_flat(p) for p in (
        me, (x_, y_, 1 - c_), (*chips[0], c_), (*chips[1], c_), (*chips[0], 1 - c_), (*chips[1], 1 - c_),
        (*chips[2], c_), (*chips[2], 1 - c_))]).astype(jnp.int32)

    first_sweep = lambda g, i, order: (jnp.where(g == 0, i, n_tiles - 1), 0)
    tab = pl.BlockSpec((tm, LANES), lambda g, i, order: (jnp.where(order[g] < 2, i, 0), 0))
    whole = lambda: pl.BlockSpec(memory_space=pltpu.VMEM)
    grid_spec = pltpu.PrefetchScalarGridSpec(
        num_scalar_prefetch=1, grid=(N_DEV, n_tiles),
        in_specs=[pl.BlockSpec((tm, D_MODEL), first_sweep),
                  pl.BlockSpec((1, D_MODEL), lambda g, i, order: (0, 0)),
                  whole(), whole(), tab, tab, tab],
        out_specs=(pl.BlockSpec((tm, COL_BLOCK), lambda g, i, order: (i, order[g])),
                   pl.BlockSpec((D_MODEL, tm), lambda g, i, order: (0, jnp.where(g == 0, i, n_tiles - 1))),
                   pl.BlockSpec(memory_space=pl.ANY), pl.BlockSpec(memory_space=pl.ANY),
                   pl.BlockSpec(memory_space=pl.ANY)),
        scratch_shapes=[pltpu.VMEM((SEQ, D_MODEL), BF16),
                        pltpu.VMEM((N_DEV, D_MODEL, COL_BLOCK), BF16),
                        pltpu.VMEM((D_MODEL, D_MODEL), BF16),
                        pltpu.VMEM((D_MODEL, COL_BLOCK), BF16),
                        pltpu.VMEM((SORT_RESIDUES, tm // SORT_RESIDUES, COL_BLOCK), F32),
                        pltpu.VMEM((COL_BLOCK // LANES, tm, LANES), F32),
                        pltpu.SemaphoreType.DMA((14,)), pltpu.SemaphoreType.DMA((14,)),
                        pltpu.SemaphoreType.DMA((N_DEV + 2,))])
    proj, hn_t, w_in_g, w_out_g, qkv_sorted = pl.pallas_call(
        body, name="gather_project", grid_spec=grid_spec,
        out_shape=(jax.ShapeDtypeStruct((SEQ, IN_COLS), F32), jax.ShapeDtypeStruct((D_MODEL, SEQ), BF16),
                   jax.ShapeDtypeStruct((D_MODEL, IN_COLS), BF16), jax.ShapeDtypeStruct((D_MODEL, D_MODEL), BF16),
                   jax.ShapeDtypeStruct((SORT_RESIDUES, SORT_ROWS, 3 * COL_BLOCK), F32)),
        compiler_params=_params(("arbitrary", "arbitrary")),
    )(order, x, mix_w, w_in, w_out, rc, rsa, rsb)
    return proj, hn_t, w_in_g, w_out_g, qkv_sorted.reshape(SEQ, 3 * COL_BLOCK)


SCORE_SCALE = HEAD_DIM ** -0.5
ATTN_GROUP_FWD = 16
ATTN_GROUP_BWD = 8
BLOCKS_PER_PATTERN = SEQ // ATTN_BLOCK
SORT_RESIDUES = 16
SORT_ROWS = SEQ // SORT_RESIDUES


def _write_band_bias(bias_ref):
    row = lax.broadcasted_iota(jnp.int32, (2 * ATTN_BLOCK, 2 * ATTN_BLOCK), 0) & (ATTN_BLOCK - 1)
    col = lax.broadcasted_iota(jnp.int32, (2 * ATTN_BLOCK, 2 * ATTN_BLOCK), 1)
    for pi, d in enumerate(DILATIONS):
        per = SORT_RESIDUES // d
        ahead = per * (row % (8 * d) - col % (16 * d)) + (row // (8 * d) - col // (16 * d))
        dist = ATTN_BLOCK + ahead
        bias_ref[2 * pi] = jnp.where((dist >= 0) & (dist <= ATTN_BLOCK), 0.0, NEG_BIG)
        bias_ref[2 * pi + 1] = jnp.where(ahead >= 0, 0.0, NEG_BIG)


def _head0_lanes():
    return lax.broadcasted_iota(jnp.int32, (ATTN_BLOCK, LANES), 1) < HEAD_DIM


def _stack_heads(t, h0):
    return jnp.concatenate([jnp.where(h0, t, 0.0), jnp.where(h0, 0.0, t)], axis=0).astype(BF16)


def _block_runs(i, d):
    nblk = BLOCKS_PER_PATTERN // d
    r, n = i // nblk, i % nblk
    kn = jnp.maximum(n - 1, 0)
    rows, keys = [], []
    for c in range(SORT_RESIDUES // d):
        base = SORT_ROWS * (c * d + r)
        rows.append(pl.ds(pl.multiple_of(base + 8 * d * n, 8), 8 * d))
        keys.append(pl.ds(pl.multiple_of(base + 8 * d * kn, 8), 16 * d))
    return rows, keys, (n == 0).astype(jnp.int32)


def _take(ref, runs):
    return jnp.concatenate([ref[run, :] for run in runs], axis=0)


def _put(ref, runs, value, add=False):
    at = 0
    for run in runs:
        piece = value[at:at + run.size]
        if add:
            ref[run, :] += piece
        else:
            ref[run, :] = piece
        at += run.size


def _sort_rows(src_ref, dst_ref):
    for r in range(SORT_RESIDUES):
        dst_ref[SORT_ROWS * r:SORT_ROWS * (r + 1), :] = src_ref[pl.ds(r, SORT_ROWS, stride=SORT_RESIDUES), :]


def _unsort_rows(src_ref, dst_ref):
    for r in range(SORT_RESIDUES):
        dst_ref[pl.ds(r, SORT_ROWS, stride=SORT_RESIDUES), :] = src_ref[SORT_ROWS * r:SORT_ROWS * (r + 1), :]


def _for_each_group(d, n_group, load, compute, store):
    def group(g, carry):
        items = [load(*_block_runs(g * n_group + u, d)) for u in range(n_group)]
        results = [compute(item) for item in items]
        for item, res in zip(items, results):
            store(item, res)
        return carry

    lax.fori_loop(0, BLOCKS_PER_PATTERN // n_group, group, 0)


def _attn_fwd_fused(qkv_sorted):
    n_pat = len(DILATIONS)
    tile2 = (2 * ATTN_BLOCK, LANES)

    def body(q_ref, k_ref, v_ref, o_ref, lse_ref, o_acc, m_acc, l_acc, bias_ref):
        pl.when(pl.program_id(0) == 0)(lambda: _write_band_bias(bias_ref))
        h0 = _head0_lanes()
        for pi, d in enumerate(DILATIONS):
            first, last = pi == 0, pi == n_pat - 1

            def load(rows, keys, which, first=first, pi=pi):
                item = dict(rows=rows, keys=keys, which=2 * pi + which)
                if not first:
                    item.update(o=_take(o_acc, rows), m=[_take(m_acc.at[h], rows) for h in range(2)],
                                l=[_take(l_acc.at[h], rows) for h in range(2)])
                return item

            def compute(item, first=first):
                kb = _take(k_ref, item["keys"]).astype(BF16)
                vb = _take(v_ref, item["keys"]).astype(BF16)
                s = _mm_nt(_stack_heads(_take(q_ref, item["rows"]) * SCORE_SCALE, h0), kb) + bias_ref[item["which"]]
                mb = jnp.max(s, axis=-1, keepdims=True)
                if first:
                    p = jnp.exp(s - mb)
                    mn = jnp.broadcast_to(mb, tile2)
                else:
                    m_old = jnp.concatenate(item["m"], axis=0)
                    mn = jnp.maximum(m_old, mb)
                    alpha = jnp.exp(m_old - mn)
                    p = jnp.exp(s - jnp.concatenate([mn, mn], axis=1))
                ls = jnp.sum(p, axis=-1, keepdims=True)
                pv = _mm(p.astype(BF16), vb)
                if first:
                    return pv, mn, jnp.broadcast_to(ls, tile2)
                o_old = jnp.concatenate([item["o"], item["o"]], axis=0)
                return alpha * o_old + pv, mn, alpha * jnp.concatenate(item["l"], axis=0) + ls

            def store(item, res, last=last):
                rows = item["rows"]
                (o0, o1), (m0, m1), (l0, l1) = ((a[:ATTN_BLOCK], a[ATTN_BLOCK:]) for a in res)
                if last:
                    _put(o_acc, rows, jnp.where(h0, o0 / l0, o1 / l1))
                    _put(lse_ref, rows, jnp.where(h0, m0 + jnp.log(l0), m1 + jnp.log(l1)))
                else:
                    _put(o_acc, rows, jnp.where(h0, o0, o1))
                    for h, (m, l) in enumerate(((m0, l0), (m1, l1))):
                        _put(m_acc.at[h], rows, m)
                        _put(l_acc.at[h], rows, l)

            _for_each_group(d, ATTN_GROUP_FWD, load, compute, store)
        _unsort_rows(o_acc, o_ref)

    slab = lambda g: pl.BlockSpec((SEQ, LANES), functools.partial(lambda hp, g: (0, 4 * g + hp), g=g))
    wide = jax.ShapeDtypeStruct((SEQ, ATTN_WIDTH), F32)
    return pl.pallas_call(
        body, name="attn_fwd", grid=(4,), out_shape=(wide, wide),
        in_specs=[slab(0), slab(1), slab(2)], out_specs=(slab(0), slab(0)),
        scratch_shapes=[pltpu.VMEM((SEQ, LANES), F32), pltpu.VMEM((2, SEQ, LANES), F32),
                        pltpu.VMEM((2, SEQ, LANES), F32),
                        pltpu.VMEM((2 * len(DILATIONS), 2 * ATTN_BLOCK, 2 * ATTN_BLOCK), F32)],
        compiler_params=_params(("arbitrary",)),
    )(qkv_sorted, qkv_sorted, qkv_sorted)


def _attn_bwd_fused(qkv_sorted, d_out, lse_sorted, delta):
    def body(q_ref, k_ref, v_ref, do_ref, lse_ref, del_ref, dq_ref, dk_ref, dv_ref,
             do_s, del_s, dq_s, dk_s, dv_s, bias_ref):
        pl.when(pl.program_id(0) == 0)(lambda: _write_band_bias(bias_ref))
        _sort_rows(do_ref, do_s)
        _sort_rows(del_ref, del_s)
        dk_s[...] = jnp.zeros_like(dk_s)
        dv_s[...] = jnp.zeros_like(dv_s)
        h0 = _head0_lanes()
        for pi, d in enumerate(DILATIONS):
            first = pi == 0

            def load(rows, keys, which, pi=pi):
                return dict(rows=rows, keys=keys, q=_take(q_ref, rows), g=_take(do_s, rows),
                            lse=_take(lse_ref, rows), delta=_take(del_s, rows),
                            k=_take(k_ref, keys).astype(BF16), v=_take(v_ref, keys).astype(BF16),
                            bias=bias_ref[2 * pi + which])

            def per_head(t):
                swapped = pltpu.roll(t, HEAD_DIM, 1)
                both = jnp.concatenate([jnp.where(h0, t, swapped), jnp.where(h0, swapped, t)], axis=0)
                return jnp.concatenate([both, both], axis=1)

            def compute(item):
                q2, g2 = _stack_heads(item["q"] * SCORE_SCALE, h0), _stack_heads(item["g"], h0)
                s = _mm_nt(q2, item["k"]) + item["bias"]
                p = jnp.exp(s - per_head(item["lse"]))
                dp = _mm_nt(g2, item["v"])
                ds = (p * (dp - per_head(item["delta"]))).astype(BF16)
                dq2 = _mm(ds, item["k"])
                dq = jnp.where(h0, dq2[:ATTN_BLOCK], dq2[ATTN_BLOCK:]) * SCORE_SCALE
                return dq, _mm_tn(ds, q2), _mm_tn(p.astype(BF16), g2)

            def store(item, res, first=first):
                _put(dq_s, item["rows"], res[0], add=not first)
                _put(dk_s, item["keys"], res[1], add=True)
                _put(dv_s, item["keys"], res[2], add=True)

            _for_each_group(d, ATTN_GROUP_BWD, load, compute, store)
        _unsort_rows(dq_s, dq_ref)
        _unsort_rows(dk_s, dk_ref)
        _unsort_rows(dv_s, dv_ref)

    slab = lambda g: pl.BlockSpec((SEQ, LANES), functools.partial(lambda hp, g: (0, 4 * g + hp), g=g))
    wide = jax.ShapeDtypeStruct((SEQ, ATTN_WIDTH), F32)
    sorted_slab = pltpu.VMEM((SEQ, LANES), F32)
    return pl.pallas_call(
        body, name="attn_bwd", grid=(4,), out_shape=(wide, wide, wide),
        scratch_shapes=[sorted_slab] * 5 + [pltpu.VMEM((2 * len(DILATIONS), 2 * ATTN_BLOCK, 2 * ATTN_BLOCK), F32)],
        in_specs=[slab(0), slab(1), slab(2), slab(0), slab(0), slab(0)], out_specs=(slab(0), slab(0), slab(0)),
        compiler_params=_params(("arbitrary",)),
    )(qkv_sorted, qkv_sorted, qkv_sorted, d_out, lse_sorted, delta)


def _hgrn_lower_bound(lb_ref):
    r0, r1 = lb_ref[0:1, :], lb_ref[1:2, :]
    mx = jnp.maximum(r0, r1)
    e0, e1 = jnp.exp(r0 - mx), jnp.exp(r1 - mx)
    return e0 / (e0 + e1)


def _hgrn_gates(hq, hf, lb):
    sq = _sigmoid(hq)
    sg = _sigmoid(hf)
    f = lb + (1.0 - lb) * sg
    return hq * sq, sq, sg, f, 1.0 - f, jnp.log(f)


HGRN_PAIR = 4
HGRN_SEQ_BLOCK = 1024
HGRN_GROUP = 4
HGRN_ROWS = HGRN_GROUP * HGRN_CHUNK


def _hgrn_specs(reverse):
    n_blocks = SEQ // HGRN_SEQ_BLOCK
    width = HGRN_PAIR * HGRN_DIM
    blk = (lambda s: n_blocks - 1 - s) if reverse else (lambda s: s)
    cols = lambda g: pl.BlockSpec((HGRN_SEQ_BLOCK, width),
                                  functools.partial(lambda p, s, g: (blk(s), (HGRN_HEADS // HGRN_PAIR) * g + p), g=g))
    pair = pl.BlockSpec((HGRN_SEQ_BLOCK, width), lambda p, s: (blk(s), p))
    lb = pl.BlockSpec((2, width), lambda p, s: (0, p))
    states = pl.BlockSpec((HGRN_PAIR, HGRN_SEQ_BLOCK // HGRN_CHUNK, HGRN_DIM, HGRN_DIM),
                          lambda p, s: (p, blk(s), 0, 0))
    return cols, pair, lb, states


def _chunk_masks():
    ri = lax.broadcasted_iota(jnp.int32, (HGRN_ROWS, HGRN_ROWS), 0)
    ci = lax.broadcasted_iota(jnp.int32, (HGRN_ROWS, HGRN_ROWS), 1)
    same = (ri // HGRN_CHUNK) == (ci // HGRN_CHUNK)
    return same, same & (ri >= ci), same & (ri <= ci)


def _mm_select(sel, v):
    hi = v.astype(BF16)
    r1 = v - hi.astype(F32)
    mid = r1.astype(BF16)
    lo = (r1 - mid.astype(F32)).astype(BF16)
    return _mm(sel, hi) + _mm(sel, mid) + _mm(sel, lo)


def _head_cols(a, h):
    return a[:, HGRN_DIM * h:HGRN_DIM * (h + 1)]


def _hgrn_fwd(proj, lb_raw):
    t, rws = HGRN_CHUNK, HGRN_ROWS

    def body(hq_ref, hf_ref, hi_ref, lb_ref, rec_ref, st_ref, state):
        @pl.when(pl.program_id(1) == 0)
        def _():
            state[...] = jnp.zeros_like(state)

        lb = _hgrn_lower_bound(lb_ref)
        same, causal, _ = _chunk_masks()
        sel = jnp.concatenate([causal, same], axis=0).astype(BF16)

        def group(g, sts):
            rows = pl.ds(pl.multiple_of(g * rws, rws), rws)
            q, _, _, _, k, lf = _hgrn_gates(hq_ref[rows, :], hf_ref[rows, :], lb)
            sums = _mm_select(sel, lf)
            cum, last = sums[:rws], sums[rws:]
            qd = (q * jnp.exp(cum)).astype(BF16)
            ki = (k * jnp.exp(-cum)).astype(BF16)
            ke = (k * jnp.exp(last - cum)).astype(BF16)
            vb = hi_ref[rows, :].astype(BF16)
            dec = jnp.exp(last)
            new_sts, recs = [], []
            for h in range(HGRN_PAIR):
                qd_h, ke_h, vb_h = _head_cols(qd, h), _head_cols(ke, h), _head_cols(vb, h)
                att = jnp.where(causal, _mm_nt(qd_h, _head_cols(ki, h)), 0.0).astype(BF16)
                intra = _mm(att, vb_h)
                st = sts[h]
                outs = []
                for c in range(HGRN_GROUP):
                    sl = slice(c * t, (c + 1) * t)
                    st_ref[h, g * HGRN_GROUP + c] = st
                    outs.append(intra[sl] + _mm_nt(qd_h[sl], st.astype(BF16)))
                    st = st * _head_cols(dec[c * t:c * t + 1, :], h) + _mm_tn(vb_h[sl], ke_h[sl])
                new_sts.append(st)
                recs.append(jnp.concatenate(outs, axis=0))
            rec_ref[rows, :] = jnp.concatenate(recs, axis=1)
            return tuple(new_sts)

        sts = lax.fori_loop(0, HGRN_SEQ_BLOCK // rws, group, tuple(state[h] for h in range(HGRN_PAIR)))
        for h in range(HGRN_PAIR):
            state[h] = sts[h]

    cols, pair, lb, states = _hgrn_specs(reverse=False)
    return pl.pallas_call(
        body, name="hgrn_fwd", grid=(HGRN_HEADS // HGRN_PAIR, SEQ // HGRN_SEQ_BLOCK),
        out_shape=(jax.ShapeDtypeStruct((SEQ, HGRN_WIDTH), F32),
                   jax.ShapeDtypeStruct((HGRN_HEADS, N_CHUNKS, HGRN_DIM, HGRN_DIM), F32)),
        in_specs=[cols(4), cols(5), cols(6), lb], out_specs=(pair, states),
        scratch_shapes=[pltpu.VMEM((HGRN_PAIR, HGRN_DIM, HGRN_DIM), F32)],
        compiler_params=_params(("parallel", "arbitrary")),
    )(proj, proj, proj, lb_raw)


def _hgrn_bwd(proj, lb_raw, d_rec, states):
    t, rws = HGRN_CHUNK, HGRN_ROWS

    def body(hq_ref, hf_ref, hi_ref, lb_ref, do_ref, st_ref, dhq_ref, dhf_ref, dhi_ref, dlb_ref,
             dstate, dlb_acc):
        lb = _hgrn_lower_bound(lb_ref)
        same, causal, anti = _chunk_masks()
        sel = jnp.concatenate([causal, same], axis=0).astype(BF16)
        sel_t = jnp.concatenate([anti, same], axis=1).astype(BF16)
        @pl.when(pl.program_id(1) == 0)
        def _():
            dstate[...] = jnp.zeros_like(dstate)
            dlb_acc[...] = jnp.zeros_like(dlb_acc)

        n_groups = HGRN_SEQ_BLOCK // rws
        chunks = [slice(c * t, (c + 1) * t) for c in range(HGRN_GROUP)]

        def group(i, dsts_in):
            g = n_groups - 1 - i
            rows = pl.ds(pl.multiple_of(g * rws, rws), rws)
            hq = hq_ref[rows, :]
            q, sq, sg, f, k, lf = _hgrn_gates(hq, hf_ref[rows, :], lb)
            sums = _mm_select(sel, lf)
            cum, last = sums[:rws], sums[rws:]
            e_cum, e_inv, e_end, dec = jnp.exp(cum), jnp.exp(-cum), jnp.exp(last - cum), jnp.exp(last)
            qd, ki, ke = q * e_cum, k * e_inv, k * e_end
            qdb, kib, keb = qd.astype(BF16), ki.astype(BF16), ke.astype(BF16)
            vb = hi_ref[rows, :].astype(BF16)
            gb = do_ref[rows, :].astype(BF16)

            dsts_out, per_head = [], []
            for h in range(HGRN_PAIR):
                qdb_h, kib_h, keb_h = _head_cols(qdb, h), _head_cols(kib, h), _head_cols(keb, h)
                vb_h, gb_h = _head_cols(vb, h), _head_cols(gb, h)
                att = jnp.where(causal, _mm_nt(qdb_h, kib_h), 0.0).astype(BF16)
                datt = jnp.where(causal, _mm_nt(gb_h, vb_h), 0.0).astype(BF16)
                dv = _mm_tn(att, gb_h)
                dqd = _mm(datt, kib_h)
                dki = _mm_tn(datt, qdb_h)

                decs = [_head_cols(dec[c * t:c * t + 1, :], h) for c in range(HGRN_GROUP)]
                dsts = [None] * HGRN_GROUP
                dst = dsts_in[h]
                for c in reversed(range(HGRN_GROUP)):
                    dsts[c] = dst
                    dst = dst * decs[c] + _mm_tn(gb_h[chunks[c]], qdb_h[chunks[c]])
                dsts_out.append(dst)

                dv_x, dqd_x, dke, dlast_x = [], [], [], []
                for c, sl in enumerate(chunks):
                    st_prev = st_ref[h, g * HGRN_GROUP + c]
                    dstb = dsts[c].astype(BF16)
                    dv_x.append(_mm_nt(keb_h[sl], dstb))
                    dqd_x.append(_mm(gb_h[sl], st_prev.astype(BF16)))
                    dke.append(_mm(vb_h[sl], dstb))
                    ddec = jnp.sum(dsts[c] * st_prev, axis=0, keepdims=True)
                    dlast_x.append(jnp.broadcast_to(ddec * decs[c], (t, HGRN_DIM)))
                per_head.append((dv + jnp.concatenate(dv_x, axis=0), dqd + jnp.concatenate(dqd_x, axis=0),
                                 dki, jnp.concatenate(dke, axis=0), jnp.concatenate(dlast_x, axis=0)))
            dv, dqd, dki, dke, dlast = (jnp.concatenate(list(parts), axis=1) for parts in zip(*per_head))

            dq = dqd * e_cum
            dk = dki * e_inv + dke * e_end
            dke_ke = dke * ke
            dcum = dqd * qd - dki * ki - dke_ke
            dlf = _mm_select(sel_t, jnp.concatenate([dcum, dke_ke], axis=0)) + dlast
            df = dlf / f - dk
            dhq_ref[rows, :] = dq * (sq * (1.0 + hq * (1.0 - sq)))
            dhf_ref[rows, :] = df * (1.0 - lb) * (sg * (1.0 - sg))
            dhi_ref[rows, :] = dv
            dlb_acc[...] += jnp.sum(df * (1.0 - sg), axis=0, keepdims=True)
            return tuple(dsts_out)

        dsts = lax.fori_loop(0, n_groups, group, tuple(dstate[h] for h in range(HGRN_PAIR)))
        for h in range(HGRN_PAIR):
            dstate[h] = dsts[h]
        g0 = dlb_acc[...] * lb * (1.0 - lb)
        dlb_ref[...] = jnp.concatenate([g0, -g0], axis=0)

    cols, pair, lb_spec, st_spec = _hgrn_specs(reverse=True)
    wide = jax.ShapeDtypeStruct((SEQ, HGRN_WIDTH), F32)
    return pl.pallas_call(
        body, name="hgrn_bwd", grid=(HGRN_HEADS // HGRN_PAIR, SEQ // HGRN_SEQ_BLOCK),
        out_shape=(wide, wide, wide, jax.ShapeDtypeStruct((2, HGRN_WIDTH), F32)),
        in_specs=[cols(4), cols(5), cols(6), lb_spec, pair, st_spec],
        out_specs=(pair, pair, pair, lb_spec),
        scratch_shapes=[pltpu.VMEM((HGRN_PAIR, HGRN_DIM, HGRN_DIM), F32),
                        pltpu.VMEM((1, HGRN_PAIR * HGRN_DIM), F32)],
        compiler_params=_params(("parallel", "arbitrary")),
    )(proj, proj, proj, lb_raw, d_rec, states)


def _group_sum(v, group):
    parts = []
    for s in range(v.shape[1] // LANES):
        slab = v[:, LANES * s:LANES * (s + 1)]
        if group == LANES:
            parts.append(jnp.broadcast_to(jnp.sum(slab, axis=-1, keepdims=True), slab.shape))
        else:
            h0 = lax.broadcasted_iota(jnp.int32, slab.shape, 1) < HEAD_DIM
            s0 = jnp.sum(jnp.where(h0, slab, 0.0), axis=-1, keepdims=True)
            s1 = jnp.sum(jnp.where(h0, 0.0, slab), axis=-1, keepdims=True)
            parts.append(jnp.where(h0, s0, s1))
    return jnp.concatenate(parts, axis=1)


def _mid(attn_o, rec, proj, x, target, w_out_g, attn_w, hgrn_w, final_w):
    tm = 256

    def branch_fwd(o, gate, w, group):
        r = lax.rsqrt(_group_sum(o * o, group) * (1.0 / group) + NORM_EPS)
        nrm = o * r
        sg = _sigmoid(gate)
        return r, nrm, sg, nrm * w * (gate * sg)

    def branch_bwd(dy, r, nrm, sg, gate, w, group):
        silu = gate * sg
        d_gate = dy * nrm * w * (sg * (1.0 + gate * (1.0 - sg)))
        d_w = jnp.sum(dy * nrm * silu, axis=0, keepdims=True)
        dn = dy * w * silu
        d_o = r * (dn - nrm * (_group_sum(dn * nrm, group) * (1.0 / group)))
        return d_o, d_gate, d_w

    def body(o_ref, rec_ref, ag_ref, hg_ref, x_ref, tgt_ref, wout_ref, aw_ref, hw_ref, fw_ref,
             dx2_ref, do_ref, delta_ref, dag_ref, drec_ref, dhg_ref, dwout_ref, dfw_ref, daw_ref, dhw_ref,
             loss_ref, dwout_acc):
        i = pl.program_id(0)

        @pl.when(i == 0)
        def _():
            dwout_acc[...] = jnp.zeros_like(dwout_acc)
            dfw_ref[...] = jnp.zeros_like(dfw_ref)
            daw_ref[...] = jnp.zeros_like(daw_ref)
            dhw_ref[...] = jnp.zeros_like(dhw_ref)
            loss_ref[...] = jnp.zeros_like(loss_ref)

        o, rc, ag, hg = o_ref[...], rec_ref[...], ag_ref[...], hg_ref[...]
        aw, hw, fw = aw_ref[...], hw_ref[...], fw_ref[...]
        ra, na, sga, ya = branch_fwd(o, ag, aw, HEAD_DIM)
        rh, nh, sgh, yh = branch_fwd(rc, hg, hw, HGRN_DIM)
        mixed = jnp.concatenate([ya, yh], axis=1).astype(BF16)
        wout = wout_ref[...]
        x2 = x_ref[...] + _mm(mixed, wout)
        rstd = lax.rsqrt(jnp.mean(x2 * x2, axis=-1, keepdims=True) + NORM_EPS)
        xn = x2 * rstd
        err = xn * fw - tgt_ref[...]
        row_loss = jnp.mean(err * err, axis=-1, keepdims=True)
        loss_ref[...] += 0.5 * jnp.sum(row_loss, axis=0, keepdims=True)
        dy = err * (1.0 / D_MODEL)
        dfw_ref[...] += jnp.sum(dy * xn, axis=0, keepdims=True)
        dxn = dy * fw
        dx2 = rstd * (dxn - xn * jnp.mean(dxn * xn, axis=-1, keepdims=True))
        dx2_ref[...] = dx2
        dx2b = dx2.astype(BF16)
        dwout_acc[...] += _mm_tn(mixed, dx2b)

        @pl.when(i == pl.num_programs(0) - 1)
        def _():
            dwout_ref[...] = dwout_acc[...].astype(BF16)

        dmixed = _mm_nt(dx2b, wout)

        d_o, d_ag, d_aw = branch_bwd(dmixed[:, :ATTN_WIDTH], ra, na, sga, ag, aw, HEAD_DIM)
        d_rec, d_hg, d_hw = branch_bwd(dmixed[:, ATTN_WIDTH:], rh, nh, sgh, hg, hw, HGRN_DIM)
        do_ref[...] = d_o
        delta_ref[...] = _group_sum(d_o * o, HEAD_DIM)
        dag_ref[...] = d_ag
        drec_ref[...] = d_rec
        dhg_ref[...] = d_hg
        daw_ref[...] += d_aw
        dhw_ref[...] += d_hw

    half = lambda: pl.BlockSpec((tm, COL_BLOCK), lambda i: (i, 0))
    full = lambda: pl.BlockSpec((tm, D_MODEL), lambda i: (i, 0))
    fixed = lambda r, c: pl.BlockSpec((r, c), lambda i: (0, 0))
    wide = jax.ShapeDtypeStruct((SEQ, COL_BLOCK), F32)
    return pl.pallas_call(
        body, name="mid", grid=(SEQ // tm,),
        out_shape=(jax.ShapeDtypeStruct((SEQ, D_MODEL), F32), wide, wide, wide, wide, wide,
                   jax.ShapeDtypeStruct((D_MODEL, D_MODEL), BF16),
                   jax.ShapeDtypeStruct((1, D_MODEL), F32), jax.ShapeDtypeStruct((1, COL_BLOCK), F32),
                   jax.ShapeDtypeStruct((1, COL_BLOCK), F32), jax.ShapeDtypeStruct((1, 1), F32)),
        scratch_shapes=[pltpu.VMEM((D_MODEL, D_MODEL), F32)],
        in_specs=[half(), half(),
                  pl.BlockSpec((tm, COL_BLOCK), lambda i: (i, 3)), pl.BlockSpec((tm, COL_BLOCK), lambda i: (i, 7)),
                  full(), full(), fixed(D_MODEL, D_MODEL), fixed(1, COL_BLOCK), fixed(1, COL_BLOCK),
                  fixed(1, D_MODEL)],
        out_specs=(full(), half(), half(), half(), half(), half(), fixed(D_MODEL, D_MODEL),
                   fixed(1, D_MODEL), fixed(1, COL_BLOCK), fixed(1, COL_BLOCK), fixed(1, 1)),
        compiler_params=_params(("arbitrary",)),
    )(attn_o, rec, proj, proj, x, target, w_out_g, attn_w, hgrn_w, final_w)


def _in_proj_bwd_rows(d_groups, w_g, x, dx2, mix_w, rc, rsa, rsb):
    tm = 256

    def body(*refs):
        dg_refs = refs[:N_DEV]
        wg_ref, x_ref, dx2_ref, w_ref, c_ref, sa_ref, sb_ref, gx_ref, dpb_ref, dmw_ref = refs[N_DEV:]

        @pl.when(pl.program_id(0) == 0)
        def _():
            dmw_ref[...] = jnp.zeros_like(dmw_ref)

        parts = []
        for j in range(N_DEV):
            dp = dg_refs[j][...]
            if j < 2:
                dp = _rot_transposed(dp, c_ref[...], sa_ref[...], sb_ref[...])
            parts.append(dp.astype(BF16))
        dpb = jnp.concatenate(parts, axis=1)
        dpb_ref[...] = dpb
        g = _mm_nt(dpb, wg_ref[...])
        xf = x_ref[...]
        rstd = lax.rsqrt(jnp.mean(xf * xf, axis=-1, keepdims=True) + NORM_EPS)
        xn = xf * rstd
        dmw_ref[...] += jnp.sum(g * xn, axis=0, keepdims=True)
        gw = g * w_ref[...]
        gx_ref[...] = dx2_ref[...] + rstd * (gw - xn * jnp.mean(gw * xn, axis=-1, keepdims=True))

    tile = lambda cols: pl.BlockSpec((tm, cols), lambda i: (i, 0))
    fixed = lambda r, c: pl.BlockSpec((r, c), lambda i: (0, 0))
    return pl.pallas_call(
        body, name="in_proj_bwd_rows", grid=(SEQ // tm,),
        out_shape=(jax.ShapeDtypeStruct((SEQ, D_MODEL), F32), jax.ShapeDtypeStruct((SEQ, IN_COLS), BF16),
                   jax.ShapeDtypeStruct((1, D_MODEL), F32)),
        in_specs=[tile(COL_BLOCK) for _ in range(N_DEV)] + [
            pl.BlockSpec((D_MODEL, IN_COLS), lambda i: (0, 0), pipeline_mode=pl.Buffered(1)),
            tile(D_MODEL), tile(D_MODEL), fixed(1, D_MODEL), tile(LANES), tile(LANES), tile(LANES)],
        out_specs=(tile(D_MODEL), tile(IN_COLS), fixed(1, D_MODEL)),
        compiler_params=_params(("arbitrary",)),
    )(*d_groups, w_g, x, dx2, mix_w, rc, rsa, rsb)


def _weights_exchange(hn_t, dproj_b, dwout_p, small_p):
    n_chips = N_DEV // 2
    rb = 128
    S1_IN, S1_OUT, SMALL, S2_IN, S2_OUT = 0, 4, 8, 15, 18
    rel_of_pair = (1, 2, 3, 0)

    def body(order_ref, hnt_ref, dp_ref, dwout_ref, small_ref, gin_ref, gout_ref, gs_ref,
             part, s1_send, s1_in, s1_out, fwd_in, fwd_out, s2_in, s2_out, land_s, send_sems, recv_sems):
        t = pl.program_id(0)
        me = _my_place()
        x, y, c = me
        my_chip = 2 * x + y
        sibling = (x, y, 1 - c)

        def remote(slot, src, dst, to):
            return pltpu.make_async_remote_copy(src_ref=src, dst_ref=dst, send_sem=send_sems.at[slot],
                                                recv_sem=recv_sems.at[slot], device_id=to, device_id_type=MESH)

        def s1_in_copy(pair):
            return remote(S1_IN + pair, s1_send.at[pair], s1_in.at[pair], sibling)

        def s1_out_copy(pair):
            q = my_chip ^ rel_of_pair[pair]
            return remote(S1_OUT + pair, dwout_ref.at[q, 1 - c], s1_out.at[pair], sibling)

        def s2_copies(rel):
            peer = _peer(me, 2 * rel)
            return [remote(S2_IN + rel - 1, fwd_in.at[rel - 1], s2_in.at[rel - 1], peer),
                    remote(S2_OUT + rel - 1, fwd_out.at[rel - 1], s2_out.at[rel - 1], peer)]

        def small_copy(rel):
            return remote(SMALL + rel - 1, small_ref, land_s.at[rel], _peer(me, rel))

        @pl.when(t == 0)
        def _():
            land_s[0] = small_ref[...]
            for pair in range(n_chips):
                s1_out_copy(pair).start()
            for rel in range(1, N_DEV):
                small_copy(rel).start()

        part[...] = _mm(hnt_ref[...], dp_ref[...])

        def rows_loop(n_rows, fn):
            def step(b, carry):
                fn(pl.ds(pl.multiple_of(b * rb, rb), rb))
                return carry
            lax.fori_loop(0, n_rows // rb, step, 0)

        for pair, rel in enumerate(rel_of_pair):
            @pl.when(t == 2 * pair)
            def _(pair=pair):
                s1_send[pair] = part[...].astype(BF16)
                s1_in_copy(pair).start()

            @pl.when(t == 2 * pair + 1)
            def _(pair=pair, rel=rel):
                q = my_chip ^ rel
                s1_in_copy(pair).wait_recv()
                s1_out_copy(pair).wait_recv()
                dst_in = fwd_in.at[rel - 1] if rel else gin_ref
                dst_out = fwd_out.at[rel - 1] if rel else gout_ref

                def add_in(rows):
                    dst_in[rows, :] = (part[rows, :] + s1_in[pair, rows, :].astype(F32)).astype(dst_in.dtype)

                def add_out(rows):
                    dst_out[rows, :] = (dwout_ref[q, c, rows, :].astype(F32)
                                        + s1_out[pair, rows, :].astype(F32)).astype(dst_out.dtype)

                rows_loop(D_MODEL, add_in)
                rows_loop(WOUT_ROWS, add_out)
                if rel:
                    for cp in s2_copies(rel):
                        cp.start()

        @pl.when(t == N_DEV - 1)
        def _():
            for rel in range(1, n_chips):
                for cp in s2_copies(rel):
                    cp.wait_recv()

            def total_in(rows):
                g = gin_ref[rows, :]
                for rel in range(1, n_chips):
                    g = g + s2_in[rel - 1, rows, :].astype(F32)
                gin_ref[rows, :] = g

            def total_out(rows):
                g = gout_ref[rows, :]
                for rel in range(1, n_chips):
                    g = g + s2_out[rel - 1, rows, :].astype(F32)
                gout_ref[rows, :] = g

            rows_loop(D_MODEL, total_in)
            rows_loop(WOUT_ROWS, total_out)

            for rel in range(1, N_DEV):
                small_copy(rel).wait_recv()
            my_flat = _flat(me)
            g = land_s[my_flat ^ 0]
            for dev in range(1, N_DEV):
                g = g + land_s[my_flat ^ dev]
            gs_ref[...] = g

            for pair in range(n_chips):
                s1_in_copy(pair).wait_send()
                s1_out_copy(pair).wait_send()
            for rel in range(1, n_chips):
                for cp in s2_copies(rel):
                    cp.wait_send()
            for rel in range(1, N_DEV):
                small_copy(rel).wait_send()

    place_x, place_y, place_c = _my_place()
    my_chip = 2 * place_x + place_y
    order = jnp.stack([2 * (my_chip ^ rel) + core for rel in rel_of_pair
                       for core in (1 - place_c, place_c)]).astype(jnp.int32)

    whole = lambda: pl.BlockSpec(memory_space=pltpu.VMEM)
    in_blocks = lambda n: pltpu.VMEM((n, D_MODEL, COL_BLOCK), BF16)
    out_blocks = lambda n: pltpu.VMEM((n, WOUT_ROWS, D_MODEL), BF16)
    grid_spec = pltpu.PrefetchScalarGridSpec(
        num_scalar_prefetch=1, grid=(N_DEV,),
        in_specs=[pl.BlockSpec((D_MODEL, SEQ), lambda t, order: (0, 0), pipeline_mode=pl.Buffered(1)),
                  pl.BlockSpec((SEQ, COL_BLOCK), lambda t, order: (0, order[t])), whole(), whole()],
        out_specs=(whole(), whole(), whole()),
        scratch_shapes=[pltpu.VMEM((D_MODEL, COL_BLOCK), F32), in_blocks(n_chips), in_blocks(n_chips),
                        out_blocks(n_chips), in_blocks(n_chips - 1), out_blocks(n_chips - 1),
                        in_blocks(n_chips - 1), out_blocks(n_chips - 1),
                        pltpu.VMEM((N_DEV, SMALL_ROWS, LANES), F32),
                        pltpu.SemaphoreType.DMA((21,)), pltpu.SemaphoreType.DMA((21,))])
    return pl.pallas_call(
        body, name="weights_exchange", grid_spec=grid_spec,
        out_shape=(jax.ShapeDtypeStruct((D_MODEL, COL_BLOCK), F32), jax.ShapeDtypeStruct((WOUT_ROWS, D_MODEL), F32),
                   jax.ShapeDtypeStruct((SMALL_ROWS, LANES), F32)),
        compiler_params=_params(("arbitrary",)),
    )(order, hn_t, dproj_b, dwout_p.reshape(n_chips, 2, WOUT_ROWS, D_MODEL), small_p)


def _adamw(w, g, m, v):
    m = ADAM_B1 * m + (1.0 - ADAM_B1) * g
    v = ADAM_B2 * v + (1.0 - ADAM_B2) * (g * g)
    m_hat = m / (1.0 - ADAM_B1 ** ADAM_STEP)
    v_hat = v / (1.0 - ADAM_B2 ** ADAM_STEP)
    delta = -ADAM_LR * (m_hat / (jnp.sqrt(v_hat) + ADAM_EPS) + ADAM_WD * w)
    return delta, m, v


def _adamw_update(grads, weights, m_old, v_old):
    rb = 256

    def body(*refs):
        g_refs, w_refs, m_refs, v_refs = refs[0:3], refs[3:6], refs[6:9], refs[9:12]
        d_refs, nm_refs, nv_refs = refs[12:15], refs[15:18], refs[18:21]
        for k in range(3):
            n_rows = g_refs[k].shape[0]
            step_rows = min(rb, n_rows)

            def step(b, carry, k=k, step_rows=step_rows):
                rows = pl.ds(pl.multiple_of(b * step_rows, 8), step_rows)
                delta, nm, nv = _adamw(w_refs[k][rows, :], g_refs[k][rows, :], m_refs[k][rows, :], v_refs[k][rows, :])
                d_refs[k][rows, :] = delta
                nm_refs[k][rows, :] = nm
                nv_refs[k][rows, :] = nv
                return carry

            lax.fori_loop(0, n_rows // step_rows, step, 0)

    shapes = tuple(jax.ShapeDtypeStruct(g.shape, F32) for g in grads)
    vm = lambda: pl.BlockSpec(memory_space=pltpu.VMEM)
    outs = pl.pallas_call(
        body, name="adamw_update", out_shape=shapes * 3,
        in_specs=[vm() for _ in range(12)], out_specs=tuple(vm() for _ in range(9)),
        compiler_params=_params(),
    )(*grads, *weights, *m_old, *v_old)
    return outs[0:3], outs[3:6], outs[6:9]


def _pack_small(mix, attn, hgrn, lb, final, loss=None):
    def rows8(a):
        a = a.reshape(-1, LANES)
        return jnp.pad(a, ((0, 8 - a.shape[0]), (0, 0)))
    last = jnp.zeros((8, LANES), F32) if loss is None else jnp.pad(loss.reshape(1, 1), ((0, 7), (0, LANES - 1)))
    return jnp.concatenate([rows8(mix), rows8(attn), rows8(hgrn), rows8(lb), rows8(final), last], axis=0)


def _unpack_small(slab):
    return (slab[ROW_MIX:ROW_MIX + 8].reshape(1, D_MODEL), slab[ROW_ATTN:ROW_ATTN + 4].reshape(1, ATTN_WIDTH),
            slab[ROW_HGRN:ROW_HGRN + 4].reshape(1, HGRN_WIDTH), slab[ROW_LB:ROW_LB + 8].reshape(2, HGRN_WIDTH),
            slab[ROW_FINAL:ROW_FINAL + 8].reshape(D_MODEL))


def _rope(pos_col):
    lane_e = np.arange(LANES) % HEAD_DIM
    inv = ROPE_THETA ** (-(lane_e % ROPE_HALF) * (2.0 / ROPE_DIMS))
    inv_lanes = np.where(lane_e < ROPE_DIMS, inv, 0.0).astype(np.float32).reshape(1, LANES)
    return _rope_tables(pos_col, jnp.asarray(inv_lanes))


def _local_step(x, proj, qkv_sorted, w_in_g, w_out_g, tables, mix_w, attn_w, hgrn_w, lb_raw, final_w, target):
    rc, rsa, rsb = tables
    attn_o, lse = _attn_fwd_fused(qkv_sorted)
    rec, states = _hgrn_fwd(proj, lb_raw)

    (dx2, d_o, delta, d_ag, d_rec, d_hg, dwout_p, d_final, d_attn_w, d_hgrn_w, loss) = _mid(
        attn_o, rec, proj, x, target, w_out_g, attn_w, hgrn_w, final_w.reshape(1, D_MODEL))

    dqkv = _attn_bwd_fused(qkv_sorted, d_o, lse, delta)
    d_hq, d_hf, d_hi, d_lb = _hgrn_bwd(proj, lb_raw, d_rec, states)

    grad_x, dproj_b, d_mix = _in_proj_bwd_rows(
        (dqkv[0], dqkv[1], dqkv[2], d_ag, d_hq, d_hf, d_hi, d_hg), w_in_g, x, dx2, mix_w, rc, rsa, rsb)
    small_p = _pack_small(d_mix, d_attn_w, d_hgrn_w, d_lb, d_final, loss)
    return grad_x, dproj_b, dwout_p, small_p


def kernel(x, positions, w_in, w_out, mix_norm_w, attn_out_norm_w, hgrn_out_norm_w, hgrn_lb_raw, final_norm_w, loss_target, m_w_in, m_w_out, m_mix_norm_w, m_attn_out_norm_w, m_hgrn_out_norm_w, m_hgrn_lb_raw, m_final_norm_w, v_w_in, v_w_out, v_mix_norm_w, v_attn_out_norm_w, v_hgrn_out_norm_w, v_hgrn_lb_raw, v_final_norm_w):
    tables = _rope(positions.reshape(SEQ, 1))
    proj, hn_t, w_in_g, w_out_g, qkv_sorted = _gather_project(x[0], mix_norm_w, w_in[0], w_out[0], *tables)
    grad_x, dproj_b, dwout_p, small_p = _local_step(
        x[0], proj, qkv_sorted, w_in_g, w_out_g, tables, mix_norm_w, attn_out_norm_w, hgrn_out_norm_w,
        hgrn_lb_raw, final_norm_w, loss_target[0])
    g_in, g_out, g_s = _weights_exchange(hn_t, dproj_b, dwout_p, small_p)

    w_s = _pack_small(mix_norm_w, attn_out_norm_w, hgrn_out_norm_w, hgrn_lb_raw, final_norm_w)
    m_s = _pack_small(m_mix_norm_w, m_attn_out_norm_w, m_hgrn_out_norm_w, m_hgrn_lb_raw, m_final_norm_w)
    v_s = _pack_small(v_mix_norm_w, v_attn_out_norm_w, v_hgrn_out_norm_w, v_hgrn_lb_raw, v_final_norm_w)
    (d_in, d_out, d_s), (nm_in, nm_out, nm_s), (nv_in, nv_out, nv_s) = _adamw_update(
        (g_in, g_out, g_s), (w_in[0], w_out[0], w_s), (m_w_in[0], m_w_out[0], m_s), (v_w_in[0], v_w_out[0], v_s))

    loss = g_s[ROW_LOSS, 0]
    return (loss, grad_x[None], g_in[None], g_out[None], *_unpack_small(g_s),
            d_in[None], d_out[None], *_unpack_small(d_s),
            nm_in[None], nm_out[None], *_unpack_small(nm_s),
            nv_in[None], nv_out[None], *_unpack_small(nv_s))
```

```python
import functools

import jax
import jax.numpy as jnp
import numpy as np
from jax import lax
from jax.experimental import pallas as pl
from jax.experimental.pallas import tpu as pltpu

F32 = jnp.float32
BF16 = jnp.bfloat16

SEQ = 4096
D_MODEL = 1024
ATTN_WIDTH = 512
HGRN_WIDTH = 512
HEAD_DIM = 64
HGRN_HEADS = 4
HGRN_DIM = 128
HGRN_CHUNK = 64
N_CHUNKS = SEQ // HGRN_CHUNK
IN_COLS = 4096
COL_BLOCK = 512
N_DEV = 8
WOUT_ROWS = D_MODEL // N_DEV
ATTN_BLOCK = 128
DILATIONS = (1, 4, 16)
ROPE_THETA = 500000.0
ROPE_DIMS = 16
ROPE_HALF = 8
NORM_EPS = 1e-6
NEG_BIG = -1e30
LANES = 128

ADAM_LR = 0.001
ADAM_B1 = 0.9
ADAM_B2 = 0.999
ADAM_EPS = 1e-08
ADAM_WD = 0.01
ADAM_STEP = 10

SMALL_ROWS = 48
ROW_MIX, ROW_ATTN, ROW_HGRN, ROW_LB, ROW_FINAL, ROW_LOSS = 0, 8, 16, 24, 32, 40

VMEM_LIMIT = 56 * 1024 * 1024
MESH = pl.DeviceIdType.MESH


def _mm(a, b):
    return lax.dot_general(a, b, (((1,), (0,)), ((), ())), preferred_element_type=F32)


def _mm_nt(a, b):
    return lax.dot_general(a, b, (((1,), (1,)), ((), ())), preferred_element_type=F32)


def _mm_tn(a, b):
    return lax.dot_general(a, b, (((0,), (0,)), ((), ())), preferred_element_type=F32)


def _mm_exact(a, b):
    return lax.dot_general(a, b, (((1,), (0,)), ((), ())), preferred_element_type=F32,
                           precision=lax.Precision.HIGHEST)


def _sigmoid(v):
    return 1.0 / (1.0 + jnp.exp(-v))


def _params(sem=None, **kw):
    return pltpu.CompilerParams(dimension_semantics=sem, vmem_limit_bytes=VMEM_LIMIT, **kw)


def _my_place():
    return lax.axis_index("x"), lax.axis_index("y"), lax.axis_index("c")


def _peer(place, rel):
    x, y, c = place
    return (x ^ ((rel >> 2) & 1), y ^ ((rel >> 1) & 1), c ^ (rel & 1))


def _flat(place):
    x, y, c = place
    return 4 * x + 2 * y + c


def _rope_tables(pos_col, inv_freq_lanes):
    tm = 512

    def body(pos_ref, invf_ref, c_ref, sa_ref, sb_ref):
        ang = pos_ref[...].astype(F32) * invf_ref[...]
        e = lax.broadcasted_iota(jnp.int32, (tm, LANES), 1) & (HEAD_DIM - 1)
        cos, sin = jnp.cos(ang), jnp.sin(ang)
        c_ref[...] = jnp.where(e < ROPE_DIMS, cos, 1.0)
        sa_ref[...] = jnp.where((e >= ROPE_HALF) & (e < ROPE_DIMS), sin, 0.0)
        sb_ref[...] = jnp.where(e < ROPE_HALF, -sin, 0.0)

    tab = jax.ShapeDtypeStruct((SEQ, LANES), F32)
    spec = pl.BlockSpec((tm, LANES), lambda i: (i, 0))
    return pl.pallas_call(
        body, name="rope_tables", grid=(SEQ // tm,), out_shape=(tab, tab, tab),
        in_specs=[pl.BlockSpec((tm, 1), lambda i: (i, 0)), pl.BlockSpec((1, LANES), lambda i: (0, 0))],
        out_specs=(spec, spec, spec), compiler_params=_params(("parallel",)),
    )(pos_col, inv_freq_lanes)


def _per_slab(fn, t):
    return jnp.concatenate([fn(t[:, LANES * s:LANES * (s + 1)]) for s in range(t.shape[1] // LANES)], axis=1)


def _rot(t, c, sa, sb):
    return _per_slab(lambda u: u * c + pltpu.roll(u, ROPE_HALF, 1) * sa + pltpu.roll(u, LANES - ROPE_HALF, 1) * sb, t)


def _rot_transposed(g, c, sa, sb):
    return _per_slab(
        lambda u: u * c + pltpu.roll(u * sa, LANES - ROPE_HALF, 1) + pltpu.roll(u * sb, ROPE_HALF, 1), g)


def _gather_project(x, mix_w, w_in, w_out, rc, rsa, rsb):
    tm = 1024
    n_tiles = SEQ // tm
    arrival_of_step = (None, 0, 1, 2, 4, 5, 3, 6)

    def body(order_ref, x_ref, w_ref, win_ref, wout_ref, c_ref, sa_ref, sb_ref,
             proj_ref, hnt_ref, gin_hbm, gout_hbm, qkv_hbm,
             hn_s, w_land, wout_land, stage, sort_stage, slab_tmp, send_sems, recv_sems, local_sems):
        g, i = pl.program_id(0), pl.program_id(1)
        me = _my_place()
        x_, y_, c_ = me
        sibling = (x_, y_, 1 - c_)
        chips = [(1 - x_, y_), (x_, 1 - y_), (1 - x_, 1 - y_)]

        def slab(which, place):
            idx = _flat(place)
            if which == 0:
                return w_land.at[idx]
            return wout_land.at[pl.ds(pl.multiple_of(idx * WOUT_ROWS, WOUT_ROWS), WOUT_ROWS), :]

        def copy(which, k, block, to, src=None):
            ref = slab(which, block)
            return pltpu.make_async_remote_copy(
                src_ref=ref if src is None else src, dst_ref=ref, send_sem=send_sems.at[7 * which + k],
                recv_sem=recv_sems.at[7 * which + k], device_id=to, device_id_type=MESH)

        def first_copies(which):
            src = stage if which == 0 else None
            return ([copy(which, 0, me, sibling, src)]
                    + [copy(which, 1 + j, me, (*chip, c_), src) for j, chip in enumerate(chips)])

        def pass_on(which, j):
            return copy(which, 4 + j, (*chips[j], c_), sibling)

        def arrival(which, k):
            if k == 0:
                return copy(which, 0, sibling, me)
            if k <= 3:
                return copy(which, k, (*chips[k - 1], c_), me)
            return copy(which, k, (*chips[k - 4], 1 - c_), me)

        def to_hbm(step):
            idx = order_ref[step]
            cols = pl.ds(pl.multiple_of(idx * COL_BLOCK, COL_BLOCK), COL_BLOCK)
            return pltpu.make_async_copy(w_land.at[idx], gin_hbm.at[:, cols], local_sems.at[step])

        @pl.when((g == 0) & (i == 0))
        def _():
            stage[...] = win_ref[...].astype(BF16)
            w_land[_flat(me)] = stage[...]
            wout_land[pl.ds(pl.multiple_of(_flat(me) * WOUT_ROWS, WOUT_ROWS), WOUT_ROWS), :] = (
                wout_ref[...].astype(BF16))
            for cp in first_copies(0) + first_copies(1)[:1]:
                cp.start()
            to_hbm(0).start()

        for step, k in enumerate(arrival_of_step):
            if k is None:
                continue

            @pl.when((g == step) & (i == 0))
            def _(k=k, step=step):
                arrival(0, k).wait_recv()
                to_hbm(step).start()
                if k == 1:
                    for cp in first_copies(1)[1:]:
                        cp.start()
                if 1 <= k <= 3:
                    pass_on(0, k - 1).start()

        rows = pl.ds(pl.multiple_of(i * tm, tm), tm)

        @pl.when(g == 0)
        def _():
            xf = x_ref[...]
            ms = jnp.mean(xf * xf, axis=-1, keepdims=True)
            hn = xf * lax.rsqrt(ms + NORM_EPS) * w_ref[...]
            hnt_ref[...] = hn.T.astype(BF16)
            hn_s[rows, :] = hn.astype(BF16)

        group = order_ref[g]

        def sorted_copy():
            per = tm // SORT_RESIDUES
            for s in range(COL_BLOCK // LANES):
                slab_tmp[s] = proj_ref[:, LANES * s:LANES * (s + 1)]
            for r in range(SORT_RESIDUES):
                for s in range(COL_BLOCK // LANES):
                    sort_stage[r, :, LANES * s:LANES * (s + 1)] = (
                        slab_tmp.at[s][pl.ds(r, per, stride=SORT_RESIDUES), :])
            cols = pl.ds(pl.multiple_of(group * COL_BLOCK, COL_BLOCK), COL_BLOCK)
            cp = pltpu.make_async_copy(
                sort_stage, qkv_hbm.at[:, pl.ds(pl.multiple_of(i * per, per), per), cols], local_sems.at[N_DEV + 1])
            cp.start()
            cp.wait()

        @pl.when(group < 2)
        def _():
            proj_ref[...] = _rot(_mm(hn_s[rows, :], w_land[group]), c_ref[...], sa_ref[...], sb_ref[...])
            sorted_copy()

        @pl.when(group == 2)
        def _():
            proj_ref[...] = _mm(hn_s[rows, :], w_land[group])
            sorted_copy()

        @pl.when(group > 2)
        def _():
            proj_ref[...] = _mm(hn_s[rows, :], w_land[group])

        @pl.when((g == N_DEV - 1) & (i == n_tiles - 1))
        def _():
            for j in range(3):
                arrival(1, 1 + j).wait_recv()
                pass_on(1, j).start()
            for k in (0, 4, 5, 6):
                arrival(1, k).wait_recv()
            for which in (0, 1):
                for cp in first_copies(which) + [pass_on(which, j) for j in range(3)]:
                    cp.wait_send()
            wout_copy = pltpu.make_async_copy(wout_land, gout_hbm, local_sems.at[N_DEV])
            wout_copy.start()
            for step in range(N_DEV):
                to_hbm(step).wait()
            wout_copy.wait()

    me = _my_place()
    x_, y_, c_ = me
    chips = [(1 - x_, y_), (x_, 1 - y_), (1 - x_, 1 - y_)]
    order = jnp.stack([_flat(p) for p in (
        me, (x_, y_, 1 - c_), (*chips[0], c_), (*chips[1], c_), (*chips[0], 1 - c_), (*chips[1], 1 - c_),
        (*chips[2], c_), (*chips[2], 1 - c_))]).astype(jnp.int32)

    first_sweep = lambda g, i, order: (jnp.where(g == 0, i, n_tiles - 1), 0)
    tab = pl.BlockSpec((tm, LANES), lambda g, i, order: (jnp.where(order[g] < 2, i, 0), 0))
    whole = lambda: pl.BlockSpec(memory_space=pltpu.VMEM)
    grid_spec = pltpu.PrefetchScalarGridSpec(
        num_scalar_prefetch=1, grid=(N_DEV, n_tiles),
        in_specs=[pl.BlockSpec((tm, D_MODEL), first_sweep),
                  pl.BlockSpec((1, D_MODEL), lambda g, i, order: (0, 0)),
                  whole(), whole(), tab, tab, tab],
        out_specs=(pl.BlockSpec((tm, COL_BLOCK), lambda g, i, order: (i, order[g])),
                   pl.BlockSpec((D_MODEL, tm), lambda g, i, order: (0, jnp.where(g == 0, i, n_tiles - 1))),
                   pl.BlockSpec(memory_space=pl.ANY), pl.BlockSpec(memory_space=pl.ANY),
                   pl.BlockSpec(memory_space=pl.ANY)),
        scratch_shapes=[pltpu.VMEM((SEQ, D_MODEL), BF16),
                        pltpu.VMEM((N_DEV, D_MODEL, COL_BLOCK), BF16),
                        pltpu.VMEM((D_MODEL, D_MODEL), BF16),
                        pltpu.VMEM((D_MODEL, COL_BLOCK), BF16),
                        pltpu.VMEM((SORT_RESIDUES, tm // SORT_RESIDUES, COL_BLOCK), F32),
                        pltpu.VMEM((COL_BLOCK // LANES, tm, LANES), F32),
                        pltpu.SemaphoreType.DMA((14,)), pltpu.SemaphoreType.DMA((14,)),
                        pltpu.SemaphoreType.DMA((N_DEV + 2,))])
    proj, hn_t, w_in_g, w_out_g, qkv_sorted = pl.pallas_call(
        body, name="gather_project", grid_spec=grid_spec,
        out_shape=(jax.ShapeDtypeStruct((SEQ, IN_COLS), F32), jax.ShapeDtypeStruct((D_MODEL, SEQ), BF16),
                   jax.ShapeDtypeStruct((D_MODEL, IN_COLS), BF16), jax.ShapeDtypeStruct((D_MODEL, D_MODEL), BF16),
                   jax.ShapeDtypeStruct((SORT_RESIDUES, SORT_ROWS, 3 * COL_BLOCK), F32)),
        compiler_params=_params(("arbitrary", "arbitrary")),
    )(order, x, mix_w, w_in, w_out, rc, rsa, rsb)
    return proj, hn_t, w_in_g, w_out_g, qkv_sorted.reshape(SEQ, 3 * COL_BLOCK)


SCORE_SCALE = HEAD_DIM ** -0.5
ATTN_GROUP_FWD = 16
ATTN_GROUP_BWD = 8
BLOCKS_PER_PATTERN = SEQ // ATTN_BLOCK
SORT_RESIDUES = 16
SORT_ROWS = SEQ // SORT_RESIDUES


def _write_band_bias(bias_ref):
    row = lax.broadcasted_iota(jnp.int32, (2 * ATTN_BLOCK, 2 * ATTN_BLOCK), 0) & (ATTN_BLOCK - 1)
    col = lax.broadcasted_iota(jnp.int32, (2 * ATTN_BLOCK, 2 * ATTN_BLOCK), 1)
    for pi, d in enumerate(DILATIONS):
        per = SORT_RESIDUES // d
        ahead = per * (row % (8 * d) - col % (16 * d)) + (row // (8 * d) - col // (16 * d))
        dist = ATTN_BLOCK + ahead
        bias_ref[2 * pi] = jnp.where((dist >= 0) & (dist <= ATTN_BLOCK), 0.0, NEG_BIG)
        bias_ref[2 * pi + 1] = jnp.where(ahead >= 0, 0.0, NEG_BIG)


def _head0_lanes():
    return lax.broadcasted_iota(jnp.int32, (ATTN_BLOCK, LANES), 1) < HEAD_DIM


def _stack_heads(t, h0):
    return jnp.concatenate([jnp.where(h0, t, 0.0), jnp.where(h0, 0.0, t)], axis=0).astype(BF16)


def _block_runs(i, d):
    nblk = BLOCKS_PER_PATTERN // d
    r, n = i // nblk, i % nblk
    kn = jnp.maximum(n - 1, 0)
    rows, keys = [], []
    for c in range(SORT_RESIDUES // d):
        base = SORT_ROWS * (c * d + r)
        rows.append(pl.ds(pl.multiple_of(base + 8 * d * n, 8), 8 * d))
        keys.append(pl.ds(pl.multiple_of(base + 8 * d * kn, 8), 16 * d))
    return rows, keys, (n == 0).astype(jnp.int32)


def _take(ref, runs):
    return jnp.concatenate([ref[run, :] for run in runs], axis=0)


def _put(ref, runs, value, add=False):
    at = 0
    for run in runs:
        piece = value[at:at + run.size]
        if add:
            ref[run, :] += piece
        else:
            ref[run, :] = piece
        at += run.size


def _sort_rows(src_ref, dst_ref):
    for r in range(SORT_RESIDUES):
        dst_ref[SORT_ROWS * r:SORT_ROWS * (r + 1), :] = src_ref[pl.ds(r, SORT_ROWS, stride=SORT_RESIDUES), :]


def _unsort_rows(src_ref, dst_ref):
    for r in range(SORT_RESIDUES):
        dst_ref[pl.ds(r, SORT_ROWS, stride=SORT_RESIDUES), :] = src_ref[SORT_ROWS * r:SORT_ROWS * (r + 1), :]


def _for_each_group(d, n_group, load, compute, store):
    def group(g, carry):
        items = [load(*_block_runs(g * n_group + u, d)) for u in range(n_group)]
        results = [compute(item) for item in items]
        for item, res in zip(items, results):
            store(item, res)
        return carry

    lax.fori_loop(0, BLOCKS_PER_PATTERN // n_group, group, 0)


def _attn_fwd_fused(qkv_sorted):
    n_pat = len(DILATIONS)
    tile2 = (2 * ATTN_BLOCK, LANES)

    def body(q_ref, k_ref, v_ref, o_ref, lse_ref, o_acc, m_acc, l_acc, bias_ref):
        pl.when(pl.program_id(0) == 0)(lambda: _write_band_bias(bias_ref))
        h0 = _head0_lanes()
        for pi, d in enumerate(DILATIONS):
            first, last = pi == 0, pi == n_pat - 1

            def load(rows, keys, which, first=first, pi=pi):
                item = dict(rows=rows, keys=keys, which=2 * pi + which)
                if not first:
                    item.update(o=_take(o_acc, rows), m=[_take(m_acc.at[h], rows) for h in range(2)],
                                l=[_take(l_acc.at[h], rows) for h in range(2)])
                return item

            def compute(item, first=first):
                kb = _take(k_ref, item["keys"]).astype(BF16)
                vb = _take(v_ref, item["keys"]).astype(BF16)
                s = _mm_nt(_stack_heads(_take(q_ref, item["rows"]) * SCORE_SCALE, h0), kb) + bias_ref[item["which"]]
                mb = jnp.max(s, axis=-1, keepdims=True)
                if first:
                    p = jnp.exp(s - mb)
                    mn = jnp.broadcast_to(mb, tile2)
                else:
                    m_old = jnp.concatenate(item["m"], axis=0)
                    mn = jnp.maximum(m_old, mb)
                    alpha = jnp.exp(m_old - mn)
                    p = jnp.exp(s - jnp.concatenate([mn, mn], axis=1))
                ls = jnp.sum(p, axis=-1, keepdims=True)
                pv = _mm(p.astype(BF16), vb)
                if first:
                    return pv, mn, jnp.broadcast_to(ls, tile2)
                o_old = jnp.concatenate([item["o"], item["o"]], axis=0)
                return alpha * o_old + pv, mn, alpha * jnp.concatenate(item["l"], axis=0) + ls

            def store(item, res, last=last):
                rows = item["rows"]
                (o0, o1), (m0, m1), (l0, l1) = ((a[:ATTN_BLOCK], a[ATTN_BLOCK:]) for a in res)
                if last:
                    _put(o_acc, rows, jnp.where(h0, o0 / l0, o1 / l1))
                    _put(lse_ref, rows, jnp.where(h0, m0 + jnp.log(l0), m1 + jnp.log(l1)))
                else:
                    _put(o_acc, rows, jnp.where(h0, o0, o1))
                    for h, (m, l) in enumerate(((m0, l0), (m1, l1))):
                        _put(m_acc.at[h], rows, m)
                        _put(l_acc.at[h], rows, l)

            _for_each_group(d, ATTN_GROUP_FWD, load, compute, store)
        _unsort_rows(o_acc, o_ref)

    slab = lambda g: pl.BlockSpec((SEQ, LANES), functools.partial(lambda hp, g: (0, 4 * g + hp), g=g))
    wide = jax.ShapeDtypeStruct((SEQ, ATTN_WIDTH), F32)
    return pl.pallas_call(
        body, name="attn_fwd", grid=(4,), out_shape=(wide, wide),
        in_specs=[slab(0), slab(1), slab(2)], out_specs=(slab(0), slab(0)),
        scratch_shapes=[pltpu.VMEM((SEQ, LANES), F32), pltpu.VMEM((2, SEQ, LANES), F32),
                        pltpu.VMEM((2, SEQ, LANES), F32),
                        pltpu.VMEM((2 * len(DILATIONS), 2 * ATTN_BLOCK, 2 * ATTN_BLOCK), F32)],
        compiler_params=_params(("arbitrary",)),
    )(qkv_sorted, qkv_sorted, qkv_sorted)


def _attn_bwd_fused(qkv_sorted, d_out, lse_sorted, delta):
    def body(q_ref, k_ref, v_ref, do_ref, lse_ref, del_ref, dq_ref, dk_ref, dv_ref,
             do_s, del_s, dq_s, dk_s, dv_s, bias_ref):
        pl.when(pl.program_id(0) == 0)(lambda: _write_band_bias(bias_ref))
        _sort_rows(do_ref, do_s)
        _sort_rows(del_ref, del_s)
        dk_s[...] = jnp.zeros_like(dk_s)
        dv_s[...] = jnp.zeros_like(dv_s)
        h0 = _head0_lanes()
        for pi, d in enumerate(DILATIONS):
            first = pi == 0

            def load(rows, keys, which, pi=pi):
                return dict(rows=rows, keys=keys, q=_take(q_ref, rows), g=_take(do_s, rows),
                            lse=_take(lse_ref, rows), delta=_take(del_s, rows),
                            k=_take(k_ref, keys).astype(BF16), v=_take(v_ref, keys).astype(BF16),
                            bias=bias_ref[2 * pi + which])

            def per_head(t):
                swapped = pltpu.roll(t, HEAD_DIM, 1)
                both = jnp.concatenate([jnp.where(h0, t, swapped), jnp.where(h0, swapped, t)], axis=0)
                return jnp.concatenate([both, both], axis=1)

            def compute(item):
                q2, g2 = _stack_heads(item["q"] * SCORE_SCALE, h0), _stack_heads(item["g"], h0)
                s = _mm_nt(q2, item["k"]) + item["bias"]
                p = jnp.exp(s - per_head(item["lse"]))
                dp = _mm_nt(g2, item["v"])
                ds = (p * (dp - per_head(item["delta"]))).astype(BF16)
                dq2 = _mm(ds, item["k"])
                dq = jnp.where(h0, dq2[:ATTN_BLOCK], dq2[ATTN_BLOCK:]) * SCORE_SCALE
                return dq, _mm_tn(ds, q2), _mm_tn(p.astype(BF16), g2)

            def store(item, res, first=first):
                _put(dq_s, item["rows"], res[0], add=not first)
                _put(dk_s, item["keys"], res[1], add=True)
                _put(dv_s, item["keys"], res[2], add=True)

            _for_each_group(d, ATTN_GROUP_BWD, load, compute, store)
        _unsort_rows(dq_s, dq_ref)
        _unsort_rows(dk_s, dk_ref)
        _unsort_rows(dv_s, dv_ref)

    slab = lambda g: pl.BlockSpec((SEQ, LANES), functools.partial(lambda hp, g: (0, 4 * g + hp), g=g))
    wide = jax.ShapeDtypeStruct((SEQ, ATTN_WIDTH), F32)
    sorted_slab = pltpu.VMEM((SEQ, LANES), F32)
    return pl.pallas_call(
        body, name="attn_bwd", grid=(4,), out_shape=(wide, wide, wide),
        scratch_shapes=[sorted_slab] * 5 + [pltpu.VMEM((2 * len(DILATIONS), 2 * ATTN_BLOCK, 2 * ATTN_BLOCK), F32)],
        in_specs=[slab(0), slab(1), slab(2), slab(0), slab(0), slab(0)], out_specs=(slab(0), slab(0), slab(0)),
        compiler_params=_params(("arbitrary",)),
    )(qkv_sorted, qkv_sorted, qkv_sorted, d_out, lse_sorted, delta)


def _hgrn_lower_bound(lb_ref):
    r0, r1 = lb_ref[0:1, :], lb_ref[1:2, :]
    mx = jnp.maximum(r0, r1)
    e0, e1 = jnp.exp(r0 - mx), jnp.exp(r1 - mx)
    return e0 / (e0 + e1)


def _hgrn_gates(hq, hf, lb):
    sq = _sigmoid(hq)
    sg = _sigmoid(hf)
    f = lb + (1.0 - lb) * sg
    return hq * sq, sq, sg, f, 1.0 - f, jnp.log(f)


HGRN_PAIR = 4
HGRN_SEQ_BLOCK = 1024
HGRN_GROUP = 4
HGRN_ROWS = HGRN_GROUP * HGRN_CHUNK


def _hgrn_specs(reverse):
    n_blocks = SEQ // HGRN_SEQ_BLOCK
    width = HGRN_PAIR * HGRN_DIM
    blk = (lambda s: n_blocks - 1 - s) if reverse else (lambda s: s)
    cols = lambda g: pl.BlockSpec((HGRN_SEQ_BLOCK, width),
                                  functools.partial(lambda p, s, g: (blk(s), (HGRN_HEADS // HGRN_PAIR) * g + p), g=g))
    pair = pl.BlockSpec((HGRN_SEQ_BLOCK, width), lambda p, s: (blk(s), p))
    lb = pl.BlockSpec((2, width), lambda p, s: (0, p))
    states = pl.BlockSpec((HGRN_PAIR, HGRN_SEQ_BLOCK // HGRN_CHUNK, HGRN_DIM, HGRN_DIM),
                          lambda p, s: (p, blk(s), 0, 0))
    return cols, pair, lb, states


def _chunk_masks():
    ri = lax.broadcasted_iota(jnp.int32, (HGRN_ROWS, HGRN_ROWS), 0)
    ci = lax.broadcasted_iota(jnp.int32, (HGRN_ROWS, HGRN_ROWS), 1)
    same = (ri // HGRN_CHUNK) == (ci // HGRN_CHUNK)
    return same, same & (ri >= ci), same & (ri <= ci)


def _mm_select(sel, v):
    hi = v.astype(BF16)
    r1 = v - hi.astype(F32)
    mid = r1.astype(BF16)
    lo = (r1 - mid.astype(F32)).astype(BF16)
    return _mm(sel, hi) + _mm(sel, mid) + _mm(sel, lo)


def _head_cols(a, h):
    return a[:, HGRN_DIM * h:HGRN_DIM * (h + 1)]


def _hgrn_fwd(proj, lb_raw):
    t, rws = HGRN_CHUNK, HGRN_ROWS

    def body(hq_ref, hf_ref, hi_ref, lb_ref, rec_ref, st_ref, state):
        @pl.when(pl.program_id(1) == 0)
        def _():
            state[...] = jnp.zeros_like(state)

        lb = _hgrn_lower_bound(lb_ref)
        same, causal, _ = _chunk_masks()
        sel = jnp.concatenate([causal, same], axis=0).astype(BF16)

        def group(g, sts):
            rows = pl.ds(pl.multiple_of(g * rws, rws), rws)
            q, _, _, _, k, lf = _hgrn_gates(hq_ref[rows, :], hf_ref[rows, :], lb)
            sums = _mm_select(sel, lf)
            cum, last = sums[:rws], sums[rws:]
            qd = (q * jnp.exp(cum)).astype(BF16)
            ki = (k * jnp.exp(-cum)).astype(BF16)
            ke = (k * jnp.exp(last - cum)).astype(BF16)
            vb = hi_ref[rows, :].astype(BF16)
            dec = jnp.exp(last)
            new_sts, recs = [], []
            for h in range(HGRN_PAIR):
                qd_h, ke_h, vb_h = _head_cols(qd, h), _head_cols(ke, h), _head_cols(vb, h)
                att = jnp.where(causal, _mm_nt(qd_h, _head_cols(ki, h)), 0.0).astype(BF16)
                intra = _mm(att, vb_h)
                st = sts[h]
                outs = []
                for c in range(HGRN_GROUP):
                    sl = slice(c * t, (c + 1) * t)
                    st_ref[h, g * HGRN_GROUP + c] = st
                    outs.append(intra[sl] + _mm_nt(qd_h[sl], st.astype(BF16)))
                    st = st * _head_cols(dec[c * t:c * t + 1, :], h) + _mm_tn(vb_h[sl], ke_h[sl])
                new_sts.append(st)
                recs.append(jnp.concatenate(outs, axis=0))
            rec_ref[rows, :] = jnp.concatenate(recs, axis=1)
            return tuple(new_sts)

        sts = lax.fori_loop(0, HGRN_SEQ_BLOCK // rws, group, tuple(state[h] for h in range(HGRN_PAIR)))
        for h in range(HGRN_PAIR):
            state[h] = sts[h]

    cols, pair, lb, states = _hgrn_specs(reverse=False)
    return pl.pallas_call(
        body, name="hgrn_fwd", grid=(HGRN_HEADS // HGRN_PAIR, SEQ // HGRN_SEQ_BLOCK),
        out_shape=(jax.ShapeDtypeStruct((SEQ, HGRN_WIDTH), F32),
                   jax.ShapeDtypeStruct((HGRN_HEADS, N_CHUNKS, HGRN_DIM, HGRN_DIM), F32)),
        in_specs=[cols(4), cols(5), cols(6), lb], out_specs=(pair, states),
        scratch_shapes=[pltpu.VMEM((HGRN_PAIR, HGRN_DIM, HGRN_DIM), F32)],
        compiler_params=_params(("parallel", "arbitrary")),
    )(proj, proj, proj, lb_raw)


def _hgrn_bwd(proj, lb_raw, d_rec, states):
    t, rws = HGRN_CHUNK, HGRN_ROWS

    def body(hq_ref, hf_ref, hi_ref, lb_ref, do_ref, st_ref, dhq_ref, dhf_ref, dhi_ref, dlb_ref,
             dstate, dlb_acc):
        lb = _hgrn_lower_bound(lb_ref)
        same, causal, anti = _chunk_masks()
        sel = jnp.concatenate([causal, same], axis=0).astype(BF16)
        sel_t = jnp.concatenate([anti, same], axis=1).astype(BF16)
        @pl.when(pl.program_id(1) == 0)
        def _():
            dstate[...] = jnp.zeros_like(dstate)
            dlb_acc[...] = jnp.zeros_like(dlb_acc)

        n_groups = HGRN_SEQ_BLOCK // rws
        chunks = [slice(c * t, (c + 1) * t) for c in range(HGRN_GROUP)]

        def group(i, dsts_in):
            g = n_groups - 1 - i
            rows = pl.ds(pl.multiple_of(g * rws, rws), rws)
            hq = hq_ref[rows, :]
            q, sq, sg, f, k, lf = _hgrn_gates(hq, hf_ref[rows, :], lb)
            sums = _mm_select(sel, lf)
            cum, last = sums[:rws], sums[rws:]
            e_cum, e_inv, e_end, dec = jnp.exp(cum), jnp.exp(-cum), jnp.exp(last - cum), jnp.exp(last)
            qd, ki, ke = q * e_cum, k * e_inv, k * e_end
            qdb, kib, keb = qd.astype(BF16), ki.astype(BF16), ke.astype(BF16)
            vb = hi_ref[rows, :].astype(BF16)
            gb = do_ref[rows, :].astype(BF16)

            dsts_out, per_head = [], []
            for h in range(HGRN_PAIR):
                qdb_h, kib_h, keb_h = _head_cols(qdb, h), _head_cols(kib, h), _head_cols(keb, h)
                vb_h, gb_h = _head_cols(vb, h), _head_cols(gb, h)
                att = jnp.where(causal, _mm_nt(qdb_h, kib_h), 0.0).astype(BF16)
                datt = jnp.where(causal, _mm_nt(gb_h, vb_h), 0.0).astype(BF16)
                dv = _mm_tn(att, gb_h)
                dqd = _mm(datt, kib_h)
                dki = _mm_tn(datt, qdb_h)

                decs = [_head_cols(dec[c * t:c * t + 1, :], h) for c in range(HGRN_GROUP)]
                dsts = [None] * HGRN_GROUP
                dst = dsts_in[h]
                for c in reversed(range(HGRN_GROUP)):
                    dsts[c] = dst
                    dst = dst * decs[c] + _mm_tn(gb_h[chunks[c]], qdb_h[chunks[c]])
                dsts_out.append(dst)

                dv_x, dqd_x, dke, dlast_x = [], [], [], []
                for c, sl in enumerate(chunks):
                    st_prev = st_ref[h, g * HGRN_GROUP + c]
                    dstb = dsts[c].astype(BF16)
                    dv_x.append(_mm_nt(keb_h[sl], dstb))
                    dqd_x.append(_mm(gb_h[sl], st_prev.astype(BF16)))
                    dke.append(_mm(vb_h[sl], dstb))
                    ddec = jnp.sum(dsts[c] * st_prev, axis=0, keepdims=True)
                    dlast_x.append(jnp.broadcast_to(ddec * decs[c], (t, HGRN_DIM)))
                per_head.append((dv + jnp.concatenate(dv_x, axis=0), dqd + jnp.concatenate(dqd_x, axis=0),
                                 dki, jnp.concatenate(dke, axis=0), jnp.concatenate(dlast_x, axis=0)))
            dv, dqd, dki, dke, dlast = (jnp.concatenate(list(parts), axis=1) for parts in zip(*per_head))

            dq = dqd * e_cum
            dk = dki * e_inv + dke * e_end
            dke_ke = dke * ke
            dcum = dqd * qd - dki * ki - dke_ke
            dlf = _mm_select(sel_t, jnp.concatenate([dcum, dke_ke], axis=0)) + dlast
            df = dlf / f - dk
            dhq_ref[rows, :] = dq * (sq * (1.0 + hq * (1.0 - sq)))
            dhf_ref[rows, :] = df * (1.0 - lb) * (sg * (1.0 - sg))
            dhi_ref[rows, :] = dv
            dlb_acc[...] += jnp.sum(df * (1.0 - sg), axis=0, keepdims=True)
            return tuple(dsts_out)

        dsts = lax.fori_loop(0, n_groups, group, tuple(dstate[h] for h in range(HGRN_PAIR)))
        for h in range(HGRN_PAIR):
            dstate[h] = dsts[h]
        g0 = dlb_acc[...] * lb * (1.0 - lb)
        dlb_ref[...] = jnp.concatenate([g0, -g0], axis=0)

    cols, pair, lb_spec, st_spec = _hgrn_specs(reverse=True)
    wide = jax.ShapeDtypeStruct((SEQ, HGRN_WIDTH), F32)
    return pl.pallas_call(
        body, name="hgrn_bwd", grid=(HGRN_HEADS // HGRN_PAIR, SEQ // HGRN_SEQ_BLOCK),
        out_shape=(wide, wide, wide, jax.ShapeDtypeStruct((2, HGRN_WIDTH), F32)),
        in_specs=[cols(4), cols(5), cols(6), lb_spec, pair, st_spec],
        out_specs=(pair, pair, pair, lb_spec),
        scratch_shapes=[pltpu.VMEM((HGRN_PAIR, HGRN_DIM, HGRN_DIM), F32),
                        pltpu.VMEM((1, HGRN_PAIR * HGRN_DIM), F32)],
        compiler_params=_params(("parallel", "arbitrary")),
    )(proj, proj, proj, lb_raw, d_rec, states)


def _group_sum(v, group):
    parts = []
    for s in range(v.shape[1] // LANES):
        slab = v[:, LANES * s:LANES * (s + 1)]
        if group == LANES:
            parts.append(jnp.broadcast_to(jnp.sum(slab, axis=-1, keepdims=True), slab.shape))
        else:
            h0 = lax.broadcasted_iota(jnp.int32, slab.shape, 1) < HEAD_DIM
            s0 = jnp.sum(jnp.where(h0, slab, 0.0), axis=-1, keepdims=True)
            s1 = jnp.sum(jnp.where(h0, 0.0, slab), axis=-1, keepdims=True)
            parts.append(jnp.where(h0, s0, s1))
    return jnp.concatenate(parts, axis=1)


def _mid(attn_o, rec, proj, x, target, w_out_g, attn_w, hgrn_w, final_w):
    tm = 256

    def branch_fwd(o, gate, w, group):
        r = lax.rsqrt(_group_sum(o * o, group) * (1.0 / group) + NORM_EPS)
        nrm = o * r
        sg = _sigmoid(gate)
        return r, nrm, sg, nrm * w * (gate * sg)

    def branch_bwd(dy, r, nrm, sg, gate, w, group):
        silu = gate * sg
        d_gate = dy * nrm * w * (sg * (1.0 + gate * (1.0 - sg)))
        d_w = jnp.sum(dy * nrm * silu, axis=0, keepdims=True)
        dn = dy * w * silu
        d_o = r * (dn - nrm * (_group_sum(dn * nrm, group) * (1.0 / group)))
        return d_o, d_gate, d_w

    def body(o_ref, rec_ref, ag_ref, hg_ref, x_ref, tgt_ref, wout_ref, aw_ref, hw_ref, fw_ref,
             dx2_ref, do_ref, delta_ref, dag_ref, drec_ref, dhg_ref, dwout_ref, dfw_ref, daw_ref, dhw_ref,
             loss_ref, dwout_acc):
        i = pl.program_id(0)

        @pl.when(i == 0)
        def _():
            dwout_acc[...] = jnp.zeros_like(dwout_acc)
            dfw_ref[...] = jnp.zeros_like(dfw_ref)
            daw_ref[...] = jnp.zeros_like(daw_ref)
            dhw_ref[...] = jnp.zeros_like(dhw_ref)
            loss_ref[...] = jnp.zeros_like(loss_ref)

        o, rc, ag, hg = o_ref[...], rec_ref[...], ag_ref[...], hg_ref[...]
        aw, hw, fw = aw_ref[...], hw_ref[...], fw_ref[...]
        ra, na, sga, ya = branch_fwd(o, ag, aw, HEAD_DIM)
        rh, nh, sgh, yh = branch_fwd(rc, hg, hw, HGRN_DIM)
        mixed = jnp.concatenate([ya, yh], axis=1).astype(BF16)
        wout = wout_ref[...]
        x2 = x_ref[...] + _mm(mixed, wout)
        rstd = lax.rsqrt(jnp.mean(x2 * x2, axis=-1, keepdims=True) + NORM_EPS)
        xn = x2 * rstd
        err = xn * fw - tgt_ref[...]
        row_loss = jnp.mean(err * err, axis=-1, keepdims=True)
        loss_ref[...] += 0.5 * jnp.sum(row_loss, axis=0, keepdims=True)
        dy = err * (1.0 / D_MODEL)
        dfw_ref[...] += jnp.sum(dy * xn, axis=0, keepdims=True)
        dxn = dy * fw
        dx2 = rstd * (dxn - xn * jnp.mean(dxn * xn, axis=-1, keepdims=True))
        dx2_ref[...] = dx2
        dx2b = dx2.astype(BF16)
        dwout_acc[...] += _mm_tn(mixed, dx2b)

        @pl.when(i == pl.num_programs(0) - 1)
        def _():
            dwout_ref[...] = dwout_acc[...].astype(BF16)

        dmixed = _mm_nt(dx2b, wout)

        d_o, d_ag, d_aw = branch_bwd(dmixed[:, :ATTN_WIDTH], ra, na, sga, ag, aw, HEAD_DIM)
        d_rec, d_hg, d_hw = branch_bwd(dmixed[:, ATTN_WIDTH:], rh, nh, sgh, hg, hw, HGRN_DIM)
        do_ref[...] = d_o
        delta_ref[...] = _group_sum(d_o * o, HEAD_DIM)
        dag_ref[...] = d_ag
        drec_ref[...] = d_rec
        dhg_ref[...] = d_hg
        daw_ref[...] += d_aw
        dhw_ref[...] += d_hw

    half = lambda: pl.BlockSpec((tm, COL_BLOCK), lambda i: (i, 0))
    full = lambda: pl.BlockSpec((tm, D_MODEL), lambda i: (i, 0))
    fixed = lambda r, c: pl.BlockSpec((r, c), lambda i: (0, 0))
    wide = jax.ShapeDtypeStruct((SEQ, COL_BLOCK), F32)
    return pl.pallas_call(
        body, name="mid", grid=(SEQ // tm,),
        out_shape=(jax.ShapeDtypeStruct((SEQ, D_MODEL), F32), wide, wide, wide, wide, wide,
                   jax.ShapeDtypeStruct((D_MODEL, D_MODEL), BF16),
                   jax.ShapeDtypeStruct((1, D_MODEL), F32), jax.ShapeDtypeStruct((1, COL_BLOCK), F32),
                   jax.ShapeDtypeStruct((1, COL_BLOCK), F32), jax.ShapeDtypeStruct((1, 1), F32)),
        scratch_shapes=[pltpu.VMEM((D_MODEL, D_MODEL), F32)],
        in_specs=[half(), half(),
                  pl.BlockSpec((tm, COL_BLOCK), lambda i: (i, 3)), pl.BlockSpec((tm, COL_BLOCK), lambda i: (i, 7)),
                  full(), full(), fixed(D_MODEL, D_MODEL), fixed(1, COL_BLOCK), fixed(1, COL_BLOCK),
                  fixed(1, D_MODEL)],
        out_specs=(full(), half(), half(), half(), half(), half(), fixed(D_MODEL, D_MODEL),
                   fixed(1, D_MODEL), fixed(1, COL_BLOCK), fixed(1, COL_BLOCK), fixed(1, 1)),
        compiler_params=_params(("arbitrary",)),
    )(attn_o, rec, proj, proj, x, target, w_out_g, attn_w, hgrn_w, final_w)


def _in_proj_bwd_rows(d_groups, w_g, x, dx2, mix_w, rc, rsa, rsb):
    tm = 256

    def body(*refs):
        dg_refs = refs[:N_DEV]
        wg_ref, x_ref, dx2_ref, w_ref, c_ref, sa_ref, sb_ref, gx_ref, dpb_ref, dmw_ref = refs[N_DEV:]

        @pl.when(pl.program_id(0) == 0)
        def _():
            dmw_ref[...] = jnp.zeros_like(dmw_ref)

        parts = []
        for j in range(N_DEV):
            dp = dg_refs[j][...]
            if j < 2:
                dp = _rot_transposed(dp, c_ref[...], sa_ref[...], sb_ref[...])
            parts.append(dp.astype(BF16))
        dpb = jnp.concatenate(parts, axis=1)
        dpb_ref[...] = dpb
        g = _mm_nt(dpb, wg_ref[...])
        xf = x_ref[...]
        rstd = lax.rsqrt(jnp.mean(xf * xf, axis=-1, keepdims=True) + NORM_EPS)
        xn = xf * rstd
        dmw_ref[...] += jnp.sum(g * xn, axis=0, keepdims=True)
        gw = g * w_ref[...]
        gx_ref[...] = dx2_ref[...] + rstd * (gw - xn * jnp.mean(gw * xn, axis=-1, keepdims=True))

    tile = lambda cols: pl.BlockSpec((tm, cols), lambda i: (i, 0))
    fixed = lambda r, c: pl.BlockSpec((r, c), lambda i: (0, 0))
    return pl.pallas_call(
        body, name="in_proj_bwd_rows", grid=(SEQ // tm,),
        out_shape=(jax.ShapeDtypeStruct((SEQ, D_MODEL), F32), jax.ShapeDtypeStruct((SEQ, IN_COLS), BF16),
                   jax.ShapeDtypeStruct((1, D_MODEL), F32)),
        in_specs=[tile(COL_BLOCK) for _ in range(N_DEV)] + [
            pl.BlockSpec((D_MODEL, IN_COLS), lambda i: (0, 0), pipeline_mode=pl.Buffered(1)),
            tile(D_MODEL), tile(D_MODEL), fixed(1, D_MODEL), tile(LANES), tile(LANES), tile(LANES)],
        out_specs=(tile(D_MODEL), tile(IN_COLS), fixed(1, D_MODEL)),
        compiler_params=_params(("arbitrary",)),
    )(*d_groups, w_g, x, dx2, mix_w, rc, rsa, rsb)


def _weights_exchange(hn_t, dproj_b, dwout_p, small_p):
    n_chips = N_DEV // 2
    rb = 128
    S1_IN, S1_OUT, SMALL, S2_IN, S2_OUT = 0, 4, 8, 15, 18
    rel_of_pair = (1, 2, 3, 0)

    def body(order_ref, hnt_ref, dp_ref, dwout_ref, small_ref, gin_ref, gout_ref, gs_ref,
             part, s1_send, s1_in, s1_out, fwd_in, fwd_out, s2_in, s2_out, land_s, send_sems, recv_sems):
        t = pl.program_id(0)
        me = _my_place()
        x, y, c = me
        my_chip = 2 * x + y
        sibling = (x, y, 1 - c)

        def remote(slot, src, dst, to):
            return pltpu.make_async_remote_copy(src_ref=src, dst_ref=dst, send_sem=send_sems.at[slot],
                                                recv_sem=recv_sems.at[slot], device_id=to, device_id_type=MESH)

        def s1_in_copy(pair):
            return remote(S1_IN + pair, s1_send.at[pair], s1_in.at[pair], sibling)

        def s1_out_copy(pair):
            q = my_chip ^ rel_of_pair[pair]
            return remote(S1_OUT + pair, dwout_ref.at[q, 1 - c], s1_out.at[pair], sibling)

        def s2_copies(rel):
            peer = _peer(me, 2 * rel)
            return [remote(S2_IN + rel - 1, fwd_in.at[rel - 1], s2_in.at[rel - 1], peer),
                    remote(S2_OUT + rel - 1, fwd_out.at[rel - 1], s2_out.at[rel - 1], peer)]

        def small_copy(rel):
            return remote(SMALL + rel - 1, small_ref, land_s.at[rel], _peer(me, rel))

        @pl.when(t == 0)
        def _():
            land_s[0] = small_ref[...]
            for pair in range(n_chips):
                s1_out_copy(pair).start()
            for rel in range(1, N_DEV):
                small_copy(rel).start()

        part[...] = _mm(hnt_ref[...], dp_ref[...])

        def rows_loop(n_rows, fn):
            def step(b, carry):
                fn(pl.ds(pl.multiple_of(b * rb, rb), rb))
                return carry
            lax.fori_loop(0, n_rows // rb, step, 0)

        for pair, rel in enumerate(rel_of_pair):
            @pl.when(t == 2 * pair)
            def _(pair=pair):
                s1_send[pair] = part[...].astype(BF16)
                s1_in_copy(pair).start()

            @pl.when(t == 2 * pair + 1)
            def _(pair=pair, rel=rel):
                q = my_chip ^ rel
                s1_in_copy(pair).wait_recv()
                s1_out_copy(pair).wait_recv()
                dst_in = fwd_in.at[rel - 1] if rel else gin_ref
                dst_out = fwd_out.at[rel - 1] if rel else gout_ref

                def add_in(rows):
                    dst_in[rows, :] = (part[rows, :] + s1_in[pair, rows, :].astype(F32)).astype(dst_in.dtype)

                def add_out(rows):
                    dst_out[rows, :] = (dwout_ref[q, c, rows, :].astype(F32)
                                        + s1_out[pair, rows, :].astype(F32)).astype(dst_out.dtype)

                rows_loop(D_MODEL, add_in)
                rows_loop(WOUT_ROWS, add_out)
                if rel:
                    for cp in s2_copies(rel):
                        cp.start()

        @pl.when(t == N_DEV - 1)
        def _():
            for rel in range(1, n_chips):
                for cp in s2_copies(rel):
                    cp.wait_recv()

            def total_in(rows):
                g = gin_ref[rows, :]
                for rel in range(1, n_chips):
                    g = g + s2_in[rel - 1, rows, :].astype(F32)
                gin_ref[rows, :] = g

            def total_out(rows):
                g = gout_ref[rows, :]
                for rel in range(1, n_chips):
                    g = g + s2_out[rel - 1, rows, :].astype(F32)
                gout_ref[rows, :] = g

            rows_loop(D_MODEL, total_in)
            rows_loop(WOUT_ROWS, total_out)

            for rel in range(1, N_DEV):
                small_copy(rel).wait_recv()
            my_flat = _flat(me)
            g = land_s[my_flat ^ 0]
            for dev in range(1, N_DEV):
                g = g + land_s[my_flat ^ dev]
            gs_ref[...] = g

            for pair in range(n_chips):
                s1_in_copy(pair).wait_send()
                s1_out_copy(pair).wait_send()
            for rel in range(1, n_chips):
                for cp in s2_copies(rel):
                    cp.wait_send()
            for rel in range(1, N_DEV):
                small_copy(rel).wait_send()

    place_x, place_y, place_c = _my_place()
    my_chip = 2 * place_x + place_y
    order = jnp.stack([2 * (my_chip ^ rel) + core for rel in rel_of_pair
                       for core in (1 - place_c, place_c)]).astype(jnp.int32)

    whole = lambda: pl.BlockSpec(memory_space=pltpu.VMEM)
    in_blocks = lambda n: pltpu.VMEM((n, D_MODEL, COL_BLOCK), BF16)
    out_blocks = lambda n: pltpu.VMEM((n, WOUT_ROWS, D_MODEL), BF16)
    grid_spec = pltpu.PrefetchScalarGridSpec(
        num_scalar_prefetch=1, grid=(N_DEV,),
        in_specs=[pl.BlockSpec((D_MODEL, SEQ), lambda t, order: (0, 0), pipeline_mode=pl.Buffered(1)),
                  pl.BlockSpec((SEQ, COL_BLOCK), lambda t, order: (0, order[t])), whole(), whole()],
        out_specs=(whole(), whole(), whole()),
        scratch_shapes=[pltpu.VMEM((D_MODEL, COL_BLOCK), F32), in_blocks(n_chips), in_blocks(n_chips),
                        out_blocks(n_chips), in_blocks(n_chips - 1), out_blocks(n_chips - 1),
                        in_blocks(n_chips - 1), out_blocks(n_chips - 1),
                        pltpu.VMEM((N_DEV, SMALL_ROWS, LANES), F32),
                        pltpu.SemaphoreType.DMA((21,)), pltpu.SemaphoreType.DMA((21,))])
    return pl.pallas_call(
        body, name="weights_exchange", grid_spec=grid_spec,
        out_shape=(jax.ShapeDtypeStruct((D_MODEL, COL_BLOCK), F32), jax.ShapeDtypeStruct((WOUT_ROWS, D_MODEL), F32),
                   jax.ShapeDtypeStruct((SMALL_ROWS, LANES), F32)),
        compiler_params=_params(("arbitrary",)),
    )(order, hn_t, dproj_b, dwout_p.reshape(n_chips, 2, WOUT_ROWS, D_MODEL), small_p)


def _adamw(w, g, m, v):
    m = ADAM_B1 * m + (1.0 - ADAM_B1) * g
    v = ADAM_B2 * v + (1.0 - ADAM_B2) * (g * g)
    m_hat = m / (1.0 - ADAM_B1 ** ADAM_STEP)
    v_hat = v / (1.0 - ADAM_B2 ** ADAM_STEP)
    delta = -ADAM_LR * (m_hat / (jnp.sqrt(v_hat) + ADAM_EPS) + ADAM_WD * w)
    return delta, m, v


def _adamw_update(grads, weights, m_old, v_old):
    rb = 256

    def body(*refs):
        g_refs, w_refs, m_refs, v_refs = refs[0:3], refs[3:6], refs[6:9], refs[9:12]
        d_refs, nm_refs, nv_refs = refs[12:15], refs[15:18], refs[18:21]
        for k in range(3):
            n_rows = g_refs[k].shape[0]
            step_rows = min(rb, n_rows)

            def step(b, carry, k=k, step_rows=step_rows):
                rows = pl.ds(pl.multiple_of(b * step_rows, 8), step_rows)
                delta, nm, nv = _adamw(w_refs[k][rows, :], g_refs[k][rows, :], m_refs[k][rows, :], v_refs[k][rows, :])
                d_refs[k][rows, :] = delta
                nm_refs[k][rows, :] = nm
                nv_refs[k][rows, :] = nv
                return carry

            lax.fori_loop(0, n_rows // step_rows, step, 0)

    shapes = tuple(jax.ShapeDtypeStruct(g.shape, F32) for g in grads)
    vm = lambda: pl.BlockSpec(memory_space=pltpu.VMEM)
    outs = pl.pallas_call(
        body, name="adamw_update", out_shape=shapes * 3,
        in_specs=[vm() for _ in range(12)], out_specs=tuple(vm() for _ in range(9)),
        compiler_params=_params(),
    )(*grads, *weights, *m_old, *v_old)
    return outs[0:3], outs[3:6], outs[6:9]


def _pack_small(mix, attn, hgrn, lb, final, loss=None):
    def rows8(a):
        a = a.reshape(-1, LANES)
        return jnp.pad(a, ((0, 8 - a.shape[0]), (0, 0)))
    last = jnp.zeros((8, LANES), F32) if loss is None else jnp.pad(loss.reshape(1, 1), ((0, 7), (0, LANES - 1)))
    return jnp.concatenate([rows8(mix), rows8(attn), rows8(hgrn), rows8(lb), rows8(final), last], axis=0)


def _unpack_small(slab):
    return (slab[ROW_MIX:ROW_MIX + 8].reshape(1, D_MODEL), slab[ROW_ATTN:ROW_ATTN + 4].reshape(1, ATTN_WIDTH),
            slab[ROW_HGRN:ROW_HGRN + 4].reshape(1, HGRN_WIDTH), slab[ROW_LB:ROW_LB + 8].reshape(2, HGRN_WIDTH),
            slab[ROW_FINAL:ROW_FINAL + 8].reshape(D_MODEL))


def _rope(pos_col):
    lane_e = np.arange(LANES) % HEAD_DIM
    inv = ROPE_THETA ** (-(lane_e % ROPE_HALF) * (2.0 / ROPE_DIMS))
    inv_lanes = np.where(lane_e < ROPE_DIMS, inv, 0.0).astype(np.float32).reshape(1, LANES)
    return _rope_tables(pos_col, jnp.asarray(inv_lanes))


def _local_step(x, proj, qkv_sorted, w_in_g, w_out_g, tables, mix_w, attn_w, hgrn_w, lb_raw, final_w, target):
    rc, rsa, rsb = tables
    attn_o, lse = _attn_fwd_fused(qkv_sorted)
    rec, states = _hgrn_fwd(proj, lb_raw)

    (dx2, d_o, delta, d_ag, d_rec, d_hg, dwout_p, d_final, d_attn_w, d_hgrn_w, loss) = _mid(
        attn_o, rec, proj, x, target, w_out_g, attn_w, hgrn_w, final_w.reshape(1, D_MODEL))

    dqkv = _attn_bwd_fused(qkv_sorted, d_o, lse, delta)
    d_hq, d_hf, d_hi, d_lb = _hgrn_bwd(proj, lb_raw, d_rec, states)

    grad_x, dproj_b, d_mix = _in_proj_bwd_rows(
        (dqkv[0], dqkv[1], dqkv[2], d_ag, d_hq, d_hf, d_hi, d_hg), w_in_g, x, dx2, mix_w, rc, rsa, rsb)
    small_p = _pack_small(d_mix, d_attn_w, d_hgrn_w, d_lb, d_final, loss)
    return grad_x, dproj_b, dwout_p, small_p


def kernel(x, positions, w_in, w_out, mix_norm_w, attn_out_norm_w, hgrn_out_norm_w, hgrn_lb_raw, final_norm_w, loss_target, m_w_in, m_w_out, m_mix_norm_w, m_attn_out_norm_w, m_hgrn_out_norm_w, m_hgrn_lb_raw, m_final_norm_w, v_w_in, v_w_out, v_mix_norm_w, v_attn_out_norm_w, v_hgrn_out_norm_w, v_hgrn_lb_raw, v_final_norm_w):
    tables = _rope(positions.reshape(SEQ, 1))
    proj, hn_t, w_in_g, w_out_g, qkv_sorted = _gather_project(x[0], mix_norm_w, w_in[0], w_out[0], *tables)
    grad_x, dproj_b, dwout_p, small_p = _local_step(
        x[0], proj, qkv_sorted, w_in_g, w_out_g, tables, mix_norm_w, attn_out_norm_w, hgrn_out_norm_w,
        hgrn_lb_raw, final_norm_w, loss_target[0])
    g_in, g_out, g_s = _weights_exchange(hn_t, dproj_b, dwout_p, small_p)

    w_s = _pack_small(mix_norm_w, attn_out_norm_w, hgrn_out_norm_w, hgrn_lb_raw, final_norm_w)
    m_s = _pack_small(m_mix_norm_w, m_attn_out_norm_w, m_hgrn_out_norm_w, m_hgrn_lb_raw, m_final_norm_w)
    v_s = _pack_small(v_mix_norm_w, v_attn_out_norm_w, v_hgrn_out_norm_w, v_hgrn_lb_raw, v_final_norm_w)
    (d_in, d_out, d_s), (nm_in, nm_out, nm_s), (nv_in, nv_out, nv_s) = _adamw_update(
        (g_in, g_out, g_s), (w_in[0], w_out[0], w_s), (m_w_in[0], m_w_out[0], m_s), (v_w_in[0], v_w_out[0], v_s))

    loss = g_s[ROW_LOSS, 0]
    return (loss, grad_x[None], g_in[None], g_out[None], *_unpack_small(g_s),
            d_in[None], d_out[None], *_unpack_small(d_s),
            nm_in[None], nm_out[None], *_unpack_small(nm_s),
            nv_in[None], nv_out[None], *_unpack_small(nv_s))
```

```python
import functools

import jax
import jax.numpy as jnp
import numpy as np
from jax import lax
from jax.experimental import pallas as pl
from jax.experimental.pallas import tpu as pltpu

F32 = jnp.float32
BF16 = jnp.bfloat16

SEQ = 4096
D_MODEL = 1024
ATTN_WIDTH = 512
HGRN_WIDTH = 512
HEAD_DIM = 64
HGRN_HEADS = 4
HGRN_DIM = 128
HGRN_CHUNK = 64
N_CHUNKS = SEQ // HGRN_CHUNK
IN_COLS = 4096
COL_BLOCK = 512
N_DEV = 8
WOUT_ROWS = D_MODEL // N_DEV
ATTN_BLOCK = 128
DILATIONS = (1, 4, 16)
ROPE_THETA = 500000.0
ROPE_DIMS = 16
ROPE_HALF = 8
NORM_EPS = 1e-6
NEG_BIG = -1e30
LANES = 128

ADAM_LR = 0.001
ADAM_B1 = 0.9
ADAM_B2 = 0.999
ADAM_EPS = 1e-08
ADAM_WD = 0.01
ADAM_STEP = 10

SMALL_ROWS = 48
ROW_MIX, ROW_ATTN, ROW_HGRN, ROW_LB, ROW_FINAL, ROW_LOSS = 0, 8, 16, 24, 32, 40

VMEM_LIMIT = 56 * 1024 * 1024
MESH = pl.DeviceIdType.MESH


def _mm(a, b):
    return lax.dot_general(a, b, (((1,), (0,)), ((), ())), preferred_element_type=F32)


def _mm_nt(a, b):
    return lax.dot_general(a, b, (((1,), (1,)), ((), ())), preferred_element_type=F32)


def _mm_tn(a, b):
    return lax.dot_general(a, b, (((0,), (0,)), ((), ())), preferred_element_type=F32)


def _mm_exact(a, b):
    return lax.dot_general(a, b, (((1,), (0,)), ((), ())), preferred_element_type=F32,
                           precision=lax.Precision.HIGHEST)


def _sigmoid(v):
    return 1.0 / (1.0 + jnp.exp(-v))


def _params(sem=None, **kw):
    return pltpu.CompilerParams(dimension_semantics=sem, vmem_limit_bytes=VMEM_LIMIT, **kw)


def _my_place():
    return lax.axis_index("x"), lax.axis_index("y"), lax.axis_index("c")


def _peer(place, rel):
    x, y, c = place
    return (x ^ ((rel >> 2) & 1), y ^ ((rel >> 1) & 1), c ^ (rel & 1))


def _flat(place):
    x, y, c = place
    return 4 * x + 2 * y + c


ROPE_ROWS = 16


def _rope_tables(pos_row, inv_freq_col, selectors):
    def body(pos_ref, invf_ref, sel_ref, c_ref, sa_ref, sb_ref):
        ang = pos_ref[...].astype(F32) * invf_ref[...]
        cos, sin = jnp.cos(ang), jnp.sin(ang)

        def spread(v, sel):
            hi = v.astype(BF16)
            r1 = v - hi.astype(F32)
            mid = r1.astype(BF16)
            lo = (r1 - mid.astype(F32)).astype(BF16)
            return _mm_tn(hi, sel) + _mm_tn(mid, sel) + _mm_tn(lo, sel)

        e = lax.broadcasted_iota(jnp.int32, (1, LANES), 1) & (HEAD_DIM - 1)
        c_ref[...] = spread(cos, sel_ref[0]) + jnp.where(e < ROPE_DIMS, 0.0, 1.0)
        sa_ref[...] = spread(sin, sel_ref[1])
        sb_ref[...] = spread(sin, sel_ref[2])

    tab = jax.ShapeDtypeStruct((SEQ, LANES), F32)
    vm = lambda: pl.BlockSpec(memory_space=pltpu.VMEM)
    return pl.pallas_call(
        body, name="rope_tables", out_shape=(tab, tab, tab),
        in_specs=[vm(), vm(), vm()], out_specs=(vm(), vm(), vm()), compiler_params=_params(),
    )(pos_row, inv_freq_col, selectors)


def _per_slab(fn, t):
    return jnp.concatenate([fn(t[:, LANES * s:LANES * (s + 1)]) for s in range(t.shape[1] // LANES)], axis=1)


def _rot(t, c, sa, sb):
    return _per_slab(lambda u: u * c + pltpu.roll(u, ROPE_HALF, 1) * sa + pltpu.roll(u, LANES - ROPE_HALF, 1) * sb, t)


def _rot_transposed(g, c, sa, sb):
    return _per_slab(
        lambda u: u * c + pltpu.roll(u * sa, LANES - ROPE_HALF, 1) + pltpu.roll(u * sb, ROPE_HALF, 1), g)


def _gather_project(x, mix_w, w_in, w_out, rc, rsa, rsb):
    tm = 1024
    n_tiles = SEQ // tm
    arrival_of_step = (None, 0, 1, 2, 4, 5, 3, 6)

    def body(order_ref, x_ref, w_ref, win_ref, wout_ref, c_ref, sa_ref, sb_ref,
             proj_ref, hnt_ref, gin_hbm, gout_hbm, qkv_hbm,
             hn_s, w_land, wout_land, stage, sort_stage, slab_tmp, send_sems, recv_sems, local_sems):
        g, i = pl.program_id(0), pl.program_id(1)
        me = _my_place()
        x_, y_, c_ = me
        sibling = (x_, y_, 1 - c_)
        chips = [(1 - x_, y_), (x_, 1 - y_), (1 - x_, 1 - y_)]

        def slab(which, place):
            idx = _flat(place)
            if which == 0:
                return w_land.at[idx]
            return wout_land.at[pl.ds(pl.multiple_of(idx * WOUT_ROWS, WOUT_ROWS), WOUT_ROWS), :]

        def copy(which, k, block, to, src=None):
            ref = slab(which, block)
            return pltpu.make_async_remote_copy(
                src_ref=ref if src is None else src, dst_ref=ref, send_sem=send_sems.at[7 * which + k],
                recv_sem=recv_sems.at[7 * which + k], device_id=to, device_id_type=MESH)

        def first_copies(which):
            src = stage if which == 0 else None
            return ([copy(which, 0, me, sibling, src)]
                    + [copy(which, 1 + j, me, (*chip, c_), src) for j, chip in enumerate(chips)])

        def pass_on(which, j):
            return copy(which, 4 + j, (*chips[j], c_), sibling)

        def arrival(which, k):
            if k == 0:
                return copy(which, 0, sibling, me)
            if k <= 3:
                return copy(which, k, (*chips[k - 1], c_), me)
            return copy(which, k, (*chips[k - 4], 1 - c_), me)

        def to_hbm(step):
            idx = order_ref[step]
            cols = pl.ds(pl.multiple_of(idx * COL_BLOCK, COL_BLOCK), COL_BLOCK)
            return pltpu.make_async_copy(w_land.at[idx], gin_hbm.at[:, cols], local_sems.at[step])

        @pl.when((g == 0) & (i == 0))
        def _():
            stage[...] = win_ref[...].astype(BF16)
            w_land[_flat(me)] = stage[...]
            wout_land[pl.ds(pl.multiple_of(_flat(me) * WOUT_ROWS, WOUT_ROWS), WOUT_ROWS), :] = (
                wout_ref[...].astype(BF16))
            for cp in first_copies(0) + first_copies(1)[:1]:
                cp.start()
            to_hbm(0).start()

        for step, k in enumerate(arrival_of_step):
            if k is None:
                continue

            @pl.when((g == step) & (i == 0))
            def _(k=k, step=step):
                arrival(0, k).wait_recv()
                to_hbm(step).start()
                if k == 1:
                    for cp in first_copies(1)[1:]:
                        cp.start()
                if 1 <= k <= 3:
                    pass_on(0, k - 1).start()

        rows = pl.ds(pl.multiple_of(i * tm, tm), tm)

        @pl.when(g == 0)
        def _():
            xf = x_ref[...]
            ms = jnp.mean(xf * xf, axis=-1, keepdims=True)
            hn = xf * lax.rsqrt(ms + NORM_EPS) * w_ref[...]
            hnt_ref[...] = hn.T.astype(BF16)
            hn_s[rows, :] = hn.astype(BF16)

        group = order_ref[g]

        def sorted_copy():
            per = tm // SORT_RESIDUES
            for s in range(COL_BLOCK // LANES):
                slab_tmp[s] = proj_ref[:, LANES * s:LANES * (s + 1)]
            for r in range(SORT_RESIDUES):
                for s in range(COL_BLOCK // LANES):
                    sort_stage[r, :, LANES * s:LANES * (s + 1)] = (
                        slab_tmp.at[s][pl.ds(r, per, stride=SORT_RESIDUES), :])
            cols = pl.ds(pl.multiple_of(group * COL_BLOCK, COL_BLOCK), COL_BLOCK)
            cp = pltpu.make_async_copy(
                sort_stage, qkv_hbm.at[:, pl.ds(pl.multiple_of(i * per, per), per), cols], local_sems.at[N_DEV + 1])
            cp.start()
            cp.wait()

        @pl.when(group < 2)
        def _():
            proj_ref[...] = _rot(_mm(hn_s[rows, :], w_land[group]), c_ref[...], sa_ref[...], sb_ref[...])
            sorted_copy()

        @pl.when(group == 2)
        def _():
            proj_ref[...] = _mm(hn_s[rows, :], w_land[group])
            sorted_copy()

        @pl.when(group > 2)
        def _():
            proj_ref[...] = _mm(hn_s[rows, :], w_land[group])

        @pl.when((g == N_DEV - 1) & (i == n_tiles - 1))
        def _():
            for j in range(3):
                arrival(1, 1 + j).wait_recv()
                pass_on(1, j).start()
            for k in (0, 4, 5, 6):
                arrival(1, k).wait_recv()
            for which in (0, 1):
                for cp in first_copies(which) + [pass_on(which, j) for j in range(3)]:
                    cp.wait_send()
            wout_copy = pltpu.make_async_copy(wout_land, gout_hbm, local_sems.at[N_DEV])
            wout_copy.start()
            for step in range(N_DEV):
                to_hbm(step).wait()
            wout_copy.wait()

    me = _my_place()
    x_, y_, c_ = me
    chips = [(1 - x_, y_), (x_, 1 - y_), (1 - x_, 1 - y_)]
    order = jnp.stack([_flat(p) for p in (
        me, (x_, y_, 1 - c_), (*chips[0], c_), (*chips[1], c_), (*chips[0], 1 - c_), (*chips[1], 1 - c_),
        (*chips[2], c_), (*chips[2], 1 - c_))]).astype(jnp.int32)

    first_sweep = lambda g, i, order: (jnp.where(g == 0, i, n_tiles - 1), 0)
    tab = pl.BlockSpec((tm, LANES), lambda g, i, order: (jnp.where(order[g] < 2, i, 0), 0))
    whole = lambda: pl.BlockSpec(memory_space=pltpu.VMEM)
    grid_spec = pltpu.PrefetchScalarGridSpec(
        num_scalar_prefetch=1, grid=(N_DEV, n_tiles),
        in_specs=[pl.BlockSpec((tm, D_MODEL), first_sweep),
                  pl.BlockSpec((1, D_MODEL), lambda g, i, order: (0, 0)),
                  whole(), whole(), tab, tab, tab],
        out_specs=(pl.BlockSpec((tm, COL_BLOCK), lambda g, i, order: (i, order[g])),
                   pl.BlockSpec((D_MODEL, tm), lambda g, i, order: (0, jnp.where(g == 0, i, n_tiles - 1))),
                   pl.BlockSpec(memory_space=pl.ANY), pl.BlockSpec(memory_space=pl.ANY),
                   pl.BlockSpec(memory_space=pl.ANY)),
        scratch_shapes=[pltpu.VMEM((SEQ, D_MODEL), BF16),
                        pltpu.VMEM((N_DEV, D_MODEL, COL_BLOCK), BF16),
                        pltpu.VMEM((D_MODEL, D_MODEL), BF16),
                        pltpu.VMEM((D_MODEL, COL_BLOCK), BF16),
                        pltpu.VMEM((SORT_RESIDUES, tm // SORT_RESIDUES, COL_BLOCK), F32),
                        pltpu.VMEM((COL_BLOCK // LANES, tm, LANES), F32),
                        pltpu.SemaphoreType.DMA((14,)), pltpu.SemaphoreType.DMA((14,)),
                        pltpu.SemaphoreType.DMA((N_DEV + 2,))])
    proj, hn_t, w_in_g, w_out_g, qkv_sorted = pl.pallas_call(
        body, name="gather_project", grid_spec=grid_spec,
        out_shape=(jax.ShapeDtypeStruct((SEQ, IN_COLS), F32), jax.ShapeDtypeStruct((D_MODEL, SEQ), BF16),
                   jax.ShapeDtypeStruct((D_MODEL, IN_COLS), BF16), jax.ShapeDtypeStruct((D_MODEL, D_MODEL), BF16),
                   jax.ShapeDtypeStruct((SORT_RESIDUES, SORT_ROWS, 3 * COL_BLOCK), F32)),
        compiler_params=_params(("arbitrary", "arbitrary")),
    )(order, x, mix_w, w_in, w_out, rc, rsa, rsb)
    return proj, hn_t, w_in_g, w_out_g, qkv_sorted.reshape(SEQ, 3 * COL_BLOCK)


SCORE_SCALE = HEAD_DIM ** -0.5
ATTN_GROUP_FWD = 16
ATTN_GROUP_BWD = 8
BLOCKS_PER_PATTERN = SEQ // ATTN_BLOCK
SORT_RESIDUES = 16
SORT_ROWS = SEQ // SORT_RESIDUES


def _write_band_bias(bias_ref):
    row = lax.broadcasted_iota(jnp.int32, (2 * ATTN_BLOCK, 2 * ATTN_BLOCK), 0) & (ATTN_BLOCK - 1)
    col = lax.broadcasted_iota(jnp.int32, (2 * ATTN_BLOCK, 2 * ATTN_BLOCK), 1)
    for pi, d in enumerate(DILATIONS):
        per = SORT_RESIDUES // d
        ahead = per * (row % (8 * d) - col % (16 * d)) + (row // (8 * d) - col // (16 * d))
        dist = ATTN_BLOCK + ahead
        bias_ref[2 * pi] = jnp.where((dist >= 0) & (dist <= ATTN_BLOCK), 0.0, NEG_BIG)
        bias_ref[2 * pi + 1] = jnp.where(ahead >= 0, 0.0, NEG_BIG)


def _head0_lanes():
    return lax.broadcasted_iota(jnp.int32, (ATTN_BLOCK, LANES), 1) < HEAD_DIM


def _stack_heads(t, h0):
    return jnp.concatenate([jnp.where(h0, t, 0.0), jnp.where(h0, 0.0, t)], axis=0).astype(BF16)


def _block_runs(i, d):
    nblk = BLOCKS_PER_PATTERN // d
    r, n = i // nblk, i % nblk
    kn = jnp.maximum(n - 1, 0)
    rows, keys = [], []
    for c in range(SORT_RESIDUES // d):
        base = SORT_ROWS * (c * d + r)
        rows.append(pl.ds(pl.multiple_of(base + 8 * d * n, 8), 8 * d))
        keys.append(pl.ds(pl.multiple_of(base + 8 * d * kn, 8), 16 * d))
    return rows, keys, (n == 0).astype(jnp.int32)


def _take(ref, runs):
    return jnp.concatenate([ref[run, :] for run in runs], axis=0)


def _put(ref, runs, value, add=False):
    at = 0
    for run in runs:
        piece = value[at:at + run.size]
        if add:
            ref[run, :] += piece
        else:
            ref[run, :] = piece
        at += run.size


def _sort_rows(src_ref, dst_ref):
    for r in range(SORT_RESIDUES):
        dst_ref[SORT_ROWS * r:SORT_ROWS * (r + 1), :] = src_ref[pl.ds(r, SORT_ROWS, stride=SORT_RESIDUES), :]


def _unsort_rows(src_ref, dst_ref):
    for r in range(SORT_RESIDUES):
        dst_ref[pl.ds(r, SORT_ROWS, stride=SORT_RESIDUES), :] = src_ref[SORT_ROWS * r:SORT_ROWS * (r + 1), :]


def _for_each_group(d, n_group, load, compute, store):
    def group(g, carry):
        items = [load(*_block_runs(g * n_group + u, d)) for u in range(n_group)]
        results = [compute(item) for item in items]
        for item, res in zip(items, results):
            store(item, res)
        return carry

    lax.fori_loop(0, BLOCKS_PER_PATTERN // n_group, group, 0)


def _attn_fwd_fused(qkv_sorted):
    n_pat = len(DILATIONS)
    tile2 = (2 * ATTN_BLOCK, LANES)

    def body(q_ref, k_ref, v_ref, o_ref, lse_ref, o_acc, m_acc, l_acc, bias_ref):
        pl.when(pl.program_id(0) == 0)(lambda: _write_band_bias(bias_ref))
        h0 = _head0_lanes()
        for pi, d in enumerate(DILATIONS):
            first, last = pi == 0, pi == n_pat - 1

            def load(rows, keys, which, first=first, pi=pi):
                item = dict(rows=rows, keys=keys, which=2 * pi + which)
                if not first:
                    item.update(o=_take(o_acc, rows), m=[_take(m_acc.at[h], rows) for h in range(2)],
                                l=[_take(l_acc.at[h], rows) for h in range(2)])
                return item

            def compute(item, first=first):
                kb = _take(k_ref, item["keys"]).astype(BF16)
                vb = _take(v_ref, item["keys"]).astype(BF16)
                s = _mm_nt(_stack_heads(_take(q_ref, item["rows"]) * SCORE_SCALE, h0), kb) + bias_ref[item["which"]]
                mb = jnp.max(s, axis=-1, keepdims=True)
                if first:
                    p = jnp.exp(s - mb)
                    mn = jnp.broadcast_to(mb, tile2)
                else:
                    m_old = jnp.concatenate(item["m"], axis=0)
                    mn = jnp.maximum(m_old, mb)
                    alpha = jnp.exp(m_old - mn)
                    p = jnp.exp(s - jnp.concatenate([mn, mn], axis=1))
                ls = jnp.sum(p, axis=-1, keepdims=True)
                pv = _mm(p.astype(BF16), vb)
                if first:
                    return pv, mn, jnp.broadcast_to(ls, tile2)
                o_old = jnp.concatenate([item["o"], item["o"]], axis=0)
                return alpha * o_old + pv, mn, alpha * jnp.concatenate(item["l"], axis=0) + ls

            def store(item, res, last=last):
                rows = item["rows"]
                (o0, o1), (m0, m1), (l0, l1) = ((a[:ATTN_BLOCK], a[ATTN_BLOCK:]) for a in res)
                if last:
                    _put(o_acc, rows, jnp.where(h0, o0 / l0, o1 / l1))
                    _put(lse_ref, rows, jnp.where(h0, m0 + jnp.log(l0), m1 + jnp.log(l1)))
                else:
                    _put(o_acc, rows, jnp.where(h0, o0, o1))
                    for h, (m, l) in enumerate(((m0, l0), (m1, l1))):
                        _put(m_acc.at[h], rows, m)
                        _put(l_acc.at[h], rows, l)

            _for_each_group(d, ATTN_GROUP_FWD, load, compute, store)
        _unsort_rows(o_acc, o_ref)

    slab = lambda g: pl.BlockSpec((SEQ, LANES), functools.partial(lambda hp, g: (0, 4 * g + hp), g=g))
    wide = jax.ShapeDtypeStruct((SEQ, ATTN_WIDTH), F32)
    return pl.pallas_call(
        body, name="attn_fwd", grid=(4,), out_shape=(wide, wide),
        in_specs=[slab(0), slab(1), slab(2)], out_specs=(slab(0), slab(0)),
        scratch_shapes=[pltpu.VMEM((SEQ, LANES), F32), pltpu.VMEM((2, SEQ, LANES), F32),
                        pltpu.VMEM((2, SEQ, LANES), F32),
                        pltpu.VMEM((2 * len(DILATIONS), 2 * ATTN_BLOCK, 2 * ATTN_BLOCK), F32)],
        compiler_params=_params(("arbitrary",)),
    )(qkv_sorted, qkv_sorted, qkv_sorted)


def _attn_bwd_fused(qkv_sorted, d_out, lse_sorted, delta):
    def body(q_ref, k_ref, v_ref, do_ref, lse_ref, del_ref, dq_ref, dk_ref, dv_ref,
             do_s, del_s, dq_s, dk_s, dv_s, bias_ref):
        pl.when(pl.program_id(0) == 0)(lambda: _write_band_bias(bias_ref))
        _sort_rows(do_ref, do_s)
        _sort_rows(del_ref, del_s)
        dk_s[...] = jnp.zeros_like(dk_s)
        dv_s[...] = jnp.zeros_like(dv_s)
        h0 = _head0_lanes()
        for pi, d in enumerate(DILATIONS):
            first = pi == 0

            def load(rows, keys, which, pi=pi):
                return dict(rows=rows, keys=keys, q=_take(q_ref, rows), g=_take(do_s, rows),
                            lse=_take(lse_ref, rows), delta=_take(del_s, rows),
                            k=_take(k_ref, keys).astype(BF16), v=_take(v_ref, keys).astype(BF16),
                            bias=bias_ref[2 * pi + which])

            def per_head(t):
                swapped = pltpu.roll(t, HEAD_DIM, 1)
                both = jnp.concatenate([jnp.where(h0, t, swapped), jnp.where(h0, swapped, t)], axis=0)
                return jnp.concatenate([both, both], axis=1)

            def compute(item):
                q2, g2 = _stack_heads(item["q"] * SCORE_SCALE, h0), _stack_heads(item["g"], h0)
                s = _mm_nt(q2, item["k"]) + item["bias"]
                p = jnp.exp(s - per_head(item["lse"]))
                dp = _mm_nt(g2, item["v"])
                ds = (p * (dp - per_head(item["delta"]))).astype(BF16)
                dq2 = _mm(ds, item["k"])
                dq = jnp.where(h0, dq2[:ATTN_BLOCK], dq2[ATTN_BLOCK:]) * SCORE_SCALE
                return dq, _mm_tn(ds, q2), _mm_tn(p.astype(BF16), g2)

            def store(item, res, first=first):
                _put(dq_s, item["rows"], res[0], add=not first)
                _put(dk_s, item["keys"], res[1], add=True)
                _put(dv_s, item["keys"], res[2], add=True)

            _for_each_group(d, ATTN_GROUP_BWD, load, compute, store)
        _unsort_rows(dq_s, dq_ref)
        _unsort_rows(dk_s, dk_ref)
        _unsort_rows(dv_s, dv_ref)

    slab = lambda g: pl.BlockSpec((SEQ, LANES), functools.partial(lambda hp, g: (0, 4 * g + hp), g=g))
    wide = jax.ShapeDtypeStruct((SEQ, ATTN_WIDTH), F32)
    sorted_slab = pltpu.VMEM((SEQ, LANES), F32)
    return pl.pallas_call(
        body, name="attn_bwd", grid=(4,), out_shape=(wide, wide, wide),
        scratch_shapes=[sorted_slab] * 5 + [pltpu.VMEM((2 * len(DILATIONS), 2 * ATTN_BLOCK, 2 * ATTN_BLOCK), F32)],
        in_specs=[slab(0), slab(1), slab(2), slab(0), slab(0), slab(0)], out_specs=(slab(0), slab(0), slab(0)),
        compiler_params=_params(("arbitrary",)),
    )(qkv_sorted, qkv_sorted, qkv_sorted, d_out, lse_sorted, delta)


def _hgrn_lower_bound(lb_ref):
    r0, r1 = lb_ref[0:1, :], lb_ref[1:2, :]
    mx = jnp.maximum(r0, r1)
    e0, e1 = jnp.exp(r0 - mx), jnp.exp(r1 - mx)
    return e0 / (e0 + e1)


def _hgrn_gates(hq, hf, lb):
    sq = _sigmoid(hq)
    sg = _sigmoid(hf)
    f = lb + (1.0 - lb) * sg
    return hq * sq, sq, sg, f, 1.0 - f, jnp.log(f)


HGRN_PAIR = 4
HGRN_SEQ_BLOCK = 1024
HGRN_GROUP = 4
HGRN_ROWS = HGRN_GROUP * HGRN_CHUNK


def _hgrn_specs(reverse):
    n_blocks = SEQ // HGRN_SEQ_BLOCK
    width = HGRN_PAIR * HGRN_DIM
    blk = (lambda s: n_blocks - 1 - s) if reverse else (lambda s: s)
    cols = lambda g: pl.BlockSpec((HGRN_SEQ_BLOCK, width),
                                  functools.partial(lambda p, s, g: (blk(s), (HGRN_HEADS // HGRN_PAIR) * g + p), g=g))
    pair = pl.BlockSpec((HGRN_SEQ_BLOCK, width), lambda p, s: (blk(s), p))
    lb = pl.BlockSpec((2, width), lambda p, s: (0, p))
    states = pl.BlockSpec((HGRN_PAIR, HGRN_SEQ_BLOCK // HGRN_CHUNK, HGRN_DIM, HGRN_DIM),
                          lambda p, s: (p, blk(s), 0, 0))
    return cols, pair, lb, states


def _chunk_masks():
    ri = lax.broadcasted_iota(jnp.int32, (HGRN_ROWS, HGRN_ROWS), 0)
    ci = lax.broadcasted_iota(jnp.int32, (HGRN_ROWS, HGRN_ROWS), 1)
    same = (ri // HGRN_CHUNK) == (ci // HGRN_CHUNK)
    return same, same & (ri >= ci), same & (ri <= ci)


def _mm_select(sel, v):
    hi = v.astype(BF16)
    r1 = v - hi.astype(F32)
    mid = r1.astype(BF16)
    lo = (r1 - mid.astype(F32)).astype(BF16)
    return _mm(sel, hi) + _mm(sel, mid) + _mm(sel, lo)


def _head_cols(a, h):
    return a[:, HGRN_DIM * h:HGRN_DIM * (h + 1)]


def _hgrn_fwd(proj, lb_raw):
    t, rws = HGRN_CHUNK, HGRN_ROWS

    def body(hq_ref, hf_ref, hi_ref, lb_ref, rec_ref, st_ref, state):
        @pl.when(pl.program_id(1) == 0)
        def _():
            state[...] = jnp.zeros_like(state)

        lb = _hgrn_lower_bound(lb_ref)
        same, causal, _ = _chunk_masks()
        sel = jnp.concatenate([causal, same], axis=0).astype(BF16)

        def group(g, sts):
            rows = pl.ds(pl.multiple_of(g * rws, rws), rws)
            q, _, _, _, k, lf = _hgrn_gates(hq_ref[rows, :], hf_ref[rows, :], lb)
            sums = _mm_select(sel, lf)
            cum, last = sums[:rws], sums[rws:]
            qd = (q * jnp.exp(cum)).astype(BF16)
            ki = (k * jnp.exp(-cum)).astype(BF16)
            ke = (k * jnp.exp(last - cum)).astype(BF16)
            vb = hi_ref[rows, :].astype(BF16)
            dec = jnp.exp(last)
            new_sts, recs = [], []
            for h in range(HGRN_PAIR):
                qd_h, ke_h, vb_h = _head_cols(qd, h), _head_cols(ke, h), _head_cols(vb, h)
                att = jnp.where(causal, _mm_nt(qd_h, _head_cols(ki, h)), 0.0).astype(BF16)
                intra = _mm(att, vb_h)
                st = sts[h]
                outs = []
                for c in range(HGRN_GROUP):
                    sl = slice(c * t, (c + 1) * t)
                    st_ref[h, g * HGRN_GROUP + c] = st
                    outs.append(intra[sl] + _mm_nt(qd_h[sl], st.astype(BF16)))
                    st = st * _head_cols(dec[c * t:c * t + 1, :], h) + _mm_tn(vb_h[sl], ke_h[sl])
                new_sts.append(st)
                recs.append(jnp.concatenate(outs, axis=0))
            rec_ref[rows, :] = jnp.concatenate(recs, axis=1)
            return tuple(new_sts)

        sts = lax.fori_loop(0, HGRN_SEQ_BLOCK // rws, group, tuple(state[h] for h in range(HGRN_PAIR)))
        for h in range(HGRN_PAIR):
            state[h] = sts[h]

    cols, pair, lb, states = _hgrn_specs(reverse=False)
    return pl.pallas_call(
        body, name="hgrn_fwd", grid=(HGRN_HEADS // HGRN_PAIR, SEQ // HGRN_SEQ_BLOCK),
        out_shape=(jax.ShapeDtypeStruct((SEQ, HGRN_WIDTH), F32),
                   jax.ShapeDtypeStruct((HGRN_HEADS, N_CHUNKS, HGRN_DIM, HGRN_DIM), F32)),
        in_specs=[cols(4), cols(5), cols(6), lb], out_specs=(pair, states),
        scratch_shapes=[pltpu.VMEM((HGRN_PAIR, HGRN_DIM, HGRN_DIM), F32)],
        compiler_params=_params(("parallel", "arbitrary")),
    )(proj, proj, proj, lb_raw)


def _hgrn_bwd(proj, lb_raw, d_rec, states):
    t, rws = HGRN_CHUNK, HGRN_ROWS

    def body(hq_ref, hf_ref, hi_ref, lb_ref, do_ref, st_ref, dhq_ref, dhf_ref, dhi_ref, dlb_ref,
             dstate, dlb_acc):
        lb = _hgrn_lower_bound(lb_ref)
        same, causal, anti = _chunk_masks()
        sel = jnp.concatenate([causal, same], axis=0).astype(BF16)
        sel_t = jnp.concatenate([anti, same], axis=1).astype(BF16)
        @pl.when(pl.program_id(1) == 0)
        def _():
            dstate[...] = jnp.zeros_like(dstate)
            dlb_acc[...] = jnp.zeros_like(dlb_acc)

        n_groups = HGRN_SEQ_BLOCK // rws
        chunks = [slice(c * t, (c + 1) * t) for c in range(HGRN_GROUP)]

        def group(i, dsts_in):
            g = n_groups - 1 - i
            rows = pl.ds(pl.multiple_of(g * rws, rws), rws)
            hq = hq_ref[rows, :]
            q, sq, sg, f, k, lf = _hgrn_gates(hq, hf_ref[rows, :], lb)
            sums = _mm_select(sel, lf)
            cum, last = sums[:rws], sums[rws:]
            e_cum, e_inv, e_end, dec = jnp.exp(cum), jnp.exp(-cum), jnp.exp(last - cum), jnp.exp(last)
            qd, ki, ke = q * e_cum, k * e_inv, k * e_end
            qdb, kib, keb = qd.astype(BF16), ki.astype(BF16), ke.astype(BF16)
            vb = hi_ref[rows, :].astype(BF16)
            gb = do_ref[rows, :].astype(BF16)

            dsts_out, per_head = [], []
            for h in range(HGRN_PAIR):
                qdb_h, kib_h, keb_h = _head_cols(qdb, h), _head_cols(kib, h), _head_cols(keb, h)
                vb_h, gb_h = _head_cols(vb, h), _head_cols(gb, h)
                att = jnp.where(causal, _mm_nt(qdb_h, kib_h), 0.0).astype(BF16)
                datt = jnp.where(causal, _mm_nt(gb_h, vb_h), 0.0).astype(BF16)
                dv = _mm_tn(att, gb_h)
                dqd = _mm(datt, kib_h)
                dki = _mm_tn(datt, qdb_h)

                decs = [_head_cols(dec[c * t:c * t + 1, :], h) for c in range(HGRN_GROUP)]
                dsts = [None] * HGRN_GROUP
                dst = dsts_in[h]
                for c in reversed(range(HGRN_GROUP)):
                    dsts[c] = dst
                    dst = dst * decs[c] + _mm_tn(gb_h[chunks[c]], qdb_h[chunks[c]])
                dsts_out.append(dst)

                dv_x, dqd_x, dke, dlast_x = [], [], [], []
                for c, sl in enumerate(chunks):
                    st_prev = st_ref[h, g * HGRN_GROUP + c]
                    dstb = dsts[c].astype(BF16)
                    dv_x.append(_mm_nt(keb_h[sl], dstb))
                    dqd_x.append(_mm(gb_h[sl], st_prev.astype(BF16)))
                    dke.append(_mm(vb_h[sl], dstb))
                    ddec = jnp.sum(dsts[c] * st_prev, axis=0, keepdims=True)
                    dlast_x.append(jnp.broadcast_to(ddec * decs[c], (t, HGRN_DIM)))
                per_head.append((dv + jnp.concatenate(dv_x, axis=0), dqd + jnp.concatenate(dqd_x, axis=0),
                                 dki, jnp.concatenate(dke, axis=0), jnp.concatenate(dlast_x, axis=0)))
            dv, dqd, dki, dke, dlast = (jnp.concatenate(list(parts), axis=1) for parts in zip(*per_head))

            dq = dqd * e_cum
            dk = dki * e_inv + dke * e_end
            dke_ke = dke * ke
            dcum = dqd * qd - dki * ki - dke_ke
            dlf = _mm_select(sel_t, jnp.concatenate([dcum, dke_ke], axis=0)) + dlast
            df = dlf / f - dk
            dhq_ref[rows, :] = dq * (sq * (1.0 + hq * (1.0 - sq)))
            dhf_ref[rows, :] = df * (1.0 - lb) * (sg * (1.0 - sg))
            dhi_ref[rows, :] = dv
            dlb_acc[...] += jnp.sum(df * (1.0 - sg), axis=0, keepdims=True)
            return tuple(dsts_out)

        dsts = lax.fori_loop(0, n_groups, group, tuple(dstate[h] for h in range(HGRN_PAIR)))
        for h in range(HGRN_PAIR):
            dstate[h] = dsts[h]
        g0 = dlb_acc[...] * lb * (1.0 - lb)
        dlb_ref[...] = jnp.concatenate([g0, -g0], axis=0)

    cols, pair, lb_spec, st_spec = _hgrn_specs(reverse=True)
    wide = jax.ShapeDtypeStruct((SEQ, HGRN_WIDTH), F32)
    return pl.pallas_call(
        body, name="hgrn_bwd", grid=(HGRN_HEADS // HGRN_PAIR, SEQ // HGRN_SEQ_BLOCK),
        out_shape=(wide, wide, wide, jax.ShapeDtypeStruct((2, HGRN_WIDTH), F32)),
        in_specs=[cols(4), cols(5), cols(6), lb_spec, pair, st_spec],
        out_specs=(pair, pair, pair, lb_spec),
        scratch_shapes=[pltpu.VMEM((HGRN_PAIR, HGRN_DIM, HGRN_DIM), F32),
                        pltpu.VMEM((1, HGRN_PAIR * HGRN_DIM), F32)],
        compiler_params=_params(("parallel", "arbitrary")),
    )(proj, proj, proj, lb_raw, d_rec, states)


def _group_sum(v, group):
    parts = []
    for s in range(v.shape[1] // LANES):
        slab = v[:, LANES * s:LANES * (s + 1)]
        if group == LANES:
            parts.append(jnp.broadcast_to(jnp.sum(slab, axis=-1, keepdims=True), slab.shape))
        else:
            h0 = lax.broadcasted_iota(jnp.int32, slab.shape, 1) < HEAD_DIM
            s0 = jnp.sum(jnp.where(h0, slab, 0.0), axis=-1, keepdims=True)
            s1 = jnp.sum(jnp.where(h0, 0.0, slab), axis=-1, keepdims=True)
            parts.append(jnp.where(h0, s0, s1))
    return jnp.concatenate(parts, axis=1)


def _mid(attn_o, rec, proj, x, target, w_out_g, attn_w, hgrn_w, final_w):
    tm = 256

    def branch_fwd(o, gate, w, group):
        r = lax.rsqrt(_group_sum(o * o, group) * (1.0 / group) + NORM_EPS)
        nrm = o * r
        sg = _sigmoid(gate)
        return r, nrm, sg, nrm * w * (gate * sg)

    def branch_bwd(dy, r, nrm, sg, gate, w, group):
        silu = gate * sg
        d_gate = dy * nrm * w * (sg * (1.0 + gate * (1.0 - sg)))
        d_w = jnp.sum(dy * nrm * silu, axis=0, keepdims=True)
        dn = dy * w * silu
        d_o = r * (dn - nrm * (_group_sum(dn * nrm, group) * (1.0 / group)))
        return d_o, d_gate, d_w

    def body(o_ref, rec_ref, ag_ref, hg_ref, x_ref, tgt_ref, wout_ref, aw_ref, hw_ref, fw_ref,
             dx2_ref, do_ref, delta_ref, dag_ref, drec_ref, dhg_ref, dwout_ref, dfw_ref, daw_ref, dhw_ref,
             loss_ref, dwout_acc):
        i = pl.program_id(0)

        @pl.when(i == 0)
        def _():
            dwout_acc[...] = jnp.zeros_like(dwout_acc)
            dfw_ref[...] = jnp.zeros_like(dfw_ref)
            daw_ref[...] = jnp.zeros_like(daw_ref)
            dhw_ref[...] = jnp.zeros_like(dhw_ref)
            loss_ref[...] = jnp.zeros_like(loss_ref)

        o, rc, ag, hg = o_ref[...], rec_ref[...], ag_ref[...], hg_ref[...]
        aw, hw, fw = aw_ref[...], hw_ref[...], fw_ref[...]
        ra, na, sga, ya = branch_fwd(o, ag, aw, HEAD_DIM)
        rh, nh, sgh, yh = branch_fwd(rc, hg, hw, HGRN_DIM)
        mixed = jnp.concatenate([ya, yh], axis=1).astype(BF16)
        wout = wout_ref[...]
        x2 = x_ref[...] + _mm(mixed, wout)
        rstd = lax.rsqrt(jnp.mean(x2 * x2, axis=-1, keepdims=True) + NORM_EPS)
        xn = x2 * rstd
        err = xn * fw - tgt_ref[...]
        row_loss = jnp.mean(err * err, axis=-1, keepdims=True)
        loss_ref[...] += 0.5 * jnp.sum(row_loss, axis=0, keepdims=True)
        dy = err * (1.0 / D_MODEL)
        dfw_ref[...] += jnp.sum(dy * xn, axis=0, keepdims=True)
        dxn = dy * fw
        dx2 = rstd * (dxn - xn * jnp.mean(dxn * xn, axis=-1, keepdims=True))
        dx2_ref[...] = dx2
        dx2b = dx2.astype(BF16)
        dwout_acc[...] += _mm_tn(mixed, dx2b)

        @pl.when(i == pl.num_programs(0) - 1)
        def _():
            dwout_ref[...] = dwout_acc[...].astype(BF16)

        dmixed = _mm_nt(dx2b, wout)

        d_o, d_ag, d_aw = branch_bwd(dmixed[:, :ATTN_WIDTH], ra, na, sga, ag, aw, HEAD_DIM)
        d_rec, d_hg, d_hw = branch_bwd(dmixed[:, ATTN_WIDTH:], rh, nh, sgh, hg, hw, HGRN_DIM)
        do_ref[...] = d_o
        delta_ref[...] = _group_sum(d_o * o, HEAD_DIM)
        dag_ref[...] = d_ag
        drec_ref[...] = d_rec
        dhg_ref[...] = d_hg
        daw_ref[...] += d_aw
        dhw_ref[...] += d_hw

    half = lambda: pl.BlockSpec((tm, COL_BLOCK), lambda i: (i, 0))
    full = lambda: pl.BlockSpec((tm, D_MODEL), lambda i: (i, 0))
    fixed = lambda r, c: pl.BlockSpec((r, c), lambda i: (0, 0))
    wide = jax.ShapeDtypeStruct((SEQ, COL_BLOCK), F32)
    return pl.pallas_call(
        body, name="mid", grid=(SEQ // tm,),
        out_shape=(jax.ShapeDtypeStruct((SEQ, D_MODEL), F32), wide, wide, wide, wide, wide,
                   jax.ShapeDtypeStruct((D_MODEL, D_MODEL), BF16),
                   jax.ShapeDtypeStruct((1, D_MODEL), F32), jax.ShapeDtypeStruct((1, COL_BLOCK), F32),
                   jax.ShapeDtypeStruct((1, COL_BLOCK), F32), jax.ShapeDtypeStruct((1, 1), F32)),
        scratch_shapes=[pltpu.VMEM((D_MODEL, D_MODEL), F32)],
        in_specs=[half(), half(),
                  pl.BlockSpec((tm, COL_BLOCK), lambda i: (i, 3)), pl.BlockSpec((tm, COL_BLOCK), lambda i: (i, 7)),
                  full(), full(), fixed(D_MODEL, D_MODEL), fixed(1, COL_BLOCK), fixed(1, COL_BLOCK),
                  fixed(1, D_MODEL)],
        out_specs=(full(), half(), half(), half(), half(), half(), fixed(D_MODEL, D_MODEL),
                   fixed(1, D_MODEL), fixed(1, COL_BLOCK), fixed(1, COL_BLOCK), fixed(1, 1)),
        compiler_params=_params(("arbitrary",)),
    )(attn_o, rec, proj, proj, x, target, w_out_g, attn_w, hgrn_w, final_w)


def _in_proj_bwd_rows(d_groups, w_g, x, dx2, mix_w, rc, rsa, rsb):
    tm = 256

    def body(*refs):
        dg_refs = refs[:N_DEV]
        wg_ref, x_ref, dx2_ref, w_ref, c_ref, sa_ref, sb_ref, gx_ref, dpb_ref, dmw_ref = refs[N_DEV:]

        @pl.when(pl.program_id(0) == 0)
        def _():
            dmw_ref[...] = jnp.zeros_like(dmw_ref)

        parts = []
        for j in range(N_DEV):
            dp = dg_refs[j][...]
            if j < 2:
                dp = _rot_transposed(dp, c_ref[...], sa_ref[...], sb_ref[...])
            parts.append(dp.astype(BF16))
        dpb = jnp.concatenate(parts, axis=1)
        dpb_ref[...] = dpb
        g = _mm_nt(dpb, wg_ref[...])
        xf = x_ref[...]
        rstd = lax.rsqrt(jnp.mean(xf * xf, axis=-1, keepdims=True) + NORM_EPS)
        xn = xf * rstd
        dmw_ref[...] += jnp.sum(g * xn, axis=0, keepdims=True)
        gw = g * w_ref[...]
        gx_ref[...] = dx2_ref[...] + rstd * (gw - xn * jnp.mean(gw * xn, axis=-1, keepdims=True))

    tile = lambda cols: pl.BlockSpec((tm, cols), lambda i: (i, 0))
    fixed = lambda r, c: pl.BlockSpec((r, c), lambda i: (0, 0))
    return pl.pallas_call(
        body, name="in_proj_bwd_rows", grid=(SEQ // tm,),
        out_shape=(jax.ShapeDtypeStruct((SEQ, D_MODEL), F32), jax.ShapeDtypeStruct((SEQ, IN_COLS), BF16),
                   jax.ShapeDtypeStruct((1, D_MODEL), F32)),
        in_specs=[tile(COL_BLOCK) for _ in range(N_DEV)] + [
            pl.BlockSpec((D_MODEL, IN_COLS), lambda i: (0, 0), pipeline_mode=pl.Buffered(1)),
            tile(D_MODEL), tile(D_MODEL), fixed(1, D_MODEL), tile(LANES), tile(LANES), tile(LANES)],
        out_specs=(tile(D_MODEL), tile(IN_COLS), fixed(1, D_MODEL)),
        compiler_params=_params(("arbitrary",)),
    )(*d_groups, w_g, x, dx2, mix_w, rc, rsa, rsb)


def _weights_exchange(hn_t, dproj_b, dwout_p, small_p):
    n_chips = N_DEV // 2
    rb = 128
    S1_IN, S1_OUT, SMALL, S2_IN, S2_OUT = 0, 4, 8, 15, 18
    rel_of_pair = (1, 2, 3, 0)

    def body(order_ref, hnt_ref, dp_ref, dwout_ref, small_ref, gin_ref, gout_ref, gs_ref,
             part, s1_send, s1_in, s1_out, fwd_in, fwd_out, s2_in, s2_out, land_s, send_sems, recv_sems):
        t = pl.program_id(0)
        me = _my_place()
        x, y, c = me
        my_chip = 2 * x + y
        sibling = (x, y, 1 - c)

        def remote(slot, src, dst, to):
            return pltpu.make_async_remote_copy(src_ref=src, dst_ref=dst, send_sem=send_sems.at[slot],
                                                recv_sem=recv_sems.at[slot], device_id=to, device_id_type=MESH)

        def s1_in_copy(pair):
            return remote(S1_IN + pair, s1_send.at[pair], s1_in.at[pair], sibling)

        def s1_out_copy(pair):
            q = my_chip ^ rel_of_pair[pair]
            return remote(S1_OUT + pair, dwout_ref.at[q, 1 - c], s1_out.at[pair], sibling)

        def s2_copies(rel):
            peer = _peer(me, 2 * rel)
            return [remote(S2_IN + rel - 1, fwd_in.at[rel - 1], s2_in.at[rel - 1], peer),
                    remote(S2_OUT + rel - 1, fwd_out.at[rel - 1], s2_out.at[rel - 1], peer)]

        def small_copy(rel):
            return remote(SMALL + rel - 1, small_ref, land_s.at[rel], _peer(me, rel))

        @pl.when(t == 0)
        def _():
            land_s[0] = small_ref[...]
            for pair in range(n_chips):
                s1_out_copy(pair).start()
            for rel in range(1, N_DEV):
                small_copy(rel).start()

        part[...] = _mm(hnt_ref[...], dp_ref[...])

        def rows_loop(n_rows, fn):
            def step(b, carry):
                fn(pl.ds(pl.multiple_of(b * rb, rb), rb))
                return carry
            lax.fori_loop(0, n_rows // rb, step, 0)

        for pair, rel in enumerate(rel_of_pair):
            @pl.when(t == 2 * pair)
            def _(pair=pair):
                s1_send[pair] = part[...].astype(BF16)
                s1_in_copy(pair).start()

            @pl.when(t == 2 * pair + 1)
            def _(pair=pair, rel=rel):
                q = my_chip ^ rel
                s1_in_copy(pair).wait_recv()
                s1_out_copy(pair).wait_recv()
                dst_in = fwd_in.at[rel - 1] if rel else gin_ref
                dst_out = fwd_out.at[rel - 1] if rel else gout_ref

                def add_in(rows):
                    dst_in[rows, :] = (part[rows, :] + s1_in[pair, rows, :].astype(F32)).astype(dst_in.dtype)

                def add_out(rows):
                    dst_out[rows, :] = (dwout_ref[q, c, rows, :].astype(F32)
                                        + s1_out[pair, rows, :].astype(F32)).astype(dst_out.dtype)

                rows_loop(D_MODEL, add_in)
                rows_loop(WOUT_ROWS, add_out)
                if rel:
                    for cp in s2_copies(rel):
                        cp.start()

        @pl.when(t == N_DEV - 1)
        def _():
            for rel in range(1, n_chips):
                for cp in s2_copies(rel):
                    cp.wait_recv()

            def total_in(rows):
                g = gin_ref[rows, :]
                for rel in range(1, n_chips):
                    g = g + s2_in[rel - 1, rows, :].astype(F32)
                gin_ref[rows, :] = g

            def total_out(rows):
                g = gout_ref[rows, :]
                for rel in range(1, n_chips):
                    g = g + s2_out[rel - 1, rows, :].astype(F32)
                gout_ref[rows, :] = g

            rows_loop(D_MODEL, total_in)
            rows_loop(WOUT_ROWS, total_out)

            for rel in range(1, N_DEV):
                small_copy(rel).wait_recv()
            my_flat = _flat(me)
            g = land_s[my_flat ^ 0]
            for dev in range(1, N_DEV):
                g = g + land_s[my_flat ^ dev]
            gs_ref[...] = g

            for pair in range(n_chips):
                s1_in_copy(pair).wait_send()
                s1_out_copy(pair).wait_send()
            for rel in range(1, n_chips):
                for cp in s2_copies(rel):
                    cp.wait_send()
            for rel in range(1, N_DEV):
                small_copy(rel).wait_send()

    place_x, place_y, place_c = _my_place()
    my_chip = 2 * place_x + place_y
    order = jnp.stack([2 * (my_chip ^ rel) + core for rel in rel_of_pair
                       for core in (1 - place_c, place_c)]).astype(jnp.int32)

    whole = lambda: pl.BlockSpec(memory_space=pltpu.VMEM)
    in_blocks = lambda n: pltpu.VMEM((n, D_MODEL, COL_BLOCK), BF16)
    out_blocks = lambda n: pltpu.VMEM((n, WOUT_ROWS, D_MODEL), BF16)
    grid_spec = pltpu.PrefetchScalarGridSpec(
        num_scalar_prefetch=1, grid=(N_DEV,),
        in_specs=[pl.BlockSpec((D_MODEL, SEQ), lambda t, order: (0, 0), pipeline_mode=pl.Buffered(1)),
                  pl.BlockSpec((SEQ, COL_BLOCK), lambda t, order: (0, order[t])), whole(), whole()],
        out_specs=(whole(), whole(), whole()),
        scratch_shapes=[pltpu.VMEM((D_MODEL, COL_BLOCK), F32), in_blocks(n_chips), in_blocks(n_chips),
                        out_blocks(n_chips), in_blocks(n_chips - 1), out_blocks(n_chips - 1),
                        in_blocks(n_chips - 1), out_blocks(n_chips - 1),
                        pltpu.VMEM((N_DEV, SMALL_ROWS, LANES), F32),
                        pltpu.SemaphoreType.DMA((21,)), pltpu.SemaphoreType.DMA((21,))])
    return pl.pallas_call(
        body, name="weights_exchange", grid_spec=grid_spec,
        out_shape=(jax.ShapeDtypeStruct((D_MODEL, COL_BLOCK), F32), jax.ShapeDtypeStruct((WOUT_ROWS, D_MODEL), F32),
                   jax.ShapeDtypeStruct((SMALL_ROWS, LANES), F32)),
        compiler_params=_params(("arbitrary",)),
    )(order, hn_t, dproj_b, dwout_p.reshape(n_chips, 2, WOUT_ROWS, D_MODEL), small_p)


def _adamw(w, g, m, v):
    m = ADAM_B1 * m + (1.0 - ADAM_B1) * g
    v = ADAM_B2 * v + (1.0 - ADAM_B2) * (g * g)
    m_hat = m / (1.0 - ADAM_B1 ** ADAM_STEP)
    v_hat = v / (1.0 - ADAM_B2 ** ADAM_STEP)
    delta = -ADAM_LR * (m_hat / (jnp.sqrt(v_hat) + ADAM_EPS) + ADAM_WD * w)
    return delta, m, v


def _adamw_update(grads, weights, m_old, v_old):
    rb = 256

    def body(*refs):
        g_refs, w_refs, m_refs, v_refs = refs[0:3], refs[3:6], refs[6:9], refs[9:12]
        d_refs, nm_refs, nv_refs = refs[12:15], refs[15:18], refs[18:21]
        for k in range(3):
            n_rows = g_refs[k].shape[0]
            step_rows = min(rb, n_rows)

            def step(b, carry, k=k, step_rows=step_rows):
                rows = pl.ds(pl.multiple_of(b * step_rows, 8), step_rows)
                delta, nm, nv = _adamw(w_refs[k][rows, :], g_refs[k][rows, :], m_refs[k][rows, :], v_refs[k][rows, :])
                d_refs[k][rows, :] = delta
                nm_refs[k][rows, :] = nm
                nv_refs[k][rows, :] = nv
                return carry

            lax.fori_loop(0, n_rows // step_rows, step, 0)

    shapes = tuple(jax.ShapeDtypeStruct(g.shape, F32) for g in grads)
    vm = lambda: pl.BlockSpec(memory_space=pltpu.VMEM)
    outs = pl.pallas_call(
        body, name="adamw_update", out_shape=shapes * 3,
        in_specs=[vm() for _ in range(12)], out_specs=tuple(vm() for _ in range(9)),
        compiler_params=_params(),
    )(*grads, *weights, *m_old, *v_old)
    return outs[0:3], outs[3:6], outs[6:9]


def _pack_small(mix, attn, hgrn, lb, final, loss=None):
    def rows8(a):
        a = a.reshape(-1, LANES)
        return jnp.pad(a, ((0, 8 - a.shape[0]), (0, 0)))
    last = jnp.zeros((8, LANES), F32) if loss is None else jnp.pad(loss.reshape(1, 1), ((0, 7), (0, LANES - 1)))
    return jnp.concatenate([rows8(mix), rows8(attn), rows8(hgrn), rows8(lb), rows8(final), last], axis=0)


def _unpack_small(slab):
    return (slab[ROW_MIX:ROW_MIX + 8].reshape(1, D_MODEL), slab[ROW_ATTN:ROW_ATTN + 4].reshape(1, ATTN_WIDTH),
            slab[ROW_HGRN:ROW_HGRN + 4].reshape(1, HGRN_WIDTH), slab[ROW_LB:ROW_LB + 8].reshape(2, HGRN_WIDTH),
            slab[ROW_FINAL:ROW_FINAL + 8].reshape(D_MODEL))


def _rope(pos_row):
    j = np.arange(ROPE_ROWS)
    inv = np.where(j < ROPE_HALF, ROPE_THETA ** (-(j % ROPE_HALF) * (2.0 / ROPE_DIMS)), 0.0)
    e = np.arange(LANES) % HEAD_DIM
    hit = (j[:, None] == (e % ROPE_HALF)[None, :]) & (j[:, None] < ROPE_HALF)
    sel = np.stack([hit & (e < ROPE_DIMS), hit & (e >= ROPE_HALF) & (e < ROPE_DIMS),
                    -1.0 * (hit & (e < ROPE_HALF))]).astype(np.float32)
    return _rope_tables(pos_row, jnp.asarray(inv.astype(np.float32).reshape(ROPE_ROWS, 1)),
                        jnp.asarray(sel, dtype=BF16))


def _local_step(x, proj, qkv_sorted, w_in_g, w_out_g, tables, mix_w, attn_w, hgrn_w, lb_raw, final_w, target):
    rc, rsa, rsb = tables
    attn_o, lse = _attn_fwd_fused(qkv_sorted)
    rec, states = _hgrn_fwd(proj, lb_raw)

    (dx2, d_o, delta, d_ag, d_rec, d_hg, dwout_p, d_final, d_attn_w, d_hgrn_w, loss) = _mid(
        attn_o, rec, proj, x, target, w_out_g, attn_w, hgrn_w, final_w.reshape(1, D_MODEL))

    dqkv = _attn_bwd_fused(qkv_sorted, d_o, lse, delta)
    d_hq, d_hf, d_hi, d_lb = _hgrn_bwd(proj, lb_raw, d_rec, states)

    grad_x, dproj_b, d_mix = _in_proj_bwd_rows(
        (dqkv[0], dqkv[1], dqkv[2], d_ag, d_hq, d_hf, d_hi, d_hg), w_in_g, x, dx2, mix_w, rc, rsa, rsb)
    small_p = _pack_small(d_mix, d_attn_w, d_hgrn_w, d_lb, d_final, loss)
    return grad_x, dproj_b, dwout_p, small_p


def kernel(x, positions, w_in, w_out, mix_norm_w, attn_out_norm_w, hgrn_out_norm_w, hgrn_lb_raw, final_norm_w, loss_target, m_w_in, m_w_out, m_mix_norm_w, m_attn_out_norm_w, m_hgrn_out_norm_w, m_hgrn_lb_raw, m_final_norm_w, v_w_in, v_w_out, v_mix_norm_w, v_attn_out_norm_w, v_hgrn_out_norm_w, v_hgrn_lb_raw, v_final_norm_w):
    tables = _rope(positions)
    proj, hn_t, w_in_g, w_out_g, qkv_sorted = _gather_project(x[0], mix_norm_w, w_in[0], w_out[0], *tables)
    grad_x, dproj_b, dwout_p, small_p = _local_step(
        x[0], proj, qkv_sorted, w_in_g, w_out_g, tables, mix_norm_w, attn_out_norm_w, hgrn_out_norm_w,
        hgrn_lb_raw, final_norm_w, loss_target[0])
    g_in, g_out, g_s = _weights_exchange(hn_t, dproj_b, dwout_p, small_p)

    w_s = _pack_small(mix_norm_w, attn_out_norm_w, hgrn_out_norm_w, hgrn_lb_raw, final_norm_w)
    m_s = _pack_small(m_mix_norm_w, m_attn_out_norm_w, m_hgrn_out_norm_w, m_hgrn_lb_raw, m_final_norm_w)
    v_s = _pack_small(v_mix_norm_w, v_attn_out_norm_w, v_hgrn_out_norm_w, v_hgrn_lb_raw, v_final_norm_w)
    (d_in, d_out, d_s), (nm_in, nm_out, nm_s), (nv_in, nv_out, nv_s) = _adamw_update(
        (g_in, g_out, g_s), (w_in[0], w_out[0], w_s), (m_w_in[0], m_w_out[0], m_s), (v_w_in[0], v_w_out[0], v_s))

    loss = g_s[ROW_LOSS, 0]
    return (loss, grad_x[None], g_in[None], g_out[None], *_unpack_small(g_s),
            d_in[None], d_out[None], *_unpack_small(d_s),
            nm_in[None], nm_out[None], *_unpack_small(nm_s),
            nv_in[None], nv_out[None], *_unpack_small(nv_s))
```

```python
import functools

import jax
import jax.numpy as jnp
import numpy as np
from jax import lax
from jax.experimental import pallas as pl
from jax.experimental.pallas import tpu as pltpu

F32 = jnp.float32
BF16 = jnp.bfloat16

SEQ = 4096
D_MODEL = 1024
ATTN_WIDTH = 512
HGRN_WIDTH = 512
HEAD_DIM = 64
HGRN_HEADS = 4
HGRN_DIM = 128
HGRN_CHUNK = 64
N_CHUNKS = SEQ // HGRN_CHUNK
IN_COLS = 4096
COL_BLOCK = 512
N_DEV = 8
WOUT_ROWS = D_MODEL // N_DEV
ATTN_BLOCK = 128
DILATIONS = (1, 4, 16)
ROPE_THETA = 500000.0
ROPE_DIMS = 16
ROPE_HALF = 8
NORM_EPS = 1e-6
NEG_BIG = -1e30
LANES = 128

ADAM_LR = 0.001
ADAM_B1 = 0.9
ADAM_B2 = 0.999
ADAM_EPS = 1e-08
ADAM_WD = 0.01
ADAM_STEP = 10

SMALL_ROWS = 48
ROW_MIX, ROW_ATTN, ROW_HGRN, ROW_LB, ROW_FINAL, ROW_LOSS = 0, 8, 16, 24, 32, 40

VMEM_LIMIT = 56 * 1024 * 1024
MESH = pl.DeviceIdType.MESH


def _mm(a, b):
    return lax.dot_general(a, b, (((1,), (0,)), ((), ())), preferred_element_type=F32)


def _mm_nt(a, b):
    return lax.dot_general(a, b, (((1,), (1,)), ((), ())), preferred_element_type=F32)


def _mm_tn(a, b):
    return lax.dot_general(a, b, (((0,), (0,)), ((), ())), preferred_element_type=F32)


def _mm_exact(a, b):
    return lax.dot_general(a, b, (((1,), (0,)), ((), ())), preferred_element_type=F32,
                           precision=lax.Precision.HIGHEST)


def _sigmoid(v):
    return 1.0 / (1.0 + jnp.exp(-v))


def _params(sem=None, **kw):
    return pltpu.CompilerParams(dimension_semantics=sem, vmem_limit_bytes=VMEM_LIMIT, **kw)


def _my_place():
    return lax.axis_index("x"), lax.axis_index("y"), lax.axis_index("c")


def _peer(place, rel):
    x, y, c = place
    return (x ^ ((rel >> 2) & 1), y ^ ((rel >> 1) & 1), c ^ (rel & 1))


def _flat(place):
    x, y, c = place
    return 4 * x + 2 * y + c


ROPE_ROWS = 16


def _rope_tables(pos_row, inv_freq_col, selectors):
    def body(pos_ref, invf_ref, sel_ref, c_ref, sa_ref, sb_ref):
        ang = pos_ref[...].astype(F32) * invf_ref[...]
        cos, sin = jnp.cos(ang), jnp.sin(ang)

        def spread(v, sel):
            hi = v.astype(BF16)
            r1 = v - hi.astype(F32)
            mid = r1.astype(BF16)
            lo = (r1 - mid.astype(F32)).astype(BF16)
            return _mm_tn(hi, sel) + _mm_tn(mid, sel) + _mm_tn(lo, sel)

        e = lax.broadcasted_iota(jnp.int32, (1, LANES), 1) & (HEAD_DIM - 1)
        c_ref[...] = spread(cos, sel_ref[0]) + jnp.where(e < ROPE_DIMS, 0.0, 1.0)
        sa_ref[...] = spread(sin, sel_ref[1])
        sb_ref[...] = spread(sin, sel_ref[2])

    tab = jax.ShapeDtypeStruct((SEQ, LANES), F32)
    vm = lambda: pl.BlockSpec(memory_space=pltpu.VMEM)
    return pl.pallas_call(
        body, name="rope_tables", out_shape=(tab, tab, tab),
        in_specs=[vm(), vm(), vm()], out_specs=(vm(), vm(), vm()), compiler_params=_params(),
    )(pos_row, inv_freq_col, selectors)


def _per_slab(fn, t):
    return jnp.concatenate([fn(t[:, LANES * s:LANES * (s + 1)]) for s in range(t.shape[1] // LANES)], axis=1)


def _rot(t, c, sa, sb):
    return _per_slab(lambda u: u * c + pltpu.roll(u, ROPE_HALF, 1) * sa + pltpu.roll(u, LANES - ROPE_HALF, 1) * sb, t)


def _rot_transposed(g, c, sa, sb):
    return _per_slab(
        lambda u: u * c + pltpu.roll(u * sa, LANES - ROPE_HALF, 1) + pltpu.roll(u * sb, ROPE_HALF, 1), g)


def _gather_project(x, mix_w, w_in, w_out, rc, rsa, rsb):
    tm = 1024
    n_tiles = SEQ // tm
    arrival_of_step = (None, 0, 1, 2, 4, 5, 3, 6)

    def body(order_ref, x_ref, w_ref, win_ref, wout_ref, c_ref, sa_ref, sb_ref,
             proj_ref, hnt_ref, gin_hbm, gout_hbm, qkv_hbm,
             hn_s, w_land, wout_land, stage, sort_stage, slab_tmp, send_sems, recv_sems, local_sems):
        g, i = pl.program_id(0), pl.program_id(1)
        me = _my_place()
        x_, y_, c_ = me
        sibling = (x_, y_, 1 - c_)
        chips = [(1 - x_, y_), (x_, 1 - y_), (1 - x_, 1 - y_)]

        def slab(which, place):
            idx = _flat(place)
            if which == 0:
                return w_land.at[idx]
            return wout_land.at[pl.ds(pl.multiple_of(idx * WOUT_ROWS, WOUT_ROWS), WOUT_ROWS), :]

        def remote(which, k, ref, to, src=None):
            return pltpu.make_async_remote_copy(
                src_ref=ref if src is None else src, dst_ref=ref, send_sem=send_sems.at[8 * which + k],
                recv_sem=recv_sems.at[8 * which + k], device_id=to, device_id_type=MESH)

        def copy(which, k, block, to, src=None):
            return remote(which, k, slab(which, block), to, src)

        def half(which, place, part):
            n = (D_MODEL if which == 0 else WOUT_ROWS) // 2
            if which == 0:
                return w_land.at[_flat(place), pl.ds(n * part, n), :]
            return wout_land.at[pl.ds(pl.multiple_of(_flat(place) * WOUT_ROWS + n * part, n), n), :]

        def first_copies(which):
            src = stage if which == 0 else None
            return ([copy(which, 0, me, sibling, src)]
                    + [copy(which, 1 + j, me, (*chips[j], c_), src) for j in range(2)])

        def relay(which, part):
            frm, to = (chips[1], chips[0]) if part == 0 else (chips[0], chips[1])
            return remote(which, 3 if part == 0 else 7, half(which, (*frm, c_), part), (*to, c_))

        def two_hop_half(which, part):
            return remote(which, 3 if part == 0 else 7, half(which, (*chips[2], c_), part), me)

        def pass_on(which, j):
            return copy(which, 4 + j, (*chips[j], c_), sibling)

        def arrival(which, k):
            if k == 0:
                return copy(which, 0, sibling, me)
            if k <= 2:
                return copy(which, k, (*chips[k - 1], c_), me)
            return copy(which, k, (*chips[k - 4], 1 - c_), me)

        def to_hbm(step):
            idx = order_ref[step]
            cols = pl.ds(pl.multiple_of(idx * COL_BLOCK, COL_BLOCK), COL_BLOCK)
            return pltpu.make_async_copy(w_land.at[idx], gin_hbm.at[:, cols], local_sems.at[step])

        @pl.when((g == 0) & (i == 0))
        def _():
            stage[...] = win_ref[...].astype(BF16)
            w_land[_flat(me)] = stage[...]
            wout_land[pl.ds(pl.multiple_of(_flat(me) * WOUT_ROWS, WOUT_ROWS), WOUT_ROWS), :] = (
                wout_ref[...].astype(BF16))
            for cp in first_copies(0) + first_copies(1)[:1]:
                cp.start()
            to_hbm(0).start()

        for step, k in enumerate(arrival_of_step):
            if k is None:
                continue

            @pl.when((g == step) & (i == 0))
            def _(k=k, step=step):
                if k == 3:
                    two_hop_half(0, 0).wait_recv()
                    two_hop_half(0, 1).wait_recv()
                else:
                    arrival(0, k).wait_recv()
                to_hbm(step).start()
                if 1 <= k <= 3:
                    pass_on(0, k - 1).start()
                if k == 1:
                    relay(0, 1).start()
                    for cp in first_copies(1)[1:]:
                        cp.start()
                if k == 2:
                    relay(0, 0).start()
                if k in (4, 5):
                    arrival(1, k - 3).wait_recv()
                    relay(1, 5 - k).start()

        rows = pl.ds(pl.multiple_of(i * tm, tm), tm)

        @pl.when(g == 0)
        def _():
            xf = x_ref[...]
            ms = jnp.mean(xf * xf, axis=-1, keepdims=True)
            hn = xf * lax.rsqrt(ms + NORM_EPS) * w_ref[...]
            hnt_ref[...] = hn.T.astype(BF16)
            hn_s[rows, :] = hn.astype(BF16)

        group = order_ref[g]

        def sorted_copy():
            per = tm // SORT_RESIDUES
            for s in range(COL_BLOCK // LANES):
                slab_tmp[s] = proj_ref[:, LANES * s:LANES * (s + 1)]
            for r in range(SORT_RESIDUES):
                for s in range(COL_BLOCK // LANES):
                    sort_stage[r, :, LANES * s:LANES * (s + 1)] = (
                        slab_tmp.at[s][pl.ds(r, per, stride=SORT_RESIDUES), :])
            cols = pl.ds(pl.multiple_of(group * COL_BLOCK, COL_BLOCK), COL_BLOCK)
            cp = pltpu.make_async_copy(
                sort_stage, qkv_hbm.at[:, pl.ds(pl.multiple_of(i * per, per), per), cols], local_sems.at[N_DEV + 1])
            cp.start()
            cp.wait()

        @pl.when(group < 2)
        def _():
            proj_ref[...] = _rot(_mm(hn_s[rows, :], w_land[group]), c_ref[...], sa_ref[...], sb_ref[...])
            sorted_copy()

        @pl.when(group == 2)
        def _():
            proj_ref[...] = _mm(hn_s[rows, :], w_land[group])
            sorted_copy()

        @pl.when(group > 2)
        def _():
            proj_ref[...] = _mm(hn_s[rows, :], w_land[group])

        @pl.when((g == N_DEV - 1) & (i == n_tiles - 1))
        def _():
            pass_on(1, 0).start()
            pass_on(1, 1).start()
            two_hop_half(1, 0).wait_recv()
            two_hop_half(1, 1).wait_recv()
            pass_on(1, 2).start()
            for k in (0, 4, 5, 6):
                arrival(1, k).wait_recv()
            for which in (0, 1):
                for cp in (first_copies(which) + [relay(which, part) for part in range(2)]
                           + [pass_on(which, j) for j in range(3)]):
                    cp.wait_send()
            wout_copy = pltpu.make_async_copy(wout_land, gout_hbm, local_sems.at[N_DEV])
            wout_copy.start()
            for step in range(N_DEV):
                to_hbm(step).wait()
            wout_copy.wait()

    me = _my_place()
    x_, y_, c_ = me
    chips = [(1 - x_, y_), (x_, 1 - y_), (1 - x_, 1 - y_)]
    order = jnp.stack([_flat(p) for p in (
        me, (x_, y_, 1 - c_), (*chips[0], c_), (*chips[1], c_), (*chips[0], 1 - c_), (*chips[1], 1 - c_),
        (*chips[2], c_), (*chips[2], 1 - c_))]).astype(jnp.int32)

    first_sweep = lambda g, i, order: (jnp.where(g == 0, i, n_tiles - 1), 0)
    tab = pl.BlockSpec((tm, LANES), lambda g, i, order: (jnp.where(order[g] < 2, i, 0), 0))
    whole = lambda: pl.BlockSpec(memory_space=pltpu.VMEM)
    grid_spec = pltpu.PrefetchScalarGridSpec(
        num_scalar_prefetch=1, grid=(N_DEV, n_tiles),
        in_specs=[pl.BlockSpec((tm, D_MODEL), first_sweep),
                  pl.BlockSpec((1, D_MODEL), lambda g, i, order: (0, 0)),
                  whole(), whole(), tab, tab, tab],
        out_specs=(pl.BlockSpec((tm, COL_BLOCK), lambda g, i, order: (i, order[g])),
                   pl.BlockSpec((D_MODEL, tm), lambda g, i, order: (0, jnp.where(g == 0, i, n_tiles - 1))),
                   pl.BlockSpec(memory_space=pl.ANY), pl.BlockSpec(memory_space=pl.ANY),
                   pl.BlockSpec(memory_space=pl.ANY)),
        scratch_shapes=[pltpu.VMEM((SEQ, D_MODEL), BF16),
                        pltpu.VMEM((N_DEV, D_MODEL, COL_BLOCK), BF16),
                        pltpu.VMEM((D_MODEL, D_MODEL), BF16),
                        pltpu.VMEM((D_MODEL, COL_BLOCK), BF16),
                        pltpu.VMEM((SORT_RESIDUES, tm // SORT_RESIDUES, COL_BLOCK), F32),
                        pltpu.VMEM((COL_BLOCK // LANES, tm, LANES), F32),
                        pltpu.SemaphoreType.DMA((16,)), pltpu.SemaphoreType.DMA((16,)),
                        pltpu.SemaphoreType.DMA((N_DEV + 2,))])
    proj, hn_t, w_in_g, w_out_g, qkv_sorted = pl.pallas_call(
        body, name="gather_project", grid_spec=grid_spec,
        out_shape=(jax.ShapeDtypeStruct((SEQ, IN_COLS), F32), jax.ShapeDtypeStruct((D_MODEL, SEQ), BF16),
                   jax.ShapeDtypeStruct((D_MODEL, IN_COLS), BF16), jax.ShapeDtypeStruct((D_MODEL, D_MODEL), BF16),
                   jax.ShapeDtypeStruct((SORT_RESIDUES, SORT_ROWS, 3 * COL_BLOCK), F32)),
        compiler_params=_params(("arbitrary", "arbitrary")),
    )(order, x, mix_w, w_in, w_out, rc, rsa, rsb)
    return proj, hn_t, w_in_g, w_out_g, qkv_sorted.reshape(SEQ, 3 * COL_BLOCK)


SCORE_SCALE = HEAD_DIM ** -0.5
ATTN_GROUP_FWD = 16
ATTN_GROUP_BWD = 8
BLOCKS_PER_PATTERN = SEQ // ATTN_BLOCK
SORT_RESIDUES = 16
SORT_ROWS = SEQ // SORT_RESIDUES


def _write_band_bias(bias_ref):
    row = lax.broadcasted_iota(jnp.int32, (2 * ATTN_BLOCK, 2 * ATTN_BLOCK), 0) & (ATTN_BLOCK - 1)
    col = lax.broadcasted_iota(jnp.int32, (2 * ATTN_BLOCK, 2 * ATTN_BLOCK), 1)
    for pi, d in enumerate(DILATIONS):
        per = SORT_RESIDUES // d
        ahead = per * (row % (8 * d) - col % (16 * d)) + (row // (8 * d) - col // (16 * d))
        dist = ATTN_BLOCK + ahead
        bias_ref[2 * pi] = jnp.where((dist >= 0) & (dist <= ATTN_BLOCK), 0.0, NEG_BIG)
        bias_ref[2 * pi + 1] = jnp.where(ahead >= 0, 0.0, NEG_BIG)


def _head0_lanes():
    return lax.broadcasted_iota(jnp.int32, (ATTN_BLOCK, LANES), 1) < HEAD_DIM


def _stack_heads(t, h0):
    return jnp.concatenate([jnp.where(h0, t, 0.0), jnp.where(h0, 0.0, t)], axis=0).astype(BF16)


def _block_runs(i, d):
    nblk = BLOCKS_PER_PATTERN // d
    r, n = i // nblk, i % nblk
    kn = jnp.maximum(n - 1, 0)
    rows, keys = [], []
    for c in range(SORT_RESIDUES // d):
        base = SORT_ROWS * (c * d + r)
        rows.append(pl.ds(pl.multiple_of(base + 8 * d * n, 8), 8 * d))
        keys.append(pl.ds(pl.multiple_of(base + 8 * d * kn, 8), 16 * d))
    return rows, keys, (n == 0).astype(jnp.int32)


def _take(ref, runs):
    return jnp.concatenate([ref[run, :] for run in runs], axis=0)


def _put(ref, runs, value, add=False):
    at = 0
    for run in runs:
        piece = value[at:at + run.size]
        if add:
            ref[run, :] += piece
        else:
            ref[run, :] = piece
        at += run.size


def _sort_rows(src_ref, dst_ref):
    for r in range(SORT_RESIDUES):
        dst_ref[SORT_ROWS * r:SORT_ROWS * (r + 1), :] = src_ref[pl.ds(r, SORT_ROWS, stride=SORT_RESIDUES), :]


def _unsort_rows(src_ref, dst_ref):
    for r in range(SORT_RESIDUES):
        dst_ref[pl.ds(r, SORT_ROWS, stride=SORT_RESIDUES), :] = src_ref[SORT_ROWS * r:SORT_ROWS * (r + 1), :]


def _for_each_group(d, n_group, load, compute, store):
    def group(g, carry):
        items = [load(*_block_runs(g * n_group + u, d)) for u in range(n_group)]
        results = [compute(item) for item in items]
        for item, res in zip(items, results):
            store(item, res)
        return carry

    lax.fori_loop(0, BLOCKS_PER_PATTERN // n_group, group, 0)


def _attn_fwd_fused(qkv_sorted):
    n_pat = len(DILATIONS)
    tile2 = (2 * ATTN_BLOCK, LANES)

    def body(q_ref, k_ref, v_ref, o_ref, lse_ref, o_acc, m_acc, l_acc, bias_ref):
        pl.when(pl.program_id(0) == 0)(lambda: _write_band_bias(bias_ref))
        h0 = _head0_lanes()
        for pi, d in enumerate(DILATIONS):
            first, last = pi == 0, pi == n_pat - 1

            def load(rows, keys, which, first=first, pi=pi):
                item = dict(rows=rows, keys=keys, which=2 * pi + which)
                if not first:
                    item.update(o=_take(o_acc, rows), m=[_take(m_acc.at[h], rows) for h in range(2)],
                                l=[_take(l_acc.at[h], rows) for h in range(2)])
                return item

            def compute(item, first=first):
                kb = _take(k_ref, item["keys"]).astype(BF16)
                vb = _take(v_ref, item["keys"]).astype(BF16)
                s = _mm_nt(_stack_heads(_take(q_ref, item["rows"]) * SCORE_SCALE, h0), kb) + bias_ref[item["which"]]
                mb = jnp.max(s, axis=-1, keepdims=True)
                if first:
                    p = jnp.exp(s - mb)
                    mn = jnp.broadcast_to(mb, tile2)
                else:
                    m_old = jnp.concatenate(item["m"], axis=0)
                    mn = jnp.maximum(m_old, mb)
                    alpha = jnp.exp(m_old - mn)
                    p = jnp.exp(s - jnp.concatenate([mn, mn], axis=1))
                ls = jnp.sum(p, axis=-1, keepdims=True)
                pv = _mm(p.astype(BF16), vb)
                if first:
                    return pv, mn, jnp.broadcast_to(ls, tile2)
                o_old = jnp.concatenate([item["o"], item["o"]], axis=0)
                return alpha * o_old + pv, mn, alpha * jnp.concatenate(item["l"], axis=0) + ls

            def store(item, res, last=last):
                rows = item["rows"]
                (o0, o1), (m0, m1), (l0, l1) = ((a[:ATTN_BLOCK], a[ATTN_BLOCK:]) for a in res)
                if last:
                    _put(o_acc, rows, jnp.where(h0, o0 / l0, o1 / l1))
                    _put(lse_ref, rows, jnp.where(h0, m0 + jnp.log(l0), m1 + jnp.log(l1)))
                else:
                    _put(o_acc, rows, jnp.where(h0, o0, o1))
                    for h, (m, l) in enumerate(((m0, l0), (m1, l1))):
                        _put(m_acc.at[h], rows, m)
                        _put(l_acc.at[h], rows, l)

            _for_each_group(d, ATTN_GROUP_FWD, load, compute, store)
        _unsort_rows(o_acc, o_ref)

    slab = lambda g: pl.BlockSpec((SEQ, LANES), functools.partial(lambda hp, g: (0, 4 * g + hp), g=g))
    wide = jax.ShapeDtypeStruct((SEQ, ATTN_WIDTH), F32)
    return pl.pallas_call(
        body, name="attn_fwd", grid=(4,), out_shape=(wide, wide),
        in_specs=[slab(0), slab(1), slab(2)], out_specs=(slab(0), slab(0)),
        scratch_shapes=[pltpu.VMEM((SEQ, LANES), F32), pltpu.VMEM((2, SEQ, LANES), F32),
                        pltpu.VMEM((2, SEQ, LANES), F32),
                        pltpu.VMEM((2 * len(DILATIONS), 2 * ATTN_BLOCK, 2 * ATTN_BLOCK), F32)],
        compiler_params=_params(("arbitrary",)),
    )(qkv_sorted, qkv_sorted, qkv_sorted)


def _attn_bwd_fused(qkv_sorted, d_out, lse_sorted, delta):
    def body(q_ref, k_ref, v_ref, do_ref, lse_ref, del_ref, dq_ref, dk_ref, dv_ref,
             do_s, del_s, dq_s, dk_s, dv_s, bias_ref):
        pl.when(pl.program_id(0) == 0)(lambda: _write_band_bias(bias_ref))
        _sort_rows(do_ref, do_s)
        _sort_rows(del_ref, del_s)
        dk_s[...] = jnp.zeros_like(dk_s)
        dv_s[...] = jnp.zeros_like(dv_s)
        h0 = _head0_lanes()
        for pi, d in enumerate(DILATIONS):
            first = pi == 0

            def load(rows, keys, which, pi=pi):
                return dict(rows=rows, keys=keys, q=_take(q_ref, rows), g=_take(do_s, rows),
                            lse=_take(lse_ref, rows), delta=_take(del_s, rows),
                            k=_take(k_ref, keys).astype(BF16), v=_take(v_ref, keys).astype(BF16),
                            bias=bias_ref[2 * pi + which])

            def per_head(t):
                swapped = pltpu.roll(t, HEAD_DIM, 1)
                both = jnp.concatenate([jnp.where(h0, t, swapped), jnp.where(h0, swapped, t)], axis=0)
                return jnp.concatenate([both, both], axis=1)

            def compute(item):
                q2, g2 = _stack_heads(item["q"] * SCORE_SCALE, h0), _stack_heads(item["g"], h0)
                s = _mm_nt(q2, item["k"]) + item["bias"]
                p = jnp.exp(s - per_head(item["lse"]))
                dp = _mm_nt(g2, item["v"])
                ds = (p * (dp - per_head(item["delta"]))).astype(BF16)
                dq2 = _mm(ds, item["k"])
                dq = jnp.where(h0, dq2[:ATTN_BLOCK], dq2[ATTN_BLOCK:]) * SCORE_SCALE
                return dq, _mm_tn(ds, q2), _mm_tn(p.astype(BF16), g2)

            def store(item, res, first=first):
                _put(dq_s, item["rows"], res[0], add=not first)
                _put(dk_s, item["keys"], res[1], add=True)
                _put(dv_s, item["keys"], res[2], add=True)

            _for_each_group(d, ATTN_GROUP_BWD, load, compute, store)
        _unsort_rows(dq_s, dq_ref)
        _unsort_rows(dk_s, dk_ref)
        _unsort_rows(dv_s, dv_ref)

    slab = lambda g: pl.BlockSpec((SEQ, LANES), functools.partial(lambda hp, g: (0, 4 * g + hp), g=g))
    wide = jax.ShapeDtypeStruct((SEQ, ATTN_WIDTH), F32)
    sorted_slab = pltpu.VMEM((SEQ, LANES), F32)
    return pl.pallas_call(
        body, name="attn_bwd", grid=(4,), out_shape=(wide, wide, wide),
        scratch_shapes=[sorted_slab] * 5 + [pltpu.VMEM((2 * len(DILATIONS), 2 * ATTN_BLOCK, 2 * ATTN_BLOCK), F32)],
        in_specs=[slab(0), slab(1), slab(2), slab(0), slab(0), slab(0)], out_specs=(slab(0), slab(0), slab(0)),
        compiler_params=_params(("arbitrary",)),
    )(qkv_sorted, qkv_sorted, qkv_sorted, d_out, lse_sorted, delta)


def _hgrn_lower_bound(lb_ref):
    r0, r1 = lb_ref[0:1, :], lb_ref[1:2, :]
    mx = jnp.maximum(r0, r1)
    e0, e1 = jnp.exp(r0 - mx), jnp.exp(r1 - mx)
    return e0 / (e0 + e1)


def _hgrn_gates(hq, hf, lb):
    sq = _sigmoid(hq)
    sg = _sigmoid(hf)
    f = lb + (1.0 - lb) * sg
    return hq * sq, sq, sg, f, 1.0 - f, jnp.log(f)


HGRN_PAIR = 4
HGRN_SEQ_BLOCK = 1024
HGRN_GROUP = 4
HGRN_ROWS = HGRN_GROUP * HGRN_CHUNK


def _hgrn_specs(reverse):
    n_blocks = SEQ // HGRN_SEQ_BLOCK
    width = HGRN_PAIR * HGRN_DIM
    blk = (lambda s: n_blocks - 1 - s) if reverse else (lambda s: s)
    cols = lambda g: pl.BlockSpec((HGRN_SEQ_BLOCK, width),
                                  functools.partial(lambda p, s, g: (blk(s), (HGRN_HEADS // HGRN_PAIR) * g + p), g=g))
    pair = pl.BlockSpec((HGRN_SEQ_BLOCK, width), lambda p, s: (blk(s), p))
    lb = pl.BlockSpec((2, width), lambda p, s: (0, p))
    states = pl.BlockSpec((HGRN_PAIR, HGRN_SEQ_BLOCK // HGRN_CHUNK, HGRN_DIM, HGRN_DIM),
                          lambda p, s: (p, blk(s), 0, 0))
    return cols, pair, lb, states


def _chunk_masks():
    ri = lax.broadcasted_iota(jnp.int32, (HGRN_ROWS, HGRN_ROWS), 0)
    ci = lax.broadcasted_iota(jnp.int32, (HGRN_ROWS, HGRN_ROWS), 1)
    same = (ri // HGRN_CHUNK) == (ci // HGRN_CHUNK)
    return same, same & (ri >= ci), same & (ri <= ci)


def _mm_select(sel, v):
    hi = v.astype(BF16)
    r1 = v - hi.astype(F32)
    mid = r1.astype(BF16)
    lo = (r1 - mid.astype(F32)).astype(BF16)
    return _mm(sel, hi) + _mm(sel, mid) + _mm(sel, lo)


def _head_cols(a, h):
    return a[:, HGRN_DIM * h:HGRN_DIM * (h + 1)]


def _hgrn_fwd(proj, lb_raw):
    t, rws = HGRN_CHUNK, HGRN_ROWS

    def body(hq_ref, hf_ref, hi_ref, lb_ref, rec_ref, st_ref, state):
        @pl.when(pl.program_id(1) == 0)
        def _():
            state[...] = jnp.zeros_like(state)

        lb = _hgrn_lower_bound(lb_ref)
        same, causal, _ = _chunk_masks()
        sel = jnp.concatenate([causal, same], axis=0).astype(BF16)

        def group(g, sts):
            rows = pl.ds(pl.multiple_of(g * rws, rws), rws)
            q, _, _, _, k, lf = _hgrn_gates(hq_ref[rows, :], hf_ref[rows, :], lb)
            sums = _mm_select(sel, lf)
            cum, last = sums[:rws], sums[rws:]
            qd = (q * jnp.exp(cum)).astype(BF16)
            ki = (k * jnp.exp(-cum)).astype(BF16)
            ke = (k * jnp.exp(last - cum)).astype(BF16)
            vb = hi_ref[rows, :].astype(BF16)
            dec = jnp.exp(last)
            new_sts, recs = [], []
            for h in range(HGRN_PAIR):
                qd_h, ke_h, vb_h = _head_cols(qd, h), _head_cols(ke, h), _head_cols(vb, h)
                att = jnp.where(causal, _mm_nt(qd_h, _head_cols(ki, h)), 0.0).astype(BF16)
                intra = _mm(att, vb_h)
                st = sts[h]
                outs = []
                for c in range(HGRN_GROUP):
                    sl = slice(c * t, (c + 1) * t)
                    st_ref[h, g * HGRN_GROUP + c] = st
                    outs.append(intra[sl] + _mm_nt(qd_h[sl], st.astype(BF16)))
                    st = st * _head_cols(dec[c * t:c * t + 1, :], h) + _mm_tn(vb_h[sl], ke_h[sl])
                new_sts.append(st)
                recs.append(jnp.concatenate(outs, axis=0))
            rec_ref[rows, :] = jnp.concatenate(recs, axis=1)
            return tuple(new_sts)

        sts = lax.fori_loop(0, HGRN_SEQ_BLOCK // rws, group, tuple(state[h] for h in range(HGRN_PAIR)))
        for h in range(HGRN_PAIR):
            state[h] = sts[h]

    cols, pair, lb, states = _hgrn_specs(reverse=False)
    return pl.pallas_call(
        body, name="hgrn_fwd", grid=(HGRN_HEADS // HGRN_PAIR, SEQ // HGRN_SEQ_BLOCK),
        out_shape=(jax.ShapeDtypeStruct((SEQ, HGRN_WIDTH), F32),
                   jax.ShapeDtypeStruct((HGRN_HEADS, N_CHUNKS, HGRN_DIM, HGRN_DIM), F32)),
        in_specs=[cols(4), cols(5), cols(6), lb], out_specs=(pair, states),
        scratch_shapes=[pltpu.VMEM((HGRN_PAIR, HGRN_DIM, HGRN_DIM), F32)],
        compiler_params=_params(("parallel", "arbitrary")),
    )(proj, proj, proj, lb_raw)


def _hgrn_bwd(proj, lb_raw, d_rec, states):
    t, rws = HGRN_CHUNK, HGRN_ROWS

    def body(hq_ref, hf_ref, hi_ref, lb_ref, do_ref, st_ref, dhq_ref, dhf_ref, dhi_ref, dlb_ref,
             dstate, dlb_acc):
        lb = _hgrn_lower_bound(lb_ref)
        same, causal, anti = _chunk_masks()
        sel = jnp.concatenate([causal, same], axis=0).astype(BF16)
        sel_t = jnp.concatenate([anti, same], axis=1).astype(BF16)
        @pl.when(pl.program_id(1) == 0)
        def _():
            dstate[...] = jnp.zeros_like(dstate)
            dlb_acc[...] = jnp.zeros_like(dlb_acc)

        n_groups = HGRN_SEQ_BLOCK // rws
        chunks = [slice(c * t, (c + 1) * t) for c in range(HGRN_GROUP)]

        def group(i, dsts_in):
            g = n_groups - 1 - i
            rows = pl.ds(pl.multiple_of(g * rws, rws), rws)
            hq = hq_ref[rows, :]
            q, sq, sg, f, k, lf = _hgrn_gates(hq, hf_ref[rows, :], lb)
            sums = _mm_select(sel, lf)
            cum, last = sums[:rws], sums[rws:]
            e_cum, e_inv, e_end, dec = jnp.exp(cum), jnp.exp(-cum), jnp.exp(last - cum), jnp.exp(last)
            qd, ki, ke = q * e_cum, k * e_inv, k * e_end
            qdb, kib, keb = qd.astype(BF16), ki.astype(BF16), ke.astype(BF16)
            vb = hi_ref[rows, :].astype(BF16)
            gb = do_ref[rows, :].astype(BF16)

            dsts_out, per_head = [], []
            for h in range(HGRN_PAIR):
                qdb_h, kib_h, keb_h = _head_cols(qdb, h), _head_cols(kib, h), _head_cols(keb, h)
                vb_h, gb_h = _head_cols(vb, h), _head_cols(gb, h)
                att = jnp.where(causal, _mm_nt(qdb_h, kib_h), 0.0).astype(BF16)
                datt = jnp.where(causal, _mm_nt(gb_h, vb_h), 0.0).astype(BF16)
                dv = _mm_tn(att, gb_h)
                dqd = _mm(datt, kib_h)
                dki = _mm_tn(datt, qdb_h)

                decs = [_head_cols(dec[c * t:c * t + 1, :], h) for c in range(HGRN_GROUP)]
                dsts = [None] * HGRN_GROUP
                dst = dsts_in[h]
                for c in reversed(range(HGRN_GROUP)):
                    dsts[c] = dst
                    dst = dst * decs[c] + _mm_tn(gb_h[chunks[c]], qdb_h[chunks[c]])
                dsts_out.append(dst)

                dv_x, dqd_x, dke, dlast_x = [], [], [], []
                for c, sl in enumerate(chunks):
                    st_prev = st_ref[h, g * HGRN_GROUP + c]
                    dstb = dsts[c].astype(BF16)
                    dv_x.append(_mm_nt(keb_h[sl], dstb))
                    dqd_x.append(_mm(gb_h[sl], st_prev.astype(BF16)))
                    dke.append(_mm(vb_h[sl], dstb))
                    ddec = jnp.sum(dsts[c] * st_prev, axis=0, keepdims=True)
                    dlast_x.append(jnp.broadcast_to(ddec * decs[c], (t, HGRN_DIM)))
                per_head.append((dv + jnp.concatenate(dv_x, axis=0), dqd + jnp.concatenate(dqd_x, axis=0),
                                 dki, jnp.concatenate(dke, axis=0), jnp.concatenate(dlast_x, axis=0)))
            dv, dqd, dki, dke, dlast = (jnp.concatenate(list(parts), axis=1) for parts in zip(*per_head))

            dq = dqd * e_cum
            dk = dki * e_inv + dke * e_end
            dke_ke = dke * ke
            dcum = dqd * qd - dki * ki - dke_ke
            dlf = _mm_select(sel_t, jnp.concatenate([dcum, dke_ke], axis=0)) + dlast
            df = dlf / f - dk
            dhq_ref[rows, :] = dq * (sq * (1.0 + hq * (1.0 - sq)))
            dhf_ref[rows, :] = df * (1.0 - lb) * (sg * (1.0 - sg))
            dhi_ref[rows, :] = dv
            dlb_acc[...] += jnp.sum(df * (1.0 - sg), axis=0, keepdims=True)
            return tuple(dsts_out)

        dsts = lax.fori_loop(0, n_groups, group, tuple(dstate[h] for h in range(HGRN_PAIR)))
        for h in range(HGRN_PAIR):
            dstate[h] = dsts[h]
        g0 = dlb_acc[...] * lb * (1.0 - lb)
        dlb_ref[...] = jnp.concatenate([g0, -g0], axis=0)

    cols, pair, lb_spec, st_spec = _hgrn_specs(reverse=True)
    wide = jax.ShapeDtypeStruct((SEQ, HGRN_WIDTH), F32)
    return pl.pallas_call(
        body, name="hgrn_bwd", grid=(HGRN_HEADS // HGRN_PAIR, SEQ // HGRN_SEQ_BLOCK),
        out_shape=(wide, wide, wide, jax.ShapeDtypeStruct((2, HGRN_WIDTH), F32)),
        in_specs=[cols(4), cols(5), cols(6), lb_spec, pair, st_spec],
        out_specs=(pair, pair, pair, lb_spec),
        scratch_shapes=[pltpu.VMEM((HGRN_PAIR, HGRN_DIM, HGRN_DIM), F32),
                        pltpu.VMEM((1, HGRN_PAIR * HGRN_DIM), F32)],
        compiler_params=_params(("parallel", "arbitrary")),
    )(proj, proj, proj, lb_raw, d_rec, states)


def _group_sum(v, group):
    parts = []
    for s in range(v.shape[1] // LANES):
        slab = v[:, LANES * s:LANES * (s + 1)]
        if group == LANES:
            parts.append(jnp.broadcast_to(jnp.sum(slab, axis=-1, keepdims=True), slab.shape))
        else:
            h0 = lax.broadcasted_iota(jnp.int32, slab.shape, 1) < HEAD_DIM
            s0 = jnp.sum(jnp.where(h0, slab, 0.0), axis=-1, keepdims=True)
            s1 = jnp.sum(jnp.where(h0, 0.0, slab), axis=-1, keepdims=True)
            parts.append(jnp.where(h0, s0, s1))
    return jnp.concatenate(parts, axis=1)


def _mid(attn_o, rec, proj, x, target, w_out_g, attn_w, hgrn_w, final_w):
    tm = 256

    def branch_fwd(o, gate, w, group):
        r = lax.rsqrt(_group_sum(o * o, group) * (1.0 / group) + NORM_EPS)
        nrm = o * r
        sg = _sigmoid(gate)
        return r, nrm, sg, nrm * w * (gate * sg)

    def branch_bwd(dy, r, nrm, sg, gate, w, group):
        silu = gate * sg
        d_gate = dy * nrm * w * (sg * (1.0 + gate * (1.0 - sg)))
        d_w = jnp.sum(dy * nrm * silu, axis=0, keepdims=True)
        dn = dy * w * silu
        d_o = r * (dn - nrm * (_group_sum(dn * nrm, group) * (1.0 / group)))
        return d_o, d_gate, d_w

    def body(o_ref, rec_ref, ag_ref, hg_ref, x_ref, tgt_ref, wout_ref, aw_ref, hw_ref, fw_ref,
             dx2_ref, do_ref, delta_ref, dag_ref, drec_ref, dhg_ref, dwout_ref, dfw_ref, daw_ref, dhw_ref,
             loss_ref, dwout_acc):
        i = pl.program_id(0)

        @pl.when(i == 0)
        def _():
            dwout_acc[...] = jnp.zeros_like(dwout_acc)
            dfw_ref[...] = jnp.zeros_like(dfw_ref)
            daw_ref[...] = jnp.zeros_like(daw_ref)
            dhw_ref[...] = jnp.zeros_like(dhw_ref)
            loss_ref[...] = jnp.zeros_like(loss_ref)

        o, rc, ag, hg = o_ref[...], rec_ref[...], ag_ref[...], hg_ref[...]
        aw, hw, fw = aw_ref[...], hw_ref[...], fw_ref[...]
        ra, na, sga, ya = branch_fwd(o, ag, aw, HEAD_DIM)
        rh, nh, sgh, yh = branch_fwd(rc, hg, hw, HGRN_DIM)
        mixed = jnp.concatenate([ya, yh], axis=1).astype(BF16)
        wout = wout_ref[...]
        x2 = x_ref[...] + _mm(mixed, wout)
        rstd = lax.rsqrt(jnp.mean(x2 * x2, axis=-1, keepdims=True) + NORM_EPS)
        xn = x2 * rstd
        err = xn * fw - tgt_ref[...]
        row_loss = jnp.mean(err * err, axis=-1, keepdims=True)
        loss_ref[...] += 0.5 * jnp.sum(row_loss, axis=0, keepdims=True)
        dy = err * (1.0 / D_MODEL)
        dfw_ref[...] += jnp.sum(dy * xn, axis=0, keepdims=True)
        dxn = dy * fw
        dx2 = rstd * (dxn - xn * jnp.mean(dxn * xn, axis=-1, keepdims=True))
        dx2_ref[...] = dx2
        dx2b = dx2.astype(BF16)
        dwout_acc[...] += _mm_tn(mixed, dx2b)

        @pl.when(i == pl.num_programs(0) - 1)
        def _():
            dwout_ref[...] = dwout_acc[...].astype(BF16)

        dmixed = _mm_nt(dx2b, wout)

        d_o, d_ag, d_aw = branch_bwd(dmixed[:, :ATTN_WIDTH], ra, na, sga, ag, aw, HEAD_DIM)
        d_rec, d_hg, d_hw = branch_bwd(dmixed[:, ATTN_WIDTH:], rh, nh, sgh, hg, hw, HGRN_DIM)
        do_ref[...] = d_o
        delta_ref[...] = _group_sum(d_o * o, HEAD_DIM)
        dag_ref[...] = d_ag
        drec_ref[...] = d_rec
        dhg_ref[...] = d_hg
        daw_ref[...] += d_aw
        dhw_ref[...] += d_hw

    half = lambda: pl.BlockSpec((tm, COL_BLOCK), lambda i: (i, 0))
    full = lambda: pl.BlockSpec((tm, D_MODEL), lambda i: (i, 0))
    fixed = lambda r, c: pl.BlockSpec((r, c), lambda i: (0, 0))
    wide = jax.ShapeDtypeStruct((SEQ, COL_BLOCK), F32)
    return pl.pallas_call(
        body, name="mid", grid=(SEQ // tm,),
        out_shape=(jax.ShapeDtypeStruct((SEQ, D_MODEL), F32), wide, wide, wide, wide, wide,
                   jax.ShapeDtypeStruct((D_MODEL, D_MODEL), BF16),
                   jax.ShapeDtypeStruct((1, D_MODEL), F32), jax.ShapeDtypeStruct((1, COL_BLOCK), F32),
                   jax.ShapeDtypeStruct((1, COL_BLOCK), F32), jax.ShapeDtypeStruct((1, 1), F32)),
        scratch_shapes=[pltpu.VMEM((D_MODEL, D_MODEL), F32)],
        in_specs=[half(), half(),
                  pl.BlockSpec((tm, COL_BLOCK), lambda i: (i, 3)), pl.BlockSpec((tm, COL_BLOCK), lambda i: (i, 7)),
                  full(), full(), fixed(D_MODEL, D_MODEL), fixed(1, COL_BLOCK), fixed(1, COL_BLOCK),
                  fixed(1, D_MODEL)],
        out_specs=(full(), half(), half(), half(), half(), half(), fixed(D_MODEL, D_MODEL),
                   fixed(1, D_MODEL), fixed(1, COL_BLOCK), fixed(1, COL_BLOCK), fixed(1, 1)),
        compiler_params=_params(("arbitrary",)),
    )(attn_o, rec, proj, proj, x, target, w_out_g, attn_w, hgrn_w, final_w)


def _in_proj_bwd_rows(d_groups, w_g, x, dx2, mix_w, rc, rsa, rsb):
    tm = 256

    def body(*refs):
        dg_refs = refs[:N_DEV]
        wg_ref, x_ref, dx2_ref, w_ref, c_ref, sa_ref, sb_ref, gx_ref, dpb_ref, dmw_ref = refs[N_DEV:]

        @pl.when(pl.program_id(0) == 0)
        def _():
            dmw_ref[...] = jnp.zeros_like(dmw_ref)

        parts = []
        for j in range(N_DEV):
            dp = dg_refs[j][...]
            if j < 2:
                dp = _rot_transposed(dp, c_ref[...], sa_ref[...], sb_ref[...])
            parts.append(dp.astype(BF16))
        dpb = jnp.concatenate(parts, axis=1)
        dpb_ref[...] = dpb
        g = _mm_nt(dpb, wg_ref[...])
        xf = x_ref[...]
        rstd = lax.rsqrt(jnp.mean(xf * xf, axis=-1, keepdims=True) + NORM_EPS)
        xn = xf * rstd
        dmw_ref[...] += jnp.sum(g * xn, axis=0, keepdims=True)
        gw = g * w_ref[...]
        gx_ref[...] = dx2_ref[...] + rstd * (gw - xn * jnp.mean(gw * xn, axis=-1, keepdims=True))

    tile = lambda cols: pl.BlockSpec((tm, cols), lambda i: (i, 0))
    fixed = lambda r, c: pl.BlockSpec((r, c), lambda i: (0, 0))
    return pl.pallas_call(
        body, name="in_proj_bwd_rows", grid=(SEQ // tm,),
        out_shape=(jax.ShapeDtypeStruct((SEQ, D_MODEL), F32), jax.ShapeDtypeStruct((SEQ, IN_COLS), BF16),
                   jax.ShapeDtypeStruct((1, D_MODEL), F32)),
        in_specs=[tile(COL_BLOCK) for _ in range(N_DEV)] + [
            pl.BlockSpec((D_MODEL, IN_COLS), lambda i: (0, 0), pipeline_mode=pl.Buffered(1)),
            tile(D_MODEL), tile(D_MODEL), fixed(1, D_MODEL), tile(LANES), tile(LANES), tile(LANES)],
        out_specs=(tile(D_MODEL), tile(IN_COLS), fixed(1, D_MODEL)),
        compiler_params=_params(("arbitrary",)),
    )(*d_groups, w_g, x, dx2, mix_w, rc, rsa, rsb)


def _weights_exchange(hn_t, dproj_b, dwout_p, small_p):
    n_chips = N_DEV // 2
    rb = 128
    S1_IN, S1_OUT, SMALL, S2_IN, S2_OUT = 0, 4, 8, 15, 18
    rel_of_pair = (1, 2, 3, 0)

    def body(order_ref, hnt_ref, dp_ref, dwout_ref, small_ref, gin_ref, gout_ref, gs_ref,
             part, s1_send, s1_in, s1_out, fwd_in, fwd_out, s2_in, s2_out, land_s, send_sems, recv_sems):
        t = pl.program_id(0)
        me = _my_place()
        x, y, c = me
        my_chip = 2 * x + y
        sibling = (x, y, 1 - c)

        def remote(slot, src, dst, to):
            return pltpu.make_async_remote_copy(src_ref=src, dst_ref=dst, send_sem=send_sems.at[slot],
                                                recv_sem=recv_sems.at[slot], device_id=to, device_id_type=MESH)

        def s1_in_copy(pair):
            return remote(S1_IN + pair, s1_send.at[pair], s1_in.at[pair], sibling)

        def s1_out_copy(pair):
            q = my_chip ^ rel_of_pair[pair]
            return remote(S1_OUT + pair, dwout_ref.at[q, 1 - c], s1_out.at[pair], sibling)

        def s2_copies(rel):
            peer = _peer(me, 2 * rel)
            return [remote(S2_IN + rel - 1, fwd_in.at[rel - 1], s2_in.at[rel - 1], peer),
                    remote(S2_OUT + rel - 1, fwd_out.at[rel - 1], s2_out.at[rel - 1], peer)]

        def small_copy(rel):
            return remote(SMALL + rel - 1, small_ref, land_s.at[rel], _peer(me, rel))

        @pl.when(t == 0)
        def _():
            land_s[0] = small_ref[...]
            for pair in range(n_chips):
                s1_out_copy(pair).start()
            for rel in range(1, N_DEV):
                small_copy(rel).start()

        part[...] = _mm(hnt_ref[...], dp_ref[...])

        def rows_loop(n_rows, fn):
            def step(b, carry):
                fn(pl.ds(pl.multiple_of(b * rb, rb), rb))
                return carry
            lax.fori_loop(0, n_rows // rb, step, 0)

        for pair, rel in enumerate(rel_of_pair):
            @pl.when(t == 2 * pair)
            def _(pair=pair):
                s1_send[pair] = part[...].astype(BF16)
                s1_in_copy(pair).start()

            @pl.when(t == 2 * pair + 1)
            def _(pair=pair, rel=rel):
                q = my_chip ^ rel
                s1_in_copy(pair).wait_recv()
                s1_out_copy(pair).wait_recv()
                dst_in = fwd_in.at[rel - 1] if rel else gin_ref
                dst_out = fwd_out.at[rel - 1] if rel else gout_ref

                def add_in(rows):
                    dst_in[rows, :] = (part[rows, :] + s1_in[pair, rows, :].astype(F32)).astype(dst_in.dtype)

                def add_out(rows):
                    dst_out[rows, :] = (dwout_ref[q, c, rows, :].astype(F32)
                                        + s1_out[pair, rows, :].astype(F32)).astype(dst_out.dtype)

                rows_loop(D_MODEL, add_in)
                rows_loop(WOUT_ROWS, add_out)
                if rel:
                    for cp in s2_copies(rel):
                        cp.start()

        @pl.when(t == N_DEV - 1)
        def _():
            for rel in range(1, n_chips):
                for cp in s2_copies(rel):
                    cp.wait_recv()

            def total_in(rows):
                g = gin_ref[rows, :]
                for rel in range(1, n_chips):
                    g = g + s2_in[rel - 1, rows, :].astype(F32)
                gin_ref[rows, :] = g

            def total_out(rows):
                g = gout_ref[rows, :]
                for rel in range(1, n_chips):
                    g = g + s2_out[rel - 1, rows, :].astype(F32)
                gout_ref[rows, :] = g

            rows_loop(D_MODEL, total_in)
            rows_loop(WOUT_ROWS, total_out)

            for rel in range(1, N_DEV):
                small_copy(rel).wait_recv()
            my_flat = _flat(me)
            g = land_s[my_flat ^ 0]
            for dev in range(1, N_DEV):
                g = g + land_s[my_flat ^ dev]
            gs_ref[...] = g

            for pair in range(n_chips):
                s1_in_copy(pair).wait_send()
                s1_out_copy(pair).wait_send()
            for rel in range(1, n_chips):
                for cp in s2_copies(rel):
                    cp.wait_send()
            for rel in range(1, N_DEV):
                small_copy(rel).wait_send()

    place_x, place_y, place_c = _my_place()
    my_chip = 2 * place_x + place_y
    order = jnp.stack([2 * (my_chip ^ rel) + core for rel in rel_of_pair
                       for core in (1 - place_c, place_c)]).astype(jnp.int32)

    whole = lambda: pl.BlockSpec(memory_space=pltpu.VMEM)
    in_blocks = lambda n: pltpu.VMEM((n, D_MODEL, COL_BLOCK), BF16)
    out_blocks = lambda n: pltpu.VMEM((n, WOUT_ROWS, D_MODEL), BF16)
    grid_spec = pltpu.PrefetchScalarGridSpec(
        num_scalar_prefetch=1, grid=(N_DEV,),
        in_specs=[pl.BlockSpec((D_MODEL, SEQ), lambda t, order: (0, 0), pipeline_mode=pl.Buffered(1)),
                  pl.BlockSpec((SEQ, COL_BLOCK), lambda t, order: (0, order[t])), whole(), whole()],
        out_specs=(whole(), whole(), whole()),
        scratch_shapes=[pltpu.VMEM((D_MODEL, COL_BLOCK), F32), in_blocks(n_chips), in_blocks(n_chips),
                        out_blocks(n_chips), in_blocks(n_chips - 1), out_blocks(n_chips - 1),
                        in_blocks(n_chips - 1), out_blocks(n_chips - 1),
                        pltpu.VMEM((N_DEV, SMALL_ROWS, LANES), F32),
                        pltpu.SemaphoreType.DMA((21,)), pltpu.SemaphoreType.DMA((21,))])
    return pl.pallas_call(
        body, name="weights_exchange", grid_spec=grid_spec,
        out_shape=(jax.ShapeDtypeStruct((D_MODEL, COL_BLOCK), F32), jax.ShapeDtypeStruct((WOUT_ROWS, D_MODEL), F32),
                   jax.ShapeDtypeStruct((SMALL_ROWS, LANES), F32)),
        compiler_params=_params(("arbitrary",)),
    )(order, hn_t, dproj_b, dwout_p.reshape(n_chips, 2, WOUT_ROWS, D_MODEL), small_p)


def _adamw(w, g, m, v):
    m = ADAM_B1 * m + (1.0 - ADAM_B1) * g
    v = ADAM_B2 * v + (1.0 - ADAM_B2) * (g * g)
    m_hat = m / (1.0 - ADAM_B1 ** ADAM_STEP)
    v_hat = v / (1.0 - ADAM_B2 ** ADAM_STEP)
    delta = -ADAM_LR * (m_hat / (jnp.sqrt(v_hat) + ADAM_EPS) + ADAM_WD * w)
    return delta, m, v


def _adamw_update(grads, weights, m_old, v_old):
    rb = 256

    def body(*refs):
        g_refs, w_refs, m_refs, v_refs = refs[0:3], refs[3:6], refs[6:9], refs[9:12]
        d_refs, nm_refs, nv_refs = refs[12:15], refs[15:18], refs[18:21]
        for k in range(3):
            n_rows = g_refs[k].shape[0]
            step_rows = min(rb, n_rows)

            def step(b, carry, k=k, step_rows=step_rows):
                rows = pl.ds(pl.multiple_of(b * step_rows, 8), step_rows)
                delta, nm, nv = _adamw(w_refs[k][rows, :], g_refs[k][rows, :], m_refs[k][rows, :], v_refs[k][rows, :])
                d_refs[k][rows, :] = delta
                nm_refs[k][rows, :] = nm
                nv_refs[k][rows, :] = nv
                return carry

            lax.fori_loop(0, n_rows // step_rows, step, 0)

    shapes = tuple(jax.ShapeDtypeStruct(g.shape, F32) for g in grads)
    vm = lambda: pl.BlockSpec(memory_space=pltpu.VMEM)
    outs = pl.pallas_call(
        body, name="adamw_update", out_shape=shapes * 3,
        in_specs=[vm() for _ in range(12)], out_specs=tuple(vm() for _ in range(9)),
        compiler_params=_params(),
    )(*grads, *weights, *m_old, *v_old)
    return outs[0:3], outs[3:6], outs[6:9]


def _pack_small(mix, attn, hgrn, lb, final, loss=None):
    def rows8(a):
        a = a.reshape(-1, LANES)
        return jnp.pad(a, ((0, 8 - a.shape[0]), (0, 0)))
    last = jnp.zeros((8, LANES), F32) if loss is None else jnp.pad(loss.reshape(1, 1), ((0, 7), (0, LANES - 1)))
    return jnp.concatenate([rows8(mix), rows8(attn), rows8(hgrn), rows8(lb), rows8(final), last], axis=0)


def _unpack_small(slab):
    return (slab[ROW_MIX:ROW_MIX + 8].reshape(1, D_MODEL), slab[ROW_ATTN:ROW_ATTN + 4].reshape(1, ATTN_WIDTH),
            slab[ROW_HGRN:ROW_HGRN + 4].reshape(1, HGRN_WIDTH), slab[ROW_LB:ROW_LB + 8].reshape(2, HGRN_WIDTH),
            slab[ROW_FINAL:ROW_FINAL + 8].reshape(D_MODEL))


def _rope(pos_row):
    j = np.arange(ROPE_ROWS)
    inv = np.where(j < ROPE_HALF, ROPE_THETA ** (-(j % ROPE_HALF) * (2.0 / ROPE_DIMS)), 0.0)
    e = np.arange(LANES) % HEAD_DIM
    hit = (j[:, None] == (e % ROPE_HALF)[None, :]) & (j[:, None] < ROPE_HALF)
    sel = np.stack([hit & (e < ROPE_DIMS), hit & (e >= ROPE_HALF) & (e < ROPE_DIMS),
                    -1.0 * (hit & (e < ROPE_HALF))]).astype(np.float32)
    return _rope_tables(pos_row, jnp.asarray(inv.astype(np.float32).reshape(ROPE_ROWS, 1)),
                        jnp.asarray(sel, dtype=BF16))


def _local_step(x, proj, qkv_sorted, w_in_g, w_out_g, tables, mix_w, attn_w, hgrn_w, lb_raw, final_w, target):
    rc, rsa, rsb = tables
    attn_o, lse = _attn_fwd_fused(qkv_sorted)
    rec, states = _hgrn_fwd(proj, lb_raw)

    (dx2, d_o, delta, d_ag, d_rec, d_hg, dwout_p, d_final, d_attn_w, d_hgrn_w, loss) = _mid(
        attn_o, rec, proj, x, target, w_out_g, attn_w, hgrn_w, final_w.reshape(1, D_MODEL))

    dqkv = _attn_bwd_fused(qkv_sorted, d_o, lse, delta)
    d_hq, d_hf, d_hi, d_lb = _hgrn_bwd(proj, lb_raw, d_rec, states)

    grad_x, dproj_b, d_mix = _in_proj_bwd_rows(
        (dqkv[0], dqkv[1], dqkv[2], d_ag, d_hq, d_hf, d_hi, d_hg), w_in_g, x, dx2, mix_w, rc, rsa, rsb)
    small_p = _pack_small(d_mix, d_attn_w, d_hgrn_w, d_lb, d_final, loss)
    return grad_x, dproj_b, dwout_p, small_p


def kernel(x, positions, w_in, w_out, mix_norm_w, attn_out_norm_w, hgrn_out_norm_w, hgrn_lb_raw, final_norm_w, loss_target, m_w_in, m_w_out, m_mix_norm_w, m_attn_out_norm_w, m_hgrn_out_norm_w, m_hgrn_lb_raw, m_final_norm_w, v_w_in, v_w_out, v_mix_norm_w, v_attn_out_norm_w, v_hgrn_out_norm_w, v_hgrn_lb_raw, v_final_norm_w):
    tables = _rope(positions)
    proj, hn_t, w_in_g, w_out_g, qkv_sorted = _gather_project(x[0], mix_norm_w, w_in[0], w_out[0], *tables)
    grad_x, dproj_b, dwout_p, small_p = _local_step(
        x[0], proj, qkv_sorted, w_in_g, w_out_g, tables, mix_norm_w, attn_out_norm_w, hgrn_out_norm_w,
        hgrn_lb_raw, final_norm_w, loss_target[0])
    g_in, g_out, g_s = _weights_exchange(hn_t, dproj_b, dwout_p, small_p)

    w_s = _pack_small(mix_norm_w, attn_out_norm_w, hgrn_out_norm_w, hgrn_lb_raw, final_norm_w)
    m_s = _pack_small(m_mix_norm_w, m_attn_out_norm_w, m_hgrn_out_norm_w, m_hgrn_lb_raw, m_final_norm_w)
    v_s = _pack_small(v_mix_norm_w, v_attn_out_norm_w, v_hgrn_out_norm_w, v_hgrn_lb_raw, v_final_norm_w)
    (d_in, d_out, d_s), (nm_in, nm_out, nm_s), (nv_in, nv_out, nv_s) = _adamw_update(
        (g_in, g_out, g_s), (w_in[0], w_out[0], w_s), (m_w_in[0], m_w_out[0], m_s), (v_w_in[0], v_w_out[0], v_s))

    loss = g_s[ROW_LOSS, 0]
    return (loss, grad_x[None], g_in[None], g_out[None], *_unpack_small(g_s),
            d_in[None], d_out[None], *_unpack_small(d_s),
            nm_in[None], nm_out[None], *_unpack_small(nm_s),
            nv_in[None], nv_out[None], *_unpack_small(nv_s))
```

```python
import functools

import jax
import jax.numpy as jnp
import numpy as np
from jax import lax
from jax.experimental import pallas as pl
from jax.experimental.pallas import tpu as pltpu

F32 = jnp.float32
BF16 = jnp.bfloat16

SEQ = 4096
D_MODEL = 1024
ATTN_WIDTH = 512
HGRN_WIDTH = 512
HEAD_DIM = 64
HGRN_HEADS = 4
HGRN_DIM = 128
HGRN_CHUNK = 64
N_CHUNKS = SEQ // HGRN_CHUNK
IN_COLS = 4096
COL_BLOCK = 512
N_DEV = 8
WOUT_ROWS = D_MODEL // N_DEV
ATTN_BLOCK = 128
DILATIONS = (1, 4, 16)
ROPE_THETA = 500000.0
ROPE_DIMS = 16
ROPE_HALF = 8
NORM_EPS = 1e-6
NEG_BIG = -1e30
LANES = 128

ADAM_LR = 0.001
ADAM_B1 = 0.9
ADAM_B2 = 0.999
ADAM_EPS = 1e-08
ADAM_WD = 0.01
ADAM_STEP = 10

SMALL_ROWS = 48
ROW_MIX, ROW_ATTN, ROW_HGRN, ROW_LB, ROW_FINAL, ROW_LOSS = 0, 8, 16, 24, 32, 40

VMEM_LIMIT = 56 * 1024 * 1024
MESH = pl.DeviceIdType.MESH


def _mm(a, b):
    return lax.dot_general(a, b, (((1,), (0,)), ((), ())), preferred_element_type=F32)


def _mm_nt(a, b):
    return lax.dot_general(a, b, (((1,), (1,)), ((), ())), preferred_element_type=F32)


def _mm_tn(a, b):
    return lax.dot_general(a, b, (((0,), (0,)), ((), ())), preferred_element_type=F32)


def _mm_exact(a, b):
    return lax.dot_general(a, b, (((1,), (0,)), ((), ())), preferred_element_type=F32,
                           precision=lax.Precision.HIGHEST)


def _sigmoid(v):
    return 1.0 / (1.0 + jnp.exp(-v))


def _params(sem=None, **kw):
    return pltpu.CompilerParams(dimension_semantics=sem, vmem_limit_bytes=VMEM_LIMIT, **kw)


def _my_place():
    return lax.axis_index("x"), lax.axis_index("y"), lax.axis_index("c")


def _peer(place, rel):
    x, y, c = place
    return (x ^ ((rel >> 2) & 1), y ^ ((rel >> 1) & 1), c ^ (rel & 1))


def _flat(place):
    x, y, c = place
    return 4 * x + 2 * y + c


ROPE_ROWS = 16


def _rope_tables(pos_row, inv_freq_col, selectors):
    def body(pos_ref, invf_ref, sel_ref, c_ref, sa_ref, sb_ref):
        ang = pos_ref[...].astype(F32) * invf_ref[...]
        cos, sin = jnp.cos(ang), jnp.sin(ang)

        def spread(v, sel):
            hi = v.astype(BF16)
            r1 = v - hi.astype(F32)
            mid = r1.astype(BF16)
            lo = (r1 - mid.astype(F32)).astype(BF16)
            return _mm_tn(hi, sel) + _mm_tn(mid, sel) + _mm_tn(lo, sel)

        e = lax.broadcasted_iota(jnp.int32, (1, LANES), 1) & (HEAD_DIM - 1)
        c_ref[...] = spread(cos, sel_ref[0]) + jnp.where(e < ROPE_DIMS, 0.0, 1.0)
        sa_ref[...] = spread(sin, sel_ref[1])
        sb_ref[...] = spread(sin, sel_ref[2])

    tab = jax.ShapeDtypeStruct((SEQ, LANES), F32)
    vm = lambda: pl.BlockSpec(memory_space=pltpu.VMEM)
    return pl.pallas_call(
        body, name="rope_tables", out_shape=(tab, tab, tab),
        in_specs=[vm(), vm(), vm()], out_specs=(vm(), vm(), vm()), compiler_params=_params(),
    )(pos_row, inv_freq_col, selectors)


def _per_slab(fn, t):
    return jnp.concatenate([fn(t[:, LANES * s:LANES * (s + 1)]) for s in range(t.shape[1] // LANES)], axis=1)


def _rot(t, c, sa, sb):
    return _per_slab(lambda u: u * c + pltpu.roll(u, ROPE_HALF, 1) * sa + pltpu.roll(u, LANES - ROPE_HALF, 1) * sb, t)


def _rot_transposed(g, c, sa, sb):
    return _per_slab(
        lambda u: u * c + pltpu.roll(u * sa, LANES - ROPE_HALF, 1) + pltpu.roll(u * sb, ROPE_HALF, 1), g)


def _gather_project(x, mix_w, w_in, w_out, rc, rsa, rsb):
    tm = 1024
    n_tiles = SEQ // tm
    arrival_of_step = (None, 0, 1, 2, 4, 5, 3, 6)

    def body(order_ref, x_ref, w_ref, win_ref, wout_ref, c_ref, sa_ref, sb_ref,
             proj_ref, hnt_ref, gin_hbm, gout_hbm, qkv_hbm,
             hn_s, w_land, wout_land, stage, sort_stage, slab_tmp, send_sems, recv_sems, local_sems):
        g, i = pl.program_id(0), pl.program_id(1)
        me = _my_place()
        x_, y_, c_ = me
        sibling = (x_, y_, 1 - c_)
        chips = [(1 - x_, y_), (x_, 1 - y_), (1 - x_, 1 - y_)]

        def slab(which, place):
            idx = _flat(place)
            if which == 0:
                return w_land.at[idx]
            return wout_land.at[pl.ds(pl.multiple_of(idx * WOUT_ROWS, WOUT_ROWS), WOUT_ROWS), :]

        def remote(which, k, ref, to, src=None):
            return pltpu.make_async_remote_copy(
                src_ref=ref if src is None else src, dst_ref=ref, send_sem=send_sems.at[8 * which + k],
                recv_sem=recv_sems.at[8 * which + k], device_id=to, device_id_type=MESH)

        def copy(which, k, block, to, src=None):
            return remote(which, k, slab(which, block), to, src)

        def half(which, place, part):
            n = (D_MODEL if which == 0 else WOUT_ROWS) // 2
            if which == 0:
                return w_land.at[_flat(place), pl.ds(n * part, n), :]
            return wout_land.at[pl.ds(pl.multiple_of(_flat(place) * WOUT_ROWS + n * part, n), n), :]

        def first_copies(which):
            src = stage if which == 0 else None
            return ([copy(which, 0, me, sibling, src)]
                    + [copy(which, 1 + j, me, (*chips[j], c_), src) for j in range(2)])

        def relay(which, part):
            frm, to = (chips[1], chips[0]) if part == 0 else (chips[0], chips[1])
            return remote(which, 3 if part == 0 else 7, half(which, (*frm, c_), part), (*to, c_))

        def two_hop_half(which, part):
            return remote(which, 3 if part == 0 else 7, half(which, (*chips[2], c_), part), me)

        def pass_on(which, j):
            return copy(which, 4 + j, (*chips[j], c_), sibling)

        def arrival(which, k):
            if k == 0:
                return copy(which, 0, sibling, me)
            if k <= 2:
                return copy(which, k, (*chips[k - 1], c_), me)
            return copy(which, k, (*chips[k - 4], 1 - c_), me)

        def to_hbm(step):
            idx = order_ref[step]
            cols = pl.ds(pl.multiple_of(idx * COL_BLOCK, COL_BLOCK), COL_BLOCK)
            return pltpu.make_async_copy(w_land.at[idx], gin_hbm.at[:, cols], local_sems.at[step])

        @pl.when((g == 0) & (i == 0))
        def _():
            stage[...] = win_ref[...].astype(BF16)
            w_land[_flat(me)] = stage[...]
            wout_land[pl.ds(pl.multiple_of(_flat(me) * WOUT_ROWS, WOUT_ROWS), WOUT_ROWS), :] = (
                wout_ref[...].astype(BF16))
            for cp in first_copies(0) + first_copies(1)[:1]:
                cp.start()
            to_hbm(0).start()

        for step, k in enumerate(arrival_of_step):
            if k is None:
                continue

            @pl.when((g == step) & (i == 0))
            def _(k=k, step=step):
                if k == 3:
                    two_hop_half(0, 0).wait_recv()
                    two_hop_half(0, 1).wait_recv()
                else:
                    arrival(0, k).wait_recv()
                to_hbm(step).start()
                if 1 <= k <= 3:
                    pass_on(0, k - 1).start()
                if k == 1:
                    relay(0, 1).start()
                    for cp in first_copies(1)[1:]:
                        cp.start()
                if k == 2:
                    relay(0, 0).start()
                if k in (4, 5):
                    arrival(1, k - 3).wait_recv()
                    relay(1, 5 - k).start()

        rows = pl.ds(pl.multiple_of(i * tm, tm), tm)

        @pl.when(g == 0)
        def _():
            xf = x_ref[...]
            ms = jnp.mean(xf * xf, axis=-1, keepdims=True)
            hn = xf * lax.rsqrt(ms + NORM_EPS) * w_ref[...]
            hnt_ref[...] = hn.T.astype(BF16)
            hn_s[rows, :] = hn.astype(BF16)

        group = order_ref[g]

        def sorted_copy():
            per = tm // SORT_RESIDUES
            for s in range(COL_BLOCK // LANES):
                slab_tmp[s] = proj_ref[:, LANES * s:LANES * (s + 1)]
            for r in range(SORT_RESIDUES):
                for s in range(COL_BLOCK // LANES):
                    sort_stage[r, :, LANES * s:LANES * (s + 1)] = (
                        slab_tmp.at[s][pl.ds(r, per, stride=SORT_RESIDUES), :])
            cols = pl.ds(pl.multiple_of(group * COL_BLOCK, COL_BLOCK), COL_BLOCK)
            cp = pltpu.make_async_copy(
                sort_stage, qkv_hbm.at[:, pl.ds(pl.multiple_of(i * per, per), per), cols], local_sems.at[N_DEV + 1])
            cp.start()
            cp.wait()

        @pl.when(group < 2)
        def _():
            proj_ref[...] = _rot(_mm(hn_s[rows, :], w_land[group]), c_ref[...], sa_ref[...], sb_ref[...])
            sorted_copy()

        @pl.when(group == 2)
        def _():
            proj_ref[...] = _mm(hn_s[rows, :], w_land[group])
            sorted_copy()

        @pl.when(group > 2)
        def _():
            proj_ref[...] = _mm(hn_s[rows, :], w_land[group])

        @pl.when((g == N_DEV - 1) & (i == n_tiles - 1))
        def _():
            pass_on(1, 0).start()
            pass_on(1, 1).start()
            two_hop_half(1, 0).wait_recv()
            two_hop_half(1, 1).wait_recv()
            pass_on(1, 2).start()
            for k in (0, 4, 5, 6):
                arrival(1, k).wait_recv()
            for which in (0, 1):
                for cp in (first_copies(which) + [relay(which, part) for part in range(2)]
                           + [pass_on(which, j) for j in range(3)]):
                    cp.wait_send()
            wout_copy = pltpu.make_async_copy(wout_land, gout_hbm, local_sems.at[N_DEV])
            wout_copy.start()
            for step in range(N_DEV):
                to_hbm(step).wait()
            wout_copy.wait()

    me = _my_place()
    x_, y_, c_ = me
    chips = [(1 - x_, y_), (x_, 1 - y_), (1 - x_, 1 - y_)]
    order = jnp.stack([_flat(p) for p in (
        me, (x_, y_, 1 - c_), (*chips[0], c_), (*chips[1], c_), (*chips[0], 1 - c_), (*chips[1], 1 - c_),
        (*chips[2], c_), (*chips[2], 1 - c_))]).astype(jnp.int32)

    first_sweep = lambda g, i, order: (jnp.where(g == 0, i, n_tiles - 1), 0)
    tab = pl.BlockSpec((tm, LANES), lambda g, i, order: (jnp.where(order[g] < 2, i, 0), 0))
    whole = lambda: pl.BlockSpec(memory_space=pltpu.VMEM)
    grid_spec = pltpu.PrefetchScalarGridSpec(
        num_scalar_prefetch=1, grid=(N_DEV, n_tiles),
        in_specs=[pl.BlockSpec((tm, D_MODEL), first_sweep),
                  pl.BlockSpec((1, D_MODEL), lambda g, i, order: (0, 0)),
                  whole(), whole(), tab, tab, tab],
        out_specs=(pl.BlockSpec((None, tm, COL_BLOCK), lambda g, i, order: (order[g], i, 0)),
                   pl.BlockSpec((D_MODEL, tm), lambda g, i, order: (0, jnp.where(g == 0, i, n_tiles - 1))),
                   pl.BlockSpec(memory_space=pl.ANY), pl.BlockSpec(memory_space=pl.ANY),
                   pl.BlockSpec(memory_space=pl.ANY)),
        scratch_shapes=[pltpu.VMEM((SEQ, D_MODEL), BF16),
                        pltpu.VMEM((N_DEV, D_MODEL, COL_BLOCK), BF16),
                        pltpu.VMEM((D_MODEL, D_MODEL), BF16),
                        pltpu.VMEM((D_MODEL, COL_BLOCK), BF16),
                        pltpu.VMEM((SORT_RESIDUES, tm // SORT_RESIDUES, COL_BLOCK), F32),
                        pltpu.VMEM((COL_BLOCK // LANES, tm, LANES), F32),
                        pltpu.SemaphoreType.DMA((16,)), pltpu.SemaphoreType.DMA((16,)),
                        pltpu.SemaphoreType.DMA((N_DEV + 2,))])
    proj, hn_t, w_in_g, w_out_g, qkv_sorted = pl.pallas_call(
        body, name="gather_project", grid_spec=grid_spec,
        out_shape=(jax.ShapeDtypeStruct((N_DEV, SEQ, COL_BLOCK), F32), jax.ShapeDtypeStruct((D_MODEL, SEQ), BF16),
                   jax.ShapeDtypeStruct((D_MODEL, IN_COLS), BF16), jax.ShapeDtypeStruct((D_MODEL, D_MODEL), BF16),
                   jax.ShapeDtypeStruct((SORT_RESIDUES, SORT_ROWS, 3 * COL_BLOCK), F32)),
        compiler_params=_params(("arbitrary", "arbitrary")),
    )(order, x, mix_w, w_in, w_out, rc, rsa, rsb)
    return proj, hn_t, w_in_g, w_out_g, qkv_sorted.reshape(SEQ, 3 * COL_BLOCK)


SCORE_SCALE = HEAD_DIM ** -0.5
ATTN_GROUP_FWD = 16
ATTN_GROUP_BWD = 8
BLOCKS_PER_PATTERN = SEQ // ATTN_BLOCK
SORT_RESIDUES = 16
SORT_ROWS = SEQ // SORT_RESIDUES


def _write_band_bias(bias_ref):
    row = lax.broadcasted_iota(jnp.int32, (2 * ATTN_BLOCK, 2 * ATTN_BLOCK), 0) & (ATTN_BLOCK - 1)
    col = lax.broadcasted_iota(jnp.int32, (2 * ATTN_BLOCK, 2 * ATTN_BLOCK), 1)
    for pi, d in enumerate(DILATIONS):
        per = SORT_RESIDUES // d
        ahead = per * (row % (8 * d) - col % (16 * d)) + (row // (8 * d) - col // (16 * d))
        dist = ATTN_BLOCK + ahead
        bias_ref[2 * pi] = jnp.where((dist >= 0) & (dist <= ATTN_BLOCK), 0.0, NEG_BIG)
        bias_ref[2 * pi + 1] = jnp.where(ahead >= 0, 0.0, NEG_BIG)


def _head0_lanes():
    return lax.broadcasted_iota(jnp.int32, (ATTN_BLOCK, LANES), 1) < HEAD_DIM


def _stack_heads(t, h0):
    return jnp.concatenate([jnp.where(h0, t, 0.0), jnp.where(h0, 0.0, t)], axis=0).astype(BF16)


def _block_runs(i, d):
    nblk = BLOCKS_PER_PATTERN // d
    r, n = i // nblk, i % nblk
    kn = jnp.maximum(n - 1, 0)
    rows, keys = [], []
    for c in range(SORT_RESIDUES // d):
        base = SORT_ROWS * (c * d + r)
        rows.append(pl.ds(pl.multiple_of(base + 8 * d * n, 8), 8 * d))
        keys.append(pl.ds(pl.multiple_of(base + 8 * d * kn, 8), 16 * d))
    return rows, keys, (n == 0).astype(jnp.int32)


def _take(ref, runs):
    return jnp.concatenate([ref[run, :] for run in runs], axis=0)


def _put(ref, runs, value, add=False):
    at = 0
    for run in runs:
        piece = value[at:at + run.size]
        if add:
            ref[run, :] += piece
        else:
            ref[run, :] = piece
        at += run.size


def _sort_rows(src_ref, dst_ref):
    for r in range(SORT_RESIDUES):
        dst_ref[SORT_ROWS * r:SORT_ROWS * (r + 1), :] = src_ref[pl.ds(r, SORT_ROWS, stride=SORT_RESIDUES), :]


def _unsort_rows(src_ref, dst_ref):
    for r in range(SORT_RESIDUES):
        dst_ref[pl.ds(r, SORT_ROWS, stride=SORT_RESIDUES), :] = src_ref[SORT_ROWS * r:SORT_ROWS * (r + 1), :]


def _for_each_group(d, n_group, load, compute, store):
    def group(g, carry):
        items = [load(*_block_runs(g * n_group + u, d)) for u in range(n_group)]
        results = [compute(item) for item in items]
        for item, res in zip(items, results):
            store(item, res)
        return carry

    lax.fori_loop(0, BLOCKS_PER_PATTERN // n_group, group, 0)


def _attn_fwd_fused(qkv_sorted):
    n_pat = len(DILATIONS)
    tile2 = (2 * ATTN_BLOCK, LANES)

    def body(q_ref, k_ref, v_ref, o_ref, lse_ref, o_acc, m_acc, l_acc, bias_ref):
        pl.when(pl.program_id(0) == 0)(lambda: _write_band_bias(bias_ref))
        h0 = _head0_lanes()
        for pi, d in enumerate(DILATIONS):
            first, last = pi == 0, pi == n_pat - 1

            def load(rows, keys, which, first=first, pi=pi):
                item = dict(rows=rows, keys=keys, which=2 * pi + which)
                if not first:
                    item.update(o=_take(o_acc, rows), m=[_take(m_acc.at[h], rows) for h in range(2)],
                                l=[_take(l_acc.at[h], rows) for h in range(2)])
                return item

            def compute(item, first=first):
                kb = _take(k_ref, item["keys"]).astype(BF16)
                vb = _take(v_ref, item["keys"]).astype(BF16)
                s = _mm_nt(_stack_heads(_take(q_ref, item["rows"]) * SCORE_SCALE, h0), kb) + bias_ref[item["which"]]
                mb = jnp.max(s, axis=-1, keepdims=True)
                if first:
                    p = jnp.exp(s - mb)
                    mn = jnp.broadcast_to(mb, tile2)
                else:
                    m_old = jnp.concatenate(item["m"], axis=0)
                    mn = jnp.maximum(m_old, mb)
                    alpha = jnp.exp(m_old - mn)
                    p = jnp.exp(s - jnp.concatenate([mn, mn], axis=1))
                ls = jnp.sum(p, axis=-1, keepdims=True)
                pv = _mm(p.astype(BF16), vb)
                if first:
                    return pv, mn, jnp.broadcast_to(ls, tile2)
                o_old = jnp.concatenate([item["o"], item["o"]], axis=0)
                return alpha * o_old + pv, mn, alpha * jnp.concatenate(item["l"], axis=0) + ls

            def store(item, res, last=last):
                rows = item["rows"]
                (o0, o1), (m0, m1), (l0, l1) = ((a[:ATTN_BLOCK], a[ATTN_BLOCK:]) for a in res)
                if last:
                    _put(o_acc, rows, jnp.where(h0, o0 / l0, o1 / l1))
                    _put(lse_ref, rows, jnp.where(h0, m0 + jnp.log(l0), m1 + jnp.log(l1)))
                else:
                    _put(o_acc, rows, jnp.where(h0, o0, o1))
                    for h, (m, l) in enumerate(((m0, l0), (m1, l1))):
                        _put(m_acc.at[h], rows, m)
                        _put(l_acc.at[h], rows, l)

            _for_each_group(d, ATTN_GROUP_FWD, load, compute, store)
        _unsort_rows(o_acc, o_ref)

    slab = lambda g: pl.BlockSpec((SEQ, LANES), functools.partial(lambda hp, g: (0, 4 * g + hp), g=g))
    wide = jax.ShapeDtypeStruct((SEQ, ATTN_WIDTH), F32)
    return pl.pallas_call(
        body, name="attn_fwd", grid=(4,), out_shape=(wide, wide),
        in_specs=[slab(0), slab(1), slab(2)], out_specs=(slab(0), slab(0)),
        scratch_shapes=[pltpu.VMEM((SEQ, LANES), F32), pltpu.VMEM((2, SEQ, LANES), F32),
                        pltpu.VMEM((2, SEQ, LANES), F32),
                        pltpu.VMEM((2 * len(DILATIONS), 2 * ATTN_BLOCK, 2 * ATTN_BLOCK), F32)],
        compiler_params=_params(("arbitrary",)),
    )(qkv_sorted, qkv_sorted, qkv_sorted)


def _attn_bwd_fused(qkv_sorted, d_out, lse_sorted, delta):
    def body(q_ref, k_ref, v_ref, do_ref, lse_ref, del_ref, dq_ref, dk_ref, dv_ref,
             do_s, del_s, dq_s, dk_s, dv_s, bias_ref):
        pl.when(pl.program_id(0) == 0)(lambda: _write_band_bias(bias_ref))
        _sort_rows(do_ref, do_s)
        _sort_rows(del_ref, del_s)
        dk_s[...] = jnp.zeros_like(dk_s)
        dv_s[...] = jnp.zeros_like(dv_s)
        h0 = _head0_lanes()
        for pi, d in enumerate(DILATIONS):
            first = pi == 0

            def load(rows, keys, which, pi=pi):
                return dict(rows=rows, keys=keys, q=_take(q_ref, rows), g=_take(do_s, rows),
                            lse=_take(lse_ref, rows), delta=_take(del_s, rows),
                            k=_take(k_ref, keys).astype(BF16), v=_take(v_ref, keys).astype(BF16),
                            bias=bias_ref[2 * pi + which])

            def per_head(t):
                swapped = pltpu.roll(t, HEAD_DIM, 1)
                both = jnp.concatenate([jnp.where(h0, t, swapped), jnp.where(h0, swapped, t)], axis=0)
                return jnp.concatenate([both, both], axis=1)

            def compute(item):
                q2, g2 = _stack_heads(item["q"] * SCORE_SCALE, h0), _stack_heads(item["g"], h0)
                s = _mm_nt(q2, item["k"]) + item["bias"]
                p = jnp.exp(s - per_head(item["lse"]))
                dp = _mm_nt(g2, item["v"])
                ds = (p * (dp - per_head(item["delta"]))).astype(BF16)
                dq2 = _mm(ds, item["k"])
                dq = jnp.where(h0, dq2[:ATTN_BLOCK], dq2[ATTN_BLOCK:]) * SCORE_SCALE
                return dq, _mm_tn(ds, q2), _mm_tn(p.astype(BF16), g2)

            def store(item, res, first=first):
                _put(dq_s, item["rows"], res[0], add=not first)
                _put(dk_s, item["keys"], res[1], add=True)
                _put(dv_s, item["keys"], res[2], add=True)

            _for_each_group(d, ATTN_GROUP_BWD, load, compute, store)
        _unsort_rows(dq_s, dq_ref)
        _unsort_rows(dk_s, dk_ref)
        _unsort_rows(dv_s, dv_ref)

    slab = lambda g: pl.BlockSpec((SEQ, LANES), functools.partial(lambda hp, g: (0, 4 * g + hp), g=g))
    wide = jax.ShapeDtypeStruct((SEQ, ATTN_WIDTH), F32)
    sorted_slab = pltpu.VMEM((SEQ, LANES), F32)
    return pl.pallas_call(
        body, name="attn_bwd", grid=(4,), out_shape=(wide, wide, wide),
        scratch_shapes=[sorted_slab] * 5 + [pltpu.VMEM((2 * len(DILATIONS), 2 * ATTN_BLOCK, 2 * ATTN_BLOCK), F32)],
        in_specs=[slab(0), slab(1), slab(2), slab(0), slab(0), slab(0)], out_specs=(slab(0), slab(0), slab(0)),
        compiler_params=_params(("arbitrary",)),
    )(qkv_sorted, qkv_sorted, qkv_sorted, d_out, lse_sorted, delta)


def _hgrn_lower_bound(lb_ref):
    r0, r1 = lb_ref[0:1, :], lb_ref[1:2, :]
    mx = jnp.maximum(r0, r1)
    e0, e1 = jnp.exp(r0 - mx), jnp.exp(r1 - mx)
    return e0 / (e0 + e1)


def _hgrn_gates(hq, hf, lb):
    sq = _sigmoid(hq)
    sg = _sigmoid(hf)
    f = lb + (1.0 - lb) * sg
    return hq * sq, sq, sg, f, 1.0 - f, jnp.log(f)


HGRN_PAIR = 4
HGRN_SEQ_BLOCK = 1024
HGRN_GROUP = 4
HGRN_ROWS = HGRN_GROUP * HGRN_CHUNK


def _hgrn_specs(reverse):
    n_blocks = SEQ // HGRN_SEQ_BLOCK
    width = HGRN_PAIR * HGRN_DIM
    blk = (lambda s: n_blocks - 1 - s) if reverse else (lambda s: s)
    cols = lambda g: pl.BlockSpec((None, HGRN_SEQ_BLOCK, width), functools.partial(lambda p, s, g: (g, blk(s), p), g=g))
    pair = pl.BlockSpec((HGRN_SEQ_BLOCK, width), lambda p, s: (blk(s), p))
    lb = pl.BlockSpec((2, width), lambda p, s: (0, p))
    states = pl.BlockSpec((HGRN_PAIR, HGRN_SEQ_BLOCK // HGRN_CHUNK, HGRN_DIM, HGRN_DIM),
                          lambda p, s: (p, blk(s), 0, 0))
    return cols, pair, lb, states


def _chunk_masks():
    ri = lax.broadcasted_iota(jnp.int32, (HGRN_ROWS, HGRN_ROWS), 0)
    ci = lax.broadcasted_iota(jnp.int32, (HGRN_ROWS, HGRN_ROWS), 1)
    same = (ri // HGRN_CHUNK) == (ci // HGRN_CHUNK)
    return same, same & (ri >= ci), same & (ri <= ci)


def _mm_select(sel, v):
    hi = v.astype(BF16)
    r1 = v - hi.astype(F32)
    mid = r1.astype(BF16)
    lo = (r1 - mid.astype(F32)).astype(BF16)
    return _mm(sel, hi) + _mm(sel, mid) + _mm(sel, lo)


def _head_cols(a, h):
    return a[:, HGRN_DIM * h:HGRN_DIM * (h + 1)]


def _hgrn_fwd(proj, lb_raw):
    t, rws = HGRN_CHUNK, HGRN_ROWS

    def body(hq_ref, hf_ref, hi_ref, lb_ref, rec_ref, st_ref, state):
        @pl.when(pl.program_id(1) == 0)
        def _():
            state[...] = jnp.zeros_like(state)

        lb = _hgrn_lower_bound(lb_ref)
        same, causal, _ = _chunk_masks()
        sel = jnp.concatenate([causal, same], axis=0).astype(BF16)

        def group(g, sts):
            rows = pl.ds(pl.multiple_of(g * rws, rws), rws)
            q, _, _, _, k, lf = _hgrn_gates(hq_ref[rows, :], hf_ref[rows, :], lb)
            sums = _mm_select(sel, lf)
            cum, last = sums[:rws], sums[rws:]
            qd = (q * jnp.exp(cum)).astype(BF16)
            ki = (k * jnp.exp(-cum)).astype(BF16)
            ke = (k * jnp.exp(last - cum)).astype(BF16)
            vb = hi_ref[rows, :].astype(BF16)
            dec = jnp.exp(last)
            new_sts, recs = [], []
            for h in range(HGRN_PAIR):
                qd_h, ke_h, vb_h = _head_cols(qd, h), _head_cols(ke, h), _head_cols(vb, h)
                att = jnp.where(causal, _mm_nt(qd_h, _head_cols(ki, h)), 0.0).astype(BF16)
                intra = _mm(att, vb_h)
                st = sts[h]
                outs = []
                for c in range(HGRN_GROUP):
                    sl = slice(c * t, (c + 1) * t)
                    st_ref[h, g * HGRN_GROUP + c] = st
                    outs.append(intra[sl] + _mm_nt(qd_h[sl], st.astype(BF16)))
                    st = st * _head_cols(dec[c * t:c * t + 1, :], h) + _mm_tn(vb_h[sl], ke_h[sl])
                new_sts.append(st)
                recs.append(jnp.concatenate(outs, axis=0))
            rec_ref[rows, :] = jnp.concatenate(recs, axis=1)
            return tuple(new_sts)

        sts = lax.fori_loop(0, HGRN_SEQ_BLOCK // rws, group, tuple(state[h] for h in range(HGRN_PAIR)))
        for h in range(HGRN_PAIR):
            state[h] = sts[h]

    cols, pair, lb, states = _hgrn_specs(reverse=False)
    return pl.pallas_call(
        body, name="hgrn_fwd", grid=(HGRN_HEADS // HGRN_PAIR, SEQ // HGRN_SEQ_BLOCK),
        out_shape=(jax.ShapeDtypeStruct((SEQ, HGRN_WIDTH), F32),
                   jax.ShapeDtypeStruct((HGRN_HEADS, N_CHUNKS, HGRN_DIM, HGRN_DIM), F32)),
        in_specs=[cols(4), cols(5), cols(6), lb], out_specs=(pair, states),
        scratch_shapes=[pltpu.VMEM((HGRN_PAIR, HGRN_DIM, HGRN_DIM), F32)],
        compiler_params=_params(("parallel", "arbitrary")),
    )(proj, proj, proj, lb_raw)


def _hgrn_bwd(proj, lb_raw, d_rec, states):
    t, rws = HGRN_CHUNK, HGRN_ROWS

    def body(hq_ref, hf_ref, hi_ref, lb_ref, do_ref, st_ref, dhq_ref, dhf_ref, dhi_ref, dlb_ref,
             dstate, dlb_acc):
        lb = _hgrn_lower_bound(lb_ref)
        same, causal, anti = _chunk_masks()
        sel = jnp.concatenate([causal, same], axis=0).astype(BF16)
        sel_t = jnp.concatenate([anti, same], axis=1).astype(BF16)
        @pl.when(pl.program_id(1) == 0)
        def _():
            dstate[...] = jnp.zeros_like(dstate)
            dlb_acc[...] = jnp.zeros_like(dlb_acc)

        n_groups = HGRN_SEQ_BLOCK // rws
        chunks = [slice(c * t, (c + 1) * t) for c in range(HGRN_GROUP)]

        def group(i, dsts_in):
            g = n_groups - 1 - i
            rows = pl.ds(pl.multiple_of(g * rws, rws), rws)
            hq = hq_ref[rows, :]
            q, sq, sg, f, k, lf = _hgrn_gates(hq, hf_ref[rows, :], lb)
            sums = _mm_select(sel, lf)
            cum, last = sums[:rws], sums[rws:]
            e_cum, e_inv, e_end, dec = jnp.exp(cum), jnp.exp(-cum), jnp.exp(last - cum), jnp.exp(last)
            qd, ki, ke = q * e_cum, k * e_inv, k * e_end
            qdb, kib, keb = qd.astype(BF16), ki.astype(BF16), ke.astype(BF16)
            vb = hi_ref[rows, :].astype(BF16)
            gb = do_ref[rows, :].astype(BF16)

            dsts_out, per_head = [], []
            for h in range(HGRN_PAIR):
                qdb_h, kib_h, keb_h = _head_cols(qdb, h), _head_cols(kib, h), _head_cols(keb, h)
                vb_h, gb_h = _head_cols(vb, h), _head_cols(gb, h)
                att = jnp.where(causal, _mm_nt(qdb_h, kib_h), 0.0).astype(BF16)
                datt = jnp.where(causal, _mm_nt(gb_h, vb_h), 0.0).astype(BF16)
                dv = _mm_tn(att, gb_h)
                dqd = _mm(datt, kib_h)
                dki = _mm_tn(datt, qdb_h)

                decs = [_head_cols(dec[c * t:c * t + 1, :], h) for c in range(HGRN_GROUP)]
                dsts = [None] * HGRN_GROUP
                dst = dsts_in[h]
                for c in reversed(range(HGRN_GROUP)):
                    dsts[c] = dst
                    dst = dst * decs[c] + _mm_tn(gb_h[chunks[c]], qdb_h[chunks[c]])
                dsts_out.append(dst)

                dv_x, dqd_x, dke, dlast_x = [], [], [], []
                for c, sl in enumerate(chunks):
                    st_prev = st_ref[h, g * HGRN_GROUP + c]
                    dstb = dsts[c].astype(BF16)
                    dv_x.append(_mm_nt(keb_h[sl], dstb))
                    dqd_x.append(_mm(gb_h[sl], st_prev.astype(BF16)))
                    dke.append(_mm(vb_h[sl], dstb))
                    ddec = jnp.sum(dsts[c] * st_prev, axis=0, keepdims=True)
                    dlast_x.append(jnp.broadcast_to(ddec * decs[c], (t, HGRN_DIM)))
                per_head.append((dv + jnp.concatenate(dv_x, axis=0), dqd + jnp.concatenate(dqd_x, axis=0),
                                 dki, jnp.concatenate(dke, axis=0), jnp.concatenate(dlast_x, axis=0)))
            dv, dqd, dki, dke, dlast = (jnp.concatenate(list(parts), axis=1) for parts in zip(*per_head))

            dq = dqd * e_cum
            dk = dki * e_inv + dke * e_end
            dke_ke = dke * ke
            dcum = dqd * qd - dki * ki - dke_ke
            dlf = _mm_select(sel_t, jnp.concatenate([dcum, dke_ke], axis=0)) + dlast
            df = dlf / f - dk
            dhq_ref[rows, :] = dq * (sq * (1.0 + hq * (1.0 - sq)))
            dhf_ref[rows, :] = df * (1.0 - lb) * (sg * (1.0 - sg))
            dhi_ref[rows, :] = dv
            dlb_acc[...] += jnp.sum(df * (1.0 - sg), axis=0, keepdims=True)
            return tuple(dsts_out)

        dsts = lax.fori_loop(0, n_groups, group, tuple(dstate[h] for h in range(HGRN_PAIR)))
        for h in range(HGRN_PAIR):
            dstate[h] = dsts[h]
        g0 = dlb_acc[...] * lb * (1.0 - lb)
        dlb_ref[...] = jnp.concatenate([g0, -g0], axis=0)

    cols, pair, lb_spec, st_spec = _hgrn_specs(reverse=True)
    wide = jax.ShapeDtypeStruct((SEQ, HGRN_WIDTH), F32)
    return pl.pallas_call(
        body, name="hgrn_bwd", grid=(HGRN_HEADS // HGRN_PAIR, SEQ // HGRN_SEQ_BLOCK),
        out_shape=(wide, wide, wide, jax.ShapeDtypeStruct((2, HGRN_WIDTH), F32)),
        in_specs=[cols(4), cols(5), cols(6), lb_spec, pair, st_spec],
        out_specs=(pair, pair, pair, lb_spec),
        scratch_shapes=[pltpu.VMEM((HGRN_PAIR, HGRN_DIM, HGRN_DIM), F32),
                        pltpu.VMEM((1, HGRN_PAIR * HGRN_DIM), F32)],
        compiler_params=_params(("parallel", "arbitrary")),
    )(proj, proj, proj, lb_raw, d_rec, states)


def _group_sum(v, group):
    parts = []
    for s in range(v.shape[1] // LANES):
        slab = v[:, LANES * s:LANES * (s + 1)]
        if group == LANES:
            parts.append(jnp.broadcast_to(jnp.sum(slab, axis=-1, keepdims=True), slab.shape))
        else:
            h0 = lax.broadcasted_iota(jnp.int32, slab.shape, 1) < HEAD_DIM
            s0 = jnp.sum(jnp.where(h0, slab, 0.0), axis=-1, keepdims=True)
            s1 = jnp.sum(jnp.where(h0, 0.0, slab), axis=-1, keepdims=True)
            parts.append(jnp.where(h0, s0, s1))
    return jnp.concatenate(parts, axis=1)


def _mid(attn_o, rec, proj, x, target, w_out_g, attn_w, hgrn_w, final_w):
    tm = 256

    def branch_fwd(o, gate, w, group):
        r = lax.rsqrt(_group_sum(o * o, group) * (1.0 / group) + NORM_EPS)
        nrm = o * r
        sg = _sigmoid(gate)
        return r, nrm, sg, nrm * w * (gate * sg)

    def branch_bwd(dy, r, nrm, sg, gate, w, group):
        silu = gate * sg
        d_gate = dy * nrm * w * (sg * (1.0 + gate * (1.0 - sg)))
        d_w = jnp.sum(dy * nrm * silu, axis=0, keepdims=True)
        dn = dy * w * silu
        d_o = r * (dn - nrm * (_group_sum(dn * nrm, group) * (1.0 / group)))
        return d_o, d_gate, d_w

    def body(o_ref, rec_ref, ag_ref, hg_ref, x_ref, tgt_ref, wout_ref, aw_ref, hw_ref, fw_ref,
             dx2_ref, do_ref, delta_ref, dag_ref, drec_ref, dhg_ref, dwout_ref, dfw_ref, daw_ref, dhw_ref,
             loss_ref, dwout_acc):
        i = pl.program_id(0)

        @pl.when(i == 0)
        def _():
            dwout_acc[...] = jnp.zeros_like(dwout_acc)
            dfw_ref[...] = jnp.zeros_like(dfw_ref)
            daw_ref[...] = jnp.zeros_like(daw_ref)
            dhw_ref[...] = jnp.zeros_like(dhw_ref)
            loss_ref[...] = jnp.zeros_like(loss_ref)

        o, rc, ag, hg = o_ref[...], rec_ref[...], ag_ref[...], hg_ref[...]
        aw, hw, fw = aw_ref[...], hw_ref[...], fw_ref[...]
        ra, na, sga, ya = branch_fwd(o, ag, aw, HEAD_DIM)
        rh, nh, sgh, yh = branch_fwd(rc, hg, hw, HGRN_DIM)
        mixed = jnp.concatenate([ya, yh], axis=1).astype(BF16)
        wout = wout_ref[...]
        x2 = x_ref[...] + _mm(mixed, wout)
        rstd = lax.rsqrt(jnp.mean(x2 * x2, axis=-1, keepdims=True) + NORM_EPS)
        xn = x2 * rstd
        err = xn * fw - tgt_ref[...]
        row_loss = jnp.mean(err * err, axis=-1, keepdims=True)
        loss_ref[...] += 0.5 * jnp.sum(row_loss, axis=0, keepdims=True)
        dy = err * (1.0 / D_MODEL)
        dfw_ref[...] += jnp.sum(dy * xn, axis=0, keepdims=True)
        dxn = dy * fw
        dx2 = rstd * (dxn - xn * jnp.mean(dxn * xn, axis=-1, keepdims=True))
        dx2_ref[...] = dx2
        dx2b = dx2.astype(BF16)
        dwout_acc[...] += _mm_tn(mixed, dx2b)

        @pl.when(i == pl.num_programs(0) - 1)
        def _():
            dwout_ref[...] = dwout_acc[...].astype(BF16)

        dmixed = _mm_nt(dx2b, wout)

        d_o, d_ag, d_aw = branch_bwd(dmixed[:, :ATTN_WIDTH], ra, na, sga, ag, aw, HEAD_DIM)
        d_rec, d_hg, d_hw = branch_bwd(dmixed[:, ATTN_WIDTH:], rh, nh, sgh, hg, hw, HGRN_DIM)
        do_ref[...] = d_o
        delta_ref[...] = _group_sum(d_o * o, HEAD_DIM)
        dag_ref[...] = d_ag
        drec_ref[...] = d_rec
        dhg_ref[...] = d_hg
        daw_ref[...] += d_aw
        dhw_ref[...] += d_hw

    half = lambda: pl.BlockSpec((tm, COL_BLOCK), lambda i: (i, 0))
    full = lambda: pl.BlockSpec((tm, D_MODEL), lambda i: (i, 0))
    fixed = lambda r, c: pl.BlockSpec((r, c), lambda i: (0, 0))
    wide = jax.ShapeDtypeStruct((SEQ, COL_BLOCK), F32)
    return pl.pallas_call(
        body, name="mid", grid=(SEQ // tm,),
        out_shape=(jax.ShapeDtypeStruct((SEQ, D_MODEL), F32), wide, wide, wide, wide, wide,
                   jax.ShapeDtypeStruct((D_MODEL, D_MODEL), BF16),
                   jax.ShapeDtypeStruct((1, D_MODEL), F32), jax.ShapeDtypeStruct((1, COL_BLOCK), F32),
                   jax.ShapeDtypeStruct((1, COL_BLOCK), F32), jax.ShapeDtypeStruct((1, 1), F32)),
        scratch_shapes=[pltpu.VMEM((D_MODEL, D_MODEL), F32)],
        in_specs=[half(), half(),
                  pl.BlockSpec((None, tm, COL_BLOCK), lambda i: (3, i, 0)),
                  pl.BlockSpec((None, tm, COL_BLOCK), lambda i: (7, i, 0)),
                  full(), full(), fixed(D_MODEL, D_MODEL), fixed(1, COL_BLOCK), fixed(1, COL_BLOCK),
                  fixed(1, D_MODEL)],
        out_specs=(full(), half(), half(), half(), half(), half(), fixed(D_MODEL, D_MODEL),
                   fixed(1, D_MODEL), fixed(1, COL_BLOCK), fixed(1, COL_BLOCK), fixed(1, 1)),
        compiler_params=_params(("arbitrary",)),
    )(attn_o, rec, proj, proj, x, target, w_out_g, attn_w, hgrn_w, final_w)


def _in_proj_bwd_rows(d_groups, w_g, x, dx2, mix_w, rc, rsa, rsb):
    tm = 256

    def body(*refs):
        dg_refs = refs[:N_DEV]
        wg_ref, x_ref, dx2_ref, w_ref, c_ref, sa_ref, sb_ref, gx_ref, dpb_ref, dmw_ref = refs[N_DEV:]

        @pl.when(pl.program_id(0) == 0)
        def _():
            dmw_ref[...] = jnp.zeros_like(dmw_ref)

        parts = []
        for j in range(N_DEV):
            dp = dg_refs[j][...]
            if j < 2:
                dp = _rot_transposed(dp, c_ref[...], sa_ref[...], sb_ref[...])
            parts.append(dp.astype(BF16))
        dpb = jnp.concatenate(parts, axis=1)
        for j in range(N_DEV):
            dpb_ref[j] = parts[j]
        g = _mm_nt(dpb, wg_ref[...])
        xf = x_ref[...]
        rstd = lax.rsqrt(jnp.mean(xf * xf, axis=-1, keepdims=True) + NORM_EPS)
        xn = xf * rstd
        dmw_ref[...] += jnp.sum(g * xn, axis=0, keepdims=True)
        gw = g * w_ref[...]
        gx_ref[...] = dx2_ref[...] + rstd * (gw - xn * jnp.mean(gw * xn, axis=-1, keepdims=True))

    tile = lambda cols: pl.BlockSpec((tm, cols), lambda i: (i, 0))
    fixed = lambda r, c: pl.BlockSpec((r, c), lambda i: (0, 0))
    return pl.pallas_call(
        body, name="in_proj_bwd_rows", grid=(SEQ // tm,),
        out_shape=(jax.ShapeDtypeStruct((SEQ, D_MODEL), F32), jax.ShapeDtypeStruct((N_DEV, SEQ, COL_BLOCK), BF16),
                   jax.ShapeDtypeStruct((1, D_MODEL), F32)),
        in_specs=[tile(COL_BLOCK) for _ in range(N_DEV)] + [
            pl.BlockSpec((D_MODEL, IN_COLS), lambda i: (0, 0), pipeline_mode=pl.Buffered(1)),
            tile(D_MODEL), tile(D_MODEL), fixed(1, D_MODEL), tile(LANES), tile(LANES), tile(LANES)],
        out_specs=(tile(D_MODEL), pl.BlockSpec((N_DEV, tm, COL_BLOCK), lambda i: (0, i, 0)), fixed(1, D_MODEL)),
        compiler_params=_params(("arbitrary",)),
    )(*d_groups, w_g, x, dx2, mix_w, rc, rsa, rsb)


def _weights_exchange(hn_t, dproj_b, dwout_p, small_p):
    n_chips = N_DEV // 2
    rb = 128
    S1_IN, S1_OUT, SMALL, S2_IN, S2_OUT = 0, 4, 8, 15, 18
    rel_of_pair = (1, 2, 3, 0)

    def body(order_ref, hnt_ref, dp_ref, dwout_ref, small_ref, gin_ref, gout_ref, gs_ref,
             part, s1_send, s1_in, s1_out, fwd_in, fwd_out, s2_in, s2_out, land_s, send_sems, recv_sems):
        t = pl.program_id(0)
        me = _my_place()
        x, y, c = me
        my_chip = 2 * x + y
        sibling = (x, y, 1 - c)

        def remote(slot, src, dst, to):
            return pltpu.make_async_remote_copy(src_ref=src, dst_ref=dst, send_sem=send_sems.at[slot],
                                                recv_sem=recv_sems.at[slot], device_id=to, device_id_type=MESH)

        def s1_in_copy(pair):
            return remote(S1_IN + pair, s1_send.at[pair], s1_in.at[pair], sibling)

        def s1_out_copy(pair):
            q = my_chip ^ rel_of_pair[pair]
            return remote(S1_OUT + pair, dwout_ref.at[q, 1 - c], s1_out.at[pair], sibling)

        def s2_copies(rel):
            peer = _peer(me, 2 * rel)
            return [remote(S2_IN + rel - 1, fwd_in.at[rel - 1], s2_in.at[rel - 1], peer),
                    remote(S2_OUT + rel - 1, fwd_out.at[rel - 1], s2_out.at[rel - 1], peer)]

        def small_copy(rel):
            return remote(SMALL + rel - 1, small_ref, land_s.at[rel], _peer(me, rel))

        @pl.when(t == 0)
        def _():
            land_s[0] = small_ref[...]
            for pair in range(n_chips):
                s1_out_copy(pair).start()
            for rel in range(1, N_DEV):
                small_copy(rel).start()

        part[...] = _mm(hnt_ref[...], dp_ref[...])

        def rows_loop(n_rows, fn):
            def step(b, carry):
                fn(pl.ds(pl.multiple_of(b * rb, rb), rb))
                return carry
            lax.fori_loop(0, n_rows // rb, step, 0)

        for pair, rel in enumerate(rel_of_pair):
            @pl.when(t == 2 * pair)
            def _(pair=pair):
                s1_send[pair] = part[...].astype(BF16)
                s1_in_copy(pair).start()

            @pl.when(t == 2 * pair + 1)
            def _(pair=pair, rel=rel):
                q = my_chip ^ rel
                s1_in_copy(pair).wait_recv()
                s1_out_copy(pair).wait_recv()
                dst_in = fwd_in.at[rel - 1] if rel else gin_ref
                dst_out = fwd_out.at[rel - 1] if rel else gout_ref

                def add_in(rows):
                    dst_in[rows, :] = (part[rows, :] + s1_in[pair, rows, :].astype(F32)).astype(dst_in.dtype)

                def add_out(rows):
                    dst_out[rows, :] = (dwout_ref[q, c, rows, :].astype(F32)
                                        + s1_out[pair, rows, :].astype(F32)).astype(dst_out.dtype)

                rows_loop(D_MODEL, add_in)
                rows_loop(WOUT_ROWS, add_out)
                if rel:
                    for cp in s2_copies(rel):
                        cp.start()

        @pl.when(t == N_DEV - 1)
        def _():
            for rel in range(1, n_chips):
                for cp in s2_copies(rel):
                    cp.wait_recv()

            def total_in(rows):
                g = gin_ref[rows, :]
                for rel in range(1, n_chips):
                    g = g + s2_in[rel - 1, rows, :].astype(F32)
                gin_ref[rows, :] = g

            def total_out(rows):
                g = gout_ref[rows, :]
                for rel in range(1, n_chips):
                    g = g + s2_out[rel - 1, rows, :].astype(F32)
                gout_ref[rows, :] = g

            rows_loop(D_MODEL, total_in)
            rows_loop(WOUT_ROWS, total_out)

            for rel in range(1, N_DEV):
                small_copy(rel).wait_recv()
            my_flat = _flat(me)
            g = land_s[my_flat ^ 0]
            for dev in range(1, N_DEV):
                g = g + land_s[my_flat ^ dev]
            gs_ref[...] = g

            for pair in range(n_chips):
                s1_in_copy(pair).wait_send()
                s1_out_copy(pair).wait_send()
            for rel in range(1, n_chips):
                for cp in s2_copies(rel):
                    cp.wait_send()
            for rel in range(1, N_DEV):
                small_copy(rel).wait_send()

    place_x, place_y, place_c = _my_place()
    my_chip = 2 * place_x + place_y
    order = jnp.stack([2 * (my_chip ^ rel) + core for rel in rel_of_pair
                       for core in (1 - place_c, place_c)]).astype(jnp.int32)

    whole = lambda: pl.BlockSpec(memory_space=pltpu.VMEM)
    in_blocks = lambda n: pltpu.VMEM((n, D_MODEL, COL_BLOCK), BF16)
    out_blocks = lambda n: pltpu.VMEM((n, WOUT_ROWS, D_MODEL), BF16)
    grid_spec = pltpu.PrefetchScalarGridSpec(
        num_scalar_prefetch=1, grid=(N_DEV,),
        in_specs=[pl.BlockSpec((D_MODEL, SEQ), lambda t, order: (0, 0), pipeline_mode=pl.Buffered(1)),
                  pl.BlockSpec((None, SEQ, COL_BLOCK), lambda t, order: (order[t], 0, 0)), whole(), whole()],
        out_specs=(whole(), whole(), whole()),
        scratch_shapes=[pltpu.VMEM((D_MODEL, COL_BLOCK), F32), in_blocks(n_chips), in_blocks(n_chips),
                        out_blocks(n_chips), in_blocks(n_chips - 1), out_blocks(n_chips - 1),
                        in_blocks(n_chips - 1), out_blocks(n_chips - 1),
                        pltpu.VMEM((N_DEV, SMALL_ROWS, LANES), F32),
                        pltpu.SemaphoreType.DMA((21,)), pltpu.SemaphoreType.DMA((21,))])
    return pl.pallas_call(
        body, name="weights_exchange", grid_spec=grid_spec,
        out_shape=(jax.ShapeDtypeStruct((D_MODEL, COL_BLOCK), F32), jax.ShapeDtypeStruct((WOUT_ROWS, D_MODEL), F32),
                   jax.ShapeDtypeStruct((SMALL_ROWS, LANES), F32)),
        compiler_params=_params(("arbitrary",)),
    )(order, hn_t, dproj_b, dwout_p.reshape(n_chips, 2, WOUT_ROWS, D_MODEL), small_p)


def _adamw(w, g, m, v):
    m = ADAM_B1 * m + (1.0 - ADAM_B1) * g
    v = ADAM_B2 * v + (1.0 - ADAM_B2) * (g * g)
    m_hat = m / (1.0 - ADAM_B1 ** ADAM_STEP)
    v_hat = v / (1.0 - ADAM_B2 ** ADAM_STEP)
    delta = -ADAM_LR * (m_hat / (jnp.sqrt(v_hat) + ADAM_EPS) + ADAM_WD * w)
    return delta, m, v


def _adamw_update(grads, weights, m_old, v_old):
    rb = 256

    def body(*refs):
        g_refs, w_refs, m_refs, v_refs = refs[0:3], refs[3:6], refs[6:9], refs[9:12]
        d_refs, nm_refs, nv_refs = refs[12:15], refs[15:18], refs[18:21]
        for k in range(3):
            n_rows = g_refs[k].shape[0]
            step_rows = min(rb, n_rows)

            def step(b, carry, k=k, step_rows=step_rows):
                rows = pl.ds(pl.multiple_of(b * step_rows, 8), step_rows)
                delta, nm, nv = _adamw(w_refs[k][rows, :], g_refs[k][rows, :], m_refs[k][rows, :], v_refs[k][rows, :])
                d_refs[k][rows, :] = delta
                nm_refs[k][rows, :] = nm
                nv_refs[k][rows, :] = nv
                return carry

            lax.fori_loop(0, n_rows // step_rows, step, 0)

    shapes = tuple(jax.ShapeDtypeStruct(g.shape, F32) for g in grads)
    vm = lambda: pl.BlockSpec(memory_space=pltpu.VMEM)
    outs = pl.pallas_call(
        body, name="adamw_update", out_shape=shapes * 3,
        in_specs=[vm() for _ in range(12)], out_specs=tuple(vm() for _ in range(9)),
        compiler_params=_params(),
    )(*grads, *weights, *m_old, *v_old)
    return outs[0:3], outs[3:6], outs[6:9]


def _pack_small(mix, attn, hgrn, lb, final, loss=None):
    def rows8(a):
        a = a.reshape(-1, LANES)
        return jnp.pad(a, ((0, 8 - a.shape[0]), (0, 0)))
    last = jnp.zeros((8, LANES), F32) if loss is None else jnp.pad(loss.reshape(1, 1), ((0, 7), (0, LANES - 1)))
    return jnp.concatenate([rows8(mix), rows8(attn), rows8(hgrn), rows8(lb), rows8(final), last], axis=0)


def _unpack_small(slab):
    return (slab[ROW_MIX:ROW_MIX + 8].reshape(1, D_MODEL), slab[ROW_ATTN:ROW_ATTN + 4].reshape(1, ATTN_WIDTH),
            slab[ROW_HGRN:ROW_HGRN + 4].reshape(1, HGRN_WIDTH), slab[ROW_LB:ROW_LB + 8].reshape(2, HGRN_WIDTH),
            slab[ROW_FINAL:ROW_FINAL + 8].reshape(D_MODEL))


def _rope(pos_row):
    j = np.arange(ROPE_ROWS)
    inv = np.where(j < ROPE_HALF, ROPE_THETA ** (-(j % ROPE_HALF) * (2.0 / ROPE_DIMS)), 0.0)
    e = np.arange(LANES) % HEAD_DIM
    hit = (j[:, None] == (e % ROPE_HALF)[None, :]) & (j[:, None] < ROPE_HALF)
    sel = np.stack([hit & (e < ROPE_DIMS), hit & (e >= ROPE_HALF) & (e < ROPE_DIMS),
                    -1.0 * (hit & (e < ROPE_HALF))]).astype(np.float32)
    return _rope_tables(pos_row, jnp.asarray(inv.astype(np.float32).reshape(ROPE_ROWS, 1)),
                        jnp.asarray(sel, dtype=BF16))


def _local_step(x, proj, qkv_sorted, w_in_g, w_out_g, tables, mix_w, attn_w, hgrn_w, lb_raw, final_w, target):
    rc, rsa, rsb = tables
    attn_o, lse = _attn_fwd_fused(qkv_sorted)
    rec, states = _hgrn_fwd(proj, lb_raw)

    (dx2, d_o, delta, d_ag, d_rec, d_hg, dwout_p, d_final, d_attn_w, d_hgrn_w, loss) = _mid(
        attn_o, rec, proj, x, target, w_out_g, attn_w, hgrn_w, final_w.reshape(1, D_MODEL))

    dqkv = _attn_bwd_fused(qkv_sorted, d_o, lse, delta)
    d_hq, d_hf, d_hi, d_lb = _hgrn_bwd(proj, lb_raw, d_rec, states)

    grad_x, dproj_b, d_mix = _in_proj_bwd_rows(
        (dqkv[0], dqkv[1], dqkv[2], d_ag, d_hq, d_hf, d_hi, d_hg), w_in_g, x, dx2, mix_w, rc, rsa, rsb)
    small_p = _pack_small(d_mix, d_attn_w, d_hgrn_w, d_lb, d_final, loss)
    return grad_x, dproj_b, dwout_p, small_p


def kernel(x, positions, w_in, w_out, mix_norm_w, attn_out_norm_w, hgrn_out_norm_w, hgrn_lb_raw, final_norm_w, loss_target, m_w_in, m_w_out, m_mix_norm_w, m_attn_out_norm_w, m_hgrn_out_norm_w, m_hgrn_lb_raw, m_final_norm_w, v_w_in, v_w_out, v_mix_norm_w, v_attn_out_norm_w, v_hgrn_out_norm_w, v_hgrn_lb_raw, v_final_norm_w):
    tables = _rope(positions)
    proj, hn_t, w_in_g, w_out_g, qkv_sorted = _gather_project(x[0], mix_norm_w, w_in[0], w_out[0], *tables)
    grad_x, dproj_b, dwout_p, small_p = _local_step(
        x[0], proj, qkv_sorted, w_in_g, w_out_g, tables, mix_norm_w, attn_out_norm_w, hgrn_out_norm_w,
        hgrn_lb_raw, final_norm_w, loss_target[0])
    g_in, g_out, g_s = _weights_exchange(hn_t, dproj_b, dwout_p, small_p)

    w_s = _pack_small(mix_norm_w, attn_out_norm_w, hgrn_out_norm_w, hgrn_lb_raw, final_norm_w)
    m_s = _pack_small(m_mix_norm_w, m_attn_out_norm_w, m_hgrn_out_norm_w, m_hgrn_lb_raw, m_final_norm_w)
    v_s = _pack_small(v_mix_norm_w, v_attn_out_norm_w, v_hgrn_out_norm_w, v_hgrn_lb_raw, v_final_norm_w)
    (d_in, d_out, d_s), (nm_in, nm_out, nm_s), (nv_in, nv_out, nv_s) = _adamw_update(
        (g_in, g_out, g_s), (w_in[0], w_out[0], w_s), (m_w_in[0], m_w_out[0], m_s), (v_w_in[0], v_w_out[0], v_s))

    loss = g_s[ROW_LOSS, 0]
    return (loss, grad_x[None], g_in[None], g_out[None], *_unpack_small(g_s),
            d_in[None], d_out[None], *_unpack_small(d_s),
            nm_in[None], nm_out[None], *_unpack_small(nm_s),
            nv_in[None], nv_out[None], *_unpack_small(nv_s))
```

```python
import functools

import jax
import jax.numpy as jnp
import numpy as np
from jax import lax
from jax.experimental import pallas as pl
from jax.experimental.pallas import tpu as pltpu

F32 = jnp.float32
BF16 = jnp.bfloat16

SEQ = 4096
D_MODEL = 1024
ATTN_WIDTH = 512
HGRN_WIDTH = 512
HEAD_DIM = 64
HGRN_HEADS = 4
HGRN_DIM = 128
HGRN_CHUNK = 64
N_CHUNKS = SEQ // HGRN_CHUNK
IN_COLS = 4096
COL_BLOCK = 512
N_DEV = 8
WOUT_ROWS = D_MODEL // N_DEV
ATTN_BLOCK = 128
DILATIONS = (1, 4, 16)
ROPE_THETA = 500000.0
ROPE_DIMS = 16
ROPE_HALF = 8
NORM_EPS = 1e-6
NEG_BIG = -1e30
LANES = 128

ADAM_LR = 0.001
ADAM_B1 = 0.9
ADAM_B2 = 0.999
ADAM_EPS = 1e-08
ADAM_WD = 0.01
ADAM_STEP = 10

SMALL_ROWS = 48
ROW_MIX, ROW_ATTN, ROW_HGRN, ROW_LB, ROW_FINAL, ROW_LOSS = 0, 8, 16, 24, 32, 40

VMEM_LIMIT = 56 * 1024 * 1024
MESH = pl.DeviceIdType.MESH


def _mm(a, b):
    return lax.dot_general(a, b, (((1,), (0,)), ((), ())), preferred_element_type=F32)


def _mm_nt(a, b):
    return lax.dot_general(a, b, (((1,), (1,)), ((), ())), preferred_element_type=F32)


def _mm_tn(a, b):
    return lax.dot_general(a, b, (((0,), (0,)), ((), ())), preferred_element_type=F32)


def _mm_exact(a, b):
    return lax.dot_general(a, b, (((1,), (0,)), ((), ())), preferred_element_type=F32,
                           precision=lax.Precision.HIGHEST)


def _sigmoid(v):
    return 1.0 / (1.0 + jnp.exp(-v))


def _params(sem=None, **kw):
    return pltpu.CompilerParams(dimension_semantics=sem, vmem_limit_bytes=VMEM_LIMIT, **kw)


def _my_place():
    return lax.axis_index("x"), lax.axis_index("y"), lax.axis_index("c")


def _peer(place, rel):
    x, y, c = place
    return (x ^ ((rel >> 2) & 1), y ^ ((rel >> 1) & 1), c ^ (rel & 1))


def _flat(place):
    x, y, c = place
    return 4 * x + 2 * y + c


ROPE_ROWS = 16


def _rope_tables(pos_row, inv_freq_col, selectors):
    def body(pos_ref, invf_ref, sel_ref, c_ref, sa_ref, sb_ref):
        ang = pos_ref[...].astype(F32) * invf_ref[...]
        cos, sin = jnp.cos(ang), jnp.sin(ang)

        def spread(v, sel):
            hi = v.astype(BF16)
            r1 = v - hi.astype(F32)
            mid = r1.astype(BF16)
            lo = (r1 - mid.astype(F32)).astype(BF16)
            return _mm_tn(hi, sel) + _mm_tn(mid, sel) + _mm_tn(lo, sel)

        e = lax.broadcasted_iota(jnp.int32, (1, LANES), 1) & (HEAD_DIM - 1)
        c_ref[...] = spread(cos, sel_ref[0]) + jnp.where(e < ROPE_DIMS, 0.0, 1.0)
        sa_ref[...] = spread(sin, sel_ref[1])
        sb_ref[...] = spread(sin, sel_ref[2])

    tab = jax.ShapeDtypeStruct((SEQ, LANES), F32)
    vm = lambda: pl.BlockSpec(memory_space=pltpu.VMEM)
    return pl.pallas_call(
        body, name="rope_tables", out_shape=(tab, tab, tab),
        in_specs=[vm(), vm(), vm()], out_specs=(vm(), vm(), vm()), compiler_params=_params(),
    )(pos_row, inv_freq_col, selectors)


def _per_slab(fn, t):
    return jnp.concatenate([fn(t[:, LANES * s:LANES * (s + 1)]) for s in range(t.shape[1] // LANES)], axis=1)


def _rot(t, c, sa, sb):
    return _per_slab(lambda u: u * c + pltpu.roll(u, ROPE_HALF, 1) * sa + pltpu.roll(u, LANES - ROPE_HALF, 1) * sb, t)


def _rot_transposed(g, c, sa, sb):
    return _per_slab(
        lambda u: u * c + pltpu.roll(u * sa, LANES - ROPE_HALF, 1) + pltpu.roll(u * sb, ROPE_HALF, 1), g)


def _gather_project(x, mix_w, w_in, w_out, rc, rsa, rsb):
    tm = 1024
    n_tiles = SEQ // tm
    arrival_of_step = (None, 0, 1, 2, 4, 5, 3, 6)

    def body(order_ref, x_ref, w_ref, win_ref, wout_ref, c_ref, sa_ref, sb_ref,
             proj_ref, hnt_ref, gin_hbm, gout_hbm, qkv_hbm,
             hn_s, w_land, wout_land, stage, sort_stage, slab_tmp, send_sems, recv_sems, local_sems):
        g, i = pl.program_id(0), pl.program_id(1)
        me = _my_place()
        x_, y_, c_ = me
        sibling = (x_, y_, 1 - c_)
        chips = [(1 - x_, y_), (x_, 1 - y_), (1 - x_, 1 - y_)]

        def slab(which, place):
            idx = _flat(place)
            if which == 0:
                return w_land.at[idx]
            return wout_land.at[pl.ds(pl.multiple_of(idx * WOUT_ROWS, WOUT_ROWS), WOUT_ROWS), :]

        def remote(which, k, ref, to, src=None):
            return pltpu.make_async_remote_copy(
                src_ref=ref if src is None else src, dst_ref=ref, send_sem=send_sems.at[8 * which + k],
                recv_sem=recv_sems.at[8 * which + k], device_id=to, device_id_type=MESH)

        def copy(which, k, block, to, src=None):
            return remote(which, k, slab(which, block), to, src)

        def half(which, place, part):
            n = (D_MODEL if which == 0 else WOUT_ROWS) // 2
            if which == 0:
                return w_land.at[_flat(place), pl.ds(n * part, n), :]
            return wout_land.at[pl.ds(pl.multiple_of(_flat(place) * WOUT_ROWS + n * part, n), n), :]

        def first_copies(which):
            src = stage if which == 0 else None
            return ([copy(which, 0, me, sibling, src)]
                    + [copy(which, 1 + j, me, (*chips[j], c_), src) for j in range(2)])

        def relay(which, part):
            frm, to = (chips[1], chips[0]) if part == 0 else (chips[0], chips[1])
            return remote(which, 3 if part == 0 else 7, half(which, (*frm, c_), part), (*to, c_))

        def two_hop_half(which, part):
            return remote(which, 3 if part == 0 else 7, half(which, (*chips[2], c_), part), me)

        def pass_on(which, j):
            return copy(which, 4 + j, (*chips[j], c_), sibling)

        def arrival(which, k):
            if k == 0:
                return copy(which, 0, sibling, me)
            if k <= 2:
                return copy(which, k, (*chips[k - 1], c_), me)
            return copy(which, k, (*chips[k - 4], 1 - c_), me)

        def to_hbm(step):
            idx = order_ref[step]
            cols = pl.ds(pl.multiple_of(idx * COL_BLOCK, COL_BLOCK), COL_BLOCK)
            return pltpu.make_async_copy(w_land.at[idx], gin_hbm.at[:, cols], local_sems.at[step])

        @pl.when((g == 0) & (i == 0))
        def _():
            stage[...] = win_ref[...].astype(BF16)
            w_land[_flat(me)] = stage[...]
            wout_land[pl.ds(pl.multiple_of(_flat(me) * WOUT_ROWS, WOUT_ROWS), WOUT_ROWS), :] = (
                wout_ref[...].astype(BF16))
            for cp in first_copies(0) + first_copies(1)[:1]:
                cp.start()
            to_hbm(0).start()

        for step, k in enumerate(arrival_of_step):
            if k is None:
                continue

            @pl.when((g == step) & (i == 0))
            def _(k=k, step=step):
                if k == 3:
                    two_hop_half(0, 0).wait_recv()
                    two_hop_half(0, 1).wait_recv()
                else:
                    arrival(0, k).wait_recv()
                to_hbm(step).start()
                if 1 <= k <= 3:
                    pass_on(0, k - 1).start()
                if k == 1:
                    relay(0, 1).start()
                    for cp in first_copies(1)[1:]:
                        cp.start()
                if k == 2:
                    relay(0, 0).start()
                if k in (4, 5):
                    arrival(1, k - 3).wait_recv()
                    relay(1, 5 - k).start()

        rows = pl.ds(pl.multiple_of(i * tm, tm), tm)

        @pl.when(g == 0)
        def _():
            xf = x_ref[...]
            ms = jnp.mean(xf * xf, axis=-1, keepdims=True)
            hn = xf * lax.rsqrt(ms + NORM_EPS) * w_ref[...]
            hnt_ref[...] = hn.T.astype(BF16)
            hn_s[rows, :] = hn.astype(BF16)

        group = order_ref[g]

        def sorted_copy():
            per = tm // SORT_RESIDUES
            cols = pl.ds(pl.multiple_of(group * COL_BLOCK, COL_BLOCK), COL_BLOCK)
            buf = i % 2

            def out_copy(tile, b):
                return pltpu.make_async_copy(
                    sort_stage.at[b], qkv_hbm.at[:, pl.ds(pl.multiple_of(tile * per, per), per), cols],
                    local_sems.at[N_DEV + 1 + b])

            @pl.when(i >= 2)
            def _():
                out_copy(i - 2, buf).wait()

            for s in range(COL_BLOCK // LANES):
                slab_tmp[s] = proj_ref[:, LANES * s:LANES * (s + 1)]
            for r in range(SORT_RESIDUES):
                for s in range(COL_BLOCK // LANES):
                    sort_stage[buf, r, :, LANES * s:LANES * (s + 1)] = (
                        slab_tmp.at[s][pl.ds(r, per, stride=SORT_RESIDUES), :])
            out_copy(i, buf).start()

            @pl.when(i == n_tiles - 1)
            def _():
                out_copy(i - 1, 1 - buf).wait()
                out_copy(i, buf).wait()

        @pl.when(group < 2)
        def _():
            proj_ref[...] = _rot(_mm(hn_s[rows, :], w_land[group]), c_ref[...], sa_ref[...], sb_ref[...])
            sorted_copy()

        @pl.when(group == 2)
        def _():
            proj_ref[...] = _mm(hn_s[rows, :], w_land[group])
            sorted_copy()

        @pl.when(group > 2)
        def _():
            proj_ref[...] = _mm(hn_s[rows, :], w_land[group])

        @pl.when((g == N_DEV - 1) & (i == n_tiles - 1))
        def _():
            pass_on(1, 0).start()
            pass_on(1, 1).start()
            two_hop_half(1, 0).wait_recv()
            two_hop_half(1, 1).wait_recv()
            pass_on(1, 2).start()
            for k in (0, 4, 5, 6):
                arrival(1, k).wait_recv()
            for which in (0, 1):
                for cp in (first_copies(which) + [relay(which, part) for part in range(2)]
                           + [pass_on(which, j) for j in range(3)]):
                    cp.wait_send()
            wout_copy = pltpu.make_async_copy(wout_land, gout_hbm, local_sems.at[N_DEV])
            wout_copy.start()
            for step in range(N_DEV):
                to_hbm(step).wait()
            wout_copy.wait()

    me = _my_place()
    x_, y_, c_ = me
    chips = [(1 - x_, y_), (x_, 1 - y_), (1 - x_, 1 - y_)]
    order = jnp.stack([_flat(p) for p in (
        me, (x_, y_, 1 - c_), (*chips[0], c_), (*chips[1], c_), (*chips[0], 1 - c_), (*chips[1], 1 - c_),
        (*chips[2], c_), (*chips[2], 1 - c_))]).astype(jnp.int32)

    first_sweep = lambda g, i, order: (jnp.where(g == 0, i, n_tiles - 1), 0)
    tab = pl.BlockSpec((tm, LANES), lambda g, i, order: (jnp.where(order[g] < 2, i, 0), 0))
    whole = lambda: pl.BlockSpec(memory_space=pltpu.VMEM)
    grid_spec = pltpu.PrefetchScalarGridSpec(
        num_scalar_prefetch=1, grid=(N_DEV, n_tiles),
        in_specs=[pl.BlockSpec((tm, D_MODEL), first_sweep),
                  pl.BlockSpec((1, D_MODEL), lambda g, i, order: (0, 0)),
                  whole(), whole(), tab, tab, tab],
        out_specs=(pl.BlockSpec((None, tm, COL_BLOCK), lambda g, i, order: (order[g], i, 0)),
                   pl.BlockSpec((D_MODEL, tm), lambda g, i, order: (0, jnp.where(g == 0, i, n_tiles - 1))),
                   pl.BlockSpec(memory_space=pl.ANY), pl.BlockSpec(memory_space=pl.ANY),
                   pl.BlockSpec(memory_space=pl.ANY)),
        scratch_shapes=[pltpu.VMEM((SEQ, D_MODEL), BF16),
                        pltpu.VMEM((N_DEV, D_MODEL, COL_BLOCK), BF16),
                        pltpu.VMEM((D_MODEL, D_MODEL), BF16),
                        pltpu.VMEM((D_MODEL, COL_BLOCK), BF16),
                        pltpu.VMEM((2, SORT_RESIDUES, tm // SORT_RESIDUES, COL_BLOCK), F32),
                        pltpu.VMEM((COL_BLOCK // LANES, tm, LANES), F32),
                        pltpu.SemaphoreType.DMA((16,)), pltpu.SemaphoreType.DMA((16,)),
                        pltpu.SemaphoreType.DMA((N_DEV + 3,))])
    proj, hn_t, w_in_g, w_out_g, qkv_sorted = pl.pallas_call(
        body, name="gather_project", grid_spec=grid_spec,
        out_shape=(jax.ShapeDtypeStruct((N_DEV, SEQ, COL_BLOCK), F32), jax.ShapeDtypeStruct((D_MODEL, SEQ), BF16),
                   jax.ShapeDtypeStruct((D_MODEL, IN_COLS), BF16), jax.ShapeDtypeStruct((D_MODEL, D_MODEL), BF16),
                   jax.ShapeDtypeStruct((SORT_RESIDUES, SORT_ROWS, 3 * COL_BLOCK), F32)),
        compiler_params=_params(("arbitrary", "arbitrary")),
    )(order, x, mix_w, w_in, w_out, rc, rsa, rsb)
    return proj, hn_t, w_in_g, w_out_g, qkv_sorted.reshape(SEQ, 3 * COL_BLOCK)


SCORE_SCALE = HEAD_DIM ** -0.5
ATTN_GROUP_FWD = 16
ATTN_GROUP_BWD = 8
BLOCKS_PER_PATTERN = SEQ // ATTN_BLOCK
SORT_RESIDUES = 16
SORT_ROWS = SEQ // SORT_RESIDUES


def _write_band_bias(bias_ref):
    row = lax.broadcasted_iota(jnp.int32, (2 * ATTN_BLOCK, 2 * ATTN_BLOCK), 0) & (ATTN_BLOCK - 1)
    col = lax.broadcasted_iota(jnp.int32, (2 * ATTN_BLOCK, 2 * ATTN_BLOCK), 1)
    for pi, d in enumerate(DILATIONS):
        per = SORT_RESIDUES // d
        ahead = per * (row % (8 * d) - col % (16 * d)) + (row // (8 * d) - col // (16 * d))
        dist = ATTN_BLOCK + ahead
        bias_ref[2 * pi] = jnp.where((dist >= 0) & (dist <= ATTN_BLOCK), 0.0, NEG_BIG)
        bias_ref[2 * pi + 1] = jnp.where(ahead >= 0, 0.0, NEG_BIG)


def _head0_lanes():
    return lax.broadcasted_iota(jnp.int32, (ATTN_BLOCK, LANES), 1) < HEAD_DIM


def _stack_heads(t, h0):
    return jnp.concatenate([jnp.where(h0, t, 0.0), jnp.where(h0, 0.0, t)], axis=0).astype(BF16)


def _block_runs(i, d):
    nblk = BLOCKS_PER_PATTERN // d
    r, n = i // nblk, i % nblk
    kn = jnp.maximum(n - 1, 0)
    rows, keys = [], []
    for c in range(SORT_RESIDUES // d):
        base = SORT_ROWS * (c * d + r)
        rows.append(pl.ds(pl.multiple_of(base + 8 * d * n, 8), 8 * d))
        keys.append(pl.ds(pl.multiple_of(base + 8 * d * kn, 8), 16 * d))
    return rows, keys, (n == 0).astype(jnp.int32)


def _take(ref, runs):
    return jnp.concatenate([ref[run, :] for run in runs], axis=0)


def _put(ref, runs, value, add=False):
    at = 0
    for run in runs:
        piece = value[at:at + run.size]
        if add:
            ref[run, :] += piece
        else:
            ref[run, :] = piece
        at += run.size


def _sort_rows(src_ref, dst_ref):
    for r in range(SORT_RESIDUES):
        dst_ref[SORT_ROWS * r:SORT_ROWS * (r + 1), :] = src_ref[pl.ds(r, SORT_ROWS, stride=SORT_RESIDUES), :]


def _unsort_rows(src_ref, dst_ref):
    for r in range(SORT_RESIDUES):
        dst_ref[pl.ds(r, SORT_ROWS, stride=SORT_RESIDUES), :] = src_ref[SORT_ROWS * r:SORT_ROWS * (r + 1), :]


def _for_each_group(d, n_group, load, compute, store):
    def group(g, carry):
        items = [load(*_block_runs(g * n_group + u, d)) for u in range(n_group)]
        results = [compute(item) for item in items]
        for item, res in zip(items, results):
            store(item, res)
        return carry

    lax.fori_loop(0, BLOCKS_PER_PATTERN // n_group, group, 0)


def _attn_fwd_fused(qkv_sorted):
    n_pat = len(DILATIONS)
    tile2 = (2 * ATTN_BLOCK, LANES)

    def body(q_ref, k_ref, v_ref, o_ref, lse_ref, o_acc, m_acc, l_acc, bias_ref):
        pl.when(pl.program_id(0) == 0)(lambda: _write_band_bias(bias_ref))
        h0 = _head0_lanes()
        for pi, d in enumerate(DILATIONS):
            first, last = pi == 0, pi == n_pat - 1

            def load(rows, keys, which, first=first, pi=pi):
                item = dict(rows=rows, keys=keys, which=2 * pi + which)
                if not first:
                    item.update(o=_take(o_acc, rows), m=[_take(m_acc.at[h], rows) for h in range(2)],
                                l=[_take(l_acc.at[h], rows) for h in range(2)])
                return item

            def compute(item, first=first):
                kb = _take(k_ref, item["keys"]).astype(BF16)
                vb = _take(v_ref, item["keys"]).astype(BF16)
                s = _mm_nt(_stack_heads(_take(q_ref, item["rows"]) * SCORE_SCALE, h0), kb) + bias_ref[item["which"]]
                mb = jnp.max(s, axis=-1, keepdims=True)
                if first:
                    p = jnp.exp(s - mb)
                    mn = jnp.broadcast_to(mb, tile2)
                else:
                    m_old = jnp.concatenate(item["m"], axis=0)
                    mn = jnp.maximum(m_old, mb)
                    alpha = jnp.exp(m_old - mn)
                    p = jnp.exp(s - jnp.concatenate([mn, mn], axis=1))
                ls = jnp.sum(p, axis=-1, keepdims=True)
                pv = _mm(p.astype(BF16), vb)
                if first:
                    return pv, mn, jnp.broadcast_to(ls, tile2)
                o_old = jnp.concatenate([item["o"], item["o"]], axis=0)
                return alpha * o_old + pv, mn, alpha * jnp.concatenate(item["l"], axis=0) + ls

            def store(item, res, last=last):
                rows = item["rows"]
                (o0, o1), (m0, m1), (l0, l1) = ((a[:ATTN_BLOCK], a[ATTN_BLOCK:]) for a in res)
                if last:
                    _put(o_acc, rows, jnp.where(h0, o0 / l0, o1 / l1))
                    _put(lse_ref, rows, jnp.where(h0, m0 + jnp.log(l0), m1 + jnp.log(l1)))
                else:
                    _put(o_acc, rows, jnp.where(h0, o0, o1))
                    for h, (m, l) in enumerate(((m0, l0), (m1, l1))):
                        _put(m_acc.at[h], rows, m)
                        _put(l_acc.at[h], rows, l)

            _for_each_group(d, ATTN_GROUP_FWD, load, compute, store)
        _unsort_rows(o_acc, o_ref)

    slab = lambda g: pl.BlockSpec((SEQ, LANES), functools.partial(lambda hp, g: (0, 4 * g + hp), g=g))
    wide = jax.ShapeDtypeStruct((SEQ, ATTN_WIDTH), F32)
    return pl.pallas_call(
        body, name="attn_fwd", grid=(4,), out_shape=(wide, wide),
        in_specs=[slab(0), slab(1), slab(2)], out_specs=(slab(0), slab(0)),
        scratch_shapes=[pltpu.VMEM((SEQ, LANES), F32), pltpu.VMEM((2, SEQ, LANES), F32),
                        pltpu.VMEM((2, SEQ, LANES), F32),
                        pltpu.VMEM((2 * len(DILATIONS), 2 * ATTN_BLOCK, 2 * ATTN_BLOCK), F32)],
        compiler_params=_params(("arbitrary",)),
    )(qkv_sorted, qkv_sorted, qkv_sorted)


def _attn_bwd_fused(qkv_sorted, d_out, lse_sorted, delta):
    def body(q_ref, k_ref, v_ref, do_ref, lse_ref, del_ref, dq_ref, dk_ref, dv_ref,
             do_s, del_s, dq_s, dk_s, dv_s, bias_ref):
        pl.when(pl.program_id(0) == 0)(lambda: _write_band_bias(bias_ref))
        _sort_rows(do_ref, do_s)
        _sort_rows(del_ref, del_s)
        dk_s[...] = jnp.zeros_like(dk_s)
        dv_s[...] = jnp.zeros_like(dv_s)
        h0 = _head0_lanes()
        for pi, d in enumerate(DILATIONS):
            first = pi == 0

            def load(rows, keys, which, pi=pi):
                return dict(rows=rows, keys=keys, q=_take(q_ref, rows), g=_take(do_s, rows),
                            lse=_take(lse_ref, rows), delta=_take(del_s, rows),
                            k=_take(k_ref, keys).astype(BF16), v=_take(v_ref, keys).astype(BF16),
                            bias=bias_ref[2 * pi + which])

            def per_head(t):
                swapped = pltpu.roll(t, HEAD_DIM, 1)
                both = jnp.concatenate([jnp.where(h0, t, swapped), jnp.where(h0, swapped, t)], axis=0)
                return jnp.concatenate([both, both], axis=1)

            def compute(item):
                q2, g2 = _stack_heads(item["q"] * SCORE_SCALE, h0), _stack_heads(item["g"], h0)
                s = _mm_nt(q2, item["k"]) + item["bias"]
                p = jnp.exp(s - per_head(item["lse"]))
                dp = _mm_nt(g2, item["v"])
                ds = (p * (dp - per_head(item["delta"]))).astype(BF16)
                dq2 = _mm(ds, item["k"])
                dq = jnp.where(h0, dq2[:ATTN_BLOCK], dq2[ATTN_BLOCK:]) * SCORE_SCALE
                return dq, _mm_tn(ds, q2), _mm_tn(p.astype(BF16), g2)

            def store(item, res, first=first):
                _put(dq_s, item["rows"], res[0], add=not first)
                _put(dk_s, item["keys"], res[1], add=True)
                _put(dv_s, item["keys"], res[2], add=True)

            _for_each_group(d, ATTN_GROUP_BWD, load, compute, store)
        _unsort_rows(dq_s, dq_ref)
        _unsort_rows(dk_s, dk_ref)
        _unsort_rows(dv_s, dv_ref)

    slab = lambda g: pl.BlockSpec((SEQ, LANES), functools.partial(lambda hp, g: (0, 4 * g + hp), g=g))
    wide = jax.ShapeDtypeStruct((SEQ, ATTN_WIDTH), F32)
    sorted_slab = pltpu.VMEM((SEQ, LANES), F32)
    return pl.pallas_call(
        body, name="attn_bwd", grid=(4,), out_shape=(wide, wide, wide),
        scratch_shapes=[sorted_slab] * 5 + [pltpu.VMEM((2 * len(DILATIONS), 2 * ATTN_BLOCK, 2 * ATTN_BLOCK), F32)],
        in_specs=[slab(0), slab(1), slab(2), slab(0), slab(0), slab(0)], out_specs=(slab(0), slab(0), slab(0)),
        compiler_params=_params(("arbitrary",)),
    )(qkv_sorted, qkv_sorted, qkv_sorted, d_out, lse_sorted, delta)


def _hgrn_lower_bound(lb_ref):
    r0, r1 = lb_ref[0:1, :], lb_ref[1:2, :]
    mx = jnp.maximum(r0, r1)
    e0, e1 = jnp.exp(r0 - mx), jnp.exp(r1 - mx)
    return e0 / (e0 + e1)


def _hgrn_gates(hq, hf, lb):
    sq = _sigmoid(hq)
    sg = _sigmoid(hf)
    f = lb + (1.0 - lb) * sg
    return hq * sq, sq, sg, f, 1.0 - f, jnp.log(f)


HGRN_PAIR = 4
HGRN_SEQ_BLOCK = 1024
HGRN_GROUP = 4
HGRN_ROWS = HGRN_GROUP * HGRN_CHUNK


def _hgrn_specs(reverse):
    n_blocks = SEQ // HGRN_SEQ_BLOCK
    width = HGRN_PAIR * HGRN_DIM
    blk = (lambda s: n_blocks - 1 - s) if reverse else (lambda s: s)
    cols = lambda g: pl.BlockSpec((None, HGRN_SEQ_BLOCK, width), functools.partial(lambda p, s, g: (g, blk(s), p), g=g))
    pair = pl.BlockSpec((HGRN_SEQ_BLOCK, width), lambda p, s: (blk(s), p))
    lb = pl.BlockSpec((2, width), lambda p, s: (0, p))
    states = pl.BlockSpec((HGRN_PAIR, HGRN_SEQ_BLOCK // HGRN_CHUNK, HGRN_DIM, HGRN_DIM),
                          lambda p, s: (p, blk(s), 0, 0))
    return cols, pair, lb, states


def _chunk_masks():
    ri = lax.broadcasted_iota(jnp.int32, (HGRN_ROWS, HGRN_ROWS), 0)
    ci = lax.broadcasted_iota(jnp.int32, (HGRN_ROWS, HGRN_ROWS), 1)
    same = (ri // HGRN_CHUNK) == (ci // HGRN_CHUNK)
    return same, same & (ri >= ci), same & (ri <= ci)


def _mm_select(sel, v):
    hi = v.astype(BF16)
    r1 = v - hi.astype(F32)
    mid = r1.astype(BF16)
    lo = (r1 - mid.astype(F32)).astype(BF16)
    return _mm(sel, hi) + _mm(sel, mid) + _mm(sel, lo)


def _head_cols(a, h):
    return a[:, HGRN_DIM * h:HGRN_DIM * (h + 1)]


def _hgrn_fwd(proj, lb_raw):
    t, rws = HGRN_CHUNK, HGRN_ROWS

    def body(hq_ref, hf_ref, hi_ref, lb_ref, rec_ref, st_ref, state):
        @pl.when(pl.program_id(1) == 0)
        def _():
            state[...] = jnp.zeros_like(state)

        lb = _hgrn_lower_bound(lb_ref)
        same, causal, _ = _chunk_masks()
        sel = jnp.concatenate([causal, same], axis=0).astype(BF16)

        def group(g, sts):
            rows = pl.ds(pl.multiple_of(g * rws, rws), rws)
            q, _, _, _, k, lf = _hgrn_gates(hq_ref[rows, :], hf_ref[rows, :], lb)
            sums = _mm_select(sel, lf)
            cum, last = sums[:rws], sums[rws:]
            qd = (q * jnp.exp(cum)).astype(BF16)
            ki = (k * jnp.exp(-cum)).astype(BF16)
            ke = (k * jnp.exp(last - cum)).astype(BF16)
            vb = hi_ref[rows, :].astype(BF16)
            dec = jnp.exp(last)
            new_sts, recs = [], []
            for h in range(HGRN_PAIR):
                qd_h, ke_h, vb_h = _head_cols(qd, h), _head_cols(ke, h), _head_cols(vb, h)
                att = jnp.where(causal, _mm_nt(qd_h, _head_cols(ki, h)), 0.0).astype(BF16)
                intra = _mm(att, vb_h)
                st = sts[h]
                outs = []
                for c in range(HGRN_GROUP):
                    sl = slice(c * t, (c + 1) * t)
                    st_ref[h, g * HGRN_GROUP + c] = st
                    outs.append(intra[sl] + _mm_nt(qd_h[sl], st.astype(BF16)))
                    st = st * _head_cols(dec[c * t:c * t + 1, :], h) + _mm_tn(vb_h[sl], ke_h[sl])
                new_sts.append(st)
                recs.append(jnp.concatenate(outs, axis=0))
            rec_ref[rows, :] = jnp.concatenate(recs, axis=1)
            return tuple(new_sts)

        sts = lax.fori_loop(0, HGRN_SEQ_BLOCK // rws, group, tuple(state[h] for h in range(HGRN_PAIR)))
        for h in range(HGRN_PAIR):
            state[h] = sts[h]

    cols, pair, lb, states = _hgrn_specs(reverse=False)
    return pl.pallas_call(
        body, name="hgrn_fwd", grid=(HGRN_HEADS // HGRN_PAIR, SEQ // HGRN_SEQ_BLOCK),
        out_shape=(jax.ShapeDtypeStruct((SEQ, HGRN_WIDTH), F32),
                   jax.ShapeDtypeStruct((HGRN_HEADS, N_CHUNKS, HGRN_DIM, HGRN_DIM), F32)),
        in_specs=[cols(4), cols(5), cols(6), lb], out_specs=(pair, states),
        scratch_shapes=[pltpu.VMEM((HGRN_PAIR, HGRN_DIM, HGRN_DIM), F32)],
        compiler_params=_params(("parallel", "arbitrary")),
    )(proj, proj, proj, lb_raw)


def _hgrn_bwd(proj, lb_raw, d_rec, states):
    t, rws = HGRN_CHUNK, HGRN_ROWS

    def body(hq_ref, hf_ref, hi_ref, lb_ref, do_ref, st_ref, dhq_ref, dhf_ref, dhi_ref, dlb_ref,
             dstate, dlb_acc):
        lb = _hgrn_lower_bound(lb_ref)
        same, causal, anti = _chunk_masks()
        sel = jnp.concatenate([causal, same], axis=0).astype(BF16)
        sel_t = jnp.concatenate([anti, same], axis=1).astype(BF16)
        @pl.when(pl.program_id(1) == 0)
        def _():
            dstate[...] = jnp.zeros_like(dstate)
            dlb_acc[...] = jnp.zeros_like(dlb_acc)

        n_groups = HGRN_SEQ_BLOCK // rws
        chunks = [slice(c * t, (c + 1) * t) for c in range(HGRN_GROUP)]

        def group(i, dsts_in):
            g = n_groups - 1 - i
            rows = pl.ds(pl.multiple_of(g * rws, rws), rws)
            hq = hq_ref[rows, :]
            q, sq, sg, f, k, lf = _hgrn_gates(hq, hf_ref[rows, :], lb)
            sums = _mm_select(sel, lf)
            cum, last = sums[:rws], sums[rws:]
            e_cum, e_inv, e_end, dec = jnp.exp(cum), jnp.exp(-cum), jnp.exp(last - cum), jnp.exp(last)
            qd, ki, ke = q * e_cum, k * e_inv, k * e_end
            qdb, kib, keb = qd.astype(BF16), ki.astype(BF16), ke.astype(BF16)
            vb = hi_ref[rows, :].astype(BF16)
            gb = do_ref[rows, :].astype(BF16)

            dsts_out, per_head = [], []
            for h in range(HGRN_PAIR):
                qdb_h, kib_h, keb_h = _head_cols(qdb, h), _head_cols(kib, h), _head_cols(keb, h)
                vb_h, gb_h = _head_cols(vb, h), _head_cols(gb, h)
                att = jnp.where(causal, _mm_nt(qdb_h, kib_h), 0.0).astype(BF16)
                datt = jnp.where(causal, _mm_nt(gb_h, vb_h), 0.0).astype(BF16)
                dv = _mm_tn(att, gb_h)
                dqd = _mm(datt, kib_h)
                dki = _mm_tn(datt, qdb_h)

                decs = [_head_cols(dec[c * t:c * t + 1, :], h) for c in range(HGRN_GROUP)]
                dsts = [None] * HGRN_GROUP
                dst = dsts_in[h]
                for c in reversed(range(HGRN_GROUP)):
                    dsts[c] = dst
                    dst = dst * decs[c] + _mm_tn(gb_h[chunks[c]], qdb_h[chunks[c]])
                dsts_out.append(dst)

                dv_x, dqd_x, dke, dlast_x = [], [], [], []
                for c, sl in enumerate(chunks):
                    st_prev = st_ref[h, g * HGRN_GROUP + c]
                    dstb = dsts[c].astype(BF16)
                    dv_x.append(_mm_nt(keb_h[sl], dstb))
                    dqd_x.append(_mm(gb_h[sl], st_prev.astype(BF16)))
                    dke.append(_mm(vb_h[sl], dstb))
                    ddec = jnp.sum(dsts[c] * st_prev, axis=0, keepdims=True)
                    dlast_x.append(jnp.broadcast_to(ddec * decs[c], (t, HGRN_DIM)))
                per_head.append((dv + jnp.concatenate(dv_x, axis=0), dqd + jnp.concatenate(dqd_x, axis=0),
                                 dki, jnp.concatenate(dke, axis=0), jnp.concatenate(dlast_x, axis=0)))
            dv, dqd, dki, dke, dlast = (jnp.concatenate(list(parts), axis=1) for parts in zip(*per_head))

            dq = dqd * e_cum
            dk = dki * e_inv + dke * e_end
            dke_ke = dke * ke
            dcum = dqd * qd - dki * ki - dke_ke
            dlf = _mm_select(sel_t, jnp.concatenate([dcum, dke_ke], axis=0)) + dlast
            df = dlf / f - dk
            dhq_ref[rows, :] = dq * (sq * (1.0 + hq * (1.0 - sq)))
            dhf_ref[rows, :] = df * (1.0 - lb) * (sg * (1.0 - sg))
            dhi_ref[rows, :] = dv
            dlb_acc[...] += jnp.sum(df * (1.0 - sg), axis=0, keepdims=True)
            return tuple(dsts_out)

        dsts = lax.fori_loop(0, n_groups, group, tuple(dstate[h] for h in range(HGRN_PAIR)))
        for h in range(HGRN_PAIR):
            dstate[h] = dsts[h]
        g0 = dlb_acc[...] * lb * (1.0 - lb)
        dlb_ref[...] = jnp.concatenate([g0, -g0], axis=0)

    cols, pair, lb_spec, st_spec = _hgrn_specs(reverse=True)
    wide = jax.ShapeDtypeStruct((SEQ, HGRN_WIDTH), F32)
    return pl.pallas_call(
        body, name="hgrn_bwd", grid=(HGRN_HEADS // HGRN_PAIR, SEQ // HGRN_SEQ_BLOCK),
        out_shape=(wide, wide, wide, jax.ShapeDtypeStruct((2, HGRN_WIDTH), F32)),
        in_specs=[cols(4), cols(5), cols(6), lb_spec, pair, st_spec],
        out_specs=(pair, pair, pair, lb_spec),
        scratch_shapes=[pltpu.VMEM((HGRN_PAIR, HGRN_DIM, HGRN_DIM), F32),
                        pltpu.VMEM((1, HGRN_PAIR * HGRN_DIM), F32)],
        compiler_params=_params(("parallel", "arbitrary")),
    )(proj, proj, proj, lb_raw, d_rec, states)


def _group_sum(v, group):
    parts = []
    for s in range(v.shape[1] // LANES):
        slab = v[:, LANES * s:LANES * (s + 1)]
        if group == LANES:
            parts.append(jnp.broadcast_to(jnp.sum(slab, axis=-1, keepdims=True), slab.shape))
        else:
            h0 = lax.broadcasted_iota(jnp.int32, slab.shape, 1) < HEAD_DIM
            s0 = jnp.sum(jnp.where(h0, slab, 0.0), axis=-1, keepdims=True)
            s1 = jnp.sum(jnp.where(h0, 0.0, slab), axis=-1, keepdims=True)
            parts.append(jnp.where(h0, s0, s1))
    return jnp.concatenate(parts, axis=1)


def _mid(attn_o, rec, proj, x, target, w_out_g, attn_w, hgrn_w, final_w):
    tm = 256

    def branch_fwd(o, gate, w, group):
        r = lax.rsqrt(_group_sum(o * o, group) * (1.0 / group) + NORM_EPS)
        nrm = o * r
        sg = _sigmoid(gate)
        return r, nrm, sg, nrm * w * (gate * sg)

    def branch_bwd(dy, r, nrm, sg, gate, w, group):
        silu = gate * sg
        d_gate = dy * nrm * w * (sg * (1.0 + gate * (1.0 - sg)))
        d_w = jnp.sum(dy * nrm * silu, axis=0, keepdims=True)
        dn = dy * w * silu
        d_o = r * (dn - nrm * (_group_sum(dn * nrm, group) * (1.0 / group)))
        return d_o, d_gate, d_w

    def body(o_ref, rec_ref, ag_ref, hg_ref, x_ref, tgt_ref, wout_ref, aw_ref, hw_ref, fw_ref,
             dx2_ref, do_ref, delta_ref, dag_ref, drec_ref, dhg_ref, dwout_ref, dfw_ref, daw_ref, dhw_ref,
             loss_ref, dwout_acc):
        i = pl.program_id(0)

        @pl.when(i == 0)
        def _():
            dwout_acc[...] = jnp.zeros_like(dwout_acc)
            dfw_ref[...] = jnp.zeros_like(dfw_ref)
            daw_ref[...] = jnp.zeros_like(daw_ref)
            dhw_ref[...] = jnp.zeros_like(dhw_ref)
            loss_ref[...] = jnp.zeros_like(loss_ref)

        o, rc, ag, hg = o_ref[...], rec_ref[...], ag_ref[...], hg_ref[...]
        aw, hw, fw = aw_ref[...], hw_ref[...], fw_ref[...]
        ra, na, sga, ya = branch_fwd(o, ag, aw, HEAD_DIM)
        rh, nh, sgh, yh = branch_fwd(rc, hg, hw, HGRN_DIM)
        mixed = jnp.concatenate([ya, yh], axis=1).astype(BF16)
        wout = wout_ref[...]
        x2 = x_ref[...] + _mm(mixed, wout)
        rstd = lax.rsqrt(jnp.mean(x2 * x2, axis=-1, keepdims=True) + NORM_EPS)
        xn = x2 * rstd
        err = xn * fw - tgt_ref[...]
        row_loss = jnp.mean(err * err, axis=-1, keepdims=True)
        loss_ref[...] += 0.5 * jnp.sum(row_loss, axis=0, keepdims=True)
        dy = err * (1.0 / D_MODEL)
        dfw_ref[...] += jnp.sum(dy * xn, axis=0, keepdims=True)
        dxn = dy * fw
        dx2 = rstd * (dxn - xn * jnp.mean(dxn * xn, axis=-1, keepdims=True))
        dx2_ref[...] = dx2
        dx2b = dx2.astype(BF16)
        dwout_acc[...] += _mm_tn(mixed, dx2b)

        @pl.when(i == pl.num_programs(0) - 1)
        def _():
            dwout_ref[...] = dwout_acc[...].astype(BF16)

        dmixed = _mm_nt(dx2b, wout)

        d_o, d_ag, d_aw = branch_bwd(dmixed[:, :ATTN_WIDTH], ra, na, sga, ag, aw, HEAD_DIM)
        d_rec, d_hg, d_hw = branch_bwd(dmixed[:, ATTN_WIDTH:], rh, nh, sgh, hg, hw, HGRN_DIM)
        do_ref[...] = d_o
        delta_ref[...] = _group_sum(d_o * o, HEAD_DIM)
        dag_ref[...] = d_ag
        drec_ref[...] = d_rec
        dhg_ref[...] = d_hg
        daw_ref[...] += d_aw
        dhw_ref[...] += d_hw

    half = lambda: pl.BlockSpec((tm, COL_BLOCK), lambda i: (i, 0))
    full = lambda: pl.BlockSpec((tm, D_MODEL), lambda i: (i, 0))
    fixed = lambda r, c: pl.BlockSpec((r, c), lambda i: (0, 0))
    wide = jax.ShapeDtypeStruct((SEQ, COL_BLOCK), F32)
    return pl.pallas_call(
        body, name="mid", grid=(SEQ // tm,),
        out_shape=(jax.ShapeDtypeStruct((SEQ, D_MODEL), F32), wide, wide, wide, wide, wide,
                   jax.ShapeDtypeStruct((D_MODEL, D_MODEL), BF16),
                   jax.ShapeDtypeStruct((1, D_MODEL), F32), jax.ShapeDtypeStruct((1, COL_BLOCK), F32),
                   jax.ShapeDtypeStruct((1, COL_BLOCK), F32), jax.ShapeDtypeStruct((1, 1), F32)),
        scratch_shapes=[pltpu.VMEM((D_MODEL, D_MODEL), F32)],
        in_specs=[half(), half(),
                  pl.BlockSpec((None, tm, COL_BLOCK), lambda i: (3, i, 0)),
                  pl.BlockSpec((None, tm, COL_BLOCK), lambda i: (7, i, 0)),
                  full(), full(), fixed(D_MODEL, D_MODEL), fixed(1, COL_BLOCK), fixed(1, COL_BLOCK),
                  fixed(1, D_MODEL)],
        out_specs=(full(), half(), half(), half(), half(), half(), fixed(D_MODEL, D_MODEL),
                   fixed(1, D_MODEL), fixed(1, COL_BLOCK), fixed(1, COL_BLOCK), fixed(1, 1)),
        compiler_params=_params(("arbitrary",)),
    )(attn_o, rec, proj, proj, x, target, w_out_g, attn_w, hgrn_w, final_w)


def _in_proj_bwd_rows(d_groups, w_g, x, dx2, mix_w, rc, rsa, rsb):
    tm = 256

    def body(*refs):
        dg_refs = refs[:N_DEV]
        wg_ref, x_ref, dx2_ref, w_ref, c_ref, sa_ref, sb_ref, gx_ref, dpb_ref, dmw_ref = refs[N_DEV:]

        @pl.when(pl.program_id(0) == 0)
        def _():
            dmw_ref[...] = jnp.zeros_like(dmw_ref)

        parts = []
        for j in range(N_DEV):
            dp = dg_refs[j][...]
            if j < 2:
                dp = _rot_transposed(dp, c_ref[...], sa_ref[...], sb_ref[...])
            parts.append(dp.astype(BF16))
        dpb = jnp.concatenate(parts, axis=1)
        for j in range(N_DEV):
            dpb_ref[j] = parts[j]
        g = _mm_nt(dpb, wg_ref[...])
        xf = x_ref[...]
        rstd = lax.rsqrt(jnp.mean(xf * xf, axis=-1, keepdims=True) + NORM_EPS)
        xn = xf * rstd
        dmw_ref[...] += jnp.sum(g * xn, axis=0, keepdims=True)
        gw = g * w_ref[...]
        gx_ref[...] = dx2_ref[...] + rstd * (gw - xn * jnp.mean(gw * xn, axis=-1, keepdims=True))

    tile = lambda cols: pl.BlockSpec((tm, cols), lambda i: (i, 0))
    fixed = lambda r, c: pl.BlockSpec((r, c), lambda i: (0, 0))
    return pl.pallas_call(
        body, name="in_proj_bwd_rows", grid=(SEQ // tm,),
        out_shape=(jax.ShapeDtypeStruct((SEQ, D_MODEL), F32), jax.ShapeDtypeStruct((N_DEV, SEQ, COL_BLOCK), BF16),
                   jax.ShapeDtypeStruct((1, D_MODEL), F32)),
        in_specs=[tile(COL_BLOCK) for _ in range(N_DEV)] + [
            pl.BlockSpec((D_MODEL, IN_COLS), lambda i: (0, 0), pipeline_mode=pl.Buffered(1)),
            tile(D_MODEL), tile(D_MODEL), fixed(1, D_MODEL), tile(LANES), tile(LANES), tile(LANES)],
        out_specs=(tile(D_MODEL), pl.BlockSpec((N_DEV, tm, COL_BLOCK), lambda i: (0, i, 0)), fixed(1, D_MODEL)),
        compiler_params=_params(("arbitrary",)),
    )(*d_groups, w_g, x, dx2, mix_w, rc, rsa, rsb)


def _weights_exchange(hn_t, dproj_b, dwout_p, small_p):
    n_chips = N_DEV // 2
    rb = 128
    S1_IN, S1_OUT, SMALL, S2_IN, S2_OUT = 0, 4, 8, 15, 18
    rel_of_pair = (1, 2, 3, 0)

    def body(order_ref, hnt_ref, dp_ref, dwout_ref, small_ref, gin_ref, gout_ref, gs_ref,
             part, s1_send, s1_in, s1_out, fwd_in, fwd_out, s2_in, s2_out, land_s, send_sems, recv_sems):
        t = pl.program_id(0)
        me = _my_place()
        x, y, c = me
        my_chip = 2 * x + y
        sibling = (x, y, 1 - c)

        def remote(slot, src, dst, to):
            return pltpu.make_async_remote_copy(src_ref=src, dst_ref=dst, send_sem=send_sems.at[slot],
                                                recv_sem=recv_sems.at[slot], device_id=to, device_id_type=MESH)

        def s1_in_copy(pair):
            return remote(S1_IN + pair, s1_send.at[pair], s1_in.at[pair], sibling)

        def s1_out_copy(pair):
            q = my_chip ^ rel_of_pair[pair]
            return remote(S1_OUT + pair, dwout_ref.at[q, 1 - c], s1_out.at[pair], sibling)

        def s2_copies(rel):
            peer = _peer(me, 2 * rel)
            return [remote(S2_IN + rel - 1, fwd_in.at[rel - 1], s2_in.at[rel - 1], peer),
                    remote(S2_OUT + rel - 1, fwd_out.at[rel - 1], s2_out.at[rel - 1], peer)]

        def small_copy(rel):
            return remote(SMALL + rel - 1, small_ref, land_s.at[rel], _peer(me, rel))

        @pl.when(t == 0)
        def _():
            land_s[0] = small_ref[...]
            for pair in range(n_chips):
                s1_out_copy(pair).start()
            for rel in range(1, N_DEV):
                small_copy(rel).start()

        part[...] = _mm(hnt_ref[...], dp_ref[...])

        def rows_loop(n_rows, fn):
            def step(b, carry):
                fn(pl.ds(pl.multiple_of(b * rb, rb), rb))
                return carry
            lax.fori_loop(0, n_rows // rb, step, 0)

        for pair, rel in enumerate(rel_of_pair):
            @pl.when(t == 2 * pair)
            def _(pair=pair):
                s1_send[pair] = part[...].astype(BF16)
                s1_in_copy(pair).start()

            @pl.when(t == 2 * pair + 1)
            def _(pair=pair, rel=rel):
                q = my_chip ^ rel
                s1_in_copy(pair).wait_recv()
                s1_out_copy(pair).wait_recv()
                dst_in = fwd_in.at[rel - 1] if rel else gin_ref
                dst_out = fwd_out.at[rel - 1] if rel else gout_ref

                def add_in(rows):
                    dst_in[rows, :] = (part[rows, :] + s1_in[pair, rows, :].astype(F32)).astype(dst_in.dtype)

                def add_out(rows):
                    dst_out[rows, :] = (dwout_ref[q, c, rows, :].astype(F32)
                                        + s1_out[pair, rows, :].astype(F32)).astype(dst_out.dtype)

                rows_loop(D_MODEL, add_in)
                rows_loop(WOUT_ROWS, add_out)
                if rel:
                    for cp in s2_copies(rel):
                        cp.start()

        @pl.when(t == N_DEV - 1)
        def _():
            for rel in range(1, n_chips):
                for cp in s2_copies(rel):
                    cp.wait_recv()

            def total_in(rows):
                g = gin_ref[rows, :]
                for rel in range(1, n_chips):
                    g = g + s2_in[rel - 1, rows, :].astype(F32)
                gin_ref[rows, :] = g

            def total_out(rows):
                g = gout_ref[rows, :]
                for rel in range(1, n_chips):
                    g = g + s2_out[rel - 1, rows, :].astype(F32)
                gout_ref[rows, :] = g

            rows_loop(D_MODEL, total_in)
            rows_loop(WOUT_ROWS, total_out)

            for rel in range(1, N_DEV):
                small_copy(rel).wait_recv()
            my_flat = _flat(me)
            g = land_s[my_flat ^ 0]
            for dev in range(1, N_DEV):
                g = g + land_s[my_flat ^ dev]
            gs_ref[...] = g

            for pair in range(n_chips):
                s1_in_copy(pair).wait_send()
                s1_out_copy(pair).wait_send()
            for rel in range(1, n_chips):
                for cp in s2_copies(rel):
                    cp.wait_send()
            for rel in range(1, N_DEV):
                small_copy(rel).wait_send()

    place_x, place_y, place_c = _my_place()
    my_chip = 2 * place_x + place_y
    order = jnp.stack([2 * (my_chip ^ rel) + core for rel in rel_of_pair
                       for core in (1 - place_c, place_c)]).astype(jnp.int32)

    whole = lambda: pl.BlockSpec(memory_space=pltpu.VMEM)
    in_blocks = lambda n: pltpu.VMEM((n, D_MODEL, COL_BLOCK), BF16)
    out_blocks = lambda n: pltpu.VMEM((n, WOUT_ROWS, D_MODEL), BF16)
    grid_spec = pltpu.PrefetchScalarGridSpec(
        num_scalar_prefetch=1, grid=(N_DEV,),
        in_specs=[pl.BlockSpec((D_MODEL, SEQ), lambda t, order: (0, 0), pipeline_mode=pl.Buffered(1)),
                  pl.BlockSpec((None, SEQ, COL_BLOCK), lambda t, order: (order[t], 0, 0)), whole(), whole()],
        out_specs=(whole(), whole(), whole()),
        scratch_shapes=[pltpu.VMEM((D_MODEL, COL_BLOCK), F32), in_blocks(n_chips), in_blocks(n_chips),
                        out_blocks(n_chips), in_blocks(n_chips - 1), out_blocks(n_chips - 1),
                        in_blocks(n_chips - 1), out_blocks(n_chips - 1),
                        pltpu.VMEM((N_DEV, SMALL_ROWS, LANES), F32),
                        pltpu.SemaphoreType.DMA((21,)), pltpu.SemaphoreType.DMA((21,))])
    return pl.pallas_call(
        body, name="weights_exchange", grid_spec=grid_spec,
        out_shape=(jax.ShapeDtypeStruct((D_MODEL, COL_BLOCK), F32), jax.ShapeDtypeStruct((WOUT_ROWS, D_MODEL), F32),
                   jax.ShapeDtypeStruct((SMALL_ROWS, LANES), F32)),
        compiler_params=_params(("arbitrary",)),
    )(order, hn_t, dproj_b, dwout_p.reshape(n_chips, 2, WOUT_ROWS, D_MODEL), small_p)


def _adamw(w, g, m, v):
    m = ADAM_B1 * m + (1.0 - ADAM_B1) * g
    v = ADAM_B2 * v + (1.0 - ADAM_B2) * (g * g)
    m_hat = m / (1.0 - ADAM_B1 ** ADAM_STEP)
    v_hat = v / (1.0 - ADAM_B2 ** ADAM_STEP)
    delta = -ADAM_LR * (m_hat / (jnp.sqrt(v_hat) + ADAM_EPS) + ADAM_WD * w)
    return delta, m, v


def _adamw_update(grads, weights, m_old, v_old):
    rb = 256

    def body(*refs):
        g_refs, w_refs, m_refs, v_refs = refs[0:3], refs[3:6], refs[6:9], refs[9:12]
        d_refs, nm_refs, nv_refs = refs[12:15], refs[15:18], refs[18:21]
        for k in range(3):
            n_rows = g_refs[k].shape[0]
            step_rows = min(rb, n_rows)

            def step(b, carry, k=k, step_rows=step_rows):
                rows = pl.ds(pl.multiple_of(b * step_rows, 8), step_rows)
                delta, nm, nv = _adamw(w_refs[k][rows, :], g_refs[k][rows, :], m_refs[k][rows, :], v_refs[k][rows, :])
                d_refs[k][rows, :] = delta
                nm_refs[k][rows, :] = nm
                nv_refs[k][rows, :] = nv
                return carry

            lax.fori_loop(0, n_rows // step_rows, step, 0)

    shapes = tuple(jax.ShapeDtypeStruct(g.shape, F32) for g in grads)
    vm = lambda: pl.BlockSpec(memory_space=pltpu.VMEM)
    outs = pl.pallas_call(
        body, name="adamw_update", out_shape=shapes * 3,
        in_specs=[vm() for _ in range(12)], out_specs=tuple(vm() for _ in range(9)),
        compiler_params=_params(),
    )(*grads, *weights, *m_old, *v_old)
    return outs[0:3], outs[3:6], outs[6:9]


def _pack_small(mix, attn, hgrn, lb, final, loss=None):
    def rows8(a):
        a = a.reshape(-1, LANES)
        return jnp.pad(a, ((0, 8 - a.shape[0]), (0, 0)))
    last = jnp.zeros((8, LANES), F32) if loss is None else jnp.pad(loss.reshape(1, 1), ((0, 7), (0, LANES - 1)))
    return jnp.concatenate([rows8(mix), rows8(attn), rows8(hgrn), rows8(lb), rows8(final), last], axis=0)


def _unpack_small(slab):
    return (slab[ROW_MIX:ROW_MIX + 8].reshape(1, D_MODEL), slab[ROW_ATTN:ROW_ATTN + 4].reshape(1, ATTN_WIDTH),
            slab[ROW_HGRN:ROW_HGRN + 4].reshape(1, HGRN_WIDTH), slab[ROW_LB:ROW_LB + 8].reshape(2, HGRN_WIDTH),
            slab[ROW_FINAL:ROW_FINAL + 8].reshape(D_MODEL))


def _rope(pos_row):
    j = np.arange(ROPE_ROWS)
    inv = np.where(j < ROPE_HALF, ROPE_THETA ** (-(j % ROPE_HALF) * (2.0 / ROPE_DIMS)), 0.0)
    e = np.arange(LANES) % HEAD_DIM
    hit = (j[:, None] == (e % ROPE_HALF)[None, :]) & (j[:, None] < ROPE_HALF)
    sel = np.stack([hit & (e < ROPE_DIMS), hit & (e >= ROPE_HALF) & (e < ROPE_DIMS),
                    -1.0 * (hit & (e < ROPE_HALF))]).astype(np.float32)
    return _rope_tables(pos_row, jnp.asarray(inv.astype(np.float32).reshape(ROPE_ROWS, 1)),
                        jnp.asarray(sel, dtype=BF16))


def _local_step(x, proj, qkv_sorted, w_in_g, w_out_g, tables, mix_w, attn_w, hgrn_w, lb_raw, final_w, target):
    rc, rsa, rsb = tables
    attn_o, lse = _attn_fwd_fused(qkv_sorted)
    rec, states = _hgrn_fwd(proj, lb_raw)

    (dx2, d_o, delta, d_ag, d_rec, d_hg, dwout_p, d_final, d_attn_w, d_hgrn_w, loss) = _mid(
        attn_o, rec, proj, x, target, w_out_g, attn_w, hgrn_w, final_w.reshape(1, D_MODEL))

    dqkv = _attn_bwd_fused(qkv_sorted, d_o, lse, delta)
    d_hq, d_hf, d_hi, d_lb = _hgrn_bwd(proj, lb_raw, d_rec, states)

    grad_x, dproj_b, d_mix = _in_proj_bwd_rows(
        (dqkv[0], dqkv[1], dqkv[2], d_ag, d_hq, d_hf, d_hi, d_hg), w_in_g, x, dx2, mix_w, rc, rsa, rsb)
    small_p = _pack_small(d_mix, d_attn_w, d_hgrn_w, d_lb, d_final, loss)
    return grad_x, dproj_b, dwout_p, small_p


def kernel(x, positions, w_in, w_out, mix_norm_w, attn_out_norm_w, hgrn_out_norm_w, hgrn_lb_raw, final_norm_w, loss_target, m_w_in, m_w_out, m_mix_norm_w, m_attn_out_norm_w, m_hgrn_out_norm_w, m_hgrn_lb_raw, m_final_norm_w, v_w_in, v_w_out, v_mix_norm_w, v_attn_out_norm_w, v_hgrn_out_norm_w, v_hgrn_lb_raw, v_final_norm_w):
    tables = _rope(positions)
    proj, hn_t, w_in_g, w_out_g, qkv_sorted = _gather_project(x[0], mix_norm_w, w_in[0], w_out[0], *tables)
    grad_x, dproj_b, dwout_p, small_p = _local_step(
        x[0], proj, qkv_sorted, w_in_g, w_out_g, tables, mix_norm_w, attn_out_norm_w, hgrn_out_norm_w,
        hgrn_lb_raw, final_norm_w, loss_target[0])
    g_in, g_out, g_s = _weights_exchange(hn_t, dproj_b, dwout_p, small_p)

    w_s = _pack_small(mix_norm_w, attn_out_norm_w, hgrn_out_norm_w, hgrn_lb_raw, final_norm_w)
    m_s = _pack_small(m_mix_norm_w, m_attn_out_norm_w, m_hgrn_out_norm_w, m_hgrn_lb_raw, m_final_norm_w)
    v_s = _pack_small(v_mix_norm_w, v_attn_out_norm_w, v_hgrn_out_norm_w, v_hgrn_lb_raw, v_final_norm_w)
    (d_in, d_out, d_s), (nm_in, nm_out, nm_s), (nv_in, nv_out, nv_s) = _adamw_update(
        (g_in, g_out, g_s), (w_in[0], w_out[0], w_s), (m_w_in[0], m_w_out[0], m_s), (v_w_in[0], v_w_out[0], v_s))

    loss = g_s[ROW_LOSS, 0]
    return (loss, grad_x[None], g_in[None], g_out[None], *_unpack_small(g_s),
            d_in[None], d_out[None], *_unpack_small(d_s),
            nm_in[None], nm_out[None], *_unpack_small(nm_s),
            nv_in[None], nv_out[None], *_unpack_small(nv_s))
```

```python
import functools

import jax
import jax.numpy as jnp
import numpy as np
from jax import lax
from jax.experimental import pallas as pl
from jax.experimental.pallas import tpu as pltpu

F32 = jnp.float32
BF16 = jnp.bfloat16

SEQ = 4096
D_MODEL = 1024
ATTN_WIDTH = 512
HGRN_WIDTH = 512
HEAD_DIM = 64
HGRN_HEADS = 4
HGRN_DIM = 128
HGRN_CHUNK = 64
N_CHUNKS = SEQ // HGRN_CHUNK
IN_COLS = 4096
COL_BLOCK = 512
N_DEV = 8
WOUT_ROWS = D_MODEL // N_DEV
ATTN_BLOCK = 128
DILATIONS = (1, 4, 16)
ROPE_THETA = 500000.0
ROPE_DIMS = 16
ROPE_HALF = 8
NORM_EPS = 1e-6
NEG_BIG = -1e30
LANES = 128

ADAM_LR = 0.001
ADAM_B1 = 0.9
ADAM_B2 = 0.999
ADAM_EPS = 1e-08
ADAM_WD = 0.01
ADAM_STEP = 10

SMALL_ROWS = 48
ROW_MIX, ROW_ATTN, ROW_HGRN, ROW_LB, ROW_FINAL, ROW_LOSS = 0, 8, 16, 24, 32, 40

VMEM_LIMIT = 56 * 1024 * 1024
MESH = pl.DeviceIdType.MESH


def _mm(a, b):
    return lax.dot_general(a, b, (((1,), (0,)), ((), ())), preferred_element_type=F32)


def _mm_nt(a, b):
    return lax.dot_general(a, b, (((1,), (1,)), ((), ())), preferred_element_type=F32)


def _mm_tn(a, b):
    return lax.dot_general(a, b, (((0,), (0,)), ((), ())), preferred_element_type=F32)


def _mm_exact(a, b):
    return lax.dot_general(a, b, (((1,), (0,)), ((), ())), preferred_element_type=F32,
                           precision=lax.Precision.HIGHEST)


def _sigmoid(v):
    return 1.0 / (1.0 + jnp.exp(-v))


def _params(sem=None, **kw):
    return pltpu.CompilerParams(dimension_semantics=sem, vmem_limit_bytes=VMEM_LIMIT, **kw)


def _my_place():
    return lax.axis_index("x"), lax.axis_index("y"), lax.axis_index("c")


def _peer(place, rel):
    x, y, c = place
    return (x ^ ((rel >> 2) & 1), y ^ ((rel >> 1) & 1), c ^ (rel & 1))


def _flat(place):
    x, y, c = place
    return 4 * x + 2 * y + c


ROPE_ROWS = 16


def _rope_tables(pos_row, inv_freq_col, selectors):
    def body(pos_ref, invf_ref, sel_ref, c_ref, sa_ref, sb_ref):
        ang = pos_ref[...].astype(F32) * invf_ref[...]
        cos, sin = jnp.cos(ang), jnp.sin(ang)

        def spread(v, sel):
            hi = v.astype(BF16)
            r1 = v - hi.astype(F32)
            mid = r1.astype(BF16)
            lo = (r1 - mid.astype(F32)).astype(BF16)
            return _mm_tn(hi, sel) + _mm_tn(mid, sel) + _mm_tn(lo, sel)

        e = lax.broadcasted_iota(jnp.int32, (1, LANES), 1) & (HEAD_DIM - 1)
        c_ref[...] = spread(cos, sel_ref[0]) + jnp.where(e < ROPE_DIMS, 0.0, 1.0)
        sa_ref[...] = spread(sin, sel_ref[1])
        sb_ref[...] = spread(sin, sel_ref[2])

    tab = jax.ShapeDtypeStruct((SEQ, LANES), F32)
    vm = lambda: pl.BlockSpec(memory_space=pltpu.VMEM)
    return pl.pallas_call(
        body, name="rope_tables", out_shape=(tab, tab, tab),
        in_specs=[vm(), vm(), vm()], out_specs=(vm(), vm(), vm()), compiler_params=_params(),
    )(pos_row, inv_freq_col, selectors)


def _per_slab(fn, t):
    return jnp.concatenate([fn(t[:, LANES * s:LANES * (s + 1)]) for s in range(t.shape[1] // LANES)], axis=1)


def _rot(t, c, sa, sb):
    return _per_slab(lambda u: u * c + pltpu.roll(u, ROPE_HALF, 1) * sa + pltpu.roll(u, LANES - ROPE_HALF, 1) * sb, t)


def _rot_transposed(g, c, sa, sb):
    return _per_slab(
        lambda u: u * c + pltpu.roll(u * sa, LANES - ROPE_HALF, 1) + pltpu.roll(u * sb, ROPE_HALF, 1), g)


def _gather_project(x, mix_w, w_in, w_out, rc, rsa, rsb):
    tm = 1024
    n_tiles = SEQ // tm
    arrival_of_step = (None, 0, 1, 2, 4, 5, 3, 6)

    def body(order_ref, x_ref, w_ref, win_ref, wout_ref, c_ref, sa_ref, sb_ref,
             proj_ref, hnt_ref, gin_hbm, gout_hbm, qkv_hbm,
             hn_s, w_land, wout_land, stage, sort_stage, slab_tmp, send_sems, recv_sems, local_sems):
        g, i = pl.program_id(0), pl.program_id(1)
        me = _my_place()
        x_, y_, c_ = me
        sibling = (x_, y_, 1 - c_)
        chips = [(1 - x_, y_), (x_, 1 - y_), (1 - x_, 1 - y_)]

        def slab(which, place):
            idx = _flat(place)
            if which == 0:
                return w_land.at[idx]
            return wout_land.at[pl.ds(pl.multiple_of(idx * WOUT_ROWS, WOUT_ROWS), WOUT_ROWS), :]

        def remote(which, k, ref, to, src=None):
            return pltpu.make_async_remote_copy(
                src_ref=ref if src is None else src, dst_ref=ref, send_sem=send_sems.at[8 * which + k],
                recv_sem=recv_sems.at[8 * which + k], device_id=to, device_id_type=MESH)

        def copy(which, k, block, to, src=None):
            return remote(which, k, slab(which, block), to, src)

        def half(which, place, part):
            n = (D_MODEL if which == 0 else WOUT_ROWS) // 2
            if which == 0:
                return w_land.at[_flat(place), pl.ds(n * part, n), :]
            return wout_land.at[pl.ds(pl.multiple_of(_flat(place) * WOUT_ROWS + n * part, n), n), :]

        def first_copies(which):
            src = stage if which == 0 else None
            return ([copy(which, 0, me, sibling, src)]
                    + [copy(which, 1 + j, me, (*chips[j], c_), src) for j in range(2)])

        def relay(which, part):
            frm, to = (chips[1], chips[0]) if part == 0 else (chips[0], chips[1])
            return remote(which, 3 if part == 0 else 7, half(which, (*frm, c_), part), (*to, c_))

        def two_hop_half(which, part):
            return remote(which, 3 if part == 0 else 7, half(which, (*chips[2], c_), part), me)

        def pass_on(which, j):
            return copy(which, 4 + j, (*chips[j], c_), sibling)

        def arrival(which, k):
            if k == 0:
                return copy(which, 0, sibling, me)
            if k <= 2:
                return copy(which, k, (*chips[k - 1], c_), me)
            return copy(which, k, (*chips[k - 4], 1 - c_), me)

        def to_hbm(step):
            idx = order_ref[step]
            cols = pl.ds(pl.multiple_of(idx * COL_BLOCK, COL_BLOCK), COL_BLOCK)
            return pltpu.make_async_copy(w_land.at[idx], gin_hbm.at[:, cols], local_sems.at[step])

        @pl.when((g == 0) & (i == 0))
        def _():
            stage[...] = win_ref[...].astype(BF16)
            w_land[_flat(me)] = stage[...]
            wout_land[pl.ds(pl.multiple_of(_flat(me) * WOUT_ROWS, WOUT_ROWS), WOUT_ROWS), :] = (
                wout_ref[...].astype(BF16))
            for cp in first_copies(0) + first_copies(1)[:1]:
                cp.start()
            to_hbm(0).start()

        for step, k in enumerate(arrival_of_step):
            if k is None:
                continue

            @pl.when((g == step) & (i == 0))
            def _(k=k, step=step):
                if k == 3:
                    two_hop_half(0, 0).wait_recv()
                    two_hop_half(0, 1).wait_recv()
                else:
                    arrival(0, k).wait_recv()
                to_hbm(step).start()
                if 1 <= k <= 3:
                    pass_on(0, k - 1).start()
                if k == 1:
                    relay(0, 1).start()
                    for cp in first_copies(1)[1:]:
                        cp.start()
                if k == 2:
                    relay(0, 0).start()
                if k in (4, 5):
                    arrival(1, k - 3).wait_recv()
                    relay(1, 5 - k).start()

        rows = pl.ds(pl.multiple_of(i * tm, tm), tm)

        @pl.when(g == 0)
        def _():
            xf = x_ref[...]
            ms = jnp.mean(xf * xf, axis=-1, keepdims=True)
            hn = xf * lax.rsqrt(ms + NORM_EPS) * w_ref[...]
            hnt_ref[...] = hn.T.astype(BF16)
            hn_s[rows, :] = hn.astype(BF16)

        group = order_ref[g]

        def sorted_copy():
            per = tm // SORT_RESIDUES
            cols = pl.ds(pl.multiple_of(group * COL_BLOCK, COL_BLOCK), COL_BLOCK)
            buf = i % 2

            def out_copy(tile, b):
                return pltpu.make_async_copy(
                    sort_stage.at[b], qkv_hbm.at[:, pl.ds(pl.multiple_of(tile * per, per), per), cols],
                    local_sems.at[N_DEV + 1 + b])

            @pl.when(i >= 2)
            def _():
                out_copy(i - 2, buf).wait()

            for s in range(COL_BLOCK // LANES):
                slab_tmp[s] = proj_ref[:, LANES * s:LANES * (s + 1)]
            for r in range(SORT_RESIDUES):
                for s in range(COL_BLOCK // LANES):
                    sort_stage[buf, r, :, LANES * s:LANES * (s + 1)] = (
                        slab_tmp.at[s][pl.ds(r, per, stride=SORT_RESIDUES), :])
            out_copy(i, buf).start()

            @pl.when(i == n_tiles - 1)
            def _():
                out_copy(i - 1, 1 - buf).wait()
                out_copy(i, buf).wait()

        @pl.when(group < 2)
        def _():
            proj_ref[...] = _rot(_mm(hn_s[rows, :], w_land[group]), c_ref[...], sa_ref[...], sb_ref[...])
            sorted_copy()

        @pl.when(group == 2)
        def _():
            proj_ref[...] = _mm(hn_s[rows, :], w_land[group])
            sorted_copy()

        @pl.when(group > 2)
        def _():
            proj_ref[...] = _mm(hn_s[rows, :], w_land[group])

        @pl.when((g == N_DEV - 1) & (i == n_tiles - 1))
        def _():
            pass_on(1, 0).start()
            pass_on(1, 1).start()
            two_hop_half(1, 0).wait_recv()
            two_hop_half(1, 1).wait_recv()
            pass_on(1, 2).start()
            for k in (0, 4, 5, 6):
                arrival(1, k).wait_recv()
            for which in (0, 1):
                for cp in (first_copies(which) + [relay(which, part) for part in range(2)]
                           + [pass_on(which, j) for j in range(3)]):
                    cp.wait_send()
            wout_copy = pltpu.make_async_copy(wout_land, gout_hbm, local_sems.at[N_DEV])
            wout_copy.start()
            for step in range(N_DEV):
                to_hbm(step).wait()
            wout_copy.wait()

    me = _my_place()
    x_, y_, c_ = me
    chips = [(1 - x_, y_), (x_, 1 - y_), (1 - x_, 1 - y_)]
    order = jnp.stack([_flat(p) for p in (
        me, (x_, y_, 1 - c_), (*chips[0], c_), (*chips[1], c_), (*chips[0], 1 - c_), (*chips[1], 1 - c_),
        (*chips[2], c_), (*chips[2], 1 - c_))]).astype(jnp.int32)

    first_sweep = lambda g, i, order: (jnp.where(g == 0, i, n_tiles - 1), 0)
    tab = pl.BlockSpec((tm, LANES), lambda g, i, order: (jnp.where(order[g] < 2, i, 0), 0))
    whole = lambda: pl.BlockSpec(memory_space=pltpu.VMEM)
    grid_spec = pltpu.PrefetchScalarGridSpec(
        num_scalar_prefetch=1, grid=(N_DEV, n_tiles),
        in_specs=[pl.BlockSpec((tm, D_MODEL), first_sweep),
                  pl.BlockSpec((1, D_MODEL), lambda g, i, order: (0, 0)),
                  whole(), whole(), tab, tab, tab],
        out_specs=(pl.BlockSpec((None, tm, COL_BLOCK), lambda g, i, order: (order[g], i, 0)),
                   pl.BlockSpec((D_MODEL, tm), lambda g, i, order: (0, jnp.where(g == 0, i, n_tiles - 1))),
                   pl.BlockSpec(memory_space=pl.ANY), pl.BlockSpec(memory_space=pl.ANY),
                   pl.BlockSpec(memory_space=pl.ANY)),
        scratch_shapes=[pltpu.VMEM((SEQ, D_MODEL), BF16),
                        pltpu.VMEM((N_DEV, D_MODEL, COL_BLOCK), BF16),
                        pltpu.VMEM((D_MODEL, D_MODEL), BF16),
                        pltpu.VMEM((D_MODEL, COL_BLOCK), BF16),
                        pltpu.VMEM((2, SORT_RESIDUES, tm // SORT_RESIDUES, COL_BLOCK), F32),
                        pltpu.VMEM((COL_BLOCK // LANES, tm, LANES), F32),
                        pltpu.SemaphoreType.DMA((16,)), pltpu.SemaphoreType.DMA((16,)),
                        pltpu.SemaphoreType.DMA((N_DEV + 3,))])
    proj, hn_t, w_in_g, w_out_g, qkv_sorted = pl.pallas_call(
        body, name="gather_project", grid_spec=grid_spec,
        out_shape=(jax.ShapeDtypeStruct((N_DEV, SEQ, COL_BLOCK), F32), jax.ShapeDtypeStruct((D_MODEL, SEQ), BF16),
                   jax.ShapeDtypeStruct((D_MODEL, IN_COLS), BF16), jax.ShapeDtypeStruct((D_MODEL, D_MODEL), BF16),
                   jax.ShapeDtypeStruct((SORT_RESIDUES, SORT_ROWS, 3 * COL_BLOCK), F32)),
        compiler_params=_params(("arbitrary", "arbitrary")),
    )(order, x, mix_w, w_in, w_out, rc, rsa, rsb)
    return proj, hn_t, w_in_g, w_out_g, qkv_sorted.reshape(SEQ, 3 * COL_BLOCK)


SCORE_SCALE = HEAD_DIM ** -0.5
ATTN_GROUP_FWD = 16
ATTN_GROUP_BWD = 8
BLOCKS_PER_PATTERN = SEQ // ATTN_BLOCK
SORT_RESIDUES = 16
SORT_ROWS = SEQ // SORT_RESIDUES


def _write_band_bias(bias_ref):
    row = lax.broadcasted_iota(jnp.int32, (2 * ATTN_BLOCK, 2 * ATTN_BLOCK), 0) & (ATTN_BLOCK - 1)
    col = lax.broadcasted_iota(jnp.int32, (2 * ATTN_BLOCK, 2 * ATTN_BLOCK), 1)
    for pi, d in enumerate(DILATIONS):
        per = SORT_RESIDUES // d
        ahead = per * (row % (8 * d) - col % (16 * d)) + (row // (8 * d) - col // (16 * d))
        dist = ATTN_BLOCK + ahead
        bias_ref[2 * pi] = jnp.where((dist >= 0) & (dist <= ATTN_BLOCK), 0.0, NEG_BIG)
        bias_ref[2 * pi + 1] = jnp.where(ahead >= 0, 0.0, NEG_BIG)


def _head0_lanes():
    return lax.broadcasted_iota(jnp.int32, (ATTN_BLOCK, LANES), 1) < HEAD_DIM


def _stack_heads(t, h0):
    return jnp.concatenate([jnp.where(h0, t, 0.0), jnp.where(h0, 0.0, t)], axis=0).astype(BF16)


def _block_runs(i, d):
    nblk = BLOCKS_PER_PATTERN // d
    r, n = i // nblk, i % nblk
    kn = jnp.maximum(n - 1, 0)
    rows, keys = [], []
    for c in range(SORT_RESIDUES // d):
        base = SORT_ROWS * (c * d + r)
        rows.append(pl.ds(pl.multiple_of(base + 8 * d * n, 8), 8 * d))
        keys.append(pl.ds(pl.multiple_of(base + 8 * d * kn, 8), 16 * d))
    return rows, keys, (n == 0).astype(jnp.int32)


def _take(ref, runs):
    return jnp.concatenate([ref[run, :] for run in runs], axis=0)


def _put(ref, runs, value, add=False):
    at = 0
    for run in runs:
        piece = value[at:at + run.size]
        if add:
            ref[run, :] += piece
        else:
            ref[run, :] = piece
        at += run.size


def _sort_rows(src_ref, dst_ref):
    for r in range(SORT_RESIDUES):
        dst_ref[SORT_ROWS * r:SORT_ROWS * (r + 1), :] = src_ref[pl.ds(r, SORT_ROWS, stride=SORT_RESIDUES), :]


def _unsort_rows(src_ref, dst_ref):
    for r in range(SORT_RESIDUES):
        dst_ref[pl.ds(r, SORT_ROWS, stride=SORT_RESIDUES), :] = src_ref[SORT_ROWS * r:SORT_ROWS * (r + 1), :]


def _unsort_copies(src_ref, dst_hbm, lane_block, sem_ref):
    lanes = pl.ds(pl.multiple_of(LANES * lane_block, LANES), LANES)
    return [pltpu.make_async_copy(src_ref.at[pl.ds(SORT_ROWS * r, SORT_ROWS), :], dst_hbm.at[:, r, lanes],
                                  sem_ref.at[r]) for r in range(SORT_RESIDUES)]


def _for_each_group(d, n_group, load, compute, store):
    def group(g, carry):
        items = [load(*_block_runs(g * n_group + u, d)) for u in range(n_group)]
        results = [compute(item) for item in items]
        for item, res in zip(items, results):
            store(item, res)
        return carry

    lax.fori_loop(0, BLOCKS_PER_PATTERN // n_group, group, 0)


def _attn_fwd_fused(qkv_sorted):
    n_pat = len(DILATIONS)
    tile2 = (2 * ATTN_BLOCK, LANES)

    def body(q_ref, k_ref, v_ref, o_hbm, lse_ref, o_slots, m_acc, l_acc, bias_ref, out_sem):
        step, n_steps = pl.program_id(0), pl.num_programs(0)
        pl.when(step == 0)(lambda: _write_band_bias(bias_ref))
        slot = step % 2
        o_acc = o_slots.at[slot]
        h0 = _head0_lanes()
        for pi, d in enumerate(DILATIONS):
            first, last = pi == 0, pi == n_pat - 1

            def load(rows, keys, which, first=first, pi=pi):
                item = dict(rows=rows, keys=keys, which=2 * pi + which)
                if not first:
                    item.update(o=_take(o_acc, rows), m=[_take(m_acc.at[h], rows) for h in range(2)],
                                l=[_take(l_acc.at[h], rows) for h in range(2)])
                return item

            def compute(item, first=first):
                kb = _take(k_ref, item["keys"]).astype(BF16)
                vb = _take(v_ref, item["keys"]).astype(BF16)
                s = _mm_nt(_stack_heads(_take(q_ref, item["rows"]) * SCORE_SCALE, h0), kb) + bias_ref[item["which"]]
                mb = jnp.max(s, axis=-1, keepdims=True)
                if first:
                    p = jnp.exp(s - mb)
                    mn = jnp.broadcast_to(mb, tile2)
                else:
                    m_old = jnp.concatenate(item["m"], axis=0)
                    mn = jnp.maximum(m_old, mb)
                    alpha = jnp.exp(m_old - mn)
                    p = jnp.exp(s - jnp.concatenate([mn, mn], axis=1))
                ls = jnp.sum(p, axis=-1, keepdims=True)
                pv = _mm(p.astype(BF16), vb)
                if first:
                    return pv, mn, jnp.broadcast_to(ls, tile2)
                o_old = jnp.concatenate([item["o"], item["o"]], axis=0)
                return alpha * o_old + pv, mn, alpha * jnp.concatenate(item["l"], axis=0) + ls

            def store(item, res, last=last):
                rows = item["rows"]
                (o0, o1), (m0, m1), (l0, l1) = ((a[:ATTN_BLOCK], a[ATTN_BLOCK:]) for a in res)
                if last:
                    _put(o_acc, rows, jnp.where(h0, o0 / l0, o1 / l1))
                    _put(lse_ref, rows, jnp.where(h0, m0 + jnp.log(l0), m1 + jnp.log(l1)))
                else:
                    _put(o_acc, rows, jnp.where(h0, o0, o1))
                    for h, (m, l) in enumerate(((m0, l0), (m1, l1))):
                        _put(m_acc.at[h], rows, m)
                        _put(l_acc.at[h], rows, l)

            _for_each_group(d, ATTN_GROUP_FWD, load, compute, store)

        def copies_out(of_step):
            return _unsort_copies(o_slots.at[of_step % 2], o_hbm, of_step, out_sem.at[of_step % 2])

        @pl.when(step > 0)
        def _():
            for copy in copies_out(step - 1):
                copy.wait()

        for copy in copies_out(step):
            copy.start()

        @pl.when(step == n_steps - 1)
        def _():
            for copy in copies_out(step):
                copy.wait()

    slab = lambda g: pl.BlockSpec((SEQ, LANES), functools.partial(lambda hp, g: (0, 4 * g + hp), g=g))
    wide = jax.ShapeDtypeStruct((SEQ, ATTN_WIDTH), F32)
    o_rows, lse = pl.pallas_call(
        body, name="attn_fwd", grid=(4,),
        out_shape=(jax.ShapeDtypeStruct((SORT_ROWS, SORT_RESIDUES, ATTN_WIDTH), F32), wide),
        in_specs=[slab(0), slab(1), slab(2)], out_specs=(pl.BlockSpec(memory_space=pl.ANY), slab(0)),
        scratch_shapes=[pltpu.VMEM((2, SEQ, LANES), F32), pltpu.VMEM((2, SEQ, LANES), F32),
                        pltpu.VMEM((2, SEQ, LANES), F32),
                        pltpu.VMEM((2 * len(DILATIONS), 2 * ATTN_BLOCK, 2 * ATTN_BLOCK), F32),
                        pltpu.SemaphoreType.DMA((2, SORT_RESIDUES))],
        compiler_params=_params(("arbitrary",)),
    )(qkv_sorted, qkv_sorted, qkv_sorted)
    return o_rows.reshape(SEQ, ATTN_WIDTH), lse


def _attn_bwd_fused(qkv_sorted, d_out, lse_sorted, delta):
    def body(q_ref, k_ref, v_ref, do_ref, lse_ref, del_ref, dq_ref, dk_ref, dv_ref,
             do_s, del_s, dq_s, dk_s, dv_s, bias_ref):
        pl.when(pl.program_id(0) == 0)(lambda: _write_band_bias(bias_ref))
        _sort_rows(do_ref, do_s)
        _sort_rows(del_ref, del_s)
        dk_s[...] = jnp.zeros_like(dk_s)
        dv_s[...] = jnp.zeros_like(dv_s)
        h0 = _head0_lanes()
        for pi, d in enumerate(DILATIONS):
            first = pi == 0

            def load(rows, keys, which, pi=pi):
                return dict(rows=rows, keys=keys, q=_take(q_ref, rows), g=_take(do_s, rows),
                            lse=_take(lse_ref, rows), delta=_take(del_s, rows),
                            k=_take(k_ref, keys).astype(BF16), v=_take(v_ref, keys).astype(BF16),
                            bias=bias_ref[2 * pi + which])

            def per_head(t):
                swapped = pltpu.roll(t, HEAD_DIM, 1)
                both = jnp.concatenate([jnp.where(h0, t, swapped), jnp.where(h0, swapped, t)], axis=0)
                return jnp.concatenate([both, both], axis=1)

            def compute(item):
                q2, g2 = _stack_heads(item["q"] * SCORE_SCALE, h0), _stack_heads(item["g"], h0)
                s = _mm_nt(q2, item["k"]) + item["bias"]
                p = jnp.exp(s - per_head(item["lse"]))
                dp = _mm_nt(g2, item["v"])
                ds = (p * (dp - per_head(item["delta"]))).astype(BF16)
                dq2 = _mm(ds, item["k"])
                dq = jnp.where(h0, dq2[:ATTN_BLOCK], dq2[ATTN_BLOCK:]) * SCORE_SCALE
                return dq, _mm_tn(ds, q2), _mm_tn(p.astype(BF16), g2)

            def store(item, res, first=first):
                _put(dq_s, item["rows"], res[0], add=not first)
                _put(dk_s, item["keys"], res[1], add=True)
                _put(dv_s, item["keys"], res[2], add=True)

            _for_each_group(d, ATTN_GROUP_BWD, load, compute, store)
        _unsort_rows(dq_s, dq_ref)
        _unsort_rows(dk_s, dk_ref)
        _unsort_rows(dv_s, dv_ref)

    slab = lambda g: pl.BlockSpec((SEQ, LANES), functools.partial(lambda hp, g: (0, 4 * g + hp), g=g))
    wide = jax.ShapeDtypeStruct((SEQ, ATTN_WIDTH), F32)
    sorted_slab = pltpu.VMEM((SEQ, LANES), F32)
    return pl.pallas_call(
        body, name="attn_bwd", grid=(4,), out_shape=(wide, wide, wide),
        scratch_shapes=[sorted_slab] * 5 + [pltpu.VMEM((2 * len(DILATIONS), 2 * ATTN_BLOCK, 2 * ATTN_BLOCK), F32)],
        in_specs=[slab(0), slab(1), slab(2), slab(0), slab(0), slab(0)], out_specs=(slab(0), slab(0), slab(0)),
        compiler_params=_params(("arbitrary",)),
    )(qkv_sorted, qkv_sorted, qkv_sorted, d_out, lse_sorted, delta)


def _hgrn_lower_bound(lb_ref):
    r0, r1 = lb_ref[0:1, :], lb_ref[1:2, :]
    mx = jnp.maximum(r0, r1)
    e0, e1 = jnp.exp(r0 - mx), jnp.exp(r1 - mx)
    return e0 / (e0 + e1)


def _hgrn_gates(hq, hf, lb):
    sq = _sigmoid(hq)
    sg = _sigmoid(hf)
    f = lb + (1.0 - lb) * sg
    return hq * sq, sq, sg, f, 1.0 - f, jnp.log(f)


HGRN_PAIR = 4
HGRN_SEQ_BLOCK = 1024
HGRN_GROUP = 4
HGRN_ROWS = HGRN_GROUP * HGRN_CHUNK


def _hgrn_specs(reverse):
    n_blocks = SEQ // HGRN_SEQ_BLOCK
    width = HGRN_PAIR * HGRN_DIM
    blk = (lambda s: n_blocks - 1 - s) if reverse else (lambda s: s)
    cols = lambda g: pl.BlockSpec((None, HGRN_SEQ_BLOCK, width), functools.partial(lambda p, s, g: (g, blk(s), p), g=g))
    pair = pl.BlockSpec((HGRN_SEQ_BLOCK, width), lambda p, s: (blk(s), p))
    lb = pl.BlockSpec((2, width), lambda p, s: (0, p))
    states = pl.BlockSpec((HGRN_PAIR, HGRN_SEQ_BLOCK // HGRN_CHUNK, HGRN_DIM, HGRN_DIM),
                          lambda p, s: (p, blk(s), 0, 0))
    return cols, pair, lb, states


def _chunk_masks():
    ri = lax.broadcasted_iota(jnp.int32, (HGRN_ROWS, HGRN_ROWS), 0)
    ci = lax.broadcasted_iota(jnp.int32, (HGRN_ROWS, HGRN_ROWS), 1)
    same = (ri // HGRN_CHUNK) == (ci // HGRN_CHUNK)
    return same, same & (ri >= ci), same & (ri <= ci)


def _mm_select(sel, v):
    hi = v.astype(BF16)
    r1 = v - hi.astype(F32)
    mid = r1.astype(BF16)
    lo = (r1 - mid.astype(F32)).astype(BF16)
    return _mm(sel, hi) + _mm(sel, mid) + _mm(sel, lo)


def _head_cols(a, h):
    return a[:, HGRN_DIM * h:HGRN_DIM * (h + 1)]


def _hgrn_fwd(proj, lb_raw):
    t, rws = HGRN_CHUNK, HGRN_ROWS

    def body(hq_ref, hf_ref, hi_ref, lb_ref, rec_ref, st_ref, state):
        @pl.when(pl.program_id(1) == 0)
        def _():
            state[...] = jnp.zeros_like(state)

        lb = _hgrn_lower_bound(lb_ref)
        same, causal, _ = _chunk_masks()
        sel = jnp.concatenate([causal, same], axis=0).astype(BF16)

        def group(g, sts):
            rows = pl.ds(pl.multiple_of(g * rws, rws), rws)
            q, _, _, _, k, lf = _hgrn_gates(hq_ref[rows, :], hf_ref[rows, :], lb)
            sums = _mm_select(sel, lf)
            cum, last = sums[:rws], sums[rws:]
            qd = (q * jnp.exp(cum)).astype(BF16)
            ki = (k * jnp.exp(-cum)).astype(BF16)
            ke = (k * jnp.exp(last - cum)).astype(BF16)
            vb = hi_ref[rows, :].astype(BF16)
            dec = jnp.exp(last)
            new_sts, recs = [], []
            for h in range(HGRN_PAIR):
                qd_h, ke_h, vb_h = _head_cols(qd, h), _head_cols(ke, h), _head_cols(vb, h)
                att = jnp.where(causal, _mm_nt(qd_h, _head_cols(ki, h)), 0.0).astype(BF16)
                intra = _mm(att, vb_h)
                st = sts[h]
                outs = []
                for c in range(HGRN_GROUP):
                    sl = slice(c * t, (c + 1) * t)
                    st_ref[h, g * HGRN_GROUP + c] = st
                    outs.append(intra[sl] + _mm_nt(qd_h[sl], st.astype(BF16)))
                    st = st * _head_cols(dec[c * t:c * t + 1, :], h) + _mm_tn(vb_h[sl], ke_h[sl])
                new_sts.append(st)
                recs.append(jnp.concatenate(outs, axis=0))
            rec_ref[rows, :] = jnp.concatenate(recs, axis=1)
            return tuple(new_sts)

        sts = lax.fori_loop(0, HGRN_SEQ_BLOCK // rws, group, tuple(state[h] for h in range(HGRN_PAIR)))
        for h in range(HGRN_PAIR):
            state[h] = sts[h]

    cols, pair, lb, states = _hgrn_specs(reverse=False)
    return pl.pallas_call(
        body, name="hgrn_fwd", grid=(HGRN_HEADS // HGRN_PAIR, SEQ // HGRN_SEQ_BLOCK),
        out_shape=(jax.ShapeDtypeStruct((SEQ, HGRN_WIDTH), F32),
                   jax.ShapeDtypeStruct((HGRN_HEADS, N_CHUNKS, HGRN_DIM, HGRN_DIM), F32)),
        in_specs=[cols(4), cols(5), cols(6), lb], out_specs=(pair, states),
        scratch_shapes=[pltpu.VMEM((HGRN_PAIR, HGRN_DIM, HGRN_DIM), F32)],
        compiler_params=_params(("parallel", "arbitrary")),
    )(proj, proj, proj, lb_raw)


def _hgrn_bwd(proj, lb_raw, d_rec, states):
    t, rws = HGRN_CHUNK, HGRN_ROWS

    def body(hq_ref, hf_ref, hi_ref, lb_ref, do_ref, st_ref, dhq_ref, dhf_ref, dhi_ref, dlb_ref,
             dstate, dlb_acc):
        lb = _hgrn_lower_bound(lb_ref)
        same, causal, anti = _chunk_masks()
        sel = jnp.concatenate([causal, same], axis=0).astype(BF16)
        sel_t = jnp.concatenate([anti, same], axis=1).astype(BF16)
        @pl.when(pl.program_id(1) == 0)
        def _():
            dstate[...] = jnp.zeros_like(dstate)
            dlb_acc[...] = jnp.zeros_like(dlb_acc)

        n_groups = HGRN_SEQ_BLOCK // rws
        chunks = [slice(c * t, (c + 1) * t) for c in range(HGRN_GROUP)]

        def group(i, dsts_in):
            g = n_groups - 1 - i
            rows = pl.ds(pl.multiple_of(g * rws, rws), rws)
            hq = hq_ref[rows, :]
            q, sq, sg, f, k, lf = _hgrn_gates(hq, hf_ref[rows, :], lb)
            sums = _mm_select(sel, lf)
            cum, last = sums[:rws], sums[rws:]
            e_cum, e_inv, e_end, dec = jnp.exp(cum), jnp.exp(-cum), jnp.exp(last - cum), jnp.exp(last)
            qd, ki, ke = q * e_cum, k * e_inv, k * e_end
            qdb, kib, keb = qd.astype(BF16), ki.astype(BF16), ke.astype(BF16)
            vb = hi_ref[rows, :].astype(BF16)
            gb = do_ref[rows, :].astype(BF16)

            dsts_out, per_head = [], []
            for h in range(HGRN_PAIR):
                qdb_h, kib_h, keb_h = _head_cols(qdb, h), _head_cols(kib, h), _head_cols(keb, h)
                vb_h, gb_h = _head_cols(vb, h), _head_cols(gb, h)
                att = jnp.where(causal, _mm_nt(qdb_h, kib_h), 0.0).astype(BF16)
                datt = jnp.where(causal, _mm_nt(gb_h, vb_h), 0.0).astype(BF16)
                dv = _mm_tn(att, gb_h)
                dqd = _mm(datt, kib_h)
                dki = _mm_tn(datt, qdb_h)

                decs = [_head_cols(dec[c * t:c * t + 1, :], h) for c in range(HGRN_GROUP)]
                dsts = [None] * HGRN_GROUP
                dst = dsts_in[h]
                for c in reversed(range(HGRN_GROUP)):
                    dsts[c] = dst
                    dst = dst * decs[c] + _mm_tn(gb_h[chunks[c]], qdb_h[chunks[c]])
                dsts_out.append(dst)

                dv_x, dqd_x, dke, dlast_x = [], [], [], []
                for c, sl in enumerate(chunks):
                    st_prev = st_ref[h, g * HGRN_GROUP + c]
                    dstb = dsts[c].astype(BF16)
                    dv_x.append(_mm_nt(keb_h[sl], dstb))
                    dqd_x.append(_mm(gb_h[sl], st_prev.astype(BF16)))
                    dke.append(_mm(vb_h[sl], dstb))
                    ddec = jnp.sum(dsts[c] * st_prev, axis=0, keepdims=True)
                    dlast_x.append(jnp.broadcast_to(ddec * decs[c], (t, HGRN_DIM)))
                per_head.append((dv + jnp.concatenate(dv_x, axis=0), dqd + jnp.concatenate(dqd_x, axis=0),
                                 dki, jnp.concatenate(dke, axis=0), jnp.concatenate(dlast_x, axis=0)))
            dv, dqd, dki, dke, dlast = (jnp.concatenate(list(parts), axis=1) for parts in zip(*per_head))

            dq = dqd * e_cum
            dk = dki * e_inv + dke * e_end
            dke_ke = dke * ke
            dcum = dqd * qd - dki * ki - dke_ke
            dlf = _mm_select(sel_t, jnp.concatenate([dcum, dke_ke], axis=0)) + dlast
            df = dlf / f - dk
            dhq_ref[rows, :] = dq * (sq * (1.0 + hq * (1.0 - sq)))
            dhf_ref[rows, :] = df * (1.0 - lb) * (sg * (1.0 - sg))
            dhi_ref[rows, :] = dv
            dlb_acc[...] += jnp.sum(df * (1.0 - sg), axis=0, keepdims=True)
            return tuple(dsts_out)

        dsts = lax.fori_loop(0, n_groups, group, tuple(dstate[h] for h in range(HGRN_PAIR)))
        for h in range(HGRN_PAIR):
            dstate[h] = dsts[h]
        g0 = dlb_acc[...] * lb * (1.0 - lb)
        dlb_ref[...] = jnp.concatenate([g0, -g0], axis=0)

    cols, pair, lb_spec, st_spec = _hgrn_specs(reverse=True)
    wide = jax.ShapeDtypeStruct((SEQ, HGRN_WIDTH), F32)
    return pl.pallas_call(
        body, name="hgrn_bwd", grid=(HGRN_HEADS // HGRN_PAIR, SEQ // HGRN_SEQ_BLOCK),
        out_shape=(wide, wide, wide, jax.ShapeDtypeStruct((2, HGRN_WIDTH), F32)),
        in_specs=[cols(4), cols(5), cols(6), lb_spec, pair, st_spec],
        out_specs=(pair, pair, pair, lb_spec),
        scratch_shapes=[pltpu.VMEM((HGRN_PAIR, HGRN_DIM, HGRN_DIM), F32),
                        pltpu.VMEM((1, HGRN_PAIR * HGRN_DIM), F32)],
        compiler_params=_params(("parallel", "arbitrary")),
    )(proj, proj, proj, lb_raw, d_rec, states)


def _group_sum(v, group):
    parts = []
    for s in range(v.shape[1] // LANES):
        slab = v[:, LANES * s:LANES * (s + 1)]
        if group == LANES:
            parts.append(jnp.broadcast_to(jnp.sum(slab, axis=-1, keepdims=True), slab.shape))
        else:
            h0 = lax.broadcasted_iota(jnp.int32, slab.shape, 1) < HEAD_DIM
            s0 = jnp.sum(jnp.where(h0, slab, 0.0), axis=-1, keepdims=True)
            s1 = jnp.sum(jnp.where(h0, 0.0, slab), axis=-1, keepdims=True)
            parts.append(jnp.where(h0, s0, s1))
    return jnp.concatenate(parts, axis=1)


def _mid(attn_o, rec, proj, x, target, w_out_g, attn_w, hgrn_w, final_w):
    tm = 256

    def branch_fwd(o, gate, w, group):
        r = lax.rsqrt(_group_sum(o * o, group) * (1.0 / group) + NORM_EPS)
        nrm = o * r
        sg = _sigmoid(gate)
        return r, nrm, sg, nrm * w * (gate * sg)

    def branch_bwd(dy, r, nrm, sg, gate, w, group):
        silu = gate * sg
        d_gate = dy * nrm * w * (sg * (1.0 + gate * (1.0 - sg)))
        d_w = jnp.sum(dy * nrm * silu, axis=0, keepdims=True)
        dn = dy * w * silu
        d_o = r * (dn - nrm * (_group_sum(dn * nrm, group) * (1.0 / group)))
        return d_o, d_gate, d_w

    def body(o_ref, rec_ref, ag_ref, hg_ref, x_ref, tgt_ref, wout_ref, aw_ref, hw_ref, fw_ref,
             dx2_ref, do_ref, delta_ref, dag_ref, drec_ref, dhg_ref, dwout_ref, dfw_ref, daw_ref, dhw_ref,
             loss_ref, dwout_acc):
        i = pl.program_id(0)

        @pl.when(i == 0)
        def _():
            dwout_acc[...] = jnp.zeros_like(dwout_acc)
            dfw_ref[...] = jnp.zeros_like(dfw_ref)
            daw_ref[...] = jnp.zeros_like(daw_ref)
            dhw_ref[...] = jnp.zeros_like(dhw_ref)
            loss_ref[...] = jnp.zeros_like(loss_ref)

        o, rc, ag, hg = o_ref[...], rec_ref[...], ag_ref[...], hg_ref[...]
        aw, hw, fw = aw_ref[...], hw_ref[...], fw_ref[...]
        ra, na, sga, ya = branch_fwd(o, ag, aw, HEAD_DIM)
        rh, nh, sgh, yh = branch_fwd(rc, hg, hw, HGRN_DIM)
        mixed = jnp.concatenate([ya, yh], axis=1).astype(BF16)
        wout = wout_ref[...]
        x2 = x_ref[...] + _mm(mixed, wout)
        rstd = lax.rsqrt(jnp.mean(x2 * x2, axis=-1, keepdims=True) + NORM_EPS)
        xn = x2 * rstd
        err = xn * fw - tgt_ref[...]
        row_loss = jnp.mean(err * err, axis=-1, keepdims=True)
        loss_ref[...] += 0.5 * jnp.sum(row_loss, axis=0, keepdims=True)
        dy = err * (1.0 / D_MODEL)
        dfw_ref[...] += jnp.sum(dy * xn, axis=0, keepdims=True)
        dxn = dy * fw
        dx2 = rstd * (dxn - xn * jnp.mean(dxn * xn, axis=-1, keepdims=True))
        dx2_ref[...] = dx2
        dx2b = dx2.astype(BF16)
        dwout_acc[...] += _mm_tn(mixed, dx2b)

        @pl.when(i == pl.num_programs(0) - 1)
        def _():
            dwout_ref[...] = dwout_acc[...].astype(BF16)

        dmixed = _mm_nt(dx2b, wout)

        d_o, d_ag, d_aw = branch_bwd(dmixed[:, :ATTN_WIDTH], ra, na, sga, ag, aw, HEAD_DIM)
        d_rec, d_hg, d_hw = branch_bwd(dmixed[:, ATTN_WIDTH:], rh, nh, sgh, hg, hw, HGRN_DIM)
        do_ref[...] = d_o
        delta_ref[...] = _group_sum(d_o * o, HEAD_DIM)
        dag_ref[...] = d_ag
        drec_ref[...] = d_rec
        dhg_ref[...] = d_hg
        daw_ref[...] += d_aw
        dhw_ref[...] += d_hw

    half = lambda: pl.BlockSpec((tm, COL_BLOCK), lambda i: (i, 0))
    full = lambda: pl.BlockSpec((tm, D_MODEL), lambda i: (i, 0))
    fixed = lambda r, c: pl.BlockSpec((r, c), lambda i: (0, 0))
    wide = jax.ShapeDtypeStruct((SEQ, COL_BLOCK), F32)
    return pl.pallas_call(
        body, name="mid", grid=(SEQ // tm,),
        out_shape=(jax.ShapeDtypeStruct((SEQ, D_MODEL), F32), wide, wide, wide, wide, wide,
                   jax.ShapeDtypeStruct((D_MODEL, D_MODEL), BF16),
                   jax.ShapeDtypeStruct((1, D_MODEL), F32), jax.ShapeDtypeStruct((1, COL_BLOCK), F32),
                   jax.ShapeDtypeStruct((1, COL_BLOCK), F32), jax.ShapeDtypeStruct((1, 1), F32)),
        scratch_shapes=[pltpu.VMEM((D_MODEL, D_MODEL), F32)],
        in_specs=[half(), half(),
                  pl.BlockSpec((None, tm, COL_BLOCK), lambda i: (3, i, 0)),
                  pl.BlockSpec((None, tm, COL_BLOCK), lambda i: (7, i, 0)),
                  full(), full(), fixed(D_MODEL, D_MODEL), fixed(1, COL_BLOCK), fixed(1, COL_BLOCK),
                  fixed(1, D_MODEL)],
        out_specs=(full(), half(), half(), half(), half(), half(), fixed(D_MODEL, D_MODEL),
                   fixed(1, D_MODEL), fixed(1, COL_BLOCK), fixed(1, COL_BLOCK), fixed(1, 1)),
        compiler_params=_params(("arbitrary",)),
    )(attn_o, rec, proj, proj, x, target, w_out_g, attn_w, hgrn_w, final_w)


def _in_proj_bwd_rows(d_groups, w_g, x, dx2, mix_w, rc, rsa, rsb):
    tm = 256

    def body(*refs):
        dg_refs = refs[:N_DEV]
        wg_ref, x_ref, dx2_ref, w_ref, c_ref, sa_ref, sb_ref, gx_ref, dpb_ref, dmw_ref = refs[N_DEV:]

        @pl.when(pl.program_id(0) == 0)
        def _():
            dmw_ref[...] = jnp.zeros_like(dmw_ref)

        parts = []
        for j in range(N_DEV):
            dp = dg_refs[j][...]
            if j < 2:
                dp = _rot_transposed(dp, c_ref[...], sa_ref[...], sb_ref[...])
            parts.append(dp.astype(BF16))
        dpb = jnp.concatenate(parts, axis=1)
        for j in range(N_DEV):
            dpb_ref[j] = parts[j]
        g = _mm_nt(dpb, wg_ref[...])
        xf = x_ref[...]
        rstd = lax.rsqrt(jnp.mean(xf * xf, axis=-1, keepdims=True) + NORM_EPS)
        xn = xf * rstd
        dmw_ref[...] += jnp.sum(g * xn, axis=0, keepdims=True)
        gw = g * w_ref[...]
        gx_ref[...] = dx2_ref[...] + rstd * (gw - xn * jnp.mean(gw * xn, axis=-1, keepdims=True))

    tile = lambda cols: pl.BlockSpec((tm, cols), lambda i: (i, 0))
    fixed = lambda r, c: pl.BlockSpec((r, c), lambda i: (0, 0))
    return pl.pallas_call(
        body, name="in_proj_bwd_rows", grid=(SEQ // tm,),
        out_shape=(jax.ShapeDtypeStruct((SEQ, D_MODEL), F32), jax.ShapeDtypeStruct((N_DEV, SEQ, COL_BLOCK), BF16),
                   jax.ShapeDtypeStruct((1, D_MODEL), F32)),
        in_specs=[tile(COL_BLOCK) for _ in range(N_DEV)] + [
            pl.BlockSpec((D_MODEL, IN_COLS), lambda i: (0, 0), pipeline_mode=pl.Buffered(1)),
            tile(D_MODEL), tile(D_MODEL), fixed(1, D_MODEL), tile(LANES), tile(LANES), tile(LANES)],
        out_specs=(tile(D_MODEL), pl.BlockSpec((N_DEV, tm, COL_BLOCK), lambda i: (0, i, 0)), fixed(1, D_MODEL)),
        compiler_params=_params(("arbitrary",)),
    )(*d_groups, w_g, x, dx2, mix_w, rc, rsa, rsb)


def _weights_exchange(hn_t, dproj_b, dwout_p, small_p):
    n_chips = N_DEV // 2
    rb = 128
    S1_IN, S1_OUT, SMALL, S2_IN, S2_OUT = 0, 4, 8, 15, 18
    rel_of_pair = (1, 2, 3, 0)

    def body(order_ref, hnt_ref, dp_ref, dwout_ref, small_ref, gin_ref, gout_ref, gs_ref,
             part, s1_send, s1_in, s1_out, fwd_in, fwd_out, s2_in, s2_out, land_s, send_sems, recv_sems):
        t = pl.program_id(0)
        me = _my_place()
        x, y, c = me
        my_chip = 2 * x + y
        sibling = (x, y, 1 - c)

        def remote(slot, src, dst, to):
            return pltpu.make_async_remote_copy(src_ref=src, dst_ref=dst, send_sem=send_sems.at[slot],
                                                recv_sem=recv_sems.at[slot], device_id=to, device_id_type=MESH)

        def s1_in_copy(pair):
            return remote(S1_IN + pair, s1_send.at[pair], s1_in.at[pair], sibling)

        def s1_out_copy(pair):
            q = my_chip ^ rel_of_pair[pair]
            return remote(S1_OUT + pair, dwout_ref.at[q, 1 - c], s1_out.at[pair], sibling)

        def s2_copies(rel):
            peer = _peer(me, 2 * rel)
            return [remote(S2_IN + rel - 1, fwd_in.at[rel - 1], s2_in.at[rel - 1], peer),
                    remote(S2_OUT + rel - 1, fwd_out.at[rel - 1], s2_out.at[rel - 1], peer)]

        def small_copy(rel):
            return remote(SMALL + rel - 1, small_ref, land_s.at[rel], _peer(me, rel))

        @pl.when(t == 0)
        def _():
            land_s[0] = small_ref[...]
            for pair in range(n_chips):
                s1_out_copy(pair).start()
            for rel in range(1, N_DEV):
                small_copy(rel).start()

        part[...] = _mm(hnt_ref[...], dp_ref[...])

        def rows_loop(n_rows, fn):
            def step(b, carry):
                fn(pl.ds(pl.multiple_of(b * rb, rb), rb))
                return carry
            lax.fori_loop(0, n_rows // rb, step, 0)

        for pair, rel in enumerate(rel_of_pair):
            @pl.when(t == 2 * pair)
            def _(pair=pair):
                s1_send[pair] = part[...].astype(BF16)
                s1_in_copy(pair).start()

            @pl.when(t == 2 * pair + 1)
            def _(pair=pair, rel=rel):
                q = my_chip ^ rel
                s1_in_copy(pair).wait_recv()
                s1_out_copy(pair).wait_recv()
                dst_in = fwd_in.at[rel - 1] if rel else gin_ref
                dst_out = fwd_out.at[rel - 1] if rel else gout_ref

                def add_in(rows):
                    dst_in[rows, :] = (part[rows, :] + s1_in[pair, rows, :].astype(F32)).astype(dst_in.dtype)

                def add_out(rows):
                    dst_out[rows, :] = (dwout_ref[q, c, rows, :].astype(F32)
                                        + s1_out[pair, rows, :].astype(F32)).astype(dst_out.dtype)

                rows_loop(D_MODEL, add_in)
                rows_loop(WOUT_ROWS, add_out)
                if rel:
                    for cp in s2_copies(rel):
                        cp.start()

        @pl.when(t == N_DEV - 1)
        def _():
            for rel in range(1, n_chips):
                for cp in s2_copies(rel):
                    cp.wait_recv()

            def total_in(rows):
                g = gin_ref[rows, :]
                for rel in range(1, n_chips):
                    g = g + s2_in[rel - 1, rows, :].astype(F32)
                gin_ref[rows, :] = g

            def total_out(rows):
                g = gout_ref[rows, :]
                for rel in range(1, n_chips):
                    g = g + s2_out[rel - 1, rows, :].astype(F32)
                gout_ref[rows, :] = g

            rows_loop(D_MODEL, total_in)
            rows_loop(WOUT_ROWS, total_out)

            for rel in range(1, N_DEV):
                small_copy(rel).wait_recv()
            my_flat = _flat(me)
            g = land_s[my_flat ^ 0]
            for dev in range(1, N_DEV):
                g = g + land_s[my_flat ^ dev]
            gs_ref[...] = g

            for pair in range(n_chips):
                s1_in_copy(pair).wait_send()
                s1_out_copy(pair).wait_send()
            for rel in range(1, n_chips):
                for cp in s2_copies(rel):
                    cp.wait_send()
            for rel in range(1, N_DEV):
                small_copy(rel).wait_send()

    place_x, place_y, place_c = _my_place()
    my_chip = 2 * place_x + place_y
    order = jnp.stack([2 * (my_chip ^ rel) + core for rel in rel_of_pair
                       for core in (1 - place_c, place_c)]).astype(jnp.int32)

    whole = lambda: pl.BlockSpec(memory_space=pltpu.VMEM)
    in_blocks = lambda n: pltpu.VMEM((n, D_MODEL, COL_BLOCK), BF16)
    out_blocks = lambda n: pltpu.VMEM((n, WOUT_ROWS, D_MODEL), BF16)
    grid_spec = pltpu.PrefetchScalarGridSpec(
        num_scalar_prefetch=1, grid=(N_DEV,),
        in_specs=[pl.BlockSpec((D_MODEL, SEQ), lambda t, order: (0, 0), pipeline_mode=pl.Buffered(1)),
                  pl.BlockSpec((None, SEQ, COL_BLOCK), lambda t, order: (order[t], 0, 0)), whole(), whole()],
        out_specs=(whole(), whole(), whole()),
        scratch_shapes=[pltpu.VMEM((D_MODEL, COL_BLOCK), F32), in_blocks(n_chips), in_blocks(n_chips),
                        out_blocks(n_chips), in_blocks(n_chips - 1), out_blocks(n_chips - 1),
                        in_blocks(n_chips - 1), out_blocks(n_chips - 1),
                        pltpu.VMEM((N_DEV, SMALL_ROWS, LANES), F32),
                        pltpu.SemaphoreType.DMA((21,)), pltpu.SemaphoreType.DMA((21,))])
    return pl.pallas_call(
        body, name="weights_exchange", grid_spec=grid_spec,
        out_shape=(jax.ShapeDtypeStruct((D_MODEL, COL_BLOCK), F32), jax.ShapeDtypeStruct((WOUT_ROWS, D_MODEL), F32),
                   jax.ShapeDtypeStruct((SMALL_ROWS, LANES), F32)),
        compiler_params=_params(("arbitrary",)),
    )(order, hn_t, dproj_b, dwout_p.reshape(n_chips, 2, WOUT_ROWS, D_MODEL), small_p)


def _adamw(w, g, m, v):
    m = ADAM_B1 * m + (1.0 - ADAM_B1) * g
    v = ADAM_B2 * v + (1.0 - ADAM_B2) * (g * g)
    m_hat = m / (1.0 - ADAM_B1 ** ADAM_STEP)
    v_hat = v / (1.0 - ADAM_B2 ** ADAM_STEP)
    delta = -ADAM_LR * (m_hat / (jnp.sqrt(v_hat) + ADAM_EPS) + ADAM_WD * w)
    return delta, m, v


def _adamw_update(grads, weights, m_old, v_old):
    rb = 256

    def body(*refs):
        g_refs, w_refs, m_refs, v_refs = refs[0:3], refs[3:6], refs[6:9], refs[9:12]
        d_refs, nm_refs, nv_refs = refs[12:15], refs[15:18], refs[18:21]
        for k in range(3):
            n_rows = g_refs[k].shape[0]
            step_rows = min(rb, n_rows)

            def step(b, carry, k=k, step_rows=step_rows):
                rows = pl.ds(pl.multiple_of(b * step_rows, 8), step_rows)
                delta, nm, nv = _adamw(w_refs[k][rows, :], g_refs[k][rows, :], m_refs[k][rows, :], v_refs[k][rows, :])
                d_refs[k][rows, :] = delta
                nm_refs[k][rows, :] = nm
                nv_refs[k][rows, :] = nv
                return carry

            lax.fori_loop(0, n_rows // step_rows, step, 0)

    shapes = tuple(jax.ShapeDtypeStruct(g.shape, F32) for g in grads)
    vm = lambda: pl.BlockSpec(memory_space=pltpu.VMEM)
    outs = pl.pallas_call(
        body, name="adamw_update", out_shape=shapes * 3,
        in_specs=[vm() for _ in range(12)], out_specs=tuple(vm() for _ in range(9)),
        compiler_params=_params(),
    )(*grads, *weights, *m_old, *v_old)
    return outs[0:3], outs[3:6], outs[6:9]


def _pack_small(mix, attn, hgrn, lb, final, loss=None):
    def rows8(a):
        a = a.reshape(-1, LANES)
        return jnp.pad(a, ((0, 8 - a.shape[0]), (0, 0)))
    last = jnp.zeros((8, LANES), F32) if loss is None else jnp.pad(loss.reshape(1, 1), ((0, 7), (0, LANES - 1)))
    return jnp.concatenate([rows8(mix), rows8(attn), rows8(hgrn), rows8(lb), rows8(final), last], axis=0)


def _unpack_small(slab):
    return (slab[ROW_MIX:ROW_MIX + 8].reshape(1, D_MODEL), slab[ROW_ATTN:ROW_ATTN + 4].reshape(1, ATTN_WIDTH),
            slab[ROW_HGRN:ROW_HGRN + 4].reshape(1, HGRN_WIDTH), slab[ROW_LB:ROW_LB + 8].reshape(2, HGRN_WIDTH),
            slab[ROW_FINAL:ROW_FINAL + 8].reshape(D_MODEL))


def _rope(pos_row):
    j = np.arange(ROPE_ROWS)
    inv = np.where(j < ROPE_HALF, ROPE_THETA ** (-(j % ROPE_HALF) * (2.0 / ROPE_DIMS)), 0.0)
    e = np.arange(LANES) % HEAD_DIM
    hit = (j[:, None] == (e % ROPE_HALF)[None, :]) & (j[:, None] < ROPE_HALF)
    sel = np.stack([hit & (e < ROPE_DIMS), hit & (e >= ROPE_HALF) & (e < ROPE_DIMS),
                    -1.0 * (hit & (e < ROPE_HALF))]).astype(np.float32)
    return _rope_tables(pos_row, jnp.asarray(inv.astype(np.float32).reshape(ROPE_ROWS, 1)),
                        jnp.asarray(sel, dtype=BF16))


def _local_step(x, proj, qkv_sorted, w_in_g, w_out_g, tables, mix_w, attn_w, hgrn_w, lb_raw, final_w, target):
    rc, rsa, rsb = tables
    attn_o, lse = _attn_fwd_fused(qkv_sorted)
    rec, states = _hgrn_fwd(proj, lb_raw)

    (dx2, d_o, delta, d_ag, d_rec, d_hg, dwout_p, d_final, d_attn_w, d_hgrn_w, loss) = _mid(
        attn_o, rec, proj, x, target, w_out_g, attn_w, hgrn_w, final_w.reshape(1, D_MODEL))

    dqkv = _attn_bwd_fused(qkv_sorted, d_o, lse, delta)
    d_hq, d_hf, d_hi, d_lb = _hgrn_bwd(proj, lb_raw, d_rec, states)

    grad_x, dproj_b, d_mix = _in_proj_bwd_rows(
        (dqkv[0], dqkv[1], dqkv[2], d_ag, d_hq, d_hf, d_hi, d_hg), w_in_g, x, dx2, mix_w, rc, rsa, rsb)
    small_p = _pack_small(d_mix, d_attn_w, d_hgrn_w, d_lb, d_final, loss)
    return grad_x, dproj_b, dwout_p, small_p


def kernel(x, positions, w_in, w_out, mix_norm_w, attn_out_norm_w, hgrn_out_norm_w, hgrn_lb_raw, final_norm_w, loss_target, m_w_in, m_w_out, m_mix_norm_w, m_attn_out_norm_w, m_hgrn_out_norm_w, m_hgrn_lb_raw, m_final_norm_w, v_w_in, v_w_out, v_mix_norm_w, v_attn_out_norm_w, v_hgrn_out_norm_w, v_hgrn_lb_raw, v_final_norm_w):
    tables = _rope(positions)
    proj, hn_t, w_in_g, w_out_g, qkv_sorted = _gather_project(x[0], mix_norm_w, w_in[0], w_out[0], *tables)
    grad_x, dproj_b, dwout_p, small_p = _local_step(
        x[0], proj, qkv_sorted, w_in_g, w_out_g, tables, mix_norm_w, attn_out_norm_w, hgrn_out_norm_w,
        hgrn_lb_raw, final_norm_w, loss_target[0])
    g_in, g_out, g_s = _weights_exchange(hn_t, dproj_b, dwout_p, small_p)

    w_s = _pack_small(mix_norm_w, attn_out_norm_w, hgrn_out_norm_w, hgrn_lb_raw, final_norm_w)
    m_s = _pack_small(m_mix_norm_w, m_attn_out_norm_w, m_hgrn_out_norm_w, m_hgrn_lb_raw, m_final_norm_w)
    v_s = _pack_small(v_mix_norm_w, v_attn_out_norm_w, v_hgrn_out_norm_w, v_hgrn_lb_raw, v_final_norm_w)
    (d_in, d_out, d_s), (nm_in, nm_out, nm_s), (nv_in, nv_out, nv_s) = _adamw_update(
        (g_in, g_out, g_s), (w_in[0], w_out[0], w_s), (m_w_in[0], m_w_out[0], m_s), (v_w_in[0], v_w_out[0], v_s))

    loss = g_s[ROW_LOSS, 0]
    return (loss, grad_x[None], g_in[None], g_out[None], *_unpack_small(g_s),
            d_in[None], d_out[None], *_unpack_small(d_s),
            nm_in[None], nm_out[None], *_unpack_small(nm_s),
            nv_in[None], nv_out[None], *_unpack_small(nv_s))
```

```python
import functools

import jax
import jax.numpy as jnp
import numpy as np
from jax import lax
from jax.experimental import pallas as pl
from jax.experimental.pallas import tpu as pltpu

F32 = jnp.float32
BF16 = jnp.bfloat16

SEQ = 4096
D_MODEL = 1024
ATTN_WIDTH = 512
HGRN_WIDTH = 512
HEAD_DIM = 64
HGRN_HEADS = 4
HGRN_DIM = 128
HGRN_CHUNK = 64
N_CHUNKS = SEQ // HGRN_CHUNK
IN_COLS = 4096
COL_BLOCK = 512
N_DEV = 8
WOUT_ROWS = D_MODEL // N_DEV
ATTN_BLOCK = 128
DILATIONS = (1, 4, 16)
ROPE_THETA = 500000.0
ROPE_DIMS = 16
ROPE_HALF = 8
NORM_EPS = 1e-6
NEG_BIG = -1e30
LANES = 128

ADAM_LR = 0.001
ADAM_B1 = 0.9
ADAM_B2 = 0.999
ADAM_EPS = 1e-08
ADAM_WD = 0.01
ADAM_STEP = 10

SMALL_ROWS = 48
ROW_MIX, ROW_ATTN, ROW_HGRN, ROW_LB, ROW_FINAL, ROW_LOSS = 0, 8, 16, 24, 32, 40

VMEM_LIMIT = 56 * 1024 * 1024
MESH = pl.DeviceIdType.MESH


def _mm(a, b):
    return lax.dot_general(a, b, (((1,), (0,)), ((), ())), preferred_element_type=F32)


def _mm_nt(a, b):
    return lax.dot_general(a, b, (((1,), (1,)), ((), ())), preferred_element_type=F32)


def _mm_tn(a, b):
    return lax.dot_general(a, b, (((0,), (0,)), ((), ())), preferred_element_type=F32)


def _mm_exact(a, b):
    return lax.dot_general(a, b, (((1,), (0,)), ((), ())), preferred_element_type=F32,
                           precision=lax.Precision.HIGHEST)


def _sigmoid(v):
    return 1.0 / (1.0 + jnp.exp(-v))


def _params(sem=None, **kw):
    return pltpu.CompilerParams(dimension_semantics=sem, vmem_limit_bytes=VMEM_LIMIT, **kw)


def _my_place():
    return lax.axis_index("x"), lax.axis_index("y"), lax.axis_index("c")


def _peer(place, rel):
    x, y, c = place
    return (x ^ ((rel >> 2) & 1), y ^ ((rel >> 1) & 1), c ^ (rel & 1))


def _flat(place):
    x, y, c = place
    return 4 * x + 2 * y + c


ROPE_ROWS = 16


def _rope_tables(pos_row, inv_freq_col, selectors):
    def body(pos_ref, invf_ref, sel_ref, c_ref, sa_ref, sb_ref):
        ang = pos_ref[...].astype(F32) * invf_ref[...]
        cos, sin = jnp.cos(ang), jnp.sin(ang)

        def spread(v, sel):
            hi = v.astype(BF16)
            r1 = v - hi.astype(F32)
            mid = r1.astype(BF16)
            lo = (r1 - mid.astype(F32)).astype(BF16)
            return _mm_tn(hi, sel) + _mm_tn(mid, sel) + _mm_tn(lo, sel)

        e = lax.broadcasted_iota(jnp.int32, (1, LANES), 1) & (HEAD_DIM - 1)
        c_ref[...] = spread(cos, sel_ref[0]) + jnp.where(e < ROPE_DIMS, 0.0, 1.0)
        sa_ref[...] = spread(sin, sel_ref[1])
        sb_ref[...] = spread(sin, sel_ref[2])

    tab = jax.ShapeDtypeStruct((SEQ, LANES), F32)
    vm = lambda: pl.BlockSpec(memory_space=pltpu.VMEM)
    return pl.pallas_call(
        body, name="rope_tables", out_shape=(tab, tab, tab),
        in_specs=[vm(), vm(), vm()], out_specs=(vm(), vm(), vm()), compiler_params=_params(),
    )(pos_row, inv_freq_col, selectors)


def _per_slab(fn, t):
    return jnp.concatenate([fn(t[:, LANES * s:LANES * (s + 1)]) for s in range(t.shape[1] // LANES)], axis=1)


def _rot(t, c, sa, sb):
    return _per_slab(lambda u: u * c + pltpu.roll(u, ROPE_HALF, 1) * sa + pltpu.roll(u, LANES - ROPE_HALF, 1) * sb, t)


def _rot_transposed(g, c, sa, sb):
    return _per_slab(
        lambda u: u * c + pltpu.roll(u * sa, LANES - ROPE_HALF, 1) + pltpu.roll(u * sb, ROPE_HALF, 1), g)


def _gather_project(x, mix_w, w_in, w_out, rc, rsa, rsb):
    tm = 1024
    n_tiles = SEQ // tm
    arrival_of_step = (None, 0, 1, 2, 4, 5, 3, 6)

    def body(order_ref, x_ref, w_ref, win_ref, wout_ref, c_ref, sa_ref, sb_ref,
             proj_ref, hnt_ref, gin_hbm, gout_hbm, qkv_hbm,
             hn_s, w_land, wout_land, stage, sort_stage, slab_tmp, send_sems, recv_sems, local_sems):
        g, i = pl.program_id(0), pl.program_id(1)
        me = _my_place()
        x_, y_, c_ = me
        sibling = (x_, y_, 1 - c_)
        chips = [(1 - x_, y_), (x_, 1 - y_), (1 - x_, 1 - y_)]

        def slab(which, place):
            idx = _flat(place)
            if which == 0:
                return w_land.at[idx]
            return wout_land.at[pl.ds(pl.multiple_of(idx * WOUT_ROWS, WOUT_ROWS), WOUT_ROWS), :]

        def remote(which, k, ref, to, src=None):
            return pltpu.make_async_remote_copy(
                src_ref=ref if src is None else src, dst_ref=ref, send_sem=send_sems.at[8 * which + k],
                recv_sem=recv_sems.at[8 * which + k], device_id=to, device_id_type=MESH)

        def copy(which, k, block, to, src=None):
            return remote(which, k, slab(which, block), to, src)

        def half(which, place, part):
            n = (D_MODEL if which == 0 else WOUT_ROWS) // 2
            if which == 0:
                return w_land.at[_flat(place), pl.ds(n * part, n), :]
            return wout_land.at[pl.ds(pl.multiple_of(_flat(place) * WOUT_ROWS + n * part, n), n), :]

        def first_copies(which):
            src = stage if which == 0 else None
            return ([copy(which, 0, me, sibling, src)]
                    + [copy(which, 1 + j, me, (*chips[j], c_), src) for j in range(2)])

        def relay(which, part):
            frm, to = (chips[1], chips[0]) if part == 0 else (chips[0], chips[1])
            return remote(which, 3 if part == 0 else 7, half(which, (*frm, c_), part), (*to, c_))

        def two_hop_half(which, part):
            return remote(which, 3 if part == 0 else 7, half(which, (*chips[2], c_), part), me)

        def pass_on(which, j):
            return copy(which, 4 + j, (*chips[j], c_), sibling)

        def arrival(which, k):
            if k == 0:
                return copy(which, 0, sibling, me)
            if k <= 2:
                return copy(which, k, (*chips[k - 1], c_), me)
            return copy(which, k, (*chips[k - 4], 1 - c_), me)

        def to_hbm(step):
            idx = order_ref[step]
            cols = pl.ds(pl.multiple_of(idx * COL_BLOCK, COL_BLOCK), COL_BLOCK)
            return pltpu.make_async_copy(w_land.at[idx], gin_hbm.at[:, cols], local_sems.at[step])

        @pl.when((g == 0) & (i == 0))
        def _():
            stage[...] = win_ref[...].astype(BF16)
            w_land[_flat(me)] = stage[...]
            wout_land[pl.ds(pl.multiple_of(_flat(me) * WOUT_ROWS, WOUT_ROWS), WOUT_ROWS), :] = (
                wout_ref[...].astype(BF16))
            for cp in first_copies(0) + first_copies(1)[:1]:
                cp.start()
            to_hbm(0).start()

        for step, k in enumerate(arrival_of_step):
            if k is None:
                continue

            @pl.when((g == step) & (i == 0))
            def _(k=k, step=step):
                if k == 3:
                    two_hop_half(0, 0).wait_recv()
                    two_hop_half(0, 1).wait_recv()
                else:
                    arrival(0, k).wait_recv()
                to_hbm(step).start()
                if 1 <= k <= 3:
                    pass_on(0, k - 1).start()
                if k == 1:
                    relay(0, 1).start()
                    for cp in first_copies(1)[1:]:
                        cp.start()
                if k == 2:
                    relay(0, 0).start()
                if k in (4, 5):
                    arrival(1, k - 3).wait_recv()
                    relay(1, 5 - k).start()

        rows = pl.ds(pl.multiple_of(i * tm, tm), tm)

        @pl.when(g == 0)
        def _():
            xf = x_ref[...]
            ms = jnp.mean(xf * xf, axis=-1, keepdims=True)
            hn = xf * lax.rsqrt(ms + NORM_EPS) * w_ref[...]
            hnt_ref[...] = hn.T.astype(BF16)
            hn_s[rows, :] = hn.astype(BF16)

        group = order_ref[g]

        def sorted_copy():
            per = tm // SORT_RESIDUES
            cols = pl.ds(pl.multiple_of(group * COL_BLOCK, COL_BLOCK), COL_BLOCK)
            buf = i % 2

            def out_copy(tile, b):
                return pltpu.make_async_copy(
                    sort_stage.at[b], qkv_hbm.at[:, pl.ds(pl.multiple_of(tile * per, per), per), cols],
                    local_sems.at[N_DEV + 1 + b])

            @pl.when(i >= 2)
            def _():
                out_copy(i - 2, buf).wait()

            for s in range(COL_BLOCK // LANES):
                slab_tmp[s] = proj_ref[:, LANES * s:LANES * (s + 1)]
            for r in range(SORT_RESIDUES):
                for s in range(COL_BLOCK // LANES):
                    sort_stage[buf, r, :, LANES * s:LANES * (s + 1)] = (
                        slab_tmp.at[s][pl.ds(r, per, stride=SORT_RESIDUES), :])
            out_copy(i, buf).start()

            @pl.when(i == n_tiles - 1)
            def _():
                out_copy(i - 1, 1 - buf).wait()
                out_copy(i, buf).wait()

        @pl.when(group < 2)
        def _():
            proj_ref[...] = _rot(_mm(hn_s[rows, :], w_land[group]), c_ref[...], sa_ref[...], sb_ref[...])
            sorted_copy()

        @pl.when(group == 2)
        def _():
            proj_ref[...] = _mm(hn_s[rows, :], w_land[group])
            sorted_copy()

        @pl.when(group > 2)
        def _():
            proj_ref[...] = _mm(hn_s[rows, :], w_land[group])

        @pl.when((g == N_DEV - 1) & (i == n_tiles - 1))
        def _():
            pass_on(1, 0).start()
            pass_on(1, 1).start()
            two_hop_half(1, 0).wait_recv()
            two_hop_half(1, 1).wait_recv()
            pass_on(1, 2).start()
            for k in (0, 4, 5, 6):
                arrival(1, k).wait_recv()
            for which in (0, 1):
                for cp in (first_copies(which) + [relay(which, part) for part in range(2)]
                           + [pass_on(which, j) for j in range(3)]):
                    cp.wait_send()
            wout_copy = pltpu.make_async_copy(wout_land, gout_hbm, local_sems.at[N_DEV])
            wout_copy.start()
            for step in range(N_DEV):
                to_hbm(step).wait()
            wout_copy.wait()

    me = _my_place()
    x_, y_, c_ = me
    chips = [(1 - x_, y_), (x_, 1 - y_), (1 - x_, 1 - y_)]
    order = jnp.stack([_flat(p) for p in (
        me, (x_, y_, 1 - c_), (*chips[0], c_), (*chips[1], c_), (*chips[0], 1 - c_), (*chips[1], 1 - c_),
        (*chips[2], c_), (*chips[2], 1 - c_))]).astype(jnp.int32)

    first_sweep = lambda g, i, order: (jnp.where(g == 0, i, n_tiles - 1), 0)
    tab = pl.BlockSpec((tm, LANES), lambda g, i, order: (jnp.where(order[g] < 2, i, 0), 0))
    whole = lambda: pl.BlockSpec(memory_space=pltpu.VMEM)
    grid_spec = pltpu.PrefetchScalarGridSpec(
        num_scalar_prefetch=1, grid=(N_DEV, n_tiles),
        in_specs=[pl.BlockSpec((tm, D_MODEL), first_sweep),
                  pl.BlockSpec((1, D_MODEL), lambda g, i, order: (0, 0)),
                  whole(), whole(), tab, tab, tab],
        out_specs=(pl.BlockSpec((None, tm, COL_BLOCK), lambda g, i, order: (order[g], i, 0)),
                   pl.BlockSpec((D_MODEL, tm), lambda g, i, order: (0, jnp.where(g == 0, i, n_tiles - 1))),
                   pl.BlockSpec(memory_space=pl.ANY), pl.BlockSpec(memory_space=pl.ANY),
                   pl.BlockSpec(memory_space=pl.ANY)),
        scratch_shapes=[pltpu.VMEM((SEQ, D_MODEL), BF16),
                        pltpu.VMEM((N_DEV, D_MODEL, COL_BLOCK), BF16),
                        pltpu.VMEM((D_MODEL, D_MODEL), BF16),
                        pltpu.VMEM((D_MODEL, COL_BLOCK), BF16),
                        pltpu.VMEM((2, SORT_RESIDUES, tm // SORT_RESIDUES, COL_BLOCK), F32),
                        pltpu.VMEM((COL_BLOCK // LANES, tm, LANES), F32),
                        pltpu.SemaphoreType.DMA((16,)), pltpu.SemaphoreType.DMA((16,)),
                        pltpu.SemaphoreType.DMA((N_DEV + 3,))])
    proj, hn_t, w_in_g, w_out_g, qkv_sorted = pl.pallas_call(
        body, name="gather_project", grid_spec=grid_spec,
        out_shape=(jax.ShapeDtypeStruct((N_DEV, SEQ, COL_BLOCK), F32), jax.ShapeDtypeStruct((D_MODEL, SEQ), BF16),
                   jax.ShapeDtypeStruct((D_MODEL, IN_COLS), BF16), jax.ShapeDtypeStruct((D_MODEL, D_MODEL), BF16),
                   jax.ShapeDtypeStruct((SORT_RESIDUES, SORT_ROWS, 3 * COL_BLOCK), F32)),
        compiler_params=_params(("arbitrary", "arbitrary")),
    )(order, x, mix_w, w_in, w_out, rc, rsa, rsb)
    return proj, hn_t, w_in_g, w_out_g, qkv_sorted.reshape(SEQ, 3 * COL_BLOCK)


SCORE_SCALE = HEAD_DIM ** -0.5
ATTN_GROUP_FWD = 16
ATTN_GROUP_BWD = 8
BLOCKS_PER_PATTERN = SEQ // ATTN_BLOCK
SORT_RESIDUES = 16
SORT_ROWS = SEQ // SORT_RESIDUES


def _write_band_bias(bias_ref):
    row = lax.broadcasted_iota(jnp.int32, (2 * ATTN_BLOCK, 2 * ATTN_BLOCK), 0) & (ATTN_BLOCK - 1)
    col = lax.broadcasted_iota(jnp.int32, (2 * ATTN_BLOCK, 2 * ATTN_BLOCK), 1)
    for pi, d in enumerate(DILATIONS):
        per = SORT_RESIDUES // d
        ahead = per * (row % (8 * d) - col % (16 * d)) + (row // (8 * d) - col // (16 * d))
        dist = ATTN_BLOCK + ahead
        bias_ref[2 * pi] = jnp.where((dist >= 0) & (dist <= ATTN_BLOCK), 0.0, NEG_BIG)
        bias_ref[2 * pi + 1] = jnp.where(ahead >= 0, 0.0, NEG_BIG)


def _head0_lanes():
    return lax.broadcasted_iota(jnp.int32, (ATTN_BLOCK, LANES), 1) < HEAD_DIM


def _stack_heads(t, h0):
    return jnp.concatenate([jnp.where(h0, t, 0.0), jnp.where(h0, 0.0, t)], axis=0).astype(BF16)


def _block_runs(i, d):
    nblk = BLOCKS_PER_PATTERN // d
    r, n = i // nblk, i % nblk
    kn = jnp.maximum(n - 1, 0)
    rows, keys = [], []
    for c in range(SORT_RESIDUES // d):
        base = SORT_ROWS * (c * d + r)
        rows.append(pl.ds(pl.multiple_of(base + 8 * d * n, 8), 8 * d))
        keys.append(pl.ds(pl.multiple_of(base + 8 * d * kn, 8), 16 * d))
    return rows, keys, (n == 0).astype(jnp.int32)


def _take(ref, runs):
    return jnp.concatenate([ref[run, :] for run in runs], axis=0)


def _put(ref, runs, value, add=False):
    at = 0
    for run in runs:
        piece = value[at:at + run.size]
        if add:
            ref[run, :] += piece
        else:
            ref[run, :] = piece
        at += run.size


def _sort_copies(src_hbm, lane_block, dst_ref, sem_ref):
    lanes = pl.ds(pl.multiple_of(LANES * lane_block, LANES), LANES)
    return [pltpu.make_async_copy(src_hbm.at[:, r, lanes], dst_ref.at[pl.ds(SORT_ROWS * r, SORT_ROWS), :],
                                  sem_ref.at[r]) for r in range(SORT_RESIDUES)]


def _unsort_copies(src_ref, dst_hbm, lane_block, sem_ref):
    lanes = pl.ds(pl.multiple_of(LANES * lane_block, LANES), LANES)
    return [pltpu.make_async_copy(src_ref.at[pl.ds(SORT_ROWS * r, SORT_ROWS), :], dst_hbm.at[:, r, lanes],
                                  sem_ref.at[r]) for r in range(SORT_RESIDUES)]


def _for_each_group(d, n_group, load, compute, store):
    def group(g, carry):
        items = [load(*_block_runs(g * n_group + u, d)) for u in range(n_group)]
        results = [compute(item) for item in items]
        for item, res in zip(items, results):
            store(item, res)
        return carry

    lax.fori_loop(0, BLOCKS_PER_PATTERN // n_group, group, 0)


def _attn_fwd_fused(qkv_sorted):
    n_pat = len(DILATIONS)
    tile2 = (2 * ATTN_BLOCK, LANES)

    def body(q_ref, k_ref, v_ref, o_hbm, lse_ref, o_slots, m_acc, l_acc, bias_ref, out_sem):
        step, n_steps = pl.program_id(0), pl.num_programs(0)
        pl.when(step == 0)(lambda: _write_band_bias(bias_ref))
        slot = step % 2
        o_acc = o_slots.at[slot]
        h0 = _head0_lanes()
        for pi, d in enumerate(DILATIONS):
            first, last = pi == 0, pi == n_pat - 1

            def load(rows, keys, which, first=first, pi=pi):
                item = dict(rows=rows, keys=keys, which=2 * pi + which)
                if not first:
                    item.update(o=_take(o_acc, rows), m=[_take(m_acc.at[h], rows) for h in range(2)],
                                l=[_take(l_acc.at[h], rows) for h in range(2)])
                return item

            def compute(item, first=first):
                kb = _take(k_ref, item["keys"]).astype(BF16)
                vb = _take(v_ref, item["keys"]).astype(BF16)
                s = _mm_nt(_stack_heads(_take(q_ref, item["rows"]) * SCORE_SCALE, h0), kb) + bias_ref[item["which"]]
                mb = jnp.max(s, axis=-1, keepdims=True)
                if first:
                    p = jnp.exp(s - mb)
                    mn = jnp.broadcast_to(mb, tile2)
                else:
                    m_old = jnp.concatenate(item["m"], axis=0)
                    mn = jnp.maximum(m_old, mb)
                    alpha = jnp.exp(m_old - mn)
                    p = jnp.exp(s - jnp.concatenate([mn, mn], axis=1))
                ls = jnp.sum(p, axis=-1, keepdims=True)
                pv = _mm(p.astype(BF16), vb)
                if first:
                    return pv, mn, jnp.broadcast_to(ls, tile2)
                o_old = jnp.concatenate([item["o"], item["o"]], axis=0)
                return alpha * o_old + pv, mn, alpha * jnp.concatenate(item["l"], axis=0) + ls

            def store(item, res, last=last):
                rows = item["rows"]
                (o0, o1), (m0, m1), (l0, l1) = ((a[:ATTN_BLOCK], a[ATTN_BLOCK:]) for a in res)
                if last:
                    _put(o_acc, rows, jnp.where(h0, o0 / l0, o1 / l1))
                    _put(lse_ref, rows, jnp.where(h0, m0 + jnp.log(l0), m1 + jnp.log(l1)))
                else:
                    _put(o_acc, rows, jnp.where(h0, o0, o1))
                    for h, (m, l) in enumerate(((m0, l0), (m1, l1))):
                        _put(m_acc.at[h], rows, m)
                        _put(l_acc.at[h], rows, l)

            _for_each_group(d, ATTN_GROUP_FWD, load, compute, store)

        def copies_out(of_step):
            return _unsort_copies(o_slots.at[of_step % 2], o_hbm, of_step, out_sem.at[of_step % 2])

        @pl.when(step > 0)
        def _():
            for copy in copies_out(step - 1):
                copy.wait()

        for copy in copies_out(step):
            copy.start()

        @pl.when(step == n_steps - 1)
        def _():
            for copy in copies_out(step):
                copy.wait()

    slab = lambda g: pl.BlockSpec((SEQ, LANES), functools.partial(lambda hp, g: (0, 4 * g + hp), g=g))
    wide = jax.ShapeDtypeStruct((SEQ, ATTN_WIDTH), F32)
    o_rows, lse = pl.pallas_call(
        body, name="attn_fwd", grid=(4,),
        out_shape=(jax.ShapeDtypeStruct((SORT_ROWS, SORT_RESIDUES, ATTN_WIDTH), F32), wide),
        in_specs=[slab(0), slab(1), slab(2)], out_specs=(pl.BlockSpec(memory_space=pl.ANY), slab(0)),
        scratch_shapes=[pltpu.VMEM((2, SEQ, LANES), F32), pltpu.VMEM((2, SEQ, LANES), F32),
                        pltpu.VMEM((2, SEQ, LANES), F32),
                        pltpu.VMEM((2 * len(DILATIONS), 2 * ATTN_BLOCK, 2 * ATTN_BLOCK), F32),
                        pltpu.SemaphoreType.DMA((2, SORT_RESIDUES))],
        compiler_params=_params(("arbitrary",)),
    )(qkv_sorted, qkv_sorted, qkv_sorted)
    return o_rows.reshape(SEQ, ATTN_WIDTH), lse


def _attn_bwd_fused(qkv_sorted, d_out, lse_sorted, delta):
    def body(q_ref, k_ref, v_ref, do_hbm, lse_ref, del_hbm, dq_hbm, dk_hbm, dv_hbm,
             in_slots, out_slots, bias_ref, in_sem, out_sem):
        step, n_steps = pl.program_id(0), pl.num_programs(0)
        slot = step % 2

        def copies_in(of_step):
            s = of_step % 2
            return [copy for j, hbm in enumerate((do_hbm, del_hbm))
                    for copy in _sort_copies(hbm, of_step, in_slots.at[s, j], in_sem.at[s, j])]

        def copies_out(of_step):
            s = of_step % 2
            return [copy for j, hbm in enumerate((dq_hbm, dk_hbm, dv_hbm))
                    for copy in _unsort_copies(out_slots.at[s, j], hbm, of_step, out_sem.at[s, j])]

        @pl.when(step == 0)
        def _():
            for copy in copies_in(step):
                copy.start()
            _write_band_bias(bias_ref)

        @pl.when(step + 1 < n_steps)
        def _():
            for copy in copies_in(step + 1):
                copy.start()

        do_s, del_s = in_slots.at[slot, 0], in_slots.at[slot, 1]
        dq_s, dk_s, dv_s = (out_slots.at[slot, j] for j in range(3))
        dk_s[...] = jnp.zeros_like(dk_s)
        dv_s[...] = jnp.zeros_like(dv_s)
        for copy in copies_in(step):
            copy.wait()
        h0 = _head0_lanes()
        for pi, d in enumerate(DILATIONS):
            first = pi == 0

            def load(rows, keys, which, pi=pi):
                return dict(rows=rows, keys=keys, q=_take(q_ref, rows), g=_take(do_s, rows),
                            lse=_take(lse_ref, rows), delta=_take(del_s, rows),
                            k=_take(k_ref, keys).astype(BF16), v=_take(v_ref, keys).astype(BF16),
                            bias=bias_ref[2 * pi + which])

            def per_head(t):
                swapped = pltpu.roll(t, HEAD_DIM, 1)
                both = jnp.concatenate([jnp.where(h0, t, swapped), jnp.where(h0, swapped, t)], axis=0)
                return jnp.concatenate([both, both], axis=1)

            def compute(item):
                q2, g2 = _stack_heads(item["q"] * SCORE_SCALE, h0), _stack_heads(item["g"], h0)
                s = _mm_nt(q2, item["k"]) + item["bias"]
                p = jnp.exp(s - per_head(item["lse"]))
                dp = _mm_nt(g2, item["v"])
                ds = (p * (dp - per_head(item["delta"]))).astype(BF16)
                dq2 = _mm(ds, item["k"])
                dq = jnp.where(h0, dq2[:ATTN_BLOCK], dq2[ATTN_BLOCK:]) * SCORE_SCALE
                return dq, _mm_tn(ds, q2), _mm_tn(p.astype(BF16), g2)

            def store(item, res, first=first):
                _put(dq_s, item["rows"], res[0], add=not first)
                _put(dk_s, item["keys"], res[1], add=True)
                _put(dv_s, item["keys"], res[2], add=True)

            _for_each_group(d, ATTN_GROUP_BWD, load, compute, store)

        @pl.when(step > 0)
        def _():
            for copy in copies_out(step - 1):
                copy.wait()

        for copy in copies_out(step):
            copy.start()

        @pl.when(step == n_steps - 1)
        def _():
            for copy in copies_out(step):
                copy.wait()

    slab = lambda g: pl.BlockSpec((SEQ, LANES), functools.partial(lambda hp, g: (0, 4 * g + hp), g=g))
    anywhere = pl.BlockSpec(memory_space=pl.ANY)
    by_residue = (SORT_ROWS, SORT_RESIDUES, ATTN_WIDTH)
    grads = pl.pallas_call(
        body, name="attn_bwd", grid=(4,), out_shape=(jax.ShapeDtypeStruct(by_residue, F32),) * 3,
        scratch_shapes=[pltpu.VMEM((2, 2, SEQ, LANES), F32), pltpu.VMEM((2, 3, SEQ, LANES), F32),
                        pltpu.VMEM((2 * len(DILATIONS), 2 * ATTN_BLOCK, 2 * ATTN_BLOCK), F32),
                        pltpu.SemaphoreType.DMA((2, 2, SORT_RESIDUES)), pltpu.SemaphoreType.DMA((2, 3, SORT_RESIDUES))],
        in_specs=[slab(0), slab(1), slab(2), anywhere, slab(0), anywhere], out_specs=(anywhere,) * 3,
        compiler_params=_params(("arbitrary",)),
    )(qkv_sorted, qkv_sorted, qkv_sorted, d_out.reshape(by_residue), lse_sorted, delta.reshape(by_residue))
    return tuple(g.reshape(SEQ, ATTN_WIDTH) for g in grads)


def _hgrn_lower_bound(lb_ref):
    r0, r1 = lb_ref[0:1, :], lb_ref[1:2, :]
    mx = jnp.maximum(r0, r1)
    e0, e1 = jnp.exp(r0 - mx), jnp.exp(r1 - mx)
    return e0 / (e0 + e1)


def _hgrn_gates(hq, hf, lb):
    sq = _sigmoid(hq)
    sg = _sigmoid(hf)
    f = lb + (1.0 - lb) * sg
    return hq * sq, sq, sg, f, 1.0 - f, jnp.log(f)


HGRN_PAIR = 4
HGRN_SEQ_BLOCK = 1024
HGRN_GROUP = 4
HGRN_ROWS = HGRN_GROUP * HGRN_CHUNK


def _hgrn_specs(reverse):
    n_blocks = SEQ // HGRN_SEQ_BLOCK
    width = HGRN_PAIR * HGRN_DIM
    blk = (lambda s: n_blocks - 1 - s) if reverse else (lambda s: s)
    cols = lambda g: pl.BlockSpec((None, HGRN_SEQ_BLOCK, width), functools.partial(lambda p, s, g: (g, blk(s), p), g=g))
    pair = pl.BlockSpec((HGRN_SEQ_BLOCK, width), lambda p, s: (blk(s), p))
    lb = pl.BlockSpec((2, width), lambda p, s: (0, p))
    states = pl.BlockSpec((HGRN_PAIR, HGRN_SEQ_BLOCK // HGRN_CHUNK, HGRN_DIM, HGRN_DIM),
                          lambda p, s: (p, blk(s), 0, 0))
    return cols, pair, lb, states


def _chunk_masks():
    ri = lax.broadcasted_iota(jnp.int32, (HGRN_ROWS, HGRN_ROWS), 0)
    ci = lax.broadcasted_iota(jnp.int32, (HGRN_ROWS, HGRN_ROWS), 1)
    same = (ri // HGRN_CHUNK) == (ci // HGRN_CHUNK)
    return same, same & (ri >= ci), same & (ri <= ci)


def _mm_select(sel, v):
    hi = v.astype(BF16)
    r1 = v - hi.astype(F32)
    mid = r1.astype(BF16)
    lo = (r1 - mid.astype(F32)).astype(BF16)
    return _mm(sel, hi) + _mm(sel, mid) + _mm(sel, lo)


def _head_cols(a, h):
    return a[:, HGRN_DIM * h:HGRN_DIM * (h + 1)]


def _hgrn_fwd(proj, lb_raw):
    t, rws = HGRN_CHUNK, HGRN_ROWS

    def body(hq_ref, hf_ref, hi_ref, lb_ref, rec_ref, st_ref, state):
        @pl.when(pl.program_id(1) == 0)
        def _():
            state[...] = jnp.zeros_like(state)

        lb = _hgrn_lower_bound(lb_ref)
        same, causal, _ = _chunk_masks()
        sel = jnp.concatenate([causal, same], axis=0).astype(BF16)

        def group(g, sts):
            rows = pl.ds(pl.multiple_of(g * rws, rws), rws)
            q, _, _, _, k, lf = _hgrn_gates(hq_ref[rows, :], hf_ref[rows, :], lb)
            sums = _mm_select(sel, lf)
            cum, last = sums[:rws], sums[rws:]
            qd = (q * jnp.exp(cum)).astype(BF16)
            ki = (k * jnp.exp(-cum)).astype(BF16)
            ke = (k * jnp.exp(last - cum)).astype(BF16)
            vb = hi_ref[rows, :].astype(BF16)
            dec = jnp.exp(last)
            new_sts, recs = [], []
            for h in range(HGRN_PAIR):
                qd_h, ke_h, vb_h = _head_cols(qd, h), _head_cols(ke, h), _head_cols(vb, h)
                att = jnp.where(causal, _mm_nt(qd_h, _head_cols(ki, h)), 0.0).astype(BF16)
                intra = _mm(att, vb_h)
                st = sts[h]
                outs = []
                for c in range(HGRN_GROUP):
                    sl = slice(c * t, (c + 1) * t)
                    st_ref[h, g * HGRN_GROUP + c] = st
                    outs.append(intra[sl] + _mm_nt(qd_h[sl], st.astype(BF16)))
                    st = st * _head_cols(dec[c * t:c * t + 1, :], h) + _mm_tn(vb_h[sl], ke_h[sl])
                new_sts.append(st)
                recs.append(jnp.concatenate(outs, axis=0))
            rec_ref[rows, :] = jnp.concatenate(recs, axis=1)
            return tuple(new_sts)

        sts = lax.fori_loop(0, HGRN_SEQ_BLOCK // rws, group, tuple(state[h] for h in range(HGRN_PAIR)))
        for h in range(HGRN_PAIR):
            state[h] = sts[h]

    cols, pair, lb, states = _hgrn_specs(reverse=False)
    return pl.pallas_call(
        body, name="hgrn_fwd", grid=(HGRN_HEADS // HGRN_PAIR, SEQ // HGRN_SEQ_BLOCK),
        out_shape=(jax.ShapeDtypeStruct((SEQ, HGRN_WIDTH), F32),
                   jax.ShapeDtypeStruct((HGRN_HEADS, N_CHUNKS, HGRN_DIM, HGRN_DIM), F32)),
        in_specs=[cols(4), cols(5), cols(6), lb], out_specs=(pair, states),
        scratch_shapes=[pltpu.VMEM((HGRN_PAIR, HGRN_DIM, HGRN_DIM), F32)],
        compiler_params=_params(("parallel", "arbitrary")),
    )(proj, proj, proj, lb_raw)


def _hgrn_bwd(proj, lb_raw, d_rec, states):
    t, rws = HGRN_CHUNK, HGRN_ROWS

    def body(hq_ref, hf_ref, hi_ref, lb_ref, do_ref, st_ref, dhq_ref, dhf_ref, dhi_ref, dlb_ref,
             dstate, dlb_acc):
        lb = _hgrn_lower_bound(lb_ref)
        same, causal, anti = _chunk_masks()
        sel = jnp.concatenate([causal, same], axis=0).astype(BF16)
        sel_t = jnp.concatenate([anti, same], axis=1).astype(BF16)
        @pl.when(pl.program_id(1) == 0)
        def _():
            dstate[...] = jnp.zeros_like(dstate)
            dlb_acc[...] = jnp.zeros_like(dlb_acc)

        n_groups = HGRN_SEQ_BLOCK // rws
        chunks = [slice(c * t, (c + 1) * t) for c in range(HGRN_GROUP)]

        def group(i, dsts_in):
            g = n_groups - 1 - i
            rows = pl.ds(pl.multiple_of(g * rws, rws), rws)
            hq = hq_ref[rows, :]
            q, sq, sg, f, k, lf = _hgrn_gates(hq, hf_ref[rows, :], lb)
            sums = _mm_select(sel, lf)
            cum, last = sums[:rws], sums[rws:]
            e_cum, e_inv, e_end, dec = jnp.exp(cum), jnp.exp(-cum), jnp.exp(last - cum), jnp.exp(last)
            qd, ki, ke = q * e_cum, k * e_inv, k * e_end
            qdb, kib, keb = qd.astype(BF16), ki.astype(BF16), ke.astype(BF16)
            vb = hi_ref[rows, :].astype(BF16)
            gb = do_ref[rows, :].astype(BF16)

            dsts_out, per_head = [], []
            for h in range(HGRN_PAIR):
                qdb_h, kib_h, keb_h = _head_cols(qdb, h), _head_cols(kib, h), _head_cols(keb, h)
                vb_h, gb_h = _head_cols(vb, h), _head_cols(gb, h)
                att = jnp.where(causal, _mm_nt(qdb_h, kib_h), 0.0).astype(BF16)
                datt = jnp.where(causal, _mm_nt(gb_h, vb_h), 0.0).astype(BF16)
                dv = _mm_tn(att, gb_h)
                dqd = _mm(datt, kib_h)
                dki = _mm_tn(datt, qdb_h)

                decs = [_head_cols(dec[c * t:c * t + 1, :], h) for c in range(HGRN_GROUP)]
                dsts = [None] * HGRN_GROUP
                dst = dsts_in[h]
                for c in reversed(range(HGRN_GROUP)):
                    dsts[c] = dst
                    dst = dst * decs[c] + _mm_tn(gb_h[chunks[c]], qdb_h[chunks[c]])
                dsts_out.append(dst)

                dv_x, dqd_x, dke, dlast_x = [], [], [], []
                for c, sl in enumerate(chunks):
                    st_prev = st_ref[h, g * HGRN_GROUP + c]
                    dstb = dsts[c].astype(BF16)
                    dv_x.append(_mm_nt(keb_h[sl], dstb))
                    dqd_x.append(_mm(gb_h[sl], st_prev.astype(BF16)))
                    dke.append(_mm(vb_h[sl], dstb))
                    ddec = jnp.sum(dsts[c] * st_prev, axis=0, keepdims=True)
                    dlast_x.append(jnp.broadcast_to(ddec * decs[c], (t, HGRN_DIM)))
                per_head.append((dv + jnp.concatenate(dv_x, axis=0), dqd + jnp.concatenate(dqd_x, axis=0),
                                 dki, jnp.concatenate(dke, axis=0), jnp.concatenate(dlast_x, axis=0)))
            dv, dqd, dki, dke, dlast = (jnp.concatenate(list(parts), axis=1) for parts in zip(*per_head))

            dq = dqd * e_cum
            dk = dki * e_inv + dke * e_end
            dke_ke = dke * ke
            dcum = dqd * qd - dki * ki - dke_ke
            dlf = _mm_select(sel_t, jnp.concatenate([dcum, dke_ke], axis=0)) + dlast
            df = dlf / f - dk
            dhq_ref[rows, :] = dq * (sq * (1.0 + hq * (1.0 - sq)))
            dhf_ref[rows, :] = df * (1.0 - lb) * (sg * (1.0 - sg))
            dhi_ref[rows, :] = dv
            dlb_acc[...] += jnp.sum(df * (1.0 - sg), axis=0, keepdims=True)
            return tuple(dsts_out)

        dsts = lax.fori_loop(0, n_groups, group, tuple(dstate[h] for h in range(HGRN_PAIR)))
        for h in range(HGRN_PAIR):
            dstate[h] = dsts[h]
        g0 = dlb_acc[...] * lb * (1.0 - lb)
        dlb_ref[...] = jnp.concatenate([g0, -g0], axis=0)

    cols, pair, lb_spec, st_spec = _hgrn_specs(reverse=True)
    wide = jax.ShapeDtypeStruct((SEQ, HGRN_WIDTH), F32)
    return pl.pallas_call(
        body, name="hgrn_bwd", grid=(HGRN_HEADS // HGRN_PAIR, SEQ // HGRN_SEQ_BLOCK),
        out_shape=(wide, wide, wide, jax.ShapeDtypeStruct((2, HGRN_WIDTH), F32)),
        in_specs=[cols(4), cols(5), cols(6), lb_spec, pair, st_spec],
        out_specs=(pair, pair, pair, lb_spec),
        scratch_shapes=[pltpu.VMEM((HGRN_PAIR, HGRN_DIM, HGRN_DIM), F32),
                        pltpu.VMEM((1, HGRN_PAIR * HGRN_DIM), F32)],
        compiler_params=_params(("parallel", "arbitrary")),
    )(proj, proj, proj, lb_raw, d_rec, states)


def _group_sum(v, group):
    parts = []
    for s in range(v.shape[1] // LANES):
        slab = v[:, LANES * s:LANES * (s + 1)]
        if group == LANES:
            parts.append(jnp.broadcast_to(jnp.sum(slab, axis=-1, keepdims=True), slab.shape))
        else:
            h0 = lax.broadcasted_iota(jnp.int32, slab.shape, 1) < HEAD_DIM
            s0 = jnp.sum(jnp.where(h0, slab, 0.0), axis=-1, keepdims=True)
            s1 = jnp.sum(jnp.where(h0, 0.0, slab), axis=-1, keepdims=True)
            parts.append(jnp.where(h0, s0, s1))
    return jnp.concatenate(parts, axis=1)


def _mid(attn_o, rec, proj, x, target, w_out_g, attn_w, hgrn_w, final_w):
    tm = 256

    def branch_fwd(o, gate, w, group):
        r = lax.rsqrt(_group_sum(o * o, group) * (1.0 / group) + NORM_EPS)
        nrm = o * r
        sg = _sigmoid(gate)
        return r, nrm, sg, nrm * w * (gate * sg)

    def branch_bwd(dy, r, nrm, sg, gate, w, group):
        silu = gate * sg
        d_gate = dy * nrm * w * (sg * (1.0 + gate * (1.0 - sg)))
        d_w = jnp.sum(dy * nrm * silu, axis=0, keepdims=True)
        dn = dy * w * silu
        d_o = r * (dn - nrm * (_group_sum(dn * nrm, group) * (1.0 / group)))
        return d_o, d_gate, d_w

    def body(o_ref, rec_ref, ag_ref, hg_ref, x_ref, tgt_ref, wout_ref, aw_ref, hw_ref, fw_ref,
             dx2_ref, do_ref, delta_ref, dag_ref, drec_ref, dhg_ref, dwout_ref, dfw_ref, daw_ref, dhw_ref,
             loss_ref, dwout_acc):
        i = pl.program_id(0)

        @pl.when(i == 0)
        def _():
            dwout_acc[...] = jnp.zeros_like(dwout_acc)
            dfw_ref[...] = jnp.zeros_like(dfw_ref)
            daw_ref[...] = jnp.zeros_like(daw_ref)
            dhw_ref[...] = jnp.zeros_like(dhw_ref)
            loss_ref[...] = jnp.zeros_like(loss_ref)

        o, rc, ag, hg = o_ref[...], rec_ref[...], ag_ref[...], hg_ref[...]
        aw, hw, fw = aw_ref[...], hw_ref[...], fw_ref[...]
        ra, na, sga, ya = branch_fwd(o, ag, aw, HEAD_DIM)
        rh, nh, sgh, yh = branch_fwd(rc, hg, hw, HGRN_DIM)
        mixed = jnp.concatenate([ya, yh], axis=1).astype(BF16)
        wout = wout_ref[...]
        x2 = x_ref[...] + _mm(mixed, wout)
        rstd = lax.rsqrt(jnp.mean(x2 * x2, axis=-1, keepdims=True) + NORM_EPS)
        xn = x2 * rstd
        err = xn * fw - tgt_ref[...]
        row_loss = jnp.mean(err * err, axis=-1, keepdims=True)
        loss_ref[...] += 0.5 * jnp.sum(row_loss, axis=0, keepdims=True)
        dy = err * (1.0 / D_MODEL)
        dfw_ref[...] += jnp.sum(dy * xn, axis=0, keepdims=True)
        dxn = dy * fw
        dx2 = rstd * (dxn - xn * jnp.mean(dxn * xn, axis=-1, keepdims=True))
        dx2_ref[...] = dx2
        dx2b = dx2.astype(BF16)
        dwout_acc[...] += _mm_tn(mixed, dx2b)

        @pl.when(i == pl.num_programs(0) - 1)
        def _():
            dwout_ref[...] = dwout_acc[...].astype(BF16)

        dmixed = _mm_nt(dx2b, wout)

        d_o, d_ag, d_aw = branch_bwd(dmixed[:, :ATTN_WIDTH], ra, na, sga, ag, aw, HEAD_DIM)
        d_rec, d_hg, d_hw = branch_bwd(dmixed[:, ATTN_WIDTH:], rh, nh, sgh, hg, hw, HGRN_DIM)
        do_ref[...] = d_o
        delta_ref[...] = _group_sum(d_o * o, HEAD_DIM)
        dag_ref[...] = d_ag
        drec_ref[...] = d_rec
        dhg_ref[...] = d_hg
        daw_ref[...] += d_aw
        dhw_ref[...] += d_hw

    half = lambda: pl.BlockSpec((tm, COL_BLOCK), lambda i: (i, 0))
    full = lambda: pl.BlockSpec((tm, D_MODEL), lambda i: (i, 0))
    fixed = lambda r, c: pl.BlockSpec((r, c), lambda i: (0, 0))
    wide = jax.ShapeDtypeStruct((SEQ, COL_BLOCK), F32)
    return pl.pallas_call(
        body, name="mid", grid=(SEQ // tm,),
        out_shape=(jax.ShapeDtypeStruct((SEQ, D_MODEL), F32), wide, wide, wide, wide, wide,
                   jax.ShapeDtypeStruct((D_MODEL, D_MODEL), BF16),
                   jax.ShapeDtypeStruct((1, D_MODEL), F32), jax.ShapeDtypeStruct((1, COL_BLOCK), F32),
                   jax.ShapeDtypeStruct((1, COL_BLOCK), F32), jax.ShapeDtypeStruct((1, 1), F32)),
        scratch_shapes=[pltpu.VMEM((D_MODEL, D_MODEL), F32)],
        in_specs=[half(), half(),
                  pl.BlockSpec((None, tm, COL_BLOCK), lambda i: (3, i, 0)),
                  pl.BlockSpec((None, tm, COL_BLOCK), lambda i: (7, i, 0)),
                  full(), full(), fixed(D_MODEL, D_MODEL), fixed(1, COL_BLOCK), fixed(1, COL_BLOCK),
                  fixed(1, D_MODEL)],
        out_specs=(full(), half(), half(), half(), half(), half(), fixed(D_MODEL, D_MODEL),
                   fixed(1, D_MODEL), fixed(1, COL_BLOCK), fixed(1, COL_BLOCK), fixed(1, 1)),
        compiler_params=_params(("arbitrary",)),
    )(attn_o, rec, proj, proj, x, target, w_out_g, attn_w, hgrn_w, final_w)


def _in_proj_bwd_rows(d_groups, w_g, x, dx2, mix_w, rc, rsa, rsb):
    tm = 256

    def body(*refs):
        dg_refs = refs[:N_DEV]
        wg_ref, x_ref, dx2_ref, w_ref, c_ref, sa_ref, sb_ref, gx_ref, dpb_ref, dmw_ref = refs[N_DEV:]

        @pl.when(pl.program_id(0) == 0)
        def _():
            dmw_ref[...] = jnp.zeros_like(dmw_ref)

        parts = []
        for j in range(N_DEV):
            dp = dg_refs[j][...]
            if j < 2:
                dp = _rot_transposed(dp, c_ref[...], sa_ref[...], sb_ref[...])
            parts.append(dp.astype(BF16))
        dpb = jnp.concatenate(parts, axis=1)
        for j in range(N_DEV):
            dpb_ref[j] = parts[j]
        g = _mm_nt(dpb, wg_ref[...])
        xf = x_ref[...]
        rstd = lax.rsqrt(jnp.mean(xf * xf, axis=-1, keepdims=True) + NORM_EPS)
        xn = xf * rstd
        dmw_ref[...] += jnp.sum(g * xn, axis=0, keepdims=True)
        gw = g * w_ref[...]
        gx_ref[...] = dx2_ref[...] + rstd * (gw - xn * jnp.mean(gw * xn, axis=-1, keepdims=True))

    tile = lambda cols: pl.BlockSpec((tm, cols), lambda i: (i, 0))
    fixed = lambda r, c: pl.BlockSpec((r, c), lambda i: (0, 0))
    return pl.pallas_call(
        body, name="in_proj_bwd_rows", grid=(SEQ // tm,),
        out_shape=(jax.ShapeDtypeStruct((SEQ, D_MODEL), F32), jax.ShapeDtypeStruct((N_DEV, SEQ, COL_BLOCK), BF16),
                   jax.ShapeDtypeStruct((1, D_MODEL), F32)),
        in_specs=[tile(COL_BLOCK) for _ in range(N_DEV)] + [
            pl.BlockSpec((D_MODEL, IN_COLS), lambda i: (0, 0), pipeline_mode=pl.Buffered(1)),
            tile(D_MODEL), tile(D_MODEL), fixed(1, D_MODEL), tile(LANES), tile(LANES), tile(LANES)],
        out_specs=(tile(D_MODEL), pl.BlockSpec((N_DEV, tm, COL_BLOCK), lambda i: (0, i, 0)), fixed(1, D_MODEL)),
        compiler_params=_params(("arbitrary",)),
    )(*d_groups, w_g, x, dx2, mix_w, rc, rsa, rsb)


def _weights_exchange(hn_t, dproj_b, dwout_p, small_p):
    n_chips = N_DEV // 2
    rb = 128
    S1_IN, S1_OUT, SMALL, S2_IN, S2_OUT = 0, 4, 8, 15, 18
    rel_of_pair = (1, 2, 3, 0)

    def body(order_ref, hnt_ref, dp_ref, dwout_ref, small_ref, gin_ref, gout_ref, gs_ref,
             part, s1_send, s1_in, s1_out, fwd_in, fwd_out, s2_in, s2_out, land_s, send_sems, recv_sems):
        t = pl.program_id(0)
        me = _my_place()
        x, y, c = me
        my_chip = 2 * x + y
        sibling = (x, y, 1 - c)

        def remote(slot, src, dst, to):
            return pltpu.make_async_remote_copy(src_ref=src, dst_ref=dst, send_sem=send_sems.at[slot],
                                                recv_sem=recv_sems.at[slot], device_id=to, device_id_type=MESH)

        def s1_in_copy(pair):
            return remote(S1_IN + pair, s1_send.at[pair], s1_in.at[pair], sibling)

        def s1_out_copy(pair):
            q = my_chip ^ rel_of_pair[pair]
            return remote(S1_OUT + pair, dwout_ref.at[q, 1 - c], s1_out.at[pair], sibling)

        def s2_copies(rel):
            peer = _peer(me, 2 * rel)
            return [remote(S2_IN + rel - 1, fwd_in.at[rel - 1], s2_in.at[rel - 1], peer),
                    remote(S2_OUT + rel - 1, fwd_out.at[rel - 1], s2_out.at[rel - 1], peer)]

        def small_copy(rel):
            return remote(SMALL + rel - 1, small_ref, land_s.at[rel], _peer(me, rel))

        @pl.when(t == 0)
        def _():
            land_s[0] = small_ref[...]
            for pair in range(n_chips):
                s1_out_copy(pair).start()
            for rel in range(1, N_DEV):
                small_copy(rel).start()

        part[...] = _mm(hnt_ref[...], dp_ref[...])

        def rows_loop(n_rows, fn):
            def step(b, carry):
                fn(pl.ds(pl.multiple_of(b * rb, rb), rb))
                return carry
            lax.fori_loop(0, n_rows // rb, step, 0)

        for pair, rel in enumerate(rel_of_pair):
            @pl.when(t == 2 * pair)
            def _(pair=pair):
                s1_send[pair] = part[...].astype(BF16)
                s1_in_copy(pair).start()

            @pl.when(t == 2 * pair + 1)
            def _(pair=pair, rel=rel):
                q = my_chip ^ rel
                s1_in_copy(pair).wait_recv()
                s1_out_copy(pair).wait_recv()
                dst_in = fwd_in.at[rel - 1] if rel else gin_ref
                dst_out = fwd_out.at[rel - 1] if rel else gout_ref

                def add_in(rows):
                    dst_in[rows, :] = (part[rows, :] + s1_in[pair, rows, :].astype(F32)).astype(dst_in.dtype)

                def add_out(rows):
                    dst_out[rows, :] = (dwout_ref[q, c, rows, :].astype(F32)
                                        + s1_out[pair, rows, :].astype(F32)).astype(dst_out.dtype)

                rows_loop(D_MODEL, add_in)
                rows_loop(WOUT_ROWS, add_out)
                if rel:
                    for cp in s2_copies(rel):
                        cp.start()

        @pl.when(t == N_DEV - 1)
        def _():
            for rel in range(1, n_chips):
                for cp in s2_copies(rel):
                    cp.wait_recv()

            def total_in(rows):
                g = gin_ref[rows, :]
                for rel in range(1, n_chips):
                    g = g + s2_in[rel - 1, rows, :].astype(F32)
                gin_ref[rows, :] = g

            def total_out(rows):
                g = gout_ref[rows, :]
                for rel in range(1, n_chips):
                    g = g + s2_out[rel - 1, rows, :].astype(F32)
                gout_ref[rows, :] = g

            rows_loop(D_MODEL, total_in)
            rows_loop(WOUT_ROWS, total_out)

            for rel in range(1, N_DEV):
                small_copy(rel).wait_recv()
            my_flat = _flat(me)
            g = land_s[my_flat ^ 0]
            for dev in range(1, N_DEV):
                g = g + land_s[my_flat ^ dev]
            gs_ref[...] = g

            for pair in range(n_chips):
                s1_in_copy(pair).wait_send()
                s1_out_copy(pair).wait_send()
            for rel in range(1, n_chips):
                for cp in s2_copies(rel):
                    cp.wait_send()
            for rel in range(1, N_DEV):
                small_copy(rel).wait_send()

    place_x, place_y, place_c = _my_place()
    my_chip = 2 * place_x + place_y
    order = jnp.stack([2 * (my_chip ^ rel) + core for rel in rel_of_pair
                       for core in (1 - place_c, place_c)]).astype(jnp.int32)

    whole = lambda: pl.BlockSpec(memory_space=pltpu.VMEM)
    in_blocks = lambda n: pltpu.VMEM((n, D_MODEL, COL_BLOCK), BF16)
    out_blocks = lambda n: pltpu.VMEM((n, WOUT_ROWS, D_MODEL), BF16)
    grid_spec = pltpu.PrefetchScalarGridSpec(
        num_scalar_prefetch=1, grid=(N_DEV,),
        in_specs=[pl.BlockSpec((D_MODEL, SEQ), lambda t, order: (0, 0), pipeline_mode=pl.Buffered(1)),
                  pl.BlockSpec((None, SEQ, COL_BLOCK), lambda t, order: (order[t], 0, 0)), whole(), whole()],
        out_specs=(whole(), whole(), whole()),
        scratch_shapes=[pltpu.VMEM((D_MODEL, COL_BLOCK), F32), in_blocks(n_chips), in_blocks(n_chips),
                        out_blocks(n_chips), in_blocks(n_chips - 1), out_blocks(n_chips - 1),
                        in_blocks(n_chips - 1), out_blocks(n_chips - 1),
                        pltpu.VMEM((N_DEV, SMALL_ROWS, LANES), F32),
                        pltpu.SemaphoreType.DMA((21,)), pltpu.SemaphoreType.DMA((21,))])
    return pl.pallas_call(
        body, name="weights_exchange", grid_spec=grid_spec,
        out_shape=(jax.ShapeDtypeStruct((D_MODEL, COL_BLOCK), F32), jax.ShapeDtypeStruct((WOUT_ROWS, D_MODEL), F32),
                   jax.ShapeDtypeStruct((SMALL_ROWS, LANES), F32)),
        compiler_params=_params(("arbitrary",)),
    )(order, hn_t, dproj_b, dwout_p.reshape(n_chips, 2, WOUT_ROWS, D_MODEL), small_p)


def _adamw(w, g, m, v):
    m = ADAM_B1 * m + (1.0 - ADAM_B1) * g
    v = ADAM_B2 * v + (1.0 - ADAM_B2) * (g * g)
    m_hat = m / (1.0 - ADAM_B1 ** ADAM_STEP)
    v_hat = v / (1.0 - ADAM_B2 ** ADAM_STEP)
    delta = -ADAM_LR * (m_hat / (jnp.sqrt(v_hat) + ADAM_EPS) + ADAM_WD * w)
    return delta, m, v


def _adamw_update(grads, weights, m_old, v_old):
    rb = 256

    def body(*refs):
        g_refs, w_refs, m_refs, v_refs = refs[0:3], refs[3:6], refs[6:9], refs[9:12]
        d_refs, nm_refs, nv_refs = refs[12:15], refs[15:18], refs[18:21]
        for k in range(3):
            n_rows = g_refs[k].shape[0]
            step_rows = min(rb, n_rows)

            def step(b, carry, k=k, step_rows=step_rows):
                rows = pl.ds(pl.multiple_of(b * step_rows, 8), step_rows)
                delta, nm, nv = _adamw(w_refs[k][rows, :], g_refs[k][rows, :], m_refs[k][rows, :], v_refs[k][rows, :])
                d_refs[k][rows, :] = delta
                nm_refs[k][rows, :] = nm
                nv_refs[k][rows, :] = nv
                return carry

            lax.fori_loop(0, n_rows // step_rows, step, 0)

    shapes = tuple(jax.ShapeDtypeStruct(g.shape, F32) for g in grads)
    vm = lambda: pl.BlockSpec(memory_space=pltpu.VMEM)
    outs = pl.pallas_call(
        body, name="adamw_update", out_shape=shapes * 3,
        in_specs=[vm() for _ in range(12)], out_specs=tuple(vm() for _ in range(9)),
        compiler_params=_params(),
    )(*grads, *weights, *m_old, *v_old)
    return outs[0:3], outs[3:6], outs[6:9]


def _pack_small(mix, attn, hgrn, lb, final, loss=None):
    def rows8(a):
        a = a.reshape(-1, LANES)
        return jnp.pad(a, ((0, 8 - a.shape[0]), (0, 0)))
    last = jnp.zeros((8, LANES), F32) if loss is None else jnp.pad(loss.reshape(1, 1), ((0, 7), (0, LANES - 1)))
    return jnp.concatenate([rows8(mix), rows8(attn), rows8(hgrn), rows8(lb), rows8(final), last], axis=0)


def _unpack_small(slab):
    return (slab[ROW_MIX:ROW_MIX + 8].reshape(1, D_MODEL), slab[ROW_ATTN:ROW_ATTN + 4].reshape(1, ATTN_WIDTH),
            slab[ROW_HGRN:ROW_HGRN + 4].reshape(1, HGRN_WIDTH), slab[ROW_LB:ROW_LB + 8].reshape(2, HGRN_WIDTH),
            slab[ROW_FINAL:ROW_FINAL + 8].reshape(D_MODEL))


def _rope(pos_row):
    j = np.arange(ROPE_ROWS)
    inv = np.where(j < ROPE_HALF, ROPE_THETA ** (-(j % ROPE_HALF) * (2.0 / ROPE_DIMS)), 0.0)
    e = np.arange(LANES) % HEAD_DIM
    hit = (j[:, None] == (e % ROPE_HALF)[None, :]) & (j[:, None] < ROPE_HALF)
    sel = np.stack([hit & (e < ROPE_DIMS), hit & (e >= ROPE_HALF) & (e < ROPE_DIMS),
                    -1.0 * (hit & (e < ROPE_HALF))]).astype(np.float32)
    return _rope_tables(pos_row, jnp.asarray(inv.astype(np.float32).reshape(ROPE_ROWS, 1)),
                        jnp.asarray(sel, dtype=BF16))


def _local_step(x, proj, qkv_sorted, w_in_g, w_out_g, tables, mix_w, attn_w, hgrn_w, lb_raw, final_w, target):
    rc, rsa, rsb = tables
    attn_o, lse = _attn_fwd_fused(qkv_sorted)
    rec, states = _hgrn_fwd(proj, lb_raw)

    (dx2, d_o, delta, d_ag, d_rec, d_hg, dwout_p, d_final, d_attn_w, d_hgrn_w, loss) = _mid(
        attn_o, rec, proj, x, target, w_out_g, attn_w, hgrn_w, final_w.reshape(1, D_MODEL))

    dqkv = _attn_bwd_fused(qkv_sorted, d_o, lse, delta)
    d_hq, d_hf, d_hi, d_lb = _hgrn_bwd(proj, lb_raw, d_rec, states)

    grad_x, dproj_b, d_mix = _in_proj_bwd_rows(
        (dqkv[0], dqkv[1], dqkv[2], d_ag, d_hq, d_hf, d_hi, d_hg), w_in_g, x, dx2, mix_w, rc, rsa, rsb)
    small_p = _pack_small(d_mix, d_attn_w, d_hgrn_w, d_lb, d_final, loss)
    return grad_x, dproj_b, dwout_p, small_p


def kernel(x, positions, w_in, w_out, mix_norm_w, attn_out_norm_w, hgrn_out_norm_w, hgrn_lb_raw, final_norm_w, loss_target, m_w_in, m_w_out, m_mix_norm_w, m_attn_out_norm_w, m_hgrn_out_norm_w, m_hgrn_lb_raw, m_final_norm_w, v_w_in, v_w_out, v_mix_norm_w, v_attn_out_norm_w, v_hgrn_out_norm_w, v_hgrn_lb_raw, v_final_norm_w):
    tables = _rope(positions)
    proj, hn_t, w_in_g, w_out_g, qkv_sorted = _gather_project(x[0], mix_norm_w, w_in[0], w_out[0], *tables)
    grad_x, dproj_b, dwout_p, small_p = _local_step(
        x[0], proj, qkv_sorted, w_in_g, w_out_g, tables, mix_norm_w, attn_out_norm_w, hgrn_out_norm_w,
        hgrn_lb_raw, final_norm_w, loss_target[0])
    g_in, g_out, g_s = _weights_exchange(hn_t, dproj_b, dwout_p, small_p)

    w_s = _pack_small(mix_norm_w, attn_out_norm_w, hgrn_out_norm_w, hgrn_lb_raw, final_norm_w)
    m_s = _pack_small(m_mix_norm_w, m_attn_out_norm_w, m_hgrn_out_norm_w, m_hgrn_lb_raw, m_final_norm_w)
    v_s = _pack_small(v_mix_norm_w, v_attn_out_norm_w, v_hgrn_out_norm_w, v_hgrn_lb_raw, v_final_norm_w)
    (d_in, d_out, d_s), (nm_in, nm_out, nm_s), (nv_in, nv_out, nv_s) = _adamw_update(
        (g_in, g_out, g_s), (w_in[0], w_out[0], w_s), (m_w_in[0], m_w_out[0], m_s), (v_w_in[0], v_w_out[0], v_s))

    loss = g_s[ROW_LOSS, 0]
    return (loss, grad_x[None], g_in[None], g_out[None], *_unpack_small(g_s),
            d_in[None], d_out[None], *_unpack_small(d_s),
            nm_in[None], nm_out[None], *_unpack_small(nm_s),
            nv_in[None], nv_out[None], *_unpack_small(nv_s))
```

```python
import functools

import jax
import jax.numpy as jnp
import numpy as np
from jax import lax
from jax.experimental import pallas as pl
from jax.experimental.pallas import tpu as pltpu

F32 = jnp.float32
BF16 = jnp.bfloat16

SEQ = 4096
D_MODEL = 1024
ATTN_WIDTH = 512
HGRN_WIDTH = 512
HEAD_DIM = 64
HGRN_HEADS = 4
HGRN_DIM = 128
HGRN_CHUNK = 64
N_CHUNKS = SEQ // HGRN_CHUNK
IN_COLS = 4096
COL_BLOCK = 512
N_DEV = 8
WOUT_ROWS = D_MODEL // N_DEV
ATTN_BLOCK = 128
DILATIONS = (1, 4, 16)
ROPE_THETA = 500000.0
ROPE_DIMS = 16
ROPE_HALF = 8
NORM_EPS = 1e-6
NEG_BIG = -1e30
LANES = 128

ADAM_LR = 0.001
ADAM_B1 = 0.9
ADAM_B2 = 0.999
ADAM_EPS = 1e-08
ADAM_WD = 0.01
ADAM_STEP = 10

SMALL_ROWS = 48
ROW_MIX, ROW_ATTN, ROW_HGRN, ROW_LB, ROW_FINAL, ROW_LOSS = 0, 8, 16, 24, 32, 40

VMEM_LIMIT = 56 * 1024 * 1024
MESH = pl.DeviceIdType.MESH


def _mm(a, b):
    return lax.dot_general(a, b, (((1,), (0,)), ((), ())), preferred_element_type=F32)


def _mm_nt(a, b):
    return lax.dot_general(a, b, (((1,), (1,)), ((), ())), preferred_element_type=F32)


def _mm_tn(a, b):
    return lax.dot_general(a, b, (((0,), (0,)), ((), ())), preferred_element_type=F32)


def _mm_exact(a, b):
    return lax.dot_general(a, b, (((1,), (0,)), ((), ())), preferred_element_type=F32,
                           precision=lax.Precision.HIGHEST)


def _sigmoid(v):
    return 1.0 / (1.0 + jnp.exp(-v))


def _params(sem=None, **kw):
    return pltpu.CompilerParams(dimension_semantics=sem, vmem_limit_bytes=VMEM_LIMIT, **kw)


def _my_place():
    return lax.axis_index("x"), lax.axis_index("y"), lax.axis_index("c")


def _peer(place, rel):
    x, y, c = place
    return (x ^ ((rel >> 2) & 1), y ^ ((rel >> 1) & 1), c ^ (rel & 1))


def _flat(place):
    x, y, c = place
    return 4 * x + 2 * y + c


ROPE_ROWS = 16


def _rope_tables(pos_row, inv_freq_col, selectors):
    def body(pos_ref, invf_ref, sel_ref, c_ref, sa_ref, sb_ref):
        ang = pos_ref[...].astype(F32) * invf_ref[...]
        cos, sin = jnp.cos(ang), jnp.sin(ang)

        def spread(v, sel):
            hi = v.astype(BF16)
            r1 = v - hi.astype(F32)
            mid = r1.astype(BF16)
            lo = (r1 - mid.astype(F32)).astype(BF16)
            return _mm_tn(hi, sel) + _mm_tn(mid, sel) + _mm_tn(lo, sel)

        e = lax.broadcasted_iota(jnp.int32, (1, LANES), 1) & (HEAD_DIM - 1)
        c_ref[...] = spread(cos, sel_ref[0]) + jnp.where(e < ROPE_DIMS, 0.0, 1.0)
        sa_ref[...] = spread(sin, sel_ref[1])
        sb_ref[...] = spread(sin, sel_ref[2])

    tab = jax.ShapeDtypeStruct((SEQ, LANES), F32)
    vm = lambda: pl.BlockSpec(memory_space=pltpu.VMEM)
    return pl.pallas_call(
        body, name="rope_tables", out_shape=(tab, tab, tab),
        in_specs=[vm(), vm(), vm()], out_specs=(vm(), vm(), vm()), compiler_params=_params(),
    )(pos_row, inv_freq_col, selectors)


def _per_slab(fn, t):
    return jnp.concatenate([fn(t[:, LANES * s:LANES * (s + 1)]) for s in range(t.shape[1] // LANES)], axis=1)


def _rot(t, c, sa, sb):
    return _per_slab(lambda u: u * c + pltpu.roll(u, ROPE_HALF, 1) * sa + pltpu.roll(u, LANES - ROPE_HALF, 1) * sb, t)


def _rot_transposed(g, c, sa, sb):
    return _per_slab(
        lambda u: u * c + pltpu.roll(u * sa, LANES - ROPE_HALF, 1) + pltpu.roll(u * sb, ROPE_HALF, 1), g)


def _gather_project(x, mix_w, w_in, w_out, rc, rsa, rsb):
    tm = 1024
    n_tiles = SEQ // tm
    arrival_of_step = (None, 0, 1, 2, 4, 5, 3, 6)

    def body(order_ref, x_ref, w_ref, win_ref, wout_ref, c_ref, sa_ref, sb_ref,
             proj_ref, hnt_ref, gin_hbm, gout_hbm, qkv_hbm,
             hn_s, w_land, wout_land, stage, sort_stage, send_sems, recv_sems, local_sems, sort_sems):
        g, i = pl.program_id(0), pl.program_id(1)
        me = _my_place()
        x_, y_, c_ = me
        sibling = (x_, y_, 1 - c_)
        chips = [(1 - x_, y_), (x_, 1 - y_), (1 - x_, 1 - y_)]

        def slab(which, place):
            idx = _flat(place)
            if which == 0:
                return w_land.at[idx]
            return wout_land.at[pl.ds(pl.multiple_of(idx * WOUT_ROWS, WOUT_ROWS), WOUT_ROWS), :]

        def remote(which, k, ref, to, src=None):
            return pltpu.make_async_remote_copy(
                src_ref=ref if src is None else src, dst_ref=ref, send_sem=send_sems.at[8 * which + k],
                recv_sem=recv_sems.at[8 * which + k], device_id=to, device_id_type=MESH)

        def copy(which, k, block, to, src=None):
            return remote(which, k, slab(which, block), to, src)

        def half(which, place, part):
            n = (D_MODEL if which == 0 else WOUT_ROWS) // 2
            if which == 0:
                return w_land.at[_flat(place), pl.ds(n * part, n), :]
            return wout_land.at[pl.ds(pl.multiple_of(_flat(place) * WOUT_ROWS + n * part, n), n), :]

        def first_copies(which):
            src = stage if which == 0 else None
            return ([copy(which, 0, me, sibling, src)]
                    + [copy(which, 1 + j, me, (*chips[j], c_), src) for j in range(2)])

        def relay(which, part):
            frm, to = (chips[1], chips[0]) if part == 0 else (chips[0], chips[1])
            return remote(which, 3 if part == 0 else 7, half(which, (*frm, c_), part), (*to, c_))

        def two_hop_half(which, part):
            return remote(which, 3 if part == 0 else 7, half(which, (*chips[2], c_), part), me)

        def pass_on(which, j):
            return copy(which, 4 + j, (*chips[j], c_), sibling)

        def arrival(which, k):
            if k == 0:
                return copy(which, 0, sibling, me)
            if k <= 2:
                return copy(which, k, (*chips[k - 1], c_), me)
            return copy(which, k, (*chips[k - 4], 1 - c_), me)

        def to_hbm(step):
            idx = order_ref[step]
            cols = pl.ds(pl.multiple_of(idx * COL_BLOCK, COL_BLOCK), COL_BLOCK)
            return pltpu.make_async_copy(w_land.at[idx], gin_hbm.at[:, cols], local_sems.at[step])

        @pl.when((g == 0) & (i == 0))
        def _():
            stage[...] = win_ref[...].astype(BF16)
            w_land[_flat(me)] = stage[...]
            wout_land[pl.ds(pl.multiple_of(_flat(me) * WOUT_ROWS, WOUT_ROWS), WOUT_ROWS), :] = (
                wout_ref[...].astype(BF16))
            for cp in first_copies(0) + first_copies(1)[:1]:
                cp.start()
            to_hbm(0).start()

        for step, k in enumerate(arrival_of_step):
            if k is None:
                continue

            @pl.when((g == step) & (i == 0))
            def _(k=k, step=step):
                if k == 3:
                    two_hop_half(0, 0).wait_recv()
                    two_hop_half(0, 1).wait_recv()
                else:
                    arrival(0, k).wait_recv()
                to_hbm(step).start()
                if 1 <= k <= 3:
                    pass_on(0, k - 1).start()
                if k == 1:
                    relay(0, 1).start()
                    for cp in first_copies(1)[1:]:
                        cp.start()
                if k == 2:
                    relay(0, 0).start()
                if k in (4, 5):
                    arrival(1, k - 3).wait_recv()
                    relay(1, 5 - k).start()

        rows = pl.ds(pl.multiple_of(i * tm, tm), tm)

        @pl.when(g == 0)
        def _():
            xf = x_ref[...]
            ms = jnp.mean(xf * xf, axis=-1, keepdims=True)
            hn = xf * lax.rsqrt(ms + NORM_EPS) * w_ref[...]
            hnt_ref[...] = hn.T.astype(BF16)
            hn_s[rows, :] = hn.astype(BF16)

        group = order_ref[g]

        def sorted_copy(tile_value):
            per = tm // SORT_RESIDUES
            cols = pl.ds(pl.multiple_of(group * COL_BLOCK, COL_BLOCK), COL_BLOCK)
            buf = i % 2

            def out_copies(tile, b):
                return [pltpu.make_async_copy(
                    sort_stage.at[b, :, r, :], qkv_hbm.at[r, pl.ds(pl.multiple_of(tile * per, per), per), cols],
                    sort_sems.at[b, r]) for r in range(SORT_RESIDUES)]

            @pl.when(i >= 2)
            def _():
                for copy in out_copies(i - 2, buf):
                    copy.wait()

            sort_stage[buf] = tile_value.reshape(per, SORT_RESIDUES, COL_BLOCK)
            for copy in out_copies(i, buf):
                copy.start()

            @pl.when(i == n_tiles - 1)
            def _():
                for copy in out_copies(i - 1, 1 - buf) + out_copies(i, buf):
                    copy.wait()

        @pl.when(group < 2)
        def _():
            rotated = _rot(_mm(hn_s[rows, :], w_land[group]), c_ref[...], sa_ref[...], sb_ref[...])
            proj_ref[...] = rotated
            sorted_copy(rotated)

        @pl.when(group == 2)
        def _():
            value = _mm(hn_s[rows, :], w_land[group])
            proj_ref[...] = value
            sorted_copy(value)

        @pl.when(group > 2)
        def _():
            proj_ref[...] = _mm(hn_s[rows, :], w_land[group])

        @pl.when((g == N_DEV - 1) & (i == n_tiles - 1))
        def _():
            pass_on(1, 0).start()
            pass_on(1, 1).start()
            two_hop_half(1, 0).wait_recv()
            two_hop_half(1, 1).wait_recv()
            pass_on(1, 2).start()
            for k in (0, 4, 5, 6):
                arrival(1, k).wait_recv()
            for which in (0, 1):
                for cp in (first_copies(which) + [relay(which, part) for part in range(2)]
                           + [pass_on(which, j) for j in range(3)]):
                    cp.wait_send()
            wout_copy = pltpu.make_async_copy(wout_land, gout_hbm, local_sems.at[N_DEV])
            wout_copy.start()
            for step in range(N_DEV):
                to_hbm(step).wait()
            wout_copy.wait()

    me = _my_place()
    x_, y_, c_ = me
    chips = [(1 - x_, y_), (x_, 1 - y_), (1 - x_, 1 - y_)]
    order = jnp.stack([_flat(p) for p in (
        me, (x_, y_, 1 - c_), (*chips[0], c_), (*chips[1], c_), (*chips[0], 1 - c_), (*chips[1], 1 - c_),
        (*chips[2], c_), (*chips[2], 1 - c_))]).astype(jnp.int32)

    first_sweep = lambda g, i, order: (jnp.where(g == 0, i, n_tiles - 1), 0)
    tab = pl.BlockSpec((tm, LANES), lambda g, i, order: (jnp.where(order[g] < 2, i, 0), 0))
    whole = lambda: pl.BlockSpec(memory_space=pltpu.VMEM)
    grid_spec = pltpu.PrefetchScalarGridSpec(
        num_scalar_prefetch=1, grid=(N_DEV, n_tiles),
        in_specs=[pl.BlockSpec((tm, D_MODEL), first_sweep),
                  pl.BlockSpec((1, D_MODEL), lambda g, i, order: (0, 0)),
                  whole(), whole(), tab, tab, tab],
        out_specs=(pl.BlockSpec((None, tm, COL_BLOCK), lambda g, i, order: (order[g], i, 0)),
                   pl.BlockSpec((D_MODEL, tm), lambda g, i, order: (0, jnp.where(g == 0, i, n_tiles - 1))),
                   pl.BlockSpec(memory_space=pl.ANY), pl.BlockSpec(memory_space=pl.ANY),
                   pl.BlockSpec(memory_space=pl.ANY)),
        scratch_shapes=[pltpu.VMEM((SEQ, D_MODEL), BF16),
                        pltpu.VMEM((N_DEV, D_MODEL, COL_BLOCK), BF16),
                        pltpu.VMEM((D_MODEL, D_MODEL), BF16),
                        pltpu.VMEM((D_MODEL, COL_BLOCK), BF16),
                        pltpu.VMEM((2, tm // SORT_RESIDUES, SORT_RESIDUES, COL_BLOCK), F32),
                        pltpu.SemaphoreType.DMA((16,)), pltpu.SemaphoreType.DMA((16,)),
                        pltpu.SemaphoreType.DMA((N_DEV + 1,)), pltpu.SemaphoreType.DMA((2, SORT_RESIDUES))])
    proj, hn_t, w_in_g, w_out_g, qkv_sorted = pl.pallas_call(
        body, name="gather_project", grid_spec=grid_spec,
        out_shape=(jax.ShapeDtypeStruct((N_DEV, SEQ, COL_BLOCK), F32), jax.ShapeDtypeStruct((D_MODEL, SEQ), BF16),
                   jax.ShapeDtypeStruct((D_MODEL, IN_COLS), BF16), jax.ShapeDtypeStruct((D_MODEL, D_MODEL), BF16),
                   jax.ShapeDtypeStruct((SORT_RESIDUES, SORT_ROWS, 3 * COL_BLOCK), F32)),
        compiler_params=_params(("arbitrary", "arbitrary")),
    )(order, x, mix_w, w_in, w_out, rc, rsa, rsb)
    return proj, hn_t, w_in_g, w_out_g, qkv_sorted.reshape(SEQ, 3 * COL_BLOCK)


SCORE_SCALE = HEAD_DIM ** -0.5
ATTN_GROUP_FWD = 16
ATTN_GROUP_BWD = 8
BLOCKS_PER_PATTERN = SEQ // ATTN_BLOCK
SORT_RESIDUES = 16
SORT_ROWS = SEQ // SORT_RESIDUES


def _write_band_bias(bias_ref):
    row = lax.broadcasted_iota(jnp.int32, (2 * ATTN_BLOCK, 2 * ATTN_BLOCK), 0) & (ATTN_BLOCK - 1)
    col = lax.broadcasted_iota(jnp.int32, (2 * ATTN_BLOCK, 2 * ATTN_BLOCK), 1)
    for pi, d in enumerate(DILATIONS):
        per = SORT_RESIDUES // d
        ahead = per * (row % (8 * d) - col % (16 * d)) + (row // (8 * d) - col // (16 * d))
        dist = ATTN_BLOCK + ahead
        bias_ref[2 * pi] = jnp.where((dist >= 0) & (dist <= ATTN_BLOCK), 0.0, NEG_BIG)
        bias_ref[2 * pi + 1] = jnp.where(ahead >= 0, 0.0, NEG_BIG)


def _head0_lanes():
    return lax.broadcasted_iota(jnp.int32, (ATTN_BLOCK, LANES), 1) < HEAD_DIM


def _stack_heads(t, h0):
    return jnp.concatenate([jnp.where(h0, t, 0.0), jnp.where(h0, 0.0, t)], axis=0).astype(BF16)


def _block_runs(i, d):
    nblk = BLOCKS_PER_PATTERN // d
    r, n = i // nblk, i % nblk
    kn = jnp.maximum(n - 1, 0)
    rows, keys = [], []
    for c in range(SORT_RESIDUES // d):
        base = SORT_ROWS * (c * d + r)
        rows.append(pl.ds(pl.multiple_of(base + 8 * d * n, 8), 8 * d))
        keys.append(pl.ds(pl.multiple_of(base + 8 * d * kn, 8), 16 * d))
    return rows, keys, (n == 0).astype(jnp.int32)


def _take(ref, runs):
    return jnp.concatenate([ref[run, :] for run in runs], axis=0)


def _put(ref, runs, value, add=False):
    at = 0
    for run in runs:
        piece = value[at:at + run.size]
        if add:
            ref[run, :] += piece
        else:
            ref[run, :] = piece
        at += run.size


def _sort_copies(src_hbm, lane_block, dst_ref, sem_ref):
    lanes = pl.ds(pl.multiple_of(LANES * lane_block, LANES), LANES)
    return [pltpu.make_async_copy(src_hbm.at[:, r, lanes], dst_ref.at[pl.ds(SORT_ROWS * r, SORT_ROWS), :],
                                  sem_ref.at[r]) for r in range(SORT_RESIDUES)]


def _unsort_copies(src_ref, dst_hbm, lane_block, sem_ref):
    lanes = pl.ds(pl.multiple_of(LANES * lane_block, LANES), LANES)
    return [pltpu.make_async_copy(src_ref.at[pl.ds(SORT_ROWS * r, SORT_ROWS), :], dst_hbm.at[:, r, lanes],
                                  sem_ref.at[r]) for r in range(SORT_RESIDUES)]


def _for_each_group(d, n_group, load, compute, store):
    def group(g, carry):
        items = [load(*_block_runs(g * n_group + u, d)) for u in range(n_group)]
        results = [compute(item) for item in items]
        for item, res in zip(items, results):
            store(item, res)
        return carry

    lax.fori_loop(0, BLOCKS_PER_PATTERN // n_group, group, 0)


def _attn_fwd_fused(qkv_sorted):
    n_pat = len(DILATIONS)
    tile2 = (2 * ATTN_BLOCK, LANES)

    def body(q_ref, k_ref, v_ref, o_hbm, lse_ref, o_slots, m_acc, l_acc, bias_ref, out_sem):
        step, n_steps = pl.program_id(0), pl.num_programs(0)
        pl.when(step == 0)(lambda: _write_band_bias(bias_ref))
        slot = step % 2
        o_acc = o_slots.at[slot]
        h0 = _head0_lanes()
        for pi, d in enumerate(DILATIONS):
            first, last = pi == 0, pi == n_pat - 1

            def load(rows, keys, which, first=first, pi=pi):
                item = dict(rows=rows, keys=keys, which=2 * pi + which)
                if not first:
                    item.update(o=_take(o_acc, rows), m=[_take(m_acc.at[h], rows) for h in range(2)],
                                l=[_take(l_acc.at[h], rows) for h in range(2)])
                return item

            def compute(item, first=first):
                kb = _take(k_ref, item["keys"]).astype(BF16)
                vb = _take(v_ref, item["keys"]).astype(BF16)
                s = _mm_nt(_stack_heads(_take(q_ref, item["rows"]) * SCORE_SCALE, h0), kb) + bias_ref[item["which"]]
                mb = jnp.max(s, axis=-1, keepdims=True)
                if first:
                    p = jnp.exp(s - mb)
                    mn = jnp.broadcast_to(mb, tile2)
                else:
                    m_old = jnp.concatenate(item["m"], axis=0)
                    mn = jnp.maximum(m_old, mb)
                    alpha = jnp.exp(m_old - mn)
                    p = jnp.exp(s - jnp.concatenate([mn, mn], axis=1))
                ls = jnp.sum(p, axis=-1, keepdims=True)
                pv = _mm(p.astype(BF16), vb)
                if first:
                    return pv, mn, jnp.broadcast_to(ls, tile2)
                o_old = jnp.concatenate([item["o"], item["o"]], axis=0)
                return alpha * o_old + pv, mn, alpha * jnp.concatenate(item["l"], axis=0) + ls

            def store(item, res, last=last):
                rows = item["rows"]
                (o0, o1), (m0, m1), (l0, l1) = ((a[:ATTN_BLOCK], a[ATTN_BLOCK:]) for a in res)
                if last:
                    _put(o_acc, rows, jnp.where(h0, o0 / l0, o1 / l1))
                    _put(lse_ref, rows, jnp.where(h0, m0 + jnp.log(l0), m1 + jnp.log(l1)))
                else:
                    _put(o_acc, rows, jnp.where(h0, o0, o1))
                    for h, (m, l) in enumerate(((m0, l0), (m1, l1))):
                        _put(m_acc.at[h], rows, m)
                        _put(l_acc.at[h], rows, l)

            _for_each_group(d, ATTN_GROUP_FWD, load, compute, store)

        def copies_out(of_step):
            return _unsort_copies(o_slots.at[of_step % 2], o_hbm, of_step, out_sem.at[of_step % 2])

        @pl.when(step > 0)
        def _():
            for copy in copies_out(step - 1):
                copy.wait()

        for copy in copies_out(step):
            copy.start()

        @pl.when(step == n_steps - 1)
        def _():
            for copy in copies_out(step):
                copy.wait()

    slab = lambda g: pl.BlockSpec((SEQ, LANES), functools.partial(lambda hp, g: (0, 4 * g + hp), g=g))
    wide = jax.ShapeDtypeStruct((SEQ, ATTN_WIDTH), F32)
    o_rows, lse = pl.pallas_call(
        body, name="attn_fwd", grid=(4,),
        out_shape=(jax.ShapeDtypeStruct((SORT_ROWS, SORT_RESIDUES, ATTN_WIDTH), F32), wide),
        in_specs=[slab(0), slab(1), slab(2)], out_specs=(pl.BlockSpec(memory_space=pl.ANY), slab(0)),
        scratch_shapes=[pltpu.VMEM((2, SEQ, LANES), F32), pltpu.VMEM((2, SEQ, LANES), F32),
                        pltpu.VMEM((2, SEQ, LANES), F32),
                        pltpu.VMEM((2 * len(DILATIONS), 2 * ATTN_BLOCK, 2 * ATTN_BLOCK), F32),
                        pltpu.SemaphoreType.DMA((2, SORT_RESIDUES))],
        compiler_params=_params(("arbitrary",)),
    )(qkv_sorted, qkv_sorted, qkv_sorted)
    return o_rows.reshape(SEQ, ATTN_WIDTH), lse


def _attn_bwd_fused(qkv_sorted, d_out, lse_sorted, delta):
    def body(q_ref, k_ref, v_ref, do_hbm, lse_ref, del_hbm, dq_hbm, dk_hbm, dv_hbm,
             in_slots, out_slots, bias_ref, in_sem, out_sem):
        step, n_steps = pl.program_id(0), pl.num_programs(0)
        slot = step % 2

        def copies_in(of_step):
            s = of_step % 2
            return [copy for j, hbm in enumerate((do_hbm, del_hbm))
                    for copy in _sort_copies(hbm, of_step, in_slots.at[s, j], in_sem.at[s, j])]

        def copies_out(of_step):
            s = of_step % 2
            return [copy for j, hbm in enumerate((dq_hbm, dk_hbm, dv_hbm))
                    for copy in _unsort_copies(out_slots.at[s, j], hbm, of_step, out_sem.at[s, j])]

        @pl.when(step == 0)
        def _():
            for copy in copies_in(step):
                copy.start()
            _write_band_bias(bias_ref)

        @pl.when(step + 1 < n_steps)
        def _():
            for copy in copies_in(step + 1):
                copy.start()

        do_s, del_s = in_slots.at[slot, 0], in_slots.at[slot, 1]
        dq_s, dk_s, dv_s = (out_slots.at[slot, j] for j in range(3))
        dk_s[...] = jnp.zeros_like(dk_s)
        dv_s[...] = jnp.zeros_like(dv_s)
        for copy in copies_in(step):
            copy.wait()
        h0 = _head0_lanes()
        for pi, d in enumerate(DILATIONS):
            first = pi == 0

            def load(rows, keys, which, pi=pi):
                return dict(rows=rows, keys=keys, q=_take(q_ref, rows), g=_take(do_s, rows),
                            lse=_take(lse_ref, rows), delta=_take(del_s, rows),
                            k=_take(k_ref, keys).astype(BF16), v=_take(v_ref, keys).astype(BF16),
                            bias=bias_ref[2 * pi + which])

            def per_head(t):
                swapped = pltpu.roll(t, HEAD_DIM, 1)
                both = jnp.concatenate([jnp.where(h0, t, swapped), jnp.where(h0, swapped, t)], axis=0)
                return jnp.concatenate([both, both], axis=1)

            def compute(item):
                q2, g2 = _stack_heads(item["q"] * SCORE_SCALE, h0), _stack_heads(item["g"], h0)
                s = _mm_nt(q2, item["k"]) + item["bias"]
                p = jnp.exp(s - per_head(item["lse"]))
                dp = _mm_nt(g2, item["v"])
                ds = (p * (dp - per_head(item["delta"]))).astype(BF16)
                dq2 = _mm(ds, item["k"])
                dq = jnp.where(h0, dq2[:ATTN_BLOCK], dq2[ATTN_BLOCK:]) * SCORE_SCALE
                return dq, _mm_tn(ds, q2), _mm_tn(p.astype(BF16), g2)

            def store(item, res, first=first):
                _put(dq_s, item["rows"], res[0], add=not first)
                _put(dk_s, item["keys"], res[1], add=True)
                _put(dv_s, item["keys"], res[2], add=True)

            _for_each_group(d, ATTN_GROUP_BWD, load, compute, store)

        @pl.when(step > 0)
        def _():
            for copy in copies_out(step - 1):
                copy.wait()

        for copy in copies_out(step):
            copy.start()

        @pl.when(step == n_steps - 1)
        def _():
            for copy in copies_out(step):
                copy.wait()

    slab = lambda g: pl.BlockSpec((SEQ, LANES), functools.partial(lambda hp, g: (0, 4 * g + hp), g=g))
    anywhere = pl.BlockSpec(memory_space=pl.ANY)
    by_residue = (SORT_ROWS, SORT_RESIDUES, ATTN_WIDTH)
    grads = pl.pallas_call(
        body, name="attn_bwd", grid=(4,), out_shape=(jax.ShapeDtypeStruct(by_residue, F32),) * 3,
        scratch_shapes=[pltpu.VMEM((2, 2, SEQ, LANES), F32), pltpu.VMEM((2, 3, SEQ, LANES), F32),
                        pltpu.VMEM((2 * len(DILATIONS), 2 * ATTN_BLOCK, 2 * ATTN_BLOCK), F32),
                        pltpu.SemaphoreType.DMA((2, 2, SORT_RESIDUES)), pltpu.SemaphoreType.DMA((2, 3, SORT_RESIDUES))],
        in_specs=[slab(0), slab(1), slab(2), anywhere, slab(0), anywhere], out_specs=(anywhere,) * 3,
        compiler_params=_params(("arbitrary",)),
    )(qkv_sorted, qkv_sorted, qkv_sorted, d_out.reshape(by_residue), lse_sorted, delta.reshape(by_residue))
    return tuple(g.reshape(SEQ, ATTN_WIDTH) for g in grads)


def _hgrn_lower_bound(lb_ref):
    r0, r1 = lb_ref[0:1, :], lb_ref[1:2, :]
    mx = jnp.maximum(r0, r1)
    e0, e1 = jnp.exp(r0 - mx), jnp.exp(r1 - mx)
    return e0 / (e0 + e1)


def _hgrn_gates(hq, hf, lb):
    sq = _sigmoid(hq)
    sg = _sigmoid(hf)
    f = lb + (1.0 - lb) * sg
    return hq * sq, sq, sg, f, 1.0 - f, jnp.log(f)


HGRN_PAIR = 4
HGRN_SEQ_BLOCK = 1024
HGRN_GROUP = 4
HGRN_ROWS = HGRN_GROUP * HGRN_CHUNK


def _hgrn_specs(reverse):
    n_blocks = SEQ // HGRN_SEQ_BLOCK
    width = HGRN_PAIR * HGRN_DIM
    blk = (lambda s: n_blocks - 1 - s) if reverse else (lambda s: s)
    cols = lambda g: pl.BlockSpec((None, HGRN_SEQ_BLOCK, width), functools.partial(lambda p, s, g: (g, blk(s), p), g=g))
    pair = pl.BlockSpec((HGRN_SEQ_BLOCK, width), lambda p, s: (blk(s), p))
    lb = pl.BlockSpec((2, width), lambda p, s: (0, p))
    states = pl.BlockSpec((HGRN_PAIR, HGRN_SEQ_BLOCK // HGRN_CHUNK, HGRN_DIM, HGRN_DIM),
                          lambda p, s: (p, blk(s), 0, 0))
    return cols, pair, lb, states


def _chunk_masks():
    ri = lax.broadcasted_iota(jnp.int32, (HGRN_ROWS, HGRN_ROWS), 0)
    ci = lax.broadcasted_iota(jnp.int32, (HGRN_ROWS, HGRN_ROWS), 1)
    same = (ri // HGRN_CHUNK) == (ci // HGRN_CHUNK)
    return same, same & (ri >= ci), same & (ri <= ci)


def _mm_select(sel, v):
    hi = v.astype(BF16)
    r1 = v - hi.astype(F32)
    mid = r1.astype(BF16)
    lo = (r1 - mid.astype(F32)).astype(BF16)
    return _mm(sel, hi) + _mm(sel, mid) + _mm(sel, lo)


def _head_cols(a, h):
    return a[:, HGRN_DIM * h:HGRN_DIM * (h + 1)]


def _hgrn_fwd(proj, lb_raw):
    t, rws = HGRN_CHUNK, HGRN_ROWS

    def body(hq_ref, hf_ref, hi_ref, lb_ref, rec_ref, st_ref, state):
        @pl.when(pl.program_id(1) == 0)
        def _():
            state[...] = jnp.zeros_like(state)

        lb = _hgrn_lower_bound(lb_ref)
        same, causal, _ = _chunk_masks()
        sel = jnp.concatenate([causal, same], axis=0).astype(BF16)

        def group(g, sts):
            rows = pl.ds(pl.multiple_of(g * rws, rws), rws)
            q, _, _, _, k, lf = _hgrn_gates(hq_ref[rows, :], hf_ref[rows, :], lb)
            sums = _mm_select(sel, lf)
            cum, last = sums[:rws], sums[rws:]
            qd = (q * jnp.exp(cum)).astype(BF16)
            ki = (k * jnp.exp(-cum)).astype(BF16)
            ke = (k * jnp.exp(last - cum)).astype(BF16)
            vb = hi_ref[rows, :].astype(BF16)
            dec = jnp.exp(last)
            new_sts, recs = [], []
            for h in range(HGRN_PAIR):
                qd_h, ke_h, vb_h = _head_cols(qd, h), _head_cols(ke, h), _head_cols(vb, h)
                att = jnp.where(causal, _mm_nt(qd_h, _head_cols(ki, h)), 0.0).astype(BF16)
                intra = _mm(att, vb_h)
                st = sts[h]
                outs = []
                for c in range(HGRN_GROUP):
                    sl = slice(c * t, (c + 1) * t)
                    st_ref[h, g * HGRN_GROUP + c] = st
                    outs.append(intra[sl] + _mm_nt(qd_h[sl], st.astype(BF16)))
                    st = st * _head_cols(dec[c * t:c * t + 1, :], h) + _mm_tn(vb_h[sl], ke_h[sl])
                new_sts.append(st)
                recs.append(jnp.concatenate(outs, axis=0))
            rec_ref[rows, :] = jnp.concatenate(recs, axis=1)
            return tuple(new_sts)

        sts = lax.fori_loop(0, HGRN_SEQ_BLOCK // rws, group, tuple(state[h] for h in range(HGRN_PAIR)))
        for h in range(HGRN_PAIR):
            state[h] = sts[h]

    cols, pair, lb, states = _hgrn_specs(reverse=False)
    return pl.pallas_call(
        body, name="hgrn_fwd", grid=(HGRN_HEADS // HGRN_PAIR, SEQ // HGRN_SEQ_BLOCK),
        out_shape=(jax.ShapeDtypeStruct((SEQ, HGRN_WIDTH), F32),
                   jax.ShapeDtypeStruct((HGRN_HEADS, N_CHUNKS, HGRN_DIM, HGRN_DIM), F32)),
        in_specs=[cols(4), cols(5), cols(6), lb], out_specs=(pair, states),
        scratch_shapes=[pltpu.VMEM((HGRN_PAIR, HGRN_DIM, HGRN_DIM), F32)],
        compiler_params=_params(("parallel", "arbitrary")),
    )(proj, proj, proj, lb_raw)


def _hgrn_bwd(proj, lb_raw, d_rec, states):
    t, rws = HGRN_CHUNK, HGRN_ROWS

    def body(hq_ref, hf_ref, hi_ref, lb_ref, do_ref, st_ref, dhq_ref, dhf_ref, dhi_ref, dlb_ref,
             dstate, dlb_acc):
        lb = _hgrn_lower_bound(lb_ref)
        same, causal, anti = _chunk_masks()
        sel = jnp.concatenate([causal, same], axis=0).astype(BF16)
        sel_t = jnp.concatenate([anti, same], axis=1).astype(BF16)
        @pl.when(pl.program_id(1) == 0)
        def _():
            dstate[...] = jnp.zeros_like(dstate)
            dlb_acc[...] = jnp.zeros_like(dlb_acc)

        n_groups = HGRN_SEQ_BLOCK // rws
        chunks = [slice(c * t, (c + 1) * t) for c in range(HGRN_GROUP)]

        def group(i, dsts_in):
            g = n_groups - 1 - i
            rows = pl.ds(pl.multiple_of(g * rws, rws), rws)
            hq = hq_ref[rows, :]
            q, sq, sg, f, k, lf = _hgrn_gates(hq, hf_ref[rows, :], lb)
            sums = _mm_select(sel, lf)
            cum, last = sums[:rws], sums[rws:]
            e_cum, e_inv, e_end, dec = jnp.exp(cum), jnp.exp(-cum), jnp.exp(last - cum), jnp.exp(last)
            qd, ki, ke = q * e_cum, k * e_inv, k * e_end
            qdb, kib, keb = qd.astype(BF16), ki.astype(BF16), ke.astype(BF16)
            vb = hi_ref[rows, :].astype(BF16)
            gb = do_ref[rows, :].astype(BF16)

            dsts_out, per_head = [], []
            for h in range(HGRN_PAIR):
                qdb_h, kib_h, keb_h = _head_cols(qdb, h), _head_cols(kib, h), _head_cols(keb, h)
                vb_h, gb_h = _head_cols(vb, h), _head_cols(gb, h)
                att = jnp.where(causal, _mm_nt(qdb_h, kib_h), 0.0).astype(BF16)
                datt = jnp.where(causal, _mm_nt(gb_h, vb_h), 0.0).astype(BF16)
                dv = _mm_tn(att, gb_h)
                dqd = _mm(datt, kib_h)
                dki = _mm_tn(datt, qdb_h)

                decs = [_head_cols(dec[c * t:c * t + 1, :], h) for c in range(HGRN_GROUP)]
                dsts = [None] * HGRN_GROUP
                dst = dsts_in[h]
                for c in reversed(range(HGRN_GROUP)):
                    dsts[c] = dst
                    dst = dst * decs[c] + _mm_tn(gb_h[chunks[c]], qdb_h[chunks[c]])
                dsts_out.append(dst)

                dv_x, dqd_x, dke, dlast_x = [], [], [], []
                for c, sl in enumerate(chunks):
                    st_prev = st_ref[h, g * HGRN_GROUP + c]
                    dstb = dsts[c].astype(BF16)
                    dv_x.append(_mm_nt(keb_h[sl], dstb))
                    dqd_x.append(_mm(gb_h[sl], st_prev.astype(BF16)))
                    dke.append(_mm(vb_h[sl], dstb))
                    ddec = jnp.sum(dsts[c] * st_prev, axis=0, keepdims=True)
                    dlast_x.append(jnp.broadcast_to(ddec * decs[c], (t, HGRN_DIM)))
                per_head.append((dv + jnp.concatenate(dv_x, axis=0), dqd + jnp.concatenate(dqd_x, axis=0),
                                 dki, jnp.concatenate(dke, axis=0), jnp.concatenate(dlast_x, axis=0)))
            dv, dqd, dki, dke, dlast = (jnp.concatenate(list(parts), axis=1) for parts in zip(*per_head))

            dq = dqd * e_cum
            dk = dki * e_inv + dke * e_end
            dke_ke = dke * ke
            dcum = dqd * qd - dki * ki - dke_ke
            dlf = _mm_select(sel_t, jnp.concatenate([dcum, dke_ke], axis=0)) + dlast
            df = dlf / f - dk
            dhq_ref[rows, :] = dq * (sq * (1.0 + hq * (1.0 - sq)))
            dhf_ref[rows, :] = df * (1.0 - lb) * (sg * (1.0 - sg))
            dhi_ref[rows, :] = dv
            dlb_acc[...] += jnp.sum(df * (1.0 - sg), axis=0, keepdims=True)
            return tuple(dsts_out)

        dsts = lax.fori_loop(0, n_groups, group, tuple(dstate[h] for h in range(HGRN_PAIR)))
        for h in range(HGRN_PAIR):
            dstate[h] = dsts[h]
        g0 = dlb_acc[...] * lb * (1.0 - lb)
        dlb_ref[...] = jnp.concatenate([g0, -g0], axis=0)

    cols, pair, lb_spec, st_spec = _hgrn_specs(reverse=True)
    wide = jax.ShapeDtypeStruct((SEQ, HGRN_WIDTH), F32)
    return pl.pallas_call(
        body, name="hgrn_bwd", grid=(HGRN_HEADS // HGRN_PAIR, SEQ // HGRN_SEQ_BLOCK),
        out_shape=(wide, wide, wide, jax.ShapeDtypeStruct((2, HGRN_WIDTH), F32)),
        in_specs=[cols(4), cols(5), cols(6), lb_spec, pair, st_spec],
        out_specs=(pair, pair, pair, lb_spec),
        scratch_shapes=[pltpu.VMEM((HGRN_PAIR, HGRN_DIM, HGRN_DIM), F32),
                        pltpu.VMEM((1, HGRN_PAIR * HGRN_DIM), F32)],
        compiler_params=_params(("parallel", "arbitrary")),
    )(proj, proj, proj, lb_raw, d_rec, states)


def _group_sum(v, group):
    parts = []
    for s in range(v.shape[1] // LANES):
        slab = v[:, LANES * s:LANES * (s + 1)]
        if group == LANES:
            parts.append(jnp.broadcast_to(jnp.sum(slab, axis=-1, keepdims=True), slab.shape))
        else:
            h0 = lax.broadcasted_iota(jnp.int32, slab.shape, 1) < HEAD_DIM
            s0 = jnp.sum(jnp.where(h0, slab, 0.0), axis=-1, keepdims=True)
            s1 = jnp.sum(jnp.where(h0, 0.0, slab), axis=-1, keepdims=True)
            parts.append(jnp.where(h0, s0, s1))
    return jnp.concatenate(parts, axis=1)


def _mid(attn_o, rec, proj, x, target, w_out_g, attn_w, hgrn_w, final_w):
    tm = 256

    def branch_fwd(o, gate, w, group):
        r = lax.rsqrt(_group_sum(o * o, group) * (1.0 / group) + NORM_EPS)
        nrm = o * r
        sg = _sigmoid(gate)
        return r, nrm, sg, nrm * w * (gate * sg)

    def branch_bwd(dy, r, nrm, sg, gate, w, group):
        silu = gate * sg
        d_gate = dy * nrm * w * (sg * (1.0 + gate * (1.0 - sg)))
        d_w = jnp.sum(dy * nrm * silu, axis=0, keepdims=True)
        dn = dy * w * silu
        d_o = r * (dn - nrm * (_group_sum(dn * nrm, group) * (1.0 / group)))
        return d_o, d_gate, d_w

    def body(o_ref, rec_ref, ag_ref, hg_ref, x_ref, tgt_ref, wout_ref, aw_ref, hw_ref, fw_ref,
             dx2_ref, do_ref, delta_ref, dag_ref, drec_ref, dhg_ref, dwout_ref, dfw_ref, daw_ref, dhw_ref,
             loss_ref, dwout_acc):
        i = pl.program_id(0)

        @pl.when(i == 0)
        def _():
            dwout_acc[...] = jnp.zeros_like(dwout_acc)
            dfw_ref[...] = jnp.zeros_like(dfw_ref)
            daw_ref[...] = jnp.zeros_like(daw_ref)
            dhw_ref[...] = jnp.zeros_like(dhw_ref)
            loss_ref[...] = jnp.zeros_like(loss_ref)

        o, rc, ag, hg = o_ref[...], rec_ref[...], ag_ref[...], hg_ref[...]
        aw, hw, fw = aw_ref[...], hw_ref[...], fw_ref[...]
        ra, na, sga, ya = branch_fwd(o, ag, aw, HEAD_DIM)
        rh, nh, sgh, yh = branch_fwd(rc, hg, hw, HGRN_DIM)
        mixed = jnp.concatenate([ya, yh], axis=1).astype(BF16)
        wout = wout_ref[...]
        x2 = x_ref[...] + _mm(mixed, wout)
        rstd = lax.rsqrt(jnp.mean(x2 * x2, axis=-1, keepdims=True) + NORM_EPS)
        xn = x2 * rstd
        err = xn * fw - tgt_ref[...]
        row_loss = jnp.mean(err * err, axis=-1, keepdims=True)
        loss_ref[...] += 0.5 * jnp.sum(row_loss, axis=0, keepdims=True)
        dy = err * (1.0 / D_MODEL)
        dfw_ref[...] += jnp.sum(dy * xn, axis=0, keepdims=True)
        dxn = dy * fw
        dx2 = rstd * (dxn - xn * jnp.mean(dxn * xn, axis=-1, keepdims=True))
        dx2_ref[...] = dx2
        dx2b = dx2.astype(BF16)
        dwout_acc[...] += _mm_tn(mixed, dx2b)

        @pl.when(i == pl.num_programs(0) - 1)
        def _():
            dwout_ref[...] = dwout_acc[...].astype(BF16)

        dmixed = _mm_nt(dx2b, wout)

        d_o, d_ag, d_aw = branch_bwd(dmixed[:, :ATTN_WIDTH], ra, na, sga, ag, aw, HEAD_DIM)
        d_rec, d_hg, d_hw = branch_bwd(dmixed[:, ATTN_WIDTH:], rh, nh, sgh, hg, hw, HGRN_DIM)
        do_ref[...] = d_o
        delta_ref[...] = _group_sum(d_o * o, HEAD_DIM)
        dag_ref[...] = d_ag
        drec_ref[...] = d_rec
        dhg_ref[...] = d_hg
        daw_ref[...] += d_aw
        dhw_ref[...] += d_hw

    half = lambda: pl.BlockSpec((tm, COL_BLOCK), lambda i: (i, 0))
    full = lambda: pl.BlockSpec((tm, D_MODEL), lambda i: (i, 0))
    fixed = lambda r, c: pl.BlockSpec((r, c), lambda i: (0, 0))
    wide = jax.ShapeDtypeStruct((SEQ, COL_BLOCK), F32)
    return pl.pallas_call(
        body, name="mid", grid=(SEQ // tm,),
        out_shape=(jax.ShapeDtypeStruct((SEQ, D_MODEL), F32), wide, wide, wide, wide, wide,
                   jax.ShapeDtypeStruct((D_MODEL, D_MODEL), BF16),
                   jax.ShapeDtypeStruct((1, D_MODEL), F32), jax.ShapeDtypeStruct((1, COL_BLOCK), F32),
                   jax.ShapeDtypeStruct((1, COL_BLOCK), F32), jax.ShapeDtypeStruct((1, 1), F32)),
        scratch_shapes=[pltpu.VMEM((D_MODEL, D_MODEL), F32)],
        in_specs=[half(), half(),
                  pl.BlockSpec((None, tm, COL_BLOCK), lambda i: (3, i, 0)),
                  pl.BlockSpec((None, tm, COL_BLOCK), lambda i: (7, i, 0)),
                  full(), full(), fixed(D_MODEL, D_MODEL), fixed(1, COL_BLOCK), fixed(1, COL_BLOCK),
                  fixed(1, D_MODEL)],
        out_specs=(full(), half(), half(), half(), half(), half(), fixed(D_MODEL, D_MODEL),
                   fixed(1, D_MODEL), fixed(1, COL_BLOCK), fixed(1, COL_BLOCK), fixed(1, 1)),
        compiler_params=_params(("arbitrary",)),
    )(attn_o, rec, proj, proj, x, target, w_out_g, attn_w, hgrn_w, final_w)


def _in_proj_bwd_rows(d_groups, w_g, x, dx2, mix_w, rc, rsa, rsb):
    tm = 256

    def body(*refs):
        dg_refs = refs[:N_DEV]
        wg_ref, x_ref, dx2_ref, w_ref, c_ref, sa_ref, sb_ref, gx_ref, dpb_ref, dmw_ref = refs[N_DEV:]

        @pl.when(pl.program_id(0) == 0)
        def _():
            dmw_ref[...] = jnp.zeros_like(dmw_ref)

        parts = []
        for j in range(N_DEV):
            dp = dg_refs[j][...]
            if j < 2:
                dp = _rot_transposed(dp, c_ref[...], sa_ref[...], sb_ref[...])
            parts.append(dp.astype(BF16))
        dpb = jnp.concatenate(parts, axis=1)
        for j in range(N_DEV):
            dpb_ref[j] = parts[j]
        g = _mm_nt(dpb, wg_ref[...])
        xf = x_ref[...]
        rstd = lax.rsqrt(jnp.mean(xf * xf, axis=-1, keepdims=True) + NORM_EPS)
        xn = xf * rstd
        dmw_ref[...] += jnp.sum(g * xn, axis=0, keepdims=True)
        gw = g * w_ref[...]
        gx_ref[...] = dx2_ref[...] + rstd * (gw - xn * jnp.mean(gw * xn, axis=-1, keepdims=True))

    tile = lambda cols: pl.BlockSpec((tm, cols), lambda i: (i, 0))
    fixed = lambda r, c: pl.BlockSpec((r, c), lambda i: (0, 0))
    return pl.pallas_call(
        body, name="in_proj_bwd_rows", grid=(SEQ // tm,),
        out_shape=(jax.ShapeDtypeStruct((SEQ, D_MODEL), F32), jax.ShapeDtypeStruct((N_DEV, SEQ, COL_BLOCK), BF16),
                   jax.ShapeDtypeStruct((1, D_MODEL), F32)),
        in_specs=[tile(COL_BLOCK) for _ in range(N_DEV)] + [
            pl.BlockSpec((D_MODEL, IN_COLS), lambda i: (0, 0), pipeline_mode=pl.Buffered(1)),
            tile(D_MODEL), tile(D_MODEL), fixed(1, D_MODEL), tile(LANES), tile(LANES), tile(LANES)],
        out_specs=(tile(D_MODEL), pl.BlockSpec((N_DEV, tm, COL_BLOCK), lambda i: (0, i, 0)), fixed(1, D_MODEL)),
        compiler_params=_params(("arbitrary",)),
    )(*d_groups, w_g, x, dx2, mix_w, rc, rsa, rsb)


def _weights_exchange(hn_t, dproj_b, dwout_p, small_p):
    n_chips = N_DEV // 2
    rb = 128
    S1_IN, S1_OUT, SMALL, S2_IN, S2_OUT = 0, 4, 8, 15, 18
    rel_of_pair = (1, 2, 3, 0)

    def body(order_ref, hnt_ref, dp_ref, dwout_ref, small_ref, gin_ref, gout_ref, gs_ref,
             part, s1_send, s1_in, s1_out, fwd_in, fwd_out, s2_in, s2_out, land_s, send_sems, recv_sems):
        t = pl.program_id(0)
        me = _my_place()
        x, y, c = me
        my_chip = 2 * x + y
        sibling = (x, y, 1 - c)

        def remote(slot, src, dst, to):
            return pltpu.make_async_remote_copy(src_ref=src, dst_ref=dst, send_sem=send_sems.at[slot],
                                                recv_sem=recv_sems.at[slot], device_id=to, device_id_type=MESH)

        def s1_in_copy(pair):
            return remote(S1_IN + pair, s1_send.at[pair], s1_in.at[pair], sibling)

        def s1_out_copy(pair):
            q = my_chip ^ rel_of_pair[pair]
            return remote(S1_OUT + pair, dwout_ref.at[q, 1 - c], s1_out.at[pair], sibling)

        def s2_copies(rel):
            peer = _peer(me, 2 * rel)
            return [remote(S2_IN + rel - 1, fwd_in.at[rel - 1], s2_in.at[rel - 1], peer),
                    remote(S2_OUT + rel - 1, fwd_out.at[rel - 1], s2_out.at[rel - 1], peer)]

        def small_copy(rel):
            return remote(SMALL + rel - 1, small_ref, land_s.at[rel], _peer(me, rel))

        @pl.when(t == 0)
        def _():
            land_s[0] = small_ref[...]
            for pair in range(n_chips):
                s1_out_copy(pair).start()
            for rel in range(1, N_DEV):
                small_copy(rel).start()

        part[...] = _mm(hnt_ref[...], dp_ref[...])

        def rows_loop(n_rows, fn):
            def step(b, carry):
                fn(pl.ds(pl.multiple_of(b * rb, rb), rb))
                return carry
            lax.fori_loop(0, n_rows // rb, step, 0)

        for pair, rel in enumerate(rel_of_pair):
            @pl.when(t == 2 * pair)
            def _(pair=pair):
                s1_send[pair] = part[...].astype(BF16)
                s1_in_copy(pair).start()

            @pl.when(t == 2 * pair + 1)
            def _(pair=pair, rel=rel):
                q = my_chip ^ rel
                s1_in_copy(pair).wait_recv()
                s1_out_copy(pair).wait_recv()
                dst_in = fwd_in.at[rel - 1] if rel else gin_ref
                dst_out = fwd_out.at[rel - 1] if rel else gout_ref

                def add_in(rows):
                    dst_in[rows, :] = (part[rows, :] + s1_in[pair, rows, :].astype(F32)).astype(dst_in.dtype)

                def add_out(rows):
                    dst_out[rows, :] = (dwout_ref[q, c, rows, :].astype(F32)
                                        + s1_out[pair, rows, :].astype(F32)).astype(dst_out.dtype)

                rows_loop(D_MODEL, add_in)
                rows_loop(WOUT_ROWS, add_out)
                if rel:
                    for cp in s2_copies(rel):
                        cp.start()

        @pl.when(t == N_DEV - 1)
        def _():
            for rel in range(1, n_chips):
                for cp in s2_copies(rel):
                    cp.wait_recv()

            def total_in(rows):
                g = gin_ref[rows, :]
                for rel in range(1, n_chips):
                    g = g + s2_in[rel - 1, rows, :].astype(F32)
                gin_ref[rows, :] = g

            def total_out(rows):
                g = gout_ref[rows, :]
                for rel in range(1, n_chips):
                    g = g + s2_out[rel - 1, rows, :].astype(F32)
                gout_ref[rows, :] = g

            rows_loop(D_MODEL, total_in)
            rows_loop(WOUT_ROWS, total_out)

            for rel in range(1, N_DEV):
                small_copy(rel).wait_recv()
            my_flat = _flat(me)
            g = land_s[my_flat ^ 0]
            for dev in range(1, N_DEV):
                g = g + land_s[my_flat ^ dev]
            gs_ref[...] = g

            for pair in range(n_chips):
                s1_in_copy(pair).wait_send()
                s1_out_copy(pair).wait_send()
            for rel in range(1, n_chips):
                for cp in s2_copies(rel):
                    cp.wait_send()
            for rel in range(1, N_DEV):
                small_copy(rel).wait_send()

    place_x, place_y, place_c = _my_place()
    my_chip = 2 * place_x + place_y
    order = jnp.stack([2 * (my_chip ^ rel) + core for rel in rel_of_pair
                       for core in (1 - place_c, place_c)]).astype(jnp.int32)

    whole = lambda: pl.BlockSpec(memory_space=pltpu.VMEM)
    in_blocks = lambda n: pltpu.VMEM((n, D_MODEL, COL_BLOCK), BF16)
    out_blocks = lambda n: pltpu.VMEM((n, WOUT_ROWS, D_MODEL), BF16)
    grid_spec = pltpu.PrefetchScalarGridSpec(
        num_scalar_prefetch=1, grid=(N_DEV,),
        in_specs=[pl.BlockSpec((D_MODEL, SEQ), lambda t, order: (0, 0), pipeline_mode=pl.Buffered(1)),
                  pl.BlockSpec((None, SEQ, COL_BLOCK), lambda t, order: (order[t], 0, 0)), whole(), whole()],
        out_specs=(whole(), whole(), whole()),
        scratch_shapes=[pltpu.VMEM((D_MODEL, COL_BLOCK), F32), in_blocks(n_chips), in_blocks(n_chips),
                        out_blocks(n_chips), in_blocks(n_chips - 1), out_blocks(n_chips - 1),
                        in_blocks(n_chips - 1), out_blocks(n_chips - 1),
                        pltpu.VMEM((N_DEV, SMALL_ROWS, LANES), F32),
                        pltpu.SemaphoreType.DMA((21,)), pltpu.SemaphoreType.DMA((21,))])
    return pl.pallas_call(
        body, name="weights_exchange", grid_spec=grid_spec,
        out_shape=(jax.ShapeDtypeStruct((D_MODEL, COL_BLOCK), F32), jax.ShapeDtypeStruct((WOUT_ROWS, D_MODEL), F32),
                   jax.ShapeDtypeStruct((SMALL_ROWS, LANES), F32)),
        compiler_params=_params(("arbitrary",)),
    )(order, hn_t, dproj_b, dwout_p.reshape(n_chips, 2, WOUT_ROWS, D_MODEL), small_p)


def _adamw(w, g, m, v):
    m = ADAM_B1 * m + (1.0 - ADAM_B1) * g
    v = ADAM_B2 * v + (1.0 - ADAM_B2) * (g * g)
    m_hat = m / (1.0 - ADAM_B1 ** ADAM_STEP)
    v_hat = v / (1.0 - ADAM_B2 ** ADAM_STEP)
    delta = -ADAM_LR * (m_hat / (jnp.sqrt(v_hat) + ADAM_EPS) + ADAM_WD * w)
    return delta, m, v


def _adamw_update(grads, weights, m_old, v_old):
    rb = 256

    def body(*refs):
        g_refs, w_refs, m_refs, v_refs = refs[0:3], refs[3:6], refs[6:9], refs[9:12]
        d_refs, nm_refs, nv_refs = refs[12:15], refs[15:18], refs[18:21]
        for k in range(3):
            n_rows = g_refs[k].shape[0]
            step_rows = min(rb, n_rows)

            def step(b, carry, k=k, step_rows=step_rows):
                rows = pl.ds(pl.multiple_of(b * step_rows, 8), step_rows)
                delta, nm, nv = _adamw(w_refs[k][rows, :], g_refs[k][rows, :], m_refs[k][rows, :], v_refs[k][rows, :])
                d_refs[k][rows, :] = delta
                nm_refs[k][rows, :] = nm
                nv_refs[k][rows, :] = nv
                return carry

            lax.fori_loop(0, n_rows // step_rows, step, 0)

    shapes = tuple(jax.ShapeDtypeStruct(g.shape, F32) for g in grads)
    vm = lambda: pl.BlockSpec(memory_space=pltpu.VMEM)
    outs = pl.pallas_call(
        body, name="adamw_update", out_shape=shapes * 3,
        in_specs=[vm() for _ in range(12)], out_specs=tuple(vm() for _ in range(9)),
        compiler_params=_params(),
    )(*grads, *weights, *m_old, *v_old)
    return outs[0:3], outs[3:6], outs[6:9]


def _pack_small(mix, attn, hgrn, lb, final, loss=None):
    def rows8(a):
        a = a.reshape(-1, LANES)
        return jnp.pad(a, ((0, 8 - a.shape[0]), (0, 0)))
    last = jnp.zeros((8, LANES), F32) if loss is None else jnp.pad(loss.reshape(1, 1), ((0, 7), (0, LANES - 1)))
    return jnp.concatenate([rows8(mix), rows8(attn), rows8(hgrn), rows8(lb), rows8(final), last], axis=0)


def _unpack_small(slab):
    return (slab[ROW_MIX:ROW_MIX + 8].reshape(1, D_MODEL), slab[ROW_ATTN:ROW_ATTN + 4].reshape(1, ATTN_WIDTH),
            slab[ROW_HGRN:ROW_HGRN + 4].reshape(1, HGRN_WIDTH), slab[ROW_LB:ROW_LB + 8].reshape(2, HGRN_WIDTH),
            slab[ROW_FINAL:ROW_FINAL + 8].reshape(D_MODEL))


def _rope(pos_row):
    j = np.arange(ROPE_ROWS)
    inv = np.where(j < ROPE_HALF, ROPE_THETA ** (-(j % ROPE_HALF) * (2.0 / ROPE_DIMS)), 0.0)
    e = np.arange(LANES) % HEAD_DIM
    hit = (j[:, None] == (e % ROPE_HALF)[None, :]) & (j[:, None] < ROPE_HALF)
    sel = np.stack([hit & (e < ROPE_DIMS), hit & (e >= ROPE_HALF) & (e < ROPE_DIMS),
                    -1.0 * (hit & (e < ROPE_HALF))]).astype(np.float32)
    return _rope_tables(pos_row, jnp.asarray(inv.astype(np.float32).reshape(ROPE_ROWS, 1)),
                        jnp.asarray(sel, dtype=BF16))


def _local_step(x, proj, qkv_sorted, w_in_g, w_out_g, tables, mix_w, attn_w, hgrn_w, lb_raw, final_w, target):
    rc, rsa, rsb = tables
    attn_o, lse = _attn_fwd_fused(qkv_sorted)
    rec, states = _hgrn_fwd(proj, lb_raw)

    (dx2, d_o, delta, d_ag, d_rec, d_hg, dwout_p, d_final, d_attn_w, d_hgrn_w, loss) = _mid(
        attn_o, rec, proj, x, target, w_out_g, attn_w, hgrn_w, final_w.reshape(1, D_MODEL))

    dqkv = _attn_bwd_fused(qkv_sorted, d_o, lse, delta)
    d_hq, d_hf, d_hi, d_lb = _hgrn_bwd(proj, lb_raw, d_rec, states)

    grad_x, dproj_b, d_mix = _in_proj_bwd_rows(
        (dqkv[0], dqkv[1], dqkv[2], d_ag, d_hq, d_hf, d_hi, d_hg), w_in_g, x, dx2, mix_w, rc, rsa, rsb)
    small_p = _pack_small(d_mix, d_attn_w, d_hgrn_w, d_lb, d_final, loss)
    return grad_x, dproj_b, dwout_p, small_p


def kernel(x, positions, w_in, w_out, mix_norm_w, attn_out_norm_w, hgrn_out_norm_w, hgrn_lb_raw, final_norm_w, loss_target, m_w_in, m_w_out, m_mix_norm_w, m_attn_out_norm_w, m_hgrn_out_norm_w, m_hgrn_lb_raw, m_final_norm_w, v_w_in, v_w_out, v_mix_norm_w, v_attn_out_norm_w, v_hgrn_out_norm_w, v_hgrn_lb_raw, v_final_norm_w):
    tables = _rope(positions)
    proj, hn_t, w_in_g, w_out_g, qkv_sorted = _gather_project(x[0], mix_norm_w, w_in[0], w_out[0], *tables)
    grad_x, dproj_b, dwout_p, small_p = _local_step(
        x[0], proj, qkv_sorted, w_in_g, w_out_g, tables, mix_norm_w, attn_out_norm_w, hgrn_out_norm_w,
        hgrn_lb_raw, final_norm_w, loss_target[0])
    g_in, g_out, g_s = _weights_exchange(hn_t, dproj_b, dwout_p, small_p)

    w_s = _pack_small(mix_norm_w, attn_out_norm_w, hgrn_out_norm_w, hgrn_lb_raw, final_norm_w)
    m_s = _pack_small(m_mix_norm_w, m_attn_out_norm_w, m_hgrn_out_norm_w, m_hgrn_lb_raw, m_final_norm_w)
    v_s = _pack_small(v_mix_norm_w, v_attn_out_norm_w, v_hgrn_out_norm_w, v_hgrn_lb_raw, v_final_norm_w)
    (d_in, d_out, d_s), (nm_in, nm_out, nm_s), (nv_in, nv_out, nv_s) = _adamw_update(
        (g_in, g_out, g_s), (w_in[0], w_out[0], w_s), (m_w_in[0], m_w_out[0], m_s), (v_w_in[0], v_w_out[0], v_s))

    loss = g_s[ROW_LOSS, 0]
    return (loss, grad_x[None], g_in[None], g_out[None], *_unpack_small(g_s),
            d_in[None], d_out[None], *_unpack_small(d_s),
            nm_in[None], nm_out[None], *_unpack_small(nm_s),
            nv_in[None], nv_out[None], *_unpack_small(nv_s))
```

```python
import functools

import jax
import jax.numpy as jnp
import numpy as np
from jax import lax
from jax.experimental import pallas as pl
from jax.experimental.pallas import tpu as pltpu

F32 = jnp.float32
BF16 = jnp.bfloat16

SEQ = 4096
D_MODEL = 1024
ATTN_WIDTH = 512
HGRN_WIDTH = 512
HEAD_DIM = 64
HGRN_HEADS = 4
HGRN_DIM = 128
HGRN_CHUNK = 64
N_CHUNKS = SEQ // HGRN_CHUNK
IN_COLS = 4096
COL_BLOCK = 512
N_DEV = 8
WOUT_ROWS = D_MODEL // N_DEV
ATTN_BLOCK = 128
DILATIONS = (1, 4, 16)
ROPE_THETA = 500000.0
ROPE_DIMS = 16
ROPE_HALF = 8
NORM_EPS = 1e-6
NEG_BIG = -1e30
LANES = 128

ADAM_LR = 0.001
ADAM_B1 = 0.9
ADAM_B2 = 0.999
ADAM_EPS = 1e-08
ADAM_WD = 0.01
ADAM_STEP = 10

SMALL_ROWS = 48
ROW_MIX, ROW_ATTN, ROW_HGRN, ROW_LB, ROW_FINAL, ROW_LOSS = 0, 8, 16, 24, 32, 40

VMEM_LIMIT = 56 * 1024 * 1024
MESH = pl.DeviceIdType.MESH


def _mm(a, b):
    return lax.dot_general(a, b, (((1,), (0,)), ((), ())), preferred_element_type=F32)


def _mm_nt(a, b):
    return lax.dot_general(a, b, (((1,), (1,)), ((), ())), preferred_element_type=F32)


def _mm_tn(a, b):
    return lax.dot_general(a, b, (((0,), (0,)), ((), ())), preferred_element_type=F32)


def _mm_exact(a, b):
    return lax.dot_general(a, b, (((1,), (0,)), ((), ())), preferred_element_type=F32,
                           precision=lax.Precision.HIGHEST)


def _sigmoid(v):
    return 1.0 / (1.0 + jnp.exp(-v))


def _params(sem=None, **kw):
    return pltpu.CompilerParams(dimension_semantics=sem, vmem_limit_bytes=VMEM_LIMIT, **kw)


def _my_place():
    return lax.axis_index("x"), lax.axis_index("y"), lax.axis_index("c")


def _peer(place, rel):
    x, y, c = place
    return (x ^ ((rel >> 2) & 1), y ^ ((rel >> 1) & 1), c ^ (rel & 1))


def _flat(place):
    x, y, c = place
    return 4 * x + 2 * y + c


ROPE_ROWS = 16


def _rope_tables(pos_row, inv_freq_col, selectors):
    def body(pos_ref, invf_ref, sel_ref, c_ref, sa_ref, sb_ref):
        ang = pos_ref[...].astype(F32) * invf_ref[...]
        cos, sin = jnp.cos(ang), jnp.sin(ang)

        def spread(v, sel):
            hi = v.astype(BF16)
            r1 = v - hi.astype(F32)
            mid = r1.astype(BF16)
            lo = (r1 - mid.astype(F32)).astype(BF16)
            return _mm_tn(hi, sel) + _mm_tn(mid, sel) + _mm_tn(lo, sel)

        e = lax.broadcasted_iota(jnp.int32, (1, LANES), 1) & (HEAD_DIM - 1)
        c_ref[...] = spread(cos, sel_ref[0]) + jnp.where(e < ROPE_DIMS, 0.0, 1.0)
        sa_ref[...] = spread(sin, sel_ref[1])
        sb_ref[...] = spread(sin, sel_ref[2])

    tab = jax.ShapeDtypeStruct((SEQ, LANES), F32)
    vm = lambda: pl.BlockSpec(memory_space=pltpu.VMEM)
    return pl.pallas_call(
        body, name="rope_tables", out_shape=(tab, tab, tab),
        in_specs=[vm(), vm(), vm()], out_specs=(vm(), vm(), vm()), compiler_params=_params(),
    )(pos_row, inv_freq_col, selectors)


def _per_slab(fn, t):
    return jnp.concatenate([fn(t[:, LANES * s:LANES * (s + 1)]) for s in range(t.shape[1] // LANES)], axis=1)


def _rot(t, c, sa, sb):
    return _per_slab(lambda u: u * c + pltpu.roll(u, ROPE_HALF, 1) * sa + pltpu.roll(u, LANES - ROPE_HALF, 1) * sb, t)


def _rot_transposed(g, c, sa, sb):
    return _per_slab(
        lambda u: u * c + pltpu.roll(u * sa, LANES - ROPE_HALF, 1) + pltpu.roll(u * sb, ROPE_HALF, 1), g)


def _gather_project(x, mix_w, w_in, w_out, rc, rsa, rsb):
    tm = 1024
    n_tiles = SEQ // tm
    arrival_of_step = (None, 0, 1, 2, 4, 5, 3, 6)

    def body(order_ref, x_ref, w_ref, win_ref, wout_ref, c_ref, sa_ref, sb_ref,
             proj_ref, hnt_ref, gin_hbm, gout_hbm, qkv_hbm,
             hn_s, w_land, wout_land, stage, sort_stage, send_sems, recv_sems, local_sems, sort_sems):
        g, i = pl.program_id(0), pl.program_id(1)
        me = _my_place()
        x_, y_, c_ = me
        sibling = (x_, y_, 1 - c_)
        chips = [(1 - x_, y_), (x_, 1 - y_), (1 - x_, 1 - y_)]

        def slab(which, place):
            idx = _flat(place)
            if which == 0:
                return w_land.at[idx]
            return wout_land.at[pl.ds(pl.multiple_of(idx * WOUT_ROWS, WOUT_ROWS), WOUT_ROWS), :]

        def remote(which, k, ref, to, src=None):
            return pltpu.make_async_remote_copy(
                src_ref=ref if src is None else src, dst_ref=ref, send_sem=send_sems.at[8 * which + k],
                recv_sem=recv_sems.at[8 * which + k], device_id=to, device_id_type=MESH)

        def copy(which, k, block, to, src=None):
            return remote(which, k, slab(which, block), to, src)

        def half(which, place, part):
            n = (D_MODEL if which == 0 else WOUT_ROWS) // 2
            if which == 0:
                return w_land.at[_flat(place), pl.ds(n * part, n), :]
            return wout_land.at[pl.ds(pl.multiple_of(_flat(place) * WOUT_ROWS + n * part, n), n), :]

        def first_copies(which):
            src = stage if which == 0 else None
            return ([copy(which, 0, me, sibling, src)]
                    + [copy(which, 1 + j, me, (*chips[j], c_), src) for j in range(2)])

        def relay(which, part):
            frm, to = (chips[1], chips[0]) if part == 0 else (chips[0], chips[1])
            return remote(which, 3 if part == 0 else 7, half(which, (*frm, c_), part), (*to, c_))

        def two_hop_half(which, part):
            return remote(which, 3 if part == 0 else 7, half(which, (*chips[2], c_), part), me)

        def pass_on(which, j):
            return copy(which, 4 + j, (*chips[j], c_), sibling)

        def arrival(which, k):
            if k == 0:
                return copy(which, 0, sibling, me)
            if k <= 2:
                return copy(which, k, (*chips[k - 1], c_), me)
            return copy(which, k, (*chips[k - 4], 1 - c_), me)

        def to_hbm(step):
            idx = order_ref[step]
            cols = pl.ds(pl.multiple_of(idx * COL_BLOCK, COL_BLOCK), COL_BLOCK)
            return pltpu.make_async_copy(w_land.at[idx], gin_hbm.at[:, cols], local_sems.at[step])

        @pl.when((g == 0) & (i == 0))
        def _():
            stage[...] = win_ref[...].astype(BF16)
            w_land[_flat(me)] = stage[...]
            wout_land[pl.ds(pl.multiple_of(_flat(me) * WOUT_ROWS, WOUT_ROWS), WOUT_ROWS), :] = (
                wout_ref[...].astype(BF16))
            for cp in first_copies(0) + first_copies(1)[:1]:
                cp.start()
            to_hbm(0).start()

        for step, k in enumerate(arrival_of_step):
            if k is None:
                continue

            @pl.when((g == step) & (i == 0))
            def _(k=k, step=step):
                if k == 3:
                    two_hop_half(0, 0).wait_recv()
                    two_hop_half(0, 1).wait_recv()
                else:
                    arrival(0, k).wait_recv()
                to_hbm(step).start()
                if 1 <= k <= 3:
                    pass_on(0, k - 1).start()
                if k == 1:
                    relay(0, 1).start()
                    for cp in first_copies(1)[1:]:
                        cp.start()
                if k == 2:
                    relay(0, 0).start()
                if k in (4, 5):
                    arrival(1, k - 3).wait_recv()
                    relay(1, 5 - k).start()

        rows = pl.ds(pl.multiple_of(i * tm, tm), tm)

        @pl.when(g == 0)
        def _():
            xf = x_ref[...]
            ms = jnp.mean(xf * xf, axis=-1, keepdims=True)
            hn = xf * lax.rsqrt(ms + NORM_EPS) * w_ref[...]
            hnt_ref[...] = hn.T.astype(BF16)
            hn_s[rows, :] = hn.astype(BF16)

        group = order_ref[g]

        def sorted_copy(tile_value):
            per = tm // SORT_RESIDUES
            cols = pl.ds(pl.multiple_of(group * COL_BLOCK, COL_BLOCK), COL_BLOCK)
            buf = i % 2

            def out_copies(tile, b):
                return [pltpu.make_async_copy(
                    sort_stage.at[b, :, r, :], qkv_hbm.at[r, pl.ds(pl.multiple_of(tile * per, per), per), cols],
                    sort_sems.at[b, r]) for r in range(SORT_RESIDUES)]

            @pl.when(i >= 2)
            def _():
                for copy in out_copies(i - 2, buf):
                    copy.wait()

            sort_stage[buf] = tile_value.reshape(per, SORT_RESIDUES, COL_BLOCK)
            for copy in out_copies(i, buf):
                copy.start()

            @pl.when(i == n_tiles - 1)
            def _():
                for copy in out_copies(i - 1, 1 - buf) + out_copies(i, buf):
                    copy.wait()

        @pl.when(group < 2)
        def _():
            rotated = _rot(_mm(hn_s[rows, :], w_land[group]), c_ref[...], sa_ref[...], sb_ref[...])
            proj_ref[...] = rotated
            sorted_copy(rotated)

        @pl.when(group == 2)
        def _():
            value = _mm(hn_s[rows, :], w_land[group])
            proj_ref[...] = value
            sorted_copy(value)

        @pl.when(group > 2)
        def _():
            proj_ref[...] = _mm(hn_s[rows, :], w_land[group])

        @pl.when((g == N_DEV - 1) & (i == n_tiles - 1))
        def _():
            pass_on(1, 0).start()
            pass_on(1, 1).start()
            two_hop_half(1, 0).wait_recv()
            two_hop_half(1, 1).wait_recv()
            pass_on(1, 2).start()
            for k in (0, 4, 5, 6):
                arrival(1, k).wait_recv()
            for which in (0, 1):
                for cp in (first_copies(which) + [relay(which, part) for part in range(2)]
                           + [pass_on(which, j) for j in range(3)]):
                    cp.wait_send()
            wout_copy = pltpu.make_async_copy(wout_land, gout_hbm, local_sems.at[N_DEV])
            wout_copy.start()
            for step in range(N_DEV):
                to_hbm(step).wait()
            wout_copy.wait()

    me = _my_place()
    x_, y_, c_ = me
    chips = [(1 - x_, y_), (x_, 1 - y_), (1 - x_, 1 - y_)]
    order = jnp.stack([_flat(p) for p in (
        me, (x_, y_, 1 - c_), (*chips[0], c_), (*chips[1], c_), (*chips[0], 1 - c_), (*chips[1], 1 - c_),
        (*chips[2], c_), (*chips[2], 1 - c_))]).astype(jnp.int32)

    first_sweep = lambda g, i, order: (jnp.where(g == 0, i, n_tiles - 1), 0)
    tab = pl.BlockSpec((tm, LANES), lambda g, i, order: (jnp.where(order[g] < 2, i, 0), 0))
    whole = lambda: pl.BlockSpec(memory_space=pltpu.VMEM)
    grid_spec = pltpu.PrefetchScalarGridSpec(
        num_scalar_prefetch=1, grid=(N_DEV, n_tiles),
        in_specs=[pl.BlockSpec((tm, D_MODEL), first_sweep),
                  pl.BlockSpec((1, D_MODEL), lambda g, i, order: (0, 0)),
                  whole(), whole(), tab, tab, tab],
        out_specs=(pl.BlockSpec((None, tm, COL_BLOCK), lambda g, i, order: (order[g], i, 0)),
                   pl.BlockSpec((D_MODEL, tm), lambda g, i, order: (0, jnp.where(g == 0, i, n_tiles - 1))),
                   pl.BlockSpec(memory_space=pl.ANY), pl.BlockSpec(memory_space=pl.ANY),
                   pl.BlockSpec(memory_space=pl.ANY)),
        scratch_shapes=[pltpu.VMEM((SEQ, D_MODEL), BF16),
                        pltpu.VMEM((N_DEV, D_MODEL, COL_BLOCK), BF16),
                        pltpu.VMEM((D_MODEL, D_MODEL), BF16),
                        pltpu.VMEM((D_MODEL, COL_BLOCK), BF16),
                        pltpu.VMEM((2, tm // SORT_RESIDUES, SORT_RESIDUES, COL_BLOCK), F32),
                        pltpu.SemaphoreType.DMA((16,)), pltpu.SemaphoreType.DMA((16,)),
                        pltpu.SemaphoreType.DMA((N_DEV + 1,)), pltpu.SemaphoreType.DMA((2, SORT_RESIDUES))])
    proj, hn_t, w_in_g, w_out_g, qkv_sorted = pl.pallas_call(
        body, name="gather_project", grid_spec=grid_spec,
        out_shape=(jax.ShapeDtypeStruct((N_DEV, SEQ, COL_BLOCK), F32), jax.ShapeDtypeStruct((D_MODEL, SEQ), BF16),
                   jax.ShapeDtypeStruct((D_MODEL, IN_COLS), BF16), jax.ShapeDtypeStruct((D_MODEL, D_MODEL), BF16),
                   jax.ShapeDtypeStruct((SORT_RESIDUES, SORT_ROWS, 3 * COL_BLOCK), F32)),
        compiler_params=_params(("arbitrary", "arbitrary")),
    )(order, x, mix_w, w_in, w_out, rc, rsa, rsb)
    return proj, hn_t, w_in_g, w_out_g, qkv_sorted.reshape(SEQ, 3 * COL_BLOCK)


SCORE_SCALE = HEAD_DIM ** -0.5
ATTN_GROUP_FWD = 16
ATTN_GROUP_BWD = 8
BLOCKS_PER_PATTERN = SEQ // ATTN_BLOCK
SORT_RESIDUES = 16
SORT_ROWS = SEQ // SORT_RESIDUES


def _write_band_bias(bias_ref):
    row = lax.broadcasted_iota(jnp.int32, (2 * ATTN_BLOCK, 2 * ATTN_BLOCK), 0) & (ATTN_BLOCK - 1)
    col = lax.broadcasted_iota(jnp.int32, (2 * ATTN_BLOCK, 2 * ATTN_BLOCK), 1)
    for pi, d in enumerate(DILATIONS):
        per = SORT_RESIDUES // d
        ahead = per * (row % (8 * d) - col % (16 * d)) + (row // (8 * d) - col // (16 * d))
        dist = ATTN_BLOCK + ahead
        bias_ref[2 * pi] = jnp.where((dist >= 0) & (dist <= ATTN_BLOCK), 0.0, NEG_BIG)
        bias_ref[2 * pi + 1] = jnp.where(ahead >= 0, 0.0, NEG_BIG)


def _head0_lanes():
    return lax.broadcasted_iota(jnp.int32, (ATTN_BLOCK, LANES), 1) < HEAD_DIM


def _stack_heads(t, h0):
    return jnp.concatenate([jnp.where(h0, t, 0.0), jnp.where(h0, 0.0, t)], axis=0).astype(BF16)


def _block_runs(i, d):
    nblk = BLOCKS_PER_PATTERN // d
    r, n = i // nblk, i % nblk
    kn = jnp.maximum(n - 1, 0)
    rows, keys = [], []
    for c in range(SORT_RESIDUES // d):
        base = SORT_ROWS * (c * d + r)
        rows.append(pl.ds(pl.multiple_of(base + 8 * d * n, 8), 8 * d))
        keys.append(pl.ds(pl.multiple_of(base + 8 * d * kn, 8), 16 * d))
    return rows, keys, (n == 0).astype(jnp.int32)


def _take(ref, runs):
    return jnp.concatenate([ref[run, :] for run in runs], axis=0)


def _put(ref, runs, value, add=False):
    at = 0
    for run in runs:
        piece = value[at:at + run.size]
        if add:
            ref[run, :] += piece
        else:
            ref[run, :] = piece
        at += run.size


def _sort_copies(src_hbm, lane_block, dst_ref, sem_ref):
    lanes = pl.ds(pl.multiple_of(LANES * lane_block, LANES), LANES)
    return [pltpu.make_async_copy(src_hbm.at[:, r, lanes], dst_ref.at[pl.ds(SORT_ROWS * r, SORT_ROWS), :],
                                  sem_ref.at[r]) for r in range(SORT_RESIDUES)]


def _unsort_copies(src_ref, dst_hbm, lane_block, sem_ref):
    lanes = pl.ds(pl.multiple_of(LANES * lane_block, LANES), LANES)
    return [pltpu.make_async_copy(src_ref.at[pl.ds(SORT_ROWS * r, SORT_ROWS), :], dst_hbm.at[:, r, lanes],
                                  sem_ref.at[r]) for r in range(SORT_RESIDUES)]


def _for_each_group(d, n_group, load, compute, store):
    def group(g, carry):
        items = [load(*_block_runs(g * n_group + u, d)) for u in range(n_group)]
        results = [compute(item) for item in items]
        for item, res in zip(items, results):
            store(item, res)
        return carry

    lax.fori_loop(0, BLOCKS_PER_PATTERN // n_group, group, 0)


def _attn_fwd_fused(qkv_sorted):
    n_pat = len(DILATIONS)
    tile2 = (2 * ATTN_BLOCK, LANES)

    def body(q_ref, k_ref, v_ref, o_hbm, lse_ref, o_slots, m_acc, l_acc, bias_ref, out_sem):
        step, n_steps = pl.program_id(0), pl.num_programs(0)
        pl.when(step == 0)(lambda: _write_band_bias(bias_ref))
        slot = step % 2
        o_acc = o_slots.at[slot]
        h0 = _head0_lanes()
        for pi, d in enumerate(DILATIONS):
            first, last = pi == 0, pi == n_pat - 1

            def load(rows, keys, which, first=first, pi=pi):
                item = dict(rows=rows, keys=keys, which=2 * pi + which)
                if not first:
                    item.update(o=_take(o_acc, rows), m=[_take(m_acc.at[h], rows) for h in range(2)],
                                l=[_take(l_acc.at[h], rows) for h in range(2)])
                return item

            def compute(item, first=first):
                kb = _take(k_ref, item["keys"]).astype(BF16)
                vb = _take(v_ref, item["keys"]).astype(BF16)
                s = _mm_nt(_stack_heads(_take(q_ref, item["rows"]) * SCORE_SCALE, h0), kb) + bias_ref[item["which"]]
                mb = jnp.max(s, axis=-1, keepdims=True)
                if first:
                    p = jnp.exp(s - mb)
                    mn = jnp.broadcast_to(mb, tile2)
                else:
                    m_old = jnp.concatenate(item["m"], axis=0)
                    mn = jnp.maximum(m_old, mb)
                    alpha = jnp.exp(m_old - mn)
                    p = jnp.exp(s - jnp.concatenate([mn, mn], axis=1))
                ls = jnp.sum(p, axis=-1, keepdims=True)
                pv = _mm(p.astype(BF16), vb)
                if first:
                    return pv, mn, jnp.broadcast_to(ls, tile2)
                o_old = jnp.concatenate([item["o"], item["o"]], axis=0)
                return alpha * o_old + pv, mn, alpha * jnp.concatenate(item["l"], axis=0) + ls

            def store(item, res, last=last):
                rows = item["rows"]
                (o0, o1), (m0, m1), (l0, l1) = ((a[:ATTN_BLOCK], a[ATTN_BLOCK:]) for a in res)
                if last:
                    _put(o_acc, rows, jnp.where(h0, o0 / l0, o1 / l1))
                    _put(lse_ref, rows, jnp.where(h0, m0 + jnp.log(l0), m1 + jnp.log(l1)))
                else:
                    _put(o_acc, rows, jnp.where(h0, o0, o1))
                    for h, (m, l) in enumerate(((m0, l0), (m1, l1))):
                        _put(m_acc.at[h], rows, m)
                        _put(l_acc.at[h], rows, l)

            _for_each_group(d, ATTN_GROUP_FWD, load, compute, store)

        def copies_out(of_step):
            return _unsort_copies(o_slots.at[of_step % 2], o_hbm, of_step, out_sem.at[of_step % 2])

        @pl.when(step > 0)
        def _():
            for copy in copies_out(step - 1):
                copy.wait()

        for copy in copies_out(step):
            copy.start()

        @pl.when(step == n_steps - 1)
        def _():
            for copy in copies_out(step):
                copy.wait()

    slab = lambda g: pl.BlockSpec((SEQ, LANES), functools.partial(lambda hp, g: (0, 4 * g + hp), g=g))
    wide = jax.ShapeDtypeStruct((SEQ, ATTN_WIDTH), F32)
    o_rows, lse = pl.pallas_call(
        body, name="attn_fwd", grid=(4,),
        out_shape=(jax.ShapeDtypeStruct((SORT_ROWS, SORT_RESIDUES, ATTN_WIDTH), F32), wide),
        in_specs=[slab(0), slab(1), slab(2)], out_specs=(pl.BlockSpec(memory_space=pl.ANY), slab(0)),
        scratch_shapes=[pltpu.VMEM((2, SEQ, LANES), F32), pltpu.VMEM((2, SEQ, LANES), F32),
                        pltpu.VMEM((2, SEQ, LANES), F32),
                        pltpu.VMEM((2 * len(DILATIONS), 2 * ATTN_BLOCK, 2 * ATTN_BLOCK), F32),
                        pltpu.SemaphoreType.DMA((2, SORT_RESIDUES))],
        compiler_params=_params(("arbitrary",)),
    )(qkv_sorted, qkv_sorted, qkv_sorted)
    return o_rows.reshape(SEQ, ATTN_WIDTH), lse


def _attn_bwd_fused(qkv_sorted, d_out, lse_sorted, delta):
    def body(q_ref, k_ref, v_ref, do_hbm, lse_ref, del_hbm, dq_hbm, dk_hbm, dv_hbm,
             in_slots, out_slots, bias_ref, in_sem, out_sem):
        step, n_steps = pl.program_id(0), pl.num_programs(0)
        slot = step % 2

        def copies_in(of_step):
            s = of_step % 2
            return [copy for j, hbm in enumerate((do_hbm, del_hbm))
                    for copy in _sort_copies(hbm, of_step, in_slots.at[s, j], in_sem.at[s, j])]

        def copies_out(of_step):
            s = of_step % 2
            return [copy for j, hbm in enumerate((dq_hbm, dk_hbm, dv_hbm))
                    for copy in _unsort_copies(out_slots.at[s, j], hbm, of_step, out_sem.at[s, j])]

        @pl.when(step == 0)
        def _():
            for copy in copies_in(step):
                copy.start()
            _write_band_bias(bias_ref)

        @pl.when(step + 1 < n_steps)
        def _():
            for copy in copies_in(step + 1):
                copy.start()

        do_s, del_s = in_slots.at[slot, 0], in_slots.at[slot, 1]
        dq_s, dk_s, dv_s = (out_slots.at[slot, j] for j in range(3))
        dk_s[...] = jnp.zeros_like(dk_s)
        dv_s[...] = jnp.zeros_like(dv_s)
        for copy in copies_in(step):
            copy.wait()
        h0 = _head0_lanes()
        for pi, d in enumerate(DILATIONS):
            first = pi == 0

            def load(rows, keys, which, pi=pi):
                return dict(rows=rows, keys=keys, q=_take(q_ref, rows), g=_take(do_s, rows),
                            lse=_take(lse_ref, rows), delta=_take(del_s, rows),
                            k=_take(k_ref, keys).astype(BF16), v=_take(v_ref, keys).astype(BF16),
                            bias=bias_ref[2 * pi + which])

            def per_head(t):
                swapped = pltpu.roll(t, HEAD_DIM, 1)
                both = jnp.concatenate([jnp.where(h0, t, swapped), jnp.where(h0, swapped, t)], axis=0)
                return jnp.concatenate([both, both], axis=1)

            def compute(item):
                q2, g2 = _stack_heads(item["q"] * SCORE_SCALE, h0), _stack_heads(item["g"], h0)
                s = _mm_nt(q2, item["k"]) + item["bias"]
                p = jnp.exp(s - per_head(item["lse"]))
                dp = _mm_nt(g2, item["v"])
                ds = (p * (dp - per_head(item["delta"]))).astype(BF16)
                dq2 = _mm(ds, item["k"])
                dq = jnp.where(h0, dq2[:ATTN_BLOCK], dq2[ATTN_BLOCK:]) * SCORE_SCALE
                return dq, _mm_tn(ds, q2), _mm_tn(p.astype(BF16), g2)

            def store(item, res, first=first):
                _put(dq_s, item["rows"], res[0], add=not first)
                _put(dk_s, item["keys"], res[1], add=True)
                _put(dv_s, item["keys"], res[2], add=True)

            _for_each_group(d, ATTN_GROUP_BWD, load, compute, store)

        @pl.when(step > 0)
        def _():
            for copy in copies_out(step - 1):
                copy.wait()

        for copy in copies_out(step):
            copy.start()

        @pl.when(step == n_steps - 1)
        def _():
            for copy in copies_out(step):
                copy.wait()

    slab = lambda g: pl.BlockSpec((SEQ, LANES), functools.partial(lambda hp, g: (0, 4 * g + hp), g=g))
    anywhere = pl.BlockSpec(memory_space=pl.ANY)
    by_residue = (SORT_ROWS, SORT_RESIDUES, ATTN_WIDTH)
    grads = pl.pallas_call(
        body, name="attn_bwd", grid=(4,), out_shape=(jax.ShapeDtypeStruct(by_residue, F32),) * 3,
        scratch_shapes=[pltpu.VMEM((2, 2, SEQ, LANES), F32), pltpu.VMEM((2, 3, SEQ, LANES), F32),
                        pltpu.VMEM((2 * len(DILATIONS), 2 * ATTN_BLOCK, 2 * ATTN_BLOCK), F32),
                        pltpu.SemaphoreType.DMA((2, 2, SORT_RESIDUES)), pltpu.SemaphoreType.DMA((2, 3, SORT_RESIDUES))],
        in_specs=[slab(0), slab(1), slab(2), anywhere, slab(0), anywhere], out_specs=(anywhere,) * 3,
        compiler_params=_params(("arbitrary",)),
    )(qkv_sorted, qkv_sorted, qkv_sorted, d_out.reshape(by_residue), lse_sorted, delta.reshape(by_residue))
    return tuple(g.reshape(SEQ, ATTN_WIDTH) for g in grads)


def _hgrn_lower_bound(lb_ref):
    r0, r1 = lb_ref[0:1, :], lb_ref[1:2, :]
    mx = jnp.maximum(r0, r1)
    e0, e1 = jnp.exp(r0 - mx), jnp.exp(r1 - mx)
    return e0 / (e0 + e1)


def _hgrn_gates(hq, hf, lb):
    sq = _sigmoid(hq)
    sg = _sigmoid(hf)
    f = lb + (1.0 - lb) * sg
    return hq * sq, sq, sg, f, 1.0 - f, jnp.log(f)


HGRN_PAIR = 4
HGRN_SEQ_BLOCK = 1024
HGRN_GROUP = 4
HGRN_ROWS = HGRN_GROUP * HGRN_CHUNK


def _hgrn_specs(reverse):
    n_blocks = SEQ // HGRN_SEQ_BLOCK
    width = HGRN_PAIR * HGRN_DIM
    blk = (lambda s: n_blocks - 1 - s) if reverse else (lambda s: s)
    cols = lambda g: pl.BlockSpec((None, HGRN_SEQ_BLOCK, width), functools.partial(lambda p, s, g: (g, blk(s), p), g=g))
    pair = pl.BlockSpec((HGRN_SEQ_BLOCK, width), lambda p, s: (blk(s), p))
    lb = pl.BlockSpec((2, width), lambda p, s: (0, p))
    states = pl.BlockSpec((HGRN_PAIR, HGRN_SEQ_BLOCK // HGRN_CHUNK, HGRN_DIM, HGRN_DIM),
                          lambda p, s: (p, blk(s), 0, 0))
    return cols, pair, lb, states


def _chunk_masks():
    ri = lax.broadcasted_iota(jnp.int32, (HGRN_ROWS, HGRN_ROWS), 0)
    ci = lax.broadcasted_iota(jnp.int32, (HGRN_ROWS, HGRN_ROWS), 1)
    same = (ri // HGRN_CHUNK) == (ci // HGRN_CHUNK)
    return same, same & (ri >= ci), same & (ri <= ci)


def _mm_select(sel, v):
    hi = v.astype(BF16)
    r1 = v - hi.astype(F32)
    mid = r1.astype(BF16)
    lo = (r1 - mid.astype(F32)).astype(BF16)
    return _mm(sel, hi) + _mm(sel, mid) + _mm(sel, lo)


def _head_cols(a, h):
    return a[:, HGRN_DIM * h:HGRN_DIM * (h + 1)]


def _hgrn_fwd(proj, lb_raw):
    t, rws = HGRN_CHUNK, HGRN_ROWS

    def body(hq_ref, hf_ref, hi_ref, lb_ref, rec_ref, st_ref, state):
        @pl.when(pl.program_id(1) == 0)
        def _():
            state[...] = jnp.zeros_like(state)

        lb = _hgrn_lower_bound(lb_ref)
        same, causal, _ = _chunk_masks()
        sel = jnp.concatenate([causal, same], axis=0).astype(BF16)

        def group(g, sts):
            rows = pl.ds(pl.multiple_of(g * rws, rws), rws)
            q, _, _, _, k, lf = _hgrn_gates(hq_ref[rows, :], hf_ref[rows, :], lb)
            sums = _mm_select(sel, lf)
            cum, last = sums[:rws], sums[rws:]
            qd = (q * jnp.exp(cum)).astype(BF16)
            ki = (k * jnp.exp(-cum)).astype(BF16)
            ke = (k * jnp.exp(last - cum)).astype(BF16)
            vb = hi_ref[rows, :].astype(BF16)
            dec = jnp.exp(last)
            new_sts, recs = [], []
            for h in range(HGRN_PAIR):
                qd_h, ke_h, vb_h = _head_cols(qd, h), _head_cols(ke, h), _head_cols(vb, h)
                att = jnp.where(causal, _mm_nt(qd_h, _head_cols(ki, h)), 0.0).astype(BF16)
                intra = _mm(att, vb_h)
                st = sts[h]
                outs = []
                for c in range(HGRN_GROUP):
                    sl = slice(c * t, (c + 1) * t)
                    st_ref[h, g * HGRN_GROUP + c] = st
                    outs.append(intra[sl] + _mm_nt(qd_h[sl], st.astype(BF16)))
                    st = st * _head_cols(dec[c * t:c * t + 1, :], h) + _mm_tn(vb_h[sl], ke_h[sl])
                new_sts.append(st)
                recs.append(jnp.concatenate(outs, axis=0))
            rec_ref[rows, :] = jnp.concatenate(recs, axis=1)
            return tuple(new_sts)

        sts = lax.fori_loop(0, HGRN_SEQ_BLOCK // rws, group, tuple(state[h] for h in range(HGRN_PAIR)))
        for h in range(HGRN_PAIR):
            state[h] = sts[h]

    cols, pair, lb, states = _hgrn_specs(reverse=False)
    return pl.pallas_call(
        body, name="hgrn_fwd", grid=(HGRN_HEADS // HGRN_PAIR, SEQ // HGRN_SEQ_BLOCK),
        out_shape=(jax.ShapeDtypeStruct((SEQ, HGRN_WIDTH), F32),
                   jax.ShapeDtypeStruct((HGRN_HEADS, N_CHUNKS, HGRN_DIM, HGRN_DIM), F32)),
        in_specs=[cols(4), cols(5), cols(6), lb], out_specs=(pair, states),
        scratch_shapes=[pltpu.VMEM((HGRN_PAIR, HGRN_DIM, HGRN_DIM), F32)],
        compiler_params=_params(("parallel", "arbitrary")),
    )(proj, proj, proj, lb_raw)


def _hgrn_bwd(proj, lb_raw, d_rec, states):
    t, rws = HGRN_CHUNK, HGRN_ROWS

    def body(hq_ref, hf_ref, hi_ref, lb_ref, do_ref, st_ref, dhq_ref, dhf_ref, dhi_ref, dlb_ref,
             dstate, dlb_acc):
        lb = _hgrn_lower_bound(lb_ref)
        same, causal, anti = _chunk_masks()
        sel = jnp.concatenate([causal, same], axis=0).astype(BF16)
        sel_t = jnp.concatenate([anti, same], axis=1).astype(BF16)
        @pl.when(pl.program_id(1) == 0)
        def _():
            dstate[...] = jnp.zeros_like(dstate)
            dlb_acc[...] = jnp.zeros_like(dlb_acc)

        n_groups = HGRN_SEQ_BLOCK // rws
        chunks = [slice(c * t, (c + 1) * t) for c in range(HGRN_GROUP)]

        def group(i, dsts_in):
            g = n_groups - 1 - i
            rows = pl.ds(pl.multiple_of(g * rws, rws), rws)
            hq = hq_ref[rows, :]
            q, sq, sg, f, k, lf = _hgrn_gates(hq, hf_ref[rows, :], lb)
            sums = _mm_select(sel, lf)
            cum, last = sums[:rws], sums[rws:]
            e_cum, e_inv, e_end, dec = jnp.exp(cum), jnp.exp(-cum), jnp.exp(last - cum), jnp.exp(last)
            qd, ki, ke = q * e_cum, k * e_inv, k * e_end
            qdb, kib, keb = qd.astype(BF16), ki.astype(BF16), ke.astype(BF16)
            vb = hi_ref[rows, :].astype(BF16)
            gb = do_ref[rows, :].astype(BF16)

            dsts_out, per_head = [], []
            for h in range(HGRN_PAIR):
                qdb_h, kib_h, keb_h = _head_cols(qdb, h), _head_cols(kib, h), _head_cols(keb, h)
                vb_h, gb_h = _head_cols(vb, h), _head_cols(gb, h)
                att = jnp.where(causal, _mm_nt(qdb_h, kib_h), 0.0).astype(BF16)
                datt = jnp.where(causal, _mm_nt(gb_h, vb_h), 0.0).astype(BF16)
                dv = _mm_tn(att, gb_h)
                dqd = _mm(datt, kib_h)
                dki = _mm_tn(datt, qdb_h)

                decs = [_head_cols(dec[c * t:c * t + 1, :], h) for c in range(HGRN_GROUP)]
                dsts = [None] * HGRN_GROUP
                dst = dsts_in[h]
                for c in reversed(range(HGRN_GROUP)):
                    dsts[c] = dst
                    dst = dst * decs[c] + _mm_tn(gb_h[chunks[c]], qdb_h[chunks[c]])
                dsts_out.append(dst)

                dv_x, dqd_x, dke, dlast_x = [], [], [], []
                for c, sl in enumerate(chunks):
                    st_prev = st_ref[h, g * HGRN_GROUP + c]
                    dstb = dsts[c].astype(BF16)
                    dv_x.append(_mm_nt(keb_h[sl], dstb))
                    dqd_x.append(_mm(gb_h[sl], st_prev.astype(BF16)))
                    dke.append(_mm(vb_h[sl], dstb))
                    ddec = jnp.sum(dsts[c] * st_prev, axis=0, keepdims=True)
                    dlast_x.append(jnp.broadcast_to(ddec * decs[c], (t, HGRN_DIM)))
                per_head.append((dv + jnp.concatenate(dv_x, axis=0), dqd + jnp.concatenate(dqd_x, axis=0),
                                 dki, jnp.concatenate(dke, axis=0), jnp.concatenate(dlast_x, axis=0)))
            dv, dqd, dki, dke, dlast = (jnp.concatenate(list(parts), axis=1) for parts in zip(*per_head))

            dq = dqd * e_cum
            dk = dki * e_inv + dke * e_end
            dke_ke = dke * ke
            dcum = dqd * qd - dki * ki - dke_ke
            dlf = _mm_select(sel_t, jnp.concatenate([dcum, dke_ke], axis=0)) + dlast
            df = dlf / f - dk
            dhq_ref[rows, :] = (dq * (sq * (1.0 + hq * (1.0 - sq)))).astype(BF16)
            dhf_ref[rows, :] = (df * (1.0 - lb) * (sg * (1.0 - sg))).astype(BF16)
            dhi_ref[rows, :] = dv.astype(BF16)
            dlb_acc[...] += jnp.sum(df * (1.0 - sg), axis=0, keepdims=True)
            return tuple(dsts_out)

        dsts = lax.fori_loop(0, n_groups, group, tuple(dstate[h] for h in range(HGRN_PAIR)))
        for h in range(HGRN_PAIR):
            dstate[h] = dsts[h]
        g0 = dlb_acc[...] * lb * (1.0 - lb)
        dlb_ref[...] = jnp.concatenate([g0, -g0], axis=0)

    cols, pair, lb_spec, st_spec = _hgrn_specs(reverse=True)
    wide = jax.ShapeDtypeStruct((SEQ, HGRN_WIDTH), BF16)
    return pl.pallas_call(
        body, name="hgrn_bwd", grid=(HGRN_HEADS // HGRN_PAIR, SEQ // HGRN_SEQ_BLOCK),
        out_shape=(wide, wide, wide, jax.ShapeDtypeStruct((2, HGRN_WIDTH), F32)),
        in_specs=[cols(4), cols(5), cols(6), lb_spec, pair, st_spec],
        out_specs=(pair, pair, pair, lb_spec),
        scratch_shapes=[pltpu.VMEM((HGRN_PAIR, HGRN_DIM, HGRN_DIM), F32),
                        pltpu.VMEM((1, HGRN_PAIR * HGRN_DIM), F32)],
        compiler_params=_params(("parallel", "arbitrary")),
    )(proj, proj, proj, lb_raw, d_rec, states)


def _group_sum(v, group):
    parts = []
    for s in range(v.shape[1] // LANES):
        slab = v[:, LANES * s:LANES * (s + 1)]
        if group == LANES:
            parts.append(jnp.broadcast_to(jnp.sum(slab, axis=-1, keepdims=True), slab.shape))
        else:
            h0 = lax.broadcasted_iota(jnp.int32, slab.shape, 1) < HEAD_DIM
            s0 = jnp.sum(jnp.where(h0, slab, 0.0), axis=-1, keepdims=True)
            s1 = jnp.sum(jnp.where(h0, 0.0, slab), axis=-1, keepdims=True)
            parts.append(jnp.where(h0, s0, s1))
    return jnp.concatenate(parts, axis=1)


def _mid(attn_o, rec, proj, x, target, w_out_g, attn_w, hgrn_w, final_w):
    tm = 256

    def branch_fwd(o, gate, w, group):
        r = lax.rsqrt(_group_sum(o * o, group) * (1.0 / group) + NORM_EPS)
        nrm = o * r
        sg = _sigmoid(gate)
        return r, nrm, sg, nrm * w * (gate * sg)

    def branch_bwd(dy, r, nrm, sg, gate, w, group):
        silu = gate * sg
        d_gate = dy * nrm * w * (sg * (1.0 + gate * (1.0 - sg)))
        d_w = jnp.sum(dy * nrm * silu, axis=0, keepdims=True)
        dn = dy * w * silu
        d_o = r * (dn - nrm * (_group_sum(dn * nrm, group) * (1.0 / group)))
        return d_o, d_gate, d_w

    def body(o_ref, rec_ref, ag_ref, hg_ref, x_ref, tgt_ref, wout_ref, aw_ref, hw_ref, fw_ref,
             dx2_ref, do_ref, delta_ref, dag_ref, drec_ref, dhg_ref, dwout_ref, dfw_ref, daw_ref, dhw_ref,
             loss_ref, dwout_acc):
        i = pl.program_id(0)

        @pl.when(i == 0)
        def _():
            dwout_acc[...] = jnp.zeros_like(dwout_acc)
            dfw_ref[...] = jnp.zeros_like(dfw_ref)
            daw_ref[...] = jnp.zeros_like(daw_ref)
            dhw_ref[...] = jnp.zeros_like(dhw_ref)
            loss_ref[...] = jnp.zeros_like(loss_ref)

        o, rc, ag, hg = o_ref[...], rec_ref[...], ag_ref[...], hg_ref[...]
        aw, hw, fw = aw_ref[...], hw_ref[...], fw_ref[...]
        ra, na, sga, ya = branch_fwd(o, ag, aw, HEAD_DIM)
        rh, nh, sgh, yh = branch_fwd(rc, hg, hw, HGRN_DIM)
        mixed = jnp.concatenate([ya, yh], axis=1).astype(BF16)
        wout = wout_ref[...]
        x2 = x_ref[...] + _mm(mixed, wout)
        rstd = lax.rsqrt(jnp.mean(x2 * x2, axis=-1, keepdims=True) + NORM_EPS)
        xn = x2 * rstd
        err = xn * fw - tgt_ref[...]
        row_loss = jnp.mean(err * err, axis=-1, keepdims=True)
        loss_ref[...] += 0.5 * jnp.sum(row_loss, axis=0, keepdims=True)
        dy = err * (1.0 / D_MODEL)
        dfw_ref[...] += jnp.sum(dy * xn, axis=0, keepdims=True)
        dxn = dy * fw
        dx2 = rstd * (dxn - xn * jnp.mean(dxn * xn, axis=-1, keepdims=True))
        dx2_ref[...] = dx2
        dx2b = dx2.astype(BF16)
        dwout_acc[...] += _mm_tn(mixed, dx2b)

        @pl.when(i == pl.num_programs(0) - 1)
        def _():
            dwout_ref[...] = dwout_acc[...].astype(BF16)

        dmixed = _mm_nt(dx2b, wout)

        d_o, d_ag, d_aw = branch_bwd(dmixed[:, :ATTN_WIDTH], ra, na, sga, ag, aw, HEAD_DIM)
        d_rec, d_hg, d_hw = branch_bwd(dmixed[:, ATTN_WIDTH:], rh, nh, sgh, hg, hw, HGRN_DIM)
        do_ref[...] = d_o
        delta_ref[...] = _group_sum(d_o * o, HEAD_DIM)
        dag_ref[...] = d_ag.astype(BF16)
        drec_ref[...] = d_rec
        dhg_ref[...] = d_hg.astype(BF16)
        daw_ref[...] += d_aw
        dhw_ref[...] += d_hw

    half = lambda: pl.BlockSpec((tm, COL_BLOCK), lambda i: (i, 0))
    full = lambda: pl.BlockSpec((tm, D_MODEL), lambda i: (i, 0))
    fixed = lambda r, c: pl.BlockSpec((r, c), lambda i: (0, 0))
    wide = jax.ShapeDtypeStruct((SEQ, COL_BLOCK), F32)
    wide_b = jax.ShapeDtypeStruct((SEQ, COL_BLOCK), BF16)
    return pl.pallas_call(
        body, name="mid", grid=(SEQ // tm,),
        out_shape=(jax.ShapeDtypeStruct((SEQ, D_MODEL), F32), wide, wide, wide_b, wide, wide_b,
                   jax.ShapeDtypeStruct((D_MODEL, D_MODEL), BF16),
                   jax.ShapeDtypeStruct((1, D_MODEL), F32), jax.ShapeDtypeStruct((1, COL_BLOCK), F32),
                   jax.ShapeDtypeStruct((1, COL_BLOCK), F32), jax.ShapeDtypeStruct((1, 1), F32)),
        scratch_shapes=[pltpu.VMEM((D_MODEL, D_MODEL), F32)],
        in_specs=[half(), half(),
                  pl.BlockSpec((None, tm, COL_BLOCK), lambda i: (3, i, 0)),
                  pl.BlockSpec((None, tm, COL_BLOCK), lambda i: (7, i, 0)),
                  full(), full(), fixed(D_MODEL, D_MODEL), fixed(1, COL_BLOCK), fixed(1, COL_BLOCK),
                  fixed(1, D_MODEL)],
        out_specs=(full(), half(), half(), half(), half(), half(), fixed(D_MODEL, D_MODEL),
                   fixed(1, D_MODEL), fixed(1, COL_BLOCK), fixed(1, COL_BLOCK), fixed(1, 1)),
        compiler_params=_params(("arbitrary",)),
    )(attn_o, rec, proj, proj, x, target, w_out_g, attn_w, hgrn_w, final_w)


def _in_proj_bwd_rows(d_groups, w_g, x, dx2, mix_w, rc, rsa, rsb):
    tm = 256

    def body(*refs):
        dg_refs = refs[:N_DEV]
        wg_ref, x_ref, dx2_ref, w_ref, c_ref, sa_ref, sb_ref, gx_ref, dpb_ref, dmw_ref = refs[N_DEV:]

        @pl.when(pl.program_id(0) == 0)
        def _():
            dmw_ref[...] = jnp.zeros_like(dmw_ref)

        parts = []
        for j in range(N_DEV):
            dp = dg_refs[j][...]
            if j < 2:
                dp = _rot_transposed(dp, c_ref[...], sa_ref[...], sb_ref[...])
            parts.append(dp.astype(BF16))
        dpb = jnp.concatenate(parts, axis=1)
        for j in range(N_DEV):
            dpb_ref[j] = parts[j]
        g = _mm_nt(dpb, wg_ref[...])
        xf = x_ref[...]
        rstd = lax.rsqrt(jnp.mean(xf * xf, axis=-1, keepdims=True) + NORM_EPS)
        xn = xf * rstd
        dmw_ref[...] += jnp.sum(g * xn, axis=0, keepdims=True)
        gw = g * w_ref[...]
        gx_ref[...] = dx2_ref[...] + rstd * (gw - xn * jnp.mean(gw * xn, axis=-1, keepdims=True))

    tile = lambda cols: pl.BlockSpec((tm, cols), lambda i: (i, 0))
    fixed = lambda r, c: pl.BlockSpec((r, c), lambda i: (0, 0))
    return pl.pallas_call(
        body, name="in_proj_bwd_rows", grid=(SEQ // tm,),
        out_shape=(jax.ShapeDtypeStruct((SEQ, D_MODEL), F32), jax.ShapeDtypeStruct((N_DEV, SEQ, COL_BLOCK), BF16),
                   jax.ShapeDtypeStruct((1, D_MODEL), F32)),
        in_specs=[tile(COL_BLOCK) for _ in range(N_DEV)] + [
            pl.BlockSpec((D_MODEL, IN_COLS), lambda i: (0, 0), pipeline_mode=pl.Buffered(1)),
            tile(D_MODEL), tile(D_MODEL), fixed(1, D_MODEL), tile(LANES), tile(LANES), tile(LANES)],
        out_specs=(tile(D_MODEL), pl.BlockSpec((N_DEV, tm, COL_BLOCK), lambda i: (0, i, 0)), fixed(1, D_MODEL)),
        compiler_params=_params(("arbitrary",)),
    )(*d_groups, w_g, x, dx2, mix_w, rc, rsa, rsb)


def _weights_exchange(hn_t, dproj_b, dwout_p, small_p):
    n_chips = N_DEV // 2
    rb = 128
    S1_IN, S1_OUT, SMALL, S2_IN, S2_OUT = 0, 4, 8, 15, 18
    rel_of_pair = (1, 2, 3, 0)

    def body(order_ref, hnt_ref, dp_ref, dwout_ref, small_ref, gin_ref, gout_ref, gs_ref,
             part, s1_send, s1_in, s1_out, fwd_in, fwd_out, s2_in, s2_out, land_s, send_sems, recv_sems):
        t = pl.program_id(0)
        me = _my_place()
        x, y, c = me
        my_chip = 2 * x + y
        sibling = (x, y, 1 - c)

        def remote(slot, src, dst, to):
            return pltpu.make_async_remote_copy(src_ref=src, dst_ref=dst, send_sem=send_sems.at[slot],
                                                recv_sem=recv_sems.at[slot], device_id=to, device_id_type=MESH)

        def s1_in_copy(pair):
            return remote(S1_IN + pair, s1_send.at[pair], s1_in.at[pair], sibling)

        def s1_out_copy(pair):
            q = my_chip ^ rel_of_pair[pair]
            return remote(S1_OUT + pair, dwout_ref.at[q, 1 - c], s1_out.at[pair], sibling)

        def s2_copies(rel):
            peer = _peer(me, 2 * rel)
            return [remote(S2_IN + rel - 1, fwd_in.at[rel - 1], s2_in.at[rel - 1], peer),
                    remote(S2_OUT + rel - 1, fwd_out.at[rel - 1], s2_out.at[rel - 1], peer)]

        def small_copy(rel):
            return remote(SMALL + rel - 1, small_ref, land_s.at[rel], _peer(me, rel))

        @pl.when(t == 0)
        def _():
            land_s[0] = small_ref[...]
            for pair in range(n_chips):
                s1_out_copy(pair).start()
            for rel in range(1, N_DEV):
                small_copy(rel).start()

        part[...] = _mm(hnt_ref[...], dp_ref[...])

        def rows_loop(n_rows, fn):
            def step(b, carry):
                fn(pl.ds(pl.multiple_of(b * rb, rb), rb))
                return carry
            lax.fori_loop(0, n_rows // rb, step, 0)

        for pair, rel in enumerate(rel_of_pair):
            @pl.when(t == 2 * pair)
            def _(pair=pair):
                s1_send[pair] = part[...].astype(BF16)
                s1_in_copy(pair).start()

            @pl.when(t == 2 * pair + 1)
            def _(pair=pair, rel=rel):
                q = my_chip ^ rel
                s1_in_copy(pair).wait_recv()
                s1_out_copy(pair).wait_recv()
                dst_in = fwd_in.at[rel - 1] if rel else gin_ref
                dst_out = fwd_out.at[rel - 1] if rel else gout_ref

                def add_in(rows):
                    dst_in[rows, :] = (part[rows, :] + s1_in[pair, rows, :].astype(F32)).astype(dst_in.dtype)

                def add_out(rows):
                    dst_out[rows, :] = (dwout_ref[q, c, rows, :].astype(F32)
                                        + s1_out[pair, rows, :].astype(F32)).astype(dst_out.dtype)

                rows_loop(D_MODEL, add_in)
                rows_loop(WOUT_ROWS, add_out)
                if rel:
                    for cp in s2_copies(rel):
                        cp.start()

        @pl.when(t == N_DEV - 1)
        def _():
            for rel in range(1, n_chips):
                for cp in s2_copies(rel):
                    cp.wait_recv()

            def total_in(rows):
                g = gin_ref[rows, :]
                for rel in range(1, n_chips):
                    g = g + s2_in[rel - 1, rows, :].astype(F32)
                gin_ref[rows, :] = g

            def total_out(rows):
                g = gout_ref[rows, :]
                for rel in range(1, n_chips):
                    g = g + s2_out[rel - 1, rows, :].astype(F32)
                gout_ref[rows, :] = g

            rows_loop(D_MODEL, total_in)
            rows_loop(WOUT_ROWS, total_out)

            for rel in range(1, N_DEV):
                small_copy(rel).wait_recv()
            my_flat = _flat(me)
            g = land_s[my_flat ^ 0]
            for dev in range(1, N_DEV):
                g = g + land_s[my_flat ^ dev]
            gs_ref[...] = g

            for pair in range(n_chips):
                s1_in_copy(pair).wait_send()
                s1_out_copy(pair).wait_send()
            for rel in range(1, n_chips):
                for cp in s2_copies(rel):
                    cp.wait_send()
            for rel in range(1, N_DEV):
                small_copy(rel).wait_send()

    place_x, place_y, place_c = _my_place()
    my_chip = 2 * place_x + place_y
    order = jnp.stack([2 * (my_chip ^ rel) + core for rel in rel_of_pair
                       for core in (1 - place_c, place_c)]).astype(jnp.int32)

    whole = lambda: pl.BlockSpec(memory_space=pltpu.VMEM)
    in_blocks = lambda n: pltpu.VMEM((n, D_MODEL, COL_BLOCK), BF16)
    out_blocks = lambda n: pltpu.VMEM((n, WOUT_ROWS, D_MODEL), BF16)
    grid_spec = pltpu.PrefetchScalarGridSpec(
        num_scalar_prefetch=1, grid=(N_DEV,),
        in_specs=[pl.BlockSpec((D_MODEL, SEQ), lambda t, order: (0, 0), pipeline_mode=pl.Buffered(1)),
                  pl.BlockSpec((None, SEQ, COL_BLOCK), lambda t, order: (order[t], 0, 0)), whole(), whole()],
        out_specs=(whole(), whole(), whole()),
        scratch_shapes=[pltpu.VMEM((D_MODEL, COL_BLOCK), F32), in_blocks(n_chips), in_blocks(n_chips),
                        out_blocks(n_chips), in_blocks(n_chips - 1), out_blocks(n_chips - 1),
                        in_blocks(n_chips - 1), out_blocks(n_chips - 1),
                        pltpu.VMEM((N_DEV, SMALL_ROWS, LANES), F32),
                        pltpu.SemaphoreType.DMA((21,)), pltpu.SemaphoreType.DMA((21,))])
    return pl.pallas_call(
        body, name="weights_exchange", grid_spec=grid_spec,
        out_shape=(jax.ShapeDtypeStruct((D_MODEL, COL_BLOCK), F32), jax.ShapeDtypeStruct((WOUT_ROWS, D_MODEL), F32),
                   jax.ShapeDtypeStruct((SMALL_ROWS, LANES), F32)),
        compiler_params=_params(("arbitrary",)),
    )(order, hn_t, dproj_b, dwout_p.reshape(n_chips, 2, WOUT_ROWS, D_MODEL), small_p)


def _adamw(w, g, m, v):
    m = ADAM_B1 * m + (1.0 - ADAM_B1) * g
    v = ADAM_B2 * v + (1.0 - ADAM_B2) * (g * g)
    m_hat = m / (1.0 - ADAM_B1 ** ADAM_STEP)
    v_hat = v / (1.0 - ADAM_B2 ** ADAM_STEP)
    delta = -ADAM_LR * (m_hat / (jnp.sqrt(v_hat) + ADAM_EPS) + ADAM_WD * w)
    return delta, m, v


def _adamw_update(grads, weights, m_old, v_old):
    rb = 256

    def body(*refs):
        g_refs, w_refs, m_refs, v_refs = refs[0:3], refs[3:6], refs[6:9], refs[9:12]
        d_refs, nm_refs, nv_refs = refs[12:15], refs[15:18], refs[18:21]
        for k in range(3):
            n_rows = g_refs[k].shape[0]
            step_rows = min(rb, n_rows)

            def step(b, carry, k=k, step_rows=step_rows):
                rows = pl.ds(pl.multiple_of(b * step_rows, 8), step_rows)
                delta, nm, nv = _adamw(w_refs[k][rows, :], g_refs[k][rows, :], m_refs[k][rows, :], v_refs[k][rows, :])
                d_refs[k][rows, :] = delta
                nm_refs[k][rows, :] = nm
                nv_refs[k][rows, :] = nv
                return carry

            lax.fori_loop(0, n_rows // step_rows, step, 0)

    shapes = tuple(jax.ShapeDtypeStruct(g.shape, F32) for g in grads)
    vm = lambda: pl.BlockSpec(memory_space=pltpu.VMEM)
    outs = pl.pallas_call(
        body, name="adamw_update", out_shape=shapes * 3,
        in_specs=[vm() for _ in range(12)], out_specs=tuple(vm() for _ in range(9)),
        compiler_params=_params(),
    )(*grads, *weights, *m_old, *v_old)
    return outs[0:3], outs[3:6], outs[6:9]


def _pack_small(mix, attn, hgrn, lb, final, loss=None):
    def rows8(a):
        a = a.reshape(-1, LANES)
        return jnp.pad(a, ((0, 8 - a.shape[0]), (0, 0)))
    last = jnp.zeros((8, LANES), F32) if loss is None else jnp.pad(loss.reshape(1, 1), ((0, 7), (0, LANES - 1)))
    return jnp.concatenate([rows8(mix), rows8(attn), rows8(hgrn), rows8(lb), rows8(final), last], axis=0)


def _unpack_small(slab):
    return (slab[ROW_MIX:ROW_MIX + 8].reshape(1, D_MODEL), slab[ROW_ATTN:ROW_ATTN + 4].reshape(1, ATTN_WIDTH),
            slab[ROW_HGRN:ROW_HGRN + 4].reshape(1, HGRN_WIDTH), slab[ROW_LB:ROW_LB + 8].reshape(2, HGRN_WIDTH),
            slab[ROW_FINAL:ROW_FINAL + 8].reshape(D_MODEL))


def _rope(pos_row):
    j = np.arange(ROPE_ROWS)
    inv = np.where(j < ROPE_HALF, ROPE_THETA ** (-(j % ROPE_HALF) * (2.0 / ROPE_DIMS)), 0.0)
    e = np.arange(LANES) % HEAD_DIM
    hit = (j[:, None] == (e % ROPE_HALF)[None, :]) & (j[:, None] < ROPE_HALF)
    sel = np.stack([hit & (e < ROPE_DIMS), hit & (e >= ROPE_HALF) & (e < ROPE_DIMS),
                    -1.0 * (hit & (e < ROPE_HALF))]).astype(np.float32)
    return _rope_tables(pos_row, jnp.asarray(inv.astype(np.float32).reshape(ROPE_ROWS, 1)),
                        jnp.asarray(sel, dtype=BF16))


def _local_step(x, proj, qkv_sorted, w_in_g, w_out_g, tables, mix_w, attn_w, hgrn_w, lb_raw, final_w, target):
    rc, rsa, rsb = tables
    attn_o, lse = _attn_fwd_fused(qkv_sorted)
    rec, states = _hgrn_fwd(proj, lb_raw)

    (dx2, d_o, delta, d_ag, d_rec, d_hg, dwout_p, d_final, d_attn_w, d_hgrn_w, loss) = _mid(
        attn_o, rec, proj, x, target, w_out_g, attn_w, hgrn_w, final_w.reshape(1, D_MODEL))

    dqkv = _attn_bwd_fused(qkv_sorted, d_o, lse, delta)
    d_hq, d_hf, d_hi, d_lb = _hgrn_bwd(proj, lb_raw, d_rec, states)

    grad_x, dproj_b, d_mix = _in_proj_bwd_rows(
        (dqkv[0], dqkv[1], dqkv[2], d_ag, d_hq, d_hf, d_hi, d_hg), w_in_g, x, dx2, mix_w, rc, rsa, rsb)
    small_p = _pack_small(d_mix, d_attn_w, d_hgrn_w, d_lb, d_final, loss)
    return grad_x, dproj_b, dwout_p, small_p


def kernel(x, positions, w_in, w_out, mix_norm_w, attn_out_norm_w, hgrn_out_norm_w, hgrn_lb_raw, final_norm_w, loss_target, m_w_in, m_w_out, m_mix_norm_w, m_attn_out_norm_w, m_hgrn_out_norm_w, m_hgrn_lb_raw, m_final_norm_w, v_w_in, v_w_out, v_mix_norm_w, v_attn_out_norm_w, v_hgrn_out_norm_w, v_hgrn_lb_raw, v_final_norm_w):
    tables = _rope(positions)
    proj, hn_t, w_in_g, w_out_g, qkv_sorted = _gather_project(x[0], mix_norm_w, w_in[0], w_out[0], *tables)
    grad_x, dproj_b, dwout_p, small_p = _local_step(
        x[0], proj, qkv_sorted, w_in_g, w_out_g, tables, mix_norm_w, attn_out_norm_w, hgrn_out_norm_w,
        hgrn_lb_raw, final_norm_w, loss_target[0])
    g_in, g_out, g_s = _weights_exchange(hn_t, dproj_b, dwout_p, small_p)

    w_s = _pack_small(mix_norm_w, attn_out_norm_w, hgrn_out_norm_w, hgrn_lb_raw, final_norm_w)
    m_s = _pack_small(m_mix_norm_w, m_attn_out_norm_w, m_hgrn_out_norm_w, m_hgrn_lb_raw, m_final_norm_w)
    v_s = _pack_small(v_mix_norm_w, v_attn_out_norm_w, v_hgrn_out_norm_w, v_hgrn_lb_raw, v_final_norm_w)
    (d_in, d_out, d_s), (nm_in, nm_out, nm_s), (nv_in, nv_out, nv_s) = _adamw_update(
        (g_in, g_out, g_s), (w_in[0], w_out[0], w_s), (m_w_in[0], m_w_out[0], m_s), (v_w_in[0], v_w_out[0], v_s))

    loss = g_s[ROW_LOSS, 0]
    return (loss, grad_x[None], g_in[None], g_out[None], *_unpack_small(g_s),
            d_in[None], d_out[None], *_unpack_small(d_s),
            nm_in[None], nm_out[None], *_unpack_small(nm_s),
            nv_in[None], nv_out[None], *_unpack_small(nv_s))
```

```python
import functools

import jax
import jax.numpy as jnp
import numpy as np
from jax import lax
from jax.experimental import pallas as pl
from jax.experimental.pallas import tpu as pltpu

F32 = jnp.float32
BF16 = jnp.bfloat16

SEQ = 4096
D_MODEL = 1024
ATTN_WIDTH = 512
HGRN_WIDTH = 512
HEAD_DIM = 64
HGRN_HEADS = 4
HGRN_DIM = 128
HGRN_CHUNK = 64
N_CHUNKS = SEQ // HGRN_CHUNK
IN_COLS = 4096
COL_BLOCK = 512
N_DEV = 8
WOUT_ROWS = D_MODEL // N_DEV
ATTN_BLOCK = 128
DILATIONS = (1, 4, 16)
ROPE_THETA = 500000.0
ROPE_DIMS = 16
ROPE_HALF = 8
NORM_EPS = 1e-6
NEG_BIG = -1e30
LANES = 128

ADAM_LR = 0.001
ADAM_B1 = 0.9
ADAM_B2 = 0.999
ADAM_EPS = 1e-08
ADAM_WD = 0.01
ADAM_STEP = 10

SMALL_ROWS = 48
ROW_MIX, ROW_ATTN, ROW_HGRN, ROW_LB, ROW_FINAL, ROW_LOSS = 0, 8, 16, 24, 32, 40

VMEM_LIMIT = 56 * 1024 * 1024
MESH = pl.DeviceIdType.MESH


def _mm(a, b):
    return lax.dot_general(a, b, (((1,), (0,)), ((), ())), preferred_element_type=F32)


def _mm_nt(a, b):
    return lax.dot_general(a, b, (((1,), (1,)), ((), ())), preferred_element_type=F32)


def _mm_tn(a, b):
    return lax.dot_general(a, b, (((0,), (0,)), ((), ())), preferred_element_type=F32)


def _mm_exact(a, b):
    return lax.dot_general(a, b, (((1,), (0,)), ((), ())), preferred_element_type=F32,
                           precision=lax.Precision.HIGHEST)


def _sigmoid(v):
    return 1.0 / (1.0 + jnp.exp(-v))


def _params(sem=None, **kw):
    return pltpu.CompilerParams(dimension_semantics=sem, vmem_limit_bytes=VMEM_LIMIT, **kw)


def _my_place():
    return lax.axis_index("x"), lax.axis_index("y"), lax.axis_index("c")


def _peer(place, rel):
    x, y, c = place
    return (x ^ ((rel >> 2) & 1), y ^ ((rel >> 1) & 1), c ^ (rel & 1))


def _flat(place):
    x, y, c = place
    return 4 * x + 2 * y + c


ROPE_ROWS = 16


def _rope_tables(pos_row, inv_freq_col, selectors):
    def body(pos_ref, invf_ref, sel_ref, c_ref, sa_ref, sb_ref):
        ang = pos_ref[...].astype(F32) * invf_ref[...]
        cos, sin = jnp.cos(ang), jnp.sin(ang)

        def spread(v, sel):
            hi = v.astype(BF16)
            r1 = v - hi.astype(F32)
            mid = r1.astype(BF16)
            lo = (r1 - mid.astype(F32)).astype(BF16)
            return _mm_tn(hi, sel) + _mm_tn(mid, sel) + _mm_tn(lo, sel)

        e = lax.broadcasted_iota(jnp.int32, (1, LANES), 1) & (HEAD_DIM - 1)
        c_ref[...] = spread(cos, sel_ref[0]) + jnp.where(e < ROPE_DIMS, 0.0, 1.0)
        sa_ref[...] = spread(sin, sel_ref[1])
        sb_ref[...] = spread(sin, sel_ref[2])

    tab = jax.ShapeDtypeStruct((SEQ, LANES), F32)
    vm = lambda: pl.BlockSpec(memory_space=pltpu.VMEM)
    return pl.pallas_call(
        body, name="rope_tables", out_shape=(tab, tab, tab),
        in_specs=[vm(), vm(), vm()], out_specs=(vm(), vm(), vm()), compiler_params=_params(),
    )(pos_row, inv_freq_col, selectors)


def _per_slab(fn, t):
    return jnp.concatenate([fn(t[:, LANES * s:LANES * (s + 1)]) for s in range(t.shape[1] // LANES)], axis=1)


def _rot(t, c, sa, sb):
    return _per_slab(lambda u: u * c + pltpu.roll(u, ROPE_HALF, 1) * sa + pltpu.roll(u, LANES - ROPE_HALF, 1) * sb, t)


def _rot_transposed(g, c, sa, sb):
    return _per_slab(
        lambda u: u * c + pltpu.roll(u * sa, LANES - ROPE_HALF, 1) + pltpu.roll(u * sb, ROPE_HALF, 1), g)


def _gather_project(x, mix_w, w_in, w_out, rc, rsa, rsb):
    tm = 1024
    n_tiles = SEQ // tm
    arrival_of_step = (None, 0, 1, 2, 4, 5, 3, 6)

    def body(order_ref, x_ref, w_ref, win_ref, wout_ref, c_ref, sa_ref, sb_ref,
             proj_ref, hnt_ref, gin_hbm, gout_hbm, qkv_hbm,
             hn_s, w_land, wout_land, stage, sort_stage, send_sems, recv_sems, local_sems, sort_sems):
        g, i = pl.program_id(0), pl.program_id(1)
        me = _my_place()
        x_, y_, c_ = me
        sibling = (x_, y_, 1 - c_)
        chips = [(1 - x_, y_), (x_, 1 - y_), (1 - x_, 1 - y_)]

        def slab(which, place):
            idx = _flat(place)
            if which == 0:
                return w_land.at[idx]
            return wout_land.at[pl.ds(pl.multiple_of(idx * WOUT_ROWS, WOUT_ROWS), WOUT_ROWS), :]

        def remote(which, k, ref, to, src=None):
            return pltpu.make_async_remote_copy(
                src_ref=ref if src is None else src, dst_ref=ref, send_sem=send_sems.at[8 * which + k],
                recv_sem=recv_sems.at[8 * which + k], device_id=to, device_id_type=MESH)

        def copy(which, k, block, to, src=None):
            return remote(which, k, slab(which, block), to, src)

        def half(which, place, part):
            n = (D_MODEL if which == 0 else WOUT_ROWS) // 2
            if which == 0:
                return w_land.at[_flat(place), pl.ds(n * part, n), :]
            return wout_land.at[pl.ds(pl.multiple_of(_flat(place) * WOUT_ROWS + n * part, n), n), :]

        def first_copies(which):
            src = stage if which == 0 else None
            return ([copy(which, 0, me, sibling, src)]
                    + [copy(which, 1 + j, me, (*chips[j], c_), src) for j in range(2)])

        def relay(which, part):
            frm, to = (chips[1], chips[0]) if part == 0 else (chips[0], chips[1])
            return remote(which, 3 if part == 0 else 7, half(which, (*frm, c_), part), (*to, c_))

        def two_hop_half(which, part):
            return remote(which, 3 if part == 0 else 7, half(which, (*chips[2], c_), part), me)

        def pass_on(which, j):
            return copy(which, 4 + j, (*chips[j], c_), sibling)

        def arrival(which, k):
            if k == 0:
                return copy(which, 0, sibling, me)
            if k <= 2:
                return copy(which, k, (*chips[k - 1], c_), me)
            return copy(which, k, (*chips[k - 4], 1 - c_), me)

        def to_hbm(step):
            idx = order_ref[step]
            cols = pl.ds(pl.multiple_of(idx * COL_BLOCK, COL_BLOCK), COL_BLOCK)
            return pltpu.make_async_copy(w_land.at[idx], gin_hbm.at[:, cols], local_sems.at[step])

        @pl.when((g == 0) & (i == 0))
        def _():
            stage[...] = win_ref[...].astype(BF16)
            w_land[_flat(me)] = stage[...]
            wout_land[pl.ds(pl.multiple_of(_flat(me) * WOUT_ROWS, WOUT_ROWS), WOUT_ROWS), :] = (
                wout_ref[...].astype(BF16))
            for cp in first_copies(0) + first_copies(1)[:1]:
                cp.start()
            to_hbm(0).start()

        for step, k in enumerate(arrival_of_step):
            if k is None:
                continue

            @pl.when((g == step) & (i == 0))
            def _(k=k, step=step):
                if k == 3:
                    two_hop_half(0, 0).wait_recv()
                    two_hop_half(0, 1).wait_recv()
                else:
                    arrival(0, k).wait_recv()
                to_hbm(step).start()
                if 1 <= k <= 3:
                    pass_on(0, k - 1).start()
                if k == 1:
                    relay(0, 1).start()
                    for cp in first_copies(1)[1:]:
                        cp.start()
                if k == 2:
                    relay(0, 0).start()
                if k in (4, 5):
                    arrival(1, k - 3).wait_recv()
                    relay(1, 5 - k).start()

        rows = pl.ds(pl.multiple_of(i * tm, tm), tm)

        @pl.when(g == 0)
        def _():
            xf = x_ref[...]
            ms = jnp.mean(xf * xf, axis=-1, keepdims=True)
            hn = xf * lax.rsqrt(ms + NORM_EPS) * w_ref[...]
            hnt_ref[...] = hn.T.astype(BF16)
            hn_s[rows, :] = hn.astype(BF16)

        group = order_ref[g]

        def sorted_copy(tile_value):
            per = tm // SORT_RESIDUES
            cols = pl.ds(pl.multiple_of(group * COL_BLOCK, COL_BLOCK), COL_BLOCK)
            buf = i % 2

            def out_copies(tile, b):
                return [pltpu.make_async_copy(
                    sort_stage.at[b, :, r, :], qkv_hbm.at[r, pl.ds(pl.multiple_of(tile * per, per), per), cols],
                    sort_sems.at[b, r]) for r in range(SORT_RESIDUES)]

            @pl.when(i >= 2)
            def _():
                for copy in out_copies(i - 2, buf):
                    copy.wait()

            sort_stage[buf] = tile_value.reshape(per, SORT_RESIDUES, COL_BLOCK)
            for copy in out_copies(i, buf):
                copy.start()

            @pl.when(i == n_tiles - 1)
            def _():
                for copy in out_copies(i - 1, 1 - buf) + out_copies(i, buf):
                    copy.wait()

        @pl.when(group < 2)
        def _():
            rotated = _rot(_mm(hn_s[rows, :], w_land[group]), c_ref[...], sa_ref[...], sb_ref[...])
            proj_ref[...] = rotated
            sorted_copy(rotated)

        @pl.when(group == 2)
        def _():
            value = _mm(hn_s[rows, :], w_land[group])
            proj_ref[...] = value
            sorted_copy(value)

        @pl.when(group > 2)
        def _():
            proj_ref[...] = _mm(hn_s[rows, :], w_land[group])

        @pl.when((g == N_DEV - 1) & (i == n_tiles - 1))
        def _():
            pass_on(1, 0).start()
            pass_on(1, 1).start()
            two_hop_half(1, 0).wait_recv()
            two_hop_half(1, 1).wait_recv()
            pass_on(1, 2).start()
            for k in (0, 4, 5, 6):
                arrival(1, k).wait_recv()
            for which in (0, 1):
                for cp in (first_copies(which) + [relay(which, part) for part in range(2)]
                           + [pass_on(which, j) for j in range(3)]):
                    cp.wait_send()
            wout_copy = pltpu.make_async_copy(wout_land, gout_hbm, local_sems.at[N_DEV])
            wout_copy.start()
            for step in range(N_DEV):
                to_hbm(step).wait()
            wout_copy.wait()

    me = _my_place()
    x_, y_, c_ = me
    chips = [(1 - x_, y_), (x_, 1 - y_), (1 - x_, 1 - y_)]
    order = jnp.stack([_flat(p) for p in (
        me, (x_, y_, 1 - c_), (*chips[0], c_), (*chips[1], c_), (*chips[0], 1 - c_), (*chips[1], 1 - c_),
        (*chips[2], c_), (*chips[2], 1 - c_))]).astype(jnp.int32)

    first_sweep = lambda g, i, order: (jnp.where(g == 0, i, n_tiles - 1), 0)
    tab = pl.BlockSpec((tm, LANES), lambda g, i, order: (jnp.where(order[g] < 2, i, 0), 0))
    whole = lambda: pl.BlockSpec(memory_space=pltpu.VMEM)
    grid_spec = pltpu.PrefetchScalarGridSpec(
        num_scalar_prefetch=1, grid=(N_DEV, n_tiles),
        in_specs=[pl.BlockSpec((tm, D_MODEL), first_sweep),
                  pl.BlockSpec((1, D_MODEL), lambda g, i, order: (0, 0)),
                  whole(), whole(), tab, tab, tab],
        out_specs=(pl.BlockSpec((None, tm, COL_BLOCK), lambda g, i, order: (order[g], i, 0)),
                   pl.BlockSpec((D_MODEL, tm), lambda g, i, order: (0, jnp.where(g == 0, i, n_tiles - 1))),
                   pl.BlockSpec(memory_space=pl.ANY), pl.BlockSpec(memory_space=pl.ANY),
                   pl.BlockSpec(memory_space=pl.ANY)),
        scratch_shapes=[pltpu.VMEM((SEQ, D_MODEL), BF16),
                        pltpu.VMEM((N_DEV, D_MODEL, COL_BLOCK), BF16),
                        pltpu.VMEM((D_MODEL, D_MODEL), BF16),
                        pltpu.VMEM((D_MODEL, COL_BLOCK), BF16),
                        pltpu.VMEM((2, tm // SORT_RESIDUES, SORT_RESIDUES, COL_BLOCK), F32),
                        pltpu.SemaphoreType.DMA((16,)), pltpu.SemaphoreType.DMA((16,)),
                        pltpu.SemaphoreType.DMA((N_DEV + 1,)), pltpu.SemaphoreType.DMA((2, SORT_RESIDUES))])
    proj, hn_t, w_in_g, w_out_g, qkv_sorted = pl.pallas_call(
        body, name="gather_project", grid_spec=grid_spec,
        out_shape=(jax.ShapeDtypeStruct((N_DEV, SEQ, COL_BLOCK), F32), jax.ShapeDtypeStruct((D_MODEL, SEQ), BF16),
                   jax.ShapeDtypeStruct((D_MODEL, IN_COLS), BF16), jax.ShapeDtypeStruct((D_MODEL, D_MODEL), BF16),
                   jax.ShapeDtypeStruct((SORT_RESIDUES, SORT_ROWS, 3 * COL_BLOCK), F32)),
        compiler_params=_params(("arbitrary", "arbitrary")),
    )(order, x, mix_w, w_in, w_out, rc, rsa, rsb)
    return proj, hn_t, w_in_g, w_out_g, qkv_sorted.reshape(SEQ, 3 * COL_BLOCK)


SCORE_SCALE = HEAD_DIM ** -0.5
ATTN_GROUP_FWD = 16
ATTN_GROUP_BWD = 8
BLOCKS_PER_PATTERN = SEQ // ATTN_BLOCK
SORT_RESIDUES = 16
SORT_ROWS = SEQ // SORT_RESIDUES


def _write_band_bias(bias_ref):
    row = lax.broadcasted_iota(jnp.int32, (2 * ATTN_BLOCK, 2 * ATTN_BLOCK), 0) & (ATTN_BLOCK - 1)
    col = lax.broadcasted_iota(jnp.int32, (2 * ATTN_BLOCK, 2 * ATTN_BLOCK), 1)
    for pi, d in enumerate(DILATIONS):
        per = SORT_RESIDUES // d
        ahead = per * (row % (8 * d) - col % (16 * d)) + (row // (8 * d) - col // (16 * d))
        dist = ATTN_BLOCK + ahead
        bias_ref[2 * pi] = jnp.where((dist >= 0) & (dist <= ATTN_BLOCK), 0.0, NEG_BIG)
        bias_ref[2 * pi + 1] = jnp.where(ahead >= 0, 0.0, NEG_BIG)


def _head0_lanes():
    return lax.broadcasted_iota(jnp.int32, (ATTN_BLOCK, LANES), 1) < HEAD_DIM


def _stack_heads(t, h0):
    return jnp.concatenate([jnp.where(h0, t, 0.0), jnp.where(h0, 0.0, t)], axis=0).astype(BF16)


def _block_runs(i, d):
    nblk = BLOCKS_PER_PATTERN // d
    r, n = i // nblk, i % nblk
    kn = jnp.maximum(n - 1, 0)
    rows, keys = [], []
    for c in range(SORT_RESIDUES // d):
        base = SORT_ROWS * (c * d + r)
        rows.append(pl.ds(pl.multiple_of(base + 8 * d * n, 8), 8 * d))
        keys.append(pl.ds(pl.multiple_of(base + 8 * d * kn, 8), 16 * d))
    return rows, keys, (n == 0).astype(jnp.int32)


def _take(ref, runs):
    return jnp.concatenate([ref[run, :] for run in runs], axis=0)


def _put(ref, runs, value, add=False):
    at = 0
    for run in runs:
        piece = value[at:at + run.size]
        if add:
            ref[run, :] += piece
        else:
            ref[run, :] = piece
        at += run.size


def _sort_copies(src_hbm, lane_block, dst_ref, sem_ref):
    lanes = pl.ds(pl.multiple_of(LANES * lane_block, LANES), LANES)
    return [pltpu.make_async_copy(src_hbm.at[:, r, lanes], dst_ref.at[pl.ds(SORT_ROWS * r, SORT_ROWS), :],
                                  sem_ref.at[r]) for r in range(SORT_RESIDUES)]


def _unsort_copies(src_ref, dst_hbm, lane_block, sem_ref):
    lanes = pl.ds(pl.multiple_of(LANES * lane_block, LANES), LANES)
    return [pltpu.make_async_copy(src_ref.at[pl.ds(SORT_ROWS * r, SORT_ROWS), :], dst_hbm.at[:, r, lanes],
                                  sem_ref.at[r]) for r in range(SORT_RESIDUES)]


def _for_each_group(d, n_group, load, compute, store):
    def group(g, carry):
        items = [load(*_block_runs(g * n_group + u, d)) for u in range(n_group)]
        results = [compute(item) for item in items]
        for item, res in zip(items, results):
            store(item, res)
        return carry

    lax.fori_loop(0, BLOCKS_PER_PATTERN // n_group, group, 0)


def _attn_fwd_fused(qkv_sorted):
    n_pat = len(DILATIONS)
    tile2 = (2 * ATTN_BLOCK, LANES)

    def body(q_ref, k_ref, v_ref, o_hbm, lse_ref, o_slots, m_acc, l_acc, bias_ref, out_sem):
        step, n_steps = pl.program_id(0), pl.num_programs(0)
        pl.when(step == 0)(lambda: _write_band_bias(bias_ref))
        slot = step % 2
        o_acc = o_slots.at[slot]
        h0 = _head0_lanes()
        for pi, d in enumerate(DILATIONS):
            first, last = pi == 0, pi == n_pat - 1

            def load(rows, keys, which, first=first, pi=pi):
                item = dict(rows=rows, keys=keys, which=2 * pi + which)
                if not first:
                    item.update(o=_take(o_acc, rows), m=[_take(m_acc.at[h], rows) for h in range(2)],
                                l=[_take(l_acc.at[h], rows) for h in range(2)])
                return item

            def compute(item, first=first):
                kb = _take(k_ref, item["keys"]).astype(BF16)
                vb = _take(v_ref, item["keys"]).astype(BF16)
                s = _mm_nt(_stack_heads(_take(q_ref, item["rows"]) * SCORE_SCALE, h0), kb) + bias_ref[item["which"]]
                mb = jnp.max(s, axis=-1, keepdims=True)
                if first:
                    p = jnp.exp(s - mb)
                    mn = jnp.broadcast_to(mb, tile2)
                else:
                    m_old = jnp.concatenate(item["m"], axis=0)
                    mn = jnp.maximum(m_old, mb)
                    alpha = jnp.exp(m_old - mn)
                    p = jnp.exp(s - jnp.concatenate([mn, mn], axis=1))
                ls = jnp.sum(p, axis=-1, keepdims=True)
                pv = _mm(p.astype(BF16), vb)
                if first:
                    return pv, mn, jnp.broadcast_to(ls, tile2)
                o_old = jnp.concatenate([item["o"], item["o"]], axis=0)
                return alpha * o_old + pv, mn, alpha * jnp.concatenate(item["l"], axis=0) + ls

            def store(item, res, last=last):
                rows = item["rows"]
                (o0, o1), (m0, m1), (l0, l1) = ((a[:ATTN_BLOCK], a[ATTN_BLOCK:]) for a in res)
                if last:
                    _put(o_acc, rows, jnp.where(h0, o0 / l0, o1 / l1))
                    _put(lse_ref, rows, jnp.where(h0, m0 + jnp.log(l0), m1 + jnp.log(l1)))
                else:
                    _put(o_acc, rows, jnp.where(h0, o0, o1))
                    for h, (m, l) in enumerate(((m0, l0), (m1, l1))):
                        _put(m_acc.at[h], rows, m)
                        _put(l_acc.at[h], rows, l)

            _for_each_group(d, ATTN_GROUP_FWD, load, compute, store)

        def copies_out(of_step):
            return _unsort_copies(o_slots.at[of_step % 2], o_hbm, of_step, out_sem.at[of_step % 2])

        @pl.when(step > 0)
        def _():
            for copy in copies_out(step - 1):
                copy.wait()

        for copy in copies_out(step):
            copy.start()

        @pl.when(step == n_steps - 1)
        def _():
            for copy in copies_out(step):
                copy.wait()

    slab = lambda g: pl.BlockSpec((SEQ, LANES), functools.partial(lambda hp, g: (0, 4 * g + hp), g=g))
    wide = jax.ShapeDtypeStruct((SEQ, ATTN_WIDTH), F32)
    o_rows, lse = pl.pallas_call(
        body, name="attn_fwd", grid=(4,),
        out_shape=(jax.ShapeDtypeStruct((SORT_ROWS, SORT_RESIDUES, ATTN_WIDTH), F32), wide),
        in_specs=[slab(0), slab(1), slab(2)], out_specs=(pl.BlockSpec(memory_space=pl.ANY), slab(0)),
        scratch_shapes=[pltpu.VMEM((2, SEQ, LANES), F32), pltpu.VMEM((2, SEQ, LANES), F32),
                        pltpu.VMEM((2, SEQ, LANES), F32),
                        pltpu.VMEM((2 * len(DILATIONS), 2 * ATTN_BLOCK, 2 * ATTN_BLOCK), F32),
                        pltpu.SemaphoreType.DMA((2, SORT_RESIDUES))],
        compiler_params=_params(("arbitrary",)),
    )(qkv_sorted, qkv_sorted, qkv_sorted)
    return o_rows.reshape(SEQ, ATTN_WIDTH), lse


def _attn_bwd_fused(qkv_sorted, d_out, lse_sorted, delta):
    def body(q_ref, k_ref, v_ref, do_hbm, lse_ref, del_hbm, dq_hbm, dk_hbm, dv_hbm,
             in_slots, out_slots, bias_ref, in_sem, out_sem):
        step, n_steps = pl.program_id(0), pl.num_programs(0)
        slot = step % 2

        def copies_in(of_step):
            s = of_step % 2
            return [copy for j, hbm in enumerate((do_hbm, del_hbm))
                    for copy in _sort_copies(hbm, of_step, in_slots.at[s, j], in_sem.at[s, j])]

        def copies_out(of_step):
            s = of_step % 2
            return [copy for j, hbm in enumerate((dq_hbm, dk_hbm, dv_hbm))
                    for copy in _unsort_copies(out_slots.at[s, j], hbm, of_step, out_sem.at[s, j])]

        @pl.when(step == 0)
        def _():
            for copy in copies_in(step):
                copy.start()
            _write_band_bias(bias_ref)

        @pl.when(step + 1 < n_steps)
        def _():
            for copy in copies_in(step + 1):
                copy.start()

        do_s, del_s = in_slots.at[slot, 0], in_slots.at[slot, 1]
        dq_s, dk_s, dv_s = (out_slots.at[slot, j] for j in range(3))
        dk_s[...] = jnp.zeros_like(dk_s)
        dv_s[...] = jnp.zeros_like(dv_s)
        for copy in copies_in(step):
            copy.wait()
        h0 = _head0_lanes()
        for pi, d in enumerate(DILATIONS):
            first = pi == 0

            def load(rows, keys, which, pi=pi):
                return dict(rows=rows, keys=keys, q=_take(q_ref, rows), g=_take(do_s, rows),
                            lse=_take(lse_ref, rows), delta=_take(del_s, rows),
                            k=_take(k_ref, keys).astype(BF16), v=_take(v_ref, keys).astype(BF16),
                            bias=bias_ref[2 * pi + which])

            def per_head(t):
                swapped = pltpu.roll(t, HEAD_DIM, 1)
                both = jnp.concatenate([jnp.where(h0, t, swapped), jnp.where(h0, swapped, t)], axis=0)
                return jnp.concatenate([both, both], axis=1)

            def compute(item):
                q2, g2 = _stack_heads(item["q"] * SCORE_SCALE, h0), _stack_heads(item["g"], h0)
                s = _mm_nt(q2, item["k"]) + item["bias"]
                p = jnp.exp(s - per_head(item["lse"]))
                dp = _mm_nt(g2, item["v"])
                ds = (p * (dp - per_head(item["delta"]))).astype(BF16)
                dq2 = _mm(ds, item["k"])
                dq = jnp.where(h0, dq2[:ATTN_BLOCK], dq2[ATTN_BLOCK:]) * SCORE_SCALE
                return dq, _mm_tn(ds, q2), _mm_tn(p.astype(BF16), g2)

            def store(item, res, first=first):
                _put(dq_s, item["rows"], res[0], add=not first)
                _put(dk_s, item["keys"], res[1], add=True)
                _put(dv_s, item["keys"], res[2], add=True)

            _for_each_group(d, ATTN_GROUP_BWD, load, compute, store)

        @pl.when(step > 0)
        def _():
            for copy in copies_out(step - 1):
                copy.wait()

        for copy in copies_out(step):
            copy.start()

        @pl.when(step == n_steps - 1)
        def _():
            for copy in copies_out(step):
                copy.wait()

    slab = lambda g: pl.BlockSpec((SEQ, LANES), functools.partial(lambda hp, g: (0, 4 * g + hp), g=g))
    anywhere = pl.BlockSpec(memory_space=pl.ANY)
    by_residue = (SORT_ROWS, SORT_RESIDUES, ATTN_WIDTH)
    grads = pl.pallas_call(
        body, name="attn_bwd", grid=(4,), out_shape=(jax.ShapeDtypeStruct(by_residue, F32),) * 3,
        scratch_shapes=[pltpu.VMEM((2, 2, SEQ, LANES), F32), pltpu.VMEM((2, 3, SEQ, LANES), F32),
                        pltpu.VMEM((2 * len(DILATIONS), 2 * ATTN_BLOCK, 2 * ATTN_BLOCK), F32),
                        pltpu.SemaphoreType.DMA((2, 2, SORT_RESIDUES)), pltpu.SemaphoreType.DMA((2, 3, SORT_RESIDUES))],
        in_specs=[slab(0), slab(1), slab(2), anywhere, slab(0), anywhere], out_specs=(anywhere,) * 3,
        compiler_params=_params(("arbitrary",)),
    )(qkv_sorted, qkv_sorted, qkv_sorted, d_out.reshape(by_residue), lse_sorted, delta.reshape(by_residue))
    return tuple(g.reshape(SEQ, ATTN_WIDTH) for g in grads)


def _hgrn_lower_bound(lb_ref):
    r0, r1 = lb_ref[0:1, :], lb_ref[1:2, :]
    mx = jnp.maximum(r0, r1)
    e0, e1 = jnp.exp(r0 - mx), jnp.exp(r1 - mx)
    return e0 / (e0 + e1)


def _hgrn_gates(hq, hf, lb):
    sq = _sigmoid(hq)
    sg = _sigmoid(hf)
    f = lb + (1.0 - lb) * sg
    return hq * sq, sq, sg, f, 1.0 - f, jnp.log(f)


HGRN_PAIR = 4
HGRN_SEQ_BLOCK = 1024
HGRN_GROUP = 4
HGRN_ROWS = HGRN_GROUP * HGRN_CHUNK


def _hgrn_specs(reverse):
    n_blocks = SEQ // HGRN_SEQ_BLOCK
    width = HGRN_PAIR * HGRN_DIM
    blk = (lambda s: n_blocks - 1 - s) if reverse else (lambda s: s)
    cols = lambda g: pl.BlockSpec((None, HGRN_SEQ_BLOCK, width), functools.partial(lambda p, s, g: (g, blk(s), p), g=g))
    pair = pl.BlockSpec((HGRN_SEQ_BLOCK, width), lambda p, s: (blk(s), p))
    lb = pl.BlockSpec((2, width), lambda p, s: (0, p))
    states = pl.BlockSpec((HGRN_PAIR, HGRN_SEQ_BLOCK // HGRN_CHUNK, HGRN_DIM, HGRN_DIM),
                          lambda p, s: (p, blk(s), 0, 0))
    return cols, pair, lb, states


def _chunk_masks():
    ri = lax.broadcasted_iota(jnp.int32, (HGRN_ROWS, HGRN_ROWS), 0)
    ci = lax.broadcasted_iota(jnp.int32, (HGRN_ROWS, HGRN_ROWS), 1)
    same = (ri // HGRN_CHUNK) == (ci // HGRN_CHUNK)
    return same, same & (ri >= ci), same & (ri <= ci)


def _mm_select(sel, v):
    hi = v.astype(BF16)
    r1 = v - hi.astype(F32)
    mid = r1.astype(BF16)
    lo = (r1 - mid.astype(F32)).astype(BF16)
    return _mm(sel, hi) + _mm(sel, mid) + _mm(sel, lo)


def _head_cols(a, h):
    return a[:, HGRN_DIM * h:HGRN_DIM * (h + 1)]


def _hgrn_fwd(proj, lb_raw):
    t, rws = HGRN_CHUNK, HGRN_ROWS

    def body(hq_ref, hf_ref, hi_ref, lb_ref, rec_ref, st_ref, state):
        @pl.when(pl.program_id(1) == 0)
        def _():
            state[...] = jnp.zeros_like(state)

        lb = _hgrn_lower_bound(lb_ref)
        same, causal, _ = _chunk_masks()
        sel = jnp.concatenate([causal, same], axis=0).astype(BF16)

        def group(g, sts):
            rows = pl.ds(pl.multiple_of(g * rws, rws), rws)
            q, _, _, _, k, lf = _hgrn_gates(hq_ref[rows, :], hf_ref[rows, :], lb)
            sums = _mm_select(sel, lf)
            cum, last = sums[:rws], sums[rws:]
            qd = (q * jnp.exp(cum)).astype(BF16)
            ki = (k * jnp.exp(-cum)).astype(BF16)
            ke = (k * jnp.exp(last - cum)).astype(BF16)
            vb = hi_ref[rows, :].astype(BF16)
            dec = jnp.exp(last)
            new_sts, recs = [], []
            for h in range(HGRN_PAIR):
                qd_h, ke_h, vb_h = _head_cols(qd, h), _head_cols(ke, h), _head_cols(vb, h)
                att = jnp.where(causal, _mm_nt(qd_h, _head_cols(ki, h)), 0.0).astype(BF16)
                intra = _mm(att, vb_h)
                st = sts[h]
                outs = []
                for c in range(HGRN_GROUP):
                    sl = slice(c * t, (c + 1) * t)
                    st_ref[h, g * HGRN_GROUP + c] = st
                    outs.append(intra[sl] + _mm_nt(qd_h[sl], st.astype(BF16)))
                    st = st * _head_cols(dec[c * t:c * t + 1, :], h) + _mm_tn(vb_h[sl], ke_h[sl])
                new_sts.append(st)
                recs.append(jnp.concatenate(outs, axis=0))
            rec_ref[rows, :] = jnp.concatenate(recs, axis=1)
            return tuple(new_sts)

        sts = lax.fori_loop(0, HGRN_SEQ_BLOCK // rws, group, tuple(state[h] for h in range(HGRN_PAIR)))
        for h in range(HGRN_PAIR):
            state[h] = sts[h]

    cols, pair, lb, states = _hgrn_specs(reverse=False)
    return pl.pallas_call(
        body, name="hgrn_fwd", grid=(HGRN_HEADS // HGRN_PAIR, SEQ // HGRN_SEQ_BLOCK),
        out_shape=(jax.ShapeDtypeStruct((SEQ, HGRN_WIDTH), F32),
                   jax.ShapeDtypeStruct((HGRN_HEADS, N_CHUNKS, HGRN_DIM, HGRN_DIM), F32)),
        in_specs=[cols(4), cols(5), cols(6), lb], out_specs=(pair, states),
        scratch_shapes=[pltpu.VMEM((HGRN_PAIR, HGRN_DIM, HGRN_DIM), F32)],
        compiler_params=_params(("parallel", "arbitrary")),
    )(proj, proj, proj, lb_raw)


def _hgrn_bwd(proj, lb_raw, d_rec, states):
    t, rws = HGRN_CHUNK, HGRN_ROWS

    def body(hq_ref, hf_ref, hi_ref, lb_ref, do_ref, st_ref, dhq_ref, dhf_ref, dhi_ref, dlb_ref,
             dstate, dlb_acc):
        lb = _hgrn_lower_bound(lb_ref)
        same, causal, anti = _chunk_masks()
        sel = jnp.concatenate([causal, same], axis=0).astype(BF16)
        sel_t = jnp.concatenate([anti, same], axis=1).astype(BF16)
        @pl.when(pl.program_id(1) == 0)
        def _():
            dstate[...] = jnp.zeros_like(dstate)
            dlb_acc[...] = jnp.zeros_like(dlb_acc)

        n_groups = HGRN_SEQ_BLOCK // rws
        chunks = [slice(c * t, (c + 1) * t) for c in range(HGRN_GROUP)]

        def group(i, dsts_in):
            g = n_groups - 1 - i
            rows = pl.ds(pl.multiple_of(g * rws, rws), rws)
            hq = hq_ref[rows, :]
            q, sq, sg, f, k, lf = _hgrn_gates(hq, hf_ref[rows, :], lb)
            sums = _mm_select(sel, lf)
            cum, last = sums[:rws], sums[rws:]
            e_cum, e_inv, e_end, dec = jnp.exp(cum), jnp.exp(-cum), jnp.exp(last - cum), jnp.exp(last)
            qd, ki, ke = q * e_cum, k * e_inv, k * e_end
            qdb, kib, keb = qd.astype(BF16), ki.astype(BF16), ke.astype(BF16)
            vb = hi_ref[rows, :].astype(BF16)
            gb = do_ref[rows, :].astype(BF16)

            dsts_out, per_head = [], []
            for h in range(HGRN_PAIR):
                qdb_h, kib_h, keb_h = _head_cols(qdb, h), _head_cols(kib, h), _head_cols(keb, h)
                vb_h, gb_h = _head_cols(vb, h), _head_cols(gb, h)
                att = jnp.where(causal, _mm_nt(qdb_h, kib_h), 0.0).astype(BF16)
                datt = jnp.where(causal, _mm_nt(gb_h, vb_h), 0.0).astype(BF16)
                dv = _mm_tn(att, gb_h)
                dqd = _mm(datt, kib_h)
                dki = _mm_tn(datt, qdb_h)

                decs = [_head_cols(dec[c * t:c * t + 1, :], h) for c in range(HGRN_GROUP)]
                dsts = [None] * HGRN_GROUP
                dst = dsts_in[h]
                for c in reversed(range(HGRN_GROUP)):
                    dsts[c] = dst
                    dst = dst * decs[c] + _mm_tn(gb_h[chunks[c]], qdb_h[chunks[c]])
                dsts_out.append(dst)

                dv_x, dqd_x, dke, dlast_x = [], [], [], []
                for c, sl in enumerate(chunks):
                    st_prev = st_ref[h, g * HGRN_GROUP + c]
                    dstb = dsts[c].astype(BF16)
                    dv_x.append(_mm_nt(keb_h[sl], dstb))
                    dqd_x.append(_mm(gb_h[sl], st_prev.astype(BF16)))
                    dke.append(_mm(vb_h[sl], dstb))
                    ddec = jnp.sum(dsts[c] * st_prev, axis=0, keepdims=True)
                    dlast_x.append(jnp.broadcast_to(ddec * decs[c], (t, HGRN_DIM)))
                per_head.append((dv + jnp.concatenate(dv_x, axis=0), dqd + jnp.concatenate(dqd_x, axis=0),
                                 dki, jnp.concatenate(dke, axis=0), jnp.concatenate(dlast_x, axis=0)))
            dv, dqd, dki, dke, dlast = (jnp.concatenate(list(parts), axis=1) for parts in zip(*per_head))

            dq = dqd * e_cum
            dk = dki * e_inv + dke * e_end
            dke_ke = dke * ke
            dcum = dqd * qd - dki * ki - dke_ke
            dlf = _mm_select(sel_t, jnp.concatenate([dcum, dke_ke], axis=0)) + dlast
            df = dlf / f - dk
            dhq_ref[rows, :] = (dq * (sq * (1.0 + hq * (1.0 - sq)))).astype(BF16)
            dhf_ref[rows, :] = (df * (1.0 - lb) * (sg * (1.0 - sg))).astype(BF16)
            dhi_ref[rows, :] = dv.astype(BF16)
            dlb_acc[...] += jnp.sum(df * (1.0 - sg), axis=0, keepdims=True)
            return tuple(dsts_out)

        dsts = lax.fori_loop(0, n_groups, group, tuple(dstate[h] for h in range(HGRN_PAIR)))
        for h in range(HGRN_PAIR):
            dstate[h] = dsts[h]
        g0 = dlb_acc[...] * lb * (1.0 - lb)
        dlb_ref[...] = jnp.concatenate([g0, -g0], axis=0)

    cols, pair, lb_spec, st_spec = _hgrn_specs(reverse=True)
    wide = jax.ShapeDtypeStruct((SEQ, HGRN_WIDTH), BF16)
    return pl.pallas_call(
        body, name="hgrn_bwd", grid=(HGRN_HEADS // HGRN_PAIR, SEQ // HGRN_SEQ_BLOCK),
        out_shape=(wide, wide, wide, jax.ShapeDtypeStruct((2, HGRN_WIDTH), F32)),
        in_specs=[cols(4), cols(5), cols(6), lb_spec, pair, st_spec],
        out_specs=(pair, pair, pair, lb_spec),
        scratch_shapes=[pltpu.VMEM((HGRN_PAIR, HGRN_DIM, HGRN_DIM), F32),
                        pltpu.VMEM((1, HGRN_PAIR * HGRN_DIM), F32)],
        compiler_params=_params(("parallel", "arbitrary")),
    )(proj, proj, proj, lb_raw, d_rec, states)


def _group_sum(v, group):
    parts = []
    for s in range(v.shape[1] // LANES):
        slab = v[:, LANES * s:LANES * (s + 1)]
        if group == LANES:
            parts.append(jnp.broadcast_to(jnp.sum(slab, axis=-1, keepdims=True), slab.shape))
        else:
            h0 = lax.broadcasted_iota(jnp.int32, slab.shape, 1) < HEAD_DIM
            s0 = jnp.sum(jnp.where(h0, slab, 0.0), axis=-1, keepdims=True)
            s1 = jnp.sum(jnp.where(h0, 0.0, slab), axis=-1, keepdims=True)
            parts.append(jnp.where(h0, s0, s1))
    return jnp.concatenate(parts, axis=1)


def _mid(attn_o, rec, proj, x, target, w_out_g, attn_w, hgrn_w, final_w):
    tm = 256

    def branch_fwd(o, gate, w, group):
        r = lax.rsqrt(_group_sum(o * o, group) * (1.0 / group) + NORM_EPS)
        nrm = o * r
        sg = _sigmoid(gate)
        return r, nrm, sg, nrm * w * (gate * sg)

    def branch_bwd(dy, r, nrm, sg, gate, w, group):
        silu = gate * sg
        d_gate = dy * nrm * w * (sg * (1.0 + gate * (1.0 - sg)))
        d_w = jnp.sum(dy * nrm * silu, axis=0, keepdims=True)
        dn = dy * w * silu
        d_o = r * (dn - nrm * (_group_sum(dn * nrm, group) * (1.0 / group)))
        return d_o, d_gate, d_w

    def body(o_ref, rec_ref, ag_ref, hg_ref, x_ref, tgt_ref, wout_ref, aw_ref, hw_ref, fw_ref,
             dx2_ref, do_ref, delta_ref, dag_ref, drec_ref, dhg_ref, dwout_ref, dfw_ref, daw_ref, dhw_ref,
             loss_ref, dwout_acc):
        i = pl.program_id(0)

        @pl.when(i == 0)
        def _():
            dwout_acc[...] = jnp.zeros_like(dwout_acc)
            dfw_ref[...] = jnp.zeros_like(dfw_ref)
            daw_ref[...] = jnp.zeros_like(daw_ref)
            dhw_ref[...] = jnp.zeros_like(dhw_ref)
            loss_ref[...] = jnp.zeros_like(loss_ref)

        o, rc, ag, hg = o_ref[...], rec_ref[...], ag_ref[...], hg_ref[...]
        aw, hw, fw = aw_ref[...], hw_ref[...], fw_ref[...]
        ra, na, sga, ya = branch_fwd(o, ag, aw, HEAD_DIM)
        rh, nh, sgh, yh = branch_fwd(rc, hg, hw, HGRN_DIM)
        mixed = jnp.concatenate([ya, yh], axis=1).astype(BF16)
        wout = wout_ref[...]
        x2 = x_ref[...] + _mm(mixed, wout)
        rstd = lax.rsqrt(jnp.mean(x2 * x2, axis=-1, keepdims=True) + NORM_EPS)
        xn = x2 * rstd
        err = xn * fw - tgt_ref[...]
        row_loss = jnp.mean(err * err, axis=-1, keepdims=True)
        loss_ref[...] += 0.5 * jnp.sum(row_loss, axis=0, keepdims=True)
        dy = err * (1.0 / D_MODEL)
        dfw_ref[...] += jnp.sum(dy * xn, axis=0, keepdims=True)
        dxn = dy * fw
        dx2 = rstd * (dxn - xn * jnp.mean(dxn * xn, axis=-1, keepdims=True))
        dx2_ref[...] = dx2
        dx2b = dx2.astype(BF16)
        dwout_acc[...] += _mm_tn(mixed, dx2b)

        @pl.when(i == pl.num_programs(0) - 1)
        def _():
            dwout_ref[...] = dwout_acc[...].astype(BF16)

        dmixed = _mm_nt(dx2b, wout)

        d_o, d_ag, d_aw = branch_bwd(dmixed[:, :ATTN_WIDTH], ra, na, sga, ag, aw, HEAD_DIM)
        d_rec, d_hg, d_hw = branch_bwd(dmixed[:, ATTN_WIDTH:], rh, nh, sgh, hg, hw, HGRN_DIM)
        do_ref[...] = d_o
        delta_ref[...] = _group_sum(d_o * o, HEAD_DIM)
        dag_ref[...] = d_ag.astype(BF16)
        drec_ref[...] = d_rec
        dhg_ref[...] = d_hg.astype(BF16)
        daw_ref[...] += d_aw
        dhw_ref[...] += d_hw

    half = lambda: pl.BlockSpec((tm, COL_BLOCK), lambda i: (i, 0))
    full = lambda: pl.BlockSpec((tm, D_MODEL), lambda i: (i, 0))
    fixed = lambda r, c: pl.BlockSpec((r, c), lambda i: (0, 0))
    wide = jax.ShapeDtypeStruct((SEQ, COL_BLOCK), F32)
    wide_b = jax.ShapeDtypeStruct((SEQ, COL_BLOCK), BF16)
    return pl.pallas_call(
        body, name="mid", grid=(SEQ // tm,),
        out_shape=(jax.ShapeDtypeStruct((SEQ, D_MODEL), F32), wide, wide, wide_b, wide, wide_b,
                   jax.ShapeDtypeStruct((D_MODEL, D_MODEL), BF16),
                   jax.ShapeDtypeStruct((1, D_MODEL), F32), jax.ShapeDtypeStruct((1, COL_BLOCK), F32),
                   jax.ShapeDtypeStruct((1, COL_BLOCK), F32), jax.ShapeDtypeStruct((1, 1), F32)),
        scratch_shapes=[pltpu.VMEM((D_MODEL, D_MODEL), F32)],
        in_specs=[half(), half(),
                  pl.BlockSpec((None, tm, COL_BLOCK), lambda i: (3, i, 0)),
                  pl.BlockSpec((None, tm, COL_BLOCK), lambda i: (7, i, 0)),
                  full(), full(), fixed(D_MODEL, D_MODEL), fixed(1, COL_BLOCK), fixed(1, COL_BLOCK),
                  fixed(1, D_MODEL)],
        out_specs=(full(), half(), half(), half(), half(), half(), fixed(D_MODEL, D_MODEL),
                   fixed(1, D_MODEL), fixed(1, COL_BLOCK), fixed(1, COL_BLOCK), fixed(1, 1)),
        compiler_params=_params(("arbitrary",)),
    )(attn_o, rec, proj, proj, x, target, w_out_g, attn_w, hgrn_w, final_w)


def _in_proj_bwd_rows(d_groups, w_g, x, dx2, mix_w, rc, rsa, rsb):
    tm = 256

    def body(*refs):
        dg_refs = refs[:N_DEV]
        wg_ref, x_ref, dx2_ref, w_ref, c_ref, sa_ref, sb_ref, gx_ref, dpb_ref, dmw_ref = refs[N_DEV:]

        @pl.when(pl.program_id(0) == 0)
        def _():
            dmw_ref[...] = jnp.zeros_like(dmw_ref)

        parts = []
        for j in range(N_DEV):
            dp = dg_refs[j][...]
            if j < 2:
                dp = _rot_transposed(dp, c_ref[...], sa_ref[...], sb_ref[...])
            parts.append(dp.astype(BF16))
        dpb = jnp.concatenate(parts, axis=1)
        for j in range(N_DEV):
            dpb_ref[j] = parts[j]
        g = _mm_nt(dpb, wg_ref[...])
        xf = x_ref[...]
        rstd = lax.rsqrt(jnp.mean(xf * xf, axis=-1, keepdims=True) + NORM_EPS)
        xn = xf * rstd
        dmw_ref[...] += jnp.sum(g * xn, axis=0, keepdims=True)
        gw = g * w_ref[...]
        gx_ref[...] = dx2_ref[...] + rstd * (gw - xn * jnp.mean(gw * xn, axis=-1, keepdims=True))

    tile = lambda cols: pl.BlockSpec((tm, cols), lambda i: (i, 0))
    fixed = lambda r, c: pl.BlockSpec((r, c), lambda i: (0, 0))
    return pl.pallas_call(
        body, name="in_proj_bwd_rows", grid=(SEQ // tm,),
        out_shape=(jax.ShapeDtypeStruct((SEQ, D_MODEL), F32), jax.ShapeDtypeStruct((N_DEV, SEQ, COL_BLOCK), BF16),
                   jax.ShapeDtypeStruct((1, D_MODEL), F32)),
        in_specs=[tile(COL_BLOCK) for _ in range(N_DEV)] + [
            pl.BlockSpec((D_MODEL, IN_COLS), lambda i: (0, 0), pipeline_mode=pl.Buffered(1)),
            tile(D_MODEL), tile(D_MODEL), fixed(1, D_MODEL), tile(LANES), tile(LANES), tile(LANES)],
        out_specs=(tile(D_MODEL), pl.BlockSpec((N_DEV, tm, COL_BLOCK), lambda i: (0, i, 0)), fixed(1, D_MODEL)),
        compiler_params=_params(("arbitrary",)),
    )(*d_groups, w_g, x, dx2, mix_w, rc, rsa, rsb)


def _weights_exchange(hn_t, dproj_b, dwout_p, small_p):
    n_chips = N_DEV // 2
    rb = 128
    S1_IN, S1_OUT, SMALL, S2_IN, S2_OUT = 0, 4, 8, 15, 18
    rel_of_pair = (3, 1, 2, 0)

    def body(order_ref, hnt_ref, dp_ref, dwout_ref, small_ref, gin_ref, gout_ref, gs_ref,
             part, s1_send, s1_in, s1_out, fwd_in, fwd_out, s2_in, s2_out, land_s, send_sems, recv_sems):
        t = pl.program_id(0)
        me = _my_place()
        x, y, c = me
        my_chip = 2 * x + y
        sibling = (x, y, 1 - c)

        def remote(slot, src, dst, to):
            return pltpu.make_async_remote_copy(src_ref=src, dst_ref=dst, send_sem=send_sems.at[slot],
                                                recv_sem=recv_sems.at[slot], device_id=to, device_id_type=MESH)

        def s1_in_copy(pair):
            return remote(S1_IN + pair, s1_send.at[pair], s1_in.at[pair], sibling)

        def s1_out_copy(pair):
            q = my_chip ^ rel_of_pair[pair]
            return remote(S1_OUT + pair, dwout_ref.at[q, 1 - c], s1_out.at[pair], sibling)

        def s2_copies(rel):
            peer = _peer(me, 2 * rel)
            return [remote(S2_IN + rel - 1, fwd_in.at[rel - 1], s2_in.at[rel - 1], peer),
                    remote(S2_OUT + rel - 1, fwd_out.at[rel - 1], s2_out.at[rel - 1], peer)]

        def small_copy(rel):
            return remote(SMALL + rel - 1, small_ref, land_s.at[rel], _peer(me, rel))

        @pl.when(t == 0)
        def _():
            land_s[0] = small_ref[...]
            for pair in range(n_chips):
                s1_out_copy(pair).start()
            for rel in range(1, N_DEV):
                small_copy(rel).start()

        part[...] = _mm(hnt_ref[...], dp_ref[...])

        def rows_loop(n_rows, fn):
            def step(b, carry):
                fn(pl.ds(pl.multiple_of(b * rb, rb), rb))
                return carry
            lax.fori_loop(0, n_rows // rb, step, 0)

        for pair, rel in enumerate(rel_of_pair):
            @pl.when(t == 2 * pair)
            def _(pair=pair):
                s1_send[pair] = part[...].astype(BF16)
                s1_in_copy(pair).start()

            @pl.when(t == 2 * pair + 1)
            def _(pair=pair, rel=rel):
                q = my_chip ^ rel
                s1_in_copy(pair).wait_recv()
                s1_out_copy(pair).wait_recv()
                dst_in = fwd_in.at[rel - 1] if rel else gin_ref
                dst_out = fwd_out.at[rel - 1] if rel else gout_ref

                def add_in(rows):
                    dst_in[rows, :] = (part[rows, :] + s1_in[pair, rows, :].astype(F32)).astype(dst_in.dtype)

                def add_out(rows):
                    dst_out[rows, :] = (dwout_ref[q, c, rows, :].astype(F32)
                                        + s1_out[pair, rows, :].astype(F32)).astype(dst_out.dtype)

                rows_loop(D_MODEL, add_in)
                rows_loop(WOUT_ROWS, add_out)
                if rel:
                    for cp in s2_copies(rel):
                        cp.start()

        @pl.when(t == N_DEV - 1)
        def _():
            for rel in range(1, n_chips):
                for cp in s2_copies(rel):
                    cp.wait_recv()

            def total_in(rows):
                g = gin_ref[rows, :]
                for rel in range(1, n_chips):
                    g = g + s2_in[rel - 1, rows, :].astype(F32)
                gin_ref[rows, :] = g

            def total_out(rows):
                g = gout_ref[rows, :]
                for rel in range(1, n_chips):
                    g = g + s2_out[rel - 1, rows, :].astype(F32)
                gout_ref[rows, :] = g

            rows_loop(D_MODEL, total_in)
            rows_loop(WOUT_ROWS, total_out)

            for rel in range(1, N_DEV):
                small_copy(rel).wait_recv()
            my_flat = _flat(me)
            g = land_s[my_flat ^ 0]
            for dev in range(1, N_DEV):
                g = g + land_s[my_flat ^ dev]
            gs_ref[...] = g

            for pair in range(n_chips):
                s1_in_copy(pair).wait_send()
                s1_out_copy(pair).wait_send()
            for rel in range(1, n_chips):
                for cp in s2_copies(rel):
                    cp.wait_send()
            for rel in range(1, N_DEV):
                small_copy(rel).wait_send()

    place_x, place_y, place_c = _my_place()
    my_chip = 2 * place_x + place_y
    order = jnp.stack([2 * (my_chip ^ rel) + core for rel in rel_of_pair
                       for core in (1 - place_c, place_c)]).astype(jnp.int32)

    whole = lambda: pl.BlockSpec(memory_space=pltpu.VMEM)
    in_blocks = lambda n: pltpu.VMEM((n, D_MODEL, COL_BLOCK), BF16)
    out_blocks = lambda n: pltpu.VMEM((n, WOUT_ROWS, D_MODEL), BF16)
    grid_spec = pltpu.PrefetchScalarGridSpec(
        num_scalar_prefetch=1, grid=(N_DEV,),
        in_specs=[pl.BlockSpec((D_MODEL, SEQ), lambda t, order: (0, 0), pipeline_mode=pl.Buffered(1)),
                  pl.BlockSpec((None, SEQ, COL_BLOCK), lambda t, order: (order[t], 0, 0)), whole(), whole()],
        out_specs=(whole(), whole(), whole()),
        scratch_shapes=[pltpu.VMEM((D_MODEL, COL_BLOCK), F32), in_blocks(n_chips), in_blocks(n_chips),
                        out_blocks(n_chips), in_blocks(n_chips - 1), out_blocks(n_chips - 1),
                        in_blocks(n_chips - 1), out_blocks(n_chips - 1),
                        pltpu.VMEM((N_DEV, SMALL_ROWS, LANES), F32),
                        pltpu.SemaphoreType.DMA((21,)), pltpu.SemaphoreType.DMA((21,))])
    return pl.pallas_call(
        body, name="weights_exchange", grid_spec=grid_spec,
        out_shape=(jax.ShapeDtypeStruct((D_MODEL, COL_BLOCK), F32), jax.ShapeDtypeStruct((WOUT_ROWS, D_MODEL), F32),
                   jax.ShapeDtypeStruct((SMALL_ROWS, LANES), F32)),
        compiler_params=_params(("arbitrary",)),
    )(order, hn_t, dproj_b, dwout_p.reshape(n_chips, 2, WOUT_ROWS, D_MODEL), small_p)


def _adamw(w, g, m, v):
    m = ADAM_B1 * m + (1.0 - ADAM_B1) * g
    v = ADAM_B2 * v + (1.0 - ADAM_B2) * (g * g)
    m_hat = m / (1.0 - ADAM_B1 ** ADAM_STEP)
    v_hat = v / (1.0 - ADAM_B2 ** ADAM_STEP)
    delta = -ADAM_LR * (m_hat / (jnp.sqrt(v_hat) + ADAM_EPS) + ADAM_WD * w)
    return delta, m, v


def _adamw_update(grads, weights, m_old, v_old):
    rb = 256

    def body(*refs):
        g_refs, w_refs, m_refs, v_refs = refs[0:3], refs[3:6], refs[6:9], refs[9:12]
        d_refs, nm_refs, nv_refs = refs[12:15], refs[15:18], refs[18:21]
        for k in range(3):
            n_rows = g_refs[k].shape[0]
            step_rows = min(rb, n_rows)

            def step(b, carry, k=k, step_rows=step_rows):
                rows = pl.ds(pl.multiple_of(b * step_rows, 8), step_rows)
                delta, nm, nv = _adamw(w_refs[k][rows, :], g_refs[k][rows, :], m_refs[k][rows, :], v_refs[k][rows, :])
                d_refs[k][rows, :] = delta
                nm_refs[k][rows, :] = nm
                nv_refs[k][rows, :] = nv
                return carry

            lax.fori_loop(0, n_rows // step_rows, step, 0)

    shapes = tuple(jax.ShapeDtypeStruct(g.shape, F32) for g in grads)
    vm = lambda: pl.BlockSpec(memory_space=pltpu.VMEM)
    outs = pl.pallas_call(
        body, name="adamw_update", out_shape=shapes * 3,
        in_specs=[vm() for _ in range(12)], out_specs=tuple(vm() for _ in range(9)),
        compiler_params=_params(),
    )(*grads, *weights, *m_old, *v_old)
    return outs[0:3], outs[3:6], outs[6:9]


def _pack_small(mix, attn, hgrn, lb, final, loss=None):
    def rows8(a):
        a = a.reshape(-1, LANES)
        return jnp.pad(a, ((0, 8 - a.shape[0]), (0, 0)))
    last = jnp.zeros((8, LANES), F32) if loss is None else jnp.pad(loss.reshape(1, 1), ((0, 7), (0, LANES - 1)))
    return jnp.concatenate([rows8(mix), rows8(attn), rows8(hgrn), rows8(lb), rows8(final), last], axis=0)


def _unpack_small(slab):
    return (slab[ROW_MIX:ROW_MIX + 8].reshape(1, D_MODEL), slab[ROW_ATTN:ROW_ATTN + 4].reshape(1, ATTN_WIDTH),
            slab[ROW_HGRN:ROW_HGRN + 4].reshape(1, HGRN_WIDTH), slab[ROW_LB:ROW_LB + 8].reshape(2, HGRN_WIDTH),
            slab[ROW_FINAL:ROW_FINAL + 8].reshape(D_MODEL))


def _rope(pos_row):
    j = np.arange(ROPE_ROWS)
    inv = np.where(j < ROPE_HALF, ROPE_THETA ** (-(j % ROPE_HALF) * (2.0 / ROPE_DIMS)), 0.0)
    e = np.arange(LANES) % HEAD_DIM
    hit = (j[:, None] == (e % ROPE_HALF)[None, :]) & (j[:, None] < ROPE_HALF)
    sel = np.stack([hit & (e < ROPE_DIMS), hit & (e >= ROPE_HALF) & (e < ROPE_DIMS),
                    -1.0 * (hit & (e < ROPE_HALF))]).astype(np.float32)
    return _rope_tables(pos_row, jnp.asarray(inv.astype(np.float32).reshape(ROPE_ROWS, 1)),
                        jnp.asarray(sel, dtype=BF16))


def _local_step(x, proj, qkv_sorted, w_in_g, w_out_g, tables, mix_w, attn_w, hgrn_w, lb_raw, final_w, target):
    rc, rsa, rsb = tables
    attn_o, lse = _attn_fwd_fused(qkv_sorted)
    rec, states = _hgrn_fwd(proj, lb_raw)

    (dx2, d_o, delta, d_ag, d_rec, d_hg, dwout_p, d_final, d_attn_w, d_hgrn_w, loss) = _mid(
        attn_o, rec, proj, x, target, w_out_g, attn_w, hgrn_w, final_w.reshape(1, D_MODEL))

    dqkv = _attn_bwd_fused(qkv_sorted, d_o, lse, delta)
    d_hq, d_hf, d_hi, d_lb = _hgrn_bwd(proj, lb_raw, d_rec, states)

    grad_x, dproj_b, d_mix = _in_proj_bwd_rows(
        (dqkv[0], dqkv[1], dqkv[2], d_ag, d_hq, d_hf, d_hi, d_hg), w_in_g, x, dx2, mix_w, rc, rsa, rsb)
    small_p = _pack_small(d_mix, d_attn_w, d_hgrn_w, d_lb, d_final, loss)
    return grad_x, dproj_b, dwout_p, small_p


def kernel(x, positions, w_in, w_out, mix_norm_w, attn_out_norm_w, hgrn_out_norm_w, hgrn_lb_raw, final_norm_w, loss_target, m_w_in, m_w_out, m_mix_norm_w, m_attn_out_norm_w, m_hgrn_out_norm_w, m_hgrn_lb_raw, m_final_norm_w, v_w_in, v_w_out, v_mix_norm_w, v_attn_out_norm_w, v_hgrn_out_norm_w, v_hgrn_lb_raw, v_final_norm_w):
    tables = _rope(positions)
    proj, hn_t, w_in_g, w_out_g, qkv_sorted = _gather_project(x[0], mix_norm_w, w_in[0], w_out[0], *tables)
    grad_x, dproj_b, dwout_p, small_p = _local_step(
        x[0], proj, qkv_sorted, w_in_g, w_out_g, tables, mix_norm_w, attn_out_norm_w, hgrn_out_norm_w,
        hgrn_lb_raw, final_norm_w, loss_target[0])
    g_in, g_out, g_s = _weights_exchange(hn_t, dproj_b, dwout_p, small_p)

    w_s = _pack_small(mix_norm_w, attn_out_norm_w, hgrn_out_norm_w, hgrn_lb_raw, final_norm_w)
    m_s = _pack_small(m_mix_norm_w, m_attn_out_norm_w, m_hgrn_out_norm_w, m_hgrn_lb_raw, m_final_norm_w)
    v_s = _pack_small(v_mix_norm_w, v_attn_out_norm_w, v_hgrn_out_norm_w, v_hgrn_lb_raw, v_final_norm_w)
    (d_in, d_out, d_s), (nm_in, nm_out, nm_s), (nv_in, nv_out, nv_s) = _adamw_update(
        (g_in, g_out, g_s), (w_in[0], w_out[0], w_s), (m_w_in[0], m_w_out[0], m_s), (v_w_in[0], v_w_out[0], v_s))

    loss = g_s[ROW_LOSS, 0]
    return (loss, grad_x[None], g_in[None], g_out[None], *_unpack_small(g_s),
            d_in[None], d_out[None], *_unpack_small(d_s),
            nm_in[None], nm_out[None], *_unpack_small(nm_s),
            nv_in[None], nv_out[None], *_unpack_small(nv_s))
```

```python
import functools

import jax
import jax.numpy as jnp
import numpy as np
from jax import lax
from jax.experimental import pallas as pl
from jax.experimental.pallas import tpu as pltpu

F32 = jnp.float32
BF16 = jnp.bfloat16

SEQ = 4096
D_MODEL = 1024
ATTN_WIDTH = 512
HGRN_WIDTH = 512
HEAD_DIM = 64
HGRN_HEADS = 4
HGRN_DIM = 128
HGRN_CHUNK = 64
N_CHUNKS = SEQ // HGRN_CHUNK
IN_COLS = 4096
COL_BLOCK = 512
N_DEV = 8
WOUT_ROWS = D_MODEL // N_DEV
ATTN_BLOCK = 128
DILATIONS = (1, 4, 16)
ROPE_THETA = 500000.0
ROPE_DIMS = 16
ROPE_HALF = 8
NORM_EPS = 1e-6
NEG_BIG = -1e30
LANES = 128

ADAM_LR = 0.001
ADAM_B1 = 0.9
ADAM_B2 = 0.999
ADAM_EPS = 1e-08
ADAM_WD = 0.01
ADAM_STEP = 10

SMALL_ROWS = 48
ROW_MIX, ROW_ATTN, ROW_HGRN, ROW_LB, ROW_FINAL, ROW_LOSS = 0, 8, 16, 24, 32, 40

VMEM_LIMIT = 56 * 1024 * 1024
MESH = pl.DeviceIdType.MESH


def _mm(a, b):
    return lax.dot_general(a, b, (((1,), (0,)), ((), ())), preferred_element_type=F32)


def _mm_nt(a, b):
    return lax.dot_general(a, b, (((1,), (1,)), ((), ())), preferred_element_type=F32)


def _mm_tn(a, b):
    return lax.dot_general(a, b, (((0,), (0,)), ((), ())), preferred_element_type=F32)


def _mm_exact(a, b):
    return lax.dot_general(a, b, (((1,), (0,)), ((), ())), preferred_element_type=F32,
                           precision=lax.Precision.HIGHEST)


def _sigmoid(v):
    return 1.0 / (1.0 + jnp.exp(-v))


def _params(sem=None, **kw):
    return pltpu.CompilerParams(dimension_semantics=sem, vmem_limit_bytes=VMEM_LIMIT, **kw)


def _my_place():
    return lax.axis_index("x"), lax.axis_index("y"), lax.axis_index("c")


def _peer(place, rel):
    x, y, c = place
    return (x ^ ((rel >> 2) & 1), y ^ ((rel >> 1) & 1), c ^ (rel & 1))


def _flat(place):
    x, y, c = place
    return 4 * x + 2 * y + c


ROPE_ROWS = 16


def _rope_tables(pos_row, inv_freq_col, selectors):
    def body(pos_ref, invf_ref, sel_ref, c_ref, sa_ref, sb_ref):
        ang = pos_ref[...].astype(F32) * invf_ref[...]
        cos, sin = jnp.cos(ang), jnp.sin(ang)

        def spread(v, sel):
            hi = v.astype(BF16)
            r1 = v - hi.astype(F32)
            mid = r1.astype(BF16)
            lo = (r1 - mid.astype(F32)).astype(BF16)
            return _mm_tn(hi, sel) + _mm_tn(mid, sel) + _mm_tn(lo, sel)

        e = lax.broadcasted_iota(jnp.int32, (1, LANES), 1) & (HEAD_DIM - 1)
        c_ref[...] = spread(cos, sel_ref[0]) + jnp.where(e < ROPE_DIMS, 0.0, 1.0)
        sa_ref[...] = spread(sin, sel_ref[1])
        sb_ref[...] = spread(sin, sel_ref[2])

    tab = jax.ShapeDtypeStruct((SEQ, LANES), F32)
    vm = lambda: pl.BlockSpec(memory_space=pltpu.VMEM)
    return pl.pallas_call(
        body, name="rope_tables", out_shape=(tab, tab, tab),
        in_specs=[vm(), vm(), vm()], out_specs=(vm(), vm(), vm()), compiler_params=_params(),
    )(pos_row, inv_freq_col, selectors)


def _per_slab(fn, t):
    return jnp.concatenate([fn(t[:, LANES * s:LANES * (s + 1)]) for s in range(t.shape[1] // LANES)], axis=1)


def _rot(t, c, sa, sb):
    return _per_slab(lambda u: u * c + pltpu.roll(u, ROPE_HALF, 1) * sa + pltpu.roll(u, LANES - ROPE_HALF, 1) * sb, t)


def _rot_transposed(g, c, sa, sb):
    return _per_slab(
        lambda u: u * c + pltpu.roll(u * sa, LANES - ROPE_HALF, 1) + pltpu.roll(u * sb, ROPE_HALF, 1), g)


def _gather_project(x, mix_w, w_in, w_out, rc, rsa, rsb):
    tm = 1024
    n_tiles = SEQ // tm
    arrival_of_step = (None, 0, 1, 2, 4, 5, 3, 6)

    def body(order_ref, x_ref, w_ref, win_ref, wout_ref, c_ref, sa_ref, sb_ref,
             proj_ref, hnt_ref, gin_hbm, gout_hbm, qkv_hbm,
             hn_s, w_land, wout_land, stage, sort_stage, send_sems, recv_sems, local_sems, sort_sems):
        g, i = pl.program_id(0), pl.program_id(1)
        me = _my_place()
        x_, y_, c_ = me
        sibling = (x_, y_, 1 - c_)
        chips = [(1 - x_, y_), (x_, 1 - y_), (1 - x_, 1 - y_)]

        def slab(which, place):
            idx = _flat(place)
            if which == 0:
                return w_land.at[idx]
            return wout_land.at[pl.ds(pl.multiple_of(idx * WOUT_ROWS, WOUT_ROWS), WOUT_ROWS), :]

        def remote(which, k, ref, to, src=None):
            return pltpu.make_async_remote_copy(
                src_ref=ref if src is None else src, dst_ref=ref, send_sem=send_sems.at[8 * which + k],
                recv_sem=recv_sems.at[8 * which + k], device_id=to, device_id_type=MESH)

        def copy(which, k, block, to, src=None):
            return remote(which, k, slab(which, block), to, src)

        def half(which, place, part):
            n = (D_MODEL if which == 0 else WOUT_ROWS) // 2
            if which == 0:
                return w_land.at[_flat(place), pl.ds(n * part, n), :]
            return wout_land.at[pl.ds(pl.multiple_of(_flat(place) * WOUT_ROWS + n * part, n), n), :]

        def first_copies(which):
            src = stage if which == 0 else None
            return ([copy(which, 0, me, sibling, src)]
                    + [copy(which, 1 + j, me, (*chips[j], c_), src) for j in range(2)])

        def relay(which, part):
            frm, to = (chips[1], chips[0]) if part == 0 else (chips[0], chips[1])
            return remote(which, 3 if part == 0 else 7, half(which, (*frm, c_), part), (*to, c_))

        def two_hop_half(which, part):
            return remote(which, 3 if part == 0 else 7, half(which, (*chips[2], c_), part), me)

        def pass_on(which, j):
            return copy(which, 4 + j, (*chips[j], c_), sibling)

        def arrival(which, k):
            if k == 0:
                return copy(which, 0, sibling, me)
            if k <= 2:
                return copy(which, k, (*chips[k - 1], c_), me)
            return copy(which, k, (*chips[k - 4], 1 - c_), me)

        def to_hbm(step):
            idx = order_ref[step]
            cols = pl.ds(pl.multiple_of(idx * COL_BLOCK, COL_BLOCK), COL_BLOCK)
            return pltpu.make_async_copy(w_land.at[idx], gin_hbm.at[:, cols], local_sems.at[step])

        @pl.when((g == 0) & (i == 0))
        def _():
            stage[...] = win_ref[...].astype(BF16)
            w_land[_flat(me)] = stage[...]
            wout_land[pl.ds(pl.multiple_of(_flat(me) * WOUT_ROWS, WOUT_ROWS), WOUT_ROWS), :] = (
                wout_ref[...].astype(BF16))
            for cp in first_copies(0) + first_copies(1)[:1]:
                cp.start()
            to_hbm(0).start()

        for step, k in enumerate(arrival_of_step):
            if k is None:
                continue

            @pl.when((g == step) & (i == 0))
            def _(k=k, step=step):
                if k == 3:
                    two_hop_half(0, 0).wait_recv()
                    two_hop_half(0, 1).wait_recv()
                else:
                    arrival(0, k).wait_recv()
                to_hbm(step).start()
                if 1 <= k <= 3:
                    pass_on(0, k - 1).start()
                if k == 1:
                    relay(0, 1).start()
                    for cp in first_copies(1)[1:]:
                        cp.start()
                if k == 2:
                    relay(0, 0).start()
                if k in (4, 5):
                    arrival(1, k - 3).wait_recv()
                    relay(1, 5 - k).start()

        rows = pl.ds(pl.multiple_of(i * tm, tm), tm)

        @pl.when(g == 0)
        def _():
            xf = x_ref[...]
            ms = jnp.mean(xf * xf, axis=-1, keepdims=True)
            hn = xf * lax.rsqrt(ms + NORM_EPS) * w_ref[...]
            hnt_ref[...] = hn.T.astype(BF16)
            hn_s[rows, :] = hn.astype(BF16)

        group = order_ref[g]

        def sorted_copy(tile_value):
            per = tm // SORT_RESIDUES
            cols = pl.ds(pl.multiple_of(group * COL_BLOCK, COL_BLOCK), COL_BLOCK)
            buf = i % 2

            def out_copies(tile, b):
                return [pltpu.make_async_copy(
                    sort_stage.at[b, :, r, :], qkv_hbm.at[r, pl.ds(pl.multiple_of(tile * per, per), per), cols],
                    sort_sems.at[b, r]) for r in range(SORT_RESIDUES)]

            @pl.when(i >= 2)
            def _():
                for copy in out_copies(i - 2, buf):
                    copy.wait()

            sort_stage[buf] = tile_value.reshape(per, SORT_RESIDUES, COL_BLOCK)
            for copy in out_copies(i, buf):
                copy.start()

            @pl.when(i == n_tiles - 1)
            def _():
                for copy in out_copies(i - 1, 1 - buf) + out_copies(i, buf):
                    copy.wait()

        @pl.when(group < 2)
        def _():
            rotated = _rot(_mm(hn_s[rows, :], w_land[group]), c_ref[...], sa_ref[...], sb_ref[...])
            proj_ref[...] = rotated
            sorted_copy(rotated)

        @pl.when(group == 2)
        def _():
            value = _mm(hn_s[rows, :], w_land[group])
            proj_ref[...] = value
            sorted_copy(value)

        @pl.when(group > 2)
        def _():
            proj_ref[...] = _mm(hn_s[rows, :], w_land[group])

        @pl.when((g == N_DEV - 1) & (i == n_tiles - 1))
        def _():
            pass_on(1, 0).start()
            pass_on(1, 1).start()
            two_hop_half(1, 0).wait_recv()
            two_hop_half(1, 1).wait_recv()
            pass_on(1, 2).start()
            for k in (0, 4, 5, 6):
                arrival(1, k).wait_recv()
            for which in (0, 1):
                for cp in (first_copies(which) + [relay(which, part) for part in range(2)]
                           + [pass_on(which, j) for j in range(3)]):
                    cp.wait_send()
            wout_copy = pltpu.make_async_copy(wout_land, gout_hbm, local_sems.at[N_DEV])
            wout_copy.start()
            for step in range(N_DEV):
                to_hbm(step).wait()
            wout_copy.wait()

    me = _my_place()
    x_, y_, c_ = me
    chips = [(1 - x_, y_), (x_, 1 - y_), (1 - x_, 1 - y_)]
    order = jnp.stack([_flat(p) for p in (
        me, (x_, y_, 1 - c_), (*chips[0], c_), (*chips[1], c_), (*chips[0], 1 - c_), (*chips[1], 1 - c_),
        (*chips[2], c_), (*chips[2], 1 - c_))]).astype(jnp.int32)

    first_sweep = lambda g, i, order: (jnp.where(g == 0, i, n_tiles - 1), 0)
    tab = pl.BlockSpec((tm, LANES), lambda g, i, order: (jnp.where(order[g] < 2, i, 0), 0))
    whole = lambda: pl.BlockSpec(memory_space=pltpu.VMEM)
    grid_spec = pltpu.PrefetchScalarGridSpec(
        num_scalar_prefetch=1, grid=(N_DEV, n_tiles),
        in_specs=[pl.BlockSpec((tm, D_MODEL), first_sweep),
                  pl.BlockSpec((1, D_MODEL), lambda g, i, order: (0, 0)),
                  whole(), whole(), tab, tab, tab],
        out_specs=(pl.BlockSpec((None, tm, COL_BLOCK), lambda g, i, order: (order[g], i, 0)),
                   pl.BlockSpec((D_MODEL, tm), lambda g, i, order: (0, jnp.where(g == 0, i, n_tiles - 1))),
                   pl.BlockSpec(memory_space=pl.ANY), pl.BlockSpec(memory_space=pl.ANY),
                   pl.BlockSpec(memory_space=pl.ANY)),
        scratch_shapes=[pltpu.VMEM((SEQ, D_MODEL), BF16),
                        pltpu.VMEM((N_DEV, D_MODEL, COL_BLOCK), BF16),
                        pltpu.VMEM((D_MODEL, D_MODEL), BF16),
                        pltpu.VMEM((D_MODEL, COL_BLOCK), BF16),
                        pltpu.VMEM((2, tm // SORT_RESIDUES, SORT_RESIDUES, COL_BLOCK), F32),
                        pltpu.SemaphoreType.DMA((16,)), pltpu.SemaphoreType.DMA((16,)),
                        pltpu.SemaphoreType.DMA((N_DEV + 1,)), pltpu.SemaphoreType.DMA((2, SORT_RESIDUES))])
    proj, hn_t, w_in_g, w_out_g, qkv_sorted = pl.pallas_call(
        body, name="gather_project", grid_spec=grid_spec,
        out_shape=(jax.ShapeDtypeStruct((N_DEV, SEQ, COL_BLOCK), F32), jax.ShapeDtypeStruct((D_MODEL, SEQ), BF16),
                   jax.ShapeDtypeStruct((D_MODEL, IN_COLS), BF16), jax.ShapeDtypeStruct((D_MODEL, D_MODEL), BF16),
                   jax.ShapeDtypeStruct((SORT_RESIDUES, SORT_ROWS, 3 * COL_BLOCK), F32)),
        compiler_params=_params(("arbitrary", "arbitrary")),
    )(order, x, mix_w, w_in, w_out, rc, rsa, rsb)
    return proj, hn_t, w_in_g, w_out_g, qkv_sorted.reshape(SEQ, 3 * COL_BLOCK)


SCORE_SCALE = HEAD_DIM ** -0.5
ATTN_GROUP_FWD = 16
ATTN_GROUP_BWD = 8
BLOCKS_PER_PATTERN = SEQ // ATTN_BLOCK
SORT_RESIDUES = 16
SORT_ROWS = SEQ // SORT_RESIDUES


def _write_band_bias(bias_ref):
    row = lax.broadcasted_iota(jnp.int32, (2 * ATTN_BLOCK, 2 * ATTN_BLOCK), 0) & (ATTN_BLOCK - 1)
    col = lax.broadcasted_iota(jnp.int32, (2 * ATTN_BLOCK, 2 * ATTN_BLOCK), 1)
    for pi, d in enumerate(DILATIONS):
        per = SORT_RESIDUES // d
        ahead = per * (row % (8 * d) - col % (16 * d)) + (row // (8 * d) - col // (16 * d))
        dist = ATTN_BLOCK + ahead
        bias_ref[2 * pi] = jnp.where((dist >= 0) & (dist <= ATTN_BLOCK), 0.0, NEG_BIG)
        bias_ref[2 * pi + 1] = jnp.where(ahead >= 0, 0.0, NEG_BIG)


def _head0_lanes():
    return lax.broadcasted_iota(jnp.int32, (ATTN_BLOCK, LANES), 1) < HEAD_DIM


def _stack_heads(t, h0):
    return jnp.concatenate([jnp.where(h0, t, 0.0), jnp.where(h0, 0.0, t)], axis=0).astype(BF16)


def _block_runs(i, d):
    nblk = BLOCKS_PER_PATTERN // d
    r, n = i // nblk, i % nblk
    kn = jnp.maximum(n - 1, 0)
    rows, keys = [], []
    for c in range(SORT_RESIDUES // d):
        base = SORT_ROWS * (c * d + r)
        rows.append(pl.ds(pl.multiple_of(base + 8 * d * n, 8), 8 * d))
        keys.append(pl.ds(pl.multiple_of(base + 8 * d * kn, 8), 16 * d))
    return rows, keys, (n == 0).astype(jnp.int32)


def _take(ref, runs):
    return jnp.concatenate([ref[run, :] for run in runs], axis=0)


def _put(ref, runs, value, add=False):
    at = 0
    for run in runs:
        piece = value[at:at + run.size]
        if add:
            ref[run, :] += piece
        else:
            ref[run, :] = piece
        at += run.size


def _sort_copies(src_hbm, lane_block, dst_ref, sem_ref):
    lanes = pl.ds(pl.multiple_of(LANES * lane_block, LANES), LANES)
    return [pltpu.make_async_copy(src_hbm.at[:, r, lanes], dst_ref.at[pl.ds(SORT_ROWS * r, SORT_ROWS), :],
                                  sem_ref.at[r]) for r in range(SORT_RESIDUES)]


def _unsort_copies(src_ref, dst_hbm, lane_block, sem_ref):
    lanes = pl.ds(pl.multiple_of(LANES * lane_block, LANES), LANES)
    return [pltpu.make_async_copy(src_ref.at[pl.ds(SORT_ROWS * r, SORT_ROWS), :], dst_hbm.at[:, r, lanes],
                                  sem_ref.at[r]) for r in range(SORT_RESIDUES)]


def _for_each_group(d, n_group, load, compute, store):
    def group(g, carry):
        items = [load(*_block_runs(g * n_group + u, d)) for u in range(n_group)]
        results = [compute(item) for item in items]
        for item, res in zip(items, results):
            store(item, res)
        return carry

    lax.fori_loop(0, BLOCKS_PER_PATTERN // n_group, group, 0)


def _attn_fwd_fused(qkv_sorted):
    n_pat = len(DILATIONS)
    tile2 = (2 * ATTN_BLOCK, LANES)

    def body(q_ref, k_ref, v_ref, o_hbm, lse_ref, o_slots, m_acc, l_acc, bias_ref, out_sem):
        step, n_steps = pl.program_id(0), pl.num_programs(0)
        pl.when(step == 0)(lambda: _write_band_bias(bias_ref))
        slot = step % 2
        o_acc = o_slots.at[slot]
        h0 = _head0_lanes()
        for pi, d in enumerate(DILATIONS):
            first, last = pi == 0, pi == n_pat - 1

            def load(rows, keys, which, first=first, pi=pi):
                item = dict(rows=rows, keys=keys, which=2 * pi + which)
                if not first:
                    item.update(o=_take(o_acc, rows), m=[_take(m_acc.at[h], rows) for h in range(2)],
                                l=[_take(l_acc.at[h], rows) for h in range(2)])
                return item

            def compute(item, first=first):
                kb = _take(k_ref, item["keys"]).astype(BF16)
                vb = _take(v_ref, item["keys"]).astype(BF16)
                s = _mm_nt(_stack_heads(_take(q_ref, item["rows"]) * SCORE_SCALE, h0), kb) + bias_ref[item["which"]]
                mb = jnp.max(s, axis=-1, keepdims=True)
                if first:
                    p = jnp.exp(s - mb)
                    mn = jnp.broadcast_to(mb, tile2)
                else:
                    m_old = jnp.concatenate(item["m"], axis=0)
                    mn = jnp.maximum(m_old, mb)
                    alpha = jnp.exp(m_old - mn)
                    p = jnp.exp(s - jnp.concatenate([mn, mn], axis=1))
                ls = jnp.sum(p, axis=-1, keepdims=True)
                pv = _mm(p.astype(BF16), vb)
                if first:
                    return pv, mn, jnp.broadcast_to(ls, tile2)
                o_old = jnp.concatenate([item["o"], item["o"]], axis=0)
                return alpha * o_old + pv, mn, alpha * jnp.concatenate(item["l"], axis=0) + ls

            def store(item, res, last=last):
                rows = item["rows"]
                (o0, o1), (m0, m1), (l0, l1) = ((a[:ATTN_BLOCK], a[ATTN_BLOCK:]) for a in res)
                if last:
                    _put(o_acc, rows, jnp.where(h0, o0 / l0, o1 / l1))
                    _put(lse_ref, rows, jnp.where(h0, m0 + jnp.log(l0), m1 + jnp.log(l1)))
                else:
                    _put(o_acc, rows, jnp.where(h0, o0, o1))
                    for h, (m, l) in enumerate(((m0, l0), (m1, l1))):
                        _put(m_acc.at[h], rows, m)
                        _put(l_acc.at[h], rows, l)

            _for_each_group(d, ATTN_GROUP_FWD, load, compute, store)

        def copies_out(of_step):
            return _unsort_copies(o_slots.at[of_step % 2], o_hbm, of_step, out_sem.at[of_step % 2])

        @pl.when(step > 0)
        def _():
            for copy in copies_out(step - 1):
                copy.wait()

        for copy in copies_out(step):
            copy.start()

        @pl.when(step == n_steps - 1)
        def _():
            for copy in copies_out(step):
                copy.wait()

    slab = lambda g: pl.BlockSpec((SEQ, LANES), functools.partial(lambda hp, g: (0, 4 * g + hp), g=g))
    wide = jax.ShapeDtypeStruct((SEQ, ATTN_WIDTH), F32)
    o_rows, lse = pl.pallas_call(
        body, name="attn_fwd", grid=(4,),
        out_shape=(jax.ShapeDtypeStruct((SORT_ROWS, SORT_RESIDUES, ATTN_WIDTH), F32), wide),
        in_specs=[slab(0), slab(1), slab(2)], out_specs=(pl.BlockSpec(memory_space=pl.ANY), slab(0)),
        scratch_shapes=[pltpu.VMEM((2, SEQ, LANES), F32), pltpu.VMEM((2, SEQ, LANES), F32),
                        pltpu.VMEM((2, SEQ, LANES), F32),
                        pltpu.VMEM((2 * len(DILATIONS), 2 * ATTN_BLOCK, 2 * ATTN_BLOCK), F32),
                        pltpu.SemaphoreType.DMA((2, SORT_RESIDUES))],
        compiler_params=_params(("arbitrary",)),
    )(qkv_sorted, qkv_sorted, qkv_sorted)
    return o_rows.reshape(SEQ, ATTN_WIDTH), lse


def _attn_bwd_fused(qkv_sorted, d_out, lse_sorted, delta):
    def body(q_ref, k_ref, v_ref, do_hbm, lse_ref, del_hbm, dq_hbm, dk_hbm, dv_hbm,
             in_slots, out_slots, bias_ref, in_sem, out_sem):
        step, n_steps = pl.program_id(0), pl.num_programs(0)
        slot = step % 2

        def copies_in(of_step):
            s = of_step % 2
            return [copy for j, hbm in enumerate((do_hbm, del_hbm))
                    for copy in _sort_copies(hbm, of_step, in_slots.at[s, j], in_sem.at[s, j])]

        def copies_out(of_step):
            s = of_step % 2
            return [copy for j, hbm in enumerate((dq_hbm, dk_hbm, dv_hbm))
                    for copy in _unsort_copies(out_slots.at[s, j], hbm, of_step, out_sem.at[s, j])]

        @pl.when(step == 0)
        def _():
            for copy in copies_in(step):
                copy.start()
            _write_band_bias(bias_ref)

        @pl.when(step + 1 < n_steps)
        def _():
            for copy in copies_in(step + 1):
                copy.start()

        do_s, del_s = in_slots.at[slot, 0], in_slots.at[slot, 1]
        dq_s, dk_s, dv_s = (out_slots.at[slot, j] for j in range(3))
        dk_s[...] = jnp.zeros_like(dk_s)
        dv_s[...] = jnp.zeros_like(dv_s)
        for copy in copies_in(step):
            copy.wait()
        h0 = _head0_lanes()
        for pi, d in enumerate(DILATIONS):
            first = pi == 0

            def load(rows, keys, which, pi=pi):
                return dict(rows=rows, keys=keys, q=_take(q_ref, rows), g=_take(do_s, rows),
                            lse=_take(lse_ref, rows), delta=_take(del_s, rows),
                            k=_take(k_ref, keys).astype(BF16), v=_take(v_ref, keys).astype(BF16),
                            bias=bias_ref[2 * pi + which])

            def per_head(t):
                swapped = pltpu.roll(t, HEAD_DIM, 1)
                both = jnp.concatenate([jnp.where(h0, t, swapped), jnp.where(h0, swapped, t)], axis=0)
                return jnp.concatenate([both, both], axis=1)

            def compute(item):
                q2, g2 = _stack_heads(item["q"] * SCORE_SCALE, h0), _stack_heads(item["g"], h0)
                s = _mm_nt(q2, item["k"]) + item["bias"]
                p = jnp.exp(s - per_head(item["lse"]))
                dp = _mm_nt(g2, item["v"])
                ds = (p * (dp - per_head(item["delta"]))).astype(BF16)
                dq2 = _mm(ds, item["k"])
                dq = jnp.where(h0, dq2[:ATTN_BLOCK], dq2[ATTN_BLOCK:]) * SCORE_SCALE
                return dq, _mm_tn(ds, q2), _mm_tn(p.astype(BF16), g2)

            def store(item, res, first=first):
                _put(dq_s, item["rows"], res[0], add=not first)
                _put(dk_s, item["keys"], res[1], add=True)
                _put(dv_s, item["keys"], res[2], add=True)

            _for_each_group(d, ATTN_GROUP_BWD, load, compute, store)

        @pl.when(step > 0)
        def _():
            for copy in copies_out(step - 1):
                copy.wait()

        for copy in copies_out(step):
            copy.start()

        @pl.when(step == n_steps - 1)
        def _():
            for copy in copies_out(step):
                copy.wait()

    slab = lambda g: pl.BlockSpec((SEQ, LANES), functools.partial(lambda hp, g: (0, 4 * g + hp), g=g))
    anywhere = pl.BlockSpec(memory_space=pl.ANY)
    by_residue = (SORT_ROWS, SORT_RESIDUES, ATTN_WIDTH)
    grads = pl.pallas_call(
        body, name="attn_bwd", grid=(4,), out_shape=(jax.ShapeDtypeStruct(by_residue, F32),) * 3,
        scratch_shapes=[pltpu.VMEM((2, 2, SEQ, LANES), F32), pltpu.VMEM((2, 3, SEQ, LANES), F32),
                        pltpu.VMEM((2 * len(DILATIONS), 2 * ATTN_BLOCK, 2 * ATTN_BLOCK), F32),
                        pltpu.SemaphoreType.DMA((2, 2, SORT_RESIDUES)), pltpu.SemaphoreType.DMA((2, 3, SORT_RESIDUES))],
        in_specs=[slab(0), slab(1), slab(2), anywhere, slab(0), anywhere], out_specs=(anywhere,) * 3,
        compiler_params=_params(("arbitrary",)),
    )(qkv_sorted, qkv_sorted, qkv_sorted, d_out.reshape(by_residue), lse_sorted, delta.reshape(by_residue))
    return tuple(g.reshape(SEQ, ATTN_WIDTH) for g in grads)


def _hgrn_lower_bound(lb_ref):
    r0, r1 = lb_ref[0:1, :], lb_ref[1:2, :]
    mx = jnp.maximum(r0, r1)
    e0, e1 = jnp.exp(r0 - mx), jnp.exp(r1 - mx)
    return e0 / (e0 + e1)


def _hgrn_gates(hq, hf, lb):
    sq = _sigmoid(hq)
    sg = _sigmoid(hf)
    f = lb + (1.0 - lb) * sg
    return hq * sq, sq, sg, f, 1.0 - f, jnp.log(f)


HGRN_PAIR = 4
HGRN_SEQ_BLOCK = 1024
HGRN_GROUP = 4
HGRN_ROWS = HGRN_GROUP * HGRN_CHUNK


def _hgrn_specs(reverse):
    n_blocks = SEQ // HGRN_SEQ_BLOCK
    width = HGRN_PAIR * HGRN_DIM
    blk = (lambda s: n_blocks - 1 - s) if reverse else (lambda s: s)
    cols = lambda g: pl.BlockSpec((None, HGRN_SEQ_BLOCK, width), functools.partial(lambda p, s, g: (g, blk(s), p), g=g))
    pair = pl.BlockSpec((HGRN_SEQ_BLOCK, width), lambda p, s: (blk(s), p))
    lb = pl.BlockSpec((2, width), lambda p, s: (0, p))
    states = pl.BlockSpec((HGRN_PAIR, HGRN_SEQ_BLOCK // HGRN_CHUNK, HGRN_DIM, HGRN_DIM),
                          lambda p, s: (p, blk(s), 0, 0))
    return cols, pair, lb, states


def _chunk_masks():
    ri = lax.broadcasted_iota(jnp.int32, (HGRN_ROWS, HGRN_ROWS), 0)
    ci = lax.broadcasted_iota(jnp.int32, (HGRN_ROWS, HGRN_ROWS), 1)
    same = (ri // HGRN_CHUNK) == (ci // HGRN_CHUNK)
    return same, same & (ri >= ci), same & (ri <= ci)


def _mm_select(sel, v):
    hi = v.astype(BF16)
    r1 = v - hi.astype(F32)
    mid = r1.astype(BF16)
    lo = (r1 - mid.astype(F32)).astype(BF16)
    return _mm(sel, hi) + _mm(sel, mid) + _mm(sel, lo)


def _head_cols(a, h):
    return a[:, HGRN_DIM * h:HGRN_DIM * (h + 1)]


def _hgrn_fwd(proj, lb_raw):
    t, rws = HGRN_CHUNK, HGRN_ROWS

    def body(hq_ref, hf_ref, hi_ref, lb_ref, rec_ref, st_ref, state):
        @pl.when(pl.program_id(1) == 0)
        def _():
            state[...] = jnp.zeros_like(state)

        lb = _hgrn_lower_bound(lb_ref)
        same, causal, _ = _chunk_masks()
        sel = jnp.concatenate([causal, same], axis=0).astype(BF16)

        def group(g, sts):
            rows = pl.ds(pl.multiple_of(g * rws, rws), rws)
            q, _, _, _, k, lf = _hgrn_gates(hq_ref[rows, :], hf_ref[rows, :], lb)
            sums = _mm_select(sel, lf)
            cum, last = sums[:rws], sums[rws:]
            qd = (q * jnp.exp(cum)).astype(BF16)
            ki = (k * jnp.exp(-cum)).astype(BF16)
            ke = (k * jnp.exp(last - cum)).astype(BF16)
            vb = hi_ref[rows, :].astype(BF16)
            dec = jnp.exp(last)
            new_sts, recs = [], []
            for h in range(HGRN_PAIR):
                qd_h, ke_h, vb_h = _head_cols(qd, h), _head_cols(ke, h), _head_cols(vb, h)
                att = jnp.where(causal, _mm_nt(qd_h, _head_cols(ki, h)), 0.0).astype(BF16)
                intra = _mm(att, vb_h)
                st = sts[h]
                outs = []
                for c in range(HGRN_GROUP):
                    sl = slice(c * t, (c + 1) * t)
                    st_ref[h, g * HGRN_GROUP + c] = st
                    outs.append(intra[sl] + _mm_nt(qd_h[sl], st.astype(BF16)))
                    st = st * _head_cols(dec[c * t:c * t + 1, :], h) + _mm_tn(vb_h[sl], ke_h[sl])
                new_sts.append(st)
                recs.append(jnp.concatenate(outs, axis=0))
            rec_ref[rows, :] = jnp.concatenate(recs, axis=1)
            return tuple(new_sts)

        sts = lax.fori_loop(0, HGRN_SEQ_BLOCK // rws, group, tuple(state[h] for h in range(HGRN_PAIR)))
        for h in range(HGRN_PAIR):
            state[h] = sts[h]

    cols, pair, lb, states = _hgrn_specs(reverse=False)
    return pl.pallas_call(
        body, name="hgrn_fwd", grid=(HGRN_HEADS // HGRN_PAIR, SEQ // HGRN_SEQ_BLOCK),
        out_shape=(jax.ShapeDtypeStruct((SEQ, HGRN_WIDTH), F32),
                   jax.ShapeDtypeStruct((HGRN_HEADS, N_CHUNKS, HGRN_DIM, HGRN_DIM), F32)),
        in_specs=[cols(4), cols(5), cols(6), lb], out_specs=(pair, states),
        scratch_shapes=[pltpu.VMEM((HGRN_PAIR, HGRN_DIM, HGRN_DIM), F32)],
        compiler_params=_params(("parallel", "arbitrary")),
    )(proj, proj, proj, lb_raw)


def _hgrn_bwd(proj, lb_raw, d_rec, states):
    t, rws = HGRN_CHUNK, HGRN_ROWS

    def body(hq_ref, hf_ref, hi_ref, lb_ref, do_ref, st_ref, dhq_ref, dhf_ref, dhi_ref, dlb_ref,
             dstate, dlb_acc):
        lb = _hgrn_lower_bound(lb_ref)
        same, causal, anti = _chunk_masks()
        sel = jnp.concatenate([causal, same], axis=0).astype(BF16)
        sel_t = jnp.concatenate([anti, same], axis=1).astype(BF16)
        @pl.when(pl.program_id(1) == 0)
        def _():
            dstate[...] = jnp.zeros_like(dstate)
            dlb_acc[...] = jnp.zeros_like(dlb_acc)

        n_groups = HGRN_SEQ_BLOCK // rws
        chunks = [slice(c * t, (c + 1) * t) for c in range(HGRN_GROUP)]

        def group(i, dsts_in):
            g = n_groups - 1 - i
            rows = pl.ds(pl.multiple_of(g * rws, rws), rws)
            hq = hq_ref[rows, :]
            q, sq, sg, f, k, lf = _hgrn_gates(hq, hf_ref[rows, :], lb)
            sums = _mm_select(sel, lf)
            cum, last = sums[:rws], sums[rws:]
            e_cum, e_inv, e_end, dec = jnp.exp(cum), jnp.exp(-cum), jnp.exp(last - cum), jnp.exp(last)
            qd, ki, ke = q * e_cum, k * e_inv, k * e_end
            qdb, kib, keb = qd.astype(BF16), ki.astype(BF16), ke.astype(BF16)
            vb = hi_ref[rows, :].astype(BF16)
            gb = do_ref[rows, :].astype(BF16)

            dsts_out, per_head = [], []
            for h in range(HGRN_PAIR):
                qdb_h, kib_h, keb_h = _head_cols(qdb, h), _head_cols(kib, h), _head_cols(keb, h)
                vb_h, gb_h = _head_cols(vb, h), _head_cols(gb, h)
                att = jnp.where(causal, _mm_nt(qdb_h, kib_h), 0.0).astype(BF16)
                datt = jnp.where(causal, _mm_nt(gb_h, vb_h), 0.0).astype(BF16)
                dv = _mm_tn(att, gb_h)
                dqd = _mm(datt, kib_h)
                dki = _mm_tn(datt, qdb_h)

                decs = [_head_cols(dec[c * t:c * t + 1, :], h) for c in range(HGRN_GROUP)]
                dsts = [None] * HGRN_GROUP
                dst = dsts_in[h]
                for c in reversed(range(HGRN_GROUP)):
                    dsts[c] = dst
                    dst = dst * decs[c] + _mm_tn(gb_h[chunks[c]], qdb_h[chunks[c]])
                dsts_out.append(dst)

                dv_x, dqd_x, dke, dlast_x = [], [], [], []
                for c, sl in enumerate(chunks):
                    st_prev = st_ref[h, g * HGRN_GROUP + c]
                    dstb = dsts[c].astype(BF16)
                    dv_x.append(_mm_nt(keb_h[sl], dstb))
                    dqd_x.append(_mm(gb_h[sl], st_prev.astype(BF16)))
                    dke.append(_mm(vb_h[sl], dstb))
                    ddec = jnp.sum(dsts[c] * st_prev, axis=0, keepdims=True)
                    dlast_x.append(jnp.broadcast_to(ddec * decs[c], (t, HGRN_DIM)))
                per_head.append((dv + jnp.concatenate(dv_x, axis=0), dqd + jnp.concatenate(dqd_x, axis=0),
                                 dki, jnp.concatenate(dke, axis=0), jnp.concatenate(dlast_x, axis=0)))
            dv, dqd, dki, dke, dlast = (jnp.concatenate(list(parts), axis=1) for parts in zip(*per_head))

            dq = dqd * e_cum
            dk = dki * e_inv + dke * e_end
            dke_ke = dke * ke
            dcum = dqd * qd - dki * ki - dke_ke
            dlf = _mm_select(sel_t, jnp.concatenate([dcum, dke_ke], axis=0)) + dlast
            df = dlf / f - dk
            dhq_ref[rows, :] = (dq * (sq * (1.0 + hq * (1.0 - sq)))).astype(BF16)
            dhf_ref[rows, :] = (df * (1.0 - lb) * (sg * (1.0 - sg))).astype(BF16)
            dhi_ref[rows, :] = dv.astype(BF16)
            dlb_acc[...] += jnp.sum(df * (1.0 - sg), axis=0, keepdims=True)
            return tuple(dsts_out)

        dsts = lax.fori_loop(0, n_groups, group, tuple(dstate[h] for h in range(HGRN_PAIR)))
        for h in range(HGRN_PAIR):
            dstate[h] = dsts[h]
        g0 = dlb_acc[...] * lb * (1.0 - lb)
        dlb_ref[...] = jnp.concatenate([g0, -g0], axis=0)

    cols, pair, lb_spec, st_spec = _hgrn_specs(reverse=True)
    wide = jax.ShapeDtypeStruct((SEQ, HGRN_WIDTH), BF16)
    return pl.pallas_call(
        body, name="hgrn_bwd", grid=(HGRN_HEADS // HGRN_PAIR, SEQ // HGRN_SEQ_BLOCK),
        out_shape=(wide, wide, wide, jax.ShapeDtypeStruct((2, HGRN_WIDTH), F32)),
        in_specs=[cols(4), cols(5), cols(6), lb_spec, pair, st_spec],
        out_specs=(pair, pair, pair, lb_spec),
        scratch_shapes=[pltpu.VMEM((HGRN_PAIR, HGRN_DIM, HGRN_DIM), F32),
                        pltpu.VMEM((1, HGRN_PAIR * HGRN_DIM), F32)],
        compiler_params=_params(("parallel", "arbitrary")),
    )(proj, proj, proj, lb_raw, d_rec, states)


def _group_sum(v, group):
    parts = []
    for s in range(v.shape[1] // LANES):
        slab = v[:, LANES * s:LANES * (s + 1)]
        if group == LANES:
            parts.append(jnp.broadcast_to(jnp.sum(slab, axis=-1, keepdims=True), slab.shape))
        else:
            h0 = lax.broadcasted_iota(jnp.int32, slab.shape, 1) < HEAD_DIM
            s0 = jnp.sum(jnp.where(h0, slab, 0.0), axis=-1, keepdims=True)
            s1 = jnp.sum(jnp.where(h0, 0.0, slab), axis=-1, keepdims=True)
            parts.append(jnp.where(h0, s0, s1))
    return jnp.concatenate(parts, axis=1)


def _mid(attn_o, rec, proj, x, target, w_out_g, attn_w, hgrn_w, final_w):
    tm = 256

    def branch_fwd(o, gate, w, group):
        r = lax.rsqrt(_group_sum(o * o, group) * (1.0 / group) + NORM_EPS)
        nrm = o * r
        sg = _sigmoid(gate)
        return r, nrm, sg, nrm * w * (gate * sg)

    def branch_bwd(dy, r, nrm, sg, gate, w, group):
        silu = gate * sg
        d_gate = dy * nrm * w * (sg * (1.0 + gate * (1.0 - sg)))
        d_w = jnp.sum(dy * nrm * silu, axis=0, keepdims=True)
        dn = dy * w * silu
        d_o = r * (dn - nrm * (_group_sum(dn * nrm, group) * (1.0 / group)))
        return d_o, d_gate, d_w

    def body(o_ref, rec_ref, ag_ref, hg_ref, x_ref, tgt_ref, wout_ref, aw_ref, hw_ref, fw_ref,
             dx2_ref, do_ref, delta_ref, dag_ref, drec_ref, dhg_ref, dwout_ref, dfw_ref, daw_ref, dhw_ref,
             loss_ref, dwout_acc):
        i = pl.program_id(0)

        @pl.when(i == 0)
        def _():
            dwout_acc[...] = jnp.zeros_like(dwout_acc)
            dfw_ref[...] = jnp.zeros_like(dfw_ref)
            daw_ref[...] = jnp.zeros_like(daw_ref)
            dhw_ref[...] = jnp.zeros_like(dhw_ref)
            loss_ref[...] = jnp.zeros_like(loss_ref)

        o, rc, ag, hg = o_ref[...], rec_ref[...], ag_ref[...], hg_ref[...]
        aw, hw, fw = aw_ref[...], hw_ref[...], fw_ref[...]
        ra, na, sga, ya = branch_fwd(o, ag, aw, HEAD_DIM)
        rh, nh, sgh, yh = branch_fwd(rc, hg, hw, HGRN_DIM)
        mixed = jnp.concatenate([ya, yh], axis=1).astype(BF16)
        wout = wout_ref[...]
        x2 = x_ref[...] + _mm(mixed, wout)
        rstd = lax.rsqrt(jnp.mean(x2 * x2, axis=-1, keepdims=True) + NORM_EPS)
        xn = x2 * rstd
        err = xn * fw - tgt_ref[...]
        row_loss = jnp.mean(err * err, axis=-1, keepdims=True)
        loss_ref[...] += 0.5 * jnp.sum(row_loss, axis=0, keepdims=True)
        dy = err * (1.0 / D_MODEL)
        dfw_ref[...] += jnp.sum(dy * xn, axis=0, keepdims=True)
        dxn = dy * fw
        dx2 = rstd * (dxn - xn * jnp.mean(dxn * xn, axis=-1, keepdims=True))
        dx2_ref[...] = dx2
        dx2b = dx2.astype(BF16)
        dwout_acc[...] += _mm_tn(mixed, dx2b)

        @pl.when(i == pl.num_programs(0) - 1)
        def _():
            dwout_ref[...] = dwout_acc[...].astype(BF16)

        dmixed = _mm_nt(dx2b, wout)

        d_o, d_ag, d_aw = branch_bwd(dmixed[:, :ATTN_WIDTH], ra, na, sga, ag, aw, HEAD_DIM)
        d_rec, d_hg, d_hw = branch_bwd(dmixed[:, ATTN_WIDTH:], rh, nh, sgh, hg, hw, HGRN_DIM)
        do_ref[...] = d_o
        delta_ref[...] = _group_sum(d_o * o, HEAD_DIM)
        dag_ref[...] = d_ag.astype(BF16)
        drec_ref[...] = d_rec
        dhg_ref[...] = d_hg.astype(BF16)
        daw_ref[...] += d_aw
        dhw_ref[...] += d_hw

    half = lambda: pl.BlockSpec((tm, COL_BLOCK), lambda i: (i, 0))
    full = lambda: pl.BlockSpec((tm, D_MODEL), lambda i: (i, 0))
    fixed = lambda r, c: pl.BlockSpec((r, c), lambda i: (0, 0))
    wide = jax.ShapeDtypeStruct((SEQ, COL_BLOCK), F32)
    wide_b = jax.ShapeDtypeStruct((SEQ, COL_BLOCK), BF16)
    return pl.pallas_call(
        body, name="mid", grid=(SEQ // tm,),
        out_shape=(jax.ShapeDtypeStruct((SEQ, D_MODEL), F32), wide, wide, wide_b, wide, wide_b,
                   jax.ShapeDtypeStruct((D_MODEL, D_MODEL), BF16),
                   jax.ShapeDtypeStruct((1, D_MODEL), F32), jax.ShapeDtypeStruct((1, COL_BLOCK), F32),
                   jax.ShapeDtypeStruct((1, COL_BLOCK), F32), jax.ShapeDtypeStruct((1, 1), F32)),
        scratch_shapes=[pltpu.VMEM((D_MODEL, D_MODEL), F32)],
        in_specs=[half(), half(),
                  pl.BlockSpec((None, tm, COL_BLOCK), lambda i: (3, i, 0)),
                  pl.BlockSpec((None, tm, COL_BLOCK), lambda i: (7, i, 0)),
                  full(), full(), fixed(D_MODEL, D_MODEL), fixed(1, COL_BLOCK), fixed(1, COL_BLOCK),
                  fixed(1, D_MODEL)],
        out_specs=(full(), half(), half(), half(), half(), half(), fixed(D_MODEL, D_MODEL),
                   fixed(1, D_MODEL), fixed(1, COL_BLOCK), fixed(1, COL_BLOCK), fixed(1, 1)),
        compiler_params=_params(("arbitrary",)),
    )(attn_o, rec, proj, proj, x, target, w_out_g, attn_w, hgrn_w, final_w)


def _in_proj_bwd_rows(d_groups, w_g, x, dx2, mix_w, rc, rsa, rsb):
    tm = 256

    def body(*refs):
        dg_refs = refs[:N_DEV]
        wg_ref, x_ref, dx2_ref, w_ref, c_ref, sa_ref, sb_ref, gx_ref, dpb_ref, dmw_ref = refs[N_DEV:]

        @pl.when(pl.program_id(0) == 0)
        def _():
            dmw_ref[...] = jnp.zeros_like(dmw_ref)

        parts = []
        for j in range(N_DEV):
            dp = dg_refs[j][...]
            if j < 2:
                dp = _rot_transposed(dp, c_ref[...], sa_ref[...], sb_ref[...])
            parts.append(dp.astype(BF16))
        dpb = jnp.concatenate(parts, axis=1)
        for j in range(N_DEV):
            dpb_ref[j] = parts[j]
        g = _mm_nt(dpb, wg_ref[...])
        xf = x_ref[...]
        rstd = lax.rsqrt(jnp.mean(xf * xf, axis=-1, keepdims=True) + NORM_EPS)
        xn = xf * rstd
        dmw_ref[...] += jnp.sum(g * xn, axis=0, keepdims=True)
        gw = g * w_ref[...]
        gx_ref[...] = dx2_ref[...] + rstd * (gw - xn * jnp.mean(gw * xn, axis=-1, keepdims=True))

    tile = lambda cols: pl.BlockSpec((tm, cols), lambda i: (i, 0))
    fixed = lambda r, c: pl.BlockSpec((r, c), lambda i: (0, 0))
    return pl.pallas_call(
        body, name="in_proj_bwd_rows", grid=(SEQ // tm,),
        out_shape=(jax.ShapeDtypeStruct((SEQ, D_MODEL), F32), jax.ShapeDtypeStruct((N_DEV, SEQ, COL_BLOCK), BF16),
                   jax.ShapeDtypeStruct((1, D_MODEL), F32)),
        in_specs=[tile(COL_BLOCK) for _ in range(N_DEV)] + [
            pl.BlockSpec((D_MODEL, IN_COLS), lambda i: (0, 0), pipeline_mode=pl.Buffered(1)),
            tile(D_MODEL), tile(D_MODEL), fixed(1, D_MODEL), tile(LANES), tile(LANES), tile(LANES)],
        out_specs=(tile(D_MODEL), pl.BlockSpec((N_DEV, tm, COL_BLOCK), lambda i: (0, i, 0)), fixed(1, D_MODEL)),
        compiler_params=_params(("arbitrary",)),
    )(*d_groups, w_g, x, dx2, mix_w, rc, rsa, rsb)


def _weights_exchange(hn_t, dproj_b, dwout_p, small_p):
    n_chips = N_DEV // 2
    rb = 128
    S1_IN, S1_OUT, SMALL, S2_IN, S2_OUT, VIA_IN, VIA_OUT = 0, 4, 8, 15, 17, 19, 21
    rel_of_pair = (3, 1, 2, 0)
    two_hop = n_chips - 1
    half_in, half_out = COL_BLOCK // 2, D_MODEL // 2

    def body(order_ref, hnt_ref, dp_ref, dwout_ref, small_ref, gin_ref, gout_ref, gs_ref,
             part, s1_send, s1_in, s1_out, fwd_in, fwd_out, s2_in, s2_out, via_in, via_out, land_s,
             send_sems, recv_sems):
        t = pl.program_id(0)
        me = _my_place()
        x, y, c = me
        my_chip = 2 * x + y
        sibling = (x, y, 1 - c)

        def remote(slot, src, dst, to):
            return pltpu.make_async_remote_copy(src_ref=src, dst_ref=dst, send_sem=send_sems.at[slot],
                                                recv_sem=recv_sems.at[slot], device_id=to, device_id_type=MESH)

        def s1_in_copy(pair):
            return remote(S1_IN + pair, s1_send.at[pair], s1_in.at[pair], sibling)

        def s1_out_copy(pair):
            q = my_chip ^ rel_of_pair[pair]
            return remote(S1_OUT + pair, dwout_ref.at[q, 1 - c], s1_out.at[pair], sibling)

        def s2_copies(rel):
            peer = _peer(me, 2 * rel)
            return [remote(S2_IN + rel - 1, fwd_in.at[rel - 1], s2_in.at[rel - 1], peer),
                    remote(S2_OUT + rel - 1, fwd_out.at[rel - 1], s2_out.at[rel - 1], peer)]

        def via_copies(k):
            peer = _peer(me, 2 * (2 - k))
            return [remote(VIA_IN + k, fwd_in.at[two_hop - 1, :, pl.ds(k * half_in, half_in)], via_in.at[k], peer),
                    remote(VIA_OUT + k, fwd_out.at[two_hop - 1, :, pl.ds(k * half_out, half_out)], via_out.at[k],
                           peer)]

        def small_copy(rel):
            return remote(SMALL + rel - 1, small_ref, land_s.at[rel], _peer(me, rel))

        @pl.when(t == 0)
        def _():
            land_s[0] = small_ref[...]
            for pair in range(n_chips):
                s1_out_copy(pair).start()
            for rel in range(1, N_DEV):
                small_copy(rel).start()

        part[...] = _mm(hnt_ref[...], dp_ref[...])

        def rows_loop(n_rows, fn):
            def step(b, carry):
                fn(pl.ds(pl.multiple_of(b * rb, rb), rb))
                return carry
            lax.fori_loop(0, n_rows // rb, step, 0)

        for pair, rel in enumerate(rel_of_pair):
            @pl.when(t == 2 * pair)
            def _(pair=pair):
                s1_send[pair] = part[...].astype(BF16)
                s1_in_copy(pair).start()

            @pl.when(t == 2 * pair + 1)
            def _(pair=pair, rel=rel):
                q = my_chip ^ rel
                s1_in_copy(pair).wait_recv()
                s1_out_copy(pair).wait_recv()
                dst_in = fwd_in.at[rel - 1] if rel else gin_ref
                dst_out = fwd_out.at[rel - 1] if rel else gout_ref
                passes_on = rel in (1, 2)
                if passes_on:
                    for cp in via_copies(rel - 1):
                        cp.wait_recv()

                def with_half(val, via, rows, width):
                    if not passes_on:
                        return val
                    extra = via[rel - 1, rows, :].astype(F32)
                    halves = [val[:, :width], val[:, width:]]
                    halves[rel - 1] = halves[rel - 1] + extra
                    return jnp.concatenate(halves, axis=1)

                def add_in(rows):
                    val = part[rows, :] + s1_in[pair, rows, :].astype(F32)
                    dst_in[rows, :] = with_half(val, via_in, rows, half_in).astype(dst_in.dtype)

                def add_out(rows):
                    val = dwout_ref[q, c, rows, :].astype(F32) + s1_out[pair, rows, :].astype(F32)
                    dst_out[rows, :] = with_half(val, via_out, rows, half_out).astype(dst_out.dtype)

                rows_loop(D_MODEL, add_in)
                rows_loop(WOUT_ROWS, add_out)
                if rel == two_hop:
                    for k in range(2):
                        for cp in via_copies(k):
                            cp.start()
                elif rel:
                    for cp in s2_copies(rel):
                        cp.start()

        @pl.when(t == N_DEV - 1)
        def _():
            for rel in range(1, two_hop):
                for cp in s2_copies(rel):
                    cp.wait_recv()

            def total_in(rows):
                g = gin_ref[rows, :]
                for rel in range(1, two_hop):
                    g = g + s2_in[rel - 1, rows, :].astype(F32)
                gin_ref[rows, :] = g

            def total_out(rows):
                g = gout_ref[rows, :]
                for rel in range(1, two_hop):
                    g = g + s2_out[rel - 1, rows, :].astype(F32)
                gout_ref[rows, :] = g

            rows_loop(D_MODEL, total_in)
            rows_loop(WOUT_ROWS, total_out)

            for rel in range(1, N_DEV):
                small_copy(rel).wait_recv()
            my_flat = _flat(me)
            g = land_s[my_flat ^ 0]
            for dev in range(1, N_DEV):
                g = g + land_s[my_flat ^ dev]
            gs_ref[...] = g

            for pair in range(n_chips):
                s1_in_copy(pair).wait_send()
                s1_out_copy(pair).wait_send()
            for rel in range(1, two_hop):
                for cp in s2_copies(rel) + via_copies(rel - 1):
                    cp.wait_send()
            for rel in range(1, N_DEV):
                small_copy(rel).wait_send()

    place_x, place_y, place_c = _my_place()
    my_chip = 2 * place_x + place_y
    order = jnp.stack([2 * (my_chip ^ rel) + core for rel in rel_of_pair
                       for core in (1 - place_c, place_c)]).astype(jnp.int32)

    whole = lambda: pl.BlockSpec(memory_space=pltpu.VMEM)
    in_blocks = lambda n: pltpu.VMEM((n, D_MODEL, COL_BLOCK), BF16)
    out_blocks = lambda n: pltpu.VMEM((n, WOUT_ROWS, D_MODEL), BF16)
    grid_spec = pltpu.PrefetchScalarGridSpec(
        num_scalar_prefetch=1, grid=(N_DEV,),
        in_specs=[pl.BlockSpec((D_MODEL, SEQ), lambda t, order: (0, 0), pipeline_mode=pl.Buffered(1)),
                  pl.BlockSpec((None, SEQ, COL_BLOCK), lambda t, order: (order[t], 0, 0)), whole(), whole()],
        out_specs=(whole(), whole(), whole()),
        scratch_shapes=[pltpu.VMEM((D_MODEL, COL_BLOCK), F32), in_blocks(n_chips), in_blocks(n_chips),
                        out_blocks(n_chips), in_blocks(n_chips - 1), out_blocks(n_chips - 1),
                        in_blocks(n_chips - 2), out_blocks(n_chips - 2),
                        pltpu.VMEM((2, D_MODEL, half_in), BF16), pltpu.VMEM((2, WOUT_ROWS, half_out), BF16),
                        pltpu.VMEM((N_DEV, SMALL_ROWS, LANES), F32),
                        pltpu.SemaphoreType.DMA((23,)), pltpu.SemaphoreType.DMA((23,))])
    return pl.pallas_call(
        body, name="weights_exchange", grid_spec=grid_spec,
        out_shape=(jax.ShapeDtypeStruct((D_MODEL, COL_BLOCK), F32), jax.ShapeDtypeStruct((WOUT_ROWS, D_MODEL), F32),
                   jax.ShapeDtypeStruct((SMALL_ROWS, LANES), F32)),
        compiler_params=_params(("arbitrary",)),
    )(order, hn_t, dproj_b, dwout_p.reshape(n_chips, 2, WOUT_ROWS, D_MODEL), small_p)


def _adamw(w, g, m, v):
    m = ADAM_B1 * m + (1.0 - ADAM_B1) * g
    v = ADAM_B2 * v + (1.0 - ADAM_B2) * (g * g)
    m_hat = m / (1.0 - ADAM_B1 ** ADAM_STEP)
    v_hat = v / (1.0 - ADAM_B2 ** ADAM_STEP)
    delta = -ADAM_LR * (m_hat / (jnp.sqrt(v_hat) + ADAM_EPS) + ADAM_WD * w)
    return delta, m, v


def _adamw_update(grads, weights, m_old, v_old):
    rb = 256

    def body(*refs):
        g_refs, w_refs, m_refs, v_refs = refs[0:3], refs[3:6], refs[6:9], refs[9:12]
        d_refs, nm_refs, nv_refs = refs[12:15], refs[15:18], refs[18:21]
        for k in range(3):
            n_rows = g_refs[k].shape[0]
            step_rows = min(rb, n_rows)

            def step(b, carry, k=k, step_rows=step_rows):
                rows = pl.ds(pl.multiple_of(b * step_rows, 8), step_rows)
                delta, nm, nv = _adamw(w_refs[k][rows, :], g_refs[k][rows, :], m_refs[k][rows, :], v_refs[k][rows, :])
                d_refs[k][rows, :] = delta
                nm_refs[k][rows, :] = nm
                nv_refs[k][rows, :] = nv
                return carry

            lax.fori_loop(0, n_rows // step_rows, step, 0)

    shapes = tuple(jax.ShapeDtypeStruct(g.shape, F32) for g in grads)
    vm = lambda: pl.BlockSpec(memory_space=pltpu.VMEM)
    outs = pl.pallas_call(
        body, name="adamw_update", out_shape=shapes * 3,
        in_specs=[vm() for _ in range(12)], out_specs=tuple(vm() for _ in range(9)),
        compiler_params=_params(),
    )(*grads, *weights, *m_old, *v_old)
    return outs[0:3], outs[3:6], outs[6:9]


def _pack_small(mix, attn, hgrn, lb, final, loss=None):
    def rows8(a):
        a = a.reshape(-1, LANES)
        return jnp.pad(a, ((0, 8 - a.shape[0]), (0, 0)))
    last = jnp.zeros((8, LANES), F32) if loss is None else jnp.pad(loss.reshape(1, 1), ((0, 7), (0, LANES - 1)))
    return jnp.concatenate([rows8(mix), rows8(attn), rows8(hgrn), rows8(lb), rows8(final), last], axis=0)


def _unpack_small(slab):
    return (slab[ROW_MIX:ROW_MIX + 8].reshape(1, D_MODEL), slab[ROW_ATTN:ROW_ATTN + 4].reshape(1, ATTN_WIDTH),
            slab[ROW_HGRN:ROW_HGRN + 4].reshape(1, HGRN_WIDTH), slab[ROW_LB:ROW_LB + 8].reshape(2, HGRN_WIDTH),
            slab[ROW_FINAL:ROW_FINAL + 8].reshape(D_MODEL))


def _rope(pos_row):
    j = np.arange(ROPE_ROWS)
    inv = np.where(j < ROPE_HALF, ROPE_THETA ** (-(j % ROPE_HALF) * (2.0 / ROPE_DIMS)), 0.0)
    e = np.arange(LANES) % HEAD_DIM
    hit = (j[:, None] == (e % ROPE_HALF)[None, :]) & (j[:, None] < ROPE_HALF)
    sel = np.stack([hit & (e < ROPE_DIMS), hit & (e >= ROPE_HALF) & (e < ROPE_DIMS),
                    -1.0 * (hit & (e < ROPE_HALF))]).astype(np.float32)
    return _rope_tables(pos_row, jnp.asarray(inv.astype(np.float32).reshape(ROPE_ROWS, 1)),
                        jnp.asarray(sel, dtype=BF16))


def _local_step(x, proj, qkv_sorted, w_in_g, w_out_g, tables, mix_w, attn_w, hgrn_w, lb_raw, final_w, target):
    rc, rsa, rsb = tables
    attn_o, lse = _attn_fwd_fused(qkv_sorted)
    rec, states = _hgrn_fwd(proj, lb_raw)

    (dx2, d_o, delta, d_ag, d_rec, d_hg, dwout_p, d_final, d_attn_w, d_hgrn_w, loss) = _mid(
        attn_o, rec, proj, x, target, w_out_g, attn_w, hgrn_w, final_w.reshape(1, D_MODEL))

    dqkv = _attn_bwd_fused(qkv_sorted, d_o, lse, delta)
    d_hq, d_hf, d_hi, d_lb = _hgrn_bwd(proj, lb_raw, d_rec, states)

    grad_x, dproj_b, d_mix = _in_proj_bwd_rows(
        (dqkv[0], dqkv[1], dqkv[2], d_ag, d_hq, d_hf, d_hi, d_hg), w_in_g, x, dx2, mix_w, rc, rsa, rsb)
    small_p = _pack_small(d_mix, d_attn_w, d_hgrn_w, d_lb, d_final, loss)
    return grad_x, dproj_b, dwout_p, small_p


def kernel(x, positions, w_in, w_out, mix_norm_w, attn_out_norm_w, hgrn_out_norm_w, hgrn_lb_raw, final_norm_w, loss_target, m_w_in, m_w_out, m_mix_norm_w, m_attn_out_norm_w, m_hgrn_out_norm_w, m_hgrn_lb_raw, m_final_norm_w, v_w_in, v_w_out, v_mix_norm_w, v_attn_out_norm_w, v_hgrn_out_norm_w, v_hgrn_lb_raw, v_final_norm_w):
    tables = _rope(positions)
    proj, hn_t, w_in_g, w_out_g, qkv_sorted = _gather_project(x[0], mix_norm_w, w_in[0], w_out[0], *tables)
    grad_x, dproj_b, dwout_p, small_p = _local_step(
        x[0], proj, qkv_sorted, w_in_g, w_out_g, tables, mix_norm_w, attn_out_norm_w, hgrn_out_norm_w,
        hgrn_lb_raw, final_norm_w, loss_target[0])
    g_in, g_out, g_s = _weights_exchange(hn_t, dproj_b, dwout_p, small_p)

    w_s = _pack_small(mix_norm_w, attn_out_norm_w, hgrn_out_norm_w, hgrn_lb_raw, final_norm_w)
    m_s = _pack_small(m_mix_norm_w, m_attn_out_norm_w, m_hgrn_out_norm_w, m_hgrn_lb_raw, m_final_norm_w)
    v_s = _pack_small(v_mix_norm_w, v_attn_out_norm_w, v_hgrn_out_norm_w, v_hgrn_lb_raw, v_final_norm_w)
    (d_in, d_out, d_s), (nm_in, nm_out, nm_s), (nv_in, nv_out, nv_s) = _adamw_update(
        (g_in, g_out, g_s), (w_in[0], w_out[0], w_s), (m_w_in[0], m_w_out[0], m_s), (v_w_in[0], v_w_out[0], v_s))

    loss = g_s[ROW_LOSS, 0]
    return (loss, grad_x[None], g_in[None], g_out[None], *_unpack_small(g_s),
            d_in[None], d_out[None], *_unpack_small(d_s),
            nm_in[None], nm_out[None], *_unpack_small(nm_s),
            nv_in[None], nv_out[None], *_unpack_small(nv_s))
```

```python
import functools

import jax
import jax.numpy as jnp
import numpy as np
from jax import lax
from jax.experimental import pallas as pl
from jax.experimental.pallas import tpu as pltpu

F32 = jnp.float32
BF16 = jnp.bfloat16

SEQ = 4096
D_MODEL = 1024
ATTN_WIDTH = 512
HGRN_WIDTH = 512
HEAD_DIM = 64
HGRN_HEADS = 4
HGRN_DIM = 128
HGRN_CHUNK = 64
N_CHUNKS = SEQ // HGRN_CHUNK
IN_COLS = 4096
COL_BLOCK = 512
N_DEV = 8
WOUT_ROWS = D_MODEL // N_DEV
ATTN_BLOCK = 128
DILATIONS = (1, 4, 16)
ROPE_THETA = 500000.0
ROPE_DIMS = 16
ROPE_HALF = 8
NORM_EPS = 1e-6
NEG_BIG = -1e30
LANES = 128

ADAM_LR = 0.001
ADAM_B1 = 0.9
ADAM_B2 = 0.999
ADAM_EPS = 1e-08
ADAM_WD = 0.01
ADAM_STEP = 10

SMALL_ROWS = 48
ROW_MIX, ROW_ATTN, ROW_HGRN, ROW_LB, ROW_FINAL, ROW_LOSS = 0, 8, 16, 24, 32, 40

VMEM_LIMIT = 56 * 1024 * 1024
MESH = pl.DeviceIdType.MESH


def _mm(a, b):
    return lax.dot_general(a, b, (((1,), (0,)), ((), ())), preferred_element_type=F32)


def _mm_nt(a, b):
    return lax.dot_general(a, b, (((1,), (1,)), ((), ())), preferred_element_type=F32)


def _mm_tn(a, b):
    return lax.dot_general(a, b, (((0,), (0,)), ((), ())), preferred_element_type=F32)


def _mm_exact(a, b):
    return lax.dot_general(a, b, (((1,), (0,)), ((), ())), preferred_element_type=F32,
                           precision=lax.Precision.HIGHEST)


def _sigmoid(v):
    return 1.0 / (1.0 + jnp.exp(-v))


def _params(sem=None, **kw):
    return pltpu.CompilerParams(dimension_semantics=sem, vmem_limit_bytes=VMEM_LIMIT, **kw)


def _my_place():
    return lax.axis_index("x"), lax.axis_index("y"), lax.axis_index("c")


def _peer(place, rel):
    x, y, c = place
    return (x ^ ((rel >> 2) & 1), y ^ ((rel >> 1) & 1), c ^ (rel & 1))


def _flat(place):
    x, y, c = place
    return 4 * x + 2 * y + c


ROPE_ROWS = 16


def _rope_tables(pos_row, inv_freq_col, selectors):
    def body(pos_ref, invf_ref, sel_ref, c_ref, sa_ref, sb_ref):
        ang = pos_ref[...].astype(F32) * invf_ref[...]
        cos, sin = jnp.cos(ang), jnp.sin(ang)

        def spread(v, sel):
            hi = v.astype(BF16)
            r1 = v - hi.astype(F32)
            mid = r1.astype(BF16)
            lo = (r1 - mid.astype(F32)).astype(BF16)
            return _mm_tn(hi, sel) + _mm_tn(mid, sel) + _mm_tn(lo, sel)

        e = lax.broadcasted_iota(jnp.int32, (1, LANES), 1) & (HEAD_DIM - 1)
        c_ref[...] = spread(cos, sel_ref[0]) + jnp.where(e < ROPE_DIMS, 0.0, 1.0)
        sa_ref[...] = spread(sin, sel_ref[1])
        sb_ref[...] = spread(sin, sel_ref[2])

    tab = jax.ShapeDtypeStruct((SEQ, LANES), F32)
    vm = lambda: pl.BlockSpec(memory_space=pltpu.VMEM)
    return pl.pallas_call(
        body, name="rope_tables", out_shape=(tab, tab, tab),
        in_specs=[vm(), vm(), vm()], out_specs=(vm(), vm(), vm()), compiler_params=_params(),
    )(pos_row, inv_freq_col, selectors)


def _per_slab(fn, t):
    return jnp.concatenate([fn(t[:, LANES * s:LANES * (s + 1)]) for s in range(t.shape[1] // LANES)], axis=1)


def _rot(t, c, sa, sb):
    return _per_slab(lambda u: u * c + pltpu.roll(u, ROPE_HALF, 1) * sa + pltpu.roll(u, LANES - ROPE_HALF, 1) * sb, t)


def _rot_transposed(g, c, sa, sb):
    return _per_slab(
        lambda u: u * c + pltpu.roll(u * sa, LANES - ROPE_HALF, 1) + pltpu.roll(u * sb, ROPE_HALF, 1), g)


def _gather_project(x, mix_w, w_in, w_out, rc, rsa, rsb):
    tm = 1024
    n_tiles = SEQ // tm
    arrival_of_step = (None, 0, 1, 2, 4, 5, 3, 6)

    def body(order_ref, x_ref, w_ref, win_ref, wout_ref, c_ref, sa_ref, sb_ref,
             proj_ref, hnt_ref, gin_hbm, gout_hbm, qkv_hbm,
             hn_s, w_land, wout_land, stage, sort_stage, send_sems, recv_sems, local_sems, sort_sems):
        g, i = pl.program_id(0), pl.program_id(1)
        me = _my_place()
        x_, y_, c_ = me
        sibling = (x_, y_, 1 - c_)
        chips = [(1 - x_, y_), (x_, 1 - y_), (1 - x_, 1 - y_)]

        def slab(which, place):
            idx = _flat(place)
            if which == 0:
                return w_land.at[idx]
            return wout_land.at[pl.ds(pl.multiple_of(idx * WOUT_ROWS, WOUT_ROWS), WOUT_ROWS), :]

        def remote(which, k, ref, to, src=None):
            return pltpu.make_async_remote_copy(
                src_ref=ref if src is None else src, dst_ref=ref, send_sem=send_sems.at[8 * which + k],
                recv_sem=recv_sems.at[8 * which + k], device_id=to, device_id_type=MESH)

        def copy(which, k, block, to, src=None):
            return remote(which, k, slab(which, block), to, src)

        def half(which, place, part):
            n = (D_MODEL if which == 0 else WOUT_ROWS) // 2
            if which == 0:
                return w_land.at[_flat(place), pl.ds(n * part, n), :]
            return wout_land.at[pl.ds(pl.multiple_of(_flat(place) * WOUT_ROWS + n * part, n), n), :]

        def first_copies(which):
            src = stage if which == 0 else None
            return ([copy(which, 0, me, sibling, src)]
                    + [copy(which, 1 + j, me, (*chips[j], c_), src) for j in range(2)])

        def relay(which, part):
            frm, to = (chips[1], chips[0]) if part == 0 else (chips[0], chips[1])
            return remote(which, 3 if part == 0 else 7, half(which, (*frm, c_), part), (*to, c_))

        def two_hop_half(which, part):
            return remote(which, 3 if part == 0 else 7, half(which, (*chips[2], c_), part), me)

        def pass_on(which, j):
            return copy(which, 4 + j, (*chips[j], c_), sibling)

        def arrival(which, k):
            if k == 0:
                return copy(which, 0, sibling, me)
            if k <= 2:
                return copy(which, k, (*chips[k - 1], c_), me)
            return copy(which, k, (*chips[k - 4], 1 - c_), me)

        def to_hbm(step):
            idx = order_ref[step]
            cols = pl.ds(pl.multiple_of(idx * COL_BLOCK, COL_BLOCK), COL_BLOCK)
            return pltpu.make_async_copy(w_land.at[idx], gin_hbm.at[:, cols], local_sems.at[step])

        @pl.when((g == 0) & (i == 0))
        def _():
            stage[...] = win_ref[...].astype(BF16)
            w_land[_flat(me)] = stage[...]
            wout_land[pl.ds(pl.multiple_of(_flat(me) * WOUT_ROWS, WOUT_ROWS), WOUT_ROWS), :] = (
                wout_ref[...].astype(BF16))
            for cp in first_copies(0) + first_copies(1)[:1]:
                cp.start()
            to_hbm(0).start()

        for step, k in enumerate(arrival_of_step):
            if k is None:
                continue

            @pl.when((g == step) & (i == 0))
            def _(k=k, step=step):
                if k == 3:
                    two_hop_half(0, 0).wait_recv()
                    two_hop_half(0, 1).wait_recv()
                else:
                    arrival(0, k).wait_recv()
                to_hbm(step).start()
                if 1 <= k <= 3:
                    pass_on(0, k - 1).start()
                if k == 1:
                    relay(0, 1).start()
                    for cp in first_copies(1)[1:]:
                        cp.start()
                if k == 2:
                    relay(0, 0).start()
                if k in (4, 5):
                    arrival(1, k - 3).wait_recv()
                    relay(1, 5 - k).start()

        rows = pl.ds(pl.multiple_of(i * tm, tm), tm)

        @pl.when(g == 0)
        def _():
            xf = x_ref[...]
            ms = jnp.mean(xf * xf, axis=-1, keepdims=True)
            hn = xf * lax.rsqrt(ms + NORM_EPS) * w_ref[...]
            hnt_ref[...] = hn.T.astype(BF16)
            hn_s[rows, :] = hn.astype(BF16)

        group = order_ref[g]

        def sorted_copy(tile_value):
            per = tm // SORT_RESIDUES
            cols = pl.ds(pl.multiple_of(group * COL_BLOCK, COL_BLOCK), COL_BLOCK)
            buf = i % 2

            def out_copies(tile, b):
                return [pltpu.make_async_copy(
                    sort_stage.at[b, :, r, :], qkv_hbm.at[r, pl.ds(pl.multiple_of(tile * per, per), per), cols],
                    sort_sems.at[b, r]) for r in range(SORT_RESIDUES)]

            @pl.when(i >= 2)
            def _():
                for copy in out_copies(i - 2, buf):
                    copy.wait()

            sort_stage[buf] = tile_value.reshape(per, SORT_RESIDUES, COL_BLOCK)
            for copy in out_copies(i, buf):
                copy.start()

            @pl.when(i == n_tiles - 1)
            def _():
                for copy in out_copies(i - 1, 1 - buf) + out_copies(i, buf):
                    copy.wait()

        @pl.when(group < 2)
        def _():
            rotated = _rot(_mm(hn_s[rows, :], w_land[group]), c_ref[...], sa_ref[...], sb_ref[...])
            proj_ref[...] = rotated
            sorted_copy(rotated)

        @pl.when(group == 2)
        def _():
            value = _mm(hn_s[rows, :], w_land[group])
            proj_ref[...] = value
            sorted_copy(value)

        @pl.when(group > 2)
        def _():
            proj_ref[...] = _mm(hn_s[rows, :], w_land[group])

        @pl.when((g == N_DEV - 1) & (i == n_tiles - 1))
        def _():
            pass_on(1, 0).start()
            pass_on(1, 1).start()
            two_hop_half(1, 0).wait_recv()
            two_hop_half(1, 1).wait_recv()
            pass_on(1, 2).start()
            for k in (0, 4, 5, 6):
                arrival(1, k).wait_recv()
            for which in (0, 1):
                for cp in (first_copies(which) + [relay(which, part) for part in range(2)]
                           + [pass_on(which, j) for j in range(3)]):
                    cp.wait_send()
            wout_copy = pltpu.make_async_copy(wout_land, gout_hbm, local_sems.at[N_DEV])
            wout_copy.start()
            for step in range(N_DEV):
                to_hbm(step).wait()
            wout_copy.wait()

    me = _my_place()
    x_, y_, c_ = me
    chips = [(1 - x_, y_), (x_, 1 - y_), (1 - x_, 1 - y_)]
    order = jnp.stack([_flat(p) for p in (
        me, (x_, y_, 1 - c_), (*chips[0], c_), (*chips[1], c_), (*chips[0], 1 - c_), (*chips[1], 1 - c_),
        (*chips[2], c_), (*chips[2], 1 - c_))]).astype(jnp.int32)

    first_sweep = lambda g, i, order: (jnp.where(g == 0, i, n_tiles - 1), 0)
    tab = pl.BlockSpec((tm, LANES), lambda g, i, order: (jnp.where(order[g] < 2, i, 0), 0))
    whole = lambda: pl.BlockSpec(memory_space=pltpu.VMEM)
    grid_spec = pltpu.PrefetchScalarGridSpec(
        num_scalar_prefetch=1, grid=(N_DEV, n_tiles),
        in_specs=[pl.BlockSpec((tm, D_MODEL), first_sweep),
                  pl.BlockSpec((1, D_MODEL), lambda g, i, order: (0, 0)),
                  whole(), whole(), tab, tab, tab],
        out_specs=(pl.BlockSpec((None, tm, COL_BLOCK), lambda g, i, order: (order[g], i, 0)),
                   pl.BlockSpec((D_MODEL, tm), lambda g, i, order: (0, jnp.where(g == 0, i, n_tiles - 1))),
                   pl.BlockSpec(memory_space=pl.ANY), pl.BlockSpec(memory_space=pl.ANY),
                   pl.BlockSpec(memory_space=pl.ANY)),
        scratch_shapes=[pltpu.VMEM((SEQ, D_MODEL), BF16),
                        pltpu.VMEM((N_DEV, D_MODEL, COL_BLOCK), BF16),
                        pltpu.VMEM((D_MODEL, D_MODEL), BF16),
                        pltpu.VMEM((D_MODEL, COL_BLOCK), BF16),
                        pltpu.VMEM((2, tm // SORT_RESIDUES, SORT_RESIDUES, COL_BLOCK), F32),
                        pltpu.SemaphoreType.DMA((16,)), pltpu.SemaphoreType.DMA((16,)),
                        pltpu.SemaphoreType.DMA((N_DEV + 1,)), pltpu.SemaphoreType.DMA((2, SORT_RESIDUES))])
    proj, hn_t, w_in_g, w_out_g, qkv_sorted = pl.pallas_call(
        body, name="gather_project", grid_spec=grid_spec,
        out_shape=(jax.ShapeDtypeStruct((N_DEV, SEQ, COL_BLOCK), F32), jax.ShapeDtypeStruct((D_MODEL, SEQ), BF16),
                   jax.ShapeDtypeStruct((D_MODEL, IN_COLS), BF16), jax.ShapeDtypeStruct((D_MODEL, D_MODEL), BF16),
                   jax.ShapeDtypeStruct((SORT_RESIDUES, SORT_ROWS, 3 * COL_BLOCK), F32)),
        compiler_params=_params(("arbitrary", "arbitrary")),
    )(order, x, mix_w, w_in, w_out, rc, rsa, rsb)
    return proj, hn_t, w_in_g, w_out_g, qkv_sorted.reshape(SEQ, 3 * COL_BLOCK)


SCORE_SCALE = HEAD_DIM ** -0.5
ATTN_GROUP_FWD = 32
ATTN_GROUP_BWD = 8
BLOCKS_PER_PATTERN = SEQ // ATTN_BLOCK
SORT_RESIDUES = 16
SORT_ROWS = SEQ // SORT_RESIDUES


def _write_band_bias(bias_ref):
    row = lax.broadcasted_iota(jnp.int32, (2 * ATTN_BLOCK, 2 * ATTN_BLOCK), 0) & (ATTN_BLOCK - 1)
    col = lax.broadcasted_iota(jnp.int32, (2 * ATTN_BLOCK, 2 * ATTN_BLOCK), 1)
    for pi, d in enumerate(DILATIONS):
        per = SORT_RESIDUES // d
        ahead = per * (row % (8 * d) - col % (16 * d)) + (row // (8 * d) - col // (16 * d))
        dist = ATTN_BLOCK + ahead
        bias_ref[2 * pi] = jnp.where((dist >= 0) & (dist <= ATTN_BLOCK), 0.0, NEG_BIG)
        bias_ref[2 * pi + 1] = jnp.where(ahead >= 0, 0.0, NEG_BIG)


def _head0_lanes():
    return lax.broadcasted_iota(jnp.int32, (ATTN_BLOCK, LANES), 1) < HEAD_DIM


def _stack_heads(t, h0):
    return jnp.concatenate([jnp.where(h0, t, 0.0), jnp.where(h0, 0.0, t)], axis=0).astype(BF16)


def _block_runs(i, d):
    nblk = BLOCKS_PER_PATTERN // d
    r, n = i // nblk, i % nblk
    kn = jnp.maximum(n - 1, 0)
    rows, keys = [], []
    for c in range(SORT_RESIDUES // d):
        base = SORT_ROWS * (c * d + r)
        rows.append(pl.ds(pl.multiple_of(base + 8 * d * n, 8), 8 * d))
        keys.append(pl.ds(pl.multiple_of(base + 8 * d * kn, 8), 16 * d))
    return rows, keys, (n == 0).astype(jnp.int32)


def _take(ref, runs):
    return jnp.concatenate([ref[run, :] for run in runs], axis=0)


def _put(ref, runs, value, add=False):
    at = 0
    for run in runs:
        piece = value[at:at + run.size]
        if add:
            ref[run, :] += piece
        else:
            ref[run, :] = piece
        at += run.size


def _sort_copies(src_hbm, lane_block, dst_ref, sem_ref):
    lanes = pl.ds(pl.multiple_of(LANES * lane_block, LANES), LANES)
    return [pltpu.make_async_copy(src_hbm.at[:, r, lanes], dst_ref.at[pl.ds(SORT_ROWS * r, SORT_ROWS), :],
                                  sem_ref.at[r]) for r in range(SORT_RESIDUES)]


def _unsort_copies(src_ref, dst_hbm, lane_block, sem_ref):
    lanes = pl.ds(pl.multiple_of(LANES * lane_block, LANES), LANES)
    return [pltpu.make_async_copy(src_ref.at[pl.ds(SORT_ROWS * r, SORT_ROWS), :], dst_hbm.at[:, r, lanes],
                                  sem_ref.at[r]) for r in range(SORT_RESIDUES)]


def _for_each_group(d, n_group, load, compute, store):
    def group(g, carry):
        items = [load(*_block_runs(g * n_group + u, d)) for u in range(n_group)]
        results = [compute(item) for item in items]
        for item, res in zip(items, results):
            store(item, res)
        return carry

    lax.fori_loop(0, BLOCKS_PER_PATTERN // n_group, group, 0)


def _attn_fwd_fused(qkv_sorted):
    n_pat = len(DILATIONS)
    tile2 = (2 * ATTN_BLOCK, LANES)

    def body(q_ref, k_ref, v_ref, o_hbm, lse_ref, o_slots, m_acc, l_acc, bias_ref, out_sem):
        step, n_steps = pl.program_id(0), pl.num_programs(0)
        pl.when(step == 0)(lambda: _write_band_bias(bias_ref))
        slot = step % 2
        o_acc = o_slots.at[slot]
        h0 = _head0_lanes()
        for pi, d in enumerate(DILATIONS):
            first, last = pi == 0, pi == n_pat - 1

            def load(rows, keys, which, first=first, pi=pi):
                item = dict(rows=rows, keys=keys, which=2 * pi + which)
                if not first:
                    item.update(o=_take(o_acc, rows), m=[_take(m_acc.at[h], rows) for h in range(2)],
                                l=[_take(l_acc.at[h], rows) for h in range(2)])
                return item

            def compute(item, first=first):
                kb = _take(k_ref, item["keys"]).astype(BF16)
                vb = _take(v_ref, item["keys"]).astype(BF16)
                s = _mm_nt(_stack_heads(_take(q_ref, item["rows"]) * SCORE_SCALE, h0), kb) + bias_ref[item["which"]]
                mb = jnp.max(s, axis=-1, keepdims=True)
                if first:
                    p = jnp.exp(s - mb)
                    mn = jnp.broadcast_to(mb, tile2)
                else:
                    m_old = jnp.concatenate(item["m"], axis=0)
                    mn = jnp.maximum(m_old, mb)
                    alpha = jnp.exp(m_old - mn)
                    p = jnp.exp(s - jnp.concatenate([mn, mn], axis=1))
                ls = jnp.sum(p, axis=-1, keepdims=True)
                pv = _mm(p.astype(BF16), vb)
                if first:
                    return pv, mn, jnp.broadcast_to(ls, tile2)
                o_old = jnp.concatenate([item["o"], item["o"]], axis=0)
                return alpha * o_old + pv, mn, alpha * jnp.concatenate(item["l"], axis=0) + ls

            def store(item, res, last=last):
                rows = item["rows"]
                (o0, o1), (m0, m1), (l0, l1) = ((a[:ATTN_BLOCK], a[ATTN_BLOCK:]) for a in res)
                if last:
                    _put(o_acc, rows, jnp.where(h0, o0 / l0, o1 / l1))
                    _put(lse_ref, rows, jnp.where(h0, m0 + jnp.log(l0), m1 + jnp.log(l1)))
                else:
                    _put(o_acc, rows, jnp.where(h0, o0, o1))
                    for h, (m, l) in enumerate(((m0, l0), (m1, l1))):
                        _put(m_acc.at[h], rows, m)
                        _put(l_acc.at[h], rows, l)

            _for_each_group(d, ATTN_GROUP_FWD, load, compute, store)

        def copies_out(of_step):
            return _unsort_copies(o_slots.at[of_step % 2], o_hbm, of_step, out_sem.at[of_step % 2])

        @pl.when(step > 0)
        def _():
            for copy in copies_out(step - 1):
                copy.wait()

        for copy in copies_out(step):
            copy.start()

        @pl.when(step == n_steps - 1)
        def _():
            for copy in copies_out(step):
                copy.wait()

    slab = lambda g: pl.BlockSpec((SEQ, LANES), functools.partial(lambda hp, g: (0, 4 * g + hp), g=g))
    wide = jax.ShapeDtypeStruct((SEQ, ATTN_WIDTH), F32)
    o_rows, lse = pl.pallas_call(
        body, name="attn_fwd", grid=(4,),
        out_shape=(jax.ShapeDtypeStruct((SORT_ROWS, SORT_RESIDUES, ATTN_WIDTH), F32), wide),
        in_specs=[slab(0), slab(1), slab(2)], out_specs=(pl.BlockSpec(memory_space=pl.ANY), slab(0)),
        scratch_shapes=[pltpu.VMEM((2, SEQ, LANES), F32), pltpu.VMEM((2, SEQ, LANES), F32),
                        pltpu.VMEM((2, SEQ, LANES), F32),
                        pltpu.VMEM((2 * len(DILATIONS), 2 * ATTN_BLOCK, 2 * ATTN_BLOCK), F32),
                        pltpu.SemaphoreType.DMA((2, SORT_RESIDUES))],
        compiler_params=_params(("arbitrary",)),
    )(qkv_sorted, qkv_sorted, qkv_sorted)
    return o_rows.reshape(SEQ, ATTN_WIDTH), lse


def _attn_bwd_fused(qkv_sorted, d_out, lse_sorted, delta):
    def body(q_ref, k_ref, v_ref, do_hbm, lse_ref, del_hbm, dq_hbm, dk_hbm, dv_hbm,
             in_slots, out_slots, bias_ref, in_sem, out_sem):
        step, n_steps = pl.program_id(0), pl.num_programs(0)
        slot = step % 2

        def copies_in(of_step):
            s = of_step % 2
            return [copy for j, hbm in enumerate((do_hbm, del_hbm))
                    for copy in _sort_copies(hbm, of_step, in_slots.at[s, j], in_sem.at[s, j])]

        def copies_out(of_step):
            s = of_step % 2
            return [copy for j, hbm in enumerate((dq_hbm, dk_hbm, dv_hbm))
                    for copy in _unsort_copies(out_slots.at[s, j], hbm, of_step, out_sem.at[s, j])]

        @pl.when(step == 0)
        def _():
            for copy in copies_in(step):
                copy.start()
            _write_band_bias(bias_ref)

        @pl.when(step + 1 < n_steps)
        def _():
            for copy in copies_in(step + 1):
                copy.start()

        do_s, del_s = in_slots.at[slot, 0], in_slots.at[slot, 1]
        dq_s, dk_s, dv_s = (out_slots.at[slot, j] for j in range(3))
        dk_s[...] = jnp.zeros_like(dk_s)
        dv_s[...] = jnp.zeros_like(dv_s)
        for copy in copies_in(step):
            copy.wait()
        h0 = _head0_lanes()
        for pi, d in enumerate(DILATIONS):
            first = pi == 0

            def load(rows, keys, which, pi=pi):
                return dict(rows=rows, keys=keys, q=_take(q_ref, rows), g=_take(do_s, rows),
                            lse=_take(lse_ref, rows), delta=_take(del_s, rows),
                            k=_take(k_ref, keys).astype(BF16), v=_take(v_ref, keys).astype(BF16),
                            bias=bias_ref[2 * pi + which])

            def per_head(t):
                swapped = pltpu.roll(t, HEAD_DIM, 1)
                both = jnp.concatenate([jnp.where(h0, t, swapped), jnp.where(h0, swapped, t)], axis=0)
                return jnp.concatenate([both, both], axis=1)

            def compute(item):
                q2, g2 = _stack_heads(item["q"] * SCORE_SCALE, h0), _stack_heads(item["g"], h0)
                s = _mm_nt(q2, item["k"]) + item["bias"]
                p = jnp.exp(s - per_head(item["lse"]))
                dp = _mm_nt(g2, item["v"])
                ds = (p * (dp - per_head(item["delta"]))).astype(BF16)
                dq2 = _mm(ds, item["k"])
                dq = jnp.where(h0, dq2[:ATTN_BLOCK], dq2[ATTN_BLOCK:]) * SCORE_SCALE
                return dq, _mm_tn(ds, q2), _mm_tn(p.astype(BF16), g2)

            def store(item, res, first=first):
                _put(dq_s, item["rows"], res[0], add=not first)
                _put(dk_s, item["keys"], res[1], add=True)
                _put(dv_s, item["keys"], res[2], add=True)

            _for_each_group(d, ATTN_GROUP_BWD, load, compute, store)

        @pl.when(step > 0)
        def _():
            for copy in copies_out(step - 1):
                copy.wait()

        for copy in copies_out(step):
            copy.start()

        @pl.when(step == n_steps - 1)
        def _():
            for copy in copies_out(step):
                copy.wait()

    slab = lambda g: pl.BlockSpec((SEQ, LANES), functools.partial(lambda hp, g: (0, 4 * g + hp), g=g))
    anywhere = pl.BlockSpec(memory_space=pl.ANY)
    by_residue = (SORT_ROWS, SORT_RESIDUES, ATTN_WIDTH)
    grads = pl.pallas_call(
        body, name="attn_bwd", grid=(4,), out_shape=(jax.ShapeDtypeStruct(by_residue, F32),) * 3,
        scratch_shapes=[pltpu.VMEM((2, 2, SEQ, LANES), F32), pltpu.VMEM((2, 3, SEQ, LANES), F32),
                        pltpu.VMEM((2 * len(DILATIONS), 2 * ATTN_BLOCK, 2 * ATTN_BLOCK), F32),
                        pltpu.SemaphoreType.DMA((2, 2, SORT_RESIDUES)), pltpu.SemaphoreType.DMA((2, 3, SORT_RESIDUES))],
        in_specs=[slab(0), slab(1), slab(2), anywhere, slab(0), anywhere], out_specs=(anywhere,) * 3,
        compiler_params=_params(("arbitrary",)),
    )(qkv_sorted, qkv_sorted, qkv_sorted, d_out.reshape(by_residue), lse_sorted, delta.reshape(by_residue))
    return tuple(g.reshape(SEQ, ATTN_WIDTH) for g in grads)


def _hgrn_lower_bound(lb_ref):
    r0, r1 = lb_ref[0:1, :], lb_ref[1:2, :]
    mx = jnp.maximum(r0, r1)
    e0, e1 = jnp.exp(r0 - mx), jnp.exp(r1 - mx)
    return e0 / (e0 + e1)


def _hgrn_gates(hq, hf, lb):
    sq = _sigmoid(hq)
    sg = _sigmoid(hf)
    f = lb + (1.0 - lb) * sg
    return hq * sq, sq, sg, f, 1.0 - f, jnp.log(f)


HGRN_PAIR = 4
HGRN_SEQ_BLOCK = 1024
HGRN_GROUP = 4
HGRN_ROWS = HGRN_GROUP * HGRN_CHUNK


def _hgrn_specs(reverse):
    n_blocks = SEQ // HGRN_SEQ_BLOCK
    width = HGRN_PAIR * HGRN_DIM
    blk = (lambda s: n_blocks - 1 - s) if reverse else (lambda s: s)
    cols = lambda g: pl.BlockSpec((None, HGRN_SEQ_BLOCK, width), functools.partial(lambda p, s, g: (g, blk(s), p), g=g))
    pair = pl.BlockSpec((HGRN_SEQ_BLOCK, width), lambda p, s: (blk(s), p))
    lb = pl.BlockSpec((2, width), lambda p, s: (0, p))
    states = pl.BlockSpec((HGRN_PAIR, HGRN_SEQ_BLOCK // HGRN_CHUNK, HGRN_DIM, HGRN_DIM),
                          lambda p, s: (p, blk(s), 0, 0))
    return cols, pair, lb, states


def _chunk_masks():
    ri = lax.broadcasted_iota(jnp.int32, (HGRN_ROWS, HGRN_ROWS), 0)
    ci = lax.broadcasted_iota(jnp.int32, (HGRN_ROWS, HGRN_ROWS), 1)
    same = (ri // HGRN_CHUNK) == (ci // HGRN_CHUNK)
    return same, same & (ri >= ci), same & (ri <= ci)


def _mm_select(sel, v):
    hi = v.astype(BF16)
    r1 = v - hi.astype(F32)
    mid = r1.astype(BF16)
    lo = (r1 - mid.astype(F32)).astype(BF16)
    return _mm(sel, hi) + _mm(sel, mid) + _mm(sel, lo)


def _head_cols(a, h):
    return a[:, HGRN_DIM * h:HGRN_DIM * (h + 1)]


def _hgrn_fwd(proj, lb_raw):
    t, rws = HGRN_CHUNK, HGRN_ROWS

    def body(hq_ref, hf_ref, hi_ref, lb_ref, rec_ref, st_ref, state):
        @pl.when(pl.program_id(1) == 0)
        def _():
            state[...] = jnp.zeros_like(state)

        lb = _hgrn_lower_bound(lb_ref)
        same, causal, _ = _chunk_masks()
        sel = jnp.concatenate([causal, same], axis=0).astype(BF16)

        def group(g, sts):
            rows = pl.ds(pl.multiple_of(g * rws, rws), rws)
            q, _, _, _, k, lf = _hgrn_gates(hq_ref[rows, :], hf_ref[rows, :], lb)
            sums = _mm_select(sel, lf)
            cum, last = sums[:rws], sums[rws:]
            qd = (q * jnp.exp(cum)).astype(BF16)
            ki = (k * jnp.exp(-cum)).astype(BF16)
            ke = (k * jnp.exp(last - cum)).astype(BF16)
            vb = hi_ref[rows, :].astype(BF16)
            dec = jnp.exp(last)
            new_sts, recs = [], []
            for h in range(HGRN_PAIR):
                qd_h, ke_h, vb_h = _head_cols(qd, h), _head_cols(ke, h), _head_cols(vb, h)
                att = jnp.where(causal, _mm_nt(qd_h, _head_cols(ki, h)), 0.0).astype(BF16)
                intra = _mm(att, vb_h)
                st = sts[h]
                outs = []
                for c in range(HGRN_GROUP):
                    sl = slice(c * t, (c + 1) * t)
                    st_ref[h, g * HGRN_GROUP + c] = st
                    outs.append(intra[sl] + _mm_nt(qd_h[sl], st.astype(BF16)))
                    st = st * _head_cols(dec[c * t:c * t + 1, :], h) + _mm_tn(vb_h[sl], ke_h[sl])
                new_sts.append(st)
                recs.append(jnp.concatenate(outs, axis=0))
            rec_ref[rows, :] = jnp.concatenate(recs, axis=1)
            return tuple(new_sts)

        sts = lax.fori_loop(0, HGRN_SEQ_BLOCK // rws, group, tuple(state[h] for h in range(HGRN_PAIR)))
        for h in range(HGRN_PAIR):
            state[h] = sts[h]

    cols, pair, lb, states = _hgrn_specs(reverse=False)
    return pl.pallas_call(
        body, name="hgrn_fwd", grid=(HGRN_HEADS // HGRN_PAIR, SEQ // HGRN_SEQ_BLOCK),
        out_shape=(jax.ShapeDtypeStruct((SEQ, HGRN_WIDTH), F32),
                   jax.ShapeDtypeStruct((HGRN_HEADS, N_CHUNKS, HGRN_DIM, HGRN_DIM), F32)),
        in_specs=[cols(4), cols(5), cols(6), lb], out_specs=(pair, states),
        scratch_shapes=[pltpu.VMEM((HGRN_PAIR, HGRN_DIM, HGRN_DIM), F32)],
        compiler_params=_params(("parallel", "arbitrary")),
    )(proj, proj, proj, lb_raw)


def _hgrn_bwd(proj, lb_raw, d_rec, states):
    t, rws = HGRN_CHUNK, HGRN_ROWS

    def body(hq_ref, hf_ref, hi_ref, lb_ref, do_ref, st_ref, dhq_ref, dhf_ref, dhi_ref, dlb_ref,
             dstate, dlb_acc):
        lb = _hgrn_lower_bound(lb_ref)
        same, causal, anti = _chunk_masks()
        sel = jnp.concatenate([causal, same], axis=0).astype(BF16)
        sel_t = jnp.concatenate([anti, same], axis=1).astype(BF16)
        @pl.when(pl.program_id(1) == 0)
        def _():
            dstate[...] = jnp.zeros_like(dstate)
            dlb_acc[...] = jnp.zeros_like(dlb_acc)

        n_groups = HGRN_SEQ_BLOCK // rws
        chunks = [slice(c * t, (c + 1) * t) for c in range(HGRN_GROUP)]

        def group(i, dsts_in):
            g = n_groups - 1 - i
            rows = pl.ds(pl.multiple_of(g * rws, rws), rws)
            hq = hq_ref[rows, :]
            q, sq, sg, f, k, lf = _hgrn_gates(hq, hf_ref[rows, :], lb)
            sums = _mm_select(sel, lf)
            cum, last = sums[:rws], sums[rws:]
            e_cum, e_inv, e_end, dec = jnp.exp(cum), jnp.exp(-cum), jnp.exp(last - cum), jnp.exp(last)
            qd, ki, ke = q * e_cum, k * e_inv, k * e_end
            qdb, kib, keb = qd.astype(BF16), ki.astype(BF16), ke.astype(BF16)
            vb = hi_ref[rows, :].astype(BF16)
            gb = do_ref[rows, :].astype(BF16)

            dsts_out, per_head = [], []
            for h in range(HGRN_PAIR):
                qdb_h, kib_h, keb_h = _head_cols(qdb, h), _head_cols(kib, h), _head_cols(keb, h)
                vb_h, gb_h = _head_cols(vb, h), _head_cols(gb, h)
                att = jnp.where(causal, _mm_nt(qdb_h, kib_h), 0.0).astype(BF16)
                datt = jnp.where(causal, _mm_nt(gb_h, vb_h), 0.0).astype(BF16)
                dv = _mm_tn(att, gb_h)
                dqd = _mm(datt, kib_h)
                dki = _mm_tn(datt, qdb_h)

                decs = [_head_cols(dec[c * t:c * t + 1, :], h) for c in range(HGRN_GROUP)]
                dsts = [None] * HGRN_GROUP
                dst = dsts_in[h]
                for c in reversed(range(HGRN_GROUP)):
                    dsts[c] = dst
                    dst = dst * decs[c] + _mm_tn(gb_h[chunks[c]], qdb_h[chunks[c]])
                dsts_out.append(dst)

                dv_x, dqd_x, dke, dlast_x = [], [], [], []
                for c, sl in enumerate(chunks):
                    st_prev = st_ref[h, g * HGRN_GROUP + c]
                    dstb = dsts[c].astype(BF16)
                    dv_x.append(_mm_nt(keb_h[sl], dstb))
                    dqd_x.append(_mm(gb_h[sl], st_prev.astype(BF16)))
                    dke.append(_mm(vb_h[sl], dstb))
                    ddec = jnp.sum(dsts[c] * st_prev, axis=0, keepdims=True)
                    dlast_x.append(jnp.broadcast_to(ddec * decs[c], (t, HGRN_DIM)))
                per_head.append((dv + jnp.concatenate(dv_x, axis=0), dqd + jnp.concatenate(dqd_x, axis=0),
                                 dki, jnp.concatenate(dke, axis=0), jnp.concatenate(dlast_x, axis=0)))
            dv, dqd, dki, dke, dlast = (jnp.concatenate(list(parts), axis=1) for parts in zip(*per_head))

            dq = dqd * e_cum
            dk = dki * e_inv + dke * e_end
            dke_ke = dke * ke
            dcum = dqd * qd - dki * ki - dke_ke
            dlf = _mm_select(sel_t, jnp.concatenate([dcum, dke_ke], axis=0)) + dlast
            df = dlf / f - dk
            dhq_ref[rows, :] = (dq * (sq * (1.0 + hq * (1.0 - sq)))).astype(BF16)
            dhf_ref[rows, :] = (df * (1.0 - lb) * (sg * (1.0 - sg))).astype(BF16)
            dhi_ref[rows, :] = dv.astype(BF16)
            dlb_acc[...] += jnp.sum(df * (1.0 - sg), axis=0, keepdims=True)
            return tuple(dsts_out)

        dsts = lax.fori_loop(0, n_groups, group, tuple(dstate[h] for h in range(HGRN_PAIR)))
        for h in range(HGRN_PAIR):
            dstate[h] = dsts[h]
        g0 = dlb_acc[...] * lb * (1.0 - lb)
        dlb_ref[...] = jnp.concatenate([g0, -g0], axis=0)

    cols, pair, lb_spec, st_spec = _hgrn_specs(reverse=True)
    wide = jax.ShapeDtypeStruct((SEQ, HGRN_WIDTH), BF16)
    return pl.pallas_call(
        body, name="hgrn_bwd", grid=(HGRN_HEADS // HGRN_PAIR, SEQ // HGRN_SEQ_BLOCK),
        out_shape=(wide, wide, wide, jax.ShapeDtypeStruct((2, HGRN_WIDTH), F32)),
        in_specs=[cols(4), cols(5), cols(6), lb_spec, pair, st_spec],
        out_specs=(pair, pair, pair, lb_spec),
        scratch_shapes=[pltpu.VMEM((HGRN_PAIR, HGRN_DIM, HGRN_DIM), F32),
                        pltpu.VMEM((1, HGRN_PAIR * HGRN_DIM), F32)],
        compiler_params=_params(("parallel", "arbitrary")),
    )(proj, proj, proj, lb_raw, d_rec, states)


def _group_sum(v, group):
    parts = []
    for s in range(v.shape[1] // LANES):
        slab = v[:, LANES * s:LANES * (s + 1)]
        if group == LANES:
            parts.append(jnp.broadcast_to(jnp.sum(slab, axis=-1, keepdims=True), slab.shape))
        else:
            h0 = lax.broadcasted_iota(jnp.int32, slab.shape, 1) < HEAD_DIM
            s0 = jnp.sum(jnp.where(h0, slab, 0.0), axis=-1, keepdims=True)
            s1 = jnp.sum(jnp.where(h0, 0.0, slab), axis=-1, keepdims=True)
            parts.append(jnp.where(h0, s0, s1))
    return jnp.concatenate(parts, axis=1)


def _mid(attn_o, rec, proj, x, target, w_out_g, attn_w, hgrn_w, final_w):
    tm = 256

    def branch_fwd(o, gate, w, group):
        r = lax.rsqrt(_group_sum(o * o, group) * (1.0 / group) + NORM_EPS)
        nrm = o * r
        sg = _sigmoid(gate)
        return r, nrm, sg, nrm * w * (gate * sg)

    def branch_bwd(dy, r, nrm, sg, gate, w, group):
        silu = gate * sg
        d_gate = dy * nrm * w * (sg * (1.0 + gate * (1.0 - sg)))
        d_w = jnp.sum(dy * nrm * silu, axis=0, keepdims=True)
        dn = dy * w * silu
        d_o = r * (dn - nrm * (_group_sum(dn * nrm, group) * (1.0 / group)))
        return d_o, d_gate, d_w

    def body(o_ref, rec_ref, ag_ref, hg_ref, x_ref, tgt_ref, wout_ref, aw_ref, hw_ref, fw_ref,
             dx2_ref, do_ref, delta_ref, dag_ref, drec_ref, dhg_ref, dwout_ref, dfw_ref, daw_ref, dhw_ref,
             loss_ref, dwout_acc):
        i = pl.program_id(0)

        @pl.when(i == 0)
        def _():
            dwout_acc[...] = jnp.zeros_like(dwout_acc)
            dfw_ref[...] = jnp.zeros_like(dfw_ref)
            daw_ref[...] = jnp.zeros_like(daw_ref)
            dhw_ref[...] = jnp.zeros_like(dhw_ref)
            loss_ref[...] = jnp.zeros_like(loss_ref)

        o, rc, ag, hg = o_ref[...], rec_ref[...], ag_ref[...], hg_ref[...]
        aw, hw, fw = aw_ref[...], hw_ref[...], fw_ref[...]
        ra, na, sga, ya = branch_fwd(o, ag, aw, HEAD_DIM)
        rh, nh, sgh, yh = branch_fwd(rc, hg, hw, HGRN_DIM)
        mixed = jnp.concatenate([ya, yh], axis=1).astype(BF16)
        wout = wout_ref[...]
        x2 = x_ref[...] + _mm(mixed, wout)
        rstd = lax.rsqrt(jnp.mean(x2 * x2, axis=-1, keepdims=True) + NORM_EPS)
        xn = x2 * rstd
        err = xn * fw - tgt_ref[...]
        row_loss = jnp.mean(err * err, axis=-1, keepdims=True)
        loss_ref[...] += 0.5 * jnp.sum(row_loss, axis=0, keepdims=True)
        dy = err * (1.0 / D_MODEL)
        dfw_ref[...] += jnp.sum(dy * xn, axis=0, keepdims=True)
        dxn = dy * fw
        dx2 = rstd * (dxn - xn * jnp.mean(dxn * xn, axis=-1, keepdims=True))
        dx2_ref[...] = dx2
        dx2b = dx2.astype(BF16)
        dwout_acc[...] += _mm_tn(mixed, dx2b)

        @pl.when(i == pl.num_programs(0) - 1)
        def _():
            dwout_ref[...] = dwout_acc[...].astype(BF16)

        dmixed = _mm_nt(dx2b, wout)

        d_o, d_ag, d_aw = branch_bwd(dmixed[:, :ATTN_WIDTH], ra, na, sga, ag, aw, HEAD_DIM)
        d_rec, d_hg, d_hw = branch_bwd(dmixed[:, ATTN_WIDTH:], rh, nh, sgh, hg, hw, HGRN_DIM)
        do_ref[...] = d_o
        delta_ref[...] = _group_sum(d_o * o, HEAD_DIM)
        dag_ref[...] = d_ag.astype(BF16)
        drec_ref[...] = d_rec
        dhg_ref[...] = d_hg.astype(BF16)
        daw_ref[...] += d_aw
        dhw_ref[...] += d_hw

    half = lambda: pl.BlockSpec((tm, COL_BLOCK), lambda i: (i, 0))
    full = lambda: pl.BlockSpec((tm, D_MODEL), lambda i: (i, 0))
    fixed = lambda r, c: pl.BlockSpec((r, c), lambda i: (0, 0))
    wide = jax.ShapeDtypeStruct((SEQ, COL_BLOCK), F32)
    wide_b = jax.ShapeDtypeStruct((SEQ, COL_BLOCK), BF16)
    return pl.pallas_call(
        body, name="mid", grid=(SEQ // tm,),
        out_shape=(jax.ShapeDtypeStruct((SEQ, D_MODEL), F32), wide, wide, wide_b, wide, wide_b,
                   jax.ShapeDtypeStruct((D_MODEL, D_MODEL), BF16),
                   jax.ShapeDtypeStruct((1, D_MODEL), F32), jax.ShapeDtypeStruct((1, COL_BLOCK), F32),
                   jax.ShapeDtypeStruct((1, COL_BLOCK), F32), jax.ShapeDtypeStruct((1, 1), F32)),
        scratch_shapes=[pltpu.VMEM((D_MODEL, D_MODEL), F32)],
        in_specs=[half(), half(),
                  pl.BlockSpec((None, tm, COL_BLOCK), lambda i: (3, i, 0)),
                  pl.BlockSpec((None, tm, COL_BLOCK), lambda i: (7, i, 0)),
                  full(), full(), fixed(D_MODEL, D_MODEL), fixed(1, COL_BLOCK), fixed(1, COL_BLOCK),
                  fixed(1, D_MODEL)],
        out_specs=(full(), half(), half(), half(), half(), half(), fixed(D_MODEL, D_MODEL),
                   fixed(1, D_MODEL), fixed(1, COL_BLOCK), fixed(1, COL_BLOCK), fixed(1, 1)),
        compiler_params=_params(("arbitrary",)),
    )(attn_o, rec, proj, proj, x, target, w_out_g, attn_w, hgrn_w, final_w)


def _in_proj_bwd_rows(d_groups, w_g, x, dx2, mix_w, rc, rsa, rsb):
    tm = 256

    def body(*refs):
        dg_refs = refs[:N_DEV]
        wg_ref, x_ref, dx2_ref, w_ref, c_ref, sa_ref, sb_ref, gx_ref, dpb_ref, dmw_ref = refs[N_DEV:]

        @pl.when(pl.program_id(0) == 0)
        def _():
            dmw_ref[...] = jnp.zeros_like(dmw_ref)

        parts = []
        for j in range(N_DEV):
            dp = dg_refs[j][...]
            if j < 2:
                dp = _rot_transposed(dp, c_ref[...], sa_ref[...], sb_ref[...])
            parts.append(dp.astype(BF16))
        dpb = jnp.concatenate(parts, axis=1)
        for j in range(N_DEV):
            dpb_ref[j] = parts[j]
        g = _mm_nt(dpb, wg_ref[...])
        xf = x_ref[...]
        rstd = lax.rsqrt(jnp.mean(xf * xf, axis=-1, keepdims=True) + NORM_EPS)
        xn = xf * rstd
        dmw_ref[...] += jnp.sum(g * xn, axis=0, keepdims=True)
        gw = g * w_ref[...]
        gx_ref[...] = dx2_ref[...] + rstd * (gw - xn * jnp.mean(gw * xn, axis=-1, keepdims=True))

    tile = lambda cols: pl.BlockSpec((tm, cols), lambda i: (i, 0))
    fixed = lambda r, c: pl.BlockSpec((r, c), lambda i: (0, 0))
    return pl.pallas_call(
        body, name="in_proj_bwd_rows", grid=(SEQ // tm,),
        out_shape=(jax.ShapeDtypeStruct((SEQ, D_MODEL), F32), jax.ShapeDtypeStruct((N_DEV, SEQ, COL_BLOCK), BF16),
                   jax.ShapeDtypeStruct((1, D_MODEL), F32)),
        in_specs=[tile(COL_BLOCK) for _ in range(N_DEV)] + [
            pl.BlockSpec((D_MODEL, IN_COLS), lambda i: (0, 0), pipeline_mode=pl.Buffered(1)),
            tile(D_MODEL), tile(D_MODEL), fixed(1, D_MODEL), tile(LANES), tile(LANES), tile(LANES)],
        out_specs=(tile(D_MODEL), pl.BlockSpec((N_DEV, tm, COL_BLOCK), lambda i: (0, i, 0)), fixed(1, D_MODEL)),
        compiler_params=_params(("arbitrary",)),
    )(*d_groups, w_g, x, dx2, mix_w, rc, rsa, rsb)


def _weights_exchange(hn_t, dproj_b, dwout_p, small_p):
    n_chips = N_DEV // 2
    rb = 128
    S1_IN, S1_OUT, SMALL, S2_IN, S2_OUT, VIA_IN, VIA_OUT = 0, 4, 8, 15, 17, 19, 21
    rel_of_pair = (3, 1, 2, 0)
    two_hop = n_chips - 1
    half_in, half_out = COL_BLOCK // 2, D_MODEL // 2

    def body(order_ref, hnt_ref, dp_ref, dwout_ref, small_ref, gin_ref, gout_ref, gs_ref,
             part, s1_send, s1_in, s1_out, fwd_in, fwd_out, s2_in, s2_out, via_in, via_out, land_s,
             send_sems, recv_sems):
        t = pl.program_id(0)
        me = _my_place()
        x, y, c = me
        my_chip = 2 * x + y
        sibling = (x, y, 1 - c)

        def remote(slot, src, dst, to):
            return pltpu.make_async_remote_copy(src_ref=src, dst_ref=dst, send_sem=send_sems.at[slot],
                                                recv_sem=recv_sems.at[slot], device_id=to, device_id_type=MESH)

        def s1_in_copy(pair):
            return remote(S1_IN + pair, s1_send.at[pair], s1_in.at[pair], sibling)

        def s1_out_copy(pair):
            q = my_chip ^ rel_of_pair[pair]
            return remote(S1_OUT + pair, dwout_ref.at[q, 1 - c], s1_out.at[pair], sibling)

        def s2_copies(rel):
            peer = _peer(me, 2 * rel)
            return [remote(S2_IN + rel - 1, fwd_in.at[rel - 1], s2_in.at[rel - 1], peer),
                    remote(S2_OUT + rel - 1, fwd_out.at[rel - 1], s2_out.at[rel - 1], peer)]

        def via_copies(k):
            peer = _peer(me, 2 * (2 - k))
            return [remote(VIA_IN + k, fwd_in.at[two_hop - 1, :, pl.ds(k * half_in, half_in)], via_in.at[k], peer),
                    remote(VIA_OUT + k, fwd_out.at[two_hop - 1, :, pl.ds(k * half_out, half_out)], via_out.at[k],
                           peer)]

        def small_copy(rel):
            return remote(SMALL + rel - 1, small_ref, land_s.at[rel], _peer(me, rel))

        @pl.when(t == 0)
        def _():
            land_s[0] = small_ref[...]
            for pair in range(n_chips):
                s1_out_copy(pair).start()
            for rel in range(1, N_DEV):
                small_copy(rel).start()

        part[...] = _mm(hnt_ref[...], dp_ref[...])

        def rows_loop(n_rows, fn):
            def step(b, carry):
                fn(pl.ds(pl.multiple_of(b * rb, rb), rb))
                return carry
            lax.fori_loop(0, n_rows // rb, step, 0)

        for pair, rel in enumerate(rel_of_pair):
            @pl.when(t == 2 * pair)
            def _(pair=pair):
                s1_send[pair] = part[...].astype(BF16)
                s1_in_copy(pair).start()

            @pl.when(t == 2 * pair + 1)
            def _(pair=pair, rel=rel):
                q = my_chip ^ rel
                s1_in_copy(pair).wait_recv()
                s1_out_copy(pair).wait_recv()
                dst_in = fwd_in.at[rel - 1] if rel else gin_ref
                dst_out = fwd_out.at[rel - 1] if rel else gout_ref
                passes_on = rel in (1, 2)
                if passes_on:
                    for cp in via_copies(rel - 1):
                        cp.wait_recv()

                def with_half(val, via, rows, width):
                    if not passes_on:
                        return val
                    extra = via[rel - 1, rows, :].astype(F32)
                    halves = [val[:, :width], val[:, width:]]
                    halves[rel - 1] = halves[rel - 1] + extra
                    return jnp.concatenate(halves, axis=1)

                def add_in(rows):
                    val = part[rows, :] + s1_in[pair, rows, :].astype(F32)
                    dst_in[rows, :] = with_half(val, via_in, rows, half_in).astype(dst_in.dtype)

                def add_out(rows):
                    val = dwout_ref[q, c, rows, :].astype(F32) + s1_out[pair, rows, :].astype(F32)
                    dst_out[rows, :] = with_half(val, via_out, rows, half_out).astype(dst_out.dtype)

                rows_loop(D_MODEL, add_in)
                rows_loop(WOUT_ROWS, add_out)
                if rel == two_hop:
                    for k in range(2):
                        for cp in via_copies(k):
                            cp.start()
                elif rel:
                    for cp in s2_copies(rel):
                        cp.start()

        @pl.when(t == N_DEV - 1)
        def _():
            for rel in range(1, two_hop):
                for cp in s2_copies(rel):
                    cp.wait_recv()

            def total_in(rows):
                g = gin_ref[rows, :]
                for rel in range(1, two_hop):
                    g = g + s2_in[rel - 1, rows, :].astype(F32)
                gin_ref[rows, :] = g

            def total_out(rows):
                g = gout_ref[rows, :]
                for rel in range(1, two_hop):
                    g = g + s2_out[rel - 1, rows, :].astype(F32)
                gout_ref[rows, :] = g

            rows_loop(D_MODEL, total_in)
            rows_loop(WOUT_ROWS, total_out)

            for rel in range(1, N_DEV):
                small_copy(rel).wait_recv()
            my_flat = _flat(me)
            g = land_s[my_flat ^ 0]
            for dev in range(1, N_DEV):
                g = g + land_s[my_flat ^ dev]
            gs_ref[...] = g

            for pair in range(n_chips):
                s1_in_copy(pair).wait_send()
                s1_out_copy(pair).wait_send()
            for rel in range(1, two_hop):
                for cp in s2_copies(rel) + via_copies(rel - 1):
                    cp.wait_send()
            for rel in range(1, N_DEV):
                small_copy(rel).wait_send()

    place_x, place_y, place_c = _my_place()
    my_chip = 2 * place_x + place_y
    order = jnp.stack([2 * (my_chip ^ rel) + core for rel in rel_of_pair
                       for core in (1 - place_c, place_c)]).astype(jnp.int32)

    whole = lambda: pl.BlockSpec(memory_space=pltpu.VMEM)
    in_blocks = lambda n: pltpu.VMEM((n, D_MODEL, COL_BLOCK), BF16)
    out_blocks = lambda n: pltpu.VMEM((n, WOUT_ROWS, D_MODEL), BF16)
    grid_spec = pltpu.PrefetchScalarGridSpec(
        num_scalar_prefetch=1, grid=(N_DEV,),
        in_specs=[pl.BlockSpec((D_MODEL, SEQ), lambda t, order: (0, 0), pipeline_mode=pl.Buffered(1)),
                  pl.BlockSpec((None, SEQ, COL_BLOCK), lambda t, order: (order[t], 0, 0)), whole(), whole()],
        out_specs=(whole(), whole(), whole()),
        scratch_shapes=[pltpu.VMEM((D_MODEL, COL_BLOCK), F32), in_blocks(n_chips), in_blocks(n_chips),
                        out_blocks(n_chips), in_blocks(n_chips - 1), out_blocks(n_chips - 1),
                        in_blocks(n_chips - 2), out_blocks(n_chips - 2),
                        pltpu.VMEM((2, D_MODEL, half_in), BF16), pltpu.VMEM((2, WOUT_ROWS, half_out), BF16),
                        pltpu.VMEM((N_DEV, SMALL_ROWS, LANES), F32),
                        pltpu.SemaphoreType.DMA((23,)), pltpu.SemaphoreType.DMA((23,))])
    return pl.pallas_call(
        body, name="weights_exchange", grid_spec=grid_spec,
        out_shape=(jax.ShapeDtypeStruct((D_MODEL, COL_BLOCK), F32), jax.ShapeDtypeStruct((WOUT_ROWS, D_MODEL), F32),
                   jax.ShapeDtypeStruct((SMALL_ROWS, LANES), F32)),
        compiler_params=_params(("arbitrary",)),
    )(order, hn_t, dproj_b, dwout_p.reshape(n_chips, 2, WOUT_ROWS, D_MODEL), small_p)


def _adamw(w, g, m, v):
    m = ADAM_B1 * m + (1.0 - ADAM_B1) * g
    v = ADAM_B2 * v + (1.0 - ADAM_B2) * (g * g)
    m_hat = m / (1.0 - ADAM_B1 ** ADAM_STEP)
    v_hat = v / (1.0 - ADAM_B2 ** ADAM_STEP)
    delta = -ADAM_LR * (m_hat / (jnp.sqrt(v_hat) + ADAM_EPS) + ADAM_WD * w)
    return delta, m, v


def _adamw_update(grads, weights, m_old, v_old):
    rb = 256

    def body(*refs):
        g_refs, w_refs, m_refs, v_refs = refs[0:3], refs[3:6], refs[6:9], refs[9:12]
        d_refs, nm_refs, nv_refs = refs[12:15], refs[15:18], refs[18:21]
        for k in range(3):
            n_rows = g_refs[k].shape[0]
            step_rows = min(rb, n_rows)

            def step(b, carry, k=k, step_rows=step_rows):
                rows = pl.ds(pl.multiple_of(b * step_rows, 8), step_rows)
                delta, nm, nv = _adamw(w_refs[k][rows, :], g_refs[k][rows, :], m_refs[k][rows, :], v_refs[k][rows, :])
                d_refs[k][rows, :] = delta
                nm_refs[k][rows, :] = nm
                nv_refs[k][rows, :] = nv
                return carry

            lax.fori_loop(0, n_rows // step_rows, step, 0)

    shapes = tuple(jax.ShapeDtypeStruct(g.shape, F32) for g in grads)
    vm = lambda: pl.BlockSpec(memory_space=pltpu.VMEM)
    outs = pl.pallas_call(
        body, name="adamw_update", out_shape=shapes * 3,
        in_specs=[vm() for _ in range(12)], out_specs=tuple(vm() for _ in range(9)),
        compiler_params=_params(),
    )(*grads, *weights, *m_old, *v_old)
    return outs[0:3], outs[3:6], outs[6:9]


def _pack_small(mix, attn, hgrn, lb, final, loss=None):
    def rows8(a):
        a = a.reshape(-1, LANES)
        return jnp.pad(a, ((0, 8 - a.shape[0]), (0, 0)))
    last = jnp.zeros((8, LANES), F32) if loss is None else jnp.pad(loss.reshape(1, 1), ((0, 7), (0, LANES - 1)))
    return jnp.concatenate([rows8(mix), rows8(attn), rows8(hgrn), rows8(lb), rows8(final), last], axis=0)


def _unpack_small(slab):
    return (slab[ROW_MIX:ROW_MIX + 8].reshape(1, D_MODEL), slab[ROW_ATTN:ROW_ATTN + 4].reshape(1, ATTN_WIDTH),
            slab[ROW_HGRN:ROW_HGRN + 4].reshape(1, HGRN_WIDTH), slab[ROW_LB:ROW_LB + 8].reshape(2, HGRN_WIDTH),
            slab[ROW_FINAL:ROW_FINAL + 8].reshape(D_MODEL))


def _rope(pos_row):
    j = np.arange(ROPE_ROWS)
    inv = np.where(j < ROPE_HALF, ROPE_THETA ** (-(j % ROPE_HALF) * (2.0 / ROPE_DIMS)), 0.0)
    e = np.arange(LANES) % HEAD_DIM
    hit = (j[:, None] == (e % ROPE_HALF)[None, :]) & (j[:, None] < ROPE_HALF)
    sel = np.stack([hit & (e < ROPE_DIMS), hit & (e >= ROPE_HALF) & (e < ROPE_DIMS),
                    -1.0 * (hit & (e < ROPE_HALF))]).astype(np.float32)
    return _rope_tables(pos_row, jnp.asarray(inv.astype(np.float32).reshape(ROPE_ROWS, 1)),
                        jnp.asarray(sel, dtype=BF16))


def _local_step(x, proj, qkv_sorted, w_in_g, w_out_g, tables, mix_w, attn_w, hgrn_w, lb_raw, final_w, target):
    rc, rsa, rsb = tables
    attn_o, lse = _attn_fwd_fused(qkv_sorted)
    rec, states = _hgrn_fwd(proj, lb_raw)

    (dx2, d_o, delta, d_ag, d_rec, d_hg, dwout_p, d_final, d_attn_w, d_hgrn_w, loss) = _mid(
        attn_o, rec, proj, x, target, w_out_g, attn_w, hgrn_w, final_w.reshape(1, D_MODEL))

    dqkv = _attn_bwd_fused(qkv_sorted, d_o, lse, delta)
    d_hq, d_hf, d_hi, d_lb = _hgrn_bwd(proj, lb_raw, d_rec, states)

    grad_x, dproj_b, d_mix = _in_proj_bwd_rows(
        (dqkv[0], dqkv[1], dqkv[2], d_ag, d_hq, d_hf, d_hi, d_hg), w_in_g, x, dx2, mix_w, rc, rsa, rsb)
    small_p = _pack_small(d_mix, d_attn_w, d_hgrn_w, d_lb, d_final, loss)
    return grad_x, dproj_b, dwout_p, small_p


def kernel(x, positions, w_in, w_out, mix_norm_w, attn_out_norm_w, hgrn_out_norm_w, hgrn_lb_raw, final_norm_w, loss_target, m_w_in, m_w_out, m_mix_norm_w, m_attn_out_norm_w, m_hgrn_out_norm_w, m_hgrn_lb_raw, m_final_norm_w, v_w_in, v_w_out, v_mix_norm_w, v_attn_out_norm_w, v_hgrn_out_norm_w, v_hgrn_lb_raw, v_final_norm_w):
    tables = _rope(positions)
    proj, hn_t, w_in_g, w_out_g, qkv_sorted = _gather_project(x[0], mix_norm_w, w_in[0], w_out[0], *tables)
    grad_x, dproj_b, dwout_p, small_p = _local_step(
        x[0], proj, qkv_sorted, w_in_g, w_out_g, tables, mix_norm_w, attn_out_norm_w, hgrn_out_norm_w,
        hgrn_lb_raw, final_norm_w, loss_target[0])
    g_in, g_out, g_s = _weights_exchange(hn_t, dproj_b, dwout_p, small_p)

    w_s = _pack_small(mix_norm_w, attn_out_norm_w, hgrn_out_norm_w, hgrn_lb_raw, final_norm_w)
    m_s = _pack_small(m_mix_norm_w, m_attn_out_norm_w, m_hgrn_out_norm_w, m_hgrn_lb_raw, m_final_norm_w)
    v_s = _pack_small(v_mix_norm_w, v_attn_out_norm_w, v_hgrn_out_norm_w, v_hgrn_lb_raw, v_final_norm_w)
    (d_in, d_out, d_s), (nm_in, nm_out, nm_s), (nv_in, nv_out, nv_s) = _adamw_update(
        (g_in, g_out, g_s), (w_in[0], w_out[0], w_s), (m_w_in[0], m_w_out[0], m_s), (v_w_in[0], v_w_out[0], v_s))

    loss = g_s[ROW_LOSS, 0]
    return (loss, grad_x[None], g_in[None], g_out[None], *_unpack_small(g_s),
            d_in[None], d_out[None], *_unpack_small(d_s),
            nm_in[None], nm_out[None], *_unpack_small(nm_s),
            nv_in[None], nv_out[None], *_unpack_small(nv_s))
```

```python
import functools

import jax
import jax.numpy as jnp
import numpy as np
from jax import lax
from jax.experimental import pallas as pl
from jax.experimental.pallas import tpu as pltpu

F32 = jnp.float32
BF16 = jnp.bfloat16

SEQ = 4096
D_MODEL = 1024
ATTN_WIDTH = 512
HGRN_WIDTH = 512
HEAD_DIM = 64
HGRN_HEADS = 4
HGRN_DIM = 128
HGRN_CHUNK = 64
N_CHUNKS = SEQ // HGRN_CHUNK
IN_COLS = 4096
COL_BLOCK = 512
N_DEV = 8
WOUT_ROWS = D_MODEL // N_DEV
ATTN_BLOCK = 128
DILATIONS = (1, 4, 16)
ROPE_THETA = 500000.0
ROPE_DIMS = 16
ROPE_HALF = 8
NORM_EPS = 1e-6
NEG_BIG = -1e30
LANES = 128

ADAM_LR = 0.001
ADAM_B1 = 0.9
ADAM_B2 = 0.999
ADAM_EPS = 1e-08
ADAM_WD = 0.01
ADAM_STEP = 10

SMALL_ROWS = 48
ROW_MIX, ROW_ATTN, ROW_HGRN, ROW_LB, ROW_FINAL, ROW_LOSS = 0, 8, 16, 24, 32, 40

VMEM_LIMIT = 56 * 1024 * 1024
MESH = pl.DeviceIdType.MESH


def _mm(a, b):
    return lax.dot_general(a, b, (((1,), (0,)), ((), ())), preferred_element_type=F32)


def _mm_nt(a, b):
    return lax.dot_general(a, b, (((1,), (1,)), ((), ())), preferred_element_type=F32)


def _mm_tn(a, b):
    return lax.dot_general(a, b, (((0,), (0,)), ((), ())), preferred_element_type=F32)


def _mm_exact(a, b):
    return lax.dot_general(a, b, (((1,), (0,)), ((), ())), preferred_element_type=F32,
                           precision=lax.Precision.HIGHEST)


def _sigmoid(v):
    return 1.0 / (1.0 + jnp.exp(-v))


def _params(sem=None, **kw):
    return pltpu.CompilerParams(dimension_semantics=sem, vmem_limit_bytes=VMEM_LIMIT, **kw)


def _my_place():
    return lax.axis_index("x"), lax.axis_index("y"), lax.axis_index("c")


def _peer(place, rel):
    x, y, c = place
    return (x ^ ((rel >> 2) & 1), y ^ ((rel >> 1) & 1), c ^ (rel & 1))


def _flat(place):
    x, y, c = place
    return 4 * x + 2 * y + c


ROPE_ROWS = 16


def _rope_tables(pos_row, inv_freq_col, selectors):
    def body(pos_ref, invf_ref, sel_ref, c_ref, sa_ref, sb_ref):
        ang = pos_ref[...].astype(F32) * invf_ref[...]
        cos, sin = jnp.cos(ang), jnp.sin(ang)

        def spread(v, sel):
            hi = v.astype(BF16)
            r1 = v - hi.astype(F32)
            mid = r1.astype(BF16)
            lo = (r1 - mid.astype(F32)).astype(BF16)
            return _mm_tn(hi, sel) + _mm_tn(mid, sel) + _mm_tn(lo, sel)

        e = lax.broadcasted_iota(jnp.int32, (1, LANES), 1) & (HEAD_DIM - 1)
        c_ref[...] = spread(cos, sel_ref[0]) + jnp.where(e < ROPE_DIMS, 0.0, 1.0)
        sa_ref[...] = spread(sin, sel_ref[1])
        sb_ref[...] = spread(sin, sel_ref[2])

    tab = jax.ShapeDtypeStruct((SEQ, LANES), F32)
    vm = lambda: pl.BlockSpec(memory_space=pltpu.VMEM)
    return pl.pallas_call(
        body, name="rope_tables", out_shape=(tab, tab, tab),
        in_specs=[vm(), vm(), vm()], out_specs=(vm(), vm(), vm()), compiler_params=_params(),
    )(pos_row, inv_freq_col, selectors)


def _per_slab(fn, t):
    return jnp.concatenate([fn(t[:, LANES * s:LANES * (s + 1)]) for s in range(t.shape[1] // LANES)], axis=1)


def _rot(t, c, sa, sb):
    return _per_slab(lambda u: u * c + pltpu.roll(u, ROPE_HALF, 1) * sa + pltpu.roll(u, LANES - ROPE_HALF, 1) * sb, t)


def _rot_transposed(g, c, sa, sb):
    return _per_slab(
        lambda u: u * c + pltpu.roll(u * sa, LANES - ROPE_HALF, 1) + pltpu.roll(u * sb, ROPE_HALF, 1), g)


def _gather_project(x, mix_w, w_in, w_out, rc, rsa, rsb):
    tm = 1024
    n_tiles = SEQ // tm
    arrival_of_step = (None, 0, 1, 2, 4, 5, 3, 6)

    def body(order_ref, x_ref, w_ref, win_ref, wout_ref, c_ref, sa_ref, sb_ref,
             proj_ref, hnt_ref, gin_hbm, gout_hbm, qkv_hbm,
             hn_s, w_land, wout_land, stage, sort_stage, send_sems, recv_sems, local_sems, sort_sems):
        g, i = pl.program_id(0), pl.program_id(1)
        me = _my_place()
        x_, y_, c_ = me
        sibling = (x_, y_, 1 - c_)
        chips = [(1 - x_, y_), (x_, 1 - y_), (1 - x_, 1 - y_)]

        def slab(which, place):
            idx = _flat(place)
            if which == 0:
                return w_land.at[idx]
            return wout_land.at[pl.ds(pl.multiple_of(idx * WOUT_ROWS, WOUT_ROWS), WOUT_ROWS), :]

        def remote(which, k, ref, to, src=None):
            return pltpu.make_async_remote_copy(
                src_ref=ref if src is None else src, dst_ref=ref, send_sem=send_sems.at[8 * which + k],
                recv_sem=recv_sems.at[8 * which + k], device_id=to, device_id_type=MESH)

        def copy(which, k, block, to, src=None):
            return remote(which, k, slab(which, block), to, src)

        def half(which, place, part):
            n = (D_MODEL if which == 0 else WOUT_ROWS) // 2
            if which == 0:
                return w_land.at[_flat(place), pl.ds(n * part, n), :]
            return wout_land.at[pl.ds(pl.multiple_of(_flat(place) * WOUT_ROWS + n * part, n), n), :]

        def first_copies(which):
            src = stage if which == 0 else None
            return ([copy(which, 0, me, sibling, src)]
                    + [copy(which, 1 + j, me, (*chips[j], c_), src) for j in range(2)])

        def relay(which, part):
            frm, to = (chips[1], chips[0]) if part == 0 else (chips[0], chips[1])
            return remote(which, 3 if part == 0 else 7, half(which, (*frm, c_), part), (*to, c_))

        def two_hop_half(which, part):
            return remote(which, 3 if part == 0 else 7, half(which, (*chips[2], c_), part), me)

        def pass_on(which, j):
            return copy(which, 4 + j, (*chips[j], c_), sibling)

        def arrival(which, k):
            if k == 0:
                return copy(which, 0, sibling, me)
            if k <= 2:
                return copy(which, k, (*chips[k - 1], c_), me)
            return copy(which, k, (*chips[k - 4], 1 - c_), me)

        def to_hbm(step):
            idx = order_ref[step]
            cols = pl.ds(pl.multiple_of(idx * COL_BLOCK, COL_BLOCK), COL_BLOCK)
            return pltpu.make_async_copy(w_land.at[idx], gin_hbm.at[:, cols], local_sems.at[step])

        @pl.when((g == 0) & (i == 0))
        def _():
            stage[...] = win_ref[...].astype(BF16)
            w_land[_flat(me)] = stage[...]
            wout_land[pl.ds(pl.multiple_of(_flat(me) * WOUT_ROWS, WOUT_ROWS), WOUT_ROWS), :] = (
                wout_ref[...].astype(BF16))
            for cp in first_copies(0) + first_copies(1)[:1]:
                cp.start()
            to_hbm(0).start()

        for step, k in enumerate(arrival_of_step):
            if k is None:
                continue

            @pl.when((g == step) & (i == 0))
            def _(k=k, step=step):
                if k == 3:
                    two_hop_half(0, 0).wait_recv()
                    two_hop_half(0, 1).wait_recv()
                else:
                    arrival(0, k).wait_recv()
                to_hbm(step).start()
                if 1 <= k <= 3:
                    pass_on(0, k - 1).start()
                if k == 1:
                    relay(0, 1).start()
                    for cp in first_copies(1)[1:]:
                        cp.start()
                if k == 2:
                    relay(0, 0).start()
                if k in (4, 5):
                    arrival(1, k - 3).wait_recv()
                    relay(1, 5 - k).start()

        rows = pl.ds(pl.multiple_of(i * tm, tm), tm)

        @pl.when(g == 0)
        def _():
            xf = x_ref[...]
            ms = jnp.mean(xf * xf, axis=-1, keepdims=True)
            hn = xf * lax.rsqrt(ms + NORM_EPS) * w_ref[...]
            hnt_ref[...] = hn.T.astype(BF16)
            hn_s[rows, :] = hn.astype(BF16)

        group = order_ref[g]

        def sorted_copy(tile_value):
            per = tm // SORT_RESIDUES
            cols = pl.ds(pl.multiple_of(group * COL_BLOCK, COL_BLOCK), COL_BLOCK)
            buf = i % 2

            def out_copies(tile, b):
                return [pltpu.make_async_copy(
                    sort_stage.at[b, :, r, :], qkv_hbm.at[r, pl.ds(pl.multiple_of(tile * per, per), per), cols],
                    sort_sems.at[b, r]) for r in range(SORT_RESIDUES)]

            @pl.when(i >= 2)
            def _():
                for copy in out_copies(i - 2, buf):
                    copy.wait()

            sort_stage[buf] = tile_value.reshape(per, SORT_RESIDUES, COL_BLOCK)
            for copy in out_copies(i, buf):
                copy.start()

            @pl.when(i == n_tiles - 1)
            def _():
                for copy in out_copies(i - 1, 1 - buf) + out_copies(i, buf):
                    copy.wait()

        @pl.when(group < 2)
        def _():
            rotated = _rot(_mm(hn_s[rows, :], w_land[group]), c_ref[...], sa_ref[...], sb_ref[...])
            proj_ref[...] = rotated
            sorted_copy(rotated)

        @pl.when(group == 2)
        def _():
            value = _mm(hn_s[rows, :], w_land[group])
            proj_ref[...] = value
            sorted_copy(value)

        @pl.when(group > 2)
        def _():
            proj_ref[...] = _mm(hn_s[rows, :], w_land[group])

        @pl.when((g == N_DEV - 1) & (i == n_tiles - 1))
        def _():
            pass_on(1, 0).start()
            pass_on(1, 1).start()
            two_hop_half(1, 0).wait_recv()
            two_hop_half(1, 1).wait_recv()
            pass_on(1, 2).start()
            for k in (0, 4, 5, 6):
                arrival(1, k).wait_recv()
            for which in (0, 1):
                for cp in (first_copies(which) + [relay(which, part) for part in range(2)]
                           + [pass_on(which, j) for j in range(3)]):
                    cp.wait_send()
            wout_copy = pltpu.make_async_copy(wout_land, gout_hbm, local_sems.at[N_DEV])
            wout_copy.start()
            for step in range(N_DEV):
                to_hbm(step).wait()
            wout_copy.wait()

    me = _my_place()
    x_, y_, c_ = me
    chips = [(1 - x_, y_), (x_, 1 - y_), (1 - x_, 1 - y_)]
    order = jnp.stack([_flat(p) for p in (
        me, (x_, y_, 1 - c_), (*chips[0], c_), (*chips[1], c_), (*chips[0], 1 - c_), (*chips[1], 1 - c_),
        (*chips[2], c_), (*chips[2], 1 - c_))]).astype(jnp.int32)

    first_sweep = lambda g, i, order: (jnp.where(g == 0, i, n_tiles - 1), 0)
    tab = pl.BlockSpec((tm, LANES), lambda g, i, order: (jnp.where(order[g] < 2, i, 0), 0))
    whole = lambda: pl.BlockSpec(memory_space=pltpu.VMEM)
    grid_spec = pltpu.PrefetchScalarGridSpec(
        num_scalar_prefetch=1, grid=(N_DEV, n_tiles),
        in_specs=[pl.BlockSpec((tm, D_MODEL), first_sweep),
                  pl.BlockSpec((1, D_MODEL), lambda g, i, order: (0, 0)),
                  whole(), whole(), tab, tab, tab],
        out_specs=(pl.BlockSpec((None, tm, COL_BLOCK), lambda g, i, order: (order[g], i, 0)),
                   pl.BlockSpec((D_MODEL, tm), lambda g, i, order: (0, jnp.where(g == 0, i, n_tiles - 1))),
                   pl.BlockSpec(memory_space=pl.ANY), pl.BlockSpec(memory_space=pl.ANY),
                   pl.BlockSpec(memory_space=pl.ANY)),
        scratch_shapes=[pltpu.VMEM((SEQ, D_MODEL), BF16),
                        pltpu.VMEM((N_DEV, D_MODEL, COL_BLOCK), BF16),
                        pltpu.VMEM((D_MODEL, D_MODEL), BF16),
                        pltpu.VMEM((D_MODEL, COL_BLOCK), BF16),
                        pltpu.VMEM((2, tm // SORT_RESIDUES, SORT_RESIDUES, COL_BLOCK), F32),
                        pltpu.SemaphoreType.DMA((16,)), pltpu.SemaphoreType.DMA((16,)),
                        pltpu.SemaphoreType.DMA((N_DEV + 1,)), pltpu.SemaphoreType.DMA((2, SORT_RESIDUES))])
    proj, hn_t, w_in_g, w_out_g, qkv_sorted = pl.pallas_call(
        body, name="gather_project", grid_spec=grid_spec,
        out_shape=(jax.ShapeDtypeStruct((N_DEV, SEQ, COL_BLOCK), F32), jax.ShapeDtypeStruct((D_MODEL, SEQ), BF16),
                   jax.ShapeDtypeStruct((D_MODEL, IN_COLS), BF16), jax.ShapeDtypeStruct((D_MODEL, D_MODEL), BF16),
                   jax.ShapeDtypeStruct((SORT_RESIDUES, SORT_ROWS, 3 * COL_BLOCK), F32)),
        compiler_params=_params(("arbitrary", "arbitrary")),
    )(order, x, mix_w, w_in, w_out, rc, rsa, rsb)
    return proj, hn_t, w_in_g, w_out_g, qkv_sorted.reshape(SEQ, 3 * COL_BLOCK)


SCORE_SCALE = HEAD_DIM ** -0.5
ATTN_GROUP_FWD = 32
ATTN_GROUP_BWD = 32
BLOCKS_PER_PATTERN = SEQ // ATTN_BLOCK
SORT_RESIDUES = 16
SORT_ROWS = SEQ // SORT_RESIDUES


def _write_band_bias(bias_ref):
    row = lax.broadcasted_iota(jnp.int32, (2 * ATTN_BLOCK, 2 * ATTN_BLOCK), 0) & (ATTN_BLOCK - 1)
    col = lax.broadcasted_iota(jnp.int32, (2 * ATTN_BLOCK, 2 * ATTN_BLOCK), 1)
    for pi, d in enumerate(DILATIONS):
        per = SORT_RESIDUES // d
        ahead = per * (row % (8 * d) - col % (16 * d)) + (row // (8 * d) - col // (16 * d))
        dist = ATTN_BLOCK + ahead
        bias_ref[2 * pi] = jnp.where((dist >= 0) & (dist <= ATTN_BLOCK), 0.0, NEG_BIG)
        bias_ref[2 * pi + 1] = jnp.where(ahead >= 0, 0.0, NEG_BIG)


def _head0_lanes():
    return lax.broadcasted_iota(jnp.int32, (ATTN_BLOCK, LANES), 1) < HEAD_DIM


def _stack_heads(t, h0):
    return jnp.concatenate([jnp.where(h0, t, 0.0), jnp.where(h0, 0.0, t)], axis=0).astype(BF16)


def _block_runs(i, d):
    nblk = BLOCKS_PER_PATTERN // d
    r, n = i // nblk, i % nblk
    kn = jnp.maximum(n - 1, 0)
    rows, keys = [], []
    for c in range(SORT_RESIDUES // d):
        base = SORT_ROWS * (c * d + r)
        rows.append(pl.ds(pl.multiple_of(base + 8 * d * n, 8), 8 * d))
        keys.append(pl.ds(pl.multiple_of(base + 8 * d * kn, 8), 16 * d))
    return rows, keys, (n == 0).astype(jnp.int32)


def _take(ref, runs):
    return jnp.concatenate([ref[run, :] for run in runs], axis=0)


def _put(ref, runs, value, add=False):
    at = 0
    for run in runs:
        piece = value[at:at + run.size]
        if add:
            ref[run, :] += piece
        else:
            ref[run, :] = piece
        at += run.size


def _sort_copies(src_hbm, lane_block, dst_ref, sem_ref):
    lanes = pl.ds(pl.multiple_of(LANES * lane_block, LANES), LANES)
    return [pltpu.make_async_copy(src_hbm.at[:, r, lanes], dst_ref.at[pl.ds(SORT_ROWS * r, SORT_ROWS), :],
                                  sem_ref.at[r]) for r in range(SORT_RESIDUES)]


def _unsort_copies(src_ref, dst_hbm, lane_block, sem_ref):
    lanes = pl.ds(pl.multiple_of(LANES * lane_block, LANES), LANES)
    return [pltpu.make_async_copy(src_ref.at[pl.ds(SORT_ROWS * r, SORT_ROWS), :], dst_hbm.at[:, r, lanes],
                                  sem_ref.at[r]) for r in range(SORT_RESIDUES)]


def _for_each_group(d, n_group, load, compute, store):
    def group(g, carry):
        items = [load(*_block_runs(g * n_group + u, d)) for u in range(n_group)]
        results = [compute(item) for item in items]
        for item, res in zip(items, results):
            store(item, res)
        return carry

    lax.fori_loop(0, BLOCKS_PER_PATTERN // n_group, group, 0)


def _attn_fwd_fused(qkv_sorted):
    n_pat = len(DILATIONS)
    tile2 = (2 * ATTN_BLOCK, LANES)

    def body(q_ref, k_ref, v_ref, o_hbm, lse_ref, o_slots, m_acc, l_acc, bias_ref, out_sem):
        step, n_steps = pl.program_id(0), pl.num_programs(0)
        pl.when(step == 0)(lambda: _write_band_bias(bias_ref))
        slot = step % 2
        o_acc = o_slots.at[slot]
        h0 = _head0_lanes()
        for pi, d in enumerate(DILATIONS):
            first, last = pi == 0, pi == n_pat - 1

            def load(rows, keys, which, first=first, pi=pi):
                item = dict(rows=rows, keys=keys, which=2 * pi + which)
                if not first:
                    item.update(o=_take(o_acc, rows), m=[_take(m_acc.at[h], rows) for h in range(2)],
                                l=[_take(l_acc.at[h], rows) for h in range(2)])
                return item

            def compute(item, first=first):
                kb = _take(k_ref, item["keys"]).astype(BF16)
                vb = _take(v_ref, item["keys"]).astype(BF16)
                s = _mm_nt(_stack_heads(_take(q_ref, item["rows"]) * SCORE_SCALE, h0), kb) + bias_ref[item["which"]]
                mb = jnp.max(s, axis=-1, keepdims=True)
                if first:
                    p = jnp.exp(s - mb)
                    mn = jnp.broadcast_to(mb, tile2)
                else:
                    m_old = jnp.concatenate(item["m"], axis=0)
                    mn = jnp.maximum(m_old, mb)
                    alpha = jnp.exp(m_old - mn)
                    p = jnp.exp(s - jnp.concatenate([mn, mn], axis=1))
                ls = jnp.sum(p, axis=-1, keepdims=True)
                pv = _mm(p.astype(BF16), vb)
                if first:
                    return pv, mn, jnp.broadcast_to(ls, tile2)
                o_old = jnp.concatenate([item["o"], item["o"]], axis=0)
                return alpha * o_old + pv, mn, alpha * jnp.concatenate(item["l"], axis=0) + ls

            def store(item, res, last=last):
                rows = item["rows"]
                (o0, o1), (m0, m1), (l0, l1) = ((a[:ATTN_BLOCK], a[ATTN_BLOCK:]) for a in res)
                if last:
                    _put(o_acc, rows, jnp.where(h0, o0 / l0, o1 / l1))
                    _put(lse_ref, rows, jnp.where(h0, m0 + jnp.log(l0), m1 + jnp.log(l1)))
                else:
                    _put(o_acc, rows, jnp.where(h0, o0, o1))
                    for h, (m, l) in enumerate(((m0, l0), (m1, l1))):
                        _put(m_acc.at[h], rows, m)
                        _put(l_acc.at[h], rows, l)

            _for_each_group(d, ATTN_GROUP_FWD, load, compute, store)

        def copies_out(of_step):
            return _unsort_copies(o_slots.at[of_step % 2], o_hbm, of_step, out_sem.at[of_step % 2])

        @pl.when(step > 0)
        def _():
            for copy in copies_out(step - 1):
                copy.wait()

        for copy in copies_out(step):
            copy.start()

        @pl.when(step == n_steps - 1)
        def _():
            for copy in copies_out(step):
                copy.wait()

    slab = lambda g: pl.BlockSpec((SEQ, LANES), functools.partial(lambda hp, g: (0, 4 * g + hp), g=g))
    wide = jax.ShapeDtypeStruct((SEQ, ATTN_WIDTH), F32)
    o_rows, lse = pl.pallas_call(
        body, name="attn_fwd", grid=(4,),
        out_shape=(jax.ShapeDtypeStruct((SORT_ROWS, SORT_RESIDUES, ATTN_WIDTH), F32), wide),
        in_specs=[slab(0), slab(1), slab(2)], out_specs=(pl.BlockSpec(memory_space=pl.ANY), slab(0)),
        scratch_shapes=[pltpu.VMEM((2, SEQ, LANES), F32), pltpu.VMEM((2, SEQ, LANES), F32),
                        pltpu.VMEM((2, SEQ, LANES), F32),
                        pltpu.VMEM((2 * len(DILATIONS), 2 * ATTN_BLOCK, 2 * ATTN_BLOCK), F32),
                        pltpu.SemaphoreType.DMA((2, SORT_RESIDUES))],
        compiler_params=_params(("arbitrary",)),
    )(qkv_sorted, qkv_sorted, qkv_sorted)
    return o_rows.reshape(SEQ, ATTN_WIDTH), lse


def _attn_bwd_fused(qkv_sorted, d_out, lse_sorted, delta):
    def body(q_ref, k_ref, v_ref, do_hbm, lse_ref, del_hbm, dq_hbm, dk_hbm, dv_hbm,
             in_slots, out_slots, bias_ref, in_sem, out_sem):
        step, n_steps = pl.program_id(0), pl.num_programs(0)
        slot = step % 2

        def copies_in(of_step):
            s = of_step % 2
            return [copy for j, hbm in enumerate((do_hbm, del_hbm))
                    for copy in _sort_copies(hbm, of_step, in_slots.at[s, j], in_sem.at[s, j])]

        def copies_out(of_step):
            s = of_step % 2
            return [copy for j, hbm in enumerate((dq_hbm, dk_hbm, dv_hbm))
                    for copy in _unsort_copies(out_slots.at[s, j], hbm, of_step, out_sem.at[s, j])]

        @pl.when(step == 0)
        def _():
            for copy in copies_in(step):
                copy.start()
            _write_band_bias(bias_ref)

        @pl.when(step + 1 < n_steps)
        def _():
            for copy in copies_in(step + 1):
                copy.start()

        do_s, del_s = in_slots.at[slot, 0], in_slots.at[slot, 1]
        dq_s, dk_s, dv_s = (out_slots.at[slot, j] for j in range(3))
        dk_s[...] = jnp.zeros_like(dk_s)
        dv_s[...] = jnp.zeros_like(dv_s)
        for copy in copies_in(step):
            copy.wait()
        h0 = _head0_lanes()
        for pi, d in enumerate(DILATIONS):
            first = pi == 0

            def load(rows, keys, which, pi=pi):
                return dict(rows=rows, keys=keys, q=_take(q_ref, rows), g=_take(do_s, rows),
                            lse=_take(lse_ref, rows), delta=_take(del_s, rows),
                            k=_take(k_ref, keys).astype(BF16), v=_take(v_ref, keys).astype(BF16),
                            bias=bias_ref[2 * pi + which])

            def per_head(t):
                swapped = pltpu.roll(t, HEAD_DIM, 1)
                both = jnp.concatenate([jnp.where(h0, t, swapped), jnp.where(h0, swapped, t)], axis=0)
                return jnp.concatenate([both, both], axis=1)

            def compute(item):
                q2, g2 = _stack_heads(item["q"] * SCORE_SCALE, h0), _stack_heads(item["g"], h0)
                s = _mm_nt(q2, item["k"]) + item["bias"]
                p = jnp.exp(s - per_head(item["lse"]))
                dp = _mm_nt(g2, item["v"])
                ds = (p * (dp - per_head(item["delta"]))).astype(BF16)
                dq2 = _mm(ds, item["k"])
                dq = jnp.where(h0, dq2[:ATTN_BLOCK], dq2[ATTN_BLOCK:]) * SCORE_SCALE
                return dq, _mm_tn(ds, q2), _mm_tn(p.astype(BF16), g2)

            def store(item, res, first=first):
                _put(dq_s, item["rows"], res[0], add=not first)
                _put(dk_s, item["keys"], res[1], add=True)
                _put(dv_s, item["keys"], res[2], add=True)

            _for_each_group(d, ATTN_GROUP_BWD, load, compute, store)

        @pl.when(step > 0)
        def _():
            for copy in copies_out(step - 1):
                copy.wait()

        for copy in copies_out(step):
            copy.start()

        @pl.when(step == n_steps - 1)
        def _():
            for copy in copies_out(step):
                copy.wait()

    slab = lambda g: pl.BlockSpec((SEQ, LANES), functools.partial(lambda hp, g: (0, 4 * g + hp), g=g))
    anywhere = pl.BlockSpec(memory_space=pl.ANY)
    by_residue = (SORT_ROWS, SORT_RESIDUES, ATTN_WIDTH)
    grads = pl.pallas_call(
        body, name="attn_bwd", grid=(4,), out_shape=(jax.ShapeDtypeStruct(by_residue, F32),) * 3,
        scratch_shapes=[pltpu.VMEM((2, 2, SEQ, LANES), F32), pltpu.VMEM((2, 3, SEQ, LANES), F32),
                        pltpu.VMEM((2 * len(DILATIONS), 2 * ATTN_BLOCK, 2 * ATTN_BLOCK), F32),
                        pltpu.SemaphoreType.DMA((2, 2, SORT_RESIDUES)), pltpu.SemaphoreType.DMA((2, 3, SORT_RESIDUES))],
        in_specs=[slab(0), slab(1), slab(2), anywhere, slab(0), anywhere], out_specs=(anywhere,) * 3,
        compiler_params=_params(("arbitrary",)),
    )(qkv_sorted, qkv_sorted, qkv_sorted, d_out.reshape(by_residue), lse_sorted, delta.reshape(by_residue))
    return tuple(g.reshape(SEQ, ATTN_WIDTH) for g in grads)


def _hgrn_lower_bound(lb_ref):
    r0, r1 = lb_ref[0:1, :], lb_ref[1:2, :]
    mx = jnp.maximum(r0, r1)
    e0, e1 = jnp.exp(r0 - mx), jnp.exp(r1 - mx)
    return e0 / (e0 + e1)


def _hgrn_gates(hq, hf, lb):
    sq = _sigmoid(hq)
    sg = _sigmoid(hf)
    f = lb + (1.0 - lb) * sg
    return hq * sq, sq, sg, f, 1.0 - f, jnp.log(f)


HGRN_PAIR = 4
HGRN_SEQ_BLOCK = 1024
HGRN_GROUP = 4
HGRN_ROWS = HGRN_GROUP * HGRN_CHUNK


def _hgrn_specs(reverse):
    n_blocks = SEQ // HGRN_SEQ_BLOCK
    width = HGRN_PAIR * HGRN_DIM
    blk = (lambda s: n_blocks - 1 - s) if reverse else (lambda s: s)
    cols = lambda g: pl.BlockSpec((None, HGRN_SEQ_BLOCK, width), functools.partial(lambda p, s, g: (g, blk(s), p), g=g))
    pair = pl.BlockSpec((HGRN_SEQ_BLOCK, width), lambda p, s: (blk(s), p))
    lb = pl.BlockSpec((2, width), lambda p, s: (0, p))
    states = pl.BlockSpec((HGRN_PAIR, HGRN_SEQ_BLOCK // HGRN_CHUNK, HGRN_DIM, HGRN_DIM),
                          lambda p, s: (p, blk(s), 0, 0))
    return cols, pair, lb, states


def _chunk_masks():
    ri = lax.broadcasted_iota(jnp.int32, (HGRN_ROWS, HGRN_ROWS), 0)
    ci = lax.broadcasted_iota(jnp.int32, (HGRN_ROWS, HGRN_ROWS), 1)
    same = (ri // HGRN_CHUNK) == (ci // HGRN_CHUNK)
    return same, same & (ri >= ci), same & (ri <= ci)


def _mm_select(sel, v):
    hi = v.astype(BF16)
    r1 = v - hi.astype(F32)
    mid = r1.astype(BF16)
    lo = (r1 - mid.astype(F32)).astype(BF16)
    return _mm(sel, hi) + _mm(sel, mid) + _mm(sel, lo)


def _head_cols(a, h):
    return a[:, HGRN_DIM * h:HGRN_DIM * (h + 1)]


def _hgrn_fwd(proj, lb_raw):
    t, rws = HGRN_CHUNK, HGRN_ROWS

    def body(hq_ref, hf_ref, hi_ref, lb_ref, rec_ref, st_ref, state):
        @pl.when(pl.program_id(1) == 0)
        def _():
            state[...] = jnp.zeros_like(state)

        lb = _hgrn_lower_bound(lb_ref)
        same, causal, _ = _chunk_masks()
        sel = jnp.concatenate([causal, same], axis=0).astype(BF16)

        def group(g, sts):
            rows = pl.ds(pl.multiple_of(g * rws, rws), rws)
            q, _, _, _, k, lf = _hgrn_gates(hq_ref[rows, :], hf_ref[rows, :], lb)
            sums = _mm_select(sel, lf)
            cum, last = sums[:rws], sums[rws:]
            qd = (q * jnp.exp(cum)).astype(BF16)
            ki = (k * jnp.exp(-cum)).astype(BF16)
            ke = (k * jnp.exp(last - cum)).astype(BF16)
            vb = hi_ref[rows, :].astype(BF16)
            dec = jnp.exp(last)
            new_sts, recs = [], []
            for h in range(HGRN_PAIR):
                qd_h, ke_h, vb_h = _head_cols(qd, h), _head_cols(ke, h), _head_cols(vb, h)
                att = jnp.where(causal, _mm_nt(qd_h, _head_cols(ki, h)), 0.0).astype(BF16)
                intra = _mm(att, vb_h)
                st = sts[h]
                outs = []
                for c in range(HGRN_GROUP):
                    sl = slice(c * t, (c + 1) * t)
                    st_ref[h, g * HGRN_GROUP + c] = st
                    outs.append(intra[sl] + _mm_nt(qd_h[sl], st.astype(BF16)))
                    st = st * _head_cols(dec[c * t:c * t + 1, :], h) + _mm_tn(vb_h[sl], ke_h[sl])
                new_sts.append(st)
                recs.append(jnp.concatenate(outs, axis=0))
            rec_ref[rows, :] = jnp.concatenate(recs, axis=1)
            return tuple(new_sts)

        sts = lax.fori_loop(0, HGRN_SEQ_BLOCK // rws, group, tuple(state[h] for h in range(HGRN_PAIR)))
        for h in range(HGRN_PAIR):
            state[h] = sts[h]

    cols, pair, lb, states = _hgrn_specs(reverse=False)
    return pl.pallas_call(
        body, name="hgrn_fwd", grid=(HGRN_HEADS // HGRN_PAIR, SEQ // HGRN_SEQ_BLOCK),
        out_shape=(jax.ShapeDtypeStruct((SEQ, HGRN_WIDTH), F32),
                   jax.ShapeDtypeStruct((HGRN_HEADS, N_CHUNKS, HGRN_DIM, HGRN_DIM), F32)),
        in_specs=[cols(4), cols(5), cols(6), lb], out_specs=(pair, states),
        scratch_shapes=[pltpu.VMEM((HGRN_PAIR, HGRN_DIM, HGRN_DIM), F32)],
        compiler_params=_params(("parallel", "arbitrary")),
    )(proj, proj, proj, lb_raw)


def _hgrn_bwd(proj, lb_raw, d_rec, states):
    t, rws = HGRN_CHUNK, HGRN_ROWS

    def body(hq_ref, hf_ref, hi_ref, lb_ref, do_ref, st_ref, dhq_ref, dhf_ref, dhi_ref, dlb_ref,
             dstate, dlb_acc):
        lb = _hgrn_lower_bound(lb_ref)
        same, causal, anti = _chunk_masks()
        sel = jnp.concatenate([causal, same], axis=0).astype(BF16)
        sel_t = jnp.concatenate([anti, same], axis=1).astype(BF16)
        @pl.when(pl.program_id(1) == 0)
        def _():
            dstate[...] = jnp.zeros_like(dstate)
            dlb_acc[...] = jnp.zeros_like(dlb_acc)

        n_groups = HGRN_SEQ_BLOCK // rws
        chunks = [slice(c * t, (c + 1) * t) for c in range(HGRN_GROUP)]

        def group(i, dsts_in):
            g = n_groups - 1 - i
            rows = pl.ds(pl.multiple_of(g * rws, rws), rws)
            hq = hq_ref[rows, :]
            q, sq, sg, f, k, lf = _hgrn_gates(hq, hf_ref[rows, :], lb)
            sums = _mm_select(sel, lf)
            cum, last = sums[:rws], sums[rws:]
            e_cum, e_inv, e_end, dec = jnp.exp(cum), jnp.exp(-cum), jnp.exp(last - cum), jnp.exp(last)
            qd, ki, ke = q * e_cum, k * e_inv, k * e_end
            qdb, kib, keb = qd.astype(BF16), ki.astype(BF16), ke.astype(BF16)
            vb = hi_ref[rows, :].astype(BF16)
            gb = do_ref[rows, :].astype(BF16)

            dsts_out, per_head = [], []
            for h in range(HGRN_PAIR):
                qdb_h, kib_h, keb_h = _head_cols(qdb, h), _head_cols(kib, h), _head_cols(keb, h)
                vb_h, gb_h = _head_cols(vb, h), _head_cols(gb, h)
                att = jnp.where(causal, _mm_nt(qdb_h, kib_h), 0.0).astype(BF16)
                datt = jnp.where(causal, _mm_nt(gb_h, vb_h), 0.0).astype(BF16)
                dv = _mm_tn(att, gb_h)
                dqd = _mm(datt, kib_h)
                dki = _mm_tn(datt, qdb_h)

                decs = [_head_cols(dec[c * t:c * t + 1, :], h) for c in range(HGRN_GROUP)]
                dsts = [None] * HGRN_GROUP
                dst = dsts_in[h]
                for c in reversed(range(HGRN_GROUP)):
                    dsts[c] = dst
                    dst = dst * decs[c] + _mm_tn(gb_h[chunks[c]], qdb_h[chunks[c]])
                dsts_out.append(dst)

                dv_x, dqd_x, dke, dlast_x = [], [], [], []
                for c, sl in enumerate(chunks):
                    st_prev = st_ref[h, g * HGRN_GROUP + c]
                    dstb = dsts[c].astype(BF16)
                    dv_x.append(_mm_nt(keb_h[sl], dstb))
                    dqd_x.append(_mm(gb_h[sl], st_prev.astype(BF16)))
                    dke.append(_mm(vb_h[sl], dstb))
                    ddec = jnp.sum(dsts[c] * st_prev, axis=0, keepdims=True)
                    dlast_x.append(jnp.broadcast_to(ddec * decs[c], (t, HGRN_DIM)))
                per_head.append((dv + jnp.concatenate(dv_x, axis=0), dqd + jnp.concatenate(dqd_x, axis=0),
                                 dki, jnp.concatenate(dke, axis=0), jnp.concatenate(dlast_x, axis=0)))
            dv, dqd, dki, dke, dlast = (jnp.concatenate(list(parts), axis=1) for parts in zip(*per_head))

            dq = dqd * e_cum
            dk = dki * e_inv + dke * e_end
            dke_ke = dke * ke
            dcum = dqd * qd - dki * ki - dke_ke
            dlf = _mm_select(sel_t, jnp.concatenate([dcum, dke_ke], axis=0)) + dlast
            df = dlf / f - dk
            dhq_ref[rows, :] = (dq * (sq * (1.0 + hq * (1.0 - sq)))).astype(BF16)
            dhf_ref[rows, :] = (df * (1.0 - lb) * (sg * (1.0 - sg))).astype(BF16)
            dhi_ref[rows, :] = dv.astype(BF16)
            dlb_acc[...] += jnp.sum(df * (1.0 - sg), axis=0, keepdims=True)
            return tuple(dsts_out)

        dsts = lax.fori_loop(0, n_groups, group, tuple(dstate[h] for h in range(HGRN_PAIR)))
        for h in range(HGRN_PAIR):
            dstate[h] = dsts[h]
        g0 = dlb_acc[...] * lb * (1.0 - lb)
        dlb_ref[...] = jnp.concatenate([g0, -g0], axis=0)

    cols, pair, lb_spec, st_spec = _hgrn_specs(reverse=True)
    wide = jax.ShapeDtypeStruct((SEQ, HGRN_WIDTH), BF16)
    return pl.pallas_call(
        body, name="hgrn_bwd", grid=(HGRN_HEADS // HGRN_PAIR, SEQ // HGRN_SEQ_BLOCK),
        out_shape=(wide, wide, wide, jax.ShapeDtypeStruct((2, HGRN_WIDTH), F32)),
        in_specs=[cols(4), cols(5), cols(6), lb_spec, pair, st_spec],
        out_specs=(pair, pair, pair, lb_spec),
        scratch_shapes=[pltpu.VMEM((HGRN_PAIR, HGRN_DIM, HGRN_DIM), F32),
                        pltpu.VMEM((1, HGRN_PAIR * HGRN_DIM), F32)],
        compiler_params=_params(("parallel", "arbitrary")),
    )(proj, proj, proj, lb_raw, d_rec, states)


def _group_sum(v, group):
    parts = []
    for s in range(v.shape[1] // LANES):
        slab = v[:, LANES * s:LANES * (s + 1)]
        if group == LANES:
            parts.append(jnp.broadcast_to(jnp.sum(slab, axis=-1, keepdims=True), slab.shape))
        else:
            h0 = lax.broadcasted_iota(jnp.int32, slab.shape, 1) < HEAD_DIM
            s0 = jnp.sum(jnp.where(h0, slab, 0.0), axis=-1, keepdims=True)
            s1 = jnp.sum(jnp.where(h0, 0.0, slab), axis=-1, keepdims=True)
            parts.append(jnp.where(h0, s0, s1))
    return jnp.concatenate(parts, axis=1)


def _mid(attn_o, rec, proj, x, target, w_out_g, attn_w, hgrn_w, final_w):
    tm = 256

    def branch_fwd(o, gate, w, group):
        r = lax.rsqrt(_group_sum(o * o, group) * (1.0 / group) + NORM_EPS)
        nrm = o * r
        sg = _sigmoid(gate)
        return r, nrm, sg, nrm * w * (gate * sg)

    def branch_bwd(dy, r, nrm, sg, gate, w, group):
        silu = gate * sg
        d_gate = dy * nrm * w * (sg * (1.0 + gate * (1.0 - sg)))
        d_w = jnp.sum(dy * nrm * silu, axis=0, keepdims=True)
        dn = dy * w * silu
        d_o = r * (dn - nrm * (_group_sum(dn * nrm, group) * (1.0 / group)))
        return d_o, d_gate, d_w

    def body(o_ref, rec_ref, ag_ref, hg_ref, x_ref, tgt_ref, wout_ref, aw_ref, hw_ref, fw_ref,
             dx2_ref, do_ref, delta_ref, dag_ref, drec_ref, dhg_ref, dwout_ref, dfw_ref, daw_ref, dhw_ref,
             loss_ref, dwout_acc):
        i = pl.program_id(0)

        @pl.when(i == 0)
        def _():
            dwout_acc[...] = jnp.zeros_like(dwout_acc)
            dfw_ref[...] = jnp.zeros_like(dfw_ref)
            daw_ref[...] = jnp.zeros_like(daw_ref)
            dhw_ref[...] = jnp.zeros_like(dhw_ref)
            loss_ref[...] = jnp.zeros_like(loss_ref)

        o, rc, ag, hg = o_ref[...], rec_ref[...], ag_ref[...], hg_ref[...]
        aw, hw, fw = aw_ref[...], hw_ref[...], fw_ref[...]
        ra, na, sga, ya = branch_fwd(o, ag, aw, HEAD_DIM)
        rh, nh, sgh, yh = branch_fwd(rc, hg, hw, HGRN_DIM)
        mixed = jnp.concatenate([ya, yh], axis=1).astype(BF16)
        wout = wout_ref[...]
        x2 = x_ref[...] + _mm(mixed, wout)
        rstd = lax.rsqrt(jnp.mean(x2 * x2, axis=-1, keepdims=True) + NORM_EPS)
        xn = x2 * rstd
        err = xn * fw - tgt_ref[...]
        row_loss = jnp.mean(err * err, axis=-1, keepdims=True)
        loss_ref[...] += 0.5 * jnp.sum(row_loss, axis=0, keepdims=True)
        dy = err * (1.0 / D_MODEL)
        dfw_ref[...] += jnp.sum(dy * xn, axis=0, keepdims=True)
        dxn = dy * fw
        dx2 = rstd * (dxn - xn * jnp.mean(dxn * xn, axis=-1, keepdims=True))
        dx2_ref[...] = dx2
        dx2b = dx2.astype(BF16)
        dwout_acc[...] += _mm_tn(mixed, dx2b)

        @pl.when(i == pl.num_programs(0) - 1)
        def _():
            dwout_ref[...] = dwout_acc[...].astype(BF16)

        dmixed = _mm_nt(dx2b, wout)

        d_o, d_ag, d_aw = branch_bwd(dmixed[:, :ATTN_WIDTH], ra, na, sga, ag, aw, HEAD_DIM)
        d_rec, d_hg, d_hw = branch_bwd(dmixed[:, ATTN_WIDTH:], rh, nh, sgh, hg, hw, HGRN_DIM)
        do_ref[...] = d_o
        delta_ref[...] = _group_sum(d_o * o, HEAD_DIM)
        dag_ref[...] = d_ag.astype(BF16)
        drec_ref[...] = d_rec
        dhg_ref[...] = d_hg.astype(BF16)
        daw_ref[...] += d_aw
        dhw_ref[...] += d_hw

    half = lambda: pl.BlockSpec((tm, COL_BLOCK), lambda i: (i, 0))
    full = lambda: pl.BlockSpec((tm, D_MODEL), lambda i: (i, 0))
    fixed = lambda r, c: pl.BlockSpec((r, c), lambda i: (0, 0))
    wide = jax.ShapeDtypeStruct((SEQ, COL_BLOCK), F32)
    wide_b = jax.ShapeDtypeStruct((SEQ, COL_BLOCK), BF16)
    return pl.pallas_call(
        body, name="mid", grid=(SEQ // tm,),
        out_shape=(jax.ShapeDtypeStruct((SEQ, D_MODEL), F32), wide, wide, wide_b, wide, wide_b,
                   jax.ShapeDtypeStruct((D_MODEL, D_MODEL), BF16),
                   jax.ShapeDtypeStruct((1, D_MODEL), F32), jax.ShapeDtypeStruct((1, COL_BLOCK), F32),
                   jax.ShapeDtypeStruct((1, COL_BLOCK), F32), jax.ShapeDtypeStruct((1, 1), F32)),
        scratch_shapes=[pltpu.VMEM((D_MODEL, D_MODEL), F32)],
        in_specs=[half(), half(),
                  pl.BlockSpec((None, tm, COL_BLOCK), lambda i: (3, i, 0)),
                  pl.BlockSpec((None, tm, COL_BLOCK), lambda i: (7, i, 0)),
                  full(), full(), fixed(D_MODEL, D_MODEL), fixed(1, COL_BLOCK), fixed(1, COL_BLOCK),
                  fixed(1, D_MODEL)],
        out_specs=(full(), half(), half(), half(), half(), half(), fixed(D_MODEL, D_MODEL),
                   fixed(1, D_MODEL), fixed(1, COL_BLOCK), fixed(1, COL_BLOCK), fixed(1, 1)),
        compiler_params=_params(("arbitrary",)),
    )(attn_o, rec, proj, proj, x, target, w_out_g, attn_w, hgrn_w, final_w)


def _in_proj_bwd_rows(d_groups, w_g, x, dx2, mix_w, rc, rsa, rsb):
    tm = 256

    def body(*refs):
        dg_refs = refs[:N_DEV]
        wg_ref, x_ref, dx2_ref, w_ref, c_ref, sa_ref, sb_ref, gx_ref, dpb_ref, dmw_ref = refs[N_DEV:]

        @pl.when(pl.program_id(0) == 0)
        def _():
            dmw_ref[...] = jnp.zeros_like(dmw_ref)

        parts = []
        for j in range(N_DEV):
            dp = dg_refs[j][...]
            if j < 2:
                dp = _rot_transposed(dp, c_ref[...], sa_ref[...], sb_ref[...])
            parts.append(dp.astype(BF16))
        dpb = jnp.concatenate(parts, axis=1)
        for j in range(N_DEV):
            dpb_ref[j] = parts[j]
        g = _mm_nt(dpb, wg_ref[...])
        xf = x_ref[...]
        rstd = lax.rsqrt(jnp.mean(xf * xf, axis=-1, keepdims=True) + NORM_EPS)
        xn = xf * rstd
        dmw_ref[...] += jnp.sum(g * xn, axis=0, keepdims=True)
        gw = g * w_ref[...]
        gx_ref[...] = dx2_ref[...] + rstd * (gw - xn * jnp.mean(gw * xn, axis=-1, keepdims=True))

    tile = lambda cols: pl.BlockSpec((tm, cols), lambda i: (i, 0))
    fixed = lambda r, c: pl.BlockSpec((r, c), lambda i: (0, 0))
    return pl.pallas_call(
        body, name="in_proj_bwd_rows", grid=(SEQ // tm,),
        out_shape=(jax.ShapeDtypeStruct((SEQ, D_MODEL), F32), jax.ShapeDtypeStruct((N_DEV, SEQ, COL_BLOCK), BF16),
                   jax.ShapeDtypeStruct((1, D_MODEL), F32)),
        in_specs=[tile(COL_BLOCK) for _ in range(N_DEV)] + [
            pl.BlockSpec((D_MODEL, IN_COLS), lambda i: (0, 0), pipeline_mode=pl.Buffered(1)),
            tile(D_MODEL), tile(D_MODEL), fixed(1, D_MODEL), tile(LANES), tile(LANES), tile(LANES)],
        out_specs=(tile(D_MODEL), pl.BlockSpec((N_DEV, tm, COL_BLOCK), lambda i: (0, i, 0)), fixed(1, D_MODEL)),
        compiler_params=_params(("arbitrary",)),
    )(*d_groups, w_g, x, dx2, mix_w, rc, rsa, rsb)


def _weights_exchange(hn_t, dproj_b, dwout_p, small_p):
    n_chips = N_DEV // 2
    rb = 128
    S1_IN, S1_OUT, SMALL, S2_IN, S2_OUT, VIA_IN, VIA_OUT = 0, 4, 8, 15, 17, 19, 21
    rel_of_pair = (3, 1, 2, 0)
    two_hop = n_chips - 1
    half_in, half_out = COL_BLOCK // 2, D_MODEL // 2

    def body(order_ref, hnt_ref, dp_ref, dwout_ref, small_ref, gin_ref, gout_ref, gs_ref,
             part, s1_send, s1_in, s1_out, fwd_in, fwd_out, s2_in, s2_out, via_in, via_out, land_s,
             send_sems, recv_sems):
        t = pl.program_id(0)
        me = _my_place()
        x, y, c = me
        my_chip = 2 * x + y
        sibling = (x, y, 1 - c)

        def remote(slot, src, dst, to):
            return pltpu.make_async_remote_copy(src_ref=src, dst_ref=dst, send_sem=send_sems.at[slot],
                                                recv_sem=recv_sems.at[slot], device_id=to, device_id_type=MESH)

        def s1_in_copy(pair):
            return remote(S1_IN + pair, s1_send.at[pair], s1_in.at[pair], sibling)

        def s1_out_copy(pair):
            q = my_chip ^ rel_of_pair[pair]
            return remote(S1_OUT + pair, dwout_ref.at[q, 1 - c], s1_out.at[pair], sibling)

        def s2_copies(rel):
            peer = _peer(me, 2 * rel)
            return [remote(S2_IN + rel - 1, fwd_in.at[rel - 1], s2_in.at[rel - 1], peer),
                    remote(S2_OUT + rel - 1, fwd_out.at[rel - 1], s2_out.at[rel - 1], peer)]

        def via_copies(k):
            peer = _peer(me, 2 * (2 - k))
            return [remote(VIA_IN + k, fwd_in.at[two_hop - 1, :, pl.ds(k * half_in, half_in)], via_in.at[k], peer),
                    remote(VIA_OUT + k, fwd_out.at[two_hop - 1, :, pl.ds(k * half_out, half_out)], via_out.at[k],
                           peer)]

        def small_copy(rel):
            return remote(SMALL + rel - 1, small_ref, land_s.at[rel], _peer(me, rel))

        @pl.when(t == 0)
        def _():
            land_s[0] = small_ref[...]
            for pair in range(n_chips):
                s1_out_copy(pair).start()
            for rel in range(1, N_DEV):
                small_copy(rel).start()

        part[...] = _mm(hnt_ref[...], dp_ref[...])

        def rows_loop(n_rows, fn):
            def step(b, carry):
                fn(pl.ds(pl.multiple_of(b * rb, rb), rb))
                return carry
            lax.fori_loop(0, n_rows // rb, step, 0)

        for pair, rel in enumerate(rel_of_pair):
            @pl.when(t == 2 * pair)
            def _(pair=pair):
                s1_send[pair] = part[...].astype(BF16)
                s1_in_copy(pair).start()

            @pl.when(t == 2 * pair + 1)
            def _(pair=pair, rel=rel):
                q = my_chip ^ rel
                s1_in_copy(pair).wait_recv()
                s1_out_copy(pair).wait_recv()
                dst_in = fwd_in.at[rel - 1] if rel else gin_ref
                dst_out = fwd_out.at[rel - 1] if rel else gout_ref
                passes_on = rel in (1, 2)
                if passes_on:
                    for cp in via_copies(rel - 1):
                        cp.wait_recv()

                def with_half(val, via, rows, width):
                    if not passes_on:
                        return val
                    extra = via[rel - 1, rows, :].astype(F32)
                    halves = [val[:, :width], val[:, width:]]
                    halves[rel - 1] = halves[rel - 1] + extra
                    return jnp.concatenate(halves, axis=1)

                def add_in(rows):
                    val = part[rows, :] + s1_in[pair, rows, :].astype(F32)
                    dst_in[rows, :] = with_half(val, via_in, rows, half_in).astype(dst_in.dtype)

                def add_out(rows):
                    val = dwout_ref[q, c, rows, :].astype(F32) + s1_out[pair, rows, :].astype(F32)
                    dst_out[rows, :] = with_half(val, via_out, rows, half_out).astype(dst_out.dtype)

                rows_loop(D_MODEL, add_in)
                rows_loop(WOUT_ROWS, add_out)
                if rel == two_hop:
                    for k in range(2):
                        for cp in via_copies(k):
                            cp.start()
                elif rel:
                    for cp in s2_copies(rel):
                        cp.start()

        @pl.when(t == N_DEV - 1)
        def _():
            for rel in range(1, two_hop):
                for cp in s2_copies(rel):
                    cp.wait_recv()

            def total_in(rows):
                g = gin_ref[rows, :]
                for rel in range(1, two_hop):
                    g = g + s2_in[rel - 1, rows, :].astype(F32)
                gin_ref[rows, :] = g

            def total_out(rows):
                g = gout_ref[rows, :]
                for rel in range(1, two_hop):
                    g = g + s2_out[rel - 1, rows, :].astype(F32)
                gout_ref[rows, :] = g

            rows_loop(D_MODEL, total_in)
            rows_loop(WOUT_ROWS, total_out)

            for rel in range(1, N_DEV):
                small_copy(rel).wait_recv()
            my_flat = _flat(me)
            g = land_s[my_flat ^ 0]
            for dev in range(1, N_DEV):
                g = g + land_s[my_flat ^ dev]
            gs_ref[...] = g

            for pair in range(n_chips):
                s1_in_copy(pair).wait_send()
                s1_out_copy(pair).wait_send()
            for rel in range(1, two_hop):
                for cp in s2_copies(rel) + via_copies(rel - 1):
                    cp.wait_send()
            for rel in range(1, N_DEV):
                small_copy(rel).wait_send()

    place_x, place_y, place_c = _my_place()
    my_chip = 2 * place_x + place_y
    order = jnp.stack([2 * (my_chip ^ rel) + core for rel in rel_of_pair
                       for core in (1 - place_c, place_c)]).astype(jnp.int32)

    whole = lambda: pl.BlockSpec(memory_space=pltpu.VMEM)
    in_blocks = lambda n: pltpu.VMEM((n, D_MODEL, COL_BLOCK), BF16)
    out_blocks = lambda n: pltpu.VMEM((n, WOUT_ROWS, D_MODEL), BF16)
    grid_spec = pltpu.PrefetchScalarGridSpec(
        num_scalar_prefetch=1, grid=(N_DEV,),
        in_specs=[pl.BlockSpec((D_MODEL, SEQ), lambda t, order: (0, 0), pipeline_mode=pl.Buffered(1)),
                  pl.BlockSpec((None, SEQ, COL_BLOCK), lambda t, order: (order[t], 0, 0)), whole(), whole()],
        out_specs=(whole(), whole(), whole()),
        scratch_shapes=[pltpu.VMEM((D_MODEL, COL_BLOCK), F32), in_blocks(n_chips), in_blocks(n_chips),
                        out_blocks(n_chips), in_blocks(n_chips - 1), out_blocks(n_chips - 1),
                        in_blocks(n_chips - 2), out_blocks(n_chips - 2),
                        pltpu.VMEM((2, D_MODEL, half_in), BF16), pltpu.VMEM((2, WOUT_ROWS, half_out), BF16),
                        pltpu.VMEM((N_DEV, SMALL_ROWS, LANES), F32),
                        pltpu.SemaphoreType.DMA((23,)), pltpu.SemaphoreType.DMA((23,))])
    return pl.pallas_call(
        body, name="weights_exchange", grid_spec=grid_spec,
        out_shape=(jax.ShapeDtypeStruct((D_MODEL, COL_BLOCK), F32), jax.ShapeDtypeStruct((WOUT_ROWS, D_MODEL), F32),
                   jax.ShapeDtypeStruct((SMALL_ROWS, LANES), F32)),
        compiler_params=_params(("arbitrary",)),
    )(order, hn_t, dproj_b, dwout_p.reshape(n_chips, 2, WOUT_ROWS, D_MODEL), small_p)


def _adamw(w, g, m, v):
    m = ADAM_B1 * m + (1.0 - ADAM_B1) * g
    v = ADAM_B2 * v + (1.0 - ADAM_B2) * (g * g)
    m_hat = m / (1.0 - ADAM_B1 ** ADAM_STEP)
    v_hat = v / (1.0 - ADAM_B2 ** ADAM_STEP)
    delta = -ADAM_LR * (m_hat / (jnp.sqrt(v_hat) + ADAM_EPS) + ADAM_WD * w)
    return delta, m, v


def _adamw_update(grads, weights, m_old, v_old):
    rb = 256

    def body(*refs):
        g_refs, w_refs, m_refs, v_refs = refs[0:3], refs[3:6], refs[6:9], refs[9:12]
        d_refs, nm_refs, nv_refs = refs[12:15], refs[15:18], refs[18:21]
        for k in range(3):
            n_rows = g_refs[k].shape[0]
            step_rows = min(rb, n_rows)

            def step(b, carry, k=k, step_rows=step_rows):
                rows = pl.ds(pl.multiple_of(b * step_rows, 8), step_rows)
                delta, nm, nv = _adamw(w_refs[k][rows, :], g_refs[k][rows, :], m_refs[k][rows, :], v_refs[k][rows, :])
                d_refs[k][rows, :] = delta
                nm_refs[k][rows, :] = nm
                nv_refs[k][rows, :] = nv
                return carry

            lax.fori_loop(0, n_rows // step_rows, step, 0)

    shapes = tuple(jax.ShapeDtypeStruct(g.shape, F32) for g in grads)
    vm = lambda: pl.BlockSpec(memory_space=pltpu.VMEM)
    outs = pl.pallas_call(
        body, name="adamw_update", out_shape=shapes * 3,
        in_specs=[vm() for _ in range(12)], out_specs=tuple(vm() for _ in range(9)),
        compiler_params=_params(),
    )(*grads, *weights, *m_old, *v_old)
    return outs[0:3], outs[3:6], outs[6:9]


def _pack_small(mix, attn, hgrn, lb, final, loss=None):
    def rows8(a):
        a = a.reshape(-1, LANES)
        return jnp.pad(a, ((0, 8 - a.shape[0]), (0, 0)))
    last = jnp.zeros((8, LANES), F32) if loss is None else jnp.pad(loss.reshape(1, 1), ((0, 7), (0, LANES - 1)))
    return jnp.concatenate([rows8(mix), rows8(attn), rows8(hgrn), rows8(lb), rows8(final), last], axis=0)


def _unpack_small(slab):
    return (slab[ROW_MIX:ROW_MIX + 8].reshape(1, D_MODEL), slab[ROW_ATTN:ROW_ATTN + 4].reshape(1, ATTN_WIDTH),
            slab[ROW_HGRN:ROW_HGRN + 4].reshape(1, HGRN_WIDTH), slab[ROW_LB:ROW_LB + 8].reshape(2, HGRN_WIDTH),
            slab[ROW_FINAL:ROW_FINAL + 8].reshape(D_MODEL))


def _rope(pos_row):
    j = np.arange(ROPE_ROWS)
    inv = np.where(j < ROPE_HALF, ROPE_THETA ** (-(j % ROPE_HALF) * (2.0 / ROPE_DIMS)), 0.0)
    e = np.arange(LANES) % HEAD_DIM
    hit = (j[:, None] == (e % ROPE_HALF)[None, :]) & (j[:, None] < ROPE_HALF)
    sel = np.stack([hit & (e < ROPE_DIMS), hit & (e >= ROPE_HALF) & (e < ROPE_DIMS),
                    -1.0 * (hit & (e < ROPE_HALF))]).astype(np.float32)
    return _rope_tables(pos_row, jnp.asarray(inv.astype(np.float32).reshape(ROPE_ROWS, 1)),
                        jnp.asarray(sel, dtype=BF16))


def _local_step(x, proj, qkv_sorted, w_in_g, w_out_g, tables, mix_w, attn_w, hgrn_w, lb_raw, final_w, target):
    rc, rsa, rsb = tables
    attn_o, lse = _attn_fwd_fused(qkv_sorted)
    rec, states = _hgrn_fwd(proj, lb_raw)

    (dx2, d_o, delta, d_ag, d_rec, d_hg, dwout_p, d_final, d_attn_w, d_hgrn_w, loss) = _mid(
        attn_o, rec, proj, x, target, w_out_g, attn_w, hgrn_w, final_w.reshape(1, D_MODEL))

    dqkv = _attn_bwd_fused(qkv_sorted, d_o, lse, delta)
    d_hq, d_hf, d_hi, d_lb = _hgrn_bwd(proj, lb_raw, d_rec, states)

    grad_x, dproj_b, d_mix = _in_proj_bwd_rows(
        (dqkv[0], dqkv[1], dqkv[2], d_ag, d_hq, d_hf, d_hi, d_hg), w_in_g, x, dx2, mix_w, rc, rsa, rsb)
    small_p = _pack_small(d_mix, d_attn_w, d_hgrn_w, d_lb, d_final, loss)
    return grad_x, dproj_b, dwout_p, small_p


def kernel(x, positions, w_in, w_out, mix_norm_w, attn_out_norm_w, hgrn_out_norm_w, hgrn_lb_raw, final_norm_w, loss_target, m_w_in, m_w_out, m_mix_norm_w, m_attn_out_norm_w, m_hgrn_out_norm_w, m_hgrn_lb_raw, m_final_norm_w, v_w_in, v_w_out, v_mix_norm_w, v_attn_out_norm_w, v_hgrn_out_norm_w, v_hgrn_lb_raw, v_final_norm_w):
    tables = _rope(positions)
    proj, hn_t, w_in_g, w_out_g, qkv_sorted = _gather_project(x[0], mix_norm_w, w_in[0], w_out[0], *tables)
    grad_x, dproj_b, dwout_p, small_p = _local_step(
        x[0], proj, qkv_sorted, w_in_g, w_out_g, tables, mix_norm_w, attn_out_norm_w, hgrn_out_norm_w,
        hgrn_lb_raw, final_norm_w, loss_target[0])
    g_in, g_out, g_s = _weights_exchange(hn_t, dproj_b, dwout_p, small_p)

    w_s = _pack_small(mix_norm_w, attn_out_norm_w, hgrn_out_norm_w, hgrn_lb_raw, final_norm_w)
    m_s = _pack_small(m_mix_norm_w, m_attn_out_norm_w, m_hgrn_out_norm_w, m_hgrn_lb_raw, m_final_norm_w)
    v_s = _pack_small(v_mix_norm_w, v_attn_out_norm_w, v_hgrn_out_norm_w, v_hgrn_lb_raw, v_final_norm_w)
    (d_in, d_out, d_s), (nm_in, nm_out, nm_s), (nv_in, nv_out, nv_s) = _adamw_update(
        (g_in, g_out, g_s), (w_in[0], w_out[0], w_s), (m_w_in[0], m_w_out[0], m_s), (v_w_in[0], v_w_out[0], v_s))

    loss = g_s[ROW_LOSS, 0]
    return (loss, grad_x[None], g_in[None], g_out[None], *_unpack_small(g_s),
            d_in[None], d_out[None], *_unpack_small(d_s),
            nm_in[None], nm_out[None], *_unpack_small(nm_s),
            nv_in[None], nv_out[None], *_unpack_small(nv_s))
```

```python
import functools

import jax
import jax.numpy as jnp
import numpy as np
from jax import lax
from jax.experimental import pallas as pl
from jax.experimental.pallas import tpu as pltpu

F32 = jnp.float32
BF16 = jnp.bfloat16

SEQ = 4096
D_MODEL = 1024
ATTN_WIDTH = 512
HGRN_WIDTH = 512
HEAD_DIM = 64
HGRN_HEADS = 4
HGRN_DIM = 128
HGRN_CHUNK = 64
N_CHUNKS = SEQ // HGRN_CHUNK
IN_COLS = 4096
COL_BLOCK = 512
N_DEV = 8
WOUT_ROWS = D_MODEL // N_DEV
ATTN_BLOCK = 128
DILATIONS = (1, 4, 16)
ROPE_THETA = 500000.0
ROPE_DIMS = 16
ROPE_HALF = 8
NORM_EPS = 1e-6
NEG_BIG = -1e30
LANES = 128

ADAM_LR = 0.001
ADAM_B1 = 0.9
ADAM_B2 = 0.999
ADAM_EPS = 1e-08
ADAM_WD = 0.01
ADAM_STEP = 10

SMALL_ROWS = 48
ROW_MIX, ROW_ATTN, ROW_HGRN, ROW_LB, ROW_FINAL, ROW_LOSS = 0, 8, 16, 24, 32, 40

VMEM_LIMIT = 56 * 1024 * 1024
MESH = pl.DeviceIdType.MESH


def _mm(a, b):
    return lax.dot_general(a, b, (((1,), (0,)), ((), ())), preferred_element_type=F32)


def _mm_nt(a, b):
    return lax.dot_general(a, b, (((1,), (1,)), ((), ())), preferred_element_type=F32)


def _mm_tn(a, b):
    return lax.dot_general(a, b, (((0,), (0,)), ((), ())), preferred_element_type=F32)


def _mm_exact(a, b):
    return lax.dot_general(a, b, (((1,), (0,)), ((), ())), preferred_element_type=F32,
                           precision=lax.Precision.HIGHEST)


def _sigmoid(v):
    return 1.0 / (1.0 + jnp.exp(-v))


def _params(sem=None, **kw):
    return pltpu.CompilerParams(dimension_semantics=sem, vmem_limit_bytes=VMEM_LIMIT, **kw)


def _my_place():
    return lax.axis_index("x"), lax.axis_index("y"), lax.axis_index("c")


def _peer(place, rel):
    x, y, c = place
    return (x ^ ((rel >> 2) & 1), y ^ ((rel >> 1) & 1), c ^ (rel & 1))


def _flat(place):
    x, y, c = place
    return 4 * x + 2 * y + c


ROPE_ROWS = 16


def _rope_tables(pos_row, inv_freq_col, selectors):
    def body(pos_ref, invf_ref, sel_ref, c_ref, sa_ref, sb_ref):
        ang = pos_ref[...].astype(F32) * invf_ref[...]
        cos, sin = jnp.cos(ang), jnp.sin(ang)

        def spread(v, sel):
            hi = v.astype(BF16)
            r1 = v - hi.astype(F32)
            mid = r1.astype(BF16)
            lo = (r1 - mid.astype(F32)).astype(BF16)
            return _mm_tn(hi, sel) + _mm_tn(mid, sel) + _mm_tn(lo, sel)

        e = lax.broadcasted_iota(jnp.int32, (1, LANES), 1) & (HEAD_DIM - 1)
        c_ref[...] = spread(cos, sel_ref[0]) + jnp.where(e < ROPE_DIMS, 0.0, 1.0)
        sa_ref[...] = spread(sin, sel_ref[1])
        sb_ref[...] = spread(sin, sel_ref[2])

    tab = jax.ShapeDtypeStruct((SEQ, LANES), F32)
    vm = lambda: pl.BlockSpec(memory_space=pltpu.VMEM)
    return pl.pallas_call(
        body, name="rope_tables", out_shape=(tab, tab, tab),
        in_specs=[vm(), vm(), vm()], out_specs=(vm(), vm(), vm()), compiler_params=_params(),
    )(pos_row, inv_freq_col, selectors)


def _per_slab(fn, t):
    return jnp.concatenate([fn(t[:, LANES * s:LANES * (s + 1)]) for s in range(t.shape[1] // LANES)], axis=1)


def _rot(t, c, sa, sb):
    return _per_slab(lambda u: u * c + pltpu.roll(u, ROPE_HALF, 1) * sa + pltpu.roll(u, LANES - ROPE_HALF, 1) * sb, t)


def _rot_transposed(g, c, sa, sb):
    return _per_slab(
        lambda u: u * c + pltpu.roll(u * sa, LANES - ROPE_HALF, 1) + pltpu.roll(u * sb, ROPE_HALF, 1), g)


def _gather_project(x, mix_w, w_in, w_out, rc, rsa, rsb):
    tm = 1024
    n_tiles = SEQ // tm
    arrival_of_step = (None, 0, 1, 2, 4, 5, 3, 6)

    def body(order_ref, x_ref, w_ref, win_ref, wout_ref, c_ref, sa_ref, sb_ref,
             proj_ref, hnt_ref, gin_hbm, gout_hbm, qkv_hbm,
             hn_s, w_land, wout_land, stage, sort_stage, send_sems, recv_sems, local_sems, sort_sems):
        g, i = pl.program_id(0), pl.program_id(1)
        me = _my_place()
        x_, y_, c_ = me
        sibling = (x_, y_, 1 - c_)
        chips = [(1 - x_, y_), (x_, 1 - y_), (1 - x_, 1 - y_)]

        def slab(which, place):
            idx = _flat(place)
            if which == 0:
                return w_land.at[idx]
            return wout_land.at[pl.ds(pl.multiple_of(idx * WOUT_ROWS, WOUT_ROWS), WOUT_ROWS), :]

        def remote(which, k, ref, to, src=None):
            return pltpu.make_async_remote_copy(
                src_ref=ref if src is None else src, dst_ref=ref, send_sem=send_sems.at[8 * which + k],
                recv_sem=recv_sems.at[8 * which + k], device_id=to, device_id_type=MESH)

        def copy(which, k, block, to, src=None):
            return remote(which, k, slab(which, block), to, src)

        def half(which, place, part):
            n = (D_MODEL if which == 0 else WOUT_ROWS) // 2
            if which == 0:
                return w_land.at[_flat(place), pl.ds(n * part, n), :]
            return wout_land.at[pl.ds(pl.multiple_of(_flat(place) * WOUT_ROWS + n * part, n), n), :]

        def first_copies(which):
            src = stage if which == 0 else None
            return ([copy(which, 0, me, sibling, src)]
                    + [copy(which, 1 + j, me, (*chips[j], c_), src) for j in range(2)])

        def relay(which, part):
            frm, to = (chips[1], chips[0]) if part == 0 else (chips[0], chips[1])
            return remote(which, 3 if part == 0 else 7, half(which, (*frm, c_), part), (*to, c_))

        def two_hop_half(which, part):
            return remote(which, 3 if part == 0 else 7, half(which, (*chips[2], c_), part), me)

        def pass_on(which, j):
            return copy(which, 4 + j, (*chips[j], c_), sibling)

        def arrival(which, k):
            if k == 0:
                return copy(which, 0, sibling, me)
            if k <= 2:
                return copy(which, k, (*chips[k - 1], c_), me)
            return copy(which, k, (*chips[k - 4], 1 - c_), me)

        def to_hbm(step):
            idx = order_ref[step]
            cols = pl.ds(pl.multiple_of(idx * COL_BLOCK, COL_BLOCK), COL_BLOCK)
            return pltpu.make_async_copy(w_land.at[idx], gin_hbm.at[:, cols], local_sems.at[step])

        @pl.when((g == 0) & (i == 0))
        def _():
            stage[...] = win_ref[...].astype(BF16)
            w_land[_flat(me)] = stage[...]
            wout_land[pl.ds(pl.multiple_of(_flat(me) * WOUT_ROWS, WOUT_ROWS), WOUT_ROWS), :] = (
                wout_ref[...].astype(BF16))
            for cp in first_copies(0) + first_copies(1)[:1]:
                cp.start()
            to_hbm(0).start()

        for step, k in enumerate(arrival_of_step):
            if k is None:
                continue

            @pl.when((g == step) & (i == 0))
            def _(k=k, step=step):
                if k == 3:
                    two_hop_half(0, 0).wait_recv()
                    two_hop_half(0, 1).wait_recv()
                else:
                    arrival(0, k).wait_recv()
                to_hbm(step).start()
                if 1 <= k <= 3:
                    pass_on(0, k - 1).start()
                if k == 1:
                    relay(0, 1).start()
                    for cp in first_copies(1)[1:]:
                        cp.start()
                if k == 2:
                    relay(0, 0).start()
                if k in (4, 5):
                    arrival(1, k - 3).wait_recv()
                    relay(1, 5 - k).start()

        rows = pl.ds(pl.multiple_of(i * tm, tm), tm)

        @pl.when(g == 0)
        def _():
            xf = x_ref[...]
            ms = jnp.mean(xf * xf, axis=-1, keepdims=True)
            hn = xf * lax.rsqrt(ms + NORM_EPS) * w_ref[...]
            hnt_ref[...] = hn.T.astype(BF16)
            hn_s[rows, :] = hn.astype(BF16)

        group = order_ref[g]

        def sorted_copy(tile_value):
            per = tm // SORT_RESIDUES
            cols = pl.ds(pl.multiple_of(group * COL_BLOCK, COL_BLOCK), COL_BLOCK)
            buf = i % 2

            def out_copies(tile, b):
                return [pltpu.make_async_copy(
                    sort_stage.at[b, :, r, :], qkv_hbm.at[r, pl.ds(pl.multiple_of(tile * per, per), per), cols],
                    sort_sems.at[b, r]) for r in range(SORT_RESIDUES)]

            @pl.when(i >= 2)
            def _():
                for copy in out_copies(i - 2, buf):
                    copy.wait()

            sort_stage[buf] = tile_value.reshape(per, SORT_RESIDUES, COL_BLOCK)
            for copy in out_copies(i, buf):
                copy.start()

            @pl.when(i == n_tiles - 1)
            def _():
                for copy in out_copies(i - 1, 1 - buf) + out_copies(i, buf):
                    copy.wait()

        @pl.when(group < 2)
        def _():
            rotated = _rot(_mm(hn_s[rows, :], w_land[group]), c_ref[...], sa_ref[...], sb_ref[...])
            proj_ref[...] = rotated
            sorted_copy(rotated)

        @pl.when(group == 2)
        def _():
            value = _mm(hn_s[rows, :], w_land[group])
            proj_ref[...] = value
            sorted_copy(value)

        @pl.when(group > 2)
        def _():
            proj_ref[...] = _mm(hn_s[rows, :], w_land[group])

        @pl.when((g == N_DEV - 1) & (i == n_tiles - 1))
        def _():
            pass_on(1, 0).start()
            pass_on(1, 1).start()
            two_hop_half(1, 0).wait_recv()
            two_hop_half(1, 1).wait_recv()
            pass_on(1, 2).start()
            for k in (0, 4, 5, 6):
                arrival(1, k).wait_recv()
            for which in (0, 1):
                for cp in (first_copies(which) + [relay(which, part) for part in range(2)]
                           + [pass_on(which, j) for j in range(3)]):
                    cp.wait_send()
            wout_copy = pltpu.make_async_copy(wout_land, gout_hbm, local_sems.at[N_DEV])
            wout_copy.start()
            for step in range(N_DEV):
                to_hbm(step).wait()
            wout_copy.wait()

    me = _my_place()
    x_, y_, c_ = me
    chips = [(1 - x_, y_), (x_, 1 - y_), (1 - x_, 1 - y_)]
    order = jnp.stack([_flat(p) for p in (
        me, (x_, y_, 1 - c_), (*chips[0], c_), (*chips[1], c_), (*chips[0], 1 - c_), (*chips[1], 1 - c_),
        (*chips[2], c_), (*chips[2], 1 - c_))]).astype(jnp.int32)

    first_sweep = lambda g, i, order: (jnp.where(g == 0, i, n_tiles - 1), 0)
    tab = pl.BlockSpec((tm, LANES), lambda g, i, order: (jnp.where(order[g] < 2, i, 0), 0))
    whole = lambda: pl.BlockSpec(memory_space=pltpu.VMEM)
    grid_spec = pltpu.PrefetchScalarGridSpec(
        num_scalar_prefetch=1, grid=(N_DEV, n_tiles),
        in_specs=[pl.BlockSpec((tm, D_MODEL), first_sweep),
                  pl.BlockSpec((1, D_MODEL), lambda g, i, order: (0, 0)),
                  whole(), whole(), tab, tab, tab],
        out_specs=(pl.BlockSpec((None, tm, COL_BLOCK), lambda g, i, order: (order[g], i, 0)),
                   pl.BlockSpec((D_MODEL, tm), lambda g, i, order: (0, jnp.where(g == 0, i, n_tiles - 1))),
                   pl.BlockSpec(memory_space=pl.ANY), pl.BlockSpec(memory_space=pl.ANY),
                   pl.BlockSpec(memory_space=pl.ANY)),
        scratch_shapes=[pltpu.VMEM((SEQ, D_MODEL), BF16),
                        pltpu.VMEM((N_DEV, D_MODEL, COL_BLOCK), BF16),
                        pltpu.VMEM((D_MODEL, D_MODEL), BF16),
                        pltpu.VMEM((D_MODEL, COL_BLOCK), BF16),
                        pltpu.VMEM((2, tm // SORT_RESIDUES, SORT_RESIDUES, COL_BLOCK), F32),
                        pltpu.SemaphoreType.DMA((16,)), pltpu.SemaphoreType.DMA((16,)),
                        pltpu.SemaphoreType.DMA((N_DEV + 1,)), pltpu.SemaphoreType.DMA((2, SORT_RESIDUES))])
    proj, hn_t, w_in_g, w_out_g, qkv_sorted = pl.pallas_call(
        body, name="gather_project", grid_spec=grid_spec,
        out_shape=(jax.ShapeDtypeStruct((N_DEV, SEQ, COL_BLOCK), F32), jax.ShapeDtypeStruct((D_MODEL, SEQ), BF16),
                   jax.ShapeDtypeStruct((D_MODEL, IN_COLS), BF16), jax.ShapeDtypeStruct((D_MODEL, D_MODEL), BF16),
                   jax.ShapeDtypeStruct((SORT_RESIDUES, SORT_ROWS, 3 * COL_BLOCK), F32)),
        compiler_params=_params(("arbitrary", "arbitrary")),
    )(order, x, mix_w, w_in, w_out, rc, rsa, rsb)
    return proj, hn_t, w_in_g, w_out_g, qkv_sorted.reshape(SEQ, 3 * COL_BLOCK)


SCORE_SCALE = HEAD_DIM ** -0.5
ATTN_GROUP_FWD = 32
ATTN_GROUP_BWD = 32
BLOCKS_PER_PATTERN = SEQ // ATTN_BLOCK
SORT_RESIDUES = 16
SORT_ROWS = SEQ // SORT_RESIDUES


def _write_band_bias(bias_ref):
    row = lax.broadcasted_iota(jnp.int32, (2 * ATTN_BLOCK, 2 * ATTN_BLOCK), 0) & (ATTN_BLOCK - 1)
    col = lax.broadcasted_iota(jnp.int32, (2 * ATTN_BLOCK, 2 * ATTN_BLOCK), 1)
    for pi, d in enumerate(DILATIONS):
        per = SORT_RESIDUES // d
        ahead = per * (row % (8 * d) - col % (16 * d)) + (row // (8 * d) - col // (16 * d))
        dist = ATTN_BLOCK + ahead
        bias_ref[2 * pi] = jnp.where((dist >= 0) & (dist <= ATTN_BLOCK), 0.0, NEG_BIG)
        bias_ref[2 * pi + 1] = jnp.where(ahead >= 0, 0.0, NEG_BIG)


def _head0_lanes():
    return lax.broadcasted_iota(jnp.int32, (ATTN_BLOCK, LANES), 1) < HEAD_DIM


def _stack_heads(t, h0):
    return jnp.concatenate([jnp.where(h0, t, 0.0), jnp.where(h0, 0.0, t)], axis=0).astype(BF16)


def _block_runs(i, d):
    nblk = BLOCKS_PER_PATTERN // d
    r, n = i // nblk, i % nblk
    kn = jnp.maximum(n - 1, 0)
    rows, keys = [], []
    for c in range(SORT_RESIDUES // d):
        base = SORT_ROWS * (c * d + r)
        rows.append(pl.ds(pl.multiple_of(base + 8 * d * n, 8), 8 * d))
        keys.append(pl.ds(pl.multiple_of(base + 8 * d * kn, 8), 16 * d))
    return rows, keys, (n == 0).astype(jnp.int32)


def _take(ref, runs):
    return jnp.concatenate([ref[run, :] for run in runs], axis=0)


def _put(ref, runs, value, add=False):
    at = 0
    for run in runs:
        piece = value[at:at + run.size]
        if add:
            ref[run, :] += piece
        else:
            ref[run, :] = piece
        at += run.size


def _sort_copies(src_hbm, lane_block, dst_ref, sem_ref):
    lanes = pl.ds(pl.multiple_of(LANES * lane_block, LANES), LANES)
    return [pltpu.make_async_copy(src_hbm.at[:, r, lanes], dst_ref.at[pl.ds(SORT_ROWS * r, SORT_ROWS), :],
                                  sem_ref.at[r]) for r in range(SORT_RESIDUES)]


def _unsort_copies(src_ref, dst_hbm, lane_block, sem_ref):
    lanes = pl.ds(pl.multiple_of(LANES * lane_block, LANES), LANES)
    return [pltpu.make_async_copy(src_ref.at[pl.ds(SORT_ROWS * r, SORT_ROWS), :], dst_hbm.at[:, r, lanes],
                                  sem_ref.at[r]) for r in range(SORT_RESIDUES)]


def _for_each_group(d, n_group, load, compute, store):
    def group(g, carry):
        items = [load(*_block_runs(g * n_group + u, d)) for u in range(n_group)]
        results = [compute(item) for item in items]
        for item, res in zip(items, results):
            store(item, res)
        return carry

    lax.fori_loop(0, BLOCKS_PER_PATTERN // n_group, group, 0)


def _attn_fwd_fused(qkv_sorted):
    n_pat = len(DILATIONS)
    tile2 = (2 * ATTN_BLOCK, LANES)

    def body(q_ref, k_ref, v_ref, o_hbm, lse_ref, o_slots, m_acc, l_acc, bias_ref, out_sem):
        step, n_steps = pl.program_id(0), pl.num_programs(0)
        pl.when(step == 0)(lambda: _write_band_bias(bias_ref))
        slot = step % 2
        o_acc = o_slots.at[slot]
        h0 = _head0_lanes()
        for pi, d in enumerate(DILATIONS):
            first, last = pi == 0, pi == n_pat - 1

            def load(rows, keys, which, first=first, pi=pi):
                item = dict(rows=rows, keys=keys, which=2 * pi + which)
                if not first:
                    item.update(o=_take(o_acc, rows), m=[_take(m_acc.at[h], rows) for h in range(2)],
                                l=[_take(l_acc.at[h], rows) for h in range(2)])
                return item

            def compute(item, first=first):
                kb = _take(k_ref, item["keys"]).astype(BF16)
                vb = _take(v_ref, item["keys"]).astype(BF16)
                s = _mm_nt(_stack_heads(_take(q_ref, item["rows"]) * SCORE_SCALE, h0), kb) + bias_ref[item["which"]]
                mb = jnp.max(s, axis=-1, keepdims=True)
                if first:
                    p = jnp.exp(s - mb)
                    mn = jnp.broadcast_to(mb, tile2)
                else:
                    m_old = jnp.concatenate(item["m"], axis=0)
                    mn = jnp.maximum(m_old, mb)
                    alpha = jnp.exp(m_old - mn)
                    p = jnp.exp(s - jnp.concatenate([mn, mn], axis=1))
                ls = jnp.sum(p, axis=-1, keepdims=True)
                pv = _mm(p.astype(BF16), vb)
                if first:
                    return pv, mn, jnp.broadcast_to(ls, tile2)
                o_old = jnp.concatenate([item["o"], item["o"]], axis=0)
                return alpha * o_old + pv, mn, alpha * jnp.concatenate(item["l"], axis=0) + ls

            def store(item, res, last=last):
                rows = item["rows"]
                (o0, o1), (m0, m1), (l0, l1) = ((a[:ATTN_BLOCK], a[ATTN_BLOCK:]) for a in res)
                if last:
                    _put(o_acc, rows, jnp.where(h0, o0 / l0, o1 / l1))
                    _put(lse_ref, rows, jnp.where(h0, m0 + jnp.log(l0), m1 + jnp.log(l1)))
                else:
                    _put(o_acc, rows, jnp.where(h0, o0, o1))
                    for h, (m, l) in enumerate(((m0, l0), (m1, l1))):
                        _put(m_acc.at[h], rows, m)
                        _put(l_acc.at[h], rows, l)

            _for_each_group(d, ATTN_GROUP_FWD, load, compute, store)

        def copies_out(of_step):
            return _unsort_copies(o_slots.at[of_step % 2], o_hbm, of_step, out_sem.at[of_step % 2])

        @pl.when(step > 0)
        def _():
            for copy in copies_out(step - 1):
                copy.wait()

        for copy in copies_out(step):
            copy.start()

        @pl.when(step == n_steps - 1)
        def _():
            for copy in copies_out(step):
                copy.wait()

    slab = lambda g: pl.BlockSpec((SEQ, LANES), functools.partial(lambda hp, g: (0, 4 * g + hp), g=g))
    wide = jax.ShapeDtypeStruct((SEQ, ATTN_WIDTH), F32)
    o_rows, lse = pl.pallas_call(
        body, name="attn_fwd", grid=(4,),
        out_shape=(jax.ShapeDtypeStruct((SORT_ROWS, SORT_RESIDUES, ATTN_WIDTH), F32), wide),
        in_specs=[slab(0), slab(1), slab(2)], out_specs=(pl.BlockSpec(memory_space=pl.ANY), slab(0)),
        scratch_shapes=[pltpu.VMEM((2, SEQ, LANES), F32), pltpu.VMEM((2, SEQ, LANES), F32),
                        pltpu.VMEM((2, SEQ, LANES), F32),
                        pltpu.VMEM((2 * len(DILATIONS), 2 * ATTN_BLOCK, 2 * ATTN_BLOCK), F32),
                        pltpu.SemaphoreType.DMA((2, SORT_RESIDUES))],
        compiler_params=_params(("arbitrary",)),
    )(qkv_sorted, qkv_sorted, qkv_sorted)
    return o_rows.reshape(SEQ, ATTN_WIDTH), lse


def _attn_bwd_fused(qkv_sorted, d_out, lse_sorted, delta):
    def body(q_ref, k_ref, v_ref, do_hbm, lse_ref, del_hbm, dq_hbm, dk_hbm, dv_hbm,
             in_slots, out_slots, bias_ref, in_sem, out_sem):
        step, n_steps = pl.program_id(0), pl.num_programs(0)
        slot = step % 2

        def copies_in(of_step):
            s = of_step % 2
            return [copy for j, hbm in enumerate((do_hbm, del_hbm))
                    for copy in _sort_copies(hbm, of_step, in_slots.at[s, j], in_sem.at[s, j])]

        def copies_out(of_step):
            s = of_step % 2
            return [copy for j, hbm in enumerate((dq_hbm, dk_hbm, dv_hbm))
                    for copy in _unsort_copies(out_slots.at[s, j], hbm, of_step, out_sem.at[s, j])]

        @pl.when(step == 0)
        def _():
            for copy in copies_in(step):
                copy.start()
            _write_band_bias(bias_ref)

        @pl.when(step + 1 < n_steps)
        def _():
            for copy in copies_in(step + 1):
                copy.start()

        do_s, del_s = in_slots.at[slot, 0], in_slots.at[slot, 1]
        dq_s, dk_s, dv_s = (out_slots.at[slot, j] for j in range(3))
        dk_s[...] = jnp.zeros_like(dk_s)
        dv_s[...] = jnp.zeros_like(dv_s)
        for copy in copies_in(step):
            copy.wait()
        h0 = _head0_lanes()
        for pi, d in enumerate(DILATIONS):
            first = pi == 0

            def load(rows, keys, which, pi=pi):
                return dict(rows=rows, keys=keys, q=_take(q_ref, rows), g=_take(do_s, rows),
                            lse=_take(lse_ref, rows), delta=_take(del_s, rows),
                            k=_take(k_ref, keys).astype(BF16), v=_take(v_ref, keys).astype(BF16),
                            bias=bias_ref[2 * pi + which])

            def per_head(t):
                swapped = pltpu.roll(t, HEAD_DIM, 1)
                both = jnp.concatenate([jnp.where(h0, t, swapped), jnp.where(h0, swapped, t)], axis=0)
                return jnp.concatenate([both, both], axis=1)

            def compute(item):
                q2, g2 = _stack_heads(item["q"] * SCORE_SCALE, h0), _stack_heads(item["g"], h0)
                s = _mm_nt(q2, item["k"]) + item["bias"]
                p = jnp.exp(s - per_head(item["lse"]))
                dp = _mm_nt(g2, item["v"])
                ds = (p * (dp - per_head(item["delta"]))).astype(BF16)
                dq2 = _mm(ds, item["k"])
                dq = jnp.where(h0, dq2[:ATTN_BLOCK], dq2[ATTN_BLOCK:]) * SCORE_SCALE
                return dq, _mm_tn(ds, q2), _mm_tn(p.astype(BF16), g2)

            def store(item, res, first=first):
                _put(dq_s, item["rows"], res[0], add=not first)
                _put(dk_s, item["keys"], res[1], add=True)
                _put(dv_s, item["keys"], res[2], add=True)

            _for_each_group(d, ATTN_GROUP_BWD, load, compute, store)

        @pl.when(step > 0)
        def _():
            for copy in copies_out(step - 1):
                copy.wait()

        for copy in copies_out(step):
            copy.start()

        @pl.when(step == n_steps - 1)
        def _():
            for copy in copies_out(step):
                copy.wait()

    slab = lambda g: pl.BlockSpec((SEQ, LANES), functools.partial(lambda hp, g: (0, 4 * g + hp), g=g))
    anywhere = pl.BlockSpec(memory_space=pl.ANY)
    by_residue = (SORT_ROWS, SORT_RESIDUES, ATTN_WIDTH)
    grads = pl.pallas_call(
        body, name="attn_bwd", grid=(4,), out_shape=(jax.ShapeDtypeStruct(by_residue, F32),) * 3,
        scratch_shapes=[pltpu.VMEM((2, 2, SEQ, LANES), F32), pltpu.VMEM((2, 3, SEQ, LANES), F32),
                        pltpu.VMEM((2 * len(DILATIONS), 2 * ATTN_BLOCK, 2 * ATTN_BLOCK), F32),
                        pltpu.SemaphoreType.DMA((2, 2, SORT_RESIDUES)), pltpu.SemaphoreType.DMA((2, 3, SORT_RESIDUES))],
        in_specs=[slab(0), slab(1), slab(2), anywhere, slab(0), anywhere], out_specs=(anywhere,) * 3,
        compiler_params=_params(("arbitrary",)),
    )(qkv_sorted, qkv_sorted, qkv_sorted, d_out.reshape(by_residue), lse_sorted, delta.reshape(by_residue))
    return tuple(g.reshape(SEQ, ATTN_WIDTH) for g in grads)


def _hgrn_lower_bound(lb_ref):
    r0, r1 = lb_ref[0:1, :], lb_ref[1:2, :]
    mx = jnp.maximum(r0, r1)
    e0, e1 = jnp.exp(r0 - mx), jnp.exp(r1 - mx)
    return e0 / (e0 + e1)


def _hgrn_gates(hq, hf, lb):
    sq = _sigmoid(hq)
    sg = _sigmoid(hf)
    f = lb + (1.0 - lb) * sg
    return hq * sq, sq, sg, f, 1.0 - f, jnp.log(f)


HGRN_PAIR = 4
HGRN_SEQ_BLOCK = 1024
HGRN_GROUP = 4
HGRN_ROWS = HGRN_GROUP * HGRN_CHUNK


def _hgrn_specs(reverse):
    n_blocks = SEQ // HGRN_SEQ_BLOCK
    width = HGRN_PAIR * HGRN_DIM
    blk = (lambda s: n_blocks - 1 - s) if reverse else (lambda s: s)
    cols = lambda g: pl.BlockSpec((None, HGRN_SEQ_BLOCK, width), functools.partial(lambda p, s, g: (g, blk(s), p), g=g))
    pair = pl.BlockSpec((HGRN_SEQ_BLOCK, width), lambda p, s: (blk(s), p))
    lb = pl.BlockSpec((2, width), lambda p, s: (0, p))
    states = pl.BlockSpec((HGRN_PAIR, HGRN_SEQ_BLOCK // HGRN_CHUNK, HGRN_DIM, HGRN_DIM),
                          lambda p, s: (p, blk(s), 0, 0))
    return cols, pair, lb, states


def _chunk_masks():
    ri = lax.broadcasted_iota(jnp.int32, (HGRN_ROWS, HGRN_ROWS), 0)
    ci = lax.broadcasted_iota(jnp.int32, (HGRN_ROWS, HGRN_ROWS), 1)
    same = (ri // HGRN_CHUNK) == (ci // HGRN_CHUNK)
    return same, same & (ri >= ci), same & (ri <= ci)


def _mm_select(sel, v):
    hi = v.astype(BF16)
    r1 = v - hi.astype(F32)
    mid = r1.astype(BF16)
    lo = (r1 - mid.astype(F32)).astype(BF16)
    return _mm(sel, hi) + _mm(sel, mid) + _mm(sel, lo)


def _head_cols(a, h):
    return a[:, HGRN_DIM * h:HGRN_DIM * (h + 1)]


def _hgrn_fwd(proj, lb_raw):
    t, rws = HGRN_CHUNK, HGRN_ROWS

    def body(hq_ref, hf_ref, hi_ref, lb_ref, rec_ref, st_ref, state):
        @pl.when(pl.program_id(1) == 0)
        def _():
            state[...] = jnp.zeros_like(state)

        lb = _hgrn_lower_bound(lb_ref)
        same, causal, _ = _chunk_masks()
        sel = jnp.concatenate([causal, same], axis=0).astype(BF16)

        def group(g, sts):
            rows = pl.ds(pl.multiple_of(g * rws, rws), rws)
            q, _, _, _, k, lf = _hgrn_gates(hq_ref[rows, :], hf_ref[rows, :], lb)
            sums = _mm_select(sel, lf)
            cum, last = sums[:rws], sums[rws:]
            qd = (q * jnp.exp(cum)).astype(BF16)
            ki = (k * jnp.exp(-cum)).astype(BF16)
            ke = (k * jnp.exp(last - cum)).astype(BF16)
            vb = hi_ref[rows, :].astype(BF16)
            dec = jnp.exp(last)
            new_sts, recs = [], []
            for h in range(HGRN_PAIR):
                qd_h, ke_h, vb_h = _head_cols(qd, h), _head_cols(ke, h), _head_cols(vb, h)
                att = jnp.where(causal, _mm_nt(qd_h, _head_cols(ki, h)), 0.0).astype(BF16)
                intra = _mm(att, vb_h)
                st = sts[h]
                outs = []
                for c in range(HGRN_GROUP):
                    sl = slice(c * t, (c + 1) * t)
                    st_ref[h, g * HGRN_GROUP + c] = st
                    outs.append(intra[sl] + _mm_nt(qd_h[sl], st.astype(BF16)))
                    st = st * _head_cols(dec[c * t:c * t + 1, :], h) + _mm_tn(vb_h[sl], ke_h[sl])
                new_sts.append(st)
                recs.append(jnp.concatenate(outs, axis=0))
            rec_ref[rows, :] = jnp.concatenate(recs, axis=1)
            return tuple(new_sts)

        sts = lax.fori_loop(0, HGRN_SEQ_BLOCK // rws, group, tuple(state[h] for h in range(HGRN_PAIR)), unroll=True)
        for h in range(HGRN_PAIR):
            state[h] = sts[h]

    cols, pair, lb, states = _hgrn_specs(reverse=False)
    return pl.pallas_call(
        body, name="hgrn_fwd", grid=(HGRN_HEADS // HGRN_PAIR, SEQ // HGRN_SEQ_BLOCK),
        out_shape=(jax.ShapeDtypeStruct((SEQ, HGRN_WIDTH), F32),
                   jax.ShapeDtypeStruct((HGRN_HEADS, N_CHUNKS, HGRN_DIM, HGRN_DIM), F32)),
        in_specs=[cols(4), cols(5), cols(6), lb], out_specs=(pair, states),
        scratch_shapes=[pltpu.VMEM((HGRN_PAIR, HGRN_DIM, HGRN_DIM), F32)],
        compiler_params=_params(("parallel", "arbitrary")),
    )(proj, proj, proj, lb_raw)


def _hgrn_bwd(proj, lb_raw, d_rec, states):
    t, rws = HGRN_CHUNK, HGRN_ROWS

    def body(hq_ref, hf_ref, hi_ref, lb_ref, do_ref, st_ref, dhq_ref, dhf_ref, dhi_ref, dlb_ref,
             dstate, dlb_acc):
        lb = _hgrn_lower_bound(lb_ref)
        same, causal, anti = _chunk_masks()
        sel = jnp.concatenate([causal, same], axis=0).astype(BF16)
        sel_t = jnp.concatenate([anti, same], axis=1).astype(BF16)
        @pl.when(pl.program_id(1) == 0)
        def _():
            dstate[...] = jnp.zeros_like(dstate)
            dlb_acc[...] = jnp.zeros_like(dlb_acc)

        n_groups = HGRN_SEQ_BLOCK // rws
        chunks = [slice(c * t, (c + 1) * t) for c in range(HGRN_GROUP)]

        def group(i, dsts_in):
            g = n_groups - 1 - i
            rows = pl.ds(pl.multiple_of(g * rws, rws), rws)
            hq = hq_ref[rows, :]
            q, sq, sg, f, k, lf = _hgrn_gates(hq, hf_ref[rows, :], lb)
            sums = _mm_select(sel, lf)
            cum, last = sums[:rws], sums[rws:]
            e_cum, e_inv, e_end, dec = jnp.exp(cum), jnp.exp(-cum), jnp.exp(last - cum), jnp.exp(last)
            qd, ki, ke = q * e_cum, k * e_inv, k * e_end
            qdb, kib, keb = qd.astype(BF16), ki.astype(BF16), ke.astype(BF16)
            vb = hi_ref[rows, :].astype(BF16)
            gb = do_ref[rows, :].astype(BF16)

            dsts_out, per_head = [], []
            for h in range(HGRN_PAIR):
                qdb_h, kib_h, keb_h = _head_cols(qdb, h), _head_cols(kib, h), _head_cols(keb, h)
                vb_h, gb_h = _head_cols(vb, h), _head_cols(gb, h)
                att = jnp.where(causal, _mm_nt(qdb_h, kib_h), 0.0).astype(BF16)
                datt = jnp.where(causal, _mm_nt(gb_h, vb_h), 0.0).astype(BF16)
                dv = _mm_tn(att, gb_h)
                dqd = _mm(datt, kib_h)
                dki = _mm_tn(datt, qdb_h)

                decs = [_head_cols(dec[c * t:c * t + 1, :], h) for c in range(HGRN_GROUP)]
                dsts = [None] * HGRN_GROUP
                dst = dsts_in[h]
                for c in reversed(range(HGRN_GROUP)):
                    dsts[c] = dst
                    dst = dst * decs[c] + _mm_tn(gb_h[chunks[c]], qdb_h[chunks[c]])
                dsts_out.append(dst)

                dv_x, dqd_x, dke, dlast_x = [], [], [], []
                for c, sl in enumerate(chunks):
                    st_prev = st_ref[h, g * HGRN_GROUP + c]
                    dstb = dsts[c].astype(BF16)
                    dv_x.append(_mm_nt(keb_h[sl], dstb))
                    dqd_x.append(_mm(gb_h[sl], st_prev.astype(BF16)))
                    dke.append(_mm(vb_h[sl], dstb))
                    ddec = jnp.sum(dsts[c] * st_prev, axis=0, keepdims=True)
                    dlast_x.append(jnp.broadcast_to(ddec * decs[c], (t, HGRN_DIM)))
                per_head.append((dv + jnp.concatenate(dv_x, axis=0), dqd + jnp.concatenate(dqd_x, axis=0),
                                 dki, jnp.concatenate(dke, axis=0), jnp.concatenate(dlast_x, axis=0)))
            dv, dqd, dki, dke, dlast = (jnp.concatenate(list(parts), axis=1) for parts in zip(*per_head))

            dq = dqd * e_cum
            dk = dki * e_inv + dke * e_end
            dke_ke = dke * ke
            dcum = dqd * qd - dki * ki - dke_ke
            dlf = _mm_select(sel_t, jnp.concatenate([dcum, dke_ke], axis=0)) + dlast
            df = dlf / f - dk
            dhq_ref[rows, :] = (dq * (sq * (1.0 + hq * (1.0 - sq)))).astype(BF16)
            dhf_ref[rows, :] = (df * (1.0 - lb) * (sg * (1.0 - sg))).astype(BF16)
            dhi_ref[rows, :] = dv.astype(BF16)
            dlb_acc[...] += jnp.sum(df * (1.0 - sg), axis=0, keepdims=True)
            return tuple(dsts_out)

        dsts = lax.fori_loop(0, n_groups, group, tuple(dstate[h] for h in range(HGRN_PAIR)), unroll=True)
        for h in range(HGRN_PAIR):
            dstate[h] = dsts[h]
        g0 = dlb_acc[...] * lb * (1.0 - lb)
        dlb_ref[...] = jnp.concatenate([g0, -g0], axis=0)

    cols, pair, lb_spec, st_spec = _hgrn_specs(reverse=True)
    wide = jax.ShapeDtypeStruct((SEQ, HGRN_WIDTH), BF16)
    return pl.pallas_call(
        body, name="hgrn_bwd", grid=(HGRN_HEADS // HGRN_PAIR, SEQ // HGRN_SEQ_BLOCK),
        out_shape=(wide, wide, wide, jax.ShapeDtypeStruct((2, HGRN_WIDTH), F32)),
        in_specs=[cols(4), cols(5), cols(6), lb_spec, pair, st_spec],
        out_specs=(pair, pair, pair, lb_spec),
        scratch_shapes=[pltpu.VMEM((HGRN_PAIR, HGRN_DIM, HGRN_DIM), F32),
                        pltpu.VMEM((1, HGRN_PAIR * HGRN_DIM), F32)],
        compiler_params=_params(("parallel", "arbitrary")),
    )(proj, proj, proj, lb_raw, d_rec, states)


def _group_sum(v, group):
    parts = []
    for s in range(v.shape[1] // LANES):
        slab = v[:, LANES * s:LANES * (s + 1)]
        if group == LANES:
            parts.append(jnp.broadcast_to(jnp.sum(slab, axis=-1, keepdims=True), slab.shape))
        else:
            h0 = lax.broadcasted_iota(jnp.int32, slab.shape, 1) < HEAD_DIM
            s0 = jnp.sum(jnp.where(h0, slab, 0.0), axis=-1, keepdims=True)
            s1 = jnp.sum(jnp.where(h0, 0.0, slab), axis=-1, keepdims=True)
            parts.append(jnp.where(h0, s0, s1))
    return jnp.concatenate(parts, axis=1)


def _mid(attn_o, rec, proj, x, target, w_out_g, attn_w, hgrn_w, final_w):
    tm = 256

    def branch_fwd(o, gate, w, group):
        r = lax.rsqrt(_group_sum(o * o, group) * (1.0 / group) + NORM_EPS)
        nrm = o * r
        sg = _sigmoid(gate)
        return r, nrm, sg, nrm * w * (gate * sg)

    def branch_bwd(dy, r, nrm, sg, gate, w, group):
        silu = gate * sg
        d_gate = dy * nrm * w * (sg * (1.0 + gate * (1.0 - sg)))
        d_w = jnp.sum(dy * nrm * silu, axis=0, keepdims=True)
        dn = dy * w * silu
        d_o = r * (dn - nrm * (_group_sum(dn * nrm, group) * (1.0 / group)))
        return d_o, d_gate, d_w

    def body(o_ref, rec_ref, ag_ref, hg_ref, x_ref, tgt_ref, wout_ref, aw_ref, hw_ref, fw_ref,
             dx2_ref, do_ref, delta_ref, dag_ref, drec_ref, dhg_ref, dwout_ref, dfw_ref, daw_ref, dhw_ref,
             loss_ref, dwout_acc):
        i = pl.program_id(0)

        @pl.when(i == 0)
        def _():
            dwout_acc[...] = jnp.zeros_like(dwout_acc)
            dfw_ref[...] = jnp.zeros_like(dfw_ref)
            daw_ref[...] = jnp.zeros_like(daw_ref)
            dhw_ref[...] = jnp.zeros_like(dhw_ref)
            loss_ref[...] = jnp.zeros_like(loss_ref)

        o, rc, ag, hg = o_ref[...], rec_ref[...], ag_ref[...], hg_ref[...]
        aw, hw, fw = aw_ref[...], hw_ref[...], fw_ref[...]
        ra, na, sga, ya = branch_fwd(o, ag, aw, HEAD_DIM)
        rh, nh, sgh, yh = branch_fwd(rc, hg, hw, HGRN_DIM)
        mixed = jnp.concatenate([ya, yh], axis=1).astype(BF16)
        wout = wout_ref[...]
        x2 = x_ref[...] + _mm(mixed, wout)
        rstd = lax.rsqrt(jnp.mean(x2 * x2, axis=-1, keepdims=True) + NORM_EPS)
        xn = x2 * rstd
        err = xn * fw - tgt_ref[...]
        row_loss = jnp.mean(err * err, axis=-1, keepdims=True)
        loss_ref[...] += 0.5 * jnp.sum(row_loss, axis=0, keepdims=True)
        dy = err * (1.0 / D_MODEL)
        dfw_ref[...] += jnp.sum(dy * xn, axis=0, keepdims=True)
        dxn = dy * fw
        dx2 = rstd * (dxn - xn * jnp.mean(dxn * xn, axis=-1, keepdims=True))
        dx2_ref[...] = dx2
        dx2b = dx2.astype(BF16)
        dwout_acc[...] += _mm_tn(mixed, dx2b)

        @pl.when(i == pl.num_programs(0) - 1)
        def _():
            dwout_ref[...] = dwout_acc[...].astype(BF16)

        dmixed = _mm_nt(dx2b, wout)

        d_o, d_ag, d_aw = branch_bwd(dmixed[:, :ATTN_WIDTH], ra, na, sga, ag, aw, HEAD_DIM)
        d_rec, d_hg, d_hw = branch_bwd(dmixed[:, ATTN_WIDTH:], rh, nh, sgh, hg, hw, HGRN_DIM)
        do_ref[...] = d_o
        delta_ref[...] = _group_sum(d_o * o, HEAD_DIM)
        dag_ref[...] = d_ag.astype(BF16)
        drec_ref[...] = d_rec
        dhg_ref[...] = d_hg.astype(BF16)
        daw_ref[...] += d_aw
        dhw_ref[...] += d_hw

    half = lambda: pl.BlockSpec((tm, COL_BLOCK), lambda i: (i, 0))
    full = lambda: pl.BlockSpec((tm, D_MODEL), lambda i: (i, 0))
    fixed = lambda r, c: pl.BlockSpec((r, c), lambda i: (0, 0))
    wide = jax.ShapeDtypeStruct((SEQ, COL_BLOCK), F32)
    wide_b = jax.ShapeDtypeStruct((SEQ, COL_BLOCK), BF16)
    return pl.pallas_call(
        body, name="mid", grid=(SEQ // tm,),
        out_shape=(jax.ShapeDtypeStruct((SEQ, D_MODEL), F32), wide, wide, wide_b, wide, wide_b,
                   jax.ShapeDtypeStruct((D_MODEL, D_MODEL), BF16),
                   jax.ShapeDtypeStruct((1, D_MODEL), F32), jax.ShapeDtypeStruct((1, COL_BLOCK), F32),
                   jax.ShapeDtypeStruct((1, COL_BLOCK), F32), jax.ShapeDtypeStruct((1, 1), F32)),
        scratch_shapes=[pltpu.VMEM((D_MODEL, D_MODEL), F32)],
        in_specs=[half(), half(),
                  pl.BlockSpec((None, tm, COL_BLOCK), lambda i: (3, i, 0)),
                  pl.BlockSpec((None, tm, COL_BLOCK), lambda i: (7, i, 0)),
                  full(), full(), fixed(D_MODEL, D_MODEL), fixed(1, COL_BLOCK), fixed(1, COL_BLOCK),
                  fixed(1, D_MODEL)],
        out_specs=(full(), half(), half(), half(), half(), half(), fixed(D_MODEL, D_MODEL),
                   fixed(1, D_MODEL), fixed(1, COL_BLOCK), fixed(1, COL_BLOCK), fixed(1, 1)),
        compiler_params=_params(("arbitrary",)),
    )(attn_o, rec, proj, proj, x, target, w_out_g, attn_w, hgrn_w, final_w)


def _in_proj_bwd_rows(d_groups, w_g, x, dx2, mix_w, rc, rsa, rsb):
    tm = 256

    def body(*refs):
        dg_refs = refs[:N_DEV]
        wg_ref, x_ref, dx2_ref, w_ref, c_ref, sa_ref, sb_ref, gx_ref, dpb_ref, dmw_ref = refs[N_DEV:]

        @pl.when(pl.program_id(0) == 0)
        def _():
            dmw_ref[...] = jnp.zeros_like(dmw_ref)

        parts = []
        for j in range(N_DEV):
            dp = dg_refs[j][...]
            if j < 2:
                dp = _rot_transposed(dp, c_ref[...], sa_ref[...], sb_ref[...])
            parts.append(dp.astype(BF16))
        dpb = jnp.concatenate(parts, axis=1)
        for j in range(N_DEV):
            dpb_ref[j] = parts[j]
        g = _mm_nt(dpb, wg_ref[...])
        xf = x_ref[...]
        rstd = lax.rsqrt(jnp.mean(xf * xf, axis=-1, keepdims=True) + NORM_EPS)
        xn = xf * rstd
        dmw_ref[...] += jnp.sum(g * xn, axis=0, keepdims=True)
        gw = g * w_ref[...]
        gx_ref[...] = dx2_ref[...] + rstd * (gw - xn * jnp.mean(gw * xn, axis=-1, keepdims=True))

    tile = lambda cols: pl.BlockSpec((tm, cols), lambda i: (i, 0))
    fixed = lambda r, c: pl.BlockSpec((r, c), lambda i: (0, 0))
    return pl.pallas_call(
        body, name="in_proj_bwd_rows", grid=(SEQ // tm,),
        out_shape=(jax.ShapeDtypeStruct((SEQ, D_MODEL), F32), jax.ShapeDtypeStruct((N_DEV, SEQ, COL_BLOCK), BF16),
                   jax.ShapeDtypeStruct((1, D_MODEL), F32)),
        in_specs=[tile(COL_BLOCK) for _ in range(N_DEV)] + [
            pl.BlockSpec((D_MODEL, IN_COLS), lambda i: (0, 0), pipeline_mode=pl.Buffered(1)),
            tile(D_MODEL), tile(D_MODEL), fixed(1, D_MODEL), tile(LANES), tile(LANES), tile(LANES)],
        out_specs=(tile(D_MODEL), pl.BlockSpec((N_DEV, tm, COL_BLOCK), lambda i: (0, i, 0)), fixed(1, D_MODEL)),
        compiler_params=_params(("arbitrary",)),
    )(*d_groups, w_g, x, dx2, mix_w, rc, rsa, rsb)


def _weights_exchange(hn_t, dproj_b, dwout_p, small_p):
    n_chips = N_DEV // 2
    rb = 128
    S1_IN, S1_OUT, SMALL, S2_IN, S2_OUT, VIA_IN, VIA_OUT = 0, 4, 8, 15, 17, 19, 21
    rel_of_pair = (3, 1, 2, 0)
    two_hop = n_chips - 1
    half_in, half_out = COL_BLOCK // 2, D_MODEL // 2

    def body(order_ref, hnt_ref, dp_ref, dwout_ref, small_ref, gin_ref, gout_ref, gs_ref,
             part, s1_send, s1_in, s1_out, fwd_in, fwd_out, s2_in, s2_out, via_in, via_out, land_s,
             send_sems, recv_sems):
        t = pl.program_id(0)
        me = _my_place()
        x, y, c = me
        my_chip = 2 * x + y
        sibling = (x, y, 1 - c)

        def remote(slot, src, dst, to):
            return pltpu.make_async_remote_copy(src_ref=src, dst_ref=dst, send_sem=send_sems.at[slot],
                                                recv_sem=recv_sems.at[slot], device_id=to, device_id_type=MESH)

        def s1_in_copy(pair):
            return remote(S1_IN + pair, s1_send.at[pair], s1_in.at[pair], sibling)

        def s1_out_copy(pair):
            q = my_chip ^ rel_of_pair[pair]
            return remote(S1_OUT + pair, dwout_ref.at[q, 1 - c], s1_out.at[pair], sibling)

        def s2_copies(rel):
            peer = _peer(me, 2 * rel)
            return [remote(S2_IN + rel - 1, fwd_in.at[rel - 1], s2_in.at[rel - 1], peer),
                    remote(S2_OUT + rel - 1, fwd_out.at[rel - 1], s2_out.at[rel - 1], peer)]

        def via_copies(k):
            peer = _peer(me, 2 * (2 - k))
            return [remote(VIA_IN + k, fwd_in.at[two_hop - 1, :, pl.ds(k * half_in, half_in)], via_in.at[k], peer),
                    remote(VIA_OUT + k, fwd_out.at[two_hop - 1, :, pl.ds(k * half_out, half_out)], via_out.at[k],
                           peer)]

        def small_copy(rel):
            return remote(SMALL + rel - 1, small_ref, land_s.at[rel], _peer(me, rel))

        @pl.when(t == 0)
        def _():
            land_s[0] = small_ref[...]
            for pair in range(n_chips):
                s1_out_copy(pair).start()
            for rel in range(1, N_DEV):
                small_copy(rel).start()

        part[...] = _mm(hnt_ref[...], dp_ref[...])

        def rows_loop(n_rows, fn):
            def step(b, carry):
                fn(pl.ds(pl.multiple_of(b * rb, rb), rb))
                return carry
            lax.fori_loop(0, n_rows // rb, step, 0)

        for pair, rel in enumerate(rel_of_pair):
            @pl.when(t == 2 * pair)
            def _(pair=pair):
                s1_send[pair] = part[...].astype(BF16)
                s1_in_copy(pair).start()

            @pl.when(t == 2 * pair + 1)
            def _(pair=pair, rel=rel):
                q = my_chip ^ rel
                s1_in_copy(pair).wait_recv()
                s1_out_copy(pair).wait_recv()
                dst_in = fwd_in.at[rel - 1] if rel else gin_ref
                dst_out = fwd_out.at[rel - 1] if rel else gout_ref
                passes_on = rel in (1, 2)
                if passes_on:
                    for cp in via_copies(rel - 1):
                        cp.wait_recv()

                def with_half(val, via, rows, width):
                    if not passes_on:
                        return val
                    extra = via[rel - 1, rows, :].astype(F32)
                    halves = [val[:, :width], val[:, width:]]
                    halves[rel - 1] = halves[rel - 1] + extra
                    return jnp.concatenate(halves, axis=1)

                def add_in(rows):
                    val = part[rows, :] + s1_in[pair, rows, :].astype(F32)
                    dst_in[rows, :] = with_half(val, via_in, rows, half_in).astype(dst_in.dtype)

                def add_out(rows):
                    val = dwout_ref[q, c, rows, :].astype(F32) + s1_out[pair, rows, :].astype(F32)
                    dst_out[rows, :] = with_half(val, via_out, rows, half_out).astype(dst_out.dtype)

                rows_loop(D_MODEL, add_in)
                rows_loop(WOUT_ROWS, add_out)
                if rel == two_hop:
                    for k in range(2):
                        for cp in via_copies(k):
                            cp.start()
                elif rel:
                    for cp in s2_copies(rel):
                        cp.start()

        @pl.when(t == N_DEV - 1)
        def _():
            for rel in range(1, two_hop):
                for cp in s2_copies(rel):
                    cp.wait_recv()

            def total_in(rows):
                g = gin_ref[rows, :]
                for rel in range(1, two_hop):
                    g = g + s2_in[rel - 1, rows, :].astype(F32)
                gin_ref[rows, :] = g

            def total_out(rows):
                g = gout_ref[rows, :]
                for rel in range(1, two_hop):
                    g = g + s2_out[rel - 1, rows, :].astype(F32)
                gout_ref[rows, :] = g

            rows_loop(D_MODEL, total_in)
            rows_loop(WOUT_ROWS, total_out)

            for rel in range(1, N_DEV):
                small_copy(rel).wait_recv()
            my_flat = _flat(me)
            g = land_s[my_flat ^ 0]
            for dev in range(1, N_DEV):
                g = g + land_s[my_flat ^ dev]
            gs_ref[...] = g

            for pair in range(n_chips):
                s1_in_copy(pair).wait_send()
                s1_out_copy(pair).wait_send()
            for rel in range(1, two_hop):
                for cp in s2_copies(rel) + via_copies(rel - 1):
                    cp.wait_send()
            for rel in range(1, N_DEV):
                small_copy(rel).wait_send()

    place_x, place_y, place_c = _my_place()
    my_chip = 2 * place_x + place_y
    order = jnp.stack([2 * (my_chip ^ rel) + core for rel in rel_of_pair
                       for core in (1 - place_c, place_c)]).astype(jnp.int32)

    whole = lambda: pl.BlockSpec(memory_space=pltpu.VMEM)
    in_blocks = lambda n: pltpu.VMEM((n, D_MODEL, COL_BLOCK), BF16)
    out_blocks = lambda n: pltpu.VMEM((n, WOUT_ROWS, D_MODEL), BF16)
    grid_spec = pltpu.PrefetchScalarGridSpec(
        num_scalar_prefetch=1, grid=(N_DEV,),
        in_specs=[pl.BlockSpec((D_MODEL, SEQ), lambda t, order: (0, 0), pipeline_mode=pl.Buffered(1)),
                  pl.BlockSpec((None, SEQ, COL_BLOCK), lambda t, order: (order[t], 0, 0)), whole(), whole()],
        out_specs=(whole(), whole(), whole()),
        scratch_shapes=[pltpu.VMEM((D_MODEL, COL_BLOCK), F32), in_blocks(n_chips), in_blocks(n_chips),
                        out_blocks(n_chips), in_blocks(n_chips - 1), out_blocks(n_chips - 1),
                        in_blocks(n_chips - 2), out_blocks(n_chips - 2),
                        pltpu.VMEM((2, D_MODEL, half_in), BF16), pltpu.VMEM((2, WOUT_ROWS, half_out), BF16),
                        pltpu.VMEM((N_DEV, SMALL_ROWS, LANES), F32),
                        pltpu.SemaphoreType.DMA((23,)), pltpu.SemaphoreType.DMA((23,))])
    return pl.pallas_call(
        body, name="weights_exchange", grid_spec=grid_spec,
        out_shape=(jax.ShapeDtypeStruct((D_MODEL, COL_BLOCK), F32), jax.ShapeDtypeStruct((WOUT_ROWS, D_MODEL), F32),
                   jax.ShapeDtypeStruct((SMALL_ROWS, LANES), F32)),
        compiler_params=_params(("arbitrary",)),
    )(order, hn_t, dproj_b, dwout_p.reshape(n_chips, 2, WOUT_ROWS, D_MODEL), small_p)


def _adamw(w, g, m, v):
    m = ADAM_B1 * m + (1.0 - ADAM_B1) * g
    v = ADAM_B2 * v + (1.0 - ADAM_B2) * (g * g)
    m_hat = m / (1.0 - ADAM_B1 ** ADAM_STEP)
    v_hat = v / (1.0 - ADAM_B2 ** ADAM_STEP)
    delta = -ADAM_LR * (m_hat / (jnp.sqrt(v_hat) + ADAM_EPS) + ADAM_WD * w)
    return delta, m, v


def _adamw_update(grads, weights, m_old, v_old):
    rb = 256

    def body(*refs):
        g_refs, w_refs, m_refs, v_refs = refs[0:3], refs[3:6], refs[6:9], refs[9:12]
        d_refs, nm_refs, nv_refs = refs[12:15], refs[15:18], refs[18:21]
        for k in range(3):
            n_rows = g_refs[k].shape[0]
            step_rows = min(rb, n_rows)

            def step(b, carry, k=k, step_rows=step_rows):
                rows = pl.ds(pl.multiple_of(b * step_rows, 8), step_rows)
                delta, nm, nv = _adamw(w_refs[k][rows, :], g_refs[k][rows, :], m_refs[k][rows, :], v_refs[k][rows, :])
                d_refs[k][rows, :] = delta
                nm_refs[k][rows, :] = nm
                nv_refs[k][rows, :] = nv
                return carry

            lax.fori_loop(0, n_rows // step_rows, step, 0)

    shapes = tuple(jax.ShapeDtypeStruct(g.shape, F32) for g in grads)
    vm = lambda: pl.BlockSpec(memory_space=pltpu.VMEM)
    outs = pl.pallas_call(
        body, name="adamw_update", out_shape=shapes * 3,
        in_specs=[vm() for _ in range(12)], out_specs=tuple(vm() for _ in range(9)),
        compiler_params=_params(),
    )(*grads, *weights, *m_old, *v_old)
    return outs[0:3], outs[3:6], outs[6:9]


def _pack_small(mix, attn, hgrn, lb, final, loss=None):
    def rows8(a):
        a = a.reshape(-1, LANES)
        return jnp.pad(a, ((0, 8 - a.shape[0]), (0, 0)))
    last = jnp.zeros((8, LANES), F32) if loss is None else jnp.pad(loss.reshape(1, 1), ((0, 7), (0, LANES - 1)))
    return jnp.concatenate([rows8(mix), rows8(attn), rows8(hgrn), rows8(lb), rows8(final), last], axis=0)


def _unpack_small(slab):
    return (slab[ROW_MIX:ROW_MIX + 8].reshape(1, D_MODEL), slab[ROW_ATTN:ROW_ATTN + 4].reshape(1, ATTN_WIDTH),
            slab[ROW_HGRN:ROW_HGRN + 4].reshape(1, HGRN_WIDTH), slab[ROW_LB:ROW_LB + 8].reshape(2, HGRN_WIDTH),
            slab[ROW_FINAL:ROW_FINAL + 8].reshape(D_MODEL))


def _rope(pos_row):
    j = np.arange(ROPE_ROWS)
    inv = np.where(j < ROPE_HALF, ROPE_THETA ** (-(j % ROPE_HALF) * (2.0 / ROPE_DIMS)), 0.0)
    e = np.arange(LANES) % HEAD_DIM
    hit = (j[:, None] == (e % ROPE_HALF)[None, :]) & (j[:, None] < ROPE_HALF)
    sel = np.stack([hit & (e < ROPE_DIMS), hit & (e >= ROPE_HALF) & (e < ROPE_DIMS),
                    -1.0 * (hit & (e < ROPE_HALF))]).astype(np.float32)
    return _rope_tables(pos_row, jnp.asarray(inv.astype(np.float32).reshape(ROPE_ROWS, 1)),
                        jnp.asarray(sel, dtype=BF16))


def _local_step(x, proj, qkv_sorted, w_in_g, w_out_g, tables, mix_w, attn_w, hgrn_w, lb_raw, final_w, target):
    rc, rsa, rsb = tables
    attn_o, lse = _attn_fwd_fused(qkv_sorted)
    rec, states = _hgrn_fwd(proj, lb_raw)

    (dx2, d_o, delta, d_ag, d_rec, d_hg, dwout_p, d_final, d_attn_w, d_hgrn_w, loss) = _mid(
        attn_o, rec, proj, x, target, w_out_g, attn_w, hgrn_w, final_w.reshape(1, D_MODEL))

    dqkv = _attn_bwd_fused(qkv_sorted, d_o, lse, delta)
    d_hq, d_hf, d_hi, d_lb = _hgrn_bwd(proj, lb_raw, d_rec, states)

    grad_x, dproj_b, d_mix = _in_proj_bwd_rows(
        (dqkv[0], dqkv[1], dqkv[2], d_ag, d_hq, d_hf, d_hi, d_hg), w_in_g, x, dx2, mix_w, rc, rsa, rsb)
    small_p = _pack_small(d_mix, d_attn_w, d_hgrn_w, d_lb, d_final, loss)
    return grad_x, dproj_b, dwout_p, small_p


def kernel(x, positions, w_in, w_out, mix_norm_w, attn_out_norm_w, hgrn_out_norm_w, hgrn_lb_raw, final_norm_w, loss_target, m_w_in, m_w_out, m_mix_norm_w, m_attn_out_norm_w, m_hgrn_out_norm_w, m_hgrn_lb_raw, m_final_norm_w, v_w_in, v_w_out, v_mix_norm_w, v_attn_out_norm_w, v_hgrn_out_norm_w, v_hgrn_lb_raw, v_final_norm_w):
    tables = _rope(positions)
    proj, hn_t, w_in_g, w_out_g, qkv_sorted = _gather_project(x[0], mix_norm_w, w_in[0], w_out[0], *tables)
    grad_x, dproj_b, dwout_p, small_p = _local_step(
        x[0], proj, qkv_sorted, w_in_g, w_out_g, tables, mix_norm_w, attn_out_norm_w, hgrn_out_norm_w,
        hgrn_lb_raw, final_norm_w, loss_target[0])
    g_in, g_out, g_s = _weights_exchange(hn_t, dproj_b, dwout_p, small_p)

    w_s = _pack_small(mix_norm_w, attn_out_norm_w, hgrn_out_norm_w, hgrn_lb_raw, final_norm_w)
    m_s = _pack_small(m_mix_norm_w, m_attn_out_norm_w, m_hgrn_out_norm_w, m_hgrn_lb_raw, m_final_norm_w)
    v_s = _pack_small(v_mix_norm_w, v_attn_out_norm_w, v_hgrn_out_norm_w, v_hgrn_lb_raw, v_final_norm_w)
    (d_in, d_out, d_s), (nm_in, nm_out, nm_s), (nv_in, nv_out, nv_s) = _adamw_update(
        (g_in, g_out, g_s), (w_in[0], w_out[0], w_s), (m_w_in[0], m_w_out[0], m_s), (v_w_in[0], v_w_out[0], v_s))

    loss = g_s[ROW_LOSS, 0]
    return (loss, grad_x[None], g_in[None], g_out[None], *_unpack_small(g_s),
            d_in[None], d_out[None], *_unpack_small(d_s),
            nm_in[None], nm_out[None], *_unpack_small(nm_s),
            nv_in[None], nv_out[None], *_unpack_small(nv_s))
```

```python
import functools

import jax
import jax.numpy as jnp
import numpy as np
from jax import lax
from jax.experimental import pallas as pl
from jax.experimental.pallas import tpu as pltpu

F32 = jnp.float32
BF16 = jnp.bfloat16

SEQ = 4096
D_MODEL = 1024
ATTN_WIDTH = 512
HGRN_WIDTH = 512
HEAD_DIM = 64
HGRN_HEADS = 4
HGRN_DIM = 128
HGRN_CHUNK = 64
N_CHUNKS = SEQ // HGRN_CHUNK
IN_COLS = 4096
COL_BLOCK = 512
N_DEV = 8
WOUT_ROWS = D_MODEL // N_DEV
ATTN_BLOCK = 128
DILATIONS = (1, 4, 16)
ROPE_THETA = 500000.0
ROPE_DIMS = 16
ROPE_HALF = 8
NORM_EPS = 1e-6
NEG_BIG = -1e30
LANES = 128

ADAM_LR = 0.001
ADAM_B1 = 0.9
ADAM_B2 = 0.999
ADAM_EPS = 1e-08
ADAM_WD = 0.01
ADAM_STEP = 10

SMALL_ROWS = 48
ROW_MIX, ROW_ATTN, ROW_HGRN, ROW_LB, ROW_FINAL, ROW_LOSS = 0, 8, 16, 24, 32, 40

VMEM_LIMIT = 56 * 1024 * 1024
MESH = pl.DeviceIdType.MESH


def _mm(a, b):
    return lax.dot_general(a, b, (((1,), (0,)), ((), ())), preferred_element_type=F32)


def _mm_nt(a, b):
    return lax.dot_general(a, b, (((1,), (1,)), ((), ())), preferred_element_type=F32)


def _mm_tn(a, b):
    return lax.dot_general(a, b, (((0,), (0,)), ((), ())), preferred_element_type=F32)


def _mm_exact(a, b):
    return lax.dot_general(a, b, (((1,), (0,)), ((), ())), preferred_element_type=F32,
                           precision=lax.Precision.HIGHEST)


def _sigmoid(v):
    return 1.0 / (1.0 + jnp.exp(-v))


def _params(sem=None, **kw):
    return pltpu.CompilerParams(dimension_semantics=sem, vmem_limit_bytes=VMEM_LIMIT, **kw)


def _my_place():
    return lax.axis_index("x"), lax.axis_index("y"), lax.axis_index("c")


def _peer(place, rel):
    x, y, c = place
    return (x ^ ((rel >> 2) & 1), y ^ ((rel >> 1) & 1), c ^ (rel & 1))


def _flat(place):
    x, y, c = place
    return 4 * x + 2 * y + c


ROPE_ROWS = 16


def _rope_tables(pos_row, inv_freq_col, selectors):
    def body(pos_ref, invf_ref, sel_ref, c_ref, sa_ref, sb_ref):
        ang = pos_ref[...].astype(F32) * invf_ref[...]
        cos, sin = jnp.cos(ang), jnp.sin(ang)

        def spread(v, sel):
            hi = v.astype(BF16)
            r1 = v - hi.astype(F32)
            mid = r1.astype(BF16)
            lo = (r1 - mid.astype(F32)).astype(BF16)
            return _mm_tn(hi, sel) + _mm_tn(mid, sel) + _mm_tn(lo, sel)

        e = lax.broadcasted_iota(jnp.int32, (1, LANES), 1) & (HEAD_DIM - 1)
        c_ref[...] = spread(cos, sel_ref[0]) + jnp.where(e < ROPE_DIMS, 0.0, 1.0)
        sa_ref[...] = spread(sin, sel_ref[1])
        sb_ref[...] = spread(sin, sel_ref[2])

    tab = jax.ShapeDtypeStruct((SEQ, LANES), F32)
    vm = lambda: pl.BlockSpec(memory_space=pltpu.VMEM)
    return pl.pallas_call(
        body, name="rope_tables", out_shape=(tab, tab, tab),
        in_specs=[vm(), vm(), vm()], out_specs=(vm(), vm(), vm()), compiler_params=_params(),
    )(pos_row, inv_freq_col, selectors)


def _per_slab(fn, t):
    return jnp.concatenate([fn(t[:, LANES * s:LANES * (s + 1)]) for s in range(t.shape[1] // LANES)], axis=1)


def _rot(t, c, sa, sb):
    return _per_slab(lambda u: u * c + pltpu.roll(u, ROPE_HALF, 1) * sa + pltpu.roll(u, LANES - ROPE_HALF, 1) * sb, t)


def _rot_transposed(g, c, sa, sb):
    return _per_slab(
        lambda u: u * c + pltpu.roll(u * sa, LANES - ROPE_HALF, 1) + pltpu.roll(u * sb, ROPE_HALF, 1), g)


def _gather_project(x, mix_w, w_in, w_out, rc, rsa, rsb):
    tm = 1024
    n_tiles = SEQ // tm
    arrival_of_step = (None, 0, 1, 2, 4, 5, 3, 6)

    def body(order_ref, x_ref, w_ref, win_ref, wout_ref, c_ref, sa_ref, sb_ref,
             proj_ref, hnt_ref, gin_hbm, gout_hbm, qkv_hbm,
             hn_s, w_land, wout_land, stage, sort_stage, send_sems, recv_sems, local_sems, sort_sems):
        g, i = pl.program_id(0), pl.program_id(1)
        me = _my_place()
        x_, y_, c_ = me
        sibling = (x_, y_, 1 - c_)
        chips = [(1 - x_, y_), (x_, 1 - y_), (1 - x_, 1 - y_)]

        def slab(which, place):
            idx = _flat(place)
            if which == 0:
                return w_land.at[idx]
            return wout_land.at[pl.ds(pl.multiple_of(idx * WOUT_ROWS, WOUT_ROWS), WOUT_ROWS), :]

        def remote(which, k, ref, to, src=None):
            return pltpu.make_async_remote_copy(
                src_ref=ref if src is None else src, dst_ref=ref, send_sem=send_sems.at[8 * which + k],
                recv_sem=recv_sems.at[8 * which + k], device_id=to, device_id_type=MESH)

        def copy(which, k, block, to, src=None):
            return remote(which, k, slab(which, block), to, src)

        def half(which, place, part):
            n = (D_MODEL if which == 0 else WOUT_ROWS) // 2
            if which == 0:
                return w_land.at[_flat(place), pl.ds(n * part, n), :]
            return wout_land.at[pl.ds(pl.multiple_of(_flat(place) * WOUT_ROWS + n * part, n), n), :]

        def first_copies(which):
            src = stage if which == 0 else None
            return ([copy(which, 0, me, sibling, src)]
                    + [copy(which, 1 + j, me, (*chips[j], c_), src) for j in range(2)])

        def relay(which, part):
            frm, to = (chips[1], chips[0]) if part == 0 else (chips[0], chips[1])
            return remote(which, 3 if part == 0 else 7, half(which, (*frm, c_), part), (*to, c_))

        def two_hop_half(which, part):
            return remote(which, 3 if part == 0 else 7, half(which, (*chips[2], c_), part), me)

        def pass_on(which, j):
            return copy(which, 4 + j, (*chips[j], c_), sibling)

        def arrival(which, k):
            if k == 0:
                return copy(which, 0, sibling, me)
            if k <= 2:
                return copy(which, k, (*chips[k - 1], c_), me)
            return copy(which, k, (*chips[k - 4], 1 - c_), me)

        def to_hbm(step):
            idx = order_ref[step]
            cols = pl.ds(pl.multiple_of(idx * COL_BLOCK, COL_BLOCK), COL_BLOCK)
            return pltpu.make_async_copy(w_land.at[idx], gin_hbm.at[:, cols], local_sems.at[step])

        @pl.when((g == 0) & (i == 0))
        def _():
            stage[...] = win_ref[...].astype(BF16)
            w_land[_flat(me)] = stage[...]
            wout_land[pl.ds(pl.multiple_of(_flat(me) * WOUT_ROWS, WOUT_ROWS), WOUT_ROWS), :] = (
                wout_ref[...].astype(BF16))
            for cp in first_copies(0) + first_copies(1)[:1]:
                cp.start()
            to_hbm(0).start()

        for step, k in enumerate(arrival_of_step):
            if k is None:
                continue

            @pl.when((g == step) & (i == 0))
            def _(k=k, step=step):
                if k == 3:
                    two_hop_half(0, 0).wait_recv()
                    two_hop_half(0, 1).wait_recv()
                else:
                    arrival(0, k).wait_recv()
                to_hbm(step).start()
                if 1 <= k <= 3:
                    pass_on(0, k - 1).start()
                if k == 1:
                    relay(0, 1).start()
                    for cp in first_copies(1)[1:]:
                        cp.start()
                if k == 2:
                    relay(0, 0).start()
                if k in (4, 5):
                    arrival(1, k - 3).wait_recv()
                    relay(1, 5 - k).start()

        rows = pl.ds(pl.multiple_of(i * tm, tm), tm)

        @pl.when(g == 0)
        def _():
            xf = x_ref[...]
            ms = jnp.mean(xf * xf, axis=-1, keepdims=True)
            hn = xf * lax.rsqrt(ms + NORM_EPS) * w_ref[...]
            hnt_ref[...] = hn.T.astype(BF16)
            hn_s[rows, :] = hn.astype(BF16)

        group = order_ref[g]

        def sorted_copy(tile_value):
            per = tm // SORT_RESIDUES
            cols = pl.ds(pl.multiple_of(group * COL_BLOCK, COL_BLOCK), COL_BLOCK)
            buf = i % 2

            def out_copies(tile, b):
                return [pltpu.make_async_copy(
                    sort_stage.at[b, :, r, :], qkv_hbm.at[r, pl.ds(pl.multiple_of(tile * per, per), per), cols],
                    sort_sems.at[b, r]) for r in range(SORT_RESIDUES)]

            @pl.when(i >= 2)
            def _():
                for copy in out_copies(i - 2, buf):
                    copy.wait()

            sort_stage[buf] = tile_value.reshape(per, SORT_RESIDUES, COL_BLOCK)
            for copy in out_copies(i, buf):
                copy.start()

            @pl.when(i == n_tiles - 1)
            def _():
                for copy in out_copies(i - 1, 1 - buf) + out_copies(i, buf):
                    copy.wait()

        @pl.when(group < 2)
        def _():
            rotated = _rot(_mm(hn_s[rows, :], w_land[group]), c_ref[...], sa_ref[...], sb_ref[...])
            proj_ref[...] = rotated
            sorted_copy(rotated)

        @pl.when(group == 2)
        def _():
            value = _mm(hn_s[rows, :], w_land[group])
            proj_ref[...] = value
            sorted_copy(value)

        @pl.when(group > 2)
        def _():
            proj_ref[...] = _mm(hn_s[rows, :], w_land[group])

        @pl.when((g == N_DEV - 1) & (i == n_tiles - 1))
        def _():
            pass_on(1, 0).start()
            pass_on(1, 1).start()
            two_hop_half(1, 0).wait_recv()
            two_hop_half(1, 1).wait_recv()
            pass_on(1, 2).start()
            for k in (0, 4, 5, 6):
                arrival(1, k).wait_recv()
            for which in (0, 1):
                for cp in (first_copies(which) + [relay(which, part) for part in range(2)]
                           + [pass_on(which, j) for j in range(3)]):
                    cp.wait_send()
            wout_copy = pltpu.make_async_copy(wout_land, gout_hbm, local_sems.at[N_DEV])
            wout_copy.start()
            for step in range(N_DEV):
                to_hbm(step).wait()
            wout_copy.wait()

    me = _my_place()
    x_, y_, c_ = me
    chips = [(1 - x_, y_), (x_, 1 - y_), (1 - x_, 1 - y_)]
    order = jnp.stack([_flat(p) for p in (
        me, (x_, y_, 1 - c_), (*chips[0], c_), (*chips[1], c_), (*chips[0], 1 - c_), (*chips[1], 1 - c_),
        (*chips[2], c_), (*chips[2], 1 - c_))]).astype(jnp.int32)

    first_sweep = lambda g, i, order: (jnp.where(g == 0, i, n_tiles - 1), 0)
    tab = pl.BlockSpec((tm, LANES), lambda g, i, order: (jnp.where(order[g] < 2, i, 0), 0))
    whole = lambda: pl.BlockSpec(memory_space=pltpu.VMEM)
    grid_spec = pltpu.PrefetchScalarGridSpec(
        num_scalar_prefetch=1, grid=(N_DEV, n_tiles),
        in_specs=[pl.BlockSpec((tm, D_MODEL), first_sweep),
                  pl.BlockSpec((1, D_MODEL), lambda g, i, order: (0, 0)),
                  whole(), whole(), tab, tab, tab],
        out_specs=(pl.BlockSpec((None, tm, COL_BLOCK), lambda g, i, order: (order[g], i, 0)),
                   pl.BlockSpec((D_MODEL, tm), lambda g, i, order: (0, jnp.where(g == 0, i, n_tiles - 1))),
                   pl.BlockSpec(memory_space=pl.ANY), pl.BlockSpec(memory_space=pl.ANY),
                   pl.BlockSpec(memory_space=pl.ANY)),
        scratch_shapes=[pltpu.VMEM((SEQ, D_MODEL), BF16),
                        pltpu.VMEM((N_DEV, D_MODEL, COL_BLOCK), BF16),
                        pltpu.VMEM((D_MODEL, D_MODEL), BF16),
                        pltpu.VMEM((D_MODEL, COL_BLOCK), BF16),
                        pltpu.VMEM((2, tm // SORT_RESIDUES, SORT_RESIDUES, COL_BLOCK), F32),
                        pltpu.SemaphoreType.DMA((16,)), pltpu.SemaphoreType.DMA((16,)),
                        pltpu.SemaphoreType.DMA((N_DEV + 1,)), pltpu.SemaphoreType.DMA((2, SORT_RESIDUES))])
    proj, hn_t, w_in_g, w_out_g, qkv_sorted = pl.pallas_call(
        body, name="gather_project", grid_spec=grid_spec,
        out_shape=(jax.ShapeDtypeStruct((N_DEV, SEQ, COL_BLOCK), F32), jax.ShapeDtypeStruct((D_MODEL, SEQ), BF16),
                   jax.ShapeDtypeStruct((D_MODEL, IN_COLS), BF16), jax.ShapeDtypeStruct((D_MODEL, D_MODEL), BF16),
                   jax.ShapeDtypeStruct((SORT_RESIDUES, SORT_ROWS, 3 * COL_BLOCK), F32)),
        compiler_params=_params(("arbitrary", "arbitrary")),
    )(order, x, mix_w, w_in, w_out, rc, rsa, rsb)
    return proj, hn_t, w_in_g, w_out_g, qkv_sorted.reshape(SEQ, 3 * COL_BLOCK)


SCORE_SCALE = HEAD_DIM ** -0.5
ATTN_GROUP_FWD = 32
ATTN_GROUP_BWD = 16
BLOCKS_PER_PATTERN = SEQ // ATTN_BLOCK
SORT_RESIDUES = 16
SORT_ROWS = SEQ // SORT_RESIDUES


def _write_band_bias(bias_ref):
    row = lax.broadcasted_iota(jnp.int32, (2 * ATTN_BLOCK, 2 * ATTN_BLOCK), 0) & (ATTN_BLOCK - 1)
    col = lax.broadcasted_iota(jnp.int32, (2 * ATTN_BLOCK, 2 * ATTN_BLOCK), 1)
    for pi, d in enumerate(DILATIONS):
        per = SORT_RESIDUES // d
        ahead = per * (row % (8 * d) - col % (16 * d)) + (row // (8 * d) - col // (16 * d))
        dist = ATTN_BLOCK + ahead
        bias_ref[2 * pi] = jnp.where((dist >= 0) & (dist <= ATTN_BLOCK), 0.0, NEG_BIG)
        bias_ref[2 * pi + 1] = jnp.where(ahead >= 0, 0.0, NEG_BIG)


def _head0_lanes():
    return lax.broadcasted_iota(jnp.int32, (ATTN_BLOCK, LANES), 1) < HEAD_DIM


def _stack_heads(t, h0):
    return jnp.concatenate([jnp.where(h0, t, 0.0), jnp.where(h0, 0.0, t)], axis=0).astype(BF16)


def _block_runs(i, d):
    nblk = BLOCKS_PER_PATTERN // d
    r, n = i // nblk, i % nblk
    kn = jnp.maximum(n - 1, 0)
    rows, keys = [], []
    for c in range(SORT_RESIDUES // d):
        base = SORT_ROWS * (c * d + r)
        rows.append(pl.ds(pl.multiple_of(base + 8 * d * n, 8), 8 * d))
        keys.append(pl.ds(pl.multiple_of(base + 8 * d * kn, 8), 16 * d))
    return rows, keys, (n == 0).astype(jnp.int32)


def _take(ref, runs):
    return jnp.concatenate([ref[run, :] for run in runs], axis=0)


def _put(ref, runs, value, add=False):
    at = 0
    for run in runs:
        piece = value[at:at + run.size]
        if add:
            ref[run, :] += piece
        else:
            ref[run, :] = piece
        at += run.size


def _sort_copies(src_hbm, lane_block, dst_ref, sem_ref):
    lanes = pl.ds(pl.multiple_of(LANES * lane_block, LANES), LANES)
    return [pltpu.make_async_copy(src_hbm.at[:, r, lanes], dst_ref.at[pl.ds(SORT_ROWS * r, SORT_ROWS), :],
                                  sem_ref.at[r]) for r in range(SORT_RESIDUES)]


def _unsort_copies(src_ref, dst_hbm, lane_block, sem_ref):
    lanes = pl.ds(pl.multiple_of(LANES * lane_block, LANES), LANES)
    return [pltpu.make_async_copy(src_ref.at[pl.ds(SORT_ROWS * r, SORT_ROWS), :], dst_hbm.at[:, r, lanes],
                                  sem_ref.at[r]) for r in range(SORT_RESIDUES)]


def _for_each_group(d, n_group, load, compute, store):
    def group(g, carry):
        items = [load(*_block_runs(g * n_group + u, d)) for u in range(n_group)]
        results = [compute(item) for item in items]
        for item, res in zip(items, results):
            store(item, res)
        return carry

    lax.fori_loop(0, BLOCKS_PER_PATTERN // n_group, group, 0)


def _attn_fwd_fused(qkv_sorted):
    n_pat = len(DILATIONS)
    tile2 = (2 * ATTN_BLOCK, LANES)

    def body(q_ref, k_ref, v_ref, o_hbm, lse_ref, o_slots, m_acc, l_acc, bias_ref, out_sem):
        step, n_steps = pl.program_id(0), pl.num_programs(0)
        pl.when(step == 0)(lambda: _write_band_bias(bias_ref))
        slot = step % 2
        o_acc = o_slots.at[slot]
        h0 = _head0_lanes()
        for pi, d in enumerate(DILATIONS):
            first, last = pi == 0, pi == n_pat - 1

            def load(rows, keys, which, first=first, pi=pi):
                item = dict(rows=rows, keys=keys, which=2 * pi + which)
                if not first:
                    item.update(o=_take(o_acc, rows), m=[_take(m_acc.at[h], rows) for h in range(2)],
                                l=[_take(l_acc.at[h], rows) for h in range(2)])
                return item

            def compute(item, first=first):
                kb = _take(k_ref, item["keys"]).astype(BF16)
                vb = _take(v_ref, item["keys"]).astype(BF16)
                s = _mm_nt(_stack_heads(_take(q_ref, item["rows"]) * SCORE_SCALE, h0), kb) + bias_ref[item["which"]]
                mb = jnp.max(s, axis=-1, keepdims=True)
                if first:
                    p = jnp.exp(s - mb)
                    mn = jnp.broadcast_to(mb, tile2)
                else:
                    m_old = jnp.concatenate(item["m"], axis=0)
                    mn = jnp.maximum(m_old, mb)
                    alpha = jnp.exp(m_old - mn)
                    p = jnp.exp(s - jnp.concatenate([mn, mn], axis=1))
                ls = jnp.sum(p, axis=-1, keepdims=True)
                pv = _mm(p.astype(BF16), vb)
                if first:
                    return pv, mn, jnp.broadcast_to(ls, tile2)
                o_old = jnp.concatenate([item["o"], item["o"]], axis=0)
                return alpha * o_old + pv, mn, alpha * jnp.concatenate(item["l"], axis=0) + ls

            def store(item, res, last=last):
                rows = item["rows"]
                (o0, o1), (m0, m1), (l0, l1) = ((a[:ATTN_BLOCK], a[ATTN_BLOCK:]) for a in res)
                if last:
                    _put(o_acc, rows, jnp.where(h0, o0 / l0, o1 / l1))
                    _put(lse_ref, rows, jnp.where(h0, m0 + jnp.log(l0), m1 + jnp.log(l1)))
                else:
                    _put(o_acc, rows, jnp.where(h0, o0, o1))
                    for h, (m, l) in enumerate(((m0, l0), (m1, l1))):
                        _put(m_acc.at[h], rows, m)
                        _put(l_acc.at[h], rows, l)

            _for_each_group(d, ATTN_GROUP_FWD, load, compute, store)

        def copies_out(of_step):
            return _unsort_copies(o_slots.at[of_step % 2], o_hbm, of_step, out_sem.at[of_step % 2])

        @pl.when(step > 0)
        def _():
            for copy in copies_out(step - 1):
                copy.wait()

        for copy in copies_out(step):
            copy.start()

        @pl.when(step == n_steps - 1)
        def _():
            for copy in copies_out(step):
                copy.wait()

    slab = lambda g: pl.BlockSpec((SEQ, LANES), functools.partial(lambda hp, g: (0, 4 * g + hp), g=g))
    wide = jax.ShapeDtypeStruct((SEQ, ATTN_WIDTH), F32)
    o_rows, lse = pl.pallas_call(
        body, name="attn_fwd", grid=(4,),
        out_shape=(jax.ShapeDtypeStruct((SORT_ROWS, SORT_RESIDUES, ATTN_WIDTH), F32), wide),
        in_specs=[slab(0), slab(1), slab(2)], out_specs=(pl.BlockSpec(memory_space=pl.ANY), slab(0)),
        scratch_shapes=[pltpu.VMEM((2, SEQ, LANES), F32), pltpu.VMEM((2, SEQ, LANES), F32),
                        pltpu.VMEM((2, SEQ, LANES), F32),
                        pltpu.VMEM((2 * len(DILATIONS), 2 * ATTN_BLOCK, 2 * ATTN_BLOCK), F32),
                        pltpu.SemaphoreType.DMA((2, SORT_RESIDUES))],
        compiler_params=_params(("arbitrary",)),
    )(qkv_sorted, qkv_sorted, qkv_sorted)
    return o_rows.reshape(SEQ, ATTN_WIDTH), lse


def _attn_bwd_fused(qkv_sorted, d_out, lse_sorted, delta):
    def body(q_ref, k_ref, v_ref, do_hbm, lse_ref, del_hbm, dq_hbm, dk_hbm, dv_hbm,
             in_slots, out_slots, bias_ref, in_sem, out_sem):
        step, n_steps = pl.program_id(0), pl.num_programs(0)
        slot = step % 2

        def copies_in(of_step):
            s = of_step % 2
            return [copy for j, hbm in enumerate((do_hbm, del_hbm))
                    for copy in _sort_copies(hbm, of_step, in_slots.at[s, j], in_sem.at[s, j])]

        def copies_out(of_step):
            s = of_step % 2
            return [copy for j, hbm in enumerate((dq_hbm, dk_hbm, dv_hbm))
                    for copy in _unsort_copies(out_slots.at[s, j], hbm, of_step, out_sem.at[s, j])]

        @pl.when(step == 0)
        def _():
            for copy in copies_in(step):
                copy.start()
            _write_band_bias(bias_ref)

        @pl.when(step + 1 < n_steps)
        def _():
            for copy in copies_in(step + 1):
                copy.start()

        do_s, del_s = in_slots.at[slot, 0], in_slots.at[slot, 1]
        dq_s, dk_s, dv_s = (out_slots.at[slot, j] for j in range(3))
        dk_s[...] = jnp.zeros_like(dk_s)
        dv_s[...] = jnp.zeros_like(dv_s)
        for copy in copies_in(step):
            copy.wait()
        h0 = _head0_lanes()
        for pi, d in enumerate(DILATIONS):
            first = pi == 0

            def load(rows, keys, which, pi=pi):
                return dict(rows=rows, keys=keys, q=_take(q_ref, rows), g=_take(do_s, rows),
                            lse=_take(lse_ref, rows), delta=_take(del_s, rows),
                            k=_take(k_ref, keys).astype(BF16), v=_take(v_ref, keys).astype(BF16),
                            bias=bias_ref[2 * pi + which])

            def per_head(t):
                swapped = pltpu.roll(t, HEAD_DIM, 1)
                both = jnp.concatenate([jnp.where(h0, t, swapped), jnp.where(h0, swapped, t)], axis=0)
                return jnp.concatenate([both, both], axis=1)

            def compute(item):
                q2, g2 = _stack_heads(item["q"] * SCORE_SCALE, h0), _stack_heads(item["g"], h0)
                s = _mm_nt(q2, item["k"]) + item["bias"]
                p = jnp.exp(s - per_head(item["lse"]))
                dp = _mm_nt(g2, item["v"])
                ds = (p * (dp - per_head(item["delta"]))).astype(BF16)
                dq2 = _mm(ds, item["k"])
                dq = jnp.where(h0, dq2[:ATTN_BLOCK], dq2[ATTN_BLOCK:]) * SCORE_SCALE
                return dq, _mm_tn(ds, q2), _mm_tn(p.astype(BF16), g2)

            def store(item, res, first=first):
                _put(dq_s, item["rows"], res[0], add=not first)
                _put(dk_s, item["keys"], res[1], add=True)
                _put(dv_s, item["keys"], res[2], add=True)

            _for_each_group(d, ATTN_GROUP_BWD, load, compute, store)

        @pl.when(step > 0)
        def _():
            for copy in copies_out(step - 1):
                copy.wait()

        for copy in copies_out(step):
            copy.start()

        @pl.when(step == n_steps - 1)
        def _():
            for copy in copies_out(step):
                copy.wait()

    slab = lambda g: pl.BlockSpec((SEQ, LANES), functools.partial(lambda hp, g: (0, 4 * g + hp), g=g))
    anywhere = pl.BlockSpec(memory_space=pl.ANY)
    by_residue = (SORT_ROWS, SORT_RESIDUES, ATTN_WIDTH)
    grads = pl.pallas_call(
        body, name="attn_bwd", grid=(4,), out_shape=(jax.ShapeDtypeStruct(by_residue, F32),) * 3,
        scratch_shapes=[pltpu.VMEM((2, 2, SEQ, LANES), F32), pltpu.VMEM((2, 3, SEQ, LANES), F32),
                        pltpu.VMEM((2 * len(DILATIONS), 2 * ATTN_BLOCK, 2 * ATTN_BLOCK), F32),
                        pltpu.SemaphoreType.DMA((2, 2, SORT_RESIDUES)), pltpu.SemaphoreType.DMA((2, 3, SORT_RESIDUES))],
        in_specs=[slab(0), slab(1), slab(2), anywhere, slab(0), anywhere], out_specs=(anywhere,) * 3,
        compiler_params=_params(("arbitrary",)),
    )(qkv_sorted, qkv_sorted, qkv_sorted, d_out.reshape(by_residue), lse_sorted, delta.reshape(by_residue))
    return tuple(g.reshape(SEQ, ATTN_WIDTH) for g in grads)


def _hgrn_lower_bound(lb_ref):
    r0, r1 = lb_ref[0:1, :], lb_ref[1:2, :]
    mx = jnp.maximum(r0, r1)
    e0, e1 = jnp.exp(r0 - mx), jnp.exp(r1 - mx)
    return e0 / (e0 + e1)


def _hgrn_gates(hq, hf, lb):
    sq = _sigmoid(hq)
    sg = _sigmoid(hf)
    f = lb + (1.0 - lb) * sg
    return hq * sq, sq, sg, f, 1.0 - f, jnp.log(f)


HGRN_PAIR = 4
HGRN_SEQ_BLOCK = 1024
HGRN_GROUP = 4
HGRN_ROWS = HGRN_GROUP * HGRN_CHUNK


def _hgrn_specs(reverse):
    n_blocks = SEQ // HGRN_SEQ_BLOCK
    width = HGRN_PAIR * HGRN_DIM
    blk = (lambda s: n_blocks - 1 - s) if reverse else (lambda s: s)
    cols = lambda g: pl.BlockSpec((None, HGRN_SEQ_BLOCK, width), functools.partial(lambda p, s, g: (g, blk(s), p), g=g))
    pair = pl.BlockSpec((HGRN_SEQ_BLOCK, width), lambda p, s: (blk(s), p))
    lb = pl.BlockSpec((2, width), lambda p, s: (0, p))
    states = pl.BlockSpec((HGRN_PAIR, HGRN_SEQ_BLOCK // HGRN_CHUNK, HGRN_DIM, HGRN_DIM),
                          lambda p, s: (p, blk(s), 0, 0))
    return cols, pair, lb, states


def _chunk_masks():
    ri = lax.broadcasted_iota(jnp.int32, (HGRN_ROWS, HGRN_ROWS), 0)
    ci = lax.broadcasted_iota(jnp.int32, (HGRN_ROWS, HGRN_ROWS), 1)
    same = (ri // HGRN_CHUNK) == (ci // HGRN_CHUNK)
    return same, same & (ri >= ci), same & (ri <= ci)


def _mm_select(sel, v):
    hi = v.astype(BF16)
    r1 = v - hi.astype(F32)
    mid = r1.astype(BF16)
    lo = (r1 - mid.astype(F32)).astype(BF16)
    return _mm(sel, hi) + _mm(sel, mid) + _mm(sel, lo)


def _head_cols(a, h):
    return a[:, HGRN_DIM * h:HGRN_DIM * (h + 1)]


def _hgrn_fwd(proj, lb_raw):
    t, rws = HGRN_CHUNK, HGRN_ROWS

    def body(hq_ref, hf_ref, hi_ref, lb_ref, rec_ref, st_ref, state):
        @pl.when(pl.program_id(1) == 0)
        def _():
            state[...] = jnp.zeros_like(state)

        lb = _hgrn_lower_bound(lb_ref)
        same, causal, _ = _chunk_masks()
        sel = jnp.concatenate([causal, same], axis=0).astype(BF16)

        def group(g, sts):
            rows = pl.ds(pl.multiple_of(g * rws, rws), rws)
            q, _, _, _, k, lf = _hgrn_gates(hq_ref[rows, :], hf_ref[rows, :], lb)
            sums = _mm_select(sel, lf)
            cum, last = sums[:rws], sums[rws:]
            qd = (q * jnp.exp(cum)).astype(BF16)
            ki = (k * jnp.exp(-cum)).astype(BF16)
            ke = (k * jnp.exp(last - cum)).astype(BF16)
            vb = hi_ref[rows, :].astype(BF16)
            dec = jnp.exp(last)
            new_sts, recs = [], []
            for h in range(HGRN_PAIR):
                qd_h, ke_h, vb_h = _head_cols(qd, h), _head_cols(ke, h), _head_cols(vb, h)
                att = jnp.where(causal, _mm_nt(qd_h, _head_cols(ki, h)), 0.0).astype(BF16)
                intra = _mm(att, vb_h)
                st = sts[h]
                outs = []
                for c in range(HGRN_GROUP):
                    sl = slice(c * t, (c + 1) * t)
                    st_ref[h, g * HGRN_GROUP + c] = st
                    outs.append(intra[sl] + _mm_nt(qd_h[sl], st.astype(BF16)))
                    st = st * _head_cols(dec[c * t:c * t + 1, :], h) + _mm_tn(vb_h[sl], ke_h[sl])
                new_sts.append(st)
                recs.append(jnp.concatenate(outs, axis=0))
            rec_ref[rows, :] = jnp.concatenate(recs, axis=1)
            return tuple(new_sts)

        sts = lax.fori_loop(0, HGRN_SEQ_BLOCK // rws, group, tuple(state[h] for h in range(HGRN_PAIR)), unroll=True)
        for h in range(HGRN_PAIR):
            state[h] = sts[h]

    cols, pair, lb, states = _hgrn_specs(reverse=False)
    return pl.pallas_call(
        body, name="hgrn_fwd", grid=(HGRN_HEADS // HGRN_PAIR, SEQ // HGRN_SEQ_BLOCK),
        out_shape=(jax.ShapeDtypeStruct((SEQ, HGRN_WIDTH), F32),
                   jax.ShapeDtypeStruct((HGRN_HEADS, N_CHUNKS, HGRN_DIM, HGRN_DIM), F32)),
        in_specs=[cols(4), cols(5), cols(6), lb], out_specs=(pair, states),
        scratch_shapes=[pltpu.VMEM((HGRN_PAIR, HGRN_DIM, HGRN_DIM), F32)],
        compiler_params=_params(("parallel", "arbitrary")),
    )(proj, proj, proj, lb_raw)


def _hgrn_bwd(proj, lb_raw, d_rec, states):
    t, rws = HGRN_CHUNK, HGRN_ROWS

    def body(hq_ref, hf_ref, hi_ref, lb_ref, do_ref, st_ref, dhq_ref, dhf_ref, dhi_ref, dlb_ref,
             dstate, dlb_acc):
        lb = _hgrn_lower_bound(lb_ref)
        same, causal, anti = _chunk_masks()
        sel = jnp.concatenate([causal, same], axis=0).astype(BF16)
        sel_t = jnp.concatenate([anti, same], axis=1).astype(BF16)
        @pl.when(pl.program_id(1) == 0)
        def _():
            dstate[...] = jnp.zeros_like(dstate)
            dlb_acc[...] = jnp.zeros_like(dlb_acc)

        n_groups = HGRN_SEQ_BLOCK // rws
        chunks = [slice(c * t, (c + 1) * t) for c in range(HGRN_GROUP)]

        def group(i, dsts_in):
            g = n_groups - 1 - i
            rows = pl.ds(pl.multiple_of(g * rws, rws), rws)
            hq = hq_ref[rows, :]
            q, sq, sg, f, k, lf = _hgrn_gates(hq, hf_ref[rows, :], lb)
            sums = _mm_select(sel, lf)
            cum, last = sums[:rws], sums[rws:]
            e_cum, e_inv, e_end, dec = jnp.exp(cum), jnp.exp(-cum), jnp.exp(last - cum), jnp.exp(last)
            qd, ki, ke = q * e_cum, k * e_inv, k * e_end
            qdb, kib, keb = qd.astype(BF16), ki.astype(BF16), ke.astype(BF16)
            vb = hi_ref[rows, :].astype(BF16)
            gb = do_ref[rows, :].astype(BF16)

            dsts_out, per_head = [], []
            for h in range(HGRN_PAIR):
                qdb_h, kib_h, keb_h = _head_cols(qdb, h), _head_cols(kib, h), _head_cols(keb, h)
                vb_h, gb_h = _head_cols(vb, h), _head_cols(gb, h)
                att = jnp.where(causal, _mm_nt(qdb_h, kib_h), 0.0).astype(BF16)
                datt = jnp.where(causal, _mm_nt(gb_h, vb_h), 0.0).astype(BF16)
                dv = _mm_tn(att, gb_h)
                dqd = _mm(datt, kib_h)
                dki = _mm_tn(datt, qdb_h)

                decs = [_head_cols(dec[c * t:c * t + 1, :], h) for c in range(HGRN_GROUP)]
                dsts = [None] * HGRN_GROUP
                dst = dsts_in[h]
                for c in reversed(range(HGRN_GROUP)):
                    dsts[c] = dst
                    dst = dst * decs[c] + _mm_tn(gb_h[chunks[c]], qdb_h[chunks[c]])
                dsts_out.append(dst)

                dv_x, dqd_x, dke, dlast_x = [], [], [], []
                for c, sl in enumerate(chunks):
                    st_prev = st_ref[h, g * HGRN_GROUP + c]
                    dstb = dsts[c].astype(BF16)
                    dv_x.append(_mm_nt(keb_h[sl], dstb))
                    dqd_x.append(_mm(gb_h[sl], st_prev.astype(BF16)))
                    dke.append(_mm(vb_h[sl], dstb))
                    ddec = jnp.sum(dsts[c] * st_prev, axis=0, keepdims=True)
                    dlast_x.append(jnp.broadcast_to(ddec * decs[c], (t, HGRN_DIM)))
                per_head.append((dv + jnp.concatenate(dv_x, axis=0), dqd + jnp.concatenate(dqd_x, axis=0),
                                 dki, jnp.concatenate(dke, axis=0), jnp.concatenate(dlast_x, axis=0)))
            dv, dqd, dki, dke, dlast = (jnp.concatenate(list(parts), axis=1) for parts in zip(*per_head))

            dq = dqd * e_cum
            dk = dki * e_inv + dke * e_end
            dke_ke = dke * ke
            dcum = dqd * qd - dki * ki - dke_ke
            dlf = _mm_select(sel_t, jnp.concatenate([dcum, dke_ke], axis=0)) + dlast
            df = dlf / f - dk
            dhq_ref[rows, :] = (dq * (sq * (1.0 + hq * (1.0 - sq)))).astype(BF16)
            dhf_ref[rows, :] = (df * (1.0 - lb) * (sg * (1.0 - sg))).astype(BF16)
            dhi_ref[rows, :] = dv.astype(BF16)
            dlb_acc[...] += jnp.sum(df * (1.0 - sg), axis=0, keepdims=True)
            return tuple(dsts_out)

        dsts = lax.fori_loop(0, n_groups, group, tuple(dstate[h] for h in range(HGRN_PAIR)), unroll=True)
        for h in range(HGRN_PAIR):
            dstate[h] = dsts[h]
        g0 = dlb_acc[...] * lb * (1.0 - lb)
        dlb_ref[...] = jnp.concatenate([g0, -g0], axis=0)

    cols, pair, lb_spec, st_spec = _hgrn_specs(reverse=True)
    wide = jax.ShapeDtypeStruct((SEQ, HGRN_WIDTH), BF16)
    return pl.pallas_call(
        body, name="hgrn_bwd", grid=(HGRN_HEADS // HGRN_PAIR, SEQ // HGRN_SEQ_BLOCK),
        out_shape=(wide, wide, wide, jax.ShapeDtypeStruct((2, HGRN_WIDTH), F32)),
        in_specs=[cols(4), cols(5), cols(6), lb_spec, pair, st_spec],
        out_specs=(pair, pair, pair, lb_spec),
        scratch_shapes=[pltpu.VMEM((HGRN_PAIR, HGRN_DIM, HGRN_DIM), F32),
                        pltpu.VMEM((1, HGRN_PAIR * HGRN_DIM), F32)],
        compiler_params=_params(("parallel", "arbitrary")),
    )(proj, proj, proj, lb_raw, d_rec, states)


def _group_sum(v, group):
    parts = []
    for s in range(v.shape[1] // LANES):
        slab = v[:, LANES * s:LANES * (s + 1)]
        if group == LANES:
            parts.append(jnp.broadcast_to(jnp.sum(slab, axis=-1, keepdims=True), slab.shape))
        else:
            h0 = lax.broadcasted_iota(jnp.int32, slab.shape, 1) < HEAD_DIM
            s0 = jnp.sum(jnp.where(h0, slab, 0.0), axis=-1, keepdims=True)
            s1 = jnp.sum(jnp.where(h0, 0.0, slab), axis=-1, keepdims=True)
            parts.append(jnp.where(h0, s0, s1))
    return jnp.concatenate(parts, axis=1)


def _mid(attn_o, rec, proj, x, target, w_out_g, attn_w, hgrn_w, final_w):
    tm = 256

    def branch_fwd(o, gate, w, group):
        r = lax.rsqrt(_group_sum(o * o, group) * (1.0 / group) + NORM_EPS)
        nrm = o * r
        sg = _sigmoid(gate)
        return r, nrm, sg, nrm * w * (gate * sg)

    def branch_bwd(dy, r, nrm, sg, gate, w, group):
        silu = gate * sg
        d_gate = dy * nrm * w * (sg * (1.0 + gate * (1.0 - sg)))
        d_w = jnp.sum(dy * nrm * silu, axis=0, keepdims=True)
        dn = dy * w * silu
        d_o = r * (dn - nrm * (_group_sum(dn * nrm, group) * (1.0 / group)))
        return d_o, d_gate, d_w

    def body(o_ref, rec_ref, ag_ref, hg_ref, x_ref, tgt_ref, wout_ref, aw_ref, hw_ref, fw_ref,
             dx2_ref, do_ref, delta_ref, dag_ref, drec_ref, dhg_ref, dwout_ref, dfw_ref, daw_ref, dhw_ref,
             loss_ref, dwout_acc):
        i = pl.program_id(0)

        @pl.when(i == 0)
        def _():
            dwout_acc[...] = jnp.zeros_like(dwout_acc)
            dfw_ref[...] = jnp.zeros_like(dfw_ref)
            daw_ref[...] = jnp.zeros_like(daw_ref)
            dhw_ref[...] = jnp.zeros_like(dhw_ref)
            loss_ref[...] = jnp.zeros_like(loss_ref)

        o, rc, ag, hg = o_ref[...], rec_ref[...], ag_ref[...], hg_ref[...]
        aw, hw, fw = aw_ref[...], hw_ref[...], fw_ref[...]
        ra, na, sga, ya = branch_fwd(o, ag, aw, HEAD_DIM)
        rh, nh, sgh, yh = branch_fwd(rc, hg, hw, HGRN_DIM)
        mixed = jnp.concatenate([ya, yh], axis=1).astype(BF16)
        wout = wout_ref[...]
        x2 = x_ref[...] + _mm(mixed, wout)
        rstd = lax.rsqrt(jnp.mean(x2 * x2, axis=-1, keepdims=True) + NORM_EPS)
        xn = x2 * rstd
        err = xn * fw - tgt_ref[...]
        row_loss = jnp.mean(err * err, axis=-1, keepdims=True)
        loss_ref[...] += 0.5 * jnp.sum(row_loss, axis=0, keepdims=True)
        dy = err * (1.0 / D_MODEL)
        dfw_ref[...] += jnp.sum(dy * xn, axis=0, keepdims=True)
        dxn = dy * fw
        dx2 = rstd * (dxn - xn * jnp.mean(dxn * xn, axis=-1, keepdims=True))
        dx2_ref[...] = dx2
        dx2b = dx2.astype(BF16)
        dwout_acc[...] += _mm_tn(mixed, dx2b)

        @pl.when(i == pl.num_programs(0) - 1)
        def _():
            dwout_ref[...] = dwout_acc[...].astype(BF16)

        dmixed = _mm_nt(dx2b, wout)

        d_o, d_ag, d_aw = branch_bwd(dmixed[:, :ATTN_WIDTH], ra, na, sga, ag, aw, HEAD_DIM)
        d_rec, d_hg, d_hw = branch_bwd(dmixed[:, ATTN_WIDTH:], rh, nh, sgh, hg, hw, HGRN_DIM)
        do_ref[...] = d_o
        delta_ref[...] = _group_sum(d_o * o, HEAD_DIM)
        dag_ref[...] = d_ag.astype(BF16)
        drec_ref[...] = d_rec
        dhg_ref[...] = d_hg.astype(BF16)
        daw_ref[...] += d_aw
        dhw_ref[...] += d_hw

    half = lambda: pl.BlockSpec((tm, COL_BLOCK), lambda i: (i, 0))
    full = lambda: pl.BlockSpec((tm, D_MODEL), lambda i: (i, 0))
    fixed = lambda r, c: pl.BlockSpec((r, c), lambda i: (0, 0))
    wide = jax.ShapeDtypeStruct((SEQ, COL_BLOCK), F32)
    wide_b = jax.ShapeDtypeStruct((SEQ, COL_BLOCK), BF16)
    return pl.pallas_call(
        body, name="mid", grid=(SEQ // tm,),
        out_shape=(jax.ShapeDtypeStruct((SEQ, D_MODEL), F32), wide, wide, wide_b, wide, wide_b,
                   jax.ShapeDtypeStruct((D_MODEL, D_MODEL), BF16),
                   jax.ShapeDtypeStruct((1, D_MODEL), F32), jax.ShapeDtypeStruct((1, COL_BLOCK), F32),
                   jax.ShapeDtypeStruct((1, COL_BLOCK), F32), jax.ShapeDtypeStruct((1, 1), F32)),
        scratch_shapes=[pltpu.VMEM((D_MODEL, D_MODEL), F32)],
        in_specs=[half(), half(),
                  pl.BlockSpec((None, tm, COL_BLOCK), lambda i: (3, i, 0)),
                  pl.BlockSpec((None, tm, COL_BLOCK), lambda i: (7, i, 0)),
                  full(), full(), fixed(D_MODEL, D_MODEL), fixed(1, COL_BLOCK), fixed(1, COL_BLOCK),
                  fixed(1, D_MODEL)],
        out_specs=(full(), half(), half(), half(), half(), half(), fixed(D_MODEL, D_MODEL),
                   fixed(1, D_MODEL), fixed(1, COL_BLOCK), fixed(1, COL_BLOCK), fixed(1, 1)),
        compiler_params=_params(("arbitrary",)),
    )(attn_o, rec, proj, proj, x, target, w_out_g, attn_w, hgrn_w, final_w)


def _in_proj_bwd_rows(d_groups, w_g, x, dx2, mix_w, rc, rsa, rsb):
    tm = 256

    def body(*refs):
        dg_refs = refs[:N_DEV]
        wg_ref, x_ref, dx2_ref, w_ref, c_ref, sa_ref, sb_ref, gx_ref, dpb_ref, dmw_ref = refs[N_DEV:]

        @pl.when(pl.program_id(0) == 0)
        def _():
            dmw_ref[...] = jnp.zeros_like(dmw_ref)

        parts = []
        for j in range(N_DEV):
            dp = dg_refs[j][...]
            if j < 2:
                dp = _rot_transposed(dp, c_ref[...], sa_ref[...], sb_ref[...])
            parts.append(dp.astype(BF16))
        dpb = jnp.concatenate(parts, axis=1)
        for j in range(N_DEV):
            dpb_ref[j] = parts[j]
        g = _mm_nt(dpb, wg_ref[...])
        xf = x_ref[...]
        rstd = lax.rsqrt(jnp.mean(xf * xf, axis=-1, keepdims=True) + NORM_EPS)
        xn = xf * rstd
        dmw_ref[...] += jnp.sum(g * xn, axis=0, keepdims=True)
        gw = g * w_ref[...]
        gx_ref[...] = dx2_ref[...] + rstd * (gw - xn * jnp.mean(gw * xn, axis=-1, keepdims=True))

    tile = lambda cols: pl.BlockSpec((tm, cols), lambda i: (i, 0))
    fixed = lambda r, c: pl.BlockSpec((r, c), lambda i: (0, 0))
    return pl.pallas_call(
        body, name="in_proj_bwd_rows", grid=(SEQ // tm,),
        out_shape=(jax.ShapeDtypeStruct((SEQ, D_MODEL), F32), jax.ShapeDtypeStruct((N_DEV, SEQ, COL_BLOCK), BF16),
                   jax.ShapeDtypeStruct((1, D_MODEL), F32)),
        in_specs=[tile(COL_BLOCK) for _ in range(N_DEV)] + [
            pl.BlockSpec((D_MODEL, IN_COLS), lambda i: (0, 0), pipeline_mode=pl.Buffered(1)),
            tile(D_MODEL), tile(D_MODEL), fixed(1, D_MODEL), tile(LANES), tile(LANES), tile(LANES)],
        out_specs=(tile(D_MODEL), pl.BlockSpec((N_DEV, tm, COL_BLOCK), lambda i: (0, i, 0)), fixed(1, D_MODEL)),
        compiler_params=_params(("arbitrary",)),
    )(*d_groups, w_g, x, dx2, mix_w, rc, rsa, rsb)


def _weights_exchange(hn_t, dproj_b, dwout_p, small_p):
    n_chips = N_DEV // 2
    rb = 128
    S1_IN, S1_OUT, SMALL, S2_IN, S2_OUT, VIA_IN, VIA_OUT = 0, 4, 8, 15, 17, 19, 21
    rel_of_pair = (3, 1, 2, 0)
    two_hop = n_chips - 1
    half_in, half_out = COL_BLOCK // 2, D_MODEL // 2

    def body(order_ref, hnt_ref, dp_ref, dwout_ref, small_ref, gin_ref, gout_ref, gs_ref,
             part, s1_send, s1_in, s1_out, fwd_in, fwd_out, s2_in, s2_out, via_in, via_out, land_s,
             send_sems, recv_sems):
        t = pl.program_id(0)
        me = _my_place()
        x, y, c = me
        my_chip = 2 * x + y
        sibling = (x, y, 1 - c)

        def remote(slot, src, dst, to):
            return pltpu.make_async_remote_copy(src_ref=src, dst_ref=dst, send_sem=send_sems.at[slot],
                                                recv_sem=recv_sems.at[slot], device_id=to, device_id_type=MESH)

        def s1_in_copy(pair):
            return remote(S1_IN + pair, s1_send.at[pair], s1_in.at[pair], sibling)

        def s1_out_copy(pair):
            q = my_chip ^ rel_of_pair[pair]
            return remote(S1_OUT + pair, dwout_ref.at[q, 1 - c], s1_out.at[pair], sibling)

        def s2_copies(rel):
            peer = _peer(me, 2 * rel)
            return [remote(S2_IN + rel - 1, fwd_in.at[rel - 1], s2_in.at[rel - 1], peer),
                    remote(S2_OUT + rel - 1, fwd_out.at[rel - 1], s2_out.at[rel - 1], peer)]

        def via_copies(k):
            peer = _peer(me, 2 * (2 - k))
            return [remote(VIA_IN + k, fwd_in.at[two_hop - 1, :, pl.ds(k * half_in, half_in)], via_in.at[k], peer),
                    remote(VIA_OUT + k, fwd_out.at[two_hop - 1, :, pl.ds(k * half_out, half_out)], via_out.at[k],
                           peer)]

        def small_copy(rel):
            return remote(SMALL + rel - 1, small_ref, land_s.at[rel], _peer(me, rel))

        @pl.when(t == 0)
        def _():
            land_s[0] = small_ref[...]
            for pair in range(n_chips):
                s1_out_copy(pair).start()
            for rel in range(1, N_DEV):
                small_copy(rel).start()

        part[...] = _mm(hnt_ref[...], dp_ref[...])

        def rows_loop(n_rows, fn):
            def step(b, carry):
                fn(pl.ds(pl.multiple_of(b * rb, rb), rb))
                return carry
            lax.fori_loop(0, n_rows // rb, step, 0)

        for pair, rel in enumerate(rel_of_pair):
            @pl.when(t == 2 * pair)
            def _(pair=pair):
                s1_send[pair] = part[...].astype(BF16)
                s1_in_copy(pair).start()

            @pl.when(t == 2 * pair + 1)
            def _(pair=pair, rel=rel):
                q = my_chip ^ rel
                s1_in_copy(pair).wait_recv()
                s1_out_copy(pair).wait_recv()
                dst_in = fwd_in.at[rel - 1] if rel else gin_ref
                dst_out = fwd_out.at[rel - 1] if rel else gout_ref
                passes_on = rel in (1, 2)
                if passes_on:
                    for cp in via_copies(rel - 1):
                        cp.wait_recv()

                def with_half(val, via, rows, width):
                    if not passes_on:
                        return val
                    extra = via[rel - 1, rows, :].astype(F32)
                    halves = [val[:, :width], val[:, width:]]
                    halves[rel - 1] = halves[rel - 1] + extra
                    return jnp.concatenate(halves, axis=1)

                def add_in(rows):
                    val = part[rows, :] + s1_in[pair, rows, :].astype(F32)
                    dst_in[rows, :] = with_half(val, via_in, rows, half_in).astype(dst_in.dtype)

                def add_out(rows):
                    val = dwout_ref[q, c, rows, :].astype(F32) + s1_out[pair, rows, :].astype(F32)
                    dst_out[rows, :] = with_half(val, via_out, rows, half_out).astype(dst_out.dtype)

                rows_loop(D_MODEL, add_in)
                rows_loop(WOUT_ROWS, add_out)
                if rel == two_hop:
                    for k in range(2):
                        for cp in via_copies(k):
                            cp.start()
                elif rel:
                    for cp in s2_copies(rel):
                        cp.start()

        @pl.when(t == N_DEV - 1)
        def _():
            for rel in range(1, two_hop):
                for cp in s2_copies(rel):
                    cp.wait_recv()

            def total_in(rows):
                g = gin_ref[rows, :]
                for rel in range(1, two_hop):
                    g = g + s2_in[rel - 1, rows, :].astype(F32)
                gin_ref[rows, :] = g

            def total_out(rows):
                g = gout_ref[rows, :]
                for rel in range(1, two_hop):
                    g = g + s2_out[rel - 1, rows, :].astype(F32)
                gout_ref[rows, :] = g

            rows_loop(D_MODEL, total_in)
            rows_loop(WOUT_ROWS, total_out)

            for rel in range(1, N_DEV):
                small_copy(rel).wait_recv()
            my_flat = _flat(me)
            g = land_s[my_flat ^ 0]
            for dev in range(1, N_DEV):
                g = g + land_s[my_flat ^ dev]
            gs_ref[...] = g

            for pair in range(n_chips):
                s1_in_copy(pair).wait_send()
                s1_out_copy(pair).wait_send()
            for rel in range(1, two_hop):
                for cp in s2_copies(rel) + via_copies(rel - 1):
                    cp.wait_send()
            for rel in range(1, N_DEV):
                small_copy(rel).wait_send()

    place_x, place_y, place_c = _my_place()
    my_chip = 2 * place_x + place_y
    order = jnp.stack([2 * (my_chip ^ rel) + core for rel in rel_of_pair
                       for core in (1 - place_c, place_c)]).astype(jnp.int32)

    whole = lambda: pl.BlockSpec(memory_space=pltpu.VMEM)
    in_blocks = lambda n: pltpu.VMEM((n, D_MODEL, COL_BLOCK), BF16)
    out_blocks = lambda n: pltpu.VMEM((n, WOUT_ROWS, D_MODEL), BF16)
    grid_spec = pltpu.PrefetchScalarGridSpec(
        num_scalar_prefetch=1, grid=(N_DEV,),
        in_specs=[pl.BlockSpec((D_MODEL, SEQ), lambda t, order: (0, 0), pipeline_mode=pl.Buffered(1)),
                  pl.BlockSpec((None, SEQ, COL_BLOCK), lambda t, order: (order[t], 0, 0)), whole(), whole()],
        out_specs=(whole(), whole(), whole()),
        scratch_shapes=[pltpu.VMEM((D_MODEL, COL_BLOCK), F32), in_blocks(n_chips), in_blocks(n_chips),
                        out_blocks(n_chips), in_blocks(n_chips - 1), out_blocks(n_chips - 1),
                        in_blocks(n_chips - 2), out_blocks(n_chips - 2),
                        pltpu.VMEM((2, D_MODEL, half_in), BF16), pltpu.VMEM((2, WOUT_ROWS, half_out), BF16),
                        pltpu.VMEM((N_DEV, SMALL_ROWS, LANES), F32),
                        pltpu.SemaphoreType.DMA((23,)), pltpu.SemaphoreType.DMA((23,))])
    return pl.pallas_call(
        body, name="weights_exchange", grid_spec=grid_spec,
        out_shape=(jax.ShapeDtypeStruct((D_MODEL, COL_BLOCK), F32), jax.ShapeDtypeStruct((WOUT_ROWS, D_MODEL), F32),
                   jax.ShapeDtypeStruct((SMALL_ROWS, LANES), F32)),
        compiler_params=_params(("arbitrary",)),
    )(order, hn_t, dproj_b, dwout_p.reshape(n_chips, 2, WOUT_ROWS, D_MODEL), small_p)


def _adamw(w, g, m, v):
    m = ADAM_B1 * m + (1.0 - ADAM_B1) * g
    v = ADAM_B2 * v + (1.0 - ADAM_B2) * (g * g)
    m_hat = m / (1.0 - ADAM_B1 ** ADAM_STEP)
    v_hat = v / (1.0 - ADAM_B2 ** ADAM_STEP)
    delta = -ADAM_LR * (m_hat / (jnp.sqrt(v_hat) + ADAM_EPS) + ADAM_WD * w)
    return delta, m, v


def _adamw_update(grads, weights, m_old, v_old):
    rb = 256

    def body(*refs):
        g_refs, w_refs, m_refs, v_refs = refs[0:3], refs[3:6], refs[6:9], refs[9:12]
        d_refs, nm_refs, nv_refs = refs[12:15], refs[15:18], refs[18:21]
        for k in range(3):
            n_rows = g_refs[k].shape[0]
            step_rows = min(rb, n_rows)

            def step(b, carry, k=k, step_rows=step_rows):
                rows = pl.ds(pl.multiple_of(b * step_rows, 8), step_rows)
                delta, nm, nv = _adamw(w_refs[k][rows, :], g_refs[k][rows, :], m_refs[k][rows, :], v_refs[k][rows, :])
                d_refs[k][rows, :] = delta
                nm_refs[k][rows, :] = nm
                nv_refs[k][rows, :] = nv
                return carry

            lax.fori_loop(0, n_rows // step_rows, step, 0)

    shapes = tuple(jax.ShapeDtypeStruct(g.shape, F32) for g in grads)
    vm = lambda: pl.BlockSpec(memory_space=pltpu.VMEM)
    outs = pl.pallas_call(
        body, name="adamw_update", out_shape=shapes * 3,
        in_specs=[vm() for _ in range(12)], out_specs=tuple(vm() for _ in range(9)),
        compiler_params=_params(),
    )(*grads, *weights, *m_old, *v_old)
    return outs[0:3], outs[3:6], outs[6:9]


def _pack_small(mix, attn, hgrn, lb, final, loss=None):
    def rows8(a):
        a = a.reshape(-1, LANES)
        return jnp.pad(a, ((0, 8 - a.shape[0]), (0, 0)))
    last = jnp.zeros((8, LANES), F32) if loss is None else jnp.pad(loss.reshape(1, 1), ((0, 7), (0, LANES - 1)))
    return jnp.concatenate([rows8(mix), rows8(attn), rows8(hgrn), rows8(lb), rows8(final), last], axis=0)


def _unpack_small(slab):
    return (slab[ROW_MIX:ROW_MIX + 8].reshape(1, D_MODEL), slab[ROW_ATTN:ROW_ATTN + 4].reshape(1, ATTN_WIDTH),
            slab[ROW_HGRN:ROW_HGRN + 4].reshape(1, HGRN_WIDTH), slab[ROW_LB:ROW_LB + 8].reshape(2, HGRN_WIDTH),
            slab[ROW_FINAL:ROW_FINAL + 8].reshape(D_MODEL))


def _rope(pos_row):
    j = np.arange(ROPE_ROWS)
    inv = np.where(j < ROPE_HALF, ROPE_THETA ** (-(j % ROPE_HALF) * (2.0 / ROPE_DIMS)), 0.0)
    e = np.arange(LANES) % HEAD_DIM
    hit = (j[:, None] == (e % ROPE_HALF)[None, :]) & (j[:, None] < ROPE_HALF)
    sel = np.stack([hit & (e < ROPE_DIMS), hit & (e >= ROPE_HALF) & (e < ROPE_DIMS),
                    -1.0 * (hit & (e < ROPE_HALF))]).astype(np.float32)
    return _rope_tables(pos_row, jnp.asarray(inv.astype(np.float32).reshape(ROPE_ROWS, 1)),
                        jnp.asarray(sel, dtype=BF16))


def _local_step(x, proj, qkv_sorted, w_in_g, w_out_g, tables, mix_w, attn_w, hgrn_w, lb_raw, final_w, target):
    rc, rsa, rsb = tables
    attn_o, lse = _attn_fwd_fused(qkv_sorted)
    rec, states = _hgrn_fwd(proj, lb_raw)

    (dx2, d_o, delta, d_ag, d_rec, d_hg, dwout_p, d_final, d_attn_w, d_hgrn_w, loss) = _mid(
        attn_o, rec, proj, x, target, w_out_g, attn_w, hgrn_w, final_w.reshape(1, D_MODEL))

    dqkv = _attn_bwd_fused(qkv_sorted, d_o, lse, delta)
    d_hq, d_hf, d_hi, d_lb = _hgrn_bwd(proj, lb_raw, d_rec, states)

    grad_x, dproj_b, d_mix = _in_proj_bwd_rows(
        (dqkv[0], dqkv[1], dqkv[2], d_ag, d_hq, d_hf, d_hi, d_hg), w_in_g, x, dx2, mix_w, rc, rsa, rsb)
    small_p = _pack_small(d_mix, d_attn_w, d_hgrn_w, d_lb, d_final, loss)
    return grad_x, dproj_b, dwout_p, small_p


def kernel(x, positions, w_in, w_out, mix_norm_w, attn_out_norm_w, hgrn_out_norm_w, hgrn_lb_raw, final_norm_w, loss_target, m_w_in, m_w_out, m_mix_norm_w, m_attn_out_norm_w, m_hgrn_out_norm_w, m_hgrn_lb_raw, m_final_norm_w, v_w_in, v_w_out, v_mix_norm_w, v_attn_out_norm_w, v_hgrn_out_norm_w, v_hgrn_lb_raw, v_final_norm_w):
    tables = _rope(positions)
    proj, hn_t, w_in_g, w_out_g, qkv_sorted = _gather_project(x[0], mix_norm_w, w_in[0], w_out[0], *tables)
    grad_x, dproj_b, dwout_p, small_p = _local_step(
        x[0], proj, qkv_sorted, w_in_g, w_out_g, tables, mix_norm_w, attn_out_norm_w, hgrn_out_norm_w,
        hgrn_lb_raw, final_norm_w, loss_target[0])
    g_in, g_out, g_s = _weights_exchange(hn_t, dproj_b, dwout_p, small_p)

    w_s = _pack_small(mix_norm_w, attn_out_norm_w, hgrn_out_norm_w, hgrn_lb_raw, final_norm_w)
    m_s = _pack_small(m_mix_norm_w, m_attn_out_norm_w, m_hgrn_out_norm_w, m_hgrn_lb_raw, m_final_norm_w)
    v_s = _pack_small(v_mix_norm_w, v_attn_out_norm_w, v_hgrn_out_norm_w, v_hgrn_lb_raw, v_final_norm_w)
    (d_in, d_out, d_s), (nm_in, nm_out, nm_s), (nv_in, nv_out, nv_s) = _adamw_update(
        (g_in, g_out, g_s), (w_in[0], w_out[0], w_s), (m_w_in[0], m_w_out[0], m_s), (v_w_in[0], v_w_out[0], v_s))

    loss = g_s[ROW_LOSS, 0]
    return (loss, grad_x[None], g_in[None], g_out[None], *_unpack_small(g_s),
            d_in[None], d_out[None], *_unpack_small(d_s),
            nm_in[None], nm_out[None], *_unpack_small(nm_s),
            nv_in[None], nv_out[None], *_unpack_small(nv_s))
```

```python
import functools

import jax
import jax.numpy as jnp
import numpy as np
from jax import lax
from jax.experimental import pallas as pl
from jax.experimental.pallas import tpu as pltpu

F32 = jnp.float32
BF16 = jnp.bfloat16

SEQ = 4096
D_MODEL = 1024
ATTN_WIDTH = 512
HGRN_WIDTH = 512
HEAD_DIM = 64
HGRN_HEADS = 4
HGRN_DIM = 128
HGRN_CHUNK = 64
N_CHUNKS = SEQ // HGRN_CHUNK
IN_COLS = 4096
COL_BLOCK = 512
N_DEV = 8
WOUT_ROWS = D_MODEL // N_DEV
ATTN_BLOCK = 128
DILATIONS = (1, 4, 16)
ROPE_THETA = 500000.0
ROPE_DIMS = 16
ROPE_HALF = 8
NORM_EPS = 1e-6
NEG_BIG = -1e30
LANES = 128

ADAM_LR = 0.001
ADAM_B1 = 0.9
ADAM_B2 = 0.999
ADAM_EPS = 1e-08
ADAM_WD = 0.01
ADAM_STEP = 10

SMALL_ROWS = 48
ROW_MIX, ROW_ATTN, ROW_HGRN, ROW_LB, ROW_FINAL, ROW_LOSS = 0, 8, 16, 24, 32, 40

VMEM_LIMIT = 56 * 1024 * 1024
MESH = pl.DeviceIdType.MESH


def _mm(a, b):
    return lax.dot_general(a, b, (((1,), (0,)), ((), ())), preferred_element_type=F32)


def _mm_nt(a, b):
    return lax.dot_general(a, b, (((1,), (1,)), ((), ())), preferred_element_type=F32)


def _mm_tn(a, b):
    return lax.dot_general(a, b, (((0,), (0,)), ((), ())), preferred_element_type=F32)


def _mm_exact(a, b):
    return lax.dot_general(a, b, (((1,), (0,)), ((), ())), preferred_element_type=F32,
                           precision=lax.Precision.HIGHEST)


def _sigmoid(v):
    return 1.0 / (1.0 + jnp.exp(-v))


def _params(sem=None, **kw):
    return pltpu.CompilerParams(dimension_semantics=sem, vmem_limit_bytes=VMEM_LIMIT, **kw)


def _my_place():
    return lax.axis_index("x"), lax.axis_index("y"), lax.axis_index("c")


def _peer(place, rel):
    x, y, c = place
    return (x ^ ((rel >> 2) & 1), y ^ ((rel >> 1) & 1), c ^ (rel & 1))


def _flat(place):
    x, y, c = place
    return 4 * x + 2 * y + c


ROPE_ROWS = 16


def _rope_tables(pos_row, inv_freq_col, selectors):
    def body(pos_ref, invf_ref, sel_ref, c_ref, sa_ref, sb_ref):
        ang = pos_ref[...].astype(F32) * invf_ref[...]
        cos, sin = jnp.cos(ang), jnp.sin(ang)

        def spread(v, sel):
            hi = v.astype(BF16)
            r1 = v - hi.astype(F32)
            mid = r1.astype(BF16)
            lo = (r1 - mid.astype(F32)).astype(BF16)
            return _mm_tn(hi, sel) + _mm_tn(mid, sel) + _mm_tn(lo, sel)

        e = lax.broadcasted_iota(jnp.int32, (1, LANES), 1) & (HEAD_DIM - 1)
        c_ref[...] = spread(cos, sel_ref[0]) + jnp.where(e < ROPE_DIMS, 0.0, 1.0)
        sa_ref[...] = spread(sin, sel_ref[1])
        sb_ref[...] = spread(sin, sel_ref[2])

    tab = jax.ShapeDtypeStruct((SEQ, LANES), F32)
    vm = lambda: pl.BlockSpec(memory_space=pltpu.VMEM)
    return pl.pallas_call(
        body, name="rope_tables", out_shape=(tab, tab, tab),
        in_specs=[vm(), vm(), vm()], out_specs=(vm(), vm(), vm()), compiler_params=_params(),
    )(pos_row, inv_freq_col, selectors)


def _per_slab(fn, t):
    return jnp.concatenate([fn(t[:, LANES * s:LANES * (s + 1)]) for s in range(t.shape[1] // LANES)], axis=1)


def _rot(t, c, sa, sb):
    return _per_slab(lambda u: u * c + pltpu.roll(u, ROPE_HALF, 1) * sa + pltpu.roll(u, LANES - ROPE_HALF, 1) * sb, t)


def _rot_transposed(g, c, sa, sb):
    return _per_slab(
        lambda u: u * c + pltpu.roll(u * sa, LANES - ROPE_HALF, 1) + pltpu.roll(u * sb, ROPE_HALF, 1), g)


def _gather_project(x, mix_w, w_in, w_out, rc, rsa, rsb):
    tm = 1024
    n_tiles = SEQ // tm
    arrival_of_step = (None, 0, 1, 5, 2, 4, 3, 6)

    def other_chips(place):
        x_, y_, c_ = place
        first_x = c_ == 0
        return [(jnp.where(first_x, 1 - x_, x_), jnp.where(first_x, y_, 1 - y_)),
                (jnp.where(first_x, x_, 1 - x_), jnp.where(first_x, 1 - y_, y_)), (1 - x_, 1 - y_)]

    def body(order_ref, x_ref, w_ref, win_ref, wout_ref, c_ref, sa_ref, sb_ref,
             proj_ref, hnt_ref, gin_hbm, gout_hbm, qkv_hbm,
             hn_s, w_land, wout_land, stage, sort_stage, send_sems, recv_sems, local_sems, sort_sems):
        g, i = pl.program_id(0), pl.program_id(1)
        me = _my_place()
        x_, y_, c_ = me
        sibling = (x_, y_, 1 - c_)
        chips = other_chips(me)

        def slab(which, place):
            idx = _flat(place)
            if which == 0:
                return w_land.at[idx]
            return wout_land.at[pl.ds(pl.multiple_of(idx * WOUT_ROWS, WOUT_ROWS), WOUT_ROWS), :]

        def remote(which, k, ref, to, src=None):
            return pltpu.make_async_remote_copy(
                src_ref=ref if src is None else src, dst_ref=ref, send_sem=send_sems.at[8 * which + k],
                recv_sem=recv_sems.at[8 * which + k], device_id=to, device_id_type=MESH)

        def copy(which, k, block, to, src=None):
            return remote(which, k, slab(which, block), to, src)

        def half(which, place, part):
            n = (D_MODEL if which == 0 else WOUT_ROWS) // 2
            if which == 0:
                return w_land.at[_flat(place), pl.ds(n * part, n), :]
            return wout_land.at[pl.ds(pl.multiple_of(_flat(place) * WOUT_ROWS + n * part, n), n), :]

        def first_copies(which):
            src = stage if which == 0 else None
            return ([copy(which, 0, me, sibling, src)]
                    + [copy(which, 1 + j, me, (*chips[j], c_), src) for j in range(2)])

        def relay(which, part):
            frm, to = (chips[1], chips[0]) if part == 0 else (chips[0], chips[1])
            return remote(which, 3 if part == 0 else 7, half(which, (*frm, c_), part), (*to, c_))

        def two_hop_half(which, part):
            return remote(which, 3 if part == 0 else 7, half(which, (*chips[2], c_), part), me)

        def pass_on(which, j):
            slot = (5, 4, 6)[j]
            return copy(which, slot, (*chips[j], c_), sibling)

        def arrival(which, k):
            if k == 0:
                return copy(which, 0, sibling, me)
            if k <= 2:
                return copy(which, k, (*chips[k - 1], c_), me)
            return copy(which, k, (*chips[k - 4], 1 - c_), me)

        def to_hbm(step):
            idx = order_ref[step]
            cols = pl.ds(pl.multiple_of(idx * COL_BLOCK, COL_BLOCK), COL_BLOCK)
            return pltpu.make_async_copy(w_land.at[idx], gin_hbm.at[:, cols], local_sems.at[step])

        @pl.when((g == 0) & (i == 0))
        def _():
            stage[...] = win_ref[...].astype(BF16)
            w_land[_flat(me)] = stage[...]
            wout_land[pl.ds(pl.multiple_of(_flat(me) * WOUT_ROWS, WOUT_ROWS), WOUT_ROWS), :] = (
                wout_ref[...].astype(BF16))
            for cp in first_copies(0)[:2] + first_copies(1)[:1]:
                cp.start()
            to_hbm(0).start()

        for step, k in enumerate(arrival_of_step):
            if k is None:
                continue

            @pl.when((g == step) & (i == 0))
            def _(k=k, step=step):
                if k == 3:
                    two_hop_half(0, 0).wait_recv()
                    two_hop_half(0, 1).wait_recv()
                else:
                    arrival(0, k).wait_recv()
                to_hbm(step).start()
                if 1 <= k <= 3:
                    pass_on(0, k - 1).start()
                if k == 1:
                    for cp in first_copies(0)[2:] + [relay(0, 1)] + first_copies(1)[1:]:
                        cp.start()
                if k == 2:
                    relay(0, 0).start()
                if k in (4, 5):
                    arrival(1, k - 3).wait_recv()
                    relay(1, 5 - k).start()

        rows = pl.ds(pl.multiple_of(i * tm, tm), tm)

        @pl.when(g == 0)
        def _():
            xf = x_ref[...]
            ms = jnp.mean(xf * xf, axis=-1, keepdims=True)
            hn = xf * lax.rsqrt(ms + NORM_EPS) * w_ref[...]
            hnt_ref[...] = hn.T.astype(BF16)
            hn_s[rows, :] = hn.astype(BF16)

        group = order_ref[g]

        def sorted_copy(tile_value):
            per = tm // SORT_RESIDUES
            cols = pl.ds(pl.multiple_of(group * COL_BLOCK, COL_BLOCK), COL_BLOCK)
            buf = i % 2

            def out_copies(tile, b):
                return [pltpu.make_async_copy(
                    sort_stage.at[b, :, r, :], qkv_hbm.at[r, pl.ds(pl.multiple_of(tile * per, per), per), cols],
                    sort_sems.at[b, r]) for r in range(SORT_RESIDUES)]

            @pl.when(i >= 2)
            def _():
                for copy in out_copies(i - 2, buf):
                    copy.wait()

            sort_stage[buf] = tile_value.reshape(per, SORT_RESIDUES, COL_BLOCK)
            for copy in out_copies(i, buf):
                copy.start()

            @pl.when(i == n_tiles - 1)
            def _():
                for copy in out_copies(i - 1, 1 - buf) + out_copies(i, buf):
                    copy.wait()

        @pl.when(group < 2)
        def _():
            rotated = _rot(_mm(hn_s[rows, :], w_land[group]), c_ref[...], sa_ref[...], sb_ref[...])
            proj_ref[...] = rotated
            sorted_copy(rotated)

        @pl.when(group == 2)
        def _():
            value = _mm(hn_s[rows, :], w_land[group])
            proj_ref[...] = value
            sorted_copy(value)

        @pl.when(group > 2)
        def _():
            proj_ref[...] = _mm(hn_s[rows, :], w_land[group])

        @pl.when((g == N_DEV - 1) & (i == n_tiles - 1))
        def _():
            pass_on(1, 0).start()
            pass_on(1, 1).start()
            two_hop_half(1, 0).wait_recv()
            two_hop_half(1, 1).wait_recv()
            pass_on(1, 2).start()
            for k in (0, 4, 5, 6):
                arrival(1, k).wait_recv()
            for which in (0, 1):
                for cp in (first_copies(which) + [relay(which, part) for part in range(2)]
                           + [pass_on(which, j) for j in range(3)]):
                    cp.wait_send()
            wout_copy = pltpu.make_async_copy(wout_land, gout_hbm, local_sems.at[N_DEV])
            wout_copy.start()
            for step in range(N_DEV):
                to_hbm(step).wait()
            wout_copy.wait()

    me = _my_place()
    x_, y_, c_ = me
    chips = other_chips(me)
    order = jnp.stack([_flat(p) for p in (
        me, (x_, y_, 1 - c_), (*chips[0], c_), (*chips[1], 1 - c_), (*chips[1], c_), (*chips[0], 1 - c_),
        (*chips[2], c_), (*chips[2], 1 - c_))]).astype(jnp.int32)

    first_sweep = lambda g, i, order: (jnp.where(g == 0, i, n_tiles - 1), 0)
    tab = pl.BlockSpec((tm, LANES), lambda g, i, order: (jnp.where(order[g] < 2, i, 0), 0))
    whole = lambda: pl.BlockSpec(memory_space=pltpu.VMEM)
    grid_spec = pltpu.PrefetchScalarGridSpec(
        num_scalar_prefetch=1, grid=(N_DEV, n_tiles),
        in_specs=[pl.BlockSpec((tm, D_MODEL), first_sweep),
                  pl.BlockSpec((1, D_MODEL), lambda g, i, order: (0, 0)),
                  whole(), whole(), tab, tab, tab],
        out_specs=(pl.BlockSpec((None, tm, COL_BLOCK), lambda g, i, order: (order[g], i, 0)),
                   pl.BlockSpec((D_MODEL, tm), lambda g, i, order: (0, jnp.where(g == 0, i, n_tiles - 1))),
                   pl.BlockSpec(memory_space=pl.ANY), pl.BlockSpec(memory_space=pl.ANY),
                   pl.BlockSpec(memory_space=pl.ANY)),
        scratch_shapes=[pltpu.VMEM((SEQ, D_MODEL), BF16),
                        pltpu.VMEM((N_DEV, D_MODEL, COL_BLOCK), BF16),
                        pltpu.VMEM((D_MODEL, D_MODEL), BF16),
                        pltpu.VMEM((D_MODEL, COL_BLOCK), BF16),
                        pltpu.VMEM((2, tm // SORT_RESIDUES, SORT_RESIDUES, COL_BLOCK), F32),
                        pltpu.SemaphoreType.DMA((16,)), pltpu.SemaphoreType.DMA((16,)),
                        pltpu.SemaphoreType.DMA((N_DEV + 1,)), pltpu.SemaphoreType.DMA((2, SORT_RESIDUES))])
    proj, hn_t, w_in_g, w_out_g, qkv_sorted = pl.pallas_call(
        body, name="gather_project", grid_spec=grid_spec,
        out_shape=(jax.ShapeDtypeStruct((N_DEV, SEQ, COL_BLOCK), F32), jax.ShapeDtypeStruct((D_MODEL, SEQ), BF16),
                   jax.ShapeDtypeStruct((D_MODEL, IN_COLS), BF16), jax.ShapeDtypeStruct((D_MODEL, D_MODEL), BF16),
                   jax.ShapeDtypeStruct((SORT_RESIDUES, SORT_ROWS, 3 * COL_BLOCK), F32)),
        compiler_params=_params(("arbitrary", "arbitrary")),
    )(order, x, mix_w, w_in, w_out, rc, rsa, rsb)
    return proj, hn_t, w_in_g, w_out_g, qkv_sorted.reshape(SEQ, 3 * COL_BLOCK)


SCORE_SCALE = HEAD_DIM ** -0.5
ATTN_GROUP_FWD = 32
ATTN_GROUP_BWD = 16
BLOCKS_PER_PATTERN = SEQ // ATTN_BLOCK
SORT_RESIDUES = 16
SORT_ROWS = SEQ // SORT_RESIDUES


def _write_band_bias(bias_ref):
    row = lax.broadcasted_iota(jnp.int32, (2 * ATTN_BLOCK, 2 * ATTN_BLOCK), 0) & (ATTN_BLOCK - 1)
    col = lax.broadcasted_iota(jnp.int32, (2 * ATTN_BLOCK, 2 * ATTN_BLOCK), 1)
    for pi, d in enumerate(DILATIONS):
        per = SORT_RESIDUES // d
        ahead = per * (row % (8 * d) - col % (16 * d)) + (row // (8 * d) - col // (16 * d))
        dist = ATTN_BLOCK + ahead
        bias_ref[2 * pi] = jnp.where((dist >= 0) & (dist <= ATTN_BLOCK), 0.0, NEG_BIG)
        bias_ref[2 * pi + 1] = jnp.where(ahead >= 0, 0.0, NEG_BIG)


def _head0_lanes():
    return lax.broadcasted_iota(jnp.int32, (ATTN_BLOCK, LANES), 1) < HEAD_DIM


def _stack_heads(t, h0):
    return jnp.concatenate([jnp.where(h0, t, 0.0), jnp.where(h0, 0.0, t)], axis=0).astype(BF16)


def _block_runs(i, d):
    nblk = BLOCKS_PER_PATTERN // d
    r, n = i // nblk, i % nblk
    kn = jnp.maximum(n - 1, 0)
    rows, keys = [], []
    for c in range(SORT_RESIDUES // d):
        base = SORT_ROWS * (c * d + r)
        rows.append(pl.ds(pl.multiple_of(base + 8 * d * n, 8), 8 * d))
        keys.append(pl.ds(pl.multiple_of(base + 8 * d * kn, 8), 16 * d))
    return rows, keys, (n == 0).astype(jnp.int32)


def _take(ref, runs):
    return jnp.concatenate([ref[run, :] for run in runs], axis=0)


def _put(ref, runs, value, add=False):
    at = 0
    for run in runs:
        piece = value[at:at + run.size]
        if add:
            ref[run, :] += piece
        else:
            ref[run, :] = piece
        at += run.size


def _sort_copies(src_hbm, lane_block, dst_ref, sem_ref):
    lanes = pl.ds(pl.multiple_of(LANES * lane_block, LANES), LANES)
    return [pltpu.make_async_copy(src_hbm.at[:, r, lanes], dst_ref.at[pl.ds(SORT_ROWS * r, SORT_ROWS), :],
                                  sem_ref.at[r]) for r in range(SORT_RESIDUES)]


def _unsort_copies(src_ref, dst_hbm, lane_block, sem_ref):
    lanes = pl.ds(pl.multiple_of(LANES * lane_block, LANES), LANES)
    return [pltpu.make_async_copy(src_ref.at[pl.ds(SORT_ROWS * r, SORT_ROWS), :], dst_hbm.at[:, r, lanes],
                                  sem_ref.at[r]) for r in range(SORT_RESIDUES)]


def _for_each_group(d, n_group, load, compute, store):
    def group(g, carry):
        items = [load(*_block_runs(g * n_group + u, d)) for u in range(n_group)]
        results = [compute(item) for item in items]
        for item, res in zip(items, results):
            store(item, res)
        return carry

    lax.fori_loop(0, BLOCKS_PER_PATTERN // n_group, group, 0)


def _attn_fwd_fused(qkv_sorted):
    n_pat = len(DILATIONS)
    tile2 = (2 * ATTN_BLOCK, LANES)

    def body(q_ref, k_ref, v_ref, o_hbm, lse_ref, o_slots, m_acc, l_acc, bias_ref, out_sem):
        step, n_steps = pl.program_id(0), pl.num_programs(0)
        pl.when(step == 0)(lambda: _write_band_bias(bias_ref))
        slot = step % 2
        o_acc = o_slots.at[slot]
        h0 = _head0_lanes()
        for pi, d in enumerate(DILATIONS):
            first, last = pi == 0, pi == n_pat - 1

            def load(rows, keys, which, first=first, pi=pi):
                item = dict(rows=rows, keys=keys, which=2 * pi + which)
                if not first:
                    item.update(o=_take(o_acc, rows), m=[_take(m_acc.at[h], rows) for h in range(2)],
                                l=[_take(l_acc.at[h], rows) for h in range(2)])
                return item

            def compute(item, first=first):
                kb = _take(k_ref, item["keys"]).astype(BF16)
                vb = _take(v_ref, item["keys"]).astype(BF16)
                s = _mm_nt(_stack_heads(_take(q_ref, item["rows"]) * SCORE_SCALE, h0), kb) + bias_ref[item["which"]]
                mb = jnp.max(s, axis=-1, keepdims=True)
                if first:
                    p = jnp.exp(s - mb)
                    mn = jnp.broadcast_to(mb, tile2)
                else:
                    m_old = jnp.concatenate(item["m"], axis=0)
                    mn = jnp.maximum(m_old, mb)
                    alpha = jnp.exp(m_old - mn)
                    p = jnp.exp(s - jnp.concatenate([mn, mn], axis=1))
                ls = jnp.sum(p, axis=-1, keepdims=True)
                pv = _mm(p.astype(BF16), vb)
                if first:
                    return pv, mn, jnp.broadcast_to(ls, tile2)
                o_old = jnp.concatenate([item["o"], item["o"]], axis=0)
                return alpha * o_old + pv, mn, alpha * jnp.concatenate(item["l"], axis=0) + ls

            def store(item, res, last=last):
                rows = item["rows"]
                (o0, o1), (m0, m1), (l0, l1) = ((a[:ATTN_BLOCK], a[ATTN_BLOCK:]) for a in res)
                if last:
                    _put(o_acc, rows, jnp.where(h0, o0 / l0, o1 / l1))
                    _put(lse_ref, rows, jnp.where(h0, m0 + jnp.log(l0), m1 + jnp.log(l1)))
                else:
                    _put(o_acc, rows, jnp.where(h0, o0, o1))
                    for h, (m, l) in enumerate(((m0, l0), (m1, l1))):
                        _put(m_acc.at[h], rows, m)
                        _put(l_acc.at[h], rows, l)

            _for_each_group(d, ATTN_GROUP_FWD, load, compute, store)

        def copies_out(of_step):
            return _unsort_copies(o_slots.at[of_step % 2], o_hbm, of_step, out_sem.at[of_step % 2])

        @pl.when(step > 0)
        def _():
            for copy in copies_out(step - 1):
                copy.wait()

        for copy in copies_out(step):
            copy.start()

        @pl.when(step == n_steps - 1)
        def _():
            for copy in copies_out(step):
                copy.wait()

    slab = lambda g: pl.BlockSpec((SEQ, LANES), functools.partial(lambda hp, g: (0, 4 * g + hp), g=g))
    wide = jax.ShapeDtypeStruct((SEQ, ATTN_WIDTH), F32)
    o_rows, lse = pl.pallas_call(
        body, name="attn_fwd", grid=(4,),
        out_shape=(jax.ShapeDtypeStruct((SORT_ROWS, SORT_RESIDUES, ATTN_WIDTH), F32), wide),
        in_specs=[slab(0), slab(1), slab(2)], out_specs=(pl.BlockSpec(memory_space=pl.ANY), slab(0)),
        scratch_shapes=[pltpu.VMEM((2, SEQ, LANES), F32), pltpu.VMEM((2, SEQ, LANES), F32),
                        pltpu.VMEM((2, SEQ, LANES), F32),
                        pltpu.VMEM((2 * len(DILATIONS), 2 * ATTN_BLOCK, 2 * ATTN_BLOCK), F32),
                        pltpu.SemaphoreType.DMA((2, SORT_RESIDUES))],
        compiler_params=_params(("arbitrary",)),
    )(qkv_sorted, qkv_sorted, qkv_sorted)
    return o_rows.reshape(SEQ, ATTN_WIDTH), lse


def _attn_bwd_fused(qkv_sorted, d_out, lse_sorted, delta):
    def body(q_ref, k_ref, v_ref, do_hbm, lse_ref, del_hbm, dq_hbm, dk_hbm, dv_hbm,
             in_slots, out_slots, bias_ref, in_sem, out_sem):
        step, n_steps = pl.program_id(0), pl.num_programs(0)
        slot = step % 2

        def copies_in(of_step):
            s = of_step % 2
            return [copy for j, hbm in enumerate((do_hbm, del_hbm))
                    for copy in _sort_copies(hbm, of_step, in_slots.at[s, j], in_sem.at[s, j])]

        def copies_out(of_step):
            s = of_step % 2
            return [copy for j, hbm in enumerate((dq_hbm, dk_hbm, dv_hbm))
                    for copy in _unsort_copies(out_slots.at[s, j], hbm, of_step, out_sem.at[s, j])]

        @pl.when(step == 0)
        def _():
            for copy in copies_in(step):
                copy.start()
            _write_band_bias(bias_ref)

        @pl.when(step + 1 < n_steps)
        def _():
            for copy in copies_in(step + 1):
                copy.start()

        do_s, del_s = in_slots.at[slot, 0], in_slots.at[slot, 1]
        dq_s, dk_s, dv_s = (out_slots.at[slot, j] for j in range(3))
        dk_s[...] = jnp.zeros_like(dk_s)
        dv_s[...] = jnp.zeros_like(dv_s)
        for copy in copies_in(step):
            copy.wait()
        h0 = _head0_lanes()
        for pi, d in enumerate(DILATIONS):
            first = pi == 0

            def load(rows, keys, which, pi=pi):
                return dict(rows=rows, keys=keys, q=_take(q_ref, rows), g=_take(do_s, rows),
                            lse=_take(lse_ref, rows), delta=_take(del_s, rows),
                            k=_take(k_ref, keys).astype(BF16), v=_take(v_ref, keys).astype(BF16),
                            bias=bias_ref[2 * pi + which])

            def per_head(t):
                swapped = pltpu.roll(t, HEAD_DIM, 1)
                both = jnp.concatenate([jnp.where(h0, t, swapped), jnp.where(h0, swapped, t)], axis=0)
                return jnp.concatenate([both, both], axis=1)

            def compute(item):
                q2, g2 = _stack_heads(item["q"] * SCORE_SCALE, h0), _stack_heads(item["g"], h0)
                s = _mm_nt(q2, item["k"]) + item["bias"]
                p = jnp.exp(s - per_head(item["lse"]))
                dp = _mm_nt(g2, item["v"])
                ds = (p * (dp - per_head(item["delta"]))).astype(BF16)
                dq2 = _mm(ds, item["k"])
                dq = jnp.where(h0, dq2[:ATTN_BLOCK], dq2[ATTN_BLOCK:]) * SCORE_SCALE
                return dq, _mm_tn(ds, q2), _mm_tn(p.astype(BF16), g2)

            def store(item, res, first=first):
                _put(dq_s, item["rows"], res[0], add=not first)
                _put(dk_s, item["keys"], res[1], add=True)
                _put(dv_s, item["keys"], res[2], add=True)

            _for_each_group(d, ATTN_GROUP_BWD, load, compute, store)

        @pl.when(step > 0)
        def _():
            for copy in copies_out(step - 1):
                copy.wait()

        for copy in copies_out(step):
            copy.start()

        @pl.when(step == n_steps - 1)
        def _():
            for copy in copies_out(step):
                copy.wait()

    slab = lambda g: pl.BlockSpec((SEQ, LANES), functools.partial(lambda hp, g: (0, 4 * g + hp), g=g))
    anywhere = pl.BlockSpec(memory_space=pl.ANY)
    by_residue = (SORT_ROWS, SORT_RESIDUES, ATTN_WIDTH)
    grads = pl.pallas_call(
        body, name="attn_bwd", grid=(4,), out_shape=(jax.ShapeDtypeStruct(by_residue, F32),) * 3,
        scratch_shapes=[pltpu.VMEM((2, 2, SEQ, LANES), F32), pltpu.VMEM((2, 3, SEQ, LANES), F32),
                        pltpu.VMEM((2 * len(DILATIONS), 2 * ATTN_BLOCK, 2 * ATTN_BLOCK), F32),
                        pltpu.SemaphoreType.DMA((2, 2, SORT_RESIDUES)), pltpu.SemaphoreType.DMA((2, 3, SORT_RESIDUES))],
        in_specs=[slab(0), slab(1), slab(2), anywhere, slab(0), anywhere], out_specs=(anywhere,) * 3,
        compiler_params=_params(("arbitrary",)),
    )(qkv_sorted, qkv_sorted, qkv_sorted, d_out.reshape(by_residue), lse_sorted, delta.reshape(by_residue))
    return tuple(g.reshape(SEQ, ATTN_WIDTH) for g in grads)


def _hgrn_lower_bound(lb_ref):
    r0, r1 = lb_ref[0:1, :], lb_ref[1:2, :]
    mx = jnp.maximum(r0, r1)
    e0, e1 = jnp.exp(r0 - mx), jnp.exp(r1 - mx)
    return e0 / (e0 + e1)


def _hgrn_gates(hq, hf, lb):
    sq = _sigmoid(hq)
    sg = _sigmoid(hf)
    f = lb + (1.0 - lb) * sg
    return hq * sq, sq, sg, f, 1.0 - f, jnp.log(f)


HGRN_PAIR = 4
HGRN_SEQ_BLOCK = 1024
HGRN_GROUP = 4
HGRN_ROWS = HGRN_GROUP * HGRN_CHUNK


def _hgrn_specs(reverse):
    n_blocks = SEQ // HGRN_SEQ_BLOCK
    width = HGRN_PAIR * HGRN_DIM
    blk = (lambda s: n_blocks - 1 - s) if reverse else (lambda s: s)
    cols = lambda g: pl.BlockSpec((None, HGRN_SEQ_BLOCK, width), functools.partial(lambda p, s, g: (g, blk(s), p), g=g))
    pair = pl.BlockSpec((HGRN_SEQ_BLOCK, width), lambda p, s: (blk(s), p))
    lb = pl.BlockSpec((2, width), lambda p, s: (0, p))
    states = pl.BlockSpec((HGRN_PAIR, HGRN_SEQ_BLOCK // HGRN_CHUNK, HGRN_DIM, HGRN_DIM),
                          lambda p, s: (p, blk(s), 0, 0))
    return cols, pair, lb, states


def _chunk_masks():
    ri = lax.broadcasted_iota(jnp.int32, (HGRN_ROWS, HGRN_ROWS), 0)
    ci = lax.broadcasted_iota(jnp.int32, (HGRN_ROWS, HGRN_ROWS), 1)
    same = (ri // HGRN_CHUNK) == (ci // HGRN_CHUNK)
    return same, same & (ri >= ci), same & (ri <= ci)


def _mm_select(sel, v):
    hi = v.astype(BF16)
    r1 = v - hi.astype(F32)
    mid = r1.astype(BF16)
    lo = (r1 - mid.astype(F32)).astype(BF16)
    return _mm(sel, hi) + _mm(sel, mid) + _mm(sel, lo)


def _head_cols(a, h):
    return a[:, HGRN_DIM * h:HGRN_DIM * (h + 1)]


def _hgrn_fwd(proj, lb_raw):
    t, rws = HGRN_CHUNK, HGRN_ROWS

    def body(hq_ref, hf_ref, hi_ref, lb_ref, rec_ref, st_ref, state):
        @pl.when(pl.program_id(1) == 0)
        def _():
            state[...] = jnp.zeros_like(state)

        lb = _hgrn_lower_bound(lb_ref)
        same, causal, _ = _chunk_masks()
        sel = jnp.concatenate([causal, same], axis=0).astype(BF16)

        def group(g, sts):
            rows = pl.ds(pl.multiple_of(g * rws, rws), rws)
            q, _, _, _, k, lf = _hgrn_gates(hq_ref[rows, :], hf_ref[rows, :], lb)
            sums = _mm_select(sel, lf)
            cum, last = sums[:rws], sums[rws:]
            qd = (q * jnp.exp(cum)).astype(BF16)
            ki = (k * jnp.exp(-cum)).astype(BF16)
            ke = (k * jnp.exp(last - cum)).astype(BF16)
            vb = hi_ref[rows, :].astype(BF16)
            dec = jnp.exp(last)
            new_sts, recs = [], []
            for h in range(HGRN_PAIR):
                qd_h, ke_h, vb_h = _head_cols(qd, h), _head_cols(ke, h), _head_cols(vb, h)
                att = jnp.where(causal, _mm_nt(qd_h, _head_cols(ki, h)), 0.0).astype(BF16)
                intra = _mm(att, vb_h)
                st = sts[h]
                outs = []
                for c in range(HGRN_GROUP):
                    sl = slice(c * t, (c + 1) * t)
                    st_ref[h, g * HGRN_GROUP + c] = st
                    outs.append(intra[sl] + _mm_nt(qd_h[sl], st.astype(BF16)))
                    st = st * _head_cols(dec[c * t:c * t + 1, :], h) + _mm_tn(vb_h[sl], ke_h[sl])
                new_sts.append(st)
                recs.append(jnp.concatenate(outs, axis=0))
            rec_ref[rows, :] = jnp.concatenate(recs, axis=1)
            return tuple(new_sts)

        sts = lax.fori_loop(0, HGRN_SEQ_BLOCK // rws, group, tuple(state[h] for h in range(HGRN_PAIR)), unroll=True)
        for h in range(HGRN_PAIR):
            state[h] = sts[h]

    cols, pair, lb, states = _hgrn_specs(reverse=False)
    return pl.pallas_call(
        body, name="hgrn_fwd", grid=(HGRN_HEADS // HGRN_PAIR, SEQ // HGRN_SEQ_BLOCK),
        out_shape=(jax.ShapeDtypeStruct((SEQ, HGRN_WIDTH), F32),
                   jax.ShapeDtypeStruct((HGRN_HEADS, N_CHUNKS, HGRN_DIM, HGRN_DIM), F32)),
        in_specs=[cols(4), cols(5), cols(6), lb], out_specs=(pair, states),
        scratch_shapes=[pltpu.VMEM((HGRN_PAIR, HGRN_DIM, HGRN_DIM), F32)],
        compiler_params=_params(("parallel", "arbitrary")),
    )(proj, proj, proj, lb_raw)


def _hgrn_bwd(proj, lb_raw, d_rec, states):
    t, rws = HGRN_CHUNK, HGRN_ROWS

    def body(hq_ref, hf_ref, hi_ref, lb_ref, do_ref, st_ref, dhq_ref, dhf_ref, dhi_ref, dlb_ref,
             dstate, dlb_acc):
        lb = _hgrn_lower_bound(lb_ref)
        same, causal, anti = _chunk_masks()
        sel = jnp.concatenate([causal, same], axis=0).astype(BF16)
        sel_t = jnp.concatenate([anti, same], axis=1).astype(BF16)
        @pl.when(pl.program_id(1) == 0)
        def _():
            dstate[...] = jnp.zeros_like(dstate)
            dlb_acc[...] = jnp.zeros_like(dlb_acc)

        n_groups = HGRN_SEQ_BLOCK // rws
        chunks = [slice(c * t, (c + 1) * t) for c in range(HGRN_GROUP)]

        def group(i, dsts_in):
            g = n_groups - 1 - i
            rows = pl.ds(pl.multiple_of(g * rws, rws), rws)
            hq = hq_ref[rows, :]
            q, sq, sg, f, k, lf = _hgrn_gates(hq, hf_ref[rows, :], lb)
            sums = _mm_select(sel, lf)
            cum, last = sums[:rws], sums[rws:]
            e_cum, e_inv, e_end, dec = jnp.exp(cum), jnp.exp(-cum), jnp.exp(last - cum), jnp.exp(last)
            qd, ki, ke = q * e_cum, k * e_inv, k * e_end
            qdb, kib, keb = qd.astype(BF16), ki.astype(BF16), ke.astype(BF16)
            vb = hi_ref[rows, :].astype(BF16)
            gb = do_ref[rows, :].astype(BF16)

            dsts_out, per_head = [], []
            for h in range(HGRN_PAIR):
                qdb_h, kib_h, keb_h = _head_cols(qdb, h), _head_cols(kib, h), _head_cols(keb, h)
                vb_h, gb_h = _head_cols(vb, h), _head_cols(gb, h)
                att = jnp.where(causal, _mm_nt(qdb_h, kib_h), 0.0).astype(BF16)
                datt = jnp.where(causal, _mm_nt(gb_h, vb_h), 0.0).astype(BF16)
                dv = _mm_tn(att, gb_h)
                dqd = _mm(datt, kib_h)
                dki = _mm_tn(datt, qdb_h)

                decs = [_head_cols(dec[c * t:c * t + 1, :], h) for c in range(HGRN_GROUP)]
                dsts = [None] * HGRN_GROUP
                dst = dsts_in[h]
                for c in reversed(range(HGRN_GROUP)):
                    dsts[c] = dst
                    dst = dst * decs[c] + _mm_tn(gb_h[chunks[c]], qdb_h[chunks[c]])
                dsts_out.append(dst)

                dv_x, dqd_x, dke, dlast_x = [], [], [], []
                for c, sl in enumerate(chunks):
                    st_prev = st_ref[h, g * HGRN_GROUP + c]
                    dstb = dsts[c].astype(BF16)
                    dv_x.append(_mm_nt(keb_h[sl], dstb))
                    dqd_x.append(_mm(gb_h[sl], st_prev.astype(BF16)))
                    dke.append(_mm(vb_h[sl], dstb))
                    ddec = jnp.sum(dsts[c] * st_prev, axis=0, keepdims=True)
                    dlast_x.append(jnp.broadcast_to(ddec * decs[c], (t, HGRN_DIM)))
                per_head.append((dv + jnp.concatenate(dv_x, axis=0), dqd + jnp.concatenate(dqd_x, axis=0),
                                 dki, jnp.concatenate(dke, axis=0), jnp.concatenate(dlast_x, axis=0)))
            dv, dqd, dki, dke, dlast = (jnp.concatenate(list(parts), axis=1) for parts in zip(*per_head))

            dq = dqd * e_cum
            dk = dki * e_inv + dke * e_end
            dke_ke = dke * ke
            dcum = dqd * qd - dki * ki - dke_ke
            dlf = _mm_select(sel_t, jnp.concatenate([dcum, dke_ke], axis=0)) + dlast
            df = dlf / f - dk
            dhq_ref[rows, :] = (dq * (sq * (1.0 + hq * (1.0 - sq)))).astype(BF16)
            dhf_ref[rows, :] = (df * (1.0 - lb) * (sg * (1.0 - sg))).astype(BF16)
            dhi_ref[rows, :] = dv.astype(BF16)
            dlb_acc[...] += jnp.sum(df * (1.0 - sg), axis=0, keepdims=True)
            return tuple(dsts_out)

        dsts = lax.fori_loop(0, n_groups, group, tuple(dstate[h] for h in range(HGRN_PAIR)), unroll=True)
        for h in range(HGRN_PAIR):
            dstate[h] = dsts[h]
        g0 = dlb_acc[...] * lb * (1.0 - lb)
        dlb_ref[...] = jnp.concatenate([g0, -g0], axis=0)

    cols, pair, lb_spec, st_spec = _hgrn_specs(reverse=True)
    wide = jax.ShapeDtypeStruct((SEQ, HGRN_WIDTH), BF16)
    return pl.pallas_call(
        body, name="hgrn_bwd", grid=(HGRN_HEADS // HGRN_PAIR, SEQ // HGRN_SEQ_BLOCK),
        out_shape=(wide, wide, wide, jax.ShapeDtypeStruct((2, HGRN_WIDTH), F32)),
        in_specs=[cols(4), cols(5), cols(6), lb_spec, pair, st_spec],
        out_specs=(pair, pair, pair, lb_spec),
        scratch_shapes=[pltpu.VMEM((HGRN_PAIR, HGRN_DIM, HGRN_DIM), F32),
                        pltpu.VMEM((1, HGRN_PAIR * HGRN_DIM), F32)],
        compiler_params=_params(("parallel", "arbitrary")),
    )(proj, proj, proj, lb_raw, d_rec, states)


def _group_sum(v, group):
    parts = []
    for s in range(v.shape[1] // LANES):
        slab = v[:, LANES * s:LANES * (s + 1)]
        if group == LANES:
            parts.append(jnp.broadcast_to(jnp.sum(slab, axis=-1, keepdims=True), slab.shape))
        else:
            h0 = lax.broadcasted_iota(jnp.int32, slab.shape, 1) < HEAD_DIM
            s0 = jnp.sum(jnp.where(h0, slab, 0.0), axis=-1, keepdims=True)
            s1 = jnp.sum(jnp.where(h0, 0.0, slab), axis=-1, keepdims=True)
            parts.append(jnp.where(h0, s0, s1))
    return jnp.concatenate(parts, axis=1)


def _mid(attn_o, rec, proj, x, target, w_out_g, attn_w, hgrn_w, final_w):
    tm = 256

    def branch_fwd(o, gate, w, group):
        r = lax.rsqrt(_group_sum(o * o, group) * (1.0 / group) + NORM_EPS)
        nrm = o * r
        sg = _sigmoid(gate)
        return r, nrm, sg, nrm * w * (gate * sg)

    def branch_bwd(dy, r, nrm, sg, gate, w, group):
        silu = gate * sg
        d_gate = dy * nrm * w * (sg * (1.0 + gate * (1.0 - sg)))
        d_w = jnp.sum(dy * nrm * silu, axis=0, keepdims=True)
        dn = dy * w * silu
        d_o = r * (dn - nrm * (_group_sum(dn * nrm, group) * (1.0 / group)))
        return d_o, d_gate, d_w

    def body(o_ref, rec_ref, ag_ref, hg_ref, x_ref, tgt_ref, wout_ref, aw_ref, hw_ref, fw_ref,
             dx2_ref, do_ref, delta_ref, dag_ref, drec_ref, dhg_ref, dwout_ref, dfw_ref, daw_ref, dhw_ref,
             loss_ref, dwout_acc):
        i = pl.program_id(0)

        @pl.when(i == 0)
        def _():
            dwout_acc[...] = jnp.zeros_like(dwout_acc)
            dfw_ref[...] = jnp.zeros_like(dfw_ref)
            daw_ref[...] = jnp.zeros_like(daw_ref)
            dhw_ref[...] = jnp.zeros_like(dhw_ref)
            loss_ref[...] = jnp.zeros_like(loss_ref)

        o, rc, ag, hg = o_ref[...], rec_ref[...], ag_ref[...], hg_ref[...]
        aw, hw, fw = aw_ref[...], hw_ref[...], fw_ref[...]
        ra, na, sga, ya = branch_fwd(o, ag, aw, HEAD_DIM)
        rh, nh, sgh, yh = branch_fwd(rc, hg, hw, HGRN_DIM)
        mixed = jnp.concatenate([ya, yh], axis=1).astype(BF16)
        wout = wout_ref[...]
        x2 = x_ref[...] + _mm(mixed, wout)
        rstd = lax.rsqrt(jnp.mean(x2 * x2, axis=-1, keepdims=True) + NORM_EPS)
        xn = x2 * rstd
        err = xn * fw - tgt_ref[...]
        row_loss = jnp.mean(err * err, axis=-1, keepdims=True)
        loss_ref[...] += 0.5 * jnp.sum(row_loss, axis=0, keepdims=True)
        dy = err * (1.0 / D_MODEL)
        dfw_ref[...] += jnp.sum(dy * xn, axis=0, keepdims=True)
        dxn = dy * fw
        dx2 = rstd * (dxn - xn * jnp.mean(dxn * xn, axis=-1, keepdims=True))
        dx2_ref[...] = dx2
        dx2b = dx2.astype(BF16)
        dwout_acc[...] += _mm_tn(mixed, dx2b)

        @pl.when(i == pl.num_programs(0) - 1)
        def _():
            dwout_ref[...] = dwout_acc[...].astype(BF16)

        dmixed = _mm_nt(dx2b, wout)

        d_o, d_ag, d_aw = branch_bwd(dmixed[:, :ATTN_WIDTH], ra, na, sga, ag, aw, HEAD_DIM)
        d_rec, d_hg, d_hw = branch_bwd(dmixed[:, ATTN_WIDTH:], rh, nh, sgh, hg, hw, HGRN_DIM)
        do_ref[...] = d_o
        delta_ref[...] = _group_sum(d_o * o, HEAD_DIM)
        dag_ref[...] = d_ag.astype(BF16)
        drec_ref[...] = d_rec
        dhg_ref[...] = d_hg.astype(BF16)
        daw_ref[...] += d_aw
        dhw_ref[...] += d_hw

    half = lambda: pl.BlockSpec((tm, COL_BLOCK), lambda i: (i, 0))
    full = lambda: pl.BlockSpec((tm, D_MODEL), lambda i: (i, 0))
    fixed = lambda r, c: pl.BlockSpec((r, c), lambda i: (0, 0))
    wide = jax.ShapeDtypeStruct((SEQ, COL_BLOCK), F32)
    wide_b = jax.ShapeDtypeStruct((SEQ, COL_BLOCK), BF16)
    return pl.pallas_call(
        body, name="mid", grid=(SEQ // tm,),
        out_shape=(jax.ShapeDtypeStruct((SEQ, D_MODEL), F32), wide, wide, wide_b, wide, wide_b,
                   jax.ShapeDtypeStruct((D_MODEL, D_MODEL), BF16),
                   jax.ShapeDtypeStruct((1, D_MODEL), F32), jax.ShapeDtypeStruct((1, COL_BLOCK), F32),
                   jax.ShapeDtypeStruct((1, COL_BLOCK), F32), jax.ShapeDtypeStruct((1, 1), F32)),
        scratch_shapes=[pltpu.VMEM((D_MODEL, D_MODEL), F32)],
        in_specs=[half(), half(),
                  pl.BlockSpec((None, tm, COL_BLOCK), lambda i: (3, i, 0)),
                  pl.BlockSpec((None, tm, COL_BLOCK), lambda i: (7, i, 0)),
                  full(), full(), fixed(D_MODEL, D_MODEL), fixed(1, COL_BLOCK), fixed(1, COL_BLOCK),
                  fixed(1, D_MODEL)],
        out_specs=(full(), half(), half(), half(), half(), half(), fixed(D_MODEL, D_MODEL),
                   fixed(1, D_MODEL), fixed(1, COL_BLOCK), fixed(1, COL_BLOCK), fixed(1, 1)),
        compiler_params=_params(("arbitrary",)),
    )(attn_o, rec, proj, proj, x, target, w_out_g, attn_w, hgrn_w, final_w)


def _in_proj_bwd_rows(d_groups, w_g, x, dx2, mix_w, rc, rsa, rsb):
    tm = 256

    def body(*refs):
        dg_refs = refs[:N_DEV]
        wg_ref, x_ref, dx2_ref, w_ref, c_ref, sa_ref, sb_ref, gx_ref, dpb_ref, dmw_ref = refs[N_DEV:]

        @pl.when(pl.program_id(0) == 0)
        def _():
            dmw_ref[...] = jnp.zeros_like(dmw_ref)

        parts = []
        for j in range(N_DEV):
            dp = dg_refs[j][...]
            if j < 2:
                dp = _rot_transposed(dp, c_ref[...], sa_ref[...], sb_ref[...])
            parts.append(dp.astype(BF16))
        dpb = jnp.concatenate(parts, axis=1)
        for j in range(N_DEV):
            dpb_ref[j] = parts[j]
        g = _mm_nt(dpb, wg_ref[...])
        xf = x_ref[...]
        rstd = lax.rsqrt(jnp.mean(xf * xf, axis=-1, keepdims=True) + NORM_EPS)
        xn = xf * rstd
        dmw_ref[...] += jnp.sum(g * xn, axis=0, keepdims=True)
        gw = g * w_ref[...]
        gx_ref[...] = dx2_ref[...] + rstd * (gw - xn * jnp.mean(gw * xn, axis=-1, keepdims=True))

    tile = lambda cols: pl.BlockSpec((tm, cols), lambda i: (i, 0))
    fixed = lambda r, c: pl.BlockSpec((r, c), lambda i: (0, 0))
    return pl.pallas_call(
        body, name="in_proj_bwd_rows", grid=(SEQ // tm,),
        out_shape=(jax.ShapeDtypeStruct((SEQ, D_MODEL), F32), jax.ShapeDtypeStruct((N_DEV, SEQ, COL_BLOCK), BF16),
                   jax.ShapeDtypeStruct((1, D_MODEL), F32)),
        in_specs=[tile(COL_BLOCK) for _ in range(N_DEV)] + [
            pl.BlockSpec((D_MODEL, IN_COLS), lambda i: (0, 0), pipeline_mode=pl.Buffered(1)),
            tile(D_MODEL), tile(D_MODEL), fixed(1, D_MODEL), tile(LANES), tile(LANES), tile(LANES)],
        out_specs=(tile(D_MODEL), pl.BlockSpec((N_DEV, tm, COL_BLOCK), lambda i: (0, i, 0)), fixed(1, D_MODEL)),
        compiler_params=_params(("arbitrary",)),
    )(*d_groups, w_g, x, dx2, mix_w, rc, rsa, rsb)


def _weights_exchange(hn_t, dproj_b, dwout_p, small_p):
    n_chips = N_DEV // 2
    rb = 128
    S1_IN, S1_OUT, SMALL, S2_IN, S2_OUT, VIA_IN, VIA_OUT = 0, 4, 8, 15, 17, 19, 21
    rel_of_pair = (3, 1, 2, 0)
    two_hop = n_chips - 1
    half_in, half_out = COL_BLOCK // 2, D_MODEL // 2

    def body(order_ref, hnt_ref, dp_ref, dwout_ref, small_ref, gin_ref, gout_ref, gs_ref,
             part, s1_send, s1_in, s1_out, fwd_in, fwd_out, s2_in, s2_out, via_in, via_out, land_s,
             send_sems, recv_sems):
        t = pl.program_id(0)
        me = _my_place()
        x, y, c = me
        my_chip = 2 * x + y
        sibling = (x, y, 1 - c)

        def remote(slot, src, dst, to):
            return pltpu.make_async_remote_copy(src_ref=src, dst_ref=dst, send_sem=send_sems.at[slot],
                                                recv_sem=recv_sems.at[slot], device_id=to, device_id_type=MESH)

        def s1_in_copy(pair):
            return remote(S1_IN + pair, s1_send.at[pair], s1_in.at[pair], sibling)

        def s1_out_copy(pair):
            q = my_chip ^ rel_of_pair[pair]
            return remote(S1_OUT + pair, dwout_ref.at[q, 1 - c], s1_out.at[pair], sibling)

        def s2_copies(rel):
            peer = _peer(me, 2 * rel)
            return [remote(S2_IN + rel - 1, fwd_in.at[rel - 1], s2_in.at[rel - 1], peer),
                    remote(S2_OUT + rel - 1, fwd_out.at[rel - 1], s2_out.at[rel - 1], peer)]

        def via_copies(k):
            peer = _peer(me, 2 * (2 - k))
            return [remote(VIA_IN + k, fwd_in.at[two_hop - 1, :, pl.ds(k * half_in, half_in)], via_in.at[k], peer),
                    remote(VIA_OUT + k, fwd_out.at[two_hop - 1, :, pl.ds(k * half_out, half_out)], via_out.at[k],
                           peer)]

        def small_copy(rel):
            return remote(SMALL + rel - 1, small_ref, land_s.at[rel], _peer(me, rel))

        @pl.when(t == 0)
        def _():
            land_s[0] = small_ref[...]
            for pair in range(n_chips):
                s1_out_copy(pair).start()
            for rel in range(1, N_DEV):
                small_copy(rel).start()

        part[...] = _mm(hnt_ref[...], dp_ref[...])

        def rows_loop(n_rows, fn):
            def step(b, carry):
                fn(pl.ds(pl.multiple_of(b * rb, rb), rb))
                return carry
            lax.fori_loop(0, n_rows // rb, step, 0)

        for pair, rel in enumerate(rel_of_pair):
            @pl.when(t == 2 * pair)
            def _(pair=pair):
                s1_send[pair] = part[...].astype(BF16)
                s1_in_copy(pair).start()

            @pl.when(t == 2 * pair + 1)
            def _(pair=pair, rel=rel):
                q = my_chip ^ rel
                s1_in_copy(pair).wait_recv()
                s1_out_copy(pair).wait_recv()
                dst_in = fwd_in.at[rel - 1] if rel else gin_ref
                dst_out = fwd_out.at[rel - 1] if rel else gout_ref
                passes_on = rel in (1, 2)
                if passes_on:
                    for cp in via_copies(rel - 1):
                        cp.wait_recv()

                def with_half(val, via, rows, width):
                    if not passes_on:
                        return val
                    extra = via[rel - 1, rows, :].astype(F32)
                    halves = [val[:, :width], val[:, width:]]
                    halves[rel - 1] = halves[rel - 1] + extra
                    return jnp.concatenate(halves, axis=1)

                def add_in(rows):
                    val = part[rows, :] + s1_in[pair, rows, :].astype(F32)
                    dst_in[rows, :] = with_half(val, via_in, rows, half_in).astype(dst_in.dtype)

                def add_out(rows):
                    val = dwout_ref[q, c, rows, :].astype(F32) + s1_out[pair, rows, :].astype(F32)
                    dst_out[rows, :] = with_half(val, via_out, rows, half_out).astype(dst_out.dtype)

                rows_loop(D_MODEL, add_in)
                rows_loop(WOUT_ROWS, add_out)
                if rel == two_hop:
                    for k in range(2):
                        for cp in via_copies(k):
                            cp.start()
                elif rel:
                    for cp in s2_copies(rel):
                        cp.start()

        @pl.when(t == N_DEV - 1)
        def _():
            for rel in range(1, two_hop):
                for cp in s2_copies(rel):
                    cp.wait_recv()

            def total_in(rows):
                g = gin_ref[rows, :]
                for rel in range(1, two_hop):
                    g = g + s2_in[rel - 1, rows, :].astype(F32)
                gin_ref[rows, :] = g

            def total_out(rows):
                g = gout_ref[rows, :]
                for rel in range(1, two_hop):
                    g = g + s2_out[rel - 1, rows, :].astype(F32)
                gout_ref[rows, :] = g

            rows_loop(D_MODEL, total_in)
            rows_loop(WOUT_ROWS, total_out)

            for rel in range(1, N_DEV):
                small_copy(rel).wait_recv()
            my_flat = _flat(me)
            g = land_s[my_flat ^ 0]
            for dev in range(1, N_DEV):
                g = g + land_s[my_flat ^ dev]
            gs_ref[...] = g

            for pair in range(n_chips):
                s1_in_copy(pair).wait_send()
                s1_out_copy(pair).wait_send()
            for rel in range(1, two_hop):
                for cp in s2_copies(rel) + via_copies(rel - 1):
                    cp.wait_send()
            for rel in range(1, N_DEV):
                small_copy(rel).wait_send()

    place_x, place_y, place_c = _my_place()
    my_chip = 2 * place_x + place_y
    order = jnp.stack([2 * (my_chip ^ rel) + core for rel in rel_of_pair
                       for core in (1 - place_c, place_c)]).astype(jnp.int32)

    whole = lambda: pl.BlockSpec(memory_space=pltpu.VMEM)
    in_blocks = lambda n: pltpu.VMEM((n, D_MODEL, COL_BLOCK), BF16)
    out_blocks = lambda n: pltpu.VMEM((n, WOUT_ROWS, D_MODEL), BF16)
    grid_spec = pltpu.PrefetchScalarGridSpec(
        num_scalar_prefetch=1, grid=(N_DEV,),
        in_specs=[pl.BlockSpec((D_MODEL, SEQ), lambda t, order: (0, 0), pipeline_mode=pl.Buffered(1)),
                  pl.BlockSpec((None, SEQ, COL_BLOCK), lambda t, order: (order[t], 0, 0)), whole(), whole()],
        out_specs=(whole(), whole(), whole()),
        scratch_shapes=[pltpu.VMEM((D_MODEL, COL_BLOCK), F32), in_blocks(n_chips), in_blocks(n_chips),
                        out_blocks(n_chips), in_blocks(n_chips - 1), out_blocks(n_chips - 1),
                        in_blocks(n_chips - 2), out_blocks(n_chips - 2),
                        pltpu.VMEM((2, D_MODEL, half_in), BF16), pltpu.VMEM((2, WOUT_ROWS, half_out), BF16),
                        pltpu.VMEM((N_DEV, SMALL_ROWS, LANES), F32),
                        pltpu.SemaphoreType.DMA((23,)), pltpu.SemaphoreType.DMA((23,))])
    return pl.pallas_call(
        body, name="weights_exchange", grid_spec=grid_spec,
        out_shape=(jax.ShapeDtypeStruct((D_MODEL, COL_BLOCK), F32), jax.ShapeDtypeStruct((WOUT_ROWS, D_MODEL), F32),
                   jax.ShapeDtypeStruct((SMALL_ROWS, LANES), F32)),
        compiler_params=_params(("arbitrary",)),
    )(order, hn_t, dproj_b, dwout_p.reshape(n_chips, 2, WOUT_ROWS, D_MODEL), small_p)


def _adamw(w, g, m, v):
    m = ADAM_B1 * m + (1.0 - ADAM_B1) * g
    v = ADAM_B2 * v + (1.0 - ADAM_B2) * (g * g)
    m_hat = m / (1.0 - ADAM_B1 ** ADAM_STEP)
    v_hat = v / (1.0 - ADAM_B2 ** ADAM_STEP)
    delta = -ADAM_LR * (m_hat / (jnp.sqrt(v_hat) + ADAM_EPS) + ADAM_WD * w)
    return delta, m, v


def _adamw_update(grads, weights, m_old, v_old):
    rb = 256

    def body(*refs):
        g_refs, w_refs, m_refs, v_refs = refs[0:3], refs[3:6], refs[6:9], refs[9:12]
        d_refs, nm_refs, nv_refs = refs[12:15], refs[15:18], refs[18:21]
        for k in range(3):
            n_rows = g_refs[k].shape[0]
            step_rows = min(rb, n_rows)

            def step(b, carry, k=k, step_rows=step_rows):
                rows = pl.ds(pl.multiple_of(b * step_rows, 8), step_rows)
                delta, nm, nv = _adamw(w_refs[k][rows, :], g_refs[k][rows, :], m_refs[k][rows, :], v_refs[k][rows, :])
                d_refs[k][rows, :] = delta
                nm_refs[k][rows, :] = nm
                nv_refs[k][rows, :] = nv
                return carry

            lax.fori_loop(0, n_rows // step_rows, step, 0)

    shapes = tuple(jax.ShapeDtypeStruct(g.shape, F32) for g in grads)
    vm = lambda: pl.BlockSpec(memory_space=pltpu.VMEM)
    outs = pl.pallas_call(
        body, name="adamw_update", out_shape=shapes * 3,
        in_specs=[vm() for _ in range(12)], out_specs=tuple(vm() for _ in range(9)),
        compiler_params=_params(),
    )(*grads, *weights, *m_old, *v_old)
    return outs[0:3], outs[3:6], outs[6:9]


def _pack_small(mix, attn, hgrn, lb, final, loss=None):
    def rows8(a):
        a = a.reshape(-1, LANES)
        return jnp.pad(a, ((0, 8 - a.shape[0]), (0, 0)))
    last = jnp.zeros((8, LANES), F32) if loss is None else jnp.pad(loss.reshape(1, 1), ((0, 7), (0, LANES - 1)))
    return jnp.concatenate([rows8(mix), rows8(attn), rows8(hgrn), rows8(lb), rows8(final), last], axis=0)


def _unpack_small(slab):
    return (slab[ROW_MIX:ROW_MIX + 8].reshape(1, D_MODEL), slab[ROW_ATTN:ROW_ATTN + 4].reshape(1, ATTN_WIDTH),
            slab[ROW_HGRN:ROW_HGRN + 4].reshape(1, HGRN_WIDTH), slab[ROW_LB:ROW_LB + 8].reshape(2, HGRN_WIDTH),
            slab[ROW_FINAL:ROW_FINAL + 8].reshape(D_MODEL))


def _rope(pos_row):
    j = np.arange(ROPE_ROWS)
    inv = np.where(j < ROPE_HALF, ROPE_THETA ** (-(j % ROPE_HALF) * (2.0 / ROPE_DIMS)), 0.0)
    e = np.arange(LANES) % HEAD_DIM
    hit = (j[:, None] == (e % ROPE_HALF)[None, :]) & (j[:, None] < ROPE_HALF)
    sel = np.stack([hit & (e < ROPE_DIMS), hit & (e >= ROPE_HALF) & (e < ROPE_DIMS),
                    -1.0 * (hit & (e < ROPE_HALF))]).astype(np.float32)
    return _rope_tables(pos_row, jnp.asarray(inv.astype(np.float32).reshape(ROPE_ROWS, 1)),
                        jnp.asarray(sel, dtype=BF16))


def _local_step(x, proj, qkv_sorted, w_in_g, w_out_g, tables, mix_w, attn_w, hgrn_w, lb_raw, final_w, target):
    rc, rsa, rsb = tables
    attn_o, lse = _attn_fwd_fused(qkv_sorted)
    rec, states = _hgrn_fwd(proj, lb_raw)

    (dx2, d_o, delta, d_ag, d_rec, d_hg, dwout_p, d_final, d_attn_w, d_hgrn_w, loss) = _mid(
        attn_o, rec, proj, x, target, w_out_g, attn_w, hgrn_w, final_w.reshape(1, D_MODEL))

    dqkv = _attn_bwd_fused(qkv_sorted, d_o, lse, delta)
    d_hq, d_hf, d_hi, d_lb = _hgrn_bwd(proj, lb_raw, d_rec, states)

    grad_x, dproj_b, d_mix = _in_proj_bwd_rows(
        (dqkv[0], dqkv[1], dqkv[2], d_ag, d_hq, d_hf, d_hi, d_hg), w_in_g, x, dx2, mix_w, rc, rsa, rsb)
    small_p = _pack_small(d_mix, d_attn_w, d_hgrn_w, d_lb, d_final, loss)
    return grad_x, dproj_b, dwout_p, small_p


def kernel(x, positions, w_in, w_out, mix_norm_w, attn_out_norm_w, hgrn_out_norm_w, hgrn_lb_raw, final_norm_w, loss_target, m_w_in, m_w_out, m_mix_norm_w, m_attn_out_norm_w, m_hgrn_out_norm_w, m_hgrn_lb_raw, m_final_norm_w, v_w_in, v_w_out, v_mix_norm_w, v_attn_out_norm_w, v_hgrn_out_norm_w, v_hgrn_lb_raw, v_final_norm_w):
    tables = _rope(positions)
    proj, hn_t, w_in_g, w_out_g, qkv_sorted = _gather_project(x[0], mix_norm_w, w_in[0], w_out[0], *tables)
    grad_x, dproj_b, dwout_p, small_p = _local_step(
        x[0], proj, qkv_sorted, w_in_g, w_out_g, tables, mix_norm_w, attn_out_norm_w, hgrn_out_norm_w,
        hgrn_lb_raw, final_norm_w, loss_target[0])
    g_in, g_out, g_s = _weights_exchange(hn_t, dproj_b, dwout_p, small_p)

    w_s = _pack_small(mix_norm_w, attn_out_norm_w, hgrn_out_norm_w, hgrn_lb_raw, final_norm_w)
    m_s = _pack_small(m_mix_norm_w, m_attn_out_norm_w, m_hgrn_out_norm_w, m_hgrn_lb_raw, m_final_norm_w)
    v_s = _pack_small(v_mix_norm_w, v_attn_out_norm_w, v_hgrn_out_norm_w, v_hgrn_lb_raw, v_final_norm_w)
    (d_in, d_out, d_s), (nm_in, nm_out, nm_s), (nv_in, nv_out, nv_s) = _adamw_update(
        (g_in, g_out, g_s), (w_in[0], w_out[0], w_s), (m_w_in[0], m_w_out[0], m_s), (v_w_in[0], v_w_out[0], v_s))

    loss = g_s[ROW_LOSS, 0]
    return (loss, grad_x[None], g_in[None], g_out[None], *_unpack_small(g_s),
            d_in[None], d_out[None], *_unpack_small(d_s),
            nm_in[None], nm_out[None], *_unpack_small(nm_s),
            nv_in[None], nv_out[None], *_unpack_small(nv_s))
```

```python
import functools

import jax
import jax.numpy as jnp
import numpy as np
from jax import lax
from jax.experimental import pallas as pl
from jax.experimental.pallas import tpu as pltpu

F32 = jnp.float32
BF16 = jnp.bfloat16

SEQ = 4096
D_MODEL = 1024
ATTN_WIDTH = 512
HGRN_WIDTH = 512
HEAD_DIM = 64
HGRN_HEADS = 4
HGRN_DIM = 128
HGRN_CHUNK = 64
N_CHUNKS = SEQ // HGRN_CHUNK
IN_COLS = 4096
COL_BLOCK = 512
N_DEV = 8
WOUT_ROWS = D_MODEL // N_DEV
ATTN_BLOCK = 128
DILATIONS = (1, 4, 16)
ROPE_THETA = 500000.0
ROPE_DIMS = 16
ROPE_HALF = 8
NORM_EPS = 1e-6
NEG_BIG = -1e30
LANES = 128

ADAM_LR = 0.001
ADAM_B1 = 0.9
ADAM_B2 = 0.999
ADAM_EPS = 1e-08
ADAM_WD = 0.01
ADAM_STEP = 10

SMALL_ROWS = 48
ROW_MIX, ROW_ATTN, ROW_HGRN, ROW_LB, ROW_FINAL, ROW_LOSS = 0, 8, 16, 24, 32, 40

VMEM_LIMIT = 56 * 1024 * 1024
MESH = pl.DeviceIdType.MESH


def _mm(a, b):
    return lax.dot_general(a, b, (((1,), (0,)), ((), ())), preferred_element_type=F32)


def _mm_nt(a, b):
    return lax.dot_general(a, b, (((1,), (1,)), ((), ())), preferred_element_type=F32)


def _mm_tn(a, b):
    return lax.dot_general(a, b, (((0,), (0,)), ((), ())), preferred_element_type=F32)


def _mm_exact(a, b):
    return lax.dot_general(a, b, (((1,), (0,)), ((), ())), preferred_element_type=F32,
                           precision=lax.Precision.HIGHEST)


def _sigmoid(v):
    return 1.0 / (1.0 + jnp.exp(-v))


def _params(sem=None, **kw):
    return pltpu.CompilerParams(dimension_semantics=sem, vmem_limit_bytes=VMEM_LIMIT, **kw)


def _my_place():
    return lax.axis_index("x"), lax.axis_index("y"), lax.axis_index("c")


def _peer(place, rel):
    x, y, c = place
    return (x ^ ((rel >> 2) & 1), y ^ ((rel >> 1) & 1), c ^ (rel & 1))


def _flat(place):
    x, y, c = place
    return 4 * x + 2 * y + c


ROPE_ROWS = 16


def _rope_tables(pos_row, inv_freq_col, selectors):
    def body(pos_ref, invf_ref, sel_ref, c_ref, sa_ref, sb_ref):
        ang = pos_ref[...].astype(F32) * invf_ref[...]
        cos, sin = jnp.cos(ang), jnp.sin(ang)

        def spread(v, sel):
            hi = v.astype(BF16)
            r1 = v - hi.astype(F32)
            mid = r1.astype(BF16)
            lo = (r1 - mid.astype(F32)).astype(BF16)
            return _mm_tn(hi, sel) + _mm_tn(mid, sel) + _mm_tn(lo, sel)

        e = lax.broadcasted_iota(jnp.int32, (1, LANES), 1) & (HEAD_DIM - 1)
        c_ref[...] = spread(cos, sel_ref[0]) + jnp.where(e < ROPE_DIMS, 0.0, 1.0)
        sa_ref[...] = spread(sin, sel_ref[1])
        sb_ref[...] = spread(sin, sel_ref[2])

    tab = jax.ShapeDtypeStruct((SEQ, LANES), F32)
    vm = lambda: pl.BlockSpec(memory_space=pltpu.VMEM)
    return pl.pallas_call(
        body, name="rope_tables", out_shape=(tab, tab, tab),
        in_specs=[vm(), vm(), vm()], out_specs=(vm(), vm(), vm()), compiler_params=_params(),
    )(pos_row, inv_freq_col, selectors)


def _per_slab(fn, t):
    return jnp.concatenate([fn(t[:, LANES * s:LANES * (s + 1)]) for s in range(t.shape[1] // LANES)], axis=1)


def _rot(t, c, sa, sb):
    return _per_slab(lambda u: u * c + pltpu.roll(u, ROPE_HALF, 1) * sa + pltpu.roll(u, LANES - ROPE_HALF, 1) * sb, t)


def _rot_transposed(g, c, sa, sb):
    return _per_slab(
        lambda u: u * c + pltpu.roll(u * sa, LANES - ROPE_HALF, 1) + pltpu.roll(u * sb, ROPE_HALF, 1), g)


def _gather_project(x, mix_w, w_in, w_out, rc, rsa, rsb):
    tm = 1024
    n_tiles = SEQ // tm
    arrival_of_step = (None, 0, 1, 2, 4, 5, 3, 6)

    def body(order_ref, x_ref, w_ref, win_ref, wout_ref, c_ref, sa_ref, sb_ref,
             proj_ref, hnt_ref, gin_hbm, gout_hbm, qkv_hbm,
             hn_s, w_land, wout_land, stage, sort_stage, send_sems, recv_sems, local_sems, sort_sems):
        g, i = pl.program_id(0), pl.program_id(1)
        me = _my_place()
        x_, y_, c_ = me
        sibling = (x_, y_, 1 - c_)
        chips = [(1 - x_, y_), (x_, 1 - y_), (1 - x_, 1 - y_)]

        def slab(which, place):
            idx = _flat(place)
            if which == 0:
                return w_land.at[idx]
            return wout_land.at[pl.ds(pl.multiple_of(idx * WOUT_ROWS, WOUT_ROWS), WOUT_ROWS), :]

        def remote(which, k, ref, to, src=None):
            return pltpu.make_async_remote_copy(
                src_ref=ref if src is None else src, dst_ref=ref, send_sem=send_sems.at[8 * which + k],
                recv_sem=recv_sems.at[8 * which + k], device_id=to, device_id_type=MESH)

        def copy(which, k, block, to, src=None):
            return remote(which, k, slab(which, block), to, src)

        def half(which, place, part):
            n = (D_MODEL if which == 0 else WOUT_ROWS) // 2
            if which == 0:
                return w_land.at[_flat(place), pl.ds(n * part, n), :]
            return wout_land.at[pl.ds(pl.multiple_of(_flat(place) * WOUT_ROWS + n * part, n), n), :]

        def first_copies(which):
            src = stage if which == 0 else None
            return ([copy(which, 0, me, sibling, src)]
                    + [copy(which, 1 + j, me, (*chips[j], c_), src) for j in range(2)])

        def relay(which, part):
            frm, to = (chips[1], chips[0]) if part == 0 else (chips[0], chips[1])
            return remote(which, 3 if part == 0 else 7, half(which, (*frm, c_), part), (*to, c_))

        def two_hop_half(which, part):
            return remote(which, 3 if part == 0 else 7, half(which, (*chips[2], c_), part), me)

        def pass_on(which, j):
            return copy(which, 4 + j, (*chips[j], c_), sibling)

        def arrival(which, k):
            if k == 0:
                return copy(which, 0, sibling, me)
            if k <= 2:
                return copy(which, k, (*chips[k - 1], c_), me)
            return copy(which, k, (*chips[k - 4], 1 - c_), me)

        def to_hbm(step):
            idx = order_ref[step]
            cols = pl.ds(pl.multiple_of(idx * COL_BLOCK, COL_BLOCK), COL_BLOCK)
            return pltpu.make_async_copy(w_land.at[idx], gin_hbm.at[:, cols], local_sems.at[step])

        @pl.when((g == 0) & (i == 0))
        def _():
            stage[...] = win_ref[...].astype(BF16)
            w_land[_flat(me)] = stage[...]
            wout_land[pl.ds(pl.multiple_of(_flat(me) * WOUT_ROWS, WOUT_ROWS), WOUT_ROWS), :] = (
                wout_ref[...].astype(BF16))
            for cp in first_copies(0) + first_copies(1)[:1]:
                cp.start()
            to_hbm(0).start()

        for step, k in enumerate(arrival_of_step):
            if k is None:
                continue

            @pl.when((g == step) & (i == 0))
            def _(k=k, step=step):
                if k == 3:
                    two_hop_half(0, 0).wait_recv()
                    two_hop_half(0, 1).wait_recv()
                else:
                    arrival(0, k).wait_recv()
                to_hbm(step).start()
                if 1 <= k <= 3:
                    pass_on(0, k - 1).start()
                if k == 1:
                    relay(0, 1).start()
                    for cp in first_copies(1)[1:]:
                        cp.start()
                if k == 2:
                    relay(0, 0).start()
                if k in (4, 5):
                    arrival(1, k - 3).wait_recv()
                    relay(1, 5 - k).start()

        rows = pl.ds(pl.multiple_of(i * tm, tm), tm)

        @pl.when(g == 0)
        def _():
            xf = x_ref[...]
            ms = jnp.mean(xf * xf, axis=-1, keepdims=True)
            hn = xf * lax.rsqrt(ms + NORM_EPS) * w_ref[...]
            hnt_ref[...] = hn.T.astype(BF16)
            hn_s[rows, :] = hn.astype(BF16)

        group = order_ref[g]

        def sorted_copy(tile_value):
            per = tm // SORT_RESIDUES
            cols = pl.ds(pl.multiple_of(group * COL_BLOCK, COL_BLOCK), COL_BLOCK)
            buf = i % 2

            def out_copies(tile, b):
                return [pltpu.make_async_copy(
                    sort_stage.at[b, :, r, :], qkv_hbm.at[r, pl.ds(pl.multiple_of(tile * per, per), per), cols],
                    sort_sems.at[b, r]) for r in range(SORT_RESIDUES)]

            @pl.when(i >= 2)
            def _():
                for copy in out_copies(i - 2, buf):
                    copy.wait()

            sort_stage[buf] = tile_value.reshape(per, SORT_RESIDUES, COL_BLOCK)
            for copy in out_copies(i, buf):
                copy.start()

            @pl.when(i == n_tiles - 1)
            def _():
                for copy in out_copies(i - 1, 1 - buf) + out_copies(i, buf):
                    copy.wait()

        @pl.when(group < 2)
        def _():
            rotated = _rot(_mm(hn_s[rows, :], w_land[group]), c_ref[...], sa_ref[...], sb_ref[...])
            proj_ref[...] = rotated
            sorted_copy(rotated)

        @pl.when(group == 2)
        def _():
            value = _mm(hn_s[rows, :], w_land[group])
            proj_ref[...] = value
            sorted_copy(value)

        @pl.when(group > 2)
        def _():
            proj_ref[...] = _mm(hn_s[rows, :], w_land[group])

        @pl.when((g == N_DEV - 1) & (i == n_tiles - 1))
        def _():
            pass_on(1, 0).start()
            pass_on(1, 1).start()
            two_hop_half(1, 0).wait_recv()
            two_hop_half(1, 1).wait_recv()
            pass_on(1, 2).start()
            for k in (0, 4, 5, 6):
                arrival(1, k).wait_recv()
            for which in (0, 1):
                for cp in (first_copies(which) + [relay(which, part) for part in range(2)]
                           + [pass_on(which, j) for j in range(3)]):
                    cp.wait_send()
            wout_copy = pltpu.make_async_copy(wout_land, gout_hbm, local_sems.at[N_DEV])
            wout_copy.start()
            for step in range(N_DEV):
                to_hbm(step).wait()
            wout_copy.wait()

    me = _my_place()
    x_, y_, c_ = me
    chips = [(1 - x_, y_), (x_, 1 - y_), (1 - x_, 1 - y_)]
    order = jnp.stack([_flat(p) for p in (
        me, (x_, y_, 1 - c_), (*chips[0], c_), (*chips[1], c_), (*chips[0], 1 - c_), (*chips[1], 1 - c_),
        (*chips[2], c_), (*chips[2], 1 - c_))]).astype(jnp.int32)

    first_sweep = lambda g, i, order: (jnp.where(g == 0, i, n_tiles - 1), 0)
    tab = pl.BlockSpec((tm, LANES), lambda g, i, order: (jnp.where(order[g] < 2, i, 0), 0))
    whole = lambda: pl.BlockSpec(memory_space=pltpu.VMEM)
    grid_spec = pltpu.PrefetchScalarGridSpec(
        num_scalar_prefetch=1, grid=(N_DEV, n_tiles),
        in_specs=[pl.BlockSpec((tm, D_MODEL), first_sweep),
                  pl.BlockSpec((1, D_MODEL), lambda g, i, order: (0, 0)),
                  whole(), whole(), tab, tab, tab],
        out_specs=(pl.BlockSpec((None, tm, COL_BLOCK), lambda g, i, order: (order[g], i, 0)),
                   pl.BlockSpec((D_MODEL, tm), lambda g, i, order: (0, jnp.where(g == 0, i, n_tiles - 1))),
                   pl.BlockSpec(memory_space=pl.ANY), pl.BlockSpec(memory_space=pl.ANY),
                   pl.BlockSpec(memory_space=pl.ANY)),
        scratch_shapes=[pltpu.VMEM((SEQ, D_MODEL), BF16),
                        pltpu.VMEM((N_DEV, D_MODEL, COL_BLOCK), BF16),
                        pltpu.VMEM((D_MODEL, D_MODEL), BF16),
                        pltpu.VMEM((D_MODEL, COL_BLOCK), BF16),
                        pltpu.VMEM((2, tm // SORT_RESIDUES, SORT_RESIDUES, COL_BLOCK), F32),
                        pltpu.SemaphoreType.DMA((16,)), pltpu.SemaphoreType.DMA((16,)),
                        pltpu.SemaphoreType.DMA((N_DEV + 1,)), pltpu.SemaphoreType.DMA((2, SORT_RESIDUES))])
    proj, hn_t, w_in_g, w_out_g, qkv_sorted = pl.pallas_call(
        body, name="gather_project", grid_spec=grid_spec,
        out_shape=(jax.ShapeDtypeStruct((N_DEV, SEQ, COL_BLOCK), F32), jax.ShapeDtypeStruct((D_MODEL, SEQ), BF16),
                   jax.ShapeDtypeStruct((D_MODEL, IN_COLS), BF16), jax.ShapeDtypeStruct((D_MODEL, D_MODEL), BF16),
                   jax.ShapeDtypeStruct((SORT_RESIDUES, SORT_ROWS, 3 * COL_BLOCK), F32)),
        compiler_params=_params(("arbitrary", "arbitrary")),
    )(order, x, mix_w, w_in, w_out, rc, rsa, rsb)
    return proj, hn_t, w_in_g, w_out_g, qkv_sorted.reshape(SEQ, 3 * COL_BLOCK)


SCORE_SCALE = HEAD_DIM ** -0.5
ATTN_GROUP_FWD = 32
ATTN_GROUP_BWD = 16
BLOCKS_PER_PATTERN = SEQ // ATTN_BLOCK
SORT_RESIDUES = 16
SORT_ROWS = SEQ // SORT_RESIDUES


def _write_band_bias(bias_ref):
    row = lax.broadcasted_iota(jnp.int32, (2 * ATTN_BLOCK, 2 * ATTN_BLOCK), 0) & (ATTN_BLOCK - 1)
    col = lax.broadcasted_iota(jnp.int32, (2 * ATTN_BLOCK, 2 * ATTN_BLOCK), 1)
    for pi, d in enumerate(DILATIONS):
        per = SORT_RESIDUES // d
        ahead = per * (row % (8 * d) - col % (16 * d)) + (row // (8 * d) - col // (16 * d))
        dist = ATTN_BLOCK + ahead
        bias_ref[2 * pi] = jnp.where((dist >= 0) & (dist <= ATTN_BLOCK), 0.0, NEG_BIG)
        bias_ref[2 * pi + 1] = jnp.where(ahead >= 0, 0.0, NEG_BIG)


def _head0_lanes():
    return lax.broadcasted_iota(jnp.int32, (ATTN_BLOCK, LANES), 1) < HEAD_DIM


def _stack_heads(t, h0):
    return jnp.concatenate([jnp.where(h0, t, 0.0), jnp.where(h0, 0.0, t)], axis=0).astype(BF16)


def _block_runs(i, d):
    nblk = BLOCKS_PER_PATTERN // d
    r, n = i // nblk, i % nblk
    kn = jnp.maximum(n - 1, 0)
    rows, keys = [], []
    for c in range(SORT_RESIDUES // d):
        base = SORT_ROWS * (c * d + r)
        rows.append(pl.ds(pl.multiple_of(base + 8 * d * n, 8), 8 * d))
        keys.append(pl.ds(pl.multiple_of(base + 8 * d * kn, 8), 16 * d))
    return rows, keys, (n == 0).astype(jnp.int32)


def _take(ref, runs):
    return jnp.concatenate([ref[run, :] for run in runs], axis=0)


def _put(ref, runs, value, add=False):
    at = 0
    for run in runs:
        piece = value[at:at + run.size]
        if add:
            ref[run, :] += piece
        else:
            ref[run, :] = piece
        at += run.size


def _sort_copies(src_hbm, lane_block, dst_ref, sem_ref):
    lanes = pl.ds(pl.multiple_of(LANES * lane_block, LANES), LANES)
    return [pltpu.make_async_copy(src_hbm.at[:, r, lanes], dst_ref.at[pl.ds(SORT_ROWS * r, SORT_ROWS), :],
                                  sem_ref.at[r]) for r in range(SORT_RESIDUES)]


def _unsort_copies(src_ref, dst_hbm, lane_block, sem_ref):
    lanes = pl.ds(pl.multiple_of(LANES * lane_block, LANES), LANES)
    return [pltpu.make_async_copy(src_ref.at[pl.ds(SORT_ROWS * r, SORT_ROWS), :], dst_hbm.at[:, r, lanes],
                                  sem_ref.at[r]) for r in range(SORT_RESIDUES)]


def _for_each_group(d, n_group, load, compute, store):
    def group(g, carry):
        items = [load(*_block_runs(g * n_group + u, d)) for u in range(n_group)]
        results = [compute(item) for item in items]
        for item, res in zip(items, results):
            store(item, res)
        return carry

    lax.fori_loop(0, BLOCKS_PER_PATTERN // n_group, group, 0)


def _attn_fwd_fused(qkv_sorted):
    n_pat = len(DILATIONS)
    tile2 = (2 * ATTN_BLOCK, LANES)

    def body(q_ref, k_ref, v_ref, o_hbm, lse_ref, o_slots, m_acc, l_acc, bias_ref, out_sem):
        step, n_steps = pl.program_id(0), pl.num_programs(0)
        pl.when(step == 0)(lambda: _write_band_bias(bias_ref))
        slot = step % 2
        o_acc = o_slots.at[slot]
        h0 = _head0_lanes()
        for pi, d in enumerate(DILATIONS):
            first, last = pi == 0, pi == n_pat - 1

            def load(rows, keys, which, first=first, pi=pi):
                item = dict(rows=rows, keys=keys, which=2 * pi + which)
                if not first:
                    item.update(o=_take(o_acc, rows), m=[_take(m_acc.at[h], rows) for h in range(2)],
                                l=[_take(l_acc.at[h], rows) for h in range(2)])
                return item

            def compute(item, first=first):
                kb = _take(k_ref, item["keys"]).astype(BF16)
                vb = _take(v_ref, item["keys"]).astype(BF16)
                s = _mm_nt(_stack_heads(_take(q_ref, item["rows"]) * SCORE_SCALE, h0), kb) + bias_ref[item["which"]]
                mb = jnp.max(s, axis=-1, keepdims=True)
                if first:
                    p = jnp.exp(s - mb)
                    mn = jnp.broadcast_to(mb, tile2)
                else:
                    m_old = jnp.concatenate(item["m"], axis=0)
                    mn = jnp.maximum(m_old, mb)
                    alpha = jnp.exp(m_old - mn)
                    p = jnp.exp(s - jnp.concatenate([mn, mn], axis=1))
                ls = jnp.sum(p, axis=-1, keepdims=True)
                pv = _mm(p.astype(BF16), vb)
                if first:
                    return pv, mn, jnp.broadcast_to(ls, tile2)
                o_old = jnp.concatenate([item["o"], item["o"]], axis=0)
                return alpha * o_old + pv, mn, alpha * jnp.concatenate(item["l"], axis=0) + ls

            def store(item, res, last=last):
                rows = item["rows"]
                (o0, o1), (m0, m1), (l0, l1) = ((a[:ATTN_BLOCK], a[ATTN_BLOCK:]) for a in res)
                if last:
                    _put(o_acc, rows, jnp.where(h0, o0 / l0, o1 / l1))
                    _put(lse_ref, rows, jnp.where(h0, m0 + jnp.log(l0), m1 + jnp.log(l1)))
                else:
                    _put(o_acc, rows, jnp.where(h0, o0, o1))
                    for h, (m, l) in enumerate(((m0, l0), (m1, l1))):
                        _put(m_acc.at[h], rows, m)
                        _put(l_acc.at[h], rows, l)

            _for_each_group(d, ATTN_GROUP_FWD, load, compute, store)

        def copies_out(of_step):
            return _unsort_copies(o_slots.at[of_step % 2], o_hbm, of_step, out_sem.at[of_step % 2])

        @pl.when(step > 0)
        def _():
            for copy in copies_out(step - 1):
                copy.wait()

        for copy in copies_out(step):
            copy.start()

        @pl.when(step == n_steps - 1)
        def _():
            for copy in copies_out(step):
                copy.wait()

    slab = lambda g: pl.BlockSpec((SEQ, LANES), functools.partial(lambda hp, g: (0, 4 * g + hp), g=g))
    wide = jax.ShapeDtypeStruct((SEQ, ATTN_WIDTH), F32)
    o_rows, lse = pl.pallas_call(
        body, name="attn_fwd", grid=(4,),
        out_shape=(jax.ShapeDtypeStruct((SORT_ROWS, SORT_RESIDUES, ATTN_WIDTH), F32), wide),
        in_specs=[slab(0), slab(1), slab(2)], out_specs=(pl.BlockSpec(memory_space=pl.ANY), slab(0)),
        scratch_shapes=[pltpu.VMEM((2, SEQ, LANES), F32), pltpu.VMEM((2, SEQ, LANES), F32),
                        pltpu.VMEM((2, SEQ, LANES), F32),
                        pltpu.VMEM((2 * len(DILATIONS), 2 * ATTN_BLOCK, 2 * ATTN_BLOCK), F32),
                        pltpu.SemaphoreType.DMA((2, SORT_RESIDUES))],
        compiler_params=_params(("arbitrary",)),
    )(qkv_sorted, qkv_sorted, qkv_sorted)
    return o_rows.reshape(SEQ, ATTN_WIDTH), lse


def _attn_bwd_fused(qkv_sorted, d_out, lse_sorted, delta):
    def body(q_ref, k_ref, v_ref, do_hbm, lse_ref, del_hbm, dq_hbm, dk_hbm, dv_hbm,
             in_slots, out_slots, bias_ref, in_sem, out_sem):
        step, n_steps = pl.program_id(0), pl.num_programs(0)
        slot = step % 2

        def copies_in(of_step):
            s = of_step % 2
            return [copy for j, hbm in enumerate((do_hbm, del_hbm))
                    for copy in _sort_copies(hbm, of_step, in_slots.at[s, j], in_sem.at[s, j])]

        def copies_out(of_step):
            s = of_step % 2
            return [copy for j, hbm in enumerate((dq_hbm, dk_hbm, dv_hbm))
                    for copy in _unsort_copies(out_slots.at[s, j], hbm, of_step, out_sem.at[s, j])]

        @pl.when(step == 0)
        def _():
            for copy in copies_in(step):
                copy.start()
            _write_band_bias(bias_ref)

        @pl.when(step + 1 < n_steps)
        def _():
            for copy in copies_in(step + 1):
                copy.start()

        do_s, del_s = in_slots.at[slot, 0], in_slots.at[slot, 1]
        dq_s, dk_s, dv_s = (out_slots.at[slot, j] for j in range(3))
        dk_s[...] = jnp.zeros_like(dk_s)
        dv_s[...] = jnp.zeros_like(dv_s)
        for copy in copies_in(step):
            copy.wait()
        h0 = _head0_lanes()
        for pi, d in enumerate(DILATIONS):
            first = pi == 0

            def load(rows, keys, which, pi=pi):
                return dict(rows=rows, keys=keys, q=_take(q_ref, rows), g=_take(do_s, rows),
                            lse=_take(lse_ref, rows), delta=_take(del_s, rows),
                            k=_take(k_ref, keys).astype(BF16), v=_take(v_ref, keys).astype(BF16),
                            bias=bias_ref[2 * pi + which])

            def per_head(t):
                swapped = pltpu.roll(t, HEAD_DIM, 1)
                both = jnp.concatenate([jnp.where(h0, t, swapped), jnp.where(h0, swapped, t)], axis=0)
                return jnp.concatenate([both, both], axis=1)

            def compute(item):
                q2, g2 = _stack_heads(item["q"] * SCORE_SCALE, h0), _stack_heads(item["g"], h0)
                s = _mm_nt(q2, item["k"]) + item["bias"]
                p = jnp.exp(s - per_head(item["lse"]))
                dp = _mm_nt(g2, item["v"])
                ds = (p * (dp - per_head(item["delta"]))).astype(BF16)
                dq2 = _mm(ds, item["k"])
                dq = jnp.where(h0, dq2[:ATTN_BLOCK], dq2[ATTN_BLOCK:]) * SCORE_SCALE
                return dq, _mm_tn(ds, q2), _mm_tn(p.astype(BF16), g2)

            def store(item, res, first=first):
                _put(dq_s, item["rows"], res[0], add=not first)
                _put(dk_s, item["keys"], res[1], add=True)
                _put(dv_s, item["keys"], res[2], add=True)

            _for_each_group(d, ATTN_GROUP_BWD, load, compute, store)

        @pl.when(step > 0)
        def _():
            for copy in copies_out(step - 1):
                copy.wait()

        for copy in copies_out(step):
            copy.start()

        @pl.when(step == n_steps - 1)
        def _():
            for copy in copies_out(step):
                copy.wait()

    slab = lambda g: pl.BlockSpec((SEQ, LANES), functools.partial(lambda hp, g: (0, 4 * g + hp), g=g))
    anywhere = pl.BlockSpec(memory_space=pl.ANY)
    by_residue = (SORT_ROWS, SORT_RESIDUES, ATTN_WIDTH)
    grads = pl.pallas_call(
        body, name="attn_bwd", grid=(4,), out_shape=(jax.ShapeDtypeStruct(by_residue, F32),) * 3,
        scratch_shapes=[pltpu.VMEM((2, 2, SEQ, LANES), F32), pltpu.VMEM((2, 3, SEQ, LANES), F32),
                        pltpu.VMEM((2 * len(DILATIONS), 2 * ATTN_BLOCK, 2 * ATTN_BLOCK), F32),
                        pltpu.SemaphoreType.DMA((2, 2, SORT_RESIDUES)), pltpu.SemaphoreType.DMA((2, 3, SORT_RESIDUES))],
        in_specs=[slab(0), slab(1), slab(2), anywhere, slab(0), anywhere], out_specs=(anywhere,) * 3,
        compiler_params=_params(("arbitrary",)),
    )(qkv_sorted, qkv_sorted, qkv_sorted, d_out.reshape(by_residue), lse_sorted, delta.reshape(by_residue))
    return tuple(g.reshape(SEQ, ATTN_WIDTH) for g in grads)


def _hgrn_lower_bound(lb_ref):
    r0, r1 = lb_ref[0:1, :], lb_ref[1:2, :]
    mx = jnp.maximum(r0, r1)
    e0, e1 = jnp.exp(r0 - mx), jnp.exp(r1 - mx)
    return e0 / (e0 + e1)


def _hgrn_gates(hq, hf, lb):
    sq = _sigmoid(hq)
    sg = _sigmoid(hf)
    f = lb + (1.0 - lb) * sg
    return hq * sq, sq, sg, f, 1.0 - f, jnp.log(f)


HGRN_PAIR = 4
HGRN_SEQ_BLOCK = 1024
HGRN_GROUP = 4
HGRN_ROWS = HGRN_GROUP * HGRN_CHUNK


def _hgrn_specs(reverse):
    n_blocks = SEQ // HGRN_SEQ_BLOCK
    width = HGRN_PAIR * HGRN_DIM
    blk = (lambda s: n_blocks - 1 - s) if reverse else (lambda s: s)
    cols = lambda g: pl.BlockSpec((None, HGRN_SEQ_BLOCK, width), functools.partial(lambda p, s, g: (g, blk(s), p), g=g))
    pair = pl.BlockSpec((HGRN_SEQ_BLOCK, width), lambda p, s: (blk(s), p))
    lb = pl.BlockSpec((2, width), lambda p, s: (0, p))
    states = pl.BlockSpec((HGRN_PAIR, HGRN_SEQ_BLOCK // HGRN_CHUNK, HGRN_DIM, HGRN_DIM),
                          lambda p, s: (p, blk(s), 0, 0))
    return cols, pair, lb, states


def _chunk_masks():
    ri = lax.broadcasted_iota(jnp.int32, (HGRN_ROWS, HGRN_ROWS), 0)
    ci = lax.broadcasted_iota(jnp.int32, (HGRN_ROWS, HGRN_ROWS), 1)
    same = (ri // HGRN_CHUNK) == (ci // HGRN_CHUNK)
    return same, same & (ri >= ci), same & (ri <= ci)


def _mm_select(sel, v):
    hi = v.astype(BF16)
    r1 = v - hi.astype(F32)
    mid = r1.astype(BF16)
    lo = (r1 - mid.astype(F32)).astype(BF16)
    return _mm(sel, hi) + _mm(sel, mid) + _mm(sel, lo)


def _head_cols(a, h):
    return a[:, HGRN_DIM * h:HGRN_DIM * (h + 1)]


def _hgrn_fwd(proj, lb_raw):
    t, rws = HGRN_CHUNK, HGRN_ROWS

    def body(hq_ref, hf_ref, hi_ref, lb_ref, rec_ref, st_ref, state):
        @pl.when(pl.program_id(1) == 0)
        def _():
            state[...] = jnp.zeros_like(state)

        lb = _hgrn_lower_bound(lb_ref)
        same, causal, _ = _chunk_masks()
        sel = jnp.concatenate([causal, same], axis=0).astype(BF16)

        def group(g, sts):
            rows = pl.ds(pl.multiple_of(g * rws, rws), rws)
            q, _, _, _, k, lf = _hgrn_gates(hq_ref[rows, :], hf_ref[rows, :], lb)
            sums = _mm_select(sel, lf)
            cum, last = sums[:rws], sums[rws:]
            qd = (q * jnp.exp(cum)).astype(BF16)
            ki = (k * jnp.exp(-cum)).astype(BF16)
            ke = (k * jnp.exp(last - cum)).astype(BF16)
            vb = hi_ref[rows, :].astype(BF16)
            dec = jnp.exp(last)
            new_sts, recs = [], []
            for h in range(HGRN_PAIR):
                qd_h, ke_h, vb_h = _head_cols(qd, h), _head_cols(ke, h), _head_cols(vb, h)
                att = jnp.where(causal, _mm_nt(qd_h, _head_cols(ki, h)), 0.0).astype(BF16)
                intra = _mm(att, vb_h)
                st = sts[h]
                outs = []
                for c in range(HGRN_GROUP):
                    sl = slice(c * t, (c + 1) * t)
                    st_ref[h, g * HGRN_GROUP + c] = st
                    outs.append(intra[sl] + _mm_nt(qd_h[sl], st.astype(BF16)))
                    st = st * _head_cols(dec[c * t:c * t + 1, :], h) + _mm_tn(vb_h[sl], ke_h[sl])
                new_sts.append(st)
                recs.append(jnp.concatenate(outs, axis=0))
            rec_ref[rows, :] = jnp.concatenate(recs, axis=1)
            return tuple(new_sts)

        sts = lax.fori_loop(0, HGRN_SEQ_BLOCK // rws, group, tuple(state[h] for h in range(HGRN_PAIR)), unroll=True)
        for h in range(HGRN_PAIR):
            state[h] = sts[h]

    cols, pair, lb, states = _hgrn_specs(reverse=False)
    return pl.pallas_call(
        body, name="hgrn_fwd", grid=(HGRN_HEADS // HGRN_PAIR, SEQ // HGRN_SEQ_BLOCK),
        out_shape=(jax.ShapeDtypeStruct((SEQ, HGRN_WIDTH), F32),
                   jax.ShapeDtypeStruct((HGRN_HEADS, N_CHUNKS, HGRN_DIM, HGRN_DIM), F32)),
        in_specs=[cols(4), cols(5), cols(6), lb], out_specs=(pair, states),
        scratch_shapes=[pltpu.VMEM((HGRN_PAIR, HGRN_DIM, HGRN_DIM), F32)],
        compiler_params=_params(("parallel", "arbitrary")),
    )(proj, proj, proj, lb_raw)


def _hgrn_bwd(proj, lb_raw, d_rec, states):
    t, rws = HGRN_CHUNK, HGRN_ROWS

    def body(hq_ref, hf_ref, hi_ref, lb_ref, do_ref, st_ref, dhq_ref, dhf_ref, dhi_ref, dlb_ref,
             dstate, dlb_acc):
        lb = _hgrn_lower_bound(lb_ref)
        same, causal, anti = _chunk_masks()
        sel = jnp.concatenate([causal, same], axis=0).astype(BF16)
        sel_t = jnp.concatenate([anti, same], axis=1).astype(BF16)
        @pl.when(pl.program_id(1) == 0)
        def _():
            dstate[...] = jnp.zeros_like(dstate)
            dlb_acc[...] = jnp.zeros_like(dlb_acc)

        n_groups = HGRN_SEQ_BLOCK // rws
        chunks = [slice(c * t, (c + 1) * t) for c in range(HGRN_GROUP)]

        def group(i, dsts_in):
            g = n_groups - 1 - i
            rows = pl.ds(pl.multiple_of(g * rws, rws), rws)
            hq = hq_ref[rows, :]
            q, sq, sg, f, k, lf = _hgrn_gates(hq, hf_ref[rows, :], lb)
            sums = _mm_select(sel, lf)
            cum, last = sums[:rws], sums[rws:]
            e_cum, e_inv, e_end, dec = jnp.exp(cum), jnp.exp(-cum), jnp.exp(last - cum), jnp.exp(last)
            qd, ki, ke = q * e_cum, k * e_inv, k * e_end
            qdb, kib, keb = qd.astype(BF16), ki.astype(BF16), ke.astype(BF16)
            vb = hi_ref[rows, :].astype(BF16)
            gb = do_ref[rows, :].astype(BF16)

            dsts_out, per_head = [], []
            for h in range(HGRN_PAIR):
                qdb_h, kib_h, keb_h = _head_cols(qdb, h), _head_cols(kib, h), _head_cols(keb, h)
                vb_h, gb_h = _head_cols(vb, h), _head_cols(gb, h)
                att = jnp.where(causal, _mm_nt(qdb_h, kib_h), 0.0).astype(BF16)
                datt = jnp.where(causal, _mm_nt(gb_h, vb_h), 0.0).astype(BF16)
                dv = _mm_tn(att, gb_h)
                dqd = _mm(datt, kib_h)
                dki = _mm_tn(datt, qdb_h)

                decs = [_head_cols(dec[c * t:c * t + 1, :], h) for c in range(HGRN_GROUP)]
                dsts = [None] * HGRN_GROUP
                dst = dsts_in[h]
                for c in reversed(range(HGRN_GROUP)):
                    dsts[c] = dst
                    dst = dst * decs[c] + _mm_tn(gb_h[chunks[c]], qdb_h[chunks[c]])
                dsts_out.append(dst)

                dv_x, dqd_x, dke, dlast_x = [], [], [], []
                for c, sl in enumerate(chunks):
                    st_prev = st_ref[h, g * HGRN_GROUP + c]
                    dstb = dsts[c].astype(BF16)
                    dv_x.append(_mm_nt(keb_h[sl], dstb))
                    dqd_x.append(_mm(gb_h[sl], st_prev.astype(BF16)))
                    dke.append(_mm(vb_h[sl], dstb))
                    ddec = jnp.sum(dsts[c] * st_prev, axis=0, keepdims=True)
                    dlast_x.append(jnp.broadcast_to(ddec * decs[c], (t, HGRN_DIM)))
                per_head.append((dv + jnp.concatenate(dv_x, axis=0), dqd + jnp.concatenate(dqd_x, axis=0),
                                 dki, jnp.concatenate(dke, axis=0), jnp.concatenate(dlast_x, axis=0)))
            dv, dqd, dki, dke, dlast = (jnp.concatenate(list(parts), axis=1) for parts in zip(*per_head))

            dq = dqd * e_cum
            dk = dki * e_inv + dke * e_end
            dke_ke = dke * ke
            dcum = dqd * qd - dki * ki - dke_ke
            dlf = _mm_select(sel_t, jnp.concatenate([dcum, dke_ke], axis=0)) + dlast
            df = dlf / f - dk
            dhq_ref[rows, :] = (dq * (sq * (1.0 + hq * (1.0 - sq)))).astype(BF16)
            dhf_ref[rows, :] = (df * (1.0 - lb) * (sg * (1.0 - sg))).astype(BF16)
            dhi_ref[rows, :] = dv.astype(BF16)
            dlb_acc[...] += jnp.sum(df * (1.0 - sg), axis=0, keepdims=True)
            return tuple(dsts_out)

        dsts = lax.fori_loop(0, n_groups, group, tuple(dstate[h] for h in range(HGRN_PAIR)), unroll=True)
        for h in range(HGRN_PAIR):
            dstate[h] = dsts[h]
        g0 = dlb_acc[...] * lb * (1.0 - lb)
        dlb_ref[...] = jnp.concatenate([g0, -g0], axis=0)

    cols, pair, lb_spec, st_spec = _hgrn_specs(reverse=True)
    wide = jax.ShapeDtypeStruct((SEQ, HGRN_WIDTH), BF16)
    return pl.pallas_call(
        body, name="hgrn_bwd", grid=(HGRN_HEADS // HGRN_PAIR, SEQ // HGRN_SEQ_BLOCK),
        out_shape=(wide, wide, wide, jax.ShapeDtypeStruct((2, HGRN_WIDTH), F32)),
        in_specs=[cols(4), cols(5), cols(6), lb_spec, pair, st_spec],
        out_specs=(pair, pair, pair, lb_spec),
        scratch_shapes=[pltpu.VMEM((HGRN_PAIR, HGRN_DIM, HGRN_DIM), F32),
                        pltpu.VMEM((1, HGRN_PAIR * HGRN_DIM), F32)],
        compiler_params=_params(("parallel", "arbitrary")),
    )(proj, proj, proj, lb_raw, d_rec, states)


def _group_sum(v, group):
    parts = []
    for s in range(v.shape[1] // LANES):
        slab = v[:, LANES * s:LANES * (s + 1)]
        if group == LANES:
            parts.append(jnp.broadcast_to(jnp.sum(slab, axis=-1, keepdims=True), slab.shape))
        else:
            h0 = lax.broadcasted_iota(jnp.int32, slab.shape, 1) < HEAD_DIM
            s0 = jnp.sum(jnp.where(h0, slab, 0.0), axis=-1, keepdims=True)
            s1 = jnp.sum(jnp.where(h0, 0.0, slab), axis=-1, keepdims=True)
            parts.append(jnp.where(h0, s0, s1))
    return jnp.concatenate(parts, axis=1)


def _mid(attn_o, rec, proj, x, target, w_out_g, attn_w, hgrn_w, final_w):
    tm = 512

    def branch_fwd(o, gate, w, group):
        r = lax.rsqrt(_group_sum(o * o, group) * (1.0 / group) + NORM_EPS)
        nrm = o * r
        sg = _sigmoid(gate)
        return r, nrm, sg, nrm * w * (gate * sg)

    def branch_bwd(dy, r, nrm, sg, gate, w, group):
        silu = gate * sg
        d_gate = dy * nrm * w * (sg * (1.0 + gate * (1.0 - sg)))
        d_w = jnp.sum(dy * nrm * silu, axis=0, keepdims=True)
        dn = dy * w * silu
        d_o = r * (dn - nrm * (_group_sum(dn * nrm, group) * (1.0 / group)))
        return d_o, d_gate, d_w

    def body(o_ref, rec_ref, ag_ref, hg_ref, x_ref, tgt_ref, wout_ref, aw_ref, hw_ref, fw_ref,
             dx2_ref, do_ref, delta_ref, dag_ref, drec_ref, dhg_ref, dwout_ref, dfw_ref, daw_ref, dhw_ref,
             loss_ref, dwout_acc):
        i = pl.program_id(0)

        @pl.when(i == 0)
        def _():
            dwout_acc[...] = jnp.zeros_like(dwout_acc)
            dfw_ref[...] = jnp.zeros_like(dfw_ref)
            daw_ref[...] = jnp.zeros_like(daw_ref)
            dhw_ref[...] = jnp.zeros_like(dhw_ref)
            loss_ref[...] = jnp.zeros_like(loss_ref)

        o, rc, ag, hg = o_ref[...], rec_ref[...], ag_ref[...], hg_ref[...]
        aw, hw, fw = aw_ref[...], hw_ref[...], fw_ref[...]
        ra, na, sga, ya = branch_fwd(o, ag, aw, HEAD_DIM)
        rh, nh, sgh, yh = branch_fwd(rc, hg, hw, HGRN_DIM)
        mixed = jnp.concatenate([ya, yh], axis=1).astype(BF16)
        wout = wout_ref[...]
        x2 = x_ref[...] + _mm(mixed, wout)
        rstd = lax.rsqrt(jnp.mean(x2 * x2, axis=-1, keepdims=True) + NORM_EPS)
        xn = x2 * rstd
        err = xn * fw - tgt_ref[...]
        row_loss = jnp.mean(err * err, axis=-1, keepdims=True)
        loss_ref[...] += 0.5 * jnp.sum(row_loss, axis=0, keepdims=True)
        dy = err * (1.0 / D_MODEL)
        dfw_ref[...] += jnp.sum(dy * xn, axis=0, keepdims=True)
        dxn = dy * fw
        dx2 = rstd * (dxn - xn * jnp.mean(dxn * xn, axis=-1, keepdims=True))
        dx2_ref[...] = dx2
        dx2b = dx2.astype(BF16)
        dwout_acc[...] += _mm_tn(mixed, dx2b)

        @pl.when(i == pl.num_programs(0) - 1)
        def _():
            dwout_ref[...] = dwout_acc[...].astype(BF16)

        dmixed = _mm_nt(dx2b, wout)

        d_o, d_ag, d_aw = branch_bwd(dmixed[:, :ATTN_WIDTH], ra, na, sga, ag, aw, HEAD_DIM)
        d_rec, d_hg, d_hw = branch_bwd(dmixed[:, ATTN_WIDTH:], rh, nh, sgh, hg, hw, HGRN_DIM)
        do_ref[...] = d_o
        delta_ref[...] = _group_sum(d_o * o, HEAD_DIM)
        dag_ref[...] = d_ag.astype(BF16)
        drec_ref[...] = d_rec
        dhg_ref[...] = d_hg.astype(BF16)
        daw_ref[...] += d_aw
        dhw_ref[...] += d_hw

    half = lambda: pl.BlockSpec((tm, COL_BLOCK), lambda i: (i, 0))
    full = lambda: pl.BlockSpec((tm, D_MODEL), lambda i: (i, 0))
    fixed = lambda r, c: pl.BlockSpec((r, c), lambda i: (0, 0))
    wide = jax.ShapeDtypeStruct((SEQ, COL_BLOCK), F32)
    wide_b = jax.ShapeDtypeStruct((SEQ, COL_BLOCK), BF16)
    return pl.pallas_call(
        body, name="mid", grid=(SEQ // tm,),
        out_shape=(jax.ShapeDtypeStruct((SEQ, D_MODEL), F32), wide, wide, wide_b, wide, wide_b,
                   jax.ShapeDtypeStruct((D_MODEL, D_MODEL), BF16),
                   jax.ShapeDtypeStruct((1, D_MODEL), F32), jax.ShapeDtypeStruct((1, COL_BLOCK), F32),
                   jax.ShapeDtypeStruct((1, COL_BLOCK), F32), jax.ShapeDtypeStruct((1, 1), F32)),
        scratch_shapes=[pltpu.VMEM((D_MODEL, D_MODEL), F32)],
        in_specs=[half(), half(),
                  pl.BlockSpec((None, tm, COL_BLOCK), lambda i: (3, i, 0)),
                  pl.BlockSpec((None, tm, COL_BLOCK), lambda i: (7, i, 0)),
                  full(), full(), fixed(D_MODEL, D_MODEL), fixed(1, COL_BLOCK), fixed(1, COL_BLOCK),
                  fixed(1, D_MODEL)],
        out_specs=(full(), half(), half(), half(), half(), half(), fixed(D_MODEL, D_MODEL),
                   fixed(1, D_MODEL), fixed(1, COL_BLOCK), fixed(1, COL_BLOCK), fixed(1, 1)),
        compiler_params=_params(("arbitrary",)),
    )(attn_o, rec, proj, proj, x, target, w_out_g, attn_w, hgrn_w, final_w)


def _in_proj_bwd_rows(d_groups, w_g, x, dx2, mix_w, rc, rsa, rsb):
    tm = 512

    def body(*refs):
        dg_refs = refs[:N_DEV]
        wg_ref, x_ref, dx2_ref, w_ref, c_ref, sa_ref, sb_ref, gx_ref, dpb_ref, dmw_ref = refs[N_DEV:]

        @pl.when(pl.program_id(0) == 0)
        def _():
            dmw_ref[...] = jnp.zeros_like(dmw_ref)

        parts = []
        for j in range(N_DEV):
            dp = dg_refs[j][...]
            if j < 2:
                dp = _rot_transposed(dp, c_ref[...], sa_ref[...], sb_ref[...])
            parts.append(dp.astype(BF16))
        dpb = jnp.concatenate(parts, axis=1)
        for j in range(N_DEV):
            dpb_ref[j] = parts[j]
        g = _mm_nt(dpb, wg_ref[...])
        xf = x_ref[...]
        rstd = lax.rsqrt(jnp.mean(xf * xf, axis=-1, keepdims=True) + NORM_EPS)
        xn = xf * rstd
        dmw_ref[...] += jnp.sum(g * xn, axis=0, keepdims=True)
        gw = g * w_ref[...]
        gx_ref[...] = dx2_ref[...] + rstd * (gw - xn * jnp.mean(gw * xn, axis=-1, keepdims=True))

    tile = lambda cols: pl.BlockSpec((tm, cols), lambda i: (i, 0))
    fixed = lambda r, c: pl.BlockSpec((r, c), lambda i: (0, 0))
    return pl.pallas_call(
        body, name="in_proj_bwd_rows", grid=(SEQ // tm,),
        out_shape=(jax.ShapeDtypeStruct((SEQ, D_MODEL), F32), jax.ShapeDtypeStruct((N_DEV, SEQ, COL_BLOCK), BF16),
                   jax.ShapeDtypeStruct((1, D_MODEL), F32)),
        in_specs=[tile(COL_BLOCK) for _ in range(N_DEV)] + [
            pl.BlockSpec((D_MODEL, IN_COLS), lambda i: (0, 0), pipeline_mode=pl.Buffered(1)),
            tile(D_MODEL), tile(D_MODEL), fixed(1, D_MODEL), tile(LANES), tile(LANES), tile(LANES)],
        out_specs=(tile(D_MODEL), pl.BlockSpec((N_DEV, tm, COL_BLOCK), lambda i: (0, i, 0)), fixed(1, D_MODEL)),
        compiler_params=_params(("arbitrary",)),
    )(*d_groups, w_g, x, dx2, mix_w, rc, rsa, rsb)


def _weights_exchange(hn_t, dproj_b, dwout_p, small_p):
    n_chips = N_DEV // 2
    rb = 128
    S1_IN, S1_OUT, SMALL, S2_IN, S2_OUT, VIA_IN, VIA_OUT = 0, 4, 8, 15, 17, 19, 21
    rel_of_pair = (3, 1, 2, 0)
    two_hop = n_chips - 1
    half_in, half_out = COL_BLOCK // 2, D_MODEL // 2

    def body(order_ref, hnt_ref, dp_ref, dwout_ref, small_ref, gin_ref, gout_ref, gs_ref,
             part, s1_send, s1_in, s1_out, fwd_in, fwd_out, s2_in, s2_out, via_in, via_out, land_s,
             send_sems, recv_sems):
        t = pl.program_id(0)
        me = _my_place()
        x, y, c = me
        my_chip = 2 * x + y
        sibling = (x, y, 1 - c)

        def remote(slot, src, dst, to):
            return pltpu.make_async_remote_copy(src_ref=src, dst_ref=dst, send_sem=send_sems.at[slot],
                                                recv_sem=recv_sems.at[slot], device_id=to, device_id_type=MESH)

        def s1_in_copy(pair):
            return remote(S1_IN + pair, s1_send.at[pair], s1_in.at[pair], sibling)

        def s1_out_copy(pair):
            q = my_chip ^ rel_of_pair[pair]
            return remote(S1_OUT + pair, dwout_ref.at[q, 1 - c], s1_out.at[pair], sibling)

        def s2_copies(rel):
            peer = _peer(me, 2 * rel)
            return [remote(S2_IN + rel - 1, fwd_in.at[rel - 1], s2_in.at[rel - 1], peer),
                    remote(S2_OUT + rel - 1, fwd_out.at[rel - 1], s2_out.at[rel - 1], peer)]

        def via_copies(k):
            peer = _peer(me, 2 * (2 - k))
            return [remote(VIA_IN + k, fwd_in.at[two_hop - 1, :, pl.ds(k * half_in, half_in)], via_in.at[k], peer),
                    remote(VIA_OUT + k, fwd_out.at[two_hop - 1, :, pl.ds(k * half_out, half_out)], via_out.at[k],
                           peer)]

        def small_copy(rel):
            return remote(SMALL + rel - 1, small_ref, land_s.at[rel], _peer(me, rel))

        @pl.when(t == 0)
        def _():
            land_s[0] = small_ref[...]
            for pair in range(n_chips):
                s1_out_copy(pair).start()
            for rel in range(1, N_DEV):
                small_copy(rel).start()

        part[...] = _mm(hnt_ref[...], dp_ref[...])

        def rows_loop(n_rows, fn):
            def step(b, carry):
                fn(pl.ds(pl.multiple_of(b * rb, rb), rb))
                return carry
            lax.fori_loop(0, n_rows // rb, step, 0)

        for pair, rel in enumerate(rel_of_pair):
            @pl.when(t == 2 * pair)
            def _(pair=pair):
                s1_send[pair] = part[...].astype(BF16)
                s1_in_copy(pair).start()

            @pl.when(t == 2 * pair + 1)
            def _(pair=pair, rel=rel):
                q = my_chip ^ rel
                s1_in_copy(pair).wait_recv()
                s1_out_copy(pair).wait_recv()
                dst_in = fwd_in.at[rel - 1] if rel else gin_ref
                dst_out = fwd_out.at[rel - 1] if rel else gout_ref
                passes_on = rel in (1, 2)
                if passes_on:
                    for cp in via_copies(rel - 1):
                        cp.wait_recv()

                def with_half(val, via, rows, width):
                    if not passes_on:
                        return val
                    extra = via[rel - 1, rows, :].astype(F32)
                    halves = [val[:, :width], val[:, width:]]
                    halves[rel - 1] = halves[rel - 1] + extra
                    return jnp.concatenate(halves, axis=1)

                def add_in(rows):
                    val = part[rows, :] + s1_in[pair, rows, :].astype(F32)
                    dst_in[rows, :] = with_half(val, via_in, rows, half_in).astype(dst_in.dtype)

                def add_out(rows):
                    val = dwout_ref[q, c, rows, :].astype(F32) + s1_out[pair, rows, :].astype(F32)
                    dst_out[rows, :] = with_half(val, via_out, rows, half_out).astype(dst_out.dtype)

                rows_loop(D_MODEL, add_in)
                rows_loop(WOUT_ROWS, add_out)
                if rel == two_hop:
                    for k in range(2):
                        for cp in via_copies(k):
                            cp.start()
                elif rel:
                    for cp in s2_copies(rel):
                        cp.start()

        @pl.when(t == N_DEV - 1)
        def _():
            for rel in range(1, two_hop):
                for cp in s2_copies(rel):
                    cp.wait_recv()

            def total_in(rows):
                g = gin_ref[rows, :]
                for rel in range(1, two_hop):
                    g = g + s2_in[rel - 1, rows, :].astype(F32)
                gin_ref[rows, :] = g

            def total_out(rows):
                g = gout_ref[rows, :]
                for rel in range(1, two_hop):
                    g = g + s2_out[rel - 1, rows, :].astype(F32)
                gout_ref[rows, :] = g

            rows_loop(D_MODEL, total_in)
            rows_loop(WOUT_ROWS, total_out)

            for rel in range(1, N_DEV):
                small_copy(rel).wait_recv()
            my_flat = _flat(me)
            g = land_s[my_flat ^ 0]
            for dev in range(1, N_DEV):
                g = g + land_s[my_flat ^ dev]
            gs_ref[...] = g

            for pair in range(n_chips):
                s1_in_copy(pair).wait_send()
                s1_out_copy(pair).wait_send()
            for rel in range(1, two_hop):
                for cp in s2_copies(rel) + via_copies(rel - 1):
                    cp.wait_send()
            for rel in range(1, N_DEV):
                small_copy(rel).wait_send()

    place_x, place_y, place_c = _my_place()
    my_chip = 2 * place_x + place_y
    order = jnp.stack([2 * (my_chip ^ rel) + core for rel in rel_of_pair
                       for core in (1 - place_c, place_c)]).astype(jnp.int32)

    whole = lambda: pl.BlockSpec(memory_space=pltpu.VMEM)
    in_blocks = lambda n: pltpu.VMEM((n, D_MODEL, COL_BLOCK), BF16)
    out_blocks = lambda n: pltpu.VMEM((n, WOUT_ROWS, D_MODEL), BF16)
    grid_spec = pltpu.PrefetchScalarGridSpec(
        num_scalar_prefetch=1, grid=(N_DEV,),
        in_specs=[pl.BlockSpec((D_MODEL, SEQ), lambda t, order: (0, 0), pipeline_mode=pl.Buffered(1)),
                  pl.BlockSpec((None, SEQ, COL_BLOCK), lambda t, order: (order[t], 0, 0)), whole(), whole()],
        out_specs=(whole(), whole(), whole()),
        scratch_shapes=[pltpu.VMEM((D_MODEL, COL_BLOCK), F32), in_blocks(n_chips), in_blocks(n_chips),
                        out_blocks(n_chips), in_blocks(n_chips - 1), out_blocks(n_chips - 1),
                        in_blocks(n_chips - 2), out_blocks(n_chips - 2),
                        pltpu.VMEM((2, D_MODEL, half_in), BF16), pltpu.VMEM((2, WOUT_ROWS, half_out), BF16),
                        pltpu.VMEM((N_DEV, SMALL_ROWS, LANES), F32),
                        pltpu.SemaphoreType.DMA((23,)), pltpu.SemaphoreType.DMA((23,))])
    return pl.pallas_call(
        body, name="weights_exchange", grid_spec=grid_spec,
        out_shape=(jax.ShapeDtypeStruct((D_MODEL, COL_BLOCK), F32), jax.ShapeDtypeStruct((WOUT_ROWS, D_MODEL), F32),
                   jax.ShapeDtypeStruct((SMALL_ROWS, LANES), F32)),
        compiler_params=_params(("arbitrary",)),
    )(order, hn_t, dproj_b, dwout_p.reshape(n_chips, 2, WOUT_ROWS, D_MODEL), small_p)


def _adamw(w, g, m, v):
    m = ADAM_B1 * m + (1.0 - ADAM_B1) * g
    v = ADAM_B2 * v + (1.0 - ADAM_B2) * (g * g)
    m_hat = m / (1.0 - ADAM_B1 ** ADAM_STEP)
    v_hat = v / (1.0 - ADAM_B2 ** ADAM_STEP)
    delta = -ADAM_LR * (m_hat / (jnp.sqrt(v_hat) + ADAM_EPS) + ADAM_WD * w)
    return delta, m, v


def _adamw_update(grads, weights, m_old, v_old):
    rb = 256

    def body(*refs):
        g_refs, w_refs, m_refs, v_refs = refs[0:3], refs[3:6], refs[6:9], refs[9:12]
        d_refs, nm_refs, nv_refs = refs[12:15], refs[15:18], refs[18:21]
        for k in range(3):
            n_rows = g_refs[k].shape[0]
            step_rows = min(rb, n_rows)

            def step(b, carry, k=k, step_rows=step_rows):
                rows = pl.ds(pl.multiple_of(b * step_rows, 8), step_rows)
                delta, nm, nv = _adamw(w_refs[k][rows, :], g_refs[k][rows, :], m_refs[k][rows, :], v_refs[k][rows, :])
                d_refs[k][rows, :] = delta
                nm_refs[k][rows, :] = nm
                nv_refs[k][rows, :] = nv
                return carry

            lax.fori_loop(0, n_rows // step_rows, step, 0)

    shapes = tuple(jax.ShapeDtypeStruct(g.shape, F32) for g in grads)
    vm = lambda: pl.BlockSpec(memory_space=pltpu.VMEM)
    outs = pl.pallas_call(
        body, name="adamw_update", out_shape=shapes * 3,
        in_specs=[vm() for _ in range(12)], out_specs=tuple(vm() for _ in range(9)),
        compiler_params=_params(),
    )(*grads, *weights, *m_old, *v_old)
    return outs[0:3], outs[3:6], outs[6:9]


def _pack_small(mix, attn, hgrn, lb, final, loss=None):
    def rows8(a):
        a = a.reshape(-1, LANES)
        return jnp.pad(a, ((0, 8 - a.shape[0]), (0, 0)))
    last = jnp.zeros((8, LANES), F32) if loss is None else jnp.pad(loss.reshape(1, 1), ((0, 7), (0, LANES - 1)))
    return jnp.concatenate([rows8(mix), rows8(attn), rows8(hgrn), rows8(lb), rows8(final), last], axis=0)


def _unpack_small(slab):
    return (slab[ROW_MIX:ROW_MIX + 8].reshape(1, D_MODEL), slab[ROW_ATTN:ROW_ATTN + 4].reshape(1, ATTN_WIDTH),
            slab[ROW_HGRN:ROW_HGRN + 4].reshape(1, HGRN_WIDTH), slab[ROW_LB:ROW_LB + 8].reshape(2, HGRN_WIDTH),
            slab[ROW_FINAL:ROW_FINAL + 8].reshape(D_MODEL))


def _rope(pos_row):
    j = np.arange(ROPE_ROWS)
    inv = np.where(j < ROPE_HALF, ROPE_THETA ** (-(j % ROPE_HALF) * (2.0 / ROPE_DIMS)), 0.0)
    e = np.arange(LANES) % HEAD_DIM
    hit = (j[:, None] == (e % ROPE_HALF)[None, :]) & (j[:, None] < ROPE_HALF)
    sel = np.stack([hit & (e < ROPE_DIMS), hit & (e >= ROPE_HALF) & (e < ROPE_DIMS),
                    -1.0 * (hit & (e < ROPE_HALF))]).astype(np.float32)
    return _rope_tables(pos_row, jnp.asarray(inv.astype(np.float32).reshape(ROPE_ROWS, 1)),
                        jnp.asarray(sel, dtype=BF16))


def _local_step(x, proj, qkv_sorted, w_in_g, w_out_g, tables, mix_w, attn_w, hgrn_w, lb_raw, final_w, target):
    rc, rsa, rsb = tables
    attn_o, lse = _attn_fwd_fused(qkv_sorted)
    rec, states = _hgrn_fwd(proj, lb_raw)

    (dx2, d_o, delta, d_ag, d_rec, d_hg, dwout_p, d_final, d_attn_w, d_hgrn_w, loss) = _mid(
        attn_o, rec, proj, x, target, w_out_g, attn_w, hgrn_w, final_w.reshape(1, D_MODEL))

    dqkv = _attn_bwd_fused(qkv_sorted, d_o, lse, delta)
    d_hq, d_hf, d_hi, d_lb = _hgrn_bwd(proj, lb_raw, d_rec, states)

    grad_x, dproj_b, d_mix = _in_proj_bwd_rows(
        (dqkv[0], dqkv[1], dqkv[2], d_ag, d_hq, d_hf, d_hi, d_hg), w_in_g, x, dx2, mix_w, rc, rsa, rsb)
    small_p = _pack_small(d_mix, d_attn_w, d_hgrn_w, d_lb, d_final, loss)
    return grad_x, dproj_b, dwout_p, small_p


def kernel(x, positions, w_in, w_out, mix_norm_w, attn_out_norm_w, hgrn_out_norm_w, hgrn_lb_raw, final_norm_w, loss_target, m_w_in, m_w_out, m_mix_norm_w, m_attn_out_norm_w, m_hgrn_out_norm_w, m_hgrn_lb_raw, m_final_norm_w, v_w_in, v_w_out, v_mix_norm_w, v_attn_out_norm_w, v_hgrn_out_norm_w, v_hgrn_lb_raw, v_final_norm_w):
    tables = _rope(positions)
    proj, hn_t, w_in_g, w_out_g, qkv_sorted = _gather_project(x[0], mix_norm_w, w_in[0], w_out[0], *tables)
    grad_x, dproj_b, dwout_p, small_p = _local_step(
        x[0], proj, qkv_sorted, w_in_g, w_out_g, tables, mix_norm_w, attn_out_norm_w, hgrn_out_norm_w,
        hgrn_lb_raw, final_norm_w, loss_target[0])
    g_in, g_out, g_s = _weights_exchange(hn_t, dproj_b, dwout_p, small_p)

    w_s = _pack_small(mix_norm_w, attn_out_norm_w, hgrn_out_norm_w, hgrn_lb_raw, final_norm_w)
    m_s = _pack_small(m_mix_norm_w, m_attn_out_norm_w, m_hgrn_out_norm_w, m_hgrn_lb_raw, m_final_norm_w)
    v_s = _pack_small(v_mix_norm_w, v_attn_out_norm_w, v_hgrn_out_norm_w, v_hgrn_lb_raw, v_final_norm_w)
    (d_in, d_out, d_s), (nm_in, nm_out, nm_s), (nv_in, nv_out, nv_s) = _adamw_update(
        (g_in, g_out, g_s), (w_in[0], w_out[0], w_s), (m_w_in[0], m_w_out[0], m_s), (v_w_in[0], v_w_out[0], v_s))

    loss = g_s[ROW_LOSS, 0]
    return (loss, grad_x[None], g_in[None], g_out[None], *_unpack_small(g_s),
            d_in[None], d_out[None], *_unpack_small(d_s),
            nm_in[None], nm_out[None], *_unpack_small(nm_s),
            nv_in[None], nv_out[None], *_unpack_small(nv_s))
```

```python
import functools

import jax
import jax.numpy as jnp
import numpy as np
from jax import lax
from jax.experimental import pallas as pl
from jax.experimental.pallas import tpu as pltpu

F32 = jnp.float32
BF16 = jnp.bfloat16

SEQ = 4096
D_MODEL = 1024
ATTN_WIDTH = 512
HGRN_WIDTH = 512
HEAD_DIM = 64
HGRN_HEADS = 4
HGRN_DIM = 128
HGRN_CHUNK = 64
N_CHUNKS = SEQ // HGRN_CHUNK
IN_COLS = 4096
COL_BLOCK = 512
N_DEV = 8
WOUT_ROWS = D_MODEL // N_DEV
ATTN_BLOCK = 128
DILATIONS = (1, 4, 16)
ROPE_THETA = 500000.0
ROPE_DIMS = 16
ROPE_HALF = 8
NORM_EPS = 1e-6
NEG_BIG = -1e30
LANES = 128

ADAM_LR = 0.001
ADAM_B1 = 0.9
ADAM_B2 = 0.999
ADAM_EPS = 1e-08
ADAM_WD = 0.01
ADAM_STEP = 10

SMALL_ROWS = 48
ROW_MIX, ROW_ATTN, ROW_HGRN, ROW_LB, ROW_FINAL, ROW_LOSS = 0, 8, 16, 24, 32, 40

VMEM_LIMIT = 56 * 1024 * 1024
MESH = pl.DeviceIdType.MESH


def _mm(a, b):
    return lax.dot_general(a, b, (((1,), (0,)), ((), ())), preferred_element_type=F32)


def _mm_nt(a, b):
    return lax.dot_general(a, b, (((1,), (1,)), ((), ())), preferred_element_type=F32)


def _mm_tn(a, b):
    return lax.dot_general(a, b, (((0,), (0,)), ((), ())), preferred_element_type=F32)


def _mm_exact(a, b):
    return lax.dot_general(a, b, (((1,), (0,)), ((), ())), preferred_element_type=F32,
                           precision=lax.Precision.HIGHEST)


def _sigmoid(v):
    return 1.0 / (1.0 + jnp.exp(-v))


def _params(sem=None, **kw):
    return pltpu.CompilerParams(dimension_semantics=sem, vmem_limit_bytes=VMEM_LIMIT, **kw)


def _my_place():
    return lax.axis_index("x"), lax.axis_index("y"), lax.axis_index("c")


def _peer(place, rel):
    x, y, c = place
    return (x ^ ((rel >> 2) & 1), y ^ ((rel >> 1) & 1), c ^ (rel & 1))


def _flat(place):
    x, y, c = place
    return 4 * x + 2 * y + c


ROPE_ROWS = 16


def _rope_tables(pos_row, inv_freq_col, selectors):
    def body(pos_ref, invf_ref, sel_ref, c_ref, sa_ref, sb_ref):
        ang = pos_ref[...].astype(F32) * invf_ref[...]
        cos, sin = jnp.cos(ang), jnp.sin(ang)

        def spread(v, sel):
            hi = v.astype(BF16)
            r1 = v - hi.astype(F32)
            mid = r1.astype(BF16)
            lo = (r1 - mid.astype(F32)).astype(BF16)
            return _mm_tn(hi, sel) + _mm_tn(mid, sel) + _mm_tn(lo, sel)

        e = lax.broadcasted_iota(jnp.int32, (1, LANES), 1) & (HEAD_DIM - 1)
        c_ref[...] = spread(cos, sel_ref[0]) + jnp.where(e < ROPE_DIMS, 0.0, 1.0)
        sa_ref[...] = spread(sin, sel_ref[1])
        sb_ref[...] = spread(sin, sel_ref[2])

    tab = jax.ShapeDtypeStruct((SEQ, LANES), F32)
    vm = lambda: pl.BlockSpec(memory_space=pltpu.VMEM)
    return pl.pallas_call(
        body, name="rope_tables", out_shape=(tab, tab, tab),
        in_specs=[vm(), vm(), vm()], out_specs=(vm(), vm(), vm()), compiler_params=_params(),
    )(pos_row, inv_freq_col, selectors)


def _per_slab(fn, t):
    return jnp.concatenate([fn(t[:, LANES * s:LANES * (s + 1)]) for s in range(t.shape[1] // LANES)], axis=1)


def _rot(t, c, sa, sb):
    return _per_slab(lambda u: u * c + pltpu.roll(u, ROPE_HALF, 1) * sa + pltpu.roll(u, LANES - ROPE_HALF, 1) * sb, t)


def _rot_transposed(g, c, sa, sb):
    return _per_slab(
        lambda u: u * c + pltpu.roll(u * sa, LANES - ROPE_HALF, 1) + pltpu.roll(u * sb, ROPE_HALF, 1), g)


def _gather_project(x, mix_w, w_in, w_out, rc, rsa, rsb):
    tm = 1024
    n_tiles = SEQ // tm
    arrival_of_step = (None, 0, 1, 2, 4, 5, 3, 6)

    def body(order_ref, x_ref, w_ref, win_ref, wout_ref, c_ref, sa_ref, sb_ref,
             proj_ref, hnt_ref, gin_hbm, gout_hbm, qkv_hbm,
             hn_s, w_land, wout_land, stage, sort_stage, send_sems, recv_sems, local_sems, sort_sems):
        g, i = pl.program_id(0), pl.program_id(1)
        me = _my_place()
        x_, y_, c_ = me
        sibling = (x_, y_, 1 - c_)
        chips = [(1 - x_, y_), (x_, 1 - y_), (1 - x_, 1 - y_)]

        def slab(which, place):
            idx = _flat(place)
            if which == 0:
                return w_land.at[idx]
            return wout_land.at[pl.ds(pl.multiple_of(idx * WOUT_ROWS, WOUT_ROWS), WOUT_ROWS), :]

        def remote(which, k, ref, to, src=None):
            return pltpu.make_async_remote_copy(
                src_ref=ref if src is None else src, dst_ref=ref, send_sem=send_sems.at[8 * which + k],
                recv_sem=recv_sems.at[8 * which + k], device_id=to, device_id_type=MESH)

        def copy(which, k, block, to, src=None):
            return remote(which, k, slab(which, block), to, src)

        def half(which, place, part):
            n = (D_MODEL if which == 0 else WOUT_ROWS) // 2
            if which == 0:
                return w_land.at[_flat(place), pl.ds(n * part, n), :]
            return wout_land.at[pl.ds(pl.multiple_of(_flat(place) * WOUT_ROWS + n * part, n), n), :]

        def first_copies(which):
            src = stage if which == 0 else None
            return ([copy(which, 0, me, sibling, src)]
                    + [copy(which, 1 + j, me, (*chips[j], c_), src) for j in range(2)])

        def relay(which, part):
            frm, to = (chips[1], chips[0]) if part == 0 else (chips[0], chips[1])
            return remote(which, 3 if part == 0 else 7, half(which, (*frm, c_), part), (*to, c_))

        def two_hop_half(which, part):
            return remote(which, 3 if part == 0 else 7, half(which, (*chips[2], c_), part), me)

        def pass_on(which, j):
            return copy(which, 4 + j, (*chips[j], c_), sibling)

        def arrival(which, k):
            if k == 0:
                return copy(which, 0, sibling, me)
            if k <= 2:
                return copy(which, k, (*chips[k - 1], c_), me)
            return copy(which, k, (*chips[k - 4], 1 - c_), me)

        def to_hbm(step):
            idx = order_ref[step]
            cols = pl.ds(pl.multiple_of(idx * COL_BLOCK, COL_BLOCK), COL_BLOCK)
            return pltpu.make_async_copy(w_land.at[idx], gin_hbm.at[:, cols], local_sems.at[step])

        @pl.when((g == 0) & (i == 0))
        def _():
            stage[...] = win_ref[...].astype(BF16)
            w_land[_flat(me)] = stage[...]
            wout_land[pl.ds(pl.multiple_of(_flat(me) * WOUT_ROWS, WOUT_ROWS), WOUT_ROWS), :] = (
                wout_ref[...].astype(BF16))
            for cp in first_copies(0) + first_copies(1)[:1]:
                cp.start()
            to_hbm(0).start()

        for step, k in enumerate(arrival_of_step):
            if k is None:
                continue

            @pl.when((g == step) & (i == 0))
            def _(k=k, step=step):
                if k == 3:
                    two_hop_half(0, 0).wait_recv()
                    two_hop_half(0, 1).wait_recv()
                else:
                    arrival(0, k).wait_recv()
                to_hbm(step).start()
                if 1 <= k <= 3:
                    pass_on(0, k - 1).start()
                if k == 1:
                    relay(0, 1).start()
                    for cp in first_copies(1)[1:]:
                        cp.start()
                if k == 2:
                    relay(0, 0).start()
                if k in (4, 5):
                    arrival(1, k - 3).wait_recv()
                    relay(1, 5 - k).start()

        rows = pl.ds(pl.multiple_of(i * tm, tm), tm)

        @pl.when(g == 0)
        def _():
            xf = x_ref[...]
            ms = jnp.mean(xf * xf, axis=-1, keepdims=True)
            hn = xf * lax.rsqrt(ms + NORM_EPS) * w_ref[...]
            hnt_ref[...] = hn.T.astype(BF16)
            hn_s[rows, :] = hn.astype(BF16)

        group = order_ref[g]

        def sorted_copy(tile_value):
            per = tm // SORT_RESIDUES
            cols = pl.ds(pl.multiple_of(group * COL_BLOCK, COL_BLOCK), COL_BLOCK)
            buf = i % 2

            def out_copies(tile, b):
                return [pltpu.make_async_copy(
                    sort_stage.at[b, :, r, :], qkv_hbm.at[r, pl.ds(pl.multiple_of(tile * per, per), per), cols],
                    sort_sems.at[b, r]) for r in range(SORT_RESIDUES)]

            @pl.when(i >= 2)
            def _():
                for copy in out_copies(i - 2, buf):
                    copy.wait()

            sort_stage[buf] = tile_value.reshape(per, SORT_RESIDUES, COL_BLOCK)
            for copy in out_copies(i, buf):
                copy.start()

            @pl.when(i == n_tiles - 1)
            def _():
                for copy in out_copies(i - 1, 1 - buf) + out_copies(i, buf):
                    copy.wait()

        @pl.when(group < 2)
        def _():
            rotated = _rot(_mm(hn_s[rows, :], w_land[group]), c_ref[...], sa_ref[...], sb_ref[...])
            proj_ref[...] = rotated
            sorted_copy(rotated)

        @pl.when(group == 2)
        def _():
            value = _mm(hn_s[rows, :], w_land[group])
            proj_ref[...] = value
            sorted_copy(value)

        @pl.when(group > 2)
        def _():
            proj_ref[...] = _mm(hn_s[rows, :], w_land[group])

        @pl.when((g == N_DEV - 1) & (i == n_tiles - 1))
        def _():
            pass_on(1, 0).start()
            pass_on(1, 1).start()
            two_hop_half(1, 0).wait_recv()
            two_hop_half(1, 1).wait_recv()
            pass_on(1, 2).start()
            for k in (0, 4, 5, 6):
                arrival(1, k).wait_recv()
            for which in (0, 1):
                for cp in (first_copies(which) + [relay(which, part) for part in range(2)]
                           + [pass_on(which, j) for j in range(3)]):
                    cp.wait_send()
            wout_copy = pltpu.make_async_copy(wout_land, gout_hbm, local_sems.at[N_DEV])
            wout_copy.start()
            for step in range(N_DEV):
                to_hbm(step).wait()
            wout_copy.wait()

    me = _my_place()
    x_, y_, c_ = me
    chips = [(1 - x_, y_), (x_, 1 - y_), (1 - x_, 1 - y_)]
    order = jnp.stack([_flat(p) for p in (
        me, (x_, y_, 1 - c_), (*chips[0], c_), (*chips[1], c_), (*chips[0], 1 - c_), (*chips[1], 1 - c_),
        (*chips[2], c_), (*chips[2], 1 - c_))]).astype(jnp.int32)

    first_sweep = lambda g, i, order: (jnp.where(g == 0, i, n_tiles - 1), 0)
    tab = pl.BlockSpec((tm, LANES), lambda g, i, order: (jnp.where(order[g] < 2, i, 0), 0))
    whole = lambda: pl.BlockSpec(memory_space=pltpu.VMEM)
    grid_spec = pltpu.PrefetchScalarGridSpec(
        num_scalar_prefetch=1, grid=(N_DEV, n_tiles),
        in_specs=[pl.BlockSpec((tm, D_MODEL), first_sweep),
                  pl.BlockSpec((1, D_MODEL), lambda g, i, order: (0, 0)),
                  whole(), whole(), tab, tab, tab],
        out_specs=(pl.BlockSpec((None, tm, COL_BLOCK), lambda g, i, order: (order[g], i, 0)),
                   pl.BlockSpec((D_MODEL, tm), lambda g, i, order: (0, jnp.where(g == 0, i, n_tiles - 1))),
                   pl.BlockSpec(memory_space=pl.ANY), pl.BlockSpec(memory_space=pl.ANY),
                   pl.BlockSpec(memory_space=pl.ANY)),
        scratch_shapes=[pltpu.VMEM((SEQ, D_MODEL), BF16),
                        pltpu.VMEM((N_DEV, D_MODEL, COL_BLOCK), BF16),
                        pltpu.VMEM((D_MODEL, D_MODEL), BF16),
                        pltpu.VMEM((D_MODEL, COL_BLOCK), BF16),
                        pltpu.VMEM((2, tm // SORT_RESIDUES, SORT_RESIDUES, COL_BLOCK), F32),
                        pltpu.SemaphoreType.DMA((16,)), pltpu.SemaphoreType.DMA((16,)),
                        pltpu.SemaphoreType.DMA((N_DEV + 1,)), pltpu.SemaphoreType.DMA((2, SORT_RESIDUES))])
    proj, hn_t, w_in_g, w_out_g, qkv_sorted = pl.pallas_call(
        body, name="gather_project", grid_spec=grid_spec,
        out_shape=(jax.ShapeDtypeStruct((N_DEV, SEQ, COL_BLOCK), F32), jax.ShapeDtypeStruct((D_MODEL, SEQ), BF16),
                   jax.ShapeDtypeStruct((D_MODEL, IN_COLS), BF16), jax.ShapeDtypeStruct((D_MODEL, D_MODEL), BF16),
                   jax.ShapeDtypeStruct((SORT_RESIDUES, SORT_ROWS, 3 * COL_BLOCK), F32)),
        compiler_params=_params(("arbitrary", "arbitrary")),
    )(order, x, mix_w, w_in, w_out, rc, rsa, rsb)
    return proj, hn_t, w_in_g, w_out_g, qkv_sorted.reshape(SEQ, 3 * COL_BLOCK)


SCORE_SCALE = HEAD_DIM ** -0.5
ATTN_GROUP_FWD = 32
ATTN_GROUP_BWD = 16
BLOCKS_PER_PATTERN = SEQ // ATTN_BLOCK
SORT_RESIDUES = 16
SORT_ROWS = SEQ // SORT_RESIDUES


def _write_band_bias(bias_ref):
    row = lax.broadcasted_iota(jnp.int32, (2 * ATTN_BLOCK, 2 * ATTN_BLOCK), 0) & (ATTN_BLOCK - 1)
    col = lax.broadcasted_iota(jnp.int32, (2 * ATTN_BLOCK, 2 * ATTN_BLOCK), 1)
    for pi, d in enumerate(DILATIONS):
        per = SORT_RESIDUES // d
        ahead = per * (row % (8 * d) - col % (16 * d)) + (row // (8 * d) - col // (16 * d))
        dist = ATTN_BLOCK + ahead
        bias_ref[2 * pi] = jnp.where((dist >= 0) & (dist <= ATTN_BLOCK), 0.0, NEG_BIG)
        bias_ref[2 * pi + 1] = jnp.where(ahead >= 0, 0.0, NEG_BIG)


def _head0_lanes():
    return lax.broadcasted_iota(jnp.int32, (ATTN_BLOCK, LANES), 1) < HEAD_DIM


def _stack_heads(t, h0):
    return jnp.concatenate([jnp.where(h0, t, 0.0), jnp.where(h0, 0.0, t)], axis=0).astype(BF16)


def _block_runs(i, d):
    nblk = BLOCKS_PER_PATTERN // d
    r, n = i // nblk, i % nblk
    kn = jnp.maximum(n - 1, 0)
    rows, keys = [], []
    for c in range(SORT_RESIDUES // d):
        base = SORT_ROWS * (c * d + r)
        rows.append(pl.ds(pl.multiple_of(base + 8 * d * n, 8), 8 * d))
        keys.append(pl.ds(pl.multiple_of(base + 8 * d * kn, 8), 16 * d))
    return rows, keys, (n == 0).astype(jnp.int32)


def _take(ref, runs):
    return jnp.concatenate([ref[run, :] for run in runs], axis=0)


def _put(ref, runs, value, add=False):
    at = 0
    for run in runs:
        piece = value[at:at + run.size]
        if add:
            ref[run, :] += piece
        else:
            ref[run, :] = piece
        at += run.size


def _sort_copies(src_hbm, lane_block, dst_ref, sem_ref):
    lanes = pl.ds(pl.multiple_of(LANES * lane_block, LANES), LANES)
    return [pltpu.make_async_copy(src_hbm.at[:, r, lanes], dst_ref.at[pl.ds(SORT_ROWS * r, SORT_ROWS), :],
                                  sem_ref.at[r]) for r in range(SORT_RESIDUES)]


def _unsort_copies(src_ref, dst_hbm, lane_block, sem_ref):
    lanes = pl.ds(pl.multiple_of(LANES * lane_block, LANES), LANES)
    return [pltpu.make_async_copy(src_ref.at[pl.ds(SORT_ROWS * r, SORT_ROWS), :], dst_hbm.at[:, r, lanes],
                                  sem_ref.at[r]) for r in range(SORT_RESIDUES)]


def _for_each_group(d, n_group, load, compute, store):
    def group(g, carry):
        items = [load(*_block_runs(g * n_group + u, d)) for u in range(n_group)]
        results = [compute(item) for item in items]
        for item, res in zip(items, results):
            store(item, res)
        return carry

    lax.fori_loop(0, BLOCKS_PER_PATTERN // n_group, group, 0)


def _attn_fwd_fused(qkv_sorted):
    n_pat = len(DILATIONS)
    tile2 = (2 * ATTN_BLOCK, LANES)

    def body(q_ref, k_ref, v_ref, o_hbm, lse_ref, o_slots, m_acc, l_acc, bias_ref, out_sem):
        step, n_steps = pl.program_id(0), pl.num_programs(0)
        pl.when(step == 0)(lambda: _write_band_bias(bias_ref))
        slot = step % 2
        o_acc = o_slots.at[slot]
        h0 = _head0_lanes()
        for pi, d in enumerate(DILATIONS):
            first, last = pi == 0, pi == n_pat - 1

            def load(rows, keys, which, first=first, pi=pi):
                item = dict(rows=rows, keys=keys, which=2 * pi + which)
                if not first:
                    item.update(o=_take(o_acc, rows), m=[_take(m_acc.at[h], rows) for h in range(2)],
                                l=[_take(l_acc.at[h], rows) for h in range(2)])
                return item

            def compute(item, first=first):
                kb = _take(k_ref, item["keys"]).astype(BF16)
                vb = _take(v_ref, item["keys"]).astype(BF16)
                s = _mm_nt(_stack_heads(_take(q_ref, item["rows"]) * SCORE_SCALE, h0), kb) + bias_ref[item["which"]]
                mb = jnp.max(s, axis=-1, keepdims=True)
                if first:
                    p = jnp.exp(s - mb)
                    mn = jnp.broadcast_to(mb, tile2)
                else:
                    m_old = jnp.concatenate(item["m"], axis=0)
                    mn = jnp.maximum(m_old, mb)
                    alpha = jnp.exp(m_old - mn)
                    p = jnp.exp(s - jnp.concatenate([mn, mn], axis=1))
                ls = jnp.sum(p, axis=-1, keepdims=True)
                pv = _mm(p.astype(BF16), vb)
                if first:
                    return pv, mn, jnp.broadcast_to(ls, tile2)
                o_old = jnp.concatenate([item["o"], item["o"]], axis=0)
                return alpha * o_old + pv, mn, alpha * jnp.concatenate(item["l"], axis=0) + ls

            def store(item, res, last=last):
                rows = item["rows"]
                (o0, o1), (m0, m1), (l0, l1) = ((a[:ATTN_BLOCK], a[ATTN_BLOCK:]) for a in res)
                if last:
                    _put(o_acc, rows, jnp.where(h0, o0 / l0, o1 / l1))
                    _put(lse_ref, rows, jnp.where(h0, m0 + jnp.log(l0), m1 + jnp.log(l1)))
                else:
                    _put(o_acc, rows, jnp.where(h0, o0, o1))
                    for h, (m, l) in enumerate(((m0, l0), (m1, l1))):
                        _put(m_acc.at[h], rows, m)
                        _put(l_acc.at[h], rows, l)

            _for_each_group(d, ATTN_GROUP_FWD, load, compute, store)

        def copies_out(of_step):
            return _unsort_copies(o_slots.at[of_step % 2], o_hbm, of_step, out_sem.at[of_step % 2])

        @pl.when(step > 0)
        def _():
            for copy in copies_out(step - 1):
                copy.wait()

        for copy in copies_out(step):
            copy.start()

        @pl.when(step == n_steps - 1)
        def _():
            for copy in copies_out(step):
                copy.wait()

    slab = lambda g: pl.BlockSpec((SEQ, LANES), functools.partial(lambda hp, g: (0, 4 * g + hp), g=g))
    wide = jax.ShapeDtypeStruct((SEQ, ATTN_WIDTH), F32)
    o_rows, lse = pl.pallas_call(
        body, name="attn_fwd", grid=(4,),
        out_shape=(jax.ShapeDtypeStruct((SORT_ROWS, SORT_RESIDUES, ATTN_WIDTH), F32), wide),
        in_specs=[slab(0), slab(1), slab(2)], out_specs=(pl.BlockSpec(memory_space=pl.ANY), slab(0)),
        scratch_shapes=[pltpu.VMEM((2, SEQ, LANES), F32), pltpu.VMEM((2, SEQ, LANES), F32),
                        pltpu.VMEM((2, SEQ, LANES), F32),
                        pltpu.VMEM((2 * len(DILATIONS), 2 * ATTN_BLOCK, 2 * ATTN_BLOCK), F32),
                        pltpu.SemaphoreType.DMA((2, SORT_RESIDUES))],
        compiler_params=_params(("arbitrary",)),
    )(qkv_sorted, qkv_sorted, qkv_sorted)
    return o_rows.reshape(SEQ, ATTN_WIDTH), lse


def _attn_bwd_fused(qkv_sorted, d_out, lse_sorted, delta):
    def body(q_ref, k_ref, v_ref, do_hbm, lse_ref, del_hbm, dq_hbm, dk_hbm, dv_hbm,
             in_slots, out_slots, bias_ref, in_sem, out_sem):
        step, n_steps = pl.program_id(0), pl.num_programs(0)
        slot = step % 2

        def copies_in(of_step):
            s = of_step % 2
            return [copy for j, hbm in enumerate((do_hbm, del_hbm))
                    for copy in _sort_copies(hbm, of_step, in_slots.at[s, j], in_sem.at[s, j])]

        def copies_out(of_step):
            s = of_step % 2
            return [copy for j, hbm in enumerate((dq_hbm, dk_hbm, dv_hbm))
                    for copy in _unsort_copies(out_slots.at[s, j], hbm, of_step, out_sem.at[s, j])]

        @pl.when(step == 0)
        def _():
            for copy in copies_in(step):
                copy.start()
            _write_band_bias(bias_ref)

        @pl.when(step + 1 < n_steps)
        def _():
            for copy in copies_in(step + 1):
                copy.start()

        do_s, del_s = in_slots.at[slot, 0], in_slots.at[slot, 1]
        dq_s, dk_s, dv_s = (out_slots.at[slot, j] for j in range(3))
        dk_s[...] = jnp.zeros_like(dk_s)
        dv_s[...] = jnp.zeros_like(dv_s)
        for copy in copies_in(step):
            copy.wait()
        h0 = _head0_lanes()
        for pi, d in enumerate(DILATIONS):
            first = pi == 0

            def load(rows, keys, which, pi=pi):
                return dict(rows=rows, keys=keys, q=_take(q_ref, rows), g=_take(do_s, rows),
                            lse=_take(lse_ref, rows), delta=_take(del_s, rows),
                            k=_take(k_ref, keys).astype(BF16), v=_take(v_ref, keys).astype(BF16),
                            bias=bias_ref[2 * pi + which])

            def per_head(t):
                swapped = pltpu.roll(t, HEAD_DIM, 1)
                both = jnp.concatenate([jnp.where(h0, t, swapped), jnp.where(h0, swapped, t)], axis=0)
                return jnp.concatenate([both, both], axis=1)

            def compute(item):
                q2, g2 = _stack_heads(item["q"] * SCORE_SCALE, h0), _stack_heads(item["g"], h0)
                s = _mm_nt(q2, item["k"]) + item["bias"]
                p = jnp.exp(s - per_head(item["lse"]))
                dp = _mm_nt(g2, item["v"])
                ds = (p * (dp - per_head(item["delta"]))).astype(BF16)
                dq2 = _mm(ds, item["k"])
                dq = jnp.where(h0, dq2[:ATTN_BLOCK], dq2[ATTN_BLOCK:]) * SCORE_SCALE
                return dq, _mm_tn(ds, q2), _mm_tn(p.astype(BF16), g2)

            def store(item, res, first=first):
                _put(dq_s, item["rows"], res[0], add=not first)
                _put(dk_s, item["keys"], res[1], add=True)
                _put(dv_s, item["keys"], res[2], add=True)

            _for_each_group(d, ATTN_GROUP_BWD, load, compute, store)

        @pl.when(step > 0)
        def _():
            for copy in copies_out(step - 1):
                copy.wait()

        for copy in copies_out(step):
            copy.start()

        @pl.when(step == n_steps - 1)
        def _():
            for copy in copies_out(step):
                copy.wait()

    slab = lambda g: pl.BlockSpec((SEQ, LANES), functools.partial(lambda hp, g: (0, 4 * g + hp), g=g))
    anywhere = pl.BlockSpec(memory_space=pl.ANY)
    by_residue = (SORT_ROWS, SORT_RESIDUES, ATTN_WIDTH)
    grads = pl.pallas_call(
        body, name="attn_bwd", grid=(4,), out_shape=(jax.ShapeDtypeStruct(by_residue, F32),) * 3,
        scratch_shapes=[pltpu.VMEM((2, 2, SEQ, LANES), F32), pltpu.VMEM((2, 3, SEQ, LANES), F32),
                        pltpu.VMEM((2 * len(DILATIONS), 2 * ATTN_BLOCK, 2 * ATTN_BLOCK), F32),
                        pltpu.SemaphoreType.DMA((2, 2, SORT_RESIDUES)), pltpu.SemaphoreType.DMA((2, 3, SORT_RESIDUES))],
        in_specs=[slab(0), slab(1), slab(2), anywhere, slab(0), anywhere], out_specs=(anywhere,) * 3,
        compiler_params=_params(("arbitrary",)),
    )(qkv_sorted, qkv_sorted, qkv_sorted, d_out.reshape(by_residue), lse_sorted, delta.reshape(by_residue))
    return tuple(g.reshape(SEQ, ATTN_WIDTH) for g in grads)


def _hgrn_lower_bound(lb_ref):
    r0, r1 = lb_ref[0:1, :], lb_ref[1:2, :]
    mx = jnp.maximum(r0, r1)
    e0, e1 = jnp.exp(r0 - mx), jnp.exp(r1 - mx)
    return e0 / (e0 + e1)


def _hgrn_gates(hq, hf, lb):
    sq = _sigmoid(hq)
    sg = _sigmoid(hf)
    f = lb + (1.0 - lb) * sg
    return hq * sq, sq, sg, f, 1.0 - f, jnp.log(f)


HGRN_PAIR = 4
HGRN_SEQ_BLOCK = 1024
HGRN_GROUP = 4
HGRN_ROWS = HGRN_GROUP * HGRN_CHUNK


def _hgrn_specs(reverse):
    n_blocks = SEQ // HGRN_SEQ_BLOCK
    width = HGRN_PAIR * HGRN_DIM
    blk = (lambda s: n_blocks - 1 - s) if reverse else (lambda s: s)
    cols = lambda g: pl.BlockSpec((None, HGRN_SEQ_BLOCK, width), functools.partial(lambda p, s, g: (g, blk(s), p), g=g))
    pair = pl.BlockSpec((HGRN_SEQ_BLOCK, width), lambda p, s: (blk(s), p))
    lb = pl.BlockSpec((2, width), lambda p, s: (0, p))
    states = pl.BlockSpec((HGRN_PAIR, HGRN_SEQ_BLOCK // HGRN_CHUNK, HGRN_DIM, HGRN_DIM),
                          lambda p, s: (p, blk(s), 0, 0))
    return cols, pair, lb, states


def _chunk_masks():
    ri = lax.broadcasted_iota(jnp.int32, (HGRN_ROWS, HGRN_ROWS), 0)
    ci = lax.broadcasted_iota(jnp.int32, (HGRN_ROWS, HGRN_ROWS), 1)
    same = (ri // HGRN_CHUNK) == (ci // HGRN_CHUNK)
    return same, same & (ri >= ci), same & (ri <= ci)


def _mm_select(sel, v):
    hi = v.astype(BF16)
    r1 = v - hi.astype(F32)
    mid = r1.astype(BF16)
    lo = (r1 - mid.astype(F32)).astype(BF16)
    return _mm(sel, hi) + _mm(sel, mid) + _mm(sel, lo)


def _head_cols(a, h):
    return a[:, HGRN_DIM * h:HGRN_DIM * (h + 1)]


def _hgrn_fwd(proj, lb_raw):
    t, rws = HGRN_CHUNK, HGRN_ROWS

    def body(hq_ref, hf_ref, hi_ref, lb_ref, rec_ref, st_ref, state):
        @pl.when(pl.program_id(1) == 0)
        def _():
            state[...] = jnp.zeros_like(state)

        lb = _hgrn_lower_bound(lb_ref)
        same, causal, _ = _chunk_masks()
        sel = jnp.concatenate([causal, same], axis=0).astype(BF16)

        def group(g, sts):
            rows = pl.ds(pl.multiple_of(g * rws, rws), rws)
            q, _, _, _, k, lf = _hgrn_gates(hq_ref[rows, :], hf_ref[rows, :], lb)
            sums = _mm_select(sel, lf)
            cum, last = sums[:rws], sums[rws:]
            qd = (q * jnp.exp(cum)).astype(BF16)
            ki = (k * jnp.exp(-cum)).astype(BF16)
            ke = (k * jnp.exp(last - cum)).astype(BF16)
            vb = hi_ref[rows, :].astype(BF16)
            dec = jnp.exp(last)
            new_sts, recs = [], []
            for h in range(HGRN_PAIR):
                qd_h, ke_h, vb_h = _head_cols(qd, h), _head_cols(ke, h), _head_cols(vb, h)
                att = jnp.where(causal, _mm_nt(qd_h, _head_cols(ki, h)), 0.0).astype(BF16)
                intra = _mm(att, vb_h)
                st = sts[h]
                outs = []
                for c in range(HGRN_GROUP):
                    sl = slice(c * t, (c + 1) * t)
                    st_ref[h, g * HGRN_GROUP + c] = st
                    outs.append(intra[sl] + _mm_nt(qd_h[sl], st.astype(BF16)))
                    st = st * _head_cols(dec[c * t:c * t + 1, :], h) + _mm_tn(vb_h[sl], ke_h[sl])
                new_sts.append(st)
                recs.append(jnp.concatenate(outs, axis=0))
            rec_ref[rows, :] = jnp.concatenate(recs, axis=1)
            return tuple(new_sts)

        sts = lax.fori_loop(0, HGRN_SEQ_BLOCK // rws, group, tuple(state[h] for h in range(HGRN_PAIR)), unroll=True)
        for h in range(HGRN_PAIR):
            state[h] = sts[h]

    cols, pair, lb, states = _hgrn_specs(reverse=False)
    return pl.pallas_call(
        body, name="hgrn_fwd", grid=(HGRN_HEADS // HGRN_PAIR, SEQ // HGRN_SEQ_BLOCK),
        out_shape=(jax.ShapeDtypeStruct((SEQ, HGRN_WIDTH), F32),
                   jax.ShapeDtypeStruct((HGRN_HEADS, N_CHUNKS, HGRN_DIM, HGRN_DIM), F32)),
        in_specs=[cols(4), cols(5), cols(6), lb], out_specs=(pair, states),
        scratch_shapes=[pltpu.VMEM((HGRN_PAIR, HGRN_DIM, HGRN_DIM), F32)],
        compiler_params=_params(("parallel", "arbitrary")),
    )(proj, proj, proj, lb_raw)


def _hgrn_bwd(proj, lb_raw, d_rec, states):
    t, rws = HGRN_CHUNK, HGRN_ROWS

    def body(hq_ref, hf_ref, hi_ref, lb_ref, do_ref, st_ref, dhq_ref, dhf_ref, dhi_ref, dlb_ref,
             dstate, dlb_acc):
        lb = _hgrn_lower_bound(lb_ref)
        same, causal, anti = _chunk_masks()
        sel = jnp.concatenate([causal, same], axis=0).astype(BF16)
        sel_t = jnp.concatenate([anti, same], axis=1).astype(BF16)
        @pl.when(pl.program_id(1) == 0)
        def _():
            dstate[...] = jnp.zeros_like(dstate)
            dlb_acc[...] = jnp.zeros_like(dlb_acc)

        n_groups = HGRN_SEQ_BLOCK // rws
        chunks = [slice(c * t, (c + 1) * t) for c in range(HGRN_GROUP)]

        def group(i, dsts_in):
            g = n_groups - 1 - i
            rows = pl.ds(pl.multiple_of(g * rws, rws), rws)
            hq = hq_ref[rows, :]
            q, sq, sg, f, k, lf = _hgrn_gates(hq, hf_ref[rows, :], lb)
            sums = _mm_select(sel, lf)
            cum, last = sums[:rws], sums[rws:]
            e_cum, e_inv, e_end, dec = jnp.exp(cum), jnp.exp(-cum), jnp.exp(last - cum), jnp.exp(last)
            qd, ki, ke = q * e_cum, k * e_inv, k * e_end
            qdb, kib, keb = qd.astype(BF16), ki.astype(BF16), ke.astype(BF16)
            vb = hi_ref[rows, :].astype(BF16)
            gb = do_ref[rows, :]

            dsts_out, per_head = [], []
            for h in range(HGRN_PAIR):
                qdb_h, kib_h, keb_h = _head_cols(qdb, h), _head_cols(kib, h), _head_cols(keb, h)
                vb_h, gb_h = _head_cols(vb, h), _head_cols(gb, h)
                att = jnp.where(causal, _mm_nt(qdb_h, kib_h), 0.0).astype(BF16)
                datt = jnp.where(causal, _mm_nt(gb_h, vb_h), 0.0).astype(BF16)
                dv = _mm_tn(att, gb_h)
                dqd = _mm(datt, kib_h)
                dki = _mm_tn(datt, qdb_h)

                decs = [_head_cols(dec[c * t:c * t + 1, :], h) for c in range(HGRN_GROUP)]
                dsts = [None] * HGRN_GROUP
                dst = dsts_in[h]
                for c in reversed(range(HGRN_GROUP)):
                    dsts[c] = dst
                    dst = dst * decs[c] + _mm_tn(gb_h[chunks[c]], qdb_h[chunks[c]])
                dsts_out.append(dst)

                dv_x, dqd_x, dke, dlast_x = [], [], [], []
                for c, sl in enumerate(chunks):
                    st_prev = st_ref[h, g * HGRN_GROUP + c]
                    dstb = dsts[c].astype(BF16)
                    dv_x.append(_mm_nt(keb_h[sl], dstb))
                    dqd_x.append(_mm(gb_h[sl], st_prev.astype(BF16)))
                    dke.append(_mm(vb_h[sl], dstb))
                    ddec = jnp.sum(dsts[c] * st_prev, axis=0, keepdims=True)
                    dlast_x.append(jnp.broadcast_to(ddec * decs[c], (t, HGRN_DIM)))
                per_head.append((dv + jnp.concatenate(dv_x, axis=0), dqd + jnp.concatenate(dqd_x, axis=0),
                                 dki, jnp.concatenate(dke, axis=0), jnp.concatenate(dlast_x, axis=0)))
            dv, dqd, dki, dke, dlast = (jnp.concatenate(list(parts), axis=1) for parts in zip(*per_head))

            dq = dqd * e_cum
            dk = dki * e_inv + dke * e_end
            dke_ke = dke * ke
            dcum = dqd * qd - dki * ki - dke_ke
            dlf = _mm_select(sel_t, jnp.concatenate([dcum, dke_ke], axis=0)) + dlast
            df = dlf / f - dk
            dhq_ref[rows, :] = (dq * (sq * (1.0 + hq * (1.0 - sq)))).astype(BF16)
            dhf_ref[rows, :] = (df * (1.0 - lb) * (sg * (1.0 - sg))).astype(BF16)
            dhi_ref[rows, :] = dv.astype(BF16)
            dlb_acc[...] += jnp.sum(df * (1.0 - sg), axis=0, keepdims=True)
            return tuple(dsts_out)

        dsts = lax.fori_loop(0, n_groups, group, tuple(dstate[h] for h in range(HGRN_PAIR)), unroll=True)
        for h in range(HGRN_PAIR):
            dstate[h] = dsts[h]
        g0 = dlb_acc[...] * lb * (1.0 - lb)
        dlb_ref[...] = jnp.concatenate([g0, -g0], axis=0)

    cols, pair, lb_spec, st_spec = _hgrn_specs(reverse=True)
    wide = jax.ShapeDtypeStruct((SEQ, HGRN_WIDTH), BF16)
    return pl.pallas_call(
        body, name="hgrn_bwd", grid=(HGRN_HEADS // HGRN_PAIR, SEQ // HGRN_SEQ_BLOCK),
        out_shape=(wide, wide, wide, jax.ShapeDtypeStruct((2, HGRN_WIDTH), F32)),
        in_specs=[cols(4), cols(5), cols(6), lb_spec, pair, st_spec],
        out_specs=(pair, pair, pair, lb_spec),
        scratch_shapes=[pltpu.VMEM((HGRN_PAIR, HGRN_DIM, HGRN_DIM), F32),
                        pltpu.VMEM((1, HGRN_PAIR * HGRN_DIM), F32)],
        compiler_params=_params(("parallel", "arbitrary")),
    )(proj, proj, proj, lb_raw, d_rec, states)


def _group_sum(v, group):
    parts = []
    for s in range(v.shape[1] // LANES):
        slab = v[:, LANES * s:LANES * (s + 1)]
        if group == LANES:
            parts.append(jnp.broadcast_to(jnp.sum(slab, axis=-1, keepdims=True), slab.shape))
        else:
            h0 = lax.broadcasted_iota(jnp.int32, slab.shape, 1) < HEAD_DIM
            s0 = jnp.sum(jnp.where(h0, slab, 0.0), axis=-1, keepdims=True)
            s1 = jnp.sum(jnp.where(h0, 0.0, slab), axis=-1, keepdims=True)
            parts.append(jnp.where(h0, s0, s1))
    return jnp.concatenate(parts, axis=1)


def _mid(attn_o, rec, proj, x, target, w_out_g, attn_w, hgrn_w, final_w):
    tm = 512

    def branch_fwd(o, gate, w, group):
        r = lax.rsqrt(_group_sum(o * o, group) * (1.0 / group) + NORM_EPS)
        nrm = o * r
        sg = _sigmoid(gate)
        return r, nrm, sg, nrm * w * (gate * sg)

    def branch_bwd(dy, r, nrm, sg, gate, w, group):
        silu = gate * sg
        d_gate = dy * nrm * w * (sg * (1.0 + gate * (1.0 - sg)))
        d_w = jnp.sum(dy * nrm * silu, axis=0, keepdims=True)
        dn = dy * w * silu
        d_o = r * (dn - nrm * (_group_sum(dn * nrm, group) * (1.0 / group)))
        return d_o, d_gate, d_w

    def body(o_ref, rec_ref, ag_ref, hg_ref, x_ref, tgt_ref, wout_ref, aw_ref, hw_ref, fw_ref,
             dx2_ref, do_ref, delta_ref, dag_ref, drec_ref, dhg_ref, dwout_ref, dfw_ref, daw_ref, dhw_ref,
             loss_ref, dwout_acc):
        i = pl.program_id(0)

        @pl.when(i == 0)
        def _():
            dwout_acc[...] = jnp.zeros_like(dwout_acc)
            dfw_ref[...] = jnp.zeros_like(dfw_ref)
            daw_ref[...] = jnp.zeros_like(daw_ref)
            dhw_ref[...] = jnp.zeros_like(dhw_ref)
            loss_ref[...] = jnp.zeros_like(loss_ref)

        o, rc, ag, hg = o_ref[...], rec_ref[...], ag_ref[...], hg_ref[...]
        aw, hw, fw = aw_ref[...], hw_ref[...], fw_ref[...]
        ra, na, sga, ya = branch_fwd(o, ag, aw, HEAD_DIM)
        rh, nh, sgh, yh = branch_fwd(rc, hg, hw, HGRN_DIM)
        mixed = jnp.concatenate([ya, yh], axis=1).astype(BF16)
        wout = wout_ref[...]
        x2 = x_ref[...] + _mm(mixed, wout)
        rstd = lax.rsqrt(jnp.mean(x2 * x2, axis=-1, keepdims=True) + NORM_EPS)
        xn = x2 * rstd
        err = xn * fw - tgt_ref[...]
        row_loss = jnp.mean(err * err, axis=-1, keepdims=True)
        loss_ref[...] += 0.5 * jnp.sum(row_loss, axis=0, keepdims=True)
        dy = err * (1.0 / D_MODEL)
        dfw_ref[...] += jnp.sum(dy * xn, axis=0, keepdims=True)
        dxn = dy * fw
        dx2 = rstd * (dxn - xn * jnp.mean(dxn * xn, axis=-1, keepdims=True))
        dx2_ref[...] = dx2
        dx2b = dx2.astype(BF16)
        dwout_acc[...] += _mm_tn(mixed, dx2b)

        @pl.when(i == pl.num_programs(0) - 1)
        def _():
            dwout_ref[...] = dwout_acc[...].astype(BF16)

        dmixed = _mm_nt(dx2b, wout)

        d_o, d_ag, d_aw = branch_bwd(dmixed[:, :ATTN_WIDTH], ra, na, sga, ag, aw, HEAD_DIM)
        d_rec, d_hg, d_hw = branch_bwd(dmixed[:, ATTN_WIDTH:], rh, nh, sgh, hg, hw, HGRN_DIM)
        do_ref[...] = d_o
        delta_ref[...] = _group_sum(d_o * o, HEAD_DIM)
        dag_ref[...] = d_ag.astype(BF16)
        drec_ref[...] = d_rec.astype(BF16)
        dhg_ref[...] = d_hg.astype(BF16)
        daw_ref[...] += d_aw
        dhw_ref[...] += d_hw

    half = lambda: pl.BlockSpec((tm, COL_BLOCK), lambda i: (i, 0))
    full = lambda: pl.BlockSpec((tm, D_MODEL), lambda i: (i, 0))
    fixed = lambda r, c: pl.BlockSpec((r, c), lambda i: (0, 0))
    wide = jax.ShapeDtypeStruct((SEQ, COL_BLOCK), F32)
    wide_b = jax.ShapeDtypeStruct((SEQ, COL_BLOCK), BF16)
    return pl.pallas_call(
        body, name="mid", grid=(SEQ // tm,),
        out_shape=(jax.ShapeDtypeStruct((SEQ, D_MODEL), F32), wide, wide, wide_b, wide_b, wide_b,
                   jax.ShapeDtypeStruct((D_MODEL, D_MODEL), BF16),
                   jax.ShapeDtypeStruct((1, D_MODEL), F32), jax.ShapeDtypeStruct((1, COL_BLOCK), F32),
                   jax.ShapeDtypeStruct((1, COL_BLOCK), F32), jax.ShapeDtypeStruct((1, 1), F32)),
        scratch_shapes=[pltpu.VMEM((D_MODEL, D_MODEL), F32)],
        in_specs=[half(), half(),
                  pl.BlockSpec((None, tm, COL_BLOCK), lambda i: (3, i, 0)),
                  pl.BlockSpec((None, tm, COL_BLOCK), lambda i: (7, i, 0)),
                  full(), full(), fixed(D_MODEL, D_MODEL), fixed(1, COL_BLOCK), fixed(1, COL_BLOCK),
                  fixed(1, D_MODEL)],
        out_specs=(full(), half(), half(), half(), half(), half(), fixed(D_MODEL, D_MODEL),
                   fixed(1, D_MODEL), fixed(1, COL_BLOCK), fixed(1, COL_BLOCK), fixed(1, 1)),
        compiler_params=_params(("arbitrary",)),
    )(attn_o, rec, proj, proj, x, target, w_out_g, attn_w, hgrn_w, final_w)


def _in_proj_bwd_rows(d_groups, w_g, x, dx2, mix_w, rc, rsa, rsb):
    tm = 512

    def body(*refs):
        dg_refs = refs[:N_DEV]
        wg_ref, x_ref, dx2_ref, w_ref, c_ref, sa_ref, sb_ref, gx_ref, dpb_ref, dmw_ref = refs[N_DEV:]

        @pl.when(pl.program_id(0) == 0)
        def _():
            dmw_ref[...] = jnp.zeros_like(dmw_ref)

        parts = []
        for j in range(N_DEV):
            dp = dg_refs[j][...]
            if j < 2:
                dp = _rot_transposed(dp, c_ref[...], sa_ref[...], sb_ref[...])
            parts.append(dp.astype(BF16))
        dpb = jnp.concatenate(parts, axis=1)
        for j in range(N_DEV):
            dpb_ref[j] = parts[j]
        g = _mm_nt(dpb, wg_ref[...])
        xf = x_ref[...]
        rstd = lax.rsqrt(jnp.mean(xf * xf, axis=-1, keepdims=True) + NORM_EPS)
        xn = xf * rstd
        dmw_ref[...] += jnp.sum(g * xn, axis=0, keepdims=True)
        gw = g * w_ref[...]
        gx_ref[...] = dx2_ref[...] + rstd * (gw - xn * jnp.mean(gw * xn, axis=-1, keepdims=True))

    tile = lambda cols: pl.BlockSpec((tm, cols), lambda i: (i, 0))
    fixed = lambda r, c: pl.BlockSpec((r, c), lambda i: (0, 0))
    return pl.pallas_call(
        body, name="in_proj_bwd_rows", grid=(SEQ // tm,),
        out_shape=(jax.ShapeDtypeStruct((SEQ, D_MODEL), F32), jax.ShapeDtypeStruct((N_DEV, SEQ, COL_BLOCK), BF16),
                   jax.ShapeDtypeStruct((1, D_MODEL), F32)),
        in_specs=[tile(COL_BLOCK) for _ in range(N_DEV)] + [
            pl.BlockSpec((D_MODEL, IN_COLS), lambda i: (0, 0), pipeline_mode=pl.Buffered(1)),
            tile(D_MODEL), tile(D_MODEL), fixed(1, D_MODEL), tile(LANES), tile(LANES), tile(LANES)],
        out_specs=(tile(D_MODEL), pl.BlockSpec((N_DEV, tm, COL_BLOCK), lambda i: (0, i, 0)), fixed(1, D_MODEL)),
        compiler_params=_params(("arbitrary",)),
    )(*d_groups, w_g, x, dx2, mix_w, rc, rsa, rsb)


def _weights_exchange(hn_t, dproj_b, dwout_p, small_p):
    n_chips = N_DEV // 2
    rb = 128
    S1_IN, S1_OUT, SMALL, S2_IN, S2_OUT, VIA_IN, VIA_OUT = 0, 4, 8, 15, 17, 19, 21
    rel_of_pair = (3, 1, 2, 0)
    two_hop = n_chips - 1
    half_in, half_out = COL_BLOCK // 2, D_MODEL // 2

    def body(order_ref, hnt_ref, dp_ref, dwout_ref, small_ref, gin_ref, gout_ref, gs_ref,
             part, s1_send, s1_in, s1_out, fwd_in, fwd_out, s2_in, s2_out, via_in, via_out, land_s,
             send_sems, recv_sems):
        t = pl.program_id(0)
        me = _my_place()
        x, y, c = me
        my_chip = 2 * x + y
        sibling = (x, y, 1 - c)

        def remote(slot, src, dst, to):
            return pltpu.make_async_remote_copy(src_ref=src, dst_ref=dst, send_sem=send_sems.at[slot],
                                                recv_sem=recv_sems.at[slot], device_id=to, device_id_type=MESH)

        def s1_in_copy(pair):
            return remote(S1_IN + pair, s1_send.at[pair], s1_in.at[pair], sibling)

        def s1_out_copy(pair):
            q = my_chip ^ rel_of_pair[pair]
            return remote(S1_OUT + pair, dwout_ref.at[q, 1 - c], s1_out.at[pair], sibling)

        def s2_copies(rel):
            peer = _peer(me, 2 * rel)
            return [remote(S2_IN + rel - 1, fwd_in.at[rel - 1], s2_in.at[rel - 1], peer),
                    remote(S2_OUT + rel - 1, fwd_out.at[rel - 1], s2_out.at[rel - 1], peer)]

        def via_copies(k):
            peer = _peer(me, 2 * (2 - k))
            return [remote(VIA_IN + k, fwd_in.at[two_hop - 1, :, pl.ds(k * half_in, half_in)], via_in.at[k], peer),
                    remote(VIA_OUT + k, fwd_out.at[two_hop - 1, :, pl.ds(k * half_out, half_out)], via_out.at[k],
                           peer)]

        def small_copy(rel):
            return remote(SMALL + rel - 1, small_ref, land_s.at[rel], _peer(me, rel))

        @pl.when(t == 0)
        def _():
            land_s[0] = small_ref[...]
            for pair in range(n_chips):
                s1_out_copy(pair).start()
            for rel in range(1, N_DEV):
                small_copy(rel).start()

        part[...] = _mm(hnt_ref[...], dp_ref[...])

        def rows_loop(n_rows, fn):
            def step(b, carry):
                fn(pl.ds(pl.multiple_of(b * rb, rb), rb))
                return carry
            lax.fori_loop(0, n_rows // rb, step, 0)

        for pair, rel in enumerate(rel_of_pair):
            @pl.when(t == 2 * pair)
            def _(pair=pair):
                s1_send[pair] = part[...].astype(BF16)
                s1_in_copy(pair).start()

            @pl.when(t == 2 * pair + 1)
            def _(pair=pair, rel=rel):
                q = my_chip ^ rel
                s1_in_copy(pair).wait_recv()
                s1_out_copy(pair).wait_recv()
                dst_in = fwd_in.at[rel - 1] if rel else gin_ref
                dst_out = fwd_out.at[rel - 1] if rel else gout_ref
                passes_on = rel in (1, 2)
                if passes_on:
                    for cp in via_copies(rel - 1):
                        cp.wait_recv()

                def with_half(val, via, rows, width):
                    if not passes_on:
                        return val
                    extra = via[rel - 1, rows, :].astype(F32)
                    halves = [val[:, :width], val[:, width:]]
                    halves[rel - 1] = halves[rel - 1] + extra
                    return jnp.concatenate(halves, axis=1)

                def add_in(rows):
                    val = part[rows, :] + s1_in[pair, rows, :].astype(F32)
                    dst_in[rows, :] = with_half(val, via_in, rows, half_in).astype(dst_in.dtype)

                def add_out(rows):
                    val = dwout_ref[q, c, rows, :].astype(F32) + s1_out[pair, rows, :].astype(F32)
                    dst_out[rows, :] = with_half(val, via_out, rows, half_out).astype(dst_out.dtype)

                rows_loop(D_MODEL, add_in)
                rows_loop(WOUT_ROWS, add_out)
                if rel == two_hop:
                    for k in range(2):
                        for cp in via_copies(k):
                            cp.start()
                elif rel:
                    for cp in s2_copies(rel):
                        cp.start()

        @pl.when(t == N_DEV - 1)
        def _():
            for rel in range(1, two_hop):
                for cp in s2_copies(rel):
                    cp.wait_recv()

            def total_in(rows):
                g = gin_ref[rows, :]
                for rel in range(1, two_hop):
                    g = g + s2_in[rel - 1, rows, :].astype(F32)
                gin_ref[rows, :] = g

            def total_out(rows):
                g = gout_ref[rows, :]
                for rel in range(1, two_hop):
                    g = g + s2_out[rel - 1, rows, :].astype(F32)
                gout_ref[rows, :] = g

            rows_loop(D_MODEL, total_in)
            rows_loop(WOUT_ROWS, total_out)

            for rel in range(1, N_DEV):
                small_copy(rel).wait_recv()
            my_flat = _flat(me)
            g = land_s[my_flat ^ 0]
            for dev in range(1, N_DEV):
                g = g + land_s[my_flat ^ dev]
            gs_ref[...] = g

            for pair in range(n_chips):
                s1_in_copy(pair).wait_send()
                s1_out_copy(pair).wait_send()
            for rel in range(1, two_hop):
                for cp in s2_copies(rel) + via_copies(rel - 1):
                    cp.wait_send()
            for rel in range(1, N_DEV):
                small_copy(rel).wait_send()

    place_x, place_y, place_c = _my_place()
    my_chip = 2 * place_x + place_y
    order = jnp.stack([2 * (my_chip ^ rel) + core for rel in rel_of_pair
                       for core in (1 - place_c, place_c)]).astype(jnp.int32)

    whole = lambda: pl.BlockSpec(memory_space=pltpu.VMEM)
    in_blocks = lambda n: pltpu.VMEM((n, D_MODEL, COL_BLOCK), BF16)
    out_blocks = lambda n: pltpu.VMEM((n, WOUT_ROWS, D_MODEL), BF16)
    grid_spec = pltpu.PrefetchScalarGridSpec(
        num_scalar_prefetch=1, grid=(N_DEV,),
        in_specs=[pl.BlockSpec((D_MODEL, SEQ), lambda t, order: (0, 0), pipeline_mode=pl.Buffered(1)),
                  pl.BlockSpec((None, SEQ, COL_BLOCK), lambda t, order: (order[t], 0, 0)), whole(), whole()],
        out_specs=(whole(), whole(), whole()),
        scratch_shapes=[pltpu.VMEM((D_MODEL, COL_BLOCK), F32), in_blocks(n_chips), in_blocks(n_chips),
                        out_blocks(n_chips), in_blocks(n_chips - 1), out_blocks(n_chips - 1),
                        in_blocks(n_chips - 2), out_blocks(n_chips - 2),
                        pltpu.VMEM((2, D_MODEL, half_in), BF16), pltpu.VMEM((2, WOUT_ROWS, half_out), BF16),
                        pltpu.VMEM((N_DEV, SMALL_ROWS, LANES), F32),
                        pltpu.SemaphoreType.DMA((23,)), pltpu.SemaphoreType.DMA((23,))])
    return pl.pallas_call(
        body, name="weights_exchange", grid_spec=grid_spec,
        out_shape=(jax.ShapeDtypeStruct((D_MODEL, COL_BLOCK), F32), jax.ShapeDtypeStruct((WOUT_ROWS, D_MODEL), F32),
                   jax.ShapeDtypeStruct((SMALL_ROWS, LANES), F32)),
        compiler_params=_params(("arbitrary",)),
    )(order, hn_t, dproj_b, dwout_p.reshape(n_chips, 2, WOUT_ROWS, D_MODEL), small_p)


def _adamw(w, g, m, v):
    m = ADAM_B1 * m + (1.0 - ADAM_B1) * g
    v = ADAM_B2 * v + (1.0 - ADAM_B2) * (g * g)
    m_hat = m / (1.0 - ADAM_B1 ** ADAM_STEP)
    v_hat = v / (1.0 - ADAM_B2 ** ADAM_STEP)
    delta = -ADAM_LR * (m_hat / (jnp.sqrt(v_hat) + ADAM_EPS) + ADAM_WD * w)
    return delta, m, v


def _adamw_update(grads, weights, m_old, v_old):
    rb = 256

    def body(*refs):
        g_refs, w_refs, m_refs, v_refs = refs[0:3], refs[3:6], refs[6:9], refs[9:12]
        d_refs, nm_refs, nv_refs = refs[12:15], refs[15:18], refs[18:21]
        for k in range(3):
            n_rows = g_refs[k].shape[0]
            step_rows = min(rb, n_rows)

            def step(b, carry, k=k, step_rows=step_rows):
                rows = pl.ds(pl.multiple_of(b * step_rows, 8), step_rows)
                delta, nm, nv = _adamw(w_refs[k][rows, :], g_refs[k][rows, :], m_refs[k][rows, :], v_refs[k][rows, :])
                d_refs[k][rows, :] = delta
                nm_refs[k][rows, :] = nm
                nv_refs[k][rows, :] = nv
                return carry

            lax.fori_loop(0, n_rows // step_rows, step, 0)

    shapes = tuple(jax.ShapeDtypeStruct(g.shape, F32) for g in grads)
    vm = lambda: pl.BlockSpec(memory_space=pltpu.VMEM)
    outs = pl.pallas_call(
        body, name="adamw_update", out_shape=shapes * 3,
        in_specs=[vm() for _ in range(12)], out_specs=tuple(vm() for _ in range(9)),
        compiler_params=_params(),
    )(*grads, *weights, *m_old, *v_old)
    return outs[0:3], outs[3:6], outs[6:9]


def _pack_small(mix, attn, hgrn, lb, final, loss=None):
    def rows8(a):
        a = a.reshape(-1, LANES)
        return jnp.pad(a, ((0, 8 - a.shape[0]), (0, 0)))
    last = jnp.zeros((8, LANES), F32) if loss is None else jnp.pad(loss.reshape(1, 1), ((0, 7), (0, LANES - 1)))
    return jnp.concatenate([rows8(mix), rows8(attn), rows8(hgrn), rows8(lb), rows8(final), last], axis=0)


def _unpack_small(slab):
    return (slab[ROW_MIX:ROW_MIX + 8].reshape(1, D_MODEL), slab[ROW_ATTN:ROW_ATTN + 4].reshape(1, ATTN_WIDTH),
            slab[ROW_HGRN:ROW_HGRN + 4].reshape(1, HGRN_WIDTH), slab[ROW_LB:ROW_LB + 8].reshape(2, HGRN_WIDTH),
            slab[ROW_FINAL:ROW_FINAL + 8].reshape(D_MODEL))


def _rope(pos_row):
    j = np.arange(ROPE_ROWS)
    inv = np.where(j < ROPE_HALF, ROPE_THETA ** (-(j % ROPE_HALF) * (2.0 / ROPE_DIMS)), 0.0)
    e = np.arange(LANES) % HEAD_DIM
    hit = (j[:, None] == (e % ROPE_HALF)[None, :]) & (j[:, None] < ROPE_HALF)
    sel = np.stack([hit & (e < ROPE_DIMS), hit & (e >= ROPE_HALF) & (e < ROPE_DIMS),
                    -1.0 * (hit & (e < ROPE_HALF))]).astype(np.float32)
    return _rope_tables(pos_row, jnp.asarray(inv.astype(np.float32).reshape(ROPE_ROWS, 1)),
                        jnp.asarray(sel, dtype=BF16))


def _local_step(x, proj, qkv_sorted, w_in_g, w_out_g, tables, mix_w, attn_w, hgrn_w, lb_raw, final_w, target):
    rc, rsa, rsb = tables
    attn_o, lse = _attn_fwd_fused(qkv_sorted)
    rec, states = _hgrn_fwd(proj, lb_raw)

    (dx2, d_o, delta, d_ag, d_rec, d_hg, dwout_p, d_final, d_attn_w, d_hgrn_w, loss) = _mid(
        attn_o, rec, proj, x, target, w_out_g, attn_w, hgrn_w, final_w.reshape(1, D_MODEL))

    dqkv = _attn_bwd_fused(qkv_sorted, d_o, lse, delta)
    d_hq, d_hf, d_hi, d_lb = _hgrn_bwd(proj, lb_raw, d_rec, states)

    grad_x, dproj_b, d_mix = _in_proj_bwd_rows(
        (dqkv[0], dqkv[1], dqkv[2], d_ag, d_hq, d_hf, d_hi, d_hg), w_in_g, x, dx2, mix_w, rc, rsa, rsb)
    small_p = _pack_small(d_mix, d_attn_w, d_hgrn_w, d_lb, d_final, loss)
    return grad_x, dproj_b, dwout_p, small_p


def kernel(x, positions, w_in, w_out, mix_norm_w, attn_out_norm_w, hgrn_out_norm_w, hgrn_lb_raw, final_norm_w, loss_target, m_w_in, m_w_out, m_mix_norm_w, m_attn_out_norm_w, m_hgrn_out_norm_w, m_hgrn_lb_raw, m_final_norm_w, v_w_in, v_w_out, v_mix_norm_w, v_attn_out_norm_w, v_hgrn_out_norm_w, v_hgrn_lb_raw, v_final_norm_w):
    tables = _rope(positions)
    proj, hn_t, w_in_g, w_out_g, qkv_sorted = _gather_project(x[0], mix_norm_w, w_in[0], w_out[0], *tables)
    grad_x, dproj_b, dwout_p, small_p = _local_step(
        x[0], proj, qkv_sorted, w_in_g, w_out_g, tables, mix_norm_w, attn_out_norm_w, hgrn_out_norm_w,
        hgrn_lb_raw, final_norm_w, loss_target[0])
    g_in, g_out, g_s = _weights_exchange(hn_t, dproj_b, dwout_p, small_p)

    w_s = _pack_small(mix_norm_w, attn_out_norm_w, hgrn_out_norm_w, hgrn_lb_raw, final_norm_w)
    m_s = _pack_small(m_mix_norm_w, m_attn_out_norm_w, m_hgrn_out_norm_w, m_hgrn_lb_raw, m_final_norm_w)
    v_s = _pack_small(v_mix_norm_w, v_attn_out_norm_w, v_hgrn_out_norm_w, v_hgrn_lb_raw, v_final_norm_w)
    (d_in, d_out, d_s), (nm_in, nm_out, nm_s), (nv_in, nv_out, nv_s) = _adamw_update(
        (g_in, g_out, g_s), (w_in[0], w_out[0], w_s), (m_w_in[0], m_w_out[0], m_s), (v_w_in[0], v_w_out[0], v_s))

    loss = g_s[ROW_LOSS, 0]
    return (loss, grad_x[None], g_in[None], g_out[None], *_unpack_small(g_s),
            d_in[None], d_out[None], *_unpack_small(d_s),
            nm_in[None], nm_out[None], *_unpack_small(nm_s),
            nv_in[None], nv_out[None], *_unpack_small(nv_s))
```

```python
import functools

import jax
import jax.numpy as jnp
import numpy as np
from jax import lax
from jax.experimental import pallas as pl
from jax.experimental.pallas import tpu as pltpu

F32 = jnp.float32
BF16 = jnp.bfloat16

SEQ = 4096
D_MODEL = 1024
ATTN_WIDTH = 512
HGRN_WIDTH = 512
HEAD_DIM = 64
HGRN_HEADS = 4
HGRN_DIM = 128
HGRN_CHUNK = 64
N_CHUNKS = SEQ // HGRN_CHUNK
IN_COLS = 4096
COL_BLOCK = 512
N_DEV = 8
WOUT_ROWS = D_MODEL // N_DEV
ATTN_BLOCK = 128
DILATIONS = (1, 4, 16)
ROPE_THETA = 500000.0
ROPE_DIMS = 16
ROPE_HALF = 8
NORM_EPS = 1e-6
NEG_BIG = -1e30
LANES = 128

ADAM_LR = 0.001
ADAM_B1 = 0.9
ADAM_B2 = 0.999
ADAM_EPS = 1e-08
ADAM_WD = 0.01
ADAM_STEP = 10

SMALL_ROWS = 48
ROW_MIX, ROW_ATTN, ROW_HGRN, ROW_LB, ROW_FINAL, ROW_LOSS = 0, 8, 16, 24, 32, 40

VMEM_LIMIT = 56 * 1024 * 1024
MESH = pl.DeviceIdType.MESH


def _mm(a, b):
    return lax.dot_general(a, b, (((1,), (0,)), ((), ())), preferred_element_type=F32)


def _mm_nt(a, b):
    return lax.dot_general(a, b, (((1,), (1,)), ((), ())), preferred_element_type=F32)


def _mm_tn(a, b):
    return lax.dot_general(a, b, (((0,), (0,)), ((), ())), preferred_element_type=F32)


def _mm_exact(a, b):
    return lax.dot_general(a, b, (((1,), (0,)), ((), ())), preferred_element_type=F32,
                           precision=lax.Precision.HIGHEST)


def _sigmoid(v):
    return 1.0 / (1.0 + jnp.exp(-v))


def _params(sem=None, **kw):
    return pltpu.CompilerParams(dimension_semantics=sem, vmem_limit_bytes=VMEM_LIMIT, **kw)


def _my_place():
    return lax.axis_index("x"), lax.axis_index("y"), lax.axis_index("c")


def _peer(place, rel):
    x, y, c = place
    return (x ^ ((rel >> 2) & 1), y ^ ((rel >> 1) & 1), c ^ (rel & 1))


def _flat(place):
    x, y, c = place
    return 4 * x + 2 * y + c


ROPE_ROWS = 16


def _rope_tables(pos_row, inv_freq_col, selectors):
    def body(pos_ref, invf_ref, sel_ref, c_ref, sa_ref, sb_ref):
        ang = pos_ref[...].astype(F32) * invf_ref[...]
        cos, sin = jnp.cos(ang), jnp.sin(ang)

        def spread(v, sel):
            hi = v.astype(BF16)
            r1 = v - hi.astype(F32)
            mid = r1.astype(BF16)
            lo = (r1 - mid.astype(F32)).astype(BF16)
            return _mm_tn(hi, sel) + _mm_tn(mid, sel) + _mm_tn(lo, sel)

        e = lax.broadcasted_iota(jnp.int32, (1, LANES), 1) & (HEAD_DIM - 1)
        c_ref[...] = spread(cos, sel_ref[0]) + jnp.where(e < ROPE_DIMS, 0.0, 1.0)
        sa_ref[...] = spread(sin, sel_ref[1])
        sb_ref[...] = spread(sin, sel_ref[2])

    tab = jax.ShapeDtypeStruct((SEQ, LANES), F32)
    vm = lambda: pl.BlockSpec(memory_space=pltpu.VMEM)
    return pl.pallas_call(
        body, name="rope_tables", out_shape=(tab, tab, tab),
        in_specs=[vm(), vm(), vm()], out_specs=(vm(), vm(), vm()), compiler_params=_params(),
    )(pos_row, inv_freq_col, selectors)


def _per_slab(fn, t):
    return jnp.concatenate([fn(t[:, LANES * s:LANES * (s + 1)]) for s in range(t.shape[1] // LANES)], axis=1)


def _rot(t, c, sa, sb):
    return _per_slab(lambda u: u * c + pltpu.roll(u, ROPE_HALF, 1) * sa + pltpu.roll(u, LANES - ROPE_HALF, 1) * sb, t)


def _rot_transposed(g, c, sa, sb):
    return _per_slab(
        lambda u: u * c + pltpu.roll(u * sa, LANES - ROPE_HALF, 1) + pltpu.roll(u * sb, ROPE_HALF, 1), g)


def _gather_project(x, mix_w, w_in, w_out, rc, rsa, rsb):
    tm = 1024
    n_tiles = SEQ // tm
    arrival_of_step = (None, 0, 1, 2, 4, 5, 3, 6)

    def body(order_ref, x_ref, w_ref, win_ref, wout_ref, c_ref, sa_ref, sb_ref,
             proj_ref, hnt_ref, gin_hbm, gout_hbm, qkv_hbm,
             hn_s, w_land, wout_land, stage, sort_stage, send_sems, recv_sems, local_sems, sort_sems):
        g, i = pl.program_id(0), pl.program_id(1)
        me = _my_place()
        x_, y_, c_ = me
        sibling = (x_, y_, 1 - c_)
        chips = [(1 - x_, y_), (x_, 1 - y_), (1 - x_, 1 - y_)]

        def slab(which, place):
            idx = _flat(place)
            if which == 0:
                return w_land.at[idx]
            return wout_land.at[pl.ds(pl.multiple_of(idx * WOUT_ROWS, WOUT_ROWS), WOUT_ROWS), :]

        def remote(which, k, ref, to, src=None):
            return pltpu.make_async_remote_copy(
                src_ref=ref if src is None else src, dst_ref=ref, send_sem=send_sems.at[8 * which + k],
                recv_sem=recv_sems.at[8 * which + k], device_id=to, device_id_type=MESH)

        def copy(which, k, block, to, src=None):
            return remote(which, k, slab(which, block), to, src)

        def half(which, place, part):
            n = (D_MODEL if which == 0 else WOUT_ROWS) // 2
            if which == 0:
                return w_land.at[_flat(place), pl.ds(n * part, n), :]
            return wout_land.at[pl.ds(pl.multiple_of(_flat(place) * WOUT_ROWS + n * part, n), n), :]

        def first_copies(which):
            src = stage if which == 0 else None
            return ([copy(which, 0, me, sibling, src)]
                    + [copy(which, 1 + j, me, (*chips[j], c_), src) for j in range(2)])

        def relay(which, part):
            frm, to = (chips[1], chips[0]) if part == 0 else (chips[0], chips[1])
            return remote(which, 3 if part == 0 else 7, half(which, (*frm, c_), part), (*to, c_))

        def two_hop_half(which, part):
            return remote(which, 3 if part == 0 else 7, half(which, (*chips[2], c_), part), me)

        def pass_on(which, j):
            return copy(which, 4 + j, (*chips[j], c_), sibling)

        def arrival(which, k):
            if k == 0:
                return copy(which, 0, sibling, me)
            if k <= 2:
                return copy(which, k, (*chips[k - 1], c_), me)
            return copy(which, k, (*chips[k - 4], 1 - c_), me)

        def to_hbm(step):
            idx = order_ref[step]
            cols = pl.ds(pl.multiple_of(idx * COL_BLOCK, COL_BLOCK), COL_BLOCK)
            return pltpu.make_async_copy(w_land.at[idx], gin_hbm.at[:, cols], local_sems.at[step])

        @pl.when((g == 0) & (i == 0))
        def _():
            stage[...] = win_ref[...].astype(BF16)
            w_land[_flat(me)] = stage[...]
            wout_land[pl.ds(pl.multiple_of(_flat(me) * WOUT_ROWS, WOUT_ROWS), WOUT_ROWS), :] = (
                wout_ref[...].astype(BF16))
            for cp in first_copies(0) + first_copies(1)[:1]:
                cp.start()
            to_hbm(0).start()

        for step, k in enumerate(arrival_of_step):
            if k is None:
                continue

            @pl.when((g == step) & (i == 0))
            def _(k=k, step=step):
                if k == 3:
                    two_hop_half(0, 0).wait_recv()
                    two_hop_half(0, 1).wait_recv()
                else:
                    arrival(0, k).wait_recv()
                to_hbm(step).start()
                if 1 <= k <= 3:
                    pass_on(0, k - 1).start()
                if k == 1:
                    relay(0, 1).start()
                    for cp in first_copies(1)[1:]:
                        cp.start()
                if k == 2:
                    relay(0, 0).start()
                if k in (4, 5):
                    arrival(1, k - 3).wait_recv()
                    relay(1, 5 - k).start()

        rows = pl.ds(pl.multiple_of(i * tm, tm), tm)

        @pl.when(g == 0)
        def _():
            xf = x_ref[...]
            ms = jnp.mean(xf * xf, axis=-1, keepdims=True)
            hn = xf * lax.rsqrt(ms + NORM_EPS) * w_ref[...]
            hnt_ref[...] = hn.T.astype(BF16)
            hn_s[rows, :] = hn.astype(BF16)

        group = order_ref[g]

        def sorted_copy(tile_value):
            per = tm // SORT_RESIDUES
            cols = pl.ds(pl.multiple_of(group * COL_BLOCK, COL_BLOCK), COL_BLOCK)
            buf = i % 2

            def out_copies(tile, b):
                return [pltpu.make_async_copy(
                    sort_stage.at[b, :, r, :], qkv_hbm.at[r, pl.ds(pl.multiple_of(tile * per, per), per), cols],
                    sort_sems.at[b, r]) for r in range(SORT_RESIDUES)]

            @pl.when(i >= 2)
            def _():
                for copy in out_copies(i - 2, buf):
                    copy.wait()

            sort_stage[buf] = tile_value.reshape(per, SORT_RESIDUES, COL_BLOCK)
            for copy in out_copies(i, buf):
                copy.start()

            @pl.when(i == n_tiles - 1)
            def _():
                for copy in out_copies(i - 1, 1 - buf) + out_copies(i, buf):
                    copy.wait()

        @pl.when(group < 2)
        def _():
            rotated = _rot(_mm(hn_s[rows, :], w_land[group]), c_ref[...], sa_ref[...], sb_ref[...])
            proj_ref[...] = rotated
            sorted_copy(rotated)

        @pl.when(group == 2)
        def _():
            value = _mm(hn_s[rows, :], w_land[group])
            proj_ref[...] = value
            sorted_copy(value)

        @pl.when(group > 2)
        def _():
            proj_ref[...] = _mm(hn_s[rows, :], w_land[group])

        @pl.when((g == N_DEV - 1) & (i == n_tiles - 1))
        def _():
            pass_on(1, 0).start()
            pass_on(1, 1).start()
            two_hop_half(1, 0).wait_recv()
            two_hop_half(1, 1).wait_recv()
            pass_on(1, 2).start()
            for k in (0, 4, 5, 6):
                arrival(1, k).wait_recv()
            for which in (0, 1):
                for cp in (first_copies(which) + [relay(which, part) for part in range(2)]
                           + [pass_on(which, j) for j in range(3)]):
                    cp.wait_send()
            wout_copy = pltpu.make_async_copy(wout_land, gout_hbm, local_sems.at[N_DEV])
            wout_copy.start()
            for step in range(N_DEV):
                to_hbm(step).wait()
            wout_copy.wait()

    me = _my_place()
    x_, y_, c_ = me
    chips = [(1 - x_, y_), (x_, 1 - y_), (1 - x_, 1 - y_)]
    order = jnp.stack([_flat(p) for p in (
        me, (x_, y_, 1 - c_), (*chips[0], c_), (*chips[1], c_), (*chips[0], 1 - c_), (*chips[1], 1 - c_),
        (*chips[2], c_), (*chips[2], 1 - c_))]).astype(jnp.int32)

    first_sweep = lambda g, i, order: (jnp.where(g == 0, i, n_tiles - 1), 0)
    tab = pl.BlockSpec((tm, LANES), lambda g, i, order: (jnp.where(order[g] < 2, i, 0), 0))
    whole = lambda: pl.BlockSpec(memory_space=pltpu.VMEM)
    grid_spec = pltpu.PrefetchScalarGridSpec(
        num_scalar_prefetch=1, grid=(N_DEV, n_tiles),
        in_specs=[pl.BlockSpec((tm, D_MODEL), first_sweep),
                  pl.BlockSpec((1, D_MODEL), lambda g, i, order: (0, 0)),
                  whole(), whole(), tab, tab, tab],
        out_specs=(pl.BlockSpec((None, tm, COL_BLOCK), lambda g, i, order: (order[g], i, 0)),
                   pl.BlockSpec((D_MODEL, tm), lambda g, i, order: (0, jnp.where(g == 0, i, n_tiles - 1))),
                   pl.BlockSpec(memory_space=pl.ANY), pl.BlockSpec(memory_space=pl.ANY),
                   pl.BlockSpec(memory_space=pl.ANY)),
        scratch_shapes=[pltpu.VMEM((SEQ, D_MODEL), BF16),
                        pltpu.VMEM((N_DEV, D_MODEL, COL_BLOCK), BF16),
                        pltpu.VMEM((D_MODEL, D_MODEL), BF16),
                        pltpu.VMEM((D_MODEL, COL_BLOCK), BF16),
                        pltpu.VMEM((2, tm // SORT_RESIDUES, SORT_RESIDUES, COL_BLOCK), F32),
                        pltpu.SemaphoreType.DMA((16,)), pltpu.SemaphoreType.DMA((16,)),
                        pltpu.SemaphoreType.DMA((N_DEV + 1,)), pltpu.SemaphoreType.DMA((2, SORT_RESIDUES))])
    proj, hn_t, w_in_g, w_out_g, qkv_sorted = pl.pallas_call(
        body, name="gather_project", grid_spec=grid_spec,
        out_shape=(jax.ShapeDtypeStruct((N_DEV, SEQ, COL_BLOCK), F32), jax.ShapeDtypeStruct((D_MODEL, SEQ), BF16),
                   jax.ShapeDtypeStruct((D_MODEL, IN_COLS), BF16), jax.ShapeDtypeStruct((D_MODEL, D_MODEL), BF16),
                   jax.ShapeDtypeStruct((SORT_RESIDUES, SORT_ROWS, 3 * COL_BLOCK), F32)),
        compiler_params=_params(("arbitrary", "arbitrary")),
    )(order, x, mix_w, w_in, w_out, rc, rsa, rsb)
    return proj, hn_t, w_in_g, w_out_g, qkv_sorted.reshape(SEQ, 3 * COL_BLOCK)


SCORE_SCALE = HEAD_DIM ** -0.5
ATTN_GROUP_FWD = 32
ATTN_GROUP_BWD = 16
BLOCKS_PER_PATTERN = SEQ // ATTN_BLOCK
SORT_RESIDUES = 16
SORT_ROWS = SEQ // SORT_RESIDUES


def _write_band_bias(bias_ref):
    row = lax.broadcasted_iota(jnp.int32, (2 * ATTN_BLOCK, 2 * ATTN_BLOCK), 0) & (ATTN_BLOCK - 1)
    col = lax.broadcasted_iota(jnp.int32, (2 * ATTN_BLOCK, 2 * ATTN_BLOCK), 1)
    for pi, d in enumerate(DILATIONS):
        per = SORT_RESIDUES // d
        ahead = per * (row % (8 * d) - col % (16 * d)) + (row // (8 * d) - col // (16 * d))
        dist = ATTN_BLOCK + ahead
        bias_ref[2 * pi] = jnp.where((dist >= 0) & (dist <= ATTN_BLOCK), 0.0, NEG_BIG)
        bias_ref[2 * pi + 1] = jnp.where(ahead >= 0, 0.0, NEG_BIG)


def _head0_lanes():
    return lax.broadcasted_iota(jnp.int32, (ATTN_BLOCK, LANES), 1) < HEAD_DIM


def _stack_heads(t, h0):
    return jnp.concatenate([jnp.where(h0, t, 0.0), jnp.where(h0, 0.0, t)], axis=0).astype(BF16)


def _block_runs(i, d):
    nblk = BLOCKS_PER_PATTERN // d
    r, n = i // nblk, i % nblk
    kn = jnp.maximum(n - 1, 0)
    rows, keys = [], []
    for c in range(SORT_RESIDUES // d):
        base = SORT_ROWS * (c * d + r)
        rows.append(pl.ds(pl.multiple_of(base + 8 * d * n, 8), 8 * d))
        keys.append(pl.ds(pl.multiple_of(base + 8 * d * kn, 8), 16 * d))
    return rows, keys, (n == 0).astype(jnp.int32)


def _take(ref, runs):
    return jnp.concatenate([ref[run, :] for run in runs], axis=0)


def _put(ref, runs, value, add=False):
    at = 0
    for run in runs:
        piece = value[at:at + run.size]
        if add:
            ref[run, :] += piece
        else:
            ref[run, :] = piece
        at += run.size


def _sort_copies(src_hbm, lane_block, dst_ref, sem_ref):
    lanes = pl.ds(pl.multiple_of(LANES * lane_block, LANES), LANES)
    return [pltpu.make_async_copy(src_hbm.at[:, r, lanes], dst_ref.at[pl.ds(SORT_ROWS * r, SORT_ROWS), :],
                                  sem_ref.at[r]) for r in range(SORT_RESIDUES)]


def _unsort_copies(src_ref, dst_hbm, lane_block, sem_ref):
    lanes = pl.ds(pl.multiple_of(LANES * lane_block, LANES), LANES)
    return [pltpu.make_async_copy(src_ref.at[pl.ds(SORT_ROWS * r, SORT_ROWS), :], dst_hbm.at[:, r, lanes],
                                  sem_ref.at[r]) for r in range(SORT_RESIDUES)]


def _for_each_group(d, n_group, load, compute, store):
    def group(g, carry):
        items = [load(*_block_runs(g * n_group + u, d)) for u in range(n_group)]
        results = [compute(item) for item in items]
        for item, res in zip(items, results):
            store(item, res)
        return carry

    lax.fori_loop(0, BLOCKS_PER_PATTERN // n_group, group, 0)


def _attn_fwd_fused(qkv_sorted):
    n_pat = len(DILATIONS)
    tile2 = (2 * ATTN_BLOCK, LANES)

    def body(q_ref, k_ref, v_ref, o_hbm, lse_ref, o_slots, m_acc, l_acc, bias_ref, out_sem):
        step, n_steps = pl.program_id(0), pl.num_programs(0)
        pl.when(step == 0)(lambda: _write_band_bias(bias_ref))
        slot = step % 2
        o_acc = o_slots.at[slot]
        h0 = _head0_lanes()
        for pi, d in enumerate(DILATIONS):
            first, last = pi == 0, pi == n_pat - 1

            def load(rows, keys, which, first=first, pi=pi):
                item = dict(rows=rows, keys=keys, which=2 * pi + which)
                if not first:
                    item.update(o=_take(o_acc, rows), m=[_take(m_acc.at[h], rows) for h in range(2)],
                                l=[_take(l_acc.at[h], rows) for h in range(2)])
                return item

            def compute(item, first=first):
                kb = _take(k_ref, item["keys"]).astype(BF16)
                vb = _take(v_ref, item["keys"]).astype(BF16)
                s = _mm_nt(_stack_heads(_take(q_ref, item["rows"]) * SCORE_SCALE, h0), kb) + bias_ref[item["which"]]
                mb = jnp.max(s, axis=-1, keepdims=True)
                if first:
                    p = jnp.exp(s - mb)
                    mn = jnp.broadcast_to(mb, tile2)
                else:
                    m_old = jnp.concatenate(item["m"], axis=0)
                    mn = jnp.maximum(m_old, mb)
                    alpha = jnp.exp(m_old - mn)
                    p = jnp.exp(s - jnp.concatenate([mn, mn], axis=1))
                ls = jnp.sum(p, axis=-1, keepdims=True)
                pv = _mm(p.astype(BF16), vb)
                if first:
                    return pv, mn, jnp.broadcast_to(ls, tile2)
                o_old = jnp.concatenate([item["o"], item["o"]], axis=0)
                return alpha * o_old + pv, mn, alpha * jnp.concatenate(item["l"], axis=0) + ls

            def store(item, res, last=last):
                rows = item["rows"]
                (o0, o1), (m0, m1), (l0, l1) = ((a[:ATTN_BLOCK], a[ATTN_BLOCK:]) for a in res)
                if last:
                    _put(o_acc, rows, jnp.where(h0, o0 / l0, o1 / l1))
                    _put(lse_ref, rows, jnp.where(h0, m0 + jnp.log(l0), m1 + jnp.log(l1)))
                else:
                    _put(o_acc, rows, jnp.where(h0, o0, o1))
                    for h, (m, l) in enumerate(((m0, l0), (m1, l1))):
                        _put(m_acc.at[h], rows, m)
                        _put(l_acc.at[h], rows, l)

            _for_each_group(d, ATTN_GROUP_FWD, load, compute, store)

        def copies_out(of_step):
            return _unsort_copies(o_slots.at[of_step % 2], o_hbm, of_step, out_sem.at[of_step % 2])

        @pl.when(step > 0)
        def _():
            for copy in copies_out(step - 1):
                copy.wait()

        for copy in copies_out(step):
            copy.start()

        @pl.when(step == n_steps - 1)
        def _():
            for copy in copies_out(step):
                copy.wait()

    slab = lambda g: pl.BlockSpec((SEQ, LANES), functools.partial(lambda hp, g: (0, 4 * g + hp), g=g))
    wide = jax.ShapeDtypeStruct((SEQ, ATTN_WIDTH), F32)
    o_rows, lse = pl.pallas_call(
        body, name="attn_fwd", grid=(4,),
        out_shape=(jax.ShapeDtypeStruct((SORT_ROWS, SORT_RESIDUES, ATTN_WIDTH), F32), wide),
        in_specs=[slab(0), slab(1), slab(2)], out_specs=(pl.BlockSpec(memory_space=pl.ANY), slab(0)),
        scratch_shapes=[pltpu.VMEM((2, SEQ, LANES), F32), pltpu.VMEM((2, SEQ, LANES), F32),
                        pltpu.VMEM((2, SEQ, LANES), F32),
                        pltpu.VMEM((2 * len(DILATIONS), 2 * ATTN_BLOCK, 2 * ATTN_BLOCK), F32),
                        pltpu.SemaphoreType.DMA((2, SORT_RESIDUES))],
        compiler_params=_params(("arbitrary",)),
    )(qkv_sorted, qkv_sorted, qkv_sorted)
    return o_rows.reshape(SEQ, ATTN_WIDTH), lse


def _attn_bwd_fused(qkv_sorted, d_out, lse_sorted, delta):
    def body(q_ref, k_ref, v_ref, do_hbm, lse_ref, del_hbm, dq_hbm, dk_hbm, dv_hbm,
             in_slots, out_slots, bias_ref, in_sem, out_sem):
        step, n_steps = pl.program_id(0), pl.num_programs(0)
        slot = step % 2

        def copies_in(of_step):
            s = of_step % 2
            return [copy for j, hbm in enumerate((do_hbm, del_hbm))
                    for copy in _sort_copies(hbm, of_step, in_slots.at[s, j], in_sem.at[s, j])]

        def copies_out(of_step):
            s = of_step % 2
            return [copy for j, hbm in enumerate((dq_hbm, dk_hbm, dv_hbm))
                    for copy in _unsort_copies(out_slots.at[s, j], hbm, of_step, out_sem.at[s, j])]

        @pl.when(step == 0)
        def _():
            for copy in copies_in(step):
                copy.start()
            _write_band_bias(bias_ref)

        @pl.when(step + 1 < n_steps)
        def _():
            for copy in copies_in(step + 1):
                copy.start()

        do_s, del_s = in_slots.at[slot, 0], in_slots.at[slot, 1]
        dq_s, dk_s, dv_s = (out_slots.at[slot, j] for j in range(3))
        dk_s[...] = jnp.zeros_like(dk_s)
        dv_s[...] = jnp.zeros_like(dv_s)
        for copy in copies_in(step):
            copy.wait()
        h0 = _head0_lanes()
        for pi, d in enumerate(DILATIONS):
            first = pi == 0

            def load(rows, keys, which, pi=pi):
                return dict(rows=rows, keys=keys, q=_take(q_ref, rows), g=_take(do_s, rows),
                            lse=_take(lse_ref, rows), delta=_take(del_s, rows),
                            k=_take(k_ref, keys).astype(BF16), v=_take(v_ref, keys).astype(BF16),
                            bias=bias_ref[2 * pi + which])

            def per_head(t):
                swapped = pltpu.roll(t, HEAD_DIM, 1)
                both = jnp.concatenate([jnp.where(h0, t, swapped), jnp.where(h0, swapped, t)], axis=0)
                return jnp.concatenate([both, both], axis=1)

            def compute(item):
                q2, g2 = _stack_heads(item["q"] * SCORE_SCALE, h0), _stack_heads(item["g"], h0)
                s = _mm_nt(q2, item["k"]) + item["bias"]
                p = jnp.exp(s - per_head(item["lse"]))
                dp = _mm_nt(g2, item["v"])
                ds = (p * (dp - per_head(item["delta"]))).astype(BF16)
                dq2 = _mm(ds, item["k"])
                dq = jnp.where(h0, dq2[:ATTN_BLOCK], dq2[ATTN_BLOCK:]) * SCORE_SCALE
                return dq, _mm_tn(ds, q2), _mm_tn(p.astype(BF16), g2)

            def store(item, res, first=first):
                _put(dq_s, item["rows"], res[0], add=not first)
                _put(dk_s, item["keys"], res[1], add=True)
                _put(dv_s, item["keys"], res[2], add=True)

            _for_each_group(d, ATTN_GROUP_BWD, load, compute, store)

        @pl.when(step > 0)
        def _():
            for copy in copies_out(step - 1):
                copy.wait()

        for copy in copies_out(step):
            copy.start()

        @pl.when(step == n_steps - 1)
        def _():
            for copy in copies_out(step):
                copy.wait()

    slab = lambda g: pl.BlockSpec((SEQ, LANES), functools.partial(lambda hp, g: (0, 4 * g + hp), g=g))
    anywhere = pl.BlockSpec(memory_space=pl.ANY)
    by_residue = (SORT_ROWS, SORT_RESIDUES, ATTN_WIDTH)
    grads = pl.pallas_call(
        body, name="attn_bwd", grid=(4,), out_shape=(jax.ShapeDtypeStruct(by_residue, F32),) * 3,
        scratch_shapes=[pltpu.VMEM((2, 2, SEQ, LANES), F32), pltpu.VMEM((2, 3, SEQ, LANES), F32),
                        pltpu.VMEM((2 * len(DILATIONS), 2 * ATTN_BLOCK, 2 * ATTN_BLOCK), F32),
                        pltpu.SemaphoreType.DMA((2, 2, SORT_RESIDUES)), pltpu.SemaphoreType.DMA((2, 3, SORT_RESIDUES))],
        in_specs=[slab(0), slab(1), slab(2), anywhere, slab(0), anywhere], out_specs=(anywhere,) * 3,
        compiler_params=_params(("arbitrary",)),
    )(qkv_sorted, qkv_sorted, qkv_sorted, d_out.reshape(by_residue), lse_sorted, delta.reshape(by_residue))
    return tuple(g.reshape(SEQ, ATTN_WIDTH) for g in grads)


def _hgrn_lower_bound(lb_ref):
    r0, r1 = lb_ref[0:1, :], lb_ref[1:2, :]
    mx = jnp.maximum(r0, r1)
    e0, e1 = jnp.exp(r0 - mx), jnp.exp(r1 - mx)
    return e0 / (e0 + e1)


def _hgrn_gates(hq, hf, lb):
    sq = _sigmoid(hq)
    sg = _sigmoid(hf)
    f = lb + (1.0 - lb) * sg
    return hq * sq, sq, sg, f, 1.0 - f, jnp.log(f)


HGRN_PAIR = 4
HGRN_SEQ_BLOCK = 1024
HGRN_GROUP = 4
HGRN_ROWS = HGRN_GROUP * HGRN_CHUNK


def _hgrn_specs(reverse):
    n_blocks = SEQ // HGRN_SEQ_BLOCK
    width = HGRN_PAIR * HGRN_DIM
    blk = (lambda s: n_blocks - 1 - s) if reverse else (lambda s: s)
    cols = lambda g: pl.BlockSpec((None, HGRN_SEQ_BLOCK, width), functools.partial(lambda p, s, g: (g, blk(s), p), g=g))
    pair = pl.BlockSpec((HGRN_SEQ_BLOCK, width), lambda p, s: (blk(s), p))
    lb = pl.BlockSpec((2, width), lambda p, s: (0, p))
    states = pl.BlockSpec((HGRN_PAIR, HGRN_SEQ_BLOCK // HGRN_CHUNK, HGRN_DIM, HGRN_DIM),
                          lambda p, s: (p, blk(s), 0, 0))
    return cols, pair, lb, states


def _chunk_masks():
    ri = lax.broadcasted_iota(jnp.int32, (HGRN_ROWS, HGRN_ROWS), 0)
    ci = lax.broadcasted_iota(jnp.int32, (HGRN_ROWS, HGRN_ROWS), 1)
    same = (ri // HGRN_CHUNK) == (ci // HGRN_CHUNK)
    return same, same & (ri >= ci), same & (ri <= ci)


def _mm_select(sel, v):
    hi = v.astype(BF16)
    r1 = v - hi.astype(F32)
    mid = r1.astype(BF16)
    lo = (r1 - mid.astype(F32)).astype(BF16)
    return _mm(sel, hi) + _mm(sel, mid) + _mm(sel, lo)


def _head_cols(a, h):
    return a[:, HGRN_DIM * h:HGRN_DIM * (h + 1)]


def _hgrn_fwd(proj, lb_raw):
    t, rws = HGRN_CHUNK, HGRN_ROWS

    def body(hq_ref, hf_ref, hi_ref, lb_ref, rec_ref, st_ref, state):
        @pl.when(pl.program_id(1) == 0)
        def _():
            state[...] = jnp.zeros_like(state)

        lb = _hgrn_lower_bound(lb_ref)
        same, causal, _ = _chunk_masks()
        sel = jnp.concatenate([causal, same], axis=0).astype(BF16)

        def group(g, sts):
            rows = pl.ds(pl.multiple_of(g * rws, rws), rws)
            q, _, _, _, k, lf = _hgrn_gates(hq_ref[rows, :], hf_ref[rows, :], lb)
            sums = _mm_select(sel, lf)
            cum, last = sums[:rws], sums[rws:]
            qd = (q * jnp.exp(cum)).astype(BF16)
            ki = (k * jnp.exp(-cum)).astype(BF16)
            ke = (k * jnp.exp(last - cum)).astype(BF16)
            vb = hi_ref[rows, :].astype(BF16)
            dec = jnp.exp(last)
            new_sts, recs = [], []
            for h in range(HGRN_PAIR):
                qd_h, ke_h, vb_h = _head_cols(qd, h), _head_cols(ke, h), _head_cols(vb, h)
                att = jnp.where(causal, _mm_nt(qd_h, _head_cols(ki, h)), 0.0).astype(BF16)
                intra = _mm(att, vb_h)
                st = sts[h]
                outs = []
                for c in range(HGRN_GROUP):
                    sl = slice(c * t, (c + 1) * t)
                    st_ref[h, g * HGRN_GROUP + c] = st
                    outs.append(intra[sl] + _mm_nt(qd_h[sl], st.astype(BF16)))
                    st = st * _head_cols(dec[c * t:c * t + 1, :], h) + _mm_tn(vb_h[sl], ke_h[sl])
                new_sts.append(st)
                recs.append(jnp.concatenate(outs, axis=0))
            rec_ref[rows, :] = jnp.concatenate(recs, axis=1)
            return tuple(new_sts)

        sts = lax.fori_loop(0, HGRN_SEQ_BLOCK // rws, group, tuple(state[h] for h in range(HGRN_PAIR)), unroll=True)
        for h in range(HGRN_PAIR):
            state[h] = sts[h]

    cols, pair, lb, states = _hgrn_specs(reverse=False)
    return pl.pallas_call(
        body, name="hgrn_fwd", grid=(HGRN_HEADS // HGRN_PAIR, SEQ // HGRN_SEQ_BLOCK),
        out_shape=(jax.ShapeDtypeStruct((SEQ, HGRN_WIDTH), F32),
                   jax.ShapeDtypeStruct((HGRN_HEADS, N_CHUNKS, HGRN_DIM, HGRN_DIM), F32)),
        in_specs=[cols(4), cols(5), cols(6), lb], out_specs=(pair, states),
        scratch_shapes=[pltpu.VMEM((HGRN_PAIR, HGRN_DIM, HGRN_DIM), F32)],
        compiler_params=_params(("parallel", "arbitrary")),
    )(proj, proj, proj, lb_raw)


def _hgrn_bwd(proj, lb_raw, d_rec, states):
    t, rws = HGRN_CHUNK, HGRN_ROWS

    def body(hq_ref, hf_ref, hi_ref, lb_ref, do_ref, st_ref, dhq_ref, dhf_ref, dhi_ref, dlb_ref,
             dstate, dlb_acc):
        lb = _hgrn_lower_bound(lb_ref)
        same, causal, anti = _chunk_masks()
        sel = jnp.concatenate([causal, same], axis=0).astype(BF16)
        sel_t = jnp.concatenate([anti, same], axis=1).astype(BF16)
        @pl.when(pl.program_id(1) == 0)
        def _():
            dstate[...] = jnp.zeros_like(dstate)
            dlb_acc[...] = jnp.zeros_like(dlb_acc)

        n_groups = HGRN_SEQ_BLOCK // rws
        chunks = [slice(c * t, (c + 1) * t) for c in range(HGRN_GROUP)]

        def group(i, dsts_in):
            g = n_groups - 1 - i
            rows = pl.ds(pl.multiple_of(g * rws, rws), rws)
            hq = hq_ref[rows, :]
            q, sq, sg, f, k, lf = _hgrn_gates(hq, hf_ref[rows, :], lb)
            sums = _mm_select(sel, lf)
            cum, last = sums[:rws], sums[rws:]
            e_cum, e_inv, e_end, dec = jnp.exp(cum), jnp.exp(-cum), jnp.exp(last - cum), jnp.exp(last)
            qd, ki, ke = q * e_cum, k * e_inv, k * e_end
            qdb, kib, keb = qd.astype(BF16), ki.astype(BF16), ke.astype(BF16)
            vb = hi_ref[rows, :].astype(BF16)
            gb = do_ref[rows, :]

            dsts_out, per_head = [], []
            for h in range(HGRN_PAIR):
                qdb_h, kib_h, keb_h = _head_cols(qdb, h), _head_cols(kib, h), _head_cols(keb, h)
                vb_h, gb_h = _head_cols(vb, h), _head_cols(gb, h)
                att = jnp.where(causal, _mm_nt(qdb_h, kib_h), 0.0).astype(BF16)
                datt = jnp.where(causal, _mm_nt(gb_h, vb_h), 0.0).astype(BF16)
                dv = _mm_tn(att, gb_h)
                dqd = _mm(datt, kib_h)
                dki = _mm_tn(datt, qdb_h)

                decs = [_head_cols(dec[c * t:c * t + 1, :], h) for c in range(HGRN_GROUP)]
                dsts = [None] * HGRN_GROUP
                dst = dsts_in[h]
                for c in reversed(range(HGRN_GROUP)):
                    dsts[c] = dst
                    dst = dst * decs[c] + _mm_tn(gb_h[chunks[c]], qdb_h[chunks[c]])
                dsts_out.append(dst)

                dv_x, dqd_x, dke, dlast_x = [], [], [], []
                for c, sl in enumerate(chunks):
                    st_prev = st_ref[h, g * HGRN_GROUP + c]
                    dstb = dsts[c].astype(BF16)
                    dv_x.append(_mm_nt(keb_h[sl], dstb))
                    dqd_x.append(_mm(gb_h[sl], st_prev.astype(BF16)))
                    dke.append(_mm(vb_h[sl], dstb))
                    ddec = jnp.sum(dsts[c] * st_prev, axis=0, keepdims=True)
                    dlast_x.append(jnp.broadcast_to(ddec * decs[c], (t, HGRN_DIM)))
                per_head.append((dv + jnp.concatenate(dv_x, axis=0), dqd + jnp.concatenate(dqd_x, axis=0),
                                 dki, jnp.concatenate(dke, axis=0), jnp.concatenate(dlast_x, axis=0)))
            dv, dqd, dki, dke, dlast = (jnp.concatenate(list(parts), axis=1) for parts in zip(*per_head))

            dq = dqd * e_cum
            dk = dki * e_inv + dke * e_end
            dke_ke = dke * ke
            dcum = dqd * qd - dki * ki - dke_ke
            dlf = _mm_select(sel_t, jnp.concatenate([dcum, dke_ke], axis=0)) + dlast
            df = dlf / f - dk
            dhq_ref[rows, :] = (dq * (sq * (1.0 + hq * (1.0 - sq)))).astype(BF16)
            dhf_ref[rows, :] = (df * (1.0 - lb) * (sg * (1.0 - sg))).astype(BF16)
            dhi_ref[rows, :] = dv.astype(BF16)
            dlb_acc[...] += jnp.sum(df * (1.0 - sg), axis=0, keepdims=True)
            return tuple(dsts_out)

        dsts = lax.fori_loop(0, n_groups, group, tuple(dstate[h] for h in range(HGRN_PAIR)), unroll=True)
        for h in range(HGRN_PAIR):
            dstate[h] = dsts[h]
        g0 = dlb_acc[...] * lb * (1.0 - lb)
        dlb_ref[...] = jnp.concatenate([g0, -g0], axis=0)

    cols, pair, lb_spec, st_spec = _hgrn_specs(reverse=True)
    wide = jax.ShapeDtypeStruct((SEQ, HGRN_WIDTH), BF16)
    return pl.pallas_call(
        body, name="hgrn_bwd", grid=(HGRN_HEADS // HGRN_PAIR, SEQ // HGRN_SEQ_BLOCK),
        out_shape=(wide, wide, wide, jax.ShapeDtypeStruct((2, HGRN_WIDTH), F32)),
        in_specs=[cols(4), cols(5), cols(6), lb_spec, pair, st_spec],
        out_specs=(pair, pair, pair, lb_spec),
        scratch_shapes=[pltpu.VMEM((HGRN_PAIR, HGRN_DIM, HGRN_DIM), F32),
                        pltpu.VMEM((1, HGRN_PAIR * HGRN_DIM), F32)],
        compiler_params=_params(("parallel", "arbitrary")),
    )(proj, proj, proj, lb_raw, d_rec, states)


def _group_sum(v, group):
    parts = []
    for s in range(v.shape[1] // LANES):
        slab = v[:, LANES * s:LANES * (s + 1)]
        if group == LANES:
            parts.append(jnp.broadcast_to(jnp.sum(slab, axis=-1, keepdims=True), slab.shape))
        else:
            h0 = lax.broadcasted_iota(jnp.int32, slab.shape, 1) < HEAD_DIM
            s0 = jnp.sum(jnp.where(h0, slab, 0.0), axis=-1, keepdims=True)
            s1 = jnp.sum(jnp.where(h0, 0.0, slab), axis=-1, keepdims=True)
            parts.append(jnp.where(h0, s0, s1))
    return jnp.concatenate(parts, axis=1)


def _mid(attn_o, rec, proj, x, target, w_out_g, attn_w, hgrn_w, final_w):
    tm = 512

    def branch_fwd(o, gate, w, group):
        r = lax.rsqrt(_group_sum(o * o, group) * (1.0 / group) + NORM_EPS)
        nrm = o * r
        sg = _sigmoid(gate)
        return r, nrm, sg, nrm * w * (gate * sg)

    def branch_bwd(dy, r, nrm, sg, gate, w, group):
        silu = gate * sg
        d_gate = dy * nrm * w * (sg * (1.0 + gate * (1.0 - sg)))
        d_w = jnp.sum(dy * nrm * silu, axis=0, keepdims=True)
        dn = dy * w * silu
        d_o = r * (dn - nrm * (_group_sum(dn * nrm, group) * (1.0 / group)))
        return d_o, d_gate, d_w

    def body(o_ref, rec_ref, ag_ref, hg_ref, x_ref, tgt_ref, wout_ref, aw_ref, hw_ref, fw_ref,
             dx2_ref, do_ref, delta_ref, dag_ref, drec_ref, dhg_ref, dwout_ref, dfw_ref, daw_ref, dhw_ref,
             loss_ref, dwout_acc):
        i = pl.program_id(0)

        @pl.when(i == 0)
        def _():
            dwout_acc[...] = jnp.zeros_like(dwout_acc)
            dfw_ref[...] = jnp.zeros_like(dfw_ref)
            daw_ref[...] = jnp.zeros_like(daw_ref)
            dhw_ref[...] = jnp.zeros_like(dhw_ref)
            loss_ref[...] = jnp.zeros_like(loss_ref)

        o, rc, ag, hg = o_ref[...], rec_ref[...], ag_ref[...], hg_ref[...]
        aw, hw, fw = aw_ref[...], hw_ref[...], fw_ref[...]
        ra, na, sga, ya = branch_fwd(o, ag, aw, HEAD_DIM)
        rh, nh, sgh, yh = branch_fwd(rc, hg, hw, HGRN_DIM)
        mixed = jnp.concatenate([ya, yh], axis=1).astype(BF16)
        wout = wout_ref[...]
        x2 = x_ref[...] + _mm(mixed, wout)
        rstd = lax.rsqrt(jnp.mean(x2 * x2, axis=-1, keepdims=True) + NORM_EPS)
        xn = x2 * rstd
        err = xn * fw - tgt_ref[...]
        row_loss = jnp.mean(err * err, axis=-1, keepdims=True)
        loss_ref[...] += 0.5 * jnp.sum(row_loss, axis=0, keepdims=True)
        dy = err * (1.0 / D_MODEL)
        dfw_ref[...] += jnp.sum(dy * xn, axis=0, keepdims=True)
        dxn = dy * fw
        dx2 = rstd * (dxn - xn * jnp.mean(dxn * xn, axis=-1, keepdims=True))
        dx2_ref[...] = dx2
        dx2b = dx2.astype(BF16)
        dwout_acc[...] += _mm_tn(mixed, dx2b)

        @pl.when(i == pl.num_programs(0) - 1)
        def _():
            dwout_ref[...] = dwout_acc[...].astype(BF16)

        dmixed = _mm_nt(dx2b, wout)

        d_o, d_ag, d_aw = branch_bwd(dmixed[:, :ATTN_WIDTH], ra, na, sga, ag, aw, HEAD_DIM)
        d_rec, d_hg, d_hw = branch_bwd(dmixed[:, ATTN_WIDTH:], rh, nh, sgh, hg, hw, HGRN_DIM)
        do_ref[...] = d_o
        delta_ref[...] = _group_sum(d_o * o, HEAD_DIM)
        dag_ref[...] = d_ag.astype(BF16)
        drec_ref[...] = d_rec.astype(BF16)
        dhg_ref[...] = d_hg.astype(BF16)
        daw_ref[...] += d_aw
        dhw_ref[...] += d_hw

    half = lambda: pl.BlockSpec((tm, COL_BLOCK), lambda i: (i, 0))
    full = lambda: pl.BlockSpec((tm, D_MODEL), lambda i: (i, 0))
    fixed = lambda r, c: pl.BlockSpec((r, c), lambda i: (0, 0))
    wide = jax.ShapeDtypeStruct((SEQ, COL_BLOCK), F32)
    wide_b = jax.ShapeDtypeStruct((SEQ, COL_BLOCK), BF16)
    return pl.pallas_call(
        body, name="mid", grid=(SEQ // tm,),
        out_shape=(jax.ShapeDtypeStruct((SEQ, D_MODEL), F32), wide, wide, wide_b, wide_b, wide_b,
                   jax.ShapeDtypeStruct((D_MODEL, D_MODEL), BF16),
                   jax.ShapeDtypeStruct((1, D_MODEL), F32), jax.ShapeDtypeStruct((1, COL_BLOCK), F32),
                   jax.ShapeDtypeStruct((1, COL_BLOCK), F32), jax.ShapeDtypeStruct((1, 1), F32)),
        scratch_shapes=[pltpu.VMEM((D_MODEL, D_MODEL), F32)],
        in_specs=[half(), half(),
                  pl.BlockSpec((None, tm, COL_BLOCK), lambda i: (3, i, 0)),
                  pl.BlockSpec((None, tm, COL_BLOCK), lambda i: (7, i, 0)),
                  full(), full(), fixed(D_MODEL, D_MODEL), fixed(1, COL_BLOCK), fixed(1, COL_BLOCK),
                  fixed(1, D_MODEL)],
        out_specs=(full(), half(), half(), half(), half(), half(), fixed(D_MODEL, D_MODEL),
                   fixed(1, D_MODEL), fixed(1, COL_BLOCK), fixed(1, COL_BLOCK), fixed(1, 1)),
        compiler_params=_params(("arbitrary",)),
    )(attn_o, rec, proj, proj, x, target, w_out_g, attn_w, hgrn_w, final_w)


def _in_proj_bwd_rows(d_groups, w_g, x, dx2, mix_w, rc, rsa, rsb):
    tm = 512

    ring = 3

    def body(*refs):
        dg_refs = refs[:N_DEV]
        (wg_ref, x_hbm, dx2_hbm, w_ref, c_ref, sa_ref, sb_ref, gx_ref, dpb_ref, dmw_ref,
         ring_s, ring_sem) = refs[N_DEV:]
        step, n_steps = pl.program_id(0), pl.num_programs(0)

        def fetch(of_step):
            rows = pl.ds(of_step * tm if isinstance(of_step, int) else pl.multiple_of(of_step * tm, tm), tm)
            slot = of_step % ring
            return [pltpu.make_async_copy(src.at[rows, :], ring_s.at[slot, j], ring_sem.at[slot, j])
                    for j, src in enumerate((x_hbm, dx2_hbm))]

        @pl.when(step == 0)
        def _():
            for first in range(ring - 1):
                for copy in fetch(first):
                    copy.start()
            dmw_ref[...] = jnp.zeros_like(dmw_ref)

        @pl.when(step + ring - 1 < n_steps)
        def _():
            for copy in fetch(step + ring - 1):
                copy.start()

        for copy in fetch(step):
            copy.wait()

        parts = []
        for j in range(N_DEV):
            dp = dg_refs[j][...]
            if j < 2:
                dp = _rot_transposed(dp, c_ref[...], sa_ref[...], sb_ref[...])
            parts.append(dp.astype(BF16))
        dpb = jnp.concatenate(parts, axis=1)
        for j in range(N_DEV):
            dpb_ref[j] = parts[j]
        g = _mm_nt(dpb, wg_ref[...])
        slot = step % ring
        xf = ring_s[slot, 0]
        rstd = lax.rsqrt(jnp.mean(xf * xf, axis=-1, keepdims=True) + NORM_EPS)
        xn = xf * rstd
        dmw_ref[...] += jnp.sum(g * xn, axis=0, keepdims=True)
        gw = g * w_ref[...]
        gx_ref[...] = ring_s[slot, 1] + rstd * (gw - xn * jnp.mean(gw * xn, axis=-1, keepdims=True))

    tile = lambda cols: pl.BlockSpec((tm, cols), lambda i: (i, 0))
    fixed = lambda r, c: pl.BlockSpec((r, c), lambda i: (0, 0))
    return pl.pallas_call(
        body, name="in_proj_bwd_rows", grid=(SEQ // tm,),
        out_shape=(jax.ShapeDtypeStruct((SEQ, D_MODEL), F32), jax.ShapeDtypeStruct((N_DEV, SEQ, COL_BLOCK), BF16),
                   jax.ShapeDtypeStruct((1, D_MODEL), F32)),
        in_specs=[tile(COL_BLOCK) for _ in range(N_DEV)] + [
            pl.BlockSpec((D_MODEL, IN_COLS), lambda i: (0, 0), pipeline_mode=pl.Buffered(1)),
            pl.BlockSpec(memory_space=pl.ANY), pl.BlockSpec(memory_space=pl.ANY),
            fixed(1, D_MODEL), tile(LANES), tile(LANES), tile(LANES)],
        out_specs=(tile(D_MODEL), pl.BlockSpec((N_DEV, tm, COL_BLOCK), lambda i: (0, i, 0)), fixed(1, D_MODEL)),
        scratch_shapes=[pltpu.VMEM((ring, 2, tm, D_MODEL), F32), pltpu.SemaphoreType.DMA((ring, 2))],
        compiler_params=_params(("arbitrary",)),
    )(*d_groups, w_g, x, dx2, mix_w, rc, rsa, rsb)


def _weights_exchange(hn_t, dproj_b, dwout_p, small_p):
    n_chips = N_DEV // 2
    rb = 128
    S1_IN, S1_OUT, SMALL, S2_IN, S2_OUT, VIA_IN, VIA_OUT = 0, 4, 8, 15, 17, 19, 21
    rel_of_pair = (3, 1, 2, 0)
    two_hop = n_chips - 1
    half_in, half_out = COL_BLOCK // 2, D_MODEL // 2

    def body(order_ref, hnt_ref, dp_ref, dwout_ref, small_ref, gin_ref, gout_ref, gs_ref,
             part, s1_send, s1_in, s1_out, fwd_in, fwd_out, s2_in, s2_out, via_in, via_out, land_s,
             send_sems, recv_sems):
        t = pl.program_id(0)
        me = _my_place()
        x, y, c = me
        my_chip = 2 * x + y
        sibling = (x, y, 1 - c)

        def remote(slot, src, dst, to):
            return pltpu.make_async_remote_copy(src_ref=src, dst_ref=dst, send_sem=send_sems.at[slot],
                                                recv_sem=recv_sems.at[slot], device_id=to, device_id_type=MESH)

        def s1_in_copy(pair):
            return remote(S1_IN + pair, s1_send.at[pair], s1_in.at[pair], sibling)

        def s1_out_copy(pair):
            q = my_chip ^ rel_of_pair[pair]
            return remote(S1_OUT + pair, dwout_ref.at[q, 1 - c], s1_out.at[pair], sibling)

        def s2_copies(rel):
            peer = _peer(me, 2 * rel)
            return [remote(S2_IN + rel - 1, fwd_in.at[rel - 1], s2_in.at[rel - 1], peer),
                    remote(S2_OUT + rel - 1, fwd_out.at[rel - 1], s2_out.at[rel - 1], peer)]

        def via_copies(k):
            peer = _peer(me, 2 * (2 - k))
            return [remote(VIA_IN + k, fwd_in.at[two_hop - 1, :, pl.ds(k * half_in, half_in)], via_in.at[k], peer),
                    remote(VIA_OUT + k, fwd_out.at[two_hop - 1, :, pl.ds(k * half_out, half_out)], via_out.at[k],
                           peer)]

        def small_copy(rel):
            return remote(SMALL + rel - 1, small_ref, land_s.at[rel], _peer(me, rel))

        @pl.when(t == 0)
        def _():
            land_s[0] = small_ref[...]
            for pair in range(n_chips):
                s1_out_copy(pair).start()
            for rel in range(1, N_DEV):
                small_copy(rel).start()

        part[...] = _mm(hnt_ref[...], dp_ref[...])

        def rows_loop(n_rows, fn):
            def step(b, carry):
                fn(pl.ds(pl.multiple_of(b * rb, rb), rb))
                return carry
            lax.fori_loop(0, n_rows // rb, step, 0)

        for pair, rel in enumerate(rel_of_pair):
            @pl.when(t == 2 * pair)
            def _(pair=pair):
                s1_send[pair] = part[...].astype(BF16)
                s1_in_copy(pair).start()

            @pl.when(t == 2 * pair + 1)
            def _(pair=pair, rel=rel):
                q = my_chip ^ rel
                s1_in_copy(pair).wait_recv()
                s1_out_copy(pair).wait_recv()
                dst_in = fwd_in.at[rel - 1] if rel else gin_ref
                dst_out = fwd_out.at[rel - 1] if rel else gout_ref
                passes_on = rel in (1, 2)
                if passes_on:
                    for cp in via_copies(rel - 1):
                        cp.wait_recv()

                def with_half(val, via, rows, width):
                    if not passes_on:
                        return val
                    extra = via[rel - 1, rows, :].astype(F32)
                    halves = [val[:, :width], val[:, width:]]
                    halves[rel - 1] = halves[rel - 1] + extra
                    return jnp.concatenate(halves, axis=1)

                def add_in(rows):
                    val = part[rows, :] + s1_in[pair, rows, :].astype(F32)
                    dst_in[rows, :] = with_half(val, via_in, rows, half_in).astype(dst_in.dtype)

                def add_out(rows):
                    val = dwout_ref[q, c, rows, :].astype(F32) + s1_out[pair, rows, :].astype(F32)
                    dst_out[rows, :] = with_half(val, via_out, rows, half_out).astype(dst_out.dtype)

                rows_loop(D_MODEL, add_in)
                rows_loop(WOUT_ROWS, add_out)
                if rel == two_hop:
                    for k in range(2):
                        for cp in via_copies(k):
                            cp.start()
                elif rel:
                    for cp in s2_copies(rel):
                        cp.start()

        @pl.when(t == N_DEV - 1)
        def _():
            for rel in range(1, two_hop):
                for cp in s2_copies(rel):
                    cp.wait_recv()

            def total_in(rows):
                g = gin_ref[rows, :]
                for rel in range(1, two_hop):
                    g = g + s2_in[rel - 1, rows, :].astype(F32)
                gin_ref[rows, :] = g

            def total_out(rows):
                g = gout_ref[rows, :]
                for rel in range(1, two_hop):
                    g = g + s2_out[rel - 1, rows, :].astype(F32)
                gout_ref[rows, :] = g

            rows_loop(D_MODEL, total_in)
            rows_loop(WOUT_ROWS, total_out)

            for rel in range(1, N_DEV):
                small_copy(rel).wait_recv()
            my_flat = _flat(me)
            g = land_s[my_flat ^ 0]
            for dev in range(1, N_DEV):
                g = g + land_s[my_flat ^ dev]
            gs_ref[...] = g

            for pair in range(n_chips):
                s1_in_copy(pair).wait_send()
                s1_out_copy(pair).wait_send()
            for rel in range(1, two_hop):
                for cp in s2_copies(rel) + via_copies(rel - 1):
                    cp.wait_send()
            for rel in range(1, N_DEV):
                small_copy(rel).wait_send()

    place_x, place_y, place_c = _my_place()
    my_chip = 2 * place_x + place_y
    order = jnp.stack([2 * (my_chip ^ rel) + core for rel in rel_of_pair
                       for core in (1 - place_c, place_c)]).astype(jnp.int32)

    whole = lambda: pl.BlockSpec(memory_space=pltpu.VMEM)
    in_blocks = lambda n: pltpu.VMEM((n, D_MODEL, COL_BLOCK), BF16)
    out_blocks = lambda n: pltpu.VMEM((n, WOUT_ROWS, D_MODEL), BF16)
    grid_spec = pltpu.PrefetchScalarGridSpec(
        num_scalar_prefetch=1, grid=(N_DEV,),
        in_specs=[pl.BlockSpec((D_MODEL, SEQ), lambda t, order: (0, 0), pipeline_mode=pl.Buffered(1)),
                  pl.BlockSpec((None, SEQ, COL_BLOCK), lambda t, order: (order[t], 0, 0)), whole(), whole()],
        out_specs=(whole(), whole(), whole()),
        scratch_shapes=[pltpu.VMEM((D_MODEL, COL_BLOCK), F32), in_blocks(n_chips), in_blocks(n_chips),
                        out_blocks(n_chips), in_blocks(n_chips - 1), out_blocks(n_chips - 1),
                        in_blocks(n_chips - 2), out_blocks(n_chips - 2),
                        pltpu.VMEM((2, D_MODEL, half_in), BF16), pltpu.VMEM((2, WOUT_ROWS, half_out), BF16),
                        pltpu.VMEM((N_DEV, SMALL_ROWS, LANES), F32),
                        pltpu.SemaphoreType.DMA((23,)), pltpu.SemaphoreType.DMA((23,))])
    return pl.pallas_call(
        body, name="weights_exchange", grid_spec=grid_spec,
        out_shape=(jax.ShapeDtypeStruct((D_MODEL, COL_BLOCK), F32), jax.ShapeDtypeStruct((WOUT_ROWS, D_MODEL), F32),
                   jax.ShapeDtypeStruct((SMALL_ROWS, LANES), F32)),
        compiler_params=_params(("arbitrary",)),
    )(order, hn_t, dproj_b, dwout_p.reshape(n_chips, 2, WOUT_ROWS, D_MODEL), small_p)


def _adamw(w, g, m, v):
    m = ADAM_B1 * m + (1.0 - ADAM_B1) * g
    v = ADAM_B2 * v + (1.0 - ADAM_B2) * (g * g)
    m_hat = m / (1.0 - ADAM_B1 ** ADAM_STEP)
    v_hat = v / (1.0 - ADAM_B2 ** ADAM_STEP)
    delta = -ADAM_LR * (m_hat / (jnp.sqrt(v_hat) + ADAM_EPS) + ADAM_WD * w)
    return delta, m, v


def _adamw_update(grads, weights, m_old, v_old):
    rb = 256

    def body(*refs):
        g_refs, w_refs, m_refs, v_refs = refs[0:3], refs[3:6], refs[6:9], refs[9:12]
        d_refs, nm_refs, nv_refs = refs[12:15], refs[15:18], refs[18:21]
        for k in range(3):
            n_rows = g_refs[k].shape[0]
            step_rows = min(rb, n_rows)

            def step(b, carry, k=k, step_rows=step_rows):
                rows = pl.ds(pl.multiple_of(b * step_rows, 8), step_rows)
                delta, nm, nv = _adamw(w_refs[k][rows, :], g_refs[k][rows, :], m_refs[k][rows, :], v_refs[k][rows, :])
                d_refs[k][rows, :] = delta
                nm_refs[k][rows, :] = nm
                nv_refs[k][rows, :] = nv
                return carry

            lax.fori_loop(0, n_rows // step_rows, step, 0)

    shapes = tuple(jax.ShapeDtypeStruct(g.shape, F32) for g in grads)
    vm = lambda: pl.BlockSpec(memory_space=pltpu.VMEM)
    outs = pl.pallas_call(
        body, name="adamw_update", out_shape=shapes * 3,
        in_specs=[vm() for _ in range(12)], out_specs=tuple(vm() for _ in range(9)),
        compiler_params=_params(),
    )(*grads, *weights, *m_old, *v_old)
    return outs[0:3], outs[3:6], outs[6:9]


def _pack_small(mix, attn, hgrn, lb, final, loss=None):
    def rows8(a):
        a = a.reshape(-1, LANES)
        return jnp.pad(a, ((0, 8 - a.shape[0]), (0, 0)))
    last = jnp.zeros((8, LANES), F32) if loss is None else jnp.pad(loss.reshape(1, 1), ((0, 7), (0, LANES - 1)))
    return jnp.concatenate([rows8(mix), rows8(attn), rows8(hgrn), rows8(lb), rows8(final), last], axis=0)


def _unpack_small(slab):
    return (slab[ROW_MIX:ROW_MIX + 8].reshape(1, D_MODEL), slab[ROW_ATTN:ROW_ATTN + 4].reshape(1, ATTN_WIDTH),
            slab[ROW_HGRN:ROW_HGRN + 4].reshape(1, HGRN_WIDTH), slab[ROW_LB:ROW_LB + 8].reshape(2, HGRN_WIDTH),
            slab[ROW_FINAL:ROW_FINAL + 8].reshape(D_MODEL))


def _rope(pos_row):
    j = np.arange(ROPE_ROWS)
    inv = np.where(j < ROPE_HALF, ROPE_THETA ** (-(j % ROPE_HALF) * (2.0 / ROPE_DIMS)), 0.0)
    e = np.arange(LANES) % HEAD_DIM
    hit = (j[:, None] == (e % ROPE_HALF)[None, :]) & (j[:, None] < ROPE_HALF)
    sel = np.stack([hit & (e < ROPE_DIMS), hit & (e >= ROPE_HALF) & (e < ROPE_DIMS),
                    -1.0 * (hit & (e < ROPE_HALF))]).astype(np.float32)
    return _rope_tables(pos_row, jnp.asarray(inv.astype(np.float32).reshape(ROPE_ROWS, 1)),
                        jnp.asarray(sel, dtype=BF16))


def _local_step(x, proj, qkv_sorted, w_in_g, w_out_g, tables, mix_w, attn_w, hgrn_w, lb_raw, final_w, target):
    rc, rsa, rsb = tables
    attn_o, lse = _attn_fwd_fused(qkv_sorted)
    rec, states = _hgrn_fwd(proj, lb_raw)

    (dx2, d_o, delta, d_ag, d_rec, d_hg, dwout_p, d_final, d_attn_w, d_hgrn_w, loss) = _mid(
        attn_o, rec, proj, x, target, w_out_g, attn_w, hgrn_w, final_w.reshape(1, D_MODEL))

    dqkv = _attn_bwd_fused(qkv_sorted, d_o, lse, delta)
    d_hq, d_hf, d_hi, d_lb = _hgrn_bwd(proj, lb_raw, d_rec, states)

    grad_x, dproj_b, d_mix = _in_proj_bwd_rows(
        (dqkv[0], dqkv[1], dqkv[2], d_ag, d_hq, d_hf, d_hi, d_hg), w_in_g, x, dx2, mix_w, rc, rsa, rsb)
    small_p = _pack_small(d_mix, d_attn_w, d_hgrn_w, d_lb, d_final, loss)
    return grad_x, dproj_b, dwout_p, small_p


def kernel(x, positions, w_in, w_out, mix_norm_w, attn_out_norm_w, hgrn_out_norm_w, hgrn_lb_raw, final_norm_w, loss_target, m_w_in, m_w_out, m_mix_norm_w, m_attn_out_norm_w, m_hgrn_out_norm_w, m_hgrn_lb_raw, m_final_norm_w, v_w_in, v_w_out, v_mix_norm_w, v_attn_out_norm_w, v_hgrn_out_norm_w, v_hgrn_lb_raw, v_final_norm_w):
    tables = _rope(positions)
    proj, hn_t, w_in_g, w_out_g, qkv_sorted = _gather_project(x[0], mix_norm_w, w_in[0], w_out[0], *tables)
    grad_x, dproj_b, dwout_p, small_p = _local_step(
        x[0], proj, qkv_sorted, w_in_g, w_out_g, tables, mix_norm_w, attn_out_norm_w, hgrn_out_norm_w,
        hgrn_lb_raw, final_norm_w, loss_target[0])
    g_in, g_out, g_s = _weights_exchange(hn_t, dproj_b, dwout_p, small_p)

    w_s = _pack_small(mix_norm_w, attn_out_norm_w, hgrn_out_norm_w, hgrn_lb_raw, final_norm_w)
    m_s = _pack_small(m_mix_norm_w, m_attn_out_norm_w, m_hgrn_out_norm_w, m_hgrn_lb_raw, m_final_norm_w)
    v_s = _pack_small(v_mix_norm_w, v_attn_out_norm_w, v_hgrn_out_norm_w, v_hgrn_lb_raw, v_final_norm_w)
    (d_in, d_out, d_s), (nm_in, nm_out, nm_s), (nv_in, nv_out, nv_s) = _adamw_update(
        (g_in, g_out, g_s), (w_in[0], w_out[0], w_s), (m_w_in[0], m_w_out[0], m_s), (v_w_in[0], v_w_out[0], v_s))

    loss = g_s[ROW_LOSS, 0]
    return (loss, grad_x[None], g_in[None], g_out[None], *_unpack_small(g_s),
            d_in[None], d_out[None], *_unpack_small(d_s),
            nm_in[None], nm_out[None], *_unpack_small(nm_s),
            nv_in[None], nv_out[None], *_unpack_small(nv_s))
```

```python
import functools

import jax
import jax.numpy as jnp
import numpy as np
from jax import lax
from jax.experimental import pallas as pl
from jax.experimental.pallas import tpu as pltpu

F32 = jnp.float32
BF16 = jnp.bfloat16

SEQ = 4096
D_MODEL = 1024
ATTN_WIDTH = 512
HGRN_WIDTH = 512
HEAD_DIM = 64
HGRN_HEADS = 4
HGRN_DIM = 128
HGRN_CHUNK = 64
N_CHUNKS = SEQ // HGRN_CHUNK
IN_COLS = 4096
COL_BLOCK = 512
N_DEV = 8
WOUT_ROWS = D_MODEL // N_DEV
ATTN_BLOCK = 128
DILATIONS = (1, 4, 16)
ROPE_THETA = 500000.0
ROPE_DIMS = 16
ROPE_HALF = 8
NORM_EPS = 1e-6
NEG_BIG = -1e30
LANES = 128

ADAM_LR = 0.001
ADAM_B1 = 0.9
ADAM_B2 = 0.999
ADAM_EPS = 1e-08
ADAM_WD = 0.01
ADAM_STEP = 10

SMALL_ROWS = 48
ROW_MIX, ROW_ATTN, ROW_HGRN, ROW_LB, ROW_FINAL, ROW_LOSS = 0, 8, 16, 24, 32, 40

VMEM_LIMIT = 56 * 1024 * 1024
MESH = pl.DeviceIdType.MESH


def _mm(a, b):
    return lax.dot_general(a, b, (((1,), (0,)), ((), ())), preferred_element_type=F32)


def _mm_nt(a, b):
    return lax.dot_general(a, b, (((1,), (1,)), ((), ())), preferred_element_type=F32)


def _mm_tn(a, b):
    return lax.dot_general(a, b, (((0,), (0,)), ((), ())), preferred_element_type=F32)


def _mm_exact(a, b):
    return lax.dot_general(a, b, (((1,), (0,)), ((), ())), preferred_element_type=F32,
                           precision=lax.Precision.HIGHEST)


def _sigmoid(v):
    return 1.0 / (1.0 + jnp.exp(-v))


def _params(sem=None, **kw):
    return pltpu.CompilerParams(dimension_semantics=sem, vmem_limit_bytes=VMEM_LIMIT, **kw)


def _my_place():
    return lax.axis_index("x"), lax.axis_index("y"), lax.axis_index("c")


def _peer(place, rel):
    x, y, c = place
    return (x ^ ((rel >> 2) & 1), y ^ ((rel >> 1) & 1), c ^ (rel & 1))


def _flat(place):
    x, y, c = place
    return 4 * x + 2 * y + c


ROPE_ROWS = 16


def _rope_tables(pos_row, inv_freq_col, selectors):
    def body(pos_ref, invf_ref, sel_ref, c_ref, sa_ref, sb_ref):
        ang = pos_ref[...].astype(F32) * invf_ref[...]
        cos, sin = jnp.cos(ang), jnp.sin(ang)

        def spread(v, sel):
            hi = v.astype(BF16)
            r1 = v - hi.astype(F32)
            mid = r1.astype(BF16)
            lo = (r1 - mid.astype(F32)).astype(BF16)
            return _mm_tn(hi, sel) + _mm_tn(mid, sel) + _mm_tn(lo, sel)

        e = lax.broadcasted_iota(jnp.int32, (1, LANES), 1) & (HEAD_DIM - 1)
        c_ref[...] = spread(cos, sel_ref[0]) + jnp.where(e < ROPE_DIMS, 0.0, 1.0)
        sa_ref[...] = spread(sin, sel_ref[1])
        sb_ref[...] = spread(sin, sel_ref[2])

    tab = jax.ShapeDtypeStruct((SEQ, LANES), F32)
    vm = lambda: pl.BlockSpec(memory_space=pltpu.VMEM)
    return pl.pallas_call(
        body, name="rope_tables", out_shape=(tab, tab, tab),
        in_specs=[vm(), vm(), vm()], out_specs=(vm(), vm(), vm()), compiler_params=_params(),
    )(pos_row, inv_freq_col, selectors)


def _per_slab(fn, t):
    return jnp.concatenate([fn(t[:, LANES * s:LANES * (s + 1)]) for s in range(t.shape[1] // LANES)], axis=1)


def _rot(t, c, sa, sb):
    return _per_slab(lambda u: u * c + pltpu.roll(u, ROPE_HALF, 1) * sa + pltpu.roll(u, LANES - ROPE_HALF, 1) * sb, t)


def _rot_transposed(g, c, sa, sb):
    return _per_slab(
        lambda u: u * c + pltpu.roll(u * sa, LANES - ROPE_HALF, 1) + pltpu.roll(u * sb, ROPE_HALF, 1), g)


def _gather_project(x, mix_w, w_in, w_out, rc, rsa, rsb):
    tm = 1024
    n_tiles = SEQ // tm
    arrival_of_step = (None, 0, 1, 2, 4, 5, 3, 6)

    def body(order_ref, x_ref, w_ref, win_ref, wout_ref, c_ref, sa_ref, sb_ref,
             proj_ref, hnt_ref, gin_hbm, gout_hbm, qkv_hbm,
             hn_s, w_land, wout_land, stage, sort_stage, send_sems, recv_sems, local_sems, sort_sems):
        g, i = pl.program_id(0), pl.program_id(1)
        me = _my_place()
        x_, y_, c_ = me
        sibling = (x_, y_, 1 - c_)
        chips = [(1 - x_, y_), (x_, 1 - y_), (1 - x_, 1 - y_)]

        def slab(which, place):
            idx = _flat(place)
            if which == 0:
                return w_land.at[idx]
            return wout_land.at[pl.ds(pl.multiple_of(idx * WOUT_ROWS, WOUT_ROWS), WOUT_ROWS), :]

        def remote(which, k, ref, to, src=None):
            return pltpu.make_async_remote_copy(
                src_ref=ref if src is None else src, dst_ref=ref, send_sem=send_sems.at[8 * which + k],
                recv_sem=recv_sems.at[8 * which + k], device_id=to, device_id_type=MESH)

        def copy(which, k, block, to, src=None):
            return remote(which, k, slab(which, block), to, src)

        def half(which, place, part):
            n = (D_MODEL if which == 0 else WOUT_ROWS) // 2
            if which == 0:
                return w_land.at[_flat(place), pl.ds(n * part, n), :]
            return wout_land.at[pl.ds(pl.multiple_of(_flat(place) * WOUT_ROWS + n * part, n), n), :]

        def first_copies(which):
            src = stage if which == 0 else None
            return ([copy(which, 0, me, sibling, src)]
                    + [copy(which, 1 + j, me, (*chips[j], c_), src) for j in range(2)])

        def relay(which, part):
            frm, to = (chips[1], chips[0]) if part == 0 else (chips[0], chips[1])
            return remote(which, 3 if part == 0 else 7, half(which, (*frm, c_), part), (*to, c_))

        def two_hop_half(which, part):
            return remote(which, 3 if part == 0 else 7, half(which, (*chips[2], c_), part), me)

        def pass_on(which, j):
            return copy(which, 4 + j, (*chips[j], c_), sibling)

        def arrival(which, k):
            if k == 0:
                return copy(which, 0, sibling, me)
            if k <= 2:
                return copy(which, k, (*chips[k - 1], c_), me)
            return copy(which, k, (*chips[k - 4], 1 - c_), me)

        def to_hbm(step):
            idx = order_ref[step]
            cols = pl.ds(pl.multiple_of(idx * COL_BLOCK, COL_BLOCK), COL_BLOCK)
            return pltpu.make_async_copy(w_land.at[idx], gin_hbm.at[:, cols], local_sems.at[step])

        @pl.when((g == 0) & (i == 0))
        def _():
            stage[...] = win_ref[...].astype(BF16)
            w_land[_flat(me)] = stage[...]
            wout_land[pl.ds(pl.multiple_of(_flat(me) * WOUT_ROWS, WOUT_ROWS), WOUT_ROWS), :] = (
                wout_ref[...].astype(BF16))
            for cp in first_copies(0) + first_copies(1)[:1]:
                cp.start()
            to_hbm(0).start()

        for step, k in enumerate(arrival_of_step):
            if k is None:
                continue

            @pl.when((g == step) & (i == 0))
            def _(k=k, step=step):
                if k == 3:
                    two_hop_half(0, 0).wait_recv()
                    two_hop_half(0, 1).wait_recv()
                else:
                    arrival(0, k).wait_recv()
                to_hbm(step).start()
                if 1 <= k <= 3:
                    pass_on(0, k - 1).start()
                if k == 1:
                    relay(0, 1).start()
                    for cp in first_copies(1)[1:]:
                        cp.start()
                if k == 2:
                    relay(0, 0).start()
                if k in (4, 5):
                    arrival(1, k - 3).wait_recv()
                    relay(1, 5 - k).start()

        rows = pl.ds(pl.multiple_of(i * tm, tm), tm)

        @pl.when(g == 0)
        def _():
            xf = x_ref[...]
            ms = jnp.mean(xf * xf, axis=-1, keepdims=True)
            hn = xf * lax.rsqrt(ms + NORM_EPS) * w_ref[...]
            hnt_ref[...] = hn.T.astype(BF16)
            hn_s[rows, :] = hn.astype(BF16)

        group = order_ref[g]

        def sorted_copy(tile_value):
            per = tm // SORT_RESIDUES
            cols = pl.ds(pl.multiple_of(group * COL_BLOCK, COL_BLOCK), COL_BLOCK)
            buf = i % 2

            def out_copies(tile, b):
                return [pltpu.make_async_copy(
                    sort_stage.at[b, :, r, :], qkv_hbm.at[r, pl.ds(pl.multiple_of(tile * per, per), per), cols],
                    sort_sems.at[b, r]) for r in range(SORT_RESIDUES)]

            @pl.when(i >= 2)
            def _():
                for copy in out_copies(i - 2, buf):
                    copy.wait()

            sort_stage[buf] = tile_value.reshape(per, SORT_RESIDUES, COL_BLOCK)
            for n, copy in enumerate(out_copies(i, buf)):
                copy.start(priority=n % 2)

            @pl.when(i == n_tiles - 1)
            def _():
                for copy in out_copies(i - 1, 1 - buf) + out_copies(i, buf):
                    copy.wait()

        @pl.when(group < 2)
        def _():
            rotated = _rot(_mm(hn_s[rows, :], w_land[group]), c_ref[...], sa_ref[...], sb_ref[...])
            proj_ref[...] = rotated
            sorted_copy(rotated)

        @pl.when(group == 2)
        def _():
            value = _mm(hn_s[rows, :], w_land[group])
            proj_ref[...] = value
            sorted_copy(value)

        @pl.when(group > 2)
        def _():
            proj_ref[...] = _mm(hn_s[rows, :], w_land[group])

        @pl.when((g == N_DEV - 1) & (i == n_tiles - 1))
        def _():
            pass_on(1, 0).start()
            pass_on(1, 1).start()
            two_hop_half(1, 0).wait_recv()
            two_hop_half(1, 1).wait_recv()
            pass_on(1, 2).start()
            for k in (0, 4, 5, 6):
                arrival(1, k).wait_recv()
            for which in (0, 1):
                for cp in (first_copies(which) + [relay(which, part) for part in range(2)]
                           + [pass_on(which, j) for j in range(3)]):
                    cp.wait_send()
            wout_copy = pltpu.make_async_copy(wout_land, gout_hbm, local_sems.at[N_DEV])
            wout_copy.start()
            for step in range(N_DEV):
                to_hbm(step).wait()
            wout_copy.wait()

    me = _my_place()
    x_, y_, c_ = me
    chips = [(1 - x_, y_), (x_, 1 - y_), (1 - x_, 1 - y_)]
    order = jnp.stack([_flat(p) for p in (
        me, (x_, y_, 1 - c_), (*chips[0], c_), (*chips[1], c_), (*chips[0], 1 - c_), (*chips[1], 1 - c_),
        (*chips[2], c_), (*chips[2], 1 - c_))]).astype(jnp.int32)

    first_sweep = lambda g, i, order: (jnp.where(g == 0, i, n_tiles - 1), 0)
    tab = pl.BlockSpec((tm, LANES), lambda g, i, order: (jnp.where(order[g] < 2, i, 0), 0))
    whole = lambda: pl.BlockSpec(memory_space=pltpu.VMEM)
    grid_spec = pltpu.PrefetchScalarGridSpec(
        num_scalar_prefetch=1, grid=(N_DEV, n_tiles),
        in_specs=[pl.BlockSpec((tm, D_MODEL), first_sweep),
                  pl.BlockSpec((1, D_MODEL), lambda g, i, order: (0, 0)),
                  whole(), whole(), tab, tab, tab],
        out_specs=(pl.BlockSpec((None, tm, COL_BLOCK), lambda g, i, order: (order[g], i, 0)),
                   pl.BlockSpec((D_MODEL, tm), lambda g, i, order: (0, jnp.where(g == 0, i, n_tiles - 1))),
                   pl.BlockSpec(memory_space=pl.ANY), pl.BlockSpec(memory_space=pl.ANY),
                   pl.BlockSpec(memory_space=pl.ANY)),
        scratch_shapes=[pltpu.VMEM((SEQ, D_MODEL), BF16),
                        pltpu.VMEM((N_DEV, D_MODEL, COL_BLOCK), BF16),
                        pltpu.VMEM((D_MODEL, D_MODEL), BF16),
                        pltpu.VMEM((D_MODEL, COL_BLOCK), BF16),
                        pltpu.VMEM((2, tm // SORT_RESIDUES, SORT_RESIDUES, COL_BLOCK), F32),
                        pltpu.SemaphoreType.DMA((16,)), pltpu.SemaphoreType.DMA((16,)),
                        pltpu.SemaphoreType.DMA((N_DEV + 1,)), pltpu.SemaphoreType.DMA((2, SORT_RESIDUES))])
    proj, hn_t, w_in_g, w_out_g, qkv_sorted = pl.pallas_call(
        body, name="gather_project", grid_spec=grid_spec,
        out_shape=(jax.ShapeDtypeStruct((N_DEV, SEQ, COL_BLOCK), F32), jax.ShapeDtypeStruct((D_MODEL, SEQ), BF16),
                   jax.ShapeDtypeStruct((D_MODEL, IN_COLS), BF16), jax.ShapeDtypeStruct((D_MODEL, D_MODEL), BF16),
                   jax.ShapeDtypeStruct((SORT_RESIDUES, SORT_ROWS, 3 * COL_BLOCK), F32)),
        compiler_params=_params(("arbitrary", "arbitrary")),
    )(order, x, mix_w, w_in, w_out, rc, rsa, rsb)
    return proj, hn_t, w_in_g, w_out_g, qkv_sorted.reshape(SEQ, 3 * COL_BLOCK)


SCORE_SCALE = HEAD_DIM ** -0.5
ATTN_GROUP_FWD = 32
ATTN_GROUP_BWD = 16
BLOCKS_PER_PATTERN = SEQ // ATTN_BLOCK
SORT_RESIDUES = 16
SORT_ROWS = SEQ // SORT_RESIDUES


def _write_band_bias(bias_ref):
    row = lax.broadcasted_iota(jnp.int32, (2 * ATTN_BLOCK, 2 * ATTN_BLOCK), 0) & (ATTN_BLOCK - 1)
    col = lax.broadcasted_iota(jnp.int32, (2 * ATTN_BLOCK, 2 * ATTN_BLOCK), 1)
    for pi, d in enumerate(DILATIONS):
        per = SORT_RESIDUES // d
        ahead = per * (row % (8 * d) - col % (16 * d)) + (row // (8 * d) - col // (16 * d))
        dist = ATTN_BLOCK + ahead
        bias_ref[2 * pi] = jnp.where((dist >= 0) & (dist <= ATTN_BLOCK), 0.0, NEG_BIG)
        bias_ref[2 * pi + 1] = jnp.where(ahead >= 0, 0.0, NEG_BIG)


def _head0_lanes():
    return lax.broadcasted_iota(jnp.int32, (ATTN_BLOCK, LANES), 1) < HEAD_DIM


def _stack_heads(t, h0):
    return jnp.concatenate([jnp.where(h0, t, 0.0), jnp.where(h0, 0.0, t)], axis=0).astype(BF16)


def _block_runs(i, d):
    nblk = BLOCKS_PER_PATTERN // d
    r, n = i // nblk, i % nblk
    kn = jnp.maximum(n - 1, 0)
    rows, keys = [], []
    for c in range(SORT_RESIDUES // d):
        base = SORT_ROWS * (c * d + r)
        rows.append(pl.ds(pl.multiple_of(base + 8 * d * n, 8), 8 * d))
        keys.append(pl.ds(pl.multiple_of(base + 8 * d * kn, 8), 16 * d))
    return rows, keys, (n == 0).astype(jnp.int32)


def _take(ref, runs):
    return jnp.concatenate([ref[run, :] for run in runs], axis=0)


def _put(ref, runs, value, add=False):
    at = 0
    for run in runs:
        piece = value[at:at + run.size]
        if add:
            ref[run, :] += piece
        else:
            ref[run, :] = piece
        at += run.size


def _sort_copies(src_hbm, lane_block, dst_ref, sem_ref):
    lanes = pl.ds(pl.multiple_of(LANES * lane_block, LANES), LANES)
    return [pltpu.make_async_copy(src_hbm.at[:, r, lanes], dst_ref.at[pl.ds(SORT_ROWS * r, SORT_ROWS), :],
                                  sem_ref.at[r]) for r in range(SORT_RESIDUES)]


def _unsort_copies(src_ref, dst_hbm, lane_block, sem_ref):
    lanes = pl.ds(pl.multiple_of(LANES * lane_block, LANES), LANES)
    return [pltpu.make_async_copy(src_ref.at[pl.ds(SORT_ROWS * r, SORT_ROWS), :], dst_hbm.at[:, r, lanes],
                                  sem_ref.at[r]) for r in range(SORT_RESIDUES)]


def _for_each_group(d, n_group, load, compute, store):
    def group(g, carry):
        items = [load(*_block_runs(g * n_group + u, d)) for u in range(n_group)]
        results = [compute(item) for item in items]
        for item, res in zip(items, results):
            store(item, res)
        return carry

    lax.fori_loop(0, BLOCKS_PER_PATTERN // n_group, group, 0)


def _attn_fwd_fused(qkv_sorted):
    n_pat = len(DILATIONS)
    tile2 = (2 * ATTN_BLOCK, LANES)

    def body(q_ref, k_ref, v_ref, o_hbm, lse_ref, o_slots, m_acc, l_acc, bias_ref, out_sem):
        step, n_steps = pl.program_id(0), pl.num_programs(0)
        pl.when(step == 0)(lambda: _write_band_bias(bias_ref))
        slot = step % 2
        o_acc = o_slots.at[slot]
        h0 = _head0_lanes()
        for pi, d in enumerate(DILATIONS):
            first, last = pi == 0, pi == n_pat - 1

            def load(rows, keys, which, first=first, pi=pi):
                item = dict(rows=rows, keys=keys, which=2 * pi + which)
                if not first:
                    item.update(o=_take(o_acc, rows), m=[_take(m_acc.at[h], rows) for h in range(2)],
                                l=[_take(l_acc.at[h], rows) for h in range(2)])
                return item

            def compute(item, first=first):
                kb = _take(k_ref, item["keys"]).astype(BF16)
                vb = _take(v_ref, item["keys"]).astype(BF16)
                s = _mm_nt(_stack_heads(_take(q_ref, item["rows"]) * SCORE_SCALE, h0), kb) + bias_ref[item["which"]]
                mb = jnp.max(s, axis=-1, keepdims=True)
                if first:
                    p = jnp.exp(s - mb)
                    mn = jnp.broadcast_to(mb, tile2)
                else:
                    m_old = jnp.concatenate(item["m"], axis=0)
                    mn = jnp.maximum(m_old, mb)
                    alpha = jnp.exp(m_old - mn)
                    p = jnp.exp(s - jnp.concatenate([mn, mn], axis=1))
                ls = jnp.sum(p, axis=-1, keepdims=True)
                pv = _mm(p.astype(BF16), vb)
                if first:
                    return pv, mn, jnp.broadcast_to(ls, tile2)
                o_old = jnp.concatenate([item["o"], item["o"]], axis=0)
                return alpha * o_old + pv, mn, alpha * jnp.concatenate(item["l"], axis=0) + ls

            def store(item, res, last=last):
                rows = item["rows"]
                (o0, o1), (m0, m1), (l0, l1) = ((a[:ATTN_BLOCK], a[ATTN_BLOCK:]) for a in res)
                if last:
                    _put(o_acc, rows, jnp.where(h0, o0 / l0, o1 / l1))
                    _put(lse_ref, rows, jnp.where(h0, m0 + jnp.log(l0), m1 + jnp.log(l1)))
                else:
                    _put(o_acc, rows, jnp.where(h0, o0, o1))
                    for h, (m, l) in enumerate(((m0, l0), (m1, l1))):
                        _put(m_acc.at[h], rows, m)
                        _put(l_acc.at[h], rows, l)

            _for_each_group(d, ATTN_GROUP_FWD, load, compute, store)

        def copies_out(of_step):
            return _unsort_copies(o_slots.at[of_step % 2], o_hbm, of_step, out_sem.at[of_step % 2])

        @pl.when(step > 0)
        def _():
            for copy in copies_out(step - 1):
                copy.wait()

        for n, copy in enumerate(copies_out(step)):
            copy.start(priority=n % 2)

        @pl.when(step == n_steps - 1)
        def _():
            for copy in copies_out(step):
                copy.wait()

    slab = lambda g: pl.BlockSpec((SEQ, LANES), functools.partial(lambda hp, g: (0, 4 * g + hp), g=g))
    wide = jax.ShapeDtypeStruct((SEQ, ATTN_WIDTH), F32)
    o_rows, lse = pl.pallas_call(
        body, name="attn_fwd", grid=(4,),
        out_shape=(jax.ShapeDtypeStruct((SORT_ROWS, SORT_RESIDUES, ATTN_WIDTH), F32), wide),
        in_specs=[slab(0), slab(1), slab(2)], out_specs=(pl.BlockSpec(memory_space=pl.ANY), slab(0)),
        scratch_shapes=[pltpu.VMEM((2, SEQ, LANES), F32), pltpu.VMEM((2, SEQ, LANES), F32),
                        pltpu.VMEM((2, SEQ, LANES), F32),
                        pltpu.VMEM((2 * len(DILATIONS), 2 * ATTN_BLOCK, 2 * ATTN_BLOCK), F32),
                        pltpu.SemaphoreType.DMA((2, SORT_RESIDUES))],
        compiler_params=_params(("arbitrary",)),
    )(qkv_sorted, qkv_sorted, qkv_sorted)
    return o_rows.reshape(SEQ, ATTN_WIDTH), lse


def _attn_bwd_fused(qkv_sorted, d_out, lse_sorted, delta):
    def body(q_ref, k_ref, v_ref, do_hbm, lse_ref, del_hbm, dq_hbm, dk_hbm, dv_hbm,
             in_slots, out_slots, bias_ref, in_sem, out_sem):
        step, n_steps = pl.program_id(0), pl.num_programs(0)
        slot = step % 2

        def copies_in(of_step):
            s = of_step % 2
            return [copy for j, hbm in enumerate((do_hbm, del_hbm))
                    for copy in _sort_copies(hbm, of_step, in_slots.at[s, j], in_sem.at[s, j])]

        def copies_out(of_step):
            s = of_step % 2
            return [copy for j, hbm in enumerate((dq_hbm, dk_hbm, dv_hbm))
                    for copy in _unsort_copies(out_slots.at[s, j], hbm, of_step, out_sem.at[s, j])]

        @pl.when(step == 0)
        def _():
            for n, copy in enumerate(copies_in(step)):
                copy.start(priority=n % 2)
            _write_band_bias(bias_ref)

        @pl.when(step + 1 < n_steps)
        def _():
            for n, copy in enumerate(copies_in(step + 1)):
                copy.start(priority=n % 2)

        do_s, del_s = in_slots.at[slot, 0], in_slots.at[slot, 1]
        dq_s, dk_s, dv_s = (out_slots.at[slot, j] for j in range(3))
        dk_s[...] = jnp.zeros_like(dk_s)
        dv_s[...] = jnp.zeros_like(dv_s)
        for copy in copies_in(step):
            copy.wait()
        h0 = _head0_lanes()
        for pi, d in enumerate(DILATIONS):
            first = pi == 0

            def load(rows, keys, which, pi=pi):
                return dict(rows=rows, keys=keys, q=_take(q_ref, rows), g=_take(do_s, rows),
                            lse=_take(lse_ref, rows), delta=_take(del_s, rows),
                            k=_take(k_ref, keys).astype(BF16), v=_take(v_ref, keys).astype(BF16),
                            bias=bias_ref[2 * pi + which])

            def per_head(t):
                swapped = pltpu.roll(t, HEAD_DIM, 1)
                both = jnp.concatenate([jnp.where(h0, t, swapped), jnp.where(h0, swapped, t)], axis=0)
                return jnp.concatenate([both, both], axis=1)

            def compute(item):
                q2, g2 = _stack_heads(item["q"] * SCORE_SCALE, h0), _stack_heads(item["g"], h0)
                s = _mm_nt(q2, item["k"]) + item["bias"]
                p = jnp.exp(s - per_head(item["lse"]))
                dp = _mm_nt(g2, item["v"])
                ds = (p * (dp - per_head(item["delta"]))).astype(BF16)
                dq2 = _mm(ds, item["k"])
                dq = jnp.where(h0, dq2[:ATTN_BLOCK], dq2[ATTN_BLOCK:]) * SCORE_SCALE
                return dq, _mm_tn(ds, q2), _mm_tn(p.astype(BF16), g2)

            def store(item, res, first=first):
                _put(dq_s, item["rows"], res[0], add=not first)
                _put(dk_s, item["keys"], res[1], add=True)
                _put(dv_s, item["keys"], res[2], add=True)

            _for_each_group(d, ATTN_GROUP_BWD, load, compute, store)

        @pl.when(step > 0)
        def _():
            for copy in copies_out(step - 1):
                copy.wait()

        for n, copy in enumerate(copies_out(step)):
            copy.start(priority=n % 2)

        @pl.when(step == n_steps - 1)
        def _():
            for copy in copies_out(step):
                copy.wait()

    slab = lambda g: pl.BlockSpec((SEQ, LANES), functools.partial(lambda hp, g: (0, 4 * g + hp), g=g))
    anywhere = pl.BlockSpec(memory_space=pl.ANY)
    by_residue = (SORT_ROWS, SORT_RESIDUES, ATTN_WIDTH)
    grads = pl.pallas_call(
        body, name="attn_bwd", grid=(4,), out_shape=(jax.ShapeDtypeStruct(by_residue, F32),) * 3,
        scratch_shapes=[pltpu.VMEM((2, 2, SEQ, LANES), F32), pltpu.VMEM((2, 3, SEQ, LANES), F32),
                        pltpu.VMEM((2 * len(DILATIONS), 2 * ATTN_BLOCK, 2 * ATTN_BLOCK), F32),
                        pltpu.SemaphoreType.DMA((2, 2, SORT_RESIDUES)), pltpu.SemaphoreType.DMA((2, 3, SORT_RESIDUES))],
        in_specs=[slab(0), slab(1), slab(2), anywhere, slab(0), anywhere], out_specs=(anywhere,) * 3,
        compiler_params=_params(("arbitrary",)),
    )(qkv_sorted, qkv_sorted, qkv_sorted, d_out.reshape(by_residue), lse_sorted, delta.reshape(by_residue))
    return tuple(g.reshape(SEQ, ATTN_WIDTH) for g in grads)


def _hgrn_lower_bound(lb_ref):
    r0, r1 = lb_ref[0:1, :], lb_ref[1:2, :]
    mx = jnp.maximum(r0, r1)
    e0, e1 = jnp.exp(r0 - mx), jnp.exp(r1 - mx)
    return e0 / (e0 + e1)


def _hgrn_gates(hq, hf, lb):
    sq = _sigmoid(hq)
    sg = _sigmoid(hf)
    f = lb + (1.0 - lb) * sg
    return hq * sq, sq, sg, f, 1.0 - f, jnp.log(f)


HGRN_PAIR = 4
HGRN_SEQ_BLOCK = 1024
HGRN_GROUP = 4
HGRN_ROWS = HGRN_GROUP * HGRN_CHUNK


def _hgrn_specs(reverse):
    n_blocks = SEQ // HGRN_SEQ_BLOCK
    width = HGRN_PAIR * HGRN_DIM
    blk = (lambda s: n_blocks - 1 - s) if reverse else (lambda s: s)
    cols = lambda g: pl.BlockSpec((None, HGRN_SEQ_BLOCK, width), functools.partial(lambda p, s, g: (g, blk(s), p), g=g))
    pair = pl.BlockSpec((HGRN_SEQ_BLOCK, width), lambda p, s: (blk(s), p))
    lb = pl.BlockSpec((2, width), lambda p, s: (0, p))
    states = pl.BlockSpec((HGRN_PAIR, HGRN_SEQ_BLOCK // HGRN_CHUNK, HGRN_DIM, HGRN_DIM),
                          lambda p, s: (p, blk(s), 0, 0))
    return cols, pair, lb, states


def _chunk_masks():
    ri = lax.broadcasted_iota(jnp.int32, (HGRN_ROWS, HGRN_ROWS), 0)
    ci = lax.broadcasted_iota(jnp.int32, (HGRN_ROWS, HGRN_ROWS), 1)
    same = (ri // HGRN_CHUNK) == (ci // HGRN_CHUNK)
    return same, same & (ri >= ci), same & (ri <= ci)


def _mm_select(sel, v):
    hi = v.astype(BF16)
    r1 = v - hi.astype(F32)
    mid = r1.astype(BF16)
    lo = (r1 - mid.astype(F32)).astype(BF16)
    return _mm(sel, hi) + _mm(sel, mid) + _mm(sel, lo)


def _head_cols(a, h):
    return a[:, HGRN_DIM * h:HGRN_DIM * (h + 1)]


def _hgrn_fwd(proj, lb_raw):
    t, rws = HGRN_CHUNK, HGRN_ROWS

    def body(hq_ref, hf_ref, hi_ref, lb_ref, rec_ref, st_ref, state):
        @pl.when(pl.program_id(1) == 0)
        def _():
            state[...] = jnp.zeros_like(state)

        lb = _hgrn_lower_bound(lb_ref)
        same, causal, _ = _chunk_masks()
        sel = jnp.concatenate([causal, same], axis=0).astype(BF16)

        def group(g, sts):
            rows = pl.ds(pl.multiple_of(g * rws, rws), rws)
            q, _, _, _, k, lf = _hgrn_gates(hq_ref[rows, :], hf_ref[rows, :], lb)
            sums = _mm_select(sel, lf)
            cum, last = sums[:rws], sums[rws:]
            qd = (q * jnp.exp(cum)).astype(BF16)
            ki = (k * jnp.exp(-cum)).astype(BF16)
            ke = (k * jnp.exp(last - cum)).astype(BF16)
            vb = hi_ref[rows, :].astype(BF16)
            dec = jnp.exp(last)
            new_sts, recs = [], []
            for h in range(HGRN_PAIR):
                qd_h, ke_h, vb_h = _head_cols(qd, h), _head_cols(ke, h), _head_cols(vb, h)
                att = jnp.where(causal, _mm_nt(qd_h, _head_cols(ki, h)), 0.0).astype(BF16)
                intra = _mm(att, vb_h)
                st = sts[h]
                outs = []
                for c in range(HGRN_GROUP):
                    sl = slice(c * t, (c + 1) * t)
                    st_ref[h, g * HGRN_GROUP + c] = st
                    outs.append(intra[sl] + _mm_nt(qd_h[sl], st.astype(BF16)))
                    st = st * _head_cols(dec[c * t:c * t + 1, :], h) + _mm_tn(vb_h[sl], ke_h[sl])
                new_sts.append(st)
                recs.append(jnp.concatenate(outs, axis=0))
            rec_ref[rows, :] = jnp.concatenate(recs, axis=1)
            return tuple(new_sts)

        sts = lax.fori_loop(0, HGRN_SEQ_BLOCK // rws, group, tuple(state[h] for h in range(HGRN_PAIR)), unroll=True)
        for h in range(HGRN_PAIR):
            state[h] = sts[h]

    cols, pair, lb, states = _hgrn_specs(reverse=False)
    return pl.pallas_call(
        body, name="hgrn_fwd", grid=(HGRN_HEADS // HGRN_PAIR, SEQ // HGRN_SEQ_BLOCK),
        out_shape=(jax.ShapeDtypeStruct((SEQ, HGRN_WIDTH), F32),
                   jax.ShapeDtypeStruct((HGRN_HEADS, N_CHUNKS, HGRN_DIM, HGRN_DIM), F32)),
        in_specs=[cols(4), cols(5), cols(6), lb], out_specs=(pair, states),
        scratch_shapes=[pltpu.VMEM((HGRN_PAIR, HGRN_DIM, HGRN_DIM), F32)],
        compiler_params=_params(("parallel", "arbitrary")),
    )(proj, proj, proj, lb_raw)


def _hgrn_bwd(proj, lb_raw, d_rec, states):
    t, rws = HGRN_CHUNK, HGRN_ROWS

    def body(hq_ref, hf_ref, hi_ref, lb_ref, do_ref, st_ref, dhq_ref, dhf_ref, dhi_ref, dlb_ref,
             dstate, dlb_acc):
        lb = _hgrn_lower_bound(lb_ref)
        same, causal, anti = _chunk_masks()
        sel = jnp.concatenate([causal, same], axis=0).astype(BF16)
        sel_t = jnp.concatenate([anti, same], axis=1).astype(BF16)
        @pl.when(pl.program_id(1) == 0)
        def _():
            dstate[...] = jnp.zeros_like(dstate)
            dlb_acc[...] = jnp.zeros_like(dlb_acc)

        n_groups = HGRN_SEQ_BLOCK // rws
        chunks = [slice(c * t, (c + 1) * t) for c in range(HGRN_GROUP)]

        def group(i, dsts_in):
            g = n_groups - 1 - i
            rows = pl.ds(pl.multiple_of(g * rws, rws), rws)
            hq = hq_ref[rows, :]
            q, sq, sg, f, k, lf = _hgrn_gates(hq, hf_ref[rows, :], lb)
            sums = _mm_select(sel, lf)
            cum, last = sums[:rws], sums[rws:]
            e_cum, e_inv, e_end, dec = jnp.exp(cum), jnp.exp(-cum), jnp.exp(last - cum), jnp.exp(last)
            qd, ki, ke = q * e_cum, k * e_inv, k * e_end
            qdb, kib, keb = qd.astype(BF16), ki.astype(BF16), ke.astype(BF16)
            vb = hi_ref[rows, :].astype(BF16)
            gb = do_ref[rows, :]

            dsts_out, per_head = [], []
            for h in range(HGRN_PAIR):
                qdb_h, kib_h, keb_h = _head_cols(qdb, h), _head_cols(kib, h), _head_cols(keb, h)
                vb_h, gb_h = _head_cols(vb, h), _head_cols(gb, h)
                att = jnp.where(causal, _mm_nt(qdb_h, kib_h), 0.0).astype(BF16)
                datt = jnp.where(causal, _mm_nt(gb_h, vb_h), 0.0).astype(BF16)
                dv = _mm_tn(att, gb_h)
                dqd = _mm(datt, kib_h)
                dki = _mm_tn(datt, qdb_h)

                decs = [_head_cols(dec[c * t:c * t + 1, :], h) for c in range(HGRN_GROUP)]
                dsts = [None] * HGRN_GROUP
                dst = dsts_in[h]
                for c in reversed(range(HGRN_GROUP)):
                    dsts[c] = dst
                    dst = dst * decs[c] + _mm_tn(gb_h[chunks[c]], qdb_h[chunks[c]])
                dsts_out.append(dst)

                dv_x, dqd_x, dke, dlast_x = [], [], [], []
                for c, sl in enumerate(chunks):
                    st_prev = st_ref[h, g * HGRN_GROUP + c]
                    dstb = dsts[c].astype(BF16)
                    dv_x.append(_mm_nt(keb_h[sl], dstb))
                    dqd_x.append(_mm(gb_h[sl], st_prev.astype(BF16)))
                    dke.append(_mm(vb_h[sl], dstb))
                    ddec = jnp.sum(dsts[c] * st_prev, axis=0, keepdims=True)
                    dlast_x.append(jnp.broadcast_to(ddec * decs[c], (t, HGRN_DIM)))
                per_head.append((dv + jnp.concatenate(dv_x, axis=0), dqd + jnp.concatenate(dqd_x, axis=0),
                                 dki, jnp.concatenate(dke, axis=0), jnp.concatenate(dlast_x, axis=0)))
            dv, dqd, dki, dke, dlast = (jnp.concatenate(list(parts), axis=1) for parts in zip(*per_head))

            dq = dqd * e_cum
            dk = dki * e_inv + dke * e_end
            dke_ke = dke * ke
            dcum = dqd * qd - dki * ki - dke_ke
            dlf = _mm_select(sel_t, jnp.concatenate([dcum, dke_ke], axis=0)) + dlast
            df = dlf / f - dk
            dhq_ref[rows, :] = (dq * (sq * (1.0 + hq * (1.0 - sq)))).astype(BF16)
            dhf_ref[rows, :] = (df * (1.0 - lb) * (sg * (1.0 - sg))).astype(BF16)
            dhi_ref[rows, :] = dv.astype(BF16)
            dlb_acc[...] += jnp.sum(df * (1.0 - sg), axis=0, keepdims=True)
            return tuple(dsts_out)

        dsts = lax.fori_loop(0, n_groups, group, tuple(dstate[h] for h in range(HGRN_PAIR)), unroll=True)
        for h in range(HGRN_PAIR):
            dstate[h] = dsts[h]
        g0 = dlb_acc[...] * lb * (1.0 - lb)
        dlb_ref[...] = jnp.concatenate([g0, -g0], axis=0)

    cols, pair, lb_spec, st_spec = _hgrn_specs(reverse=True)
    wide = jax.ShapeDtypeStruct((SEQ, HGRN_WIDTH), BF16)
    return pl.pallas_call(
        body, name="hgrn_bwd", grid=(HGRN_HEADS // HGRN_PAIR, SEQ // HGRN_SEQ_BLOCK),
        out_shape=(wide, wide, wide, jax.ShapeDtypeStruct((2, HGRN_WIDTH), F32)),
        in_specs=[cols(4), cols(5), cols(6), lb_spec, pair, st_spec],
        out_specs=(pair, pair, pair, lb_spec),
        scratch_shapes=[pltpu.VMEM((HGRN_PAIR, HGRN_DIM, HGRN_DIM), F32),
                        pltpu.VMEM((1, HGRN_PAIR * HGRN_DIM), F32)],
        compiler_params=_params(("parallel", "arbitrary")),
    )(proj, proj, proj, lb_raw, d_rec, states)


def _group_sum(v, group):
    parts = []
    for s in range(v.shape[1] // LANES):
        slab = v[:, LANES * s:LANES * (s + 1)]
        if group == LANES:
            parts.append(jnp.broadcast_to(jnp.sum(slab, axis=-1, keepdims=True), slab.shape))
        else:
            h0 = lax.broadcasted_iota(jnp.int32, slab.shape, 1) < HEAD_DIM
            s0 = jnp.sum(jnp.where(h0, slab, 0.0), axis=-1, keepdims=True)
            s1 = jnp.sum(jnp.where(h0, 0.0, slab), axis=-1, keepdims=True)
            parts.append(jnp.where(h0, s0, s1))
    return jnp.concatenate(parts, axis=1)


def _mid(attn_o, rec, proj, x, target, w_out_g, attn_w, hgrn_w, final_w):
    tm = 512

    def branch_fwd(o, gate, w, group):
        r = lax.rsqrt(_group_sum(o * o, group) * (1.0 / group) + NORM_EPS)
        nrm = o * r
        sg = _sigmoid(gate)
        return r, nrm, sg, nrm * w * (gate * sg)

    def branch_bwd(dy, r, nrm, sg, gate, w, group):
        silu = gate * sg
        d_gate = dy * nrm * w * (sg * (1.0 + gate * (1.0 - sg)))
        d_w = jnp.sum(dy * nrm * silu, axis=0, keepdims=True)
        dn = dy * w * silu
        d_o = r * (dn - nrm * (_group_sum(dn * nrm, group) * (1.0 / group)))
        return d_o, d_gate, d_w

    def body(o_ref, rec_ref, ag_ref, hg_ref, x_ref, tgt_ref, wout_ref, aw_ref, hw_ref, fw_ref,
             dx2_ref, do_ref, delta_ref, dag_ref, drec_ref, dhg_ref, dwout_ref, dfw_ref, daw_ref, dhw_ref,
             loss_ref, dwout_acc):
        i = pl.program_id(0)

        @pl.when(i == 0)
        def _():
            dwout_acc[...] = jnp.zeros_like(dwout_acc)
            dfw_ref[...] = jnp.zeros_like(dfw_ref)
            daw_ref[...] = jnp.zeros_like(daw_ref)
            dhw_ref[...] = jnp.zeros_like(dhw_ref)
            loss_ref[...] = jnp.zeros_like(loss_ref)

        o, rc, ag, hg = o_ref[...], rec_ref[...], ag_ref[...], hg_ref[...]
        aw, hw, fw = aw_ref[...], hw_ref[...], fw_ref[...]
        ra, na, sga, ya = branch_fwd(o, ag, aw, HEAD_DIM)
        rh, nh, sgh, yh = branch_fwd(rc, hg, hw, HGRN_DIM)
        mixed = jnp.concatenate([ya, yh], axis=1).astype(BF16)
        wout = wout_ref[...]
        x2 = x_ref[...] + _mm(mixed, wout)
        rstd = lax.rsqrt(jnp.mean(x2 * x2, axis=-1, keepdims=True) + NORM_EPS)
        xn = x2 * rstd
        err = xn * fw - tgt_ref[...]
        row_loss = jnp.mean(err * err, axis=-1, keepdims=True)
        loss_ref[...] += 0.5 * jnp.sum(row_loss, axis=0, keepdims=True)
        dy = err * (1.0 / D_MODEL)
        dfw_ref[...] += jnp.sum(dy * xn, axis=0, keepdims=True)
        dxn = dy * fw
        dx2 = rstd * (dxn - xn * jnp.mean(dxn * xn, axis=-1, keepdims=True))
        dx2_ref[...] = dx2
        dx2b = dx2.astype(BF16)
        dwout_acc[...] += _mm_tn(mixed, dx2b)

        @pl.when(i == pl.num_programs(0) - 1)
        def _():
            dwout_ref[...] = dwout_acc[...].astype(BF16)

        dmixed = _mm_nt(dx2b, wout)

        d_o, d_ag, d_aw = branch_bwd(dmixed[:, :ATTN_WIDTH], ra, na, sga, ag, aw, HEAD_DIM)
        d_rec, d_hg, d_hw = branch_bwd(dmixed[:, ATTN_WIDTH:], rh, nh, sgh, hg, hw, HGRN_DIM)
        do_ref[...] = d_o
        delta_ref[...] = _group_sum(d_o * o, HEAD_DIM)
        dag_ref[...] = d_ag.astype(BF16)
        drec_ref[...] = d_rec.astype(BF16)
        dhg_ref[...] = d_hg.astype(BF16)
        daw_ref[...] += d_aw
        dhw_ref[...] += d_hw

    half = lambda: pl.BlockSpec((tm, COL_BLOCK), lambda i: (i, 0))
    full = lambda: pl.BlockSpec((tm, D_MODEL), lambda i: (i, 0))
    fixed = lambda r, c: pl.BlockSpec((r, c), lambda i: (0, 0))
    wide = jax.ShapeDtypeStruct((SEQ, COL_BLOCK), F32)
    wide_b = jax.ShapeDtypeStruct((SEQ, COL_BLOCK), BF16)
    return pl.pallas_call(
        body, name="mid", grid=(SEQ // tm,),
        out_shape=(jax.ShapeDtypeStruct((SEQ, D_MODEL), F32), wide, wide, wide_b, wide_b, wide_b,
                   jax.ShapeDtypeStruct((D_MODEL, D_MODEL), BF16),
                   jax.ShapeDtypeStruct((1, D_MODEL), F32), jax.ShapeDtypeStruct((1, COL_BLOCK), F32),
                   jax.ShapeDtypeStruct((1, COL_BLOCK), F32), jax.ShapeDtypeStruct((1, 1), F32)),
        scratch_shapes=[pltpu.VMEM((D_MODEL, D_MODEL), F32)],
        in_specs=[half(), half(),
                  pl.BlockSpec((None, tm, COL_BLOCK), lambda i: (3, i, 0)),
                  pl.BlockSpec((None, tm, COL_BLOCK), lambda i: (7, i, 0)),
                  full(), full(), fixed(D_MODEL, D_MODEL), fixed(1, COL_BLOCK), fixed(1, COL_BLOCK),
                  fixed(1, D_MODEL)],
        out_specs=(full(), half(), half(), half(), half(), half(), fixed(D_MODEL, D_MODEL),
                   fixed(1, D_MODEL), fixed(1, COL_BLOCK), fixed(1, COL_BLOCK), fixed(1, 1)),
        compiler_params=_params(("arbitrary",)),
    )(attn_o, rec, proj, proj, x, target, w_out_g, attn_w, hgrn_w, final_w)


def _in_proj_bwd_rows(d_groups, w_g, x, dx2, mix_w, rc, rsa, rsb):
    tm = 512

    def body(*refs):
        dg_refs = refs[:N_DEV]
        wg_ref, x_ref, dx2_ref, w_ref, c_ref, sa_ref, sb_ref, gx_ref, dpb_ref, dmw_ref = refs[N_DEV:]

        @pl.when(pl.program_id(0) == 0)
        def _():
            dmw_ref[...] = jnp.zeros_like(dmw_ref)

        parts = []
        for j in range(N_DEV):
            dp = dg_refs[j][...]
            if j < 2:
                dp = _rot_transposed(dp, c_ref[...], sa_ref[...], sb_ref[...])
            parts.append(dp.astype(BF16))
        dpb = jnp.concatenate(parts, axis=1)
        for j in range(N_DEV):
            dpb_ref[j] = parts[j]
        g = _mm_nt(dpb, wg_ref[...])
        xf = x_ref[...]
        rstd = lax.rsqrt(jnp.mean(xf * xf, axis=-1, keepdims=True) + NORM_EPS)
        xn = xf * rstd
        dmw_ref[...] += jnp.sum(g * xn, axis=0, keepdims=True)
        gw = g * w_ref[...]
        gx_ref[...] = dx2_ref[...] + rstd * (gw - xn * jnp.mean(gw * xn, axis=-1, keepdims=True))

    tile = lambda cols: pl.BlockSpec((tm, cols), lambda i: (i, 0))
    fixed = lambda r, c: pl.BlockSpec((r, c), lambda i: (0, 0))
    return pl.pallas_call(
        body, name="in_proj_bwd_rows", grid=(SEQ // tm,),
        out_shape=(jax.ShapeDtypeStruct((SEQ, D_MODEL), F32), jax.ShapeDtypeStruct((N_DEV, SEQ, COL_BLOCK), BF16),
                   jax.ShapeDtypeStruct((1, D_MODEL), F32)),
        in_specs=[tile(COL_BLOCK) for _ in range(N_DEV)] + [
            pl.BlockSpec((D_MODEL, IN_COLS), lambda i: (0, 0), pipeline_mode=pl.Buffered(1)),
            tile(D_MODEL), tile(D_MODEL), fixed(1, D_MODEL), tile(LANES), tile(LANES), tile(LANES)],
        out_specs=(tile(D_MODEL), pl.BlockSpec((N_DEV, tm, COL_BLOCK), lambda i: (0, i, 0)), fixed(1, D_MODEL)),
        compiler_params=_params(("arbitrary",)),
    )(*d_groups, w_g, x, dx2, mix_w, rc, rsa, rsb)


def _weights_exchange(hn_t, dproj_b, dwout_p, small_p):
    n_chips = N_DEV // 2
    rb = 128
    S1_IN, S1_OUT, SMALL, S2_IN, S2_OUT, VIA_IN, VIA_OUT = 0, 4, 8, 15, 17, 19, 21
    rel_of_pair = (3, 1, 2, 0)
    two_hop = n_chips - 1
    half_in, half_out = COL_BLOCK // 2, D_MODEL // 2

    def body(order_ref, hnt_ref, dp_ref, dwout_ref, small_ref, gin_ref, gout_ref, gs_ref,
             part, s1_send, s1_in, s1_out, fwd_in, fwd_out, s2_in, s2_out, via_in, via_out, land_s,
             send_sems, recv_sems):
        t = pl.program_id(0)
        me = _my_place()
        x, y, c = me
        my_chip = 2 * x + y
        sibling = (x, y, 1 - c)

        def remote(slot, src, dst, to):
            return pltpu.make_async_remote_copy(src_ref=src, dst_ref=dst, send_sem=send_sems.at[slot],
                                                recv_sem=recv_sems.at[slot], device_id=to, device_id_type=MESH)

        def s1_in_copy(pair):
            return remote(S1_IN + pair, s1_send.at[pair], s1_in.at[pair], sibling)

        def s1_out_copy(pair):
            q = my_chip ^ rel_of_pair[pair]
            return remote(S1_OUT + pair, dwout_ref.at[q, 1 - c], s1_out.at[pair], sibling)

        def s2_copies(rel):
            peer = _peer(me, 2 * rel)
            return [remote(S2_IN + rel - 1, fwd_in.at[rel - 1], s2_in.at[rel - 1], peer),
                    remote(S2_OUT + rel - 1, fwd_out.at[rel - 1], s2_out.at[rel - 1], peer)]

        def via_copies(k):
            peer = _peer(me, 2 * (2 - k))
            return [remote(VIA_IN + k, fwd_in.at[two_hop - 1, :, pl.ds(k * half_in, half_in)], via_in.at[k], peer),
                    remote(VIA_OUT + k, fwd_out.at[two_hop - 1, :, pl.ds(k * half_out, half_out)], via_out.at[k],
                           peer)]

        def small_copy(rel):
            return remote(SMALL + rel - 1, small_ref, land_s.at[rel], _peer(me, rel))

        @pl.when(t == 0)
        def _():
            land_s[0] = small_ref[...]
            for pair in range(n_chips):
                s1_out_copy(pair).start()
            for rel in range(1, N_DEV):
                small_copy(rel).start()

        part[...] = _mm(hnt_ref[...], dp_ref[...])

        def rows_loop(n_rows, fn):
            def step(b, carry):
                fn(pl.ds(pl.multiple_of(b * rb, rb), rb))
                return carry
            lax.fori_loop(0, n_rows // rb, step, 0)

        for pair, rel in enumerate(rel_of_pair):
            @pl.when(t == 2 * pair)
            def _(pair=pair):
                s1_send[pair] = part[...].astype(BF16)
                s1_in_copy(pair).start()

            @pl.when(t == 2 * pair + 1)
            def _(pair=pair, rel=rel):
                q = my_chip ^ rel
                s1_in_copy(pair).wait_recv()
                s1_out_copy(pair).wait_recv()
                dst_in = fwd_in.at[rel - 1] if rel else gin_ref
                dst_out = fwd_out.at[rel - 1] if rel else gout_ref
                passes_on = rel in (1, 2)
                if passes_on:
                    for cp in via_copies(rel - 1):
                        cp.wait_recv()

                def with_half(val, via, rows, width):
                    if not passes_on:
                        return val
                    extra = via[rel - 1, rows, :].astype(F32)
                    halves = [val[:, :width], val[:, width:]]
                    halves[rel - 1] = halves[rel - 1] + extra
                    return jnp.concatenate(halves, axis=1)

                def add_in(rows):
                    val = part[rows, :] + s1_in[pair, rows, :].astype(F32)
                    dst_in[rows, :] = with_half(val, via_in, rows, half_in).astype(dst_in.dtype)

                def add_out(rows):
                    val = dwout_ref[q, c, rows, :].astype(F32) + s1_out[pair, rows, :].astype(F32)
                    dst_out[rows, :] = with_half(val, via_out, rows, half_out).astype(dst_out.dtype)

                rows_loop(D_MODEL, add_in)
                rows_loop(WOUT_ROWS, add_out)
                if rel == two_hop:
                    for k in range(2):
                        for cp in via_copies(k):
                            cp.start()
                elif rel:
                    for cp in s2_copies(rel):
                        cp.start()

        @pl.when(t == N_DEV - 1)
        def _():
            for rel in range(1, two_hop):
                for cp in s2_copies(rel):
                    cp.wait_recv()

            def total_in(rows):
                g = gin_ref[rows, :]
                for rel in range(1, two_hop):
                    g = g + s2_in[rel - 1, rows, :].astype(F32)
                gin_ref[rows, :] = g

            def total_out(rows):
                g = gout_ref[rows, :]
                for rel in range(1, two_hop):
                    g = g + s2_out[rel - 1, rows, :].astype(F32)
                gout_ref[rows, :] = g

            rows_loop(D_MODEL, total_in)
            rows_loop(WOUT_ROWS, total_out)

            for rel in range(1, N_DEV):
                small_copy(rel).wait_recv()
            my_flat = _flat(me)
            g = land_s[my_flat ^ 0]
            for dev in range(1, N_DEV):
                g = g + land_s[my_flat ^ dev]
            gs_ref[...] = g

            for pair in range(n_chips):
                s1_in_copy(pair).wait_send()
                s1_out_copy(pair).wait_send()
            for rel in range(1, two_hop):
                for cp in s2_copies(rel) + via_copies(rel - 1):
                    cp.wait_send()
            for rel in range(1, N_DEV):
                small_copy(rel).wait_send()

    place_x, place_y, place_c = _my_place()
    my_chip = 2 * place_x + place_y
    order = jnp.stack([2 * (my_chip ^ rel) + core for rel in rel_of_pair
                       for core in (1 - place_c, place_c)]).astype(jnp.int32)

    whole = lambda: pl.BlockSpec(memory_space=pltpu.VMEM)
    in_blocks = lambda n: pltpu.VMEM((n, D_MODEL, COL_BLOCK), BF16)
    out_blocks = lambda n: pltpu.VMEM((n, WOUT_ROWS, D_MODEL), BF16)
    grid_spec = pltpu.PrefetchScalarGridSpec(
        num_scalar_prefetch=1, grid=(N_DEV,),
        in_specs=[pl.BlockSpec((D_MODEL, SEQ), lambda t, order: (0, 0), pipeline_mode=pl.Buffered(1)),
                  pl.BlockSpec((None, SEQ, COL_BLOCK), lambda t, order: (order[t], 0, 0)), whole(), whole()],
        out_specs=(whole(), whole(), whole()),
        scratch_shapes=[pltpu.VMEM((D_MODEL, COL_BLOCK), F32), in_blocks(n_chips), in_blocks(n_chips),
                        out_blocks(n_chips), in_blocks(n_chips - 1), out_blocks(n_chips - 1),
                        in_blocks(n_chips - 2), out_blocks(n_chips - 2),
                        pltpu.VMEM((2, D_MODEL, half_in), BF16), pltpu.VMEM((2, WOUT_ROWS, half_out), BF16),
                        pltpu.VMEM((N_DEV, SMALL_ROWS, LANES), F32),
                        pltpu.SemaphoreType.DMA((23,)), pltpu.SemaphoreType.DMA((23,))])
    return pl.pallas_call(
        body, name="weights_exchange", grid_spec=grid_spec,
        out_shape=(jax.ShapeDtypeStruct((D_MODEL, COL_BLOCK), F32), jax.ShapeDtypeStruct((WOUT_ROWS, D_MODEL), F32),
                   jax.ShapeDtypeStruct((SMALL_ROWS, LANES), F32)),
        compiler_params=_params(("arbitrary",)),
    )(order, hn_t, dproj_b, dwout_p.reshape(n_chips, 2, WOUT_ROWS, D_MODEL), small_p)


def _adamw(w, g, m, v):
    m = ADAM_B1 * m + (1.0 - ADAM_B1) * g
    v = ADAM_B2 * v + (1.0 - ADAM_B2) * (g * g)
    m_hat = m / (1.0 - ADAM_B1 ** ADAM_STEP)
    v_hat = v / (1.0 - ADAM_B2 ** ADAM_STEP)
    delta = -ADAM_LR * (m_hat / (jnp.sqrt(v_hat) + ADAM_EPS) + ADAM_WD * w)
    return delta, m, v


def _adamw_update(grads, weights, m_old, v_old):
    rb = 256

    def body(*refs):
        g_refs, w_refs, m_refs, v_refs = refs[0:3], refs[3:6], refs[6:9], refs[9:12]
        d_refs, nm_refs, nv_refs = refs[12:15], refs[15:18], refs[18:21]
        for k in range(3):
            n_rows = g_refs[k].shape[0]
            step_rows = min(rb, n_rows)

            def step(b, carry, k=k, step_rows=step_rows):
                rows = pl.ds(pl.multiple_of(b * step_rows, 8), step_rows)
                delta, nm, nv = _adamw(w_refs[k][rows, :], g_refs[k][rows, :], m_refs[k][rows, :], v_refs[k][rows, :])
                d_refs[k][rows, :] = delta
                nm_refs[k][rows, :] = nm
                nv_refs[k][rows, :] = nv
                return carry

            lax.fori_loop(0, n_rows // step_rows, step, 0)

    shapes = tuple(jax.ShapeDtypeStruct(g.shape, F32) for g in grads)
    vm = lambda: pl.BlockSpec(memory_space=pltpu.VMEM)
    outs = pl.pallas_call(
        body, name="adamw_update", out_shape=shapes * 3,
        in_specs=[vm() for _ in range(12)], out_specs=tuple(vm() for _ in range(9)),
        compiler_params=_params(),
    )(*grads, *weights, *m_old, *v_old)
    return outs[0:3], outs[3:6], outs[6:9]


def _pack_small(mix, attn, hgrn, lb, final, loss=None):
    def rows8(a):
        a = a.reshape(-1, LANES)
        return jnp.pad(a, ((0, 8 - a.shape[0]), (0, 0)))
    last = jnp.zeros((8, LANES), F32) if loss is None else jnp.pad(loss.reshape(1, 1), ((0, 7), (0, LANES - 1)))
    return jnp.concatenate([rows8(mix), rows8(attn), rows8(hgrn), rows8(lb), rows8(final), last], axis=0)


def _unpack_small(slab):
    return (slab[ROW_MIX:ROW_MIX + 8].reshape(1, D_MODEL), slab[ROW_ATTN:ROW_ATTN + 4].reshape(1, ATTN_WIDTH),
            slab[ROW_HGRN:ROW_HGRN + 4].reshape(1, HGRN_WIDTH), slab[ROW_LB:ROW_LB + 8].reshape(2, HGRN_WIDTH),
            slab[ROW_FINAL:ROW_FINAL + 8].reshape(D_MODEL))


def _rope(pos_row):
    j = np.arange(ROPE_ROWS)
    inv = np.where(j < ROPE_HALF, ROPE_THETA ** (-(j % ROPE_HALF) * (2.0 / ROPE_DIMS)), 0.0)
    e = np.arange(LANES) % HEAD_DIM
    hit = (j[:, None] == (e % ROPE_HALF)[None, :]) & (j[:, None] < ROPE_HALF)
    sel = np.stack([hit & (e < ROPE_DIMS), hit & (e >= ROPE_HALF) & (e < ROPE_DIMS),
                    -1.0 * (hit & (e < ROPE_HALF))]).astype(np.float32)
    return _rope_tables(pos_row, jnp.asarray(inv.astype(np.float32).reshape(ROPE_ROWS, 1)),
                        jnp.asarray(sel, dtype=BF16))


def _local_step(x, proj, qkv_sorted, w_in_g, w_out_g, tables, mix_w, attn_w, hgrn_w, lb_raw, final_w, target):
    rc, rsa, rsb = tables
    attn_o, lse = _attn_fwd_fused(qkv_sorted)
    rec, states = _hgrn_fwd(proj, lb_raw)

    (dx2, d_o, delta, d_ag, d_rec, d_hg, dwout_p, d_final, d_attn_w, d_hgrn_w, loss) = _mid(
        attn_o, rec, proj, x, target, w_out_g, attn_w, hgrn_w, final_w.reshape(1, D_MODEL))

    dqkv = _attn_bwd_fused(qkv_sorted, d_o, lse, delta)
    d_hq, d_hf, d_hi, d_lb = _hgrn_bwd(proj, lb_raw, d_rec, states)

    grad_x, dproj_b, d_mix = _in_proj_bwd_rows(
        (dqkv[0], dqkv[1], dqkv[2], d_ag, d_hq, d_hf, d_hi, d_hg), w_in_g, x, dx2, mix_w, rc, rsa, rsb)
    small_p = _pack_small(d_mix, d_attn_w, d_hgrn_w, d_lb, d_final, loss)
    return grad_x, dproj_b, dwout_p, small_p


def kernel(x, positions, w_in, w_out, mix_norm_w, attn_out_norm_w, hgrn_out_norm_w, hgrn_lb_raw, final_norm_w, loss_target, m_w_in, m_w_out, m_mix_norm_w, m_attn_out_norm_w, m_hgrn_out_norm_w, m_hgrn_lb_raw, m_final_norm_w, v_w_in, v_w_out, v_mix_norm_w, v_attn_out_norm_w, v_hgrn_out_norm_w, v_hgrn_lb_raw, v_final_norm_w):
    tables = _rope(positions)
    proj, hn_t, w_in_g, w_out_g, qkv_sorted = _gather_project(x[0], mix_norm_w, w_in[0], w_out[0], *tables)
    grad_x, dproj_b, dwout_p, small_p = _local_step(
        x[0], proj, qkv_sorted, w_in_g, w_out_g, tables, mix_norm_w, attn_out_norm_w, hgrn_out_norm_w,
        hgrn_lb_raw, final_norm_w, loss_target[0])
    g_in, g_out, g_s = _weights_exchange(hn_t, dproj_b, dwout_p, small_p)

    w_s = _pack_small(mix_norm_w, attn_out_norm_w, hgrn_out_norm_w, hgrn_lb_raw, final_norm_w)
    m_s = _pack_small(m_mix_norm_w, m_attn_out_norm_w, m_hgrn_out_norm_w, m_hgrn_lb_raw, m_final_norm_w)
    v_s = _pack_small(v_mix_norm_w, v_attn_out_norm_w, v_hgrn_out_norm_w, v_hgrn_lb_raw, v_final_norm_w)
    (d_in, d_out, d_s), (nm_in, nm_out, nm_s), (nv_in, nv_out, nv_s) = _adamw_update(
        (g_in, g_out, g_s), (w_in[0], w_out[0], w_s), (m_w_in[0], m_w_out[0], m_s), (v_w_in[0], v_w_out[0], v_s))

    loss = g_s[ROW_LOSS, 0]
    return (loss, grad_x[None], g_in[None], g_out[None], *_unpack_small(g_s),
            d_in[None], d_out[None], *_unpack_small(d_s),
            nm_in[None], nm_out[None], *_unpack_small(nm_s),
            nv_in[None], nv_out[None], *_unpack_small(nv_s))
```

```python
import functools

import jax
import jax.numpy as jnp
import numpy as np
from jax import lax
from jax.experimental import pallas as pl
from jax.experimental.pallas import tpu as pltpu

F32 = jnp.float32
BF16 = jnp.bfloat16

SEQ = 4096
D_MODEL = 1024
ATTN_WIDTH = 512
HGRN_WIDTH = 512
HEAD_DIM = 64
HGRN_HEADS = 4
HGRN_DIM = 128
HGRN_CHUNK = 64
N_CHUNKS = SEQ // HGRN_CHUNK
IN_COLS = 4096
COL_BLOCK = 512
N_DEV = 8
WOUT_ROWS = D_MODEL // N_DEV
ATTN_BLOCK = 128
DILATIONS = (1, 4, 16)
ROPE_THETA = 500000.0
ROPE_DIMS = 16
ROPE_HALF = 8
NORM_EPS = 1e-6
NEG_BIG = -1e30
LANES = 128

ADAM_LR = 0.001
ADAM_B1 = 0.9
ADAM_B2 = 0.999
ADAM_EPS = 1e-08
ADAM_WD = 0.01
ADAM_STEP = 10

SMALL_ROWS = 48
ROW_MIX, ROW_ATTN, ROW_HGRN, ROW_LB, ROW_FINAL, ROW_LOSS = 0, 8, 16, 24, 32, 40

VMEM_LIMIT = 56 * 1024 * 1024
MESH = pl.DeviceIdType.MESH


def _mm(a, b):
    return lax.dot_general(a, b, (((1,), (0,)), ((), ())), preferred_element_type=F32)


def _mm_nt(a, b):
    return lax.dot_general(a, b, (((1,), (1,)), ((), ())), preferred_element_type=F32)


def _mm_tn(a, b):
    return lax.dot_general(a, b, (((0,), (0,)), ((), ())), preferred_element_type=F32)


def _mm_exact(a, b):
    return lax.dot_general(a, b, (((1,), (0,)), ((), ())), preferred_element_type=F32,
                           precision=lax.Precision.HIGHEST)


def _sigmoid(v):
    return 1.0 / (1.0 + jnp.exp(-v))


def _params(sem=None, **kw):
    return pltpu.CompilerParams(dimension_semantics=sem, vmem_limit_bytes=VMEM_LIMIT, **kw)


def _my_place():
    return lax.axis_index("x"), lax.axis_index("y"), lax.axis_index("c")


def _peer(place, rel):
    x, y, c = place
    return (x ^ ((rel >> 2) & 1), y ^ ((rel >> 1) & 1), c ^ (rel & 1))


def _flat(place):
    x, y, c = place
    return 4 * x + 2 * y + c


ROPE_ROWS = 16


def _rope_tables(pos_row, inv_freq_col, selectors):
    def body(pos_ref, invf_ref, sel_ref, c_ref, sa_ref, sb_ref):
        ang = pos_ref[...].astype(F32) * invf_ref[...]
        cos, sin = jnp.cos(ang), jnp.sin(ang)

        def spread(v, sel):
            hi = v.astype(BF16)
            r1 = v - hi.astype(F32)
            mid = r1.astype(BF16)
            lo = (r1 - mid.astype(F32)).astype(BF16)
            return _mm_tn(hi, sel) + _mm_tn(mid, sel) + _mm_tn(lo, sel)

        e = lax.broadcasted_iota(jnp.int32, (1, LANES), 1) & (HEAD_DIM - 1)
        c_ref[...] = spread(cos, sel_ref[0]) + jnp.where(e < ROPE_DIMS, 0.0, 1.0)
        sa_ref[...] = spread(sin, sel_ref[1])
        sb_ref[...] = spread(sin, sel_ref[2])

    tab = jax.ShapeDtypeStruct((SEQ, LANES), F32)
    vm = lambda: pl.BlockSpec(memory_space=pltpu.VMEM)
    return pl.pallas_call(
        body, name="rope_tables", out_shape=(tab, tab, tab),
        in_specs=[vm(), vm(), vm()], out_specs=(vm(), vm(), vm()), compiler_params=_params(),
    )(pos_row, inv_freq_col, selectors)


def _per_slab(fn, t):
    return jnp.concatenate([fn(t[:, LANES * s:LANES * (s + 1)]) for s in range(t.shape[1] // LANES)], axis=1)


def _rot(t, c, sa, sb):
    return _per_slab(lambda u: u * c + pltpu.roll(u, ROPE_HALF, 1) * sa + pltpu.roll(u, LANES - ROPE_HALF, 1) * sb, t)


def _rot_transposed(g, c, sa, sb):
    return _per_slab(
        lambda u: u * c + pltpu.roll(u * sa, LANES - ROPE_HALF, 1) + pltpu.roll(u * sb, ROPE_HALF, 1), g)


def _gather_project(x, mix_w, w_in, w_out, rc, rsa, rsb):
    tm = 1024
    n_tiles = SEQ // tm
    arrival_of_step = (None, 0, 1, 2, 4, 5, 3, 6)

    def body(order_ref, x_ref, w_ref, win_ref, wout_ref, c_ref, sa_ref, sb_ref,
             proj_ref, hnt_ref, gin_hbm, gout_hbm, qkv_hbm,
             hn_s, w_land, wout_land, stage, sort_stage, send_sems, recv_sems, local_sems, sort_sems):
        g, i = pl.program_id(0), pl.program_id(1)
        me = _my_place()
        x_, y_, c_ = me
        sibling = (x_, y_, 1 - c_)
        chips = [(1 - x_, y_), (x_, 1 - y_), (1 - x_, 1 - y_)]

        def slab(which, place):
            idx = _flat(place)
            if which == 0:
                return w_land.at[idx]
            return wout_land.at[pl.ds(pl.multiple_of(idx * WOUT_ROWS, WOUT_ROWS), WOUT_ROWS), :]

        def remote(which, k, ref, to, src=None):
            return pltpu.make_async_remote_copy(
                src_ref=ref if src is None else src, dst_ref=ref, send_sem=send_sems.at[8 * which + k],
                recv_sem=recv_sems.at[8 * which + k], device_id=to, device_id_type=MESH)

        def copy(which, k, block, to, src=None):
            return remote(which, k, slab(which, block), to, src)

        def half(which, place, part):
            n = (D_MODEL if which == 0 else WOUT_ROWS) // 2
            if which == 0:
                return w_land.at[_flat(place), pl.ds(n * part, n), :]
            return wout_land.at[pl.ds(pl.multiple_of(_flat(place) * WOUT_ROWS + n * part, n), n), :]

        def first_copies(which):
            src = stage if which == 0 else None
            return ([copy(which, 0, me, sibling, src)]
                    + [copy(which, 1 + j, me, (*chips[j], c_), src) for j in range(2)])

        def relay(which, part):
            frm, to = (chips[1], chips[0]) if part == 0 else (chips[0], chips[1])
            return remote(which, 3 if part == 0 else 7, half(which, (*frm, c_), part), (*to, c_))

        def two_hop_half(which, part):
            return remote(which, 3 if part == 0 else 7, half(which, (*chips[2], c_), part), me)

        def pass_on(which, j):
            return copy(which, 4 + j, (*chips[j], c_), sibling)

        def arrival(which, k):
            if k == 0:
                return copy(which, 0, sibling, me)
            if k <= 2:
                return copy(which, k, (*chips[k - 1], c_), me)
            return copy(which, k, (*chips[k - 4], 1 - c_), me)

        def to_hbm(step):
            idx = order_ref[step]
            cols = pl.ds(pl.multiple_of(idx * COL_BLOCK, COL_BLOCK), COL_BLOCK)
            return pltpu.make_async_copy(w_land.at[idx], gin_hbm.at[:, cols], local_sems.at[step])

        @pl.when((g == 0) & (i == 0))
        def _():
            stage[...] = win_ref[...].astype(BF16)
            w_land[_flat(me)] = stage[...]
            wout_land[pl.ds(pl.multiple_of(_flat(me) * WOUT_ROWS, WOUT_ROWS), WOUT_ROWS), :] = (
                wout_ref[...].astype(BF16))
            for cp in first_copies(0) + first_copies(1)[:1]:
                cp.start()
            to_hbm(0).start()

        for step, k in enumerate(arrival_of_step):
            if k is None:
                continue

            @pl.when((g == step) & (i == 0))
            def _(k=k, step=step):
                if k == 3:
                    two_hop_half(0, 0).wait_recv()
                    two_hop_half(0, 1).wait_recv()
                else:
                    arrival(0, k).wait_recv()
                to_hbm(step).start()
                if 1 <= k <= 3:
                    pass_on(0, k - 1).start()
                if k == 1:
                    relay(0, 1).start()
                    for cp in first_copies(1)[1:]:
                        cp.start()
                if k == 2:
                    relay(0, 0).start()
                if k in (4, 5):
                    arrival(1, k - 3).wait_recv()
                    relay(1, 5 - k).start()

        rows = pl.ds(pl.multiple_of(i * tm, tm), tm)

        @pl.when(g == 0)
        def _():
            xf = x_ref[...]
            ms = jnp.mean(xf * xf, axis=-1, keepdims=True)
            hn = xf * lax.rsqrt(ms + NORM_EPS) * w_ref[...]
            hnt_ref[...] = hn.T.astype(BF16)
            hn_s[rows, :] = hn.astype(BF16)

        group = order_ref[g]

        def sorted_copy(tile_value):
            per = tm // SORT_RESIDUES
            cols = pl.ds(pl.multiple_of(group * COL_BLOCK, COL_BLOCK), COL_BLOCK)
            buf = i % 2

            def out_copies(tile, b):
                return [pltpu.make_async_copy(
                    sort_stage.at[b, :, r, :], qkv_hbm.at[r, pl.ds(pl.multiple_of(tile * per, per), per), cols],
                    sort_sems.at[b, r]) for r in range(SORT_RESIDUES)]

            @pl.when(i >= 2)
            def _():
                for copy in out_copies(i - 2, buf):
                    copy.wait()

            sort_stage[buf] = tile_value.reshape(per, SORT_RESIDUES, COL_BLOCK)
            for copy in out_copies(i, buf):
                copy.start()

            @pl.when(i == n_tiles - 1)
            def _():
                for copy in out_copies(i - 1, 1 - buf) + out_copies(i, buf):
                    copy.wait()

        @pl.when(group < 2)
        def _():
            rotated = _rot(_mm(hn_s[rows, :], w_land[group]), c_ref[...], sa_ref[...], sb_ref[...])
            proj_ref[...] = rotated
            sorted_copy(rotated)

        @pl.when(group == 2)
        def _():
            value = _mm(hn_s[rows, :], w_land[group])
            proj_ref[...] = value
            sorted_copy(value)

        @pl.when(group > 2)
        def _():
            proj_ref[...] = _mm(hn_s[rows, :], w_land[group])

        @pl.when((g == N_DEV - 1) & (i == n_tiles - 1))
        def _():
            pass_on(1, 0).start()
            pass_on(1, 1).start()
            two_hop_half(1, 0).wait_recv()
            two_hop_half(1, 1).wait_recv()
            pass_on(1, 2).start()
            for k in (0, 4, 5, 6):
                arrival(1, k).wait_recv()
            for which in (0, 1):
                for cp in (first_copies(which) + [relay(which, part) for part in range(2)]
                           + [pass_on(which, j) for j in range(3)]):
                    cp.wait_send()
            wout_copy = pltpu.make_async_copy(wout_land, gout_hbm, local_sems.at[N_DEV])
            wout_copy.start()
            for step in range(N_DEV):
                to_hbm(step).wait()
            wout_copy.wait()

    me = _my_place()
    x_, y_, c_ = me
    chips = [(1 - x_, y_), (x_, 1 - y_), (1 - x_, 1 - y_)]
    order = jnp.stack([_flat(p) for p in (
        me, (x_, y_, 1 - c_), (*chips[0], c_), (*chips[1], c_), (*chips[0], 1 - c_), (*chips[1], 1 - c_),
        (*chips[2], c_), (*chips[2], 1 - c_))]).astype(jnp.int32)

    first_sweep = lambda g, i, order: (jnp.where(g == 0, i, n_tiles - 1), 0)
    tab = pl.BlockSpec((tm, LANES), lambda g, i, order: (jnp.where(order[g] < 2, i, 0), 0))
    whole = lambda: pl.BlockSpec(memory_space=pltpu.VMEM)
    grid_spec = pltpu.PrefetchScalarGridSpec(
        num_scalar_prefetch=1, grid=(N_DEV, n_tiles),
        in_specs=[pl.BlockSpec((tm, D_MODEL), first_sweep),
                  pl.BlockSpec((1, D_MODEL), lambda g, i, order: (0, 0)),
                  whole(), whole(), tab, tab, tab],
        out_specs=(pl.BlockSpec((None, tm, COL_BLOCK), lambda g, i, order: (order[g], i, 0)),
                   pl.BlockSpec((D_MODEL, tm), lambda g, i, order: (0, jnp.where(g == 0, i, n_tiles - 1))),
                   pl.BlockSpec(memory_space=pl.ANY), pl.BlockSpec(memory_space=pl.ANY),
                   pl.BlockSpec(memory_space=pl.ANY)),
        scratch_shapes=[pltpu.VMEM((SEQ, D_MODEL), BF16),
                        pltpu.VMEM((N_DEV, D_MODEL, COL_BLOCK), BF16),
                        pltpu.VMEM((D_MODEL, D_MODEL), BF16),
                        pltpu.VMEM((D_MODEL, COL_BLOCK), BF16),
                        pltpu.VMEM((2, tm // SORT_RESIDUES, SORT_RESIDUES, COL_BLOCK), F32),
                        pltpu.SemaphoreType.DMA((16,)), pltpu.SemaphoreType.DMA((16,)),
                        pltpu.SemaphoreType.DMA((N_DEV + 1,)), pltpu.SemaphoreType.DMA((2, SORT_RESIDUES))])
    proj, hn_t, w_in_g, w_out_g, qkv_sorted = pl.pallas_call(
        body, name="gather_project", grid_spec=grid_spec,
        out_shape=(jax.ShapeDtypeStruct((N_DEV, SEQ, COL_BLOCK), F32), jax.ShapeDtypeStruct((D_MODEL, SEQ), BF16),
                   jax.ShapeDtypeStruct((D_MODEL, IN_COLS), BF16), jax.ShapeDtypeStruct((D_MODEL, D_MODEL), BF16),
                   jax.ShapeDtypeStruct((SORT_RESIDUES, SORT_ROWS, 3 * COL_BLOCK), F32)),
        compiler_params=_params(("arbitrary", "arbitrary")),
    )(order, x, mix_w, w_in, w_out, rc, rsa, rsb)
    return proj, hn_t, w_in_g, w_out_g, qkv_sorted.reshape(SEQ, 3 * COL_BLOCK)


SCORE_SCALE = HEAD_DIM ** -0.5
ATTN_GROUP_FWD = 32
ATTN_GROUP_BWD = 16
BLOCKS_PER_PATTERN = SEQ // ATTN_BLOCK
SORT_RESIDUES = 16
SORT_ROWS = SEQ // SORT_RESIDUES


def _write_band_bias(bias_ref):
    row = lax.broadcasted_iota(jnp.int32, (2 * ATTN_BLOCK, 2 * ATTN_BLOCK), 0) & (ATTN_BLOCK - 1)
    col = lax.broadcasted_iota(jnp.int32, (2 * ATTN_BLOCK, 2 * ATTN_BLOCK), 1)
    for pi, d in enumerate(DILATIONS):
        per = SORT_RESIDUES // d
        ahead = per * (row % (8 * d) - col % (16 * d)) + (row // (8 * d) - col // (16 * d))
        dist = ATTN_BLOCK + ahead
        bias_ref[2 * pi] = jnp.where((dist >= 0) & (dist <= ATTN_BLOCK), 0.0, NEG_BIG)
        bias_ref[2 * pi + 1] = jnp.where(ahead >= 0, 0.0, NEG_BIG)


def _head0_lanes():
    return lax.broadcasted_iota(jnp.int32, (ATTN_BLOCK, LANES), 1) < HEAD_DIM


def _stack_heads(t, h0):
    return jnp.concatenate([jnp.where(h0, t, 0.0), jnp.where(h0, 0.0, t)], axis=0).astype(BF16)


def _block_runs(i, d):
    nblk = BLOCKS_PER_PATTERN // d
    r, n = i // nblk, i % nblk
    kn = jnp.maximum(n - 1, 0)
    rows, keys = [], []
    for c in range(SORT_RESIDUES // d):
        base = SORT_ROWS * (c * d + r)
        rows.append(pl.ds(pl.multiple_of(base + 8 * d * n, 8), 8 * d))
        keys.append(pl.ds(pl.multiple_of(base + 8 * d * kn, 8), 16 * d))
    return rows, keys, (n == 0).astype(jnp.int32)


def _take(ref, runs):
    return jnp.concatenate([ref[run, :] for run in runs], axis=0)


def _put(ref, runs, value, add=False):
    at = 0
    for run in runs:
        piece = value[at:at + run.size]
        if add:
            ref[run, :] += piece
        else:
            ref[run, :] = piece
        at += run.size


def _sort_copies(src_hbm, lane_block, dst_ref, sem_ref):
    lanes = pl.ds(pl.multiple_of(LANES * lane_block, LANES), LANES)
    return [pltpu.make_async_copy(src_hbm.at[:, r, lanes], dst_ref.at[pl.ds(SORT_ROWS * r, SORT_ROWS), :],
                                  sem_ref.at[r]) for r in range(SORT_RESIDUES)]


def _unsort_copies(src_ref, dst_hbm, lane_block, sem_ref):
    lanes = pl.ds(pl.multiple_of(LANES * lane_block, LANES), LANES)
    return [pltpu.make_async_copy(src_ref.at[pl.ds(SORT_ROWS * r, SORT_ROWS), :], dst_hbm.at[:, r, lanes],
                                  sem_ref.at[r]) for r in range(SORT_RESIDUES)]


def _for_each_group(d, n_group, load, compute, store):
    def group(g, carry):
        items = [load(*_block_runs(g * n_group + u, d)) for u in range(n_group)]
        results = [compute(item) for item in items]
        for item, res in zip(items, results):
            store(item, res)
        return carry

    lax.fori_loop(0, BLOCKS_PER_PATTERN // n_group, group, 0)


def _attn_fwd_fused(qkv_sorted):
    n_pat = len(DILATIONS)
    tile2 = (2 * ATTN_BLOCK, LANES)

    def body(q_ref, k_ref, v_ref, o_hbm, lse_ref, o_slots, m_acc, l_acc, bias_ref, out_sem):
        step, n_steps = pl.program_id(0), pl.num_programs(0)
        pl.when(step == 0)(lambda: _write_band_bias(bias_ref))
        slot = step % 2
        o_acc = o_slots.at[slot]
        h0 = _head0_lanes()
        for pi, d in enumerate(DILATIONS):
            first, last = pi == 0, pi == n_pat - 1

            def load(rows, keys, which, first=first, pi=pi):
                item = dict(rows=rows, keys=keys, which=2 * pi + which)
                if not first:
                    item.update(o=_take(o_acc, rows), m=[_take(m_acc.at[h], rows) for h in range(2)],
                                l=[_take(l_acc.at[h], rows) for h in range(2)])
                return item

            def compute(item, first=first):
                kb = _take(k_ref, item["keys"]).astype(BF16)
                vb = _take(v_ref, item["keys"]).astype(BF16)
                s = _mm_nt(_stack_heads(_take(q_ref, item["rows"]) * SCORE_SCALE, h0), kb) + bias_ref[item["which"]]
                mb = jnp.max(s, axis=-1, keepdims=True)
                if first:
                    p = jnp.exp(s - mb)
                    mn = jnp.broadcast_to(mb, tile2)
                else:
                    m_old = jnp.concatenate(item["m"], axis=0)
                    mn = jnp.maximum(m_old, mb)
                    alpha = jnp.exp(m_old - mn)
                    p = jnp.exp(s - jnp.concatenate([mn, mn], axis=1))
                ls = jnp.sum(p, axis=-1, keepdims=True)
                pv = _mm(p.astype(BF16), vb)
                if first:
                    return pv, mn, jnp.broadcast_to(ls, tile2)
                o_old = jnp.concatenate([item["o"], item["o"]], axis=0)
                return alpha * o_old + pv, mn, alpha * jnp.concatenate(item["l"], axis=0) + ls

            def store(item, res, last=last):
                rows = item["rows"]
                (o0, o1), (m0, m1), (l0, l1) = ((a[:ATTN_BLOCK], a[ATTN_BLOCK:]) for a in res)
                if last:
                    _put(o_acc, rows, jnp.where(h0, o0 / l0, o1 / l1))
                    _put(lse_ref, rows, jnp.where(h0, m0 + jnp.log(l0), m1 + jnp.log(l1)))
                else:
                    _put(o_acc, rows, jnp.where(h0, o0, o1))
                    for h, (m, l) in enumerate(((m0, l0), (m1, l1))):
                        _put(m_acc.at[h], rows, m)
                        _put(l_acc.at[h], rows, l)

            _for_each_group(d, ATTN_GROUP_FWD, load, compute, store)

        def copies_out(of_step):
            return _unsort_copies(o_slots.at[of_step % 2], o_hbm, of_step, out_sem.at[of_step % 2])

        @pl.when(step > 0)
        def _():
            for copy in copies_out(step - 1):
                copy.wait()

        for copy in copies_out(step):
            copy.start()

        @pl.when(step == n_steps - 1)
        def _():
            for copy in copies_out(step):
                copy.wait()

    slab = lambda g: pl.BlockSpec((SEQ, LANES), functools.partial(lambda hp, g: (0, 4 * g + hp), g=g))
    wide = jax.ShapeDtypeStruct((SEQ, ATTN_WIDTH), F32)
    o_rows, lse = pl.pallas_call(
        body, name="attn_fwd", grid=(4,),
        out_shape=(jax.ShapeDtypeStruct((SORT_ROWS, SORT_RESIDUES, ATTN_WIDTH), F32), wide),
        in_specs=[slab(0), slab(1), slab(2)], out_specs=(pl.BlockSpec(memory_space=pl.ANY), slab(0)),
        scratch_shapes=[pltpu.VMEM((2, SEQ, LANES), F32), pltpu.VMEM((2, SEQ, LANES), F32),
                        pltpu.VMEM((2, SEQ, LANES), F32),
                        pltpu.VMEM((2 * len(DILATIONS), 2 * ATTN_BLOCK, 2 * ATTN_BLOCK), F32),
                        pltpu.SemaphoreType.DMA((2, SORT_RESIDUES))],
        compiler_params=_params(("arbitrary",)),
    )(qkv_sorted, qkv_sorted, qkv_sorted)
    return o_rows.reshape(SEQ, ATTN_WIDTH), lse


def _attn_bwd_fused(qkv_sorted, d_out, lse_sorted, delta):
    def body(q_ref, k_ref, v_ref, do_hbm, lse_ref, del_hbm, dq_hbm, dk_hbm, dv_hbm,
             in_slots, out_slots, bias_ref, in_sem, out_sem):
        step, n_steps = pl.program_id(0), pl.num_programs(0)
        slot = step % 2

        def copies_in(of_step):
            s = of_step % 2
            return [copy for j, hbm in enumerate((do_hbm, del_hbm))
                    for copy in _sort_copies(hbm, of_step, in_slots.at[s, j], in_sem.at[s, j])]

        def copies_out(of_step):
            s = of_step % 2
            return [copy for j, hbm in enumerate((dq_hbm, dk_hbm, dv_hbm))
                    for copy in _unsort_copies(out_slots.at[s, j], hbm, of_step, out_sem.at[s, j])]

        @pl.when(step == 0)
        def _():
            for copy in copies_in(step):
                copy.start()
            _write_band_bias(bias_ref)

        @pl.when(step + 1 < n_steps)
        def _():
            for copy in copies_in(step + 1):
                copy.start()

        do_s, del_s = in_slots.at[slot, 0], in_slots.at[slot, 1]
        dq_s, dk_s, dv_s = (out_slots.at[slot, j] for j in range(3))
        dk_s[...] = jnp.zeros_like(dk_s)
        dv_s[...] = jnp.zeros_like(dv_s)
        for copy in copies_in(step):
            copy.wait()
        h0 = _head0_lanes()
        for pi, d in enumerate(DILATIONS):
            first = pi == 0

            def load(rows, keys, which, pi=pi):
                return dict(rows=rows, keys=keys, q=_take(q_ref, rows), g=_take(do_s, rows),
                            lse=_take(lse_ref, rows), delta=_take(del_s, rows),
                            k=_take(k_ref, keys).astype(BF16), v=_take(v_ref, keys).astype(BF16),
                            bias=bias_ref[2 * pi + which])

            def per_head(t):
                swapped = pltpu.roll(t, HEAD_DIM, 1)
                both = jnp.concatenate([jnp.where(h0, t, swapped), jnp.where(h0, swapped, t)], axis=0)
                return jnp.concatenate([both, both], axis=1)

            def compute(item):
                q2, g2 = _stack_heads(item["q"] * SCORE_SCALE, h0), _stack_heads(item["g"], h0)
                s = _mm_nt(q2, item["k"]) + item["bias"]
                p = jnp.exp(s - per_head(item["lse"]))
                dp = _mm_nt(g2, item["v"])
                ds = (p * (dp - per_head(item["delta"]))).astype(BF16)
                dq2 = _mm(ds, item["k"])
                dq = jnp.where(h0, dq2[:ATTN_BLOCK], dq2[ATTN_BLOCK:]) * SCORE_SCALE
                return dq, _mm_tn(ds, q2), _mm_tn(p.astype(BF16), g2)

            def store(item, res, first=first):
                _put(dq_s, item["rows"], res[0], add=not first)
                _put(dk_s, item["keys"], res[1], add=True)
                _put(dv_s, item["keys"], res[2], add=True)

            _for_each_group(d, ATTN_GROUP_BWD, load, compute, store)

        @pl.when(step > 0)
        def _():
            for copy in copies_out(step - 1):
                copy.wait()

        for copy in copies_out(step):
            copy.start()

        @pl.when(step == n_steps - 1)
        def _():
            for copy in copies_out(step):
                copy.wait()

    slab = lambda g: pl.BlockSpec((SEQ, LANES), functools.partial(lambda hp, g: (0, 4 * g + hp), g=g))
    anywhere = pl.BlockSpec(memory_space=pl.ANY)
    by_residue = (SORT_ROWS, SORT_RESIDUES, ATTN_WIDTH)
    grads = pl.pallas_call(
        body, name="attn_bwd", grid=(4,), out_shape=(jax.ShapeDtypeStruct(by_residue, F32),) * 3,
        scratch_shapes=[pltpu.VMEM((2, 2, SEQ, LANES), F32), pltpu.VMEM((2, 3, SEQ, LANES), F32),
                        pltpu.VMEM((2 * len(DILATIONS), 2 * ATTN_BLOCK, 2 * ATTN_BLOCK), F32),
                        pltpu.SemaphoreType.DMA((2, 2, SORT_RESIDUES)), pltpu.SemaphoreType.DMA((2, 3, SORT_RESIDUES))],
        in_specs=[slab(0), slab(1), slab(2), anywhere, slab(0), anywhere], out_specs=(anywhere,) * 3,
        compiler_params=_params(("arbitrary",)),
    )(qkv_sorted, qkv_sorted, qkv_sorted, d_out.reshape(by_residue), lse_sorted, delta.reshape(by_residue))
    return tuple(g.reshape(SEQ, ATTN_WIDTH) for g in grads)


def _hgrn_lower_bound(lb_ref):
    r0, r1 = lb_ref[0:1, :], lb_ref[1:2, :]
    mx = jnp.maximum(r0, r1)
    e0, e1 = jnp.exp(r0 - mx), jnp.exp(r1 - mx)
    return e0 / (e0 + e1)


def _hgrn_gates(hq, hf, lb):
    sq = _sigmoid(hq)
    sg = _sigmoid(hf)
    f = lb + (1.0 - lb) * sg
    return hq * sq, sq, sg, f, 1.0 - f, jnp.log(f)


HGRN_PAIR = 4
HGRN_SEQ_BLOCK = 1024
HGRN_GROUP = 4
HGRN_ROWS = HGRN_GROUP * HGRN_CHUNK


def _hgrn_specs(reverse):
    n_blocks = SEQ // HGRN_SEQ_BLOCK
    width = HGRN_PAIR * HGRN_DIM
    blk = (lambda s: n_blocks - 1 - s) if reverse else (lambda s: s)
    cols = lambda g: pl.BlockSpec((None, HGRN_SEQ_BLOCK, width), functools.partial(lambda p, s, g: (g, blk(s), p), g=g))
    pair = pl.BlockSpec((HGRN_SEQ_BLOCK, width), lambda p, s: (blk(s), p))
    lb = pl.BlockSpec((2, width), lambda p, s: (0, p))
    states = pl.BlockSpec((HGRN_PAIR, HGRN_SEQ_BLOCK // HGRN_CHUNK, HGRN_DIM, HGRN_DIM),
                          lambda p, s: (p, blk(s), 0, 0))
    return cols, pair, lb, states


def _chunk_masks():
    ri = lax.broadcasted_iota(jnp.int32, (HGRN_ROWS, HGRN_ROWS), 0)
    ci = lax.broadcasted_iota(jnp.int32, (HGRN_ROWS, HGRN_ROWS), 1)
    same = (ri // HGRN_CHUNK) == (ci // HGRN_CHUNK)
    return same, same & (ri >= ci), same & (ri <= ci)


def _mm_select(sel, v):
    hi = v.astype(BF16)
    r1 = v - hi.astype(F32)
    mid = r1.astype(BF16)
    lo = (r1 - mid.astype(F32)).astype(BF16)
    return _mm(sel, hi) + _mm(sel, mid) + _mm(sel, lo)


def _head_cols(a, h):
    return a[:, HGRN_DIM * h:HGRN_DIM * (h + 1)]


def _hgrn_fwd(proj, lb_raw):
    t, rws = HGRN_CHUNK, HGRN_ROWS

    def body(hq_ref, hf_ref, hi_ref, lb_ref, rec_ref, st_ref, state):
        @pl.when(pl.program_id(1) == 0)
        def _():
            state[...] = jnp.zeros_like(state)

        lb = _hgrn_lower_bound(lb_ref)
        same, causal, _ = _chunk_masks()
        sel = jnp.concatenate([causal, same], axis=0).astype(BF16)

        def group(g, sts):
            rows = pl.ds(pl.multiple_of(g * rws, rws), rws)
            q, _, _, _, k, lf = _hgrn_gates(hq_ref[rows, :], hf_ref[rows, :], lb)
            sums = _mm_select(sel, lf)
            cum, last = sums[:rws], sums[rws:]
            qd = (q * jnp.exp(cum)).astype(BF16)
            ki = (k * jnp.exp(-cum)).astype(BF16)
            ke = (k * jnp.exp(last - cum)).astype(BF16)
            vb = hi_ref[rows, :].astype(BF16)
            dec = jnp.exp(last)
            new_sts, recs = [], []
            for h in range(HGRN_PAIR):
                qd_h, ke_h, vb_h = _head_cols(qd, h), _head_cols(ke, h), _head_cols(vb, h)
                att = jnp.where(causal, _mm_nt(qd_h, _head_cols(ki, h)), 0.0).astype(BF16)
                intra = _mm(att, vb_h)
                st = sts[h]
                outs = []
                for c in range(HGRN_GROUP):
                    sl = slice(c * t, (c + 1) * t)
                    st_ref[h, g * HGRN_GROUP + c] = st
                    outs.append(intra[sl] + _mm_nt(qd_h[sl], st.astype(BF16)))
                    st = st * _head_cols(dec[c * t:c * t + 1, :], h) + _mm_tn(vb_h[sl], ke_h[sl])
                new_sts.append(st)
                recs.append(jnp.concatenate(outs, axis=0))
            rec_ref[rows, :] = jnp.concatenate(recs, axis=1)
            return tuple(new_sts)

        sts = lax.fori_loop(0, HGRN_SEQ_BLOCK // rws, group, tuple(state[h] for h in range(HGRN_PAIR)), unroll=True)
        for h in range(HGRN_PAIR):
            state[h] = sts[h]

    cols, pair, lb, states = _hgrn_specs(reverse=False)
    return pl.pallas_call(
        body, name="hgrn_fwd", grid=(HGRN_HEADS // HGRN_PAIR, SEQ // HGRN_SEQ_BLOCK),
        out_shape=(jax.ShapeDtypeStruct((SEQ, HGRN_WIDTH), F32),
                   jax.ShapeDtypeStruct((HGRN_HEADS, N_CHUNKS, HGRN_DIM, HGRN_DIM), F32)),
        in_specs=[cols(4), cols(5), cols(6), lb], out_specs=(pair, states),
        scratch_shapes=[pltpu.VMEM((HGRN_PAIR, HGRN_DIM, HGRN_DIM), F32)],
        compiler_params=_params(("parallel", "arbitrary")),
    )(proj, proj, proj, lb_raw)


def _hgrn_bwd(proj, lb_raw, d_rec, states):
    t, rws = HGRN_CHUNK, HGRN_ROWS

    def body(hq_ref, hf_ref, hi_ref, lb_ref, do_ref, st_ref, dhq_ref, dhf_ref, dhi_ref, dlb_ref,
             dstate, dlb_acc):
        lb = _hgrn_lower_bound(lb_ref)
        same, causal, anti = _chunk_masks()
        sel = jnp.concatenate([causal, same], axis=0).astype(BF16)
        sel_t = jnp.concatenate([anti, same], axis=1).astype(BF16)
        @pl.when(pl.program_id(1) == 0)
        def _():
            dstate[...] = jnp.zeros_like(dstate)
            dlb_acc[...] = jnp.zeros_like(dlb_acc)

        n_groups = HGRN_SEQ_BLOCK // rws
        chunks = [slice(c * t, (c + 1) * t) for c in range(HGRN_GROUP)]

        def group(i, dsts_in):
            g = n_groups - 1 - i
            rows = pl.ds(pl.multiple_of(g * rws, rws), rws)
            hq = hq_ref[rows, :]
            q, sq, sg, f, k, lf = _hgrn_gates(hq, hf_ref[rows, :], lb)
            sums = _mm_select(sel, lf)
            cum, last = sums[:rws], sums[rws:]
            e_cum, e_inv, e_end, dec = jnp.exp(cum), jnp.exp(-cum), jnp.exp(last - cum), jnp.exp(last)
            qd, ki, ke = q * e_cum, k * e_inv, k * e_end
            qdb, kib, keb = qd.astype(BF16), ki.astype(BF16), ke.astype(BF16)
            vb = hi_ref[rows, :].astype(BF16)
            gb = do_ref[rows, :]

            dsts_out, per_head = [], []
            for h in range(HGRN_PAIR):
                qdb_h, kib_h, keb_h = _head_cols(qdb, h), _head_cols(kib, h), _head_cols(keb, h)
                vb_h, gb_h = _head_cols(vb, h), _head_cols(gb, h)
                att = jnp.where(causal, _mm_nt(qdb_h, kib_h), 0.0).astype(BF16)
                datt = jnp.where(causal, _mm_nt(gb_h, vb_h), 0.0).astype(BF16)
                dv = _mm_tn(att, gb_h)
                dqd = _mm(datt, kib_h)
                dki = _mm_tn(datt, qdb_h)

                decs = [_head_cols(dec[c * t:c * t + 1, :], h) for c in range(HGRN_GROUP)]
                dsts = [None] * HGRN_GROUP
                dst = dsts_in[h]
                for c in reversed(range(HGRN_GROUP)):
                    dsts[c] = dst
                    dst = dst * decs[c] + _mm_tn(gb_h[chunks[c]], qdb_h[chunks[c]])
                dsts_out.append(dst)

                dv_x, dqd_x, dke, dlast_x = [], [], [], []
                for c, sl in enumerate(chunks):
                    st_prev = st_ref[h, g * HGRN_GROUP + c]
                    dstb = dsts[c].astype(BF16)
                    dv_x.append(_mm_nt(keb_h[sl], dstb))
                    dqd_x.append(_mm(gb_h[sl], st_prev.astype(BF16)))
                    dke.append(_mm(vb_h[sl], dstb))
                    ddec = jnp.sum(dsts[c] * st_prev, axis=0, keepdims=True)
                    dlast_x.append(jnp.broadcast_to(ddec * decs[c], (t, HGRN_DIM)))
                per_head.append((dv + jnp.concatenate(dv_x, axis=0), dqd + jnp.concatenate(dqd_x, axis=0),
                                 dki, jnp.concatenate(dke, axis=0), jnp.concatenate(dlast_x, axis=0)))
            dv, dqd, dki, dke, dlast = (jnp.concatenate(list(parts), axis=1) for parts in zip(*per_head))

            dq = dqd * e_cum
            dk = dki * e_inv + dke * e_end
            dke_ke = dke * ke
            dcum = dqd * qd - dki * ki - dke_ke
            dlf = _mm_select(sel_t, jnp.concatenate([dcum, dke_ke], axis=0)) + dlast
            df = dlf / f - dk
            dhq_ref[rows, :] = (dq * (sq * (1.0 + hq * (1.0 - sq)))).astype(BF16)
            dhf_ref[rows, :] = (df * (1.0 - lb) * (sg * (1.0 - sg))).astype(BF16)
            dhi_ref[rows, :] = dv.astype(BF16)
            dlb_acc[...] += jnp.sum(df * (1.0 - sg), axis=0, keepdims=True)
            return tuple(dsts_out)

        dsts = lax.fori_loop(0, n_groups, group, tuple(dstate[h] for h in range(HGRN_PAIR)), unroll=True)
        for h in range(HGRN_PAIR):
            dstate[h] = dsts[h]
        g0 = dlb_acc[...] * lb * (1.0 - lb)
        dlb_ref[...] = jnp.concatenate([g0, -g0], axis=0)

    cols, pair, lb_spec, st_spec = _hgrn_specs(reverse=True)
    wide = jax.ShapeDtypeStruct((SEQ, HGRN_WIDTH), BF16)
    return pl.pallas_call(
        body, name="hgrn_bwd", grid=(HGRN_HEADS // HGRN_PAIR, SEQ // HGRN_SEQ_BLOCK),
        out_shape=(wide, wide, wide, jax.ShapeDtypeStruct((2, HGRN_WIDTH), F32)),
        in_specs=[cols(4), cols(5), cols(6), lb_spec, pair, st_spec],
        out_specs=(pair, pair, pair, lb_spec),
        scratch_shapes=[pltpu.VMEM((HGRN_PAIR, HGRN_DIM, HGRN_DIM), F32),
                        pltpu.VMEM((1, HGRN_PAIR * HGRN_DIM), F32)],
        compiler_params=_params(("parallel", "arbitrary")),
    )(proj, proj, proj, lb_raw, d_rec, states)


def _group_sum(v, group):
    parts = []
    for s in range(v.shape[1] // LANES):
        slab = v[:, LANES * s:LANES * (s + 1)]
        if group == LANES:
            parts.append(jnp.broadcast_to(jnp.sum(slab, axis=-1, keepdims=True), slab.shape))
        else:
            h0 = lax.broadcasted_iota(jnp.int32, slab.shape, 1) < HEAD_DIM
            s0 = jnp.sum(jnp.where(h0, slab, 0.0), axis=-1, keepdims=True)
            s1 = jnp.sum(jnp.where(h0, 0.0, slab), axis=-1, keepdims=True)
            parts.append(jnp.where(h0, s0, s1))
    return jnp.concatenate(parts, axis=1)


def _mid(attn_o, rec, proj, x, target, w_out_g, attn_w, hgrn_w, final_w):
    tm = 512

    def branch_fwd(o, gate, w, group):
        r = lax.rsqrt(_group_sum(o * o, group) * (1.0 / group) + NORM_EPS)
        nrm = o * r
        sg = _sigmoid(gate)
        return r, nrm, sg, nrm * w * (gate * sg)

    def branch_bwd(dy, r, nrm, sg, gate, w, group):
        silu = gate * sg
        d_gate = dy * nrm * w * (sg * (1.0 + gate * (1.0 - sg)))
        d_w = jnp.sum(dy * nrm * silu, axis=0, keepdims=True)
        dn = dy * w * silu
        d_o = r * (dn - nrm * (_group_sum(dn * nrm, group) * (1.0 / group)))
        return d_o, d_gate, d_w

    def body(o_ref, rec_ref, ag_ref, hg_ref, x_ref, tgt_ref, wout_ref, aw_ref, hw_ref, fw_ref,
             dx2_ref, do_ref, delta_ref, dag_ref, drec_ref, dhg_ref, dwout_ref, dfw_ref, daw_ref, dhw_ref,
             loss_ref, dwout_acc):
        i = pl.program_id(0)

        @pl.when(i == 0)
        def _():
            dwout_acc[...] = jnp.zeros_like(dwout_acc)
            dfw_ref[...] = jnp.zeros_like(dfw_ref)
            daw_ref[...] = jnp.zeros_like(daw_ref)
            dhw_ref[...] = jnp.zeros_like(dhw_ref)
            loss_ref[...] = jnp.zeros_like(loss_ref)

        o, rc, ag, hg = o_ref[...], rec_ref[...], ag_ref[...], hg_ref[...]
        aw, hw, fw = aw_ref[...], hw_ref[...], fw_ref[...]
        ra, na, sga, ya = branch_fwd(o, ag, aw, HEAD_DIM)
        rh, nh, sgh, yh = branch_fwd(rc, hg, hw, HGRN_DIM)
        mixed = jnp.concatenate([ya, yh], axis=1).astype(BF16)
        wout = wout_ref[...]
        x2 = x_ref[...] + _mm(mixed, wout)
        rstd = lax.rsqrt(jnp.mean(x2 * x2, axis=-1, keepdims=True) + NORM_EPS)
        xn = x2 * rstd
        err = xn * fw - tgt_ref[...]
        row_loss = jnp.mean(err * err, axis=-1, keepdims=True)
        loss_ref[...] += 0.5 * jnp.sum(row_loss, axis=0, keepdims=True)
        dy = err * (1.0 / D_MODEL)
        dfw_ref[...] += jnp.sum(dy * xn, axis=0, keepdims=True)
        dxn = dy * fw
        dx2 = rstd * (dxn - xn * jnp.mean(dxn * xn, axis=-1, keepdims=True))
        dx2_ref[...] = dx2
        dx2b = dx2.astype(BF16)
        dwout_acc[...] += _mm_tn(mixed, dx2b)

        @pl.when(i == pl.num_programs(0) - 1)
        def _():
            dwout_ref[...] = dwout_acc[...].astype(BF16)

        dmixed = _mm_nt(dx2b, wout)

        d_o, d_ag, d_aw = branch_bwd(dmixed[:, :ATTN_WIDTH], ra, na, sga, ag, aw, HEAD_DIM)
        d_rec, d_hg, d_hw = branch_bwd(dmixed[:, ATTN_WIDTH:], rh, nh, sgh, hg, hw, HGRN_DIM)
        do_ref[...] = d_o
        delta_ref[...] = _group_sum(d_o * o, HEAD_DIM)
        dag_ref[...] = d_ag.astype(BF16)
        drec_ref[...] = d_rec.astype(BF16)
        dhg_ref[...] = d_hg.astype(BF16)
        daw_ref[...] += d_aw
        dhw_ref[...] += d_hw

    half = lambda: pl.BlockSpec((tm, COL_BLOCK), lambda i: (i, 0))
    full = lambda: pl.BlockSpec((tm, D_MODEL), lambda i: (i, 0))
    fixed = lambda r, c: pl.BlockSpec((r, c), lambda i: (0, 0))
    wide = jax.ShapeDtypeStruct((SEQ, COL_BLOCK), F32)
    wide_b = jax.ShapeDtypeStruct((SEQ, COL_BLOCK), BF16)
    return pl.pallas_call(
        body, name="mid", grid=(SEQ // tm,),
        out_shape=(jax.ShapeDtypeStruct((SEQ, D_MODEL), F32), wide, wide, wide_b, wide_b, wide_b,
                   jax.ShapeDtypeStruct((D_MODEL, D_MODEL), BF16),
                   jax.ShapeDtypeStruct((1, D_MODEL), F32), jax.ShapeDtypeStruct((1, COL_BLOCK), F32),
                   jax.ShapeDtypeStruct((1, COL_BLOCK), F32), jax.ShapeDtypeStruct((1, 1), F32)),
        scratch_shapes=[pltpu.VMEM((D_MODEL, D_MODEL), F32)],
        in_specs=[half(), half(),
                  pl.BlockSpec((None, tm, COL_BLOCK), lambda i: (3, i, 0)),
                  pl.BlockSpec((None, tm, COL_BLOCK), lambda i: (7, i, 0)),
                  full(), full(), fixed(D_MODEL, D_MODEL), fixed(1, COL_BLOCK), fixed(1, COL_BLOCK),
                  fixed(1, D_MODEL)],
        out_specs=(full(), half(), half(), half(), half(), half(), fixed(D_MODEL, D_MODEL),
                   fixed(1, D_MODEL), fixed(1, COL_BLOCK), fixed(1, COL_BLOCK), fixed(1, 1)),
        compiler_params=_params(("arbitrary",)),
    )(attn_o, rec, proj, proj, x, target, w_out_g, attn_w, hgrn_w, final_w)


def _in_proj_bwd_rows(d_groups, w_g, x, dx2, mix_w, rc, rsa, rsb):
    tm = 512

    def body(*refs):
        dg_refs = refs[:N_DEV]
        wg_ref, x_ref, dx2_ref, w_ref, c_ref, sa_ref, sb_ref, gx_ref, dpb_ref, dmw_ref = refs[N_DEV:]

        @pl.when(pl.program_id(0) == 0)
        def _():
            dmw_ref[...] = jnp.zeros_like(dmw_ref)

        parts = []
        for j in range(N_DEV):
            dp = dg_refs[j][...]
            if j < 2:
                dp = _rot_transposed(dp, c_ref[...], sa_ref[...], sb_ref[...])
            parts.append(dp.astype(BF16))
        dpb = jnp.concatenate(parts, axis=1)
        for j in range(N_DEV):
            dpb_ref[j] = parts[j]
        g = _mm_nt(dpb, wg_ref[...])
        xf = x_ref[...]
        rstd = lax.rsqrt(jnp.mean(xf * xf, axis=-1, keepdims=True) + NORM_EPS)
        xn = xf * rstd
        dmw_ref[...] += jnp.sum(g * xn, axis=0, keepdims=True)
        gw = g * w_ref[...]
        gx_ref[...] = dx2_ref[...] + rstd * (gw - xn * jnp.mean(gw * xn, axis=-1, keepdims=True))

    tile = lambda cols: pl.BlockSpec((tm, cols), lambda i: (i, 0))
    fixed = lambda r, c: pl.BlockSpec((r, c), lambda i: (0, 0))
    return pl.pallas_call(
        body, name="in_proj_bwd_rows", grid=(SEQ // tm,),
        out_shape=(jax.ShapeDtypeStruct((SEQ, D_MODEL), F32), jax.ShapeDtypeStruct((N_DEV, SEQ, COL_BLOCK), BF16),
                   jax.ShapeDtypeStruct((1, D_MODEL), F32)),
        in_specs=[tile(COL_BLOCK) for _ in range(N_DEV)] + [
            pl.BlockSpec((D_MODEL, IN_COLS), lambda i: (0, 0), pipeline_mode=pl.Buffered(1)),
            tile(D_MODEL), tile(D_MODEL), fixed(1, D_MODEL), tile(LANES), tile(LANES), tile(LANES)],
        out_specs=(tile(D_MODEL), pl.BlockSpec((N_DEV, tm, COL_BLOCK), lambda i: (0, i, 0)), fixed(1, D_MODEL)),
        compiler_params=_params(("arbitrary",)),
    )(*d_groups, w_g, x, dx2, mix_w, rc, rsa, rsb)


def _weights_exchange(hn_t, dproj_b, dwout_p, small_p):
    n_chips = N_DEV // 2
    rb = 128
    S1_IN, S1_OUT, SMALL, S2_IN, S2_OUT, VIA_IN, VIA_OUT = 0, 4, 8, 15, 17, 19, 21
    rel_of_pair = (3, 2, 1, 0)
    two_hop = n_chips - 1
    half_in, half_out = COL_BLOCK // 2, D_MODEL // 2

    def body(order_ref, hnt_ref, dp_ref, dwout_ref, small_ref, gin_ref, gout_ref, gs_ref,
             part, s1_send, s1_in, s1_out, fwd_in, fwd_out, s2_in, s2_out, via_in, via_out, land_s,
             send_sems, recv_sems):
        t = pl.program_id(0)
        me = _my_place()
        x, y, c = me
        my_chip = 2 * x + y
        sibling = (x, y, 1 - c)

        def remote(slot, src, dst, to):
            return pltpu.make_async_remote_copy(src_ref=src, dst_ref=dst, send_sem=send_sems.at[slot],
                                                recv_sem=recv_sems.at[slot], device_id=to, device_id_type=MESH)

        def s1_in_copy(pair):
            return remote(S1_IN + pair, s1_send.at[pair], s1_in.at[pair], sibling)

        def s1_out_copy(pair):
            q = my_chip ^ rel_of_pair[pair]
            return remote(S1_OUT + pair, dwout_ref.at[q, 1 - c], s1_out.at[pair], sibling)

        def s2_copies(rel):
            peer = _peer(me, 2 * rel)
            return [remote(S2_IN + rel - 1, fwd_in.at[rel - 1], s2_in.at[rel - 1], peer),
                    remote(S2_OUT + rel - 1, fwd_out.at[rel - 1], s2_out.at[rel - 1], peer)]

        def via_copies(k):
            peer = _peer(me, 2 * (2 - k))
            return [remote(VIA_IN + k, fwd_in.at[two_hop - 1, :, pl.ds(k * half_in, half_in)], via_in.at[k], peer),
                    remote(VIA_OUT + k, fwd_out.at[two_hop - 1, :, pl.ds(k * half_out, half_out)], via_out.at[k],
                           peer)]

        def small_copy(rel):
            return remote(SMALL + rel - 1, small_ref, land_s.at[rel], _peer(me, rel))

        @pl.when(t == 0)
        def _():
            land_s[0] = small_ref[...]
            for pair in range(n_chips):
                s1_out_copy(pair).start()
            for rel in range(1, N_DEV):
                small_copy(rel).start()

        part[...] = _mm(hnt_ref[...], dp_ref[...])

        def rows_loop(n_rows, fn):
            def step(b, carry):
                fn(pl.ds(pl.multiple_of(b * rb, rb), rb))
                return carry
            lax.fori_loop(0, n_rows // rb, step, 0)

        for pair, rel in enumerate(rel_of_pair):
            @pl.when(t == 2 * pair)
            def _(pair=pair):
                s1_send[pair] = part[...].astype(BF16)
                s1_in_copy(pair).start()

            @pl.when(t == 2 * pair + 1)
            def _(pair=pair, rel=rel):
                q = my_chip ^ rel
                s1_in_copy(pair).wait_recv()
                s1_out_copy(pair).wait_recv()
                dst_in = fwd_in.at[rel - 1] if rel else gin_ref
                dst_out = fwd_out.at[rel - 1] if rel else gout_ref
                passes_on = rel in (1, 2)
                if passes_on:
                    for cp in via_copies(rel - 1):
                        cp.wait_recv()

                def with_half(val, via, rows, width):
                    if not passes_on:
                        return val
                    extra = via[rel - 1, rows, :].astype(F32)
                    halves = [val[:, :width], val[:, width:]]
                    halves[rel - 1] = halves[rel - 1] + extra
                    return jnp.concatenate(halves, axis=1)

                def add_in(rows):
                    val = part[rows, :] + s1_in[pair, rows, :].astype(F32)
                    dst_in[rows, :] = with_half(val, via_in, rows, half_in).astype(dst_in.dtype)

                def add_out(rows):
                    val = dwout_ref[q, c, rows, :].astype(F32) + s1_out[pair, rows, :].astype(F32)
                    dst_out[rows, :] = with_half(val, via_out, rows, half_out).astype(dst_out.dtype)

                rows_loop(D_MODEL, add_in)
                rows_loop(WOUT_ROWS, add_out)
                if rel == two_hop:
                    for k in range(2):
                        for cp in via_copies(k):
                            cp.start()
                elif rel:
                    for cp in s2_copies(rel):
                        cp.start()

        @pl.when(t == N_DEV - 1)
        def _():
            for rel in range(1, two_hop):
                for cp in s2_copies(rel):
                    cp.wait_recv()

            def total_in(rows):
                g = gin_ref[rows, :]
                for rel in range(1, two_hop):
                    g = g + s2_in[rel - 1, rows, :].astype(F32)
                gin_ref[rows, :] = g

            def total_out(rows):
                g = gout_ref[rows, :]
                for rel in range(1, two_hop):
                    g = g + s2_out[rel - 1, rows, :].astype(F32)
                gout_ref[rows, :] = g

            rows_loop(D_MODEL, total_in)
            rows_loop(WOUT_ROWS, total_out)

            for rel in range(1, N_DEV):
                small_copy(rel).wait_recv()
            my_flat = _flat(me)
            g = land_s[my_flat ^ 0]
            for dev in range(1, N_DEV):
                g = g + land_s[my_flat ^ dev]
            gs_ref[...] = g

            for pair in range(n_chips):
                s1_in_copy(pair).wait_send()
                s1_out_copy(pair).wait_send()
            for rel in range(1, two_hop):
                for cp in s2_copies(rel) + via_copies(rel - 1):
                    cp.wait_send()
            for rel in range(1, N_DEV):
                small_copy(rel).wait_send()

    place_x, place_y, place_c = _my_place()
    my_chip = 2 * place_x + place_y
    order = jnp.stack([2 * (my_chip ^ rel) + core for rel in rel_of_pair
                       for core in (1 - place_c, place_c)]).astype(jnp.int32)

    whole = lambda: pl.BlockSpec(memory_space=pltpu.VMEM)
    in_blocks = lambda n: pltpu.VMEM((n, D_MODEL, COL_BLOCK), BF16)
    out_blocks = lambda n: pltpu.VMEM((n, WOUT_ROWS, D_MODEL), BF16)
    grid_spec = pltpu.PrefetchScalarGridSpec(
        num_scalar_prefetch=1, grid=(N_DEV,),
        in_specs=[pl.BlockSpec((D_MODEL, SEQ), lambda t, order: (0, 0), pipeline_mode=pl.Buffered(1)),
                  pl.BlockSpec((None, SEQ, COL_BLOCK), lambda t, order: (order[t], 0, 0)), whole(), whole()],
        out_specs=(whole(), whole(), whole()),
        scratch_shapes=[pltpu.VMEM((D_MODEL, COL_BLOCK), F32), in_blocks(n_chips), in_blocks(n_chips),
                        out_blocks(n_chips), in_blocks(n_chips - 1), out_blocks(n_chips - 1),
                        in_blocks(n_chips - 2), out_blocks(n_chips - 2),
                        pltpu.VMEM((2, D_MODEL, half_in), BF16), pltpu.VMEM((2, WOUT_ROWS, half_out), BF16),
                        pltpu.VMEM((N_DEV, SMALL_ROWS, LANES), F32),
                        pltpu.SemaphoreType.DMA((23,)), pltpu.SemaphoreType.DMA((23,))])
    return pl.pallas_call(
        body, name="weights_exchange", grid_spec=grid_spec,
        out_shape=(jax.ShapeDtypeStruct((D_MODEL, COL_BLOCK), F32), jax.ShapeDtypeStruct((WOUT_ROWS, D_MODEL), F32),
                   jax.ShapeDtypeStruct((SMALL_ROWS, LANES), F32)),
        compiler_params=_params(("arbitrary",)),
    )(order, hn_t, dproj_b, dwout_p.reshape(n_chips, 2, WOUT_ROWS, D_MODEL), small_p)


def _adamw(w, g, m, v):
    m = ADAM_B1 * m + (1.0 - ADAM_B1) * g
    v = ADAM_B2 * v + (1.0 - ADAM_B2) * (g * g)
    m_hat = m / (1.0 - ADAM_B1 ** ADAM_STEP)
    v_hat = v / (1.0 - ADAM_B2 ** ADAM_STEP)
    delta = -ADAM_LR * (m_hat / (jnp.sqrt(v_hat) + ADAM_EPS) + ADAM_WD * w)
    return delta, m, v


def _adamw_update(grads, weights, m_old, v_old):
    rb = 256

    def body(*refs):
        g_refs, w_refs, m_refs, v_refs = refs[0:3], refs[3:6], refs[6:9], refs[9:12]
        d_refs, nm_refs, nv_refs = refs[12:15], refs[15:18], refs[18:21]
        for k in range(3):
            n_rows = g_refs[k].shape[0]
            step_rows = min(rb, n_rows)

            def step(b, carry, k=k, step_rows=step_rows):
                rows = pl.ds(pl.multiple_of(b * step_rows, 8), step_rows)
                delta, nm, nv = _adamw(w_refs[k][rows, :], g_refs[k][rows, :], m_refs[k][rows, :], v_refs[k][rows, :])
                d_refs[k][rows, :] = delta
                nm_refs[k][rows, :] = nm
                nv_refs[k][rows, :] = nv
                return carry

            lax.fori_loop(0, n_rows // step_rows, step, 0)

    shapes = tuple(jax.ShapeDtypeStruct(g.shape, F32) for g in grads)
    vm = lambda: pl.BlockSpec(memory_space=pltpu.VMEM)
    outs = pl.pallas_call(
        body, name="adamw_update", out_shape=shapes * 3,
        in_specs=[vm() for _ in range(12)], out_specs=tuple(vm() for _ in range(9)),
        compiler_params=_params(),
    )(*grads, *weights, *m_old, *v_old)
    return outs[0:3], outs[3:6], outs[6:9]


def _pack_small(mix, attn, hgrn, lb, final, loss=None):
    def rows8(a):
        a = a.reshape(-1, LANES)
        return jnp.pad(a, ((0, 8 - a.shape[0]), (0, 0)))
    last = jnp.zeros((8, LANES), F32) if loss is None else jnp.pad(loss.reshape(1, 1), ((0, 7), (0, LANES - 1)))
    return jnp.concatenate([rows8(mix), rows8(attn), rows8(hgrn), rows8(lb), rows8(final), last], axis=0)


def _unpack_small(slab):
    return (slab[ROW_MIX:ROW_MIX + 8].reshape(1, D_MODEL), slab[ROW_ATTN:ROW_ATTN + 4].reshape(1, ATTN_WIDTH),
            slab[ROW_HGRN:ROW_HGRN + 4].reshape(1, HGRN_WIDTH), slab[ROW_LB:ROW_LB + 8].reshape(2, HGRN_WIDTH),
            slab[ROW_FINAL:ROW_FINAL + 8].reshape(D_MODEL))


def _rope(pos_row):
    j = np.arange(ROPE_ROWS)
    inv = np.where(j < ROPE_HALF, ROPE_THETA ** (-(j % ROPE_HALF) * (2.0 / ROPE_DIMS)), 0.0)
    e = np.arange(LANES) % HEAD_DIM
    hit = (j[:, None] == (e % ROPE_HALF)[None, :]) & (j[:, None] < ROPE_HALF)
    sel = np.stack([hit & (e < ROPE_DIMS), hit & (e >= ROPE_HALF) & (e < ROPE_DIMS),
                    -1.0 * (hit & (e < ROPE_HALF))]).astype(np.float32)
    return _rope_tables(pos_row, jnp.asarray(inv.astype(np.float32).reshape(ROPE_ROWS, 1)),
                        jnp.asarray(sel, dtype=BF16))


def _local_step(x, proj, qkv_sorted, w_in_g, w_out_g, tables, mix_w, attn_w, hgrn_w, lb_raw, final_w, target):
    rc, rsa, rsb = tables
    attn_o, lse = _attn_fwd_fused(qkv_sorted)
    rec, states = _hgrn_fwd(proj, lb_raw)

    (dx2, d_o, delta, d_ag, d_rec, d_hg, dwout_p, d_final, d_attn_w, d_hgrn_w, loss) = _mid(
        attn_o, rec, proj, x, target, w_out_g, attn_w, hgrn_w, final_w.reshape(1, D_MODEL))

    dqkv = _attn_bwd_fused(qkv_sorted, d_o, lse, delta)
    d_hq, d_hf, d_hi, d_lb = _hgrn_bwd(proj, lb_raw, d_rec, states)

    grad_x, dproj_b, d_mix = _in_proj_bwd_rows(
        (dqkv[0], dqkv[1], dqkv[2], d_ag, d_hq, d_hf, d_hi, d_hg), w_in_g, x, dx2, mix_w, rc, rsa, rsb)
    small_p = _pack_small(d_mix, d_attn_w, d_hgrn_w, d_lb, d_final, loss)
    return grad_x, dproj_b, dwout_p, small_p


def kernel(x, positions, w_in, w_out, mix_norm_w, attn_out_norm_w, hgrn_out_norm_w, hgrn_lb_raw, final_norm_w, loss_target, m_w_in, m_w_out, m_mix_norm_w, m_attn_out_norm_w, m_hgrn_out_norm_w, m_hgrn_lb_raw, m_final_norm_w, v_w_in, v_w_out, v_mix_norm_w, v_attn_out_norm_w, v_hgrn_out_norm_w, v_hgrn_lb_raw, v_final_norm_w):
    tables = _rope(positions)
    proj, hn_t, w_in_g, w_out_g, qkv_sorted = _gather_project(x[0], mix_norm_w, w_in[0], w_out[0], *tables)
    grad_x, dproj_b, dwout_p, small_p = _local_step(
        x[0], proj, qkv_sorted, w_in_g, w_out_g, tables, mix_norm_w, attn_out_norm_w, hgrn_out_norm_w,
        hgrn_lb_raw, final_norm_w, loss_target[0])
    g_in, g_out, g_s = _weights_exchange(hn_t, dproj_b, dwout_p, small_p)

    w_s = _pack_small(mix_norm_w, attn_out_norm_w, hgrn_out_norm_w, hgrn_lb_raw, final_norm_w)
    m_s = _pack_small(m_mix_norm_w, m_attn_out_norm_w, m_hgrn_out_norm_w, m_hgrn_lb_raw, m_final_norm_w)
    v_s = _pack_small(v_mix_norm_w, v_attn_out_norm_w, v_hgrn_out_norm_w, v_hgrn_lb_raw, v_final_norm_w)
    (d_in, d_out, d_s), (nm_in, nm_out, nm_s), (nv_in, nv_out, nv_s) = _adamw_update(
        (g_in, g_out, g_s), (w_in[0], w_out[0], w_s), (m_w_in[0], m_w_out[0], m_s), (v_w_in[0], v_w_out[0], v_s))

    loss = g_s[ROW_LOSS, 0]
    return (loss, grad_x[None], g_in[None], g_out[None], *_unpack_small(g_s),
            d_in[None], d_out[None], *_unpack_small(d_s),
            nm_in[None], nm_out[None], *_unpack_small(nm_s),
            nv_in[None], nv_out[None], *_unpack_small(nv_s))
```
